```python
import jax, jax.numpy as jnp
from jax import lax
import numpy as np

D_MODEL = 1024
BATCH = 8
SEQ = 4096
DEPTH = 1

GRID_W = 64
CTX_LEN = 256
D_MIX = D_MODEL
D_ATTN = D_MIX // 2
D_POOL = D_MIX - D_ATTN
MLA_HEADS = 4
QK_NOPE_DIM = 128
QK_ROPE_DIM = 64
QK_HEAD_DIM = QK_NOPE_DIM + QK_ROPE_DIM
V_HEAD_DIM = D_ATTN // MLA_HEADS
Q_LORA_RANK = 256
KV_LORA_RANK = 128
ROPE_AXIS_DIM = QK_ROPE_DIM // 2
ROPE_BASE = 10000.0
POOL_WINDOWS = (2, 4, 8, 16)
POOL_GROUPS = len(POOL_WINDOWS)
POOL_GROUP_DIM = D_POOL // POOL_GROUPS
Q_BLOCK = 128
NORM_EPS = 1e-6

OFF_CQ = 0
OFF_CKV = OFF_CQ + Q_LORA_RANK
OFF_KR = OFF_CKV + KV_LORA_RANK
OFF_GA = OFF_KR + QK_ROPE_DIM
OFF_PIN = OFF_GA + D_ATTN
OFF_GP = OFF_PIN + D_POOL
D_IN_PROJ = OFF_GP + D_POOL

kernel_name = "hymba_mla_pool_adaln_prefix"


def _rms(x, g):
    xf = x.astype(jnp.float32)
    y = xf * lax.rsqrt(jnp.mean(xf * xf, axis=-1, keepdims=True) + NORM_EPS)
    return (y * g.astype(jnp.float32)).astype(x.dtype)


def _rotate_half(x):
    x1, x2 = jnp.split(x, 2, axis=-1)
    return jnp.concatenate([-x2, x1], axis=-1)


def _axial_rope_tables(L):
    rows = L // GRID_W
    row = jnp.repeat(jnp.arange(rows, dtype=jnp.float32), GRID_W)
    col = jnp.tile(jnp.arange(GRID_W, dtype=jnp.float32), rows)
    n_freq = ROPE_AXIS_DIM // 2
    inv = ROPE_BASE ** (-jnp.arange(n_freq, dtype=jnp.float32) / n_freq)
    ang_r = row[:, None] * inv
    ang_c = col[:, None] * inv
    ang = jnp.concatenate([ang_r, ang_r, ang_c, ang_c], axis=-1)
    return jnp.cos(ang), jnp.sin(ang)


def _apply_axial_rope(x, cos, sin):
    xr, xc = jnp.split(x, 2, axis=-1)
    rot = jnp.concatenate([_rotate_half(xr), _rotate_half(xc)], axis=-1)
    c = cos[:, None, :]
    s = sin[:, None, :]
    return (x.astype(jnp.float32) * c + rot.astype(jnp.float32) * s).astype(x.dtype)


def _mla_qkv(u, q_lora_g, w_uq, kv_lora_g, w_ukv, q_norm_g, k_norm_g, rope):
    B, L = u.shape[0], u.shape[1]
    cq = u[..., OFF_CQ:OFF_CKV]
    ckv = u[..., OFF_CKV:OFF_KR]
    k_rope = u[..., OFF_KR:OFF_GA]
    q = (_rms(cq, q_lora_g) @ w_uq).reshape(B, L, MLA_HEADS, QK_HEAD_DIM)
    kv = (_rms(ckv, kv_lora_g) @ w_ukv).reshape(B, L, MLA_HEADS, QK_NOPE_DIM + V_HEAD_DIM)
    k_nope, v = kv[..., :QK_NOPE_DIM], kv[..., QK_NOPE_DIM:]
    k = jnp.concatenate(
        [k_nope, jnp.broadcast_to(k_rope[:, :, None, :], (B, L, MLA_HEADS, QK_ROPE_DIM))], axis=-1)
    q = _rms(q, q_norm_g)
    k = _rms(k, k_norm_g)
    if rope is not None:
        cos, sin = rope
        q = jnp.concatenate([q[..., :QK_NOPE_DIM], _apply_axial_rope(q[..., QK_NOPE_DIM:], cos, sin)], axis=-1)
        k = jnp.concatenate([k[..., :QK_NOPE_DIM], _apply_axial_rope(k[..., QK_NOPE_DIM:], cos, sin)], axis=-1)
    tr = lambda t: jnp.transpose(t, (0, 2, 1, 3))
    return tr(q), tr(k), tr(v)


def _attend(q, k, v):
    B, H, Lq, dk = q.shape
    nb = Lq // Q_BLOCK
    scale = QK_HEAD_DIM ** -0.5
    qb = jnp.transpose(q.reshape(B, H, nb, Q_BLOCK, dk), (2, 0, 1, 3, 4))

    def one(qblk):
        s = jnp.einsum('bhqd,bhkd->bhqk', qblk, k).astype(jnp.float32) * scale
        p = jax.nn.softmax(s, axis=-1)
        return jnp.einsum('bhqk,bhkd->bhqd', p.astype(v.dtype), v)

    o = lax.map(one, qb)
    o = jnp.transpose(o, (1, 3, 0, 2, 4)).reshape(B, Lq, H * v.shape[-1])
    return o


def _multiscale_pool(u, w_pool, pool_scale):
    B, L, _ = u.shape
    ug = u.reshape(B, L, POOL_GROUPS, POOL_GROUP_DIM)
    cs = jnp.concatenate(
        [jnp.zeros((B, 1, POOL_GROUPS, POOL_GROUP_DIM), jnp.float32),
         jnp.cumsum(ug.astype(jnp.float32), axis=1)], axis=1)
    t = jnp.arange(L, dtype=jnp.int32)[:, None]
    w = jnp.array(POOL_WINDOWS, dtype=jnp.int32)[None, :]
    lo = jnp.clip(t - w // 2, 0, L)
    hi = jnp.clip(t - w // 2 + w, 0, L)
    g = jnp.arange(POOL_GROUPS, dtype=jnp.int32)[None, :]
    win_sum = cs[:, hi, g, :] - cs[:, lo, g, :]
    cnt = (hi - lo).astype(jnp.float32)[None, :, :, None]
    pooled = (win_sum / cnt - ug.astype(jnp.float32)).astype(u.dtype)
    y = jnp.einsum('blgc,gcd->blgd', pooled, w_pool).reshape(B, L, D_POOL)
    return y * pool_scale


def _branches(u, mla_out, w_pool, pool_scale):
    gate_a = u[..., OFF_GA:OFF_PIN]
    pool_in = u[..., OFF_PIN:OFF_GP]
    gate_p = u[..., OFF_GP:D_IN_PROJ]
    br_a = jax.nn.silu(gate_a) * mla_out
    br_p = jax.nn.silu(gate_p) * _multiscale_pool(pool_in, w_pool, pool_scale)
    return jnp.concatenate([br_a, br_p], axis=-1)


def _fwd_setup_inputs(seed: int = 0) -> dict:
    key = jax.random.key(seed)
    ks = jax.random.split(key, 20)
    f32 = jnp.float32
    nrm = lambda k, shape, s: jax.random.normal(k, shape, f32) * s
    return {
        "x": nrm(ks[0], (BATCH, SEQ, D_MODEL), 1.0),
        "c": nrm(ks[1], (BATCH, D_MODEL), 1.0),
        "ctx": nrm(ks[2], (BATCH, CTX_LEN, D_MODEL), 1.0),
        "c_ctx": nrm(ks[3], (D_MODEL,), 1.0),
        "w_mod": nrm(ks[4], (DEPTH, D_MODEL, 3 * D_MODEL), 0.5 * D_MODEL ** -0.5),
        "b_mod": nrm(ks[5], (DEPTH, 3 * D_MODEL), 0.02),
        "norm_g": 1.0 + nrm(ks[6], (DEPTH, D_MODEL), 0.1),
        "w_in": nrm(ks[7], (DEPTH, D_MODEL, D_IN_PROJ), D_MODEL ** -0.5),
        "q_lora_g": 1.0 + nrm(ks[8], (DEPTH, Q_LORA_RANK), 0.1),
        "w_uq": nrm(ks[9], (DEPTH, Q_LORA_RANK, MLA_HEADS * QK_HEAD_DIM), Q_LORA_RANK ** -0.5),
        "kv_lora_g": 1.0 + nrm(ks[10], (DEPTH, KV_LORA_RANK), 0.1),
        "w_ukv": nrm(ks[11], (DEPTH, KV_LORA_RANK, MLA_HEADS * (QK_NOPE_DIM + V_HEAD_DIM)), KV_LORA_RANK ** -0.5),
        "q_norm_g": 1.0 + nrm(ks[12], (DEPTH, QK_HEAD_DIM), 0.1),
        "k_norm_g": 1.0 + nrm(ks[13], (DEPTH, QK_HEAD_DIM), 0.1),
        "w_pool": nrm(ks[14], (DEPTH, POOL_GROUPS, POOL_GROUP_DIM, POOL_GROUP_DIM), POOL_GROUP_DIM ** -0.5),
        "pool_scale": 1.0 + nrm(ks[15], (DEPTH, D_POOL), 0.1),
        "w_out": nrm(ks[16], (DEPTH, D_MIX, D_MODEL), D_MIX ** -0.5),
    }


def _fwd_reference(x, c, ctx, c_ctx, w_mod, b_mod, norm_g, w_in, q_lora_g, w_uq, kv_lora_g, w_ukv,
              q_norm_g, k_norm_g, w_pool, pool_scale, w_out):
    L = x.shape[1]
    rope = _axial_rope_tables(L)
    for l in range(DEPTH):
        mod = jax.nn.silu(c) @ w_mod[l] + b_mod[l]
        shift, scale, gate = jnp.split(mod, 3, axis=-1)
        mod_c = jax.nn.silu(c_ctx) @ w_mod[l] + b_mod[l]
        shift_c, scale_c, gate_c = jnp.split(mod_c, 3, axis=-1)

        h = _rms(x, norm_g[l]) * (1.0 + scale[:, None, :]) + shift[:, None, :]
        hc = _rms(ctx, norm_g[l]) * (1.0 + scale_c) + shift_c
        u = h @ w_in[l]
        uc = hc @ w_in[l]

        q, k, v = _mla_qkv(u, q_lora_g[l], w_uq[l], kv_lora_g[l], w_ukv[l],
                           q_norm_g[l], k_norm_g[l], rope)
        qc, kc, vc = _mla_qkv(uc, q_lora_g[l], w_uq[l], kv_lora_g[l], w_ukv[l],
                              q_norm_g[l], k_norm_g[l], None)
        k_all = jnp.concatenate([kc, k], axis=2)
        v_all = jnp.concatenate([vc, v], axis=2)
        attn = _attend(q, k_all, v_all)
        y = _branches(u, attn, w_pool[l], pool_scale[l]) @ w_out[l]
        x_new = x + gate[:, None, :] * y

        if l < DEPTH - 1:
            attn_c = _attend(qc, kc, vc)
            yc = _branches(uc, attn_c, w_pool[l], pool_scale[l]) @ w_out[l]
            ctx = ctx + gate_c * yc
        x = x_new
    return x


import jax as _jax
import jax.numpy as _jnp

TWIN_FORMAT = 'train_step'
FWD_PARAMS = ['x', 'c', 'ctx', 'c_ctx', 'w_mod', 'b_mod', 'norm_g', 'w_in', 'q_lora_g', 'w_uq', 'kv_lora_g', 'w_ukv', 'q_norm_g', 'k_norm_g', 'w_pool', 'pool_scale', 'w_out']
TWIN_WEIGHTS = ['c_ctx', 'w_mod', 'b_mod', 'norm_g', 'w_in', 'q_lora_g', 'w_uq', 'kv_lora_g', 'w_ukv', 'q_norm_g', 'k_norm_g', 'w_pool', 'pool_scale', 'w_out']
TWIN_DIFF_INPUT = 'x'
TWIN_INPUTS = ['x', 'c', 'ctx', 'c_ctx', 'w_mod', 'b_mod', 'norm_g', 'w_in', 'q_lora_g', 'w_uq', 'kv_lora_g', 'w_ukv', 'q_norm_g', 'k_norm_g', 'w_pool', 'pool_scale', 'w_out', 'loss_target', 'm_c_ctx', 'm_w_mod', 'm_b_mod', 'm_norm_g', 'm_w_in', 'm_q_lora_g', 'm_w_uq', 'm_kv_lora_g', 'm_w_ukv', 'm_q_norm_g', 'm_k_norm_g', 'm_w_pool', 'm_pool_scale', 'm_w_out', 'v_c_ctx', 'v_w_mod', 'v_b_mod', 'v_norm_g', 'v_w_in', 'v_q_lora_g', 'v_w_uq', 'v_kv_lora_g', 'v_w_ukv', 'v_q_norm_g', 'v_k_norm_g', 'v_w_pool', 'v_pool_scale', 'v_w_out']
TWIN_OUTPUTS = ['loss', 'grad_x', 'grad_c_ctx', 'grad_w_mod', 'grad_b_mod', 'grad_norm_g', 'grad_w_in', 'grad_q_lora_g', 'grad_w_uq', 'grad_kv_lora_g', 'grad_w_ukv', 'grad_q_norm_g', 'grad_k_norm_g', 'grad_w_pool', 'grad_pool_scale', 'grad_w_out', 'delta_c_ctx', 'delta_w_mod', 'delta_b_mod', 'delta_norm_g', 'delta_w_in', 'delta_q_lora_g', 'delta_w_uq', 'delta_kv_lora_g', 'delta_w_ukv', 'delta_q_norm_g', 'delta_k_norm_g', 'delta_w_pool', 'delta_pool_scale', 'delta_w_out', 'new_m_c_ctx', 'new_m_w_mod', 'new_m_b_mod', 'new_m_norm_g', 'new_m_w_in', 'new_m_q_lora_g', 'new_m_w_uq', 'new_m_kv_lora_g', 'new_m_w_ukv', 'new_m_q_norm_g', 'new_m_k_norm_g', 'new_m_w_pool', 'new_m_pool_scale', 'new_m_w_out', 'new_v_c_ctx', 'new_v_w_mod', 'new_v_b_mod', 'new_v_norm_g', 'new_v_w_in', 'new_v_q_lora_g', 'new_v_w_uq', 'new_v_kv_lora_g', 'new_v_w_ukv', 'new_v_q_norm_g', 'new_v_k_norm_g', 'new_v_w_pool', 'new_v_pool_scale', 'new_v_w_out']
TWIN_LEAF_KINDS = {'loss': 'loss', 'grad_x': 'grad_x', 'grad_c_ctx': 'grad_w', 'grad_w_mod': 'grad_w', 'grad_b_mod': 'grad_w', 'grad_norm_g': 'grad_w', 'grad_w_in': 'grad_w', 'grad_q_lora_g': 'grad_w', 'grad_w_uq': 'grad_w', 'grad_kv_lora_g': 'grad_w', 'grad_w_ukv': 'grad_w', 'grad_q_norm_g': 'grad_w', 'grad_k_norm_g': 'grad_w', 'grad_w_pool': 'grad_w', 'grad_pool_scale': 'grad_w', 'grad_w_out': 'grad_w', 'delta_c_ctx': 'delta_w', 'delta_w_mod': 'delta_w', 'delta_b_mod': 'delta_w', 'delta_norm_g': 'delta_w', 'delta_w_in': 'delta_w', 'delta_q_lora_g': 'delta_w', 'delta_w_uq': 'delta_w', 'delta_kv_lora_g': 'delta_w', 'delta_w_ukv': 'delta_w', 'delta_q_norm_g': 'delta_w', 'delta_k_norm_g': 'delta_w', 'delta_w_pool': 'delta_w', 'delta_pool_scale': 'delta_w', 'delta_w_out': 'delta_w', 'new_m_c_ctx': 'new_m', 'new_m_w_mod': 'new_m', 'new_m_b_mod': 'new_m', 'new_m_norm_g': 'new_m', 'new_m_w_in': 'new_m', 'new_m_q_lora_g': 'new_m', 'new_m_w_uq': 'new_m', 'new_m_kv_lora_g': 'new_m', 'new_m_w_ukv': 'new_m', 'new_m_q_norm_g': 'new_m', 'new_m_k_norm_g': 'new_m', 'new_m_w_pool': 'new_m', 'new_m_pool_scale': 'new_m', 'new_m_w_out': 'new_m', 'new_v_c_ctx': 'new_v', 'new_v_w_mod': 'new_v', 'new_v_b_mod': 'new_v', 'new_v_norm_g': 'new_v', 'new_v_w_in': 'new_v', 'new_v_q_lora_g': 'new_v', 'new_v_w_uq': 'new_v', 'new_v_kv_lora_g': 'new_v', 'new_v_w_ukv': 'new_v', 'new_v_q_norm_g': 'new_v', 'new_v_k_norm_g': 'new_v', 'new_v_w_pool': 'new_v', 'new_v_pool_scale': 'new_v', 'new_v_w_out': 'new_v'}


def _forward(args):
    return _fwd_reference(*[args[k] for k in FWD_PARAMS])


def _output_shape():
    out = _jax.eval_shape(lambda: _forward(_fwd_setup_inputs(0)))
    return out.shape, out.dtype

N_MICROBATCH = 1
ADAM_LR = 0.001
ADAM_B1 = 0.9
ADAM_B2 = 0.999
ADAM_EPS = 1e-08
ADAM_WD = 0.01
ADAM_STEP = 10
PER_EXAMPLE_BATCH_AXIS = {'x': 0, 'c': 0, 'ctx': 0, 'loss_target': 0}
SHARED_INPUTS = []
_WEIGHT_DTYPES = {'c_ctx': _jnp.float32, 'w_mod': _jnp.float32, 'b_mod': _jnp.float32, 'norm_g': _jnp.float32, 'w_in': _jnp.float32, 'q_lora_g': _jnp.float32, 'w_uq': _jnp.float32, 'kv_lora_g': _jnp.float32, 'w_ukv': _jnp.float32, 'q_norm_g': _jnp.float32, 'k_norm_g': _jnp.float32, 'w_pool': _jnp.float32, 'pool_scale': _jnp.float32, 'w_out': _jnp.float32}
MOMENT_SCALE = {'c_ctx': 1.024069e-02, 'w_mod': 4.221215e-01, 'b_mod': 9.366542e-01, 'norm_g': 1.208513e+00, 'w_in': 7.674466e-02, 'q_lora_g': 8.215538e-03, 'w_uq': 4.667472e-03, 'kv_lora_g': 4.328014e-01, 'w_ukv': 2.824430e-02, 'q_norm_g': 1.280476e-02, 'k_norm_g': 1.211437e-02, 'w_pool': 9.043938e-02, 'pool_scale': 1.170099e+00, 'w_out': 4.387642e-02}


def _to_microbatches(a, axis):
    t = _jnp.moveaxis(a, axis, 0)
    t = t.reshape((N_MICROBATCH, t.shape[0] // N_MICROBATCH) + t.shape[1:])
    return _jnp.moveaxis(t, 1, axis + 1)


def setup_inputs(seed: int = 0) -> dict:
    inp = _fwd_setup_inputs(seed)
    key = _jax.random.fold_in(_jax.random.key(seed), 7919)
    shape, _ = _output_shape()
    out = dict(inp)
    out["loss_target"] = _jax.random.normal(_jax.random.fold_in(key, 0), shape, _jnp.float32)
    for i, name in enumerate(TWIN_WEIGHTS):
        w = inp[name].astype(_jnp.float32)
        if MOMENT_SCALE is None:
            s = _jnp.sqrt(_jnp.mean(_jnp.square(w)) + 1e-30)
        else:
            s = MOMENT_SCALE[name]
        km, kv = _jax.random.split(_jax.random.fold_in(key, i + 1))
        out[name] = w
        out["m_" + name] = s * _jax.random.normal(km, w.shape, _jnp.float32)
        out["v_" + name] = (s * s) * _jax.random.uniform(kv, w.shape, _jnp.float32, 0.5, 1.5)
    if N_MICROBATCH > 1:
        for name, axis in PER_EXAMPLE_BATCH_AXIS.items():
            out[name] = _to_microbatches(out[name], axis)
    return {'x': out['x'], 'c': out['c'], 'ctx': out['ctx'], 'c_ctx': out['c_ctx'], 'w_mod': out['w_mod'], 'b_mod': out['b_mod'], 'norm_g': out['norm_g'], 'w_in': out['w_in'], 'q_lora_g': out['q_lora_g'], 'w_uq': out['w_uq'], 'kv_lora_g': out['kv_lora_g'], 'w_ukv': out['w_ukv'], 'q_norm_g': out['q_norm_g'], 'k_norm_g': out['k_norm_g'], 'w_pool': out['w_pool'], 'pool_scale': out['pool_scale'], 'w_out': out['w_out'], 'loss_target': out['loss_target'], 'm_c_ctx': out['m_c_ctx'], 'm_w_mod': out['m_w_mod'], 'm_b_mod': out['m_b_mod'], 'm_norm_g': out['m_norm_g'], 'm_w_in': out['m_w_in'], 'm_q_lora_g': out['m_q_lora_g'], 'm_w_uq': out['m_w_uq'], 'm_kv_lora_g': out['m_kv_lora_g'], 'm_w_ukv': out['m_w_ukv'], 'm_q_norm_g': out['m_q_norm_g'], 'm_k_norm_g': out['m_k_norm_g'], 'm_w_pool': out['m_w_pool'], 'm_pool_scale': out['m_pool_scale'], 'm_w_out': out['m_w_out'], 'v_c_ctx': out['v_c_ctx'], 'v_w_mod': out['v_w_mod'], 'v_b_mod': out['v_b_mod'], 'v_norm_g': out['v_norm_g'], 'v_w_in': out['v_w_in'], 'v_q_lora_g': out['v_q_lora_g'], 'v_w_uq': out['v_w_uq'], 'v_kv_lora_g': out['v_kv_lora_g'], 'v_w_ukv': out['v_w_ukv'], 'v_q_norm_g': out['v_q_norm_g'], 'v_k_norm_g': out['v_k_norm_g'], 'v_w_pool': out['v_w_pool'], 'v_pool_scale': out['v_pool_scale'], 'v_w_out': out['v_w_out']}


def _loss(weights, diff, rest, loss_target):
    with _jax.named_scope("forward"):
        args = {**rest, TWIN_DIFF_INPUT: diff, **{k: w.astype(_WEIGHT_DTYPES[k]) for k, w in weights.items()}}
        y = _forward(args)
    with _jax.named_scope("loss_head"):
        err = _jnp.square(y.astype(_jnp.float32) - loss_target)
        return 0.5 * _jnp.sum(_jnp.mean(err, axis=-1)) if err.ndim else 0.5 * err


def _adamw(w, g, m, v):
    m = ADAM_B1 * m + (1.0 - ADAM_B1) * g
    v = ADAM_B2 * v + (1.0 - ADAM_B2) * _jnp.square(g)
    m_hat = m / (1.0 - ADAM_B1 ** ADAM_STEP)
    v_hat = v / (1.0 - ADAM_B2 ** ADAM_STEP)
    delta = -ADAM_LR * (m_hat / (_jnp.sqrt(v_hat) + ADAM_EPS) + ADAM_WD * w)
    return delta, m, v


def reference(x, c, ctx, c_ctx, w_mod, b_mod, norm_g, w_in, q_lora_g, w_uq, kv_lora_g, w_ukv, q_norm_g, k_norm_g, w_pool, pool_scale, w_out, loss_target, m_c_ctx, m_w_mod, m_b_mod, m_norm_g, m_w_in, m_q_lora_g, m_w_uq, m_kv_lora_g, m_w_ukv, m_q_norm_g, m_k_norm_g, m_w_pool, m_pool_scale, m_w_out, v_c_ctx, v_w_mod, v_b_mod, v_norm_g, v_w_in, v_q_lora_g, v_w_uq, v_kv_lora_g, v_w_ukv, v_q_norm_g, v_k_norm_g, v_w_pool, v_pool_scale, v_w_out):
    given = dict(x=x, c=c, ctx=ctx, c_ctx=c_ctx, w_mod=w_mod, b_mod=b_mod, norm_g=norm_g, w_in=w_in, q_lora_g=q_lora_g, w_uq=w_uq, kv_lora_g=kv_lora_g, w_ukv=w_ukv, q_norm_g=q_norm_g, k_norm_g=k_norm_g, w_pool=w_pool, pool_scale=pool_scale, w_out=w_out, loss_target=loss_target, m_c_ctx=m_c_ctx, m_w_mod=m_w_mod, m_b_mod=m_b_mod, m_norm_g=m_norm_g, m_w_in=m_w_in, m_q_lora_g=m_q_lora_g, m_w_uq=m_w_uq, m_kv_lora_g=m_kv_lora_g, m_w_ukv=m_w_ukv, m_q_norm_g=m_q_norm_g, m_k_norm_g=m_k_norm_g, m_w_pool=m_w_pool, m_pool_scale=m_pool_scale, m_w_out=m_w_out, v_c_ctx=v_c_ctx, v_w_mod=v_w_mod, v_b_mod=v_b_mod, v_norm_g=v_norm_g, v_w_in=v_w_in, v_q_lora_g=v_q_lora_g, v_w_uq=v_w_uq, v_kv_lora_g=v_kv_lora_g, v_w_ukv=v_w_ukv, v_q_norm_g=v_q_norm_g, v_k_norm_g=v_k_norm_g, v_w_pool=v_w_pool, v_pool_scale=v_pool_scale, v_w_out=v_w_out)
    weights = {n: given[n] for n in TWIN_WEIGHTS}
    shared = {n: given[n] for n in SHARED_INPUTS}
    per_example = {n: given[n] for n in ['x', 'c', 'ctx']}
    grad_fn = _jax.value_and_grad(_loss, argnums=(0, 1))

    def one_microbatch(ex, loss_target):
        ex = dict(ex)
        diff = ex.pop(TWIN_DIFF_INPUT)
        return grad_fn(weights, diff, {**shared, **ex}, loss_target)

    if N_MICROBATCH == 1:
        loss, (grad_w, grad_x) = one_microbatch(per_example, given["loss_target"])
    else:
        def body(carry, xs):
            loss_sum, grad_sum = carry
            l_k, (gw_k, gx_k) = one_microbatch(xs[0], xs[1])
            with _jax.named_scope("update"):
                return (loss_sum + l_k, _jax.tree.map(_jnp.add, grad_sum, gw_k)), gx_k

        init = (_jnp.zeros((), _jnp.float32), _jax.tree.map(_jnp.zeros_like, weights))
        (loss, grad_w), grad_x = _jax.lax.scan(body, init, (per_example, given["loss_target"]))
    with _jax.named_scope("update"):
        delta_w, new_m, new_v = {}, {}, {}
        for n in TWIN_WEIGHTS:
            delta_w[n], new_m[n], new_v[n] = _adamw(weights[n], grad_w[n], given["m_" + n], given["v_" + n])
    return (loss, grad_x, *[grad_w[n] for n in TWIN_WEIGHTS], *[delta_w[n] for n in TWIN_WEIGHTS],
            *[new_m[n] for n in TWIN_WEIGHTS], *[new_v[n] for n in TWIN_WEIGHTS])
```

```python
import functools

import jax
import jax.numpy as jnp
from jax import lax
from jax.experimental import pallas as pl
from jax.experimental.pallas import tpu as pltpu

F32 = jnp.float32
BF16 = jnp.bfloat16
MESH = pl.DeviceIdType.MESH
HIGHEST = lax.Precision.HIGHEST

D_MODEL = 1024
N_HEADS = 4
D_NOPE = 128
D_ROPE = 64
D_HEAD = D_NOPE + D_ROPE
D_HEAD_PAD = 256
D_V = 128
R_Q = 256
R_KV = 128
D_ATTN = 512
D_POOL = 512
POOL_WINDOWS = (2, 4, 8, 16)
GROUP_DIM = 128
GRID_W = 64
ROPE_BASE = 10000.0
NORM_EPS = 1e-6
ATTN_SCALE = D_HEAD ** -0.5
D_IN_PROJ = 1984
U_PAD = 2048
O_GA, O_PIN, O_GP, O_CQ, O_CKV, O_KR = 0, 512, 1024, 1536, 1792, 1920
TILE = 256
Q_BLOCK = 128
HALO = 8
N_DEV = 8
VMEM_LIMIT = 56 * 1024 * 1024

ADAM_LR = 0.001
ADAM_B1 = 0.9
ADAM_B2 = 0.999
ADAM_EPS = 1e-08
ADAM_WD = 0.01
ADAM_STEP = 10

ROWS_WIN, ROWS_WUQ, ROWS_WUKV, ROWS_WOUT, ROWS_WPOOL = 248, 24, 16, 128, 8
ROWS_W = ROWS_WIN + ROWS_WUQ + ROWS_WUKV + ROWS_WOUT
ROWS_G = ROWS_W + ROWS_WPOOL
SMALL_ROWS = 16


def _dot(a, b):
    return lax.dot_general(a, b, (((1,), (0,)), ((), ())), preferred_element_type=F32)


def _dot_nt(a, b):
    return lax.dot_general(a, b, (((1,), (1,)), ((), ())), preferred_element_type=F32)


def _dot_tn(a, b):
    return lax.dot_general(a, b, (((0,), (0,)), ((), ())), preferred_element_type=F32)


def _sigmoid(x):
    return 1.0 / (1.0 + jnp.exp(-x))


def _rope(p, cos, slo, shi):
    return p * cos + pltpu.roll(p, 112, 1) * slo + pltpu.roll(p, 16, 1) * shi


def _rope_t(d, cos, slo, shi):
    return d * cos + pltpu.roll(d * slo, 16, 1) + pltpu.roll(d * shi, 112, 1)


def _shift_rows(a, k):
    n = a.shape[0]
    return pltpu.roll(a, (-k) % n, 0)


def _window_sum(x, w, transposed):
    s = (x + _shift_rows(x, 1)) if transposed else (_shift_rows(x, -1) + x)
    step = 1
    while 2 * step < w:
        s = _shift_rows(s, -step) + _shift_rows(s, step)
        step *= 2
    return s


def _inv_count(tpos, w, seq):
    lo = jnp.maximum(tpos - w // 2, 0)
    hi = jnp.minimum(tpos - w // 2 + w, seq)
    return 1.0 / jnp.maximum(hi - lo, 1).astype(F32)


def _coords():
    return lax.axis_index("x"), lax.axis_index("y"), lax.axis_index("c")


def _flip(v, bit):
    return (1 - v) if bit else v


def _peer(k):
    x, y, c = _coords()
    return (_flip(x, (k >> 2) & 1), _flip(y, (k >> 1) & 1), _flip(c, k & 1))


def _lin(p):
    return 4 * p[0] + 2 * p[1] + p[2]


def _gather_to_all(src_ref, slots_ref, send_sems, recv_sems):
    me = _coords()
    copies = []
    for k in range(1, N_DEV):
        cp = pltpu.make_async_remote_copy(
            src_ref=src_ref, dst_ref=slots_ref.at[_lin(me)], send_sem=send_sems.at[k - 1], recv_sem=recv_sems.at[k - 1],
            device_id=_peer(k), device_id_type=MESH)
        cp.start()
        copies.append(cp)

    def wait_recv():
        for k in range(1, N_DEV):
            pltpu.make_async_remote_copy(
                src_ref=src_ref, dst_ref=slots_ref.at[_lin(_peer(k))], send_sem=send_sems.at[k - 1],
                recv_sem=recv_sems.at[k - 1], device_id=_peer(k), device_id_type=MESH).wait_recv()

    def wait_send():
        for cp in copies:
            cp.wait_send()

    return wait_recv, wait_send


def _prologue(c8, cctx8, w_mod_l, wpack):
    n_mod = w_mod_l.shape[1]

    def body(c8_ref, cctx_ref, wmod_ref, wpack_ref, cg_ref, modg_ref, wg_ref, modblk_ref,
             ssem_w, rsem_w, ssem_c, rsem_c, ssem_m, rsem_m):
        x, y, c = _coords()
        me = (x, y, c)
        sibling = (x, y, 1 - c)
        chips = [(1 - x, y), (x, 1 - y), (1 - x, 1 - y)]

        def wcopy(k, block, to, src=None):
            slot = wg_ref.at[_lin(block)]
            return pltpu.make_async_remote_copy(
                src_ref=slot if src is None else src, dst_ref=slot, send_sem=ssem_w.at[k], recv_sem=rsem_w.at[k],
                device_id=to, device_id_type=MESH)

        first = [wcopy(0, me, sibling, src=wpack_ref)]
        first += [wcopy(1 + j, me, (*chip, c), src=wpack_ref) for j, chip in enumerate(chips)]
        for cp in first:
            cp.start()
        wg_ref[_lin(me)] = wpack_ref[...]

        cg_ref[_lin(me)] = c8_ref[...]
        c_recv, c_send = _gather_to_all(c8_ref, cg_ref, ssem_c, rsem_c)
        c_recv()
        row = lax.broadcasted_iota(jnp.int32, (8, D_MODEL), 0)
        c_all = jnp.zeros((8, D_MODEL), F32)
        for d in range(N_DEV):
            c_all = c_all + jnp.where(row == d, cg_ref[d], 0.0)
        cc = cctx_ref[...]
        a = jnp.concatenate([c_all * _sigmoid(c_all), cc * _sigmoid(cc)], axis=0)
        modblk_ref[...] = lax.dot_general(a, wmod_ref[...], (((1,), (0,)), ((), ())), precision=HIGHEST,
                                          preferred_element_type=F32)
        modg_ref[_lin(me)] = modblk_ref[...]
        m_recv, m_send = _gather_to_all(modblk_ref, modg_ref, ssem_m, rsem_m)
        m_recv()

        passed = [wcopy(4 + j, (*chip, c), sibling) for j, chip in enumerate(chips)]
        for j, chip in enumerate(chips):
            wcopy(1 + j, (*chip, c), me).wait_recv()
            passed[j].start()
        wcopy(0, sibling, me).wait_recv()
        for j, chip in enumerate(chips):
            wcopy(4 + j, (*chip, 1 - c), me).wait_recv()
        for cp in first + passed:
            cp.wait_send()
        c_send()
        m_send()

    vm = pl.BlockSpec(memory_space=pltpu.VMEM)
    return pl.pallas_call(
        body, name="prologue_gather",
        out_shape=(jax.ShapeDtypeStruct((N_DEV, 8, D_MODEL), F32),
                   jax.ShapeDtypeStruct((N_DEV, 16, n_mod), F32),
                   jax.ShapeDtypeStruct((N_DEV,) + wpack.shape, wpack.dtype)),
        in_specs=[vm, vm, vm, vm], out_specs=(vm, vm, vm),
        scratch_shapes=[pltpu.VMEM((16, n_mod), F32),
                        pltpu.SemaphoreType.DMA((7,)), pltpu.SemaphoreType.DMA((7,)),
                        pltpu.SemaphoreType.DMA((7,)), pltpu.SemaphoreType.DMA((7,)),
                        pltpu.SemaphoreType.DMA((7,)), pltpu.SemaphoreType.DMA((7,))],
        compiler_params=pltpu.CompilerParams(vmem_limit_bytes=VMEM_LIMIT),
    )(c8, cctx8, w_mod_l, wpack)


def _modulated_input(i, x_ref, ctx_ref, mod_ref, bmod_ref, ng_ref):
    is_ctx = i == 0
    xt = jnp.where(is_ctx, ctx_ref[...], x_ref[...])
    shift = jnp.where(is_ctx, mod_ref[3:4, :] + bmod_ref[3:4, :], mod_ref[0:1, :] + bmod_ref[0:1, :])
    scale = jnp.where(is_ctx, mod_ref[4:5, :] + bmod_ref[4:5, :], mod_ref[1:2, :] + bmod_ref[1:2, :])
    r = lax.rsqrt(jnp.mean(xt * xt, axis=-1, keepdims=True) + NORM_EPS)
    xg = (xt * r) * ng_ref[...]
    h = xg * (1.0 + scale) + shift
    return xt, r, xg, h, scale


def _inproj_fwd(x, ctx, modrows, bmodrows, norm_g, w_in_p, q_lora_g, w_uq_p, kv_lora_g, w_ukv, qng_p, kng_p, cos, slo, shi):
    seq = x.shape[0]
    n_tiles = seq // TILE + 1
    tot = seq + TILE

    def body(x_ref, ctx_ref, mod_ref, bmod_ref, ng_ref, win_ref, qlg_ref, wuq_ref, kvlg_ref, wukv_ref, qng_ref, kng_ref,
             cos_ref, slo_ref, shi_ref, u_ref, q_ref, k_ref, v_ref):
        i = pl.program_id(0)
        _, _, _, h, _ = _modulated_input(i, x_ref, ctx_ref, mod_ref, bmod_ref, ng_ref)
        u = _dot(h.astype(BF16), win_ref[...])
        u_ref[...] = u
        cos_t, slo_t, shi_t = cos_ref[...], slo_ref[...], shi_ref[...]

        cq = u[:, O_CQ:O_CQ + R_Q]
        qn = cq * lax.rsqrt(jnp.mean(cq * cq, axis=-1, keepdims=True) + NORM_EPS) * qlg_ref[...]
        q = _dot(qn.astype(BF16), wuq_ref[...])
        qg = qng_ref[...]
        for hd in range(N_HEADS):
            qh = q[:, hd * D_HEAD_PAD:(hd + 1) * D_HEAD_PAD]
            rr = lax.rsqrt(jnp.sum(qh * qh, axis=-1, keepdims=True) * (1.0 / D_HEAD) + NORM_EPS)
            qy = qh * rr * qg
            q_nope = qy[:, 0:D_NOPE].astype(BF16)
            q_rope = _rope(qy[:, D_NOPE:D_HEAD_PAD], cos_t, slo_t, shi_t).astype(BF16)
            for half in range(TILE // Q_BLOCK):
                rows = slice(half * Q_BLOCK, (half + 1) * Q_BLOCK)
                q_ref[hd, :, half * D_HEAD_PAD:half * D_HEAD_PAD + D_NOPE] = q_nope[rows]
                q_ref[hd, :, half * D_HEAD_PAD + D_NOPE:(half + 1) * D_HEAD_PAD] = q_rope[rows]

        ckv = u[:, O_CKV:O_CKV + R_KV]
        kvn = ckv * lax.rsqrt(jnp.mean(ckv * ckv, axis=-1, keepdims=True) + NORM_EPS) * kvlg_ref[...]
        kv = _dot(kvn.astype(BF16), wukv_ref[...])
        krz = u[:, O_KR:U_PAD]
        kr_ss = jnp.sum(krz * krz, axis=-1, keepdims=True)
        kg = kng_ref[...]
        for hd in range(N_HEADS):
            kn = kv[:, hd * 256:hd * 256 + D_NOPE]
            rr = lax.rsqrt((jnp.sum(kn * kn, axis=-1, keepdims=True) + kr_ss) * (1.0 / D_HEAD) + NORM_EPS)
            k_ref[hd, :, 0:D_NOPE] = (kn * rr * kg[:, 0:D_NOPE]).astype(BF16)
            k_ref[hd, :, D_NOPE:D_HEAD_PAD] = _rope(krz * rr * kg[:, D_NOPE:D_HEAD_PAD], cos_t, slo_t, shi_t).astype(BF16)
            v_ref[hd] = kv[:, hd * 256 + D_NOPE:(hd + 1) * 256].astype(BF16)

    lat = lambda i: (jnp.maximum(i - 1, 0), 0)
    full = lambda shape: pl.BlockSpec(shape, lambda i: (0,) * len(shape))
    return pl.pallas_call(
        body, name="inproj_fwd", grid=(n_tiles,),
        out_shape=(jax.ShapeDtypeStruct((tot, U_PAD), F32),
                   jax.ShapeDtypeStruct((N_HEADS, Q_BLOCK, seq // Q_BLOCK * D_HEAD_PAD), BF16),
                   jax.ShapeDtypeStruct((N_HEADS, tot, D_HEAD_PAD), BF16),
                   jax.ShapeDtypeStruct((N_HEADS, tot, D_V), BF16)),
        in_specs=[pl.BlockSpec((TILE, D_MODEL), lat), full((TILE, D_MODEL)), full((8, D_MODEL)), full((8, D_MODEL)),
                  full((1, D_MODEL)), full((D_MODEL, U_PAD)), full((1, R_Q)), full((R_Q, N_HEADS * D_HEAD_PAD)),
                  full((1, R_KV)), full((R_KV, N_HEADS * 256)), full((1, D_HEAD_PAD)), full((1, D_HEAD_PAD)),
                  pl.BlockSpec((TILE, 128), lambda i: (i, 0)), pl.BlockSpec((TILE, 128), lambda i: (i, 0)),
                  pl.BlockSpec((TILE, 128), lambda i: (i, 0))],
        out_specs=(pl.BlockSpec((TILE, U_PAD), lambda i: (i, 0)),
                   pl.BlockSpec((N_HEADS, Q_BLOCK, TILE // Q_BLOCK * D_HEAD_PAD), lambda i: (0, 0, jnp.maximum(i - 1, 0))),
                   pl.BlockSpec((N_HEADS, TILE, D_HEAD_PAD), lambda i: (0, i, 0)),
                   pl.BlockSpec((N_HEADS, TILE, D_V), lambda i: (0, i, 0))),
        compiler_params=pltpu.CompilerParams(dimension_semantics=("arbitrary",), vmem_limit_bytes=VMEM_LIMIT),
    )(x, ctx, modrows, bmodrows, norm_g, w_in_p, q_lora_g, w_uq_p, kv_lora_g, w_ukv, qng_p, kng_p, cos, slo, shi)


def _attn_fwd(q, k, v, block_q):
    _, seq, _ = q.shape
    n_keys = k.shape[1]

    def body(q_ref, k_ref, v_ref, o_ref, lse_ref):
        s = _dot_nt(q_ref[0], k_ref[0]) * ATTN_SCALE
        m = jnp.max(s, axis=-1, keepdims=True)
        p = jnp.exp(s - m)
        l = jnp.sum(p, axis=-1, keepdims=True)
        o_ref[...] = _dot(p.astype(BF16), v_ref[0]) * (1.0 / l)
        lse_ref[0] = m + jnp.log(l)

    return pl.pallas_call(
        body, name="attn_fwd", grid=(N_HEADS, seq // block_q),
        out_shape=(jax.ShapeDtypeStruct((seq, D_ATTN), F32), jax.ShapeDtypeStruct((N_HEADS, seq, 1), F32)),
        in_specs=[pl.BlockSpec((1, block_q, D_HEAD_PAD), lambda h, i: (h, i, 0)),
                  pl.BlockSpec((1, n_keys, D_HEAD_PAD), lambda h, i: (h, 0, 0)),
                  pl.BlockSpec((1, n_keys, D_V), lambda h, i: (h, 0, 0))],
        out_specs=(pl.BlockSpec((block_q, D_V), lambda h, i: (i, h)),
                   pl.BlockSpec((1, block_q, 1), lambda h, i: (h, i, 0))),
        compiler_params=pltpu.CompilerParams(dimension_semantics=("arbitrary", "arbitrary"), vmem_limit_bytes=VMEM_LIMIT),
    )(q, k, v)


def _attn_bwd(q, k, v, o, lse, d_o, block_q):
    _, seq, _ = q.shape
    n_keys = k.shape[1]

    def body(q_ref, k_ref, v_ref, o_ref, lse_ref, do_ref, dq_ref, dk_ref, dv_ref):
        i = pl.program_id(1)

        @pl.when(i == 0)
        def _():
            dk_ref[...] = jnp.zeros_like(dk_ref)
            dv_ref[...] = jnp.zeros_like(dv_ref)

        qb, kb, vb = q_ref[0], k_ref[0], v_ref[0]
        d_out = do_ref[...]
        p = jnp.exp(_dot_nt(qb, kb) * ATTN_SCALE - lse_ref[0])
        dob = d_out.astype(BF16)
        dp = _dot_nt(dob, vb)
        delta = jnp.sum(d_out * o_ref[...], axis=-1, keepdims=True)
        ds = (p * (dp - delta) * ATTN_SCALE).astype(BF16)
        dq_ref[0] = _dot(ds, kb)
        dk_ref[0] += _dot_tn(ds, qb)
        dv_ref[0] += _dot_tn(p.astype(BF16), dob)

    return pl.pallas_call(
        body, name="attn_bwd", grid=(N_HEADS, seq // block_q),
        out_shape=(jax.ShapeDtypeStruct((N_HEADS, seq, D_HEAD_PAD), F32),
                   jax.ShapeDtypeStruct((N_HEADS, n_keys, D_HEAD_PAD), F32),
                   jax.ShapeDtypeStruct((N_HEADS, n_keys, D_V), F32)),
        in_specs=[pl.BlockSpec((1, block_q, D_HEAD_PAD), lambda h, i: (h, i, 0)),
                  pl.BlockSpec((1, n_keys, D_HEAD_PAD), lambda h, i: (h, 0, 0)),
                  pl.BlockSpec((1, n_keys, D_V), lambda h, i: (h, 0, 0)),
                  pl.BlockSpec((block_q, D_V), lambda h, i: (i, h)),
                  pl.BlockSpec((1, block_q, 1), lambda h, i: (h, i, 0)),
                  pl.BlockSpec((block_q, D_V), lambda h, i: (i, h))],
        out_specs=(pl.BlockSpec((1, block_q, D_HEAD_PAD), lambda h, i: (h, i, 0)),
                   pl.BlockSpec((1, n_keys, D_HEAD_PAD), lambda h, i: (h, 0, 0)),
                   pl.BlockSpec((1, n_keys, D_V), lambda h, i: (h, 0, 0))),
        compiler_params=pltpu.CompilerParams(dimension_semantics=("arbitrary", "arbitrary"), vmem_limit_bytes=VMEM_LIMIT),
    )(q, k, v, o, lse, d_o)


def _mix(x, target, attn, u, modrows, bmodrows, w_pool, pool_scale, w_out):
    seq = x.shape[0]
    n_tiles = seq // TILE
    rows8 = TILE // HALO
    last8 = (seq + TILE) // HALO - 1

    def body(x_ref, t_ref, o_ref, ga_ref, pin_ref, hb_ref, ha_ref, gp_ref, mod_ref, bmod_ref, wp_ref, ps_ref, wo_ref,
             loss_ref, g1_ref, dgate_ref, do_ref, dga_ref, dgp_ref, dpl_ref, gwo_ref, gwp_ref, dps_ref):
        j = pl.program_id(0)

        @pl.when(j == 0)
        def _():
            loss_ref[...] = jnp.zeros_like(loss_ref)
            dgate_ref[...] = jnp.zeros_like(dgate_ref)
            gwo_ref[...] = jnp.zeros_like(gwo_ref)
            gwp_ref[...] = jnp.zeros_like(gwp_ref)
            dps_ref[...] = jnp.zeros_like(dps_ref)

        gate = mod_ref[2:3, :] + bmod_ref[2:3, :]
        ga = ga_ref[...]
        sga = _sigmoid(ga)
        silu_ga = ga * sga
        attn_t = o_ref[...]
        br_a = silu_ga * attn_t

        pin = pin_ref[...]
        xs = jnp.concatenate([jnp.where(j > 0, hb_ref[...], 0.0), pin, jnp.where(j < n_tiles - 1, ha_ref[...], 0.0)], axis=0)
        tpos = j * TILE + lax.broadcasted_iota(jnp.int32, (TILE, 1), 0)
        ps = ps_ref[...]
        pooled, yps = [], []
        for g, w in enumerate(POOL_WINDOWS):
            sl = slice(g * GROUP_DIM, (g + 1) * GROUP_DIM)
            ws = _window_sum(xs[:, sl], w, False)[HALO:HALO + TILE]
            pg = (ws * _inv_count(tpos, w, seq) - pin[:, sl]).astype(BF16)
            pooled.append(pg)
            yps.append(_dot(pg, wp_ref[g].astype(BF16)))
        yp = jnp.concatenate(yps, axis=1)
        ypool = yp * ps
        gp = gp_ref[...]
        sgp = _sigmoid(gp)
        silu_gp = gp * sgp
        br_p = silu_gp * ypool

        z = jnp.concatenate([br_a, br_p], axis=1).astype(BF16)
        y = _dot(z, wo_ref[...])
        res = x_ref[...] + gate * y - t_ref[...]
        loss_ref[...] += jnp.sum(res * res)
        dxn = res * (1.0 / D_MODEL)
        g1_ref[...] = dxn
        dgate_ref[...] += jnp.sum(dxn * y, axis=0, keepdims=True)
        dy = (gate * dxn).astype(BF16)
        dz = _dot_nt(dy, wo_ref[...])
        gwo_ref[...] += _dot_tn(z, dy)

        dbra = dz[:, 0:D_ATTN]
        dbrp = dz[:, D_ATTN:]
        do_ref[...] = dbra * silu_ga
        dga_ref[...] = dbra * attn_t * (sga * (1.0 + ga * (1.0 - sga)))
        dgp_ref[...] = dbrp * ypool * (sgp * (1.0 + gp * (1.0 - sgp)))
        dyp_s = dbrp * silu_gp
        dps_ref[...] += jnp.sum(dyp_s * yp, axis=0, keepdims=True)
        dyp = (dyp_s * ps).astype(BF16)
        for g in range(len(POOL_WINDOWS)):
            sl = slice(g * GROUP_DIM, (g + 1) * GROUP_DIM)
            gwp_ref[g] += _dot_tn(pooled[g], dyp[:, sl])
            dpl_ref[:, sl] = _dot_nt(dyp[:, sl], wp_ref[g].astype(BF16))

    tile = lambda w: pl.BlockSpec((TILE, w), lambda j: (j, 0))
    ucol = lambda col: pl.BlockSpec((TILE, 512), lambda j: (j + 1, col))
    full = lambda shape: pl.BlockSpec(shape, lambda j: (0,) * len(shape))
    return pl.pallas_call(
        body, name="mix_fwd_bwd", grid=(n_tiles,),
        out_shape=(jax.ShapeDtypeStruct((8, 128), F32), jax.ShapeDtypeStruct((seq, D_MODEL), F32),
                   jax.ShapeDtypeStruct((1, D_MODEL), F32), jax.ShapeDtypeStruct((seq, D_ATTN), F32),
                   jax.ShapeDtypeStruct((seq, D_ATTN), F32), jax.ShapeDtypeStruct((seq, D_POOL), F32),
                   jax.ShapeDtypeStruct((seq, D_POOL), F32), jax.ShapeDtypeStruct((D_MODEL, D_MODEL), F32),
                   jax.ShapeDtypeStruct((4, GROUP_DIM, GROUP_DIM), F32), jax.ShapeDtypeStruct((1, D_POOL), F32)),
        in_specs=[tile(D_MODEL), tile(D_MODEL), tile(D_ATTN), ucol(0), ucol(1),
                  pl.BlockSpec((HALO, 512), lambda j: ((j + 1) * rows8 - 1, 1)),
                  pl.BlockSpec((HALO, 512), lambda j: (jnp.minimum((j + 2) * rows8, last8), 1)),
                  ucol(2), full((8, D_MODEL)), full((8, D_MODEL)), full((4, GROUP_DIM, GROUP_DIM)), full((1, D_POOL)),
                  full((D_MODEL, D_MODEL))],
        out_specs=(full((8, 128)), tile(D_MODEL), full((1, D_MODEL)), tile(D_ATTN), tile(D_ATTN), tile(D_POOL),
                   tile(D_POOL), full((D_MODEL, D_MODEL)), full((4, GROUP_DIM, GROUP_DIM)), full((1, D_POOL))),
        compiler_params=pltpu.CompilerParams(dimension_semantics=("arbitrary",), vmem_limit_bytes=VMEM_LIMIT),
    )(x, target, attn, u, u, u, u, u, modrows, bmodrows, w_pool, pool_scale, w_out)


def _inproj_bwd(x, ctx, modrows, bmodrows, norm_g, w_in_p, q_lora_g, w_uq_p, kv_lora_g, w_ukv, qng_p, kng_p, cos, slo, shi,
                u, dq, dk, dv, dga, dgp, dpl, g1):
    seq = x.shape[0]
    n_tiles = seq // TILE + 1
    rows8 = TILE // HALO
    last8 = seq // HALO - 1

    def body(x_ref, ctx_ref, mod_ref, bmod_ref, ng_ref, win_ref, qlg_ref, wuq_ref, kvlg_ref, wukv_ref, qng_ref, kng_ref,
             cos_ref, slo_ref, shi_ref, cq_ref, ckv_ref, kr_ref, dq_ref, dk_ref, dv_ref, dga_ref, dgp_ref, dpl_ref,
             hb_ref, ha_ref, g1_ref,
             gx_ref, gwin_ref, gwuq_ref, gwukv_ref, dqlg_ref, dkvlg_ref, dqng_ref, dkng_ref, dng_ref, dmod_ref,
             du_ref):
        i = pl.program_id(0)
        is_lat = i > 0

        @pl.when(i == 0)
        def _():
            for ref in (gwin_ref, gwuq_ref, gwukv_ref, dqlg_ref, dkvlg_ref, dqng_ref, dkng_ref, dng_ref, dmod_ref):
                ref[...] = jnp.zeros_like(ref)

        xt, r, xg, h, scale = _modulated_input(i, x_ref, ctx_ref, mod_ref, bmod_ref, ng_ref)
        hb = h.astype(BF16)
        cos_t, slo_t, shi_t = cos_ref[...], slo_ref[...], shi_ref[...]

        cq = cq_ref[...]
        rq = lax.rsqrt(jnp.mean(cq * cq, axis=-1, keepdims=True) + NORM_EPS)
        qhat = cq * rq
        qnb = (qhat * qlg_ref[...]).astype(BF16)
        q = _dot(qnb, wuq_ref[...])
        qg = qng_ref[...]
        dq_heads = []
        dqng = jnp.zeros((1, D_HEAD_PAD), F32)
        for hd in range(N_HEADS):
            qh = q[:, hd * D_HEAD_PAD:(hd + 1) * D_HEAD_PAD]
            rr = lax.rsqrt(jnp.sum(qh * qh, axis=-1, keepdims=True) * (1.0 / D_HEAD) + NORM_EPS)
            yq = qh * rr
            dq_h = dq_ref[hd]
            dq_h = jnp.concatenate([dq_h[:, half * D_HEAD_PAD:(half + 1) * D_HEAD_PAD] for half in range(TILE // Q_BLOCK)],
                                   axis=0)
            dpost = jnp.where(is_lat, dq_h, 0.0)
            dyn = jnp.concatenate([dpost[:, 0:D_NOPE], _rope_t(dpost[:, D_NOPE:], cos_t, slo_t, shi_t)], axis=1)
            dqng = dqng + jnp.sum(dyn * yq, axis=0, keepdims=True)
            dyg = dyn * qg
            dq_heads.append(rr * (dyg - yq * (jnp.sum(dyg * yq, axis=-1, keepdims=True) * (1.0 / D_HEAD))))
        dqng_ref[...] += dqng
        dqf = jnp.concatenate(dq_heads, axis=1).astype(BF16)
        gwuq_ref[...] += _dot_tn(dqf, qnb)
        dqn = _dot_nt(dqf, wuq_ref[...])
        dqlg_ref[...] += jnp.sum(dqn * qhat, axis=0, keepdims=True)
        dqhat = dqn * qlg_ref[...]
        dcq = rq * (dqhat - qhat * jnp.mean(dqhat * qhat, axis=-1, keepdims=True))

        ckv = ckv_ref[...]
        rkv = lax.rsqrt(jnp.mean(ckv * ckv, axis=-1, keepdims=True) + NORM_EPS)
        kvhat = ckv * rkv
        kvnb = (kvhat * kvlg_ref[...]).astype(BF16)
        kv = _dot(kvnb, wukv_ref[...])
        krz = kr_ref[...]
        kr_ss = jnp.sum(krz * krz, axis=-1, keepdims=True)
        kg = kng_ref[...]
        kg_n, kg_r = kg[:, 0:D_NOPE], kg[:, D_NOPE:]
        dkv_parts = []
        dkr = jnp.zeros((TILE, 128), F32)
        dkng_n = jnp.zeros((1, D_NOPE), F32)
        dkng_r = jnp.zeros((1, 128), F32)
        for hd in range(N_HEADS):
            kn = kv[:, hd * 256:hd * 256 + D_NOPE]
            rr = lax.rsqrt((jnp.sum(kn * kn, axis=-1, keepdims=True) + kr_ss) * (1.0 / D_HEAD) + NORM_EPS)
            yn, yr = kn * rr, krz * rr
            dk_h = dk_ref[hd]
            dpn = dk_h[:, 0:D_NOPE]
            dpr = _rope_t(dk_h[:, D_NOPE:], cos_t, slo_t, shi_t)
            dkng_n = dkng_n + jnp.sum(dpn * yn, axis=0, keepdims=True)
            dkng_r = dkng_r + jnp.sum(dpr * yr, axis=0, keepdims=True)
            dyn, dyr = dpn * kg_n, dpr * kg_r
            proj = (jnp.sum(dyn * yn, axis=-1, keepdims=True) + jnp.sum(dyr * yr, axis=-1, keepdims=True)) * (1.0 / D_HEAD)
            dkv_parts.append(rr * (dyn - yn * proj))
            dkv_parts.append(dv_ref[hd])
            dkr = dkr + rr * (dyr - yr * proj)
        dkng_ref[:, 0:D_NOPE] += dkng_n
        dkng_ref[:, D_NOPE:] += dkng_r
        dkvf = jnp.concatenate(dkv_parts, axis=1).astype(BF16)
        gwukv_ref[...] += _dot_tn(dkvf, kvnb)
        dkvn = _dot_nt(dkvf, wukv_ref[...])
        dkvlg_ref[...] += jnp.sum(dkvn * kvhat, axis=0, keepdims=True)
        dkvhat = dkvn * kvlg_ref[...]
        dckv = rkv * (dkvhat - kvhat * jnp.mean(dkvhat * kvhat, axis=-1, keepdims=True))

        lat_t = jnp.maximum(i - 1, 0)
        dpl_t = dpl_ref[...]
        ds = jnp.concatenate([jnp.where(i > 1, hb_ref[...], 0.0), dpl_t,
                              jnp.where(i < n_tiles - 1, ha_ref[...], 0.0)], axis=0)
        tpos = lat_t * TILE - HALO + lax.broadcasted_iota(jnp.int32, (TILE + 2 * HALO, 1), 0)
        for g, w in enumerate(POOL_WINDOWS):
            sl = slice(g * GROUP_DIM, (g + 1) * GROUP_DIM)
            wsum = _window_sum(ds[:, sl] * _inv_count(tpos, w, seq), w, True)[HALO:HALO + TILE]
            du_ref[:, O_PIN + g * GROUP_DIM:O_PIN + (g + 1) * GROUP_DIM] = jnp.where(
                is_lat, wsum - dpl_t[:, sl], 0.0).astype(BF16)

        du_ref[:, O_GA:O_GA + D_ATTN] = jnp.where(is_lat, dga_ref[...], 0.0).astype(BF16)
        du_ref[:, O_GP:O_GP + D_POOL] = jnp.where(is_lat, dgp_ref[...], 0.0).astype(BF16)
        du_ref[:, O_CQ:O_CQ + R_Q] = dcq.astype(BF16)
        du_ref[:, O_CKV:O_CKV + R_KV] = dckv.astype(BF16)
        du_ref[:, O_KR:U_PAD] = dkr.astype(BF16)
        dub = du_ref[...]
        dh = _dot_nt(dub, win_ref[...])
        gwin_ref[...] += _dot_tn(dub, hb)

        dshift = jnp.sum(dh, axis=0, keepdims=True)
        dscale = jnp.sum(dh * xg, axis=0, keepdims=True)
        zero = jnp.zeros_like(dshift)
        dmod_ref[0:1, :] += jnp.where(is_lat, dshift, zero)
        dmod_ref[1:2, :] += jnp.where(is_lat, dscale, zero)
        dmod_ref[2:3, :] += jnp.where(is_lat, zero, dshift)
        dmod_ref[3:4, :] += jnp.where(is_lat, zero, dscale)
        dxg = dh * (1.0 + scale)
        xr = xt * r
        dng_ref[...] += jnp.sum(dxg * xr, axis=0, keepdims=True)
        dxn = dxg * ng_ref[...]
        gx_ref[...] = g1_ref[...] + r * (dxn - xr * jnp.mean(dxn * xr, axis=-1, keepdims=True))

    lat = lambda i: (jnp.maximum(i - 1, 0), 0)
    full = lambda shape: pl.BlockSpec(shape, lambda i: (0,) * len(shape))
    rope_spec = pl.BlockSpec((TILE, 128), lambda i: (i, 0))
    return pl.pallas_call(
        body, name="inproj_bwd", grid=(n_tiles,),
        out_shape=(jax.ShapeDtypeStruct((seq, D_MODEL), F32), jax.ShapeDtypeStruct((U_PAD, D_MODEL), F32),
                   jax.ShapeDtypeStruct((N_HEADS * D_HEAD_PAD, R_Q), F32), jax.ShapeDtypeStruct((N_HEADS * 256, R_KV), F32),
                   jax.ShapeDtypeStruct((1, R_Q), F32), jax.ShapeDtypeStruct((1, R_KV), F32),
                   jax.ShapeDtypeStruct((1, D_HEAD_PAD), F32), jax.ShapeDtypeStruct((1, D_HEAD_PAD), F32),
                   jax.ShapeDtypeStruct((1, D_MODEL), F32), jax.ShapeDtypeStruct((8, D_MODEL), F32)),
        in_specs=[pl.BlockSpec((TILE, D_MODEL), lat), full((TILE, D_MODEL)), full((8, D_MODEL)), full((8, D_MODEL)),
                  full((1, D_MODEL)), full((D_MODEL, U_PAD)), full((1, R_Q)), full((R_Q, N_HEADS * D_HEAD_PAD)),
                  full((1, R_KV)), full((R_KV, N_HEADS * 256)), full((1, D_HEAD_PAD)), full((1, D_HEAD_PAD)),
                  rope_spec, rope_spec, rope_spec,
                  pl.BlockSpec((TILE, R_Q), lambda i: (i, O_CQ // R_Q)),
                  pl.BlockSpec((TILE, R_KV), lambda i: (i, O_CKV // R_KV)),
                  pl.BlockSpec((TILE, 128), lambda i: (i, O_KR // 128)),
                  pl.BlockSpec((N_HEADS, Q_BLOCK, TILE // Q_BLOCK * D_HEAD_PAD), lambda i: (0, 0, jnp.maximum(i - 1, 0))),
                  pl.BlockSpec((N_HEADS, TILE, D_HEAD_PAD), lambda i: (0, i, 0)),
                  pl.BlockSpec((N_HEADS, TILE, D_V), lambda i: (0, i, 0)),
                  pl.BlockSpec((TILE, D_ATTN), lat), pl.BlockSpec((TILE, D_POOL), lat), pl.BlockSpec((TILE, D_POOL), lat),
                  pl.BlockSpec((HALO, D_POOL), lambda i: (jnp.maximum((i - 1) * rows8 - 1, 0), 0)),
                  pl.BlockSpec((HALO, D_POOL), lambda i: (jnp.minimum(jnp.maximum(i, 1) * rows8, last8), 0)),
                  pl.BlockSpec((TILE, D_MODEL), lat)],
        out_specs=(pl.BlockSpec((TILE, D_MODEL), lat), full((U_PAD, D_MODEL)), full((N_HEADS * D_HEAD_PAD, R_Q)),
                   full((N_HEADS * 256, R_KV)), full((1, R_Q)), full((1, R_KV)), full((1, D_HEAD_PAD)),
                   full((1, D_HEAD_PAD)), full((1, D_MODEL)), full((8, D_MODEL))),
        scratch_shapes=[pltpu.VMEM((TILE, U_PAD), BF16)],
        compiler_params=pltpu.CompilerParams(dimension_semantics=("arbitrary",), vmem_limit_bytes=VMEM_LIMIT),
    )(x, ctx, modrows, bmodrows, norm_g, w_in_p, q_lora_g, w_uq_p, kv_lora_g, w_ukv, qng_p, kng_p, cos, slo, shi,
      u, u, u, dq, dk, dv, dga, dgp, dpl, dpl, dpl, g1)


def _epilogue(bigpack, smallpack, w_mod_l):
    rows = bigpack.shape[2]
    n_mod = w_mod_l.shape[1]

    def body(big_ref, small_ref, wmod_ref, red_ref, smallg_ref, tot_ref, p2g_ref,
             recv1_ref, sum1_ref, recv2_ref, p2_ref,
             ssem1, rsem1, ssem2, rsem2, ssem_s, rsem_s, ssem_p, rsem_p):
        x, y, c = _coords()
        me = (x, y, c)
        sibling = (x, y, 1 - c)

        cp1 = pltpu.make_async_remote_copy(src_ref=big_ref.at[1 - c], dst_ref=recv1_ref, send_sem=ssem1, recv_sem=rsem1,
                                           device_id=sibling, device_id_type=MESH)
        cp1.start()

        smallg_ref[_lin(me)] = small_ref[...]
        s_recv, s_send = _gather_to_all(small_ref, smallg_ref, ssem_s, rsem_s)
        s_recv()
        tot = smallg_ref[0]
        for d in range(1, N_DEV):
            tot = tot + smallg_ref[d]
        tot_ref[...] = tot
        dmodc = jnp.concatenate([tot_ref[9:10, :], tot_ref[10:11, :], tot_ref[11:12, :]], axis=1)
        lin = _lin(me)
        dmy = jnp.zeros((1, n_mod), F32)
        for d in range(N_DEV):
            dmy = dmy + jnp.where(lin == d, dmodc[:, d * n_mod:(d + 1) * n_mod], 0.0)
        part = lax.dot_general(jnp.broadcast_to(dmy, (8, n_mod)), wmod_ref[...], (((1,), (1,)), ((), ())),
                               precision=HIGHEST, preferred_element_type=F32)

        cp1.wait_recv()
        for b in range(4):
            sum1_ref[b] = big_ref[c, b] + recv1_ref[b]

        copies = []
        for k in (1, 2, 3):
            tx, ty = _flip(x, (k >> 1) & 1), _flip(y, k & 1)
            cp = pltpu.make_async_remote_copy(src_ref=sum1_ref.at[2 * tx + ty], dst_ref=recv2_ref.at[k - 1],
                                              send_sem=ssem2.at[k - 1], recv_sem=rsem2.at[k - 1],
                                              device_id=(tx, ty, c), device_id_type=MESH)
            cp.start()
            copies.append(cp)
        for cp in copies:
            cp.wait_recv()
        red_ref[...] = sum1_ref[2 * x + y] + recv2_ref[0] + recv2_ref[1] + recv2_ref[2]

        p2_ref[0:8, :] = red_ref[rows - ROWS_WPOOL:rows, :]
        p2_ref[8:16, :] = jnp.where(lax.broadcasted_iota(jnp.int32, (8, D_MODEL), 0) == 0, part, 0.0)
        p2g_ref[_lin(me)] = p2_ref[...]
        p_recv, p_send = _gather_to_all(p2_ref, p2g_ref, ssem_p, rsem_p)
        p_recv()
        cp1.wait_send()
        for cp in copies:
            cp.wait_send()
        s_send()
        p_send()

    vm = pl.BlockSpec(memory_space=pltpu.VMEM)
    return pl.pallas_call(
        body, name="epilogue_reduce",
        out_shape=(jax.ShapeDtypeStruct((rows, D_MODEL), F32), jax.ShapeDtypeStruct((N_DEV, SMALL_ROWS, D_MODEL), F32),
                   jax.ShapeDtypeStruct((SMALL_ROWS, D_MODEL), F32), jax.ShapeDtypeStruct((N_DEV, SMALL_ROWS, D_MODEL), F32)),
        in_specs=[vm, vm, vm], out_specs=(vm, vm, vm, vm),
        scratch_shapes=[pltpu.VMEM((4, rows, D_MODEL), F32), pltpu.VMEM((4, rows, D_MODEL), F32),
                        pltpu.VMEM((3, rows, D_MODEL), F32), pltpu.VMEM((SMALL_ROWS, D_MODEL), F32),
                        pltpu.SemaphoreType.DMA, pltpu.SemaphoreType.DMA,
                        pltpu.SemaphoreType.DMA((3,)), pltpu.SemaphoreType.DMA((3,)),
                        pltpu.SemaphoreType.DMA((7,)), pltpu.SemaphoreType.DMA((7,)),
                        pltpu.SemaphoreType.DMA((7,)), pltpu.SemaphoreType.DMA((7,))],
        compiler_params=pltpu.CompilerParams(vmem_limit_bytes=VMEM_LIMIT),
    )(bigpack, smallpack, w_mod_l)


def _adamw(weights, grads, ms, vs, c_all_t, dmod_my, p2g, c_ctx_row):
    n = len(weights)

    def body(*refs):
        w_refs = refs[0:n]
        g_in = refs[n:2 * n - 2]
        m_refs = refs[2 * n - 2:3 * n - 2]
        v_refs = refs[3 * n - 2:4 * n - 2]
        cat_ref, dmod_ref, p2g_ref = refs[4 * n - 2:4 * n + 1]
        outs = refs[4 * n + 1:]
        gcc_ref, gwm_ref = outs[0], outs[1]
        d_refs, nm_refs, nv_refs = outs[2:2 + n], outs[2 + n:2 + 2 * n], outs[2 + 2 * n:2 + 3 * n]

        part = p2g_ref[0, 8:9, :]
        for d in range(1, N_DEV):
            part = part + p2g_ref[d, 8:9, :]
        cc = w_refs[0][...]
        sg = _sigmoid(cc)
        gcc_ref[...] = part * (sg * (1.0 + cc * (1.0 - sg)))
        cat = cat_ref[...]
        gwm_ref[...] = lax.dot_general(cat * _sigmoid(cat), dmod_ref[...], (((1,), (0,)), ((), ())), precision=HIGHEST,
                                       preferred_element_type=F32)
        for idx in range(n):
            g = (gcc_ref, gwm_ref)[idx][...] if idx < 2 else g_in[idx - 2][...]
            m = ADAM_B1 * m_refs[idx][...] + (1.0 - ADAM_B1) * g
            v = ADAM_B2 * v_refs[idx][...] + (1.0 - ADAM_B2) * (g * g)
            m_hat = m / (1.0 - ADAM_B1 ** ADAM_STEP)
            v_hat = v / (1.0 - ADAM_B2 ** ADAM_STEP)
            d_refs[idx][...] = -ADAM_LR * (m_hat / (jnp.sqrt(v_hat) + ADAM_EPS) + ADAM_WD * w_refs[idx][...])
            nm_refs[idx][...] = m
            nv_refs[idx][...] = v

    vm = pl.BlockSpec(memory_space=pltpu.VMEM)
    shapes = [jax.ShapeDtypeStruct(w.shape, F32) for w in weights]
    args = list(weights) + list(grads[2:]) + list(ms) + list(vs) + [c_all_t, dmod_my, p2g]
    del c_ctx_row
    out_shape = tuple(shapes[0:2] + shapes * 3)
    return pl.pallas_call(
        body, name="adamw_update", out_shape=out_shape, in_specs=[vm] * len(args), out_specs=tuple([vm] * len(out_shape)),
        compiler_params=pltpu.CompilerParams(vmem_limit_bytes=VMEM_LIMIT),
    )(*args)


def _rope_tables(seq):
    rows = seq // GRID_W
    row = jnp.repeat(jnp.arange(rows, dtype=F32), GRID_W)
    col = jnp.tile(jnp.arange(GRID_W, dtype=F32), rows)
    n_freq = D_ROPE // 4
    inv = ROPE_BASE ** (-jnp.arange(n_freq, dtype=F32) / n_freq)
    ang_r = row[:, None] * inv
    ang_c = col[:, None] * inv
    ang = jnp.concatenate([ang_r, ang_r, ang_c, ang_c], axis=-1)
    cos, sin = jnp.cos(ang), jnp.sin(ang)
    low = (jnp.arange(D_ROPE) % 32) < 16
    slo = jnp.where(low, -sin, 0.0)
    shi = jnp.where(low, 0.0, sin)
    pad = lambda t, fill: jnp.concatenate(
        [jnp.full((TILE, 128), fill, F32), jnp.concatenate([t, jnp.full((seq, 128 - D_ROPE), fill, F32)], axis=1)], axis=0)
    cos = jnp.concatenate([jnp.ones((TILE, 128), F32),
                           jnp.concatenate([cos, jnp.ones((seq, 128 - D_ROPE), F32)], axis=1)], axis=0)
    return cos, pad(slo, 0.0), pad(shi, 0.0)


def _pad_heads(a):
    lead = a.shape[:-1]
    a4 = a.reshape(lead + (N_HEADS, D_HEAD))
    a4 = jnp.concatenate([a4, jnp.zeros(lead + (N_HEADS, D_HEAD_PAD - D_HEAD), a.dtype)], axis=-1)
    return a4.reshape(lead + (N_HEADS * D_HEAD_PAD,))


def kernel(x, c, ctx, c_ctx, w_mod, b_mod, norm_g, w_in, q_lora_g, w_uq, kv_lora_g, w_ukv, q_norm_g, k_norm_g, w_pool, pool_scale, w_out, loss_target, m_c_ctx, m_w_mod, m_b_mod, m_norm_g, m_w_in, m_q_lora_g, m_w_uq, m_kv_lora_g, m_w_ukv, m_q_norm_g, m_k_norm_g, m_w_pool, m_pool_scale, m_w_out, v_c_ctx, v_w_mod, v_b_mod, v_norm_g, v_w_in, v_q_lora_g, v_w_uq, v_kv_lora_g, v_w_ukv, v_q_norm_g, v_k_norm_g, v_w_pool, v_pool_scale, v_w_out):
    seq = x.shape[1]
    assert ctx.shape[1] == TILE and seq % TILE == 0 and seq % GRID_W == 0
    ix, iy, ic = lax.axis_index("x"), lax.axis_index("y"), lax.axis_index("c")
    me = 4 * ix + 2 * iy + ic
    x2, ctx2, tgt2 = x[0], ctx[0], loss_target[0]
    w_mod_l, w_in_l, w_uq_l, w_ukv_l, w_out_l = w_mod[0], w_in[0], w_uq[0], w_ukv[0], w_out[0]
    n_mod = w_mod_l.shape[1]
    c_ctx_row = c_ctx.reshape(1, D_MODEL)

    wpack = jnp.concatenate([w_in_l.astype(BF16).reshape(ROWS_WIN, D_MODEL), w_uq_l.astype(BF16).reshape(ROWS_WUQ, D_MODEL),
                             w_ukv_l.astype(BF16).reshape(ROWS_WUKV, D_MODEL), w_out_l.astype(BF16)], axis=0)
    cg, modg, wg = _prologue(jnp.broadcast_to(c, (8, D_MODEL)), jnp.broadcast_to(c_ctx_row, (8, D_MODEL)), w_mod_l, wpack)
    mod_all = jnp.transpose(modg, (1, 0, 2)).reshape(16, 3 * D_MODEL)
    mod_b = lax.dynamic_slice_in_dim(mod_all, me, 1, axis=0).reshape(3, D_MODEL)
    mod_c = mod_all[8].reshape(3, D_MODEL)
    modrows = jnp.concatenate([mod_b, mod_c[0:2], jnp.zeros((3, D_MODEL), F32)], axis=0)
    b3 = b_mod.reshape(3, D_MODEL)
    bmodrows = jnp.concatenate([b3, b3[0:2], jnp.zeros((3, D_MODEL), F32)], axis=0)

    r0, r1, r2 = ROWS_WIN, ROWS_WIN + ROWS_WUQ, ROWS_WIN + ROWS_WUQ + ROWS_WUKV
    w_in_f = jnp.transpose(wg[:, 0:r0].reshape(N_DEV, D_MODEL, D_IN_PROJ // N_DEV), (1, 0, 2)).reshape(D_MODEL, D_IN_PROJ)
    w_in_p = jnp.concatenate([w_in_f[:, 448:], w_in_f[:, 0:384], w_in_f[:, 384:448],
                              jnp.zeros((D_MODEL, U_PAD - D_IN_PROJ), BF16)], axis=1)
    w_uq_f = jnp.transpose(wg[:, r0:r1].reshape(N_DEV, R_Q, N_HEADS * D_HEAD // N_DEV), (1, 0, 2)).reshape(R_Q, N_HEADS * D_HEAD)
    w_uq_p = _pad_heads(w_uq_f)
    w_ukv_f = jnp.transpose(wg[:, r1:r2].reshape(N_DEV, R_KV, 128), (1, 0, 2)).reshape(R_KV, N_HEADS * 256)
    w_out_f = wg[:, r2:].reshape(D_MODEL, D_MODEL)
    qng_p = jnp.concatenate([q_norm_g, jnp.zeros((1, D_HEAD_PAD - D_HEAD), F32)], axis=1)
    kng_p = jnp.concatenate([k_norm_g, jnp.zeros((1, D_HEAD_PAD - D_HEAD), F32)], axis=1)
    cos, slo, shi = _rope_tables(seq)

    u, q_perm, k, v = _inproj_fwd(x2, ctx2, modrows, bmodrows, norm_g, w_in_p, q_lora_g, w_uq_p, kv_lora_g, w_ukv_f,
                                  qng_p, kng_p, cos, slo, shi)
    q = q_perm.reshape(N_HEADS, seq, D_HEAD_PAD)
    attn, lse = _attn_fwd(q, k, v, 256)
    (loss_acc, g1, dgate, d_attn, dga, dgp, dpl, gwo, gwp, dps) = _mix(
        x2, tgt2, attn, u, modrows, bmodrows, w_pool[0], pool_scale, w_out_f)

    dq, dk, dv = _attn_bwd(q, k, v, attn, lse, d_attn, 256)
    dq = dq.reshape(N_HEADS, Q_BLOCK, seq // Q_BLOCK * D_HEAD_PAD)
    (grad_x, gwin_p, gwuq_p, gwukv_t, dqlg, dkvlg, dqng, dkng, dng, dmod4) = _inproj_bwd(
        x2, ctx2, modrows, bmodrows, norm_g, w_in_p, q_lora_g, w_uq_p, kv_lora_g, w_ukv_f, qng_p, kng_p, cos, slo, shi,
        u, dq, dk, dv, dga, dgp, dpl, g1)

    gwin_t = jnp.concatenate([gwin_p[O_CQ:O_KR], gwin_p[O_KR:D_IN_PROJ], gwin_p[0:O_CQ]], axis=0)
    gwuq_t = gwuq_p.reshape(N_HEADS, D_HEAD_PAD, R_Q)[:, 0:D_HEAD].reshape(N_HEADS * D_HEAD, R_Q)
    bigpack = jnp.concatenate([gwin_t.reshape(N_DEV, ROWS_WIN, D_MODEL), gwuq_t.reshape(N_DEV, ROWS_WUQ, D_MODEL),
                               gwukv_t.reshape(N_DEV, ROWS_WUKV, D_MODEL), gwo.reshape(N_DEV, ROWS_WOUT, D_MODEL),
                               gwp.reshape(N_DEV, ROWS_WPOOL, D_MODEL)], axis=1)
    bigpack = jnp.transpose(bigpack.reshape(2, 2, 2, ROWS_G, D_MODEL), (2, 0, 1, 3, 4)).reshape(2, 4, ROWS_G, D_MODEL)
    row = lambda a: jnp.concatenate([a, jnp.zeros((1, D_MODEL - a.shape[1]), F32)], axis=1)
    smallpack = jnp.concatenate([dng, row(dqlg), row(dkvlg), row(dqng), row(dkng), row(dps),
                                 dmod4[0:2], dgate, dmod4[2:4], jnp.zeros((SMALL_ROWS - 11, D_MODEL), F32)], axis=0)
    red, smallg, tot, p2g = _epilogue(bigpack, smallpack, w_mod_l)

    g_w_in = jnp.transpose(red[0:r0])
    g_w_uq = jnp.transpose(red[r0:r1].reshape(N_HEADS * D_HEAD // N_DEV, R_Q))
    g_w_ukv = jnp.transpose(red[r1:r2].reshape(128, R_KV))
    g_w_out = red[r2:r2 + ROWS_WOUT]
    g_w_pool = p2g[:, 0:8].reshape(4 * GROUP_DIM, GROUP_DIM)
    dmod_all = smallg[:, 6:9].reshape(N_DEV, 3 * D_MODEL)
    dmodc_tot = tot[9:12].reshape(1, 3 * D_MODEL)
    dmod_my = jnp.concatenate([lax.dynamic_slice_in_dim(dmod_all, me * n_mod, n_mod, axis=1),
                               lax.dynamic_slice_in_dim(dmodc_tot, me * n_mod, n_mod, axis=1),
                               jnp.zeros((16 - N_DEV - 1, n_mod), F32)], axis=0)
    c_rows = cg[:, 0, :]
    c_all_t = jnp.transpose(jnp.concatenate([c_rows, c_ctx_row, jnp.zeros((16 - N_DEV - 1, D_MODEL), F32)], axis=0))

    weights = [c_ctx_row, w_mod_l, b_mod, norm_g, w_in_l, q_lora_g, w_uq_l, kv_lora_g, w_ukv_l, q_norm_g, k_norm_g,
               w_pool.reshape(4 * GROUP_DIM, GROUP_DIM), pool_scale, w_out_l]
    grads = [None, None, (tot[6:9] + tot[9:12]).reshape(1, 3 * D_MODEL), tot[0:1], g_w_in, tot[1:2, 0:R_Q], g_w_uq,
             tot[2:3, 0:R_KV], g_w_ukv, tot[3:4, 0:D_HEAD], tot[4:5, 0:D_HEAD], g_w_pool, tot[5:6, 0:D_POOL], g_w_out]
    ms = [m_c_ctx.reshape(1, D_MODEL), m_w_mod[0], m_b_mod, m_norm_g, m_w_in[0], m_q_lora_g, m_w_uq[0], m_kv_lora_g,
          m_w_ukv[0], m_q_norm_g, m_k_norm_g, m_w_pool.reshape(4 * GROUP_DIM, GROUP_DIM), m_pool_scale, m_w_out[0]]
    vs = [v_c_ctx.reshape(1, D_MODEL), v_w_mod[0], v_b_mod, v_norm_g, v_w_in[0], v_q_lora_g, v_w_uq[0], v_kv_lora_g,
          v_w_ukv[0], v_q_norm_g, v_k_norm_g, v_w_pool.reshape(4 * GROUP_DIM, GROUP_DIM), v_pool_scale, v_w_out[0]]
    outs = _adamw(weights, grads, ms, vs, c_all_t, dmod_my, p2g, c_ctx_row)
    grads[0], grads[1] = outs[0], outs[1]
    n = len(weights)
    deltas, new_m, new_v = outs[2:2 + n], outs[2 + n:2 + 2 * n], outs[2 + 2 * n:2 + 3 * n]

    loss = lax.psum(loss_acc[0, 0] * (0.5 / D_MODEL), ("x", "y", "c"))
    final = [c_ctx.shape, w_mod.shape, b_mod.shape, norm_g.shape, w_in.shape, q_lora_g.shape, w_uq.shape, kv_lora_g.shape,
             w_ukv.shape, q_norm_g.shape, k_norm_g.shape, w_pool.shape, pool_scale.shape, w_out.shape]
    shaped = lambda arrs: [a.reshape(s) for a, s in zip(arrs, final)]
    return (loss, grad_x[None], *shaped(grads), *shaped(deltas), *shaped(new_m), *shaped(new_v))
```

```python
import numpy as np

import jax
import jax.numpy as jnp
from jax import lax
from jax.experimental import pallas as pl
from jax.experimental.pallas import tpu as pltpu

F32 = jnp.float32
BF16 = jnp.bfloat16
MESH = pl.DeviceIdType.MESH
HIGHEST = lax.Precision.HIGHEST

D_MODEL = 1024
N_HEADS = 4
D_NOPE = 128
D_ROPE = 64
D_HEAD = D_NOPE + D_ROPE
D_HEAD_PAD = 256
D_V = 128
R_Q = 256
R_KV = 128
D_ATTN = 512
D_POOL = 512
POOL_WINDOWS = (2, 4, 8, 16)
GROUP_DIM = 128
GRID_W = 64
ROPE_BASE = 10000.0
NORM_EPS = 1e-6
ATTN_SCALE = D_HEAD ** -0.5
D_IN_PROJ = 1984
U_PAD = 2048
O_GA, O_PIN, O_GP, O_CQ, O_CKV, O_KR = 0, 512, 1024, 1536, 1792, 1920
TILE = 256
Q_BLOCK = 128
HALO = 8
N_DEV = 8
VMEM_LIMIT = 56 * 1024 * 1024

ADAM_LR = 0.001
ADAM_B1 = 0.9
ADAM_B2 = 0.999
ADAM_EPS = 1e-08
ADAM_WD = 0.01
ADAM_STEP = 10

SMALL_ROWS = 16


def _dot(a, b):
    return lax.dot_general(a, b, (((1,), (0,)), ((), ())), preferred_element_type=F32)


def _dot_nt(a, b):
    return lax.dot_general(a, b, (((1,), (1,)), ((), ())), preferred_element_type=F32)


def _dot_tn(a, b):
    return lax.dot_general(a, b, (((0,), (0,)), ((), ())), preferred_element_type=F32)


def _sigmoid(x):
    return 1.0 / (1.0 + jnp.exp(-x))


def _rope(p, cos, slo, shi):
    return p * cos + pltpu.roll(p, 112, 1) * slo + pltpu.roll(p, 16, 1) * shi


def _rope_t(d, cos, slo, shi):
    return d * cos + pltpu.roll(d * slo, 16, 1) + pltpu.roll(d * shi, 112, 1)


def _shift_rows(a, k):
    n = a.shape[0]
    return pltpu.roll(a, (-k) % n, 0)


def _window_sum(x, w, transposed):
    s = (x + _shift_rows(x, 1)) if transposed else (_shift_rows(x, -1) + x)
    step = 1
    while 2 * step < w:
        s = _shift_rows(s, -step) + _shift_rows(s, step)
        step *= 2
    return s


def _inv_count(tpos, w, seq):
    lo = jnp.maximum(tpos - w // 2, 0)
    hi = jnp.minimum(tpos - w // 2 + w, seq)
    return 1.0 / jnp.maximum(hi - lo, 1).astype(F32)


def _coords():
    return lax.axis_index("x"), lax.axis_index("y"), lax.axis_index("c")


def _flip(v, bit):
    return (1 - v) if bit else v


def _peer(k):
    x, y, c = _coords()
    return (_flip(x, (k >> 2) & 1), _flip(y, (k >> 1) & 1), _flip(c, k & 1))


def _lin(p):
    return 4 * p[0] + 2 * p[1] + p[2]


def _gather_to_all(src_ref, slots_ref, send_sems, recv_sems):
    me = _coords()
    copies = []
    for k in range(1, N_DEV):
        cp = pltpu.make_async_remote_copy(
            src_ref=src_ref, dst_ref=slots_ref.at[_lin(me)], send_sem=send_sems.at[k - 1], recv_sem=recv_sems.at[k - 1],
            device_id=_peer(k), device_id_type=MESH)
        cp.start()
        copies.append(cp)

    def wait_recv():
        for k in range(1, N_DEV):
            pltpu.make_async_remote_copy(
                src_ref=src_ref, dst_ref=slots_ref.at[_lin(_peer(k))], send_sem=send_sems.at[k - 1],
                recv_sem=recv_sems.at[k - 1], device_id=_peer(k), device_id_type=MESH).wait_recv()

    def wait_send():
        for cp in copies:
            cp.wait_send()

    return wait_recv, wait_send


def _prologue(c8, cctx8, w_mod_l, shards):
    n_mod = w_mod_l.shape[1]
    n_w = len(shards)

    def body(c8_ref, cctx_ref, wmod_ref, *refs):
        w_refs = refs[0:n_w]
        cg_ref, modg_ref = refs[n_w], refs[n_w + 1]
        wg_refs = refs[n_w + 2:2 * n_w + 2]
        modblk_ref = refs[2 * n_w + 2]
        wb_refs = refs[2 * n_w + 3:3 * n_w + 3]
        ssem_w, rsem_w, ssem_c, rsem_c, ssem_m, rsem_m = refs[3 * n_w + 3:]
        x, y, c = _coords()
        me = (x, y, c)
        sibling = (x, y, 1 - c)
        chips = [(1 - x, y), (x, 1 - y), (1 - x, 1 - y)]

        def wcopies(k, block, to, own=False):
            out = []
            for a in range(n_w):
                slot = wg_refs[a].at[_lin(block)]
                out.append(pltpu.make_async_remote_copy(
                    src_ref=wb_refs[a] if own else slot, dst_ref=slot, send_sem=ssem_w.at[a, k], recv_sem=rsem_w.at[a, k],
                    device_id=to, device_id_type=MESH))
            return out

        for a in range(n_w):
            wb_refs[a][...] = w_refs[a][...].astype(BF16)
        first = wcopies(0, me, sibling, own=True)
        for j, chip in enumerate(chips):
            first += wcopies(1 + j, me, (*chip, c), own=True)
        for cp in first:
            cp.start()
        for a in range(n_w):
            wg_refs[a][_lin(me)] = wb_refs[a][...]

        cg_ref[_lin(me)] = c8_ref[...]
        c_recv, c_send = _gather_to_all(c8_ref, cg_ref, ssem_c, rsem_c)
        c_recv()
        row = lax.broadcasted_iota(jnp.int32, (8, D_MODEL), 0)
        c_all = jnp.zeros((8, D_MODEL), F32)
        for d in range(N_DEV):
            c_all = c_all + jnp.where(row == d, cg_ref[d], 0.0)
        cc = cctx_ref[...]
        a = jnp.concatenate([c_all * _sigmoid(c_all), cc * _sigmoid(cc)], axis=0)
        modblk_ref[...] = lax.dot_general(a, wmod_ref[...], (((1,), (0,)), ((), ())), precision=HIGHEST,
                                          preferred_element_type=F32)
        modg_ref[_lin(me)] = modblk_ref[...]
        m_recv, m_send = _gather_to_all(modblk_ref, modg_ref, ssem_m, rsem_m)
        m_recv()

        passed = []
        for j, chip in enumerate(chips):
            for cp in wcopies(1 + j, (*chip, c), me):
                cp.wait_recv()
            fwd = wcopies(4 + j, (*chip, c), sibling)
            for cp in fwd:
                cp.start()
            passed += fwd
        for cp in wcopies(0, sibling, me):
            cp.wait_recv()
        for j, chip in enumerate(chips):
            for cp in wcopies(4 + j, (*chip, 1 - c), me):
                cp.wait_recv()
        for cp in first + passed:
            cp.wait_send()
        c_send()
        m_send()

    vm = pl.BlockSpec(memory_space=pltpu.VMEM)
    return pl.pallas_call(
        body, name="prologue_gather",
        out_shape=(jax.ShapeDtypeStruct((N_DEV, 8, D_MODEL), F32), jax.ShapeDtypeStruct((N_DEV, 16, n_mod), F32),
                   *[jax.ShapeDtypeStruct((N_DEV,) + s.shape, BF16) for s in shards]),
        in_specs=[vm] * (3 + n_w), out_specs=tuple([vm] * (2 + n_w)),
        scratch_shapes=[pltpu.VMEM((16, n_mod), F32), *[pltpu.VMEM(s.shape, BF16) for s in shards],
                        pltpu.SemaphoreType.DMA((n_w, 7)), pltpu.SemaphoreType.DMA((n_w, 7)),
                        pltpu.SemaphoreType.DMA((7,)), pltpu.SemaphoreType.DMA((7,)),
                        pltpu.SemaphoreType.DMA((7,)), pltpu.SemaphoreType.DMA((7,))],
        compiler_params=pltpu.CompilerParams(vmem_limit_bytes=VMEM_LIMIT),
    )(c8, cctx8, w_mod_l, *shards)


def _modulated_input(i, x_ref, ctx_ref, mod_ref, bmod_ref, ng_ref):
    is_ctx = i == 0
    xt = jnp.where(is_ctx, ctx_ref[...], x_ref[...])
    shift = jnp.where(is_ctx, mod_ref[3:4, :] + bmod_ref[3:4, :], mod_ref[0:1, :] + bmod_ref[0:1, :])
    scale = jnp.where(is_ctx, mod_ref[4:5, :] + bmod_ref[4:5, :], mod_ref[1:2, :] + bmod_ref[1:2, :])
    r = lax.rsqrt(jnp.mean(xt * xt, axis=-1, keepdims=True) + NORM_EPS)
    xg = (xt * r) * ng_ref[...]
    h = xg * (1.0 + scale) + shift
    return xt, r, xg, h, scale


def _inproj_fwd(x, ctx, modrows, bmodrows, norm_g, w_in_p, q_lora_g, w_uq_p, kv_lora_g, w_ukv, qng_p, kng_p, cos, slo, shi):
    seq = x.shape[0]
    n_tiles = seq // TILE + 1
    tot = seq + TILE

    def body(x_ref, ctx_ref, mod_ref, bmod_ref, ng_ref, win_ref, qlg_ref, wuq_ref, kvlg_ref, wukv_ref, qng_ref, kng_ref,
             cos_ref, slo_ref, shi_ref, u_ref, q_ref, k_ref, v_ref):
        i = pl.program_id(0)
        _, _, _, h, _ = _modulated_input(i, x_ref, ctx_ref, mod_ref, bmod_ref, ng_ref)
        u = _dot(h.astype(BF16), win_ref[...])
        u_ref[...] = u
        cos_t, slo_t, shi_t = cos_ref[...], slo_ref[...], shi_ref[...]

        cq = u[:, O_CQ:O_CQ + R_Q]
        qn = cq * lax.rsqrt(jnp.mean(cq * cq, axis=-1, keepdims=True) + NORM_EPS) * qlg_ref[...]
        q = _dot(qn.astype(BF16), wuq_ref[...])
        qg = qng_ref[...]
        for hd in range(N_HEADS):
            qh = q[:, hd * D_HEAD_PAD:(hd + 1) * D_HEAD_PAD]
            rr = lax.rsqrt(jnp.sum(qh * qh, axis=-1, keepdims=True) * (1.0 / D_HEAD) + NORM_EPS)
            qy = qh * rr * qg
            q_nope = qy[:, 0:D_NOPE].astype(BF16)
            q_rope = _rope(qy[:, D_NOPE:D_HEAD_PAD], cos_t, slo_t, shi_t).astype(BF16)
            for half in range(TILE // Q_BLOCK):
                rows = slice(half * Q_BLOCK, (half + 1) * Q_BLOCK)
                q_ref[hd, :, half * D_HEAD_PAD:half * D_HEAD_PAD + D_NOPE] = q_nope[rows]
                q_ref[hd, :, half * D_HEAD_PAD + D_NOPE:(half + 1) * D_HEAD_PAD] = q_rope[rows]

        ckv = u[:, O_CKV:O_CKV + R_KV]
        kvn = ckv * lax.rsqrt(jnp.mean(ckv * ckv, axis=-1, keepdims=True) + NORM_EPS) * kvlg_ref[...]
        kv = _dot(kvn.astype(BF16), wukv_ref[...])
        krz = u[:, O_KR:U_PAD]
        kr_ss = jnp.sum(krz * krz, axis=-1, keepdims=True)
        kg = kng_ref[...]
        for hd in range(N_HEADS):
            kn = kv[:, hd * 256:hd * 256 + D_NOPE]
            rr = lax.rsqrt((jnp.sum(kn * kn, axis=-1, keepdims=True) + kr_ss) * (1.0 / D_HEAD) + NORM_EPS)
            k_ref[hd, :, 0:D_NOPE] = (kn * rr * kg[:, 0:D_NOPE]).astype(BF16)
            k_ref[hd, :, D_NOPE:D_HEAD_PAD] = _rope(krz * rr * kg[:, D_NOPE:D_HEAD_PAD], cos_t, slo_t, shi_t).astype(BF16)
            v_ref[hd] = kv[:, hd * 256 + D_NOPE:(hd + 1) * 256].astype(BF16)

    lat = lambda i: (jnp.maximum(i - 1, 0), 0)
    full = lambda shape: pl.BlockSpec(shape, lambda i: (0,) * len(shape))
    return pl.pallas_call(
        body, name="inproj_fwd", grid=(n_tiles,),
        out_shape=(jax.ShapeDtypeStruct((tot, U_PAD), F32),
                   jax.ShapeDtypeStruct((N_HEADS, Q_BLOCK, seq // Q_BLOCK * D_HEAD_PAD), BF16),
                   jax.ShapeDtypeStruct((N_HEADS, tot, D_HEAD_PAD), BF16),
                   jax.ShapeDtypeStruct((N_HEADS, tot, D_V), BF16)),
        in_specs=[pl.BlockSpec((TILE, D_MODEL), lat), full((TILE, D_MODEL)), full((8, D_MODEL)), full((8, D_MODEL)),
                  full((1, D_MODEL)), full((D_MODEL, U_PAD)), full((1, R_Q)), full((R_Q, N_HEADS * D_HEAD_PAD)),
                  full((1, R_KV)), full((R_KV, N_HEADS * 256)), full((1, D_HEAD_PAD)), full((1, D_HEAD_PAD)),
                  pl.BlockSpec((TILE, 128), lambda i: (i, 0)), pl.BlockSpec((TILE, 128), lambda i: (i, 0)),
                  pl.BlockSpec((TILE, 128), lambda i: (i, 0))],
        out_specs=(pl.BlockSpec((TILE, U_PAD), lambda i: (i, 0)),
                   pl.BlockSpec((N_HEADS, Q_BLOCK, TILE // Q_BLOCK * D_HEAD_PAD), lambda i: (0, 0, jnp.maximum(i - 1, 0))),
                   pl.BlockSpec((N_HEADS, TILE, D_HEAD_PAD), lambda i: (0, i, 0)),
                   pl.BlockSpec((N_HEADS, TILE, D_V), lambda i: (0, i, 0))),
        compiler_params=pltpu.CompilerParams(dimension_semantics=("arbitrary",), vmem_limit_bytes=VMEM_LIMIT),
    )(x, ctx, modrows, bmodrows, norm_g, w_in_p, q_lora_g, w_uq_p, kv_lora_g, w_ukv, qng_p, kng_p, cos, slo, shi)


def _attn_fwd(q, k, v, block_q):
    _, seq, _ = q.shape
    n_keys = k.shape[1]

    def body(q_ref, k_ref, v_ref, o_ref, lse_ref):
        s = _dot_nt(q_ref[0], k_ref[0]) * ATTN_SCALE
        m = jnp.max(s, axis=-1, keepdims=True)
        p = jnp.exp(s - m)
        l = jnp.sum(p, axis=-1, keepdims=True)
        o_ref[...] = _dot(p.astype(BF16), v_ref[0]) * (1.0 / l)
        lse_ref[0] = m + jnp.log(l)

    return pl.pallas_call(
        body, name="attn_fwd", grid=(N_HEADS, seq // block_q),
        out_shape=(jax.ShapeDtypeStruct((seq, D_ATTN), F32), jax.ShapeDtypeStruct((N_HEADS, seq, 1), F32)),
        in_specs=[pl.BlockSpec((1, block_q, D_HEAD_PAD), lambda h, i: (h, i, 0)),
                  pl.BlockSpec((1, n_keys, D_HEAD_PAD), lambda h, i: (h, 0, 0)),
                  pl.BlockSpec((1, n_keys, D_V), lambda h, i: (h, 0, 0))],
        out_specs=(pl.BlockSpec((block_q, D_V), lambda h, i: (i, h)),
                   pl.BlockSpec((1, block_q, 1), lambda h, i: (h, i, 0))),
        compiler_params=pltpu.CompilerParams(dimension_semantics=("arbitrary", "arbitrary"), vmem_limit_bytes=VMEM_LIMIT),
    )(q, k, v)


def _attn_bwd(q, k, v, o, lse, d_o, block_q):
    _, seq, _ = q.shape
    n_keys = k.shape[1]

    def body(q_ref, k_ref, v_ref, o_ref, lse_ref, do_ref, dq_ref, dk_ref, dv_ref):
        i = pl.program_id(1)

        @pl.when(i == 0)
        def _():
            dk_ref[...] = jnp.zeros_like(dk_ref)
            dv_ref[...] = jnp.zeros_like(dv_ref)

        qb, kb, vb = q_ref[0], k_ref[0], v_ref[0]
        d_out = do_ref[...]
        p = jnp.exp(_dot_nt(qb, kb) * ATTN_SCALE - lse_ref[0])
        dob = d_out.astype(BF16)
        dp = _dot_nt(dob, vb)
        delta = jnp.sum(d_out * o_ref[...], axis=-1, keepdims=True)
        ds = (p * (dp - delta) * ATTN_SCALE).astype(BF16)
        dq_ref[0] = _dot(ds, kb)
        dk_ref[0] += _dot_tn(ds, qb)
        dv_ref[0] += _dot_tn(p.astype(BF16), dob)

    return pl.pallas_call(
        body, name="attn_bwd", grid=(N_HEADS, seq // block_q),
        out_shape=(jax.ShapeDtypeStruct((N_HEADS, seq, D_HEAD_PAD), F32),
                   jax.ShapeDtypeStruct((N_HEADS, n_keys, D_HEAD_PAD), F32),
                   jax.ShapeDtypeStruct((N_HEADS, n_keys, D_V), F32)),
        in_specs=[pl.BlockSpec((1, block_q, D_HEAD_PAD), lambda h, i: (h, i, 0)),
                  pl.BlockSpec((1, n_keys, D_HEAD_PAD), lambda h, i: (h, 0, 0)),
                  pl.BlockSpec((1, n_keys, D_V), lambda h, i: (h, 0, 0)),
                  pl.BlockSpec((block_q, D_V), lambda h, i: (i, h)),
                  pl.BlockSpec((1, block_q, 1), lambda h, i: (h, i, 0)),
                  pl.BlockSpec((block_q, D_V), lambda h, i: (i, h))],
        out_specs=(pl.BlockSpec((1, block_q, D_HEAD_PAD), lambda h, i: (h, i, 0)),
                   pl.BlockSpec((1, n_keys, D_HEAD_PAD), lambda h, i: (h, 0, 0)),
                   pl.BlockSpec((1, n_keys, D_V), lambda h, i: (h, 0, 0))),
        compiler_params=pltpu.CompilerParams(dimension_semantics=("arbitrary", "arbitrary"), vmem_limit_bytes=VMEM_LIMIT),
    )(q, k, v, o, lse, d_o)


def _mix(x, target, attn, u, modrows, bmodrows, w_pool, pool_scale, w_out):
    seq = x.shape[0]
    n_tiles = seq // TILE
    rows8 = TILE // HALO
    last8 = (seq + TILE) // HALO - 1

    def body(x_ref, t_ref, o_ref, ga_ref, pin_ref, hb_ref, ha_ref, gp_ref, mod_ref, bmod_ref, wp_ref, ps_ref, wo_ref,
             loss_ref, g1_ref, dgate_ref, do_ref, dga_ref, dgp_ref, dpl_ref, gwo_ref, gwp_ref, dps_ref):
        j = pl.program_id(0)

        @pl.when(j == 0)
        def _():
            loss_ref[...] = jnp.zeros_like(loss_ref)
            dgate_ref[...] = jnp.zeros_like(dgate_ref)
            gwo_ref[...] = jnp.zeros_like(gwo_ref)
            gwp_ref[...] = jnp.zeros_like(gwp_ref)
            dps_ref[...] = jnp.zeros_like(dps_ref)

        gate = mod_ref[2:3, :] + bmod_ref[2:3, :]
        ga = ga_ref[...]
        sga = _sigmoid(ga)
        silu_ga = ga * sga
        attn_t = o_ref[...]
        br_a = silu_ga * attn_t

        pin = pin_ref[...]
        xs = jnp.concatenate([jnp.where(j > 0, hb_ref[...], 0.0), pin, jnp.where(j < n_tiles - 1, ha_ref[...], 0.0)], axis=0)
        tpos = j * TILE + lax.broadcasted_iota(jnp.int32, (TILE, 1), 0)
        ps = ps_ref[...]
        pooled, yps = [], []
        for g, w in enumerate(POOL_WINDOWS):
            sl = slice(g * GROUP_DIM, (g + 1) * GROUP_DIM)
            ws = _window_sum(xs[:, sl], w, False)[HALO:HALO + TILE]
            pg = (ws * _inv_count(tpos, w, seq) - pin[:, sl]).astype(BF16)
            pooled.append(pg)
            yps.append(_dot(pg, wp_ref[g].astype(BF16)))
        yp = jnp.concatenate(yps, axis=1)
        ypool = yp * ps
        gp = gp_ref[...]
        sgp = _sigmoid(gp)
        silu_gp = gp * sgp
        br_p = silu_gp * ypool

        z = jnp.concatenate([br_a, br_p], axis=1).astype(BF16)
        y = _dot(z, wo_ref[...])
        res = x_ref[...] + gate * y - t_ref[...]
        loss_ref[...] += jnp.sum(res * res) * (0.5 / D_MODEL)
        dxn = res * (1.0 / D_MODEL)
        g1_ref[...] = dxn
        dgate_ref[...] += jnp.sum(dxn * y, axis=0, keepdims=True)
        dy = (gate * dxn).astype(BF16)
        dz = _dot_nt(dy, wo_ref[...])
        gwo_ref[...] += _dot_tn(z, dy)

        dbra = dz[:, 0:D_ATTN]
        dbrp = dz[:, D_ATTN:]
        do_ref[...] = dbra * silu_ga
        dga_ref[...] = dbra * attn_t * (sga * (1.0 + ga * (1.0 - sga)))
        dgp_ref[...] = dbrp * ypool * (sgp * (1.0 + gp * (1.0 - sgp)))
        dyp_s = dbrp * silu_gp
        dps_ref[...] += jnp.sum(dyp_s * yp, axis=0, keepdims=True)
        dyp = (dyp_s * ps).astype(BF16)
        for g in range(len(POOL_WINDOWS)):
            sl = slice(g * GROUP_DIM, (g + 1) * GROUP_DIM)
            gwp_ref[g] += _dot_tn(pooled[g], dyp[:, sl])
            dpl_ref[:, sl] = _dot_nt(dyp[:, sl], wp_ref[g].astype(BF16))

    tile = lambda w: pl.BlockSpec((TILE, w), lambda j: (j, 0))
    ucol = lambda col: pl.BlockSpec((TILE, 512), lambda j: (j + 1, col))
    full = lambda shape: pl.BlockSpec(shape, lambda j: (0,) * len(shape))
    return pl.pallas_call(
        body, name="mix_fwd_bwd", grid=(n_tiles,),
        out_shape=(jax.ShapeDtypeStruct((8, 128), F32), jax.ShapeDtypeStruct((seq, D_MODEL), F32),
                   jax.ShapeDtypeStruct((1, D_MODEL), F32), jax.ShapeDtypeStruct((seq, D_ATTN), F32),
                   jax.ShapeDtypeStruct((seq, D_ATTN), F32), jax.ShapeDtypeStruct((seq, D_POOL), F32),
                   jax.ShapeDtypeStruct((seq, D_POOL), F32), jax.ShapeDtypeStruct((D_MODEL, D_MODEL), F32),
                   jax.ShapeDtypeStruct((4, GROUP_DIM, GROUP_DIM), F32), jax.ShapeDtypeStruct((1, D_POOL), F32)),
        in_specs=[tile(D_MODEL), tile(D_MODEL), tile(D_ATTN), ucol(0), ucol(1),
                  pl.BlockSpec((HALO, 512), lambda j: ((j + 1) * rows8 - 1, 1)),
                  pl.BlockSpec((HALO, 512), lambda j: (jnp.minimum((j + 2) * rows8, last8), 1)),
                  ucol(2), full((8, D_MODEL)), full((8, D_MODEL)), full((4, GROUP_DIM, GROUP_DIM)), full((1, D_POOL)),
                  full((D_MODEL, D_MODEL))],
        out_specs=(full((8, 128)), tile(D_MODEL), full((1, D_MODEL)), tile(D_ATTN), tile(D_ATTN), tile(D_POOL),
                   tile(D_POOL), full((D_MODEL, D_MODEL)), full((4, GROUP_DIM, GROUP_DIM)), full((1, D_POOL))),
        compiler_params=pltpu.CompilerParams(dimension_semantics=("arbitrary",), vmem_limit_bytes=VMEM_LIMIT),
    )(x, target, attn, u, u, u, u, u, modrows, bmodrows, w_pool, pool_scale, w_out)


def _inproj_bwd(x, ctx, modrows, bmodrows, norm_g, w_in_p, q_lora_g, w_uq_p, kv_lora_g, w_ukv, qng_p, kng_p, cos, slo, shi,
                u, dq, dk, dv, dga, dgp, dpl, g1):
    seq = x.shape[0]
    n_tiles = seq // TILE + 1
    rows8 = TILE // HALO
    last8 = seq // HALO - 1

    def body(x_ref, ctx_ref, mod_ref, bmod_ref, ng_ref, win_ref, qlg_ref, wuq_ref, kvlg_ref, wukv_ref, qng_ref, kng_ref,
             cos_ref, slo_ref, shi_ref, cq_ref, ckv_ref, kr_ref, dq_ref, dk_ref, dv_ref, dga_ref, dgp_ref, dpl_ref,
             hb_ref, ha_ref, g1_ref,
             gx_ref, gwin_ref, gwuq_ref, gwukv_ref, dqlg_ref, dkvlg_ref, dqng_ref, dkng_ref, dng_ref, dmod_ref,
             du_ref):
        i = pl.program_id(0)
        is_lat = i > 0

        @pl.when(i == 0)
        def _():
            for ref in (gwin_ref, gwuq_ref, gwukv_ref, dqlg_ref, dkvlg_ref, dqng_ref, dkng_ref, dng_ref, dmod_ref):
                ref[...] = jnp.zeros_like(ref)

        xt, r, xg, h, scale = _modulated_input(i, x_ref, ctx_ref, mod_ref, bmod_ref, ng_ref)
        hb = h.astype(BF16)
        cos_t, slo_t, shi_t = cos_ref[...], slo_ref[...], shi_ref[...]

        cq = cq_ref[...]
        rq = lax.rsqrt(jnp.mean(cq * cq, axis=-1, keepdims=True) + NORM_EPS)
        qhat = cq * rq
        qnb = (qhat * qlg_ref[...]).astype(BF16)
        q = _dot(qnb, wuq_ref[...])
        qg = qng_ref[...]
        dq_heads = []
        dqng = jnp.zeros((1, D_HEAD_PAD), F32)
        for hd in range(N_HEADS):
            qh = q[:, hd * D_HEAD_PAD:(hd + 1) * D_HEAD_PAD]
            rr = lax.rsqrt(jnp.sum(qh * qh, axis=-1, keepdims=True) * (1.0 / D_HEAD) + NORM_EPS)
            yq = qh * rr
            dq_h = dq_ref[hd]
            dq_h = jnp.concatenate([dq_h[:, half * D_HEAD_PAD:(half + 1) * D_HEAD_PAD] for half in range(TILE // Q_BLOCK)],
                                   axis=0)
            dpost = jnp.where(is_lat, dq_h, 0.0)
            dyn = jnp.concatenate([dpost[:, 0:D_NOPE], _rope_t(dpost[:, D_NOPE:], cos_t, slo_t, shi_t)], axis=1)
            dqng = dqng + jnp.sum(dyn * yq, axis=0, keepdims=True)
            dyg = dyn * qg
            dq_heads.append(rr * (dyg - yq * (jnp.sum(dyg * yq, axis=-1, keepdims=True) * (1.0 / D_HEAD))))
        dqng_ref[...] += dqng
        dqf = jnp.concatenate(dq_heads, axis=1).astype(BF16)
        gwuq_ref[...] += _dot_tn(dqf, qnb)
        dqn = _dot_nt(dqf, wuq_ref[...])
        dqlg_ref[...] += jnp.sum(dqn * qhat, axis=0, keepdims=True)
        dqhat = dqn * qlg_ref[...]
        dcq = rq * (dqhat - qhat * jnp.mean(dqhat * qhat, axis=-1, keepdims=True))

        ckv = ckv_ref[...]
        rkv = lax.rsqrt(jnp.mean(ckv * ckv, axis=-1, keepdims=True) + NORM_EPS)
        kvhat = ckv * rkv
        kvnb = (kvhat * kvlg_ref[...]).astype(BF16)
        kv = _dot(kvnb, wukv_ref[...])
        krz = kr_ref[...]
        kr_ss = jnp.sum(krz * krz, axis=-1, keepdims=True)
        kg = kng_ref[...]
        kg_n, kg_r = kg[:, 0:D_NOPE], kg[:, D_NOPE:]
        dkv_parts = []
        dkr = jnp.zeros((TILE, 128), F32)
        dkng_n = jnp.zeros((1, D_NOPE), F32)
        dkng_r = jnp.zeros((1, 128), F32)
        for hd in range(N_HEADS):
            kn = kv[:, hd * 256:hd * 256 + D_NOPE]
            rr = lax.rsqrt((jnp.sum(kn * kn, axis=-1, keepdims=True) + kr_ss) * (1.0 / D_HEAD) + NORM_EPS)
            yn, yr = kn * rr, krz * rr
            dk_h = dk_ref[hd]
            dpn = dk_h[:, 0:D_NOPE]
            dpr = _rope_t(dk_h[:, D_NOPE:], cos_t, slo_t, shi_t)
            dkng_n = dkng_n + jnp.sum(dpn * yn, axis=0, keepdims=True)
            dkng_r = dkng_r + jnp.sum(dpr * yr, axis=0, keepdims=True)
            dyn, dyr = dpn * kg_n, dpr * kg_r
            proj = (jnp.sum(dyn * yn, axis=-1, keepdims=True) + jnp.sum(dyr * yr, axis=-1, keepdims=True)) * (1.0 / D_HEAD)
            dkv_parts.append(rr * (dyn - yn * proj))
            dkv_parts.append(dv_ref[hd])
            dkr = dkr + rr * (dyr - yr * proj)
        dkng_ref[:, 0:D_NOPE] += dkng_n
        dkng_ref[:, D_NOPE:] += dkng_r
        dkvf = jnp.concatenate(dkv_parts, axis=1).astype(BF16)
        gwukv_ref[...] += _dot_tn(dkvf, kvnb)
        dkvn = _dot_nt(dkvf, wukv_ref[...])
        dkvlg_ref[...] += jnp.sum(dkvn * kvhat, axis=0, keepdims=True)
        dkvhat = dkvn * kvlg_ref[...]
        dckv = rkv * (dkvhat - kvhat * jnp.mean(dkvhat * kvhat, axis=-1, keepdims=True))

        lat_t = jnp.maximum(i - 1, 0)
        dpl_t = dpl_ref[...]
        ds = jnp.concatenate([jnp.where(i > 1, hb_ref[...], 0.0), dpl_t,
                              jnp.where(i < n_tiles - 1, ha_ref[...], 0.0)], axis=0)
        tpos = lat_t * TILE - HALO + lax.broadcasted_iota(jnp.int32, (TILE + 2 * HALO, 1), 0)
        for g, w in enumerate(POOL_WINDOWS):
            sl = slice(g * GROUP_DIM, (g + 1) * GROUP_DIM)
            wsum = _window_sum(ds[:, sl] * _inv_count(tpos, w, seq), w, True)[HALO:HALO + TILE]
            du_ref[:, O_PIN + g * GROUP_DIM:O_PIN + (g + 1) * GROUP_DIM] = jnp.where(
                is_lat, wsum - dpl_t[:, sl], 0.0).astype(BF16)

        du_ref[:, O_GA:O_GA + D_ATTN] = jnp.where(is_lat, dga_ref[...], 0.0).astype(BF16)
        du_ref[:, O_GP:O_GP + D_POOL] = jnp.where(is_lat, dgp_ref[...], 0.0).astype(BF16)
        du_ref[:, O_CQ:O_CQ + R_Q] = dcq.astype(BF16)
        du_ref[:, O_CKV:O_CKV + R_KV] = dckv.astype(BF16)
        du_ref[:, O_KR:U_PAD] = dkr.astype(BF16)
        dub = du_ref[...]
        dh = _dot_nt(dub, win_ref[...])
        gwin_ref[...] += _dot_tn(dub, hb)

        dshift = jnp.sum(dh, axis=0, keepdims=True)
        dscale = jnp.sum(dh * xg, axis=0, keepdims=True)
        zero = jnp.zeros_like(dshift)
        dmod_ref[0:1, :] += jnp.where(is_lat, dshift, zero)
        dmod_ref[1:2, :] += jnp.where(is_lat, dscale, zero)
        dmod_ref[2:3, :] += jnp.where(is_lat, zero, dshift)
        dmod_ref[3:4, :] += jnp.where(is_lat, zero, dscale)
        dxg = dh * (1.0 + scale)
        xr = xt * r
        dng_ref[...] += jnp.sum(dxg * xr, axis=0, keepdims=True)
        dxn = dxg * ng_ref[...]
        gx_ref[...] = g1_ref[...] + r * (dxn - xr * jnp.mean(dxn * xr, axis=-1, keepdims=True))

    lat = lambda i: (jnp.maximum(i - 1, 0), 0)
    full = lambda shape: pl.BlockSpec(shape, lambda i: (0,) * len(shape))
    rope_spec = pl.BlockSpec((TILE, 128), lambda i: (i, 0))
    return pl.pallas_call(
        body, name="inproj_bwd", grid=(n_tiles,),
        out_shape=(jax.ShapeDtypeStruct((seq, D_MODEL), F32), jax.ShapeDtypeStruct((U_PAD, D_MODEL), F32),
                   jax.ShapeDtypeStruct((N_HEADS * D_HEAD_PAD, R_Q), F32), jax.ShapeDtypeStruct((N_HEADS * 256, R_KV), F32),
                   jax.ShapeDtypeStruct((1, R_Q), F32), jax.ShapeDtypeStruct((1, R_KV), F32),
                   jax.ShapeDtypeStruct((1, D_HEAD_PAD), F32), jax.ShapeDtypeStruct((1, D_HEAD_PAD), F32),
                   jax.ShapeDtypeStruct((1, D_MODEL), F32), jax.ShapeDtypeStruct((8, D_MODEL), F32)),
        in_specs=[pl.BlockSpec((TILE, D_MODEL), lat), full((TILE, D_MODEL)), full((8, D_MODEL)), full((8, D_MODEL)),
                  full((1, D_MODEL)), full((D_MODEL, U_PAD)), full((1, R_Q)), full((R_Q, N_HEADS * D_HEAD_PAD)),
                  full((1, R_KV)), full((R_KV, N_HEADS * 256)), full((1, D_HEAD_PAD)), full((1, D_HEAD_PAD)),
                  rope_spec, rope_spec, rope_spec,
                  pl.BlockSpec((TILE, R_Q), lambda i: (i, O_CQ // R_Q)),
                  pl.BlockSpec((TILE, R_KV), lambda i: (i, O_CKV // R_KV)),
                  pl.BlockSpec((TILE, 128), lambda i: (i, O_KR // 128)),
                  pl.BlockSpec((N_HEADS, Q_BLOCK, TILE // Q_BLOCK * D_HEAD_PAD), lambda i: (0, 0, jnp.maximum(i - 1, 0))),
                  pl.BlockSpec((N_HEADS, TILE, D_HEAD_PAD), lambda i: (0, i, 0)),
                  pl.BlockSpec((N_HEADS, TILE, D_V), lambda i: (0, i, 0)),
                  pl.BlockSpec((TILE, D_ATTN), lat), pl.BlockSpec((TILE, D_POOL), lat), pl.BlockSpec((TILE, D_POOL), lat),
                  pl.BlockSpec((HALO, D_POOL), lambda i: (jnp.maximum((i - 1) * rows8 - 1, 0), 0)),
                  pl.BlockSpec((HALO, D_POOL), lambda i: (jnp.minimum(jnp.maximum(i, 1) * rows8, last8), 0)),
                  pl.BlockSpec((TILE, D_MODEL), lat)],
        out_specs=(pl.BlockSpec((TILE, D_MODEL), lat), full((U_PAD, D_MODEL)), full((N_HEADS * D_HEAD_PAD, R_Q)),
                   full((N_HEADS * 256, R_KV)), full((1, R_Q)), full((1, R_KV)), full((1, D_HEAD_PAD)),
                   full((1, D_HEAD_PAD)), full((1, D_MODEL)), full((8, D_MODEL))),
        scratch_shapes=[pltpu.VMEM((TILE, U_PAD), BF16)],
        compiler_params=pltpu.CompilerParams(dimension_semantics=("arbitrary",), vmem_limit_bytes=VMEM_LIMIT),
    )(x, ctx, modrows, bmodrows, norm_g, w_in_p, q_lora_g, w_uq_p, kv_lora_g, w_ukv, qng_p, kng_p, cos, slo, shi,
      u, u, u, dq, dk, dv, dga, dgp, dpl, dpl, dpl, g1)


SMALL_LAYOUT = ((0, D_MODEL), (1, R_Q), (2, R_KV), (3, D_HEAD_PAD), (4, D_HEAD_PAD), (5, D_POOL))
ROW_DMOD_B, ROW_DMOD_C, ROW_LOSS = 6, 9, 12


def _epilogue(parts, smalls, dmod4, dgate, loss_acc, w_mod_l):
    n_a, n_s = len(parts), len(smalls)
    n_mod = w_mod_l.shape[1]
    blk = [p.shape[1:] for p in parts]
    small_out = [D_MODEL, R_Q, R_KV, D_HEAD, D_HEAD, D_POOL]

    def body(*refs):
        part_refs = refs[0:n_a]
        small_in = refs[n_a:n_a + n_s]
        dmod4_ref, dgate_ref, loss_ref, wmod_ref = refs[n_a + n_s:n_a + n_s + 4]
        outs = refs[n_a + n_s + 4:]
        red_refs = outs[0:n_a]
        lastg_ref, ccg_ref = outs[n_a], outs[n_a + 1]
        sum_refs = outs[n_a + 2:n_a + 2 + n_s]
        gbmod_ref, dmy_ref, lossout_ref = outs[n_a + 2 + n_s:n_a + 5 + n_s]
        scr = outs[n_a + 5 + n_s:]
        recv1, sum1b, recv2 = scr[0:n_a], scr[n_a:2 * n_a], scr[2 * n_a:3 * n_a]
        small_ref, smallg_ref, tot_ref, cc_ref = scr[3 * n_a:3 * n_a + 4]
        ssem1, rsem1, ssem2, rsem2, ssem_s, rsem_s, ssem_l, rsem_l, ssem_cc, rsem_cc = scr[3 * n_a + 4:]
        x, y, c = _coords()
        me = (x, y, c)
        sibling = (x, y, 1 - c)

        stage1 = []
        for a in range(n_a):
            for b in range(4):
                stage1.append(pltpu.make_async_remote_copy(
                    src_ref=part_refs[a].at[4 * (b >> 1) + 2 * (b & 1) + (1 - c)], dst_ref=recv1[a].at[b],
                    send_sem=ssem1.at[a, b], recv_sem=rsem1.at[a, b], device_id=sibling, device_id_type=MESH))
        for cp in stage1:
            cp.start()

        small_ref[...] = jnp.zeros_like(small_ref)
        for ref, (row, width) in zip(small_in, SMALL_LAYOUT):
            small_ref[row:row + 1, 0:width] = ref[...]
        small_ref[ROW_DMOD_B:ROW_DMOD_B + 2, :] = dmod4_ref[0:2, :]
        small_ref[ROW_DMOD_B + 2:ROW_DMOD_B + 3, :] = dgate_ref[...]
        small_ref[ROW_DMOD_C:ROW_DMOD_C + 2, :] = dmod4_ref[2:4, :]
        small_ref[ROW_LOSS:ROW_LOSS + 1, 0:128] = loss_ref[0:1, :]
        smallg_ref[_lin(me)] = small_ref[...]
        s_recv, s_send = _gather_to_all(small_ref, smallg_ref, ssem_s, rsem_s)
        s_recv()
        tot = smallg_ref[0]
        for d in range(1, N_DEV):
            tot = tot + smallg_ref[d]
        tot_ref[...] = tot
        for ref, (row, _), width in zip(sum_refs, SMALL_LAYOUT, small_out):
            ref[...] = tot_ref[row:row + 1, 0:width]
        lossout_ref[...] = tot_ref[8:16, 0:128]
        lin = _lin(me)

        def my_columns(three_rows):
            v = jnp.concatenate(three_rows, axis=1)
            out = jnp.zeros((1, n_mod), F32)
            for d in range(N_DEV):
                out = out + jnp.where(lin == d, v[:, d * n_mod:(d + 1) * n_mod], 0.0)
            return out

        rows3 = lambda ref, r0: [ref[r0 + t:r0 + t + 1, :] for t in range(3)]
        dmodc = rows3(tot_ref, ROW_DMOD_C)
        gbmod_ref[...] = jnp.concatenate(rows3(tot_ref, ROW_DMOD_B), axis=1) + jnp.concatenate(dmodc, axis=1)
        dmy_ref[...] = jnp.zeros_like(dmy_ref)
        for b in range(N_DEV):
            dmy_ref[b:b + 1, :] = my_columns(rows3(smallg_ref.at[b], ROW_DMOD_B))
        dmy = my_columns(dmodc)
        dmy_ref[N_DEV:N_DEV + 1, :] = dmy
        part = lax.dot_general(jnp.broadcast_to(dmy, (8, n_mod)), wmod_ref[...], (((1,), (1,)), ((), ())),
                               precision=HIGHEST, preferred_element_type=F32)

        cc_ref[...] = part
        ccg_ref[_lin(me)] = part
        cc_recv, cc_send = _gather_to_all(cc_ref, ccg_ref, ssem_cc, rsem_cc)

        for a in range(n_a):
            for b in range(4):
                pltpu.make_async_remote_copy(
                    src_ref=part_refs[a].at[0], dst_ref=recv1[a].at[b], send_sem=ssem1.at[a, b], recv_sem=rsem1.at[a, b],
                    device_id=sibling, device_id_type=MESH).wait_recv()
                sum1b[a][b] = (part_refs[a][4 * (b >> 1) + 2 * (b & 1) + c] + recv1[a][b]).astype(BF16)

        stage2 = []
        for a in range(n_a):
            for k in (1, 2, 3):
                tx, ty = _flip(x, (k >> 1) & 1), _flip(y, k & 1)
                stage2.append(pltpu.make_async_remote_copy(
                    src_ref=sum1b[a].at[2 * tx + ty], dst_ref=recv2[a].at[k - 1], send_sem=ssem2.at[a, k - 1],
                    recv_sem=rsem2.at[a, k - 1], device_id=(tx, ty, c), device_id_type=MESH))
        for cp in stage2:
            cp.start()
        for a in range(n_a):
            own = part_refs[a][_lin(me)] + recv1[a][2 * x + y]
            for k in (1, 2, 3):
                stage2[3 * a + k - 1].wait_recv()
                own = own + recv2[a][k - 1].astype(F32)
            red_refs[a][...] = own

        lastg_ref[_lin(me)] = red_refs[n_a - 1][...]
        l_recv, l_send = _gather_to_all(red_refs[n_a - 1], lastg_ref, ssem_l, rsem_l)
        l_recv()
        cc_recv()
        for cp in stage1 + stage2:
            cp.wait_send()
        s_send()
        cc_send()
        l_send()

    vm = pl.BlockSpec(memory_space=pltpu.VMEM)
    sds = jax.ShapeDtypeStruct
    return pl.pallas_call(
        body, name="epilogue_reduce",
        out_shape=(*[sds(s, F32) for s in blk], sds((N_DEV,) + blk[-1], F32), sds((N_DEV, 8, D_MODEL), F32),
                   *[sds((1, w), F32) for w in small_out], sds((1, 3 * D_MODEL), F32), sds((16, n_mod), F32),
                   sds((8, 128), F32)),
        in_specs=[vm] * (n_a + n_s + 4), out_specs=tuple([vm] * (n_a + n_s + 5)),
        scratch_shapes=[*[pltpu.VMEM((4,) + s, F32) for s in blk], *[pltpu.VMEM((4,) + s, BF16) for s in blk],
                        *[pltpu.VMEM((3,) + s, BF16) for s in blk],
                        pltpu.VMEM((SMALL_ROWS, D_MODEL), F32), pltpu.VMEM((N_DEV, SMALL_ROWS, D_MODEL), F32),
                        pltpu.VMEM((SMALL_ROWS, D_MODEL), F32), pltpu.VMEM((8, D_MODEL), F32),
                        pltpu.SemaphoreType.DMA((n_a, 4)), pltpu.SemaphoreType.DMA((n_a, 4)),
                        pltpu.SemaphoreType.DMA((n_a, 3)), pltpu.SemaphoreType.DMA((n_a, 3)),
                        pltpu.SemaphoreType.DMA((7,)), pltpu.SemaphoreType.DMA((7,)),
                        pltpu.SemaphoreType.DMA((7,)), pltpu.SemaphoreType.DMA((7,)),
                        pltpu.SemaphoreType.DMA((7,)), pltpu.SemaphoreType.DMA((7,))],
        compiler_params=pltpu.CompilerParams(vmem_limit_bytes=VMEM_LIMIT),
    )(*parts, *smalls, dmod4, dgate, loss_acc, w_mod_l)


def _adamw(weights, grads, ms, vs, c_all_t, dmod_my, p2g):
    n = len(weights)

    def body(*refs):
        w_refs = refs[0:n]
        g_in = refs[n:2 * n - 2]
        m_refs = refs[2 * n - 2:3 * n - 2]
        v_refs = refs[3 * n - 2:4 * n - 2]
        cat_ref, dmod_ref, p2g_ref = refs[4 * n - 2:4 * n + 1]
        outs = refs[4 * n + 1:]
        gcc_ref, gwm_ref = outs[0], outs[1]
        d_refs, nm_refs, nv_refs = outs[2:2 + n], outs[2 + n:2 + 2 * n], outs[2 + 2 * n:2 + 3 * n]

        part = p2g_ref[0, 0:1, :]
        for d in range(1, N_DEV):
            part = part + p2g_ref[d, 0:1, :]
        cc = w_refs[0][...]
        sg = _sigmoid(cc)
        gcc_ref[...] = part * (sg * (1.0 + cc * (1.0 - sg)))
        cat = cat_ref[...]
        gwm_ref[...] = lax.dot_general(cat * _sigmoid(cat), dmod_ref[...], (((1,), (0,)), ((), ())), precision=HIGHEST,
                                       preferred_element_type=F32)
        for idx in range(n):
            g = (gcc_ref, gwm_ref)[idx][...] if idx < 2 else g_in[idx - 2][...]
            m = ADAM_B1 * m_refs[idx][...] + (1.0 - ADAM_B1) * g
            v = ADAM_B2 * v_refs[idx][...] + (1.0 - ADAM_B2) * (g * g)
            m_hat = m / (1.0 - ADAM_B1 ** ADAM_STEP)
            v_hat = v / (1.0 - ADAM_B2 ** ADAM_STEP)
            d_refs[idx][...] = -ADAM_LR * (m_hat / (jnp.sqrt(v_hat) + ADAM_EPS) + ADAM_WD * w_refs[idx][...])
            nm_refs[idx][...] = m
            nv_refs[idx][...] = v

    vm = pl.BlockSpec(memory_space=pltpu.VMEM)
    shapes = [jax.ShapeDtypeStruct(w.shape, F32) for w in weights]
    args = list(weights) + list(grads[2:]) + list(ms) + list(vs) + [c_all_t, dmod_my, p2g]
    out_shape = tuple(shapes[0:2] + shapes * 3)
    return pl.pallas_call(
        body, name="adamw_update", out_shape=out_shape, in_specs=[vm] * len(args), out_specs=tuple([vm] * len(out_shape)),
        compiler_params=pltpu.CompilerParams(vmem_limit_bytes=VMEM_LIMIT),
    )(*args)


def _rope_tables(seq):
    rows = seq // GRID_W
    row = np.repeat(np.arange(rows, dtype=np.float32), GRID_W)
    col = np.tile(np.arange(GRID_W, dtype=np.float32), rows)
    n_freq = D_ROPE // 4
    inv = (np.float32(ROPE_BASE) ** (-np.arange(n_freq, dtype=np.float32) / np.float32(n_freq))).astype(np.float32)
    ang_r = (row[:, None] * inv).astype(np.float32)
    ang_c = (col[:, None] * inv).astype(np.float32)
    ang = np.concatenate([ang_r, ang_r, ang_c, ang_c], axis=-1)
    cos, sin = np.cos(ang).astype(np.float32), np.sin(ang).astype(np.float32)
    low = (np.arange(D_ROPE) % 32) < 16

    def table(t, fill):
        out = np.full((TILE + seq, 128), fill, np.float32)
        out[TILE:, 0:D_ROPE] = t
        return jnp.asarray(out)

    return table(cos, 1.0), table(np.where(low, -sin, 0.0), 0.0), table(np.where(low, 0.0, sin), 0.0)


def _pad_heads(a):
    lead = a.shape[:-1]
    a4 = a.reshape(lead + (N_HEADS, D_HEAD))
    a4 = jnp.concatenate([a4, jnp.zeros(lead + (N_HEADS, D_HEAD_PAD - D_HEAD), a.dtype)], axis=-1)
    return a4.reshape(lead + (N_HEADS * D_HEAD_PAD,))


def kernel(x, c, ctx, c_ctx, w_mod, b_mod, norm_g, w_in, q_lora_g, w_uq, kv_lora_g, w_ukv, q_norm_g, k_norm_g, w_pool, pool_scale, w_out, loss_target, m_c_ctx, m_w_mod, m_b_mod, m_norm_g, m_w_in, m_q_lora_g, m_w_uq, m_kv_lora_g, m_w_ukv, m_q_norm_g, m_k_norm_g, m_w_pool, m_pool_scale, m_w_out, v_c_ctx, v_w_mod, v_b_mod, v_norm_g, v_w_in, v_q_lora_g, v_w_uq, v_kv_lora_g, v_w_ukv, v_q_norm_g, v_k_norm_g, v_w_pool, v_pool_scale, v_w_out):
    seq = x.shape[1]
    assert ctx.shape[1] == TILE and seq % TILE == 0 and seq % GRID_W == 0
    ix, iy, ic = lax.axis_index("x"), lax.axis_index("y"), lax.axis_index("c")
    me = 4 * ix + 2 * iy + ic
    x2, ctx2, tgt2 = x[0], ctx[0], loss_target[0]
    w_mod_l, w_in_l, w_uq_l, w_ukv_l, w_out_l = w_mod[0], w_in[0], w_uq[0], w_ukv[0], w_out[0]
    n_mod = w_mod_l.shape[1]
    c_ctx_row = c_ctx.reshape(1, D_MODEL)

    cg, modg, wg_in, wg_uq, wg_ukv, wg_out = _prologue(
        jnp.broadcast_to(c, (8, D_MODEL)), jnp.broadcast_to(c_ctx_row, (8, D_MODEL)), w_mod_l,
        [w_in_l, w_uq_l, w_ukv_l, w_out_l])
    mod_all = jnp.transpose(modg, (1, 0, 2)).reshape(16, 3 * D_MODEL)
    mod_b = lax.dynamic_slice_in_dim(mod_all, me, 1, axis=0).reshape(3, D_MODEL)
    mod_c = mod_all[8].reshape(3, D_MODEL)
    modrows = jnp.concatenate([mod_b, mod_c[0:2], jnp.zeros((3, D_MODEL), F32)], axis=0)
    b3 = b_mod.reshape(3, D_MODEL)
    bmodrows = jnp.concatenate([b3, b3[0:2], jnp.zeros((3, D_MODEL), F32)], axis=0)

    w_in_f = jnp.transpose(wg_in, (1, 0, 2)).reshape(D_MODEL, D_IN_PROJ)
    w_in_p = jnp.concatenate([w_in_f[:, 448:], w_in_f[:, 0:384], w_in_f[:, 384:448],
                              jnp.zeros((D_MODEL, U_PAD - D_IN_PROJ), BF16)], axis=1)
    w_uq_p = _pad_heads(jnp.transpose(wg_uq, (1, 0, 2)).reshape(R_Q, N_HEADS * D_HEAD))
    w_ukv_f = jnp.transpose(wg_ukv, (1, 0, 2)).reshape(R_KV, N_HEADS * 256)
    w_out_f = wg_out.reshape(D_MODEL, D_MODEL)
    qng_p = jnp.concatenate([q_norm_g, jnp.zeros((1, D_HEAD_PAD - D_HEAD), F32)], axis=1)
    kng_p = jnp.concatenate([k_norm_g, jnp.zeros((1, D_HEAD_PAD - D_HEAD), F32)], axis=1)
    cos, slo, shi = _rope_tables(seq)

    u, q_perm, k, v = _inproj_fwd(x2, ctx2, modrows, bmodrows, norm_g, w_in_p, q_lora_g, w_uq_p, kv_lora_g, w_ukv_f,
                                  qng_p, kng_p, cos, slo, shi)
    q = q_perm.reshape(N_HEADS, seq, D_HEAD_PAD)
    attn, lse = _attn_fwd(q, k, v, 256)
    (loss_acc, g1, dgate, d_attn, dga, dgp, dpl, gwo, gwp, dps) = _mix(
        x2, tgt2, attn, u, modrows, bmodrows, w_pool[0], pool_scale, w_out_f)

    dq, dk, dv = _attn_bwd(q, k, v, attn, lse, d_attn, 256)
    dq = dq.reshape(N_HEADS, Q_BLOCK, seq // Q_BLOCK * D_HEAD_PAD)
    (grad_x, gwin_p, gwuq_p, gwukv_t, dqlg, dkvlg, dqng, dkng, dng, dmod4) = _inproj_bwd(
        x2, ctx2, modrows, bmodrows, norm_g, w_in_p, q_lora_g, w_uq_p, kv_lora_g, w_ukv_f, qng_p, kng_p, cos, slo, shi,
        u, dq, dk, dv, dga, dgp, dpl, g1)

    gwin_t = jnp.concatenate([gwin_p[O_CQ:O_KR], gwin_p[O_KR:D_IN_PROJ], gwin_p[0:O_CQ]], axis=0)
    gwuq_t = gwuq_p.reshape(N_HEADS, D_HEAD_PAD, R_Q)[:, 0:D_HEAD].reshape(N_HEADS * D_HEAD, R_Q)
    blocks = lambda a: a.reshape((N_DEV, a.shape[0] // N_DEV) + a.shape[1:])
    parts = [blocks(gwin_t), blocks(gwuq_t), blocks(gwukv_t), blocks(gwo), blocks(gwp.reshape(4 * GROUP_DIM, GROUP_DIM))]
    (r_win, r_wuq, r_wukv, r_wout, _, wpool_g, ccg, g_ng, g_qlg, g_kvlg, g_qng, g_kng, g_ps, g_bmod, dmod_my,
     loss_rows) = _epilogue(parts, [dng, dqlg, dkvlg, dqng, dkng, dps], dmod4, dgate, loss_acc, w_mod_l)

    g_w_in = jnp.transpose(r_win)
    g_w_uq = jnp.transpose(r_wuq)
    g_w_ukv = jnp.transpose(r_wukv)
    g_w_pool = wpool_g.reshape(4 * GROUP_DIM, GROUP_DIM)
    c_rows = cg[:, 0, :]
    c_all_t = jnp.transpose(jnp.concatenate([c_rows, c_ctx_row, jnp.zeros((16 - N_DEV - 1, D_MODEL), F32)], axis=0))

    weights = [c_ctx_row, w_mod_l, b_mod, norm_g, w_in_l, q_lora_g, w_uq_l, kv_lora_g, w_ukv_l, q_norm_g, k_norm_g,
               w_pool.reshape(4 * GROUP_DIM, GROUP_DIM), pool_scale, w_out_l]
    grads = [None, None, g_bmod, g_ng, g_w_in, g_qlg, g_w_uq, g_kvlg, g_w_ukv, g_qng, g_kng, g_w_pool, g_ps, r_wout]
    ms = [m_c_ctx.reshape(1, D_MODEL), m_w_mod[0], m_b_mod, m_norm_g, m_w_in[0], m_q_lora_g, m_w_uq[0], m_kv_lora_g,
          m_w_ukv[0], m_q_norm_g, m_k_norm_g, m_w_pool.reshape(4 * GROUP_DIM, GROUP_DIM), m_pool_scale, m_w_out[0]]
    vs = [v_c_ctx.reshape(1, D_MODEL), v_w_mod[0], v_b_mod, v_norm_g, v_w_in[0], v_q_lora_g, v_w_uq[0], v_kv_lora_g,
          v_w_ukv[0], v_q_norm_g, v_k_norm_g, v_w_pool.reshape(4 * GROUP_DIM, GROUP_DIM), v_pool_scale, v_w_out[0]]
    outs = _adamw(weights, grads, ms, vs, c_all_t, dmod_my, ccg)
    grads[0], grads[1] = outs[0], outs[1]
    n = len(weights)
    deltas, new_m, new_v = outs[2:2 + n], outs[2 + n:2 + 2 * n], outs[2 + 2 * n:2 + 3 * n]

    loss = loss_rows[ROW_LOSS - 8, 0]
    final = [c_ctx.shape, w_mod.shape, b_mod.shape, norm_g.shape, w_in.shape, q_lora_g.shape, w_uq.shape, kv_lora_g.shape,
             w_ukv.shape, q_norm_g.shape, k_norm_g.shape, w_pool.shape, pool_scale.shape, w_out.shape]
    shaped = lambda arrs: [a.reshape(s) for a, s in zip(arrs, final)]
    return (loss, grad_x[None], *shaped(grads), *shaped(deltas), *shaped(new_m), *shaped(new_v))
```

```python
import numpy as np

import jax
import jax.numpy as jnp
from jax import lax
from jax.experimental import pallas as pl
from jax.experimental.pallas import tpu as pltpu

F32 = jnp.float32
BF16 = jnp.bfloat16
MESH = pl.DeviceIdType.MESH
HIGHEST = lax.Precision.HIGHEST

D_MODEL = 1024
N_HEADS = 4
D_NOPE = 128
D_ROPE = 64
D_HEAD = D_NOPE + D_ROPE
D_HEAD_PAD = 256
D_V = 128
R_Q = 256
R_KV = 128
D_ATTN = 512
D_POOL = 512
POOL_WINDOWS = (2, 4, 8, 16)
GROUP_DIM = 128
GRID_W = 64
ROPE_BASE = 10000.0
NORM_EPS = 1e-6
ATTN_SCALE = D_HEAD ** -0.5
LOG2_E = 1.4426950408889634
LN_2 = 0.6931471805599453
ATTN_ROWS = 256
D_IN_PROJ = 1984
U_PAD = 2048
O_GA, O_PIN, O_GP, O_CQ, O_CKV, O_KR = 0, 512, 1024, 1536, 1792, 1920
TILE = 256
Q_BLOCK = 128
HALO = 8
N_DEV = 8
VMEM_LIMIT = 56 * 1024 * 1024

ADAM_LR = 0.001
ADAM_B1 = 0.9
ADAM_B2 = 0.999
ADAM_EPS = 1e-08
ADAM_WD = 0.01
ADAM_STEP = 10

SMALL_ROWS = 16


def _dot(a, b):
    return lax.dot_general(a, b, (((1,), (0,)), ((), ())), preferred_element_type=F32)


def _dot_nt(a, b):
    return lax.dot_general(a, b, (((1,), (1,)), ((), ())), preferred_element_type=F32)


def _dot_tn(a, b):
    return lax.dot_general(a, b, (((0,), (0,)), ((), ())), preferred_element_type=F32)


def _sigmoid(x):
    return 1.0 / (1.0 + jnp.exp(-x))


def _rope(p, cos, slo, shi):
    return p * cos + pltpu.roll(p, 112, 1) * slo + pltpu.roll(p, 16, 1) * shi


def _rope_t(d, cos, slo, shi):
    return d * cos + pltpu.roll(d * slo, 16, 1) + pltpu.roll(d * shi, 112, 1)


def _shift_rows(a, k):
    n = a.shape[0]
    return pltpu.roll(a, (-k) % n, 0)


def _window_sum(x, w, transposed):
    s = (x + _shift_rows(x, 1)) if transposed else (_shift_rows(x, -1) + x)
    step = 1
    while 2 * step < w:
        s = _shift_rows(s, -step) + _shift_rows(s, step)
        step *= 2
    return s


def _inv_count(tpos, w, seq):
    lo = jnp.maximum(tpos - w // 2, 0)
    hi = jnp.minimum(tpos - w // 2 + w, seq)
    return 1.0 / jnp.maximum(hi - lo, 1).astype(F32)


def _coords():
    return lax.axis_index("x"), lax.axis_index("y"), lax.axis_index("c")


def _flip(v, bit):
    return (1 - v) if bit else v


def _peer(k):
    x, y, c = _coords()
    return (_flip(x, (k >> 2) & 1), _flip(y, (k >> 1) & 1), _flip(c, k & 1))


def _lin(p):
    return 4 * p[0] + 2 * p[1] + p[2]


def _gather_to_all(src_ref, slots_ref, send_sems, recv_sems):
    me = _coords()
    copies = []
    for k in range(1, N_DEV):
        cp = pltpu.make_async_remote_copy(
            src_ref=src_ref, dst_ref=slots_ref.at[_lin(me)], send_sem=send_sems.at[k - 1], recv_sem=recv_sems.at[k - 1],
            device_id=_peer(k), device_id_type=MESH)
        cp.start()
        copies.append(cp)

    def wait_recv():
        for k in range(1, N_DEV):
            pltpu.make_async_remote_copy(
                src_ref=src_ref, dst_ref=slots_ref.at[_lin(_peer(k))], send_sem=send_sems.at[k - 1],
                recv_sem=recv_sems.at[k - 1], device_id=_peer(k), device_id_type=MESH).wait_recv()

    def wait_send():
        for cp in copies:
            cp.wait_send()

    return wait_recv, wait_send


def _prologue(c8, cctx8, w_mod_l, shards):
    n_mod = w_mod_l.shape[1]
    n_w = len(shards)

    def body(c8_ref, cctx_ref, wmod_ref, *refs):
        w_refs = refs[0:n_w]
        cg_ref, modg_ref = refs[n_w], refs[n_w + 1]
        wg_refs = refs[n_w + 2:2 * n_w + 2]
        modblk_ref = refs[2 * n_w + 2]
        wb_refs = refs[2 * n_w + 3:3 * n_w + 3]
        ssem_w, rsem_w, ssem_c, rsem_c, ssem_m, rsem_m = refs[3 * n_w + 3:]
        x, y, c = _coords()
        me = (x, y, c)
        sibling = (x, y, 1 - c)
        chips = [(1 - x, y), (x, 1 - y), (1 - x, 1 - y)]

        def wcopies(k, block, to, own=False):
            out = []
            for a in range(n_w):
                slot = wg_refs[a].at[_lin(block)]
                out.append(pltpu.make_async_remote_copy(
                    src_ref=wb_refs[a] if own else slot, dst_ref=slot, send_sem=ssem_w.at[a, k], recv_sem=rsem_w.at[a, k],
                    device_id=to, device_id_type=MESH))
            return out

        for a in range(n_w):
            wb_refs[a][...] = w_refs[a][...].astype(BF16)
        first = wcopies(0, me, sibling, own=True)
        for j, chip in enumerate(chips):
            first += wcopies(1 + j, me, (*chip, c), own=True)
        for cp in first:
            cp.start()
        for a in range(n_w):
            wg_refs[a][_lin(me)] = wb_refs[a][...]

        cg_ref[_lin(me)] = c8_ref[...]
        c_recv, c_send = _gather_to_all(c8_ref, cg_ref, ssem_c, rsem_c)
        c_recv()
        row = lax.broadcasted_iota(jnp.int32, (8, D_MODEL), 0)
        c_all = jnp.zeros((8, D_MODEL), F32)
        for d in range(N_DEV):
            c_all = c_all + jnp.where(row == d, cg_ref[d], 0.0)
        cc = cctx_ref[...]
        a = jnp.concatenate([c_all * _sigmoid(c_all), cc * _sigmoid(cc)], axis=0)
        modblk_ref[...] = lax.dot_general(a, wmod_ref[...], (((1,), (0,)), ((), ())), precision=HIGHEST,
                                          preferred_element_type=F32)
        modg_ref[_lin(me)] = modblk_ref[...]
        m_recv, m_send = _gather_to_all(modblk_ref, modg_ref, ssem_m, rsem_m)
        m_recv()

        passed = []
        for j, chip in enumerate(chips):
            for cp in wcopies(1 + j, (*chip, c), me):
                cp.wait_recv()
            fwd = wcopies(4 + j, (*chip, c), sibling)
            for cp in fwd:
                cp.start()
            passed += fwd
        for cp in wcopies(0, sibling, me):
            cp.wait_recv()
        for j, chip in enumerate(chips):
            for cp in wcopies(4 + j, (*chip, 1 - c), me):
                cp.wait_recv()
        for cp in first + passed:
            cp.wait_send()
        c_send()
        m_send()

    vm = pl.BlockSpec(memory_space=pltpu.VMEM)
    return pl.pallas_call(
        body, name="prologue_gather",
        out_shape=(jax.ShapeDtypeStruct((N_DEV, 8, D_MODEL), F32), jax.ShapeDtypeStruct((N_DEV, 16, n_mod), F32),
                   *[jax.ShapeDtypeStruct((N_DEV,) + s.shape, BF16) for s in shards]),
        in_specs=[vm] * (3 + n_w), out_specs=tuple([vm] * (2 + n_w)),
        scratch_shapes=[pltpu.VMEM((16, n_mod), F32), *[pltpu.VMEM(s.shape, BF16) for s in shards],
                        pltpu.SemaphoreType.DMA((n_w, 7)), pltpu.SemaphoreType.DMA((n_w, 7)),
                        pltpu.SemaphoreType.DMA((7,)), pltpu.SemaphoreType.DMA((7,)),
                        pltpu.SemaphoreType.DMA((7,)), pltpu.SemaphoreType.DMA((7,))],
        compiler_params=pltpu.CompilerParams(vmem_limit_bytes=VMEM_LIMIT),
    )(c8, cctx8, w_mod_l, *shards)


def _modulated_input(i, x_ref, ctx_ref, mod_ref, bmod_ref, ng_ref):
    is_ctx = i == 0
    xt = jnp.where(is_ctx, ctx_ref[...], x_ref[...])
    shift = jnp.where(is_ctx, mod_ref[3:4, :] + bmod_ref[3:4, :], mod_ref[0:1, :] + bmod_ref[0:1, :])
    scale = jnp.where(is_ctx, mod_ref[4:5, :] + bmod_ref[4:5, :], mod_ref[1:2, :] + bmod_ref[1:2, :])
    r = lax.rsqrt(jnp.mean(xt * xt, axis=-1, keepdims=True) + NORM_EPS)
    xg = (xt * r) * ng_ref[...]
    h = xg * (1.0 + scale) + shift
    return xt, r, xg, h, scale


def _inproj_fwd(x, ctx, modrows, bmodrows, norm_g, w_in_p, q_lora_g, w_uq_p, kv_lora_g, w_ukv, qng_p, kng_p, cos, slo, shi):
    seq = x.shape[0]
    n_tiles = seq // TILE + 1
    tot = seq + TILE

    def body(x_ref, ctx_ref, mod_ref, bmod_ref, ng_ref, win_ref, qlg_ref, wuq_ref, kvlg_ref, wukv_ref, qng_ref, kng_ref,
             cos_ref, slo_ref, shi_ref, u_ref, q_ref, k_ref, v_ref):
        i = pl.program_id(0)
        _, _, _, h, _ = _modulated_input(i, x_ref, ctx_ref, mod_ref, bmod_ref, ng_ref)
        u = _dot(h.astype(BF16), win_ref[...])
        u_ref[...] = u
        cos_t, slo_t, shi_t = cos_ref[...], slo_ref[...], shi_ref[...]

        cq = u[:, O_CQ:O_CQ + R_Q]
        qn = cq * lax.rsqrt(jnp.mean(cq * cq, axis=-1, keepdims=True) + NORM_EPS) * qlg_ref[...]
        q = _dot(qn.astype(BF16), wuq_ref[...])
        qg = qng_ref[...] * (ATTN_SCALE * LOG2_E)
        for hd in range(N_HEADS):
            qh = q[:, hd * D_HEAD_PAD:(hd + 1) * D_HEAD_PAD]
            rr = lax.rsqrt(jnp.sum(qh * qh, axis=-1, keepdims=True) * (1.0 / D_HEAD) + NORM_EPS)
            qy = qh * rr * qg
            q_nope = qy[:, 0:D_NOPE].astype(BF16)
            q_rope = _rope(qy[:, D_NOPE:D_HEAD_PAD], cos_t, slo_t, shi_t).astype(BF16)
            for half in range(TILE // Q_BLOCK):
                rows = slice(half * Q_BLOCK, (half + 1) * Q_BLOCK)
                q_ref[hd, :, half * D_HEAD_PAD:half * D_HEAD_PAD + D_NOPE] = q_nope[rows]
                q_ref[hd, :, half * D_HEAD_PAD + D_NOPE:(half + 1) * D_HEAD_PAD] = q_rope[rows]

        ckv = u[:, O_CKV:O_CKV + R_KV]
        kvn = ckv * lax.rsqrt(jnp.mean(ckv * ckv, axis=-1, keepdims=True) + NORM_EPS) * kvlg_ref[...]
        kv = _dot(kvn.astype(BF16), wukv_ref[...])
        krz = u[:, O_KR:U_PAD]
        kr_ss = jnp.sum(krz * krz, axis=-1, keepdims=True)
        kg = kng_ref[...]
        for hd in range(N_HEADS):
            kn = kv[:, hd * 256:hd * 256 + D_NOPE]
            rr = lax.rsqrt((jnp.sum(kn * kn, axis=-1, keepdims=True) + kr_ss) * (1.0 / D_HEAD) + NORM_EPS)
            k_ref[hd, :, 0:D_NOPE] = (kn * rr * kg[:, 0:D_NOPE]).astype(BF16)
            k_ref[hd, :, D_NOPE:D_HEAD_PAD] = _rope(krz * rr * kg[:, D_NOPE:D_HEAD_PAD], cos_t, slo_t, shi_t).astype(BF16)
            v_ref[hd] = kv[:, hd * 256 + D_NOPE:(hd + 1) * 256].astype(BF16)

    lat = lambda i: (jnp.maximum(i - 1, 0), 0)
    full = lambda shape: pl.BlockSpec(shape, lambda i: (0,) * len(shape))
    return pl.pallas_call(
        body, name="inproj_fwd", grid=(n_tiles,),
        out_shape=(jax.ShapeDtypeStruct((tot, U_PAD), F32),
                   jax.ShapeDtypeStruct((N_HEADS, Q_BLOCK, seq // Q_BLOCK * D_HEAD_PAD), BF16),
                   jax.ShapeDtypeStruct((N_HEADS, tot, D_HEAD_PAD), BF16),
                   jax.ShapeDtypeStruct((N_HEADS, tot, D_V), BF16)),
        in_specs=[pl.BlockSpec((TILE, D_MODEL), lat), full((TILE, D_MODEL)), full((8, D_MODEL)), full((8, D_MODEL)),
                  full((1, D_MODEL)), full((D_MODEL, U_PAD)), full((1, R_Q)), full((R_Q, N_HEADS * D_HEAD_PAD)),
                  full((1, R_KV)), full((R_KV, N_HEADS * 256)), full((1, D_HEAD_PAD)), full((1, D_HEAD_PAD)),
                  pl.BlockSpec((TILE, 128), lambda i: (i, 0)), pl.BlockSpec((TILE, 128), lambda i: (i, 0)),
                  pl.BlockSpec((TILE, 128), lambda i: (i, 0))],
        out_specs=(pl.BlockSpec((TILE, U_PAD), lambda i: (i, 0)),
                   pl.BlockSpec((N_HEADS, Q_BLOCK, TILE // Q_BLOCK * D_HEAD_PAD), lambda i: (0, 0, jnp.maximum(i - 1, 0))),
                   pl.BlockSpec((N_HEADS, TILE, D_HEAD_PAD), lambda i: (0, i, 0)),
                   pl.BlockSpec((N_HEADS, TILE, D_V), lambda i: (0, i, 0))),
        compiler_params=pltpu.CompilerParams(dimension_semantics=("arbitrary",), vmem_limit_bytes=VMEM_LIMIT),
    )(x, ctx, modrows, bmodrows, norm_g, w_in_p, q_lora_g, w_uq_p, kv_lora_g, w_ukv, qng_p, kng_p, cos, slo, shi)


def _attn_fwd(q, k, v, block_q):
    _, seq, _ = q.shape
    n_keys = k.shape[1]

    def body(q_ref, k_ref, v_ref, o_ref, lse_ref):
        kb, vb = k_ref[0], v_ref[0]
        for r0 in range(0, block_q, ATTN_ROWS):
            rows = slice(r0, r0 + ATTN_ROWS)
            s = _dot_nt(q_ref[0, rows, :], kb)
            m = jnp.max(s, axis=-1, keepdims=True)
            p = jnp.exp2(s - m)
            l = jnp.sum(p, axis=-1, keepdims=True)
            o_ref[rows, :] = _dot(p.astype(BF16), vb) * (1.0 / l)
            lse_ref[0, rows, :] = m + jnp.log2(l)

    return pl.pallas_call(
        body, name="attn_fwd", grid=(N_HEADS, seq // block_q),
        out_shape=(jax.ShapeDtypeStruct((seq, D_ATTN), F32), jax.ShapeDtypeStruct((N_HEADS, seq, 1), F32)),
        in_specs=[pl.BlockSpec((1, block_q, D_HEAD_PAD), lambda h, i: (h, i, 0)),
                  pl.BlockSpec((1, n_keys, D_HEAD_PAD), lambda h, i: (h, 0, 0)),
                  pl.BlockSpec((1, n_keys, D_V), lambda h, i: (h, 0, 0))],
        out_specs=(pl.BlockSpec((block_q, D_V), lambda h, i: (i, h)),
                   pl.BlockSpec((1, block_q, 1), lambda h, i: (h, i, 0))),
        compiler_params=pltpu.CompilerParams(dimension_semantics=("arbitrary", "arbitrary"), vmem_limit_bytes=VMEM_LIMIT),
    )(q, k, v)


def _attn_bwd(q, k, v, o, lse, d_o, block_q):
    _, seq, _ = q.shape
    n_keys = k.shape[1]

    def body(q_ref, k_ref, v_ref, o_ref, lse_ref, do_ref, dq_ref, dkt_ref, dvt_ref):
        i = pl.program_id(1)

        @pl.when(i == 0)
        def _():
            dkt_ref[...] = jnp.zeros_like(dkt_ref)
            dvt_ref[...] = jnp.zeros_like(dvt_ref)

        kb, vb = k_ref[0], v_ref[0]
        for r0 in range(0, block_q, ATTN_ROWS):
            rows = slice(r0, r0 + ATTN_ROWS)
            qb = q_ref[0, rows, :]
            d_out = do_ref[rows, :]
            p = jnp.exp2(_dot_nt(qb, kb) - lse_ref[0, rows, :])
            dob = d_out.astype(BF16)
            dp = _dot_nt(dob, vb)
            delta = jnp.sum(d_out * o_ref[rows, :], axis=-1, keepdims=True)
            raw = (p * (dp - delta)).astype(BF16)
            dq_ref[0, rows, :] = _dot(raw, kb)
            dkt_ref[0] += _dot_tn(qb, raw)
            dvt_ref[0] += _dot_tn(dob, p.astype(BF16))

    return pl.pallas_call(
        body, name="attn_bwd", grid=(N_HEADS, seq // block_q),
        out_shape=(jax.ShapeDtypeStruct((N_HEADS, seq, D_HEAD_PAD), F32),
                   jax.ShapeDtypeStruct((N_HEADS, D_HEAD_PAD, n_keys), F32),
                   jax.ShapeDtypeStruct((N_HEADS, D_V, n_keys), F32)),
        in_specs=[pl.BlockSpec((1, block_q, D_HEAD_PAD), lambda h, i: (h, i, 0)),
                  pl.BlockSpec((1, n_keys, D_HEAD_PAD), lambda h, i: (h, 0, 0)),
                  pl.BlockSpec((1, n_keys, D_V), lambda h, i: (h, 0, 0)),
                  pl.BlockSpec((block_q, D_V), lambda h, i: (i, h)),
                  pl.BlockSpec((1, block_q, 1), lambda h, i: (h, i, 0)),
                  pl.BlockSpec((block_q, D_V), lambda h, i: (i, h))],
        out_specs=(pl.BlockSpec((1, block_q, D_HEAD_PAD), lambda h, i: (h, i, 0)),
                   pl.BlockSpec((1, D_HEAD_PAD, n_keys), lambda h, i: (h, 0, 0)),
                   pl.BlockSpec((1, D_V, n_keys), lambda h, i: (h, 0, 0))),
        compiler_params=pltpu.CompilerParams(dimension_semantics=("arbitrary", "arbitrary"), vmem_limit_bytes=VMEM_LIMIT),
    )(q, k, v, o, lse, d_o)


def _mix(x, target, attn, u, modrows, bmodrows, w_pool, pool_scale, w_out):
    seq = x.shape[0]
    n_tiles = seq // TILE
    rows8 = TILE // HALO
    last8 = (seq + TILE) // HALO - 1

    def body(x_ref, t_ref, o_ref, ga_ref, pin_ref, hb_ref, ha_ref, gp_ref, mod_ref, bmod_ref, wp_ref, ps_ref, wo_ref,
             loss_ref, g1_ref, dgate_ref, do_ref, dga_ref, dgp_ref, dpl_ref, gwo_ref, gwp_ref, dps_ref):
        j = pl.program_id(0)

        @pl.when(j == 0)
        def _():
            loss_ref[...] = jnp.zeros_like(loss_ref)
            dgate_ref[...] = jnp.zeros_like(dgate_ref)
            gwo_ref[...] = jnp.zeros_like(gwo_ref)
            gwp_ref[...] = jnp.zeros_like(gwp_ref)
            dps_ref[...] = jnp.zeros_like(dps_ref)

        gate = mod_ref[2:3, :] + bmod_ref[2:3, :]
        ga = ga_ref[...]
        sga = _sigmoid(ga)
        silu_ga = ga * sga
        attn_t = o_ref[...]
        br_a = silu_ga * attn_t

        pin = pin_ref[...]
        xs = jnp.concatenate([jnp.where(j > 0, hb_ref[...], 0.0), pin, jnp.where(j < n_tiles - 1, ha_ref[...], 0.0)], axis=0)
        tpos = j * TILE + lax.broadcasted_iota(jnp.int32, (TILE, 1), 0)
        ps = ps_ref[...]
        pooled, yps = [], []
        for g, w in enumerate(POOL_WINDOWS):
            sl = slice(g * GROUP_DIM, (g + 1) * GROUP_DIM)
            ws = _window_sum(xs[:, sl], w, False)[HALO:HALO + TILE]
            pg = (ws * _inv_count(tpos, w, seq) - pin[:, sl]).astype(BF16)
            pooled.append(pg)
            yps.append(_dot(pg, wp_ref[g].astype(BF16)))
        yp = jnp.concatenate(yps, axis=1)
        ypool = yp * ps
        gp = gp_ref[...]
        sgp = _sigmoid(gp)
        silu_gp = gp * sgp
        br_p = silu_gp * ypool

        z = jnp.concatenate([br_a, br_p], axis=1).astype(BF16)
        y = _dot(z, wo_ref[...])
        res = x_ref[...] + gate * y - t_ref[...]
        loss_ref[...] += jnp.sum(res * res) * (0.5 / D_MODEL)
        dxn = res * (1.0 / D_MODEL)
        g1_ref[...] = dxn
        dgate_ref[...] += jnp.sum(dxn * y, axis=0, keepdims=True)
        dy = (gate * dxn).astype(BF16)
        dz = _dot_nt(dy, wo_ref[...])
        gwo_ref[...] += _dot_tn(z, dy)

        dbra = dz[:, 0:D_ATTN]
        dbrp = dz[:, D_ATTN:]
        do_ref[...] = dbra * silu_ga
        dga_ref[...] = dbra * attn_t * (sga * (1.0 + ga * (1.0 - sga)))
        dgp_ref[...] = dbrp * ypool * (sgp * (1.0 + gp * (1.0 - sgp)))
        dyp_s = dbrp * silu_gp
        dps_ref[...] += jnp.sum(dyp_s * yp, axis=0, keepdims=True)
        dyp = (dyp_s * ps).astype(BF16)
        for g in range(len(POOL_WINDOWS)):
            sl = slice(g * GROUP_DIM, (g + 1) * GROUP_DIM)
            gwp_ref[g] += _dot_tn(pooled[g], dyp[:, sl])
            dpl_ref[:, sl] = _dot_nt(dyp[:, sl], wp_ref[g].astype(BF16))

    tile = lambda w: pl.BlockSpec((TILE, w), lambda j: (j, 0))
    ucol = lambda col: pl.BlockSpec((TILE, 512), lambda j: (j + 1, col))
    full = lambda shape: pl.BlockSpec(shape, lambda j: (0,) * len(shape))
    return pl.pallas_call(
        body, name="mix_fwd_bwd", grid=(n_tiles,),
        out_shape=(jax.ShapeDtypeStruct((8, 128), F32), jax.ShapeDtypeStruct((seq, D_MODEL), F32),
                   jax.ShapeDtypeStruct((1, D_MODEL), F32), jax.ShapeDtypeStruct((seq, D_ATTN), F32),
                   jax.ShapeDtypeStruct((seq, D_ATTN), F32), jax.ShapeDtypeStruct((seq, D_POOL), F32),
                   jax.ShapeDtypeStruct((seq, D_POOL), F32), jax.ShapeDtypeStruct((D_MODEL, D_MODEL), F32),
                   jax.ShapeDtypeStruct((4, GROUP_DIM, GROUP_DIM), F32), jax.ShapeDtypeStruct((1, D_POOL), F32)),
        in_specs=[tile(D_MODEL), tile(D_MODEL), tile(D_ATTN), ucol(0), ucol(1),
                  pl.BlockSpec((HALO, 512), lambda j: ((j + 1) * rows8 - 1, 1)),
                  pl.BlockSpec((HALO, 512), lambda j: (jnp.minimum((j + 2) * rows8, last8), 1)),
                  ucol(2), full((8, D_MODEL)), full((8, D_MODEL)), full((4, GROUP_DIM, GROUP_DIM)), full((1, D_POOL)),
                  full((D_MODEL, D_MODEL))],
        out_specs=(full((8, 128)), tile(D_MODEL), full((1, D_MODEL)), tile(D_ATTN), tile(D_ATTN), tile(D_POOL),
                   tile(D_POOL), full((D_MODEL, D_MODEL)), full((4, GROUP_DIM, GROUP_DIM)), full((1, D_POOL))),
        compiler_params=pltpu.CompilerParams(dimension_semantics=("arbitrary",), vmem_limit_bytes=VMEM_LIMIT),
    )(x, target, attn, u, u, u, u, u, modrows, bmodrows, w_pool, pool_scale, w_out)


def _inproj_bwd(x, ctx, modrows, bmodrows, norm_g, w_in_p, q_lora_g, w_uq_p, kv_lora_g, w_ukv, qng_p, kng_p, cos, slo, shi,
                u, dq, dk, dv, dga, dgp, dpl, g1):
    seq = x.shape[0]
    n_tiles = seq // TILE + 1
    rows8 = TILE // HALO
    last8 = seq // HALO - 1

    def body(x_ref, ctx_ref, mod_ref, bmod_ref, ng_ref, win_ref, qlg_ref, wuq_ref, kvlg_ref, wukv_ref, qng_ref, kng_ref,
             cos_ref, slo_ref, shi_ref, cq_ref, ckv_ref, kr_ref, dq_ref, dk_ref, dv_ref, dga_ref, dgp_ref, dpl_ref,
             hb_ref, ha_ref, g1_ref,
             gx_ref, gwin_ref, gwuq_ref, gwukv_ref, dqlg_ref, dkvlg_ref, dqng_ref, dkng_ref, dng_ref, dmod_ref,
             du_ref):
        i = pl.program_id(0)
        is_lat = i > 0

        @pl.when(i == 0)
        def _():
            for ref in (gwin_ref, gwuq_ref, gwukv_ref, dqlg_ref, dkvlg_ref, dqng_ref, dkng_ref, dng_ref, dmod_ref):
                ref[...] = jnp.zeros_like(ref)

        xt, r, xg, h, scale = _modulated_input(i, x_ref, ctx_ref, mod_ref, bmod_ref, ng_ref)
        hb = h.astype(BF16)
        cos_t, slo_t, shi_t = cos_ref[...], slo_ref[...], shi_ref[...]

        cq = cq_ref[...]
        rq = lax.rsqrt(jnp.mean(cq * cq, axis=-1, keepdims=True) + NORM_EPS)
        qhat = cq * rq
        qnb = (qhat * qlg_ref[...]).astype(BF16)
        q = _dot(qnb, wuq_ref[...])
        qg = qng_ref[...]
        dq_heads = []
        dqng = jnp.zeros((1, D_HEAD_PAD), F32)
        for hd in range(N_HEADS):
            qh = q[:, hd * D_HEAD_PAD:(hd + 1) * D_HEAD_PAD]
            rr = lax.rsqrt(jnp.sum(qh * qh, axis=-1, keepdims=True) * (1.0 / D_HEAD) + NORM_EPS)
            yq = qh * rr
            dq_h = dq_ref[hd]
            dq_h = jnp.concatenate([dq_h[:, half * D_HEAD_PAD:(half + 1) * D_HEAD_PAD] for half in range(TILE // Q_BLOCK)],
                                   axis=0)
            dpost = jnp.where(is_lat, dq_h * ATTN_SCALE, 0.0)
            dyn = jnp.concatenate([dpost[:, 0:D_NOPE], _rope_t(dpost[:, D_NOPE:], cos_t, slo_t, shi_t)], axis=1)
            dqng = dqng + jnp.sum(dyn * yq, axis=0, keepdims=True)
            dyg = dyn * qg
            dq_heads.append(rr * (dyg - yq * (jnp.sum(dyg * yq, axis=-1, keepdims=True) * (1.0 / D_HEAD))))
        dqng_ref[...] += dqng
        dqf = jnp.concatenate(dq_heads, axis=1).astype(BF16)
        gwuq_ref[...] += _dot_tn(dqf, qnb)
        dqn = _dot_nt(dqf, wuq_ref[...])
        dqlg_ref[...] += jnp.sum(dqn * qhat, axis=0, keepdims=True)
        dqhat = dqn * qlg_ref[...]
        dcq = rq * (dqhat - qhat * jnp.mean(dqhat * qhat, axis=-1, keepdims=True))

        ckv = ckv_ref[...]
        rkv = lax.rsqrt(jnp.mean(ckv * ckv, axis=-1, keepdims=True) + NORM_EPS)
        kvhat = ckv * rkv
        kvnb = (kvhat * kvlg_ref[...]).astype(BF16)
        kv = _dot(kvnb, wukv_ref[...])
        krz = kr_ref[...]
        kr_ss = jnp.sum(krz * krz, axis=-1, keepdims=True)
        kg = kng_ref[...]
        kg_n, kg_r = kg[:, 0:D_NOPE], kg[:, D_NOPE:]
        dkv_parts = []
        dkr = jnp.zeros((TILE, 128), F32)
        dkng_n = jnp.zeros((1, D_NOPE), F32)
        dkng_r = jnp.zeros((1, 128), F32)
        for hd in range(N_HEADS):
            kn = kv[:, hd * 256:hd * 256 + D_NOPE]
            rr = lax.rsqrt((jnp.sum(kn * kn, axis=-1, keepdims=True) + kr_ss) * (1.0 / D_HEAD) + NORM_EPS)
            yn, yr = kn * rr, krz * rr
            dk_h = jnp.transpose(dk_ref[hd]) * LN_2
            dpn = dk_h[:, 0:D_NOPE]
            dpr = _rope_t(dk_h[:, D_NOPE:], cos_t, slo_t, shi_t)
            dkng_n = dkng_n + jnp.sum(dpn * yn, axis=0, keepdims=True)
            dkng_r = dkng_r + jnp.sum(dpr * yr, axis=0, keepdims=True)
            dyn, dyr = dpn * kg_n, dpr * kg_r
            proj = (jnp.sum(dyn * yn, axis=-1, keepdims=True) + jnp.sum(dyr * yr, axis=-1, keepdims=True)) * (1.0 / D_HEAD)
            dkv_parts.append(rr * (dyn - yn * proj))
            dkv_parts.append(jnp.transpose(dv_ref[hd]))
            dkr = dkr + rr * (dyr - yr * proj)
        dkng_ref[:, 0:D_NOPE] += dkng_n
        dkng_ref[:, D_NOPE:] += dkng_r
        dkvf = jnp.concatenate(dkv_parts, axis=1).astype(BF16)
        gwukv_ref[...] += _dot_tn(dkvf, kvnb)
        dkvn = _dot_nt(dkvf, wukv_ref[...])
        dkvlg_ref[...] += jnp.sum(dkvn * kvhat, axis=0, keepdims=True)
        dkvhat = dkvn * kvlg_ref[...]
        dckv = rkv * (dkvhat - kvhat * jnp.mean(dkvhat * kvhat, axis=-1, keepdims=True))

        lat_t = jnp.maximum(i - 1, 0)
        dpl_t = dpl_ref[...]
        ds = jnp.concatenate([jnp.where(i > 1, hb_ref[...], 0.0), dpl_t,
                              jnp.where(i < n_tiles - 1, ha_ref[...], 0.0)], axis=0)
        tpos = lat_t * TILE - HALO + lax.broadcasted_iota(jnp.int32, (TILE + 2 * HALO, 1), 0)
        for g, w in enumerate(POOL_WINDOWS):
            sl = slice(g * GROUP_DIM, (g + 1) * GROUP_DIM)
            wsum = _window_sum(ds[:, sl] * _inv_count(tpos, w, seq), w, True)[HALO:HALO + TILE]
            du_ref[:, O_PIN + g * GROUP_DIM:O_PIN + (g + 1) * GROUP_DIM] = jnp.where(
                is_lat, wsum - dpl_t[:, sl], 0.0).astype(BF16)

        du_ref[:, O_GA:O_GA + D_ATTN] = jnp.where(is_lat, dga_ref[...], 0.0).astype(BF16)
        du_ref[:, O_GP:O_GP + D_POOL] = jnp.where(is_lat, dgp_ref[...], 0.0).astype(BF16)
        du_ref[:, O_CQ:O_CQ + R_Q] = dcq.astype(BF16)
        du_ref[:, O_CKV:O_CKV + R_KV] = dckv.astype(BF16)
        du_ref[:, O_KR:U_PAD] = dkr.astype(BF16)
        dub = du_ref[...]
        dh = _dot_nt(dub, win_ref[...])
        gwin_ref[...] += _dot_tn(dub, hb)

        dshift = jnp.sum(dh, axis=0, keepdims=True)
        dscale = jnp.sum(dh * xg, axis=0, keepdims=True)
        zero = jnp.zeros_like(dshift)
        dmod_ref[0:1, :] += jnp.where(is_lat, dshift, zero)
        dmod_ref[1:2, :] += jnp.where(is_lat, dscale, zero)
        dmod_ref[2:3, :] += jnp.where(is_lat, zero, dshift)
        dmod_ref[3:4, :] += jnp.where(is_lat, zero, dscale)
        dxg = dh * (1.0 + scale)
        xr = xt * r
        dng_ref[...] += jnp.sum(dxg * xr, axis=0, keepdims=True)
        dxn = dxg * ng_ref[...]
        gx_ref[...] = g1_ref[...] + r * (dxn - xr * jnp.mean(dxn * xr, axis=-1, keepdims=True))

    lat = lambda i: (jnp.maximum(i - 1, 0), 0)
    full = lambda shape: pl.BlockSpec(shape, lambda i: (0,) * len(shape))
    rope_spec = pl.BlockSpec((TILE, 128), lambda i: (i, 0))
    return pl.pallas_call(
        body, name="inproj_bwd", grid=(n_tiles,),
        out_shape=(jax.ShapeDtypeStruct((seq, D_MODEL), F32), jax.ShapeDtypeStruct((U_PAD, D_MODEL), F32),
                   jax.ShapeDtypeStruct((N_HEADS * D_HEAD_PAD, R_Q), F32), jax.ShapeDtypeStruct((N_HEADS * 256, R_KV), F32),
                   jax.ShapeDtypeStruct((1, R_Q), F32), jax.ShapeDtypeStruct((1, R_KV), F32),
                   jax.ShapeDtypeStruct((1, D_HEAD_PAD), F32), jax.ShapeDtypeStruct((1, D_HEAD_PAD), F32),
                   jax.ShapeDtypeStruct((1, D_MODEL), F32), jax.ShapeDtypeStruct((8, D_MODEL), F32)),
        in_specs=[pl.BlockSpec((TILE, D_MODEL), lat), full((TILE, D_MODEL)), full((8, D_MODEL)), full((8, D_MODEL)),
                  full((1, D_MODEL)), full((D_MODEL, U_PAD)), full((1, R_Q)), full((R_Q, N_HEADS * D_HEAD_PAD)),
                  full((1, R_KV)), full((R_KV, N_HEADS * 256)), full((1, D_HEAD_PAD)), full((1, D_HEAD_PAD)),
                  rope_spec, rope_spec, rope_spec,
                  pl.BlockSpec((TILE, R_Q), lambda i: (i, O_CQ // R_Q)),
                  pl.BlockSpec((TILE, R_KV), lambda i: (i, O_CKV // R_KV)),
                  pl.BlockSpec((TILE, 128), lambda i: (i, O_KR // 128)),
                  pl.BlockSpec((N_HEADS, Q_BLOCK, TILE // Q_BLOCK * D_HEAD_PAD), lambda i: (0, 0, jnp.maximum(i - 1, 0))),
                  pl.BlockSpec((N_HEADS, D_HEAD_PAD, TILE), lambda i: (0, 0, i)),
                  pl.BlockSpec((N_HEADS, D_V, TILE), lambda i: (0, 0, i)),
                  pl.BlockSpec((TILE, D_ATTN), lat), pl.BlockSpec((TILE, D_POOL), lat), pl.BlockSpec((TILE, D_POOL), lat),
                  pl.BlockSpec((HALO, D_POOL), lambda i: (jnp.maximum((i - 1) * rows8 - 1, 0), 0)),
                  pl.BlockSpec((HALO, D_POOL), lambda i: (jnp.minimum(jnp.maximum(i, 1) * rows8, last8), 0)),
                  pl.BlockSpec((TILE, D_MODEL), lat)],
        out_specs=(pl.BlockSpec((TILE, D_MODEL), lat), full((U_PAD, D_MODEL)), full((N_HEADS * D_HEAD_PAD, R_Q)),
                   full((N_HEADS * 256, R_KV)), full((1, R_Q)), full((1, R_KV)), full((1, D_HEAD_PAD)),
                   full((1, D_HEAD_PAD)), full((1, D_MODEL)), full((8, D_MODEL))),
        scratch_shapes=[pltpu.VMEM((TILE, U_PAD), BF16)],
        compiler_params=pltpu.CompilerParams(dimension_semantics=("arbitrary",), vmem_limit_bytes=VMEM_LIMIT),
    )(x, ctx, modrows, bmodrows, norm_g, w_in_p, q_lora_g, w_uq_p, kv_lora_g, w_ukv, qng_p, kng_p, cos, slo, shi,
      u, u, u, dq, dk, dv, dga, dgp, dpl, dpl, dpl, g1)


SMALL_LAYOUT = ((0, D_MODEL), (1, R_Q), (2, R_KV), (3, D_HEAD_PAD), (4, D_HEAD_PAD), (5, D_POOL))
ROW_DMOD_B, ROW_DMOD_C, ROW_LOSS = 6, 9, 12


def _epilogue(parts, smalls, dmod4, dgate, loss_acc, w_mod_l):
    n_a, n_s = len(parts), len(smalls)
    n_mod = w_mod_l.shape[1]
    blk = [p.shape[1:] for p in parts]
    small_out = [D_MODEL, R_Q, R_KV, D_HEAD, D_HEAD, D_POOL]

    def body(*refs):
        part_refs = refs[0:n_a]
        small_in = refs[n_a:n_a + n_s]
        dmod4_ref, dgate_ref, loss_ref, wmod_ref = refs[n_a + n_s:n_a + n_s + 4]
        outs = refs[n_a + n_s + 4:]
        red_refs = outs[0:n_a]
        lastg_ref, ccg_ref = outs[n_a], outs[n_a + 1]
        sum_refs = outs[n_a + 2:n_a + 2 + n_s]
        gbmod_ref, dmy_ref, lossout_ref = outs[n_a + 2 + n_s:n_a + 5 + n_s]
        scr = outs[n_a + 5 + n_s:]
        recv1, sum1b, recv2 = scr[0:n_a], scr[n_a:2 * n_a], scr[2 * n_a:3 * n_a]
        small_ref, smallg_ref, tot_ref, cc_ref = scr[3 * n_a:3 * n_a + 4]
        ssem1, rsem1, ssem2, rsem2, ssem_s, rsem_s, ssem_l, rsem_l, ssem_cc, rsem_cc = scr[3 * n_a + 4:]
        x, y, c = _coords()
        me = (x, y, c)
        sibling = (x, y, 1 - c)

        stage1 = []
        for a in range(n_a):
            for b in range(4):
                stage1.append(pltpu.make_async_remote_copy(
                    src_ref=part_refs[a].at[4 * (b >> 1) + 2 * (b & 1) + (1 - c)], dst_ref=recv1[a].at[b],
                    send_sem=ssem1.at[a, b], recv_sem=rsem1.at[a, b], device_id=sibling, device_id_type=MESH))
        for cp in stage1:
            cp.start()

        small_ref[...] = jnp.zeros_like(small_ref)
        for ref, (row, width) in zip(small_in, SMALL_LAYOUT):
            small_ref[row:row + 1, 0:width] = ref[...]
        small_ref[ROW_DMOD_B:ROW_DMOD_B + 2, :] = dmod4_ref[0:2, :]
        small_ref[ROW_DMOD_B + 2:ROW_DMOD_B + 3, :] = dgate_ref[...]
        small_ref[ROW_DMOD_C:ROW_DMOD_C + 2, :] = dmod4_ref[2:4, :]
        small_ref[ROW_LOSS:ROW_LOSS + 1, 0:128] = loss_ref[0:1, :]
        smallg_ref[_lin(me)] = small_ref[...]
        s_recv, s_send = _gather_to_all(small_ref, smallg_ref, ssem_s, rsem_s)
        s_recv()
        tot = smallg_ref[0]
        for d in range(1, N_DEV):
            tot = tot + smallg_ref[d]
        tot_ref[...] = tot
        for ref, (row, _), width in zip(sum_refs, SMALL_LAYOUT, small_out):
            ref[...] = tot_ref[row:row + 1, 0:width]
        lossout_ref[...] = tot_ref[8:16, 0:128]
        lin = _lin(me)

        def my_columns(three_rows):
            v = jnp.concatenate(three_rows, axis=1)
            out = jnp.zeros((1, n_mod), F32)
            for d in range(N_DEV):
                out = out + jnp.where(lin == d, v[:, d * n_mod:(d + 1) * n_mod], 0.0)
            return out

        rows3 = lambda ref, r0: [ref[r0 + t:r0 + t + 1, :] for t in range(3)]
        dmodc = rows3(tot_ref, ROW_DMOD_C)
        gbmod_ref[...] = jnp.concatenate(rows3(tot_ref, ROW_DMOD_B), axis=1) + jnp.concatenate(dmodc, axis=1)
        dmy_ref[...] = jnp.zeros_like(dmy_ref)
        for b in range(N_DEV):
            dmy_ref[b:b + 1, :] = my_columns(rows3(smallg_ref.at[b], ROW_DMOD_B))
        dmy = my_columns(dmodc)
        dmy_ref[N_DEV:N_DEV + 1, :] = dmy
        part = lax.dot_general(jnp.broadcast_to(dmy, (8, n_mod)), wmod_ref[...], (((1,), (1,)), ((), ())),
                               precision=HIGHEST, preferred_element_type=F32)

        cc_ref[...] = part
        ccg_ref[_lin(me)] = part
        cc_recv, cc_send = _gather_to_all(cc_ref, ccg_ref, ssem_cc, rsem_cc)

        for a in range(n_a):
            for b in range(4):
                pltpu.make_async_remote_copy(
                    src_ref=part_refs[a].at[0], dst_ref=recv1[a].at[b], send_sem=ssem1.at[a, b], recv_sem=rsem1.at[a, b],
                    device_id=sibling, device_id_type=MESH).wait_recv()
                sum1b[a][b] = (part_refs[a][4 * (b >> 1) + 2 * (b & 1) + c] + recv1[a][b]).astype(BF16)

        stage2 = []
        for a in range(n_a):
            for k in (1, 2, 3):
                tx, ty = _flip(x, (k >> 1) & 1), _flip(y, k & 1)
                stage2.append(pltpu.make_async_remote_copy(
                    src_ref=sum1b[a].at[2 * tx + ty], dst_ref=recv2[a].at[k - 1], send_sem=ssem2.at[a, k - 1],
                    recv_sem=rsem2.at[a, k - 1], device_id=(tx, ty, c), device_id_type=MESH))
        for cp in stage2:
            cp.start()
        for a in range(n_a):
            own = part_refs[a][_lin(me)] + recv1[a][2 * x + y]
            for k in (1, 2, 3):
                stage2[3 * a + k - 1].wait_recv()
                own = own + recv2[a][k - 1].astype(F32)
            red_refs[a][...] = own

        lastg_ref[_lin(me)] = red_refs[n_a - 1][...]
        l_recv, l_send = _gather_to_all(red_refs[n_a - 1], lastg_ref, ssem_l, rsem_l)
        l_recv()
        cc_recv()
        for cp in stage1 + stage2:
            cp.wait_send()
        s_send()
        cc_send()
        l_send()

    vm = pl.BlockSpec(memory_space=pltpu.VMEM)
    sds = jax.ShapeDtypeStruct
    return pl.pallas_call(
        body, name="epilogue_reduce",
        out_shape=(*[sds(s, F32) for s in blk], sds((N_DEV,) + blk[-1], F32), sds((N_DEV, 8, D_MODEL), F32),
                   *[sds((1, w), F32) for w in small_out], sds((1, 3 * D_MODEL), F32), sds((16, n_mod), F32),
                   sds((8, 128), F32)),
        in_specs=[vm] * (n_a + n_s + 4), out_specs=tuple([vm] * (n_a + n_s + 5)),
        scratch_shapes=[*[pltpu.VMEM((4,) + s, F32) for s in blk], *[pltpu.VMEM((4,) + s, BF16) for s in blk],
                        *[pltpu.VMEM((3,) + s, BF16) for s in blk],
                        pltpu.VMEM((SMALL_ROWS, D_MODEL), F32), pltpu.VMEM((N_DEV, SMALL_ROWS, D_MODEL), F32),
                        pltpu.VMEM((SMALL_ROWS, D_MODEL), F32), pltpu.VMEM((8, D_MODEL), F32),
                        pltpu.SemaphoreType.DMA((n_a, 4)), pltpu.SemaphoreType.DMA((n_a, 4)),
                        pltpu.SemaphoreType.DMA((n_a, 3)), pltpu.SemaphoreType.DMA((n_a, 3)),
                        pltpu.SemaphoreType.DMA((7,)), pltpu.SemaphoreType.DMA((7,)),
                        pltpu.SemaphoreType.DMA((7,)), pltpu.SemaphoreType.DMA((7,)),
                        pltpu.SemaphoreType.DMA((7,)), pltpu.SemaphoreType.DMA((7,))],
        compiler_params=pltpu.CompilerParams(vmem_limit_bytes=VMEM_LIMIT),
    )(*parts, *smalls, dmod4, dgate, loss_acc, w_mod_l)


def _adamw(weights, grads, ms, vs, c_all_t, dmod_my, p2g):
    n = len(weights)

    def body(*refs):
        w_refs = refs[0:n]
        g_in = refs[n:2 * n - 2]
        m_refs = refs[2 * n - 2:3 * n - 2]
        v_refs = refs[3 * n - 2:4 * n - 2]
        cat_ref, dmod_ref, p2g_ref = refs[4 * n - 2:4 * n + 1]
        outs = refs[4 * n + 1:]
        gcc_ref, gwm_ref = outs[0], outs[1]
        d_refs, nm_refs, nv_refs = outs[2:2 + n], outs[2 + n:2 + 2 * n], outs[2 + 2 * n:2 + 3 * n]

        part = p2g_ref[0, 0:1, :]
        for d in range(1, N_DEV):
            part = part + p2g_ref[d, 0:1, :]
        cc = w_refs[0][...]
        sg = _sigmoid(cc)
        gcc_ref[...] = part * (sg * (1.0 + cc * (1.0 - sg)))
        cat = cat_ref[...]
        gwm_ref[...] = lax.dot_general(cat * _sigmoid(cat), dmod_ref[...], (((1,), (0,)), ((), ())), precision=HIGHEST,
                                       preferred_element_type=F32)
        for idx in range(n):
            g = (gcc_ref, gwm_ref)[idx][...] if idx < 2 else g_in[idx - 2][...]
            m = ADAM_B1 * m_refs[idx][...] + (1.0 - ADAM_B1) * g
            v = ADAM_B2 * v_refs[idx][...] + (1.0 - ADAM_B2) * (g * g)
            m_hat = m / (1.0 - ADAM_B1 ** ADAM_STEP)
            v_hat = v / (1.0 - ADAM_B2 ** ADAM_STEP)
            d_refs[idx][...] = -ADAM_LR * (m_hat / (jnp.sqrt(v_hat) + ADAM_EPS) + ADAM_WD * w_refs[idx][...])
            nm_refs[idx][...] = m
            nv_refs[idx][...] = v

    vm = pl.BlockSpec(memory_space=pltpu.VMEM)
    shapes = [jax.ShapeDtypeStruct(w.shape, F32) for w in weights]
    args = list(weights) + list(grads[2:]) + list(ms) + list(vs) + [c_all_t, dmod_my, p2g]
    out_shape = tuple(shapes[0:2] + shapes * 3)
    return pl.pallas_call(
        body, name="adamw_update", out_shape=out_shape, in_specs=[vm] * len(args), out_specs=tuple([vm] * len(out_shape)),
        compiler_params=pltpu.CompilerParams(vmem_limit_bytes=VMEM_LIMIT),
    )(*args)


def _rope_tables(seq):
    rows = seq // GRID_W
    row = np.repeat(np.arange(rows, dtype=np.float32), GRID_W)
    col = np.tile(np.arange(GRID_W, dtype=np.float32), rows)
    n_freq = D_ROPE // 4
    inv = (np.float32(ROPE_BASE) ** (-np.arange(n_freq, dtype=np.float32) / np.float32(n_freq))).astype(np.float32)
    ang_r = (row[:, None] * inv).astype(np.float32)
    ang_c = (col[:, None] * inv).astype(np.float32)
    ang = np.concatenate([ang_r, ang_r, ang_c, ang_c], axis=-1)
    cos, sin = np.cos(ang).astype(np.float32), np.sin(ang).astype(np.float32)
    low = (np.arange(D_ROPE) % 32) < 16

    def table(t, fill):
        out = np.full((TILE + seq, 128), fill, np.float32)
        out[TILE:, 0:D_ROPE] = t
        return jnp.asarray(out)

    return table(cos, 1.0), table(np.where(low, -sin, 0.0), 0.0), table(np.where(low, 0.0, sin), 0.0)


def _pad_heads(a):
    lead = a.shape[:-1]
    a4 = a.reshape(lead + (N_HEADS, D_HEAD))
    a4 = jnp.concatenate([a4, jnp.zeros(lead + (N_HEADS, D_HEAD_PAD - D_HEAD), a.dtype)], axis=-1)
    return a4.reshape(lead + (N_HEADS * D_HEAD_PAD,))


def kernel(x, c, ctx, c_ctx, w_mod, b_mod, norm_g, w_in, q_lora_g, w_uq, kv_lora_g, w_ukv, q_norm_g, k_norm_g, w_pool, pool_scale, w_out, loss_target, m_c_ctx, m_w_mod, m_b_mod, m_norm_g, m_w_in, m_q_lora_g, m_w_uq, m_kv_lora_g, m_w_ukv, m_q_norm_g, m_k_norm_g, m_w_pool, m_pool_scale, m_w_out, v_c_ctx, v_w_mod, v_b_mod, v_norm_g, v_w_in, v_q_lora_g, v_w_uq, v_kv_lora_g, v_w_ukv, v_q_norm_g, v_k_norm_g, v_w_pool, v_pool_scale, v_w_out):
    seq = x.shape[1]
    assert ctx.shape[1] == TILE and seq % TILE == 0 and seq % GRID_W == 0
    ix, iy, ic = lax.axis_index("x"), lax.axis_index("y"), lax.axis_index("c")
    me = 4 * ix + 2 * iy + ic
    x2, ctx2, tgt2 = x[0], ctx[0], loss_target[0]
    w_mod_l, w_in_l, w_uq_l, w_ukv_l, w_out_l = w_mod[0], w_in[0], w_uq[0], w_ukv[0], w_out[0]
    n_mod = w_mod_l.shape[1]
    c_ctx_row = c_ctx.reshape(1, D_MODEL)

    cg, modg, wg_in, wg_uq, wg_ukv, wg_out = _prologue(
        jnp.broadcast_to(c, (8, D_MODEL)), jnp.broadcast_to(c_ctx_row, (8, D_MODEL)), w_mod_l,
        [w_in_l, w_uq_l, w_ukv_l, w_out_l])
    mod_all = jnp.transpose(modg, (1, 0, 2)).reshape(16, 3 * D_MODEL)
    mod_b = lax.dynamic_slice_in_dim(mod_all, me, 1, axis=0).reshape(3, D_MODEL)
    mod_c = mod_all[8].reshape(3, D_MODEL)
    modrows = jnp.concatenate([mod_b, mod_c[0:2], jnp.zeros((3, D_MODEL), F32)], axis=0)
    b3 = b_mod.reshape(3, D_MODEL)
    bmodrows = jnp.concatenate([b3, b3[0:2], jnp.zeros((3, D_MODEL), F32)], axis=0)

    w_in_f = jnp.transpose(wg_in, (1, 0, 2)).reshape(D_MODEL, D_IN_PROJ)
    w_in_p = jnp.concatenate([w_in_f[:, 448:], w_in_f[:, 0:384], w_in_f[:, 384:448],
                              jnp.zeros((D_MODEL, U_PAD - D_IN_PROJ), BF16)], axis=1)
    w_uq_p = _pad_heads(jnp.transpose(wg_uq, (1, 0, 2)).reshape(R_Q, N_HEADS * D_HEAD))
    w_ukv_f = jnp.transpose(wg_ukv, (1, 0, 2)).reshape(R_KV, N_HEADS * 256)
    w_out_f = wg_out.reshape(D_MODEL, D_MODEL)
    qng_p = jnp.concatenate([q_norm_g, jnp.zeros((1, D_HEAD_PAD - D_HEAD), F32)], axis=1)
    kng_p = jnp.concatenate([k_norm_g, jnp.zeros((1, D_HEAD_PAD - D_HEAD), F32)], axis=1)
    cos, slo, shi = _rope_tables(seq)

    u, q_perm, k, v = _inproj_fwd(x2, ctx2, modrows, bmodrows, norm_g, w_in_p, q_lora_g, w_uq_p, kv_lora_g, w_ukv_f,
                                  qng_p, kng_p, cos, slo, shi)
    q = q_perm.reshape(N_HEADS, seq, D_HEAD_PAD)
    attn, lse = _attn_fwd(q, k, v, 1024)
    (loss_acc, g1, dgate, d_attn, dga, dgp, dpl, gwo, gwp, dps) = _mix(
        x2, tgt2, attn, u, modrows, bmodrows, w_pool[0], pool_scale, w_out_f)

    dq, dk, dv = _attn_bwd(q, k, v, attn, lse, d_attn, 256)
    dq = dq.reshape(N_HEADS, Q_BLOCK, seq // Q_BLOCK * D_HEAD_PAD)
    (grad_x, gwin_p, gwuq_p, gwukv_t, dqlg, dkvlg, dqng, dkng, dng, dmod4) = _inproj_bwd(
        x2, ctx2, modrows, bmodrows, norm_g, w_in_p, q_lora_g, w_uq_p, kv_lora_g, w_ukv_f, qng_p, kng_p, cos, slo, shi,
        u, dq, dk, dv, dga, dgp, dpl, g1)

    gwin_t = jnp.concatenate([gwin_p[O_CQ:O_KR], gwin_p[O_KR:D_IN_PROJ], gwin_p[0:O_CQ]], axis=0)
    gwuq_t = gwuq_p.reshape(N_HEADS, D_HEAD_PAD, R_Q)[:, 0:D_HEAD].reshape(N_HEADS * D_HEAD, R_Q)
    blocks = lambda a: a.reshape((N_DEV, a.shape[0] // N_DEV) + a.shape[1:])
    parts = [blocks(gwin_t), blocks(gwuq_t), blocks(gwukv_t), blocks(gwo), blocks(gwp.reshape(4 * GROUP_DIM, GROUP_DIM))]
    (r_win, r_wuq, r_wukv, r_wout, _, wpool_g, ccg, g_ng, g_qlg, g_kvlg, g_qng, g_kng, g_ps, g_bmod, dmod_my,
     loss_rows) = _epilogue(parts, [dng, dqlg, dkvlg, dqng, dkng, dps], dmod4, dgate, loss_acc, w_mod_l)

    g_w_in = jnp.transpose(r_win)
    g_w_uq = jnp.transpose(r_wuq)
    g_w_ukv = jnp.transpose(r_wukv)
    g_w_pool = wpool_g.reshape(4 * GROUP_DIM, GROUP_DIM)
    c_rows = cg[:, 0, :]
    c_all_t = jnp.transpose(jnp.concatenate([c_rows, c_ctx_row, jnp.zeros((16 - N_DEV - 1, D_MODEL), F32)], axis=0))

    weights = [c_ctx_row, w_mod_l, b_mod, norm_g, w_in_l, q_lora_g, w_uq_l, kv_lora_g, w_ukv_l, q_norm_g, k_norm_g,
               w_pool.reshape(4 * GROUP_DIM, GROUP_DIM), pool_scale, w_out_l]
    grads = [None, None, g_bmod, g_ng, g_w_in, g_qlg, g_w_uq, g_kvlg, g_w_ukv, g_qng, g_kng, g_w_pool, g_ps, r_wout]
    ms = [m_c_ctx.reshape(1, D_MODEL), m_w_mod[0], m_b_mod, m_norm_g, m_w_in[0], m_q_lora_g, m_w_uq[0], m_kv_lora_g,
          m_w_ukv[0], m_q_norm_g, m_k_norm_g, m_w_pool.reshape(4 * GROUP_DIM, GROUP_DIM), m_pool_scale, m_w_out[0]]
    vs = [v_c_ctx.reshape(1, D_MODEL), v_w_mod[0], v_b_mod, v_norm_g, v_w_in[0], v_q_lora_g, v_w_uq[0], v_kv_lora_g,
          v_w_ukv[0], v_q_norm_g, v_k_norm_g, v_w_pool.reshape(4 * GROUP_DIM, GROUP_DIM), v_pool_scale, v_w_out[0]]
    outs = _adamw(weights, grads, ms, vs, c_all_t, dmod_my, ccg)
    grads[0], grads[1] = outs[0], outs[1]
    n = len(weights)
    deltas, new_m, new_v = outs[2:2 + n], outs[2 + n:2 + 2 * n], outs[2 + 2 * n:2 + 3 * n]

    loss = loss_rows[ROW_LOSS - 8, 0]
    final = [c_ctx.shape, w_mod.shape, b_mod.shape, norm_g.shape, w_in.shape, q_lora_g.shape, w_uq.shape, kv_lora_g.shape,
             w_ukv.shape, q_norm_g.shape, k_norm_g.shape, w_pool.shape, pool_scale.shape, w_out.shape]
    shaped = lambda arrs: [a.reshape(s) for a, s in zip(arrs, final)]
    return (loss, grad_x[None], *shaped(grads), *shaped(deltas), *shaped(new_m), *shaped(new_v))
```

```python
import numpy as np

import jax
import jax.numpy as jnp
from jax import lax
from jax.experimental import pallas as pl
from jax.experimental.pallas import tpu as pltpu

F32 = jnp.float32
BF16 = jnp.bfloat16
MESH = pl.DeviceIdType.MESH
HIGHEST = lax.Precision.HIGHEST

D_MODEL = 1024
N_HEADS = 4
D_NOPE = 128
D_ROPE = 64
D_HEAD = D_NOPE + D_ROPE
D_HEAD_PAD = 256
D_V = 128
R_Q = 256
R_KV = 128
D_ATTN = 512
D_POOL = 512
POOL_WINDOWS = (2, 4, 8, 16)
GROUP_DIM = 128
GRID_W = 64
ROPE_BASE = 10000.0
NORM_EPS = 1e-6
ATTN_SCALE = D_HEAD ** -0.5
LOG2_E = 1.4426950408889634
LN_2 = 0.6931471805599453
ATTN_ROWS = 256
D_IN_PROJ = 1984
U_PAD = 2048
O_GA, O_PIN, O_GP, O_CQ, O_CKV, O_KR = 0, 512, 1024, 1536, 1792, 1920
TILE = 256
Q_BLOCK = 128
HALO = 8
N_DEV = 8
VMEM_LIMIT = 56 * 1024 * 1024

ADAM_LR = 0.001
ADAM_B1 = 0.9
ADAM_B2 = 0.999
ADAM_EPS = 1e-08
ADAM_WD = 0.01
ADAM_STEP = 10

SMALL_ROWS = 16


def _dot(a, b):
    return lax.dot_general(a, b, (((1,), (0,)), ((), ())), preferred_element_type=F32)


def _dot_nt(a, b):
    return lax.dot_general(a, b, (((1,), (1,)), ((), ())), preferred_element_type=F32)


def _dot_tn(a, b):
    return lax.dot_general(a, b, (((0,), (0,)), ((), ())), preferred_element_type=F32)


def _sigmoid(x):
    return 1.0 / (1.0 + jnp.exp(-x))


def _rope(p, cos, slo, shi):
    return p * cos + pltpu.roll(p, 112, 1) * slo + pltpu.roll(p, 16, 1) * shi


def _rope_t(d, cos, slo, shi):
    return d * cos + pltpu.roll(d * slo, 16, 1) + pltpu.roll(d * shi, 112, 1)


def _shift_rows(a, k):
    n = a.shape[0]
    return pltpu.roll(a, (-k) % n, 0)


def _window_sum(x, w, transposed):
    s = (x + _shift_rows(x, 1)) if transposed else (_shift_rows(x, -1) + x)
    step = 1
    while 2 * step < w:
        s = _shift_rows(s, -step) + _shift_rows(s, step)
        step *= 2
    return s


def _inv_count(tpos, w, seq):
    lo = jnp.maximum(tpos - w // 2, 0)
    hi = jnp.minimum(tpos - w // 2 + w, seq)
    return 1.0 / jnp.maximum(hi - lo, 1).astype(F32)


def _coords():
    return lax.axis_index("x"), lax.axis_index("y"), lax.axis_index("c")


def _flip(v, bit):
    return (1 - v) if bit else v


def _peer(k):
    x, y, c = _coords()
    return (_flip(x, (k >> 2) & 1), _flip(y, (k >> 1) & 1), _flip(c, k & 1))


def _lin(p):
    return 4 * p[0] + 2 * p[1] + p[2]


def _gather_to_all(src_ref, slots_ref, send_sems, recv_sems):
    me = _coords()
    copies = []
    for k in range(1, N_DEV):
        cp = pltpu.make_async_remote_copy(
            src_ref=src_ref, dst_ref=slots_ref.at[_lin(me)], send_sem=send_sems.at[k - 1], recv_sem=recv_sems.at[k - 1],
            device_id=_peer(k), device_id_type=MESH)
        cp.start()
        copies.append(cp)

    def wait_recv():
        for k in range(1, N_DEV):
            pltpu.make_async_remote_copy(
                src_ref=src_ref, dst_ref=slots_ref.at[_lin(_peer(k))], send_sem=send_sems.at[k - 1],
                recv_sem=recv_sems.at[k - 1], device_id=_peer(k), device_id_type=MESH).wait_recv()

    def wait_send():
        for cp in copies:
            cp.wait_send()

    return wait_recv, wait_send


def _prologue(c8, cctx8, w_mod_l, shards):
    n_mod = w_mod_l.shape[1]
    n_w = len(shards)

    def body(c8_ref, cctx_ref, wmod_ref, *refs):
        w_refs = refs[0:n_w]
        cg_ref, modg_ref = refs[n_w], refs[n_w + 1]
        wg_refs = refs[n_w + 2:2 * n_w + 2]
        modblk_ref = refs[2 * n_w + 2]
        wb_refs = refs[2 * n_w + 3:3 * n_w + 3]
        ssem_w, rsem_w, ssem_c, rsem_c, ssem_m, rsem_m = refs[3 * n_w + 3:]
        x, y, c = _coords()
        me = (x, y, c)
        sibling = (x, y, 1 - c)
        chips = [(1 - x, y), (x, 1 - y), (1 - x, 1 - y)]

        def wcopies(k, block, to, own=False):
            out = []
            for a in range(n_w):
                slot = wg_refs[a].at[_lin(block)]
                out.append(pltpu.make_async_remote_copy(
                    src_ref=wb_refs[a] if own else slot, dst_ref=slot, send_sem=ssem_w.at[a, k], recv_sem=rsem_w.at[a, k],
                    device_id=to, device_id_type=MESH))
            return out

        for a in range(n_w):
            wb_refs[a][...] = w_refs[a][...].astype(BF16)
        first = wcopies(0, me, sibling, own=True)
        for j, chip in enumerate(chips):
            first += wcopies(1 + j, me, (*chip, c), own=True)
        for cp in first:
            cp.start()
        for a in range(n_w):
            wg_refs[a][_lin(me)] = wb_refs[a][...]

        cg_ref[_lin(me)] = c8_ref[...]
        c_recv, c_send = _gather_to_all(c8_ref, cg_ref, ssem_c, rsem_c)
        c_recv()
        row = lax.broadcasted_iota(jnp.int32, (8, D_MODEL), 0)
        c_all = jnp.zeros((8, D_MODEL), F32)
        for d in range(N_DEV):
            c_all = c_all + jnp.where(row == d, cg_ref[d], 0.0)
        cc = cctx_ref[...]
        a = jnp.concatenate([c_all * _sigmoid(c_all), cc * _sigmoid(cc)], axis=0)
        modblk_ref[...] = lax.dot_general(a, wmod_ref[...], (((1,), (0,)), ((), ())), precision=HIGHEST,
                                          preferred_element_type=F32)
        modg_ref[_lin(me)] = modblk_ref[...]
        m_recv, m_send = _gather_to_all(modblk_ref, modg_ref, ssem_m, rsem_m)
        m_recv()

        passed = []
        for j, chip in enumerate(chips):
            for cp in wcopies(1 + j, (*chip, c), me):
                cp.wait_recv()
            fwd = wcopies(4 + j, (*chip, c), sibling)
            for cp in fwd:
                cp.start()
            passed += fwd
        for cp in wcopies(0, sibling, me):
            cp.wait_recv()
        for j, chip in enumerate(chips):
            for cp in wcopies(4 + j, (*chip, 1 - c), me):
                cp.wait_recv()
        for cp in first + passed:
            cp.wait_send()
        c_send()
        m_send()

    vm = pl.BlockSpec(memory_space=pltpu.VMEM)
    return pl.pallas_call(
        body, name="prologue_gather",
        out_shape=(jax.ShapeDtypeStruct((N_DEV, 8, D_MODEL), F32), jax.ShapeDtypeStruct((N_DEV, 16, n_mod), F32),
                   *[jax.ShapeDtypeStruct((N_DEV,) + s.shape, BF16) for s in shards]),
        in_specs=[vm] * (3 + n_w), out_specs=tuple([vm] * (2 + n_w)),
        scratch_shapes=[pltpu.VMEM((16, n_mod), F32), *[pltpu.VMEM(s.shape, BF16) for s in shards],
                        pltpu.SemaphoreType.DMA((n_w, 7)), pltpu.SemaphoreType.DMA((n_w, 7)),
                        pltpu.SemaphoreType.DMA((7,)), pltpu.SemaphoreType.DMA((7,)),
                        pltpu.SemaphoreType.DMA((7,)), pltpu.SemaphoreType.DMA((7,))],
        compiler_params=pltpu.CompilerParams(vmem_limit_bytes=VMEM_LIMIT),
    )(c8, cctx8, w_mod_l, *shards)


def _modulated_input(i, x_ref, ctx_ref, mod_ref, bmod_ref, ng_ref):
    is_ctx = i == 0
    xt = jnp.where(is_ctx, ctx_ref[...], x_ref[...])
    shift = jnp.where(is_ctx, mod_ref[3:4, :] + bmod_ref[3:4, :], mod_ref[0:1, :] + bmod_ref[0:1, :])
    scale = jnp.where(is_ctx, mod_ref[4:5, :] + bmod_ref[4:5, :], mod_ref[1:2, :] + bmod_ref[1:2, :])
    r = lax.rsqrt(jnp.mean(xt * xt, axis=-1, keepdims=True) + NORM_EPS)
    xg = (xt * r) * ng_ref[...]
    h = xg * (1.0 + scale) + shift
    return xt, r, xg, h, scale


def _inproj_fwd(x, ctx, modrows, bmodrows, norm_g, w_in_p, q_lora_g, w_uq_p, kv_lora_g, w_ukv, qng_p, kng_p, cos, slo, shi):
    seq = x.shape[0]
    n_tiles = seq // TILE + 1
    tot = seq + TILE

    def body(x_ref, ctx_ref, mod_ref, bmod_ref, ng_ref, win_ref, qlg_ref, wuq_ref, kvlg_ref, wukv_ref, qng_ref, kng_ref,
             cos_ref, slo_ref, shi_ref, u_ref, q_ref, k_ref, v_ref):
        i = pl.program_id(0)
        _, _, _, h, _ = _modulated_input(i, x_ref, ctx_ref, mod_ref, bmod_ref, ng_ref)
        u = _dot_nt(h.astype(BF16), win_ref[...])
        u_ref[...] = u
        cos_t, slo_t, shi_t = cos_ref[...], slo_ref[...], shi_ref[...]

        cq = u[:, O_CQ:O_CQ + R_Q]
        qn = cq * lax.rsqrt(jnp.mean(cq * cq, axis=-1, keepdims=True) + NORM_EPS) * qlg_ref[...]
        q = _dot_nt(qn.astype(BF16), wuq_ref[...])
        qg = qng_ref[...] * (ATTN_SCALE * LOG2_E)
        for hd in range(N_HEADS):
            qh = q[:, hd * D_HEAD_PAD:(hd + 1) * D_HEAD_PAD]
            rr = lax.rsqrt(jnp.sum(qh * qh, axis=-1, keepdims=True) * (1.0 / D_HEAD) + NORM_EPS)
            qy = qh * rr * qg
            q_ref[hd, :, 0:D_NOPE] = qy[:, 0:D_NOPE].astype(BF16)
            q_ref[hd, :, D_NOPE:D_HEAD_PAD] = _rope(qy[:, D_NOPE:D_HEAD_PAD], cos_t, slo_t, shi_t).astype(BF16)

        ckv = u[:, O_CKV:O_CKV + R_KV]
        kvn = ckv * lax.rsqrt(jnp.mean(ckv * ckv, axis=-1, keepdims=True) + NORM_EPS) * kvlg_ref[...]
        kv = _dot(kvn.astype(BF16), wukv_ref[...])
        krz = u[:, O_KR:U_PAD]
        kr_ss = jnp.sum(krz * krz, axis=-1, keepdims=True)
        kg = kng_ref[...]
        for hd in range(N_HEADS):
            kn = kv[:, hd * 256:hd * 256 + D_NOPE]
            rr = lax.rsqrt((jnp.sum(kn * kn, axis=-1, keepdims=True) + kr_ss) * (1.0 / D_HEAD) + NORM_EPS)
            k_ref[hd, :, 0:D_NOPE] = (kn * rr * kg[:, 0:D_NOPE]).astype(BF16)
            k_ref[hd, :, D_NOPE:D_HEAD_PAD] = _rope(krz * rr * kg[:, D_NOPE:D_HEAD_PAD], cos_t, slo_t, shi_t).astype(BF16)
            v_ref[hd] = kv[:, hd * 256 + D_NOPE:(hd + 1) * 256].astype(BF16)

    lat = lambda i: (jnp.maximum(i - 1, 0), 0)
    full = lambda shape: pl.BlockSpec(shape, lambda i: (0,) * len(shape))
    return pl.pallas_call(
        body, name="inproj_fwd", grid=(n_tiles,),
        out_shape=(jax.ShapeDtypeStruct((tot, U_PAD), F32),
                   jax.ShapeDtypeStruct((N_HEADS, seq, D_HEAD_PAD), BF16),
                   jax.ShapeDtypeStruct((N_HEADS, tot, D_HEAD_PAD), BF16),
                   jax.ShapeDtypeStruct((N_HEADS, tot, D_V), BF16)),
        in_specs=[pl.BlockSpec((TILE, D_MODEL), lat), full((TILE, D_MODEL)), full((8, D_MODEL)), full((8, D_MODEL)),
                  full((1, D_MODEL)), full((U_PAD, D_MODEL)), full((1, R_Q)), full((N_HEADS * D_HEAD_PAD, R_Q)),
                  full((1, R_KV)), full((R_KV, N_HEADS * 256)), full((1, D_HEAD_PAD)), full((1, D_HEAD_PAD)),
                  pl.BlockSpec((TILE, 128), lambda i: (i, 0)), pl.BlockSpec((TILE, 128), lambda i: (i, 0)),
                  pl.BlockSpec((TILE, 128), lambda i: (i, 0))],
        out_specs=(pl.BlockSpec((TILE, U_PAD), lambda i: (i, 0)),
                   pl.BlockSpec((N_HEADS, TILE, D_HEAD_PAD), lambda i: (0, jnp.maximum(i - 1, 0), 0)),
                   pl.BlockSpec((N_HEADS, TILE, D_HEAD_PAD), lambda i: (0, i, 0)),
                   pl.BlockSpec((N_HEADS, TILE, D_V), lambda i: (0, i, 0))),
        compiler_params=pltpu.CompilerParams(dimension_semantics=("arbitrary",), vmem_limit_bytes=VMEM_LIMIT),
    )(x, ctx, modrows, bmodrows, norm_g, w_in_p, q_lora_g, w_uq_p, kv_lora_g, w_ukv, qng_p, kng_p, cos, slo, shi)


def _attn_fwd(q, k, v, block_q):
    _, seq, _ = q.shape
    n_keys = k.shape[1]

    def body(q_ref, k_ref, v_ref, o_ref, lse_ref):
        kb, vb = k_ref[0], v_ref[0]
        for r0 in range(0, block_q, ATTN_ROWS):
            rows = slice(r0, r0 + ATTN_ROWS)
            s = _dot_nt(q_ref[0, rows, :], kb)
            m = jnp.max(s, axis=-1, keepdims=True)
            p = jnp.exp2(s - m)
            l = jnp.sum(p, axis=-1, keepdims=True)
            o_ref[rows, :] = _dot(p.astype(BF16), vb) * (1.0 / l)
            lse_ref[0, rows, :] = m + jnp.log2(l)

    return pl.pallas_call(
        body, name="attn_fwd", grid=(N_HEADS, seq // block_q),
        out_shape=(jax.ShapeDtypeStruct((seq, D_ATTN), F32), jax.ShapeDtypeStruct((N_HEADS, seq, 1), F32)),
        in_specs=[pl.BlockSpec((1, block_q, D_HEAD_PAD), lambda h, i: (h, i, 0)),
                  pl.BlockSpec((1, n_keys, D_HEAD_PAD), lambda h, i: (h, 0, 0)),
                  pl.BlockSpec((1, n_keys, D_V), lambda h, i: (h, 0, 0))],
        out_specs=(pl.BlockSpec((block_q, D_V), lambda h, i: (i, h)),
                   pl.BlockSpec((1, block_q, 1), lambda h, i: (h, i, 0))),
        compiler_params=pltpu.CompilerParams(dimension_semantics=("arbitrary", "arbitrary"), vmem_limit_bytes=VMEM_LIMIT),
    )(q, k, v)


def _attn_bwd(q, k, v, o, lse, d_o, block_q):
    _, seq, _ = q.shape
    n_keys = k.shape[1]

    def body(q_ref, k_ref, v_ref, o_ref, lse_ref, do_ref, dq_ref, dkt_ref, dvt_ref):
        i = pl.program_id(1)

        @pl.when(i == 0)
        def _():
            dkt_ref[...] = jnp.zeros_like(dkt_ref)
            dvt_ref[...] = jnp.zeros_like(dvt_ref)

        kb, vb = k_ref[0], v_ref[0]
        for r0 in range(0, block_q, ATTN_ROWS):
            rows = slice(r0, r0 + ATTN_ROWS)
            qb = q_ref[0, rows, :]
            d_out = do_ref[rows, :]
            p = jnp.exp2(_dot_nt(qb, kb) - lse_ref[0, rows, :])
            dob = d_out.astype(BF16)
            dp = _dot_nt(dob, vb)
            delta = jnp.sum(d_out * o_ref[rows, :], axis=-1, keepdims=True)
            raw = (p * (dp - delta)).astype(BF16)
            dq_ref[0, rows, :] = _dot(raw, kb)
            dkt_ref[0] += _dot_tn(qb, raw)
            dvt_ref[0] += _dot_tn(dob, p.astype(BF16))

    return pl.pallas_call(
        body, name="attn_bwd", grid=(N_HEADS, seq // block_q),
        out_shape=(jax.ShapeDtypeStruct((N_HEADS, seq, D_HEAD_PAD), F32),
                   jax.ShapeDtypeStruct((N_HEADS, D_HEAD_PAD, n_keys), F32),
                   jax.ShapeDtypeStruct((N_HEADS, D_V, n_keys), F32)),
        in_specs=[pl.BlockSpec((1, block_q, D_HEAD_PAD), lambda h, i: (h, i, 0)),
                  pl.BlockSpec((1, n_keys, D_HEAD_PAD), lambda h, i: (h, 0, 0)),
                  pl.BlockSpec((1, n_keys, D_V), lambda h, i: (h, 0, 0)),
                  pl.BlockSpec((block_q, D_V), lambda h, i: (i, h)),
                  pl.BlockSpec((1, block_q, 1), lambda h, i: (h, i, 0)),
                  pl.BlockSpec((block_q, D_V), lambda h, i: (i, h))],
        out_specs=(pl.BlockSpec((1, block_q, D_HEAD_PAD), lambda h, i: (h, i, 0)),
                   pl.BlockSpec((1, D_HEAD_PAD, n_keys), lambda h, i: (h, 0, 0)),
                   pl.BlockSpec((1, D_V, n_keys), lambda h, i: (h, 0, 0))),
        compiler_params=pltpu.CompilerParams(dimension_semantics=("arbitrary", "arbitrary"), vmem_limit_bytes=VMEM_LIMIT),
    )(q, k, v, o, lse, d_o)


def _mix(x, target, attn, u, modrows, bmodrows, w_pool, pool_scale, w_out):
    seq = x.shape[0]
    n_tiles = seq // TILE
    rows8 = TILE // HALO
    last8 = (seq + TILE) // HALO - 1

    n_blk = seq // Q_BLOCK
    per = TILE // n_blk
    assert TILE % n_blk == 0 and per % 8 == 0

    def body(x_ref, t_ref, o_hbm, ga_ref, pin_ref, hb_ref, ha_ref, gp_ref, mod_ref, bmod_ref, wp_ref, ps_ref, wo_ref,
             loss_ref, g1_ref, dgate_ref, do_hbm, dga_ref, dgp_ref, dpl_ref, gwo_ref, gwp_ref, dps_ref,
             abuf, dbuf, rsem, wsem):
        j = pl.program_id(0)

        def slab_copies(tile, slot, fetch):
            out = []
            for b in range(n_blk):
                hbm_rows = pl.ds(b * Q_BLOCK + tile * per, per)
                buf_rows = pl.ds(b * per, per)
                if fetch:
                    out.append(pltpu.make_async_copy(o_hbm.at[hbm_rows], abuf.at[slot, buf_rows], rsem.at[slot]))
                else:
                    out.append(pltpu.make_async_copy(dbuf.at[slot, buf_rows], do_hbm.at[hbm_rows], wsem.at[slot]))
            return out

        def wait_all(slot, fetch):
            if fetch:
                pltpu.make_async_copy(o_hbm.at[pl.ds(0, TILE)], abuf.at[slot], rsem.at[slot]).wait()
            else:
                pltpu.make_async_copy(dbuf.at[slot], do_hbm.at[pl.ds(0, TILE)], wsem.at[slot]).wait()

        slot = lax.rem(j, 2)

        @pl.when(j == 0)
        def _():
            loss_ref[...] = jnp.zeros_like(loss_ref)
            dgate_ref[...] = jnp.zeros_like(dgate_ref)
            gwo_ref[...] = jnp.zeros_like(gwo_ref)
            gwp_ref[...] = jnp.zeros_like(gwp_ref)
            dps_ref[...] = jnp.zeros_like(dps_ref)
            for cp in slab_copies(0, 0, True):
                cp.start()

        @pl.when(j + 1 < n_tiles)
        def _():
            for cp in slab_copies(j + 1, 1 - slot, True):
                cp.start()

        wait_all(slot, True)
        attn_t = jnp.swapaxes(abuf[slot].reshape(n_blk, per, D_ATTN), 0, 1).reshape(TILE, D_ATTN)

        gate = mod_ref[2:3, :] + bmod_ref[2:3, :]
        ga = ga_ref[...]
        sga = _sigmoid(ga)
        silu_ga = ga * sga
        br_a = silu_ga * attn_t

        pin = pin_ref[...]
        xs = jnp.concatenate([jnp.where(j > 0, hb_ref[...], 0.0), pin, jnp.where(j < n_tiles - 1, ha_ref[...], 0.0)], axis=0)
        tpos = j * TILE + lax.broadcasted_iota(jnp.int32, (TILE, 1), 0)
        ps = ps_ref[...]
        pooled, yps = [], []
        for g, w in enumerate(POOL_WINDOWS):
            sl = slice(g * GROUP_DIM, (g + 1) * GROUP_DIM)
            ws = _window_sum(xs[:, sl], w, False)[HALO:HALO + TILE]
            pg = (ws * _inv_count(tpos, w, seq) - pin[:, sl]).astype(BF16)
            pooled.append(pg)
            yps.append(_dot(pg, wp_ref[g].astype(BF16)))
        yp = jnp.concatenate(yps, axis=1)
        ypool = yp * ps
        gp = gp_ref[...]
        sgp = _sigmoid(gp)
        silu_gp = gp * sgp
        br_p = silu_gp * ypool

        z = jnp.concatenate([br_a, br_p], axis=1).astype(BF16)
        y = _dot(z, wo_ref[...])
        res = x_ref[...] + gate * y - t_ref[...]
        loss_ref[...] += jnp.sum(res * res) * (0.5 / D_MODEL)
        dxn = res * (1.0 / D_MODEL)
        g1_ref[...] = dxn
        dgate_ref[...] += jnp.sum(dxn * y, axis=0, keepdims=True)
        dy = (gate * dxn).astype(BF16)
        dz = _dot_nt(dy, wo_ref[...])
        gwo_ref[...] += _dot_tn(z, dy)

        dbra = dz[:, 0:D_ATTN]
        dbrp = dz[:, D_ATTN:]
        dga_ref[...] = dbra * attn_t * (sga * (1.0 + ga * (1.0 - sga)))

        @pl.when(j >= 2)
        def _():
            wait_all(slot, False)
        dbuf[slot] = jnp.swapaxes((dbra * silu_ga).reshape(per, n_blk, D_ATTN), 0, 1).reshape(TILE, D_ATTN)
        for cp in slab_copies(j, slot, False):
            cp.start()

        @pl.when(j == n_tiles - 1)
        def _():
            wait_all(slot, False)
            if n_tiles >= 2:
                wait_all(1 - slot, False)

        dgp_ref[...] = dbrp * ypool * (sgp * (1.0 + gp * (1.0 - sgp)))
        dyp_s = dbrp * silu_gp
        dps_ref[...] += jnp.sum(dyp_s * yp, axis=0, keepdims=True)
        dyp = (dyp_s * ps).astype(BF16)
        for g in range(len(POOL_WINDOWS)):
            sl = slice(g * GROUP_DIM, (g + 1) * GROUP_DIM)
            gwp_ref[g] += _dot_tn(pooled[g], dyp[:, sl])
            dpl_ref[:, sl] = _dot_nt(dyp[:, sl], wp_ref[g].astype(BF16))

    tile = lambda w: pl.BlockSpec((TILE, w), lambda j: (j, 0))
    ucol = lambda col: pl.BlockSpec((TILE, 512), lambda j: (j + 1, col))
    full = lambda shape: pl.BlockSpec(shape, lambda j: (0,) * len(shape))
    return pl.pallas_call(
        body, name="mix_fwd_bwd", grid=(n_tiles,),
        out_shape=(jax.ShapeDtypeStruct((8, 128), F32), jax.ShapeDtypeStruct((seq, D_MODEL), F32),
                   jax.ShapeDtypeStruct((1, D_MODEL), F32), jax.ShapeDtypeStruct((seq, D_ATTN), F32),
                   jax.ShapeDtypeStruct((seq, D_ATTN), F32), jax.ShapeDtypeStruct((seq, D_POOL), F32),
                   jax.ShapeDtypeStruct((seq, D_POOL), F32), jax.ShapeDtypeStruct((D_MODEL, D_MODEL), F32),
                   jax.ShapeDtypeStruct((4, GROUP_DIM, GROUP_DIM), F32), jax.ShapeDtypeStruct((1, D_POOL), F32)),
        in_specs=[tile(D_MODEL), tile(D_MODEL), pl.BlockSpec(memory_space=pl.ANY), ucol(0), ucol(1),
                  pl.BlockSpec((HALO, 512), lambda j: ((j + 1) * rows8 - 1, 1)),
                  pl.BlockSpec((HALO, 512), lambda j: (jnp.minimum((j + 2) * rows8, last8), 1)),
                  ucol(2), full((8, D_MODEL)), full((8, D_MODEL)), full((4, GROUP_DIM, GROUP_DIM)), full((1, D_POOL)),
                  full((D_MODEL, D_MODEL))],
        out_specs=(full((8, 128)), tile(D_MODEL), full((1, D_MODEL)), pl.BlockSpec(memory_space=pl.ANY), tile(D_ATTN),
                   tile(D_POOL), tile(D_POOL), full((D_MODEL, D_MODEL)), full((4, GROUP_DIM, GROUP_DIM)), full((1, D_POOL))),
        scratch_shapes=[pltpu.VMEM((2, TILE, D_ATTN), F32), pltpu.VMEM((2, TILE, D_ATTN), F32),
                        pltpu.SemaphoreType.DMA((2,)), pltpu.SemaphoreType.DMA((2,))],
        compiler_params=pltpu.CompilerParams(dimension_semantics=("arbitrary",), vmem_limit_bytes=VMEM_LIMIT),
    )(x, target, attn, u, u, u, u, u, modrows, bmodrows, w_pool, pool_scale, w_out)


def _inproj_bwd(x, ctx, modrows, bmodrows, norm_g, w_in_p, q_lora_g, w_uq_p, kv_lora_g, w_ukv, qng_p, kng_p, cos, slo, shi,
                u, dq, dk, dv, dga, dgp, dpl, g1):
    seq = x.shape[0]
    n_tiles = seq // TILE + 1
    rows8 = TILE // HALO
    last8 = seq // HALO - 1

    def body(x_ref, ctx_ref, mod_ref, bmod_ref, ng_ref, win_ref, qlg_ref, wuq_ref, kvlg_ref, wukv_ref, qng_ref, kng_ref,
             cos_ref, slo_ref, shi_ref, cq_ref, ckv_ref, kr_ref, dq_ref, dk_ref, dv_ref, dga_ref, dgp_ref, dpl_ref,
             hb_ref, ha_ref, g1_ref,
             gx_ref, gwin_ref, gwuq_ref, gwukv_ref, dqlg_ref, dkvlg_ref, dqng_ref, dkng_ref, dng_ref, dmod_ref,
             du_ref):
        i = pl.program_id(0)
        is_lat = i > 0

        @pl.when(i == 0)
        def _():
            for ref in (gwin_ref, gwuq_ref, gwukv_ref, dqlg_ref, dkvlg_ref, dqng_ref, dkng_ref, dng_ref, dmod_ref):
                ref[...] = jnp.zeros_like(ref)

        xt, r, xg, h, scale = _modulated_input(i, x_ref, ctx_ref, mod_ref, bmod_ref, ng_ref)
        hb = h.astype(BF16)
        cos_t, slo_t, shi_t = cos_ref[...], slo_ref[...], shi_ref[...]

        cq = cq_ref[...]
        rq = lax.rsqrt(jnp.mean(cq * cq, axis=-1, keepdims=True) + NORM_EPS)
        qhat = cq * rq
        qnb = (qhat * qlg_ref[...]).astype(BF16)
        q = _dot_nt(qnb, wuq_ref[...])
        qg = qng_ref[...]
        dq_heads = []
        dqng = jnp.zeros((1, D_HEAD_PAD), F32)
        for hd in range(N_HEADS):
            qh = q[:, hd * D_HEAD_PAD:(hd + 1) * D_HEAD_PAD]
            rr = lax.rsqrt(jnp.sum(qh * qh, axis=-1, keepdims=True) * (1.0 / D_HEAD) + NORM_EPS)
            yq = qh * rr
            dpost = jnp.where(is_lat, dq_ref[hd] * ATTN_SCALE, 0.0)
            dyn = jnp.concatenate([dpost[:, 0:D_NOPE], _rope_t(dpost[:, D_NOPE:], cos_t, slo_t, shi_t)], axis=1)
            dqng = dqng + jnp.sum(dyn * yq, axis=0, keepdims=True)
            dyg = dyn * qg
            dq_heads.append(rr * (dyg - yq * (jnp.sum(dyg * yq, axis=-1, keepdims=True) * (1.0 / D_HEAD))))
        dqng_ref[...] += dqng
        dqf = jnp.concatenate(dq_heads, axis=1).astype(BF16)
        gwuq_ref[...] += _dot_tn(dqf, qnb)
        dqn = _dot(dqf, wuq_ref[...])
        dqlg_ref[...] += jnp.sum(dqn * qhat, axis=0, keepdims=True)
        dqhat = dqn * qlg_ref[...]
        dcq = rq * (dqhat - qhat * jnp.mean(dqhat * qhat, axis=-1, keepdims=True))

        ckv = ckv_ref[...]
        rkv = lax.rsqrt(jnp.mean(ckv * ckv, axis=-1, keepdims=True) + NORM_EPS)
        kvhat = ckv * rkv
        kvnb = (kvhat * kvlg_ref[...]).astype(BF16)
        kv = _dot(kvnb, wukv_ref[...])
        krz = kr_ref[...]
        kr_ss = jnp.sum(krz * krz, axis=-1, keepdims=True)
        kg = kng_ref[...]
        kg_n, kg_r = kg[:, 0:D_NOPE], kg[:, D_NOPE:]
        dkv_parts = []
        dkr = jnp.zeros((TILE, 128), F32)
        dkng_n = jnp.zeros((1, D_NOPE), F32)
        dkng_r = jnp.zeros((1, 128), F32)
        for hd in range(N_HEADS):
            kn = kv[:, hd * 256:hd * 256 + D_NOPE]
            rr = lax.rsqrt((jnp.sum(kn * kn, axis=-1, keepdims=True) + kr_ss) * (1.0 / D_HEAD) + NORM_EPS)
            yn, yr = kn * rr, krz * rr
            dk_h = jnp.transpose(dk_ref[hd]) * LN_2
            dpn = dk_h[:, 0:D_NOPE]
            dpr = _rope_t(dk_h[:, D_NOPE:], cos_t, slo_t, shi_t)
            dkng_n = dkng_n + jnp.sum(dpn * yn, axis=0, keepdims=True)
            dkng_r = dkng_r + jnp.sum(dpr * yr, axis=0, keepdims=True)
            dyn, dyr = dpn * kg_n, dpr * kg_r
            proj = (jnp.sum(dyn * yn, axis=-1, keepdims=True) + jnp.sum(dyr * yr, axis=-1, keepdims=True)) * (1.0 / D_HEAD)
            dkv_parts.append(rr * (dyn - yn * proj))
            dkv_parts.append(jnp.transpose(dv_ref[hd]))
            dkr = dkr + rr * (dyr - yr * proj)
        dkng_ref[:, 0:D_NOPE] += dkng_n
        dkng_ref[:, D_NOPE:] += dkng_r
        dkvf = jnp.concatenate(dkv_parts, axis=1).astype(BF16)
        gwukv_ref[...] += _dot_tn(dkvf, kvnb)
        dkvn = _dot_nt(dkvf, wukv_ref[...])
        dkvlg_ref[...] += jnp.sum(dkvn * kvhat, axis=0, keepdims=True)
        dkvhat = dkvn * kvlg_ref[...]
        dckv = rkv * (dkvhat - kvhat * jnp.mean(dkvhat * kvhat, axis=-1, keepdims=True))

        lat_t = jnp.maximum(i - 1, 0)
        dpl_t = dpl_ref[...]
        ds = jnp.concatenate([jnp.where(i > 1, hb_ref[...], 0.0), dpl_t,
                              jnp.where(i < n_tiles - 1, ha_ref[...], 0.0)], axis=0)
        tpos = lat_t * TILE - HALO + lax.broadcasted_iota(jnp.int32, (TILE + 2 * HALO, 1), 0)
        for g, w in enumerate(POOL_WINDOWS):
            sl = slice(g * GROUP_DIM, (g + 1) * GROUP_DIM)
            wsum = _window_sum(ds[:, sl] * _inv_count(tpos, w, seq), w, True)[HALO:HALO + TILE]
            du_ref[:, O_PIN + g * GROUP_DIM:O_PIN + (g + 1) * GROUP_DIM] = jnp.where(
                is_lat, wsum - dpl_t[:, sl], 0.0).astype(BF16)

        du_ref[:, O_GA:O_GA + D_ATTN] = jnp.where(is_lat, dga_ref[...], 0.0).astype(BF16)
        du_ref[:, O_GP:O_GP + D_POOL] = jnp.where(is_lat, dgp_ref[...], 0.0).astype(BF16)
        du_ref[:, O_CQ:O_CQ + R_Q] = dcq.astype(BF16)
        du_ref[:, O_CKV:O_CKV + R_KV] = dckv.astype(BF16)
        du_ref[:, O_KR:U_PAD] = dkr.astype(BF16)
        dub = du_ref[...]
        dh = _dot(dub, win_ref[...])
        gwin_ref[...] += _dot_tn(dub, hb)

        dshift = jnp.sum(dh, axis=0, keepdims=True)
        dscale = jnp.sum(dh * xg, axis=0, keepdims=True)
        zero = jnp.zeros_like(dshift)
        dmod_ref[0:1, :] += jnp.where(is_lat, dshift, zero)
        dmod_ref[1:2, :] += jnp.where(is_lat, dscale, zero)
        dmod_ref[2:3, :] += jnp.where(is_lat, zero, dshift)
        dmod_ref[3:4, :] += jnp.where(is_lat, zero, dscale)
        dxg = dh * (1.0 + scale)
        xr = xt * r
        dng_ref[...] += jnp.sum(dxg * xr, axis=0, keepdims=True)
        dxn = dxg * ng_ref[...]
        gx_ref[...] = g1_ref[...] + r * (dxn - xr * jnp.mean(dxn * xr, axis=-1, keepdims=True))

    lat = lambda i: (jnp.maximum(i - 1, 0), 0)
    full = lambda shape: pl.BlockSpec(shape, lambda i: (0,) * len(shape))
    rope_spec = pl.BlockSpec((TILE, 128), lambda i: (i, 0))
    return pl.pallas_call(
        body, name="inproj_bwd", grid=(n_tiles,),
        out_shape=(jax.ShapeDtypeStruct((seq, D_MODEL), F32), jax.ShapeDtypeStruct((U_PAD, D_MODEL), F32),
                   jax.ShapeDtypeStruct((N_HEADS * D_HEAD_PAD, R_Q), F32), jax.ShapeDtypeStruct((N_HEADS * 256, R_KV), F32),
                   jax.ShapeDtypeStruct((1, R_Q), F32), jax.ShapeDtypeStruct((1, R_KV), F32),
                   jax.ShapeDtypeStruct((1, D_HEAD_PAD), F32), jax.ShapeDtypeStruct((1, D_HEAD_PAD), F32),
                   jax.ShapeDtypeStruct((1, D_MODEL), F32), jax.ShapeDtypeStruct((8, D_MODEL), F32)),
        in_specs=[pl.BlockSpec((TILE, D_MODEL), lat), full((TILE, D_MODEL)), full((8, D_MODEL)), full((8, D_MODEL)),
                  full((1, D_MODEL)), full((U_PAD, D_MODEL)), full((1, R_Q)), full((N_HEADS * D_HEAD_PAD, R_Q)),
                  full((1, R_KV)), full((R_KV, N_HEADS * 256)), full((1, D_HEAD_PAD)), full((1, D_HEAD_PAD)),
                  rope_spec, rope_spec, rope_spec,
                  pl.BlockSpec((TILE, R_Q), lambda i: (i, O_CQ // R_Q)),
                  pl.BlockSpec((TILE, R_KV), lambda i: (i, O_CKV // R_KV)),
                  pl.BlockSpec((TILE, 128), lambda i: (i, O_KR // 128)),
                  pl.BlockSpec((N_HEADS, TILE, D_HEAD_PAD), lambda i: (0, jnp.maximum(i - 1, 0), 0)),
                  pl.BlockSpec((N_HEADS, D_HEAD_PAD, TILE), lambda i: (0, 0, i)),
                  pl.BlockSpec((N_HEADS, D_V, TILE), lambda i: (0, 0, i)),
                  pl.BlockSpec((TILE, D_ATTN), lat), pl.BlockSpec((TILE, D_POOL), lat), pl.BlockSpec((TILE, D_POOL), lat),
                  pl.BlockSpec((HALO, D_POOL), lambda i: (jnp.maximum((i - 1) * rows8 - 1, 0), 0)),
                  pl.BlockSpec((HALO, D_POOL), lambda i: (jnp.minimum(jnp.maximum(i, 1) * rows8, last8), 0)),
                  pl.BlockSpec((TILE, D_MODEL), lat)],
        out_specs=(pl.BlockSpec((TILE, D_MODEL), lat), full((U_PAD, D_MODEL)), full((N_HEADS * D_HEAD_PAD, R_Q)),
                   full((N_HEADS * 256, R_KV)), full((1, R_Q)), full((1, R_KV)), full((1, D_HEAD_PAD)),
                   full((1, D_HEAD_PAD)), full((1, D_MODEL)), full((8, D_MODEL))),
        scratch_shapes=[pltpu.VMEM((TILE, U_PAD), BF16)],
        compiler_params=pltpu.CompilerParams(dimension_semantics=("arbitrary",), vmem_limit_bytes=VMEM_LIMIT),
    )(x, ctx, modrows, bmodrows, norm_g, w_in_p, q_lora_g, w_uq_p, kv_lora_g, w_ukv, qng_p, kng_p, cos, slo, shi,
      u, u, u, dq, dk, dv, dga, dgp, dpl, dpl, dpl, g1)


SMALL_LAYOUT = ((0, D_MODEL), (1, R_Q), (2, R_KV), (3, D_HEAD_PAD), (4, D_HEAD_PAD), (5, D_POOL))
ROW_DMOD_B, ROW_DMOD_C, ROW_LOSS = 6, 9, 12


def _epilogue(parts, smalls, dmod4, dgate, loss_acc, w_mod_l):
    n_a, n_s = len(parts), len(smalls)
    n_mod = w_mod_l.shape[1]
    blk = [p.shape[1:] for p in parts]
    small_out = [D_MODEL, R_Q, R_KV, D_HEAD, D_HEAD, D_POOL]

    def body(*refs):
        part_refs = refs[0:n_a]
        small_in = refs[n_a:n_a + n_s]
        dmod4_ref, dgate_ref, loss_ref, wmod_ref = refs[n_a + n_s:n_a + n_s + 4]
        outs = refs[n_a + n_s + 4:]
        red_refs = outs[0:n_a]
        lastg_ref, ccg_ref = outs[n_a], outs[n_a + 1]
        sum_refs = outs[n_a + 2:n_a + 2 + n_s]
        gbmod_ref, dmy_ref, lossout_ref = outs[n_a + 2 + n_s:n_a + 5 + n_s]
        scr = outs[n_a + 5 + n_s:]
        recv1, sum1b, recv2 = scr[0:n_a], scr[n_a:2 * n_a], scr[2 * n_a:3 * n_a]
        small_ref, smallg_ref, tot_ref, cc_ref = scr[3 * n_a:3 * n_a + 4]
        ssem1, rsem1, ssem2, rsem2, ssem_s, rsem_s, ssem_l, rsem_l, ssem_cc, rsem_cc = scr[3 * n_a + 4:]
        x, y, c = _coords()
        me = (x, y, c)
        sibling = (x, y, 1 - c)

        stage1 = []
        for a in range(n_a):
            for b in range(4):
                stage1.append(pltpu.make_async_remote_copy(
                    src_ref=part_refs[a].at[4 * (b >> 1) + 2 * (b & 1) + (1 - c)], dst_ref=recv1[a].at[b],
                    send_sem=ssem1.at[a, b], recv_sem=rsem1.at[a, b], device_id=sibling, device_id_type=MESH))
        for cp in stage1:
            cp.start()

        small_ref[...] = jnp.zeros_like(small_ref)
        for ref, (row, width) in zip(small_in, SMALL_LAYOUT):
            small_ref[row:row + 1, 0:width] = ref[...]
        small_ref[ROW_DMOD_B:ROW_DMOD_B + 2, :] = dmod4_ref[0:2, :]
        small_ref[ROW_DMOD_B + 2:ROW_DMOD_B + 3, :] = dgate_ref[...]
        small_ref[ROW_DMOD_C:ROW_DMOD_C + 2, :] = dmod4_ref[2:4, :]
        small_ref[ROW_LOSS:ROW_LOSS + 1, 0:128] = loss_ref[0:1, :]
        smallg_ref[_lin(me)] = small_ref[...]
        s_recv, s_send = _gather_to_all(small_ref, smallg_ref, ssem_s, rsem_s)
        s_recv()
        tot = smallg_ref[0]
        for d in range(1, N_DEV):
            tot = tot + smallg_ref[d]
        tot_ref[...] = tot
        for ref, (row, _), width in zip(sum_refs, SMALL_LAYOUT, small_out):
            ref[...] = tot_ref[row:row + 1, 0:width]
        lossout_ref[...] = tot_ref[8:16, 0:128]
        lin = _lin(me)

        def my_columns(three_rows):
            v = jnp.concatenate(three_rows, axis=1)
            out = jnp.zeros((1, n_mod), F32)
            for d in range(N_DEV):
                out = out + jnp.where(lin == d, v[:, d * n_mod:(d + 1) * n_mod], 0.0)
            return out

        rows3 = lambda ref, r0: [ref[r0 + t:r0 + t + 1, :] for t in range(3)]
        dmodc = rows3(tot_ref, ROW_DMOD_C)
        gbmod_ref[...] = jnp.concatenate(rows3(tot_ref, ROW_DMOD_B), axis=1) + jnp.concatenate(dmodc, axis=1)
        dmy_ref[...] = jnp.zeros_like(dmy_ref)
        for b in range(N_DEV):
            dmy_ref[b:b + 1, :] = my_columns(rows3(smallg_ref.at[b], ROW_DMOD_B))
        dmy = my_columns(dmodc)
        dmy_ref[N_DEV:N_DEV + 1, :] = dmy
        part = lax.dot_general(jnp.broadcast_to(dmy, (8, n_mod)), wmod_ref[...], (((1,), (1,)), ((), ())),
                               precision=HIGHEST, preferred_element_type=F32)

        cc_ref[...] = part
        ccg_ref[_lin(me)] = part
        cc_recv, cc_send = _gather_to_all(cc_ref, ccg_ref, ssem_cc, rsem_cc)

        for a in range(n_a):
            for b in range(4):
                pltpu.make_async_remote_copy(
                    src_ref=part_refs[a].at[0], dst_ref=recv1[a].at[b], send_sem=ssem1.at[a, b], recv_sem=rsem1.at[a, b],
                    device_id=sibling, device_id_type=MESH).wait_recv()
                sum1b[a][b] = (part_refs[a][4 * (b >> 1) + 2 * (b & 1) + c] + recv1[a][b]).astype(BF16)

        stage2 = []
        for a in range(n_a):
            for k in (1, 2, 3):
                tx, ty = _flip(x, (k >> 1) & 1), _flip(y, k & 1)
                stage2.append(pltpu.make_async_remote_copy(
                    src_ref=sum1b[a].at[2 * tx + ty], dst_ref=recv2[a].at[k - 1], send_sem=ssem2.at[a, k - 1],
                    recv_sem=rsem2.at[a, k - 1], device_id=(tx, ty, c), device_id_type=MESH))
        for cp in stage2:
            cp.start()
        for a in range(n_a):
            own = part_refs[a][_lin(me)] + recv1[a][2 * x + y]
            for k in (1, 2, 3):
                stage2[3 * a + k - 1].wait_recv()
                own = own + recv2[a][k - 1].astype(F32)
            red_refs[a][...] = own

        lastg_ref[_lin(me)] = red_refs[n_a - 1][...]
        l_recv, l_send = _gather_to_all(red_refs[n_a - 1], lastg_ref, ssem_l, rsem_l)
        l_recv()
        cc_recv()
        for cp in stage1 + stage2:
            cp.wait_send()
        s_send()
        cc_send()
        l_send()

    vm = pl.BlockSpec(memory_space=pltpu.VMEM)
    sds = jax.ShapeDtypeStruct
    return pl.pallas_call(
        body, name="epilogue_reduce",
        out_shape=(*[sds(s, F32) for s in blk], sds((N_DEV,) + blk[-1], F32), sds((N_DEV, 8, D_MODEL), F32),
                   *[sds((1, w), F32) for w in small_out], sds((1, 3 * D_MODEL), F32), sds((16, n_mod), F32),
                   sds((8, 128), F32)),
        in_specs=[vm] * (n_a + n_s + 4), out_specs=tuple([vm] * (n_a + n_s + 5)),
        scratch_shapes=[*[pltpu.VMEM((4,) + s, F32) for s in blk], *[pltpu.VMEM((4,) + s, BF16) for s in blk],
                        *[pltpu.VMEM((3,) + s, BF16) for s in blk],
                        pltpu.VMEM((SMALL_ROWS, D_MODEL), F32), pltpu.VMEM((N_DEV, SMALL_ROWS, D_MODEL), F32),
                        pltpu.VMEM((SMALL_ROWS, D_MODEL), F32), pltpu.VMEM((8, D_MODEL), F32),
                        pltpu.SemaphoreType.DMA((n_a, 4)), pltpu.SemaphoreType.DMA((n_a, 4)),
                        pltpu.SemaphoreType.DMA((n_a, 3)), pltpu.SemaphoreType.DMA((n_a, 3)),
                        pltpu.SemaphoreType.DMA((7,)), pltpu.SemaphoreType.DMA((7,)),
                        pltpu.SemaphoreType.DMA((7,)), pltpu.SemaphoreType.DMA((7,)),
                        pltpu.SemaphoreType.DMA((7,)), pltpu.SemaphoreType.DMA((7,))],
        compiler_params=pltpu.CompilerParams(vmem_limit_bytes=VMEM_LIMIT),
    )(*parts, *smalls, dmod4, dgate, loss_acc, w_mod_l)


def _adamw(weights, grads, ms, vs, c_all_t, dmod_my, p2g):
    n = len(weights)

    def body(*refs):
        w_refs = refs[0:n]
        g_in = refs[n:2 * n - 2]
        m_refs = refs[2 * n - 2:3 * n - 2]
        v_refs = refs[3 * n - 2:4 * n - 2]
        cat_ref, dmod_ref, p2g_ref = refs[4 * n - 2:4 * n + 1]
        outs = refs[4 * n + 1:]
        gcc_ref, gwm_ref = outs[0], outs[1]
        d_refs, nm_refs, nv_refs = outs[2:2 + n], outs[2 + n:2 + 2 * n], outs[2 + 2 * n:2 + 3 * n]

        part = p2g_ref[0, 0:1, :]
        for d in range(1, N_DEV):
            part = part + p2g_ref[d, 0:1, :]
        cc = w_refs[0][...]
        sg = _sigmoid(cc)
        gcc_ref[...] = part * (sg * (1.0 + cc * (1.0 - sg)))
        cat = cat_ref[...]
        gwm_ref[...] = lax.dot_general(cat * _sigmoid(cat), dmod_ref[...], (((1,), (0,)), ((), ())), precision=HIGHEST,
                                       preferred_element_type=F32)
        for idx in range(n):
            g = (gcc_ref, gwm_ref)[idx][...] if idx < 2 else g_in[idx - 2][...]
            m = ADAM_B1 * m_refs[idx][...] + (1.0 - ADAM_B1) * g
            v = ADAM_B2 * v_refs[idx][...] + (1.0 - ADAM_B2) * (g * g)
            m_hat = m / (1.0 - ADAM_B1 ** ADAM_STEP)
            v_hat = v / (1.0 - ADAM_B2 ** ADAM_STEP)
            d_refs[idx][...] = -ADAM_LR * (m_hat / (jnp.sqrt(v_hat) + ADAM_EPS) + ADAM_WD * w_refs[idx][...])
            nm_refs[idx][...] = m
            nv_refs[idx][...] = v

    vm = pl.BlockSpec(memory_space=pltpu.VMEM)
    shapes = [jax.ShapeDtypeStruct(w.shape, F32) for w in weights]
    args = list(weights) + list(grads[2:]) + list(ms) + list(vs) + [c_all_t, dmod_my, p2g]
    out_shape = tuple(shapes[0:2] + shapes * 3)
    return pl.pallas_call(
        body, name="adamw_update", out_shape=out_shape, in_specs=[vm] * len(args), out_specs=tuple([vm] * len(out_shape)),
        compiler_params=pltpu.CompilerParams(vmem_limit_bytes=VMEM_LIMIT),
    )(*args)


def _rope_tables(seq):
    rows = seq // GRID_W
    row = np.repeat(np.arange(rows, dtype=np.float32), GRID_W)
    col = np.tile(np.arange(GRID_W, dtype=np.float32), rows)
    n_freq = D_ROPE // 4
    inv = (np.float32(ROPE_BASE) ** (-np.arange(n_freq, dtype=np.float32) / np.float32(n_freq))).astype(np.float32)
    ang_r = (row[:, None] * inv).astype(np.float32)
    ang_c = (col[:, None] * inv).astype(np.float32)
    ang = np.concatenate([ang_r, ang_r, ang_c, ang_c], axis=-1)
    cos, sin = np.cos(ang).astype(np.float32), np.sin(ang).astype(np.float32)
    low = (np.arange(D_ROPE) % 32) < 16

    def table(t, fill):
        out = np.full((TILE + seq, 128), fill, np.float32)
        out[TILE:, 0:D_ROPE] = t
        return jnp.asarray(out)

    return table(cos, 1.0), table(np.where(low, -sin, 0.0), 0.0), table(np.where(low, 0.0, sin), 0.0)


def kernel(x, c, ctx, c_ctx, w_mod, b_mod, norm_g, w_in, q_lora_g, w_uq, kv_lora_g, w_ukv, q_norm_g, k_norm_g, w_pool, pool_scale, w_out, loss_target, m_c_ctx, m_w_mod, m_b_mod, m_norm_g, m_w_in, m_q_lora_g, m_w_uq, m_kv_lora_g, m_w_ukv, m_q_norm_g, m_k_norm_g, m_w_pool, m_pool_scale, m_w_out, v_c_ctx, v_w_mod, v_b_mod, v_norm_g, v_w_in, v_q_lora_g, v_w_uq, v_kv_lora_g, v_w_ukv, v_q_norm_g, v_k_norm_g, v_w_pool, v_pool_scale, v_w_out):
    seq = x.shape[1]
    assert ctx.shape[1] == TILE and seq % TILE == 0 and seq % GRID_W == 0
    ix, iy, ic = lax.axis_index("x"), lax.axis_index("y"), lax.axis_index("c")
    me = 4 * ix + 2 * iy + ic
    x2, ctx2, tgt2 = x[0], ctx[0], loss_target[0]
    w_mod_l, w_ukv_l, w_out_l = w_mod[0], w_ukv[0], w_out[0]
    c_ctx_row = c_ctx.reshape(1, D_MODEL)

    tr = lambda a: jnp.transpose(a[0])
    w_in_tl, w_uq_tl = tr(w_in), tr(w_uq)
    cg, modg, wg_in, wg_uq, wg_ukv, wg_out = _prologue(
        jnp.broadcast_to(c, (8, D_MODEL)), jnp.broadcast_to(c_ctx_row, (8, D_MODEL)), w_mod_l,
        [w_in_tl, w_uq_tl, w_ukv_l, w_out_l])
    mod_all = jnp.transpose(modg, (1, 0, 2)).reshape(16, 3 * D_MODEL)
    mod_b = lax.dynamic_slice_in_dim(mod_all, me, 1, axis=0).reshape(3, D_MODEL)
    mod_c = mod_all[8].reshape(3, D_MODEL)
    modrows = jnp.concatenate([mod_b, mod_c[0:2], jnp.zeros((3, D_MODEL), F32)], axis=0)
    b3 = b_mod.reshape(3, D_MODEL)
    bmodrows = jnp.concatenate([b3, b3[0:2], jnp.zeros((3, D_MODEL), F32)], axis=0)

    w_in_t = wg_in.reshape(D_IN_PROJ, D_MODEL)
    w_in_p = jnp.concatenate([w_in_t[448:], w_in_t[0:384], w_in_t[384:448],
                              jnp.zeros((U_PAD - D_IN_PROJ, D_MODEL), BF16)], axis=0)
    w_uq_p = jnp.concatenate([wg_uq.reshape(N_HEADS, D_HEAD, R_Q), jnp.zeros((N_HEADS, D_HEAD_PAD - D_HEAD, R_Q), BF16)],
                             axis=1).reshape(N_HEADS * D_HEAD_PAD, R_Q)
    w_ukv_f = jnp.transpose(wg_ukv, (1, 0, 2)).reshape(R_KV, N_HEADS * 256)
    w_out_f = wg_out.reshape(D_MODEL, D_MODEL)
    qng_p = jnp.concatenate([q_norm_g, jnp.zeros((1, D_HEAD_PAD - D_HEAD), F32)], axis=1)
    kng_p = jnp.concatenate([k_norm_g, jnp.zeros((1, D_HEAD_PAD - D_HEAD), F32)], axis=1)
    cos, slo, shi = _rope_tables(seq)

    u, q, k, v = _inproj_fwd(x2, ctx2, modrows, bmodrows, norm_g, w_in_p, q_lora_g, w_uq_p, kv_lora_g, w_ukv_f,
                             qng_p, kng_p, cos, slo, shi)
    attn, lse = _attn_fwd(q, k, v, min(1024, seq))
    (loss_acc, g1, dgate, d_attn, dga, dgp, dpl, gwo, gwp, dps) = _mix(
        x2, tgt2, attn, u, modrows, bmodrows, w_pool[0], pool_scale, w_out_f)

    dq, dk, dv = _attn_bwd(q, k, v, attn, lse, d_attn, 256)
    (grad_x, gwin_p, gwuq_p, gwukv_t, dqlg, dkvlg, dqng, dkng, dng, dmod4) = _inproj_bwd(
        x2, ctx2, modrows, bmodrows, norm_g, w_in_p, q_lora_g, w_uq_p, kv_lora_g, w_ukv_f, qng_p, kng_p, cos, slo, shi,
        u, dq, dk, dv, dga, dgp, dpl, g1)

    gwin_t = jnp.concatenate([gwin_p[O_CQ:O_KR], gwin_p[O_KR:D_IN_PROJ], gwin_p[0:O_CQ]], axis=0)
    gwuq_t = gwuq_p.reshape(N_HEADS, D_HEAD_PAD, R_Q)[:, 0:D_HEAD].reshape(N_HEADS * D_HEAD, R_Q)
    blocks = lambda a: a.reshape((N_DEV, a.shape[0] // N_DEV) + a.shape[1:])
    parts = [blocks(gwin_t), blocks(gwuq_t), blocks(gwukv_t), blocks(gwo), blocks(gwp.reshape(4 * GROUP_DIM, GROUP_DIM))]
    (r_win, r_wuq, r_wukv, r_wout, _, wpool_g, ccg, g_ng, g_qlg, g_kvlg, g_qng, g_kng, g_ps, g_bmod, dmod_my,
     loss_rows) = _epilogue(parts, [dng, dqlg, dkvlg, dqng, dkng, dps], dmod4, dgate, loss_acc, w_mod_l)

    g_w_ukv = jnp.transpose(r_wukv)
    g_w_pool = wpool_g.reshape(4 * GROUP_DIM, GROUP_DIM)
    c_rows = cg[:, 0, :]
    c_all_t = jnp.transpose(jnp.concatenate([c_rows, c_ctx_row, jnp.zeros((16 - N_DEV - 1, D_MODEL), F32)], axis=0))

    weights = [c_ctx_row, w_mod_l, b_mod, norm_g, w_in_tl, q_lora_g, w_uq_tl, kv_lora_g, w_ukv_l, q_norm_g, k_norm_g,
               w_pool.reshape(4 * GROUP_DIM, GROUP_DIM), pool_scale, w_out_l]
    grads = [None, None, g_bmod, g_ng, r_win, g_qlg, r_wuq, g_kvlg, g_w_ukv, g_qng, g_kng, g_w_pool, g_ps, r_wout]
    ms = [m_c_ctx.reshape(1, D_MODEL), m_w_mod[0], m_b_mod, m_norm_g, tr(m_w_in), m_q_lora_g, tr(m_w_uq), m_kv_lora_g,
          m_w_ukv[0], m_q_norm_g, m_k_norm_g, m_w_pool.reshape(4 * GROUP_DIM, GROUP_DIM), m_pool_scale, m_w_out[0]]
    vs = [v_c_ctx.reshape(1, D_MODEL), v_w_mod[0], v_b_mod, v_norm_g, tr(v_w_in), v_q_lora_g, tr(v_w_uq), v_kv_lora_g,
          v_w_ukv[0], v_q_norm_g, v_k_norm_g, v_w_pool.reshape(4 * GROUP_DIM, GROUP_DIM), v_pool_scale, v_w_out[0]]
    outs = _adamw(weights, grads, ms, vs, c_all_t, dmod_my, ccg)
    grads[0], grads[1] = outs[0], outs[1]
    n = len(weights)
    deltas, new_m, new_v = outs[2:2 + n], outs[2 + n:2 + 2 * n], outs[2 + 2 * n:2 + 3 * n]

    loss = loss_rows[ROW_LOSS - 8, 0]
    final = [c_ctx.shape, w_mod.shape, b_mod.shape, norm_g.shape, w_in.shape, q_lora_g.shape, w_uq.shape, kv_lora_g.shape,
             w_ukv.shape, q_norm_g.shape, k_norm_g.shape, w_pool.shape, pool_scale.shape, w_out.shape]
    transposed = (4, 6)

    def shaped(arrs):
        return [(jnp.transpose(a) if i in transposed else a).reshape(s) for i, (a, s) in enumerate(zip(arrs, final))]

    return (loss, grad_x[None], *shaped(grads), *shaped(deltas), *shaped(new_m), *shaped(new_v))
```

```python
import numpy as np

import jax
import jax.numpy as jnp
from jax import lax
from jax.experimental import pallas as pl
from jax.experimental.pallas import tpu as pltpu

F32 = jnp.float32
BF16 = jnp.bfloat16
MESH = pl.DeviceIdType.MESH
HIGHEST = lax.Precision.HIGHEST

D_MODEL = 1024
N_HEADS = 4
D_NOPE = 128
D_ROPE = 64
D_HEAD = D_NOPE + D_ROPE
D_HEAD_PAD = 256
D_V = 128
R_Q = 256
R_KV = 128
D_ATTN = 512
D_POOL = 512
POOL_WINDOWS = (2, 4, 8, 16)
GROUP_DIM = 128
GRID_W = 64
ROPE_BASE = 10000.0
NORM_EPS = 1e-6
ATTN_SCALE = D_HEAD ** -0.5
LOG2_E = 1.4426950408889634
LN_2 = 0.6931471805599453
ATTN_ROWS = 256
D_IN_PROJ = 1984
U_PAD = 2048
O_GA, O_PIN, O_GP, O_CQ, O_CKV, O_KR = 0, 512, 1024, 1536, 1792, 1920
TILE = 256
Q_BLOCK = 128
HALO = 8
N_DEV = 8
VMEM_LIMIT = 56 * 1024 * 1024

ADAM_LR = 0.001
ADAM_B1 = 0.9
ADAM_B2 = 0.999
ADAM_EPS = 1e-08
ADAM_WD = 0.01
ADAM_STEP = 10

SMALL_ROWS = 16


def _dot(a, b):
    return lax.dot_general(a, b, (((1,), (0,)), ((), ())), preferred_element_type=F32)


def _dot_nt(a, b):
    return lax.dot_general(a, b, (((1,), (1,)), ((), ())), preferred_element_type=F32)


def _dot_tn(a, b):
    return lax.dot_general(a, b, (((0,), (0,)), ((), ())), preferred_element_type=F32)


def _sigmoid(x):
    return 1.0 / (1.0 + jnp.exp(-x))


def _rope(p, cos, slo, shi):
    return p * cos + pltpu.roll(p, 112, 1) * slo + pltpu.roll(p, 16, 1) * shi


def _rope_t(d, cos, slo, shi):
    return d * cos + pltpu.roll(d * slo, 16, 1) + pltpu.roll(d * shi, 112, 1)


def _shift_rows(a, k):
    n = a.shape[0]
    return pltpu.roll(a, (-k) % n, 0)


def _window_sum(x, w, transposed):
    s = (x + _shift_rows(x, 1)) if transposed else (_shift_rows(x, -1) + x)
    step = 1
    while 2 * step < w:
        s = _shift_rows(s, -step) + _shift_rows(s, step)
        step *= 2
    return s


def _inv_count(tpos, w, seq):
    lo = jnp.maximum(tpos - w // 2, 0)
    hi = jnp.minimum(tpos - w // 2 + w, seq)
    return 1.0 / jnp.maximum(hi - lo, 1).astype(F32)


def _coords():
    return lax.axis_index("x"), lax.axis_index("y"), lax.axis_index("c")


def _flip(v, bit):
    return (1 - v) if bit else v


def _peer(k):
    x, y, c = _coords()
    return (_flip(x, (k >> 2) & 1), _flip(y, (k >> 1) & 1), _flip(c, k & 1))


def _lin(p):
    return 4 * p[0] + 2 * p[1] + p[2]


def _gather_to_all(src_ref, slots_ref, send_sems, recv_sems):
    me = _coords()
    copies = []
    for k in range(1, N_DEV):
        cp = pltpu.make_async_remote_copy(
            src_ref=src_ref, dst_ref=slots_ref.at[_lin(me)], send_sem=send_sems.at[k - 1], recv_sem=recv_sems.at[k - 1],
            device_id=_peer(k), device_id_type=MESH)
        cp.start()
        copies.append(cp)

    def wait_recv():
        for k in range(1, N_DEV):
            pltpu.make_async_remote_copy(
                src_ref=src_ref, dst_ref=slots_ref.at[_lin(_peer(k))], send_sem=send_sems.at[k - 1],
                recv_sem=recv_sems.at[k - 1], device_id=_peer(k), device_id_type=MESH).wait_recv()

    def wait_send():
        for cp in copies:
            cp.wait_send()

    return wait_recv, wait_send


def _prologue(c8, cctx8, w_mod_l, shards):
    n_mod = w_mod_l.shape[1]
    n_w = len(shards)

    def body(c8_ref, cctx_ref, wmod_ref, *refs):
        w_refs = refs[0:n_w]
        cg_ref, modg_ref = refs[n_w], refs[n_w + 1]
        wg_refs = refs[n_w + 2:2 * n_w + 2]
        modblk_ref = refs[2 * n_w + 2]
        wb_refs = refs[2 * n_w + 3:3 * n_w + 3]
        ssem_w, rsem_w, ssem_c, rsem_c, ssem_m, rsem_m = refs[3 * n_w + 3:]
        x, y, c = _coords()
        me = (x, y, c)
        sibling = (x, y, 1 - c)
        chips = [(1 - x, y), (x, 1 - y), (1 - x, 1 - y)]

        def wcopies(k, block, to, own=False):
            out = []
            for a in range(n_w):
                slot = wg_refs[a].at[_lin(block)]
                out.append(pltpu.make_async_remote_copy(
                    src_ref=wb_refs[a] if own else slot, dst_ref=slot, send_sem=ssem_w.at[a, k], recv_sem=rsem_w.at[a, k],
                    device_id=to, device_id_type=MESH))
            return out

        for a in range(n_w):
            wb_refs[a][...] = w_refs[a][...].astype(BF16)
        first = wcopies(0, me, sibling, own=True)
        for j, chip in enumerate(chips):
            first += wcopies(1 + j, me, (*chip, c), own=True)
        for cp in first:
            cp.start()
        for a in range(n_w):
            wg_refs[a][_lin(me)] = wb_refs[a][...]

        cg_ref[_lin(me)] = c8_ref[...]
        c_recv, c_send = _gather_to_all(c8_ref, cg_ref, ssem_c, rsem_c)
        c_recv()
        row = lax.broadcasted_iota(jnp.int32, (8, D_MODEL), 0)
        c_all = jnp.zeros((8, D_MODEL), F32)
        for d in range(N_DEV):
            c_all = c_all + jnp.where(row == d, cg_ref[d], 0.0)
        cc = cctx_ref[...]
        a = jnp.concatenate([c_all * _sigmoid(c_all), cc * _sigmoid(cc)], axis=0)
        modblk_ref[...] = lax.dot_general(a, wmod_ref[...], (((1,), (0,)), ((), ())), precision=HIGHEST,
                                          preferred_element_type=F32)
        modg_ref[_lin(me)] = modblk_ref[...]
        m_recv, m_send = _gather_to_all(modblk_ref, modg_ref, ssem_m, rsem_m)
        m_recv()

        passed = []
        for j, chip in enumerate(chips):
            for cp in wcopies(1 + j, (*chip, c), me):
                cp.wait_recv()
            fwd = wcopies(4 + j, (*chip, c), sibling)
            for cp in fwd:
                cp.start()
            passed += fwd
        for cp in wcopies(0, sibling, me):
            cp.wait_recv()
        for j, chip in enumerate(chips):
            for cp in wcopies(4 + j, (*chip, 1 - c), me):
                cp.wait_recv()
        for cp in first + passed:
            cp.wait_send()
        c_send()
        m_send()

    vm = pl.BlockSpec(memory_space=pltpu.VMEM)
    return pl.pallas_call(
        body, name="prologue_gather",
        out_shape=(jax.ShapeDtypeStruct((N_DEV, 8, D_MODEL), F32), jax.ShapeDtypeStruct((N_DEV, 16, n_mod), F32),
                   *[jax.ShapeDtypeStruct((N_DEV,) + s.shape, BF16) for s in shards]),
        in_specs=[vm] * (3 + n_w), out_specs=tuple([vm] * (2 + n_w)),
        scratch_shapes=[pltpu.VMEM((16, n_mod), F32), *[pltpu.VMEM(s.shape, BF16) for s in shards],
                        pltpu.SemaphoreType.DMA((n_w, 7)), pltpu.SemaphoreType.DMA((n_w, 7)),
                        pltpu.SemaphoreType.DMA((7,)), pltpu.SemaphoreType.DMA((7,)),
                        pltpu.SemaphoreType.DMA((7,)), pltpu.SemaphoreType.DMA((7,))],
        compiler_params=pltpu.CompilerParams(vmem_limit_bytes=VMEM_LIMIT),
    )(c8, cctx8, w_mod_l, *shards)


def _modulated_input(i, x_ref, ctx_ref, mod_ref, bmod_ref, ng_ref):
    is_ctx = i == 0
    xt = jnp.where(is_ctx, ctx_ref[...], x_ref[...])
    shift = jnp.where(is_ctx, mod_ref[3:4, :] + bmod_ref[3:4, :], mod_ref[0:1, :] + bmod_ref[0:1, :])
    scale = jnp.where(is_ctx, mod_ref[4:5, :] + bmod_ref[4:5, :], mod_ref[1:2, :] + bmod_ref[1:2, :])
    r = lax.rsqrt(jnp.mean(xt * xt, axis=-1, keepdims=True) + NORM_EPS)
    xg = (xt * r) * ng_ref[...]
    h = xg * (1.0 + scale) + shift
    return xt, r, xg, h, scale


def _inproj_fwd(x, ctx, modrows, bmodrows, norm_g, w_in_p, q_lora_g, w_uq_p, kv_lora_g, w_ukv, qng_p, kng_p, cos, slo, shi):
    seq = x.shape[0]
    n_tiles = seq // TILE + 1
    tot = seq + TILE

    def body(x_ref, ctx_ref, mod_ref, bmod_ref, ng_ref, win_ref, qlg_ref, wuq_ref, kvlg_ref, wukv_ref, qng_ref, kng_ref,
             cos_ref, slo_ref, shi_ref, u_ref, q_ref, k_ref, v_ref):
        i = pl.program_id(0)
        _, _, _, h, _ = _modulated_input(i, x_ref, ctx_ref, mod_ref, bmod_ref, ng_ref)
        u = _dot_nt(h.astype(BF16), win_ref[...])
        u_ref[...] = u
        cos_t, slo_t, shi_t = cos_ref[...], slo_ref[...], shi_ref[...]

        cq = u[:, O_CQ:O_CQ + R_Q]
        qn = cq * lax.rsqrt(jnp.mean(cq * cq, axis=-1, keepdims=True) + NORM_EPS) * qlg_ref[...]
        q = _dot_nt(qn.astype(BF16), wuq_ref[...])
        qg = qng_ref[...] * (ATTN_SCALE * LOG2_E)
        for hd in range(N_HEADS):
            qh = q[:, hd * D_HEAD_PAD:(hd + 1) * D_HEAD_PAD]
            rr = lax.rsqrt(jnp.sum(qh * qh, axis=-1, keepdims=True) * (1.0 / D_HEAD) + NORM_EPS)
            qy = qh * rr * qg
            q_ref[hd, :, 0:D_NOPE] = qy[:, 0:D_NOPE].astype(BF16)
            q_ref[hd, :, D_NOPE:D_HEAD_PAD] = _rope(qy[:, D_NOPE:D_HEAD_PAD], cos_t, slo_t, shi_t).astype(BF16)

        ckv = u[:, O_CKV:O_CKV + R_KV]
        kvn = ckv * lax.rsqrt(jnp.mean(ckv * ckv, axis=-1, keepdims=True) + NORM_EPS) * kvlg_ref[...]
        kv = _dot(kvn.astype(BF16), wukv_ref[...])
        krz = u[:, O_KR:U_PAD]
        kr_ss = jnp.sum(krz * krz, axis=-1, keepdims=True)
        kg = kng_ref[...]
        for hd in range(N_HEADS):
            kn = kv[:, hd * 256:hd * 256 + D_NOPE]
            rr = lax.rsqrt((jnp.sum(kn * kn, axis=-1, keepdims=True) + kr_ss) * (1.0 / D_HEAD) + NORM_EPS)
            k_ref[hd, :, 0:D_NOPE] = (kn * rr * kg[:, 0:D_NOPE]).astype(BF16)
            k_ref[hd, :, D_NOPE:D_HEAD_PAD] = _rope(krz * rr * kg[:, D_NOPE:D_HEAD_PAD], cos_t, slo_t, shi_t).astype(BF16)
            v_ref[hd] = kv[:, hd * 256 + D_NOPE:(hd + 1) * 256].astype(BF16)

    lat = lambda i: (jnp.maximum(i - 1, 0), 0)
    full = lambda shape: pl.BlockSpec(shape, lambda i: (0,) * len(shape))
    return pl.pallas_call(
        body, name="inproj_fwd", grid=(n_tiles,),
        out_shape=(jax.ShapeDtypeStruct((tot, U_PAD), F32),
                   jax.ShapeDtypeStruct((N_HEADS, seq, D_HEAD_PAD), BF16),
                   jax.ShapeDtypeStruct((N_HEADS, tot, D_HEAD_PAD), BF16),
                   jax.ShapeDtypeStruct((N_HEADS, tot, D_V), BF16)),
        in_specs=[pl.BlockSpec((TILE, D_MODEL), lat), full((TILE, D_MODEL)), full((8, D_MODEL)), full((8, D_MODEL)),
                  full((1, D_MODEL)), full((U_PAD, D_MODEL)), full((1, R_Q)), full((N_HEADS * D_HEAD_PAD, R_Q)),
                  full((1, R_KV)), full((R_KV, N_HEADS * 256)), full((1, D_HEAD_PAD)), full((1, D_HEAD_PAD)),
                  pl.BlockSpec((TILE, 128), lambda i: (i, 0)), pl.BlockSpec((TILE, 128), lambda i: (i, 0)),
                  pl.BlockSpec((TILE, 128), lambda i: (i, 0))],
        out_specs=(pl.BlockSpec((TILE, U_PAD), lambda i: (i, 0)),
                   pl.BlockSpec((N_HEADS, TILE, D_HEAD_PAD), lambda i: (0, jnp.maximum(i - 1, 0), 0)),
                   pl.BlockSpec((N_HEADS, TILE, D_HEAD_PAD), lambda i: (0, i, 0)),
                   pl.BlockSpec((N_HEADS, TILE, D_V), lambda i: (0, i, 0))),
        compiler_params=pltpu.CompilerParams(dimension_semantics=("arbitrary",), vmem_limit_bytes=VMEM_LIMIT),
    )(x, ctx, modrows, bmodrows, norm_g, w_in_p, q_lora_g, w_uq_p, kv_lora_g, w_ukv, qng_p, kng_p, cos, slo, shi)


def _attn_fwd(q, k, v, block_q):
    _, seq, _ = q.shape
    n_keys = k.shape[1]

    def body(q_ref, k_ref, v_ref, o_ref, lse_ref):
        kb, vb = k_ref[0], v_ref[0]
        for r0 in range(0, block_q, ATTN_ROWS):
            rows = slice(r0, r0 + ATTN_ROWS)
            s = _dot_nt(q_ref[0, rows, :], kb)
            m = jnp.max(s, axis=-1, keepdims=True)
            p = jnp.exp2(s - m)
            l = jnp.sum(p, axis=-1, keepdims=True)
            o_ref[rows, :] = _dot(p.astype(BF16), vb) * (1.0 / l)
            lse_ref[0, rows, :] = m + jnp.log2(l)

    return pl.pallas_call(
        body, name="attn_fwd", grid=(N_HEADS, seq // block_q),
        out_shape=(jax.ShapeDtypeStruct((seq, D_ATTN), F32), jax.ShapeDtypeStruct((N_HEADS, seq, 1), F32)),
        in_specs=[pl.BlockSpec((1, block_q, D_HEAD_PAD), lambda h, i: (h, i, 0)),
                  pl.BlockSpec((1, n_keys, D_HEAD_PAD), lambda h, i: (h, 0, 0)),
                  pl.BlockSpec((1, n_keys, D_V), lambda h, i: (h, 0, 0))],
        out_specs=(pl.BlockSpec((block_q, D_V), lambda h, i: (i, h)),
                   pl.BlockSpec((1, block_q, 1), lambda h, i: (h, i, 0))),
        compiler_params=pltpu.CompilerParams(dimension_semantics=("arbitrary", "arbitrary"), vmem_limit_bytes=VMEM_LIMIT),
    )(q, k, v)


def _attn_bwd(q, k, v, o, lse, d_o, block_q):
    _, seq, _ = q.shape
    n_keys = k.shape[1]

    def body(q_ref, k_ref, v_ref, o_ref, lse_ref, do_ref, dq_ref, dkt_ref, dvt_ref):
        i = pl.program_id(1)

        @pl.when(i == 0)
        def _():
            dkt_ref[...] = jnp.zeros_like(dkt_ref)
            dvt_ref[...] = jnp.zeros_like(dvt_ref)

        kb, vb = k_ref[0], v_ref[0]
        raws, ps = [], []
        for r0 in range(0, block_q, ATTN_ROWS):
            rows = slice(r0, r0 + ATTN_ROWS)
            d_out = do_ref[rows, :]
            p = jnp.exp2(_dot_nt(q_ref[0, rows, :], kb) - lse_ref[0, rows, :])
            dp = _dot_nt(d_out.astype(BF16), vb)
            delta = jnp.sum(d_out * o_ref[rows, :], axis=-1, keepdims=True)
            raw = (p * (dp - delta)).astype(BF16)
            dq_ref[0, rows, :] = _dot(raw, kb)
            raws.append(raw)
            ps.append(p.astype(BF16))
        dkt_ref[0] += _dot_tn(q_ref[0], jnp.concatenate(raws, axis=0))
        dvt_ref[0] += _dot_tn(do_ref[...].astype(BF16), jnp.concatenate(ps, axis=0))

    return pl.pallas_call(
        body, name="attn_bwd", grid=(N_HEADS, seq // block_q),
        out_shape=(jax.ShapeDtypeStruct((N_HEADS, seq, D_HEAD_PAD), F32),
                   jax.ShapeDtypeStruct((N_HEADS, D_HEAD_PAD, n_keys), F32),
                   jax.ShapeDtypeStruct((N_HEADS, D_V, n_keys), F32)),
        in_specs=[pl.BlockSpec((1, block_q, D_HEAD_PAD), lambda h, i: (h, i, 0)),
                  pl.BlockSpec((1, n_keys, D_HEAD_PAD), lambda h, i: (h, 0, 0)),
                  pl.BlockSpec((1, n_keys, D_V), lambda h, i: (h, 0, 0)),
                  pl.BlockSpec((block_q, D_V), lambda h, i: (i, h)),
                  pl.BlockSpec((1, block_q, 1), lambda h, i: (h, i, 0)),
                  pl.BlockSpec((block_q, D_V), lambda h, i: (i, h))],
        out_specs=(pl.BlockSpec((1, block_q, D_HEAD_PAD), lambda h, i: (h, i, 0)),
                   pl.BlockSpec((1, D_HEAD_PAD, n_keys), lambda h, i: (h, 0, 0)),
                   pl.BlockSpec((1, D_V, n_keys), lambda h, i: (h, 0, 0))),
        compiler_params=pltpu.CompilerParams(dimension_semantics=("arbitrary", "arbitrary"), vmem_limit_bytes=VMEM_LIMIT),
    )(q, k, v, o, lse, d_o)


def _mix(x, target, attn, u, modrows, bmodrows, w_pool, pool_scale, w_out):
    seq = x.shape[0]
    n_tiles = seq // TILE
    rows8 = TILE // HALO
    last8 = (seq + TILE) // HALO - 1

    n_blk = seq // Q_BLOCK
    per = TILE // n_blk
    assert TILE % n_blk == 0 and per % 8 == 0

    def body(x_ref, t_ref, o_hbm, ga_ref, pin_ref, hb_ref, ha_ref, gp_ref, mod_ref, bmod_ref, wp_ref, ps_ref, wo_ref,
             loss_ref, g1_ref, dgate_ref, do_hbm, dga_ref, dgp_ref, dpl_ref, gwo_ref, gwp_ref, dps_ref,
             abuf, dbuf, rsem, wsem):
        j = pl.program_id(0)

        def slab_copies(tile, slot, fetch):
            out = []
            for b in range(n_blk):
                hbm_rows = pl.ds(b * Q_BLOCK + tile * per, per)
                buf_rows = pl.ds(b * per, per)
                if fetch:
                    out.append(pltpu.make_async_copy(o_hbm.at[hbm_rows], abuf.at[slot, buf_rows], rsem.at[slot]))
                else:
                    out.append(pltpu.make_async_copy(dbuf.at[slot, buf_rows], do_hbm.at[hbm_rows], wsem.at[slot]))
            return out

        def wait_all(slot, fetch):
            if fetch:
                pltpu.make_async_copy(o_hbm.at[pl.ds(0, TILE)], abuf.at[slot], rsem.at[slot]).wait()
            else:
                pltpu.make_async_copy(dbuf.at[slot], do_hbm.at[pl.ds(0, TILE)], wsem.at[slot]).wait()

        slot = lax.rem(j, 2)

        @pl.when(j == 0)
        def _():
            loss_ref[...] = jnp.zeros_like(loss_ref)
            dgate_ref[...] = jnp.zeros_like(dgate_ref)
            gwo_ref[...] = jnp.zeros_like(gwo_ref)
            gwp_ref[...] = jnp.zeros_like(gwp_ref)
            dps_ref[...] = jnp.zeros_like(dps_ref)
            for cp in slab_copies(0, 0, True):
                cp.start()

        @pl.when(j + 1 < n_tiles)
        def _():
            for cp in slab_copies(j + 1, 1 - slot, True):
                cp.start()

        wait_all(slot, True)
        attn_t = jnp.swapaxes(abuf[slot].reshape(n_blk, per, D_ATTN), 0, 1).reshape(TILE, D_ATTN)

        gate = mod_ref[2:3, :] + bmod_ref[2:3, :]
        ga = ga_ref[...]
        sga = _sigmoid(ga)
        silu_ga = ga * sga
        br_a = silu_ga * attn_t

        pin = pin_ref[...]
        xs = jnp.concatenate([jnp.where(j > 0, hb_ref[...], 0.0), pin, jnp.where(j < n_tiles - 1, ha_ref[...], 0.0)], axis=0)
        tpos = j * TILE + lax.broadcasted_iota(jnp.int32, (TILE, 1), 0)
        ps = ps_ref[...]
        pooled, yps = [], []
        for g, w in enumerate(POOL_WINDOWS):
            sl = slice(g * GROUP_DIM, (g + 1) * GROUP_DIM)
            ws = _window_sum(xs[:, sl], w, False)[HALO:HALO + TILE]
            pg = (ws * _inv_count(tpos, w, seq) - pin[:, sl]).astype(BF16)
            pooled.append(pg)
            yps.append(_dot(pg, wp_ref[g].astype(BF16)))
        yp = jnp.concatenate(yps, axis=1)
        ypool = yp * ps
        gp = gp_ref[...]
        sgp = _sigmoid(gp)
        silu_gp = gp * sgp
        br_p = silu_gp * ypool

        z = jnp.concatenate([br_a, br_p], axis=1).astype(BF16)
        y = _dot(z, wo_ref[...])
        res = x_ref[...] + gate * y - t_ref[...]
        loss_ref[...] += jnp.sum(res * res) * (0.5 / D_MODEL)
        dxn = res * (1.0 / D_MODEL)
        g1_ref[...] = dxn
        dgate_ref[...] += jnp.sum(dxn * y, axis=0, keepdims=True)
        dy = (gate * dxn).astype(BF16)
        dz = _dot_nt(dy, wo_ref[...])
        gwo_ref[...] += _dot_tn(z, dy)

        dbra = dz[:, 0:D_ATTN]
        dbrp = dz[:, D_ATTN:]
        dga_ref[...] = dbra * attn_t * (sga * (1.0 + ga * (1.0 - sga)))

        @pl.when(j >= 2)
        def _():
            wait_all(slot, False)
        dbuf[slot] = jnp.swapaxes((dbra * silu_ga).reshape(per, n_blk, D_ATTN), 0, 1).reshape(TILE, D_ATTN)
        for cp in slab_copies(j, slot, False):
            cp.start()

        @pl.when(j == n_tiles - 1)
        def _():
            wait_all(slot, False)
            if n_tiles >= 2:
                wait_all(1 - slot, False)

        dgp_ref[...] = dbrp * ypool * (sgp * (1.0 + gp * (1.0 - sgp)))
        dyp_s = dbrp * silu_gp
        dps_ref[...] += jnp.sum(dyp_s * yp, axis=0, keepdims=True)
        dyp = (dyp_s * ps).astype(BF16)
        for g in range(len(POOL_WINDOWS)):
            sl = slice(g * GROUP_DIM, (g + 1) * GROUP_DIM)
            gwp_ref[g] += _dot_tn(pooled[g], dyp[:, sl])
            dpl_ref[:, sl] = _dot_nt(dyp[:, sl], wp_ref[g].astype(BF16))

    tile = lambda w: pl.BlockSpec((TILE, w), lambda j: (j, 0))
    ucol = lambda col: pl.BlockSpec((TILE, 512), lambda j: (j + 1, col))
    full = lambda shape: pl.BlockSpec(shape, lambda j: (0,) * len(shape))
    return pl.pallas_call(
        body, name="mix_fwd_bwd", grid=(n_tiles,),
        out_shape=(jax.ShapeDtypeStruct((8, 128), F32), jax.ShapeDtypeStruct((seq, D_MODEL), F32),
                   jax.ShapeDtypeStruct((1, D_MODEL), F32), jax.ShapeDtypeStruct((seq, D_ATTN), F32),
                   jax.ShapeDtypeStruct((seq, D_ATTN), F32), jax.ShapeDtypeStruct((seq, D_POOL), F32),
                   jax.ShapeDtypeStruct((seq, D_POOL), F32), jax.ShapeDtypeStruct((D_MODEL, D_MODEL), F32),
                   jax.ShapeDtypeStruct((4, GROUP_DIM, GROUP_DIM), F32), jax.ShapeDtypeStruct((1, D_POOL), F32)),
        in_specs=[tile(D_MODEL), tile(D_MODEL), pl.BlockSpec(memory_space=pl.ANY), ucol(0), ucol(1),
                  pl.BlockSpec((HALO, 512), lambda j: ((j + 1) * rows8 - 1, 1)),
                  pl.BlockSpec((HALO, 512), lambda j: (jnp.minimum((j + 2) * rows8, last8), 1)),
                  ucol(2), full((8, D_MODEL)), full((8, D_MODEL)), full((4, GROUP_DIM, GROUP_DIM)), full((1, D_POOL)),
                  full((D_MODEL, D_MODEL))],
        out_specs=(full((8, 128)), tile(D_MODEL), full((1, D_MODEL)), pl.BlockSpec(memory_space=pl.ANY), tile(D_ATTN),
                   tile(D_POOL), tile(D_POOL), full((D_MODEL, D_MODEL)), full((4, GROUP_DIM, GROUP_DIM)), full((1, D_POOL))),
        scratch_shapes=[pltpu.VMEM((2, TILE, D_ATTN), F32), pltpu.VMEM((2, TILE, D_ATTN), F32),
                        pltpu.SemaphoreType.DMA((2,)), pltpu.SemaphoreType.DMA((2,))],
        compiler_params=pltpu.CompilerParams(dimension_semantics=("arbitrary",), vmem_limit_bytes=VMEM_LIMIT),
    )(x, target, attn, u, u, u, u, u, modrows, bmodrows, w_pool, pool_scale, w_out)


def _inproj_bwd(x, ctx, modrows, bmodrows, norm_g, w_in_p, q_lora_g, w_uq_p, kv_lora_g, w_ukv, qng_p, kng_p, cos, slo, shi,
                u, dq, dk, dv, dga, dgp, dpl, g1):
    seq = x.shape[0]
    n_tiles = seq // TILE + 1
    rows8 = TILE // HALO
    last8 = seq // HALO - 1

    def body(x_ref, ctx_ref, mod_ref, bmod_ref, ng_ref, win_ref, qlg_ref, wuq_ref, kvlg_ref, wukv_ref, qng_ref, kng_ref,
             cos_ref, slo_ref, shi_ref, cq_ref, ckv_ref, kr_ref, dq_ref, dk_ref, dv_ref, dga_ref, dgp_ref, dpl_ref,
             hb_ref, ha_ref, g1_ref,
             gx_ref, gwin_ref, gwuq_ref, gwukv_ref, dqlg_ref, dkvlg_ref, dqng_ref, dkng_ref, dng_ref, dmod_ref,
             du_ref):
        i = pl.program_id(0)
        is_lat = i > 0

        @pl.when(i == 0)
        def _():
            for ref in (gwin_ref, gwuq_ref, gwukv_ref, dqlg_ref, dkvlg_ref, dqng_ref, dkng_ref, dng_ref, dmod_ref):
                ref[...] = jnp.zeros_like(ref)

        xt, r, xg, h, scale = _modulated_input(i, x_ref, ctx_ref, mod_ref, bmod_ref, ng_ref)
        hb = h.astype(BF16)
        cos_t, slo_t, shi_t = cos_ref[...], slo_ref[...], shi_ref[...]

        cq = cq_ref[...]
        rq = lax.rsqrt(jnp.mean(cq * cq, axis=-1, keepdims=True) + NORM_EPS)
        qhat = cq * rq
        qnb = (qhat * qlg_ref[...]).astype(BF16)
        q = _dot_nt(qnb, wuq_ref[...])
        qg = qng_ref[...]
        dq_heads = []
        dqng = jnp.zeros((1, D_HEAD_PAD), F32)
        for hd in range(N_HEADS):
            qh = q[:, hd * D_HEAD_PAD:(hd + 1) * D_HEAD_PAD]
            rr = lax.rsqrt(jnp.sum(qh * qh, axis=-1, keepdims=True) * (1.0 / D_HEAD) + NORM_EPS)
            yq = qh * rr
            dpost = jnp.where(is_lat, dq_ref[hd] * ATTN_SCALE, 0.0)
            dyn = jnp.concatenate([dpost[:, 0:D_NOPE], _rope_t(dpost[:, D_NOPE:], cos_t, slo_t, shi_t)], axis=1)
            dqng = dqng + jnp.sum(dyn * yq, axis=0, keepdims=True)
            dyg = dyn * qg
            dq_heads.append(rr * (dyg - yq * (jnp.sum(dyg * yq, axis=-1, keepdims=True) * (1.0 / D_HEAD))))
        dqng_ref[...] += dqng
        dqf = jnp.concatenate(dq_heads, axis=1).astype(BF16)
        gwuq_ref[...] += _dot_tn(dqf, qnb)
        dqn = _dot(dqf, wuq_ref[...])
        dqlg_ref[...] += jnp.sum(dqn * qhat, axis=0, keepdims=True)
        dqhat = dqn * qlg_ref[...]
        dcq = rq * (dqhat - qhat * jnp.mean(dqhat * qhat, axis=-1, keepdims=True))

        ckv = ckv_ref[...]
        rkv = lax.rsqrt(jnp.mean(ckv * ckv, axis=-1, keepdims=True) + NORM_EPS)
        kvhat = ckv * rkv
        kvnb = (kvhat * kvlg_ref[...]).astype(BF16)
        kv = _dot(kvnb, wukv_ref[...])
        krz = kr_ref[...]
        kr_ss = jnp.sum(krz * krz, axis=-1, keepdims=True)
        kg = kng_ref[...]
        kg_n, kg_r = kg[:, 0:D_NOPE], kg[:, D_NOPE:]
        dkv_parts = []
        dkr = jnp.zeros((TILE, 128), F32)
        dkng_n = jnp.zeros((1, D_NOPE), F32)
        dkng_r = jnp.zeros((1, 128), F32)
        for hd in range(N_HEADS):
            kn = kv[:, hd * 256:hd * 256 + D_NOPE]
            rr = lax.rsqrt((jnp.sum(kn * kn, axis=-1, keepdims=True) + kr_ss) * (1.0 / D_HEAD) + NORM_EPS)
            yn, yr = kn * rr, krz * rr
            dk_h = jnp.transpose(dk_ref[hd]) * LN_2
            dpn = dk_h[:, 0:D_NOPE]
            dpr = _rope_t(dk_h[:, D_NOPE:], cos_t, slo_t, shi_t)
            dkng_n = dkng_n + jnp.sum(dpn * yn, axis=0, keepdims=True)
            dkng_r = dkng_r + jnp.sum(dpr * yr, axis=0, keepdims=True)
            dyn, dyr = dpn * kg_n, dpr * kg_r
            proj = (jnp.sum(dyn * yn, axis=-1, keepdims=True) + jnp.sum(dyr * yr, axis=-1, keepdims=True)) * (1.0 / D_HEAD)
            dkv_parts.append(rr * (dyn - yn * proj))
            dkv_parts.append(jnp.transpose(dv_ref[hd]))
            dkr = dkr + rr * (dyr - yr * proj)
        dkng_ref[:, 0:D_NOPE] += dkng_n
        dkng_ref[:, D_NOPE:] += dkng_r
        dkvf = jnp.concatenate(dkv_parts, axis=1).astype(BF16)
        gwukv_ref[...] += _dot_tn(dkvf, kvnb)
        dkvn = _dot_nt(dkvf, wukv_ref[...])
        dkvlg_ref[...] += jnp.sum(dkvn * kvhat, axis=0, keepdims=True)
        dkvhat = dkvn * kvlg_ref[...]
        dckv = rkv * (dkvhat - kvhat * jnp.mean(dkvhat * kvhat, axis=-1, keepdims=True))

        lat_t = jnp.maximum(i - 1, 0)
        dpl_t = dpl_ref[...]
        ds = jnp.concatenate([jnp.where(i > 1, hb_ref[...], 0.0), dpl_t,
                              jnp.where(i < n_tiles - 1, ha_ref[...], 0.0)], axis=0)
        tpos = lat_t * TILE - HALO + lax.broadcasted_iota(jnp.int32, (TILE + 2 * HALO, 1), 0)
        for g, w in enumerate(POOL_WINDOWS):
            sl = slice(g * GROUP_DIM, (g + 1) * GROUP_DIM)
            wsum = _window_sum(ds[:, sl] * _inv_count(tpos, w, seq), w, True)[HALO:HALO + TILE]
            du_ref[:, O_PIN + g * GROUP_DIM:O_PIN + (g + 1) * GROUP_DIM] = jnp.where(
                is_lat, wsum - dpl_t[:, sl], 0.0).astype(BF16)

        du_ref[:, O_GA:O_GA + D_ATTN] = jnp.where(is_lat, dga_ref[...], 0.0).astype(BF16)
        du_ref[:, O_GP:O_GP + D_POOL] = jnp.where(is_lat, dgp_ref[...], 0.0).astype(BF16)
        du_ref[:, O_CQ:O_CQ + R_Q] = dcq.astype(BF16)
        du_ref[:, O_CKV:O_CKV + R_KV] = dckv.astype(BF16)
        du_ref[:, O_KR:U_PAD] = dkr.astype(BF16)
        dub = du_ref[...]
        dh = _dot(dub, win_ref[...])
        gwin_ref[...] += _dot_tn(dub, hb)

        dshift = jnp.sum(dh, axis=0, keepdims=True)
        dscale = jnp.sum(dh * xg, axis=0, keepdims=True)
        zero = jnp.zeros_like(dshift)
        dmod_ref[0:1, :] += jnp.where(is_lat, dshift, zero)
        dmod_ref[1:2, :] += jnp.where(is_lat, dscale, zero)
        dmod_ref[2:3, :] += jnp.where(is_lat, zero, dshift)
        dmod_ref[3:4, :] += jnp.where(is_lat, zero, dscale)
        dxg = dh * (1.0 + scale)
        xr = xt * r
        dng_ref[...] += jnp.sum(dxg * xr, axis=0, keepdims=True)
        dxn = dxg * ng_ref[...]
        gx_ref[...] = g1_ref[...] + r * (dxn - xr * jnp.mean(dxn * xr, axis=-1, keepdims=True))

    lat = lambda i: (jnp.maximum(i - 1, 0), 0)
    full = lambda shape: pl.BlockSpec(shape, lambda i: (0,) * len(shape))
    rope_spec = pl.BlockSpec((TILE, 128), lambda i: (i, 0))
    return pl.pallas_call(
        body, name="inproj_bwd", grid=(n_tiles,),
        out_shape=(jax.ShapeDtypeStruct((seq, D_MODEL), F32), jax.ShapeDtypeStruct((U_PAD, D_MODEL), F32),
                   jax.ShapeDtypeStruct((N_HEADS * D_HEAD_PAD, R_Q), F32), jax.ShapeDtypeStruct((N_HEADS * 256, R_KV), F32),
                   jax.ShapeDtypeStruct((1, R_Q), F32), jax.ShapeDtypeStruct((1, R_KV), F32),
                   jax.ShapeDtypeStruct((1, D_HEAD_PAD), F32), jax.ShapeDtypeStruct((1, D_HEAD_PAD), F32),
                   jax.ShapeDtypeStruct((1, D_MODEL), F32), jax.ShapeDtypeStruct((8, D_MODEL), F32)),
        in_specs=[pl.BlockSpec((TILE, D_MODEL), lat), full((TILE, D_MODEL)), full((8, D_MODEL)), full((8, D_MODEL)),
                  full((1, D_MODEL)), full((U_PAD, D_MODEL)), full((1, R_Q)), full((N_HEADS * D_HEAD_PAD, R_Q)),
                  full((1, R_KV)), full((R_KV, N_HEADS * 256)), full((1, D_HEAD_PAD)), full((1, D_HEAD_PAD)),
                  rope_spec, rope_spec, rope_spec,
                  pl.BlockSpec((TILE, R_Q), lambda i: (i, O_CQ // R_Q)),
                  pl.BlockSpec((TILE, R_KV), lambda i: (i, O_CKV // R_KV)),
                  pl.BlockSpec((TILE, 128), lambda i: (i, O_KR // 128)),
                  pl.BlockSpec((N_HEADS, TILE, D_HEAD_PAD), lambda i: (0, jnp.maximum(i - 1, 0), 0)),
                  pl.BlockSpec((N_HEADS, D_HEAD_PAD, TILE), lambda i: (0, 0, i)),
                  pl.BlockSpec((N_HEADS, D_V, TILE), lambda i: (0, 0, i)),
                  pl.BlockSpec((TILE, D_ATTN), lat), pl.BlockSpec((TILE, D_POOL), lat), pl.BlockSpec((TILE, D_POOL), lat),
                  pl.BlockSpec((HALO, D_POOL), lambda i: (jnp.maximum((i - 1) * rows8 - 1, 0), 0)),
                  pl.BlockSpec((HALO, D_POOL), lambda i: (jnp.minimum(jnp.maximum(i, 1) * rows8, last8), 0)),
                  pl.BlockSpec((TILE, D_MODEL), lat)],
        out_specs=(pl.BlockSpec((TILE, D_MODEL), lat), full((U_PAD, D_MODEL)), full((N_HEADS * D_HEAD_PAD, R_Q)),
                   full((N_HEADS * 256, R_KV)), full((1, R_Q)), full((1, R_KV)), full((1, D_HEAD_PAD)),
                   full((1, D_HEAD_PAD)), full((1, D_MODEL)), full((8, D_MODEL))),
        scratch_shapes=[pltpu.VMEM((TILE, U_PAD), BF16)],
        compiler_params=pltpu.CompilerParams(dimension_semantics=("arbitrary",), vmem_limit_bytes=VMEM_LIMIT),
    )(x, ctx, modrows, bmodrows, norm_g, w_in_p, q_lora_g, w_uq_p, kv_lora_g, w_ukv, qng_p, kng_p, cos, slo, shi,
      u, u, u, dq, dk, dv, dga, dgp, dpl, dpl, dpl, g1)


SMALL_LAYOUT = ((0, D_MODEL), (1, R_Q), (2, R_KV), (3, D_HEAD_PAD), (4, D_HEAD_PAD), (5, D_POOL))
ROW_DMOD_B, ROW_DMOD_C, ROW_LOSS = 6, 9, 12


def _epilogue(parts, smalls, dmod4, dgate, loss_acc, w_mod_l):
    n_a, n_s = len(parts), len(smalls)
    n_mod = w_mod_l.shape[1]
    blk = [p.shape[1:] for p in parts]
    small_out = [D_MODEL, R_Q, R_KV, D_HEAD, D_HEAD, D_POOL]

    def body(*refs):
        part_refs = refs[0:n_a]
        small_in = refs[n_a:n_a + n_s]
        dmod4_ref, dgate_ref, loss_ref, wmod_ref = refs[n_a + n_s:n_a + n_s + 4]
        outs = refs[n_a + n_s + 4:]
        red_refs = outs[0:n_a]
        lastg_ref, ccg_ref = outs[n_a], outs[n_a + 1]
        sum_refs = outs[n_a + 2:n_a + 2 + n_s]
        gbmod_ref, dmy_ref, lossout_ref = outs[n_a + 2 + n_s:n_a + 5 + n_s]
        scr = outs[n_a + 5 + n_s:]
        recv1, own1, sum1b, recv2 = scr[0:n_a], scr[n_a:2 * n_a], scr[2 * n_a:3 * n_a], scr[3 * n_a:4 * n_a]
        small_ref, smallg_ref, tot_ref, cc_ref = scr[4 * n_a:4 * n_a + 4]
        ssem1, rsem1, lsem, ssem2, rsem2, ssem_s, rsem_s, ssem_l, rsem_l, ssem_cc, rsem_cc = scr[4 * n_a + 4:]
        x, y, c = _coords()
        me = (x, y, c)
        sibling = (x, y, 1 - c)

        stage1, own_copies = [], []
        for a in range(n_a):
            for b in range(4):
                dev = 4 * (b >> 1) + 2 * (b & 1)
                stage1.append(pltpu.make_async_remote_copy(
                    src_ref=part_refs[a].at[dev + (1 - c)], dst_ref=recv1[a].at[b],
                    send_sem=ssem1.at[a, b], recv_sem=rsem1.at[a, b], device_id=sibling, device_id_type=MESH))
                own_copies.append(pltpu.make_async_copy(part_refs[a].at[dev + c], own1[a].at[b], lsem.at[a, b]))
                stage1[-1].start()
                own_copies[-1].start()

        small_ref[...] = jnp.zeros_like(small_ref)
        for ref, (row, width) in zip(small_in, SMALL_LAYOUT):
            small_ref[row:row + 1, 0:width] = ref[...]
        small_ref[ROW_DMOD_B:ROW_DMOD_B + 2, :] = dmod4_ref[0:2, :]
        small_ref[ROW_DMOD_B + 2:ROW_DMOD_B + 3, :] = dgate_ref[...]
        small_ref[ROW_DMOD_C:ROW_DMOD_C + 2, :] = dmod4_ref[2:4, :]
        small_ref[ROW_LOSS:ROW_LOSS + 1, 0:128] = loss_ref[0:1, :]
        smallg_ref[_lin(me)] = small_ref[...]
        s_recv, s_send = _gather_to_all(small_ref, smallg_ref, ssem_s, rsem_s)
        s_recv()
        tot = smallg_ref[0]
        for d in range(1, N_DEV):
            tot = tot + smallg_ref[d]
        tot_ref[...] = tot
        for ref, (row, _), width in zip(sum_refs, SMALL_LAYOUT, small_out):
            ref[...] = tot_ref[row:row + 1, 0:width]
        lossout_ref[...] = tot_ref[8:16, 0:128]
        lin = _lin(me)

        def my_columns(three_rows):
            v = jnp.concatenate(three_rows, axis=1)
            out = jnp.zeros((1, n_mod), F32)
            for d in range(N_DEV):
                out = out + jnp.where(lin == d, v[:, d * n_mod:(d + 1) * n_mod], 0.0)
            return out

        rows3 = lambda ref, r0: [ref[r0 + t:r0 + t + 1, :] for t in range(3)]
        dmodc = rows3(tot_ref, ROW_DMOD_C)
        gbmod_ref[...] = jnp.concatenate(rows3(tot_ref, ROW_DMOD_B), axis=1) + jnp.concatenate(dmodc, axis=1)
        dmy_ref[...] = jnp.zeros_like(dmy_ref)
        for b in range(N_DEV):
            dmy_ref[b:b + 1, :] = my_columns(rows3(smallg_ref.at[b], ROW_DMOD_B))
        dmy = my_columns(dmodc)
        dmy_ref[N_DEV:N_DEV + 1, :] = dmy
        part = lax.dot_general(jnp.broadcast_to(dmy, (8, n_mod)), wmod_ref[...], (((1,), (1,)), ((), ())),
                               precision=HIGHEST, preferred_element_type=F32)

        cc_ref[...] = part
        ccg_ref[_lin(me)] = part
        cc_recv, cc_send = _gather_to_all(cc_ref, ccg_ref, ssem_cc, rsem_cc)

        stage2 = []
        for a in range(n_a):
            for b in range(4):
                stage1[4 * a + b].wait_recv()
                own_copies[4 * a + b].wait()
                sum1b[a][b] = (own1[a][b] + recv1[a][b]).astype(BF16)
            for k in (1, 2, 3):
                tx, ty = _flip(x, (k >> 1) & 1), _flip(y, k & 1)
                stage2.append(pltpu.make_async_remote_copy(
                    src_ref=sum1b[a].at[2 * tx + ty], dst_ref=recv2[a].at[k - 1], send_sem=ssem2.at[a, k - 1],
                    recv_sem=rsem2.at[a, k - 1], device_id=(tx, ty, c), device_id_type=MESH))
                stage2[-1].start()
        for a in range(n_a):
            own = own1[a][2 * x + y] + recv1[a][2 * x + y]
            for k in (1, 2, 3):
                stage2[3 * a + k - 1].wait_recv()
                own = own + recv2[a][k - 1].astype(F32)
            red_refs[a][...] = own

        lastg_ref[_lin(me)] = red_refs[n_a - 1][...]
        l_recv, l_send = _gather_to_all(red_refs[n_a - 1], lastg_ref, ssem_l, rsem_l)
        l_recv()
        cc_recv()
        for cp in stage1 + stage2:
            cp.wait_send()
        s_send()
        cc_send()
        l_send()

    vm = pl.BlockSpec(memory_space=pltpu.VMEM)
    sds = jax.ShapeDtypeStruct
    return pl.pallas_call(
        body, name="epilogue_reduce",
        out_shape=(*[sds(s, F32) for s in blk], sds((N_DEV,) + blk[-1], F32), sds((N_DEV, 8, D_MODEL), F32),
                   *[sds((1, w), F32) for w in small_out], sds((1, 3 * D_MODEL), F32), sds((16, n_mod), F32),
                   sds((8, 128), F32)),
        in_specs=[pl.BlockSpec(memory_space=pl.ANY)] * n_a + [vm] * (n_s + 4), out_specs=tuple([vm] * (n_a + n_s + 5)),
        scratch_shapes=[*[pltpu.VMEM((4,) + s, F32) for s in blk], *[pltpu.VMEM((4,) + s, F32) for s in blk],
                        *[pltpu.VMEM((4,) + s, BF16) for s in blk], *[pltpu.VMEM((3,) + s, BF16) for s in blk],
                        pltpu.VMEM((SMALL_ROWS, D_MODEL), F32), pltpu.VMEM((N_DEV, SMALL_ROWS, D_MODEL), F32),
                        pltpu.VMEM((SMALL_ROWS, D_MODEL), F32), pltpu.VMEM((8, D_MODEL), F32),
                        pltpu.SemaphoreType.DMA((n_a, 4)), pltpu.SemaphoreType.DMA((n_a, 4)), pltpu.SemaphoreType.DMA((n_a, 4)),
                        pltpu.SemaphoreType.DMA((n_a, 3)), pltpu.SemaphoreType.DMA((n_a, 3)),
                        pltpu.SemaphoreType.DMA((7,)), pltpu.SemaphoreType.DMA((7,)),
                        pltpu.SemaphoreType.DMA((7,)), pltpu.SemaphoreType.DMA((7,)),
                        pltpu.SemaphoreType.DMA((7,)), pltpu.SemaphoreType.DMA((7,))],
        compiler_params=pltpu.CompilerParams(vmem_limit_bytes=VMEM_LIMIT),
    )(*parts, *smalls, dmod4, dgate, loss_acc, w_mod_l)


def _adamw(weights, grads, ms, vs, c_all_t, dmod_my, p2g):
    n = len(weights)

    def body(*refs):
        w_refs = refs[0:n]
        g_in = refs[n:2 * n - 2]
        m_refs = refs[2 * n - 2:3 * n - 2]
        v_refs = refs[3 * n - 2:4 * n - 2]
        cat_ref, dmod_ref, p2g_ref = refs[4 * n - 2:4 * n + 1]
        outs = refs[4 * n + 1:]
        gcc_ref, gwm_ref = outs[0], outs[1]
        d_refs, nm_refs, nv_refs = outs[2:2 + n], outs[2 + n:2 + 2 * n], outs[2 + 2 * n:2 + 3 * n]

        part = p2g_ref[0, 0:1, :]
        for d in range(1, N_DEV):
            part = part + p2g_ref[d, 0:1, :]
        cc = w_refs[0][...]
        sg = _sigmoid(cc)
        gcc_ref[...] = part * (sg * (1.0 + cc * (1.0 - sg)))
        cat = cat_ref[...]
        gwm_ref[...] = lax.dot_general(cat * _sigmoid(cat), dmod_ref[...], (((1,), (0,)), ((), ())), precision=HIGHEST,
                                       preferred_element_type=F32)
        for idx in range(n):
            g = (gcc_ref, gwm_ref)[idx][...] if idx < 2 else g_in[idx - 2][...]
            m = ADAM_B1 * m_refs[idx][...] + (1.0 - ADAM_B1) * g
            v = ADAM_B2 * v_refs[idx][...] + (1.0 - ADAM_B2) * (g * g)
            m_hat = m / (1.0 - ADAM_B1 ** ADAM_STEP)
            v_hat = v / (1.0 - ADAM_B2 ** ADAM_STEP)
            d_refs[idx][...] = -ADAM_LR * (m_hat / (jnp.sqrt(v_hat) + ADAM_EPS) + ADAM_WD * w_refs[idx][...])
            nm_refs[idx][...] = m
            nv_refs[idx][...] = v

    vm = pl.BlockSpec(memory_space=pltpu.VMEM)
    shapes = [jax.ShapeDtypeStruct(w.shape, F32) for w in weights]
    args = list(weights) + list(grads[2:]) + list(ms) + list(vs) + [c_all_t, dmod_my, p2g]
    out_shape = tuple(shapes[0:2] + shapes * 3)
    return pl.pallas_call(
        body, name="adamw_update", out_shape=out_shape, in_specs=[vm] * len(args), out_specs=tuple([vm] * len(out_shape)),
        compiler_params=pltpu.CompilerParams(vmem_limit_bytes=VMEM_LIMIT),
    )(*args)


def _rope_tables(seq):
    rows = seq // GRID_W
    row = np.repeat(np.arange(rows, dtype=np.float32), GRID_W)
    col = np.tile(np.arange(GRID_W, dtype=np.float32), rows)
    n_freq = D_ROPE // 4
    inv = (np.float32(ROPE_BASE) ** (-np.arange(n_freq, dtype=np.float32) / np.float32(n_freq))).astype(np.float32)
    ang_r = (row[:, None] * inv).astype(np.float32)
    ang_c = (col[:, None] * inv).astype(np.float32)
    ang = np.concatenate([ang_r, ang_r, ang_c, ang_c], axis=-1)
    cos, sin = np.cos(ang).astype(np.float32), np.sin(ang).astype(np.float32)
    low = (np.arange(D_ROPE) % 32) < 16

    def table(t, fill):
        out = np.full((TILE + seq, 128), fill, np.float32)
        out[TILE:, 0:D_ROPE] = t
        return jnp.asarray(out)

    return table(cos, 1.0), table(np.where(low, -sin, 0.0), 0.0), table(np.where(low, 0.0, sin), 0.0)


def kernel(x, c, ctx, c_ctx, w_mod, b_mod, norm_g, w_in, q_lora_g, w_uq, kv_lora_g, w_ukv, q_norm_g, k_norm_g, w_pool, pool_scale, w_out, loss_target, m_c_ctx, m_w_mod, m_b_mod, m_norm_g, m_w_in, m_q_lora_g, m_w_uq, m_kv_lora_g, m_w_ukv, m_q_norm_g, m_k_norm_g, m_w_pool, m_pool_scale, m_w_out, v_c_ctx, v_w_mod, v_b_mod, v_norm_g, v_w_in, v_q_lora_g, v_w_uq, v_kv_lora_g, v_w_ukv, v_q_norm_g, v_k_norm_g, v_w_pool, v_pool_scale, v_w_out):
    seq = x.shape[1]
    assert ctx.shape[1] == TILE and seq % TILE == 0 and seq % GRID_W == 0
    ix, iy, ic = lax.axis_index("x"), lax.axis_index("y"), lax.axis_index("c")
    me = 4 * ix + 2 * iy + ic
    x2, ctx2, tgt2 = x[0], ctx[0], loss_target[0]
    w_mod_l, w_ukv_l, w_out_l = w_mod[0], w_ukv[0], w_out[0]
    c_ctx_row = c_ctx.reshape(1, D_MODEL)

    tr = lambda a: jnp.transpose(a[0])
    w_in_tl, w_uq_tl = tr(w_in), tr(w_uq)
    cg, modg, wg_in, wg_uq, wg_ukv, wg_out = _prologue(
        jnp.broadcast_to(c, (8, D_MODEL)), jnp.broadcast_to(c_ctx_row, (8, D_MODEL)), w_mod_l,
        [w_in_tl, w_uq_tl, w_ukv_l, w_out_l])
    mod_all = jnp.transpose(modg, (1, 0, 2)).reshape(16, 3 * D_MODEL)
    mod_b = lax.dynamic_slice_in_dim(mod_all, me, 1, axis=0).reshape(3, D_MODEL)
    mod_c = mod_all[8].reshape(3, D_MODEL)
    modrows = jnp.concatenate([mod_b, mod_c[0:2], jnp.zeros((3, D_MODEL), F32)], axis=0)
    b3 = b_mod.reshape(3, D_MODEL)
    bmodrows = jnp.concatenate([b3, b3[0:2], jnp.zeros((3, D_MODEL), F32)], axis=0)

    w_in_t = wg_in.reshape(D_IN_PROJ, D_MODEL)
    w_in_p = jnp.concatenate([w_in_t[448:], w_in_t[0:384], w_in_t[384:448],
                              jnp.zeros((U_PAD - D_IN_PROJ, D_MODEL), BF16)], axis=0)
    w_uq_p = jnp.concatenate([wg_uq.reshape(N_HEADS, D_HEAD, R_Q), jnp.zeros((N_HEADS, D_HEAD_PAD - D_HEAD, R_Q), BF16)],
                             axis=1).reshape(N_HEADS * D_HEAD_PAD, R_Q)
    w_ukv_f = jnp.transpose(wg_ukv, (1, 0, 2)).reshape(R_KV, N_HEADS * 256)
    w_out_f = wg_out.reshape(D_MODEL, D_MODEL)
    qng_p = jnp.concatenate([q_norm_g, jnp.zeros((1, D_HEAD_PAD - D_HEAD), F32)], axis=1)
    kng_p = jnp.concatenate([k_norm_g, jnp.zeros((1, D_HEAD_PAD - D_HEAD), F32)], axis=1)
    cos, slo, shi = _rope_tables(seq)

    u, q, k, v = _inproj_fwd(x2, ctx2, modrows, bmodrows, norm_g, w_in_p, q_lora_g, w_uq_p, kv_lora_g, w_ukv_f,
                             qng_p, kng_p, cos, slo, shi)
    attn, lse = _attn_fwd(q, k, v, min(1024, seq))
    (loss_acc, g1, dgate, d_attn, dga, dgp, dpl, gwo, gwp, dps) = _mix(
        x2, tgt2, attn, u, modrows, bmodrows, w_pool[0], pool_scale, w_out_f)

    dq, dk, dv = _attn_bwd(q, k, v, attn, lse, d_attn, min(1024, seq))
    (grad_x, gwin_p, gwuq_p, gwukv_t, dqlg, dkvlg, dqng, dkng, dng, dmod4) = _inproj_bwd(
        x2, ctx2, modrows, bmodrows, norm_g, w_in_p, q_lora_g, w_uq_p, kv_lora_g, w_ukv_f, qng_p, kng_p, cos, slo, shi,
        u, dq, dk, dv, dga, dgp, dpl, g1)

    gwin_t = jnp.concatenate([gwin_p[O_CQ:O_KR], gwin_p[O_KR:D_IN_PROJ], gwin_p[0:O_CQ]], axis=0)
    gwuq_t = gwuq_p.reshape(N_HEADS, D_HEAD_PAD, R_Q)[:, 0:D_HEAD].reshape(N_HEADS * D_HEAD, R_Q)
    blocks = lambda a: a.reshape((N_DEV, a.shape[0] // N_DEV) + a.shape[1:])
    parts = [blocks(gwin_t), blocks(gwuq_t), blocks(gwukv_t), blocks(gwo), blocks(gwp.reshape(4 * GROUP_DIM, GROUP_DIM))]
    (r_win, r_wuq, r_wukv, r_wout, _, wpool_g, ccg, g_ng, g_qlg, g_kvlg, g_qng, g_kng, g_ps, g_bmod, dmod_my,
     loss_rows) = _epilogue(parts, [dng, dqlg, dkvlg, dqng, dkng, dps], dmod4, dgate, loss_acc, w_mod_l)

    g_w_ukv = jnp.transpose(r_wukv)
    g_w_pool = wpool_g.reshape(4 * GROUP_DIM, GROUP_DIM)
    c_rows = cg[:, 0, :]
    c_all_t = jnp.transpose(jnp.concatenate([c_rows, c_ctx_row, jnp.zeros((16 - N_DEV - 1, D_MODEL), F32)], axis=0))

    weights = [c_ctx_row, w_mod_l, b_mod, norm_g, w_in_tl, q_lora_g, w_uq_tl, kv_lora_g, w_ukv_l, q_norm_g, k_norm_g,
               w_pool.reshape(4 * GROUP_DIM, GROUP_DIM), pool_scale, w_out_l]
    grads = [None, None, g_bmod, g_ng, r_win, g_qlg, r_wuq, g_kvlg, g_w_ukv, g_qng, g_kng, g_w_pool, g_ps, r_wout]
    ms = [m_c_ctx.reshape(1, D_MODEL), m_w_mod[0], m_b_mod, m_norm_g, tr(m_w_in), m_q_lora_g, tr(m_w_uq), m_kv_lora_g,
          m_w_ukv[0], m_q_norm_g, m_k_norm_g, m_w_pool.reshape(4 * GROUP_DIM, GROUP_DIM), m_pool_scale, m_w_out[0]]
    vs = [v_c_ctx.reshape(1, D_MODEL), v_w_mod[0], v_b_mod, v_norm_g, tr(v_w_in), v_q_lora_g, tr(v_w_uq), v_kv_lora_g,
          v_w_ukv[0], v_q_norm_g, v_k_norm_g, v_w_pool.reshape(4 * GROUP_DIM, GROUP_DIM), v_pool_scale, v_w_out[0]]
    outs = _adamw(weights, grads, ms, vs, c_all_t, dmod_my, ccg)
    grads[0], grads[1] = outs[0], outs[1]
    n = len(weights)
    deltas, new_m, new_v = outs[2:2 + n], outs[2 + n:2 + 2 * n], outs[2 + 2 * n:2 + 3 * n]

    loss = loss_rows[ROW_LOSS - 8, 0]
    final = [c_ctx.shape, w_mod.shape, b_mod.shape, norm_g.shape, w_in.shape, q_lora_g.shape, w_uq.shape, kv_lora_g.shape,
             w_ukv.shape, q_norm_g.shape, k_norm_g.shape, w_pool.shape, pool_scale.shape, w_out.shape]
    transposed = (4, 6)

    def shaped(arrs):
        return [(jnp.transpose(a) if i in transposed else a).reshape(s) for i, (a, s) in enumerate(zip(arrs, final))]

    return (loss, grad_x[None], *shaped(grads), *shaped(deltas), *shaped(new_m), *shaped(new_v))
```

```python
import numpy as np

import jax
import jax.numpy as jnp
from jax import lax
from jax.experimental import pallas as pl
from jax.experimental.pallas import tpu as pltpu

F32 = jnp.float32
BF16 = jnp.bfloat16
MESH = pl.DeviceIdType.MESH
HIGHEST = lax.Precision.HIGHEST

D_MODEL = 1024
N_HEADS = 4
D_NOPE = 128
D_ROPE = 64
D_HEAD = D_NOPE + D_ROPE
D_HEAD_PAD = 256
D_V = 128
R_Q = 256
R_KV = 128
D_ATTN = 512
D_POOL = 512
POOL_WINDOWS = (2, 4, 8, 16)
GROUP_DIM = 128
GRID_W = 64
ROPE_BASE = 10000.0
NORM_EPS = 1e-6
ATTN_SCALE = D_HEAD ** -0.5
LOG2_E = 1.4426950408889634
LN_2 = 0.6931471805599453
ATTN_ROWS = 256
D_IN_PROJ = 1984
U_PAD = 2048
O_GA, O_PIN, O_GP, O_CQ, O_CKV, O_KR = 0, 512, 1024, 1536, 1792, 1920
TILE = 256
Q_BLOCK = 128
HALO = 8
N_DEV = 8
VMEM_LIMIT = 56 * 1024 * 1024

ADAM_LR = 0.001
ADAM_B1 = 0.9
ADAM_B2 = 0.999
ADAM_EPS = 1e-08
ADAM_WD = 0.01
ADAM_STEP = 10

SMALL_ROWS = 16


def _dot(a, b):
    return lax.dot_general(a, b, (((1,), (0,)), ((), ())), preferred_element_type=F32)


def _dot_nt(a, b):
    return lax.dot_general(a, b, (((1,), (1,)), ((), ())), preferred_element_type=F32)


def _dot_tn(a, b):
    return lax.dot_general(a, b, (((0,), (0,)), ((), ())), preferred_element_type=F32)


def _sigmoid(x):
    return 1.0 / (1.0 + jnp.exp(-x))


def _rope(p, cos, slo, shi):
    return p * cos + pltpu.roll(p, 112, 1) * slo + pltpu.roll(p, 16, 1) * shi


def _rope_t(d, cos, slo, shi):
    return d * cos + pltpu.roll(d * slo, 16, 1) + pltpu.roll(d * shi, 112, 1)


def _shift_rows(a, k):
    n = a.shape[0]
    return pltpu.roll(a, (-k) % n, 0)


def _window_sum(x, w, transposed):
    s = (x + _shift_rows(x, 1)) if transposed else (_shift_rows(x, -1) + x)
    step = 1
    while 2 * step < w:
        s = _shift_rows(s, -step) + _shift_rows(s, step)
        step *= 2
    return s


def _inv_count(tpos, w, seq):
    lo = jnp.maximum(tpos - w // 2, 0)
    hi = jnp.minimum(tpos - w // 2 + w, seq)
    return 1.0 / jnp.maximum(hi - lo, 1).astype(F32)


def _coords():
    return lax.axis_index("x"), lax.axis_index("y"), lax.axis_index("c")


def _flip(v, bit):
    return (1 - v) if bit else v


def _peer(k):
    x, y, c = _coords()
    return (_flip(x, (k >> 2) & 1), _flip(y, (k >> 1) & 1), _flip(c, k & 1))


def _lin(p):
    return 4 * p[0] + 2 * p[1] + p[2]


def _gather_to_all(src_ref, slots_ref, send_sems, recv_sems):
    me = _coords()
    copies = []
    for k in range(1, N_DEV):
        cp = pltpu.make_async_remote_copy(
            src_ref=src_ref, dst_ref=slots_ref.at[_lin(me)], send_sem=send_sems.at[k - 1], recv_sem=recv_sems.at[k - 1],
            device_id=_peer(k), device_id_type=MESH)
        cp.start()
        copies.append(cp)

    def wait_recv():
        for k in range(1, N_DEV):
            pltpu.make_async_remote_copy(
                src_ref=src_ref, dst_ref=slots_ref.at[_lin(_peer(k))], send_sem=send_sems.at[k - 1],
                recv_sem=recv_sems.at[k - 1], device_id=_peer(k), device_id_type=MESH).wait_recv()

    def wait_send():
        for cp in copies:
            cp.wait_send()

    return wait_recv, wait_send


def _prologue(c8, cctx8, w_mod_l, shards):
    n_mod = w_mod_l.shape[1]
    n_w = len(shards)

    def body(c8_ref, cctx_ref, wmod_ref, *refs):
        w_refs = refs[0:n_w]
        cg_ref, modg_ref = refs[n_w], refs[n_w + 1]
        wg_refs = refs[n_w + 2:2 * n_w + 2]
        modblk_ref = refs[2 * n_w + 2]
        wb_refs = refs[2 * n_w + 3:3 * n_w + 3]
        ssem_w, rsem_w, ssem_c, rsem_c, ssem_m, rsem_m = refs[3 * n_w + 3:]
        x, y, c = _coords()
        me = (x, y, c)
        sibling = (x, y, 1 - c)
        chips = [(1 - x, y), (x, 1 - y), (1 - x, 1 - y)]

        def wcopies(k, block, to, own=False):
            out = []
            for a in range(n_w):
                slot = wg_refs[a].at[_lin(block)]
                out.append(pltpu.make_async_remote_copy(
                    src_ref=wb_refs[a] if own else slot, dst_ref=slot, send_sem=ssem_w.at[a, k], recv_sem=rsem_w.at[a, k],
                    device_id=to, device_id_type=MESH))
            return out

        for a in range(n_w):
            wb_refs[a][...] = w_refs[a][...].astype(BF16)
        first = wcopies(0, me, sibling, own=True)
        for j, chip in enumerate(chips):
            first += wcopies(1 + j, me, (*chip, c), own=True)
        for cp in first:
            cp.start()
        for a in range(n_w):
            wg_refs[a][_lin(me)] = wb_refs[a][...]

        cg_ref[_lin(me)] = c8_ref[...]
        c_recv, c_send = _gather_to_all(c8_ref, cg_ref, ssem_c, rsem_c)
        c_recv()
        row = lax.broadcasted_iota(jnp.int32, (8, D_MODEL), 0)
        c_all = jnp.zeros((8, D_MODEL), F32)
        for d in range(N_DEV):
            c_all = c_all + jnp.where(row == d, cg_ref[d], 0.0)
        cc = cctx_ref[...]
        a = jnp.concatenate([c_all * _sigmoid(c_all), cc * _sigmoid(cc)], axis=0)
        modblk_ref[...] = lax.dot_general(a, wmod_ref[...], (((1,), (0,)), ((), ())), precision=HIGHEST,
                                          preferred_element_type=F32)
        modg_ref[_lin(me)] = modblk_ref[...]
        m_recv, m_send = _gather_to_all(modblk_ref, modg_ref, ssem_m, rsem_m)
        m_recv()

        passed = []
        for j, chip in enumerate(chips):
            for cp in wcopies(1 + j, (*chip, c), me):
                cp.wait_recv()
            fwd = wcopies(4 + j, (*chip, c), sibling)
            for cp in fwd:
                cp.start()
            passed += fwd
        for cp in wcopies(0, sibling, me):
            cp.wait_recv()
        for j, chip in enumerate(chips):
            for cp in wcopies(4 + j, (*chip, 1 - c), me):
                cp.wait_recv()
        for cp in first + passed:
            cp.wait_send()
        c_send()
        m_send()

    vm = pl.BlockSpec(memory_space=pltpu.VMEM)
    return pl.pallas_call(
        body, name="prologue_gather",
        out_shape=(jax.ShapeDtypeStruct((N_DEV, 8, D_MODEL), F32), jax.ShapeDtypeStruct((N_DEV, 16, n_mod), F32),
                   *[jax.ShapeDtypeStruct((N_DEV,) + s.shape, BF16) for s in shards]),
        in_specs=[vm] * (3 + n_w), out_specs=tuple([vm] * (2 + n_w)),
        scratch_shapes=[pltpu.VMEM((16, n_mod), F32), *[pltpu.VMEM(s.shape, BF16) for s in shards],
                        pltpu.SemaphoreType.DMA((n_w, 7)), pltpu.SemaphoreType.DMA((n_w, 7)),
                        pltpu.SemaphoreType.DMA((7,)), pltpu.SemaphoreType.DMA((7,)),
                        pltpu.SemaphoreType.DMA((7,)), pltpu.SemaphoreType.DMA((7,))],
        compiler_params=pltpu.CompilerParams(vmem_limit_bytes=VMEM_LIMIT),
    )(c8, cctx8, w_mod_l, *shards)


def _modulated_input(i, x_ref, ctx_ref, mod_ref, bmod_ref, ng_ref):
    is_ctx = i == 0
    xt = jnp.where(is_ctx, ctx_ref[...], x_ref[...])
    shift = jnp.where(is_ctx, mod_ref[3:4, :] + bmod_ref[3:4, :], mod_ref[0:1, :] + bmod_ref[0:1, :])
    scale = jnp.where(is_ctx, mod_ref[4:5, :] + bmod_ref[4:5, :], mod_ref[1:2, :] + bmod_ref[1:2, :])
    r = lax.rsqrt(jnp.mean(xt * xt, axis=-1, keepdims=True) + NORM_EPS)
    xg = (xt * r) * ng_ref[...]
    h = xg * (1.0 + scale) + shift
    return xt, r, xg, h, scale


def _inproj_fwd(x, ctx, modrows, bmodrows, norm_g, w_in_p, q_lora_g, w_uq_p, kv_lora_g, w_ukv, qng_p, kng_p, cos, slo, shi):
    seq = x.shape[0]
    n_tiles = seq // TILE + 1
    tot = seq + TILE

    def body(x_ref, ctx_ref, mod_ref, bmod_ref, ng_ref, win_ref, qlg_ref, wuq_ref, kvlg_ref, wukv_ref, qng_ref, kng_ref,
             cos_ref, slo_ref, shi_ref, u_ref, q_ref, k_ref, v_ref):
        i = pl.program_id(0)
        _, _, _, h, _ = _modulated_input(i, x_ref, ctx_ref, mod_ref, bmod_ref, ng_ref)
        u = _dot_nt(h.astype(BF16), win_ref[...])
        u_ref[...] = u
        cos_t, slo_t, shi_t = cos_ref[...], slo_ref[...], shi_ref[...]

        cq = u[:, O_CQ:O_CQ + R_Q]
        qn = cq * lax.rsqrt(jnp.mean(cq * cq, axis=-1, keepdims=True) + NORM_EPS) * qlg_ref[...]
        q = _dot_nt(qn.astype(BF16), wuq_ref[...])
        qg = qng_ref[...] * (ATTN_SCALE * LOG2_E)
        for hd in range(N_HEADS):
            qh = q[:, hd * D_HEAD_PAD:(hd + 1) * D_HEAD_PAD]
            rr = lax.rsqrt(jnp.sum(qh * qh, axis=-1, keepdims=True) * (1.0 / D_HEAD) + NORM_EPS)
            qy = qh * rr * qg
            q_ref[hd, :, 0:D_NOPE] = qy[:, 0:D_NOPE].astype(BF16)
            q_ref[hd, :, D_NOPE:D_HEAD_PAD] = _rope(qy[:, D_NOPE:D_HEAD_PAD], cos_t, slo_t, shi_t).astype(BF16)

        ckv = u[:, O_CKV:O_CKV + R_KV]
        kvn = ckv * lax.rsqrt(jnp.mean(ckv * ckv, axis=-1, keepdims=True) + NORM_EPS) * kvlg_ref[...]
        kv = _dot(kvn.astype(BF16), wukv_ref[...])
        krz = u[:, O_KR:U_PAD]
        kr_ss = jnp.sum(krz * krz, axis=-1, keepdims=True)
        kg = kng_ref[...]
        for hd in range(N_HEADS):
            kn = kv[:, hd * 256:hd * 256 + D_NOPE]
            rr = lax.rsqrt((jnp.sum(kn * kn, axis=-1, keepdims=True) + kr_ss) * (1.0 / D_HEAD) + NORM_EPS)
            k_ref[hd, :, 0:D_NOPE] = (kn * rr * kg[:, 0:D_NOPE]).astype(BF16)
            k_ref[hd, :, D_NOPE:D_HEAD_PAD] = _rope(krz * rr * kg[:, D_NOPE:D_HEAD_PAD], cos_t, slo_t, shi_t).astype(BF16)
            v_ref[hd] = kv[:, hd * 256 + D_NOPE:(hd + 1) * 256].astype(BF16)

    lat = lambda i: (jnp.maximum(i - 1, 0), 0)
    full = lambda shape: pl.BlockSpec(shape, lambda i: (0,) * len(shape))
    return pl.pallas_call(
        body, name="inproj_fwd", grid=(n_tiles,),
        out_shape=(jax.ShapeDtypeStruct((tot, U_PAD), F32),
                   jax.ShapeDtypeStruct((N_HEADS, seq, D_HEAD_PAD), BF16),
                   jax.ShapeDtypeStruct((N_HEADS, tot, D_HEAD_PAD), BF16),
                   jax.ShapeDtypeStruct((N_HEADS, tot, D_V), BF16)),
        in_specs=[pl.BlockSpec((TILE, D_MODEL), lat), full((TILE, D_MODEL)), full((8, D_MODEL)), full((8, D_MODEL)),
                  full((1, D_MODEL)), full((U_PAD, D_MODEL)), full((1, R_Q)), full((N_HEADS * D_HEAD_PAD, R_Q)),
                  full((1, R_KV)), full((R_KV, N_HEADS * 256)), full((1, D_HEAD_PAD)), full((1, D_HEAD_PAD)),
                  pl.BlockSpec((TILE, 128), lambda i: (i, 0)), pl.BlockSpec((TILE, 128), lambda i: (i, 0)),
                  pl.BlockSpec((TILE, 128), lambda i: (i, 0))],
        out_specs=(pl.BlockSpec((TILE, U_PAD), lambda i: (i, 0)),
                   pl.BlockSpec((N_HEADS, TILE, D_HEAD_PAD), lambda i: (0, jnp.maximum(i - 1, 0), 0)),
                   pl.BlockSpec((N_HEADS, TILE, D_HEAD_PAD), lambda i: (0, i, 0)),
                   pl.BlockSpec((N_HEADS, TILE, D_V), lambda i: (0, i, 0))),
        compiler_params=pltpu.CompilerParams(dimension_semantics=("arbitrary",), vmem_limit_bytes=VMEM_LIMIT),
    )(x, ctx, modrows, bmodrows, norm_g, w_in_p, q_lora_g, w_uq_p, kv_lora_g, w_ukv, qng_p, kng_p, cos, slo, shi)


def _hosted_exchange(first, last, items, send_sems, recv_sems, local_sems):
    me = _lin(_coords())

    def remote(n, k, src, land, scatter):
        peer = _peer(k)
        return pltpu.make_async_remote_copy(
            src_ref=src.at[_lin(peer)] if scatter else src, dst_ref=land.at[me], send_sem=send_sems.at[n, k - 1],
            recv_sem=recv_sems.at[n, k - 1], device_id=peer, device_id_type=MESH)

    def local(n, src, land, scatter):
        return pltpu.make_async_copy(src.at[me] if scatter else src, land.at[me], local_sems.at[n])

    @pl.when(first)
    def _():
        for n, (src, land, scatter) in enumerate(items):
            for k in range(1, N_DEV):
                remote(n, k, src, land, scatter).start()
            local(n, src, land, scatter).start()

    @pl.when(last)
    def _():
        for n, (src, land, scatter) in enumerate(items):
            for k in range(1, N_DEV):
                peer = _peer(k)
                pltpu.make_async_remote_copy(
                    src_ref=src.at[0] if scatter else src, dst_ref=land.at[_lin(peer)], send_sem=send_sems.at[n, k - 1],
                    recv_sem=recv_sems.at[n, k - 1], device_id=peer, device_id_type=MESH).wait_recv()
            for k in range(1, N_DEV):
                remote(n, k, src, land, scatter).wait_send()
            local(n, src, land, scatter).wait()


def _attn_fwd(q, k, v, block_q, w_out_b):
    _, seq, _ = q.shape
    n_keys = k.shape[1]
    n_q = seq // block_q

    def body(q_ref, k_ref, v_ref, wo_ref, o_ref, lse_ref, wog_ref, ssem, rsem, lsem):
        h, i = pl.program_id(0), pl.program_id(1)
        _hosted_exchange((h == 0) & (i == 0), (h == N_HEADS - 1) & (i == n_q - 1), [(wo_ref, wog_ref, False)],
                         ssem, rsem, lsem)
        kb, vb = k_ref[0], v_ref[0]
        for r0 in range(0, block_q, ATTN_ROWS):
            rows = slice(r0, r0 + ATTN_ROWS)
            s = _dot_nt(q_ref[0, rows, :], kb)
            m = jnp.max(s, axis=-1, keepdims=True)
            p = jnp.exp2(s - m)
            l = jnp.sum(p, axis=-1, keepdims=True)
            o_ref[rows, :] = _dot(p.astype(BF16), vb) * (1.0 / l)
            lse_ref[0, rows, :] = m + jnp.log2(l)

    hbm = pl.BlockSpec(memory_space=pl.ANY)
    return pl.pallas_call(
        body, name="attn_fwd", grid=(N_HEADS, n_q),
        out_shape=(jax.ShapeDtypeStruct((seq, D_ATTN), F32), jax.ShapeDtypeStruct((N_HEADS, seq, 1), F32),
                   jax.ShapeDtypeStruct((N_DEV,) + w_out_b.shape, w_out_b.dtype)),
        in_specs=[pl.BlockSpec((1, block_q, D_HEAD_PAD), lambda h, i: (h, i, 0)),
                  pl.BlockSpec((1, n_keys, D_HEAD_PAD), lambda h, i: (h, 0, 0)),
                  pl.BlockSpec((1, n_keys, D_V), lambda h, i: (h, 0, 0)), hbm],
        out_specs=(pl.BlockSpec((block_q, D_V), lambda h, i: (i, h)),
                   pl.BlockSpec((1, block_q, 1), lambda h, i: (h, i, 0)), hbm),
        scratch_shapes=[pltpu.SemaphoreType.DMA((1, 7)), pltpu.SemaphoreType.DMA((1, 7)), pltpu.SemaphoreType.DMA((1,))],
        compiler_params=pltpu.CompilerParams(dimension_semantics=("arbitrary", "arbitrary"), vmem_limit_bytes=VMEM_LIMIT),
    )(q, k, v, w_out_b)


def _attn_bwd(q, k, v, o, lse, d_o, block_q, gwo_blocks, gwp):
    _, seq, _ = q.shape
    n_keys = k.shape[1]
    n_q = seq // block_q

    def body(q_ref, k_ref, v_ref, o_ref, lse_ref, do_ref, gwo_ref, gwp_ref, dq_ref, dkt_ref, dvt_ref, lwo_ref, lwp_ref,
             ssem, rsem, lsem):
        h, i = pl.program_id(0), pl.program_id(1)
        _hosted_exchange((h == 0) & (i == 0), (h == N_HEADS - 1) & (i == n_q - 1),
                         [(gwo_ref, lwo_ref, True), (gwp_ref, lwp_ref, False)], ssem, rsem, lsem)

        @pl.when(i == 0)
        def _():
            dkt_ref[...] = jnp.zeros_like(dkt_ref)
            dvt_ref[...] = jnp.zeros_like(dvt_ref)

        kb, vb = k_ref[0], v_ref[0]
        raws, ps = [], []
        for r0 in range(0, block_q, ATTN_ROWS):
            rows = slice(r0, r0 + ATTN_ROWS)
            d_out = do_ref[rows, :]
            p = jnp.exp2(_dot_nt(q_ref[0, rows, :], kb) - lse_ref[0, rows, :])
            dp = _dot_nt(d_out.astype(BF16), vb)
            delta = jnp.sum(d_out * o_ref[rows, :], axis=-1, keepdims=True)
            raw = (p * (dp - delta)).astype(BF16)
            dq_ref[0, rows, :] = _dot(raw, kb)
            raws.append(raw)
            ps.append(p.astype(BF16))
        dkt_ref[0] += _dot_tn(q_ref[0], jnp.concatenate(raws, axis=0))
        dvt_ref[0] += _dot_tn(do_ref[...].astype(BF16), jnp.concatenate(ps, axis=0))

    hbm = pl.BlockSpec(memory_space=pl.ANY)
    return pl.pallas_call(
        body, name="attn_bwd", grid=(N_HEADS, n_q),
        out_shape=(jax.ShapeDtypeStruct((N_HEADS, seq, D_HEAD_PAD), F32),
                   jax.ShapeDtypeStruct((N_HEADS, D_HEAD_PAD, n_keys), F32),
                   jax.ShapeDtypeStruct((N_HEADS, D_V, n_keys), F32),
                   jax.ShapeDtypeStruct(gwo_blocks.shape, gwo_blocks.dtype),
                   jax.ShapeDtypeStruct((N_DEV,) + gwp.shape, gwp.dtype)),
        in_specs=[pl.BlockSpec((1, block_q, D_HEAD_PAD), lambda h, i: (h, i, 0)),
                  pl.BlockSpec((1, n_keys, D_HEAD_PAD), lambda h, i: (h, 0, 0)),
                  pl.BlockSpec((1, n_keys, D_V), lambda h, i: (h, 0, 0)),
                  pl.BlockSpec((block_q, D_V), lambda h, i: (i, h)),
                  pl.BlockSpec((1, block_q, 1), lambda h, i: (h, i, 0)),
                  pl.BlockSpec((block_q, D_V), lambda h, i: (i, h)), hbm, hbm],
        out_specs=(pl.BlockSpec((1, block_q, D_HEAD_PAD), lambda h, i: (h, i, 0)),
                   pl.BlockSpec((1, D_HEAD_PAD, n_keys), lambda h, i: (h, 0, 0)),
                   pl.BlockSpec((1, D_V, n_keys), lambda h, i: (h, 0, 0)), hbm, hbm),
        scratch_shapes=[pltpu.SemaphoreType.DMA((2, 7)), pltpu.SemaphoreType.DMA((2, 7)), pltpu.SemaphoreType.DMA((2,))],
        compiler_params=pltpu.CompilerParams(dimension_semantics=("arbitrary", "arbitrary"), vmem_limit_bytes=VMEM_LIMIT),
    )(q, k, v, o, lse, d_o, gwo_blocks, gwp)


def _mix(x, target, attn, u, modrows, bmodrows, w_pool, pool_scale, w_out):
    seq = x.shape[0]
    n_tiles = seq // TILE
    rows8 = TILE // HALO
    last8 = (seq + TILE) // HALO - 1

    n_blk = seq // Q_BLOCK
    per = TILE // n_blk
    assert TILE % n_blk == 0 and per % 8 == 0

    def body(x_ref, t_ref, o_hbm, ga_ref, pin_ref, hb_ref, ha_ref, gp_ref, mod_ref, bmod_ref, wp_ref, ps_ref, wo_ref,
             loss_ref, g1_ref, dgate_ref, do_hbm, dga_ref, dgp_ref, dpl_ref, gwob_ref, gwp_ref, dps_ref,
             abuf, dbuf, gwo_ref, rsem, wsem):
        j = pl.program_id(0)

        def slab_copies(tile, slot, fetch):
            out = []
            for b in range(n_blk):
                hbm_rows = pl.ds(b * Q_BLOCK + tile * per, per)
                buf_rows = pl.ds(b * per, per)
                if fetch:
                    out.append(pltpu.make_async_copy(o_hbm.at[hbm_rows], abuf.at[slot, buf_rows], rsem.at[slot]))
                else:
                    out.append(pltpu.make_async_copy(dbuf.at[slot, buf_rows], do_hbm.at[hbm_rows], wsem.at[slot]))
            return out

        def wait_all(slot, fetch):
            if fetch:
                pltpu.make_async_copy(o_hbm.at[pl.ds(0, TILE)], abuf.at[slot], rsem.at[slot]).wait()
            else:
                pltpu.make_async_copy(dbuf.at[slot], do_hbm.at[pl.ds(0, TILE)], wsem.at[slot]).wait()

        slot = lax.rem(j, 2)

        @pl.when(j == 0)
        def _():
            loss_ref[...] = jnp.zeros_like(loss_ref)
            dgate_ref[...] = jnp.zeros_like(dgate_ref)
            gwo_ref[...] = jnp.zeros_like(gwo_ref)
            gwp_ref[...] = jnp.zeros_like(gwp_ref)
            dps_ref[...] = jnp.zeros_like(dps_ref)
            for cp in slab_copies(0, 0, True):
                cp.start()

        @pl.when(j + 1 < n_tiles)
        def _():
            for cp in slab_copies(j + 1, 1 - slot, True):
                cp.start()

        wait_all(slot, True)
        attn_t = jnp.swapaxes(abuf[slot].reshape(n_blk, per, D_ATTN), 0, 1).reshape(TILE, D_ATTN)

        gate = mod_ref[2:3, :] + bmod_ref[2:3, :]
        ga = ga_ref[...]
        sga = _sigmoid(ga)
        silu_ga = ga * sga
        br_a = silu_ga * attn_t

        pin = pin_ref[...]
        xs = jnp.concatenate([jnp.where(j > 0, hb_ref[...], 0.0), pin, jnp.where(j < n_tiles - 1, ha_ref[...], 0.0)], axis=0)
        tpos = j * TILE + lax.broadcasted_iota(jnp.int32, (TILE, 1), 0)
        ps = ps_ref[...]
        pooled, yps = [], []
        for g, w in enumerate(POOL_WINDOWS):
            sl = slice(g * GROUP_DIM, (g + 1) * GROUP_DIM)
            ws = _window_sum(xs[:, sl], w, False)[HALO:HALO + TILE]
            pg = (ws * _inv_count(tpos, w, seq) - pin[:, sl]).astype(BF16)
            pooled.append(pg)
            yps.append(_dot(pg, wp_ref[g].astype(BF16)))
        yp = jnp.concatenate(yps, axis=1)
        ypool = yp * ps
        gp = gp_ref[...]
        sgp = _sigmoid(gp)
        silu_gp = gp * sgp
        br_p = silu_gp * ypool

        z = jnp.concatenate([br_a, br_p], axis=1).astype(BF16)
        y = _dot(z, wo_ref[...])
        res = x_ref[...] + gate * y - t_ref[...]
        loss_ref[...] += jnp.sum(res * res) * (0.5 / D_MODEL)
        dxn = res * (1.0 / D_MODEL)
        g1_ref[...] = dxn
        dgate_ref[...] += jnp.sum(dxn * y, axis=0, keepdims=True)
        dy = (gate * dxn).astype(BF16)
        dz = _dot_nt(dy, wo_ref[...])
        gwo_ref[...] += _dot_tn(z, dy)

        dbra = dz[:, 0:D_ATTN]
        dbrp = dz[:, D_ATTN:]
        dga_ref[...] = dbra * attn_t * (sga * (1.0 + ga * (1.0 - sga)))

        @pl.when(j >= 2)
        def _():
            wait_all(slot, False)
        dbuf[slot] = jnp.swapaxes((dbra * silu_ga).reshape(per, n_blk, D_ATTN), 0, 1).reshape(TILE, D_ATTN)
        for cp in slab_copies(j, slot, False):
            cp.start()

        @pl.when(j == n_tiles - 1)
        def _():
            wait_all(slot, False)
            if n_tiles >= 2:
                wait_all(1 - slot, False)

        dgp_ref[...] = dbrp * ypool * (sgp * (1.0 + gp * (1.0 - sgp)))
        dyp_s = dbrp * silu_gp
        dps_ref[...] += jnp.sum(dyp_s * yp, axis=0, keepdims=True)
        dyp = (dyp_s * ps).astype(BF16)
        for g in range(len(POOL_WINDOWS)):
            sl = slice(g * GROUP_DIM, (g + 1) * GROUP_DIM)
            gwp_ref[g] += _dot_tn(pooled[g], dyp[:, sl])
            dpl_ref[:, sl] = _dot_nt(dyp[:, sl], wp_ref[g].astype(BF16))

        @pl.when(j == n_tiles - 1)
        def _():
            gwob_ref[...] = gwo_ref[...].astype(BF16)

    tile = lambda w: pl.BlockSpec((TILE, w), lambda j: (j, 0))
    ucol = lambda col: pl.BlockSpec((TILE, 512), lambda j: (j + 1, col))
    full = lambda shape: pl.BlockSpec(shape, lambda j: (0,) * len(shape))
    return pl.pallas_call(
        body, name="mix_fwd_bwd", grid=(n_tiles,),
        out_shape=(jax.ShapeDtypeStruct((8, 128), F32), jax.ShapeDtypeStruct((seq, D_MODEL), F32),
                   jax.ShapeDtypeStruct((1, D_MODEL), F32), jax.ShapeDtypeStruct((seq, D_ATTN), F32),
                   jax.ShapeDtypeStruct((seq, D_ATTN), F32), jax.ShapeDtypeStruct((seq, D_POOL), F32),
                   jax.ShapeDtypeStruct((seq, D_POOL), F32), jax.ShapeDtypeStruct((D_MODEL, D_MODEL), BF16),
                   jax.ShapeDtypeStruct((4, GROUP_DIM, GROUP_DIM), F32), jax.ShapeDtypeStruct((1, D_POOL), F32)),
        in_specs=[tile(D_MODEL), tile(D_MODEL), pl.BlockSpec(memory_space=pl.ANY), ucol(0), ucol(1),
                  pl.BlockSpec((HALO, 512), lambda j: ((j + 1) * rows8 - 1, 1)),
                  pl.BlockSpec((HALO, 512), lambda j: (jnp.minimum((j + 2) * rows8, last8), 1)),
                  ucol(2), full((8, D_MODEL)), full((8, D_MODEL)), full((4, GROUP_DIM, GROUP_DIM)), full((1, D_POOL)),
                  full((D_MODEL, D_MODEL))],
        out_specs=(full((8, 128)), tile(D_MODEL), full((1, D_MODEL)), pl.BlockSpec(memory_space=pl.ANY), tile(D_ATTN),
                   tile(D_POOL), tile(D_POOL), full((D_MODEL, D_MODEL)), full((4, GROUP_DIM, GROUP_DIM)), full((1, D_POOL))),
        scratch_shapes=[pltpu.VMEM((2, TILE, D_ATTN), F32), pltpu.VMEM((2, TILE, D_ATTN), F32),
                        pltpu.VMEM((D_MODEL, D_MODEL), F32), pltpu.SemaphoreType.DMA((2,)), pltpu.SemaphoreType.DMA((2,))],
        compiler_params=pltpu.CompilerParams(dimension_semantics=("arbitrary",), vmem_limit_bytes=VMEM_LIMIT),
    )(x, target, attn, u, u, u, u, u, modrows, bmodrows, w_pool, pool_scale, w_out)


def _inproj_bwd(x, ctx, modrows, bmodrows, norm_g, w_in_p, q_lora_g, w_uq_p, kv_lora_g, w_ukv, qng_p, kng_p, cos, slo, shi,
                u, dq, dk, dv, dga, dgp, dpl, g1):
    seq = x.shape[0]
    n_tiles = seq // TILE + 1
    rows8 = TILE // HALO
    last8 = seq // HALO - 1

    def body(x_ref, ctx_ref, mod_ref, bmod_ref, ng_ref, win_ref, qlg_ref, wuq_ref, kvlg_ref, wukv_ref, qng_ref, kng_ref,
             cos_ref, slo_ref, shi_ref, cq_ref, ckv_ref, kr_ref, dq_ref, dk_ref, dv_ref, dga_ref, dgp_ref, dpl_ref,
             hb_ref, ha_ref, g1_ref,
             gx_ref, gwin_ref, gwuq_ref, gwukv_ref, dqlg_ref, dkvlg_ref, dqng_ref, dkng_ref, dng_ref, dmod_ref,
             du_ref):
        i = pl.program_id(0)
        is_lat = i > 0

        @pl.when(i == 0)
        def _():
            for ref in (gwin_ref, gwuq_ref, gwukv_ref, dqlg_ref, dkvlg_ref, dqng_ref, dkng_ref, dng_ref, dmod_ref):
                ref[...] = jnp.zeros_like(ref)

        xt, r, xg, h, scale = _modulated_input(i, x_ref, ctx_ref, mod_ref, bmod_ref, ng_ref)
        hb = h.astype(BF16)
        cos_t, slo_t, shi_t = cos_ref[...], slo_ref[...], shi_ref[...]

        cq = cq_ref[...]
        rq = lax.rsqrt(jnp.mean(cq * cq, axis=-1, keepdims=True) + NORM_EPS)
        qhat = cq * rq
        qnb = (qhat * qlg_ref[...]).astype(BF16)
        q = _dot_nt(qnb, wuq_ref[...])
        qg = qng_ref[...]
        dq_heads = []
        dqng = jnp.zeros((1, D_HEAD_PAD), F32)
        for hd in range(N_HEADS):
            qh = q[:, hd * D_HEAD_PAD:(hd + 1) * D_HEAD_PAD]
            rr = lax.rsqrt(jnp.sum(qh * qh, axis=-1, keepdims=True) * (1.0 / D_HEAD) + NORM_EPS)
            yq = qh * rr
            dpost = jnp.where(is_lat, dq_ref[hd] * ATTN_SCALE, 0.0)
            dyn = jnp.concatenate([dpost[:, 0:D_NOPE], _rope_t(dpost[:, D_NOPE:], cos_t, slo_t, shi_t)], axis=1)
            dqng = dqng + jnp.sum(dyn * yq, axis=0, keepdims=True)
            dyg = dyn * qg
            dq_heads.append(rr * (dyg - yq * (jnp.sum(dyg * yq, axis=-1, keepdims=True) * (1.0 / D_HEAD))))
        dqng_ref[...] += dqng
        dqf = jnp.concatenate(dq_heads, axis=1).astype(BF16)
        gwuq_ref[...] += _dot_tn(dqf, qnb)
        dqn = _dot(dqf, wuq_ref[...])
        dqlg_ref[...] += jnp.sum(dqn * qhat, axis=0, keepdims=True)
        dqhat = dqn * qlg_ref[...]
        dcq = rq * (dqhat - qhat * jnp.mean(dqhat * qhat, axis=-1, keepdims=True))

        ckv = ckv_ref[...]
        rkv = lax.rsqrt(jnp.mean(ckv * ckv, axis=-1, keepdims=True) + NORM_EPS)
        kvhat = ckv * rkv
        kvnb = (kvhat * kvlg_ref[...]).astype(BF16)
        kv = _dot(kvnb, wukv_ref[...])
        krz = kr_ref[...]
        kr_ss = jnp.sum(krz * krz, axis=-1, keepdims=True)
        kg = kng_ref[...]
        kg_n, kg_r = kg[:, 0:D_NOPE], kg[:, D_NOPE:]
        dkv_parts = []
        dkr = jnp.zeros((TILE, 128), F32)
        dkng_n = jnp.zeros((1, D_NOPE), F32)
        dkng_r = jnp.zeros((1, 128), F32)
        for hd in range(N_HEADS):
            kn = kv[:, hd * 256:hd * 256 + D_NOPE]
            rr = lax.rsqrt((jnp.sum(kn * kn, axis=-1, keepdims=True) + kr_ss) * (1.0 / D_HEAD) + NORM_EPS)
            yn, yr = kn * rr, krz * rr
            dk_h = jnp.transpose(dk_ref[hd]) * LN_2
            dpn = dk_h[:, 0:D_NOPE]
            dpr = _rope_t(dk_h[:, D_NOPE:], cos_t, slo_t, shi_t)
            dkng_n = dkng_n + jnp.sum(dpn * yn, axis=0, keepdims=True)
            dkng_r = dkng_r + jnp.sum(dpr * yr, axis=0, keepdims=True)
            dyn, dyr = dpn * kg_n, dpr * kg_r
            proj = (jnp.sum(dyn * yn, axis=-1, keepdims=True) + jnp.sum(dyr * yr, axis=-1, keepdims=True)) * (1.0 / D_HEAD)
            dkv_parts.append(rr * (dyn - yn * proj))
            dkv_parts.append(jnp.transpose(dv_ref[hd]))
            dkr = dkr + rr * (dyr - yr * proj)
        dkng_ref[:, 0:D_NOPE] += dkng_n
        dkng_ref[:, D_NOPE:] += dkng_r
        dkvf = jnp.concatenate(dkv_parts, axis=1).astype(BF16)
        gwukv_ref[...] += _dot_tn(dkvf, kvnb)
        dkvn = _dot_nt(dkvf, wukv_ref[...])
        dkvlg_ref[...] += jnp.sum(dkvn * kvhat, axis=0, keepdims=True)
        dkvhat = dkvn * kvlg_ref[...]
        dckv = rkv * (dkvhat - kvhat * jnp.mean(dkvhat * kvhat, axis=-1, keepdims=True))

        lat_t = jnp.maximum(i - 1, 0)
        dpl_t = dpl_ref[...]
        ds = jnp.concatenate([jnp.where(i > 1, hb_ref[...], 0.0), dpl_t,
                              jnp.where(i < n_tiles - 1, ha_ref[...], 0.0)], axis=0)
        tpos = lat_t * TILE - HALO + lax.broadcasted_iota(jnp.int32, (TILE + 2 * HALO, 1), 0)
        for g, w in enumerate(POOL_WINDOWS):
            sl = slice(g * GROUP_DIM, (g + 1) * GROUP_DIM)
            wsum = _window_sum(ds[:, sl] * _inv_count(tpos, w, seq), w, True)[HALO:HALO + TILE]
            du_ref[:, O_PIN + g * GROUP_DIM:O_PIN + (g + 1) * GROUP_DIM] = jnp.where(
                is_lat, wsum - dpl_t[:, sl], 0.0).astype(BF16)

        du_ref[:, O_GA:O_GA + D_ATTN] = jnp.where(is_lat, dga_ref[...], 0.0).astype(BF16)
        du_ref[:, O_GP:O_GP + D_POOL] = jnp.where(is_lat, dgp_ref[...], 0.0).astype(BF16)
        du_ref[:, O_CQ:O_CQ + R_Q] = dcq.astype(BF16)
        du_ref[:, O_CKV:O_CKV + R_KV] = dckv.astype(BF16)
        du_ref[:, O_KR:U_PAD] = dkr.astype(BF16)
        dub = du_ref[...]
        dh = _dot(dub, win_ref[...])
        gwin_ref[...] += _dot_tn(dub, hb)

        dshift = jnp.sum(dh, axis=0, keepdims=True)
        dscale = jnp.sum(dh * xg, axis=0, keepdims=True)
        zero = jnp.zeros_like(dshift)
        dmod_ref[0:1, :] += jnp.where(is_lat, dshift, zero)
        dmod_ref[1:2, :] += jnp.where(is_lat, dscale, zero)
        dmod_ref[2:3, :] += jnp.where(is_lat, zero, dshift)
        dmod_ref[3:4, :] += jnp.where(is_lat, zero, dscale)
        dxg = dh * (1.0 + scale)
        xr = xt * r
        dng_ref[...] += jnp.sum(dxg * xr, axis=0, keepdims=True)
        dxn = dxg * ng_ref[...]
        gx_ref[...] = g1_ref[...] + r * (dxn - xr * jnp.mean(dxn * xr, axis=-1, keepdims=True))

    lat = lambda i: (jnp.maximum(i - 1, 0), 0)
    full = lambda shape: pl.BlockSpec(shape, lambda i: (0,) * len(shape))
    rope_spec = pl.BlockSpec((TILE, 128), lambda i: (i, 0))
    return pl.pallas_call(
        body, name="inproj_bwd", grid=(n_tiles,),
        out_shape=(jax.ShapeDtypeStruct((seq, D_MODEL), F32), jax.ShapeDtypeStruct((U_PAD, D_MODEL), F32),
                   jax.ShapeDtypeStruct((N_HEADS * D_HEAD_PAD, R_Q), F32), jax.ShapeDtypeStruct((N_HEADS * 256, R_KV), F32),
                   jax.ShapeDtypeStruct((1, R_Q), F32), jax.ShapeDtypeStruct((1, R_KV), F32),
                   jax.ShapeDtypeStruct((1, D_HEAD_PAD), F32), jax.ShapeDtypeStruct((1, D_HEAD_PAD), F32),
                   jax.ShapeDtypeStruct((1, D_MODEL), F32), jax.ShapeDtypeStruct((8, D_MODEL), F32)),
        in_specs=[pl.BlockSpec((TILE, D_MODEL), lat), full((TILE, D_MODEL)), full((8, D_MODEL)), full((8, D_MODEL)),
                  full((1, D_MODEL)), full((U_PAD, D_MODEL)), full((1, R_Q)), full((N_HEADS * D_HEAD_PAD, R_Q)),
                  full((1, R_KV)), full((R_KV, N_HEADS * 256)), full((1, D_HEAD_PAD)), full((1, D_HEAD_PAD)),
                  rope_spec, rope_spec, rope_spec,
                  pl.BlockSpec((TILE, R_Q), lambda i: (i, O_CQ // R_Q)),
                  pl.BlockSpec((TILE, R_KV), lambda i: (i, O_CKV // R_KV)),
                  pl.BlockSpec((TILE, 128), lambda i: (i, O_KR // 128)),
                  pl.BlockSpec((N_HEADS, TILE, D_HEAD_PAD), lambda i: (0, jnp.maximum(i - 1, 0), 0)),
                  pl.BlockSpec((N_HEADS, D_HEAD_PAD, TILE), lambda i: (0, 0, i)),
                  pl.BlockSpec((N_HEADS, D_V, TILE), lambda i: (0, 0, i)),
                  pl.BlockSpec((TILE, D_ATTN), lat), pl.BlockSpec((TILE, D_POOL), lat), pl.BlockSpec((TILE, D_POOL), lat),
                  pl.BlockSpec((HALO, D_POOL), lambda i: (jnp.maximum((i - 1) * rows8 - 1, 0), 0)),
                  pl.BlockSpec((HALO, D_POOL), lambda i: (jnp.minimum(jnp.maximum(i, 1) * rows8, last8), 0)),
                  pl.BlockSpec((TILE, D_MODEL), lat)],
        out_specs=(pl.BlockSpec((TILE, D_MODEL), lat), full((U_PAD, D_MODEL)), full((N_HEADS * D_HEAD_PAD, R_Q)),
                   full((N_HEADS * 256, R_KV)), full((1, R_Q)), full((1, R_KV)), full((1, D_HEAD_PAD)),
                   full((1, D_HEAD_PAD)), full((1, D_MODEL)), full((8, D_MODEL))),
        scratch_shapes=[pltpu.VMEM((TILE, U_PAD), BF16)],
        compiler_params=pltpu.CompilerParams(dimension_semantics=("arbitrary",), vmem_limit_bytes=VMEM_LIMIT),
    )(x, ctx, modrows, bmodrows, norm_g, w_in_p, q_lora_g, w_uq_p, kv_lora_g, w_ukv, qng_p, kng_p, cos, slo, shi,
      u, u, u, dq, dk, dv, dga, dgp, dpl, dpl, dpl, g1)


SMALL_LAYOUT = ((0, D_MODEL), (1, R_Q), (2, R_KV), (3, D_HEAD_PAD), (4, D_HEAD_PAD), (5, D_POOL))
ROW_DMOD_B, ROW_DMOD_C, ROW_LOSS = 6, 9, 12


def _epilogue(parts, landed, smalls, dmod4, dgate, loss_acc, w_mod_l):
    n_a, n_l, n_s = len(parts), len(landed), len(smalls)
    n_mod = w_mod_l.shape[1]
    blk = [p.shape[1:] for p in parts]
    small_out = [D_MODEL, R_Q, R_KV, D_HEAD, D_HEAD, D_POOL]

    def body(*refs):
        part_refs = refs[0:n_a]
        land_refs = refs[n_a:n_a + n_l]
        small_in = refs[n_a + n_l:n_a + n_l + n_s]
        dmod4_ref, dgate_ref, loss_ref, wmod_ref = refs[n_a + n_l + n_s:n_a + n_l + n_s + 4]
        outs = refs[n_a + n_l + n_s + 4:]
        red_refs = outs[0:n_a]
        landsum_refs = outs[n_a:n_a + n_l]
        ccg_ref = outs[n_a + n_l]
        sum_refs = outs[n_a + n_l + 1:n_a + n_l + 1 + n_s]
        gbmod_ref, dmy_ref, lossout_ref = outs[n_a + n_l + 1 + n_s:n_a + n_l + 4 + n_s]
        scr = outs[n_a + n_l + 4 + n_s:]
        recv1, own1, sum1b, recv2 = scr[0:n_a], scr[n_a:2 * n_a], scr[2 * n_a:3 * n_a], scr[3 * n_a:4 * n_a]
        small_ref, smallg_ref, tot_ref, cc_ref = scr[4 * n_a:4 * n_a + 4]
        ssem1, rsem1, lsem, ssem2, rsem2, ssem_s, rsem_s, ssem_cc, rsem_cc = scr[4 * n_a + 4:]
        x, y, c = _coords()
        me = (x, y, c)
        sibling = (x, y, 1 - c)

        stage1, own_copies = [], []
        for a in range(n_a):
            for b in range(4):
                dev = 4 * (b >> 1) + 2 * (b & 1)
                stage1.append(pltpu.make_async_remote_copy(
                    src_ref=part_refs[a].at[dev + (1 - c)], dst_ref=recv1[a].at[b],
                    send_sem=ssem1.at[a, b], recv_sem=rsem1.at[a, b], device_id=sibling, device_id_type=MESH))
                own_copies.append(pltpu.make_async_copy(part_refs[a].at[dev + c], own1[a].at[b], lsem.at[a, b]))
                stage1[-1].start()
                own_copies[-1].start()

        small_ref[...] = jnp.zeros_like(small_ref)
        for ref, (row, width) in zip(small_in, SMALL_LAYOUT):
            small_ref[row:row + 1, 0:width] = ref[...]
        small_ref[ROW_DMOD_B:ROW_DMOD_B + 2, :] = dmod4_ref[0:2, :]
        small_ref[ROW_DMOD_B + 2:ROW_DMOD_B + 3, :] = dgate_ref[...]
        small_ref[ROW_DMOD_C:ROW_DMOD_C + 2, :] = dmod4_ref[2:4, :]
        small_ref[ROW_LOSS:ROW_LOSS + 1, 0:128] = loss_ref[0:1, :]
        smallg_ref[_lin(me)] = small_ref[...]
        s_recv, s_send = _gather_to_all(small_ref, smallg_ref, ssem_s, rsem_s)
        s_recv()
        tot = smallg_ref[0]
        for d in range(1, N_DEV):
            tot = tot + smallg_ref[d]
        tot_ref[...] = tot
        for ref, (row, _), width in zip(sum_refs, SMALL_LAYOUT, small_out):
            ref[...] = tot_ref[row:row + 1, 0:width]
        lossout_ref[...] = tot_ref[8:16, 0:128]
        lin = _lin(me)

        def my_columns(three_rows):
            v = jnp.concatenate(three_rows, axis=1)
            out = jnp.zeros((1, n_mod), F32)
            for d in range(N_DEV):
                out = out + jnp.where(lin == d, v[:, d * n_mod:(d + 1) * n_mod], 0.0)
            return out

        rows3 = lambda ref, r0: [ref[r0 + t:r0 + t + 1, :] for t in range(3)]
        dmodc = rows3(tot_ref, ROW_DMOD_C)
        gbmod_ref[...] = jnp.concatenate(rows3(tot_ref, ROW_DMOD_B), axis=1) + jnp.concatenate(dmodc, axis=1)
        dmy_ref[...] = jnp.zeros_like(dmy_ref)
        for b in range(N_DEV):
            dmy_ref[b:b + 1, :] = my_columns(rows3(smallg_ref.at[b], ROW_DMOD_B))
        dmy = my_columns(dmodc)
        dmy_ref[N_DEV:N_DEV + 1, :] = dmy
        part = lax.dot_general(jnp.broadcast_to(dmy, (8, n_mod)), wmod_ref[...], (((1,), (1,)), ((), ())),
                               precision=HIGHEST, preferred_element_type=F32)

        cc_ref[...] = part
        ccg_ref[_lin(me)] = part
        cc_recv, cc_send = _gather_to_all(cc_ref, ccg_ref, ssem_cc, rsem_cc)

        stage2 = []
        for a in range(n_a):
            for b in range(4):
                stage1[4 * a + b].wait_recv()
                own_copies[4 * a + b].wait()
                sum1b[a][b] = (own1[a][b] + recv1[a][b]).astype(BF16)
            for k in (1, 2, 3):
                tx, ty = _flip(x, (k >> 1) & 1), _flip(y, k & 1)
                stage2.append(pltpu.make_async_remote_copy(
                    src_ref=sum1b[a].at[2 * tx + ty], dst_ref=recv2[a].at[k - 1], send_sem=ssem2.at[a, k - 1],
                    recv_sem=rsem2.at[a, k - 1], device_id=(tx, ty, c), device_id_type=MESH))
                stage2[-1].start()
        for a in range(n_a):
            own = own1[a][2 * x + y] + recv1[a][2 * x + y]
            for k in (1, 2, 3):
                stage2[3 * a + k - 1].wait_recv()
                own = own + recv2[a][k - 1].astype(F32)
            red_refs[a][...] = own

        for ref, out in zip(land_refs, landsum_refs):
            acc = ref[0].astype(F32)
            for d in range(1, N_DEV):
                acc = acc + ref[d].astype(F32)
            out[...] = acc
        cc_recv()
        for cp in stage1 + stage2:
            cp.wait_send()
        s_send()
        cc_send()

    vm = pl.BlockSpec(memory_space=pltpu.VMEM)
    sds = jax.ShapeDtypeStruct
    return pl.pallas_call(
        body, name="epilogue_reduce",
        out_shape=(*[sds(s, F32) for s in blk], *[sds(a.shape[1:], F32) for a in landed], sds((N_DEV, 8, D_MODEL), F32),
                   *[sds((1, w), F32) for w in small_out], sds((1, 3 * D_MODEL), F32), sds((16, n_mod), F32),
                   sds((8, 128), F32)),
        in_specs=[pl.BlockSpec(memory_space=pl.ANY)] * n_a + [vm] * (n_l + n_s + 4),
        out_specs=tuple([vm] * (n_a + n_l + n_s + 4)),
        scratch_shapes=[*[pltpu.VMEM((4,) + s, F32) for s in blk], *[pltpu.VMEM((4,) + s, F32) for s in blk],
                        *[pltpu.VMEM((4,) + s, BF16) for s in blk], *[pltpu.VMEM((3,) + s, BF16) for s in blk],
                        pltpu.VMEM((SMALL_ROWS, D_MODEL), F32), pltpu.VMEM((N_DEV, SMALL_ROWS, D_MODEL), F32),
                        pltpu.VMEM((SMALL_ROWS, D_MODEL), F32), pltpu.VMEM((8, D_MODEL), F32),
                        pltpu.SemaphoreType.DMA((n_a, 4)), pltpu.SemaphoreType.DMA((n_a, 4)), pltpu.SemaphoreType.DMA((n_a, 4)),
                        pltpu.SemaphoreType.DMA((n_a, 3)), pltpu.SemaphoreType.DMA((n_a, 3)),
                        pltpu.SemaphoreType.DMA((7,)), pltpu.SemaphoreType.DMA((7,)),
                        pltpu.SemaphoreType.DMA((7,)), pltpu.SemaphoreType.DMA((7,))],
        compiler_params=pltpu.CompilerParams(vmem_limit_bytes=VMEM_LIMIT),
    )(*parts, *landed, *smalls, dmod4, dgate, loss_acc, w_mod_l)


def _adamw(weights, grads, ms, vs, c_all_t, dmod_my, p2g):
    n = len(weights)

    def body(*refs):
        w_refs = refs[0:n]
        g_in = refs[n:2 * n - 2]
        m_refs = refs[2 * n - 2:3 * n - 2]
        v_refs = refs[3 * n - 2:4 * n - 2]
        cat_ref, dmod_ref, p2g_ref = refs[4 * n - 2:4 * n + 1]
        outs = refs[4 * n + 1:]
        gcc_ref, gwm_ref = outs[0], outs[1]
        d_refs, nm_refs, nv_refs = outs[2:2 + n], outs[2 + n:2 + 2 * n], outs[2 + 2 * n:2 + 3 * n]

        part = p2g_ref[0, 0:1, :]
        for d in range(1, N_DEV):
            part = part + p2g_ref[d, 0:1, :]
        cc = w_refs[0][...]
        sg = _sigmoid(cc)
        gcc_ref[...] = part * (sg * (1.0 + cc * (1.0 - sg)))
        cat = cat_ref[...]
        gwm_ref[...] = lax.dot_general(cat * _sigmoid(cat), dmod_ref[...], (((1,), (0,)), ((), ())), precision=HIGHEST,
                                       preferred_element_type=F32)
        for idx in range(n):
            g = (gcc_ref, gwm_ref)[idx][...] if idx < 2 else g_in[idx - 2][...]
            m = ADAM_B1 * m_refs[idx][...] + (1.0 - ADAM_B1) * g
            v = ADAM_B2 * v_refs[idx][...] + (1.0 - ADAM_B2) * (g * g)
            m_hat = m / (1.0 - ADAM_B1 ** ADAM_STEP)
            v_hat = v / (1.0 - ADAM_B2 ** ADAM_STEP)
            d_refs[idx][...] = -ADAM_LR * (m_hat / (jnp.sqrt(v_hat) + ADAM_EPS) + ADAM_WD * w_refs[idx][...])
            nm_refs[idx][...] = m
            nv_refs[idx][...] = v

    vm = pl.BlockSpec(memory_space=pltpu.VMEM)
    shapes = [jax.ShapeDtypeStruct(w.shape, F32) for w in weights]
    args = list(weights) + list(grads[2:]) + list(ms) + list(vs) + [c_all_t, dmod_my, p2g]
    out_shape = tuple(shapes[0:2] + shapes * 3)
    return pl.pallas_call(
        body, name="adamw_update", out_shape=out_shape, in_specs=[vm] * len(args), out_specs=tuple([vm] * len(out_shape)),
        compiler_params=pltpu.CompilerParams(vmem_limit_bytes=VMEM_LIMIT),
    )(*args)


def _rope_tables(seq):
    rows = seq // GRID_W
    row = np.repeat(np.arange(rows, dtype=np.float32), GRID_W)
    col = np.tile(np.arange(GRID_W, dtype=np.float32), rows)
    n_freq = D_ROPE // 4
    inv = (np.float32(ROPE_BASE) ** (-np.arange(n_freq, dtype=np.float32) / np.float32(n_freq))).astype(np.float32)
    ang_r = (row[:, None] * inv).astype(np.float32)
    ang_c = (col[:, None] * inv).astype(np.float32)
    ang = np.concatenate([ang_r, ang_r, ang_c, ang_c], axis=-1)
    cos, sin = np.cos(ang).astype(np.float32), np.sin(ang).astype(np.float32)
    low = (np.arange(D_ROPE) % 32) < 16

    def table(t, fill):
        out = np.full((TILE + seq, 128), fill, np.float32)
        out[TILE:, 0:D_ROPE] = t
        return jnp.asarray(out)

    return table(cos, 1.0), table(np.where(low, -sin, 0.0), 0.0), table(np.where(low, 0.0, sin), 0.0)


def kernel(x, c, ctx, c_ctx, w_mod, b_mod, norm_g, w_in, q_lora_g, w_uq, kv_lora_g, w_ukv, q_norm_g, k_norm_g, w_pool, pool_scale, w_out, loss_target, m_c_ctx, m_w_mod, m_b_mod, m_norm_g, m_w_in, m_q_lora_g, m_w_uq, m_kv_lora_g, m_w_ukv, m_q_norm_g, m_k_norm_g, m_w_pool, m_pool_scale, m_w_out, v_c_ctx, v_w_mod, v_b_mod, v_norm_g, v_w_in, v_q_lora_g, v_w_uq, v_kv_lora_g, v_w_ukv, v_q_norm_g, v_k_norm_g, v_w_pool, v_pool_scale, v_w_out):
    seq = x.shape[1]
    assert ctx.shape[1] == TILE and seq % TILE == 0 and seq % GRID_W == 0
    ix, iy, ic = lax.axis_index("x"), lax.axis_index("y"), lax.axis_index("c")
    me = 4 * ix + 2 * iy + ic
    x2, ctx2, tgt2 = x[0], ctx[0], loss_target[0]
    w_mod_l, w_ukv_l, w_out_l = w_mod[0], w_ukv[0], w_out[0]
    c_ctx_row = c_ctx.reshape(1, D_MODEL)

    tr = lambda a: jnp.transpose(a[0])
    w_in_tl, w_uq_tl = tr(w_in), tr(w_uq)
    cg, modg, wg_in, wg_uq, wg_ukv = _prologue(
        jnp.broadcast_to(c, (8, D_MODEL)), jnp.broadcast_to(c_ctx_row, (8, D_MODEL)), w_mod_l,
        [w_in_tl, w_uq_tl, w_ukv_l])
    mod_all = jnp.transpose(modg, (1, 0, 2)).reshape(16, 3 * D_MODEL)
    mod_b = lax.dynamic_slice_in_dim(mod_all, me, 1, axis=0).reshape(3, D_MODEL)
    mod_c = mod_all[8].reshape(3, D_MODEL)
    modrows = jnp.concatenate([mod_b, mod_c[0:2], jnp.zeros((3, D_MODEL), F32)], axis=0)
    b3 = b_mod.reshape(3, D_MODEL)
    bmodrows = jnp.concatenate([b3, b3[0:2], jnp.zeros((3, D_MODEL), F32)], axis=0)

    w_in_t = wg_in.reshape(D_IN_PROJ, D_MODEL)
    w_in_p = jnp.concatenate([w_in_t[448:], w_in_t[0:384], w_in_t[384:448],
                              jnp.zeros((U_PAD - D_IN_PROJ, D_MODEL), BF16)], axis=0)
    w_uq_p = jnp.concatenate([wg_uq.reshape(N_HEADS, D_HEAD, R_Q), jnp.zeros((N_HEADS, D_HEAD_PAD - D_HEAD, R_Q), BF16)],
                             axis=1).reshape(N_HEADS * D_HEAD_PAD, R_Q)
    w_ukv_f = jnp.transpose(wg_ukv, (1, 0, 2)).reshape(R_KV, N_HEADS * 256)
    qng_p = jnp.concatenate([q_norm_g, jnp.zeros((1, D_HEAD_PAD - D_HEAD), F32)], axis=1)
    kng_p = jnp.concatenate([k_norm_g, jnp.zeros((1, D_HEAD_PAD - D_HEAD), F32)], axis=1)
    cos, slo, shi = _rope_tables(seq)

    u, q, k, v = _inproj_fwd(x2, ctx2, modrows, bmodrows, norm_g, w_in_p, q_lora_g, w_uq_p, kv_lora_g, w_ukv_f,
                             qng_p, kng_p, cos, slo, shi)
    attn, lse, wg_out = _attn_fwd(q, k, v, min(1024, seq), w_out_l.astype(BF16))
    (loss_acc, g1, dgate, d_attn, dga, dgp, dpl, gwo_b, gwp, dps) = _mix(
        x2, tgt2, attn, u, modrows, bmodrows, w_pool[0], pool_scale, wg_out.reshape(D_MODEL, D_MODEL))

    blocks = lambda a: a.reshape((N_DEV, a.shape[0] // N_DEV) + a.shape[1:])
    dq, dk, dv, land_wo, land_wp = _attn_bwd(q, k, v, attn, lse, d_attn, min(1024, seq), blocks(gwo_b),
                                             gwp.reshape(4 * GROUP_DIM, GROUP_DIM))
    (grad_x, gwin_p, gwuq_p, gwukv_t, dqlg, dkvlg, dqng, dkng, dng, dmod4) = _inproj_bwd(
        x2, ctx2, modrows, bmodrows, norm_g, w_in_p, q_lora_g, w_uq_p, kv_lora_g, w_ukv_f, qng_p, kng_p, cos, slo, shi,
        u, dq, dk, dv, dga, dgp, dpl, g1)

    gwin_t = jnp.concatenate([gwin_p[O_CQ:O_KR], gwin_p[O_KR:D_IN_PROJ], gwin_p[0:O_CQ]], axis=0)
    gwuq_t = gwuq_p.reshape(N_HEADS, D_HEAD_PAD, R_Q)[:, 0:D_HEAD].reshape(N_HEADS * D_HEAD, R_Q)
    parts = [blocks(gwin_t), blocks(gwuq_t), blocks(gwukv_t)]
    (r_win, r_wuq, r_wukv, r_wout, g_w_pool, ccg, g_ng, g_qlg, g_kvlg, g_qng, g_kng, g_ps, g_bmod, dmod_my,
     loss_rows) = _epilogue(parts, [land_wo, land_wp], [dng, dqlg, dkvlg, dqng, dkng, dps], dmod4, dgate, loss_acc, w_mod_l)

    g_w_ukv = jnp.transpose(r_wukv)
    c_rows = cg[:, 0, :]
    c_all_t = jnp.transpose(jnp.concatenate([c_rows, c_ctx_row, jnp.zeros((16 - N_DEV - 1, D_MODEL), F32)], axis=0))

    weights = [c_ctx_row, w_mod_l, b_mod, norm_g, w_in_tl, q_lora_g, w_uq_tl, kv_lora_g, w_ukv_l, q_norm_g, k_norm_g,
               w_pool.reshape(4 * GROUP_DIM, GROUP_DIM), pool_scale, w_out_l]
    grads = [None, None, g_bmod, g_ng, r_win, g_qlg, r_wuq, g_kvlg, g_w_ukv, g_qng, g_kng, g_w_pool, g_ps, r_wout]
    ms = [m_c_ctx.reshape(1, D_MODEL), m_w_mod[0], m_b_mod, m_norm_g, tr(m_w_in), m_q_lora_g, tr(m_w_uq), m_kv_lora_g,
          m_w_ukv[0], m_q_norm_g, m_k_norm_g, m_w_pool.reshape(4 * GROUP_DIM, GROUP_DIM), m_pool_scale, m_w_out[0]]
    vs = [v_c_ctx.reshape(1, D_MODEL), v_w_mod[0], v_b_mod, v_norm_g, tr(v_w_in), v_q_lora_g, tr(v_w_uq), v_kv_lora_g,
          v_w_ukv[0], v_q_norm_g, v_k_norm_g, v_w_pool.reshape(4 * GROUP_DIM, GROUP_DIM), v_pool_scale, v_w_out[0]]
    outs = _adamw(weights, grads, ms, vs, c_all_t, dmod_my, ccg)
    grads[0], grads[1] = outs[0], outs[1]
    n = len(weights)
    deltas, new_m, new_v = outs[2:2 + n], outs[2 + n:2 + 2 * n], outs[2 + 2 * n:2 + 3 * n]

    loss = loss_rows[ROW_LOSS - 8, 0]
    final = [c_ctx.shape, w_mod.shape, b_mod.shape, norm_g.shape, w_in.shape, q_lora_g.shape, w_uq.shape, kv_lora_g.shape,
             w_ukv.shape, q_norm_g.shape, k_norm_g.shape, w_pool.shape, pool_scale.shape, w_out.shape]
    transposed = (4, 6)

    def shaped(arrs):
        return [(jnp.transpose(a) if i in transposed else a).reshape(s) for i, (a, s) in enumerate(zip(arrs, final))]

    return (loss, grad_x[None], *shaped(grads), *shaped(deltas), *shaped(new_m), *shaped(new_v))
```

```python
import numpy as np

import jax
import jax.numpy as jnp
from jax import lax
from jax.experimental import pallas as pl
from jax.experimental.pallas import tpu as pltpu

F32 = jnp.float32
BF16 = jnp.bfloat16
MESH = pl.DeviceIdType.MESH
HIGHEST = lax.Precision.HIGHEST

D_MODEL = 1024
N_HEADS = 4
D_NOPE = 128
D_ROPE = 64
D_HEAD = D_NOPE + D_ROPE
D_HEAD_PAD = 256
D_V = 128
R_Q = 256
R_KV = 128
D_ATTN = 512
D_POOL = 512
POOL_WINDOWS = (2, 4, 8, 16)
GROUP_DIM = 128
GRID_W = 64
ROPE_BASE = 10000.0
NORM_EPS = 1e-6
ATTN_SCALE = D_HEAD ** -0.5
LOG2_E = 1.4426950408889634
LN_2 = 0.6931471805599453
ATTN_ROWS = 256
D_IN_PROJ = 1984
U_PAD = 2048
O_GA, O_PIN, O_GP, O_CQ, O_CKV, O_KR = 0, 512, 1024, 1536, 1792, 1920
TILE = 256
Q_BLOCK = 128
HALO = 8
N_DEV = 8
VMEM_LIMIT = 56 * 1024 * 1024

ADAM_LR = 0.001
ADAM_B1 = 0.9
ADAM_B2 = 0.999
ADAM_EPS = 1e-08
ADAM_WD = 0.01
ADAM_STEP = 10

SMALL_ROWS = 16


def _dot(a, b):
    return lax.dot_general(a, b, (((1,), (0,)), ((), ())), preferred_element_type=F32)


def _dot_nt(a, b):
    return lax.dot_general(a, b, (((1,), (1,)), ((), ())), preferred_element_type=F32)


def _dot_tn(a, b):
    return lax.dot_general(a, b, (((0,), (0,)), ((), ())), preferred_element_type=F32)


def _sigmoid(x):
    return 1.0 / (1.0 + jnp.exp(-x))


def _rope(p, cos, slo, shi):
    return p * cos + pltpu.roll(p, 112, 1) * slo + pltpu.roll(p, 16, 1) * shi


def _rope_t(d, cos, slo, shi):
    return d * cos + pltpu.roll(d * slo, 16, 1) + pltpu.roll(d * shi, 112, 1)


def _shift_rows(a, k):
    n = a.shape[0]
    return pltpu.roll(a, (-k) % n, 0)


def _window_sum(x, w, transposed):
    s = (x + _shift_rows(x, 1)) if transposed else (_shift_rows(x, -1) + x)
    step = 1
    while 2 * step < w:
        s = _shift_rows(s, -step) + _shift_rows(s, step)
        step *= 2
    return s


def _inv_count(tpos, w, seq):
    lo = jnp.maximum(tpos - w // 2, 0)
    hi = jnp.minimum(tpos - w // 2 + w, seq)
    return 1.0 / jnp.maximum(hi - lo, 1).astype(F32)


def _coords():
    return lax.axis_index("x"), lax.axis_index("y"), lax.axis_index("c")


def _flip(v, bit):
    return (1 - v) if bit else v


def _peer(k):
    x, y, c = _coords()
    return (_flip(x, (k >> 2) & 1), _flip(y, (k >> 1) & 1), _flip(c, k & 1))


def _lin(p):
    return 4 * p[0] + 2 * p[1] + p[2]


def _gather_to_all(src_ref, slots_ref, send_sems, recv_sems):
    me = _coords()
    copies = []
    for k in range(1, N_DEV):
        cp = pltpu.make_async_remote_copy(
            src_ref=src_ref, dst_ref=slots_ref.at[_lin(me)], send_sem=send_sems.at[k - 1], recv_sem=recv_sems.at[k - 1],
            device_id=_peer(k), device_id_type=MESH)
        cp.start()
        copies.append(cp)

    def wait_recv():
        for k in range(1, N_DEV):
            pltpu.make_async_remote_copy(
                src_ref=src_ref, dst_ref=slots_ref.at[_lin(_peer(k))], send_sem=send_sems.at[k - 1],
                recv_sem=recv_sems.at[k - 1], device_id=_peer(k), device_id_type=MESH).wait_recv()

    def wait_send():
        for cp in copies:
            cp.wait_send()

    return wait_recv, wait_send


def _prologue(c8, cctx8, w_mod_l, shards):
    n_mod = w_mod_l.shape[1]
    n_w = len(shards)

    def body(c8_ref, cctx_ref, wmod_ref, *refs):
        w_refs = refs[0:n_w]
        cg_ref, modg_ref = refs[n_w], refs[n_w + 1]
        wg_refs = refs[n_w + 2:2 * n_w + 2]
        modblk_ref = refs[2 * n_w + 2]
        wb_refs = refs[2 * n_w + 3:3 * n_w + 3]
        ssem_w, rsem_w, ssem_c, rsem_c, ssem_m, rsem_m = refs[3 * n_w + 3:]
        x, y, c = _coords()
        me = (x, y, c)
        sibling = (x, y, 1 - c)
        chips = [(1 - x, y), (x, 1 - y), (1 - x, 1 - y)]

        def wcopies(k, block, to, own=False):
            out = []
            for a in range(n_w):
                slot = wg_refs[a].at[_lin(block)]
                out.append(pltpu.make_async_remote_copy(
                    src_ref=wb_refs[a] if own else slot, dst_ref=slot, send_sem=ssem_w.at[a, k], recv_sem=rsem_w.at[a, k],
                    device_id=to, device_id_type=MESH))
            return out

        for a in range(n_w):
            wb_refs[a][...] = w_refs[a][...].astype(BF16)
        first = wcopies(0, me, sibling, own=True)
        for j, chip in enumerate(chips):
            first += wcopies(1 + j, me, (*chip, c), own=True)
        for cp in first:
            cp.start()
        for a in range(n_w):
            wg_refs[a][_lin(me)] = wb_refs[a][...]

        cg_ref[_lin(me)] = c8_ref[...]
        c_recv, c_send = _gather_to_all(c8_ref, cg_ref, ssem_c, rsem_c)
        c_recv()
        row = lax.broadcasted_iota(jnp.int32, (8, D_MODEL), 0)
        c_all = jnp.zeros((8, D_MODEL), F32)
        for d in range(N_DEV):
            c_all = c_all + jnp.where(row == d, cg_ref[d], 0.0)
        cc = cctx_ref[...]
        a = jnp.concatenate([c_all * _sigmoid(c_all), cc * _sigmoid(cc)], axis=0)
        modblk_ref[...] = lax.dot_general(a, wmod_ref[...], (((1,), (0,)), ((), ())), precision=HIGHEST,
                                          preferred_element_type=F32)
        modg_ref[_lin(me)] = modblk_ref[...]
        m_recv, m_send = _gather_to_all(modblk_ref, modg_ref, ssem_m, rsem_m)
        m_recv()

        passed = []
        for j, chip in enumerate(chips):
            for cp in wcopies(1 + j, (*chip, c), me):
                cp.wait_recv()
            fwd = wcopies(4 + j, (*chip, c), sibling)
            for cp in fwd:
                cp.start()
            passed += fwd
        for cp in wcopies(0, sibling, me):
            cp.wait_recv()
        for j, chip in enumerate(chips):
            for cp in wcopies(4 + j, (*chip, 1 - c), me):
                cp.wait_recv()
        for cp in first + passed:
            cp.wait_send()
        c_send()
        m_send()

    vm = pl.BlockSpec(memory_space=pltpu.VMEM)
    return pl.pallas_call(
        body, name="prologue_gather",
        out_shape=(jax.ShapeDtypeStruct((N_DEV, 8, D_MODEL), F32), jax.ShapeDtypeStruct((N_DEV, 16, n_mod), F32),
                   *[jax.ShapeDtypeStruct((N_DEV,) + s.shape, BF16) for s in shards]),
        in_specs=[vm] * (3 + n_w), out_specs=tuple([vm] * (2 + n_w)),
        scratch_shapes=[pltpu.VMEM((16, n_mod), F32), *[pltpu.VMEM(s.shape, BF16) for s in shards],
                        pltpu.SemaphoreType.DMA((n_w, 7)), pltpu.SemaphoreType.DMA((n_w, 7)),
                        pltpu.SemaphoreType.DMA((7,)), pltpu.SemaphoreType.DMA((7,)),
                        pltpu.SemaphoreType.DMA((7,)), pltpu.SemaphoreType.DMA((7,))],
        compiler_params=pltpu.CompilerParams(vmem_limit_bytes=VMEM_LIMIT),
    )(c8, cctx8, w_mod_l, *shards)


def _modulated_input(i, x_ref, ctx_ref, mod_ref, bmod_ref, ng_ref):
    is_ctx = i == 0
    xt = jnp.where(is_ctx, ctx_ref[...], x_ref[...])
    shift = jnp.where(is_ctx, mod_ref[3:4, :] + bmod_ref[3:4, :], mod_ref[0:1, :] + bmod_ref[0:1, :])
    scale = jnp.where(is_ctx, mod_ref[4:5, :] + bmod_ref[4:5, :], mod_ref[1:2, :] + bmod_ref[1:2, :])
    r = lax.rsqrt(jnp.mean(xt * xt, axis=-1, keepdims=True) + NORM_EPS)
    xg = (xt * r) * ng_ref[...]
    h = xg * (1.0 + scale) + shift
    return xt, r, xg, h, scale


def _inproj_fwd(x, ctx, modrows, bmodrows, norm_g, w_in_p, q_lora_g, w_uq_p, kv_lora_g, w_ukv, qng_p, kng_p, cos, slo, shi):
    seq = x.shape[0]
    n_tiles = seq // TILE + 1
    tot = seq + TILE

    n_mla = R_Q + R_KV + 128

    def body(x_ref, ctx_ref, mod_ref, bmod_ref, ng_ref, win_ref, qlg_ref, wuq_ref, kvlg_ref, wukv_ref, qng_ref, kng_ref,
             cos_ref, slo_ref, shi_ref, u_ref, q_ref, k_ref, v_ref, stash_even, stash_odd):
        s = pl.program_id(0)

        @pl.when(s == 0)
        def _():
            stash_odd[...] = jnp.zeros_like(stash_odd)

        refs = (x_ref, ctx_ref, mod_ref, bmod_ref, ng_ref, win_ref, qlg_ref, wuq_ref, kvlg_ref, wukv_ref, qng_ref, kng_ref,
                cos_ref, slo_ref, shi_ref, u_ref, q_ref, k_ref, v_ref)

        @pl.when(lax.rem(s, 2) == 0)
        def _():
            stages(s, refs, stash_even, stash_odd)

        @pl.when(lax.rem(s, 2) == 1)
        def _():
            stages(s, refs, stash_odd, stash_even)

    def stages(s, refs, fill, prev_ref):
        (x_ref, ctx_ref, mod_ref, bmod_ref, ng_ref, win_ref, qlg_ref, wuq_ref, kvlg_ref, wukv_ref, qng_ref, kng_ref,
         cos_ref, slo_ref, shi_ref, u_ref, q_ref, k_ref, v_ref) = refs
        cos_t, slo_t, shi_t = cos_ref[...], slo_ref[...], shi_ref[...]
        prev = prev_ref[...]
        cq = prev[:, 0:R_Q]
        qn = cq * lax.rsqrt(jnp.mean(cq * cq, axis=-1, keepdims=True) + NORM_EPS) * qlg_ref[...]
        q = _dot_nt(qn.astype(BF16), wuq_ref[...])
        qg = qng_ref[...] * (ATTN_SCALE * LOG2_E)
        for hd in range(N_HEADS):
            qh = q[:, hd * D_HEAD_PAD:(hd + 1) * D_HEAD_PAD]
            rr = lax.rsqrt(jnp.sum(qh * qh, axis=-1, keepdims=True) * (1.0 / D_HEAD) + NORM_EPS)
            qy = qh * rr * qg
            q_ref[hd, :, 0:D_NOPE] = qy[:, 0:D_NOPE].astype(BF16)
            q_ref[hd, :, D_NOPE:D_HEAD_PAD] = _rope(qy[:, D_NOPE:D_HEAD_PAD], cos_t, slo_t, shi_t).astype(BF16)

        ckv = prev[:, R_Q:R_Q + R_KV]
        kvn = ckv * lax.rsqrt(jnp.mean(ckv * ckv, axis=-1, keepdims=True) + NORM_EPS) * kvlg_ref[...]
        kv = _dot(kvn.astype(BF16), wukv_ref[...])
        krz = prev[:, R_Q + R_KV:n_mla]
        kr_ss = jnp.sum(krz * krz, axis=-1, keepdims=True)
        kg = kng_ref[...]
        for hd in range(N_HEADS):
            kn = kv[:, hd * 256:hd * 256 + D_NOPE]
            rr = lax.rsqrt((jnp.sum(kn * kn, axis=-1, keepdims=True) + kr_ss) * (1.0 / D_HEAD) + NORM_EPS)
            k_ref[hd, :, 0:D_NOPE] = (kn * rr * kg[:, 0:D_NOPE]).astype(BF16)
            k_ref[hd, :, D_NOPE:D_HEAD_PAD] = _rope(krz * rr * kg[:, D_NOPE:D_HEAD_PAD], cos_t, slo_t, shi_t).astype(BF16)
            v_ref[hd] = kv[:, hd * 256 + D_NOPE:(hd + 1) * 256].astype(BF16)

        _, _, _, h, _ = _modulated_input(s, x_ref, ctx_ref, mod_ref, bmod_ref, ng_ref)
        u = _dot_nt(h.astype(BF16), win_ref[...])
        u_ref[...] = u
        fill[:, 0:R_Q + R_KV] = u[:, O_CQ:O_CQ + R_Q + R_KV]
        fill[:, R_Q + R_KV:n_mla] = u[:, O_KR:U_PAD]

    first = lambda s: jnp.minimum(s, n_tiles - 1)
    second = lambda s: jnp.maximum(s - 1, 0)
    latent = lambda t: jnp.maximum(t - 1, 0)
    full = lambda shape: pl.BlockSpec(shape, lambda s: (0,) * len(shape))
    rope_spec = pl.BlockSpec((TILE, 128), lambda s: (second(s), 0))
    return pl.pallas_call(
        body, name="inproj_fwd", grid=(n_tiles + 1,),
        out_shape=(jax.ShapeDtypeStruct((tot, U_PAD), F32),
                   jax.ShapeDtypeStruct((N_HEADS, seq, D_HEAD_PAD), BF16),
                   jax.ShapeDtypeStruct((N_HEADS, tot, D_HEAD_PAD), BF16),
                   jax.ShapeDtypeStruct((N_HEADS, tot, D_V), BF16)),
        in_specs=[pl.BlockSpec((TILE, D_MODEL), lambda s: (latent(first(s)), 0)), full((TILE, D_MODEL)), full((8, D_MODEL)),
                  full((8, D_MODEL)), full((1, D_MODEL)), full((U_PAD, D_MODEL)), full((1, R_Q)),
                  full((N_HEADS * D_HEAD_PAD, R_Q)), full((1, R_KV)), full((R_KV, N_HEADS * 256)), full((1, D_HEAD_PAD)),
                  full((1, D_HEAD_PAD)), rope_spec, rope_spec, rope_spec],
        out_specs=(pl.BlockSpec((TILE, U_PAD), lambda s: (first(s), 0)),
                   pl.BlockSpec((N_HEADS, TILE, D_HEAD_PAD), lambda s: (0, latent(second(s)), 0)),
                   pl.BlockSpec((N_HEADS, TILE, D_HEAD_PAD), lambda s: (0, second(s), 0)),
                   pl.BlockSpec((N_HEADS, TILE, D_V), lambda s: (0, second(s), 0))),
        scratch_shapes=[pltpu.VMEM((TILE, n_mla), F32), pltpu.VMEM((TILE, n_mla), F32)],
        compiler_params=pltpu.CompilerParams(dimension_semantics=("arbitrary",), vmem_limit_bytes=VMEM_LIMIT),
    )(x, ctx, modrows, bmodrows, norm_g, w_in_p, q_lora_g, w_uq_p, kv_lora_g, w_ukv, qng_p, kng_p, cos, slo, shi)


def _hosted_exchange(first, last, items, send_sems, recv_sems, local_sems):
    me = _lin(_coords())

    def remote(n, k, src, land, scatter):
        peer = _peer(k)
        return pltpu.make_async_remote_copy(
            src_ref=src.at[_lin(peer)] if scatter else src, dst_ref=land.at[me], send_sem=send_sems.at[n, k - 1],
            recv_sem=recv_sems.at[n, k - 1], device_id=peer, device_id_type=MESH)

    def local(n, src, land, scatter):
        return pltpu.make_async_copy(src.at[me] if scatter else src, land.at[me], local_sems.at[n])

    @pl.when(first)
    def _():
        for n, (src, land, scatter) in enumerate(items):
            for k in range(1, N_DEV):
                remote(n, k, src, land, scatter).start()
            local(n, src, land, scatter).start()

    @pl.when(last)
    def _():
        for n, (src, land, scatter) in enumerate(items):
            for k in range(1, N_DEV):
                peer = _peer(k)
                pltpu.make_async_remote_copy(
                    src_ref=src.at[0] if scatter else src, dst_ref=land.at[_lin(peer)], send_sem=send_sems.at[n, k - 1],
                    recv_sem=recv_sems.at[n, k - 1], device_id=peer, device_id_type=MESH).wait_recv()
            for k in range(1, N_DEV):
                remote(n, k, src, land, scatter).wait_send()
            local(n, src, land, scatter).wait()


def _attn_fwd(q, k, v, block_q, w_out_b):
    _, seq, _ = q.shape
    n_keys = k.shape[1]
    n_q = seq // block_q

    def body(q_ref, k_ref, v_ref, wo_ref, o_ref, lse_ref, wog_ref, ssem, rsem, lsem):
        h, i = pl.program_id(0), pl.program_id(1)
        _hosted_exchange((h == 0) & (i == 0), (h == N_HEADS - 1) & (i == n_q - 1), [(wo_ref, wog_ref, False)],
                         ssem, rsem, lsem)
        kb, vb = k_ref[0], v_ref[0]
        for r0 in range(0, block_q, ATTN_ROWS):
            rows = slice(r0, r0 + ATTN_ROWS)
            s = _dot_nt(q_ref[0, rows, :], kb)
            m = jnp.max(s, axis=-1, keepdims=True)
            p = jnp.exp2(s - m)
            l = jnp.sum(p, axis=-1, keepdims=True)
            o_ref[rows, :] = _dot(p.astype(BF16), vb) * (1.0 / l)
            lse_ref[0, rows, :] = m + jnp.log2(l)

    hbm = pl.BlockSpec(memory_space=pl.ANY)
    return pl.pallas_call(
        body, name="attn_fwd", grid=(N_HEADS, n_q),
        out_shape=(jax.ShapeDtypeStruct((seq, D_ATTN), F32), jax.ShapeDtypeStruct((N_HEADS, seq, 1), F32),
                   jax.ShapeDtypeStruct((N_DEV,) + w_out_b.shape, w_out_b.dtype)),
        in_specs=[pl.BlockSpec((1, block_q, D_HEAD_PAD), lambda h, i: (h, i, 0)),
                  pl.BlockSpec((1, n_keys, D_HEAD_PAD), lambda h, i: (h, 0, 0)),
                  pl.BlockSpec((1, n_keys, D_V), lambda h, i: (h, 0, 0)), hbm],
        out_specs=(pl.BlockSpec((block_q, D_V), lambda h, i: (i, h)),
                   pl.BlockSpec((1, block_q, 1), lambda h, i: (h, i, 0)), hbm),
        scratch_shapes=[pltpu.SemaphoreType.DMA((1, 7)), pltpu.SemaphoreType.DMA((1, 7)), pltpu.SemaphoreType.DMA((1,))],
        compiler_params=pltpu.CompilerParams(dimension_semantics=("arbitrary", "arbitrary"), vmem_limit_bytes=VMEM_LIMIT),
    )(q, k, v, w_out_b)


def _attn_bwd(q, k, v, o, lse, d_o, block_q, gwo_blocks, gwp):
    _, seq, _ = q.shape
    n_keys = k.shape[1]
    n_q = seq // block_q

    def body(q_ref, k_ref, v_ref, o_ref, lse_ref, do_ref, gwo_ref, gwp_ref, dq_ref, dkt_ref, dvt_ref, lwo_ref, lwp_ref,
             ssem, rsem, lsem):
        h, i = pl.program_id(0), pl.program_id(1)
        _hosted_exchange((h == 0) & (i == 0), (h == N_HEADS - 1) & (i == n_q - 1),
                         [(gwo_ref, lwo_ref, True), (gwp_ref, lwp_ref, False)], ssem, rsem, lsem)

        @pl.when(i == 0)
        def _():
            dkt_ref[...] = jnp.zeros_like(dkt_ref)
            dvt_ref[...] = jnp.zeros_like(dvt_ref)

        kb, vb = k_ref[0], v_ref[0]
        raws, ps = [], []
        for r0 in range(0, block_q, ATTN_ROWS):
            rows = slice(r0, r0 + ATTN_ROWS)
            d_out = do_ref[rows, :]
            p = jnp.exp2(_dot_nt(q_ref[0, rows, :], kb) - lse_ref[0, rows, :])
            dp = _dot_nt(d_out.astype(BF16), vb)
            delta = jnp.sum(d_out * o_ref[rows, :], axis=-1, keepdims=True)
            raw = (p * (dp - delta)).astype(BF16)
            dq_ref[0, rows, :] = _dot(raw, kb)
            raws.append(raw)
            ps.append(p.astype(BF16))
        dkt_ref[0] += _dot_tn(q_ref[0], jnp.concatenate(raws, axis=0))
        dvt_ref[0] += _dot_tn(do_ref[...].astype(BF16), jnp.concatenate(ps, axis=0))

    hbm = pl.BlockSpec(memory_space=pl.ANY)
    return pl.pallas_call(
        body, name="attn_bwd", grid=(N_HEADS, n_q),
        out_shape=(jax.ShapeDtypeStruct((N_HEADS, seq, D_HEAD_PAD), F32),
                   jax.ShapeDtypeStruct((N_HEADS, D_HEAD_PAD, n_keys), F32),
                   jax.ShapeDtypeStruct((N_HEADS, D_V, n_keys), F32),
                   jax.ShapeDtypeStruct(gwo_blocks.shape, gwo_blocks.dtype),
                   jax.ShapeDtypeStruct((N_DEV,) + gwp.shape, gwp.dtype)),
        in_specs=[pl.BlockSpec((1, block_q, D_HEAD_PAD), lambda h, i: (h, i, 0)),
                  pl.BlockSpec((1, n_keys, D_HEAD_PAD), lambda h, i: (h, 0, 0)),
                  pl.BlockSpec((1, n_keys, D_V), lambda h, i: (h, 0, 0)),
                  pl.BlockSpec((block_q, D_V), lambda h, i: (i, h)),
                  pl.BlockSpec((1, block_q, 1), lambda h, i: (h, i, 0)),
                  pl.BlockSpec((block_q, D_V), lambda h, i: (i, h)), hbm, hbm],
        out_specs=(pl.BlockSpec((1, block_q, D_HEAD_PAD), lambda h, i: (h, i, 0)),
                   pl.BlockSpec((1, D_HEAD_PAD, n_keys), lambda h, i: (h, 0, 0)),
                   pl.BlockSpec((1, D_V, n_keys), lambda h, i: (h, 0, 0)), hbm, hbm),
        scratch_shapes=[pltpu.SemaphoreType.DMA((2, 7)), pltpu.SemaphoreType.DMA((2, 7)), pltpu.SemaphoreType.DMA((2,))],
        compiler_params=pltpu.CompilerParams(dimension_semantics=("arbitrary", "arbitrary"), vmem_limit_bytes=VMEM_LIMIT),
    )(q, k, v, o, lse, d_o, gwo_blocks, gwp)


def _mix(x, target, attn, u, modrows, bmodrows, w_pool, pool_scale, w_out):
    seq = x.shape[0]
    n_tiles = seq // TILE
    rows8 = TILE // HALO
    last8 = (seq + TILE) // HALO - 1

    n_blk = seq // Q_BLOCK
    per = TILE // n_blk
    assert TILE % n_blk == 0 and per % 8 == 0

    def body(x_ref, t_ref, o_hbm, ga_ref, pin_ref, hb_ref, ha_ref, gp_ref, mod_ref, bmod_ref, wp_ref, ps_ref, wo_ref,
             loss_ref, g1_ref, dgate_ref, do_hbm, dga_ref, dgp_ref, dpl_ref, gwob_ref, gwp_ref, dps_ref,
             abuf, dbuf, gwo_ref, rsem, wsem):
        j = pl.program_id(0)

        def slab_copies(tile, slot, fetch):
            out = []
            for b in range(n_blk):
                hbm_rows = pl.ds(b * Q_BLOCK + tile * per, per)
                buf_rows = pl.ds(b * per, per)
                if fetch:
                    out.append(pltpu.make_async_copy(o_hbm.at[hbm_rows], abuf.at[slot, buf_rows], rsem.at[slot]))
                else:
                    out.append(pltpu.make_async_copy(dbuf.at[slot, buf_rows], do_hbm.at[hbm_rows], wsem.at[slot]))
            return out

        def wait_all(slot, fetch):
            if fetch:
                pltpu.make_async_copy(o_hbm.at[pl.ds(0, TILE)], abuf.at[slot], rsem.at[slot]).wait()
            else:
                pltpu.make_async_copy(dbuf.at[slot], do_hbm.at[pl.ds(0, TILE)], wsem.at[slot]).wait()

        slot = lax.rem(j, 2)

        @pl.when(j == 0)
        def _():
            loss_ref[...] = jnp.zeros_like(loss_ref)
            dgate_ref[...] = jnp.zeros_like(dgate_ref)
            gwo_ref[...] = jnp.zeros_like(gwo_ref)
            gwp_ref[...] = jnp.zeros_like(gwp_ref)
            dps_ref[...] = jnp.zeros_like(dps_ref)
            for cp in slab_copies(0, 0, True):
                cp.start()

        @pl.when(j + 1 < n_tiles)
        def _():
            for cp in slab_copies(j + 1, 1 - slot, True):
                cp.start()

        wait_all(slot, True)
        attn_t = jnp.swapaxes(abuf[slot].reshape(n_blk, per, D_ATTN), 0, 1).reshape(TILE, D_ATTN)

        gate = mod_ref[2:3, :] + bmod_ref[2:3, :]
        ga = ga_ref[...]
        sga = _sigmoid(ga)
        silu_ga = ga * sga
        br_a = silu_ga * attn_t

        pin = pin_ref[...]
        xs = jnp.concatenate([jnp.where(j > 0, hb_ref[...], 0.0), pin, jnp.where(j < n_tiles - 1, ha_ref[...], 0.0)], axis=0)
        tpos = j * TILE + lax.broadcasted_iota(jnp.int32, (TILE, 1), 0)
        ps = ps_ref[...]
        pooled, yps = [], []
        for g, w in enumerate(POOL_WINDOWS):
            sl = slice(g * GROUP_DIM, (g + 1) * GROUP_DIM)
            ws = _window_sum(xs[:, sl], w, False)[HALO:HALO + TILE]
            pg = (ws * _inv_count(tpos, w, seq) - pin[:, sl]).astype(BF16)
            pooled.append(pg)
            yps.append(_dot(pg, wp_ref[g].astype(BF16)))
        yp = jnp.concatenate(yps, axis=1)
        ypool = yp * ps
        gp = gp_ref[...]
        sgp = _sigmoid(gp)
        silu_gp = gp * sgp
        br_p = silu_gp * ypool

        z = jnp.concatenate([br_a, br_p], axis=1).astype(BF16)
        y = _dot(z, wo_ref[...])
        res = x_ref[...] + gate * y - t_ref[...]
        loss_ref[...] += jnp.sum(res * res) * (0.5 / D_MODEL)
        dxn = res * (1.0 / D_MODEL)
        g1_ref[...] = dxn
        dgate_ref[...] += jnp.sum(dxn * y, axis=0, keepdims=True)
        dy = (gate * dxn).astype(BF16)
        dz = _dot_nt(dy, wo_ref[...])
        gwo_ref[...] += _dot_tn(z, dy)

        dbra = dz[:, 0:D_ATTN]
        dbrp = dz[:, D_ATTN:]
        dga_ref[...] = dbra * attn_t * (sga * (1.0 + ga * (1.0 - sga)))

        @pl.when(j >= 2)
        def _():
            wait_all(slot, False)
        dbuf[slot] = jnp.swapaxes((dbra * silu_ga).reshape(per, n_blk, D_ATTN), 0, 1).reshape(TILE, D_ATTN)
        for cp in slab_copies(j, slot, False):
            cp.start()

        @pl.when(j == n_tiles - 1)
        def _():
            wait_all(slot, False)
            if n_tiles >= 2:
                wait_all(1 - slot, False)

        dgp_ref[...] = dbrp * ypool * (sgp * (1.0 + gp * (1.0 - sgp)))
        dyp_s = dbrp * silu_gp
        dps_ref[...] += jnp.sum(dyp_s * yp, axis=0, keepdims=True)
        dyp = (dyp_s * ps).astype(BF16)
        for g in range(len(POOL_WINDOWS)):
            sl = slice(g * GROUP_DIM, (g + 1) * GROUP_DIM)
            gwp_ref[g] += _dot_tn(pooled[g], dyp[:, sl])
            dpl_ref[:, sl] = _dot_nt(dyp[:, sl], wp_ref[g].astype(BF16))

        @pl.when(j == n_tiles - 1)
        def _():
            gwob_ref[...] = gwo_ref[...].astype(BF16)

    tile = lambda w: pl.BlockSpec((TILE, w), lambda j: (j, 0))
    ucol = lambda col: pl.BlockSpec((TILE, 512), lambda j: (j + 1, col))
    full = lambda shape: pl.BlockSpec(shape, lambda j: (0,) * len(shape))
    return pl.pallas_call(
        body, name="mix_fwd_bwd", grid=(n_tiles,),
        out_shape=(jax.ShapeDtypeStruct((8, 128), F32), jax.ShapeDtypeStruct((seq, D_MODEL), F32),
                   jax.ShapeDtypeStruct((1, D_MODEL), F32), jax.ShapeDtypeStruct((seq, D_ATTN), F32),
                   jax.ShapeDtypeStruct((seq, D_ATTN), F32), jax.ShapeDtypeStruct((seq, D_POOL), F32),
                   jax.ShapeDtypeStruct((seq, D_POOL), F32), jax.ShapeDtypeStruct((D_MODEL, D_MODEL), BF16),
                   jax.ShapeDtypeStruct((4, GROUP_DIM, GROUP_DIM), F32), jax.ShapeDtypeStruct((1, D_POOL), F32)),
        in_specs=[tile(D_MODEL), tile(D_MODEL), pl.BlockSpec(memory_space=pl.ANY), ucol(0), ucol(1),
                  pl.BlockSpec((HALO, 512), lambda j: ((j + 1) * rows8 - 1, 1)),
                  pl.BlockSpec((HALO, 512), lambda j: (jnp.minimum((j + 2) * rows8, last8), 1)),
                  ucol(2), full((8, D_MODEL)), full((8, D_MODEL)), full((4, GROUP_DIM, GROUP_DIM)), full((1, D_POOL)),
                  full((D_MODEL, D_MODEL))],
        out_specs=(full((8, 128)), tile(D_MODEL), full((1, D_MODEL)), pl.BlockSpec(memory_space=pl.ANY), tile(D_ATTN),
                   tile(D_POOL), tile(D_POOL), full((D_MODEL, D_MODEL)), full((4, GROUP_DIM, GROUP_DIM)), full((1, D_POOL))),
        scratch_shapes=[pltpu.VMEM((2, TILE, D_ATTN), F32), pltpu.VMEM((2, TILE, D_ATTN), F32),
                        pltpu.VMEM((D_MODEL, D_MODEL), F32), pltpu.SemaphoreType.DMA((2,)), pltpu.SemaphoreType.DMA((2,))],
        compiler_params=pltpu.CompilerParams(dimension_semantics=("arbitrary",), vmem_limit_bytes=VMEM_LIMIT),
    )(x, target, attn, u, u, u, u, u, modrows, bmodrows, w_pool, pool_scale, w_out)


def _inproj_bwd(x, ctx, modrows, bmodrows, norm_g, w_in_p, q_lora_g, w_uq_p, kv_lora_g, w_ukv, qng_p, kng_p, cos, slo, shi,
                u, dq, dk, dv, dga, dgp, dpl, g1):
    seq = x.shape[0]
    n_tiles = seq // TILE + 1
    rows8 = TILE // HALO
    last8 = seq // HALO - 1

    def body(x_ref, ctx_ref, mod_ref, bmod_ref, ng_ref, win_ref, qlg_ref, wuq_ref, kvlg_ref, wukv_ref, qng_ref, kng_ref,
             cos_ref, slo_ref, shi_ref, cq_ref, ckv_ref, kr_ref, dq_ref, dk_ref, dv_ref, dga_ref, dgp_ref, dpl_ref,
             hb_ref, ha_ref, g1_ref,
             gx_ref, gwin_ref, gwuq_ref, gwukv_ref, dqlg_ref, dkvlg_ref, dqng_ref, dkng_ref, dng_ref, dmod_ref,
             du_ref):
        i = pl.program_id(0)
        is_lat = i > 0

        @pl.when(i == 0)
        def _():
            for ref in (gwin_ref, gwuq_ref, gwukv_ref, dqlg_ref, dkvlg_ref, dqng_ref, dkng_ref, dng_ref, dmod_ref):
                ref[...] = jnp.zeros_like(ref)

        xt, r, xg, h, scale = _modulated_input(i, x_ref, ctx_ref, mod_ref, bmod_ref, ng_ref)
        hb = h.astype(BF16)
        cos_t, slo_t, shi_t = cos_ref[...], slo_ref[...], shi_ref[...]

        cq = cq_ref[...]
        rq = lax.rsqrt(jnp.mean(cq * cq, axis=-1, keepdims=True) + NORM_EPS)
        qhat = cq * rq
        qnb = (qhat * qlg_ref[...]).astype(BF16)
        q = _dot_nt(qnb, wuq_ref[...])
        qg = qng_ref[...]
        dq_heads = []
        dqng = jnp.zeros((1, D_HEAD_PAD), F32)
        for hd in range(N_HEADS):
            qh = q[:, hd * D_HEAD_PAD:(hd + 1) * D_HEAD_PAD]
            rr = lax.rsqrt(jnp.sum(qh * qh, axis=-1, keepdims=True) * (1.0 / D_HEAD) + NORM_EPS)
            yq = qh * rr
            dpost = jnp.where(is_lat, dq_ref[hd] * ATTN_SCALE, 0.0)
            dyn = jnp.concatenate([dpost[:, 0:D_NOPE], _rope_t(dpost[:, D_NOPE:], cos_t, slo_t, shi_t)], axis=1)
            dqng = dqng + jnp.sum(dyn * yq, axis=0, keepdims=True)
            dyg = dyn * qg
            dq_heads.append(rr * (dyg - yq * (jnp.sum(dyg * yq, axis=-1, keepdims=True) * (1.0 / D_HEAD))))
        dqng_ref[...] += dqng
        dqf = jnp.concatenate(dq_heads, axis=1).astype(BF16)
        gwuq_ref[...] += _dot_tn(dqf, qnb)
        dqn = _dot(dqf, wuq_ref[...])
        dqlg_ref[...] += jnp.sum(dqn * qhat, axis=0, keepdims=True)
        dqhat = dqn * qlg_ref[...]
        dcq = rq * (dqhat - qhat * jnp.mean(dqhat * qhat, axis=-1, keepdims=True))

        ckv = ckv_ref[...]
        rkv = lax.rsqrt(jnp.mean(ckv * ckv, axis=-1, keepdims=True) + NORM_EPS)
        kvhat = ckv * rkv
        kvnb = (kvhat * kvlg_ref[...]).astype(BF16)
        kv = _dot(kvnb, wukv_ref[...])
        krz = kr_ref[...]
        kr_ss = jnp.sum(krz * krz, axis=-1, keepdims=True)
        kg = kng_ref[...]
        kg_n, kg_r = kg[:, 0:D_NOPE], kg[:, D_NOPE:]
        dkv_parts = []
        dkr = jnp.zeros((TILE, 128), F32)
        dkng_n = jnp.zeros((1, D_NOPE), F32)
        dkng_r = jnp.zeros((1, 128), F32)
        for hd in range(N_HEADS):
            kn = kv[:, hd * 256:hd * 256 + D_NOPE]
            rr = lax.rsqrt((jnp.sum(kn * kn, axis=-1, keepdims=True) + kr_ss) * (1.0 / D_HEAD) + NORM_EPS)
            yn, yr = kn * rr, krz * rr
            dk_h = jnp.transpose(dk_ref[hd]) * LN_2
            dpn = dk_h[:, 0:D_NOPE]
            dpr = _rope_t(dk_h[:, D_NOPE:], cos_t, slo_t, shi_t)
            dkng_n = dkng_n + jnp.sum(dpn * yn, axis=0, keepdims=True)
            dkng_r = dkng_r + jnp.sum(dpr * yr, axis=0, keepdims=True)
            dyn, dyr = dpn * kg_n, dpr * kg_r
            proj = (jnp.sum(dyn * yn, axis=-1, keepdims=True) + jnp.sum(dyr * yr, axis=-1, keepdims=True)) * (1.0 / D_HEAD)
            dkv_parts.append(rr * (dyn - yn * proj))
            dkv_parts.append(jnp.transpose(dv_ref[hd]))
            dkr = dkr + rr * (dyr - yr * proj)
        dkng_ref[:, 0:D_NOPE] += dkng_n
        dkng_ref[:, D_NOPE:] += dkng_r
        dkvf = jnp.concatenate(dkv_parts, axis=1).astype(BF16)
        gwukv_ref[...] += _dot_tn(dkvf, kvnb)
        dkvn = _dot_nt(dkvf, wukv_ref[...])
        dkvlg_ref[...] += jnp.sum(dkvn * kvhat, axis=0, keepdims=True)
        dkvhat = dkvn * kvlg_ref[...]
        dckv = rkv * (dkvhat - kvhat * jnp.mean(dkvhat * kvhat, axis=-1, keepdims=True))

        lat_t = jnp.maximum(i - 1, 0)
        dpl_t = dpl_ref[...]
        ds = jnp.concatenate([jnp.where(i > 1, hb_ref[...], 0.0), dpl_t,
                              jnp.where(i < n_tiles - 1, ha_ref[...], 0.0)], axis=0)
        tpos = lat_t * TILE - HALO + lax.broadcasted_iota(jnp.int32, (TILE + 2 * HALO, 1), 0)
        for g, w in enumerate(POOL_WINDOWS):
            sl = slice(g * GROUP_DIM, (g + 1) * GROUP_DIM)
            wsum = _window_sum(ds[:, sl] * _inv_count(tpos, w, seq), w, True)[HALO:HALO + TILE]
            du_ref[:, O_PIN + g * GROUP_DIM:O_PIN + (g + 1) * GROUP_DIM] = jnp.where(
                is_lat, wsum - dpl_t[:, sl], 0.0).astype(BF16)

        du_ref[:, O_GA:O_GA + D_ATTN] = jnp.where(is_lat, dga_ref[...], 0.0).astype(BF16)
        du_ref[:, O_GP:O_GP + D_POOL] = jnp.where(is_lat, dgp_ref[...], 0.0).astype(BF16)
        du_ref[:, O_CQ:O_CQ + R_Q] = dcq.astype(BF16)
        du_ref[:, O_CKV:O_CKV + R_KV] = dckv.astype(BF16)
        du_ref[:, O_KR:U_PAD] = dkr.astype(BF16)
        dub = du_ref[...]
        dh = _dot(dub, win_ref[...])
        gwin_ref[...] += _dot_tn(dub, hb)

        dshift = jnp.sum(dh, axis=0, keepdims=True)
        dscale = jnp.sum(dh * xg, axis=0, keepdims=True)
        zero = jnp.zeros_like(dshift)
        dmod_ref[0:1, :] += jnp.where(is_lat, dshift, zero)
        dmod_ref[1:2, :] += jnp.where(is_lat, dscale, zero)
        dmod_ref[2:3, :] += jnp.where(is_lat, zero, dshift)
        dmod_ref[3:4, :] += jnp.where(is_lat, zero, dscale)
        dxg = dh * (1.0 + scale)
        xr = xt * r
        dng_ref[...] += jnp.sum(dxg * xr, axis=0, keepdims=True)
        dxn = dxg * ng_ref[...]
        gx_ref[...] = g1_ref[...] + r * (dxn - xr * jnp.mean(dxn * xr, axis=-1, keepdims=True))

    lat = lambda i: (jnp.maximum(i - 1, 0), 0)
    full = lambda shape: pl.BlockSpec(shape, lambda i: (0,) * len(shape))
    rope_spec = pl.BlockSpec((TILE, 128), lambda i: (i, 0))
    return pl.pallas_call(
        body, name="inproj_bwd", grid=(n_tiles,),
        out_shape=(jax.ShapeDtypeStruct((seq, D_MODEL), F32), jax.ShapeDtypeStruct((U_PAD, D_MODEL), F32),
                   jax.ShapeDtypeStruct((N_HEADS * D_HEAD_PAD, R_Q), F32), jax.ShapeDtypeStruct((N_HEADS * 256, R_KV), F32),
                   jax.ShapeDtypeStruct((1, R_Q), F32), jax.ShapeDtypeStruct((1, R_KV), F32),
                   jax.ShapeDtypeStruct((1, D_HEAD_PAD), F32), jax.ShapeDtypeStruct((1, D_HEAD_PAD), F32),
                   jax.ShapeDtypeStruct((1, D_MODEL), F32), jax.ShapeDtypeStruct((8, D_MODEL), F32)),
        in_specs=[pl.BlockSpec((TILE, D_MODEL), lat), full((TILE, D_MODEL)), full((8, D_MODEL)), full((8, D_MODEL)),
                  full((1, D_MODEL)), full((U_PAD, D_MODEL)), full((1, R_Q)), full((N_HEADS * D_HEAD_PAD, R_Q)),
                  full((1, R_KV)), full((R_KV, N_HEADS * 256)), full((1, D_HEAD_PAD)), full((1, D_HEAD_PAD)),
                  rope_spec, rope_spec, rope_spec,
                  pl.BlockSpec((TILE, R_Q), lambda i: (i, O_CQ // R_Q)),
                  pl.BlockSpec((TILE, R_KV), lambda i: (i, O_CKV // R_KV)),
                  pl.BlockSpec((TILE, 128), lambda i: (i, O_KR // 128)),
                  pl.BlockSpec((N_HEADS, TILE, D_HEAD_PAD), lambda i: (0, jnp.maximum(i - 1, 0), 0)),
                  pl.BlockSpec((N_HEADS, D_HEAD_PAD, TILE), lambda i: (0, 0, i)),
                  pl.BlockSpec((N_HEADS, D_V, TILE), lambda i: (0, 0, i)),
                  pl.BlockSpec((TILE, D_ATTN), lat), pl.BlockSpec((TILE, D_POOL), lat), pl.BlockSpec((TILE, D_POOL), lat),
                  pl.BlockSpec((HALO, D_POOL), lambda i: (jnp.maximum((i - 1) * rows8 - 1, 0), 0)),
                  pl.BlockSpec((HALO, D_POOL), lambda i: (jnp.minimum(jnp.maximum(i, 1) * rows8, last8), 0)),
                  pl.BlockSpec((TILE, D_MODEL), lat)],
        out_specs=(pl.BlockSpec((TILE, D_MODEL), lat), full((U_PAD, D_MODEL)), full((N_HEADS * D_HEAD_PAD, R_Q)),
                   full((N_HEADS * 256, R_KV)), full((1, R_Q)), full((1, R_KV)), full((1, D_HEAD_PAD)),
                   full((1, D_HEAD_PAD)), full((1, D_MODEL)), full((8, D_MODEL))),
        scratch_shapes=[pltpu.VMEM((TILE, U_PAD), BF16)],
        compiler_params=pltpu.CompilerParams(dimension_semantics=("arbitrary",), vmem_limit_bytes=VMEM_LIMIT),
    )(x, ctx, modrows, bmodrows, norm_g, w_in_p, q_lora_g, w_uq_p, kv_lora_g, w_ukv, qng_p, kng_p, cos, slo, shi,
      u, u, u, dq, dk, dv, dga, dgp, dpl, dpl, dpl, g1)


SMALL_LAYOUT = ((0, D_MODEL), (1, R_Q), (2, R_KV), (3, D_HEAD_PAD), (4, D_HEAD_PAD), (5, D_POOL))
ROW_DMOD_B, ROW_DMOD_C, ROW_LOSS = 6, 9, 12


def _epilogue(parts, landed, smalls, dmod4, dgate, loss_acc, w_mod_l):
    n_a, n_l, n_s = len(parts), len(landed), len(smalls)
    n_mod = w_mod_l.shape[1]
    blk = [p.shape[1:] for p in parts]
    small_out = [D_MODEL, R_Q, R_KV, D_HEAD, D_HEAD, D_POOL]

    def body(*refs):
        part_refs = refs[0:n_a]
        land_refs = refs[n_a:n_a + n_l]
        small_in = refs[n_a + n_l:n_a + n_l + n_s]
        dmod4_ref, dgate_ref, loss_ref, wmod_ref = refs[n_a + n_l + n_s:n_a + n_l + n_s + 4]
        outs = refs[n_a + n_l + n_s + 4:]
        red_refs = outs[0:n_a]
        landsum_refs = outs[n_a:n_a + n_l]
        ccg_ref = outs[n_a + n_l]
        sum_refs = outs[n_a + n_l + 1:n_a + n_l + 1 + n_s]
        gbmod_ref, dmy_ref, lossout_ref = outs[n_a + n_l + 1 + n_s:n_a + n_l + 4 + n_s]
        scr = outs[n_a + n_l + 4 + n_s:]
        recv1, own1, sum1b, recv2 = scr[0:n_a], scr[n_a:2 * n_a], scr[2 * n_a:3 * n_a], scr[3 * n_a:4 * n_a]
        small_ref, smallg_ref, tot_ref, cc_ref = scr[4 * n_a:4 * n_a + 4]
        ssem1, rsem1, lsem, ssem2, rsem2, ssem_s, rsem_s, ssem_cc, rsem_cc = scr[4 * n_a + 4:]
        x, y, c = _coords()
        me = (x, y, c)
        sibling = (x, y, 1 - c)

        stage1, own_copies = [], []
        for a in range(n_a):
            for b in range(4):
                dev = 4 * (b >> 1) + 2 * (b & 1)
                stage1.append(pltpu.make_async_remote_copy(
                    src_ref=part_refs[a].at[dev + (1 - c)], dst_ref=recv1[a].at[b],
                    send_sem=ssem1.at[a, b], recv_sem=rsem1.at[a, b], device_id=sibling, device_id_type=MESH))
                own_copies.append(pltpu.make_async_copy(part_refs[a].at[dev + c], own1[a].at[b], lsem.at[a, b]))
                stage1[-1].start()
                own_copies[-1].start()

        small_ref[...] = jnp.zeros_like(small_ref)
        for ref, (row, width) in zip(small_in, SMALL_LAYOUT):
            small_ref[row:row + 1, 0:width] = ref[...]
        small_ref[ROW_DMOD_B:ROW_DMOD_B + 2, :] = dmod4_ref[0:2, :]
        small_ref[ROW_DMOD_B + 2:ROW_DMOD_B + 3, :] = dgate_ref[...]
        small_ref[ROW_DMOD_C:ROW_DMOD_C + 2, :] = dmod4_ref[2:4, :]
        small_ref[ROW_LOSS:ROW_LOSS + 1, 0:128] = loss_ref[0:1, :]
        smallg_ref[_lin(me)] = small_ref[...]
        s_recv, s_send = _gather_to_all(small_ref, smallg_ref, ssem_s, rsem_s)
        s_recv()
        tot = smallg_ref[0]
        for d in range(1, N_DEV):
            tot = tot + smallg_ref[d]
        tot_ref[...] = tot
        for ref, (row, _), width in zip(sum_refs, SMALL_LAYOUT, small_out):
            ref[...] = tot_ref[row:row + 1, 0:width]
        lossout_ref[...] = tot_ref[8:16, 0:128]
        lin = _lin(me)

        def my_columns(three_rows):
            v = jnp.concatenate(three_rows, axis=1)
            out = jnp.zeros((1, n_mod), F32)
            for d in range(N_DEV):
                out = out + jnp.where(lin == d, v[:, d * n_mod:(d + 1) * n_mod], 0.0)
            return out

        rows3 = lambda ref, r0: [ref[r0 + t:r0 + t + 1, :] for t in range(3)]
        dmodc = rows3(tot_ref, ROW_DMOD_C)
        gbmod_ref[...] = jnp.concatenate(rows3(tot_ref, ROW_DMOD_B), axis=1) + jnp.concatenate(dmodc, axis=1)
        dmy_ref[...] = jnp.zeros_like(dmy_ref)
        for b in range(N_DEV):
            dmy_ref[b:b + 1, :] = my_columns(rows3(smallg_ref.at[b], ROW_DMOD_B))
        dmy = my_columns(dmodc)
        dmy_ref[N_DEV:N_DEV + 1, :] = dmy
        part = lax.dot_general(jnp.broadcast_to(dmy, (8, n_mod)), wmod_ref[...], (((1,), (1,)), ((), ())),
                               precision=HIGHEST, preferred_element_type=F32)

        cc_ref[...] = part
        ccg_ref[_lin(me)] = part
        cc_recv, cc_send = _gather_to_all(cc_ref, ccg_ref, ssem_cc, rsem_cc)

        stage2 = []
        for a in range(n_a):
            for b in range(4):
                stage1[4 * a + b].wait_recv()
                own_copies[4 * a + b].wait()
                sum1b[a][b] = (own1[a][b] + recv1[a][b]).astype(BF16)
            for k in (1, 2, 3):
                tx, ty = _flip(x, (k >> 1) & 1), _flip(y, k & 1)
                stage2.append(pltpu.make_async_remote_copy(
                    src_ref=sum1b[a].at[2 * tx + ty], dst_ref=recv2[a].at[k - 1], send_sem=ssem2.at[a, k - 1],
                    recv_sem=rsem2.at[a, k - 1], device_id=(tx, ty, c), device_id_type=MESH))
                stage2[-1].start()
        for a in range(n_a):
            own = own1[a][2 * x + y] + recv1[a][2 * x + y]
            for k in (1, 2, 3):
                stage2[3 * a + k - 1].wait_recv()
                own = own + recv2[a][k - 1].astype(F32)
            red_refs[a][...] = own

        for ref, out in zip(land_refs, landsum_refs):
            acc = ref[0].astype(F32)
            for d in range(1, N_DEV):
                acc = acc + ref[d].astype(F32)
            out[...] = acc
        cc_recv()
        for cp in stage1 + stage2:
            cp.wait_send()
        s_send()
        cc_send()

    vm = pl.BlockSpec(memory_space=pltpu.VMEM)
    sds = jax.ShapeDtypeStruct
    return pl.pallas_call(
        body, name="epilogue_reduce",
        out_shape=(*[sds(s, F32) for s in blk], *[sds(a.shape[1:], F32) for a in landed], sds((N_DEV, 8, D_MODEL), F32),
                   *[sds((1, w), F32) for w in small_out], sds((1, 3 * D_MODEL), F32), sds((16, n_mod), F32),
                   sds((8, 128), F32)),
        in_specs=[pl.BlockSpec(memory_space=pl.ANY)] * n_a + [vm] * (n_l + n_s + 4),
        out_specs=tuple([vm] * (n_a + n_l + n_s + 4)),
        scratch_shapes=[*[pltpu.VMEM((4,) + s, F32) for s in blk], *[pltpu.VMEM((4,) + s, F32) for s in blk],
                        *[pltpu.VMEM((4,) + s, BF16) for s in blk], *[pltpu.VMEM((3,) + s, BF16) for s in blk],
                        pltpu.VMEM((SMALL_ROWS, D_MODEL), F32), pltpu.VMEM((N_DEV, SMALL_ROWS, D_MODEL), F32),
                        pltpu.VMEM((SMALL_ROWS, D_MODEL), F32), pltpu.VMEM((8, D_MODEL), F32),
                        pltpu.SemaphoreType.DMA((n_a, 4)), pltpu.SemaphoreType.DMA((n_a, 4)), pltpu.SemaphoreType.DMA((n_a, 4)),
                        pltpu.SemaphoreType.DMA((n_a, 3)), pltpu.SemaphoreType.DMA((n_a, 3)),
                        pltpu.SemaphoreType.DMA((7,)), pltpu.SemaphoreType.DMA((7,)),
                        pltpu.SemaphoreType.DMA((7,)), pltpu.SemaphoreType.DMA((7,))],
        compiler_params=pltpu.CompilerParams(vmem_limit_bytes=VMEM_LIMIT),
    )(*parts, *landed, *smalls, dmod4, dgate, loss_acc, w_mod_l)


def _adamw(weights, grads, ms, vs, c_all_t, dmod_my, p2g):
    n = len(weights)

    def body(*refs):
        w_refs = refs[0:n]
        g_in = refs[n:2 * n - 2]
        m_refs = refs[2 * n - 2:3 * n - 2]
        v_refs = refs[3 * n - 2:4 * n - 2]
        cat_ref, dmod_ref, p2g_ref = refs[4 * n - 2:4 * n + 1]
        outs = refs[4 * n + 1:]
        gcc_ref, gwm_ref = outs[0], outs[1]
        d_refs, nm_refs, nv_refs = outs[2:2 + n], outs[2 + n:2 + 2 * n], outs[2 + 2 * n:2 + 3 * n]

        part = p2g_ref[0, 0:1, :]
        for d in range(1, N_DEV):
            part = part + p2g_ref[d, 0:1, :]
        cc = w_refs[0][...]
        sg = _sigmoid(cc)
        gcc_ref[...] = part * (sg * (1.0 + cc * (1.0 - sg)))
        cat = cat_ref[...]
        gwm_ref[...] = lax.dot_general(cat * _sigmoid(cat), dmod_ref[...], (((1,), (0,)), ((), ())), precision=HIGHEST,
                                       preferred_element_type=F32)
        for idx in range(n):
            g = (gcc_ref, gwm_ref)[idx][...] if idx < 2 else g_in[idx - 2][...]
            m = ADAM_B1 * m_refs[idx][...] + (1.0 - ADAM_B1) * g
            v = ADAM_B2 * v_refs[idx][...] + (1.0 - ADAM_B2) * (g * g)
            m_hat = m / (1.0 - ADAM_B1 ** ADAM_STEP)
            v_hat = v / (1.0 - ADAM_B2 ** ADAM_STEP)
            d_refs[idx][...] = -ADAM_LR * (m_hat / (jnp.sqrt(v_hat) + ADAM_EPS) + ADAM_WD * w_refs[idx][...])
            nm_refs[idx][...] = m
            nv_refs[idx][...] = v

    vm = pl.BlockSpec(memory_space=pltpu.VMEM)
    shapes = [jax.ShapeDtypeStruct(w.shape, F32) for w in weights]
    args = list(weights) + list(grads[2:]) + list(ms) + list(vs) + [c_all_t, dmod_my, p2g]
    out_shape = tuple(shapes[0:2] + shapes * 3)
    return pl.pallas_call(
        body, name="adamw_update", out_shape=out_shape, in_specs=[vm] * len(args), out_specs=tuple([vm] * len(out_shape)),
        compiler_params=pltpu.CompilerParams(vmem_limit_bytes=VMEM_LIMIT),
    )(*args)


def _rope_tables(seq):
    rows = seq // GRID_W
    row = np.repeat(np.arange(rows, dtype=np.float32), GRID_W)
    col = np.tile(np.arange(GRID_W, dtype=np.float32), rows)
    n_freq = D_ROPE // 4
    inv = (np.float32(ROPE_BASE) ** (-np.arange(n_freq, dtype=np.float32) / np.float32(n_freq))).astype(np.float32)
    ang_r = (row[:, None] * inv).astype(np.float32)
    ang_c = (col[:, None] * inv).astype(np.float32)
    ang = np.concatenate([ang_r, ang_r, ang_c, ang_c], axis=-1)
    cos, sin = np.cos(ang).astype(np.float32), np.sin(ang).astype(np.float32)
    low = (np.arange(D_ROPE) % 32) < 16

    def table(t, fill):
        out = np.full((TILE + seq, 128), fill, np.float32)
        out[TILE:, 0:D_ROPE] = t
        return jnp.asarray(out)

    return table(cos, 1.0), table(np.where(low, -sin, 0.0), 0.0), table(np.where(low, 0.0, sin), 0.0)


def kernel(x, c, ctx, c_ctx, w_mod, b_mod, norm_g, w_in, q_lora_g, w_uq, kv_lora_g, w_ukv, q_norm_g, k_norm_g, w_pool, pool_scale, w_out, loss_target, m_c_ctx, m_w_mod, m_b_mod, m_norm_g, m_w_in, m_q_lora_g, m_w_uq, m_kv_lora_g, m_w_ukv, m_q_norm_g, m_k_norm_g, m_w_pool, m_pool_scale, m_w_out, v_c_ctx, v_w_mod, v_b_mod, v_norm_g, v_w_in, v_q_lora_g, v_w_uq, v_kv_lora_g, v_w_ukv, v_q_norm_g, v_k_norm_g, v_w_pool, v_pool_scale, v_w_out):
    seq = x.shape[1]
    assert ctx.shape[1] == TILE and seq % TILE == 0 and seq % GRID_W == 0
    ix, iy, ic = lax.axis_index("x"), lax.axis_index("y"), lax.axis_index("c")
    me = 4 * ix + 2 * iy + ic
    x2, ctx2, tgt2 = x[0], ctx[0], loss_target[0]
    w_mod_l, w_ukv_l, w_out_l = w_mod[0], w_ukv[0], w_out[0]
    c_ctx_row = c_ctx.reshape(1, D_MODEL)

    tr = lambda a: jnp.transpose(a[0])
    w_in_tl, w_uq_tl = tr(w_in), tr(w_uq)
    cg, modg, wg_in, wg_uq, wg_ukv = _prologue(
        jnp.broadcast_to(c, (8, D_MODEL)), jnp.broadcast_to(c_ctx_row, (8, D_MODEL)), w_mod_l,
        [w_in_tl, w_uq_tl, w_ukv_l])
    mod_all = jnp.transpose(modg, (1, 0, 2)).reshape(16, 3 * D_MODEL)
    mod_b = lax.dynamic_slice_in_dim(mod_all, me, 1, axis=0).reshape(3, D_MODEL)
    mod_c = mod_all[8].reshape(3, D_MODEL)
    modrows = jnp.concatenate([mod_b, mod_c[0:2], jnp.zeros((3, D_MODEL), F32)], axis=0)
    b3 = b_mod.reshape(3, D_MODEL)
    bmodrows = jnp.concatenate([b3, b3[0:2], jnp.zeros((3, D_MODEL), F32)], axis=0)

    w_in_t = wg_in.reshape(D_IN_PROJ, D_MODEL)
    w_in_p = jnp.concatenate([w_in_t[448:], w_in_t[0:384], w_in_t[384:448],
                              jnp.zeros((U_PAD - D_IN_PROJ, D_MODEL), BF16)], axis=0)
    w_uq_p = jnp.concatenate([wg_uq.reshape(N_HEADS, D_HEAD, R_Q), jnp.zeros((N_HEADS, D_HEAD_PAD - D_HEAD, R_Q), BF16)],
                             axis=1).reshape(N_HEADS * D_HEAD_PAD, R_Q)
    w_ukv_f = jnp.transpose(wg_ukv, (1, 0, 2)).reshape(R_KV, N_HEADS * 256)
    qng_p = jnp.concatenate([q_norm_g, jnp.zeros((1, D_HEAD_PAD - D_HEAD), F32)], axis=1)
    kng_p = jnp.concatenate([k_norm_g, jnp.zeros((1, D_HEAD_PAD - D_HEAD), F32)], axis=1)
    cos, slo, shi = _rope_tables(seq)

    u, q, k, v = _inproj_fwd(x2, ctx2, modrows, bmodrows, norm_g, w_in_p, q_lora_g, w_uq_p, kv_lora_g, w_ukv_f,
                             qng_p, kng_p, cos, slo, shi)
    attn, lse, wg_out = _attn_fwd(q, k, v, min(2048, seq), w_out_l.astype(BF16))
    (loss_acc, g1, dgate, d_attn, dga, dgp, dpl, gwo_b, gwp, dps) = _mix(
        x2, tgt2, attn, u, modrows, bmodrows, w_pool[0], pool_scale, wg_out.reshape(D_MODEL, D_MODEL))

    blocks = lambda a: a.reshape((N_DEV, a.shape[0] // N_DEV) + a.shape[1:])
    dq, dk, dv, land_wo, land_wp = _attn_bwd(q, k, v, attn, lse, d_attn, min(1024, seq), blocks(gwo_b),
                                             gwp.reshape(4 * GROUP_DIM, GROUP_DIM))
    (grad_x, gwin_p, gwuq_p, gwukv_t, dqlg, dkvlg, dqng, dkng, dng, dmod4) = _inproj_bwd(
        x2, ctx2, modrows, bmodrows, norm_g, w_in_p, q_lora_g, w_uq_p, kv_lora_g, w_ukv_f, qng_p, kng_p, cos, slo, shi,
        u, dq, dk, dv, dga, dgp, dpl, g1)

    gwin_t = jnp.concatenate([gwin_p[O_CQ:O_KR], gwin_p[O_KR:D_IN_PROJ], gwin_p[0:O_CQ]], axis=0)
    gwuq_t = gwuq_p.reshape(N_HEADS, D_HEAD_PAD, R_Q)[:, 0:D_HEAD].reshape(N_HEADS * D_HEAD, R_Q)
    parts = [blocks(gwin_t), blocks(gwuq_t), blocks(gwukv_t)]
    (r_win, r_wuq, r_wukv, r_wout, g_w_pool, ccg, g_ng, g_qlg, g_kvlg, g_qng, g_kng, g_ps, g_bmod, dmod_my,
     loss_rows) = _epilogue(parts, [land_wo, land_wp], [dng, dqlg, dkvlg, dqng, dkng, dps], dmod4, dgate, loss_acc, w_mod_l)

    g_w_ukv = jnp.transpose(r_wukv)
    c_rows = cg[:, 0, :]
    c_all_t = jnp.transpose(jnp.concatenate([c_rows, c_ctx_row, jnp.zeros((16 - N_DEV - 1, D_MODEL), F32)], axis=0))

    weights = [c_ctx_row, w_mod_l, b_mod, norm_g, w_in_tl, q_lora_g, w_uq_tl, kv_lora_g, w_ukv_l, q_norm_g, k_norm_g,
               w_pool.reshape(4 * GROUP_DIM, GROUP_DIM), pool_scale, w_out_l]
    grads = [None, None, g_bmod, g_ng, r_win, g_qlg, r_wuq, g_kvlg, g_w_ukv, g_qng, g_kng, g_w_pool, g_ps, r_wout]
    ms = [m_c_ctx.reshape(1, D_MODEL), m_w_mod[0], m_b_mod, m_norm_g, tr(m_w_in), m_q_lora_g, tr(m_w_uq), m_kv_lora_g,
          m_w_ukv[0], m_q_norm_g, m_k_norm_g, m_w_pool.reshape(4 * GROUP_DIM, GROUP_DIM), m_pool_scale, m_w_out[0]]
    vs = [v_c_ctx.reshape(1, D_MODEL), v_w_mod[0], v_b_mod, v_norm_g, tr(v_w_in), v_q_lora_g, tr(v_w_uq), v_kv_lora_g,
          v_w_ukv[0], v_q_norm_g, v_k_norm_g, v_w_pool.reshape(4 * GROUP_DIM, GROUP_DIM), v_pool_scale, v_w_out[0]]
    outs = _adamw(weights, grads, ms, vs, c_all_t, dmod_my, ccg)
    grads[0], grads[1] = outs[0], outs[1]
    n = len(weights)
    deltas, new_m, new_v = outs[2:2 + n], outs[2 + n:2 + 2 * n], outs[2 + 2 * n:2 + 3 * n]

    loss = loss_rows[ROW_LOSS - 8, 0]
    final = [c_ctx.shape, w_mod.shape, b_mod.shape, norm_g.shape, w_in.shape, q_lora_g.shape, w_uq.shape, kv_lora_g.shape,
             w_ukv.shape, q_norm_g.shape, k_norm_g.shape, w_pool.shape, pool_scale.shape, w_out.shape]
    transposed = (4, 6)

    def shaped(arrs):
        return [(jnp.transpose(a) if i in transposed else a).reshape(s) for i, (a, s) in enumerate(zip(arrs, final))]

    return (loss, grad_x[None], *shaped(grads), *shaped(deltas), *shaped(new_m), *shaped(new_v))
```

```python
import numpy as np

import jax
import jax.numpy as jnp
from jax import lax
from jax.experimental import pallas as pl
from jax.experimental.pallas import tpu as pltpu

F32 = jnp.float32
BF16 = jnp.bfloat16
MESH = pl.DeviceIdType.MESH
HIGHEST = lax.Precision.HIGHEST

D_MODEL = 1024
N_HEADS = 4
D_NOPE = 128
D_ROPE = 64
D_HEAD = D_NOPE + D_ROPE
D_HEAD_PAD = 256
D_V = 128
R_Q = 256
R_KV = 128
D_ATTN = 512
D_POOL = 512
POOL_WINDOWS = (2, 4, 8, 16)
GROUP_DIM = 128
GRID_W = 64
ROPE_BASE = 10000.0
NORM_EPS = 1e-6
ATTN_SCALE = D_HEAD ** -0.5
LOG2_E = 1.4426950408889634
LN_2 = 0.6931471805599453
ATTN_ROWS = 256
D_IN_PROJ = 1984
U_PAD = 2048
O_GA, O_PIN, O_GP, O_CQ, O_CKV, O_KR = 0, 512, 1024, 1536, 1792, 1920
TILE = 256
Q_BLOCK = 128
HALO = 8
N_DEV = 8
VMEM_LIMIT = 56 * 1024 * 1024

ADAM_LR = 0.001
ADAM_B1 = 0.9
ADAM_B2 = 0.999
ADAM_EPS = 1e-08
ADAM_WD = 0.01
ADAM_STEP = 10

SMALL_ROWS = 16


def _dot(a, b):
    return lax.dot_general(a, b, (((1,), (0,)), ((), ())), preferred_element_type=F32)


def _dot_nt(a, b):
    return lax.dot_general(a, b, (((1,), (1,)), ((), ())), preferred_element_type=F32)


def _dot_tn(a, b):
    return lax.dot_general(a, b, (((0,), (0,)), ((), ())), preferred_element_type=F32)


def _sigmoid(x):
    return 1.0 / (1.0 + jnp.exp(-x))


def _rope(p, cos, slo, shi):
    return p * cos + pltpu.roll(p, 112, 1) * slo + pltpu.roll(p, 16, 1) * shi


def _rope_t(d, cos, slo, shi):
    return d * cos + pltpu.roll(d * slo, 16, 1) + pltpu.roll(d * shi, 112, 1)


def _shift_rows(a, k):
    n = a.shape[0]
    return pltpu.roll(a, (-k) % n, 0)


def _window_sum(x, w, transposed):
    s = (x + _shift_rows(x, 1)) if transposed else (_shift_rows(x, -1) + x)
    step = 1
    while 2 * step < w:
        s = _shift_rows(s, -step) + _shift_rows(s, step)
        step *= 2
    return s


def _inv_count(tpos, w, seq):
    lo = jnp.maximum(tpos - w // 2, 0)
    hi = jnp.minimum(tpos - w // 2 + w, seq)
    return 1.0 / jnp.maximum(hi - lo, 1).astype(F32)


def _coords():
    return lax.axis_index("x"), lax.axis_index("y"), lax.axis_index("c")


def _flip(v, bit):
    return (1 - v) if bit else v


def _peer(k):
    x, y, c = _coords()
    return (_flip(x, (k >> 2) & 1), _flip(y, (k >> 1) & 1), _flip(c, k & 1))


def _lin(p):
    return 4 * p[0] + 2 * p[1] + p[2]


def _gather_to_all(src_ref, slots_ref, send_sems, recv_sems):
    me = _coords()
    copies = []
    for k in range(1, N_DEV):
        cp = pltpu.make_async_remote_copy(
            src_ref=src_ref, dst_ref=slots_ref.at[_lin(me)], send_sem=send_sems.at[k - 1], recv_sem=recv_sems.at[k - 1],
            device_id=_peer(k), device_id_type=MESH)
        cp.start()
        copies.append(cp)

    def wait_recv():
        for k in range(1, N_DEV):
            pltpu.make_async_remote_copy(
                src_ref=src_ref, dst_ref=slots_ref.at[_lin(_peer(k))], send_sem=send_sems.at[k - 1],
                recv_sem=recv_sems.at[k - 1], device_id=_peer(k), device_id_type=MESH).wait_recv()

    def wait_send():
        for cp in copies:
            cp.wait_send()

    return wait_recv, wait_send


def _prologue(c8, cctx8, w_mod_l, shards):
    n_mod = w_mod_l.shape[1]
    n_w = len(shards)

    def body(c8_ref, cctx_ref, wmod_ref, *refs):
        w_refs = refs[0:n_w]
        cg_ref, modg_ref = refs[n_w], refs[n_w + 1]
        wg_refs = refs[n_w + 2:2 * n_w + 2]
        modblk_ref = refs[2 * n_w + 2]
        wb_refs = refs[2 * n_w + 3:3 * n_w + 3]
        ssem_w, rsem_w, ssem_c, rsem_c, ssem_m, rsem_m = refs[3 * n_w + 3:]
        x, y, c = _coords()
        me = (x, y, c)
        sibling = (x, y, 1 - c)
        chips = [(1 - x, y), (x, 1 - y), (1 - x, 1 - y)]

        def wcopies(k, block, to, own=False):
            out = []
            for a in range(n_w):
                slot = wg_refs[a].at[_lin(block)]
                out.append(pltpu.make_async_remote_copy(
                    src_ref=wb_refs[a] if own else slot, dst_ref=slot, send_sem=ssem_w.at[a, k], recv_sem=rsem_w.at[a, k],
                    device_id=to, device_id_type=MESH))
            return out

        for a in range(n_w):
            wb_refs[a][...] = w_refs[a][...].astype(BF16)
        first = wcopies(0, me, sibling, own=True)
        for j, chip in enumerate(chips):
            first += wcopies(1 + j, me, (*chip, c), own=True)
        for cp in first:
            cp.start()
        for a in range(n_w):
            wg_refs[a][_lin(me)] = wb_refs[a][...]

        cg_ref[_lin(me)] = c8_ref[...]
        c_recv, c_send = _gather_to_all(c8_ref, cg_ref, ssem_c, rsem_c)
        c_recv()
        row = lax.broadcasted_iota(jnp.int32, (8, D_MODEL), 0)
        c_all = jnp.zeros((8, D_MODEL), F32)
        for d in range(N_DEV):
            c_all = c_all + jnp.where(row == d, cg_ref[d], 0.0)
        cc = cctx_ref[...]
        a = jnp.concatenate([c_all * _sigmoid(c_all), cc * _sigmoid(cc)], axis=0)
        modblk_ref[...] = lax.dot_general(a, wmod_ref[...], (((1,), (0,)), ((), ())), precision=HIGHEST,
                                          preferred_element_type=F32)
        modg_ref[_lin(me)] = modblk_ref[...]
        m_recv, m_send = _gather_to_all(modblk_ref, modg_ref, ssem_m, rsem_m)
        m_recv()

        passed = []
        for j, chip in enumerate(chips):
            for cp in wcopies(1 + j, (*chip, c), me):
                cp.wait_recv()
            fwd = wcopies(4 + j, (*chip, c), sibling)
            for cp in fwd:
                cp.start()
            passed += fwd
        for cp in wcopies(0, sibling, me):
            cp.wait_recv()
        for j, chip in enumerate(chips):
            for cp in wcopies(4 + j, (*chip, 1 - c), me):
                cp.wait_recv()
        for cp in first + passed:
            cp.wait_send()
        c_send()
        m_send()

    vm = pl.BlockSpec(memory_space=pltpu.VMEM)
    return pl.pallas_call(
        body, name="prologue_gather",
        out_shape=(jax.ShapeDtypeStruct((N_DEV, 8, D_MODEL), F32), jax.ShapeDtypeStruct((N_DEV, 16, n_mod), F32),
                   *[jax.ShapeDtypeStruct((N_DEV,) + s.shape, BF16) for s in shards]),
        in_specs=[vm] * (3 + n_w), out_specs=tuple([vm] * (2 + n_w)),
        scratch_shapes=[pltpu.VMEM((16, n_mod), F32), *[pltpu.VMEM(s.shape, BF16) for s in shards],
                        pltpu.SemaphoreType.DMA((n_w, 7)), pltpu.SemaphoreType.DMA((n_w, 7)),
                        pltpu.SemaphoreType.DMA((7,)), pltpu.SemaphoreType.DMA((7,)),
                        pltpu.SemaphoreType.DMA((7,)), pltpu.SemaphoreType.DMA((7,))],
        compiler_params=pltpu.CompilerParams(vmem_limit_bytes=VMEM_LIMIT),
    )(c8, cctx8, w_mod_l, *shards)


def _modulated_input(is_ctx, x_ref, ctx_ref, mod_ref, bmod_ref, ng_ref):
    xt = jnp.where(is_ctx, ctx_ref[...], x_ref[...])
    shift = jnp.where(is_ctx, mod_ref[3:4, :] + bmod_ref[3:4, :], mod_ref[0:1, :] + bmod_ref[0:1, :])
    scale = jnp.where(is_ctx, mod_ref[4:5, :] + bmod_ref[4:5, :], mod_ref[1:2, :] + bmod_ref[1:2, :])
    r = lax.rsqrt(jnp.mean(xt * xt, axis=-1, keepdims=True) + NORM_EPS)
    xg = (xt * r) * ng_ref[...]
    h = xg * (1.0 + scale) + shift
    return xt, r, xg, h, scale


def _inproj_fwd(x, ctx, modrows, bmodrows, norm_g, w_in_p, q_lora_g, w_uq_p, kv_lora_g, w_ukv, qng_p, kng_p, cos, slo, shi):
    seq = x.shape[0]
    n_tiles = seq // TILE + 1
    tot = seq + TILE

    n_mla = R_Q + R_KV + 128

    def body(x_ref, ctx_ref, mod_ref, bmod_ref, ng_ref, win_ref, qlg_ref, wuq_ref, kvlg_ref, wukv_ref, qng_ref, kng_ref,
             cos_ref, slo_ref, shi_ref, u_ref, q_ref, k_ref, v_ref, stash_even, stash_odd):
        s = pl.program_id(0)

        @pl.when(s == 0)
        def _():
            stash_odd[...] = jnp.zeros_like(stash_odd)

        refs = (x_ref, ctx_ref, mod_ref, bmod_ref, ng_ref, win_ref, qlg_ref, wuq_ref, kvlg_ref, wukv_ref, qng_ref, kng_ref,
                cos_ref, slo_ref, shi_ref, u_ref, q_ref, k_ref, v_ref)

        @pl.when(lax.rem(s, 2) == 0)
        def _():
            stages(s, refs, stash_even, stash_odd)

        @pl.when(lax.rem(s, 2) == 1)
        def _():
            stages(s, refs, stash_odd, stash_even)

    def stages(s, refs, fill, prev_ref):
        (x_ref, ctx_ref, mod_ref, bmod_ref, ng_ref, win_ref, qlg_ref, wuq_ref, kvlg_ref, wukv_ref, qng_ref, kng_ref,
         cos_ref, slo_ref, shi_ref, u_ref, q_ref, k_ref, v_ref) = refs
        cos_t, slo_t, shi_t = cos_ref[...], slo_ref[...], shi_ref[...]
        prev = prev_ref[...]
        cq = prev[:, 0:R_Q]
        qn = cq * lax.rsqrt(jnp.mean(cq * cq, axis=-1, keepdims=True) + NORM_EPS) * qlg_ref[...]
        q = _dot_nt(qn.astype(BF16), wuq_ref[...])
        qg = qng_ref[...] * (ATTN_SCALE * LOG2_E)
        for hd in range(N_HEADS):
            qh = q[:, hd * D_HEAD_PAD:(hd + 1) * D_HEAD_PAD]
            rr = lax.rsqrt(jnp.sum(qh * qh, axis=-1, keepdims=True) * (1.0 / D_HEAD) + NORM_EPS)
            qy = qh * rr * qg
            q_ref[hd, :, 0:D_NOPE] = qy[:, 0:D_NOPE].astype(BF16)
            q_ref[hd, :, D_NOPE:D_HEAD_PAD] = _rope(qy[:, D_NOPE:D_HEAD_PAD], cos_t, slo_t, shi_t).astype(BF16)

        ckv = prev[:, R_Q:R_Q + R_KV]
        kvn = ckv * lax.rsqrt(jnp.mean(ckv * ckv, axis=-1, keepdims=True) + NORM_EPS) * kvlg_ref[...]
        kv = _dot(kvn.astype(BF16), wukv_ref[...])
        krz = prev[:, R_Q + R_KV:n_mla]
        kr_ss = jnp.sum(krz * krz, axis=-1, keepdims=True)
        kg = kng_ref[...]
        for hd in range(N_HEADS):
            kn = kv[:, hd * 256:hd * 256 + D_NOPE]
            rr = lax.rsqrt((jnp.sum(kn * kn, axis=-1, keepdims=True) + kr_ss) * (1.0 / D_HEAD) + NORM_EPS)
            k_ref[hd, :, 0:D_NOPE] = (kn * rr * kg[:, 0:D_NOPE]).astype(BF16)
            k_ref[hd, :, D_NOPE:D_HEAD_PAD] = _rope(krz * rr * kg[:, D_NOPE:D_HEAD_PAD], cos_t, slo_t, shi_t).astype(BF16)
            v_ref[hd] = kv[:, hd * 256 + D_NOPE:(hd + 1) * 256].astype(BF16)

        _, _, _, h, _ = _modulated_input(s == 0, x_ref, ctx_ref, mod_ref, bmod_ref, ng_ref)
        u = _dot_nt(h.astype(BF16), win_ref[...])
        u_ref[...] = u
        fill[:, 0:R_Q + R_KV] = u[:, O_CQ:O_CQ + R_Q + R_KV]
        fill[:, R_Q + R_KV:n_mla] = u[:, O_KR:U_PAD]

    first = lambda s: jnp.minimum(s, n_tiles - 1)
    second = lambda s: jnp.maximum(s - 1, 0)
    latent = lambda t: jnp.maximum(t - 1, 0)
    full = lambda shape: pl.BlockSpec(shape, lambda s: (0,) * len(shape))
    rope_spec = pl.BlockSpec((TILE, 128), lambda s: (second(s), 0))
    return pl.pallas_call(
        body, name="inproj_fwd", grid=(n_tiles + 1,),
        out_shape=(jax.ShapeDtypeStruct((tot, U_PAD), F32),
                   jax.ShapeDtypeStruct((N_HEADS, seq, D_HEAD_PAD), BF16),
                   jax.ShapeDtypeStruct((N_HEADS, tot, D_HEAD_PAD), BF16),
                   jax.ShapeDtypeStruct((N_HEADS, tot, D_V), BF16)),
        in_specs=[pl.BlockSpec((TILE, D_MODEL), lambda s: (latent(first(s)), 0)), full((TILE, D_MODEL)), full((8, D_MODEL)),
                  full((8, D_MODEL)), full((1, D_MODEL)), full((U_PAD, D_MODEL)), full((1, R_Q)),
                  full((N_HEADS * D_HEAD_PAD, R_Q)), full((1, R_KV)), full((R_KV, N_HEADS * 256)), full((1, D_HEAD_PAD)),
                  full((1, D_HEAD_PAD)), rope_spec, rope_spec, rope_spec],
        out_specs=(pl.BlockSpec((TILE, U_PAD), lambda s: (first(s), 0)),
                   pl.BlockSpec((N_HEADS, TILE, D_HEAD_PAD), lambda s: (0, latent(second(s)), 0)),
                   pl.BlockSpec((N_HEADS, TILE, D_HEAD_PAD), lambda s: (0, second(s), 0)),
                   pl.BlockSpec((N_HEADS, TILE, D_V), lambda s: (0, second(s), 0))),
        scratch_shapes=[pltpu.VMEM((TILE, n_mla), F32), pltpu.VMEM((TILE, n_mla), F32)],
        compiler_params=pltpu.CompilerParams(dimension_semantics=("arbitrary",), vmem_limit_bytes=VMEM_LIMIT),
    )(x, ctx, modrows, bmodrows, norm_g, w_in_p, q_lora_g, w_uq_p, kv_lora_g, w_ukv, qng_p, kng_p, cos, slo, shi)


def _hosted_exchange(first, last, items, send_sems, recv_sems, local_sems):
    me = _lin(_coords())

    def remote(n, k, src, land, scatter):
        peer = _peer(k)
        return pltpu.make_async_remote_copy(
            src_ref=src.at[_lin(peer)] if scatter else src, dst_ref=land.at[me], send_sem=send_sems.at[n, k - 1],
            recv_sem=recv_sems.at[n, k - 1], device_id=peer, device_id_type=MESH)

    def local(n, src, land, scatter):
        return pltpu.make_async_copy(src.at[me] if scatter else src, land.at[me], local_sems.at[n])

    @pl.when(first)
    def _():
        for n, (src, land, scatter) in enumerate(items):
            for k in range(1, N_DEV):
                remote(n, k, src, land, scatter).start()
            local(n, src, land, scatter).start()

    @pl.when(last)
    def _():
        for n, (src, land, scatter) in enumerate(items):
            for k in range(1, N_DEV):
                peer = _peer(k)
                pltpu.make_async_remote_copy(
                    src_ref=src.at[0] if scatter else src, dst_ref=land.at[_lin(peer)], send_sem=send_sems.at[n, k - 1],
                    recv_sem=recv_sems.at[n, k - 1], device_id=peer, device_id_type=MESH).wait_recv()
            for k in range(1, N_DEV):
                remote(n, k, src, land, scatter).wait_send()
            local(n, src, land, scatter).wait()


def _attn_fwd(q, k, v, block_q, w_out_b):
    _, seq, _ = q.shape
    n_keys = k.shape[1]
    n_q = seq // block_q

    def body(q_ref, k_ref, v_ref, wo_ref, o_ref, lse_ref, wog_ref, ssem, rsem, lsem):
        h, i = pl.program_id(0), pl.program_id(1)
        _hosted_exchange((h == 0) & (i == 0), (h == N_HEADS - 1) & (i == n_q - 1), [(wo_ref, wog_ref, False)],
                         ssem, rsem, lsem)
        kb, vb = k_ref[0], v_ref[0]
        for r0 in range(0, block_q, ATTN_ROWS):
            rows = slice(r0, r0 + ATTN_ROWS)
            s = _dot_nt(q_ref[0, rows, :], kb)
            m = jnp.max(s, axis=-1, keepdims=True)
            p = jnp.exp2(s - m)
            l = jnp.sum(p, axis=-1, keepdims=True)
            o_ref[rows, :] = _dot(p.astype(BF16), vb) * (1.0 / l)
            lse_ref[0, rows, :] = m + jnp.log2(l)

    hbm = pl.BlockSpec(memory_space=pl.ANY)
    return pl.pallas_call(
        body, name="attn_fwd", grid=(N_HEADS, n_q),
        out_shape=(jax.ShapeDtypeStruct((seq, D_ATTN), F32), jax.ShapeDtypeStruct((N_HEADS, seq, 1), F32),
                   jax.ShapeDtypeStruct((N_DEV,) + w_out_b.shape, w_out_b.dtype)),
        in_specs=[pl.BlockSpec((1, block_q, D_HEAD_PAD), lambda h, i: (h, i, 0)),
                  pl.BlockSpec((1, n_keys, D_HEAD_PAD), lambda h, i: (h, 0, 0)),
                  pl.BlockSpec((1, n_keys, D_V), lambda h, i: (h, 0, 0)), hbm],
        out_specs=(pl.BlockSpec((block_q, D_V), lambda h, i: (i, h)),
                   pl.BlockSpec((1, block_q, 1), lambda h, i: (h, i, 0)), hbm),
        scratch_shapes=[pltpu.SemaphoreType.DMA((1, 7)), pltpu.SemaphoreType.DMA((1, 7)), pltpu.SemaphoreType.DMA((1,))],
        compiler_params=pltpu.CompilerParams(dimension_semantics=("arbitrary", "arbitrary"), vmem_limit_bytes=VMEM_LIMIT),
    )(q, k, v, w_out_b)


def _attn_bwd(q, k, v, o, lse, d_o, block_q, gwo_blocks, gwp):
    _, seq, _ = q.shape
    n_keys = k.shape[1]
    n_q = seq // block_q

    def body(q_ref, k_ref, v_ref, o_ref, lse_ref, do_ref, gwo_ref, gwp_ref, dq_ref, dkt_ref, dvt_ref, lwo_ref, lwp_ref,
             ssem, rsem, lsem):
        h, i = pl.program_id(0), pl.program_id(1)
        _hosted_exchange((h == 0) & (i == 0), (h == N_HEADS - 1) & (i == n_q - 1),
                         [(gwo_ref, lwo_ref, True), (gwp_ref, lwp_ref, False)], ssem, rsem, lsem)

        @pl.when(i == 0)
        def _():
            dkt_ref[...] = jnp.zeros_like(dkt_ref)
            dvt_ref[...] = jnp.zeros_like(dvt_ref)

        kb, vb = k_ref[0], v_ref[0]
        raws, ps = [], []
        for r0 in range(0, block_q, ATTN_ROWS):
            rows = slice(r0, r0 + ATTN_ROWS)
            d_out = do_ref[rows, :]
            p = jnp.exp2(_dot_nt(q_ref[0, rows, :], kb) - lse_ref[0, rows, :])
            dp = _dot_nt(d_out.astype(BF16), vb)
            delta = jnp.sum(d_out * o_ref[rows, :], axis=-1, keepdims=True)
            raw = (p * (dp - delta)).astype(BF16)
            dq_ref[0, rows, :] = _dot(raw, kb)
            raws.append(raw)
            ps.append(p.astype(BF16))
        dkt_ref[0] += _dot_tn(q_ref[0], jnp.concatenate(raws, axis=0))
        dvt_ref[0] += _dot_tn(do_ref[...].astype(BF16), jnp.concatenate(ps, axis=0))

    hbm = pl.BlockSpec(memory_space=pl.ANY)
    return pl.pallas_call(
        body, name="attn_bwd", grid=(N_HEADS, n_q),
        out_shape=(jax.ShapeDtypeStruct((N_HEADS, seq, D_HEAD_PAD), F32),
                   jax.ShapeDtypeStruct((N_HEADS, D_HEAD_PAD, n_keys), F32),
                   jax.ShapeDtypeStruct((N_HEADS, D_V, n_keys), F32),
                   jax.ShapeDtypeStruct(gwo_blocks.shape, gwo_blocks.dtype),
                   jax.ShapeDtypeStruct((N_DEV,) + gwp.shape, gwp.dtype)),
        in_specs=[pl.BlockSpec((1, block_q, D_HEAD_PAD), lambda h, i: (h, i, 0)),
                  pl.BlockSpec((1, n_keys, D_HEAD_PAD), lambda h, i: (h, 0, 0)),
                  pl.BlockSpec((1, n_keys, D_V), lambda h, i: (h, 0, 0)),
                  pl.BlockSpec((block_q, D_V), lambda h, i: (i, h)),
                  pl.BlockSpec((1, block_q, 1), lambda h, i: (h, i, 0)),
                  pl.BlockSpec((block_q, D_V), lambda h, i: (i, h)), hbm, hbm],
        out_specs=(pl.BlockSpec((1, block_q, D_HEAD_PAD), lambda h, i: (h, i, 0)),
                   pl.BlockSpec((1, D_HEAD_PAD, n_keys), lambda h, i: (h, 0, 0)),
                   pl.BlockSpec((1, D_V, n_keys), lambda h, i: (h, 0, 0)), hbm, hbm),
        scratch_shapes=[pltpu.SemaphoreType.DMA((2, 7)), pltpu.SemaphoreType.DMA((2, 7)), pltpu.SemaphoreType.DMA((2,))],
        compiler_params=pltpu.CompilerParams(dimension_semantics=("arbitrary", "arbitrary"), vmem_limit_bytes=VMEM_LIMIT),
    )(q, k, v, o, lse, d_o, gwo_blocks, gwp)


def _mix(x, target, attn, u, modrows, bmodrows, w_pool, pool_scale, w_out):
    seq = x.shape[0]
    n_tiles = seq // TILE
    rows8 = TILE // HALO
    last8 = (seq + TILE) // HALO - 1

    n_blk = seq // Q_BLOCK
    per = TILE // n_blk
    assert TILE % n_blk == 0 and per % 8 == 0

    def body(x_ref, t_ref, o_hbm, ga_ref, pin_ref, hb_ref, ha_ref, gp_ref, mod_ref, bmod_ref, wp_ref, ps_ref, wo_ref,
             loss_ref, g1_ref, dgate_ref, do_hbm, dga_ref, dgp_ref, dpl_ref, gwob_ref, gwp_ref, dps_ref,
             abuf, dbuf, gwo_ref, rsem, wsem):
        j = pl.program_id(0)

        def slab_copies(tile, slot, fetch):
            out = []
            for b in range(n_blk):
                hbm_rows = pl.ds(b * Q_BLOCK + tile * per, per)
                buf_rows = pl.ds(b * per, per)
                if fetch:
                    out.append(pltpu.make_async_copy(o_hbm.at[hbm_rows], abuf.at[slot, buf_rows], rsem.at[slot]))
                else:
                    out.append(pltpu.make_async_copy(dbuf.at[slot, buf_rows], do_hbm.at[hbm_rows], wsem.at[slot]))
            return out

        def wait_all(slot, fetch):
            if fetch:
                pltpu.make_async_copy(o_hbm.at[pl.ds(0, TILE)], abuf.at[slot], rsem.at[slot]).wait()
            else:
                pltpu.make_async_copy(dbuf.at[slot], do_hbm.at[pl.ds(0, TILE)], wsem.at[slot]).wait()

        slot = lax.rem(j, 2)

        @pl.when(j == 0)
        def _():
            loss_ref[...] = jnp.zeros_like(loss_ref)
            dgate_ref[...] = jnp.zeros_like(dgate_ref)
            gwo_ref[...] = jnp.zeros_like(gwo_ref)
            gwp_ref[...] = jnp.zeros_like(gwp_ref)
            dps_ref[...] = jnp.zeros_like(dps_ref)
            for cp in slab_copies(0, 0, True):
                cp.start()

        @pl.when(j + 1 < n_tiles)
        def _():
            for cp in slab_copies(j + 1, 1 - slot, True):
                cp.start()

        wait_all(slot, True)
        attn_t = jnp.swapaxes(abuf[slot].reshape(n_blk, per, D_ATTN), 0, 1).reshape(TILE, D_ATTN)

        gate = mod_ref[2:3, :] + bmod_ref[2:3, :]
        ga = ga_ref[...]
        sga = _sigmoid(ga)
        silu_ga = ga * sga
        br_a = silu_ga * attn_t

        pin = pin_ref[...]
        xs = jnp.concatenate([jnp.where(j > 0, hb_ref[...], 0.0), pin, jnp.where(j < n_tiles - 1, ha_ref[...], 0.0)], axis=0)
        tpos = j * TILE + lax.broadcasted_iota(jnp.int32, (TILE, 1), 0)
        ps = ps_ref[...]
        pooled, yps = [], []
        for g, w in enumerate(POOL_WINDOWS):
            sl = slice(g * GROUP_DIM, (g + 1) * GROUP_DIM)
            ws = _window_sum(xs[:, sl], w, False)[HALO:HALO + TILE]
            pg = (ws * _inv_count(tpos, w, seq) - pin[:, sl]).astype(BF16)
            pooled.append(pg)
            yps.append(_dot(pg, wp_ref[g].astype(BF16)))
        yp = jnp.concatenate(yps, axis=1)
        ypool = yp * ps
        gp = gp_ref[...]
        sgp = _sigmoid(gp)
        silu_gp = gp * sgp
        br_p = silu_gp * ypool

        z = jnp.concatenate([br_a, br_p], axis=1).astype(BF16)
        y = _dot(z, wo_ref[...])
        res = x_ref[...] + gate * y - t_ref[...]
        loss_ref[...] += jnp.sum(res * res) * (0.5 / D_MODEL)
        dxn = res * (1.0 / D_MODEL)
        g1_ref[...] = dxn
        dgate_ref[...] += jnp.sum(dxn * y, axis=0, keepdims=True)
        dy = (gate * dxn).astype(BF16)
        dz = _dot_nt(dy, wo_ref[...])
        gwo_ref[...] += _dot_tn(z, dy)

        dbra = dz[:, 0:D_ATTN]
        dbrp = dz[:, D_ATTN:]
        dga_ref[...] = dbra * attn_t * (sga * (1.0 + ga * (1.0 - sga)))

        @pl.when(j >= 2)
        def _():
            wait_all(slot, False)
        dbuf[slot] = jnp.swapaxes((dbra * silu_ga).reshape(per, n_blk, D_ATTN), 0, 1).reshape(TILE, D_ATTN)
        for cp in slab_copies(j, slot, False):
            cp.start()

        @pl.when(j == n_tiles - 1)
        def _():
            wait_all(slot, False)
            if n_tiles >= 2:
                wait_all(1 - slot, False)

        dgp_ref[...] = dbrp * ypool * (sgp * (1.0 + gp * (1.0 - sgp)))
        dyp_s = dbrp * silu_gp
        dps_ref[...] += jnp.sum(dyp_s * yp, axis=0, keepdims=True)
        dyp = (dyp_s * ps).astype(BF16)
        for g in range(len(POOL_WINDOWS)):
            sl = slice(g * GROUP_DIM, (g + 1) * GROUP_DIM)
            gwp_ref[g] += _dot_tn(pooled[g], dyp[:, sl])
            dpl_ref[:, sl] = _dot_nt(dyp[:, sl], wp_ref[g].astype(BF16))

        @pl.when(j == n_tiles - 1)
        def _():
            gwob_ref[...] = gwo_ref[...].astype(BF16)

    tile = lambda w: pl.BlockSpec((TILE, w), lambda j: (j, 0))
    ucol = lambda col: pl.BlockSpec((TILE, 512), lambda j: (j + 1, col))
    full = lambda shape: pl.BlockSpec(shape, lambda j: (0,) * len(shape))
    return pl.pallas_call(
        body, name="mix_fwd_bwd", grid=(n_tiles,),
        out_shape=(jax.ShapeDtypeStruct((8, 128), F32), jax.ShapeDtypeStruct((seq, D_MODEL), F32),
                   jax.ShapeDtypeStruct((1, D_MODEL), F32), jax.ShapeDtypeStruct((seq, D_ATTN), F32),
                   jax.ShapeDtypeStruct((seq, D_ATTN), F32), jax.ShapeDtypeStruct((seq, D_POOL), F32),
                   jax.ShapeDtypeStruct((seq, D_POOL), F32), jax.ShapeDtypeStruct((D_MODEL, D_MODEL), BF16),
                   jax.ShapeDtypeStruct((4, GROUP_DIM, GROUP_DIM), F32), jax.ShapeDtypeStruct((1, D_POOL), F32)),
        in_specs=[tile(D_MODEL), tile(D_MODEL), pl.BlockSpec(memory_space=pl.ANY), ucol(0), ucol(1),
                  pl.BlockSpec((HALO, 512), lambda j: ((j + 1) * rows8 - 1, 1)),
                  pl.BlockSpec((HALO, 512), lambda j: (jnp.minimum((j + 2) * rows8, last8), 1)),
                  ucol(2), full((8, D_MODEL)), full((8, D_MODEL)), full((4, GROUP_DIM, GROUP_DIM)), full((1, D_POOL)),
                  full((D_MODEL, D_MODEL))],
        out_specs=(full((8, 128)), tile(D_MODEL), full((1, D_MODEL)), pl.BlockSpec(memory_space=pl.ANY), tile(D_ATTN),
                   tile(D_POOL), tile(D_POOL), full((D_MODEL, D_MODEL)), full((4, GROUP_DIM, GROUP_DIM)), full((1, D_POOL))),
        scratch_shapes=[pltpu.VMEM((2, TILE, D_ATTN), F32), pltpu.VMEM((2, TILE, D_ATTN), F32),
                        pltpu.VMEM((D_MODEL, D_MODEL), F32), pltpu.SemaphoreType.DMA((2,)), pltpu.SemaphoreType.DMA((2,))],
        compiler_params=pltpu.CompilerParams(dimension_semantics=("arbitrary",), vmem_limit_bytes=VMEM_LIMIT),
    )(x, target, attn, u, u, u, u, u, modrows, bmodrows, w_pool, pool_scale, w_out)


def _inproj_bwd(x, ctx, modrows, bmodrows, norm_g, w_in_p, q_lora_g, w_uq_p, kv_lora_g, w_ukv, qng_p, kng_p, cos, slo, shi,
                u, dq, dk, dv, dga, dgp, dpl, g1):
    seq = x.shape[0]
    n_tiles = seq // TILE + 1
    rows8 = TILE // HALO
    last8 = seq // HALO - 1

    def body(*refs):
        du_even, du_odd, dh_even, dh_odd = refs[-4:]
        gwin_ref, gwuq_ref, gwukv_ref, dqlg_ref, dkvlg_ref, dqng_ref, dkng_ref, dng_ref, dmod_ref = refs[-13:-4]
        s = pl.program_id(0)

        @pl.when(s == 0)
        def _():
            for ref in (gwin_ref, gwuq_ref, gwukv_ref, dqlg_ref, dkvlg_ref, dqng_ref, dkng_ref, dng_ref, dmod_ref,
                        du_odd, dh_even):
                ref[...] = jnp.zeros_like(ref)

        @pl.when(lax.rem(s, 2) == 0)
        def _():
            stages(s, refs[:-4], du_even, du_odd, dh_odd, dh_even)

        @pl.when(lax.rem(s, 2) == 1)
        def _():
            stages(s, refs[:-4], du_odd, du_even, dh_even, dh_odd)

    def stages(s, refs, du_ref, du_prev, dh_fill, dh_prev):
        (xb_ref, xc_ref, ctx_ref, mod_ref, bmod_ref, ng_ref, win_ref, qlg_ref, wuq_ref, kvlg_ref, wukv_ref, qng_ref, kng_ref,
         cos_ref, slo_ref, shi_ref, cq_ref, ckv_ref, kr_ref, dq_ref, dk_ref, dv_ref, dga_ref, dgp_ref, dpl_ref,
         hb_ref, ha_ref, g1_ref,
         gx_ref, gwin_ref, gwuq_ref, gwukv_ref, dqlg_ref, dkvlg_ref, dqng_ref, dkng_ref, dng_ref, dmod_ref) = refs


        i = jnp.minimum(s, n_tiles - 1)
        valid = s < n_tiles
        is_lat = (s > 0) & valid
        cq = cq_ref[...]
        rq = lax.rsqrt(jnp.mean(cq * cq, axis=-1, keepdims=True) + NORM_EPS)
        qhat = cq * rq
        qnb = (qhat * qlg_ref[...]).astype(BF16)
        q = _dot_nt(qnb, wuq_ref[...])
        ckv = ckv_ref[...]
        rkv = lax.rsqrt(jnp.mean(ckv * ckv, axis=-1, keepdims=True) + NORM_EPS)
        kvhat = ckv * rkv
        kvnb = (kvhat * kvlg_ref[...]).astype(BF16)
        kv = _dot(kvnb, wukv_ref[...])

        _, _, _, h2, _ = _modulated_input(s <= 1, xb_ref, ctx_ref, mod_ref, bmod_ref, ng_ref)
        dub = du_prev[...]
        dh_fill[...] = _dot(dub, win_ref[...])
        gwin_ref[...] += _dot_tn(dub, h2.astype(BF16))

        ctx3 = s <= 2
        xt, r, xg, _, scale = _modulated_input(ctx3, xc_ref, ctx_ref, mod_ref, bmod_ref, ng_ref)
        dh = dh_prev[...]
        dshift = jnp.sum(dh, axis=0, keepdims=True)
        dscale = jnp.sum(dh * xg, axis=0, keepdims=True)
        zero = jnp.zeros_like(dshift)
        dmod_ref[0:1, :] += jnp.where(ctx3, zero, dshift)
        dmod_ref[1:2, :] += jnp.where(ctx3, zero, dscale)
        dmod_ref[2:3, :] += jnp.where(ctx3, dshift, zero)
        dmod_ref[3:4, :] += jnp.where(ctx3, dscale, zero)
        dxg = dh * (1.0 + scale)
        xr = xt * r
        dng_ref[...] += jnp.sum(dxg * xr, axis=0, keepdims=True)
        dxn = dxg * ng_ref[...]
        gx_ref[...] = g1_ref[...] + r * (dxn - xr * jnp.mean(dxn * xr, axis=-1, keepdims=True))

        cos_t, slo_t, shi_t = cos_ref[...], slo_ref[...], shi_ref[...]
        qg = qng_ref[...]
        dq_heads = []
        dqng = jnp.zeros((1, D_HEAD_PAD), F32)
        for hd in range(N_HEADS):
            qh = q[:, hd * D_HEAD_PAD:(hd + 1) * D_HEAD_PAD]
            rr = lax.rsqrt(jnp.sum(qh * qh, axis=-1, keepdims=True) * (1.0 / D_HEAD) + NORM_EPS)
            yq = qh * rr
            dpost = jnp.where(is_lat, dq_ref[hd] * ATTN_SCALE, 0.0)
            dyn = jnp.concatenate([dpost[:, 0:D_NOPE], _rope_t(dpost[:, D_NOPE:], cos_t, slo_t, shi_t)], axis=1)
            dqng = dqng + jnp.sum(dyn * yq, axis=0, keepdims=True)
            dyg = dyn * qg
            dq_heads.append(rr * (dyg - yq * (jnp.sum(dyg * yq, axis=-1, keepdims=True) * (1.0 / D_HEAD))))
        dqng_ref[...] += dqng
        dqf = jnp.concatenate(dq_heads, axis=1).astype(BF16)

        krz = kr_ref[...]
        kr_ss = jnp.sum(krz * krz, axis=-1, keepdims=True)
        kg = kng_ref[...]
        kg_n, kg_r = kg[:, 0:D_NOPE], kg[:, D_NOPE:]
        dkv_parts = []
        dkr = jnp.zeros((TILE, 128), F32)
        dkng_n = jnp.zeros((1, D_NOPE), F32)
        dkng_r = jnp.zeros((1, 128), F32)
        for hd in range(N_HEADS):
            kn = kv[:, hd * 256:hd * 256 + D_NOPE]
            rr = lax.rsqrt((jnp.sum(kn * kn, axis=-1, keepdims=True) + kr_ss) * (1.0 / D_HEAD) + NORM_EPS)
            yn, yr = kn * rr, krz * rr
            dk_h = jnp.where(valid, jnp.transpose(dk_ref[hd]) * LN_2, 0.0)
            dpn = dk_h[:, 0:D_NOPE]
            dpr = _rope_t(dk_h[:, D_NOPE:], cos_t, slo_t, shi_t)
            dkng_n = dkng_n + jnp.sum(dpn * yn, axis=0, keepdims=True)
            dkng_r = dkng_r + jnp.sum(dpr * yr, axis=0, keepdims=True)
            dyn, dyr = dpn * kg_n, dpr * kg_r
            proj = (jnp.sum(dyn * yn, axis=-1, keepdims=True) + jnp.sum(dyr * yr, axis=-1, keepdims=True)) * (1.0 / D_HEAD)
            dkv_parts.append(rr * (dyn - yn * proj))
            dkv_parts.append(jnp.where(valid, jnp.transpose(dv_ref[hd]), 0.0))
            dkr = dkr + rr * (dyr - yr * proj)
        dkng_ref[:, 0:D_NOPE] += dkng_n
        dkng_ref[:, D_NOPE:] += dkng_r
        dkvf = jnp.concatenate(dkv_parts, axis=1).astype(BF16)

        lat_t = jnp.maximum(i - 1, 0)
        dpl_t = dpl_ref[...]
        ds = jnp.concatenate([jnp.where(i > 1, hb_ref[...], 0.0), dpl_t,
                              jnp.where(i < n_tiles - 1, ha_ref[...], 0.0)], axis=0)
        tpos = lat_t * TILE - HALO + lax.broadcasted_iota(jnp.int32, (TILE + 2 * HALO, 1), 0)
        for g, w in enumerate(POOL_WINDOWS):
            sl = slice(g * GROUP_DIM, (g + 1) * GROUP_DIM)
            wsum = _window_sum(ds[:, sl] * _inv_count(tpos, w, seq), w, True)[HALO:HALO + TILE]
            du_ref[:, O_PIN + g * GROUP_DIM:O_PIN + (g + 1) * GROUP_DIM] = jnp.where(
                is_lat, wsum - dpl_t[:, sl], 0.0).astype(BF16)

        du_ref[:, O_GA:O_GA + D_ATTN] = jnp.where(is_lat, dga_ref[...], 0.0).astype(BF16)
        du_ref[:, O_GP:O_GP + D_POOL] = jnp.where(is_lat, dgp_ref[...], 0.0).astype(BF16)
        du_ref[:, O_KR:U_PAD] = dkr.astype(BF16)

        gwuq_ref[...] += _dot_tn(dqf, qnb)
        dqn = _dot(dqf, wuq_ref[...])
        gwukv_ref[...] += _dot_tn(dkvf, kvnb)
        dkvn = _dot_nt(dkvf, wukv_ref[...])
        dqlg_ref[...] += jnp.sum(dqn * qhat, axis=0, keepdims=True)
        dqhat = dqn * qlg_ref[...]
        dcq = rq * (dqhat - qhat * jnp.mean(dqhat * qhat, axis=-1, keepdims=True))
        dkvlg_ref[...] += jnp.sum(dkvn * kvhat, axis=0, keepdims=True)
        dkvhat = dkvn * kvlg_ref[...]
        dckv = rkv * (dkvhat - kvhat * jnp.mean(dkvhat * kvhat, axis=-1, keepdims=True))
        du_ref[:, O_CQ:O_CQ + R_Q] = dcq.astype(BF16)
        du_ref[:, O_CKV:O_CKV + R_KV] = dckv.astype(BF16)

    t1 = lambda s: jnp.minimum(s, n_tiles - 1)
    t2 = lambda s: jnp.clip(s - 1, 0, n_tiles - 1)
    t3 = lambda s: jnp.clip(s - 2, 0, n_tiles - 1)
    latent = lambda t: jnp.maximum(t - 1, 0)
    lat = lambda s: (latent(t1(s)), 0)
    lat3 = lambda s: (latent(t3(s)), 0)
    full = lambda shape: pl.BlockSpec(shape, lambda s: (0,) * len(shape))
    rope_spec = pl.BlockSpec((TILE, 128), lambda s: (t1(s), 0))
    return pl.pallas_call(
        body, name="inproj_bwd", grid=(n_tiles + 2,),
        out_shape=(jax.ShapeDtypeStruct((seq, D_MODEL), F32), jax.ShapeDtypeStruct((U_PAD, D_MODEL), F32),
                   jax.ShapeDtypeStruct((N_HEADS * D_HEAD_PAD, R_Q), F32), jax.ShapeDtypeStruct((N_HEADS * 256, R_KV), F32),
                   jax.ShapeDtypeStruct((1, R_Q), F32), jax.ShapeDtypeStruct((1, R_KV), F32),
                   jax.ShapeDtypeStruct((1, D_HEAD_PAD), F32), jax.ShapeDtypeStruct((1, D_HEAD_PAD), F32),
                   jax.ShapeDtypeStruct((1, D_MODEL), F32), jax.ShapeDtypeStruct((8, D_MODEL), F32)),
        in_specs=[pl.BlockSpec((TILE, D_MODEL), lambda s: (latent(t2(s)), 0)), pl.BlockSpec((TILE, D_MODEL), lat3),
                  full((TILE, D_MODEL)), full((8, D_MODEL)), full((8, D_MODEL)),
                  full((1, D_MODEL)), full((U_PAD, D_MODEL)), full((1, R_Q)), full((N_HEADS * D_HEAD_PAD, R_Q)),
                  full((1, R_KV)), full((R_KV, N_HEADS * 256)), full((1, D_HEAD_PAD)), full((1, D_HEAD_PAD)),
                  rope_spec, rope_spec, rope_spec,
                  pl.BlockSpec((TILE, R_Q), lambda s: (t1(s), O_CQ // R_Q)),
                  pl.BlockSpec((TILE, R_KV), lambda s: (t1(s), O_CKV // R_KV)),
                  pl.BlockSpec((TILE, 128), lambda s: (t1(s), O_KR // 128)),
                  pl.BlockSpec((N_HEADS, TILE, D_HEAD_PAD), lambda s: (0, latent(t1(s)), 0)),
                  pl.BlockSpec((N_HEADS, D_HEAD_PAD, TILE), lambda s: (0, 0, t1(s))),
                  pl.BlockSpec((N_HEADS, D_V, TILE), lambda s: (0, 0, t1(s))),
                  pl.BlockSpec((TILE, D_ATTN), lat), pl.BlockSpec((TILE, D_POOL), lat), pl.BlockSpec((TILE, D_POOL), lat),
                  pl.BlockSpec((HALO, D_POOL), lambda s: (jnp.maximum((t1(s) - 1) * rows8 - 1, 0), 0)),
                  pl.BlockSpec((HALO, D_POOL), lambda s: (jnp.minimum(jnp.maximum(t1(s), 1) * rows8, last8), 0)),
                  pl.BlockSpec((TILE, D_MODEL), lat3)],
        out_specs=(pl.BlockSpec((TILE, D_MODEL), lat3), full((U_PAD, D_MODEL)), full((N_HEADS * D_HEAD_PAD, R_Q)),
                   full((N_HEADS * 256, R_KV)), full((1, R_Q)), full((1, R_KV)), full((1, D_HEAD_PAD)),
                   full((1, D_HEAD_PAD)), full((1, D_MODEL)), full((8, D_MODEL))),
        scratch_shapes=[pltpu.VMEM((TILE, U_PAD), BF16), pltpu.VMEM((TILE, U_PAD), BF16),
                        pltpu.VMEM((TILE, D_MODEL), F32), pltpu.VMEM((TILE, D_MODEL), F32)],
        compiler_params=pltpu.CompilerParams(dimension_semantics=("arbitrary",), vmem_limit_bytes=VMEM_LIMIT),
    )(x, x, ctx, modrows, bmodrows, norm_g, w_in_p, q_lora_g, w_uq_p, kv_lora_g, w_ukv, qng_p, kng_p, cos, slo, shi,
      u, u, u, dq, dk, dv, dga, dgp, dpl, dpl, dpl, g1)


SMALL_LAYOUT = ((0, D_MODEL), (1, R_Q), (2, R_KV), (3, D_HEAD_PAD), (4, D_HEAD_PAD), (5, D_POOL))
ROW_DMOD_B, ROW_DMOD_C, ROW_LOSS = 6, 9, 12


def _epilogue(parts, landed, smalls, dmod4, dgate, loss_acc, w_mod_l):
    n_a, n_l, n_s = len(parts), len(landed), len(smalls)
    n_mod = w_mod_l.shape[1]
    blk = [p.shape[1:] for p in parts]
    small_out = [D_MODEL, R_Q, R_KV, D_HEAD, D_HEAD, D_POOL]

    def body(*refs):
        part_refs = refs[0:n_a]
        land_refs = refs[n_a:n_a + n_l]
        small_in = refs[n_a + n_l:n_a + n_l + n_s]
        dmod4_ref, dgate_ref, loss_ref, wmod_ref = refs[n_a + n_l + n_s:n_a + n_l + n_s + 4]
        outs = refs[n_a + n_l + n_s + 4:]
        red_refs = outs[0:n_a]
        landsum_refs = outs[n_a:n_a + n_l]
        ccg_ref = outs[n_a + n_l]
        sum_refs = outs[n_a + n_l + 1:n_a + n_l + 1 + n_s]
        gbmod_ref, dmy_ref, lossout_ref = outs[n_a + n_l + 1 + n_s:n_a + n_l + 4 + n_s]
        scr = outs[n_a + n_l + 4 + n_s:]
        recv1, own1, sum1b, recv2 = scr[0:n_a], scr[n_a:2 * n_a], scr[2 * n_a:3 * n_a], scr[3 * n_a:4 * n_a]
        small_ref, smallg_ref, tot_ref, cc_ref = scr[4 * n_a:4 * n_a + 4]
        ssem1, rsem1, lsem, ssem2, rsem2, ssem_s, rsem_s, ssem_cc, rsem_cc = scr[4 * n_a + 4:]
        x, y, c = _coords()
        me = (x, y, c)
        sibling = (x, y, 1 - c)

        stage1, own_copies = [], []
        for a in range(n_a):
            for b in range(4):
                dev = 4 * (b >> 1) + 2 * (b & 1)
                stage1.append(pltpu.make_async_remote_copy(
                    src_ref=part_refs[a].at[dev + (1 - c)], dst_ref=recv1[a].at[b],
                    send_sem=ssem1.at[a, b], recv_sem=rsem1.at[a, b], device_id=sibling, device_id_type=MESH))
                own_copies.append(pltpu.make_async_copy(part_refs[a].at[dev + c], own1[a].at[b], lsem.at[a, b]))
                stage1[-1].start()
                own_copies[-1].start()

        small_ref[...] = jnp.zeros_like(small_ref)
        for ref, (row, width) in zip(small_in, SMALL_LAYOUT):
            small_ref[row:row + 1, 0:width] = ref[...]
        small_ref[ROW_DMOD_B:ROW_DMOD_B + 2, :] = dmod4_ref[0:2, :]
        small_ref[ROW_DMOD_B + 2:ROW_DMOD_B + 3, :] = dgate_ref[...]
        small_ref[ROW_DMOD_C:ROW_DMOD_C + 2, :] = dmod4_ref[2:4, :]
        small_ref[ROW_LOSS:ROW_LOSS + 1, 0:128] = loss_ref[0:1, :]
        smallg_ref[_lin(me)] = small_ref[...]
        s_recv, s_send = _gather_to_all(small_ref, smallg_ref, ssem_s, rsem_s)
        s_recv()
        tot = smallg_ref[0]
        for d in range(1, N_DEV):
            tot = tot + smallg_ref[d]
        tot_ref[...] = tot
        for ref, (row, _), width in zip(sum_refs, SMALL_LAYOUT, small_out):
            ref[...] = tot_ref[row:row + 1, 0:width]
        lossout_ref[...] = tot_ref[8:16, 0:128]
        lin = _lin(me)

        def my_columns(three_rows):
            v = jnp.concatenate(three_rows, axis=1)
            out = jnp.zeros((1, n_mod), F32)
            for d in range(N_DEV):
                out = out + jnp.where(lin == d, v[:, d * n_mod:(d + 1) * n_mod], 0.0)
            return out

        rows3 = lambda ref, r0: [ref[r0 + t:r0 + t + 1, :] for t in range(3)]
        dmodc = rows3(tot_ref, ROW_DMOD_C)
        gbmod_ref[...] = jnp.concatenate(rows3(tot_ref, ROW_DMOD_B), axis=1) + jnp.concatenate(dmodc, axis=1)
        dmy_ref[...] = jnp.zeros_like(dmy_ref)
        for b in range(N_DEV):
            dmy_ref[b:b + 1, :] = my_columns(rows3(smallg_ref.at[b], ROW_DMOD_B))
        dmy = my_columns(dmodc)
        dmy_ref[N_DEV:N_DEV + 1, :] = dmy
        part = lax.dot_general(jnp.broadcast_to(dmy, (8, n_mod)), wmod_ref[...], (((1,), (1,)), ((), ())),
                               precision=HIGHEST, preferred_element_type=F32)

        cc_ref[...] = part
        ccg_ref[_lin(me)] = part
        cc_recv, cc_send = _gather_to_all(cc_ref, ccg_ref, ssem_cc, rsem_cc)

        stage2 = []
        for a in range(n_a):
            for b in range(4):
                stage1[4 * a + b].wait_recv()
                own_copies[4 * a + b].wait()
                sum1b[a][b] = (own1[a][b] + recv1[a][b]).astype(BF16)
            for k in (1, 2, 3):
                tx, ty = _flip(x, (k >> 1) & 1), _flip(y, k & 1)
                stage2.append(pltpu.make_async_remote_copy(
                    src_ref=sum1b[a].at[2 * tx + ty], dst_ref=recv2[a].at[k - 1], send_sem=ssem2.at[a, k - 1],
                    recv_sem=rsem2.at[a, k - 1], device_id=(tx, ty, c), device_id_type=MESH))
                stage2[-1].start()
        for a in range(n_a):
            own = own1[a][2 * x + y] + recv1[a][2 * x + y]
            for k in (1, 2, 3):
                stage2[3 * a + k - 1].wait_recv()
                own = own + recv2[a][k - 1].astype(F32)
            red_refs[a][...] = own

        for ref, out in zip(land_refs, landsum_refs):
            acc = ref[0].astype(F32)
            for d in range(1, N_DEV):
                acc = acc + ref[d].astype(F32)
            out[...] = acc
        cc_recv()
        for cp in stage1 + stage2:
            cp.wait_send()
        s_send()
        cc_send()

    vm = pl.BlockSpec(memory_space=pltpu.VMEM)
    sds = jax.ShapeDtypeStruct
    return pl.pallas_call(
        body, name="epilogue_reduce",
        out_shape=(*[sds(s, F32) for s in blk], *[sds(a.shape[1:], F32) for a in landed], sds((N_DEV, 8, D_MODEL), F32),
                   *[sds((1, w), F32) for w in small_out], sds((1, 3 * D_MODEL), F32), sds((16, n_mod), F32),
                   sds((8, 128), F32)),
        in_specs=[pl.BlockSpec(memory_space=pl.ANY)] * n_a + [vm] * (n_l + n_s + 4),
        out_specs=tuple([vm] * (n_a + n_l + n_s + 4)),
        scratch_shapes=[*[pltpu.VMEM((4,) + s, F32) for s in blk], *[pltpu.VMEM((4,) + s, F32) for s in blk],
                        *[pltpu.VMEM((4,) + s, BF16) for s in blk], *[pltpu.VMEM((3,) + s, BF16) for s in blk],
                        pltpu.VMEM((SMALL_ROWS, D_MODEL), F32), pltpu.VMEM((N_DEV, SMALL_ROWS, D_MODEL), F32),
                        pltpu.VMEM((SMALL_ROWS, D_MODEL), F32), pltpu.VMEM((8, D_MODEL), F32),
                        pltpu.SemaphoreType.DMA((n_a, 4)), pltpu.SemaphoreType.DMA((n_a, 4)), pltpu.SemaphoreType.DMA((n_a, 4)),
                        pltpu.SemaphoreType.DMA((n_a, 3)), pltpu.SemaphoreType.DMA((n_a, 3)),
                        pltpu.SemaphoreType.DMA((7,)), pltpu.SemaphoreType.DMA((7,)),
                        pltpu.SemaphoreType.DMA((7,)), pltpu.SemaphoreType.DMA((7,))],
        compiler_params=pltpu.CompilerParams(vmem_limit_bytes=VMEM_LIMIT),
    )(*parts, *landed, *smalls, dmod4, dgate, loss_acc, w_mod_l)


def _adamw(weights, grads, ms, vs, c_all_t, dmod_my, p2g):
    n = len(weights)

    def body(*refs):
        w_refs = refs[0:n]
        g_in = refs[n:2 * n - 2]
        m_refs = refs[2 * n - 2:3 * n - 2]
        v_refs = refs[3 * n - 2:4 * n - 2]
        cat_ref, dmod_ref, p2g_ref = refs[4 * n - 2:4 * n + 1]
        outs = refs[4 * n + 1:]
        gcc_ref, gwm_ref = outs[0], outs[1]
        d_refs, nm_refs, nv_refs = outs[2:2 + n], outs[2 + n:2 + 2 * n], outs[2 + 2 * n:2 + 3 * n]

        part = p2g_ref[0, 0:1, :]
        for d in range(1, N_DEV):
            part = part + p2g_ref[d, 0:1, :]
        cc = w_refs[0][...]
        sg = _sigmoid(cc)
        gcc_ref[...] = part * (sg * (1.0 + cc * (1.0 - sg)))
        cat = cat_ref[...]
        gwm_ref[...] = lax.dot_general(cat * _sigmoid(cat), dmod_ref[...], (((1,), (0,)), ((), ())), precision=HIGHEST,
                                       preferred_element_type=F32)
        for idx in range(n):
            g = (gcc_ref, gwm_ref)[idx][...] if idx < 2 else g_in[idx - 2][...]
            m = ADAM_B1 * m_refs[idx][...] + (1.0 - ADAM_B1) * g
            v = ADAM_B2 * v_refs[idx][...] + (1.0 - ADAM_B2) * (g * g)
            m_hat = m / (1.0 - ADAM_B1 ** ADAM_STEP)
            v_hat = v / (1.0 - ADAM_B2 ** ADAM_STEP)
            d_refs[idx][...] = -ADAM_LR * (m_hat / (jnp.sqrt(v_hat) + ADAM_EPS) + ADAM_WD * w_refs[idx][...])
            nm_refs[idx][...] = m
            nv_refs[idx][...] = v

    vm = pl.BlockSpec(memory_space=pltpu.VMEM)
    shapes = [jax.ShapeDtypeStruct(w.shape, F32) for w in weights]
    args = list(weights) + list(grads[2:]) + list(ms) + list(vs) + [c_all_t, dmod_my, p2g]
    out_shape = tuple(shapes[0:2] + shapes * 3)
    return pl.pallas_call(
        body, name="adamw_update", out_shape=out_shape, in_specs=[vm] * len(args), out_specs=tuple([vm] * len(out_shape)),
        compiler_params=pltpu.CompilerParams(vmem_limit_bytes=VMEM_LIMIT),
    )(*args)


def _rope_tables(seq):
    rows = seq // GRID_W
    row = np.repeat(np.arange(rows, dtype=np.float32), GRID_W)
    col = np.tile(np.arange(GRID_W, dtype=np.float32), rows)
    n_freq = D_ROPE // 4
    inv = (np.float32(ROPE_BASE) ** (-np.arange(n_freq, dtype=np.float32) / np.float32(n_freq))).astype(np.float32)
    ang_r = (row[:, None] * inv).astype(np.float32)
    ang_c = (col[:, None] * inv).astype(np.float32)
    ang = np.concatenate([ang_r, ang_r, ang_c, ang_c], axis=-1)
    cos, sin = np.cos(ang).astype(np.float32), np.sin(ang).astype(np.float32)
    low = (np.arange(D_ROPE) % 32) < 16

    def table(t, fill):
        out = np.full((TILE + seq, 128), fill, np.float32)
        out[TILE:, 0:D_ROPE] = t
        return jnp.asarray(out)

    return table(cos, 1.0), table(np.where(low, -sin, 0.0), 0.0), table(np.where(low, 0.0, sin), 0.0)


def kernel(x, c, ctx, c_ctx, w_mod, b_mod, norm_g, w_in, q_lora_g, w_uq, kv_lora_g, w_ukv, q_norm_g, k_norm_g, w_pool, pool_scale, w_out, loss_target, m_c_ctx, m_w_mod, m_b_mod, m_norm_g, m_w_in, m_q_lora_g, m_w_uq, m_kv_lora_g, m_w_ukv, m_q_norm_g, m_k_norm_g, m_w_pool, m_pool_scale, m_w_out, v_c_ctx, v_w_mod, v_b_mod, v_norm_g, v_w_in, v_q_lora_g, v_w_uq, v_kv_lora_g, v_w_ukv, v_q_norm_g, v_k_norm_g, v_w_pool, v_pool_scale, v_w_out):
    seq = x.shape[1]
    assert ctx.shape[1] == TILE and seq % TILE == 0 and seq % GRID_W == 0
    ix, iy, ic = lax.axis_index("x"), lax.axis_index("y"), lax.axis_index("c")
    me = 4 * ix + 2 * iy + ic
    x2, ctx2, tgt2 = x[0], ctx[0], loss_target[0]
    w_mod_l, w_ukv_l, w_out_l = w_mod[0], w_ukv[0], w_out[0]
    c_ctx_row = c_ctx.reshape(1, D_MODEL)

    tr = lambda a: jnp.transpose(a[0])
    w_in_tl, w_uq_tl = tr(w_in), tr(w_uq)
    cg, modg, wg_in, wg_uq, wg_ukv = _prologue(
        jnp.broadcast_to(c, (8, D_MODEL)), jnp.broadcast_to(c_ctx_row, (8, D_MODEL)), w_mod_l,
        [w_in_tl, w_uq_tl, w_ukv_l])
    mod_all = jnp.transpose(modg, (1, 0, 2)).reshape(16, 3 * D_MODEL)
    mod_b = lax.dynamic_slice_in_dim(mod_all, me, 1, axis=0).reshape(3, D_MODEL)
    mod_c = mod_all[8].reshape(3, D_MODEL)
    modrows = jnp.concatenate([mod_b, mod_c[0:2], jnp.zeros((3, D_MODEL), F32)], axis=0)
    b3 = b_mod.reshape(3, D_MODEL)
    bmodrows = jnp.concatenate([b3, b3[0:2], jnp.zeros((3, D_MODEL), F32)], axis=0)

    w_in_t = wg_in.reshape(D_IN_PROJ, D_MODEL)
    w_in_p = jnp.concatenate([w_in_t[448:], w_in_t[0:384], w_in_t[384:448],
                              jnp.zeros((U_PAD - D_IN_PROJ, D_MODEL), BF16)], axis=0)
    w_uq_p = jnp.concatenate([wg_uq.reshape(N_HEADS, D_HEAD, R_Q), jnp.zeros((N_HEADS, D_HEAD_PAD - D_HEAD, R_Q), BF16)],
                             axis=1).reshape(N_HEADS * D_HEAD_PAD, R_Q)
    w_ukv_f = jnp.transpose(wg_ukv, (1, 0, 2)).reshape(R_KV, N_HEADS * 256)
    qng_p = jnp.concatenate([q_norm_g, jnp.zeros((1, D_HEAD_PAD - D_HEAD), F32)], axis=1)
    kng_p = jnp.concatenate([k_norm_g, jnp.zeros((1, D_HEAD_PAD - D_HEAD), F32)], axis=1)
    cos, slo, shi = _rope_tables(seq)

    u, q, k, v = _inproj_fwd(x2, ctx2, modrows, bmodrows, norm_g, w_in_p, q_lora_g, w_uq_p, kv_lora_g, w_ukv_f,
                             qng_p, kng_p, cos, slo, shi)
    attn, lse, wg_out = _attn_fwd(q, k, v, min(2048, seq), w_out_l.astype(BF16))
    (loss_acc, g1, dgate, d_attn, dga, dgp, dpl, gwo_b, gwp, dps) = _mix(
        x2, tgt2, attn, u, modrows, bmodrows, w_pool[0], pool_scale, wg_out.reshape(D_MODEL, D_MODEL))

    blocks = lambda a: a.reshape((N_DEV, a.shape[0] // N_DEV) + a.shape[1:])
    dq, dk, dv, land_wo, land_wp = _attn_bwd(q, k, v, attn, lse, d_attn, min(1024, seq), blocks(gwo_b),
                                             gwp.reshape(4 * GROUP_DIM, GROUP_DIM))
    (grad_x, gwin_p, gwuq_p, gwukv_t, dqlg, dkvlg, dqng, dkng, dng, dmod4) = _inproj_bwd(
        x2, ctx2, modrows, bmodrows, norm_g, w_in_p, q_lora_g, w_uq_p, kv_lora_g, w_ukv_f, qng_p, kng_p, cos, slo, shi,
        u, dq, dk, dv, dga, dgp, dpl, g1)

    gwin_t = jnp.concatenate([gwin_p[O_CQ:O_KR], gwin_p[O_KR:D_IN_PROJ], gwin_p[0:O_CQ]], axis=0)
    gwuq_t = gwuq_p.reshape(N_HEADS, D_HEAD_PAD, R_Q)[:, 0:D_HEAD].reshape(N_HEADS * D_HEAD, R_Q)
    parts = [blocks(gwin_t), blocks(gwuq_t), blocks(gwukv_t)]
    (r_win, r_wuq, r_wukv, r_wout, g_w_pool, ccg, g_ng, g_qlg, g_kvlg, g_qng, g_kng, g_ps, g_bmod, dmod_my,
     loss_rows) = _epilogue(parts, [land_wo, land_wp], [dng, dqlg, dkvlg, dqng, dkng, dps], dmod4, dgate, loss_acc, w_mod_l)

    g_w_ukv = jnp.transpose(r_wukv)
    c_rows = cg[:, 0, :]
    c_all_t = jnp.transpose(jnp.concatenate([c_rows, c_ctx_row, jnp.zeros((16 - N_DEV - 1, D_MODEL), F32)], axis=0))

    weights = [c_ctx_row, w_mod_l, b_mod, norm_g, w_in_tl, q_lora_g, w_uq_tl, kv_lora_g, w_ukv_l, q_norm_g, k_norm_g,
               w_pool.reshape(4 * GROUP_DIM, GROUP_DIM), pool_scale, w_out_l]
    grads = [None, None, g_bmod, g_ng, r_win, g_qlg, r_wuq, g_kvlg, g_w_ukv, g_qng, g_kng, g_w_pool, g_ps, r_wout]
    ms = [m_c_ctx.reshape(1, D_MODEL), m_w_mod[0], m_b_mod, m_norm_g, tr(m_w_in), m_q_lora_g, tr(m_w_uq), m_kv_lora_g,
          m_w_ukv[0], m_q_norm_g, m_k_norm_g, m_w_pool.reshape(4 * GROUP_DIM, GROUP_DIM), m_pool_scale, m_w_out[0]]
    vs = [v_c_ctx.reshape(1, D_MODEL), v_w_mod[0], v_b_mod, v_norm_g, tr(v_w_in), v_q_lora_g, tr(v_w_uq), v_kv_lora_g,
          v_w_ukv[0], v_q_norm_g, v_k_norm_g, v_w_pool.reshape(4 * GROUP_DIM, GROUP_DIM), v_pool_scale, v_w_out[0]]
    outs = _adamw(weights, grads, ms, vs, c_all_t, dmod_my, ccg)
    grads[0], grads[1] = outs[0], outs[1]
    n = len(weights)
    deltas, new_m, new_v = outs[2:2 + n], outs[2 + n:2 + 2 * n], outs[2 + 2 * n:2 + 3 * n]

    loss = loss_rows[ROW_LOSS - 8, 0]
    final = [c_ctx.shape, w_mod.shape, b_mod.shape, norm_g.shape, w_in.shape, q_lora_g.shape, w_uq.shape, kv_lora_g.shape,
             w_ukv.shape, q_norm_g.shape, k_norm_g.shape, w_pool.shape, pool_scale.shape, w_out.shape]
    transposed = (4, 6)

    def shaped(arrs):
        return [(jnp.transpose(a) if i in transposed else a).reshape(s) for i, (a, s) in enumerate(zip(arrs, final))]

    return (loss, grad_x[None], *shaped(grads), *shaped(deltas), *shaped(new_m), *shaped(new_v))
```

```python
import numpy as np

import jax
import jax.numpy as jnp
from jax import lax
from jax.experimental import pallas as pl
from jax.experimental.pallas import tpu as pltpu

F32 = jnp.float32
BF16 = jnp.bfloat16
MESH = pl.DeviceIdType.MESH
HIGHEST = lax.Precision.HIGHEST

D_MODEL = 1024
N_HEADS = 4
D_NOPE = 128
D_ROPE = 64
D_HEAD = D_NOPE + D_ROPE
D_HEAD_PAD = 256
D_V = 128
R_Q = 256
R_KV = 128
D_ATTN = 512
D_POOL = 512
POOL_WINDOWS = (2, 4, 8, 16)
GROUP_DIM = 128
GRID_W = 64
ROPE_BASE = 10000.0
NORM_EPS = 1e-6
ATTN_SCALE = D_HEAD ** -0.5
LOG2_E = 1.4426950408889634
LN_2 = 0.6931471805599453
ATTN_ROWS = 256
D_IN_PROJ = 1984
U_PAD = 2048
O_GA, O_PIN, O_GP, O_CQ, O_CKV, O_KR = 0, 512, 1024, 1536, 1792, 1920
TILE = 256
Q_BLOCK = 128
HALO = 8
N_DEV = 8
VMEM_LIMIT = 56 * 1024 * 1024

ADAM_LR = 0.001
ADAM_B1 = 0.9
ADAM_B2 = 0.999
ADAM_EPS = 1e-08
ADAM_WD = 0.01
ADAM_STEP = 10

SMALL_ROWS = 16


def _dot(a, b):
    return lax.dot_general(a, b, (((1,), (0,)), ((), ())), preferred_element_type=F32)


def _dot_nt(a, b):
    return lax.dot_general(a, b, (((1,), (1,)), ((), ())), preferred_element_type=F32)


def _dot_tn(a, b):
    return lax.dot_general(a, b, (((0,), (0,)), ((), ())), preferred_element_type=F32)


def _sigmoid(x):
    return 1.0 / (1.0 + jnp.exp(-x))


def _rope(p, cos, slo, shi):
    return p * cos + pltpu.roll(p, 112, 1) * slo + pltpu.roll(p, 16, 1) * shi


def _rope_t(d, cos, slo, shi):
    return d * cos + pltpu.roll(d * slo, 16, 1) + pltpu.roll(d * shi, 112, 1)


def _shift_rows(a, k):
    n = a.shape[0]
    return pltpu.roll(a, (-k) % n, 0)


def _window_sum(x, w, transposed):
    s = (x + _shift_rows(x, 1)) if transposed else (_shift_rows(x, -1) + x)
    step = 1
    while 2 * step < w:
        s = _shift_rows(s, -step) + _shift_rows(s, step)
        step *= 2
    return s


def _inv_count(tpos, w, seq):
    lo = jnp.maximum(tpos - w // 2, 0)
    hi = jnp.minimum(tpos - w // 2 + w, seq)
    return 1.0 / jnp.maximum(hi - lo, 1).astype(F32)


def _coords():
    return lax.axis_index("x"), lax.axis_index("y"), lax.axis_index("c")


def _flip(v, bit):
    return (1 - v) if bit else v


def _peer(k):
    x, y, c = _coords()
    return (_flip(x, (k >> 2) & 1), _flip(y, (k >> 1) & 1), _flip(c, k & 1))


def _lin(p):
    return 4 * p[0] + 2 * p[1] + p[2]


def _gather_to_all(src_ref, slots_ref, send_sems, recv_sems):
    me = _coords()
    copies = []
    for k in range(1, N_DEV):
        cp = pltpu.make_async_remote_copy(
            src_ref=src_ref, dst_ref=slots_ref.at[_lin(me)], send_sem=send_sems.at[k - 1], recv_sem=recv_sems.at[k - 1],
            device_id=_peer(k), device_id_type=MESH)
        cp.start()
        copies.append(cp)

    def wait_recv():
        for k in range(1, N_DEV):
            pltpu.make_async_remote_copy(
                src_ref=src_ref, dst_ref=slots_ref.at[_lin(_peer(k))], send_sem=send_sems.at[k - 1],
                recv_sem=recv_sems.at[k - 1], device_id=_peer(k), device_id_type=MESH).wait_recv()

    def wait_send():
        for cp in copies:
            cp.wait_send()

    return wait_recv, wait_send


def _prologue(c8, cctx8, w_mod_l, shards):
    n_mod = w_mod_l.shape[1]
    n_w = len(shards)

    def body(c8_ref, cctx_ref, wmod_ref, *refs):
        w_refs = refs[0:n_w]
        cg_ref, modg_ref = refs[n_w], refs[n_w + 1]
        wg_refs = refs[n_w + 2:2 * n_w + 2]
        modblk_ref = refs[2 * n_w + 2]
        wb_refs = refs[2 * n_w + 3:3 * n_w + 3]
        ssem_w, rsem_w, ssem_c, rsem_c, ssem_m, rsem_m = refs[3 * n_w + 3:]
        x, y, c = _coords()
        me = (x, y, c)
        sibling = (x, y, 1 - c)
        chips = [(1 - x, y), (x, 1 - y), (1 - x, 1 - y)]

        def wcopies(k, block, to, own=False):
            out = []
            for a in range(n_w):
                slot = wg_refs[a].at[_lin(block)]
                out.append(pltpu.make_async_remote_copy(
                    src_ref=wb_refs[a] if own else slot, dst_ref=slot, send_sem=ssem_w.at[a, k], recv_sem=rsem_w.at[a, k],
                    device_id=to, device_id_type=MESH))
            return out

        for a in range(n_w):
            wb_refs[a][...] = w_refs[a][...].astype(BF16)
        first = wcopies(0, me, sibling, own=True)
        for j, chip in enumerate(chips):
            first += wcopies(1 + j, me, (*chip, c), own=True)
        for cp in first:
            cp.start()
        for a in range(n_w):
            wg_refs[a][_lin(me)] = wb_refs[a][...]

        cg_ref[_lin(me)] = c8_ref[...]
        c_recv, c_send = _gather_to_all(c8_ref, cg_ref, ssem_c, rsem_c)
        c_recv()
        row = lax.broadcasted_iota(jnp.int32, (8, D_MODEL), 0)
        c_all = jnp.zeros((8, D_MODEL), F32)
        for d in range(N_DEV):
            c_all = c_all + jnp.where(row == d, cg_ref[d], 0.0)
        cc = cctx_ref[...]
        a = jnp.concatenate([c_all * _sigmoid(c_all), cc * _sigmoid(cc)], axis=0)
        modblk_ref[...] = lax.dot_general(a, wmod_ref[...], (((1,), (0,)), ((), ())), precision=HIGHEST,
                                          preferred_element_type=F32)
        modg_ref[_lin(me)] = modblk_ref[...]
        m_recv, m_send = _gather_to_all(modblk_ref, modg_ref, ssem_m, rsem_m)
        m_recv()

        passed = []
        for j, chip in enumerate(chips):
            for cp in wcopies(1 + j, (*chip, c), me):
                cp.wait_recv()
            fwd = wcopies(4 + j, (*chip, c), sibling)
            for cp in fwd:
                cp.start()
            passed += fwd
        for cp in wcopies(0, sibling, me):
            cp.wait_recv()
        for j, chip in enumerate(chips):
            for cp in wcopies(4 + j, (*chip, 1 - c), me):
                cp.wait_recv()
        for cp in first + passed:
            cp.wait_send()
        c_send()
        m_send()

    vm = pl.BlockSpec(memory_space=pltpu.VMEM)
    return pl.pallas_call(
        body, name="prologue_gather",
        out_shape=(jax.ShapeDtypeStruct((N_DEV, 8, D_MODEL), F32), jax.ShapeDtypeStruct((N_DEV, 16, n_mod), F32),
                   *[jax.ShapeDtypeStruct((N_DEV,) + s.shape, BF16) for s in shards]),
        in_specs=[vm] * (3 + n_w), out_specs=tuple([vm] * (2 + n_w)),
        scratch_shapes=[pltpu.VMEM((16, n_mod), F32), *[pltpu.VMEM(s.shape, BF16) for s in shards],
                        pltpu.SemaphoreType.DMA((n_w, 7)), pltpu.SemaphoreType.DMA((n_w, 7)),
                        pltpu.SemaphoreType.DMA((7,)), pltpu.SemaphoreType.DMA((7,)),
                        pltpu.SemaphoreType.DMA((7,)), pltpu.SemaphoreType.DMA((7,))],
        compiler_params=pltpu.CompilerParams(vmem_limit_bytes=VMEM_LIMIT),
    )(c8, cctx8, w_mod_l, *shards)


def _modulated_input(is_ctx, x_ref, ctx_ref, mod_ref, bmod_ref, ng_ref):
    xt = jnp.where(is_ctx, ctx_ref[...], x_ref[...])
    shift = jnp.where(is_ctx, mod_ref[3:4, :] + bmod_ref[3:4, :], mod_ref[0:1, :] + bmod_ref[0:1, :])
    scale = jnp.where(is_ctx, mod_ref[4:5, :] + bmod_ref[4:5, :], mod_ref[1:2, :] + bmod_ref[1:2, :])
    r = lax.rsqrt(jnp.mean(xt * xt, axis=-1, keepdims=True) + NORM_EPS)
    xg = (xt * r) * ng_ref[...]
    h = xg * (1.0 + scale) + shift
    return xt, r, xg, h, scale


def _inproj_fwd(x, ctx, modrows, bmodrows, norm_g, w_in_p, q_lora_g, w_uq_p, kv_lora_g, w_ukv, qng_p, kng_p, cos, slo, shi):
    seq = x.shape[0]
    n_tiles = seq // TILE + 1
    tot = seq + TILE

    n_mla = R_Q + R_KV + 128

    def body(x_ref, ctx_ref, mod_ref, bmod_ref, ng_ref, win_ref, qlg_ref, wuq_ref, kvlg_ref, wukv_ref, qng_ref, kng_ref,
             cos_ref, slo_ref, shi_ref, u_ref, q_ref, k_ref, v_ref, stash_even, stash_odd):
        s = pl.program_id(0)

        @pl.when(s == 0)
        def _():
            stash_odd[...] = jnp.zeros_like(stash_odd)

        refs = (x_ref, ctx_ref, mod_ref, bmod_ref, ng_ref, win_ref, qlg_ref, wuq_ref, kvlg_ref, wukv_ref, qng_ref, kng_ref,
                cos_ref, slo_ref, shi_ref, u_ref, q_ref, k_ref, v_ref)

        @pl.when(lax.rem(s, 2) == 0)
        def _():
            stages(s, refs, stash_even, stash_odd)

        @pl.when(lax.rem(s, 2) == 1)
        def _():
            stages(s, refs, stash_odd, stash_even)

    def stages(s, refs, fill, prev_ref):
        (x_ref, ctx_ref, mod_ref, bmod_ref, ng_ref, win_ref, qlg_ref, wuq_ref, kvlg_ref, wukv_ref, qng_ref, kng_ref,
         cos_ref, slo_ref, shi_ref, u_ref, q_ref, k_ref, v_ref) = refs
        cos_t, slo_t, shi_t = cos_ref[...], slo_ref[...], shi_ref[...]
        prev = prev_ref[...]
        cq = prev[:, 0:R_Q]
        qn = cq * lax.rsqrt(jnp.mean(cq * cq, axis=-1, keepdims=True) + NORM_EPS) * qlg_ref[...]
        q = _dot_nt(qn.astype(BF16), wuq_ref[...])
        qg = qng_ref[...] * (ATTN_SCALE * LOG2_E)
        for hd in range(N_HEADS):
            qh = q[:, hd * D_HEAD_PAD:(hd + 1) * D_HEAD_PAD]
            rr = lax.rsqrt(jnp.sum(qh * qh, axis=-1, keepdims=True) * (1.0 / D_HEAD) + NORM_EPS)
            qy = qh * rr * qg
            q_ref[hd, :, 0:D_NOPE] = qy[:, 0:D_NOPE].astype(BF16)
            q_ref[hd, :, D_NOPE:D_HEAD_PAD] = _rope(qy[:, D_NOPE:D_HEAD_PAD], cos_t, slo_t, shi_t).astype(BF16)

        ckv = prev[:, R_Q:R_Q + R_KV]
        kvn = ckv * lax.rsqrt(jnp.mean(ckv * ckv, axis=-1, keepdims=True) + NORM_EPS) * kvlg_ref[...]
        kv = _dot(kvn.astype(BF16), wukv_ref[...])
        krz = prev[:, R_Q + R_KV:n_mla]
        kr_ss = jnp.sum(krz * krz, axis=-1, keepdims=True)
        kg = kng_ref[...]
        for hd in range(N_HEADS):
            kn = kv[:, hd * 256:hd * 256 + D_NOPE]
            rr = lax.rsqrt((jnp.sum(kn * kn, axis=-1, keepdims=True) + kr_ss) * (1.0 / D_HEAD) + NORM_EPS)
            k_ref[hd, :, 0:D_NOPE] = (kn * rr * kg[:, 0:D_NOPE]).astype(BF16)
            k_ref[hd, :, D_NOPE:D_HEAD_PAD] = _rope(krz * rr * kg[:, D_NOPE:D_HEAD_PAD], cos_t, slo_t, shi_t).astype(BF16)
            v_ref[hd] = kv[:, hd * 256 + D_NOPE:(hd + 1) * 256].astype(BF16)

        _, _, _, h, _ = _modulated_input(s == 0, x_ref, ctx_ref, mod_ref, bmod_ref, ng_ref)
        u = _dot_nt(h.astype(BF16), win_ref[...])
        u_ref[...] = u
        fill[:, 0:R_Q + R_KV] = u[:, O_CQ:O_CQ + R_Q + R_KV]
        fill[:, R_Q + R_KV:n_mla] = u[:, O_KR:U_PAD]

    first = lambda s: jnp.minimum(s, n_tiles - 1)
    second = lambda s: jnp.maximum(s - 1, 0)
    latent = lambda t: jnp.maximum(t - 1, 0)
    full = lambda shape: pl.BlockSpec(shape, lambda s: (0,) * len(shape))
    rope_spec = pl.BlockSpec((TILE, 128), lambda s: (second(s), 0))
    return pl.pallas_call(
        body, name="inproj_fwd", grid=(n_tiles + 1,),
        out_shape=(jax.ShapeDtypeStruct((tot, U_PAD), F32),
                   jax.ShapeDtypeStruct((N_HEADS, seq, D_HEAD_PAD), BF16),
                   jax.ShapeDtypeStruct((N_HEADS, tot, D_HEAD_PAD), BF16),
                   jax.ShapeDtypeStruct((N_HEADS, tot, D_V), BF16)),
        in_specs=[pl.BlockSpec((TILE, D_MODEL), lambda s: (latent(first(s)), 0)), full((TILE, D_MODEL)), full((8, D_MODEL)),
                  full((8, D_MODEL)), full((1, D_MODEL)), full((U_PAD, D_MODEL)), full((1, R_Q)),
                  full((N_HEADS * D_HEAD_PAD, R_Q)), full((1, R_KV)), full((R_KV, N_HEADS * 256)), full((1, D_HEAD_PAD)),
                  full((1, D_HEAD_PAD)), rope_spec, rope_spec, rope_spec],
        out_specs=(pl.BlockSpec((TILE, U_PAD), lambda s: (first(s), 0)),
                   pl.BlockSpec((N_HEADS, TILE, D_HEAD_PAD), lambda s: (0, latent(second(s)), 0)),
                   pl.BlockSpec((N_HEADS, TILE, D_HEAD_PAD), lambda s: (0, second(s), 0)),
                   pl.BlockSpec((N_HEADS, TILE, D_V), lambda s: (0, second(s), 0))),
        scratch_shapes=[pltpu.VMEM((TILE, n_mla), F32), pltpu.VMEM((TILE, n_mla), F32)],
        compiler_params=pltpu.CompilerParams(dimension_semantics=("arbitrary",), vmem_limit_bytes=VMEM_LIMIT),
    )(x, ctx, modrows, bmodrows, norm_g, w_in_p, q_lora_g, w_uq_p, kv_lora_g, w_ukv, qng_p, kng_p, cos, slo, shi)


def _hosted_exchange(first, last, items, send_sems, recv_sems, local_sems):
    me = _lin(_coords())

    def remote(n, k, src, land, scatter):
        peer = _peer(k)
        return pltpu.make_async_remote_copy(
            src_ref=src.at[_lin(peer)] if scatter else src, dst_ref=land.at[me], send_sem=send_sems.at[n, k - 1],
            recv_sem=recv_sems.at[n, k - 1], device_id=peer, device_id_type=MESH)

    def local(n, src, land, scatter):
        return pltpu.make_async_copy(src.at[me] if scatter else src, land.at[me], local_sems.at[n])

    @pl.when(first)
    def _():
        for n, (src, land, scatter) in enumerate(items):
            for k in range(1, N_DEV):
                remote(n, k, src, land, scatter).start()
            local(n, src, land, scatter).start()

    @pl.when(last)
    def _():
        for n, (src, land, scatter) in enumerate(items):
            for k in range(1, N_DEV):
                peer = _peer(k)
                pltpu.make_async_remote_copy(
                    src_ref=src.at[0] if scatter else src, dst_ref=land.at[_lin(peer)], send_sem=send_sems.at[n, k - 1],
                    recv_sem=recv_sems.at[n, k - 1], device_id=peer, device_id_type=MESH).wait_recv()
            for k in range(1, N_DEV):
                remote(n, k, src, land, scatter).wait_send()
            local(n, src, land, scatter).wait()


def _attn_fwd(q, k, v, block_q, w_out_b):
    _, seq, _ = q.shape
    n_keys = k.shape[1]
    n_q = seq // block_q

    def body(q_ref, k_ref, v_ref, wo_ref, o_ref, lse_ref, wog_ref, ssem, rsem, lsem):
        h, i = pl.program_id(0), pl.program_id(1)
        _hosted_exchange((h == 0) & (i == 0), (h == N_HEADS - 1) & (i == n_q - 1), [(wo_ref, wog_ref, False)],
                         ssem, rsem, lsem)
        kb, vb = k_ref[0], v_ref[0]
        for r0 in range(0, block_q, ATTN_ROWS):
            rows = slice(r0, r0 + ATTN_ROWS)
            s = _dot_nt(q_ref[0, rows, :], kb)
            m = jnp.max(s, axis=-1, keepdims=True)
            p = jnp.exp2(s - m)
            l = jnp.sum(p, axis=-1, keepdims=True)
            o_ref[rows, :] = _dot(p.astype(BF16), vb) * (1.0 / l)
            lse_ref[0, rows, :] = m + jnp.log2(l)

    hbm = pl.BlockSpec(memory_space=pl.ANY)
    return pl.pallas_call(
        body, name="attn_fwd", grid=(N_HEADS, n_q),
        out_shape=(jax.ShapeDtypeStruct((seq, D_ATTN), F32), jax.ShapeDtypeStruct((N_HEADS, seq, 1), F32),
                   jax.ShapeDtypeStruct((N_DEV,) + w_out_b.shape, w_out_b.dtype)),
        in_specs=[pl.BlockSpec((1, block_q, D_HEAD_PAD), lambda h, i: (h, i, 0)),
                  pl.BlockSpec((1, n_keys, D_HEAD_PAD), lambda h, i: (h, 0, 0)),
                  pl.BlockSpec((1, n_keys, D_V), lambda h, i: (h, 0, 0)), hbm],
        out_specs=(pl.BlockSpec((block_q, D_V), lambda h, i: (i, h)),
                   pl.BlockSpec((1, block_q, 1), lambda h, i: (h, i, 0)), hbm),
        scratch_shapes=[pltpu.SemaphoreType.DMA((1, 7)), pltpu.SemaphoreType.DMA((1, 7)), pltpu.SemaphoreType.DMA((1,))],
        compiler_params=pltpu.CompilerParams(dimension_semantics=("arbitrary", "arbitrary"), vmem_limit_bytes=VMEM_LIMIT),
    )(q, k, v, w_out_b)


def _attn_bwd(q, k, v, o, lse, d_o, block_q, gwo_blocks, gwp):
    _, seq, _ = q.shape
    n_keys = k.shape[1]
    n_q = seq // block_q

    def body(q_ref, k_ref, v_ref, o_ref, lse_ref, do_ref, gwo_ref, gwp_ref, dq_ref, dkt_ref, dvt_ref, lwo_ref, lwp_ref,
             ssem, rsem, lsem):
        h, i = pl.program_id(0), pl.program_id(1)
        _hosted_exchange((h == 0) & (i == 0), (h == N_HEADS - 1) & (i == n_q - 1),
                         [(gwo_ref, lwo_ref, True), (gwp_ref, lwp_ref, False)], ssem, rsem, lsem)

        @pl.when(i == 0)
        def _():
            dkt_ref[...] = jnp.zeros_like(dkt_ref)
            dvt_ref[...] = jnp.zeros_like(dvt_ref)

        kb, vb = k_ref[0], v_ref[0]
        raws, ps = [], []
        for r0 in range(0, block_q, ATTN_ROWS):
            rows = slice(r0, r0 + ATTN_ROWS)
            d_out = do_ref[rows, :]
            p = jnp.exp2(_dot_nt(q_ref[0, rows, :], kb) - lse_ref[0, rows, :])
            dp = _dot_nt(d_out.astype(BF16), vb)
            delta = jnp.sum(d_out * o_ref[rows, :], axis=-1, keepdims=True)
            raw = (p * (dp - delta)).astype(BF16)
            dq_ref[0, rows, :] = _dot(raw, kb)
            raws.append(raw)
            ps.append(p.astype(BF16))
        dkt_ref[0] += _dot_tn(q_ref[0], jnp.concatenate(raws, axis=0))
        dvt_ref[0] += _dot_tn(do_ref[...].astype(BF16), jnp.concatenate(ps, axis=0))

    hbm = pl.BlockSpec(memory_space=pl.ANY)
    return pl.pallas_call(
        body, name="attn_bwd", grid=(N_HEADS, n_q),
        out_shape=(jax.ShapeDtypeStruct((N_HEADS, seq, D_HEAD_PAD), F32),
                   jax.ShapeDtypeStruct((N_HEADS, D_HEAD_PAD, n_keys), F32),
                   jax.ShapeDtypeStruct((N_HEADS, D_V, n_keys), F32),
                   jax.ShapeDtypeStruct(gwo_blocks.shape, gwo_blocks.dtype),
                   jax.ShapeDtypeStruct((N_DEV,) + gwp.shape, gwp.dtype)),
        in_specs=[pl.BlockSpec((1, block_q, D_HEAD_PAD), lambda h, i: (h, i, 0)),
                  pl.BlockSpec((1, n_keys, D_HEAD_PAD), lambda h, i: (h, 0, 0)),
                  pl.BlockSpec((1, n_keys, D_V), lambda h, i: (h, 0, 0)),
                  pl.BlockSpec((block_q, D_V), lambda h, i: (i, h)),
                  pl.BlockSpec((1, block_q, 1), lambda h, i: (h, i, 0)),
                  pl.BlockSpec((block_q, D_V), lambda h, i: (i, h)), hbm, hbm],
        out_specs=(pl.BlockSpec((1, block_q, D_HEAD_PAD), lambda h, i: (h, i, 0)),
                   pl.BlockSpec((1, D_HEAD_PAD, n_keys), lambda h, i: (h, 0, 0)),
                   pl.BlockSpec((1, D_V, n_keys), lambda h, i: (h, 0, 0)), hbm, hbm),
        scratch_shapes=[pltpu.SemaphoreType.DMA((2, 7)), pltpu.SemaphoreType.DMA((2, 7)), pltpu.SemaphoreType.DMA((2,))],
        compiler_params=pltpu.CompilerParams(dimension_semantics=("arbitrary", "arbitrary"), vmem_limit_bytes=VMEM_LIMIT),
    )(q, k, v, o, lse, d_o, gwo_blocks, gwp)


def _mix(x, target, attn, u, modrows, bmodrows, w_pool, pool_scale, w_out):
    seq = x.shape[0]
    n_tiles = seq // TILE
    rows8 = TILE // HALO
    last8 = (seq + TILE) // HALO - 1

    n_blk = seq // Q_BLOCK
    per = TILE // n_blk
    assert TILE % n_blk == 0 and per % 8 == 0 and n_tiles >= 2
    n_stash = 8

    def body(x_ref, t_ref, o_hbm, ga_ref, pin_ref, hb_ref, ha_ref, gp_ref, mod_ref, bmod_ref, wp_ref, ps_ref, wo_ref,
             loss_ref, g1_ref, dgate_ref, do_hbm, dga_ref, dgp_ref, dpl_ref, gwob_ref, gwp_ref, dps_ref,
             abuf, dbuf, gwo_ref, *rest):
        stash_even, stash_odd = rest[0:n_stash], rest[n_stash:2 * n_stash]
        rsem, wsem = rest[2 * n_stash:]
        s = pl.program_id(0)

        def slab_copies(tile, slot, fetch):
            out = []
            for b in range(n_blk):
                hbm_rows = pl.ds(b * Q_BLOCK + tile * per, per)
                buf_rows = pl.ds(b * per, per)
                if fetch:
                    out.append(pltpu.make_async_copy(o_hbm.at[hbm_rows], abuf.at[slot, buf_rows], rsem.at[slot]))
                else:
                    out.append(pltpu.make_async_copy(dbuf.at[slot, buf_rows], do_hbm.at[hbm_rows], wsem.at[slot]))
            return out

        def wait_all(slot, fetch):
            if fetch:
                pltpu.make_async_copy(o_hbm.at[pl.ds(0, TILE)], abuf.at[slot], rsem.at[slot]).wait()
            else:
                pltpu.make_async_copy(dbuf.at[slot], do_hbm.at[pl.ds(0, TILE)], wsem.at[slot]).wait()

        a_slot = lax.rem(s, 2)
        b_slot = 1 - a_slot

        @pl.when(s == 0)
        def _():
            loss_ref[...] = jnp.zeros_like(loss_ref)
            dgate_ref[...] = jnp.zeros_like(dgate_ref)
            gwo_ref[...] = jnp.zeros_like(gwo_ref)
            gwp_ref[...] = jnp.zeros_like(gwp_ref)
            dps_ref[...] = jnp.zeros_like(dps_ref)
            for cp in slab_copies(0, 0, True):
                cp.start()

        @pl.when(s + 1 < n_tiles)
        def _():
            for cp in slab_copies(s + 1, b_slot, True):
                cp.start()

        @pl.when(s < n_tiles)
        def _():
            wait_all(a_slot, True)

        @pl.when(s >= 3)
        def _():
            wait_all(b_slot, False)

        gate = mod_ref[2:3, :] + bmod_ref[2:3, :]
        ps = ps_ref[...]

        def a_vector(slot):
            attn_t = jnp.swapaxes(abuf[slot].reshape(n_blk, per, D_ATTN), 0, 1).reshape(TILE, D_ATTN)
            ga = ga_ref[...]
            sga = _sigmoid(ga)
            silu_ga = ga * sga
            pin = pin_ref[...]
            xs = jnp.concatenate([jnp.where(s > 0, hb_ref[...], 0.0), pin, jnp.where(s < n_tiles - 1, ha_ref[...], 0.0)],
                                 axis=0)
            tpos = s * TILE + lax.broadcasted_iota(jnp.int32, (TILE, 1), 0)
            pooled = []
            for g, w in enumerate(POOL_WINDOWS):
                sl = slice(g * GROUP_DIM, (g + 1) * GROUP_DIM)
                ws = _window_sum(xs[:, sl], w, False)[HALO:HALO + TILE]
                pooled.append((ws * _inv_count(tpos, w, seq) - pin[:, sl]).astype(BF16))
            return attn_t, ga, sga, silu_ga, pooled

        def a_group_maps(pooled):
            return jnp.concatenate([_dot(pooled[g], wp_ref[g].astype(BF16)) for g in range(len(POOL_WINDOWS))], axis=1)

        def a_project(stash, yp, attn_t, ga, sga, silu_ga, pooled):
            zb_ref, _, fa_ref, sa_ref, fp_ref, sp_ref, yp_ref, pooled_ref = stash
            ypool = yp * ps
            gp = gp_ref[...]
            sgp = _sigmoid(gp)
            silu_gp = gp * sgp
            z = jnp.concatenate([silu_ga * attn_t, silu_gp * ypool], axis=1).astype(BF16)
            y = _dot(z, wo_ref[...])
            zb_ref[...] = z
            fa_ref[...] = attn_t * (sga * (1.0 + ga * (1.0 - sga)))
            sa_ref[...] = silu_ga
            fp_ref[...] = ypool * (sgp * (1.0 + gp * (1.0 - sgp)))
            sp_ref[...] = silu_gp
            yp_ref[...] = yp
            pooled_ref[...] = jnp.concatenate(pooled, axis=1)
            return y

        def a_loss(stash, y):
            res = x_ref[...] + gate * y - t_ref[...]
            loss_ref[...] += jnp.sum(res * res) * (0.5 / D_MODEL)
            dxn = res * (1.0 / D_MODEL)
            g1_ref[...] = dxn
            dgate_ref[...] += jnp.sum(dxn * y, axis=0, keepdims=True)
            stash[1][...] = (gate * dxn).astype(BF16)

        def b_dz(stash):
            return _dot_nt(stash[1][...], wo_ref[...])

        def b_gwo(stash):
            gwo_ref[...] += _dot_tn(stash[0][...], stash[1][...])

        def b_vector(stash, slot, dz):
            _, _, fa_ref, sa_ref, fp_ref, sp_ref, yp_ref, _ = stash
            dbra = dz[:, 0:D_ATTN]
            dbrp = dz[:, D_ATTN:]
            dga_ref[...] = dbra * fa_ref[...]
            dbuf[slot] = jnp.swapaxes((dbra * sa_ref[...]).reshape(per, n_blk, D_ATTN), 0, 1).reshape(TILE, D_ATTN)
            dgp_ref[...] = dbrp * fp_ref[...]
            dyp_s = dbrp * sp_ref[...]
            dps_ref[...] += jnp.sum(dyp_s * yp_ref[...], axis=0, keepdims=True)
            return (dyp_s * ps).astype(BF16)

        def b_small(stash, dyp):
            pooled_ref = stash[7]
            for g in range(len(POOL_WINDOWS)):
                sl = slice(g * GROUP_DIM, (g + 1) * GROUP_DIM)
                gwp_ref[g] += _dot_tn(pooled_ref[:, sl], dyp[:, sl])
                dpl_ref[:, sl] = _dot_nt(dyp[:, sl], wp_ref[g].astype(BF16))

        def both(a_stash, b_stash, slot_a):
            dz = b_dz(b_stash)
            front = a_vector(slot_a)
            yp = a_group_maps(front[-1])
            b_gwo(b_stash)
            y = a_project(a_stash, yp, *front)
            dyp = b_vector(b_stash, 1 - slot_a, dz)
            a_loss(a_stash, y)
            b_small(b_stash, dyp)

        @pl.when(s == 0)
        def _():
            front = a_vector(0)
            a_loss(stash_even, a_project(stash_even, a_group_maps(front[-1]), *front))

        @pl.when((s > 0) & (s < n_tiles) & (a_slot == 0))
        def _():
            both(stash_even, stash_odd, 0)

        @pl.when((s > 0) & (s < n_tiles) & (a_slot == 1))
        def _():
            both(stash_odd, stash_even, 1)

        @pl.when(s == n_tiles)
        def _():
            last = (n_tiles - 1) % 2
            stash = stash_odd if last else stash_even
            dz = b_dz(stash)
            b_gwo(stash)
            b_small(stash, b_vector(stash, last, dz))
            gwob_ref[...] = gwo_ref[...].astype(BF16)

        @pl.when(s >= 1)
        def _():
            for cp in slab_copies(s - 1, b_slot, False):
                cp.start()

        @pl.when(s == n_tiles)
        def _():
            wait_all(b_slot, False)
            wait_all(a_slot, False)

    ta = lambda s: jnp.minimum(s, n_tiles - 1)
    tb = lambda s: jnp.maximum(s - 1, 0)
    tile = lambda w: pl.BlockSpec((TILE, w), lambda s: (ta(s), 0))
    tile_b = lambda w: pl.BlockSpec((TILE, w), lambda s: (tb(s), 0))
    ucol = lambda col: pl.BlockSpec((TILE, 512), lambda s: (ta(s) + 1, col))
    full = lambda shape: pl.BlockSpec(shape, lambda s: (0,) * len(shape))
    stash_shapes = [pltpu.VMEM((TILE, D_MODEL), BF16), pltpu.VMEM((TILE, D_MODEL), BF16)] + \
        [pltpu.VMEM((TILE, 512), F32)] * 5 + [pltpu.VMEM((TILE, D_POOL), BF16)]
    return pl.pallas_call(
        body, name="mix_fwd_bwd", grid=(n_tiles + 1,),
        out_shape=(jax.ShapeDtypeStruct((8, 128), F32), jax.ShapeDtypeStruct((seq, D_MODEL), F32),
                   jax.ShapeDtypeStruct((1, D_MODEL), F32), jax.ShapeDtypeStruct((seq, D_ATTN), F32),
                   jax.ShapeDtypeStruct((seq, D_ATTN), F32), jax.ShapeDtypeStruct((seq, D_POOL), F32),
                   jax.ShapeDtypeStruct((seq, D_POOL), F32), jax.ShapeDtypeStruct((D_MODEL, D_MODEL), BF16),
                   jax.ShapeDtypeStruct((4, GROUP_DIM, GROUP_DIM), F32), jax.ShapeDtypeStruct((1, D_POOL), F32)),
        in_specs=[tile(D_MODEL), tile(D_MODEL), pl.BlockSpec(memory_space=pl.ANY), ucol(0), ucol(1),
                  pl.BlockSpec((HALO, 512), lambda s: ((ta(s) + 1) * rows8 - 1, 1)),
                  pl.BlockSpec((HALO, 512), lambda s: (jnp.minimum((ta(s) + 2) * rows8, last8), 1)),
                  ucol(2), full((8, D_MODEL)), full((8, D_MODEL)), full((4, GROUP_DIM, GROUP_DIM)), full((1, D_POOL)),
                  full((D_MODEL, D_MODEL))],
        out_specs=(full((8, 128)), tile(D_MODEL), full((1, D_MODEL)), pl.BlockSpec(memory_space=pl.ANY), tile_b(D_ATTN),
                   tile_b(D_POOL), tile_b(D_POOL), full((D_MODEL, D_MODEL)), full((4, GROUP_DIM, GROUP_DIM)),
                   full((1, D_POOL))),
        scratch_shapes=[pltpu.VMEM((2, TILE, D_ATTN), F32), pltpu.VMEM((2, TILE, D_ATTN), F32),
                        pltpu.VMEM((D_MODEL, D_MODEL), F32), *stash_shapes, *stash_shapes,
                        pltpu.SemaphoreType.DMA((2,)), pltpu.SemaphoreType.DMA((2,))],
        compiler_params=pltpu.CompilerParams(dimension_semantics=("arbitrary",), vmem_limit_bytes=VMEM_LIMIT),
    )(x, target, attn, u, u, u, u, u, modrows, bmodrows, w_pool, pool_scale, w_out)


def _inproj_bwd(x, ctx, modrows, bmodrows, norm_g, w_in_p, q_lora_g, w_uq_p, kv_lora_g, w_ukv, qng_p, kng_p, cos, slo, shi,
                u, dq, dk, dv, dga, dgp, dpl, g1):
    seq = x.shape[0]
    n_tiles = seq // TILE + 1
    rows8 = TILE // HALO
    last8 = seq // HALO - 1

    def body(*refs):
        du_even, du_odd, dh_even, dh_odd = refs[-4:]
        gwin_ref, gwuq_ref, gwukv_ref, dqlg_ref, dkvlg_ref, dqng_ref, dkng_ref, dng_ref, dmod_ref = refs[-13:-4]
        s = pl.program_id(0)

        @pl.when(s == 0)
        def _():
            for ref in (gwin_ref, gwuq_ref, gwukv_ref, dqlg_ref, dkvlg_ref, dqng_ref, dkng_ref, dng_ref, dmod_ref,
                        du_odd, dh_even):
                ref[...] = jnp.zeros_like(ref)

        @pl.when(lax.rem(s, 2) == 0)
        def _():
            stages(s, refs[:-4], du_even, du_odd, dh_odd, dh_even)

        @pl.when(lax.rem(s, 2) == 1)
        def _():
            stages(s, refs[:-4], du_odd, du_even, dh_even, dh_odd)

    def stages(s, refs, du_ref, du_prev, dh_fill, dh_prev):
        (xb_ref, xc_ref, ctx_ref, mod_ref, bmod_ref, ng_ref, win_ref, qlg_ref, wuq_ref, kvlg_ref, wukv_ref, qng_ref, kng_ref,
         cos_ref, slo_ref, shi_ref, cq_ref, ckv_ref, kr_ref, dq_ref, dk_ref, dv_ref, dga_ref, dgp_ref, dpl_ref,
         hb_ref, ha_ref, g1_ref,
         gx_ref, gwin_ref, gwuq_ref, gwukv_ref, dqlg_ref, dkvlg_ref, dqng_ref, dkng_ref, dng_ref, dmod_ref) = refs


        i = jnp.minimum(s, n_tiles - 1)
        valid = s < n_tiles
        is_lat = (s > 0) & valid
        cq = cq_ref[...]
        rq = lax.rsqrt(jnp.mean(cq * cq, axis=-1, keepdims=True) + NORM_EPS)
        qhat = cq * rq
        qnb = (qhat * qlg_ref[...]).astype(BF16)
        q = _dot_nt(qnb, wuq_ref[...])
        ckv = ckv_ref[...]
        rkv = lax.rsqrt(jnp.mean(ckv * ckv, axis=-1, keepdims=True) + NORM_EPS)
        kvhat = ckv * rkv
        kvnb = (kvhat * kvlg_ref[...]).astype(BF16)
        kv = _dot(kvnb, wukv_ref[...])

        _, _, _, h2, _ = _modulated_input(s <= 1, xb_ref, ctx_ref, mod_ref, bmod_ref, ng_ref)
        dub = du_prev[...]
        dh_fill[...] = _dot(dub, win_ref[...])
        gwin_ref[...] += _dot_tn(dub, h2.astype(BF16))

        ctx3 = s <= 2
        xt, r, xg, _, scale = _modulated_input(ctx3, xc_ref, ctx_ref, mod_ref, bmod_ref, ng_ref)
        dh = dh_prev[...]
        dshift = jnp.sum(dh, axis=0, keepdims=True)
        dscale = jnp.sum(dh * xg, axis=0, keepdims=True)
        zero = jnp.zeros_like(dshift)
        dmod_ref[0:1, :] += jnp.where(ctx3, zero, dshift)
        dmod_ref[1:2, :] += jnp.where(ctx3, zero, dscale)
        dmod_ref[2:3, :] += jnp.where(ctx3, dshift, zero)
        dmod_ref[3:4, :] += jnp.where(ctx3, dscale, zero)
        dxg = dh * (1.0 + scale)
        xr = xt * r
        dng_ref[...] += jnp.sum(dxg * xr, axis=0, keepdims=True)
        dxn = dxg * ng_ref[...]
        gx_ref[...] = g1_ref[...] + r * (dxn - xr * jnp.mean(dxn * xr, axis=-1, keepdims=True))

        cos_t, slo_t, shi_t = cos_ref[...], slo_ref[...], shi_ref[...]
        qg = qng_ref[...]
        dq_heads = []
        dqng = jnp.zeros((1, D_HEAD_PAD), F32)
        for hd in range(N_HEADS):
            qh = q[:, hd * D_HEAD_PAD:(hd + 1) * D_HEAD_PAD]
            rr = lax.rsqrt(jnp.sum(qh * qh, axis=-1, keepdims=True) * (1.0 / D_HEAD) + NORM_EPS)
            yq = qh * rr
            dpost = jnp.where(is_lat, dq_ref[hd] * ATTN_SCALE, 0.0)
            dyn = jnp.concatenate([dpost[:, 0:D_NOPE], _rope_t(dpost[:, D_NOPE:], cos_t, slo_t, shi_t)], axis=1)
            dqng = dqng + jnp.sum(dyn * yq, axis=0, keepdims=True)
            dyg = dyn * qg
            dq_heads.append(rr * (dyg - yq * (jnp.sum(dyg * yq, axis=-1, keepdims=True) * (1.0 / D_HEAD))))
        dqng_ref[...] += dqng
        dqf = jnp.concatenate(dq_heads, axis=1).astype(BF16)

        krz = kr_ref[...]
        kr_ss = jnp.sum(krz * krz, axis=-1, keepdims=True)
        kg = kng_ref[...]
        kg_n, kg_r = kg[:, 0:D_NOPE], kg[:, D_NOPE:]
        dkv_parts = []
        dkr = jnp.zeros((TILE, 128), F32)
        dkng_n = jnp.zeros((1, D_NOPE), F32)
        dkng_r = jnp.zeros((1, 128), F32)
        for hd in range(N_HEADS):
            kn = kv[:, hd * 256:hd * 256 + D_NOPE]
            rr = lax.rsqrt((jnp.sum(kn * kn, axis=-1, keepdims=True) + kr_ss) * (1.0 / D_HEAD) + NORM_EPS)
            yn, yr = kn * rr, krz * rr
            dk_h = jnp.where(valid, jnp.transpose(dk_ref[hd]) * LN_2, 0.0)
            dpn = dk_h[:, 0:D_NOPE]
            dpr = _rope_t(dk_h[:, D_NOPE:], cos_t, slo_t, shi_t)
            dkng_n = dkng_n + jnp.sum(dpn * yn, axis=0, keepdims=True)
            dkng_r = dkng_r + jnp.sum(dpr * yr, axis=0, keepdims=True)
            dyn, dyr = dpn * kg_n, dpr * kg_r
            proj = (jnp.sum(dyn * yn, axis=-1, keepdims=True) + jnp.sum(dyr * yr, axis=-1, keepdims=True)) * (1.0 / D_HEAD)
            dkv_parts.append(rr * (dyn - yn * proj))
            dkv_parts.append(jnp.where(valid, jnp.transpose(dv_ref[hd]), 0.0))
            dkr = dkr + rr * (dyr - yr * proj)
        dkng_ref[:, 0:D_NOPE] += dkng_n
        dkng_ref[:, D_NOPE:] += dkng_r
        dkvf = jnp.concatenate(dkv_parts, axis=1).astype(BF16)

        lat_t = jnp.maximum(i - 1, 0)
        dpl_t = dpl_ref[...]
        ds = jnp.concatenate([jnp.where(i > 1, hb_ref[...], 0.0), dpl_t,
                              jnp.where(i < n_tiles - 1, ha_ref[...], 0.0)], axis=0)
        tpos = lat_t * TILE - HALO + lax.broadcasted_iota(jnp.int32, (TILE + 2 * HALO, 1), 0)
        for g, w in enumerate(POOL_WINDOWS):
            sl = slice(g * GROUP_DIM, (g + 1) * GROUP_DIM)
            wsum = _window_sum(ds[:, sl] * _inv_count(tpos, w, seq), w, True)[HALO:HALO + TILE]
            du_ref[:, O_PIN + g * GROUP_DIM:O_PIN + (g + 1) * GROUP_DIM] = jnp.where(
                is_lat, wsum - dpl_t[:, sl], 0.0).astype(BF16)

        du_ref[:, O_GA:O_GA + D_ATTN] = jnp.where(is_lat, dga_ref[...], 0.0).astype(BF16)
        du_ref[:, O_GP:O_GP + D_POOL] = jnp.where(is_lat, dgp_ref[...], 0.0).astype(BF16)
        du_ref[:, O_KR:U_PAD] = dkr.astype(BF16)

        gwuq_ref[...] += _dot_tn(dqf, qnb)
        dqn = _dot(dqf, wuq_ref[...])
        gwukv_ref[...] += _dot_tn(dkvf, kvnb)
        dkvn = _dot_nt(dkvf, wukv_ref[...])
        dqlg_ref[...] += jnp.sum(dqn * qhat, axis=0, keepdims=True)
        dqhat = dqn * qlg_ref[...]
        dcq = rq * (dqhat - qhat * jnp.mean(dqhat * qhat, axis=-1, keepdims=True))
        dkvlg_ref[...] += jnp.sum(dkvn * kvhat, axis=0, keepdims=True)
        dkvhat = dkvn * kvlg_ref[...]
        dckv = rkv * (dkvhat - kvhat * jnp.mean(dkvhat * kvhat, axis=-1, keepdims=True))
        du_ref[:, O_CQ:O_CQ + R_Q] = dcq.astype(BF16)
        du_ref[:, O_CKV:O_CKV + R_KV] = dckv.astype(BF16)

    t1 = lambda s: jnp.minimum(s, n_tiles - 1)
    t2 = lambda s: jnp.clip(s - 1, 0, n_tiles - 1)
    t3 = lambda s: jnp.clip(s - 2, 0, n_tiles - 1)
    latent = lambda t: jnp.maximum(t - 1, 0)
    lat = lambda s: (latent(t1(s)), 0)
    lat3 = lambda s: (latent(t3(s)), 0)
    full = lambda shape: pl.BlockSpec(shape, lambda s: (0,) * len(shape))
    rope_spec = pl.BlockSpec((TILE, 128), lambda s: (t1(s), 0))
    return pl.pallas_call(
        body, name="inproj_bwd", grid=(n_tiles + 2,),
        out_shape=(jax.ShapeDtypeStruct((seq, D_MODEL), F32), jax.ShapeDtypeStruct((U_PAD, D_MODEL), F32),
                   jax.ShapeDtypeStruct((N_HEADS * D_HEAD_PAD, R_Q), F32), jax.ShapeDtypeStruct((N_HEADS * 256, R_KV), F32),
                   jax.ShapeDtypeStruct((1, R_Q), F32), jax.ShapeDtypeStruct((1, R_KV), F32),
                   jax.ShapeDtypeStruct((1, D_HEAD_PAD), F32), jax.ShapeDtypeStruct((1, D_HEAD_PAD), F32),
                   jax.ShapeDtypeStruct((1, D_MODEL), F32), jax.ShapeDtypeStruct((8, D_MODEL), F32)),
        in_specs=[pl.BlockSpec((TILE, D_MODEL), lambda s: (latent(t2(s)), 0)), pl.BlockSpec((TILE, D_MODEL), lat3),
                  full((TILE, D_MODEL)), full((8, D_MODEL)), full((8, D_MODEL)),
                  full((1, D_MODEL)), full((U_PAD, D_MODEL)), full((1, R_Q)), full((N_HEADS * D_HEAD_PAD, R_Q)),
                  full((1, R_KV)), full((R_KV, N_HEADS * 256)), full((1, D_HEAD_PAD)), full((1, D_HEAD_PAD)),
                  rope_spec, rope_spec, rope_spec,
                  pl.BlockSpec((TILE, R_Q), lambda s: (t1(s), O_CQ // R_Q)),
                  pl.BlockSpec((TILE, R_KV), lambda s: (t1(s), O_CKV // R_KV)),
                  pl.BlockSpec((TILE, 128), lambda s: (t1(s), O_KR // 128)),
                  pl.BlockSpec((N_HEADS, TILE, D_HEAD_PAD), lambda s: (0, latent(t1(s)), 0)),
                  pl.BlockSpec((N_HEADS, D_HEAD_PAD, TILE), lambda s: (0, 0, t1(s))),
                  pl.BlockSpec((N_HEADS, D_V, TILE), lambda s: (0, 0, t1(s))),
                  pl.BlockSpec((TILE, D_ATTN), lat), pl.BlockSpec((TILE, D_POOL), lat), pl.BlockSpec((TILE, D_POOL), lat),
                  pl.BlockSpec((HALO, D_POOL), lambda s: (jnp.maximum((t1(s) - 1) * rows8 - 1, 0), 0)),
                  pl.BlockSpec((HALO, D_POOL), lambda s: (jnp.minimum(jnp.maximum(t1(s), 1) * rows8, last8), 0)),
                  pl.BlockSpec((TILE, D_MODEL), lat3)],
        out_specs=(pl.BlockSpec((TILE, D_MODEL), lat3), full((U_PAD, D_MODEL)), full((N_HEADS * D_HEAD_PAD, R_Q)),
                   full((N_HEADS * 256, R_KV)), full((1, R_Q)), full((1, R_KV)), full((1, D_HEAD_PAD)),
                   full((1, D_HEAD_PAD)), full((1, D_MODEL)), full((8, D_MODEL))),
        scratch_shapes=[pltpu.VMEM((TILE, U_PAD), BF16), pltpu.VMEM((TILE, U_PAD), BF16),
                        pltpu.VMEM((TILE, D_MODEL), F32), pltpu.VMEM((TILE, D_MODEL), F32)],
        compiler_params=pltpu.CompilerParams(dimension_semantics=("arbitrary",), vmem_limit_bytes=VMEM_LIMIT),
    )(x, x, ctx, modrows, bmodrows, norm_g, w_in_p, q_lora_g, w_uq_p, kv_lora_g, w_ukv, qng_p, kng_p, cos, slo, shi,
      u, u, u, dq, dk, dv, dga, dgp, dpl, dpl, dpl, g1)


SMALL_LAYOUT = ((0, D_MODEL), (1, R_Q), (2, R_KV), (3, D_HEAD_PAD), (4, D_HEAD_PAD), (5, D_POOL))
ROW_DMOD_B, ROW_DMOD_C, ROW_LOSS = 6, 9, 12


def _epilogue(parts, landed, smalls, dmod4, dgate, loss_acc, w_mod_l):
    n_a, n_l, n_s = len(parts), len(landed), len(smalls)
    n_mod = w_mod_l.shape[1]
    blk = [p.shape[1:] for p in parts]
    small_out = [D_MODEL, R_Q, R_KV, D_HEAD, D_HEAD, D_POOL]

    def body(*refs):
        part_refs = refs[0:n_a]
        land_refs = refs[n_a:n_a + n_l]
        small_in = refs[n_a + n_l:n_a + n_l + n_s]
        dmod4_ref, dgate_ref, loss_ref, wmod_ref = refs[n_a + n_l + n_s:n_a + n_l + n_s + 4]
        outs = refs[n_a + n_l + n_s + 4:]
        red_refs = outs[0:n_a]
        landsum_refs = outs[n_a:n_a + n_l]
        ccg_ref = outs[n_a + n_l]
        sum_refs = outs[n_a + n_l + 1:n_a + n_l + 1 + n_s]
        gbmod_ref, dmy_ref, lossout_ref = outs[n_a + n_l + 1 + n_s:n_a + n_l + 4 + n_s]
        scr = outs[n_a + n_l + 4 + n_s:]
        recv1, own1, sum1b, recv2 = scr[0:n_a], scr[n_a:2 * n_a], scr[2 * n_a:3 * n_a], scr[3 * n_a:4 * n_a]
        small_ref, smallg_ref, tot_ref, cc_ref = scr[4 * n_a:4 * n_a + 4]
        ssem1, rsem1, lsem, ssem2, rsem2, ssem_s, rsem_s, ssem_cc, rsem_cc = scr[4 * n_a + 4:]
        x, y, c = _coords()
        me = (x, y, c)
        sibling = (x, y, 1 - c)

        stage1, own_copies = [], []
        for a in range(n_a):
            for b in range(4):
                dev = 4 * (b >> 1) + 2 * (b & 1)
                stage1.append(pltpu.make_async_remote_copy(
                    src_ref=part_refs[a].at[dev + (1 - c)], dst_ref=recv1[a].at[b],
                    send_sem=ssem1.at[a, b], recv_sem=rsem1.at[a, b], device_id=sibling, device_id_type=MESH))
                own_copies.append(pltpu.make_async_copy(part_refs[a].at[dev + c], own1[a].at[b], lsem.at[a, b]))
                stage1[-1].start()
                own_copies[-1].start()

        small_ref[...] = jnp.zeros_like(small_ref)
        for ref, (row, width) in zip(small_in, SMALL_LAYOUT):
            small_ref[row:row + 1, 0:width] = ref[...]
        small_ref[ROW_DMOD_B:ROW_DMOD_B + 2, :] = dmod4_ref[0:2, :]
        small_ref[ROW_DMOD_B + 2:ROW_DMOD_B + 3, :] = dgate_ref[...]
        small_ref[ROW_DMOD_C:ROW_DMOD_C + 2, :] = dmod4_ref[2:4, :]
        small_ref[ROW_LOSS:ROW_LOSS + 1, 0:128] = loss_ref[0:1, :]
        smallg_ref[_lin(me)] = small_ref[...]
        s_recv, s_send = _gather_to_all(small_ref, smallg_ref, ssem_s, rsem_s)
        s_recv()
        tot = smallg_ref[0]
        for d in range(1, N_DEV):
            tot = tot + smallg_ref[d]
        tot_ref[...] = tot
        for ref, (row, _), width in zip(sum_refs, SMALL_LAYOUT, small_out):
            ref[...] = tot_ref[row:row + 1, 0:width]
        lossout_ref[...] = tot_ref[8:16, 0:128]
        lin = _lin(me)

        def my_columns(three_rows):
            v = jnp.concatenate(three_rows, axis=1)
            out = jnp.zeros((1, n_mod), F32)
            for d in range(N_DEV):
                out = out + jnp.where(lin == d, v[:, d * n_mod:(d + 1) * n_mod], 0.0)
            return out

        rows3 = lambda ref, r0: [ref[r0 + t:r0 + t + 1, :] for t in range(3)]
        dmodc = rows3(tot_ref, ROW_DMOD_C)
        gbmod_ref[...] = jnp.concatenate(rows3(tot_ref, ROW_DMOD_B), axis=1) + jnp.concatenate(dmodc, axis=1)
        dmy_ref[...] = jnp.zeros_like(dmy_ref)
        for b in range(N_DEV):
            dmy_ref[b:b + 1, :] = my_columns(rows3(smallg_ref.at[b], ROW_DMOD_B))
        dmy = my_columns(dmodc)
        dmy_ref[N_DEV:N_DEV + 1, :] = dmy
        part = lax.dot_general(jnp.broadcast_to(dmy, (8, n_mod)), wmod_ref[...], (((1,), (1,)), ((), ())),
                               precision=HIGHEST, preferred_element_type=F32)

        cc_ref[...] = part
        ccg_ref[_lin(me)] = part
        cc_recv, cc_send = _gather_to_all(cc_ref, ccg_ref, ssem_cc, rsem_cc)

        stage2 = []
        for a in range(n_a):
            for b in range(4):
                stage1[4 * a + b].wait_recv()
                own_copies[4 * a + b].wait()
                sum1b[a][b] = (own1[a][b] + recv1[a][b]).astype(BF16)
            for k in (1, 2, 3):
                tx, ty = _flip(x, (k >> 1) & 1), _flip(y, k & 1)
                stage2.append(pltpu.make_async_remote_copy(
                    src_ref=sum1b[a].at[2 * tx + ty], dst_ref=recv2[a].at[k - 1], send_sem=ssem2.at[a, k - 1],
                    recv_sem=rsem2.at[a, k - 1], device_id=(tx, ty, c), device_id_type=MESH))
                stage2[-1].start()
        for a in range(n_a):
            own = own1[a][2 * x + y] + recv1[a][2 * x + y]
            for k in (1, 2, 3):
                stage2[3 * a + k - 1].wait_recv()
                own = own + recv2[a][k - 1].astype(F32)
            red_refs[a][...] = own

        for ref, out in zip(land_refs, landsum_refs):
            acc = ref[0].astype(F32)
            for d in range(1, N_DEV):
                acc = acc + ref[d].astype(F32)
            out[...] = acc
        cc_recv()
        for cp in stage1 + stage2:
            cp.wait_send()
        s_send()
        cc_send()

    vm = pl.BlockSpec(memory_space=pltpu.VMEM)
    sds = jax.ShapeDtypeStruct
    return pl.pallas_call(
        body, name="epilogue_reduce",
        out_shape=(*[sds(s, F32) for s in blk], *[sds(a.shape[1:], F32) for a in landed], sds((N_DEV, 8, D_MODEL), F32),
                   *[sds((1, w), F32) for w in small_out], sds((1, 3 * D_MODEL), F32), sds((16, n_mod), F32),
                   sds((8, 128), F32)),
        in_specs=[pl.BlockSpec(memory_space=pl.ANY)] * n_a + [vm] * (n_l + n_s + 4),
        out_specs=tuple([vm] * (n_a + n_l + n_s + 4)),
        scratch_shapes=[*[pltpu.VMEM((4,) + s, F32) for s in blk], *[pltpu.VMEM((4,) + s, F32) for s in blk],
                        *[pltpu.VMEM((4,) + s, BF16) for s in blk], *[pltpu.VMEM((3,) + s, BF16) for s in blk],
                        pltpu.VMEM((SMALL_ROWS, D_MODEL), F32), pltpu.VMEM((N_DEV, SMALL_ROWS, D_MODEL), F32),
                        pltpu.VMEM((SMALL_ROWS, D_MODEL), F32), pltpu.VMEM((8, D_MODEL), F32),
                        pltpu.SemaphoreType.DMA((n_a, 4)), pltpu.SemaphoreType.DMA((n_a, 4)), pltpu.SemaphoreType.DMA((n_a, 4)),
                        pltpu.SemaphoreType.DMA((n_a, 3)), pltpu.SemaphoreType.DMA((n_a, 3)),
                        pltpu.SemaphoreType.DMA((7,)), pltpu.SemaphoreType.DMA((7,)),
                        pltpu.SemaphoreType.DMA((7,)), pltpu.SemaphoreType.DMA((7,))],
        compiler_params=pltpu.CompilerParams(vmem_limit_bytes=VMEM_LIMIT),
    )(*parts, *landed, *smalls, dmod4, dgate, loss_acc, w_mod_l)


def _adamw(weights, grads, ms, vs, c_all_t, dmod_my, p2g):
    n = len(weights)

    def body(*refs):
        w_refs = refs[0:n]
        g_in = refs[n:2 * n - 2]
        m_refs = refs[2 * n - 2:3 * n - 2]
        v_refs = refs[3 * n - 2:4 * n - 2]
        cat_ref, dmod_ref, p2g_ref = refs[4 * n - 2:4 * n + 1]
        outs = refs[4 * n + 1:]
        gcc_ref, gwm_ref = outs[0], outs[1]
        d_refs, nm_refs, nv_refs = outs[2:2 + n], outs[2 + n:2 + 2 * n], outs[2 + 2 * n:2 + 3 * n]

        part = p2g_ref[0, 0:1, :]
        for d in range(1, N_DEV):
            part = part + p2g_ref[d, 0:1, :]
        cc = w_refs[0][...]
        sg = _sigmoid(cc)
        gcc_ref[...] = part * (sg * (1.0 + cc * (1.0 - sg)))
        cat = cat_ref[...]
        gwm_ref[...] = lax.dot_general(cat * _sigmoid(cat), dmod_ref[...], (((1,), (0,)), ((), ())), precision=HIGHEST,
                                       preferred_element_type=F32)
        for idx in range(n):
            g = (gcc_ref, gwm_ref)[idx][...] if idx < 2 else g_in[idx - 2][...]
            m = ADAM_B1 * m_refs[idx][...] + (1.0 - ADAM_B1) * g
            v = ADAM_B2 * v_refs[idx][...] + (1.0 - ADAM_B2) * (g * g)
            m_hat = m / (1.0 - ADAM_B1 ** ADAM_STEP)
            v_hat = v / (1.0 - ADAM_B2 ** ADAM_STEP)
            d_refs[idx][...] = -ADAM_LR * (m_hat / (jnp.sqrt(v_hat) + ADAM_EPS) + ADAM_WD * w_refs[idx][...])
            nm_refs[idx][...] = m
            nv_refs[idx][...] = v

    vm = pl.BlockSpec(memory_space=pltpu.VMEM)
    shapes = [jax.ShapeDtypeStruct(w.shape, F32) for w in weights]
    args = list(weights) + list(grads[2:]) + list(ms) + list(vs) + [c_all_t, dmod_my, p2g]
    out_shape = tuple(shapes[0:2] + shapes * 3)
    return pl.pallas_call(
        body, name="adamw_update", out_shape=out_shape, in_specs=[vm] * len(args), out_specs=tuple([vm] * len(out_shape)),
        compiler_params=pltpu.CompilerParams(vmem_limit_bytes=VMEM_LIMIT),
    )(*args)


def _rope_tables(seq):
    rows = seq // GRID_W
    row = np.repeat(np.arange(rows, dtype=np.float32), GRID_W)
    col = np.tile(np.arange(GRID_W, dtype=np.float32), rows)
    n_freq = D_ROPE // 4
    inv = (np.float32(ROPE_BASE) ** (-np.arange(n_freq, dtype=np.float32) / np.float32(n_freq))).astype(np.float32)
    ang_r = (row[:, None] * inv).astype(np.float32)
    ang_c = (col[:, None] * inv).astype(np.float32)
    ang = np.concatenate([ang_r, ang_r, ang_c, ang_c], axis=-1)
    cos, sin = np.cos(ang).astype(np.float32), np.sin(ang).astype(np.float32)
    low = (np.arange(D_ROPE) % 32) < 16

    def table(t, fill):
        out = np.full((TILE + seq, 128), fill, np.float32)
        out[TILE:, 0:D_ROPE] = t
        return jnp.asarray(out)

    return table(cos, 1.0), table(np.where(low, -sin, 0.0), 0.0), table(np.where(low, 0.0, sin), 0.0)


def kernel(x, c, ctx, c_ctx, w_mod, b_mod, norm_g, w_in, q_lora_g, w_uq, kv_lora_g, w_ukv, q_norm_g, k_norm_g, w_pool, pool_scale, w_out, loss_target, m_c_ctx, m_w_mod, m_b_mod, m_norm_g, m_w_in, m_q_lora_g, m_w_uq, m_kv_lora_g, m_w_ukv, m_q_norm_g, m_k_norm_g, m_w_pool, m_pool_scale, m_w_out, v_c_ctx, v_w_mod, v_b_mod, v_norm_g, v_w_in, v_q_lora_g, v_w_uq, v_kv_lora_g, v_w_ukv, v_q_norm_g, v_k_norm_g, v_w_pool, v_pool_scale, v_w_out):
    seq = x.shape[1]
    assert ctx.shape[1] == TILE and seq % TILE == 0 and seq % GRID_W == 0
    ix, iy, ic = lax.axis_index("x"), lax.axis_index("y"), lax.axis_index("c")
    me = 4 * ix + 2 * iy + ic
    x2, ctx2, tgt2 = x[0], ctx[0], loss_target[0]
    w_mod_l, w_ukv_l, w_out_l = w_mod[0], w_ukv[0], w_out[0]
    c_ctx_row = c_ctx.reshape(1, D_MODEL)

    tr = lambda a: jnp.transpose(a[0])
    w_in_tl, w_uq_tl = tr(w_in), tr(w_uq)
    cg, modg, wg_in, wg_uq, wg_ukv = _prologue(
        jnp.broadcast_to(c, (8, D_MODEL)), jnp.broadcast_to(c_ctx_row, (8, D_MODEL)), w_mod_l,
        [w_in_tl, w_uq_tl, w_ukv_l])
    mod_all = jnp.transpose(modg, (1, 0, 2)).reshape(16, 3 * D_MODEL)
    mod_b = lax.dynamic_slice_in_dim(mod_all, me, 1, axis=0).reshape(3, D_MODEL)
    mod_c = mod_all[8].reshape(3, D_MODEL)
    modrows = jnp.concatenate([mod_b, mod_c[0:2], jnp.zeros((3, D_MODEL), F32)], axis=0)
    b3 = b_mod.reshape(3, D_MODEL)
    bmodrows = jnp.concatenate([b3, b3[0:2], jnp.zeros((3, D_MODEL), F32)], axis=0)

    w_in_t = wg_in.reshape(D_IN_PROJ, D_MODEL)
    w_in_p = jnp.concatenate([w_in_t[448:], w_in_t[0:384], w_in_t[384:448],
                              jnp.zeros((U_PAD - D_IN_PROJ, D_MODEL), BF16)], axis=0)
    w_uq_p = jnp.concatenate([wg_uq.reshape(N_HEADS, D_HEAD, R_Q), jnp.zeros((N_HEADS, D_HEAD_PAD - D_HEAD, R_Q), BF16)],
                             axis=1).reshape(N_HEADS * D_HEAD_PAD, R_Q)
    w_ukv_f = jnp.transpose(wg_ukv, (1, 0, 2)).reshape(R_KV, N_HEADS * 256)
    qng_p = jnp.concatenate([q_norm_g, jnp.zeros((1, D_HEAD_PAD - D_HEAD), F32)], axis=1)
    kng_p = jnp.concatenate([k_norm_g, jnp.zeros((1, D_HEAD_PAD - D_HEAD), F32)], axis=1)
    cos, slo, shi = _rope_tables(seq)

    u, q, k, v = _inproj_fwd(x2, ctx2, modrows, bmodrows, norm_g, w_in_p, q_lora_g, w_uq_p, kv_lora_g, w_ukv_f,
                             qng_p, kng_p, cos, slo, shi)
    attn, lse, wg_out = _attn_fwd(q, k, v, min(2048, seq), w_out_l.astype(BF16))
    (loss_acc, g1, dgate, d_attn, dga, dgp, dpl, gwo_b, gwp, dps) = _mix(
        x2, tgt2, attn, u, modrows, bmodrows, w_pool[0], pool_scale, wg_out.reshape(D_MODEL, D_MODEL))

    blocks = lambda a: a.reshape((N_DEV, a.shape[0] // N_DEV) + a.shape[1:])
    dq, dk, dv, land_wo, land_wp = _attn_bwd(q, k, v, attn, lse, d_attn, min(1024, seq), blocks(gwo_b),
                                             gwp.reshape(4 * GROUP_DIM, GROUP_DIM))
    (grad_x, gwin_p, gwuq_p, gwukv_t, dqlg, dkvlg, dqng, dkng, dng, dmod4) = _inproj_bwd(
        x2, ctx2, modrows, bmodrows, norm_g, w_in_p, q_lora_g, w_uq_p, kv_lora_g, w_ukv_f, qng_p, kng_p, cos, slo, shi,
        u, dq, dk, dv, dga, dgp, dpl, g1)

    gwin_t = jnp.concatenate([gwin_p[O_CQ:O_KR], gwin_p[O_KR:D_IN_PROJ], gwin_p[0:O_CQ]], axis=0)
    gwuq_t = gwuq_p.reshape(N_HEADS, D_HEAD_PAD, R_Q)[:, 0:D_HEAD].reshape(N_HEADS * D_HEAD, R_Q)
    parts = [blocks(gwin_t), blocks(gwuq_t), blocks(gwukv_t)]
    (r_win, r_wuq, r_wukv, r_wout, g_w_pool, ccg, g_ng, g_qlg, g_kvlg, g_qng, g_kng, g_ps, g_bmod, dmod_my,
     loss_rows) = _epilogue(parts, [land_wo, land_wp], [dng, dqlg, dkvlg, dqng, dkng, dps], dmod4, dgate, loss_acc, w_mod_l)

    g_w_ukv = jnp.transpose(r_wukv)
    c_rows = cg[:, 0, :]
    c_all_t = jnp.transpose(jnp.concatenate([c_rows, c_ctx_row, jnp.zeros((16 - N_DEV - 1, D_MODEL), F32)], axis=0))

    weights = [c_ctx_row, w_mod_l, b_mod, norm_g, w_in_tl, q_lora_g, w_uq_tl, kv_lora_g, w_ukv_l, q_norm_g, k_norm_g,
               w_pool.reshape(4 * GROUP_DIM, GROUP_DIM), pool_scale, w_out_l]
    grads = [None, None, g_bmod, g_ng, r_win, g_qlg, r_wuq, g_kvlg, g_w_ukv, g_qng, g_kng, g_w_pool, g_ps, r_wout]
    ms = [m_c_ctx.reshape(1, D_MODEL), m_w_mod[0], m_b_mod, m_norm_g, tr(m_w_in), m_q_lora_g, tr(m_w_uq), m_kv_lora_g,
          m_w_ukv[0], m_q_norm_g, m_k_norm_g, m_w_pool.reshape(4 * GROUP_DIM, GROUP_DIM), m_pool_scale, m_w_out[0]]
    vs = [v_c_ctx.reshape(1, D_MODEL), v_w_mod[0], v_b_mod, v_norm_g, tr(v_w_in), v_q_lora_g, tr(v_w_uq), v_kv_lora_g,
          v_w_ukv[0], v_q_norm_g, v_k_norm_g, v_w_pool.reshape(4 * GROUP_DIM, GROUP_DIM), v_pool_scale, v_w_out[0]]
    outs = _adamw(weights, grads, ms, vs, c_all_t, dmod_my, ccg)
    grads[0], grads[1] = outs[0], outs[1]
    n = len(weights)
    deltas, new_m, new_v = outs[2:2 + n], outs[2 + n:2 + 2 * n], outs[2 + 2 * n:2 + 3 * n]

    loss = loss_rows[ROW_LOSS - 8, 0]
    final = [c_ctx.shape, w_mod.shape, b_mod.shape, norm_g.shape, w_in.shape, q_lora_g.shape, w_uq.shape, kv_lora_g.shape,
             w_ukv.shape, q_norm_g.shape, k_norm_g.shape, w_pool.shape, pool_scale.shape, w_out.shape]
    transposed = (4, 6)

    def shaped(arrs):
        return [(jnp.transpose(a) if i in transposed else a).reshape(s) for i, (a, s) in enumerate(zip(arrs, final))]

    return (loss, grad_x[None], *shaped(grads), *shaped(deltas), *shaped(new_m), *shaped(new_v))
```

```python
import numpy as np

import jax
import jax.numpy as jnp
from jax import lax
from jax.experimental import pallas as pl
from jax.experimental.pallas import tpu as pltpu

F32 = jnp.float32
BF16 = jnp.bfloat16
MESH = pl.DeviceIdType.MESH
HIGHEST = lax.Precision.HIGHEST

D_MODEL = 1024
N_HEADS = 4
D_NOPE = 128
D_ROPE = 64
D_HEAD = D_NOPE + D_ROPE
D_HEAD_PAD = 256
D_V = 128
R_Q = 256
R_KV = 128
D_ATTN = 512
D_POOL = 512
POOL_WINDOWS = (2, 4, 8, 16)
GROUP_DIM = 128
GRID_W = 64
ROPE_BASE = 10000.0
NORM_EPS = 1e-6
ATTN_SCALE = D_HEAD ** -0.5
LOG2_E = 1.4426950408889634
LN_2 = 0.6931471805599453
ATTN_ROWS = 256
D_IN_PROJ = 1984
U_PAD = 2048
O_GA, O_PIN, O_GP, O_CQ, O_CKV, O_KR = 0, 512, 1024, 1536, 1792, 1920
TILE = 256
Q_BLOCK = 128
HALO = 16
N_GATES = 1536
N_DEV = 8
VMEM_LIMIT = 56 * 1024 * 1024

ADAM_LR = 0.001
ADAM_B1 = 0.9
ADAM_B2 = 0.999
ADAM_EPS = 1e-08
ADAM_WD = 0.01
ADAM_STEP = 10

SMALL_ROWS = 16


def _dot(a, b):
    return lax.dot_general(a, b, (((1,), (0,)), ((), ())), preferred_element_type=F32)


def _dot_nt(a, b):
    return lax.dot_general(a, b, (((1,), (1,)), ((), ())), preferred_element_type=F32)


def _dot_tn(a, b):
    return lax.dot_general(a, b, (((0,), (0,)), ((), ())), preferred_element_type=F32)


def _sigmoid(x):
    return 1.0 / (1.0 + jnp.exp(-x))


def _rope(p, cos, slo, shi):
    return p * cos + pltpu.roll(p, 112, 1) * slo + pltpu.roll(p, 16, 1) * shi


def _rope_t(d, cos, slo, shi):
    return d * cos + pltpu.roll(d * slo, 16, 1) + pltpu.roll(d * shi, 112, 1)


def _shift_rows(a, k):
    n = a.shape[0]
    return pltpu.roll(a, (-k) % n, 0)


def _window_sum(x, w, transposed):
    s = (x + _shift_rows(x, 1)) if transposed else (_shift_rows(x, -1) + x)
    step = 1
    while 2 * step < w:
        s = _shift_rows(s, -step) + _shift_rows(s, step)
        step *= 2
    return s


def _inv_count(tpos, w, seq):
    lo = jnp.maximum(tpos - w // 2, 0)
    hi = jnp.minimum(tpos - w // 2 + w, seq)
    return 1.0 / jnp.maximum(hi - lo, 1).astype(F32)


def _coords():
    return lax.axis_index("x"), lax.axis_index("y"), lax.axis_index("c")


def _flip(v, bit):
    return (1 - v) if bit else v


def _peer(k):
    x, y, c = _coords()
    return (_flip(x, (k >> 2) & 1), _flip(y, (k >> 1) & 1), _flip(c, k & 1))


def _lin(p):
    return 4 * p[0] + 2 * p[1] + p[2]


def _gather_to_all(src_ref, slots_ref, send_sems, recv_sems):
    me = _coords()
    copies = []
    for k in range(1, N_DEV):
        cp = pltpu.make_async_remote_copy(
            src_ref=src_ref, dst_ref=slots_ref.at[_lin(me)], send_sem=send_sems.at[k - 1], recv_sem=recv_sems.at[k - 1],
            device_id=_peer(k), device_id_type=MESH)
        cp.start()
        copies.append(cp)

    def wait_recv():
        for k in range(1, N_DEV):
            pltpu.make_async_remote_copy(
                src_ref=src_ref, dst_ref=slots_ref.at[_lin(_peer(k))], send_sem=send_sems.at[k - 1],
                recv_sem=recv_sems.at[k - 1], device_id=_peer(k), device_id_type=MESH).wait_recv()

    def wait_send():
        for cp in copies:
            cp.wait_send()

    return wait_recv, wait_send


def _prologue(c8, cctx8, w_mod_l, shards):
    n_mod = w_mod_l.shape[1]
    n_w = len(shards)

    def body(c8_ref, cctx_ref, wmod_ref, *refs):
        w_refs = refs[0:n_w]
        cg_ref, modg_ref = refs[n_w], refs[n_w + 1]
        wg_refs = refs[n_w + 2:2 * n_w + 2]
        modblk_ref = refs[2 * n_w + 2]
        wb_refs = refs[2 * n_w + 3:3 * n_w + 3]
        ssem_w, rsem_w, ssem_c, rsem_c, ssem_m, rsem_m = refs[3 * n_w + 3:]
        x, y, c = _coords()
        me = (x, y, c)
        sibling = (x, y, 1 - c)
        chips = [(1 - x, y), (x, 1 - y), (1 - x, 1 - y)]

        def wcopies(k, block, to, own=False):
            out = []
            for a in range(n_w):
                slot = wg_refs[a].at[_lin(block)]
                out.append(pltpu.make_async_remote_copy(
                    src_ref=wb_refs[a] if own else slot, dst_ref=slot, send_sem=ssem_w.at[a, k], recv_sem=rsem_w.at[a, k],
                    device_id=to, device_id_type=MESH))
            return out

        for a in range(n_w):
            wb_refs[a][...] = w_refs[a][...].astype(BF16)
        first = wcopies(0, me, sibling, own=True)
        for j, chip in enumerate(chips):
            first += wcopies(1 + j, me, (*chip, c), own=True)
        for cp in first:
            cp.start()
        for a in range(n_w):
            wg_refs[a][_lin(me)] = wb_refs[a][...]

        cg_ref[_lin(me)] = c8_ref[...]
        c_recv, c_send = _gather_to_all(c8_ref, cg_ref, ssem_c, rsem_c)
        c_recv()
        row = lax.broadcasted_iota(jnp.int32, (8, D_MODEL), 0)
        c_all = jnp.zeros((8, D_MODEL), F32)
        for d in range(N_DEV):
            c_all = c_all + jnp.where(row == d, cg_ref[d], 0.0)
        cc = cctx_ref[...]
        a = jnp.concatenate([c_all * _sigmoid(c_all), cc * _sigmoid(cc)], axis=0)
        modblk_ref[...] = lax.dot_general(a, wmod_ref[...], (((1,), (0,)), ((), ())), precision=HIGHEST,
                                          preferred_element_type=F32)
        modg_ref[_lin(me)] = modblk_ref[...]
        m_recv, m_send = _gather_to_all(modblk_ref, modg_ref, ssem_m, rsem_m)
        m_recv()

        passed = []
        for j, chip in enumerate(chips):
            for cp in wcopies(1 + j, (*chip, c), me):
                cp.wait_recv()
            fwd = wcopies(4 + j, (*chip, c), sibling)
            for cp in fwd:
                cp.start()
            passed += fwd
        for cp in wcopies(0, sibling, me):
            cp.wait_recv()
        for j, chip in enumerate(chips):
            for cp in wcopies(4 + j, (*chip, 1 - c), me):
                cp.wait_recv()
        for cp in first + passed:
            cp.wait_send()
        c_send()
        m_send()

    vm = pl.BlockSpec(memory_space=pltpu.VMEM)
    return pl.pallas_call(
        body, name="prologue_gather",
        out_shape=(jax.ShapeDtypeStruct((N_DEV, 8, D_MODEL), F32), jax.ShapeDtypeStruct((N_DEV, 16, n_mod), F32),
                   *[jax.ShapeDtypeStruct((N_DEV,) + s.shape, BF16) for s in shards]),
        in_specs=[vm] * (3 + n_w), out_specs=tuple([vm] * (2 + n_w)),
        scratch_shapes=[pltpu.VMEM((16, n_mod), F32), *[pltpu.VMEM(s.shape, BF16) for s in shards],
                        pltpu.SemaphoreType.DMA((n_w, 7)), pltpu.SemaphoreType.DMA((n_w, 7)),
                        pltpu.SemaphoreType.DMA((7,)), pltpu.SemaphoreType.DMA((7,)),
                        pltpu.SemaphoreType.DMA((7,)), pltpu.SemaphoreType.DMA((7,))],
        compiler_params=pltpu.CompilerParams(vmem_limit_bytes=VMEM_LIMIT),
    )(c8, cctx8, w_mod_l, *shards)


def _modulated_input(is_ctx, x_ref, ctx_ref, mod_ref, bmod_ref, ng_ref):
    xt = jnp.where(is_ctx, ctx_ref[...], x_ref[...])
    shift = jnp.where(is_ctx, mod_ref[3:4, :] + bmod_ref[3:4, :], mod_ref[0:1, :] + bmod_ref[0:1, :])
    scale = jnp.where(is_ctx, mod_ref[4:5, :] + bmod_ref[4:5, :], mod_ref[1:2, :] + bmod_ref[1:2, :])
    r = lax.rsqrt(jnp.mean(xt * xt, axis=-1, keepdims=True) + NORM_EPS)
    xg = (xt * r) * ng_ref[...]
    h = xg * (1.0 + scale) + shift
    return xt, r, xg, h, scale


def _inproj_fwd(x, ctx, modrows, bmodrows, norm_g, w_in_p, q_lora_g, w_uq_p, kv_lora_g, w_ukv, qng_p, kng_p, cos, slo, shi):
    seq = x.shape[0]
    n_tiles = seq // TILE + 1
    tot = seq + TILE

    n_mla = R_Q + R_KV + 128

    def body(x_ref, ctx_ref, mod_ref, bmod_ref, ng_ref, win_ref, qlg_ref, wuq_ref, kvlg_ref, wukv_ref, qng_ref, kng_ref,
             cos_ref, slo_ref, shi_ref, ug_ref, um_ref, q_ref, k_ref, v_ref, stash_even, stash_odd):
        s = pl.program_id(0)

        @pl.when(s == 0)
        def _():
            stash_odd[...] = jnp.zeros_like(stash_odd)

        refs = (x_ref, ctx_ref, mod_ref, bmod_ref, ng_ref, win_ref, qlg_ref, wuq_ref, kvlg_ref, wukv_ref, qng_ref, kng_ref,
                cos_ref, slo_ref, shi_ref, ug_ref, um_ref, q_ref, k_ref, v_ref)

        @pl.when(lax.rem(s, 2) == 0)
        def _():
            stages(s, refs, stash_even, stash_odd)

        @pl.when(lax.rem(s, 2) == 1)
        def _():
            stages(s, refs, stash_odd, stash_even)

    def stages(s, refs, fill, prev_ref):
        (x_ref, ctx_ref, mod_ref, bmod_ref, ng_ref, win_ref, qlg_ref, wuq_ref, kvlg_ref, wukv_ref, qng_ref, kng_ref,
         cos_ref, slo_ref, shi_ref, ug_ref, um_ref, q_ref, k_ref, v_ref) = refs
        cos_t, slo_t, shi_t = cos_ref[...], slo_ref[...], shi_ref[...]
        prev = prev_ref[...]
        cq = prev[:, 0:R_Q]
        qn = cq * lax.rsqrt(jnp.mean(cq * cq, axis=-1, keepdims=True) + NORM_EPS) * qlg_ref[...]
        q = _dot_nt(qn.astype(BF16), wuq_ref[...])
        qg = qng_ref[...] * (ATTN_SCALE * LOG2_E)
        for hd in range(N_HEADS):
            qh = q[:, hd * D_HEAD_PAD:(hd + 1) * D_HEAD_PAD]
            rr = lax.rsqrt(jnp.sum(qh * qh, axis=-1, keepdims=True) * (1.0 / D_HEAD) + NORM_EPS)
            qy = qh * rr * qg
            q_ref[hd, :, 0:D_NOPE] = qy[:, 0:D_NOPE].astype(BF16)
            q_ref[hd, :, D_NOPE:D_HEAD_PAD] = _rope(qy[:, D_NOPE:D_HEAD_PAD], cos_t, slo_t, shi_t).astype(BF16)

        ckv = prev[:, R_Q:R_Q + R_KV]
        kvn = ckv * lax.rsqrt(jnp.mean(ckv * ckv, axis=-1, keepdims=True) + NORM_EPS) * kvlg_ref[...]
        kv = _dot(kvn.astype(BF16), wukv_ref[...])
        krz = prev[:, R_Q + R_KV:n_mla]
        kr_ss = jnp.sum(krz * krz, axis=-1, keepdims=True)
        kg = kng_ref[...]
        for hd in range(N_HEADS):
            kn = kv[:, hd * 256:hd * 256 + D_NOPE]
            rr = lax.rsqrt((jnp.sum(kn * kn, axis=-1, keepdims=True) + kr_ss) * (1.0 / D_HEAD) + NORM_EPS)
            k_ref[hd, :, 0:D_NOPE] = (kn * rr * kg[:, 0:D_NOPE]).astype(BF16)
            k_ref[hd, :, D_NOPE:D_HEAD_PAD] = _rope(krz * rr * kg[:, D_NOPE:D_HEAD_PAD], cos_t, slo_t, shi_t).astype(BF16)
            v_ref[hd] = kv[:, hd * 256 + D_NOPE:(hd + 1) * 256].astype(BF16)

        _, _, _, h, _ = _modulated_input(s == 0, x_ref, ctx_ref, mod_ref, bmod_ref, ng_ref)
        u = _dot_nt(h.astype(BF16), win_ref[...])
        ug_ref[...] = u[:, 0:N_GATES].astype(BF16)
        um_ref[...] = u[:, N_GATES:U_PAD]
        fill[...] = u[:, N_GATES:U_PAD]

    first = lambda s: jnp.minimum(s, n_tiles - 1)
    second = lambda s: jnp.maximum(s - 1, 0)
    latent = lambda t: jnp.maximum(t - 1, 0)
    full = lambda shape: pl.BlockSpec(shape, lambda s: (0,) * len(shape))
    rope_spec = pl.BlockSpec((TILE, 128), lambda s: (second(s), 0))
    return pl.pallas_call(
        body, name="inproj_fwd", grid=(n_tiles + 1,),
        out_shape=(jax.ShapeDtypeStruct((seq, N_GATES), BF16), jax.ShapeDtypeStruct((tot, n_mla), F32),
                   jax.ShapeDtypeStruct((N_HEADS, seq, D_HEAD_PAD), BF16),
                   jax.ShapeDtypeStruct((N_HEADS, tot, D_HEAD_PAD), BF16),
                   jax.ShapeDtypeStruct((N_HEADS, tot, D_V), BF16)),
        in_specs=[pl.BlockSpec((TILE, D_MODEL), lambda s: (latent(first(s)), 0)), full((TILE, D_MODEL)), full((8, D_MODEL)),
                  full((8, D_MODEL)), full((1, D_MODEL)), full((U_PAD, D_MODEL)), full((1, R_Q)),
                  full((N_HEADS * D_HEAD_PAD, R_Q)), full((1, R_KV)), full((R_KV, N_HEADS * 256)), full((1, D_HEAD_PAD)),
                  full((1, D_HEAD_PAD)), rope_spec, rope_spec, rope_spec],
        out_specs=(pl.BlockSpec((TILE, N_GATES), lambda s: (latent(first(s)), 0)),
                   pl.BlockSpec((TILE, n_mla), lambda s: (first(s), 0)),
                   pl.BlockSpec((N_HEADS, TILE, D_HEAD_PAD), lambda s: (0, latent(second(s)), 0)),
                   pl.BlockSpec((N_HEADS, TILE, D_HEAD_PAD), lambda s: (0, second(s), 0)),
                   pl.BlockSpec((N_HEADS, TILE, D_V), lambda s: (0, second(s), 0))),
        scratch_shapes=[pltpu.VMEM((TILE, n_mla), F32), pltpu.VMEM((TILE, n_mla), F32)],
        compiler_params=pltpu.CompilerParams(dimension_semantics=("arbitrary",), vmem_limit_bytes=VMEM_LIMIT),
    )(x, ctx, modrows, bmodrows, norm_g, w_in_p, q_lora_g, w_uq_p, kv_lora_g, w_ukv, qng_p, kng_p, cos, slo, shi)


def _hosted_exchange(first, last, items, send_sems, recv_sems, local_sems):
    me = _lin(_coords())

    def remote(n, k, src, land, scatter):
        peer = _peer(k)
        return pltpu.make_async_remote_copy(
            src_ref=src.at[_lin(peer)] if scatter else src, dst_ref=land.at[me], send_sem=send_sems.at[n, k - 1],
            recv_sem=recv_sems.at[n, k - 1], device_id=peer, device_id_type=MESH)

    def local(n, src, land, scatter):
        return pltpu.make_async_copy(src.at[me] if scatter else src, land.at[me], local_sems.at[n])

    @pl.when(first)
    def _():
        for n, (src, land, scatter) in enumerate(items):
            for k in range(1, N_DEV):
                remote(n, k, src, land, scatter).start()
            local(n, src, land, scatter).start()

    @pl.when(last)
    def _():
        for n, (src, land, scatter) in enumerate(items):
            for k in range(1, N_DEV):
                peer = _peer(k)
                pltpu.make_async_remote_copy(
                    src_ref=src.at[0] if scatter else src, dst_ref=land.at[_lin(peer)], send_sem=send_sems.at[n, k - 1],
                    recv_sem=recv_sems.at[n, k - 1], device_id=peer, device_id_type=MESH).wait_recv()
            for k in range(1, N_DEV):
                remote(n, k, src, land, scatter).wait_send()
            local(n, src, land, scatter).wait()


def _attn_fwd(q, k, v, block_q, w_out_b):
    _, seq, _ = q.shape
    n_keys = k.shape[1]
    n_q = seq // block_q

    def body(q_ref, k_ref, v_ref, wo_ref, o_ref, lse_ref, wog_ref, ssem, rsem, lsem):
        h, i = pl.program_id(0), pl.program_id(1)
        _hosted_exchange((h == 0) & (i == 0), (h == N_HEADS - 1) & (i == n_q - 1), [(wo_ref, wog_ref, False)],
                         ssem, rsem, lsem)
        kb, vb = k_ref[0], v_ref[0]
        for r0 in range(0, block_q, ATTN_ROWS):
            rows = slice(r0, r0 + ATTN_ROWS)
            s = _dot_nt(q_ref[0, rows, :], kb)
            m = jnp.max(s, axis=-1, keepdims=True)
            p = jnp.exp2(s - m)
            l = jnp.sum(p, axis=-1, keepdims=True)
            o_ref[rows, :] = _dot(p.astype(BF16), vb) * (1.0 / l)
            lse_ref[0, rows, :] = m + jnp.log2(l)

    hbm = pl.BlockSpec(memory_space=pl.ANY)
    return pl.pallas_call(
        body, name="attn_fwd", grid=(N_HEADS, n_q),
        out_shape=(jax.ShapeDtypeStruct((seq, D_ATTN), F32), jax.ShapeDtypeStruct((N_HEADS, seq, 1), F32),
                   jax.ShapeDtypeStruct((N_DEV,) + w_out_b.shape, w_out_b.dtype)),
        in_specs=[pl.BlockSpec((1, block_q, D_HEAD_PAD), lambda h, i: (h, i, 0)),
                  pl.BlockSpec((1, n_keys, D_HEAD_PAD), lambda h, i: (h, 0, 0)),
                  pl.BlockSpec((1, n_keys, D_V), lambda h, i: (h, 0, 0)), hbm],
        out_specs=(pl.BlockSpec((block_q, D_V), lambda h, i: (i, h)),
                   pl.BlockSpec((1, block_q, 1), lambda h, i: (h, i, 0)), hbm),
        scratch_shapes=[pltpu.SemaphoreType.DMA((1, 7)), pltpu.SemaphoreType.DMA((1, 7)), pltpu.SemaphoreType.DMA((1,))],
        compiler_params=pltpu.CompilerParams(dimension_semantics=("arbitrary", "arbitrary"), vmem_limit_bytes=VMEM_LIMIT),
    )(q, k, v, w_out_b)


def _attn_bwd(q, k, v, o, lse, d_o, block_q, gwo_blocks, gwp):
    _, seq, _ = q.shape
    n_keys = k.shape[1]
    n_q = seq // block_q

    def body(q_ref, k_ref, v_ref, o_ref, lse_ref, do_ref, gwo_ref, gwp_ref, dq_ref, dkt_ref, dvt_ref, lwo_ref, lwp_ref,
             ssem, rsem, lsem):
        h, i = pl.program_id(0), pl.program_id(1)
        _hosted_exchange((h == 0) & (i == 0), (h == N_HEADS - 1) & (i == n_q - 1),
                         [(gwo_ref, lwo_ref, True), (gwp_ref, lwp_ref, False)], ssem, rsem, lsem)

        @pl.when(i == 0)
        def _():
            dkt_ref[...] = jnp.zeros_like(dkt_ref)
            dvt_ref[...] = jnp.zeros_like(dvt_ref)

        kb, vb = k_ref[0], v_ref[0]
        raws, ps = [], []
        for r0 in range(0, block_q, ATTN_ROWS):
            rows = slice(r0, r0 + ATTN_ROWS)
            d_out = do_ref[rows, :]
            p = jnp.exp2(_dot_nt(q_ref[0, rows, :], kb) - lse_ref[0, rows, :])
            dp = _dot_nt(d_out.astype(BF16), vb)
            delta = jnp.sum(d_out * o_ref[rows, :], axis=-1, keepdims=True)
            raw = (p * (dp - delta)).astype(BF16)
            dq_ref[0, rows, :] = _dot(raw, kb)
            raws.append(raw)
            ps.append(p.astype(BF16))
        dkt_ref[0] += _dot_tn(q_ref[0], jnp.concatenate(raws, axis=0))
        dvt_ref[0] += _dot_tn(do_ref[...].astype(BF16), jnp.concatenate(ps, axis=0))

    hbm = pl.BlockSpec(memory_space=pl.ANY)
    return pl.pallas_call(
        body, name="attn_bwd", grid=(N_HEADS, n_q),
        out_shape=(jax.ShapeDtypeStruct((N_HEADS, seq, D_HEAD_PAD), F32),
                   jax.ShapeDtypeStruct((N_HEADS, D_HEAD_PAD, n_keys), F32),
                   jax.ShapeDtypeStruct((N_HEADS, D_V, n_keys), F32),
                   jax.ShapeDtypeStruct(gwo_blocks.shape, gwo_blocks.dtype),
                   jax.ShapeDtypeStruct((N_DEV,) + gwp.shape, gwp.dtype)),
        in_specs=[pl.BlockSpec((1, block_q, D_HEAD_PAD), lambda h, i: (h, i, 0)),
                  pl.BlockSpec((1, n_keys, D_HEAD_PAD), lambda h, i: (h, 0, 0)),
                  pl.BlockSpec((1, n_keys, D_V), lambda h, i: (h, 0, 0)),
                  pl.BlockSpec((block_q, D_V), lambda h, i: (i, h)),
                  pl.BlockSpec((1, block_q, 1), lambda h, i: (h, i, 0)),
                  pl.BlockSpec((block_q, D_V), lambda h, i: (i, h)), hbm, hbm],
        out_specs=(pl.BlockSpec((1, block_q, D_HEAD_PAD), lambda h, i: (h, i, 0)),
                   pl.BlockSpec((1, D_HEAD_PAD, n_keys), lambda h, i: (h, 0, 0)),
                   pl.BlockSpec((1, D_V, n_keys), lambda h, i: (h, 0, 0)), hbm, hbm),
        scratch_shapes=[pltpu.SemaphoreType.DMA((2, 7)), pltpu.SemaphoreType.DMA((2, 7)), pltpu.SemaphoreType.DMA((2,))],
        compiler_params=pltpu.CompilerParams(dimension_semantics=("arbitrary", "arbitrary"), vmem_limit_bytes=VMEM_LIMIT),
    )(q, k, v, o, lse, d_o, gwo_blocks, gwp)


def _mix(x, target, attn, u, modrows, bmodrows, w_pool, pool_scale, w_out):
    seq = x.shape[0]
    n_tiles = seq // TILE
    rows8 = TILE // HALO
    last8 = seq // HALO - 1

    n_blk = seq // Q_BLOCK
    per = TILE // n_blk
    assert TILE % n_blk == 0 and per % 8 == 0 and n_tiles >= 2
    n_stash = 8

    def body(x_ref, t_ref, o_hbm, ga_ref, pin_ref, hb_ref, ha_ref, gp_ref, mod_ref, bmod_ref, wp_ref, ps_ref, wo_ref,
             loss_ref, g1_ref, dgate_ref, do_hbm, dga_ref, dgp_ref, dpl_ref, gwob_ref, gwp_ref, dps_ref,
             abuf, dbuf, gwo_ref, *rest):
        stash_even, stash_odd = rest[0:n_stash], rest[n_stash:2 * n_stash]
        rsem, wsem = rest[2 * n_stash:]
        s = pl.program_id(0)

        def slab_copies(tile, slot, fetch):
            out = []
            for b in range(n_blk):
                hbm_rows = pl.ds(b * Q_BLOCK + tile * per, per)
                buf_rows = pl.ds(b * per, per)
                if fetch:
                    out.append(pltpu.make_async_copy(o_hbm.at[hbm_rows], abuf.at[slot, buf_rows], rsem.at[slot]))
                else:
                    out.append(pltpu.make_async_copy(dbuf.at[slot, buf_rows], do_hbm.at[hbm_rows], wsem.at[slot]))
            return out

        def wait_all(slot, fetch):
            if fetch:
                pltpu.make_async_copy(o_hbm.at[pl.ds(0, TILE)], abuf.at[slot], rsem.at[slot]).wait()
            else:
                pltpu.make_async_copy(dbuf.at[slot], do_hbm.at[pl.ds(0, TILE)], wsem.at[slot]).wait()

        a_slot = lax.rem(s, 2)
        b_slot = 1 - a_slot

        @pl.when(s == 0)
        def _():
            loss_ref[...] = jnp.zeros_like(loss_ref)
            dgate_ref[...] = jnp.zeros_like(dgate_ref)
            gwo_ref[...] = jnp.zeros_like(gwo_ref)
            gwp_ref[...] = jnp.zeros_like(gwp_ref)
            dps_ref[...] = jnp.zeros_like(dps_ref)
            for cp in slab_copies(0, 0, True):
                cp.start()

        @pl.when(s + 1 < n_tiles)
        def _():
            for cp in slab_copies(s + 1, b_slot, True):
                cp.start()

        @pl.when(s < n_tiles)
        def _():
            wait_all(a_slot, True)

        @pl.when(s >= 3)
        def _():
            wait_all(b_slot, False)

        gate = mod_ref[2:3, :] + bmod_ref[2:3, :]
        ps = ps_ref[...]

        def a_vector(slot):
            attn_t = jnp.swapaxes(abuf[slot].reshape(n_blk, per, D_ATTN), 0, 1).reshape(TILE, D_ATTN)
            ga = ga_ref[...].astype(F32)
            sga = _sigmoid(ga)
            silu_ga = ga * sga
            pin = pin_ref[...].astype(F32)
            xs = jnp.concatenate([jnp.where(s > 0, hb_ref[...].astype(F32), 0.0), pin,
                                  jnp.where(s < n_tiles - 1, ha_ref[...].astype(F32), 0.0)], axis=0)
            tpos = s * TILE + lax.broadcasted_iota(jnp.int32, (TILE, 1), 0)
            pooled = []
            for g, w in enumerate(POOL_WINDOWS):
                sl = slice(g * GROUP_DIM, (g + 1) * GROUP_DIM)
                ws = _window_sum(xs[:, sl], w, False)[HALO:HALO + TILE]
                pooled.append((ws * _inv_count(tpos, w, seq) - pin[:, sl]).astype(BF16))
            return attn_t, ga, sga, silu_ga, pooled

        def a_group_maps(pooled):
            return jnp.concatenate([_dot(pooled[g], wp_ref[g].astype(BF16)) for g in range(len(POOL_WINDOWS))], axis=1)

        def a_project(stash, yp, attn_t, ga, sga, silu_ga, pooled):
            zb_ref, _, fa_ref, sa_ref, fp_ref, sp_ref, yp_ref, pooled_ref = stash
            ypool = yp * ps
            gp = gp_ref[...].astype(F32)
            sgp = _sigmoid(gp)
            silu_gp = gp * sgp
            z = jnp.concatenate([silu_ga * attn_t, silu_gp * ypool], axis=1).astype(BF16)
            y = _dot(z, wo_ref[...])
            zb_ref[...] = z
            fa_ref[...] = attn_t * (sga * (1.0 + ga * (1.0 - sga)))
            sa_ref[...] = silu_ga
            fp_ref[...] = ypool * (sgp * (1.0 + gp * (1.0 - sgp)))
            sp_ref[...] = silu_gp
            yp_ref[...] = yp
            pooled_ref[...] = jnp.concatenate(pooled, axis=1)
            return y

        def a_loss(stash, y):
            res = x_ref[...] + gate * y - t_ref[...]
            loss_ref[...] += jnp.sum(res * res) * (0.5 / D_MODEL)
            dxn = res * (1.0 / D_MODEL)
            g1_ref[...] = dxn
            dgate_ref[...] += jnp.sum(dxn * y, axis=0, keepdims=True)
            stash[1][...] = (gate * dxn).astype(BF16)

        def b_dz(stash):
            return _dot_nt(stash[1][...], wo_ref[...])

        def b_gwo(stash):
            gwo_ref[...] += _dot_tn(stash[0][...], stash[1][...])

        def b_vector(stash, slot, dz):
            _, _, fa_ref, sa_ref, fp_ref, sp_ref, yp_ref, _ = stash
            dbra = dz[:, 0:D_ATTN]
            dbrp = dz[:, D_ATTN:]
            dga_ref[...] = (dbra * fa_ref[...]).astype(BF16)
            dbuf[slot] = jnp.swapaxes((dbra * sa_ref[...]).reshape(per, n_blk, D_ATTN), 0, 1).reshape(TILE, D_ATTN)
            dgp_ref[...] = (dbrp * fp_ref[...]).astype(BF16)
            dyp_s = dbrp * sp_ref[...]
            dps_ref[...] += jnp.sum(dyp_s * yp_ref[...], axis=0, keepdims=True)
            return (dyp_s * ps).astype(BF16)

        def b_small(stash, dyp):
            pooled_ref = stash[7]
            for g in range(len(POOL_WINDOWS)):
                sl = slice(g * GROUP_DIM, (g + 1) * GROUP_DIM)
                gwp_ref[g] += _dot_tn(pooled_ref[:, sl], dyp[:, sl])
                dpl_ref[:, sl] = _dot_nt(dyp[:, sl], wp_ref[g].astype(BF16)).astype(BF16)

        def both(a_stash, b_stash, slot_a):
            dz = b_dz(b_stash)
            front = a_vector(slot_a)
            yp = a_group_maps(front[-1])
            b_gwo(b_stash)
            y = a_project(a_stash, yp, *front)
            dyp = b_vector(b_stash, 1 - slot_a, dz)
            a_loss(a_stash, y)
            b_small(b_stash, dyp)

        @pl.when(s == 0)
        def _():
            front = a_vector(0)
            a_loss(stash_even, a_project(stash_even, a_group_maps(front[-1]), *front))

        @pl.when((s > 0) & (s < n_tiles) & (a_slot == 0))
        def _():
            both(stash_even, stash_odd, 0)

        @pl.when((s > 0) & (s < n_tiles) & (a_slot == 1))
        def _():
            both(stash_odd, stash_even, 1)

        @pl.when(s == n_tiles)
        def _():
            last = (n_tiles - 1) % 2
            stash = stash_odd if last else stash_even
            dz = b_dz(stash)
            b_gwo(stash)
            b_small(stash, b_vector(stash, last, dz))
            gwob_ref[...] = gwo_ref[...].astype(BF16)

        @pl.when(s >= 1)
        def _():
            for cp in slab_copies(s - 1, b_slot, False):
                cp.start()

        @pl.when(s == n_tiles)
        def _():
            wait_all(b_slot, False)
            wait_all(a_slot, False)

    ta = lambda s: jnp.minimum(s, n_tiles - 1)
    tb = lambda s: jnp.maximum(s - 1, 0)
    tile = lambda w: pl.BlockSpec((TILE, w), lambda s: (ta(s), 0))
    tile_b = lambda w: pl.BlockSpec((TILE, w), lambda s: (tb(s), 0))
    ucol = lambda col: pl.BlockSpec((TILE, 512), lambda s: (ta(s), col))
    full = lambda shape: pl.BlockSpec(shape, lambda s: (0,) * len(shape))
    stash_shapes = [pltpu.VMEM((TILE, D_MODEL), BF16), pltpu.VMEM((TILE, D_MODEL), BF16)] + \
        [pltpu.VMEM((TILE, 512), F32)] * 5 + [pltpu.VMEM((TILE, D_POOL), BF16)]
    return pl.pallas_call(
        body, name="mix_fwd_bwd", grid=(n_tiles + 1,),
        out_shape=(jax.ShapeDtypeStruct((8, 128), F32), jax.ShapeDtypeStruct((seq, D_MODEL), F32),
                   jax.ShapeDtypeStruct((1, D_MODEL), F32), jax.ShapeDtypeStruct((seq, D_ATTN), F32),
                   jax.ShapeDtypeStruct((seq, D_ATTN), BF16), jax.ShapeDtypeStruct((seq, D_POOL), BF16),
                   jax.ShapeDtypeStruct((seq, D_POOL), BF16), jax.ShapeDtypeStruct((D_MODEL, D_MODEL), BF16),
                   jax.ShapeDtypeStruct((4, GROUP_DIM, GROUP_DIM), F32), jax.ShapeDtypeStruct((1, D_POOL), F32)),
        in_specs=[tile(D_MODEL), tile(D_MODEL), pl.BlockSpec(memory_space=pl.ANY), ucol(0), ucol(1),
                  pl.BlockSpec((HALO, 512), lambda s: (jnp.maximum(ta(s) * rows8 - 1, 0), 1)),
                  pl.BlockSpec((HALO, 512), lambda s: (jnp.minimum((ta(s) + 1) * rows8, last8), 1)),
                  ucol(2), full((8, D_MODEL)), full((8, D_MODEL)), full((4, GROUP_DIM, GROUP_DIM)), full((1, D_POOL)),
                  full((D_MODEL, D_MODEL))],
        out_specs=(full((8, 128)), tile(D_MODEL), full((1, D_MODEL)), pl.BlockSpec(memory_space=pl.ANY), tile_b(D_ATTN),
                   tile_b(D_POOL), tile_b(D_POOL), full((D_MODEL, D_MODEL)), full((4, GROUP_DIM, GROUP_DIM)),
                   full((1, D_POOL))),
        scratch_shapes=[pltpu.VMEM((2, TILE, D_ATTN), F32), pltpu.VMEM((2, TILE, D_ATTN), F32),
                        pltpu.VMEM((D_MODEL, D_MODEL), F32), *stash_shapes, *stash_shapes,
                        pltpu.SemaphoreType.DMA((2,)), pltpu.SemaphoreType.DMA((2,))],
        compiler_params=pltpu.CompilerParams(dimension_semantics=("arbitrary",), vmem_limit_bytes=VMEM_LIMIT),
    )(x, target, attn, u, u, u, u, u, modrows, bmodrows, w_pool, pool_scale, w_out)


def _inproj_bwd(x, ctx, modrows, bmodrows, norm_g, w_in_p, q_lora_g, w_uq_p, kv_lora_g, w_ukv, qng_p, kng_p, cos, slo, shi,
                u, dq, dk, dv, dga, dgp, dpl, g1):
    seq = x.shape[0]
    n_tiles = seq // TILE + 1
    rows8 = TILE // HALO
    last8 = seq // HALO - 1

    def body(*refs):
        du_even, du_odd, dh_even, dh_odd = refs[-4:]
        gwin_ref, gwuq_ref, gwukv_ref, dqlg_ref, dkvlg_ref, dqng_ref, dkng_ref, dng_ref, dmod_ref = refs[-13:-4]
        s = pl.program_id(0)

        @pl.when(s == 0)
        def _():
            for ref in (gwin_ref, gwuq_ref, gwukv_ref, dqlg_ref, dkvlg_ref, dqng_ref, dkng_ref, dng_ref, dmod_ref,
                        du_odd, dh_even):
                ref[...] = jnp.zeros_like(ref)

        @pl.when(lax.rem(s, 2) == 0)
        def _():
            stages(s, refs[:-4], du_even, du_odd, dh_odd, dh_even)

        @pl.when(lax.rem(s, 2) == 1)
        def _():
            stages(s, refs[:-4], du_odd, du_even, dh_even, dh_odd)

    def stages(s, refs, du_ref, du_prev, dh_fill, dh_prev):
        (xb_ref, xc_ref, ctx_ref, mod_ref, bmod_ref, ng_ref, win_ref, qlg_ref, wuq_ref, kvlg_ref, wukv_ref, qng_ref, kng_ref,
         cos_ref, slo_ref, shi_ref, cq_ref, ckv_ref, kr_ref, dq_ref, dk_ref, dv_ref, dga_ref, dgp_ref, dpl_ref,
         hb_ref, ha_ref, g1_ref,
         gx_ref, gwin_ref, gwuq_ref, gwukv_ref, dqlg_ref, dkvlg_ref, dqng_ref, dkng_ref, dng_ref, dmod_ref) = refs


        i = jnp.minimum(s, n_tiles - 1)
        valid = s < n_tiles
        is_lat = (s > 0) & valid
        cq = cq_ref[...]
        rq = lax.rsqrt(jnp.mean(cq * cq, axis=-1, keepdims=True) + NORM_EPS)
        qhat = cq * rq
        qnb = (qhat * qlg_ref[...]).astype(BF16)
        q = _dot_nt(qnb, wuq_ref[...])
        ckv = ckv_ref[...]
        rkv = lax.rsqrt(jnp.mean(ckv * ckv, axis=-1, keepdims=True) + NORM_EPS)
        kvhat = ckv * rkv
        kvnb = (kvhat * kvlg_ref[...]).astype(BF16)
        kv = _dot(kvnb, wukv_ref[...])

        _, _, _, h2, _ = _modulated_input(s <= 1, xb_ref, ctx_ref, mod_ref, bmod_ref, ng_ref)
        dub = du_prev[...]
        dh_fill[...] = _dot(dub, win_ref[...])
        gwin_ref[...] += _dot_tn(dub, h2.astype(BF16))

        ctx3 = s <= 2
        xt, r, xg, _, scale = _modulated_input(ctx3, xc_ref, ctx_ref, mod_ref, bmod_ref, ng_ref)
        dh = dh_prev[...]
        dshift = jnp.sum(dh, axis=0, keepdims=True)
        dscale = jnp.sum(dh * xg, axis=0, keepdims=True)
        zero = jnp.zeros_like(dshift)
        dmod_ref[0:1, :] += jnp.where(ctx3, zero, dshift)
        dmod_ref[1:2, :] += jnp.where(ctx3, zero, dscale)
        dmod_ref[2:3, :] += jnp.where(ctx3, dshift, zero)
        dmod_ref[3:4, :] += jnp.where(ctx3, dscale, zero)
        dxg = dh * (1.0 + scale)
        xr = xt * r
        dng_ref[...] += jnp.sum(dxg * xr, axis=0, keepdims=True)
        dxn = dxg * ng_ref[...]
        gx_ref[...] = g1_ref[...] + r * (dxn - xr * jnp.mean(dxn * xr, axis=-1, keepdims=True))

        cos_t, slo_t, shi_t = cos_ref[...], slo_ref[...], shi_ref[...]
        qg = qng_ref[...]
        dq_heads = []
        dqng = jnp.zeros((1, D_HEAD_PAD), F32)
        for hd in range(N_HEADS):
            qh = q[:, hd * D_HEAD_PAD:(hd + 1) * D_HEAD_PAD]
            rr = lax.rsqrt(jnp.sum(qh * qh, axis=-1, keepdims=True) * (1.0 / D_HEAD) + NORM_EPS)
            yq = qh * rr
            dpost = jnp.where(is_lat, dq_ref[hd] * ATTN_SCALE, 0.0)
            dyn = jnp.concatenate([dpost[:, 0:D_NOPE], _rope_t(dpost[:, D_NOPE:], cos_t, slo_t, shi_t)], axis=1)
            dqng = dqng + jnp.sum(dyn * yq, axis=0, keepdims=True)
            dyg = dyn * qg
            dq_heads.append(rr * (dyg - yq * (jnp.sum(dyg * yq, axis=-1, keepdims=True) * (1.0 / D_HEAD))))
        dqng_ref[...] += dqng
        dqf = jnp.concatenate(dq_heads, axis=1).astype(BF16)

        krz = kr_ref[...]
        kr_ss = jnp.sum(krz * krz, axis=-1, keepdims=True)
        kg = kng_ref[...]
        kg_n, kg_r = kg[:, 0:D_NOPE], kg[:, D_NOPE:]
        dkv_parts = []
        dkr = jnp.zeros((TILE, 128), F32)
        dkng_n = jnp.zeros((1, D_NOPE), F32)
        dkng_r = jnp.zeros((1, 128), F32)
        for hd in range(N_HEADS):
            kn = kv[:, hd * 256:hd * 256 + D_NOPE]
            rr = lax.rsqrt((jnp.sum(kn * kn, axis=-1, keepdims=True) + kr_ss) * (1.0 / D_HEAD) + NORM_EPS)
            yn, yr = kn * rr, krz * rr
            dk_h = jnp.where(valid, jnp.transpose(dk_ref[hd]) * LN_2, 0.0)
            dpn = dk_h[:, 0:D_NOPE]
            dpr = _rope_t(dk_h[:, D_NOPE:], cos_t, slo_t, shi_t)
            dkng_n = dkng_n + jnp.sum(dpn * yn, axis=0, keepdims=True)
            dkng_r = dkng_r + jnp.sum(dpr * yr, axis=0, keepdims=True)
            dyn, dyr = dpn * kg_n, dpr * kg_r
            proj = (jnp.sum(dyn * yn, axis=-1, keepdims=True) + jnp.sum(dyr * yr, axis=-1, keepdims=True)) * (1.0 / D_HEAD)
            dkv_parts.append(rr * (dyn - yn * proj))
            dkv_parts.append(jnp.where(valid, jnp.transpose(dv_ref[hd]), 0.0))
            dkr = dkr + rr * (dyr - yr * proj)
        dkng_ref[:, 0:D_NOPE] += dkng_n
        dkng_ref[:, D_NOPE:] += dkng_r
        dkvf = jnp.concatenate(dkv_parts, axis=1).astype(BF16)

        lat_t = jnp.maximum(i - 1, 0)
        dpl_t = dpl_ref[...].astype(F32)
        ds = jnp.concatenate([jnp.where(i > 1, hb_ref[...].astype(F32), 0.0), dpl_t,
                              jnp.where(i < n_tiles - 1, ha_ref[...].astype(F32), 0.0)], axis=0)
        tpos = lat_t * TILE - HALO + lax.broadcasted_iota(jnp.int32, (TILE + 2 * HALO, 1), 0)
        for g, w in enumerate(POOL_WINDOWS):
            sl = slice(g * GROUP_DIM, (g + 1) * GROUP_DIM)
            wsum = _window_sum(ds[:, sl] * _inv_count(tpos, w, seq), w, True)[HALO:HALO + TILE]
            du_ref[:, O_PIN + g * GROUP_DIM:O_PIN + (g + 1) * GROUP_DIM] = jnp.where(
                is_lat, wsum - dpl_t[:, sl], 0.0).astype(BF16)

        du_ref[:, O_GA:O_GA + D_ATTN] = jnp.where(is_lat, dga_ref[...], jnp.zeros((), BF16))
        du_ref[:, O_GP:O_GP + D_POOL] = jnp.where(is_lat, dgp_ref[...], jnp.zeros((), BF16))
        du_ref[:, O_KR:U_PAD] = dkr.astype(BF16)

        gwuq_ref[...] += _dot_tn(dqf, qnb)
        dqn = _dot(dqf, wuq_ref[...])
        gwukv_ref[...] += _dot_tn(dkvf, kvnb)
        dkvn = _dot_nt(dkvf, wukv_ref[...])
        dqlg_ref[...] += jnp.sum(dqn * qhat, axis=0, keepdims=True)
        dqhat = dqn * qlg_ref[...]
        dcq = rq * (dqhat - qhat * jnp.mean(dqhat * qhat, axis=-1, keepdims=True))
        dkvlg_ref[...] += jnp.sum(dkvn * kvhat, axis=0, keepdims=True)
        dkvhat = dkvn * kvlg_ref[...]
        dckv = rkv * (dkvhat - kvhat * jnp.mean(dkvhat * kvhat, axis=-1, keepdims=True))
        du_ref[:, O_CQ:O_CQ + R_Q] = dcq.astype(BF16)
        du_ref[:, O_CKV:O_CKV + R_KV] = dckv.astype(BF16)

    t1 = lambda s: jnp.minimum(s, n_tiles - 1)
    t2 = lambda s: jnp.clip(s - 1, 0, n_tiles - 1)
    t3 = lambda s: jnp.clip(s - 2, 0, n_tiles - 1)
    latent = lambda t: jnp.maximum(t - 1, 0)
    lat = lambda s: (latent(t1(s)), 0)
    lat3 = lambda s: (latent(t3(s)), 0)
    full = lambda shape: pl.BlockSpec(shape, lambda s: (0,) * len(shape))
    rope_spec = pl.BlockSpec((TILE, 128), lambda s: (t1(s), 0))
    return pl.pallas_call(
        body, name="inproj_bwd", grid=(n_tiles + 2,),
        out_shape=(jax.ShapeDtypeStruct((seq, D_MODEL), F32), jax.ShapeDtypeStruct((U_PAD, D_MODEL), F32),
                   jax.ShapeDtypeStruct((N_HEADS * D_HEAD_PAD, R_Q), F32), jax.ShapeDtypeStruct((N_HEADS * 256, R_KV), F32),
                   jax.ShapeDtypeStruct((1, R_Q), F32), jax.ShapeDtypeStruct((1, R_KV), F32),
                   jax.ShapeDtypeStruct((1, D_HEAD_PAD), F32), jax.ShapeDtypeStruct((1, D_HEAD_PAD), F32),
                   jax.ShapeDtypeStruct((1, D_MODEL), F32), jax.ShapeDtypeStruct((8, D_MODEL), F32)),
        in_specs=[pl.BlockSpec((TILE, D_MODEL), lambda s: (latent(t2(s)), 0)), pl.BlockSpec((TILE, D_MODEL), lat3),
                  full((TILE, D_MODEL)), full((8, D_MODEL)), full((8, D_MODEL)),
                  full((1, D_MODEL)), full((U_PAD, D_MODEL)), full((1, R_Q)), full((N_HEADS * D_HEAD_PAD, R_Q)),
                  full((1, R_KV)), full((R_KV, N_HEADS * 256)), full((1, D_HEAD_PAD)), full((1, D_HEAD_PAD)),
                  rope_spec, rope_spec, rope_spec,
                  pl.BlockSpec((TILE, R_Q), lambda s: (t1(s), (O_CQ - N_GATES) // R_Q)),
                  pl.BlockSpec((TILE, R_KV), lambda s: (t1(s), (O_CKV - N_GATES) // R_KV)),
                  pl.BlockSpec((TILE, 128), lambda s: (t1(s), (O_KR - N_GATES) // 128)),
                  pl.BlockSpec((N_HEADS, TILE, D_HEAD_PAD), lambda s: (0, latent(t1(s)), 0)),
                  pl.BlockSpec((N_HEADS, D_HEAD_PAD, TILE), lambda s: (0, 0, t1(s))),
                  pl.BlockSpec((N_HEADS, D_V, TILE), lambda s: (0, 0, t1(s))),
                  pl.BlockSpec((TILE, D_ATTN), lat), pl.BlockSpec((TILE, D_POOL), lat), pl.BlockSpec((TILE, D_POOL), lat),
                  pl.BlockSpec((HALO, D_POOL), lambda s: (jnp.maximum((t1(s) - 1) * rows8 - 1, 0), 0)),
                  pl.BlockSpec((HALO, D_POOL), lambda s: (jnp.minimum(jnp.maximum(t1(s), 1) * rows8, last8), 0)),
                  pl.BlockSpec((TILE, D_MODEL), lat3)],
        out_specs=(pl.BlockSpec((TILE, D_MODEL), lat3), full((U_PAD, D_MODEL)), full((N_HEADS * D_HEAD_PAD, R_Q)),
                   full((N_HEADS * 256, R_KV)), full((1, R_Q)), full((1, R_KV)), full((1, D_HEAD_PAD)),
                   full((1, D_HEAD_PAD)), full((1, D_MODEL)), full((8, D_MODEL))),
        scratch_shapes=[pltpu.VMEM((TILE, U_PAD), BF16), pltpu.VMEM((TILE, U_PAD), BF16),
                        pltpu.VMEM((TILE, D_MODEL), F32), pltpu.VMEM((TILE, D_MODEL), F32)],
        compiler_params=pltpu.CompilerParams(dimension_semantics=("arbitrary",), vmem_limit_bytes=VMEM_LIMIT),
    )(x, x, ctx, modrows, bmodrows, norm_g, w_in_p, q_lora_g, w_uq_p, kv_lora_g, w_ukv, qng_p, kng_p, cos, slo, shi,
      u, u, u, dq, dk, dv, dga, dgp, dpl, dpl, dpl, g1)


SMALL_LAYOUT = ((0, D_MODEL), (1, R_Q), (2, R_KV), (3, D_HEAD_PAD), (4, D_HEAD_PAD), (5, D_POOL))
ROW_DMOD_B, ROW_DMOD_C, ROW_LOSS = 6, 9, 12


def _epilogue(parts, landed, smalls, dmod4, dgate, loss_acc, w_mod_l):
    n_a, n_l, n_s = len(parts), len(landed), len(smalls)
    n_mod = w_mod_l.shape[1]
    blk = [p.shape[1:] for p in parts]
    small_out = [D_MODEL, R_Q, R_KV, D_HEAD, D_HEAD, D_POOL]

    def body(*refs):
        part_refs = refs[0:n_a]
        land_refs = refs[n_a:n_a + n_l]
        small_in = refs[n_a + n_l:n_a + n_l + n_s]
        dmod4_ref, dgate_ref, loss_ref, wmod_ref = refs[n_a + n_l + n_s:n_a + n_l + n_s + 4]
        outs = refs[n_a + n_l + n_s + 4:]
        red_refs = outs[0:n_a]
        landsum_refs = outs[n_a:n_a + n_l]
        ccg_ref = outs[n_a + n_l]
        sum_refs = outs[n_a + n_l + 1:n_a + n_l + 1 + n_s]
        gbmod_ref, dmy_ref, lossout_ref = outs[n_a + n_l + 1 + n_s:n_a + n_l + 4 + n_s]
        scr = outs[n_a + n_l + 4 + n_s:]
        recv1, own1, sum1b, recv2 = scr[0:n_a], scr[n_a:2 * n_a], scr[2 * n_a:3 * n_a], scr[3 * n_a:4 * n_a]
        small_ref, smallg_ref, tot_ref, cc_ref = scr[4 * n_a:4 * n_a + 4]
        ssem1, rsem1, lsem, ssem2, rsem2, ssem_s, rsem_s, ssem_cc, rsem_cc = scr[4 * n_a + 4:]
        x, y, c = _coords()
        me = (x, y, c)
        sibling = (x, y, 1 - c)

        stage1, own_copies = [], []
        for a in range(n_a):
            for b in range(4):
                dev = 4 * (b >> 1) + 2 * (b & 1)
                stage1.append(pltpu.make_async_remote_copy(
                    src_ref=part_refs[a].at[dev + (1 - c)], dst_ref=recv1[a].at[b],
                    send_sem=ssem1.at[a, b], recv_sem=rsem1.at[a, b], device_id=sibling, device_id_type=MESH))
                own_copies.append(pltpu.make_async_copy(part_refs[a].at[dev + c], own1[a].at[b], lsem.at[a, b]))
                stage1[-1].start()
                own_copies[-1].start()

        small_ref[...] = jnp.zeros_like(small_ref)
        for ref, (row, width) in zip(small_in, SMALL_LAYOUT):
            small_ref[row:row + 1, 0:width] = ref[...]
        small_ref[ROW_DMOD_B:ROW_DMOD_B + 2, :] = dmod4_ref[0:2, :]
        small_ref[ROW_DMOD_B + 2:ROW_DMOD_B + 3, :] = dgate_ref[...]
        small_ref[ROW_DMOD_C:ROW_DMOD_C + 2, :] = dmod4_ref[2:4, :]
        small_ref[ROW_LOSS:ROW_LOSS + 1, 0:128] = loss_ref[0:1, :]
        smallg_ref[_lin(me)] = small_ref[...]
        s_recv, s_send = _gather_to_all(small_ref, smallg_ref, ssem_s, rsem_s)
        s_recv()
        tot = smallg_ref[0]
        for d in range(1, N_DEV):
            tot = tot + smallg_ref[d]
        tot_ref[...] = tot
        for ref, (row, _), width in zip(sum_refs, SMALL_LAYOUT, small_out):
            ref[...] = tot_ref[row:row + 1, 0:width]
        lossout_ref[...] = tot_ref[8:16, 0:128]
        lin = _lin(me)

        def my_columns(three_rows):
            v = jnp.concatenate(three_rows, axis=1)
            out = jnp.zeros((1, n_mod), F32)
            for d in range(N_DEV):
                out = out + jnp.where(lin == d, v[:, d * n_mod:(d + 1) * n_mod], 0.0)
            return out

        rows3 = lambda ref, r0: [ref[r0 + t:r0 + t + 1, :] for t in range(3)]
        dmodc = rows3(tot_ref, ROW_DMOD_C)
        gbmod_ref[...] = jnp.concatenate(rows3(tot_ref, ROW_DMOD_B), axis=1) + jnp.concatenate(dmodc, axis=1)
        dmy_ref[...] = jnp.zeros_like(dmy_ref)
        for b in range(N_DEV):
            dmy_ref[b:b + 1, :] = my_columns(rows3(smallg_ref.at[b], ROW_DMOD_B))
        dmy = my_columns(dmodc)
        dmy_ref[N_DEV:N_DEV + 1, :] = dmy
        part = lax.dot_general(jnp.broadcast_to(dmy, (8, n_mod)), wmod_ref[...], (((1,), (1,)), ((), ())),
                               precision=HIGHEST, preferred_element_type=F32)

        cc_ref[...] = part
        ccg_ref[_lin(me)] = part
        cc_recv, cc_send = _gather_to_all(cc_ref, ccg_ref, ssem_cc, rsem_cc)

        stage2 = []
        for a in range(n_a):
            for b in range(4):
                stage1[4 * a + b].wait_recv()
                own_copies[4 * a + b].wait()
                sum1b[a][b] = (own1[a][b] + recv1[a][b]).astype(BF16)
            for k in (1, 2, 3):
                tx, ty = _flip(x, (k >> 1) & 1), _flip(y, k & 1)
                stage2.append(pltpu.make_async_remote_copy(
                    src_ref=sum1b[a].at[2 * tx + ty], dst_ref=recv2[a].at[k - 1], send_sem=ssem2.at[a, k - 1],
                    recv_sem=rsem2.at[a, k - 1], device_id=(tx, ty, c), device_id_type=MESH))
                stage2[-1].start()
        for a in range(n_a):
            own = own1[a][2 * x + y] + recv1[a][2 * x + y]
            for k in (1, 2, 3):
                stage2[3 * a + k - 1].wait_recv()
                own = own + recv2[a][k - 1].astype(F32)
            red_refs[a][...] = own

        for ref, out in zip(land_refs, landsum_refs):
            acc = ref[0].astype(F32)
            for d in range(1, N_DEV):
                acc = acc + ref[d].astype(F32)
            out[...] = acc
        cc_recv()
        for cp in stage1 + stage2:
            cp.wait_send()
        s_send()
        cc_send()

    vm = pl.BlockSpec(memory_space=pltpu.VMEM)
    sds = jax.ShapeDtypeStruct
    return pl.pallas_call(
        body, name="epilogue_reduce",
        out_shape=(*[sds(s, F32) for s in blk], *[sds(a.shape[1:], F32) for a in landed], sds((N_DEV, 8, D_MODEL), F32),
                   *[sds((1, w), F32) for w in small_out], sds((1, 3 * D_MODEL), F32), sds((16, n_mod), F32),
                   sds((8, 128), F32)),
        in_specs=[pl.BlockSpec(memory_space=pl.ANY)] * n_a + [vm] * (n_l + n_s + 4),
        out_specs=tuple([vm] * (n_a + n_l + n_s + 4)),
        scratch_shapes=[*[pltpu.VMEM((4,) + s, F32) for s in blk], *[pltpu.VMEM((4,) + s, F32) for s in blk],
                        *[pltpu.VMEM((4,) + s, BF16) for s in blk], *[pltpu.VMEM((3,) + s, BF16) for s in blk],
                        pltpu.VMEM((SMALL_ROWS, D_MODEL), F32), pltpu.VMEM((N_DEV, SMALL_ROWS, D_MODEL), F32),
                        pltpu.VMEM((SMALL_ROWS, D_MODEL), F32), pltpu.VMEM((8, D_MODEL), F32),
                        pltpu.SemaphoreType.DMA((n_a, 4)), pltpu.SemaphoreType.DMA((n_a, 4)), pltpu.SemaphoreType.DMA((n_a, 4)),
                        pltpu.SemaphoreType.DMA((n_a, 3)), pltpu.SemaphoreType.DMA((n_a, 3)),
                        pltpu.SemaphoreType.DMA((7,)), pltpu.SemaphoreType.DMA((7,)),
                        pltpu.SemaphoreType.DMA((7,)), pltpu.SemaphoreType.DMA((7,))],
        compiler_params=pltpu.CompilerParams(vmem_limit_bytes=VMEM_LIMIT),
    )(*parts, *landed, *smalls, dmod4, dgate, loss_acc, w_mod_l)


def _adamw(weights, grads, ms, vs, c_all_t, dmod_my, p2g):
    n = len(weights)

    def body(*refs):
        w_refs = refs[0:n]
        g_in = refs[n:2 * n - 2]
        m_refs = refs[2 * n - 2:3 * n - 2]
        v_refs = refs[3 * n - 2:4 * n - 2]
        cat_ref, dmod_ref, p2g_ref = refs[4 * n - 2:4 * n + 1]
        outs = refs[4 * n + 1:]
        gcc_ref, gwm_ref = outs[0], outs[1]
        d_refs, nm_refs, nv_refs = outs[2:2 + n], outs[2 + n:2 + 2 * n], outs[2 + 2 * n:2 + 3 * n]

        part = p2g_ref[0, 0:1, :]
        for d in range(1, N_DEV):
            part = part + p2g_ref[d, 0:1, :]
        cc = w_refs[0][...]
        sg = _sigmoid(cc)
        gcc_ref[...] = part * (sg * (1.0 + cc * (1.0 - sg)))
        cat = cat_ref[...]
        gwm_ref[...] = lax.dot_general(cat * _sigmoid(cat), dmod_ref[...], (((1,), (0,)), ((), ())), precision=HIGHEST,
                                       preferred_element_type=F32)
        for idx in range(n):
            g = (gcc_ref, gwm_ref)[idx][...] if idx < 2 else g_in[idx - 2][...]
            m = ADAM_B1 * m_refs[idx][...] + (1.0 - ADAM_B1) * g
            v = ADAM_B2 * v_refs[idx][...] + (1.0 - ADAM_B2) * (g * g)
            m_hat = m / (1.0 - ADAM_B1 ** ADAM_STEP)
            v_hat = v / (1.0 - ADAM_B2 ** ADAM_STEP)
            d_refs[idx][...] = -ADAM_LR * (m_hat / (jnp.sqrt(v_hat) + ADAM_EPS) + ADAM_WD * w_refs[idx][...])
            nm_refs[idx][...] = m
            nv_refs[idx][...] = v

    vm = pl.BlockSpec(memory_space=pltpu.VMEM)
    shapes = [jax.ShapeDtypeStruct(w.shape, F32) for w in weights]
    args = list(weights) + list(grads[2:]) + list(ms) + list(vs) + [c_all_t, dmod_my, p2g]
    out_shape = tuple(shapes[0:2] + shapes * 3)
    return pl.pallas_call(
        body, name="adamw_update", out_shape=out_shape, in_specs=[vm] * len(args), out_specs=tuple([vm] * len(out_shape)),
        compiler_params=pltpu.CompilerParams(vmem_limit_bytes=VMEM_LIMIT),
    )(*args)


def _rope_tables(seq):
    rows = seq // GRID_W
    row = np.repeat(np.arange(rows, dtype=np.float32), GRID_W)
    col = np.tile(np.arange(GRID_W, dtype=np.float32), rows)
    n_freq = D_ROPE // 4
    inv = (np.float32(ROPE_BASE) ** (-np.arange(n_freq, dtype=np.float32) / np.float32(n_freq))).astype(np.float32)
    ang_r = (row[:, None] * inv).astype(np.float32)
    ang_c = (col[:, None] * inv).astype(np.float32)
    ang = np.concatenate([ang_r, ang_r, ang_c, ang_c], axis=-1)
    cos, sin = np.cos(ang).astype(np.float32), np.sin(ang).astype(np.float32)
    low = (np.arange(D_ROPE) % 32) < 16

    def table(t, fill):
        out = np.full((TILE + seq, 128), fill, np.float32)
        out[TILE:, 0:D_ROPE] = t
        return jnp.asarray(out)

    return table(cos, 1.0), table(np.where(low, -sin, 0.0), 0.0), table(np.where(low, 0.0, sin), 0.0)


def kernel(x, c, ctx, c_ctx, w_mod, b_mod, norm_g, w_in, q_lora_g, w_uq, kv_lora_g, w_ukv, q_norm_g, k_norm_g, w_pool, pool_scale, w_out, loss_target, m_c_ctx, m_w_mod, m_b_mod, m_norm_g, m_w_in, m_q_lora_g, m_w_uq, m_kv_lora_g, m_w_ukv, m_q_norm_g, m_k_norm_g, m_w_pool, m_pool_scale, m_w_out, v_c_ctx, v_w_mod, v_b_mod, v_norm_g, v_w_in, v_q_lora_g, v_w_uq, v_kv_lora_g, v_w_ukv, v_q_norm_g, v_k_norm_g, v_w_pool, v_pool_scale, v_w_out):
    seq = x.shape[1]
    assert ctx.shape[1] == TILE and seq % TILE == 0 and seq % GRID_W == 0
    ix, iy, ic = lax.axis_index("x"), lax.axis_index("y"), lax.axis_index("c")
    me = 4 * ix + 2 * iy + ic
    x2, ctx2, tgt2 = x[0], ctx[0], loss_target[0]
    w_mod_l, w_ukv_l, w_out_l = w_mod[0], w_ukv[0], w_out[0]
    c_ctx_row = c_ctx.reshape(1, D_MODEL)

    tr = lambda a: jnp.transpose(a[0])
    w_in_tl, w_uq_tl = tr(w_in), tr(w_uq)
    cg, modg, wg_in, wg_uq, wg_ukv = _prologue(
        jnp.broadcast_to(c, (8, D_MODEL)), jnp.broadcast_to(c_ctx_row, (8, D_MODEL)), w_mod_l,
        [w_in_tl, w_uq_tl, w_ukv_l])
    mod_all = jnp.transpose(modg, (1, 0, 2)).reshape(16, 3 * D_MODEL)
    mod_b = lax.dynamic_slice_in_dim(mod_all, me, 1, axis=0).reshape(3, D_MODEL)
    mod_c = mod_all[8].reshape(3, D_MODEL)
    modrows = jnp.concatenate([mod_b, mod_c[0:2], jnp.zeros((3, D_MODEL), F32)], axis=0)
    b3 = b_mod.reshape(3, D_MODEL)
    bmodrows = jnp.concatenate([b3, b3[0:2], jnp.zeros((3, D_MODEL), F32)], axis=0)

    w_in_t = wg_in.reshape(D_IN_PROJ, D_MODEL)
    w_in_p = jnp.concatenate([w_in_t[448:], w_in_t[0:384], w_in_t[384:448],
                              jnp.zeros((U_PAD - D_IN_PROJ, D_MODEL), BF16)], axis=0)
    w_uq_p = jnp.concatenate([wg_uq.reshape(N_HEADS, D_HEAD, R_Q), jnp.zeros((N_HEADS, D_HEAD_PAD - D_HEAD, R_Q), BF16)],
                             axis=1).reshape(N_HEADS * D_HEAD_PAD, R_Q)
    w_ukv_f = jnp.transpose(wg_ukv, (1, 0, 2)).reshape(R_KV, N_HEADS * 256)
    qng_p = jnp.concatenate([q_norm_g, jnp.zeros((1, D_HEAD_PAD - D_HEAD), F32)], axis=1)
    kng_p = jnp.concatenate([k_norm_g, jnp.zeros((1, D_HEAD_PAD - D_HEAD), F32)], axis=1)
    cos, slo, shi = _rope_tables(seq)

    u_gates, u_mla, q, k, v = _inproj_fwd(x2, ctx2, modrows, bmodrows, norm_g, w_in_p, q_lora_g, w_uq_p, kv_lora_g, w_ukv_f,
                             qng_p, kng_p, cos, slo, shi)
    attn, lse, wg_out = _attn_fwd(q, k, v, min(2048, seq), w_out_l.astype(BF16))
    (loss_acc, g1, dgate, d_attn, dga, dgp, dpl, gwo_b, gwp, dps) = _mix(
        x2, tgt2, attn, u_gates, modrows, bmodrows, w_pool[0], pool_scale, wg_out.reshape(D_MODEL, D_MODEL))

    blocks = lambda a: a.reshape((N_DEV, a.shape[0] // N_DEV) + a.shape[1:])
    dq, dk, dv, land_wo, land_wp = _attn_bwd(q, k, v, attn, lse, d_attn, min(1024, seq), blocks(gwo_b),
                                             gwp.reshape(4 * GROUP_DIM, GROUP_DIM))
    (grad_x, gwin_p, gwuq_p, gwukv_t, dqlg, dkvlg, dqng, dkng, dng, dmod4) = _inproj_bwd(
        x2, ctx2, modrows, bmodrows, norm_g, w_in_p, q_lora_g, w_uq_p, kv_lora_g, w_ukv_f, qng_p, kng_p, cos, slo, shi,
        u_mla, dq, dk, dv, dga, dgp, dpl, g1)

    gwin_t = jnp.concatenate([gwin_p[O_CQ:O_KR], gwin_p[O_KR:D_IN_PROJ], gwin_p[0:O_CQ]], axis=0)
    gwuq_t = gwuq_p.reshape(N_HEADS, D_HEAD_PAD, R_Q)[:, 0:D_HEAD].reshape(N_HEADS * D_HEAD, R_Q)
    parts = [blocks(gwin_t), blocks(gwuq_t), blocks(gwukv_t)]
    (r_win, r_wuq, r_wukv, r_wout, g_w_pool, ccg, g_ng, g_qlg, g_kvlg, g_qng, g_kng, g_ps, g_bmod, dmod_my,
     loss_rows) = _epilogue(parts, [land_wo, land_wp], [dng, dqlg, dkvlg, dqng, dkng, dps], dmod4, dgate, loss_acc, w_mod_l)

    g_w_ukv = jnp.transpose(r_wukv)
    c_rows = cg[:, 0, :]
    c_all_t = jnp.transpose(jnp.concatenate([c_rows, c_ctx_row, jnp.zeros((16 - N_DEV - 1, D_MODEL), F32)], axis=0))

    weights = [c_ctx_row, w_mod_l, b_mod, norm_g, w_in_tl, q_lora_g, w_uq_tl, kv_lora_g, w_ukv_l, q_norm_g, k_norm_g,
               w_pool.reshape(4 * GROUP_DIM, GROUP_DIM), pool_scale, w_out_l]
    grads = [None, None, g_bmod, g_ng, r_win, g_qlg, r_wuq, g_kvlg, g_w_ukv, g_qng, g_kng, g_w_pool, g_ps, r_wout]
    ms = [m_c_ctx.reshape(1, D_MODEL), m_w_mod[0], m_b_mod, m_norm_g, tr(m_w_in), m_q_lora_g, tr(m_w_uq), m_kv_lora_g,
          m_w_ukv[0], m_q_norm_g, m_k_norm_g, m_w_pool.reshape(4 * GROUP_DIM, GROUP_DIM), m_pool_scale, m_w_out[0]]
    vs = [v_c_ctx.reshape(1, D_MODEL), v_w_mod[0], v_b_mod, v_norm_g, tr(v_w_in), v_q_lora_g, tr(v_w_uq), v_kv_lora_g,
          v_w_ukv[0], v_q_norm_g, v_k_norm_g, v_w_pool.reshape(4 * GROUP_DIM, GROUP_DIM), v_pool_scale, v_w_out[0]]
    outs = _adamw(weights, grads, ms, vs, c_all_t, dmod_my, ccg)
    grads[0], grads[1] = outs[0], outs[1]
    n = len(weights)
    deltas, new_m, new_v = outs[2:2 + n], outs[2 + n:2 + 2 * n], outs[2 + 2 * n:2 + 3 * n]

    loss = loss_rows[ROW_LOSS - 8, 0]
    final = [c_ctx.shape, w_mod.shape, b_mod.shape, norm_g.shape, w_in.shape, q_lora_g.shape, w_uq.shape, kv_lora_g.shape,
             w_ukv.shape, q_norm_g.shape, k_norm_g.shape, w_pool.shape, pool_scale.shape, w_out.shape]
    transposed = (4, 6)

    def shaped(arrs):
        return [(jnp.transpose(a) if i in transposed else a).reshape(s) for i, (a, s) in enumerate(zip(arrs, final))]

    return (loss, grad_x[None], *shaped(grads), *shaped(deltas), *shaped(new_m), *shaped(new_v))
```

```python
import numpy as np

import jax
import jax.numpy as jnp
from jax import lax
from jax.experimental import pallas as pl
from jax.experimental.pallas import tpu as pltpu

F32 = jnp.float32
BF16 = jnp.bfloat16
MESH = pl.DeviceIdType.MESH
HIGHEST = lax.Precision.HIGHEST

D_MODEL = 1024
N_HEADS = 4
D_NOPE = 128
D_ROPE = 64
D_HEAD = D_NOPE + D_ROPE
D_HEAD_PAD = 256
D_V = 128
R_Q = 256
R_KV = 128
D_ATTN = 512
D_POOL = 512
POOL_WINDOWS = (2, 4, 8, 16)
GROUP_DIM = 128
GRID_W = 64
ROPE_BASE = 10000.0
NORM_EPS = 1e-6
ATTN_SCALE = D_HEAD ** -0.5
LOG2_E = 1.4426950408889634
LN_2 = 0.6931471805599453
ATTN_ROWS = 256
D_IN_PROJ = 1984
U_PAD = 2048
O_GA, O_PIN, O_GP, O_CQ, O_CKV, O_KR = 0, 512, 1024, 1536, 1792, 1920
TILE = 256
Q_BLOCK = 128
HALO = 16
N_GATES = 1536
N_DEV = 8
VMEM_LIMIT = 56 * 1024 * 1024

ADAM_LR = 0.001
ADAM_B1 = 0.9
ADAM_B2 = 0.999
ADAM_EPS = 1e-08
ADAM_WD = 0.01
ADAM_STEP = 10

SMALL_ROWS = 16


def _dot(a, b):
    return lax.dot_general(a, b, (((1,), (0,)), ((), ())), preferred_element_type=F32)


def _dot_nt(a, b):
    return lax.dot_general(a, b, (((1,), (1,)), ((), ())), preferred_element_type=F32)


def _dot_tn(a, b):
    return lax.dot_general(a, b, (((0,), (0,)), ((), ())), preferred_element_type=F32)


def _sigmoid(x):
    return 1.0 / (1.0 + jnp.exp(-x))


def _rope(p, cos, slo, shi):
    return p * cos + pltpu.roll(p, 112, 1) * slo + pltpu.roll(p, 16, 1) * shi


def _rope_t(d, cos, slo, shi):
    return d * cos + pltpu.roll(d * slo, 16, 1) + pltpu.roll(d * shi, 112, 1)


def _shift_rows(a, k):
    n = a.shape[0]
    return pltpu.roll(a, (-k) % n, 0)


def _window_sum(x, w, transposed):
    s = (x + _shift_rows(x, 1)) if transposed else (_shift_rows(x, -1) + x)
    step = 1
    while 2 * step < w:
        s = _shift_rows(s, -step) + _shift_rows(s, step)
        step *= 2
    return s


def _inv_count(tpos, w, seq):
    lo = jnp.maximum(tpos - w // 2, 0)
    hi = jnp.minimum(tpos - w // 2 + w, seq)
    return 1.0 / jnp.maximum(hi - lo, 1).astype(F32)


def _coords():
    return lax.axis_index("x"), lax.axis_index("y"), lax.axis_index("c")


def _flip(v, bit):
    return (1 - v) if bit else v


def _peer(k):
    x, y, c = _coords()
    return (_flip(x, (k >> 2) & 1), _flip(y, (k >> 1) & 1), _flip(c, k & 1))


def _lin(p):
    return 4 * p[0] + 2 * p[1] + p[2]


def _gather_to_all(src_ref, slots_ref, send_sems, recv_sems):
    me = _coords()
    copies = []
    for k in range(1, N_DEV):
        cp = pltpu.make_async_remote_copy(
            src_ref=src_ref, dst_ref=slots_ref.at[_lin(me)], send_sem=send_sems.at[k - 1], recv_sem=recv_sems.at[k - 1],
            device_id=_peer(k), device_id_type=MESH)
        cp.start()
        copies.append(cp)

    def wait_recv():
        for k in range(1, N_DEV):
            pltpu.make_async_remote_copy(
                src_ref=src_ref, dst_ref=slots_ref.at[_lin(_peer(k))], send_sem=send_sems.at[k - 1],
                recv_sem=recv_sems.at[k - 1], device_id=_peer(k), device_id_type=MESH).wait_recv()

    def wait_send():
        for cp in copies:
            cp.wait_send()

    return wait_recv, wait_send


def _prologue(c8, cctx8, w_mod_l, shards):
    n_mod = w_mod_l.shape[1]
    n_w = len(shards)

    def body(c8_ref, cctx_ref, wmod_ref, *refs):
        w_refs = refs[0:n_w]
        cg_ref, modg_ref = refs[n_w], refs[n_w + 1]
        wg_refs = refs[n_w + 2:2 * n_w + 2]
        modblk_ref = refs[2 * n_w + 2]
        wb_refs = refs[2 * n_w + 3:3 * n_w + 3]
        ssem_w, rsem_w, ssem_c, rsem_c, ssem_m, rsem_m = refs[3 * n_w + 3:]
        x, y, c = _coords()
        me = (x, y, c)
        sibling = (x, y, 1 - c)
        chips = [(1 - x, y), (x, 1 - y), (1 - x, 1 - y)]

        def wcopies(k, block, to, own=False):
            out = []
            for a in range(n_w):
                slot = wg_refs[a].at[_lin(block)]
                out.append(pltpu.make_async_remote_copy(
                    src_ref=wb_refs[a] if own else slot, dst_ref=slot, send_sem=ssem_w.at[a, k], recv_sem=rsem_w.at[a, k],
                    device_id=to, device_id_type=MESH))
            return out

        for a in range(n_w):
            wb_refs[a][...] = w_refs[a][...].astype(BF16)
        first = wcopies(0, me, sibling, own=True)
        for j, chip in enumerate(chips):
            first += wcopies(1 + j, me, (*chip, c), own=True)
        for cp in first:
            cp.start()
        for a in range(n_w):
            wg_refs[a][_lin(me)] = wb_refs[a][...]

        cg_ref[_lin(me)] = c8_ref[...]
        c_recv, c_send = _gather_to_all(c8_ref, cg_ref, ssem_c, rsem_c)
        c_recv()
        row = lax.broadcasted_iota(jnp.int32, (8, D_MODEL), 0)
        c_all = jnp.zeros((8, D_MODEL), F32)
        for d in range(N_DEV):
            c_all = c_all + jnp.where(row == d, cg_ref[d], 0.0)
        cc = cctx_ref[...]
        a = jnp.concatenate([c_all * _sigmoid(c_all), cc * _sigmoid(cc)], axis=0)
        modblk_ref[...] = lax.dot_general(a, wmod_ref[...], (((1,), (0,)), ((), ())), precision=HIGHEST,
                                          preferred_element_type=F32)
        modg_ref[_lin(me)] = modblk_ref[...]
        m_recv, m_send = _gather_to_all(modblk_ref, modg_ref, ssem_m, rsem_m)
        m_recv()

        passed = []
        for j, chip in enumerate(chips):
            for cp in wcopies(1 + j, (*chip, c), me):
                cp.wait_recv()
            fwd = wcopies(4 + j, (*chip, c), sibling)
            for cp in fwd:
                cp.start()
            passed += fwd
        for cp in wcopies(0, sibling, me):
            cp.wait_recv()
        for j, chip in enumerate(chips):
            for cp in wcopies(4 + j, (*chip, 1 - c), me):
                cp.wait_recv()
        for cp in first + passed:
            cp.wait_send()
        c_send()
        m_send()

    vm = pl.BlockSpec(memory_space=pltpu.VMEM)
    return pl.pallas_call(
        body, name="prologue_gather",
        out_shape=(jax.ShapeDtypeStruct((N_DEV, 8, D_MODEL), F32), jax.ShapeDtypeStruct((N_DEV, 16, n_mod), F32),
                   *[jax.ShapeDtypeStruct((N_DEV,) + s.shape, BF16) for s in shards]),
        in_specs=[vm] * (3 + n_w), out_specs=tuple([vm] * (2 + n_w)),
        scratch_shapes=[pltpu.VMEM((16, n_mod), F32), *[pltpu.VMEM(s.shape, BF16) for s in shards],
                        pltpu.SemaphoreType.DMA((n_w, 7)), pltpu.SemaphoreType.DMA((n_w, 7)),
                        pltpu.SemaphoreType.DMA((7,)), pltpu.SemaphoreType.DMA((7,)),
                        pltpu.SemaphoreType.DMA((7,)), pltpu.SemaphoreType.DMA((7,))],
        compiler_params=pltpu.CompilerParams(vmem_limit_bytes=VMEM_LIMIT),
    )(c8, cctx8, w_mod_l, *shards)


def _modulated_input(is_ctx, x_ref, ctx_ref, mod_ref, bmod_ref, ng_ref):
    xt = jnp.where(is_ctx, ctx_ref[...], x_ref[...])
    shift = jnp.where(is_ctx, mod_ref[3:4, :] + bmod_ref[3:4, :], mod_ref[0:1, :] + bmod_ref[0:1, :])
    scale = jnp.where(is_ctx, mod_ref[4:5, :] + bmod_ref[4:5, :], mod_ref[1:2, :] + bmod_ref[1:2, :])
    r = lax.rsqrt(jnp.mean(xt * xt, axis=-1, keepdims=True) + NORM_EPS)
    xg = (xt * r) * ng_ref[...]
    h = xg * (1.0 + scale) + shift
    return xt, r, xg, h, scale


def _inproj_fwd(x, ctx, modrows, bmodrows, norm_g, w_in_p, q_lora_g, w_uq_p, kv_lora_g, w_ukv, qng_p, kng_p, cos, slo, shi):
    seq = x.shape[0]
    n_tiles = seq // TILE + 1
    tot = seq + TILE

    n_mla = R_Q + R_KV + 128

    def body(x_ref, ctx_ref, mod_ref, bmod_ref, ng_ref, win_ref, qlg_ref, wuq_ref, kvlg_ref, wukv_ref, qng_ref, kng_ref,
             cos_ref, slo_ref, shi_ref, ug_ref, um_ref, q_ref, k_ref, v_ref, stash_even, stash_odd):
        s = pl.program_id(0)

        @pl.when(s == 0)
        def _():
            stash_odd[...] = jnp.zeros_like(stash_odd)

        refs = (x_ref, ctx_ref, mod_ref, bmod_ref, ng_ref, win_ref, qlg_ref, wuq_ref, kvlg_ref, wukv_ref, qng_ref, kng_ref,
                cos_ref, slo_ref, shi_ref, ug_ref, um_ref, q_ref, k_ref, v_ref)

        @pl.when(lax.rem(s, 2) == 0)
        def _():
            stages(s, refs, stash_even, stash_odd)

        @pl.when(lax.rem(s, 2) == 1)
        def _():
            stages(s, refs, stash_odd, stash_even)

    def stages(s, refs, fill, prev_ref):
        (x_ref, ctx_ref, mod_ref, bmod_ref, ng_ref, win_ref, qlg_ref, wuq_ref, kvlg_ref, wukv_ref, qng_ref, kng_ref,
         cos_ref, slo_ref, shi_ref, ug_ref, um_ref, q_ref, k_ref, v_ref) = refs
        cos_t, slo_t, shi_t = cos_ref[...], slo_ref[...], shi_ref[...]
        prev = prev_ref[...]
        cq = prev[:, 0:R_Q]
        qn = cq * lax.rsqrt(jnp.mean(cq * cq, axis=-1, keepdims=True) + NORM_EPS) * qlg_ref[...]
        q = _dot_nt(qn.astype(BF16), wuq_ref[...])
        qg = qng_ref[...] * (ATTN_SCALE * LOG2_E)
        for hd in range(N_HEADS):
            qh = q[:, hd * D_HEAD_PAD:(hd + 1) * D_HEAD_PAD]
            rr = lax.rsqrt(jnp.sum(qh * qh, axis=-1, keepdims=True) * (1.0 / D_HEAD) + NORM_EPS)
            qy = qh * rr * qg
            q_ref[hd, :, 0:D_NOPE] = qy[:, 0:D_NOPE].astype(BF16)
            q_ref[hd, :, D_NOPE:D_HEAD_PAD] = _rope(qy[:, D_NOPE:D_HEAD_PAD], cos_t, slo_t, shi_t).astype(BF16)

        ckv = prev[:, R_Q:R_Q + R_KV]
        kvn = ckv * lax.rsqrt(jnp.mean(ckv * ckv, axis=-1, keepdims=True) + NORM_EPS) * kvlg_ref[...]
        kv = _dot(kvn.astype(BF16), wukv_ref[...])
        krz = prev[:, R_Q + R_KV:n_mla]
        kr_ss = jnp.sum(krz * krz, axis=-1, keepdims=True)
        kg = kng_ref[...]
        for hd in range(N_HEADS):
            kn = kv[:, hd * 256:hd * 256 + D_NOPE]
            rr = lax.rsqrt((jnp.sum(kn * kn, axis=-1, keepdims=True) + kr_ss) * (1.0 / D_HEAD) + NORM_EPS)
            k_ref[hd, :, 0:D_NOPE] = (kn * rr * kg[:, 0:D_NOPE]).astype(BF16)
            k_ref[hd, :, D_NOPE:D_HEAD_PAD] = _rope(krz * rr * kg[:, D_NOPE:D_HEAD_PAD], cos_t, slo_t, shi_t).astype(BF16)
            v_ref[hd] = kv[:, hd * 256 + D_NOPE:(hd + 1) * 256].astype(BF16)

        _, _, _, h, _ = _modulated_input(s == 0, x_ref, ctx_ref, mod_ref, bmod_ref, ng_ref)
        u = _dot_nt(h.astype(BF16), win_ref[...])
        ug_ref[...] = u[:, 0:N_GATES].astype(BF16)
        um_ref[...] = u[:, N_GATES:U_PAD]
        fill[...] = u[:, N_GATES:U_PAD]

    first = lambda s: jnp.minimum(s, n_tiles - 1)
    second = lambda s: jnp.maximum(s - 1, 0)
    latent = lambda t: jnp.maximum(t - 1, 0)
    full = lambda shape: pl.BlockSpec(shape, lambda s: (0,) * len(shape))
    rope_spec = pl.BlockSpec((TILE, 128), lambda s: (second(s), 0))
    return pl.pallas_call(
        body, name="inproj_fwd", grid=(n_tiles + 1,),
        out_shape=(jax.ShapeDtypeStruct((seq, N_GATES), BF16), jax.ShapeDtypeStruct((tot, n_mla), F32),
                   jax.ShapeDtypeStruct((N_HEADS, seq, D_HEAD_PAD), BF16),
                   jax.ShapeDtypeStruct((N_HEADS, tot, D_HEAD_PAD), BF16),
                   jax.ShapeDtypeStruct((N_HEADS, tot, D_V), BF16)),
        in_specs=[pl.BlockSpec((TILE, D_MODEL), lambda s: (latent(first(s)), 0)), full((TILE, D_MODEL)), full((8, D_MODEL)),
                  full((8, D_MODEL)), full((1, D_MODEL)), full((U_PAD, D_MODEL)), full((1, R_Q)),
                  full((N_HEADS * D_HEAD_PAD, R_Q)), full((1, R_KV)), full((R_KV, N_HEADS * 256)), full((1, D_HEAD_PAD)),
                  full((1, D_HEAD_PAD)), rope_spec, rope_spec, rope_spec],
        out_specs=(pl.BlockSpec((TILE, N_GATES), lambda s: (latent(first(s)), 0)),
                   pl.BlockSpec((TILE, n_mla), lambda s: (first(s), 0)),
                   pl.BlockSpec((N_HEADS, TILE, D_HEAD_PAD), lambda s: (0, latent(second(s)), 0)),
                   pl.BlockSpec((N_HEADS, TILE, D_HEAD_PAD), lambda s: (0, second(s), 0)),
                   pl.BlockSpec((N_HEADS, TILE, D_V), lambda s: (0, second(s), 0))),
        scratch_shapes=[pltpu.VMEM((TILE, n_mla), F32), pltpu.VMEM((TILE, n_mla), F32)],
        compiler_params=pltpu.CompilerParams(dimension_semantics=("arbitrary",), vmem_limit_bytes=VMEM_LIMIT),
    )(x, ctx, modrows, bmodrows, norm_g, w_in_p, q_lora_g, w_uq_p, kv_lora_g, w_ukv, qng_p, kng_p, cos, slo, shi)


def _hosted_exchange(first, last, items, send_sems, recv_sems, local_sems):
    me = _lin(_coords())

    def remote(n, k, src, land, scatter):
        peer = _peer(k)
        return pltpu.make_async_remote_copy(
            src_ref=src.at[_lin(peer)] if scatter else src, dst_ref=land.at[me], send_sem=send_sems.at[n, k - 1],
            recv_sem=recv_sems.at[n, k - 1], device_id=peer, device_id_type=MESH)

    def local(n, src, land, scatter):
        return pltpu.make_async_copy(src.at[me] if scatter else src, land.at[me], local_sems.at[n])

    @pl.when(first)
    def _():
        for n, (src, land, scatter) in enumerate(items):
            for k in range(1, N_DEV):
                remote(n, k, src, land, scatter).start()
            local(n, src, land, scatter).start()

    @pl.when(last)
    def _():
        for n, (src, land, scatter) in enumerate(items):
            for k in range(1, N_DEV):
                peer = _peer(k)
                pltpu.make_async_remote_copy(
                    src_ref=src.at[0] if scatter else src, dst_ref=land.at[_lin(peer)], send_sem=send_sems.at[n, k - 1],
                    recv_sem=recv_sems.at[n, k - 1], device_id=peer, device_id_type=MESH).wait_recv()
            for k in range(1, N_DEV):
                remote(n, k, src, land, scatter).wait_send()
            local(n, src, land, scatter).wait()


def _attn_fwd(q, k, v, block_q, w_out_b):
    _, seq, _ = q.shape
    n_keys = k.shape[1]
    n_q = seq // block_q

    def body(q_ref, k_ref, v_ref, wo_ref, o_ref, lse_ref, wog_ref, ssem, rsem, lsem):
        h, i = pl.program_id(0), pl.program_id(1)
        _hosted_exchange((h == 0) & (i == 0), (h == N_HEADS - 1) & (i == n_q - 1), [(wo_ref, wog_ref, False)],
                         ssem, rsem, lsem)
        kb, vb = k_ref[0], v_ref[0]
        for r0 in range(0, block_q, ATTN_ROWS):
            rows = slice(r0, r0 + ATTN_ROWS)
            s = _dot_nt(q_ref[0, rows, :], kb)
            m = jnp.max(s, axis=-1, keepdims=True)
            p = jnp.exp2(s - m)
            l = jnp.sum(p, axis=-1, keepdims=True)
            o_ref[rows, :] = _dot(p.astype(BF16), vb) * (1.0 / l)
            lse_ref[0, rows, :] = m + jnp.log2(l)

    hbm = pl.BlockSpec(memory_space=pl.ANY)
    return pl.pallas_call(
        body, name="attn_fwd", grid=(N_HEADS, n_q),
        out_shape=(jax.ShapeDtypeStruct((seq, D_ATTN), F32), jax.ShapeDtypeStruct((N_HEADS, seq, 1), F32),
                   jax.ShapeDtypeStruct((N_DEV,) + w_out_b.shape, w_out_b.dtype)),
        in_specs=[pl.BlockSpec((1, block_q, D_HEAD_PAD), lambda h, i: (h, i, 0)),
                  pl.BlockSpec((1, n_keys, D_HEAD_PAD), lambda h, i: (h, 0, 0)),
                  pl.BlockSpec((1, n_keys, D_V), lambda h, i: (h, 0, 0)), hbm],
        out_specs=(pl.BlockSpec((block_q, D_V), lambda h, i: (i, h)),
                   pl.BlockSpec((1, block_q, 1), lambda h, i: (h, i, 0)), hbm),
        scratch_shapes=[pltpu.SemaphoreType.DMA((1, 7)), pltpu.SemaphoreType.DMA((1, 7)), pltpu.SemaphoreType.DMA((1,))],
        compiler_params=pltpu.CompilerParams(dimension_semantics=("arbitrary", "arbitrary"), vmem_limit_bytes=VMEM_LIMIT),
    )(q, k, v, w_out_b)


def _attn_bwd(q, k, v, o, lse, d_o, block_q, gwo_blocks, gwp):
    _, seq, _ = q.shape
    n_keys = k.shape[1]
    n_q = seq // block_q

    def body(q_ref, k_ref, v_ref, o_ref, lse_ref, do_ref, gwo_ref, gwp_ref, dq_ref, dkt_ref, dvt_ref, lwo_ref, lwp_ref,
             ssem, rsem, lsem):
        h, i = pl.program_id(0), pl.program_id(1)
        _hosted_exchange((h == 0) & (i == 0), (h == N_HEADS - 1) & (i == n_q - 1),
                         [(gwo_ref, lwo_ref, True), (gwp_ref, lwp_ref, False)], ssem, rsem, lsem)

        @pl.when(i == 0)
        def _():
            dkt_ref[...] = jnp.zeros_like(dkt_ref)
            dvt_ref[...] = jnp.zeros_like(dvt_ref)

        kb, vb = k_ref[0], v_ref[0]
        raws, ps = [], []
        for r0 in range(0, block_q, ATTN_ROWS):
            rows = slice(r0, r0 + ATTN_ROWS)
            d_out = do_ref[rows, :]
            p = jnp.exp2(_dot_nt(q_ref[0, rows, :], kb) - lse_ref[0, rows, :])
            dp = _dot_nt(d_out.astype(BF16), vb)
            delta = jnp.sum(d_out * o_ref[rows, :], axis=-1, keepdims=True)
            raw = (p * (dp - delta)).astype(BF16)
            dq_ref[0, rows, :] = _dot(raw, kb)
            raws.append(raw)
            ps.append(p.astype(BF16))
        dkt_ref[0] += _dot_tn(q_ref[0], jnp.concatenate(raws, axis=0))
        dvt_ref[0] += _dot_tn(do_ref[...].astype(BF16), jnp.concatenate(ps, axis=0))

    hbm = pl.BlockSpec(memory_space=pl.ANY)
    return pl.pallas_call(
        body, name="attn_bwd", grid=(N_HEADS, n_q),
        out_shape=(jax.ShapeDtypeStruct((N_HEADS, seq, D_HEAD_PAD), F32),
                   jax.ShapeDtypeStruct((N_HEADS, D_HEAD_PAD, n_keys), F32),
                   jax.ShapeDtypeStruct((N_HEADS, D_V, n_keys), F32),
                   jax.ShapeDtypeStruct(gwo_blocks.shape, gwo_blocks.dtype),
                   jax.ShapeDtypeStruct((N_DEV,) + gwp.shape, gwp.dtype)),
        in_specs=[pl.BlockSpec((1, block_q, D_HEAD_PAD), lambda h, i: (h, i, 0)),
                  pl.BlockSpec((1, n_keys, D_HEAD_PAD), lambda h, i: (h, 0, 0)),
                  pl.BlockSpec((1, n_keys, D_V), lambda h, i: (h, 0, 0)),
                  pl.BlockSpec((block_q, D_V), lambda h, i: (i, h)),
                  pl.BlockSpec((1, block_q, 1), lambda h, i: (h, i, 0)),
                  pl.BlockSpec((block_q, D_V), lambda h, i: (i, h)), hbm, hbm],
        out_specs=(pl.BlockSpec((1, block_q, D_HEAD_PAD), lambda h, i: (h, i, 0)),
                   pl.BlockSpec((1, D_HEAD_PAD, n_keys), lambda h, i: (h, 0, 0)),
                   pl.BlockSpec((1, D_V, n_keys), lambda h, i: (h, 0, 0)), hbm, hbm),
        scratch_shapes=[pltpu.SemaphoreType.DMA((2, 7)), pltpu.SemaphoreType.DMA((2, 7)), pltpu.SemaphoreType.DMA((2,))],
        compiler_params=pltpu.CompilerParams(dimension_semantics=("arbitrary", "arbitrary"), vmem_limit_bytes=VMEM_LIMIT),
    )(q, k, v, o, lse, d_o, gwo_blocks, gwp)


def _mix(x, target, attn, u, modrows, bmodrows, w_pool, pool_scale, w_out):
    seq = x.shape[0]
    n_tiles = seq // TILE
    rows8 = TILE // HALO
    last8 = seq // HALO - 1

    n_blk = seq // Q_BLOCK
    per = TILE // n_blk
    assert TILE % n_blk == 0 and per % 8 == 0 and n_tiles >= 2
    n_stash = 8

    def body(x_ref, t_ref, o_hbm, ga_ref, pin_ref, hb_ref, ha_ref, gp_ref, mod_ref, bmod_ref, wp_ref, ps_ref, wo_ref,
             loss_ref, g1_ref, dgate_ref, do_hbm, dga_ref, dgp_ref, dpl_ref, gwob_ref, gwp_ref, dps_ref,
             abuf, dbuf, gwo_ref, *rest):
        stash_even, stash_odd = rest[0:n_stash], rest[n_stash:2 * n_stash]
        rsem, wsem = rest[2 * n_stash:]
        s = pl.program_id(0)

        def slab_copies(tile, slot, fetch):
            window = pl.ds(tile * per, per)
            if fetch:
                return [pltpu.make_async_copy(o_hbm.at[:, window, :], abuf.at[slot], rsem.at[slot])]
            return [pltpu.make_async_copy(dbuf.at[slot], do_hbm.at[:, window, :], wsem.at[slot])]

        def wait_all(slot, fetch):
            slab_copies(0, slot, fetch)[0].wait()

        a_slot = lax.rem(s, 2)
        b_slot = 1 - a_slot

        @pl.when(s == 0)
        def _():
            loss_ref[...] = jnp.zeros_like(loss_ref)
            dgate_ref[...] = jnp.zeros_like(dgate_ref)
            gwo_ref[...] = jnp.zeros_like(gwo_ref)
            gwp_ref[...] = jnp.zeros_like(gwp_ref)
            dps_ref[...] = jnp.zeros_like(dps_ref)
            for cp in slab_copies(0, 0, True):
                cp.start()

        @pl.when(s + 1 < n_tiles)
        def _():
            for cp in slab_copies(s + 1, b_slot, True):
                cp.start()

        @pl.when(s < n_tiles)
        def _():
            wait_all(a_slot, True)

        @pl.when(s >= 3)
        def _():
            wait_all(b_slot, False)

        gate = mod_ref[2:3, :] + bmod_ref[2:3, :]
        ps = ps_ref[...]

        def a_vector(slot):
            attn_t = jnp.swapaxes(abuf[slot], 0, 1).reshape(TILE, D_ATTN)
            ga = ga_ref[...].astype(F32)
            sga = _sigmoid(ga)
            silu_ga = ga * sga
            pin = pin_ref[...].astype(F32)
            xs = jnp.concatenate([jnp.where(s > 0, hb_ref[...].astype(F32), 0.0), pin,
                                  jnp.where(s < n_tiles - 1, ha_ref[...].astype(F32), 0.0)], axis=0)
            tpos = s * TILE + lax.broadcasted_iota(jnp.int32, (TILE, 1), 0)
            pooled = []
            for g, w in enumerate(POOL_WINDOWS):
                sl = slice(g * GROUP_DIM, (g + 1) * GROUP_DIM)
                ws = _window_sum(xs[:, sl], w, False)[HALO:HALO + TILE]
                pooled.append((ws * _inv_count(tpos, w, seq) - pin[:, sl]).astype(BF16))
            return attn_t, ga, sga, silu_ga, pooled

        def a_group_maps(pooled):
            return jnp.concatenate([_dot(pooled[g], wp_ref[g].astype(BF16)) for g in range(len(POOL_WINDOWS))], axis=1)

        def a_project(stash, yp, attn_t, ga, sga, silu_ga, pooled):
            zb_ref, _, fa_ref, sa_ref, fp_ref, sp_ref, yp_ref, pooled_ref = stash
            ypool = yp * ps
            gp = gp_ref[...].astype(F32)
            sgp = _sigmoid(gp)
            silu_gp = gp * sgp
            z = jnp.concatenate([silu_ga * attn_t, silu_gp * ypool], axis=1).astype(BF16)
            y = _dot(z, wo_ref[...])
            zb_ref[...] = z
            fa_ref[...] = attn_t * (sga * (1.0 + ga * (1.0 - sga)))
            sa_ref[...] = silu_ga
            fp_ref[...] = ypool * (sgp * (1.0 + gp * (1.0 - sgp)))
            sp_ref[...] = silu_gp
            yp_ref[...] = yp
            pooled_ref[...] = jnp.concatenate(pooled, axis=1)
            return y

        def a_loss(stash, y):
            res = x_ref[...] + gate * y - t_ref[...]
            loss_ref[...] += jnp.sum(res * res) * (0.5 / D_MODEL)
            dxn = res * (1.0 / D_MODEL)
            g1_ref[...] = dxn
            dgate_ref[...] += jnp.sum(dxn * y, axis=0, keepdims=True)
            stash[1][...] = (gate * dxn).astype(BF16)

        def b_dz(stash):
            return _dot_nt(stash[1][...], wo_ref[...])

        def b_gwo(stash):
            gwo_ref[...] += _dot_tn(stash[0][...], stash[1][...])

        def b_vector(stash, slot, dz):
            _, _, fa_ref, sa_ref, fp_ref, sp_ref, yp_ref, _ = stash
            dbra = dz[:, 0:D_ATTN]
            dbrp = dz[:, D_ATTN:]
            dga_ref[...] = (dbra * fa_ref[...]).astype(BF16)
            dbuf[slot] = jnp.swapaxes((dbra * sa_ref[...]).reshape(per, n_blk, D_ATTN), 0, 1)
            dgp_ref[...] = (dbrp * fp_ref[...]).astype(BF16)
            dyp_s = dbrp * sp_ref[...]
            dps_ref[...] += jnp.sum(dyp_s * yp_ref[...], axis=0, keepdims=True)
            return (dyp_s * ps).astype(BF16)

        def b_small(stash, dyp):
            pooled_ref = stash[7]
            for g in range(len(POOL_WINDOWS)):
                sl = slice(g * GROUP_DIM, (g + 1) * GROUP_DIM)
                gwp_ref[g] += _dot_tn(pooled_ref[:, sl], dyp[:, sl])
                dpl_ref[:, sl] = _dot_nt(dyp[:, sl], wp_ref[g].astype(BF16)).astype(BF16)

        def both(a_stash, b_stash, slot_a):
            dz = b_dz(b_stash)
            front = a_vector(slot_a)
            yp = a_group_maps(front[-1])
            b_gwo(b_stash)
            y = a_project(a_stash, yp, *front)
            dyp = b_vector(b_stash, 1 - slot_a, dz)
            a_loss(a_stash, y)
            b_small(b_stash, dyp)

        @pl.when(s == 0)
        def _():
            front = a_vector(0)
            a_loss(stash_even, a_project(stash_even, a_group_maps(front[-1]), *front))

        @pl.when((s > 0) & (s < n_tiles) & (a_slot == 0))
        def _():
            both(stash_even, stash_odd, 0)

        @pl.when((s > 0) & (s < n_tiles) & (a_slot == 1))
        def _():
            both(stash_odd, stash_even, 1)

        @pl.when(s == n_tiles)
        def _():
            last = (n_tiles - 1) % 2
            stash = stash_odd if last else stash_even
            dz = b_dz(stash)
            b_gwo(stash)
            b_small(stash, b_vector(stash, last, dz))
            gwob_ref[...] = gwo_ref[...].astype(BF16)

        @pl.when(s >= 1)
        def _():
            for cp in slab_copies(s - 1, b_slot, False):
                cp.start()

        @pl.when(s == n_tiles)
        def _():
            wait_all(b_slot, False)
            wait_all(a_slot, False)

    ta = lambda s: jnp.minimum(s, n_tiles - 1)
    tb = lambda s: jnp.maximum(s - 1, 0)
    tile = lambda w: pl.BlockSpec((TILE, w), lambda s: (ta(s), 0))
    tile_b = lambda w: pl.BlockSpec((TILE, w), lambda s: (tb(s), 0))
    ucol = lambda col: pl.BlockSpec((TILE, 512), lambda s: (ta(s), col))
    full = lambda shape: pl.BlockSpec(shape, lambda s: (0,) * len(shape))
    stash_shapes = [pltpu.VMEM((TILE, D_MODEL), BF16), pltpu.VMEM((TILE, D_MODEL), BF16)] + \
        [pltpu.VMEM((TILE, 512), F32)] * 5 + [pltpu.VMEM((TILE, D_POOL), BF16)]
    return pl.pallas_call(
        body, name="mix_fwd_bwd", grid=(n_tiles + 1,),
        out_shape=(jax.ShapeDtypeStruct((8, 128), F32), jax.ShapeDtypeStruct((seq, D_MODEL), F32),
                   jax.ShapeDtypeStruct((1, D_MODEL), F32), jax.ShapeDtypeStruct((n_blk, Q_BLOCK, D_ATTN), F32),
                   jax.ShapeDtypeStruct((seq, D_ATTN), BF16), jax.ShapeDtypeStruct((seq, D_POOL), BF16),
                   jax.ShapeDtypeStruct((seq, D_POOL), BF16), jax.ShapeDtypeStruct((D_MODEL, D_MODEL), BF16),
                   jax.ShapeDtypeStruct((4, GROUP_DIM, GROUP_DIM), F32), jax.ShapeDtypeStruct((1, D_POOL), F32)),
        in_specs=[tile(D_MODEL), tile(D_MODEL), pl.BlockSpec(memory_space=pl.ANY), ucol(0), ucol(1),
                  pl.BlockSpec((HALO, 512), lambda s: (jnp.maximum(ta(s) * rows8 - 1, 0), 1)),
                  pl.BlockSpec((HALO, 512), lambda s: (jnp.minimum((ta(s) + 1) * rows8, last8), 1)),
                  ucol(2), full((8, D_MODEL)), full((8, D_MODEL)), full((4, GROUP_DIM, GROUP_DIM)), full((1, D_POOL)),
                  full((D_MODEL, D_MODEL))],
        out_specs=(full((8, 128)), tile(D_MODEL), full((1, D_MODEL)), pl.BlockSpec(memory_space=pl.ANY), tile_b(D_ATTN),
                   tile_b(D_POOL), tile_b(D_POOL), full((D_MODEL, D_MODEL)), full((4, GROUP_DIM, GROUP_DIM)),
                   full((1, D_POOL))),
        scratch_shapes=[pltpu.VMEM((2, n_blk, per, D_ATTN), F32), pltpu.VMEM((2, n_blk, per, D_ATTN), F32),
                        pltpu.VMEM((D_MODEL, D_MODEL), F32), *stash_shapes, *stash_shapes,
                        pltpu.SemaphoreType.DMA((2,)), pltpu.SemaphoreType.DMA((2,))],
        compiler_params=pltpu.CompilerParams(dimension_semantics=("arbitrary",), vmem_limit_bytes=VMEM_LIMIT),
    )(x, target, attn.reshape(n_blk, Q_BLOCK, D_ATTN), u, u, u, u, u, modrows, bmodrows, w_pool, pool_scale, w_out)


def _inproj_bwd(x, ctx, modrows, bmodrows, norm_g, w_in_p, q_lora_g, w_uq_p, kv_lora_g, w_ukv, qng_p, kng_p, cos, slo, shi,
                u, dq, dk, dv, dga, dgp, dpl, g1):
    seq = x.shape[0]
    n_tiles = seq // TILE + 1
    rows8 = TILE // HALO
    last8 = seq // HALO - 1

    def body(*refs):
        du_even, du_odd, dh_even, dh_odd = refs[-4:]
        gwin_ref, gwuq_ref, gwukv_ref, dqlg_ref, dkvlg_ref, dqng_ref, dkng_ref, dng_ref, dmod_ref = refs[-13:-4]
        s = pl.program_id(0)

        @pl.when(s == 0)
        def _():
            for ref in (gwin_ref, gwuq_ref, gwukv_ref, dqlg_ref, dkvlg_ref, dqng_ref, dkng_ref, dng_ref, dmod_ref,
                        du_odd, dh_even):
                ref[...] = jnp.zeros_like(ref)

        @pl.when(lax.rem(s, 2) == 0)
        def _():
            stages(s, refs[:-4], du_even, du_odd, dh_odd, dh_even)

        @pl.when(lax.rem(s, 2) == 1)
        def _():
            stages(s, refs[:-4], du_odd, du_even, dh_even, dh_odd)

    def stages(s, refs, du_ref, du_prev, dh_fill, dh_prev):
        (xb_ref, xc_ref, ctx_ref, mod_ref, bmod_ref, ng_ref, win_ref, qlg_ref, wuq_ref, kvlg_ref, wukv_ref, qng_ref, kng_ref,
         cos_ref, slo_ref, shi_ref, cq_ref, ckv_ref, kr_ref, dq_ref, dk_ref, dv_ref, dga_ref, dgp_ref, dpl_ref,
         hb_ref, ha_ref, g1_ref,
         gx_ref, gwin_ref, gwuq_ref, gwukv_ref, dqlg_ref, dkvlg_ref, dqng_ref, dkng_ref, dng_ref, dmod_ref) = refs


        i = jnp.minimum(s, n_tiles - 1)
        valid = s < n_tiles
        is_lat = (s > 0) & valid
        cq = cq_ref[...]
        rq = lax.rsqrt(jnp.mean(cq * cq, axis=-1, keepdims=True) + NORM_EPS)
        qhat = cq * rq
        qnb = (qhat * qlg_ref[...]).astype(BF16)
        q = _dot_nt(qnb, wuq_ref[...])
        ckv = ckv_ref[...]
        rkv = lax.rsqrt(jnp.mean(ckv * ckv, axis=-1, keepdims=True) + NORM_EPS)
        kvhat = ckv * rkv
        kvnb = (kvhat * kvlg_ref[...]).astype(BF16)
        kv = _dot(kvnb, wukv_ref[...])

        _, _, _, h2, _ = _modulated_input(s <= 1, xb_ref, ctx_ref, mod_ref, bmod_ref, ng_ref)
        dub = du_prev[...]
        dh_fill[...] = _dot(dub, win_ref[...])
        gwin_ref[...] += _dot_tn(dub, h2.astype(BF16))

        ctx3 = s <= 2
        xt, r, xg, _, scale = _modulated_input(ctx3, xc_ref, ctx_ref, mod_ref, bmod_ref, ng_ref)
        dh = dh_prev[...]
        dshift = jnp.sum(dh, axis=0, keepdims=True)
        dscale = jnp.sum(dh * xg, axis=0, keepdims=True)
        zero = jnp.zeros_like(dshift)
        dmod_ref[0:1, :] += jnp.where(ctx3, zero, dshift)
        dmod_ref[1:2, :] += jnp.where(ctx3, zero, dscale)
        dmod_ref[2:3, :] += jnp.where(ctx3, dshift, zero)
        dmod_ref[3:4, :] += jnp.where(ctx3, dscale, zero)
        dxg = dh * (1.0 + scale)
        xr = xt * r
        dng_ref[...] += jnp.sum(dxg * xr, axis=0, keepdims=True)
        dxn = dxg * ng_ref[...]
        gx_ref[...] = g1_ref[...] + r * (dxn - xr * jnp.mean(dxn * xr, axis=-1, keepdims=True))

        cos_t, slo_t, shi_t = cos_ref[...], slo_ref[...], shi_ref[...]
        qg = qng_ref[...]
        dq_heads = []
        dqng = jnp.zeros((1, D_HEAD_PAD), F32)
        for hd in range(N_HEADS):
            qh = q[:, hd * D_HEAD_PAD:(hd + 1) * D_HEAD_PAD]
            rr = lax.rsqrt(jnp.sum(qh * qh, axis=-1, keepdims=True) * (1.0 / D_HEAD) + NORM_EPS)
            yq = qh * rr
            dpost = jnp.where(is_lat, dq_ref[hd] * ATTN_SCALE, 0.0)
            dyn = jnp.concatenate([dpost[:, 0:D_NOPE], _rope_t(dpost[:, D_NOPE:], cos_t, slo_t, shi_t)], axis=1)
            dqng = dqng + jnp.sum(dyn * yq, axis=0, keepdims=True)
            dyg = dyn * qg
            dq_heads.append(rr * (dyg - yq * (jnp.sum(dyg * yq, axis=-1, keepdims=True) * (1.0 / D_HEAD))))
        dqng_ref[...] += dqng
        dqf = jnp.concatenate(dq_heads, axis=1).astype(BF16)

        krz = kr_ref[...]
        kr_ss = jnp.sum(krz * krz, axis=-1, keepdims=True)
        kg = kng_ref[...]
        kg_n, kg_r = kg[:, 0:D_NOPE], kg[:, D_NOPE:]
        dkv_parts = []
        dkr = jnp.zeros((TILE, 128), F32)
        dkng_n = jnp.zeros((1, D_NOPE), F32)
        dkng_r = jnp.zeros((1, 128), F32)
        for hd in range(N_HEADS):
            kn = kv[:, hd * 256:hd * 256 + D_NOPE]
            rr = lax.rsqrt((jnp.sum(kn * kn, axis=-1, keepdims=True) + kr_ss) * (1.0 / D_HEAD) + NORM_EPS)
            yn, yr = kn * rr, krz * rr
            dk_h = jnp.where(valid, jnp.transpose(dk_ref[hd]) * LN_2, 0.0)
            dpn = dk_h[:, 0:D_NOPE]
            dpr = _rope_t(dk_h[:, D_NOPE:], cos_t, slo_t, shi_t)
            dkng_n = dkng_n + jnp.sum(dpn * yn, axis=0, keepdims=True)
            dkng_r = dkng_r + jnp.sum(dpr * yr, axis=0, keepdims=True)
            dyn, dyr = dpn * kg_n, dpr * kg_r
            proj = (jnp.sum(dyn * yn, axis=-1, keepdims=True) + jnp.sum(dyr * yr, axis=-1, keepdims=True)) * (1.0 / D_HEAD)
            dkv_parts.append(rr * (dyn - yn * proj))
            dkv_parts.append(jnp.where(valid, jnp.transpose(dv_ref[hd]), 0.0))
            dkr = dkr + rr * (dyr - yr * proj)
        dkng_ref[:, 0:D_NOPE] += dkng_n
        dkng_ref[:, D_NOPE:] += dkng_r
        dkvf = jnp.concatenate(dkv_parts, axis=1).astype(BF16)

        lat_t = jnp.maximum(i - 1, 0)
        dpl_t = dpl_ref[...].astype(F32)
        ds = jnp.concatenate([jnp.where(i > 1, hb_ref[...].astype(F32), 0.0), dpl_t,
                              jnp.where(i < n_tiles - 1, ha_ref[...].astype(F32), 0.0)], axis=0)
        tpos = lat_t * TILE - HALO + lax.broadcasted_iota(jnp.int32, (TILE + 2 * HALO, 1), 0)
        for g, w in enumerate(POOL_WINDOWS):
            sl = slice(g * GROUP_DIM, (g + 1) * GROUP_DIM)
            wsum = _window_sum(ds[:, sl] * _inv_count(tpos, w, seq), w, True)[HALO:HALO + TILE]
            du_ref[:, O_PIN + g * GROUP_DIM:O_PIN + (g + 1) * GROUP_DIM] = jnp.where(
                is_lat, wsum - dpl_t[:, sl], 0.0).astype(BF16)

        du_ref[:, O_GA:O_GA + D_ATTN] = jnp.where(is_lat, dga_ref[...], jnp.zeros((), BF16))
        du_ref[:, O_GP:O_GP + D_POOL] = jnp.where(is_lat, dgp_ref[...], jnp.zeros((), BF16))
        du_ref[:, O_KR:U_PAD] = dkr.astype(BF16)

        gwuq_ref[...] += _dot_tn(dqf, qnb)
        dqn = _dot(dqf, wuq_ref[...])
        gwukv_ref[...] += _dot_tn(dkvf, kvnb)
        dkvn = _dot_nt(dkvf, wukv_ref[...])
        dqlg_ref[...] += jnp.sum(dqn * qhat, axis=0, keepdims=True)
        dqhat = dqn * qlg_ref[...]
        dcq = rq * (dqhat - qhat * jnp.mean(dqhat * qhat, axis=-1, keepdims=True))
        dkvlg_ref[...] += jnp.sum(dkvn * kvhat, axis=0, keepdims=True)
        dkvhat = dkvn * kvlg_ref[...]
        dckv = rkv * (dkvhat - kvhat * jnp.mean(dkvhat * kvhat, axis=-1, keepdims=True))
        du_ref[:, O_CQ:O_CQ + R_Q] = dcq.astype(BF16)
        du_ref[:, O_CKV:O_CKV + R_KV] = dckv.astype(BF16)

    t1 = lambda s: jnp.minimum(s, n_tiles - 1)
    t2 = lambda s: jnp.clip(s - 1, 0, n_tiles - 1)
    t3 = lambda s: jnp.clip(s - 2, 0, n_tiles - 1)
    latent = lambda t: jnp.maximum(t - 1, 0)
    lat = lambda s: (latent(t1(s)), 0)
    lat3 = lambda s: (latent(t3(s)), 0)
    full = lambda shape: pl.BlockSpec(shape, lambda s: (0,) * len(shape))
    rope_spec = pl.BlockSpec((TILE, 128), lambda s: (t1(s), 0))
    return pl.pallas_call(
        body, name="inproj_bwd", grid=(n_tiles + 2,),
        out_shape=(jax.ShapeDtypeStruct((seq, D_MODEL), F32), jax.ShapeDtypeStruct((U_PAD, D_MODEL), F32),
                   jax.ShapeDtypeStruct((N_HEADS * D_HEAD_PAD, R_Q), F32), jax.ShapeDtypeStruct((N_HEADS * 256, R_KV), F32),
                   jax.ShapeDtypeStruct((1, R_Q), F32), jax.ShapeDtypeStruct((1, R_KV), F32),
                   jax.ShapeDtypeStruct((1, D_HEAD_PAD), F32), jax.ShapeDtypeStruct((1, D_HEAD_PAD), F32),
                   jax.ShapeDtypeStruct((1, D_MODEL), F32), jax.ShapeDtypeStruct((8, D_MODEL), F32)),
        in_specs=[pl.BlockSpec((TILE, D_MODEL), lambda s: (latent(t2(s)), 0)), pl.BlockSpec((TILE, D_MODEL), lat3),
                  full((TILE, D_MODEL)), full((8, D_MODEL)), full((8, D_MODEL)),
                  full((1, D_MODEL)), full((U_PAD, D_MODEL)), full((1, R_Q)), full((N_HEADS * D_HEAD_PAD, R_Q)),
                  full((1, R_KV)), full((R_KV, N_HEADS * 256)), full((1, D_HEAD_PAD)), full((1, D_HEAD_PAD)),
                  rope_spec, rope_spec, rope_spec,
                  pl.BlockSpec((TILE, R_Q), lambda s: (t1(s), (O_CQ - N_GATES) // R_Q)),
                  pl.BlockSpec((TILE, R_KV), lambda s: (t1(s), (O_CKV - N_GATES) // R_KV)),
                  pl.BlockSpec((TILE, 128), lambda s: (t1(s), (O_KR - N_GATES) // 128)),
                  pl.BlockSpec((N_HEADS, TILE, D_HEAD_PAD), lambda s: (0, latent(t1(s)), 0)),
                  pl.BlockSpec((N_HEADS, D_HEAD_PAD, TILE), lambda s: (0, 0, t1(s))),
                  pl.BlockSpec((N_HEADS, D_V, TILE), lambda s: (0, 0, t1(s))),
                  pl.BlockSpec((TILE, D_ATTN), lat), pl.BlockSpec((TILE, D_POOL), lat), pl.BlockSpec((TILE, D_POOL), lat),
                  pl.BlockSpec((HALO, D_POOL), lambda s: (jnp.maximum((t1(s) - 1) * rows8 - 1, 0), 0)),
                  pl.BlockSpec((HALO, D_POOL), lambda s: (jnp.minimum(jnp.maximum(t1(s), 1) * rows8, last8), 0)),
                  pl.BlockSpec((TILE, D_MODEL), lat3)],
        out_specs=(pl.BlockSpec((TILE, D_MODEL), lat3), full((U_PAD, D_MODEL)), full((N_HEADS * D_HEAD_PAD, R_Q)),
                   full((N_HEADS * 256, R_KV)), full((1, R_Q)), full((1, R_KV)), full((1, D_HEAD_PAD)),
                   full((1, D_HEAD_PAD)), full((1, D_MODEL)), full((8, D_MODEL))),
        scratch_shapes=[pltpu.VMEM((TILE, U_PAD), BF16), pltpu.VMEM((TILE, U_PAD), BF16),
                        pltpu.VMEM((TILE, D_MODEL), F32), pltpu.VMEM((TILE, D_MODEL), F32)],
        compiler_params=pltpu.CompilerParams(dimension_semantics=("arbitrary",), vmem_limit_bytes=VMEM_LIMIT),
    )(x, x, ctx, modrows, bmodrows, norm_g, w_in_p, q_lora_g, w_uq_p, kv_lora_g, w_ukv, qng_p, kng_p, cos, slo, shi,
      u, u, u, dq, dk, dv, dga, dgp, dpl, dpl, dpl, g1)


SMALL_LAYOUT = ((0, D_MODEL), (1, R_Q), (2, R_KV), (3, D_HEAD_PAD), (4, D_HEAD_PAD), (5, D_POOL))
ROW_DMOD_B, ROW_DMOD_C, ROW_LOSS = 6, 9, 12


def _epilogue(parts, landed, smalls, dmod4, dgate, loss_acc, w_mod_l):
    n_a, n_l, n_s = len(parts), len(landed), len(smalls)
    n_mod = w_mod_l.shape[1]
    blk = [p.shape[1:] for p in parts]
    small_out = [D_MODEL, R_Q, R_KV, D_HEAD, D_HEAD, D_POOL]

    def body(*refs):
        part_refs = refs[0:n_a]
        land_refs = refs[n_a:n_a + n_l]
        small_in = refs[n_a + n_l:n_a + n_l + n_s]
        dmod4_ref, dgate_ref, loss_ref, wmod_ref = refs[n_a + n_l + n_s:n_a + n_l + n_s + 4]
        outs = refs[n_a + n_l + n_s + 4:]
        red_refs = outs[0:n_a]
        landsum_refs = outs[n_a:n_a + n_l]
        ccg_ref = outs[n_a + n_l]
        sum_refs = outs[n_a + n_l + 1:n_a + n_l + 1 + n_s]
        gbmod_ref, dmy_ref, lossout_ref = outs[n_a + n_l + 1 + n_s:n_a + n_l + 4 + n_s]
        scr = outs[n_a + n_l + 4 + n_s:]
        recv1, own1, sum1b, recv2 = scr[0:n_a], scr[n_a:2 * n_a], scr[2 * n_a:3 * n_a], scr[3 * n_a:4 * n_a]
        small_ref, smallg_ref, tot_ref, cc_ref = scr[4 * n_a:4 * n_a + 4]
        ssem1, rsem1, lsem, ssem2, rsem2, ssem_s, rsem_s, ssem_cc, rsem_cc = scr[4 * n_a + 4:]
        x, y, c = _coords()
        me = (x, y, c)
        sibling = (x, y, 1 - c)

        stage1, own_copies = [], []
        for a in range(n_a):
            for b in range(4):
                dev = 4 * (b >> 1) + 2 * (b & 1)
                stage1.append(pltpu.make_async_remote_copy(
                    src_ref=part_refs[a].at[dev + (1 - c)], dst_ref=recv1[a].at[b],
                    send_sem=ssem1.at[a, b], recv_sem=rsem1.at[a, b], device_id=sibling, device_id_type=MESH))
                own_copies.append(pltpu.make_async_copy(part_refs[a].at[dev + c], own1[a].at[b], lsem.at[a, b]))
                stage1[-1].start()
                own_copies[-1].start()

        small_ref[...] = jnp.zeros_like(small_ref)
        for ref, (row, width) in zip(small_in, SMALL_LAYOUT):
            small_ref[row:row + 1, 0:width] = ref[...]
        small_ref[ROW_DMOD_B:ROW_DMOD_B + 2, :] = dmod4_ref[0:2, :]
        small_ref[ROW_DMOD_B + 2:ROW_DMOD_B + 3, :] = dgate_ref[...]
        small_ref[ROW_DMOD_C:ROW_DMOD_C + 2, :] = dmod4_ref[2:4, :]
        small_ref[ROW_LOSS:ROW_LOSS + 1, 0:128] = loss_ref[0:1, :]
        smallg_ref[_lin(me)] = small_ref[...]
        s_recv, s_send = _gather_to_all(small_ref, smallg_ref, ssem_s, rsem_s)
        s_recv()
        tot = smallg_ref[0]
        for d in range(1, N_DEV):
            tot = tot + smallg_ref[d]
        tot_ref[...] = tot
        for ref, (row, _), width in zip(sum_refs, SMALL_LAYOUT, small_out):
            ref[...] = tot_ref[row:row + 1, 0:width]
        lossout_ref[...] = tot_ref[8:16, 0:128]
        lin = _lin(me)

        def my_columns(three_rows):
            v = jnp.concatenate(three_rows, axis=1)
            out = jnp.zeros((1, n_mod), F32)
            for d in range(N_DEV):
                out = out + jnp.where(lin == d, v[:, d * n_mod:(d + 1) * n_mod], 0.0)
            return out

        rows3 = lambda ref, r0: [ref[r0 + t:r0 + t + 1, :] for t in range(3)]
        dmodc = rows3(tot_ref, ROW_DMOD_C)
        gbmod_ref[...] = jnp.concatenate(rows3(tot_ref, ROW_DMOD_B), axis=1) + jnp.concatenate(dmodc, axis=1)
        dmy_ref[...] = jnp.zeros_like(dmy_ref)
        for b in range(N_DEV):
            dmy_ref[b:b + 1, :] = my_columns(rows3(smallg_ref.at[b], ROW_DMOD_B))
        dmy = my_columns(dmodc)
        dmy_ref[N_DEV:N_DEV + 1, :] = dmy
        part = lax.dot_general(jnp.broadcast_to(dmy, (8, n_mod)), wmod_ref[...], (((1,), (1,)), ((), ())),
                               precision=HIGHEST, preferred_element_type=F32)

        cc_ref[...] = part
        ccg_ref[_lin(me)] = part
        cc_recv, cc_send = _gather_to_all(cc_ref, ccg_ref, ssem_cc, rsem_cc)

        stage2 = []
        for a in range(n_a):
            for b in range(4):
                stage1[4 * a + b].wait_recv()
                own_copies[4 * a + b].wait()
                sum1b[a][b] = (own1[a][b] + recv1[a][b]).astype(BF16)
            for k in (1, 2, 3):
                tx, ty = _flip(x, (k >> 1) & 1), _flip(y, k & 1)
                stage2.append(pltpu.make_async_remote_copy(
                    src_ref=sum1b[a].at[2 * tx + ty], dst_ref=recv2[a].at[k - 1], send_sem=ssem2.at[a, k - 1],
                    recv_sem=rsem2.at[a, k - 1], device_id=(tx, ty, c), device_id_type=MESH))
                stage2[-1].start()
        for a in range(n_a):
            own = own1[a][2 * x + y] + recv1[a][2 * x + y]
            for k in (1, 2, 3):
                stage2[3 * a + k - 1].wait_recv()
                own = own + recv2[a][k - 1].astype(F32)
            red_refs[a][...] = own

        for ref, out in zip(land_refs, landsum_refs):
            acc = ref[0].astype(F32)
            for d in range(1, N_DEV):
                acc = acc + ref[d].astype(F32)
            out[...] = acc
        cc_recv()
        for cp in stage1 + stage2:
            cp.wait_send()
        s_send()
        cc_send()

    vm = pl.BlockSpec(memory_space=pltpu.VMEM)
    sds = jax.ShapeDtypeStruct
    return pl.pallas_call(
        body, name="epilogue_reduce",
        out_shape=(*[sds(s, F32) for s in blk], *[sds(a.shape[1:], F32) for a in landed], sds((N_DEV, 8, D_MODEL), F32),
                   *[sds((1, w), F32) for w in small_out], sds((1, 3 * D_MODEL), F32), sds((16, n_mod), F32),
                   sds((8, 128), F32)),
        in_specs=[pl.BlockSpec(memory_space=pl.ANY)] * n_a + [vm] * (n_l + n_s + 4),
        out_specs=tuple([vm] * (n_a + n_l + n_s + 4)),
        scratch_shapes=[*[pltpu.VMEM((4,) + s, F32) for s in blk], *[pltpu.VMEM((4,) + s, F32) for s in blk],
                        *[pltpu.VMEM((4,) + s, BF16) for s in blk], *[pltpu.VMEM((3,) + s, BF16) for s in blk],
                        pltpu.VMEM((SMALL_ROWS, D_MODEL), F32), pltpu.VMEM((N_DEV, SMALL_ROWS, D_MODEL), F32),
                        pltpu.VMEM((SMALL_ROWS, D_MODEL), F32), pltpu.VMEM((8, D_MODEL), F32),
                        pltpu.SemaphoreType.DMA((n_a, 4)), pltpu.SemaphoreType.DMA((n_a, 4)), pltpu.SemaphoreType.DMA((n_a, 4)),
                        pltpu.SemaphoreType.DMA((n_a, 3)), pltpu.SemaphoreType.DMA((n_a, 3)),
                        pltpu.SemaphoreType.DMA((7,)), pltpu.SemaphoreType.DMA((7,)),
                        pltpu.SemaphoreType.DMA((7,)), pltpu.SemaphoreType.DMA((7,))],
        compiler_params=pltpu.CompilerParams(vmem_limit_bytes=VMEM_LIMIT),
    )(*parts, *landed, *smalls, dmod4, dgate, loss_acc, w_mod_l)


def _adamw(weights, grads, ms, vs, c_all_t, dmod_my, p2g):
    n = len(weights)

    def body(*refs):
        w_refs = refs[0:n]
        g_in = refs[n:2 * n - 2]
        m_refs = refs[2 * n - 2:3 * n - 2]
        v_refs = refs[3 * n - 2:4 * n - 2]
        cat_ref, dmod_ref, p2g_ref = refs[4 * n - 2:4 * n + 1]
        outs = refs[4 * n + 1:]
        gcc_ref, gwm_ref = outs[0], outs[1]
        d_refs, nm_refs, nv_refs = outs[2:2 + n], outs[2 + n:2 + 2 * n], outs[2 + 2 * n:2 + 3 * n]

        part = p2g_ref[0, 0:1, :]
        for d in range(1, N_DEV):
            part = part + p2g_ref[d, 0:1, :]
        cc = w_refs[0][...]
        sg = _sigmoid(cc)
        gcc_ref[...] = part * (sg * (1.0 + cc * (1.0 - sg)))
        cat = cat_ref[...]
        gwm_ref[...] = lax.dot_general(cat * _sigmoid(cat), dmod_ref[...], (((1,), (0,)), ((), ())), precision=HIGHEST,
                                       preferred_element_type=F32)
        for idx in range(n):
            g = (gcc_ref, gwm_ref)[idx][...] if idx < 2 else g_in[idx - 2][...]
            m = ADAM_B1 * m_refs[idx][...] + (1.0 - ADAM_B1) * g
            v = ADAM_B2 * v_refs[idx][...] + (1.0 - ADAM_B2) * (g * g)
            m_hat = m / (1.0 - ADAM_B1 ** ADAM_STEP)
            v_hat = v / (1.0 - ADAM_B2 ** ADAM_STEP)
            d_refs[idx][...] = -ADAM_LR * (m_hat / (jnp.sqrt(v_hat) + ADAM_EPS) + ADAM_WD * w_refs[idx][...])
            nm_refs[idx][...] = m
            nv_refs[idx][...] = v

    vm = pl.BlockSpec(memory_space=pltpu.VMEM)
    shapes = [jax.ShapeDtypeStruct(w.shape, F32) for w in weights]
    args = list(weights) + list(grads[2:]) + list(ms) + list(vs) + [c_all_t, dmod_my, p2g]
    out_shape = tuple(shapes[0:2] + shapes * 3)
    return pl.pallas_call(
        body, name="adamw_update", out_shape=out_shape, in_specs=[vm] * len(args), out_specs=tuple([vm] * len(out_shape)),
        compiler_params=pltpu.CompilerParams(vmem_limit_bytes=VMEM_LIMIT),
    )(*args)


def _rope_tables(seq):
    rows = seq // GRID_W
    row = np.repeat(np.arange(rows, dtype=np.float32), GRID_W)
    col = np.tile(np.arange(GRID_W, dtype=np.float32), rows)
    n_freq = D_ROPE // 4
    inv = (np.float32(ROPE_BASE) ** (-np.arange(n_freq, dtype=np.float32) / np.float32(n_freq))).astype(np.float32)
    ang_r = (row[:, None] * inv).astype(np.float32)
    ang_c = (col[:, None] * inv).astype(np.float32)
    ang = np.concatenate([ang_r, ang_r, ang_c, ang_c], axis=-1)
    cos, sin = np.cos(ang).astype(np.float32), np.sin(ang).astype(np.float32)
    low = (np.arange(D_ROPE) % 32) < 16

    def table(t, fill):
        out = np.full((TILE + seq, 128), fill, np.float32)
        out[TILE:, 0:D_ROPE] = t
        return jnp.asarray(out)

    return table(cos, 1.0), table(np.where(low, -sin, 0.0), 0.0), table(np.where(low, 0.0, sin), 0.0)


def kernel(x, c, ctx, c_ctx, w_mod, b_mod, norm_g, w_in, q_lora_g, w_uq, kv_lora_g, w_ukv, q_norm_g, k_norm_g, w_pool, pool_scale, w_out, loss_target, m_c_ctx, m_w_mod, m_b_mod, m_norm_g, m_w_in, m_q_lora_g, m_w_uq, m_kv_lora_g, m_w_ukv, m_q_norm_g, m_k_norm_g, m_w_pool, m_pool_scale, m_w_out, v_c_ctx, v_w_mod, v_b_mod, v_norm_g, v_w_in, v_q_lora_g, v_w_uq, v_kv_lora_g, v_w_ukv, v_q_norm_g, v_k_norm_g, v_w_pool, v_pool_scale, v_w_out):
    seq = x.shape[1]
    assert ctx.shape[1] == TILE and seq % TILE == 0 and seq % GRID_W == 0
    ix, iy, ic = lax.axis_index("x"), lax.axis_index("y"), lax.axis_index("c")
    me = 4 * ix + 2 * iy + ic
    x2, ctx2, tgt2 = x[0], ctx[0], loss_target[0]
    w_mod_l, w_ukv_l, w_out_l = w_mod[0], w_ukv[0], w_out[0]
    c_ctx_row = c_ctx.reshape(1, D_MODEL)

    tr = lambda a: jnp.transpose(a[0])
    w_in_tl, w_uq_tl = tr(w_in), tr(w_uq)
    cg, modg, wg_in, wg_uq, wg_ukv = _prologue(
        jnp.broadcast_to(c, (8, D_MODEL)), jnp.broadcast_to(c_ctx_row, (8, D_MODEL)), w_mod_l,
        [w_in_tl, w_uq_tl, w_ukv_l])
    mod_all = jnp.transpose(modg, (1, 0, 2)).reshape(16, 3 * D_MODEL)
    mod_b = lax.dynamic_slice_in_dim(mod_all, me, 1, axis=0).reshape(3, D_MODEL)
    mod_c = mod_all[8].reshape(3, D_MODEL)
    modrows = jnp.concatenate([mod_b, mod_c[0:2], jnp.zeros((3, D_MODEL), F32)], axis=0)
    b3 = b_mod.reshape(3, D_MODEL)
    bmodrows = jnp.concatenate([b3, b3[0:2], jnp.zeros((3, D_MODEL), F32)], axis=0)

    w_in_t = wg_in.reshape(D_IN_PROJ, D_MODEL)
    w_in_p = jnp.concatenate([w_in_t[448:], w_in_t[0:384], w_in_t[384:448],
                              jnp.zeros((U_PAD - D_IN_PROJ, D_MODEL), BF16)], axis=0)
    w_uq_p = jnp.concatenate([wg_uq.reshape(N_HEADS, D_HEAD, R_Q), jnp.zeros((N_HEADS, D_HEAD_PAD - D_HEAD, R_Q), BF16)],
                             axis=1).reshape(N_HEADS * D_HEAD_PAD, R_Q)
    w_ukv_f = jnp.transpose(wg_ukv, (1, 0, 2)).reshape(R_KV, N_HEADS * 256)
    qng_p = jnp.concatenate([q_norm_g, jnp.zeros((1, D_HEAD_PAD - D_HEAD), F32)], axis=1)
    kng_p = jnp.concatenate([k_norm_g, jnp.zeros((1, D_HEAD_PAD - D_HEAD), F32)], axis=1)
    cos, slo, shi = _rope_tables(seq)

    u_gates, u_mla, q, k, v = _inproj_fwd(x2, ctx2, modrows, bmodrows, norm_g, w_in_p, q_lora_g, w_uq_p, kv_lora_g, w_ukv_f,
                             qng_p, kng_p, cos, slo, shi)
    attn, lse, wg_out = _attn_fwd(q, k, v, min(2048, seq), w_out_l.astype(BF16))
    (loss_acc, g1, dgate, d_attn, dga, dgp, dpl, gwo_b, gwp, dps) = _mix(
        x2, tgt2, attn, u_gates, modrows, bmodrows, w_pool[0], pool_scale, wg_out.reshape(D_MODEL, D_MODEL))

    blocks = lambda a: a.reshape((N_DEV, a.shape[0] // N_DEV) + a.shape[1:])
    dq, dk, dv, land_wo, land_wp = _attn_bwd(q, k, v, attn, lse, d_attn.reshape(seq, D_ATTN), min(1024, seq), blocks(gwo_b),
                                             gwp.reshape(4 * GROUP_DIM, GROUP_DIM))
    (grad_x, gwin_p, gwuq_p, gwukv_t, dqlg, dkvlg, dqng, dkng, dng, dmod4) = _inproj_bwd(
        x2, ctx2, modrows, bmodrows, norm_g, w_in_p, q_lora_g, w_uq_p, kv_lora_g, w_ukv_f, qng_p, kng_p, cos, slo, shi,
        u_mla, dq, dk, dv, dga, dgp, dpl, g1)

    gwin_t = jnp.concatenate([gwin_p[O_CQ:O_KR], gwin_p[O_KR:D_IN_PROJ], gwin_p[0:O_CQ]], axis=0)
    gwuq_t = gwuq_p.reshape(N_HEADS, D_HEAD_PAD, R_Q)[:, 0:D_HEAD].reshape(N_HEADS * D_HEAD, R_Q)
    parts = [blocks(gwin_t), blocks(gwuq_t), blocks(gwukv_t)]
    (r_win, r_wuq, r_wukv, r_wout, g_w_pool, ccg, g_ng, g_qlg, g_kvlg, g_qng, g_kng, g_ps, g_bmod, dmod_my,
     loss_rows) = _epilogue(parts, [land_wo, land_wp], [dng, dqlg, dkvlg, dqng, dkng, dps], dmod4, dgate, loss_acc, w_mod_l)

    g_w_ukv = jnp.transpose(r_wukv)
    c_rows = cg[:, 0, :]
    c_all_t = jnp.transpose(jnp.concatenate([c_rows, c_ctx_row, jnp.zeros((16 - N_DEV - 1, D_MODEL), F32)], axis=0))

    weights = [c_ctx_row, w_mod_l, b_mod, norm_g, w_in_tl, q_lora_g, w_uq_tl, kv_lora_g, w_ukv_l, q_norm_g, k_norm_g,
               w_pool.reshape(4 * GROUP_DIM, GROUP_DIM), pool_scale, w_out_l]
    grads = [None, None, g_bmod, g_ng, r_win, g_qlg, r_wuq, g_kvlg, g_w_ukv, g_qng, g_kng, g_w_pool, g_ps, r_wout]
    ms = [m_c_ctx.reshape(1, D_MODEL), m_w_mod[0], m_b_mod, m_norm_g, tr(m_w_in), m_q_lora_g, tr(m_w_uq), m_kv_lora_g,
          m_w_ukv[0], m_q_norm_g, m_k_norm_g, m_w_pool.reshape(4 * GROUP_DIM, GROUP_DIM), m_pool_scale, m_w_out[0]]
    vs = [v_c_ctx.reshape(1, D_MODEL), v_w_mod[0], v_b_mod, v_norm_g, tr(v_w_in), v_q_lora_g, tr(v_w_uq), v_kv_lora_g,
          v_w_ukv[0], v_q_norm_g, v_k_norm_g, v_w_pool.reshape(4 * GROUP_DIM, GROUP_DIM), v_pool_scale, v_w_out[0]]
    outs = _adamw(weights, grads, ms, vs, c_all_t, dmod_my, ccg)
    grads[0], grads[1] = outs[0], outs[1]
    n = len(weights)
    deltas, new_m, new_v = outs[2:2 + n], outs[2 + n:2 + 2 * n], outs[2 + 2 * n:2 + 3 * n]

    loss = loss_rows[ROW_LOSS - 8, 0]
    final = [c_ctx.shape, w_mod.shape, b_mod.shape, norm_g.shape, w_in.shape, q_lora_g.shape, w_uq.shape, kv_lora_g.shape,
             w_ukv.shape, q_norm_g.shape, k_norm_g.shape, w_pool.shape, pool_scale.shape, w_out.shape]
    transposed = (4, 6)

    def shaped(arrs):
        return [(jnp.transpose(a) if i in transposed else a).reshape(s) for i, (a, s) in enumerate(zip(arrs, final))]

    return (loss, grad_x[None], *shaped(grads), *shaped(deltas), *shaped(new_m), *shaped(new_v))
```

```python
import numpy as np

import jax
import jax.numpy as jnp
from jax import lax
from jax.experimental import pallas as pl
from jax.experimental.pallas import tpu as pltpu

F32 = jnp.float32
BF16 = jnp.bfloat16
MESH = pl.DeviceIdType.MESH
HIGHEST = lax.Precision.HIGHEST

D_MODEL = 1024
N_HEADS = 4
D_NOPE = 128
D_ROPE = 64
D_HEAD = D_NOPE + D_ROPE
D_HEAD_PAD = 256
D_V = 128
R_Q = 256
R_KV = 128
D_ATTN = 512
D_POOL = 512
POOL_WINDOWS = (2, 4, 8, 16)
GROUP_DIM = 128
GRID_W = 64
ROPE_BASE = 10000.0
NORM_EPS = 1e-6
ATTN_SCALE = D_HEAD ** -0.5
LOG2_E = 1.4426950408889634
LN_2 = 0.6931471805599453
ATTN_ROWS = 256
D_IN_PROJ = 1984
U_PAD = 2048
O_GA, O_PIN, O_GP, O_CQ, O_CKV, O_KR = 0, 512, 1024, 1536, 1792, 1920
TILE = 256
Q_BLOCK = 128
HALO = 16
N_GATES = 1536
N_DEV = 8
VMEM_LIMIT = 56 * 1024 * 1024

ADAM_LR = 0.001
ADAM_B1 = 0.9
ADAM_B2 = 0.999
ADAM_EPS = 1e-08
ADAM_WD = 0.01
ADAM_STEP = 10

SMALL_ROWS = 16


def _dot(a, b):
    return lax.dot_general(a, b, (((1,), (0,)), ((), ())), preferred_element_type=F32)


def _dot_nt(a, b):
    return lax.dot_general(a, b, (((1,), (1,)), ((), ())), preferred_element_type=F32)


def _dot_tn(a, b):
    return lax.dot_general(a, b, (((0,), (0,)), ((), ())), preferred_element_type=F32)


def _sigmoid(x):
    return 1.0 / (1.0 + jnp.exp(-x))


def _rope(p, cos, slo, shi):
    return p * cos + pltpu.roll(p, 112, 1) * slo + pltpu.roll(p, 16, 1) * shi


def _rope_t(d, cos, slo, shi):
    return d * cos + pltpu.roll(d * slo, 16, 1) + pltpu.roll(d * shi, 112, 1)


def _shift_rows(a, k):
    n = a.shape[0]
    return pltpu.roll(a, (-k) % n, 0)


def _window_sum(x, w, transposed):
    s = (x + _shift_rows(x, 1)) if transposed else (_shift_rows(x, -1) + x)
    step = 1
    while 2 * step < w:
        s = _shift_rows(s, -step) + _shift_rows(s, step)
        step *= 2
    return s


def _inv_count(tpos, w, seq):
    lo = jnp.maximum(tpos - w // 2, 0)
    hi = jnp.minimum(tpos - w // 2 + w, seq)
    return 1.0 / jnp.maximum(hi - lo, 1).astype(F32)


def _coords():
    return lax.axis_index("x"), lax.axis_index("y"), lax.axis_index("c")


def _flip(v, bit):
    return (1 - v) if bit else v


def _peer(k):
    x, y, c = _coords()
    return (_flip(x, (k >> 2) & 1), _flip(y, (k >> 1) & 1), _flip(c, k & 1))


def _lin(p):
    return 4 * p[0] + 2 * p[1] + p[2]


def _gather_to_all(src_ref, slots_ref, send_sems, recv_sems):
    me = _coords()
    copies = []
    for k in range(1, N_DEV):
        cp = pltpu.make_async_remote_copy(
            src_ref=src_ref, dst_ref=slots_ref.at[_lin(me)], send_sem=send_sems.at[k - 1], recv_sem=recv_sems.at[k - 1],
            device_id=_peer(k), device_id_type=MESH)
        cp.start()
        copies.append(cp)

    def wait_recv():
        for k in range(1, N_DEV):
            pltpu.make_async_remote_copy(
                src_ref=src_ref, dst_ref=slots_ref.at[_lin(_peer(k))], send_sem=send_sems.at[k - 1],
                recv_sem=recv_sems.at[k - 1], device_id=_peer(k), device_id_type=MESH).wait_recv()

    def wait_send():
        for cp in copies:
            cp.wait_send()

    return wait_recv, wait_send


def _prologue(c8, cctx8, w_mod_l, shards):
    n_mod = w_mod_l.shape[1]
    n_w = len(shards)

    def body(c_ref, cctx_ref, wmod_ref, *refs):
        w_refs = refs[0:n_w]
        cg_ref, modg_ref = refs[n_w], refs[n_w + 1]
        wg_refs = refs[n_w + 2:2 * n_w + 2]
        modblk_ref = refs[2 * n_w + 2]
        wb_refs = refs[2 * n_w + 3:3 * n_w + 3]
        c8_ref = refs[3 * n_w + 3]
        ssem_w, rsem_w, ssem_c, rsem_c, ssem_m, rsem_m = refs[3 * n_w + 4:]
        x, y, c = _coords()
        me = (x, y, c)
        sibling = (x, y, 1 - c)
        chips = [(1 - x, y), (x, 1 - y), (1 - x, 1 - y)]

        def wcopies(k, block, to, own=False):
            out = []
            for a in range(n_w):
                slot = wg_refs[a].at[_lin(block)]
                out.append(pltpu.make_async_remote_copy(
                    src_ref=wb_refs[a] if own else slot, dst_ref=slot, send_sem=ssem_w.at[a, k], recv_sem=rsem_w.at[a, k],
                    device_id=to, device_id_type=MESH))
            return out

        for a in range(n_w):
            wb_refs[a][...] = w_refs[a][...].astype(BF16)
        first = wcopies(0, me, sibling, own=True)
        for j, chip in enumerate(chips):
            first += wcopies(1 + j, me, (*chip, c), own=True)
        for cp in first:
            cp.start()
        for a in range(n_w):
            wg_refs[a][_lin(me)] = wb_refs[a][...]

        c8_ref[...] = jnp.broadcast_to(c_ref[...], (8, D_MODEL))
        cg_ref[_lin(me)] = c8_ref[...]
        c_recv, c_send = _gather_to_all(c8_ref, cg_ref, ssem_c, rsem_c)
        c_recv()
        row = lax.broadcasted_iota(jnp.int32, (8, D_MODEL), 0)
        c_all = jnp.zeros((8, D_MODEL), F32)
        for d in range(N_DEV):
            c_all = c_all + jnp.where(row == d, cg_ref[d], 0.0)
        cc = jnp.broadcast_to(cctx_ref[...], (8, D_MODEL))
        a = jnp.concatenate([c_all * _sigmoid(c_all), cc * _sigmoid(cc)], axis=0)
        modblk_ref[...] = lax.dot_general(a, wmod_ref[...], (((1,), (0,)), ((), ())), precision=HIGHEST,
                                          preferred_element_type=F32)
        modg_ref[_lin(me)] = modblk_ref[...]
        m_recv, m_send = _gather_to_all(modblk_ref, modg_ref, ssem_m, rsem_m)
        m_recv()

        passed = []
        for j, chip in enumerate(chips):
            for cp in wcopies(1 + j, (*chip, c), me):
                cp.wait_recv()
            fwd = wcopies(4 + j, (*chip, c), sibling)
            for cp in fwd:
                cp.start()
            passed += fwd
        for cp in wcopies(0, sibling, me):
            cp.wait_recv()
        for j, chip in enumerate(chips):
            for cp in wcopies(4 + j, (*chip, 1 - c), me):
                cp.wait_recv()
        for cp in first + passed:
            cp.wait_send()
        c_send()
        m_send()

    vm = pl.BlockSpec(memory_space=pltpu.VMEM)
    return pl.pallas_call(
        body, name="prologue_gather",
        out_shape=(jax.ShapeDtypeStruct((N_DEV, 8, D_MODEL), F32), jax.ShapeDtypeStruct((N_DEV, 16, n_mod), F32),
                   *[jax.ShapeDtypeStruct((N_DEV,) + s.shape, BF16) for s in shards]),
        in_specs=[vm] * (3 + n_w), out_specs=tuple([vm] * (2 + n_w)),
        scratch_shapes=[pltpu.VMEM((16, n_mod), F32), *[pltpu.VMEM(s.shape, BF16) for s in shards],
                        pltpu.VMEM((8, D_MODEL), F32),
                        pltpu.SemaphoreType.DMA((n_w, 7)), pltpu.SemaphoreType.DMA((n_w, 7)),
                        pltpu.SemaphoreType.DMA((7,)), pltpu.SemaphoreType.DMA((7,)),
                        pltpu.SemaphoreType.DMA((7,)), pltpu.SemaphoreType.DMA((7,))],
        compiler_params=pltpu.CompilerParams(vmem_limit_bytes=VMEM_LIMIT),
    )(c8, cctx8, w_mod_l, *shards)


def _modulated_input(is_ctx, x_ref, ctx_ref, mod_ref, bmod_ref, ng_ref):
    xt = jnp.where(is_ctx, ctx_ref[...], x_ref[...])
    shift = jnp.where(is_ctx, mod_ref[3:4, :] + bmod_ref[3:4, :], mod_ref[0:1, :] + bmod_ref[0:1, :])
    scale = jnp.where(is_ctx, mod_ref[4:5, :] + bmod_ref[4:5, :], mod_ref[1:2, :] + bmod_ref[1:2, :])
    r = lax.rsqrt(jnp.mean(xt * xt, axis=-1, keepdims=True) + NORM_EPS)
    xg = (xt * r) * ng_ref[...]
    h = xg * (1.0 + scale) + shift
    return xt, r, xg, h, scale


def _inproj_fwd(x, ctx, modrows, bmodrows, norm_g, w_in_p, q_lora_g, w_uq_p, kv_lora_g, w_ukv, qng_p, kng_p, cos, slo, shi):
    seq = x.shape[0]
    n_tiles = seq // TILE + 1
    tot = seq + TILE

    n_mla = R_Q + R_KV + 128

    def body(x_ref, ctx_ref, mod_ref, bmod_ref, ng_ref, win_ref, qlg_ref, wuq_ref, kvlg_ref, wukv_ref, qng_ref, kng_ref,
             cos_ref, slo_ref, shi_ref, ug_ref, um_ref, q_ref, k_ref, v_ref, stash_even, stash_odd):
        s = pl.program_id(0)

        @pl.when(s == 0)
        def _():
            stash_odd[...] = jnp.zeros_like(stash_odd)

        refs = (x_ref, ctx_ref, mod_ref, bmod_ref, ng_ref, win_ref, qlg_ref, wuq_ref, kvlg_ref, wukv_ref, qng_ref, kng_ref,
                cos_ref, slo_ref, shi_ref, ug_ref, um_ref, q_ref, k_ref, v_ref)

        @pl.when(lax.rem(s, 2) == 0)
        def _():
            stages(s, refs, stash_even, stash_odd)

        @pl.when(lax.rem(s, 2) == 1)
        def _():
            stages(s, refs, stash_odd, stash_even)

    def stages(s, refs, fill, prev_ref):
        (x_ref, ctx_ref, mod_ref, bmod_ref, ng_ref, win_ref, qlg_ref, wuq_ref, kvlg_ref, wukv_ref, qng_ref, kng_ref,
         cos_ref, slo_ref, shi_ref, ug_ref, um_ref, q_ref, k_ref, v_ref) = refs
        cos_t, slo_t, shi_t = cos_ref[...], slo_ref[...], shi_ref[...]
        prev = prev_ref[...]
        cq = prev[:, 0:R_Q]
        qn = cq * lax.rsqrt(jnp.mean(cq * cq, axis=-1, keepdims=True) + NORM_EPS) * qlg_ref[...]
        q = _dot_nt(qn.astype(BF16), wuq_ref[...])
        qg = qng_ref[...] * (ATTN_SCALE * LOG2_E)
        for hd in range(N_HEADS):
            qh = q[:, hd * D_HEAD_PAD:(hd + 1) * D_HEAD_PAD]
            rr = lax.rsqrt(jnp.sum(qh * qh, axis=-1, keepdims=True) * (1.0 / D_HEAD) + NORM_EPS)
            qy = qh * rr * qg
            q_ref[hd, :, 0:D_NOPE] = qy[:, 0:D_NOPE].astype(BF16)
            q_ref[hd, :, D_NOPE:D_HEAD_PAD] = _rope(qy[:, D_NOPE:D_HEAD_PAD], cos_t, slo_t, shi_t).astype(BF16)

        ckv = prev[:, R_Q:R_Q + R_KV]
        kvn = ckv * lax.rsqrt(jnp.mean(ckv * ckv, axis=-1, keepdims=True) + NORM_EPS) * kvlg_ref[...]
        kv = _dot(kvn.astype(BF16), wukv_ref[...])
        krz = prev[:, R_Q + R_KV:n_mla]
        kr_ss = jnp.sum(krz * krz, axis=-1, keepdims=True)
        kg = kng_ref[...]
        for hd in range(N_HEADS):
            kn = kv[:, hd * 256:hd * 256 + D_NOPE]
            rr = lax.rsqrt((jnp.sum(kn * kn, axis=-1, keepdims=True) + kr_ss) * (1.0 / D_HEAD) + NORM_EPS)
            k_ref[hd, :, 0:D_NOPE] = (kn * rr * kg[:, 0:D_NOPE]).astype(BF16)
            k_ref[hd, :, D_NOPE:D_HEAD_PAD] = _rope(krz * rr * kg[:, D_NOPE:D_HEAD_PAD], cos_t, slo_t, shi_t).astype(BF16)
            v_ref[hd] = kv[:, hd * 256 + D_NOPE:(hd + 1) * 256].astype(BF16)

        _, _, _, h, _ = _modulated_input(s == 0, x_ref, ctx_ref, mod_ref, bmod_ref, ng_ref)
        u = _dot_nt(h.astype(BF16), win_ref[...])
        ug_ref[...] = u[:, 0:N_GATES].astype(BF16)
        um_ref[...] = u[:, N_GATES:U_PAD]
        fill[...] = u[:, N_GATES:U_PAD]

    first = lambda s: jnp.minimum(s, n_tiles - 1)
    second = lambda s: jnp.maximum(s - 1, 0)
    latent = lambda t: jnp.maximum(t - 1, 0)
    full = lambda shape: pl.BlockSpec(shape, lambda s: (0,) * len(shape))
    rope_spec = pl.BlockSpec((TILE, 128), lambda s: (second(s), 0))
    return pl.pallas_call(
        body, name="inproj_fwd", grid=(n_tiles + 1,),
        out_shape=(jax.ShapeDtypeStruct((seq, N_GATES), BF16), jax.ShapeDtypeStruct((tot, n_mla), F32),
                   jax.ShapeDtypeStruct((N_HEADS, seq, D_HEAD_PAD), BF16),
                   jax.ShapeDtypeStruct((N_HEADS, tot, D_HEAD_PAD), BF16),
                   jax.ShapeDtypeStruct((N_HEADS, tot, D_V), BF16)),
        in_specs=[pl.BlockSpec((TILE, D_MODEL), lambda s: (latent(first(s)), 0)), full((TILE, D_MODEL)), full((8, D_MODEL)),
                  full((8, D_MODEL)), full((1, D_MODEL)), full((U_PAD, D_MODEL)), full((1, R_Q)),
                  full((N_HEADS * D_HEAD_PAD, R_Q)), full((1, R_KV)), full((R_KV, N_HEADS * 256)), full((1, D_HEAD_PAD)),
                  full((1, D_HEAD_PAD)), rope_spec, rope_spec, rope_spec],
        out_specs=(pl.BlockSpec((TILE, N_GATES), lambda s: (latent(first(s)), 0)),
                   pl.BlockSpec((TILE, n_mla), lambda s: (first(s), 0)),
                   pl.BlockSpec((N_HEADS, TILE, D_HEAD_PAD), lambda s: (0, latent(second(s)), 0)),
                   pl.BlockSpec((N_HEADS, TILE, D_HEAD_PAD), lambda s: (0, second(s), 0)),
                   pl.BlockSpec((N_HEADS, TILE, D_V), lambda s: (0, second(s), 0))),
        scratch_shapes=[pltpu.VMEM((TILE, n_mla), F32), pltpu.VMEM((TILE, n_mla), F32)],
        compiler_params=pltpu.CompilerParams(dimension_semantics=("arbitrary",), vmem_limit_bytes=VMEM_LIMIT),
    )(x, ctx, modrows, bmodrows, norm_g, w_in_p, q_lora_g, w_uq_p, kv_lora_g, w_ukv, qng_p, kng_p, cos, slo, shi)


def _hosted_exchange(first, last, items, send_sems, recv_sems, local_sems):
    me = _lin(_coords())

    def remote(n, k, src, land, scatter):
        peer = _peer(k)
        return pltpu.make_async_remote_copy(
            src_ref=src.at[_lin(peer)] if scatter else src, dst_ref=land.at[me], send_sem=send_sems.at[n, k - 1],
            recv_sem=recv_sems.at[n, k - 1], device_id=peer, device_id_type=MESH)

    def local(n, src, land, scatter):
        return pltpu.make_async_copy(src.at[me] if scatter else src, land.at[me], local_sems.at[n])

    @pl.when(first)
    def _():
        for n, (src, land, scatter) in enumerate(items):
            for k in range(1, N_DEV):
                remote(n, k, src, land, scatter).start()
            local(n, src, land, scatter).start()

    @pl.when(last)
    def _():
        for n, (src, land, scatter) in enumerate(items):
            for k in range(1, N_DEV):
                peer = _peer(k)
                pltpu.make_async_remote_copy(
                    src_ref=src.at[0] if scatter else src, dst_ref=land.at[_lin(peer)], send_sem=send_sems.at[n, k - 1],
                    recv_sem=recv_sems.at[n, k - 1], device_id=peer, device_id_type=MESH).wait_recv()
            for k in range(1, N_DEV):
                remote(n, k, src, land, scatter).wait_send()
            local(n, src, land, scatter).wait()


def _attn_fwd(q, k, v, block_q, w_out_b):
    _, seq, _ = q.shape
    n_keys = k.shape[1]
    n_q = seq // block_q

    def body(q_ref, k_ref, v_ref, wo_ref, o_ref, lse_ref, wog_ref, ssem, rsem, lsem):
        h, i = pl.program_id(0), pl.program_id(1)
        _hosted_exchange((h == 0) & (i == 0), (h == N_HEADS - 1) & (i == n_q - 1), [(wo_ref, wog_ref, False)],
                         ssem, rsem, lsem)
        kb, vb = k_ref[0], v_ref[0]
        for r0 in range(0, block_q, ATTN_ROWS):
            rows = slice(r0, r0 + ATTN_ROWS)
            s = _dot_nt(q_ref[0, rows, :], kb)
            m = jnp.max(s, axis=-1, keepdims=True)
            p = jnp.exp2(s - m)
            l = jnp.sum(p, axis=-1, keepdims=True)
            o_ref[rows, :] = _dot(p.astype(BF16), vb) * (1.0 / l)
            lse_ref[0, rows, :] = m + jnp.log2(l)

    hbm = pl.BlockSpec(memory_space=pl.ANY)
    return pl.pallas_call(
        body, name="attn_fwd", grid=(N_HEADS, n_q),
        out_shape=(jax.ShapeDtypeStruct((seq, D_ATTN), F32), jax.ShapeDtypeStruct((N_HEADS, seq, 1), F32),
                   jax.ShapeDtypeStruct((N_DEV,) + w_out_b.shape, w_out_b.dtype)),
        in_specs=[pl.BlockSpec((1, block_q, D_HEAD_PAD), lambda h, i: (h, i, 0)),
                  pl.BlockSpec((1, n_keys, D_HEAD_PAD), lambda h, i: (h, 0, 0)),
                  pl.BlockSpec((1, n_keys, D_V), lambda h, i: (h, 0, 0)), hbm],
        out_specs=(pl.BlockSpec((block_q, D_V), lambda h, i: (i, h)),
                   pl.BlockSpec((1, block_q, 1), lambda h, i: (h, i, 0)), hbm),
        scratch_shapes=[pltpu.SemaphoreType.DMA((1, 7)), pltpu.SemaphoreType.DMA((1, 7)), pltpu.SemaphoreType.DMA((1,))],
        compiler_params=pltpu.CompilerParams(dimension_semantics=("arbitrary", "arbitrary"), vmem_limit_bytes=VMEM_LIMIT),
    )(q, k, v, w_out_b)


def _attn_bwd(q, k, v, o, lse, d_o, block_q, gwo_blocks, gwp):
    _, seq, _ = q.shape
    n_keys = k.shape[1]
    n_q = seq // block_q

    def body(q_ref, k_ref, v_ref, o_ref, lse_ref, do_ref, gwo_ref, gwp_ref, dq_ref, dkt_ref, dvt_ref, lwo_ref, lwp_ref,
             ssem, rsem, lsem):
        h, i = pl.program_id(0), pl.program_id(1)
        _hosted_exchange((h == 0) & (i == 0), (h == N_HEADS - 1) & (i == n_q - 1),
                         [(gwo_ref, lwo_ref, True), (gwp_ref, lwp_ref, False)], ssem, rsem, lsem)

        @pl.when(i == 0)
        def _():
            dkt_ref[...] = jnp.zeros_like(dkt_ref)
            dvt_ref[...] = jnp.zeros_like(dvt_ref)

        kb, vb = k_ref[0], v_ref[0]
        raws, ps = [], []
        for r0 in range(0, block_q, ATTN_ROWS):
            rows = slice(r0, r0 + ATTN_ROWS)
            d_out = do_ref[rows, :]
            p = jnp.exp2(_dot_nt(q_ref[0, rows, :], kb) - lse_ref[0, rows, :])
            dp = _dot_nt(d_out.astype(BF16), vb)
            delta = jnp.sum(d_out * o_ref[rows, :], axis=-1, keepdims=True)
            raw = (p * (dp - delta)).astype(BF16)
            dq_ref[0, rows, :] = _dot(raw, kb)
            raws.append(raw)
            ps.append(p.astype(BF16))
        dkt_ref[0] += _dot_tn(q_ref[0], jnp.concatenate(raws, axis=0))
        dvt_ref[0] += _dot_tn(do_ref[...].astype(BF16), jnp.concatenate(ps, axis=0))

    hbm = pl.BlockSpec(memory_space=pl.ANY)
    return pl.pallas_call(
        body, name="attn_bwd", grid=(N_HEADS, n_q),
        out_shape=(jax.ShapeDtypeStruct((N_HEADS, seq, D_HEAD_PAD), F32),
                   jax.ShapeDtypeStruct((N_HEADS, D_HEAD_PAD, n_keys), F32),
                   jax.ShapeDtypeStruct((N_HEADS, D_V, n_keys), F32),
                   jax.ShapeDtypeStruct(gwo_blocks.shape, gwo_blocks.dtype),
                   jax.ShapeDtypeStruct((N_DEV,) + gwp.shape, gwp.dtype)),
        in_specs=[pl.BlockSpec((1, block_q, D_HEAD_PAD), lambda h, i: (h, i, 0)),
                  pl.BlockSpec((1, n_keys, D_HEAD_PAD), lambda h, i: (h, 0, 0)),
                  pl.BlockSpec((1, n_keys, D_V), lambda h, i: (h, 0, 0)),
                  pl.BlockSpec((block_q, D_V), lambda h, i: (i, h)),
                  pl.BlockSpec((1, block_q, 1), lambda h, i: (h, i, 0)),
                  pl.BlockSpec((block_q, D_V), lambda h, i: (i, h)), hbm, hbm],
        out_specs=(pl.BlockSpec((1, block_q, D_HEAD_PAD), lambda h, i: (h, i, 0)),
                   pl.BlockSpec((1, D_HEAD_PAD, n_keys), lambda h, i: (h, 0, 0)),
                   pl.BlockSpec((1, D_V, n_keys), lambda h, i: (h, 0, 0)), hbm, hbm),
        scratch_shapes=[pltpu.SemaphoreType.DMA((2, 7)), pltpu.SemaphoreType.DMA((2, 7)), pltpu.SemaphoreType.DMA((2,))],
        compiler_params=pltpu.CompilerParams(dimension_semantics=("arbitrary", "arbitrary"), vmem_limit_bytes=VMEM_LIMIT),
    )(q, k, v, o, lse, d_o, gwo_blocks, gwp)


def _mix(x, target, attn, u, modrows, bmodrows, w_pool, pool_scale, w_out):
    seq = x.shape[0]
    n_tiles = seq // TILE
    rows8 = TILE // HALO
    last8 = seq // HALO - 1

    n_blk = seq // Q_BLOCK
    per = TILE // n_blk
    assert TILE % n_blk == 0 and per % 8 == 0 and n_tiles >= 2
    n_stash = 8

    def body(x_ref, t_ref, o_hbm, ga_ref, pin_ref, hb_ref, ha_ref, gp_ref, mod_ref, bmod_ref, wp_ref, ps_ref, wo_ref,
             loss_ref, g1_ref, dgate_ref, do_hbm, dga_ref, dgp_ref, dpl_ref, gwob_ref, gwp_ref, dps_ref,
             abuf, dbuf, gwo_ref, *rest):
        stash_even, stash_odd = rest[0:n_stash], rest[n_stash:2 * n_stash]
        rsem, wsem = rest[2 * n_stash:]
        s = pl.program_id(0)

        def slab_copies(tile, slot, fetch):
            window = pl.ds(tile * per, per)
            if fetch:
                return [pltpu.make_async_copy(o_hbm.at[:, window, :], abuf.at[slot], rsem.at[slot])]
            return [pltpu.make_async_copy(dbuf.at[slot], do_hbm.at[:, window, :], wsem.at[slot])]

        def wait_all(slot, fetch):
            slab_copies(0, slot, fetch)[0].wait()

        a_slot = lax.rem(s, 2)
        b_slot = 1 - a_slot

        @pl.when(s == 0)
        def _():
            loss_ref[...] = jnp.zeros_like(loss_ref)
            dgate_ref[...] = jnp.zeros_like(dgate_ref)
            gwo_ref[...] = jnp.zeros_like(gwo_ref)
            gwp_ref[...] = jnp.zeros_like(gwp_ref)
            dps_ref[...] = jnp.zeros_like(dps_ref)
            for cp in slab_copies(0, 0, True):
                cp.start()

        @pl.when(s + 1 < n_tiles)
        def _():
            for cp in slab_copies(s + 1, b_slot, True):
                cp.start()

        @pl.when(s < n_tiles)
        def _():
            wait_all(a_slot, True)

        @pl.when(s >= 3)
        def _():
            wait_all(b_slot, False)

        gate = mod_ref[2:3, :] + bmod_ref[2:3, :]
        ps = ps_ref[...]

        def a_vector(slot):
            attn_t = jnp.swapaxes(abuf[slot], 0, 1).reshape(TILE, D_ATTN)
            ga = ga_ref[...].astype(F32)
            sga = _sigmoid(ga)
            silu_ga = ga * sga
            pin = pin_ref[...].astype(F32)
            xs = jnp.concatenate([jnp.where(s > 0, hb_ref[...].astype(F32), 0.0), pin,
                                  jnp.where(s < n_tiles - 1, ha_ref[...].astype(F32), 0.0)], axis=0)
            tpos = s * TILE + lax.broadcasted_iota(jnp.int32, (TILE, 1), 0)
            pooled = []
            for g, w in enumerate(POOL_WINDOWS):
                sl = slice(g * GROUP_DIM, (g + 1) * GROUP_DIM)
                ws = _window_sum(xs[:, sl], w, False)[HALO:HALO + TILE]
                pooled.append((ws * _inv_count(tpos, w, seq) - pin[:, sl]).astype(BF16))
            return attn_t, ga, sga, silu_ga, pooled

        def a_group_maps(pooled):
            return jnp.concatenate([_dot(pooled[g], wp_ref[g].astype(BF16)) for g in range(len(POOL_WINDOWS))], axis=1)

        def a_project(stash, yp, attn_t, ga, sga, silu_ga, pooled):
            zb_ref, _, fa_ref, sa_ref, fp_ref, sp_ref, yp_ref, pooled_ref = stash
            ypool = yp * ps
            gp = gp_ref[...].astype(F32)
            sgp = _sigmoid(gp)
            silu_gp = gp * sgp
            z = jnp.concatenate([silu_ga * attn_t, silu_gp * ypool], axis=1).astype(BF16)
            y = _dot(z, wo_ref[...])
            zb_ref[...] = z
            fa_ref[...] = attn_t * (sga * (1.0 + ga * (1.0 - sga)))
            sa_ref[...] = silu_ga
            fp_ref[...] = ypool * (sgp * (1.0 + gp * (1.0 - sgp)))
            sp_ref[...] = silu_gp
            yp_ref[...] = yp
            pooled_ref[...] = jnp.concatenate(pooled, axis=1)
            return y

        def a_loss(stash, y):
            res = x_ref[...] + gate * y - t_ref[...]
            loss_ref[...] += jnp.sum(res * res) * (0.5 / D_MODEL)
            dxn = res * (1.0 / D_MODEL)
            g1_ref[...] = dxn
            dgate_ref[...] += jnp.sum(dxn * y, axis=0, keepdims=True)
            stash[1][...] = (gate * dxn).astype(BF16)

        def b_dz(stash):
            return _dot_nt(stash[1][...], wo_ref[...])

        def b_gwo(stash):
            gwo_ref[...] += _dot_tn(stash[0][...], stash[1][...])

        def b_vector(stash, slot, dz):
            _, _, fa_ref, sa_ref, fp_ref, sp_ref, yp_ref, _ = stash
            dbra = dz[:, 0:D_ATTN]
            dbrp = dz[:, D_ATTN:]
            dga_ref[...] = (dbra * fa_ref[...]).astype(BF16)
            dbuf[slot] = jnp.swapaxes((dbra * sa_ref[...]).reshape(per, n_blk, D_ATTN), 0, 1)
            dgp_ref[...] = (dbrp * fp_ref[...]).astype(BF16)
            dyp_s = dbrp * sp_ref[...]
            dps_ref[...] += jnp.sum(dyp_s * yp_ref[...], axis=0, keepdims=True)
            return (dyp_s * ps).astype(BF16)

        def b_small(stash, dyp):
            pooled_ref = stash[7]
            for g in range(len(POOL_WINDOWS)):
                sl = slice(g * GROUP_DIM, (g + 1) * GROUP_DIM)
                gwp_ref[g] += _dot_tn(pooled_ref[:, sl], dyp[:, sl])
                dpl_ref[:, sl] = _dot_nt(dyp[:, sl], wp_ref[g].astype(BF16)).astype(BF16)

        def both(a_stash, b_stash, slot_a):
            dz = b_dz(b_stash)
            front = a_vector(slot_a)
            yp = a_group_maps(front[-1])
            b_gwo(b_stash)
            y = a_project(a_stash, yp, *front)
            dyp = b_vector(b_stash, 1 - slot_a, dz)
            a_loss(a_stash, y)
            b_small(b_stash, dyp)

        @pl.when(s == 0)
        def _():
            front = a_vector(0)
            a_loss(stash_even, a_project(stash_even, a_group_maps(front[-1]), *front))

        @pl.when((s > 0) & (s < n_tiles) & (a_slot == 0))
        def _():
            both(stash_even, stash_odd, 0)

        @pl.when((s > 0) & (s < n_tiles) & (a_slot == 1))
        def _():
            both(stash_odd, stash_even, 1)

        @pl.when(s == n_tiles)
        def _():
            last = (n_tiles - 1) % 2
            stash = stash_odd if last else stash_even
            dz = b_dz(stash)
            b_gwo(stash)
            b_small(stash, b_vector(stash, last, dz))
            gwob_ref[...] = gwo_ref[...].astype(BF16)

        @pl.when(s >= 1)
        def _():
            for cp in slab_copies(s - 1, b_slot, False):
                cp.start()

        @pl.when(s == n_tiles)
        def _():
            wait_all(b_slot, False)
            wait_all(a_slot, False)

    ta = lambda s: jnp.minimum(s, n_tiles - 1)
    tb = lambda s: jnp.maximum(s - 1, 0)
    tile = lambda w: pl.BlockSpec((TILE, w), lambda s: (ta(s), 0))
    tile_b = lambda w: pl.BlockSpec((TILE, w), lambda s: (tb(s), 0))
    ucol = lambda col: pl.BlockSpec((TILE, 512), lambda s: (ta(s), col))
    full = lambda shape: pl.BlockSpec(shape, lambda s: (0,) * len(shape))
    stash_shapes = [pltpu.VMEM((TILE, D_MODEL), BF16), pltpu.VMEM((TILE, D_MODEL), BF16)] + \
        [pltpu.VMEM((TILE, 512), F32)] * 5 + [pltpu.VMEM((TILE, D_POOL), BF16)]
    return pl.pallas_call(
        body, name="mix_fwd_bwd", grid=(n_tiles + 1,),
        out_shape=(jax.ShapeDtypeStruct((8, 128), F32), jax.ShapeDtypeStruct((seq, D_MODEL), F32),
                   jax.ShapeDtypeStruct((1, D_MODEL), F32), jax.ShapeDtypeStruct((n_blk, Q_BLOCK, D_ATTN), F32),
                   jax.ShapeDtypeStruct((seq, D_ATTN), BF16), jax.ShapeDtypeStruct((seq, D_POOL), BF16),
                   jax.ShapeDtypeStruct((seq, D_POOL), BF16), jax.ShapeDtypeStruct((D_MODEL, D_MODEL), BF16),
                   jax.ShapeDtypeStruct((4, GROUP_DIM, GROUP_DIM), F32), jax.ShapeDtypeStruct((1, D_POOL), F32)),
        in_specs=[tile(D_MODEL), tile(D_MODEL), pl.BlockSpec(memory_space=pl.ANY), ucol(0), ucol(1),
                  pl.BlockSpec((HALO, 512), lambda s: (jnp.maximum(ta(s) * rows8 - 1, 0), 1)),
                  pl.BlockSpec((HALO, 512), lambda s: (jnp.minimum((ta(s) + 1) * rows8, last8), 1)),
                  ucol(2), full((8, D_MODEL)), full((8, D_MODEL)), full((4, GROUP_DIM, GROUP_DIM)), full((1, D_POOL)),
                  full((D_MODEL, D_MODEL))],
        out_specs=(full((8, 128)), tile(D_MODEL), full((1, D_MODEL)), pl.BlockSpec(memory_space=pl.ANY), tile_b(D_ATTN),
                   tile_b(D_POOL), tile_b(D_POOL), full((D_MODEL, D_MODEL)), full((4, GROUP_DIM, GROUP_DIM)),
                   full((1, D_POOL))),
        scratch_shapes=[pltpu.VMEM((2, n_blk, per, D_ATTN), F32), pltpu.VMEM((2, n_blk, per, D_ATTN), F32),
                        pltpu.VMEM((D_MODEL, D_MODEL), F32), *stash_shapes, *stash_shapes,
                        pltpu.SemaphoreType.DMA((2,)), pltpu.SemaphoreType.DMA((2,))],
        compiler_params=pltpu.CompilerParams(dimension_semantics=("arbitrary",), vmem_limit_bytes=VMEM_LIMIT),
    )(x, target, attn.reshape(n_blk, Q_BLOCK, D_ATTN), u, u, u, u, u, modrows, bmodrows, w_pool, pool_scale, w_out)


def _inproj_bwd(x, ctx, modrows, bmodrows, norm_g, w_in_p, q_lora_g, w_uq_p, kv_lora_g, w_ukv, qng_p, kng_p, cos, slo, shi,
                u, dq, dk, dv, dga, dgp, dpl, g1):
    seq = x.shape[0]
    n_tiles = seq // TILE + 1
    rows8 = TILE // HALO
    last8 = seq // HALO - 1

    def body(*refs):
        du_even, du_odd, dh_even, dh_odd = refs[-4:]
        gwin_ref, gwuq_ref, gwukv_ref, dqlg_ref, dkvlg_ref, dqng_ref, dkng_ref, dng_ref, dmod_ref = refs[-13:-4]
        s = pl.program_id(0)

        @pl.when(s == 0)
        def _():
            for ref in (gwin_ref, gwuq_ref, gwukv_ref, dqlg_ref, dkvlg_ref, dqng_ref, dkng_ref, dng_ref, dmod_ref,
                        du_odd, dh_even):
                ref[...] = jnp.zeros_like(ref)

        @pl.when(lax.rem(s, 2) == 0)
        def _():
            stages(s, refs[:-4], du_even, du_odd, dh_odd, dh_even)

        @pl.when(lax.rem(s, 2) == 1)
        def _():
            stages(s, refs[:-4], du_odd, du_even, dh_even, dh_odd)

    def stages(s, refs, du_ref, du_prev, dh_fill, dh_prev):
        (xb_ref, xc_ref, ctx_ref, mod_ref, bmod_ref, ng_ref, win_ref, qlg_ref, wuq_ref, kvlg_ref, wukv_ref, qng_ref, kng_ref,
         cos_ref, slo_ref, shi_ref, cq_ref, ckv_ref, kr_ref, dq_ref, dk_ref, dv_ref, dga_ref, dgp_ref, dpl_ref,
         hb_ref, ha_ref, g1_ref,
         gx_ref, gwin_ref, gwuq_ref, gwukv_ref, dqlg_ref, dkvlg_ref, dqng_ref, dkng_ref, dng_ref, dmod_ref) = refs


        i = jnp.minimum(s, n_tiles - 1)
        valid = s < n_tiles
        is_lat = (s > 0) & valid
        cq = cq_ref[...]
        rq = lax.rsqrt(jnp.mean(cq * cq, axis=-1, keepdims=True) + NORM_EPS)
        qhat = cq * rq
        qnb = (qhat * qlg_ref[...]).astype(BF16)
        q = _dot_nt(qnb, wuq_ref[...])
        ckv = ckv_ref[...]
        rkv = lax.rsqrt(jnp.mean(ckv * ckv, axis=-1, keepdims=True) + NORM_EPS)
        kvhat = ckv * rkv
        kvnb = (kvhat * kvlg_ref[...]).astype(BF16)
        kv = _dot(kvnb, wukv_ref[...])

        _, _, _, h2, _ = _modulated_input(s <= 1, xb_ref, ctx_ref, mod_ref, bmod_ref, ng_ref)
        dub = du_prev[...]
        dh_fill[...] = _dot(dub, win_ref[...])
        gwin_ref[...] += _dot_tn(dub, h2.astype(BF16))

        ctx3 = s <= 2
        xt, r, xg, _, scale = _modulated_input(ctx3, xc_ref, ctx_ref, mod_ref, bmod_ref, ng_ref)
        dh = dh_prev[...]
        dshift = jnp.sum(dh, axis=0, keepdims=True)
        dscale = jnp.sum(dh * xg, axis=0, keepdims=True)
        zero = jnp.zeros_like(dshift)
        dmod_ref[0:1, :] += jnp.where(ctx3, zero, dshift)
        dmod_ref[1:2, :] += jnp.where(ctx3, zero, dscale)
        dmod_ref[2:3, :] += jnp.where(ctx3, dshift, zero)
        dmod_ref[3:4, :] += jnp.where(ctx3, dscale, zero)
        dxg = dh * (1.0 + scale)
        xr = xt * r
        dng_ref[...] += jnp.sum(dxg * xr, axis=0, keepdims=True)
        dxn = dxg * ng_ref[...]
        gx_ref[...] = g1_ref[...] + r * (dxn - xr * jnp.mean(dxn * xr, axis=-1, keepdims=True))

        cos_t, slo_t, shi_t = cos_ref[...], slo_ref[...], shi_ref[...]
        qg = qng_ref[...]
        dq_heads = []
        dqng = jnp.zeros((1, D_HEAD_PAD), F32)
        for hd in range(N_HEADS):
            qh = q[:, hd * D_HEAD_PAD:(hd + 1) * D_HEAD_PAD]
            rr = lax.rsqrt(jnp.sum(qh * qh, axis=-1, keepdims=True) * (1.0 / D_HEAD) + NORM_EPS)
            yq = qh * rr
            dpost = jnp.where(is_lat, dq_ref[hd] * ATTN_SCALE, 0.0)
            dyn = jnp.concatenate([dpost[:, 0:D_NOPE], _rope_t(dpost[:, D_NOPE:], cos_t, slo_t, shi_t)], axis=1)
            dqng = dqng + jnp.sum(dyn * yq, axis=0, keepdims=True)
            dyg = dyn * qg
            dq_heads.append(rr * (dyg - yq * (jnp.sum(dyg * yq, axis=-1, keepdims=True) * (1.0 / D_HEAD))))
        dqng_ref[...] += dqng
        dqf = jnp.concatenate(dq_heads, axis=1).astype(BF16)

        krz = kr_ref[...]
        kr_ss = jnp.sum(krz * krz, axis=-1, keepdims=True)
        kg = kng_ref[...]
        kg_n, kg_r = kg[:, 0:D_NOPE], kg[:, D_NOPE:]
        dkv_parts = []
        dkr = jnp.zeros((TILE, 128), F32)
        dkng_n = jnp.zeros((1, D_NOPE), F32)
        dkng_r = jnp.zeros((1, 128), F32)
        for hd in range(N_HEADS):
            kn = kv[:, hd * 256:hd * 256 + D_NOPE]
            rr = lax.rsqrt((jnp.sum(kn * kn, axis=-1, keepdims=True) + kr_ss) * (1.0 / D_HEAD) + NORM_EPS)
            yn, yr = kn * rr, krz * rr
            dk_h = jnp.where(valid, jnp.transpose(dk_ref[hd]) * LN_2, 0.0)
            dpn = dk_h[:, 0:D_NOPE]
            dpr = _rope_t(dk_h[:, D_NOPE:], cos_t, slo_t, shi_t)
            dkng_n = dkng_n + jnp.sum(dpn * yn, axis=0, keepdims=True)
            dkng_r = dkng_r + jnp.sum(dpr * yr, axis=0, keepdims=True)
            dyn, dyr = dpn * kg_n, dpr * kg_r
            proj = (jnp.sum(dyn * yn, axis=-1, keepdims=True) + jnp.sum(dyr * yr, axis=-1, keepdims=True)) * (1.0 / D_HEAD)
            dkv_parts.append(rr * (dyn - yn * proj))
            dkv_parts.append(jnp.where(valid, jnp.transpose(dv_ref[hd]), 0.0))
            dkr = dkr + rr * (dyr - yr * proj)
        dkng_ref[:, 0:D_NOPE] += dkng_n
        dkng_ref[:, D_NOPE:] += dkng_r
        dkvf = jnp.concatenate(dkv_parts, axis=1).astype(BF16)

        lat_t = jnp.maximum(i - 1, 0)
        dpl_t = dpl_ref[...].astype(F32)
        ds = jnp.concatenate([jnp.where(i > 1, hb_ref[...].astype(F32), 0.0), dpl_t,
                              jnp.where(i < n_tiles - 1, ha_ref[...].astype(F32), 0.0)], axis=0)
        tpos = lat_t * TILE - HALO + lax.broadcasted_iota(jnp.int32, (TILE + 2 * HALO, 1), 0)
        for g, w in enumerate(POOL_WINDOWS):
            sl = slice(g * GROUP_DIM, (g + 1) * GROUP_DIM)
            wsum = _window_sum(ds[:, sl] * _inv_count(tpos, w, seq), w, True)[HALO:HALO + TILE]
            du_ref[:, O_PIN + g * GROUP_DIM:O_PIN + (g + 1) * GROUP_DIM] = jnp.where(
                is_lat, wsum - dpl_t[:, sl], 0.0).astype(BF16)

        du_ref[:, O_GA:O_GA + D_ATTN] = jnp.where(is_lat, dga_ref[...], jnp.zeros((), BF16))
        du_ref[:, O_GP:O_GP + D_POOL] = jnp.where(is_lat, dgp_ref[...], jnp.zeros((), BF16))
        du_ref[:, O_KR:U_PAD] = dkr.astype(BF16)

        gwuq_ref[...] += _dot_tn(dqf, qnb)
        dqn = _dot(dqf, wuq_ref[...])
        gwukv_ref[...] += _dot_tn(dkvf, kvnb)
        dkvn = _dot_nt(dkvf, wukv_ref[...])
        dqlg_ref[...] += jnp.sum(dqn * qhat, axis=0, keepdims=True)
        dqhat = dqn * qlg_ref[...]
        dcq = rq * (dqhat - qhat * jnp.mean(dqhat * qhat, axis=-1, keepdims=True))
        dkvlg_ref[...] += jnp.sum(dkvn * kvhat, axis=0, keepdims=True)
        dkvhat = dkvn * kvlg_ref[...]
        dckv = rkv * (dkvhat - kvhat * jnp.mean(dkvhat * kvhat, axis=-1, keepdims=True))
        du_ref[:, O_CQ:O_CQ + R_Q] = dcq.astype(BF16)
        du_ref[:, O_CKV:O_CKV + R_KV] = dckv.astype(BF16)

    t1 = lambda s: jnp.minimum(s, n_tiles - 1)
    t2 = lambda s: jnp.clip(s - 1, 0, n_tiles - 1)
    t3 = lambda s: jnp.clip(s - 2, 0, n_tiles - 1)
    latent = lambda t: jnp.maximum(t - 1, 0)
    lat = lambda s: (latent(t1(s)), 0)
    lat3 = lambda s: (latent(t3(s)), 0)
    full = lambda shape: pl.BlockSpec(shape, lambda s: (0,) * len(shape))
    rope_spec = pl.BlockSpec((TILE, 128), lambda s: (t1(s), 0))
    return pl.pallas_call(
        body, name="inproj_bwd", grid=(n_tiles + 2,),
        out_shape=(jax.ShapeDtypeStruct((seq, D_MODEL), F32), jax.ShapeDtypeStruct((U_PAD, D_MODEL), F32),
                   jax.ShapeDtypeStruct((N_HEADS * D_HEAD_PAD, R_Q), F32), jax.ShapeDtypeStruct((N_HEADS * 256, R_KV), F32),
                   jax.ShapeDtypeStruct((1, R_Q), F32), jax.ShapeDtypeStruct((1, R_KV), F32),
                   jax.ShapeDtypeStruct((1, D_HEAD_PAD), F32), jax.ShapeDtypeStruct((1, D_HEAD_PAD), F32),
                   jax.ShapeDtypeStruct((1, D_MODEL), F32), jax.ShapeDtypeStruct((8, D_MODEL), F32)),
        in_specs=[pl.BlockSpec((TILE, D_MODEL), lambda s: (latent(t2(s)), 0)), pl.BlockSpec((TILE, D_MODEL), lat3),
                  full((TILE, D_MODEL)), full((8, D_MODEL)), full((8, D_MODEL)),
                  full((1, D_MODEL)), full((U_PAD, D_MODEL)), full((1, R_Q)), full((N_HEADS * D_HEAD_PAD, R_Q)),
                  full((1, R_KV)), full((R_KV, N_HEADS * 256)), full((1, D_HEAD_PAD)), full((1, D_HEAD_PAD)),
                  rope_spec, rope_spec, rope_spec,
                  pl.BlockSpec((TILE, R_Q), lambda s: (t1(s), (O_CQ - N_GATES) // R_Q)),
                  pl.BlockSpec((TILE, R_KV), lambda s: (t1(s), (O_CKV - N_GATES) // R_KV)),
                  pl.BlockSpec((TILE, 128), lambda s: (t1(s), (O_KR - N_GATES) // 128)),
                  pl.BlockSpec((N_HEADS, TILE, D_HEAD_PAD), lambda s: (0, latent(t1(s)), 0)),
                  pl.BlockSpec((N_HEADS, D_HEAD_PAD, TILE), lambda s: (0, 0, t1(s))),
                  pl.BlockSpec((N_HEADS, D_V, TILE), lambda s: (0, 0, t1(s))),
                  pl.BlockSpec((TILE, D_ATTN), lat), pl.BlockSpec((TILE, D_POOL), lat), pl.BlockSpec((TILE, D_POOL), lat),
                  pl.BlockSpec((HALO, D_POOL), lambda s: (jnp.maximum((t1(s) - 1) * rows8 - 1, 0), 0)),
                  pl.BlockSpec((HALO, D_POOL), lambda s: (jnp.minimum(jnp.maximum(t1(s), 1) * rows8, last8), 0)),
                  pl.BlockSpec((TILE, D_MODEL), lat3)],
        out_specs=(pl.BlockSpec((TILE, D_MODEL), lat3), full((U_PAD, D_MODEL)), full((N_HEADS * D_HEAD_PAD, R_Q)),
                   full((N_HEADS * 256, R_KV)), full((1, R_Q)), full((1, R_KV)), full((1, D_HEAD_PAD)),
                   full((1, D_HEAD_PAD)), full((1, D_MODEL)), full((8, D_MODEL))),
        scratch_shapes=[pltpu.VMEM((TILE, U_PAD), BF16), pltpu.VMEM((TILE, U_PAD), BF16),
                        pltpu.VMEM((TILE, D_MODEL), F32), pltpu.VMEM((TILE, D_MODEL), F32)],
        compiler_params=pltpu.CompilerParams(dimension_semantics=("arbitrary",), vmem_limit_bytes=VMEM_LIMIT),
    )(x, x, ctx, modrows, bmodrows, norm_g, w_in_p, q_lora_g, w_uq_p, kv_lora_g, w_ukv, qng_p, kng_p, cos, slo, shi,
      u, u, u, dq, dk, dv, dga, dgp, dpl, dpl, dpl, g1)


SMALL_LAYOUT = ((0, D_MODEL), (1, R_Q), (2, R_KV), (3, D_HEAD_PAD), (4, D_HEAD_PAD), (5, D_POOL))
ROW_DMOD_B, ROW_DMOD_C, ROW_LOSS = 6, 9, 12


def _epilogue(parts, landed, smalls, dmod4, dgate, loss_acc, w_mod_l):
    n_a, n_l, n_s = len(parts), len(landed), len(smalls)
    n_mod = w_mod_l.shape[1]
    blk = [(D_IN_PROJ // N_DEV, D_MODEL)] + [p.shape[1:] for p in parts[1:]]
    small_out = [D_MODEL, R_Q, R_KV, D_HEAD, D_HEAD, D_POOL]

    def body(*refs):
        part_refs = refs[0:n_a]
        land_refs = refs[n_a:n_a + n_l]
        small_in = refs[n_a + n_l:n_a + n_l + n_s]
        dmod4_ref, dgate_ref, loss_ref, wmod_ref = refs[n_a + n_l + n_s:n_a + n_l + n_s + 4]
        outs = refs[n_a + n_l + n_s + 4:]
        red_refs = outs[0:n_a]
        landsum_refs = outs[n_a:n_a + n_l]
        ccg_ref = outs[n_a + n_l]
        sum_refs = outs[n_a + n_l + 1:n_a + n_l + 1 + n_s]
        gbmod_ref, dmy_ref, lossout_ref = outs[n_a + n_l + 1 + n_s:n_a + n_l + 4 + n_s]
        scr = outs[n_a + n_l + 4 + n_s:]
        recv1, own1, sum1b, recv2 = scr[0:n_a], scr[n_a:2 * n_a], scr[2 * n_a:3 * n_a], scr[3 * n_a:4 * n_a]
        small_ref, smallg_ref, tot_ref, cc_ref = scr[4 * n_a:4 * n_a + 4]
        ssem1, rsem1, lsem, ssem2, rsem2, ssem_s, rsem_s, ssem_cc, rsem_cc = scr[4 * n_a + 4:]
        x, y, c = _coords()
        me = (x, y, c)
        sibling = (x, y, 1 - c)

        def rotated_rows(dev):
            n = blk[0][0]
            first, behind = D_IN_PROJ - N_GATES, N_GATES
            lo, hi = dev * n, (dev + 1) * n
            pieces = []
            if lo < first:
                pieces.append((lo + behind, min(hi, first) - lo, 0))
            if hi > first:
                start = max(lo, first)
                pieces.append((start - first, hi - start, start - lo))
            return pieces

        stage1, own_copies = [], []
        for cc in (0, 1):
            @pl.when(c == cc)
            def _():
                for b in range(4):
                    dev = 4 * (b >> 1) + 2 * (b & 1)
                    for src, n, dst in rotated_rows(dev + 1 - cc):
                        pltpu.make_async_remote_copy(
                            src_ref=part_refs[0].at[pl.ds(src, n)], dst_ref=recv1[0].at[b, pl.ds(dst, n)],
                            send_sem=ssem1.at[0, b], recv_sem=rsem1.at[0, b], device_id=sibling, device_id_type=MESH).start()
                    for src, n, dst in rotated_rows(dev + cc):
                        pltpu.make_async_copy(part_refs[0].at[pl.ds(src, n)], own1[0].at[b, pl.ds(dst, n)],
                                              lsem.at[0, b]).start()
        for a in range(n_a):
            for b in range(4):
                dev = 4 * (b >> 1) + 2 * (b & 1)
                if a == 0:
                    stage1.append(pltpu.make_async_remote_copy(
                        src_ref=recv1[a].at[b], dst_ref=recv1[a].at[b], send_sem=ssem1.at[a, b], recv_sem=rsem1.at[a, b],
                        device_id=sibling, device_id_type=MESH))
                    own_copies.append(pltpu.make_async_copy(own1[a].at[b], own1[a].at[b], lsem.at[a, b]))
                    continue
                stage1.append(pltpu.make_async_remote_copy(
                    src_ref=part_refs[a].at[dev + (1 - c)], dst_ref=recv1[a].at[b],
                    send_sem=ssem1.at[a, b], recv_sem=rsem1.at[a, b], device_id=sibling, device_id_type=MESH))
                own_copies.append(pltpu.make_async_copy(part_refs[a].at[dev + c], own1[a].at[b], lsem.at[a, b]))
                stage1[-1].start()
                own_copies[-1].start()

        small_ref[...] = jnp.zeros_like(small_ref)
        for ref, (row, width) in zip(small_in, SMALL_LAYOUT):
            small_ref[row:row + 1, 0:width] = ref[...]
        small_ref[ROW_DMOD_B:ROW_DMOD_B + 2, :] = dmod4_ref[0:2, :]
        small_ref[ROW_DMOD_B + 2:ROW_DMOD_B + 3, :] = dgate_ref[...]
        small_ref[ROW_DMOD_C:ROW_DMOD_C + 2, :] = dmod4_ref[2:4, :]
        small_ref[ROW_LOSS:ROW_LOSS + 1, 0:128] = loss_ref[0:1, :]
        smallg_ref[_lin(me)] = small_ref[...]
        s_recv, s_send = _gather_to_all(small_ref, smallg_ref, ssem_s, rsem_s)
        s_recv()
        tot = smallg_ref[0]
        for d in range(1, N_DEV):
            tot = tot + smallg_ref[d]
        tot_ref[...] = tot
        for ref, (row, _), width in zip(sum_refs, SMALL_LAYOUT, small_out):
            ref[...] = tot_ref[row:row + 1, 0:width]
        lossout_ref[...] = tot_ref[8:16, 0:128]
        lin = _lin(me)

        def my_columns(three_rows):
            v = jnp.concatenate(three_rows, axis=1)
            out = jnp.zeros((1, n_mod), F32)
            for d in range(N_DEV):
                out = out + jnp.where(lin == d, v[:, d * n_mod:(d + 1) * n_mod], 0.0)
            return out

        rows3 = lambda ref, r0: [ref[r0 + t:r0 + t + 1, :] for t in range(3)]
        dmodc = rows3(tot_ref, ROW_DMOD_C)
        gbmod_ref[...] = jnp.concatenate(rows3(tot_ref, ROW_DMOD_B), axis=1) + jnp.concatenate(dmodc, axis=1)
        dmy_ref[...] = jnp.zeros_like(dmy_ref)
        for b in range(N_DEV):
            dmy_ref[b:b + 1, :] = my_columns(rows3(smallg_ref.at[b], ROW_DMOD_B))
        dmy = my_columns(dmodc)
        dmy_ref[N_DEV:N_DEV + 1, :] = dmy
        part = lax.dot_general(jnp.broadcast_to(dmy, (8, n_mod)), wmod_ref[...], (((1,), (1,)), ((), ())),
                               precision=HIGHEST, preferred_element_type=F32)

        cc_ref[...] = part
        ccg_ref[_lin(me)] = part
        cc_recv, cc_send = _gather_to_all(cc_ref, ccg_ref, ssem_cc, rsem_cc)

        stage2 = []
        for a in range(n_a):
            for b in range(4):
                stage1[4 * a + b].wait_recv()
                own_copies[4 * a + b].wait()
                sum1b[a][b] = (own1[a][b] + recv1[a][b]).astype(BF16)
            for k in (1, 2, 3):
                tx, ty = _flip(x, (k >> 1) & 1), _flip(y, k & 1)
                stage2.append(pltpu.make_async_remote_copy(
                    src_ref=sum1b[a].at[2 * tx + ty], dst_ref=recv2[a].at[k - 1], send_sem=ssem2.at[a, k - 1],
                    recv_sem=rsem2.at[a, k - 1], device_id=(tx, ty, c), device_id_type=MESH))
                stage2[-1].start()
        for a in range(n_a):
            own = own1[a][2 * x + y] + recv1[a][2 * x + y]
            for k in (1, 2, 3):
                stage2[3 * a + k - 1].wait_recv()
                own = own + recv2[a][k - 1].astype(F32)
            red_refs[a][...] = own

        for ref, out in zip(land_refs, landsum_refs):
            acc = ref[0].astype(F32)
            for d in range(1, N_DEV):
                acc = acc + ref[d].astype(F32)
            out[...] = acc
        cc_recv()
        for cp in stage1 + stage2:
            cp.wait_send()
        s_send()
        cc_send()

    vm = pl.BlockSpec(memory_space=pltpu.VMEM)
    sds = jax.ShapeDtypeStruct
    return pl.pallas_call(
        body, name="epilogue_reduce",
        out_shape=(*[sds(s, F32) for s in blk], *[sds(a.shape[1:], F32) for a in landed], sds((N_DEV, 8, D_MODEL), F32),
                   *[sds((1, w), F32) for w in small_out], sds((1, 3 * D_MODEL), F32), sds((16, n_mod), F32),
                   sds((8, 128), F32)),
        in_specs=[pl.BlockSpec(memory_space=pl.ANY)] * n_a + [vm] * (n_l + n_s + 4),
        out_specs=tuple([vm] * (n_a + n_l + n_s + 4)),
        scratch_shapes=[*[pltpu.VMEM((4,) + s, F32) for s in blk], *[pltpu.VMEM((4,) + s, F32) for s in blk],
                        *[pltpu.VMEM((4,) + s, BF16) for s in blk], *[pltpu.VMEM((3,) + s, BF16) for s in blk],
                        pltpu.VMEM((SMALL_ROWS, D_MODEL), F32), pltpu.VMEM((N_DEV, SMALL_ROWS, D_MODEL), F32),
                        pltpu.VMEM((SMALL_ROWS, D_MODEL), F32), pltpu.VMEM((8, D_MODEL), F32),
                        pltpu.SemaphoreType.DMA((n_a, 4)), pltpu.SemaphoreType.DMA((n_a, 4)), pltpu.SemaphoreType.DMA((n_a, 4)),
                        pltpu.SemaphoreType.DMA((n_a, 3)), pltpu.SemaphoreType.DMA((n_a, 3)),
                        pltpu.SemaphoreType.DMA((7,)), pltpu.SemaphoreType.DMA((7,)),
                        pltpu.SemaphoreType.DMA((7,)), pltpu.SemaphoreType.DMA((7,))],
        compiler_params=pltpu.CompilerParams(vmem_limit_bytes=VMEM_LIMIT),
    )(*parts, *landed, *smalls, dmod4, dgate, loss_acc, w_mod_l)


def _adamw(weights, grads, ms, vs, c_all_t, dmod_my, p2g):
    n = len(weights)

    def body(*refs):
        w_refs = refs[0:n]
        g_in = refs[n:2 * n - 2]
        m_refs = refs[2 * n - 2:3 * n - 2]
        v_refs = refs[3 * n - 2:4 * n - 2]
        cat_ref, dmod_ref, p2g_ref = refs[4 * n - 2:4 * n + 1]
        outs = refs[4 * n + 1:]
        g_refs, d_refs, nm_refs, nv_refs = outs[0:n], outs[n:2 * n], outs[2 * n:3 * n], outs[3 * n:4 * n]
        gcc_ref, gwm_ref = g_refs[0], g_refs[1]

        part = p2g_ref[0, 0:1, :]
        for d in range(1, N_DEV):
            part = part + p2g_ref[d, 0:1, :]
        cc = w_refs[0][...]
        sg = _sigmoid(cc)
        gcc_ref[...] = part * (sg * (1.0 + cc * (1.0 - sg)))
        cat = cat_ref[...]
        gwm_ref[...] = lax.dot_general(cat * _sigmoid(cat), dmod_ref[...], (((1,), (0,)), ((), ())), precision=HIGHEST,
                                       preferred_element_type=F32)
        for idx in range(n):
            if idx >= 2:
                g_refs[idx][...] = g_in[idx - 2][...]
            g = g_refs[idx][...]
            m = ADAM_B1 * m_refs[idx][...] + (1.0 - ADAM_B1) * g
            v = ADAM_B2 * v_refs[idx][...] + (1.0 - ADAM_B2) * (g * g)
            m_hat = m / (1.0 - ADAM_B1 ** ADAM_STEP)
            v_hat = v / (1.0 - ADAM_B2 ** ADAM_STEP)
            d_refs[idx][...] = -ADAM_LR * (m_hat / (jnp.sqrt(v_hat) + ADAM_EPS) + ADAM_WD * w_refs[idx][...])
            nm_refs[idx][...] = m
            nv_refs[idx][...] = v

    vm = pl.BlockSpec(memory_space=pltpu.VMEM)
    shapes = [jax.ShapeDtypeStruct(w.shape, F32) for w in weights]
    args = list(weights) + list(grads[2:]) + list(ms) + list(vs) + [c_all_t, dmod_my, p2g]
    out_shape = tuple(shapes * 4)
    return pl.pallas_call(
        body, name="adamw_update", out_shape=out_shape, in_specs=[vm] * len(args), out_specs=tuple([vm] * len(out_shape)),
        compiler_params=pltpu.CompilerParams(vmem_limit_bytes=VMEM_LIMIT),
    )(*args)


def _rope_tables(seq):
    rows = seq // GRID_W
    row = np.repeat(np.arange(rows, dtype=np.float32), GRID_W)
    col = np.tile(np.arange(GRID_W, dtype=np.float32), rows)
    n_freq = D_ROPE // 4
    inv = (np.float32(ROPE_BASE) ** (-np.arange(n_freq, dtype=np.float32) / np.float32(n_freq))).astype(np.float32)
    ang_r = (row[:, None] * inv).astype(np.float32)
    ang_c = (col[:, None] * inv).astype(np.float32)
    ang = np.concatenate([ang_r, ang_r, ang_c, ang_c], axis=-1)
    cos, sin = np.cos(ang).astype(np.float32), np.sin(ang).astype(np.float32)
    low = (np.arange(D_ROPE) % 32) < 16

    def table(t, fill):
        out = np.full((TILE + seq, 128), fill, np.float32)
        out[TILE:, 0:D_ROPE] = t
        return jnp.asarray(out)

    return table(cos, 1.0), table(np.where(low, -sin, 0.0), 0.0), table(np.where(low, 0.0, sin), 0.0)


def kernel(x, c, ctx, c_ctx, w_mod, b_mod, norm_g, w_in, q_lora_g, w_uq, kv_lora_g, w_ukv, q_norm_g, k_norm_g, w_pool, pool_scale, w_out, loss_target, m_c_ctx, m_w_mod, m_b_mod, m_norm_g, m_w_in, m_q_lora_g, m_w_uq, m_kv_lora_g, m_w_ukv, m_q_norm_g, m_k_norm_g, m_w_pool, m_pool_scale, m_w_out, v_c_ctx, v_w_mod, v_b_mod, v_norm_g, v_w_in, v_q_lora_g, v_w_uq, v_kv_lora_g, v_w_ukv, v_q_norm_g, v_k_norm_g, v_w_pool, v_pool_scale, v_w_out):
    seq = x.shape[1]
    assert ctx.shape[1] == TILE and seq % TILE == 0 and seq % GRID_W == 0
    ix, iy, ic = lax.axis_index("x"), lax.axis_index("y"), lax.axis_index("c")
    me = 4 * ix + 2 * iy + ic
    x2, ctx2, tgt2 = x[0], ctx[0], loss_target[0]
    w_mod_l, w_ukv_l, w_out_l = w_mod[0], w_ukv[0], w_out[0]
    c_ctx_row = c_ctx.reshape(1, D_MODEL)

    tr = lambda a: jnp.transpose(a[0])
    w_in_tl, w_uq_tl = tr(w_in), tr(w_uq)
    cg, modg, wg_in, wg_uq, wg_ukv = _prologue(
        c, c_ctx_row, w_mod_l,
        [w_in_tl, w_uq_tl, w_ukv_l])
    mod_all = jnp.transpose(modg, (1, 0, 2)).reshape(16, 3 * D_MODEL)
    mod_b = lax.dynamic_slice_in_dim(mod_all, me, 1, axis=0).reshape(3, D_MODEL)
    mod_c = mod_all[8].reshape(3, D_MODEL)
    modrows = jnp.concatenate([mod_b, mod_c[0:2], jnp.zeros((3, D_MODEL), F32)], axis=0)
    b3 = b_mod.reshape(3, D_MODEL)
    bmodrows = jnp.concatenate([b3, b3[0:2], jnp.zeros((3, D_MODEL), F32)], axis=0)

    w_in_t = wg_in.reshape(D_IN_PROJ, D_MODEL)
    w_in_p = jnp.concatenate([w_in_t[448:], w_in_t[0:384], w_in_t[384:448],
                              jnp.zeros((U_PAD - D_IN_PROJ, D_MODEL), BF16)], axis=0)
    w_uq_p = jnp.concatenate([wg_uq.reshape(N_HEADS, D_HEAD, R_Q), jnp.zeros((N_HEADS, D_HEAD_PAD - D_HEAD, R_Q), BF16)],
                             axis=1).reshape(N_HEADS * D_HEAD_PAD, R_Q)
    w_ukv_f = jnp.transpose(wg_ukv, (1, 0, 2)).reshape(R_KV, N_HEADS * 256)
    qng_p = jnp.concatenate([q_norm_g, jnp.zeros((1, D_HEAD_PAD - D_HEAD), F32)], axis=1)
    kng_p = jnp.concatenate([k_norm_g, jnp.zeros((1, D_HEAD_PAD - D_HEAD), F32)], axis=1)
    cos, slo, shi = _rope_tables(seq)

    u_gates, u_mla, q, k, v = _inproj_fwd(x2, ctx2, modrows, bmodrows, norm_g, w_in_p, q_lora_g, w_uq_p, kv_lora_g, w_ukv_f,
                             qng_p, kng_p, cos, slo, shi)
    attn, lse, wg_out = _attn_fwd(q, k, v, min(2048, seq), w_out_l.astype(BF16))
    (loss_acc, g1, dgate, d_attn, dga, dgp, dpl, gwo_b, gwp, dps) = _mix(
        x2, tgt2, attn, u_gates, modrows, bmodrows, w_pool[0], pool_scale, wg_out.reshape(D_MODEL, D_MODEL))

    blocks = lambda a: a.reshape((N_DEV, a.shape[0] // N_DEV) + a.shape[1:])
    dq, dk, dv, land_wo, land_wp = _attn_bwd(q, k, v, attn, lse, d_attn.reshape(seq, D_ATTN), min(1024, seq), blocks(gwo_b),
                                             gwp.reshape(4 * GROUP_DIM, GROUP_DIM))
    (grad_x, gwin_p, gwuq_p, gwukv_t, dqlg, dkvlg, dqng, dkng, dng, dmod4) = _inproj_bwd(
        x2, ctx2, modrows, bmodrows, norm_g, w_in_p, q_lora_g, w_uq_p, kv_lora_g, w_ukv_f, qng_p, kng_p, cos, slo, shi,
        u_mla, dq, dk, dv, dga, dgp, dpl, g1)

    gwuq_t = gwuq_p.reshape(N_HEADS, D_HEAD_PAD, R_Q)[:, 0:D_HEAD].reshape(N_HEADS * D_HEAD, R_Q)
    parts = [gwin_p, blocks(gwuq_t), blocks(gwukv_t)]
    (r_win, r_wuq, r_wukv, r_wout, g_w_pool, ccg, g_ng, g_qlg, g_kvlg, g_qng, g_kng, g_ps, g_bmod, dmod_my,
     loss_rows) = _epilogue(parts, [land_wo, land_wp], [dng, dqlg, dkvlg, dqng, dkng, dps], dmod4, dgate, loss_acc, w_mod_l)

    g_w_ukv = jnp.transpose(r_wukv)
    c_rows = cg[:, 0, :]
    c_all_t = jnp.transpose(jnp.concatenate([c_rows, c_ctx_row, jnp.zeros((16 - N_DEV - 1, D_MODEL), F32)], axis=0))

    weights = [c_ctx_row, w_mod_l, b_mod, norm_g, w_in_tl, q_lora_g, w_uq_tl, kv_lora_g, w_ukv_l, q_norm_g, k_norm_g,
               w_pool.reshape(4 * GROUP_DIM, GROUP_DIM), pool_scale, w_out_l]
    grads = [None, None, g_bmod, g_ng, r_win, g_qlg, r_wuq, g_kvlg, g_w_ukv, g_qng, g_kng, g_w_pool, g_ps, r_wout]
    ms = [m_c_ctx.reshape(1, D_MODEL), m_w_mod[0], m_b_mod, m_norm_g, tr(m_w_in), m_q_lora_g, tr(m_w_uq), m_kv_lora_g,
          m_w_ukv[0], m_q_norm_g, m_k_norm_g, m_w_pool.reshape(4 * GROUP_DIM, GROUP_DIM), m_pool_scale, m_w_out[0]]
    vs = [v_c_ctx.reshape(1, D_MODEL), v_w_mod[0], v_b_mod, v_norm_g, tr(v_w_in), v_q_lora_g, tr(v_w_uq), v_kv_lora_g,
          v_w_ukv[0], v_q_norm_g, v_k_norm_g, v_w_pool.reshape(4 * GROUP_DIM, GROUP_DIM), v_pool_scale, v_w_out[0]]
    outs = _adamw(weights, grads, ms, vs, c_all_t, dmod_my, ccg)
    n = len(weights)
    grads, deltas, new_m, new_v = outs[0:n], outs[n:2 * n], outs[2 * n:3 * n], outs[3 * n:4 * n]

    loss = loss_rows[ROW_LOSS - 8, 0]
    final = [c_ctx.shape, w_mod.shape, b_mod.shape, norm_g.shape, w_in.shape, q_lora_g.shape, w_uq.shape, kv_lora_g.shape,
             w_ukv.shape, q_norm_g.shape, k_norm_g.shape, w_pool.shape, pool_scale.shape, w_out.shape]
    transposed = (4, 6)

    def shaped(arrs):
        return [(jnp.transpose(a) if i in transposed else a).reshape(s) for i, (a, s) in enumerate(zip(arrs, final))]

    return (loss, grad_x[None], *shaped(grads), *shaped(deltas), *shaped(new_m), *shaped(new_v))
```

```python
import numpy as np

import jax
import jax.numpy as jnp
from jax import lax
from jax.experimental import pallas as pl
from jax.experimental.pallas import tpu as pltpu

F32 = jnp.float32
BF16 = jnp.bfloat16
MESH = pl.DeviceIdType.MESH
HIGHEST = lax.Precision.HIGHEST

D_MODEL = 1024
N_HEADS = 4
D_NOPE = 128
D_ROPE = 64
D_HEAD = D_NOPE + D_ROPE
D_HEAD_PAD = 256
D_V = 128
R_Q = 256
R_KV = 128
D_ATTN = 512
D_POOL = 512
POOL_WINDOWS = (2, 4, 8, 16)
GROUP_DIM = 128
GRID_W = 64
ROPE_BASE = 10000.0
NORM_EPS = 1e-6
ATTN_SCALE = D_HEAD ** -0.5
LOG2_E = 1.4426950408889634
LN_2 = 0.6931471805599453
ATTN_ROWS = 256
D_IN_PROJ = 1984
U_PAD = 2048
O_GA, O_PIN, O_GP, O_CQ, O_CKV, O_KR = 0, 512, 1024, 1536, 1792, 1920
TILE = 256
MIX_TILE = 512
Q_BLOCK = 128
HALO = 16
N_GATES = 1536
N_MLA = D_IN_PROJ - N_GATES
N_DEV = 8
VMEM_LIMIT = 56 * 1024 * 1024

ADAM_LR = 0.001
ADAM_B1 = 0.9
ADAM_B2 = 0.999
ADAM_EPS = 1e-08
ADAM_WD = 0.01
ADAM_STEP = 10

SMALL_ROWS = 16


def _dot(a, b):
    return lax.dot_general(a, b, (((1,), (0,)), ((), ())), preferred_element_type=F32)


def _dot_nt(a, b):
    return lax.dot_general(a, b, (((1,), (1,)), ((), ())), preferred_element_type=F32)


def _dot_tn(a, b):
    return lax.dot_general(a, b, (((0,), (0,)), ((), ())), preferred_element_type=F32)


def _sigmoid(x):
    return 1.0 / (1.0 + jnp.exp(-x))


def _rope(p, cos, slo, shi):
    return p * cos + pltpu.roll(p, 112, 1) * slo + pltpu.roll(p, 16, 1) * shi


def _rope_t(d, cos, slo, shi):
    return d * cos + pltpu.roll(d * slo, 16, 1) + pltpu.roll(d * shi, 112, 1)


def _shift_rows(a, k):
    n = a.shape[0]
    return pltpu.roll(a, (-k) % n, 0)


def _window_sum(x, w, transposed):
    s = (x + _shift_rows(x, 1)) if transposed else (_shift_rows(x, -1) + x)
    step = 1
    while 2 * step < w:
        s = _shift_rows(s, -step) + _shift_rows(s, step)
        step *= 2
    return s


def _inv_count(tpos, w, seq):
    lo = jnp.maximum(tpos - w // 2, 0)
    hi = jnp.minimum(tpos - w // 2 + w, seq)
    return 1.0 / jnp.maximum(hi - lo, 1).astype(F32)


def _coords():
    return lax.axis_index("x"), lax.axis_index("y"), lax.axis_index("c")


def _flip(v, bit):
    return (1 - v) if bit else v


def _peer(k):
    x, y, c = _coords()
    return (_flip(x, (k >> 2) & 1), _flip(y, (k >> 1) & 1), _flip(c, k & 1))


def _lin(p):
    return 4 * p[0] + 2 * p[1] + p[2]


def _gather_to_all(src_ref, slots_ref, send_sems, recv_sems):
    me = _coords()
    copies = []
    for k in range(1, N_DEV):
        cp = pltpu.make_async_remote_copy(
            src_ref=src_ref, dst_ref=slots_ref.at[_lin(me)], send_sem=send_sems.at[k - 1], recv_sem=recv_sems.at[k - 1],
            device_id=_peer(k), device_id_type=MESH)
        cp.start()
        copies.append(cp)

    def wait_recv():
        for k in range(1, N_DEV):
            pltpu.make_async_remote_copy(
                src_ref=src_ref, dst_ref=slots_ref.at[_lin(_peer(k))], send_sem=send_sems.at[k - 1],
                recv_sem=recv_sems.at[k - 1], device_id=_peer(k), device_id_type=MESH).wait_recv()

    def wait_send():
        for cp in copies:
            cp.wait_send()

    return wait_recv, wait_send


def _prologue(c8, cctx8, w_mod_l, shards):
    n_mod = w_mod_l.shape[1]
    n_w = len(shards)

    def body(c_ref, cctx_ref, wmod_ref, *refs):
        w_refs = refs[0:n_w]
        cg_ref, modg_ref = refs[n_w], refs[n_w + 1]
        wg_refs = refs[n_w + 2:2 * n_w + 2]
        modblk_ref = refs[2 * n_w + 2]
        wb_refs = refs[2 * n_w + 3:3 * n_w + 3]
        c8_ref = refs[3 * n_w + 3]
        ssem_w, rsem_w, ssem_c, rsem_c, ssem_m, rsem_m = refs[3 * n_w + 4:]
        x, y, c = _coords()
        me = (x, y, c)
        sibling = (x, y, 1 - c)
        chips = [(1 - x, y), (x, 1 - y), (1 - x, 1 - y)]

        def wcopies(k, block, to, own=False):
            out = []
            for a in range(n_w):
                slot = wg_refs[a].at[_lin(block)]
                out.append(pltpu.make_async_remote_copy(
                    src_ref=wb_refs[a] if own else slot, dst_ref=slot, send_sem=ssem_w.at[a, k], recv_sem=rsem_w.at[a, k],
                    device_id=to, device_id_type=MESH))
            return out

        for a in range(n_w):
            wb_refs[a][...] = w_refs[a][...].astype(BF16)
        first = wcopies(0, me, sibling, own=True)
        for j, chip in enumerate(chips):
            first += wcopies(1 + j, me, (*chip, c), own=True)
        for cp in first:
            cp.start()
        for a in range(n_w):
            wg_refs[a][_lin(me)] = wb_refs[a][...]

        c8_ref[...] = jnp.broadcast_to(c_ref[...], (8, D_MODEL))
        cg_ref[_lin(me)] = c8_ref[...]
        c_recv, c_send = _gather_to_all(c8_ref, cg_ref, ssem_c, rsem_c)
        c_recv()
        row = lax.broadcasted_iota(jnp.int32, (8, D_MODEL), 0)
        c_all = jnp.zeros((8, D_MODEL), F32)
        for d in range(N_DEV):
            c_all = c_all + jnp.where(row == d, cg_ref[d], 0.0)
        cc = jnp.broadcast_to(cctx_ref[...], (8, D_MODEL))
        a = jnp.concatenate([c_all * _sigmoid(c_all), cc * _sigmoid(cc)], axis=0)
        modblk_ref[...] = lax.dot_general(a, wmod_ref[...], (((1,), (0,)), ((), ())), precision=HIGHEST,
                                          preferred_element_type=F32)
        modg_ref[_lin(me)] = modblk_ref[...]
        m_recv, m_send = _gather_to_all(modblk_ref, modg_ref, ssem_m, rsem_m)
        m_recv()

        passed = []
        for j, chip in enumerate(chips):
            for cp in wcopies(1 + j, (*chip, c), me):
                cp.wait_recv()
            fwd = wcopies(4 + j, (*chip, c), sibling)
            for cp in fwd:
                cp.start()
            passed += fwd
        for cp in wcopies(0, sibling, me):
            cp.wait_recv()
        for j, chip in enumerate(chips):
            for cp in wcopies(4 + j, (*chip, 1 - c), me):
                cp.wait_recv()
        for cp in first + passed:
            cp.wait_send()
        c_send()
        m_send()

    vm = pl.BlockSpec(memory_space=pltpu.VMEM)
    return pl.pallas_call(
        body, name="prologue_gather",
        out_shape=(jax.ShapeDtypeStruct((N_DEV, 8, D_MODEL), F32), jax.ShapeDtypeStruct((N_DEV, 16, n_mod), F32),
                   *[jax.ShapeDtypeStruct((N_DEV,) + s.shape, BF16) for s in shards]),
        in_specs=[vm] * (3 + n_w), out_specs=tuple([vm] * (2 + n_w)),
        scratch_shapes=[pltpu.VMEM((16, n_mod), F32), *[pltpu.VMEM(s.shape, BF16) for s in shards],
                        pltpu.VMEM((8, D_MODEL), F32),
                        pltpu.SemaphoreType.DMA((n_w, 7)), pltpu.SemaphoreType.DMA((n_w, 7)),
                        pltpu.SemaphoreType.DMA((7,)), pltpu.SemaphoreType.DMA((7,)),
                        pltpu.SemaphoreType.DMA((7,)), pltpu.SemaphoreType.DMA((7,))],
        compiler_params=pltpu.CompilerParams(vmem_limit_bytes=VMEM_LIMIT),
    )(c8, cctx8, w_mod_l, *shards)


def _modulated_input(is_ctx, x_ref, ctx_ref, mod_ref, bmod_ref, ng_ref):
    xt = jnp.where(is_ctx, ctx_ref[...], x_ref[...])
    shift = jnp.where(is_ctx, mod_ref[3:4, :] + bmod_ref[3:4, :], mod_ref[0:1, :] + bmod_ref[0:1, :])
    scale = jnp.where(is_ctx, mod_ref[4:5, :] + bmod_ref[4:5, :], mod_ref[1:2, :] + bmod_ref[1:2, :])
    r = lax.rsqrt(jnp.mean(xt * xt, axis=-1, keepdims=True) + NORM_EPS)
    xg = (xt * r) * ng_ref[...]
    h = xg * (1.0 + scale) + shift
    return xt, r, xg, h, scale


def _inproj_fwd(x, ctx, modrows, bmodrows, norm_g, w_in_p, q_lora_g, w_uq_p, kv_lora_g, w_ukv, qng_p, kng_p, cos, slo, shi):
    seq = x.shape[0]
    n_tiles = seq // TILE + 1
    tot = seq + TILE

    n_mla = R_Q + R_KV + 128

    def body(x_ref, ctx_ref, mod_ref, bmod_ref, ng_ref, win_ref, qlg_ref, wuq_ref, kvlg_ref, wukv_ref, qng_ref, kng_ref,
             cos_ref, slo_ref, shi_ref, ug_ref, um_ref, q_ref, k_ref, v_ref, stash_even, stash_odd):
        s = pl.program_id(0)

        @pl.when(s == 0)
        def _():
            stash_odd[...] = jnp.zeros_like(stash_odd)

        refs = (x_ref, ctx_ref, mod_ref, bmod_ref, ng_ref, win_ref, qlg_ref, wuq_ref, kvlg_ref, wukv_ref, qng_ref, kng_ref,
                cos_ref, slo_ref, shi_ref, ug_ref, um_ref, q_ref, k_ref, v_ref)

        @pl.when(lax.rem(s, 2) == 0)
        def _():
            stages(s, refs, stash_even, stash_odd)

        @pl.when(lax.rem(s, 2) == 1)
        def _():
            stages(s, refs, stash_odd, stash_even)

    def stages(s, refs, fill, prev_ref):
        (x_ref, ctx_ref, mod_ref, bmod_ref, ng_ref, win_ref, qlg_ref, wuq_ref, kvlg_ref, wukv_ref, qng_ref, kng_ref,
         cos_ref, slo_ref, shi_ref, ug_ref, um_ref, q_ref, k_ref, v_ref) = refs
        cos_t, slo_t, shi_t = cos_ref[...], slo_ref[...], shi_ref[...]
        prev = prev_ref[...]
        cq = prev[:, 0:R_Q]
        qn = cq * lax.rsqrt(jnp.mean(cq * cq, axis=-1, keepdims=True) + NORM_EPS) * qlg_ref[...]
        q = _dot_nt(qn.astype(BF16), wuq_ref[...])
        qg = qng_ref[...] * (ATTN_SCALE * LOG2_E)
        for hd in range(N_HEADS):
            qh = q[:, hd * D_HEAD_PAD:(hd + 1) * D_HEAD_PAD]
            rr = lax.rsqrt(jnp.sum(qh * qh, axis=-1, keepdims=True) * (1.0 / D_HEAD) + NORM_EPS)
            qy = qh * rr * qg
            q_ref[hd, :, 0:D_NOPE] = qy[:, 0:D_NOPE].astype(BF16)
            q_ref[hd, :, D_NOPE:D_HEAD_PAD] = _rope(qy[:, D_NOPE:D_HEAD_PAD], cos_t, slo_t, shi_t).astype(BF16)

        ckv = prev[:, R_Q:R_Q + R_KV]
        kvn = ckv * lax.rsqrt(jnp.mean(ckv * ckv, axis=-1, keepdims=True) + NORM_EPS) * kvlg_ref[...]
        kv = _dot(kvn.astype(BF16), wukv_ref[...])
        krz = prev[:, R_Q + R_KV:n_mla]
        kr_ss = jnp.sum(krz * krz, axis=-1, keepdims=True)
        kg = kng_ref[...]
        for hd in range(N_HEADS):
            kn = kv[:, hd * 256:hd * 256 + D_NOPE]
            rr = lax.rsqrt((jnp.sum(kn * kn, axis=-1, keepdims=True) + kr_ss) * (1.0 / D_HEAD) + NORM_EPS)
            k_ref[hd, :, 0:D_NOPE] = (kn * rr * kg[:, 0:D_NOPE]).astype(BF16)
            k_ref[hd, :, D_NOPE:D_HEAD_PAD] = _rope(krz * rr * kg[:, D_NOPE:D_HEAD_PAD], cos_t, slo_t, shi_t).astype(BF16)
            v_ref[hd] = kv[:, hd * 256 + D_NOPE:(hd + 1) * 256].astype(BF16)

        _, _, _, h, _ = _modulated_input(s == 0, x_ref, ctx_ref, mod_ref, bmod_ref, ng_ref)
        hb = h.astype(BF16)
        ug_ref[...] = _dot_nt(hb, win_ref[N_MLA:D_IN_PROJ, :]).astype(BF16)
        um = _dot_nt(hb, win_ref[0:n_mla, :])
        lane = lax.broadcasted_iota(jnp.int32, (TILE, 128), 1)
        um = jnp.concatenate([um[:, 0:R_Q + R_KV], jnp.where(lane < D_ROPE, um[:, R_Q + R_KV:n_mla], 0.0)], axis=1)
        um_ref[...] = um
        fill[...] = um

    first = lambda s: jnp.minimum(s, n_tiles - 1)
    second = lambda s: jnp.maximum(s - 1, 0)
    latent = lambda t: jnp.maximum(t - 1, 0)
    full = lambda shape: pl.BlockSpec(shape, lambda s: (0,) * len(shape))
    rope_spec = pl.BlockSpec((TILE, 128), lambda s: (second(s), 0))
    return pl.pallas_call(
        body, name="inproj_fwd", grid=(n_tiles + 1,),
        out_shape=(jax.ShapeDtypeStruct((seq, N_GATES), BF16), jax.ShapeDtypeStruct((tot, n_mla), F32),
                   jax.ShapeDtypeStruct((N_HEADS, seq, D_HEAD_PAD), BF16),
                   jax.ShapeDtypeStruct((N_HEADS, tot, D_HEAD_PAD), BF16),
                   jax.ShapeDtypeStruct((N_HEADS, tot, D_V), BF16)),
        in_specs=[pl.BlockSpec((TILE, D_MODEL), lambda s: (latent(first(s)), 0)), full((TILE, D_MODEL)), full((8, D_MODEL)),
                  full((8, D_MODEL)), full((1, D_MODEL)), full((D_IN_PROJ, D_MODEL)), full((1, R_Q)),
                  full((N_HEADS * D_HEAD_PAD, R_Q)), full((1, R_KV)), full((R_KV, N_HEADS * 256)), full((1, D_HEAD_PAD)),
                  full((1, D_HEAD_PAD)), rope_spec, rope_spec, rope_spec],
        out_specs=(pl.BlockSpec((TILE, N_GATES), lambda s: (latent(first(s)), 0)),
                   pl.BlockSpec((TILE, n_mla), lambda s: (first(s), 0)),
                   pl.BlockSpec((N_HEADS, TILE, D_HEAD_PAD), lambda s: (0, latent(second(s)), 0)),
                   pl.BlockSpec((N_HEADS, TILE, D_HEAD_PAD), lambda s: (0, second(s), 0)),
                   pl.BlockSpec((N_HEADS, TILE, D_V), lambda s: (0, second(s), 0))),
        scratch_shapes=[pltpu.VMEM((TILE, n_mla), F32), pltpu.VMEM((TILE, n_mla), F32)],
        compiler_params=pltpu.CompilerParams(dimension_semantics=("arbitrary",), vmem_limit_bytes=VMEM_LIMIT),
    )(x, ctx, modrows, bmodrows, norm_g, w_in_p, q_lora_g, w_uq_p, kv_lora_g, w_ukv, qng_p, kng_p, cos, slo, shi)


def _hosted_exchange(first, last, items, send_sems, recv_sems, local_sems):
    me = _lin(_coords())

    def remote(n, k, src, land, scatter):
        peer = _peer(k)
        return pltpu.make_async_remote_copy(
            src_ref=src.at[_lin(peer)] if scatter else src, dst_ref=land.at[me], send_sem=send_sems.at[n, k - 1],
            recv_sem=recv_sems.at[n, k - 1], device_id=peer, device_id_type=MESH)

    def local(n, src, land, scatter):
        return pltpu.make_async_copy(src.at[me] if scatter else src, land.at[me], local_sems.at[n])

    @pl.when(first)
    def _():
        for n, (src, land, scatter) in enumerate(items):
            for k in range(1, N_DEV):
                remote(n, k, src, land, scatter).start()
            local(n, src, land, scatter).start()

    @pl.when(last)
    def _():
        for n, (src, land, scatter) in enumerate(items):
            for k in range(1, N_DEV):
                peer = _peer(k)
                pltpu.make_async_remote_copy(
                    src_ref=src.at[0] if scatter else src, dst_ref=land.at[_lin(peer)], send_sem=send_sems.at[n, k - 1],
                    recv_sem=recv_sems.at[n, k - 1], device_id=peer, device_id_type=MESH).wait_recv()
            for k in range(1, N_DEV):
                remote(n, k, src, land, scatter).wait_send()
            local(n, src, land, scatter).wait()


def _attn_fwd(q, k, v, block_q, w_out_b):
    _, seq, _ = q.shape
    n_keys = k.shape[1]
    n_q = seq // block_q

    def body(q_ref, k_ref, v_ref, wo_ref, o_ref, lse_ref, wog_ref, ssem, rsem, lsem):
        h, i = pl.program_id(0), pl.program_id(1)
        _hosted_exchange((h == 0) & (i == 0), (h == N_HEADS - 1) & (i == n_q - 1), [(wo_ref, wog_ref, False)],
                         ssem, rsem, lsem)
        kb, vb = k_ref[0], v_ref[0]
        for r0 in range(0, block_q, ATTN_ROWS):
            rows = slice(r0, r0 + ATTN_ROWS)
            s = _dot_nt(q_ref[0, rows, :], kb)
            m = jnp.max(s, axis=-1, keepdims=True)
            p = jnp.exp2(s - m)
            l = jnp.sum(p, axis=-1, keepdims=True)
            o_ref[rows, :] = _dot(p.astype(BF16), vb) * (1.0 / l)
            lse_ref[0, rows, :] = m + jnp.log2(l)

    hbm = pl.BlockSpec(memory_space=pl.ANY)
    return pl.pallas_call(
        body, name="attn_fwd", grid=(N_HEADS, n_q),
        out_shape=(jax.ShapeDtypeStruct((seq, D_ATTN), F32), jax.ShapeDtypeStruct((N_HEADS, seq, 1), F32),
                   jax.ShapeDtypeStruct((N_DEV,) + w_out_b.shape, w_out_b.dtype)),
        in_specs=[pl.BlockSpec((1, block_q, D_HEAD_PAD), lambda h, i: (h, i, 0)),
                  pl.BlockSpec((1, n_keys, D_HEAD_PAD), lambda h, i: (h, 0, 0)),
                  pl.BlockSpec((1, n_keys, D_V), lambda h, i: (h, 0, 0)), hbm],
        out_specs=(pl.BlockSpec((block_q, D_V), lambda h, i: (i, h)),
                   pl.BlockSpec((1, block_q, 1), lambda h, i: (h, i, 0)), hbm),
        scratch_shapes=[pltpu.SemaphoreType.DMA((1, 7)), pltpu.SemaphoreType.DMA((1, 7)), pltpu.SemaphoreType.DMA((1,))],
        compiler_params=pltpu.CompilerParams(dimension_semantics=("arbitrary", "arbitrary"), vmem_limit_bytes=VMEM_LIMIT),
    )(q, k, v, w_out_b)


def _attn_bwd(q, k, v, o, lse, d_o, block_q, gwo_blocks, gwp):
    _, seq, _ = q.shape
    n_keys = k.shape[1]
    n_q = seq // block_q

    def body(q_ref, k_ref, v_ref, o_ref, lse_ref, do_ref, gwo_ref, gwp_ref, dq_ref, dkt_ref, dvt_ref, lwo_ref, lwp_ref,
             ssem, rsem, lsem):
        h, i = pl.program_id(0), pl.program_id(1)
        _hosted_exchange((h == 0) & (i == 0), (h == N_HEADS - 1) & (i == n_q - 1),
                         [(gwo_ref, lwo_ref, True), (gwp_ref, lwp_ref, False)], ssem, rsem, lsem)

        @pl.when(i == 0)
        def _():
            dkt_ref[...] = jnp.zeros_like(dkt_ref)
            dvt_ref[...] = jnp.zeros_like(dvt_ref)

        kb, vb = k_ref[0], v_ref[0]
        raws, ps = [], []
        for r0 in range(0, block_q, ATTN_ROWS):
            rows = slice(r0, r0 + ATTN_ROWS)
            d_out = do_ref[rows, :]
            p = jnp.exp2(_dot_nt(q_ref[0, rows, :], kb) - lse_ref[0, rows, :])
            dp = _dot_nt(d_out.astype(BF16), vb)
            delta = jnp.sum(d_out * o_ref[rows, :], axis=-1, keepdims=True)
            raw = (p * (dp - delta)).astype(BF16)
            dq_ref[0, rows, :] = _dot(raw, kb)
            raws.append(raw)
            ps.append(p.astype(BF16))
        dkt_ref[0] += _dot_tn(q_ref[0], jnp.concatenate(raws, axis=0))
        dvt_ref[0] += _dot_tn(do_ref[...].astype(BF16), jnp.concatenate(ps, axis=0))

    hbm = pl.BlockSpec(memory_space=pl.ANY)
    return pl.pallas_call(
        body, name="attn_bwd", grid=(N_HEADS, n_q),
        out_shape=(jax.ShapeDtypeStruct((N_HEADS, seq, D_HEAD_PAD), F32),
                   jax.ShapeDtypeStruct((N_HEADS, D_HEAD_PAD, n_keys), F32),
                   jax.ShapeDtypeStruct((N_HEADS, D_V, n_keys), F32),
                   jax.ShapeDtypeStruct(gwo_blocks.shape, gwo_blocks.dtype),
                   jax.ShapeDtypeStruct((N_DEV,) + gwp.shape, gwp.dtype)),
        in_specs=[pl.BlockSpec((1, block_q, D_HEAD_PAD), lambda h, i: (h, i, 0)),
                  pl.BlockSpec((1, n_keys, D_HEAD_PAD), lambda h, i: (h, 0, 0)),
                  pl.BlockSpec((1, n_keys, D_V), lambda h, i: (h, 0, 0)),
                  pl.BlockSpec((block_q, D_V), lambda h, i: (i, h)),
                  pl.BlockSpec((1, block_q, 1), lambda h, i: (h, i, 0)),
                  pl.BlockSpec((block_q, D_V), lambda h, i: (i, h)), hbm, hbm],
        out_specs=(pl.BlockSpec((1, block_q, D_HEAD_PAD), lambda h, i: (h, i, 0)),
                   pl.BlockSpec((1, D_HEAD_PAD, n_keys), lambda h, i: (h, 0, 0)),
                   pl.BlockSpec((1, D_V, n_keys), lambda h, i: (h, 0, 0)), hbm, hbm),
        scratch_shapes=[pltpu.SemaphoreType.DMA((2, 7)), pltpu.SemaphoreType.DMA((2, 7)), pltpu.SemaphoreType.DMA((2,))],
        compiler_params=pltpu.CompilerParams(dimension_semantics=("arbitrary", "arbitrary"), vmem_limit_bytes=VMEM_LIMIT),
    )(q, k, v, o, lse, d_o, gwo_blocks, gwp)


def _mix(x, target, attn, u, modrows, bmodrows, w_pool, pool_scale, w_out):
    seq = x.shape[0]
    rows = min(MIX_TILE, seq // 2)
    n_tiles = seq // rows
    rows8 = rows // HALO
    last8 = seq // HALO - 1

    n_blk = seq // Q_BLOCK
    per = rows // n_blk
    assert rows % n_blk == 0 and per % 8 == 0 and n_tiles >= 2
    n_stash = 8

    def body(x_ref, t_ref, o_hbm, ga_ref, pin_ref, hb_ref, ha_ref, gp_ref, mod_ref, bmod_ref, wp_ref, ps_ref, wo_ref,
             loss_ref, g1_ref, dgate_ref, do_hbm, dga_ref, dgp_ref, dpl_ref, gwob_ref, gwp_ref, dps_ref,
             abuf, dbuf, gwo_ref, *rest):
        stash_even, stash_odd = rest[0:n_stash], rest[n_stash:2 * n_stash]
        rsem, wsem = rest[2 * n_stash:]
        s = pl.program_id(0)

        def slab_copies(tile, slot, fetch):
            window = pl.ds(tile * per, per)
            if fetch:
                return [pltpu.make_async_copy(o_hbm.at[:, window, :], abuf.at[slot], rsem.at[slot])]
            return [pltpu.make_async_copy(dbuf.at[slot], do_hbm.at[:, window, :], wsem.at[slot])]

        def wait_all(slot, fetch):
            slab_copies(0, slot, fetch)[0].wait()

        a_slot = lax.rem(s, 2)
        b_slot = 1 - a_slot

        @pl.when(s == 0)
        def _():
            loss_ref[...] = jnp.zeros_like(loss_ref)
            dgate_ref[...] = jnp.zeros_like(dgate_ref)
            gwo_ref[...] = jnp.zeros_like(gwo_ref)
            gwp_ref[...] = jnp.zeros_like(gwp_ref)
            dps_ref[...] = jnp.zeros_like(dps_ref)
            for cp in slab_copies(0, 0, True):
                cp.start()

        @pl.when(s + 1 < n_tiles)
        def _():
            for cp in slab_copies(s + 1, b_slot, True):
                cp.start()

        @pl.when(s < n_tiles)
        def _():
            wait_all(a_slot, True)

        @pl.when(s >= 3)
        def _():
            wait_all(b_slot, False)

        gate = mod_ref[2:3, :] + bmod_ref[2:3, :]
        ps = ps_ref[...]

        def a_vector(slot):
            attn_t = jnp.swapaxes(abuf[slot], 0, 1).reshape(rows, D_ATTN)
            ga = ga_ref[...].astype(F32)
            sga = _sigmoid(ga)
            silu_ga = ga * sga
            pin = pin_ref[...].astype(F32)
            xs = jnp.concatenate([jnp.where(s > 0, hb_ref[...].astype(F32), 0.0), pin,
                                  jnp.where(s < n_tiles - 1, ha_ref[...].astype(F32), 0.0)], axis=0)
            tpos = s * rows + lax.broadcasted_iota(jnp.int32, (rows, 1), 0)
            pooled = []
            for g, w in enumerate(POOL_WINDOWS):
                sl = slice(g * GROUP_DIM, (g + 1) * GROUP_DIM)
                ws = _window_sum(xs[:, sl], w, False)[HALO:HALO + rows]
                pooled.append((ws * _inv_count(tpos, w, seq) - pin[:, sl]).astype(BF16))
            return attn_t, ga, sga, silu_ga, pooled

        def a_group_maps(pooled):
            return jnp.concatenate([_dot(pooled[g], wp_ref[g].astype(BF16)) for g in range(len(POOL_WINDOWS))], axis=1)

        def a_project(stash, yp, attn_t, ga, sga, silu_ga, pooled):
            zb_ref, _, fa_ref, sa_ref, fp_ref, sp_ref, yp_ref, pooled_ref = stash
            ypool = yp * ps
            gp = gp_ref[...].astype(F32)
            sgp = _sigmoid(gp)
            silu_gp = gp * sgp
            z = jnp.concatenate([silu_ga * attn_t, silu_gp * ypool], axis=1).astype(BF16)
            y = _dot(z, wo_ref[...])
            zb_ref[...] = z
            fa_ref[...] = attn_t * (sga * (1.0 + ga * (1.0 - sga)))
            sa_ref[...] = silu_ga
            fp_ref[...] = ypool * (sgp * (1.0 + gp * (1.0 - sgp)))
            sp_ref[...] = silu_gp
            yp_ref[...] = yp
            pooled_ref[...] = jnp.concatenate(pooled, axis=1)
            return y

        def a_loss(stash, y):
            res = x_ref[...] + gate * y - t_ref[...]
            loss_ref[...] += jnp.sum(res * res) * (0.5 / D_MODEL)
            dxn = res * (1.0 / D_MODEL)
            g1_ref[...] = dxn
            dgate_ref[...] += jnp.sum(dxn * y, axis=0, keepdims=True)
            stash[1][...] = (gate * dxn).astype(BF16)

        def b_dz(stash):
            return _dot_nt(stash[1][...], wo_ref[...])

        def b_gwo(stash):
            gwo_ref[...] += _dot_tn(stash[0][...], stash[1][...])

        def b_vector(stash, slot, dz):
            _, _, fa_ref, sa_ref, fp_ref, sp_ref, yp_ref, _ = stash
            dbra = dz[:, 0:D_ATTN]
            dbrp = dz[:, D_ATTN:]
            dga_ref[...] = (dbra * fa_ref[...]).astype(BF16)
            dbuf[slot] = jnp.swapaxes((dbra * sa_ref[...]).reshape(per, n_blk, D_ATTN), 0, 1)
            dgp_ref[...] = (dbrp * fp_ref[...]).astype(BF16)
            dyp_s = dbrp * sp_ref[...]
            dps_ref[...] += jnp.sum(dyp_s * yp_ref[...], axis=0, keepdims=True)
            return (dyp_s * ps).astype(BF16)

        def b_small(stash, dyp):
            pooled_ref = stash[7]
            for g in range(len(POOL_WINDOWS)):
                sl = slice(g * GROUP_DIM, (g + 1) * GROUP_DIM)
                gwp_ref[g] += _dot_tn(pooled_ref[:, sl], dyp[:, sl])
                dpl_ref[:, sl] = _dot_nt(dyp[:, sl], wp_ref[g].astype(BF16)).astype(BF16)

        def both(a_stash, b_stash, slot_a):
            dz = b_dz(b_stash)
            front = a_vector(slot_a)
            yp = a_group_maps(front[-1])
            b_gwo(b_stash)
            y = a_project(a_stash, yp, *front)
            dyp = b_vector(b_stash, 1 - slot_a, dz)
            a_loss(a_stash, y)
            b_small(b_stash, dyp)

        @pl.when(s == 0)
        def _():
            front = a_vector(0)
            a_loss(stash_even, a_project(stash_even, a_group_maps(front[-1]), *front))

        @pl.when((s > 0) & (s < n_tiles) & (a_slot == 0))
        def _():
            both(stash_even, stash_odd, 0)

        @pl.when((s > 0) & (s < n_tiles) & (a_slot == 1))
        def _():
            both(stash_odd, stash_even, 1)

        @pl.when(s == n_tiles)
        def _():
            last = (n_tiles - 1) % 2
            stash = stash_odd if last else stash_even
            dz = b_dz(stash)
            b_gwo(stash)
            b_small(stash, b_vector(stash, last, dz))
            gwob_ref[...] = gwo_ref[...].astype(BF16)

        @pl.when(s >= 1)
        def _():
            for cp in slab_copies(s - 1, b_slot, False):
                cp.start()

        @pl.when(s == n_tiles)
        def _():
            wait_all(b_slot, False)
            wait_all(a_slot, False)

    ta = lambda s: jnp.minimum(s, n_tiles - 1)
    tb = lambda s: jnp.maximum(s - 1, 0)
    tile = lambda w: pl.BlockSpec((rows, w), lambda s: (ta(s), 0))
    tile_b = lambda w: pl.BlockSpec((rows, w), lambda s: (tb(s), 0))
    ucol = lambda col: pl.BlockSpec((rows, 512), lambda s: (ta(s), col))
    full = lambda shape: pl.BlockSpec(shape, lambda s: (0,) * len(shape))
    stash_shapes = [pltpu.VMEM((rows, D_MODEL), BF16), pltpu.VMEM((rows, D_MODEL), BF16)] + \
        [pltpu.VMEM((rows, 512), F32)] * 5 + [pltpu.VMEM((rows, D_POOL), BF16)]
    return pl.pallas_call(
        body, name="mix_fwd_bwd", grid=(n_tiles + 1,),
        out_shape=(jax.ShapeDtypeStruct((8, 128), F32), jax.ShapeDtypeStruct((seq, D_MODEL), F32),
                   jax.ShapeDtypeStruct((1, D_MODEL), F32), jax.ShapeDtypeStruct((n_blk, Q_BLOCK, D_ATTN), F32),
                   jax.ShapeDtypeStruct((seq, D_ATTN), BF16), jax.ShapeDtypeStruct((seq, D_POOL), BF16),
                   jax.ShapeDtypeStruct((seq, D_POOL), BF16), jax.ShapeDtypeStruct((D_MODEL, D_MODEL), BF16),
                   jax.ShapeDtypeStruct((4, GROUP_DIM, GROUP_DIM), F32), jax.ShapeDtypeStruct((1, D_POOL), F32)),
        in_specs=[tile(D_MODEL), tile(D_MODEL), pl.BlockSpec(memory_space=pl.ANY), ucol(0), ucol(1),
                  pl.BlockSpec((HALO, 512), lambda s: (jnp.maximum(ta(s) * rows8 - 1, 0), 1)),
                  pl.BlockSpec((HALO, 512), lambda s: (jnp.minimum((ta(s) + 1) * rows8, last8), 1)),
                  ucol(2), full((8, D_MODEL)), full((8, D_MODEL)), full((4, GROUP_DIM, GROUP_DIM)), full((1, D_POOL)),
                  full((D_MODEL, D_MODEL))],
        out_specs=(full((8, 128)), tile(D_MODEL), full((1, D_MODEL)), pl.BlockSpec(memory_space=pl.ANY), tile_b(D_ATTN),
                   tile_b(D_POOL), tile_b(D_POOL), full((D_MODEL, D_MODEL)), full((4, GROUP_DIM, GROUP_DIM)),
                   full((1, D_POOL))),
        scratch_shapes=[pltpu.VMEM((2, n_blk, per, D_ATTN), F32), pltpu.VMEM((2, n_blk, per, D_ATTN), F32),
                        pltpu.VMEM((D_MODEL, D_MODEL), F32), *stash_shapes, *stash_shapes,
                        pltpu.SemaphoreType.DMA((2,)), pltpu.SemaphoreType.DMA((2,))],
        compiler_params=pltpu.CompilerParams(dimension_semantics=("arbitrary",), vmem_limit_bytes=VMEM_LIMIT),
    )(x, target, attn.reshape(n_blk, Q_BLOCK, D_ATTN), u, u, u, u, u, modrows, bmodrows, w_pool, pool_scale, w_out)


def _inproj_bwd(x, ctx, modrows, bmodrows, norm_g, w_in_p, q_lora_g, w_uq_p, kv_lora_g, w_ukv, qng_p, kng_p, cos, slo, shi,
                u, dq, dk, dv, dga, dgp, dpl, g1):
    seq = x.shape[0]
    n_tiles = seq // TILE + 1
    rows8 = TILE // HALO
    last8 = seq // HALO - 1

    def body(*refs):
        du_even, du_odd, dh_even, dh_odd = refs[-4:]
        gwin_ref, gwuq_ref, gwukv_ref, dqlg_ref, dkvlg_ref, dqng_ref, dkng_ref, dng_ref, dmod_ref = refs[-13:-4]
        s = pl.program_id(0)

        @pl.when(s == 0)
        def _():
            for ref in (gwin_ref, gwuq_ref, gwukv_ref, dqlg_ref, dkvlg_ref, dqng_ref, dkng_ref, dng_ref, dmod_ref,
                        du_odd, dh_even):
                ref[...] = jnp.zeros_like(ref)

        @pl.when(lax.rem(s, 2) == 0)
        def _():
            stages(s, refs[:-4], du_even, du_odd, dh_odd, dh_even)

        @pl.when(lax.rem(s, 2) == 1)
        def _():
            stages(s, refs[:-4], du_odd, du_even, dh_even, dh_odd)

    def stages(s, refs, du_ref, du_prev, dh_fill, dh_prev):
        (xb_ref, xc_ref, ctx_ref, mod_ref, bmod_ref, ng_ref, win_ref, qlg_ref, wuq_ref, kvlg_ref, wukv_ref, qng_ref, kng_ref,
         cos_ref, slo_ref, shi_ref, cq_ref, ckv_ref, kr_ref, dq_ref, dk_ref, dv_ref, dga_ref, dgp_ref, dpl_ref,
         hb_ref, ha_ref, g1_ref,
         gx_ref, gwin_ref, gwuq_ref, gwukv_ref, dqlg_ref, dkvlg_ref, dqng_ref, dkng_ref, dng_ref, dmod_ref) = refs


        i = jnp.minimum(s, n_tiles - 1)
        valid = s < n_tiles
        is_lat = (s > 0) & valid
        cq = cq_ref[...]
        rq = lax.rsqrt(jnp.mean(cq * cq, axis=-1, keepdims=True) + NORM_EPS)
        qhat = cq * rq
        qnb = (qhat * qlg_ref[...]).astype(BF16)
        q = _dot_nt(qnb, wuq_ref[...])
        ckv = ckv_ref[...]
        rkv = lax.rsqrt(jnp.mean(ckv * ckv, axis=-1, keepdims=True) + NORM_EPS)
        kvhat = ckv * rkv
        kvnb = (kvhat * kvlg_ref[...]).astype(BF16)
        kv = _dot(kvnb, wukv_ref[...])

        _, _, _, h2, _ = _modulated_input(s <= 1, xb_ref, ctx_ref, mod_ref, bmod_ref, ng_ref)
        dub = du_prev[...]
        du_g, du_m = dub[:, 0:N_GATES], dub[:, N_GATES:U_PAD]
        h2b = h2.astype(BF16)
        dh_fill[...] = _dot(du_g, win_ref[N_MLA:D_IN_PROJ, :]) + _dot(du_m, win_ref[0:U_PAD - N_GATES, :])
        gwin_ref[N_MLA:D_IN_PROJ, :] += _dot_tn(du_g, h2b)
        gwin_ref[0:N_MLA, :] += _dot_tn(du_m, h2b)[0:N_MLA]

        ctx3 = s <= 2
        xt, r, xg, _, scale = _modulated_input(ctx3, xc_ref, ctx_ref, mod_ref, bmod_ref, ng_ref)
        dh = dh_prev[...]
        dshift = jnp.sum(dh, axis=0, keepdims=True)
        dscale = jnp.sum(dh * xg, axis=0, keepdims=True)
        zero = jnp.zeros_like(dshift)
        dmod_ref[0:1, :] += jnp.where(ctx3, zero, dshift)
        dmod_ref[1:2, :] += jnp.where(ctx3, zero, dscale)
        dmod_ref[2:3, :] += jnp.where(ctx3, dshift, zero)
        dmod_ref[3:4, :] += jnp.where(ctx3, dscale, zero)
        dxg = dh * (1.0 + scale)
        xr = xt * r
        dng_ref[...] += jnp.sum(dxg * xr, axis=0, keepdims=True)
        dxn = dxg * ng_ref[...]
        gx_ref[...] = g1_ref[...] + r * (dxn - xr * jnp.mean(dxn * xr, axis=-1, keepdims=True))

        cos_t, slo_t, shi_t = cos_ref[...], slo_ref[...], shi_ref[...]
        qg = qng_ref[...]
        dq_heads = []
        dqng = jnp.zeros((1, D_HEAD_PAD), F32)
        for hd in range(N_HEADS):
            qh = q[:, hd * D_HEAD_PAD:(hd + 1) * D_HEAD_PAD]
            rr = lax.rsqrt(jnp.sum(qh * qh, axis=-1, keepdims=True) * (1.0 / D_HEAD) + NORM_EPS)
            yq = qh * rr
            dpost = jnp.where(is_lat, dq_ref[hd] * ATTN_SCALE, 0.0)
            dyn = jnp.concatenate([dpost[:, 0:D_NOPE], _rope_t(dpost[:, D_NOPE:], cos_t, slo_t, shi_t)], axis=1)
            dqng = dqng + jnp.sum(dyn * yq, axis=0, keepdims=True)
            dyg = dyn * qg
            dq_heads.append(rr * (dyg - yq * (jnp.sum(dyg * yq, axis=-1, keepdims=True) * (1.0 / D_HEAD))))
        dqng_ref[...] += dqng
        dqf = jnp.concatenate(dq_heads, axis=1).astype(BF16)

        krz = kr_ref[...]
        kr_ss = jnp.sum(krz * krz, axis=-1, keepdims=True)
        kg = kng_ref[...]
        kg_n, kg_r = kg[:, 0:D_NOPE], kg[:, D_NOPE:]
        dkv_parts = []
        dkr = jnp.zeros((TILE, 128), F32)
        dkng_n = jnp.zeros((1, D_NOPE), F32)
        dkng_r = jnp.zeros((1, 128), F32)
        for hd in range(N_HEADS):
            kn = kv[:, hd * 256:hd * 256 + D_NOPE]
            rr = lax.rsqrt((jnp.sum(kn * kn, axis=-1, keepdims=True) + kr_ss) * (1.0 / D_HEAD) + NORM_EPS)
            yn, yr = kn * rr, krz * rr
            dk_h = jnp.where(valid, jnp.transpose(dk_ref[hd]) * LN_2, 0.0)
            dpn = dk_h[:, 0:D_NOPE]
            dpr = _rope_t(dk_h[:, D_NOPE:], cos_t, slo_t, shi_t)
            dkng_n = dkng_n + jnp.sum(dpn * yn, axis=0, keepdims=True)
            dkng_r = dkng_r + jnp.sum(dpr * yr, axis=0, keepdims=True)
            dyn, dyr = dpn * kg_n, dpr * kg_r
            proj = (jnp.sum(dyn * yn, axis=-1, keepdims=True) + jnp.sum(dyr * yr, axis=-1, keepdims=True)) * (1.0 / D_HEAD)
            dkv_parts.append(rr * (dyn - yn * proj))
            dkv_parts.append(jnp.where(valid, jnp.transpose(dv_ref[hd]), 0.0))
            dkr = dkr + rr * (dyr - yr * proj)
        dkng_ref[:, 0:D_NOPE] += dkng_n
        dkng_ref[:, D_NOPE:] += dkng_r
        dkvf = jnp.concatenate(dkv_parts, axis=1).astype(BF16)

        lat_t = jnp.maximum(i - 1, 0)
        dpl_t = dpl_ref[...].astype(F32)
        ds = jnp.concatenate([jnp.where(i > 1, hb_ref[...].astype(F32), 0.0), dpl_t,
                              jnp.where(i < n_tiles - 1, ha_ref[...].astype(F32), 0.0)], axis=0)
        tpos = lat_t * TILE - HALO + lax.broadcasted_iota(jnp.int32, (TILE + 2 * HALO, 1), 0)
        for g, w in enumerate(POOL_WINDOWS):
            sl = slice(g * GROUP_DIM, (g + 1) * GROUP_DIM)
            wsum = _window_sum(ds[:, sl] * _inv_count(tpos, w, seq), w, True)[HALO:HALO + TILE]
            du_ref[:, O_PIN + g * GROUP_DIM:O_PIN + (g + 1) * GROUP_DIM] = jnp.where(
                is_lat, wsum - dpl_t[:, sl], 0.0).astype(BF16)

        du_ref[:, O_GA:O_GA + D_ATTN] = jnp.where(is_lat, dga_ref[...], jnp.zeros((), BF16))
        du_ref[:, O_GP:O_GP + D_POOL] = jnp.where(is_lat, dgp_ref[...], jnp.zeros((), BF16))
        du_ref[:, O_KR:U_PAD] = dkr.astype(BF16)

        gwuq_ref[...] += _dot_tn(dqf, qnb)
        dqn = _dot(dqf, wuq_ref[...])
        gwukv_ref[...] += _dot_tn(dkvf, kvnb)
        dkvn = _dot_nt(dkvf, wukv_ref[...])
        dqlg_ref[...] += jnp.sum(dqn * qhat, axis=0, keepdims=True)
        dqhat = dqn * qlg_ref[...]
        dcq = rq * (dqhat - qhat * jnp.mean(dqhat * qhat, axis=-1, keepdims=True))
        dkvlg_ref[...] += jnp.sum(dkvn * kvhat, axis=0, keepdims=True)
        dkvhat = dkvn * kvlg_ref[...]
        dckv = rkv * (dkvhat - kvhat * jnp.mean(dkvhat * kvhat, axis=-1, keepdims=True))
        du_ref[:, O_CQ:O_CQ + R_Q] = dcq.astype(BF16)
        du_ref[:, O_CKV:O_CKV + R_KV] = dckv.astype(BF16)

    t1 = lambda s: jnp.minimum(s, n_tiles - 1)
    t2 = lambda s: jnp.clip(s - 1, 0, n_tiles - 1)
    t3 = lambda s: jnp.clip(s - 2, 0, n_tiles - 1)
    latent = lambda t: jnp.maximum(t - 1, 0)
    lat = lambda s: (latent(t1(s)), 0)
    lat3 = lambda s: (latent(t3(s)), 0)
    full = lambda shape: pl.BlockSpec(shape, lambda s: (0,) * len(shape))
    rope_spec = pl.BlockSpec((TILE, 128), lambda s: (t1(s), 0))
    return pl.pallas_call(
        body, name="inproj_bwd", grid=(n_tiles + 2,),
        out_shape=(jax.ShapeDtypeStruct((seq, D_MODEL), F32), jax.ShapeDtypeStruct((D_IN_PROJ, D_MODEL), F32),
                   jax.ShapeDtypeStruct((N_HEADS * D_HEAD_PAD, R_Q), F32), jax.ShapeDtypeStruct((N_HEADS * 256, R_KV), F32),
                   jax.ShapeDtypeStruct((1, R_Q), F32), jax.ShapeDtypeStruct((1, R_KV), F32),
                   jax.ShapeDtypeStruct((1, D_HEAD_PAD), F32), jax.ShapeDtypeStruct((1, D_HEAD_PAD), F32),
                   jax.ShapeDtypeStruct((1, D_MODEL), F32), jax.ShapeDtypeStruct((8, D_MODEL), F32)),
        in_specs=[pl.BlockSpec((TILE, D_MODEL), lambda s: (latent(t2(s)), 0)), pl.BlockSpec((TILE, D_MODEL), lat3),
                  full((TILE, D_MODEL)), full((8, D_MODEL)), full((8, D_MODEL)),
                  full((1, D_MODEL)), full((D_IN_PROJ, D_MODEL)), full((1, R_Q)), full((N_HEADS * D_HEAD_PAD, R_Q)),
                  full((1, R_KV)), full((R_KV, N_HEADS * 256)), full((1, D_HEAD_PAD)), full((1, D_HEAD_PAD)),
                  rope_spec, rope_spec, rope_spec,
                  pl.BlockSpec((TILE, R_Q), lambda s: (t1(s), (O_CQ - N_GATES) // R_Q)),
                  pl.BlockSpec((TILE, R_KV), lambda s: (t1(s), (O_CKV - N_GATES) // R_KV)),
                  pl.BlockSpec((TILE, 128), lambda s: (t1(s), (O_KR - N_GATES) // 128)),
                  pl.BlockSpec((N_HEADS, TILE, D_HEAD_PAD), lambda s: (0, latent(t1(s)), 0)),
                  pl.BlockSpec((N_HEADS, D_HEAD_PAD, TILE), lambda s: (0, 0, t1(s))),
                  pl.BlockSpec((N_HEADS, D_V, TILE), lambda s: (0, 0, t1(s))),
                  pl.BlockSpec((TILE, D_ATTN), lat), pl.BlockSpec((TILE, D_POOL), lat), pl.BlockSpec((TILE, D_POOL), lat),
                  pl.BlockSpec((HALO, D_POOL), lambda s: (jnp.maximum((t1(s) - 1) * rows8 - 1, 0), 0)),
                  pl.BlockSpec((HALO, D_POOL), lambda s: (jnp.minimum(jnp.maximum(t1(s), 1) * rows8, last8), 0)),
                  pl.BlockSpec((TILE, D_MODEL), lat3)],
        out_specs=(pl.BlockSpec((TILE, D_MODEL), lat3), full((D_IN_PROJ, D_MODEL)), full((N_HEADS * D_HEAD_PAD, R_Q)),
                   full((N_HEADS * 256, R_KV)), full((1, R_Q)), full((1, R_KV)), full((1, D_HEAD_PAD)),
                   full((1, D_HEAD_PAD)), full((1, D_MODEL)), full((8, D_MODEL))),
        scratch_shapes=[pltpu.VMEM((TILE, U_PAD), BF16), pltpu.VMEM((TILE, U_PAD), BF16),
                        pltpu.VMEM((TILE, D_MODEL), F32), pltpu.VMEM((TILE, D_MODEL), F32)],
        compiler_params=pltpu.CompilerParams(dimension_semantics=("arbitrary",), vmem_limit_bytes=VMEM_LIMIT),
    )(x, x, ctx, modrows, bmodrows, norm_g, w_in_p, q_lora_g, w_uq_p, kv_lora_g, w_ukv, qng_p, kng_p, cos, slo, shi,
      u, u, u, dq, dk, dv, dga, dgp, dpl, dpl, dpl, g1)


SMALL_LAYOUT = ((0, D_MODEL), (1, R_Q), (2, R_KV), (3, D_HEAD_PAD), (4, D_HEAD_PAD), (5, D_POOL))
ROW_DMOD_B, ROW_DMOD_C, ROW_LOSS = 6, 9, 12


def _epilogue(parts, landed, smalls, dmod4, dgate, loss_acc, w_mod_l):
    n_a, n_l, n_s = len(parts), len(landed), len(smalls)
    n_mod = w_mod_l.shape[1]
    blk = [p.shape[1:] for p in parts]
    small_out = [D_MODEL, R_Q, R_KV, D_HEAD, D_HEAD, D_POOL]

    def body(*refs):
        part_refs = refs[0:n_a]
        land_refs = refs[n_a:n_a + n_l]
        small_in = refs[n_a + n_l:n_a + n_l + n_s]
        dmod4_ref, dgate_ref, loss_ref, wmod_ref = refs[n_a + n_l + n_s:n_a + n_l + n_s + 4]
        outs = refs[n_a + n_l + n_s + 4:]
        red_refs = outs[0:n_a]
        landsum_refs = outs[n_a:n_a + n_l]
        ccg_ref = outs[n_a + n_l]
        sum_refs = outs[n_a + n_l + 1:n_a + n_l + 1 + n_s]
        gbmod_ref, dmy_ref, lossout_ref = outs[n_a + n_l + 1 + n_s:n_a + n_l + 4 + n_s]
        scr = outs[n_a + n_l + 4 + n_s:]
        recv1, own1, sum1b, recv2 = scr[0:n_a], scr[n_a:2 * n_a], scr[2 * n_a:3 * n_a], scr[3 * n_a:4 * n_a]
        small_ref, smallg_ref, tot_ref, cc_ref = scr[4 * n_a:4 * n_a + 4]
        ssem1, rsem1, lsem, ssem2, rsem2, ssem_s, rsem_s, ssem_cc, rsem_cc = scr[4 * n_a + 4:]
        x, y, c = _coords()
        me = (x, y, c)
        sibling = (x, y, 1 - c)

        stage1, own_copies = [], []
        for a in range(n_a):
            for b in range(4):
                dev = 4 * (b >> 1) + 2 * (b & 1)
                stage1.append(pltpu.make_async_remote_copy(
                    src_ref=part_refs[a].at[dev + (1 - c)], dst_ref=recv1[a].at[b],
                    send_sem=ssem1.at[a, b], recv_sem=rsem1.at[a, b], device_id=sibling, device_id_type=MESH))
                own_copies.append(pltpu.make_async_copy(part_refs[a].at[dev + c], own1[a].at[b], lsem.at[a, b]))
                stage1[-1].start()
                own_copies[-1].start()

        small_ref[...] = jnp.zeros_like(small_ref)
        for ref, (row, width) in zip(small_in, SMALL_LAYOUT):
            small_ref[row:row + 1, 0:width] = ref[...]
        small_ref[ROW_DMOD_B:ROW_DMOD_B + 2, :] = dmod4_ref[0:2, :]
        small_ref[ROW_DMOD_B + 2:ROW_DMOD_B + 3, :] = dgate_ref[...]
        small_ref[ROW_DMOD_C:ROW_DMOD_C + 2, :] = dmod4_ref[2:4, :]
        small_ref[ROW_LOSS:ROW_LOSS + 1, 0:128] = loss_ref[0:1, :]
        smallg_ref[_lin(me)] = small_ref[...]
        s_recv, s_send = _gather_to_all(small_ref, smallg_ref, ssem_s, rsem_s)
        s_recv()
        tot = smallg_ref[0]
        for d in range(1, N_DEV):
            tot = tot + smallg_ref[d]
        tot_ref[...] = tot
        for ref, (row, _), width in zip(sum_refs, SMALL_LAYOUT, small_out):
            ref[...] = tot_ref[row:row + 1, 0:width]
        lossout_ref[...] = tot_ref[8:16, 0:128]
        lin = _lin(me)

        def my_columns(three_rows):
            v = jnp.concatenate(three_rows, axis=1)
            out = jnp.zeros((1, n_mod), F32)
            for d in range(N_DEV):
                out = out + jnp.where(lin == d, v[:, d * n_mod:(d + 1) * n_mod], 0.0)
            return out

        rows3 = lambda ref, r0: [ref[r0 + t:r0 + t + 1, :] for t in range(3)]
        dmodc = rows3(tot_ref, ROW_DMOD_C)
        gbmod_ref[...] = jnp.concatenate(rows3(tot_ref, ROW_DMOD_B), axis=1) + jnp.concatenate(dmodc, axis=1)
        dmy_ref[...] = jnp.zeros_like(dmy_ref)
        for b in range(N_DEV):
            dmy_ref[b:b + 1, :] = my_columns(rows3(smallg_ref.at[b], ROW_DMOD_B))
        dmy = my_columns(dmodc)
        dmy_ref[N_DEV:N_DEV + 1, :] = dmy
        part = lax.dot_general(jnp.broadcast_to(dmy, (8, n_mod)), wmod_ref[...], (((1,), (1,)), ((), ())),
                               precision=HIGHEST, preferred_element_type=F32)

        cc_ref[...] = part
        ccg_ref[_lin(me)] = part
        cc_recv, cc_send = _gather_to_all(cc_ref, ccg_ref, ssem_cc, rsem_cc)

        stage2 = []
        for a in range(n_a):
            for b in range(4):
                stage1[4 * a + b].wait_recv()
                own_copies[4 * a + b].wait()
                sum1b[a][b] = (own1[a][b] + recv1[a][b]).astype(BF16)
            for k in (1, 2, 3):
                tx, ty = _flip(x, (k >> 1) & 1), _flip(y, k & 1)
                stage2.append(pltpu.make_async_remote_copy(
                    src_ref=sum1b[a].at[2 * tx + ty], dst_ref=recv2[a].at[k - 1], send_sem=ssem2.at[a, k - 1],
                    recv_sem=rsem2.at[a, k - 1], device_id=(tx, ty, c), device_id_type=MESH))
                stage2[-1].start()
        for a in range(n_a):
            own = own1[a][2 * x + y] + recv1[a][2 * x + y]
            for k in (1, 2, 3):
                stage2[3 * a + k - 1].wait_recv()
                own = own + recv2[a][k - 1].astype(F32)
            red_refs[a][...] = own

        for ref, out in zip(land_refs, landsum_refs):
            acc = ref[0].astype(F32)
            for d in range(1, N_DEV):
                acc = acc + ref[d].astype(F32)
            out[...] = acc
        cc_recv()
        for cp in stage1 + stage2:
            cp.wait_send()
        s_send()
        cc_send()

    vm = pl.BlockSpec(memory_space=pltpu.VMEM)
    sds = jax.ShapeDtypeStruct
    return pl.pallas_call(
        body, name="epilogue_reduce",
        out_shape=(*[sds(s, F32) for s in blk], *[sds(a.shape[1:], F32) for a in landed], sds((N_DEV, 8, D_MODEL), F32),
                   *[sds((1, w), F32) for w in small_out], sds((1, 3 * D_MODEL), F32), sds((16, n_mod), F32),
                   sds((8, 128), F32)),
        in_specs=[pl.BlockSpec(memory_space=pl.ANY)] * n_a + [vm] * (n_l + n_s + 4),
        out_specs=tuple([vm] * (n_a + n_l + n_s + 4)),
        scratch_shapes=[*[pltpu.VMEM((4,) + s, F32) for s in blk], *[pltpu.VMEM((4,) + s, F32) for s in blk],
                        *[pltpu.VMEM((4,) + s, BF16) for s in blk], *[pltpu.VMEM((3,) + s, BF16) for s in blk],
                        pltpu.VMEM((SMALL_ROWS, D_MODEL), F32), pltpu.VMEM((N_DEV, SMALL_ROWS, D_MODEL), F32),
                        pltpu.VMEM((SMALL_ROWS, D_MODEL), F32), pltpu.VMEM((8, D_MODEL), F32),
                        pltpu.SemaphoreType.DMA((n_a, 4)), pltpu.SemaphoreType.DMA((n_a, 4)), pltpu.SemaphoreType.DMA((n_a, 4)),
                        pltpu.SemaphoreType.DMA((n_a, 3)), pltpu.SemaphoreType.DMA((n_a, 3)),
                        pltpu.SemaphoreType.DMA((7,)), pltpu.SemaphoreType.DMA((7,)),
                        pltpu.SemaphoreType.DMA((7,)), pltpu.SemaphoreType.DMA((7,))],
        compiler_params=pltpu.CompilerParams(vmem_limit_bytes=VMEM_LIMIT),
    )(*parts, *landed, *smalls, dmod4, dgate, loss_acc, w_mod_l)


def _adamw(weights, grads, ms, vs, c_all_t, dmod_my, p2g):
    n = len(weights)

    def body(*refs):
        w_refs = refs[0:n]
        g_in = refs[n:2 * n - 2]
        m_refs = refs[2 * n - 2:3 * n - 2]
        v_refs = refs[3 * n - 2:4 * n - 2]
        cat_ref, dmod_ref, p2g_ref = refs[4 * n - 2:4 * n + 1]
        outs = refs[4 * n + 1:]
        g_refs, d_refs, nm_refs, nv_refs = outs[0:n], outs[n:2 * n], outs[2 * n:3 * n], outs[3 * n:4 * n]
        gcc_ref, gwm_ref = g_refs[0], g_refs[1]

        part = p2g_ref[0, 0:1, :]
        for d in range(1, N_DEV):
            part = part + p2g_ref[d, 0:1, :]
        cc = w_refs[0][...]
        sg = _sigmoid(cc)
        gcc_ref[...] = part * (sg * (1.0 + cc * (1.0 - sg)))
        cat = cat_ref[...]
        gwm_ref[...] = lax.dot_general(cat * _sigmoid(cat), dmod_ref[...], (((1,), (0,)), ((), ())), precision=HIGHEST,
                                       preferred_element_type=F32)
        for idx in range(n):
            if idx >= 2:
                g_refs[idx][...] = g_in[idx - 2][...]
            g = g_refs[idx][...]
            m = ADAM_B1 * m_refs[idx][...] + (1.0 - ADAM_B1) * g
            v = ADAM_B2 * v_refs[idx][...] + (1.0 - ADAM_B2) * (g * g)
            m_hat = m / (1.0 - ADAM_B1 ** ADAM_STEP)
            v_hat = v / (1.0 - ADAM_B2 ** ADAM_STEP)
            d_refs[idx][...] = -ADAM_LR * (m_hat / (jnp.sqrt(v_hat) + ADAM_EPS) + ADAM_WD * w_refs[idx][...])
            nm_refs[idx][...] = m
            nv_refs[idx][...] = v

    vm = pl.BlockSpec(memory_space=pltpu.VMEM)
    shapes = [jax.ShapeDtypeStruct(w.shape, F32) for w in weights]
    args = list(weights) + list(grads[2:]) + list(ms) + list(vs) + [c_all_t, dmod_my, p2g]
    out_shape = tuple(shapes * 4)
    return pl.pallas_call(
        body, name="adamw_update", out_shape=out_shape, in_specs=[vm] * len(args), out_specs=tuple([vm] * len(out_shape)),
        compiler_params=pltpu.CompilerParams(vmem_limit_bytes=VMEM_LIMIT),
    )(*args)


def _rope_tables(seq):
    rows = seq // GRID_W
    row = np.repeat(np.arange(rows, dtype=np.float32), GRID_W)
    col = np.tile(np.arange(GRID_W, dtype=np.float32), rows)
    n_freq = D_ROPE // 4
    inv = (np.float32(ROPE_BASE) ** (-np.arange(n_freq, dtype=np.float32) / np.float32(n_freq))).astype(np.float32)
    ang_r = (row[:, None] * inv).astype(np.float32)
    ang_c = (col[:, None] * inv).astype(np.float32)
    ang = np.concatenate([ang_r, ang_r, ang_c, ang_c], axis=-1)
    cos, sin = np.cos(ang).astype(np.float32), np.sin(ang).astype(np.float32)
    low = (np.arange(D_ROPE) % 32) < 16

    def table(t, fill):
        out = np.full((TILE + seq, 128), fill, np.float32)
        out[TILE:, 0:D_ROPE] = t
        return jnp.asarray(out)

    return table(cos, 1.0), table(np.where(low, -sin, 0.0), 0.0), table(np.where(low, 0.0, sin), 0.0)


def kernel(x, c, ctx, c_ctx, w_mod, b_mod, norm_g, w_in, q_lora_g, w_uq, kv_lora_g, w_ukv, q_norm_g, k_norm_g, w_pool, pool_scale, w_out, loss_target, m_c_ctx, m_w_mod, m_b_mod, m_norm_g, m_w_in, m_q_lora_g, m_w_uq, m_kv_lora_g, m_w_ukv, m_q_norm_g, m_k_norm_g, m_w_pool, m_pool_scale, m_w_out, v_c_ctx, v_w_mod, v_b_mod, v_norm_g, v_w_in, v_q_lora_g, v_w_uq, v_kv_lora_g, v_w_ukv, v_q_norm_g, v_k_norm_g, v_w_pool, v_pool_scale, v_w_out):
    seq = x.shape[1]
    assert ctx.shape[1] == TILE and seq % TILE == 0 and seq % GRID_W == 0
    ix, iy, ic = lax.axis_index("x"), lax.axis_index("y"), lax.axis_index("c")
    me = 4 * ix + 2 * iy + ic
    x2, ctx2, tgt2 = x[0], ctx[0], loss_target[0]
    w_mod_l, w_ukv_l, w_out_l = w_mod[0], w_ukv[0], w_out[0]
    c_ctx_row = c_ctx.reshape(1, D_MODEL)

    tr = lambda a: jnp.transpose(a[0])
    w_in_tl, w_uq_tl = tr(w_in), tr(w_uq)
    cg, modg, wg_in, wg_uq, wg_ukv = _prologue(
        c, c_ctx_row, w_mod_l,
        [w_in_tl, w_uq_tl, w_ukv_l])
    mod_all = jnp.transpose(modg, (1, 0, 2)).reshape(16, 3 * D_MODEL)
    mod_b = lax.dynamic_slice_in_dim(mod_all, me, 1, axis=0).reshape(3, D_MODEL)
    mod_c = mod_all[8].reshape(3, D_MODEL)
    modrows = jnp.concatenate([mod_b, mod_c[0:2], jnp.zeros((3, D_MODEL), F32)], axis=0)
    b3 = b_mod.reshape(3, D_MODEL)
    bmodrows = jnp.concatenate([b3, b3[0:2], jnp.zeros((3, D_MODEL), F32)], axis=0)

    w_in_p = wg_in.reshape(D_IN_PROJ, D_MODEL)
    w_uq_p = jnp.concatenate([wg_uq.reshape(N_HEADS, D_HEAD, R_Q), jnp.zeros((N_HEADS, D_HEAD_PAD - D_HEAD, R_Q), BF16)],
                             axis=1).reshape(N_HEADS * D_HEAD_PAD, R_Q)
    w_ukv_f = jnp.transpose(wg_ukv, (1, 0, 2)).reshape(R_KV, N_HEADS * 256)
    qng_p = jnp.concatenate([q_norm_g, jnp.zeros((1, D_HEAD_PAD - D_HEAD), F32)], axis=1)
    kng_p = jnp.concatenate([k_norm_g, jnp.zeros((1, D_HEAD_PAD - D_HEAD), F32)], axis=1)
    cos, slo, shi = _rope_tables(seq)

    u_gates, u_mla, q, k, v = _inproj_fwd(x2, ctx2, modrows, bmodrows, norm_g, w_in_p, q_lora_g, w_uq_p, kv_lora_g, w_ukv_f,
                             qng_p, kng_p, cos, slo, shi)
    attn, lse, wg_out = _attn_fwd(q, k, v, min(2048, seq), w_out_l.astype(BF16))
    (loss_acc, g1, dgate, d_attn, dga, dgp, dpl, gwo_b, gwp, dps) = _mix(
        x2, tgt2, attn, u_gates, modrows, bmodrows, w_pool[0], pool_scale, wg_out.reshape(D_MODEL, D_MODEL))

    blocks = lambda a: a.reshape((N_DEV, a.shape[0] // N_DEV) + a.shape[1:])
    dq, dk, dv, land_wo, land_wp = _attn_bwd(q, k, v, attn, lse, d_attn.reshape(seq, D_ATTN), min(1024, seq), blocks(gwo_b),
                                             gwp.reshape(4 * GROUP_DIM, GROUP_DIM))
    (grad_x, gwin_t, gwuq_p, gwukv_t, dqlg, dkvlg, dqng, dkng, dng, dmod4) = _inproj_bwd(
        x2, ctx2, modrows, bmodrows, norm_g, w_in_p, q_lora_g, w_uq_p, kv_lora_g, w_ukv_f, qng_p, kng_p, cos, slo, shi,
        u_mla, dq, dk, dv, dga, dgp, dpl, g1)

    gwuq_t = gwuq_p.reshape(N_HEADS, D_HEAD_PAD, R_Q)[:, 0:D_HEAD].reshape(N_HEADS * D_HEAD, R_Q)
    parts = [blocks(gwin_t), blocks(gwuq_t), blocks(gwukv_t)]
    (r_win, r_wuq, r_wukv, r_wout, g_w_pool, ccg, g_ng, g_qlg, g_kvlg, g_qng, g_kng, g_ps, g_bmod, dmod_my,
     loss_rows) = _epilogue(parts, [land_wo, land_wp], [dng, dqlg, dkvlg, dqng, dkng, dps], dmod4, dgate, loss_acc, w_mod_l)

    g_w_ukv = jnp.transpose(r_wukv)
    c_rows = cg[:, 0, :]
    c_all_t = jnp.transpose(jnp.concatenate([c_rows, c_ctx_row, jnp.zeros((16 - N_DEV - 1, D_MODEL), F32)], axis=0))

    weights = [c_ctx_row, w_mod_l, b_mod, norm_g, w_in_tl, q_lora_g, w_uq_tl, kv_lora_g, w_ukv_l, q_norm_g, k_norm_g,
               w_pool.reshape(4 * GROUP_DIM, GROUP_DIM), pool_scale, w_out_l]
    grads = [None, None, g_bmod, g_ng, r_win, g_qlg, r_wuq, g_kvlg, g_w_ukv, g_qng, g_kng, g_w_pool, g_ps, r_wout]
    ms = [m_c_ctx.reshape(1, D_MODEL), m_w_mod[0], m_b_mod, m_norm_g, tr(m_w_in), m_q_lora_g, tr(m_w_uq), m_kv_lora_g,
          m_w_ukv[0], m_q_norm_g, m_k_norm_g, m_w_pool.reshape(4 * GROUP_DIM, GROUP_DIM), m_pool_scale, m_w_out[0]]
    vs = [v_c_ctx.reshape(1, D_MODEL), v_w_mod[0], v_b_mod, v_norm_g, tr(v_w_in), v_q_lora_g, tr(v_w_uq), v_kv_lora_g,
          v_w_ukv[0], v_q_norm_g, v_k_norm_g, v_w_pool.reshape(4 * GROUP_DIM, GROUP_DIM), v_pool_scale, v_w_out[0]]
    outs = _adamw(weights, grads, ms, vs, c_all_t, dmod_my, ccg)
    n = len(weights)
    grads, deltas, new_m, new_v = outs[0:n], outs[n:2 * n], outs[2 * n:3 * n], outs[3 * n:4 * n]

    loss = loss_rows[ROW_LOSS - 8, 0]
    final = [c_ctx.shape, w_mod.shape, b_mod.shape, norm_g.shape, w_in.shape, q_lora_g.shape, w_uq.shape, kv_lora_g.shape,
             w_ukv.shape, q_norm_g.shape, k_norm_g.shape, w_pool.shape, pool_scale.shape, w_out.shape]
    transposed = (4, 6)

    def shaped(arrs):
        return [(jnp.transpose(a) if i in transposed else a).reshape(s) for i, (a, s) in enumerate(zip(arrs, final))]

    return (loss, grad_x[None], *shaped(grads), *shaped(deltas), *shaped(new_m), *shaped(new_v))
```

```python
import numpy as np

import jax
import jax.numpy as jnp
from jax import lax
from jax.experimental import pallas as pl
from jax.experimental.pallas import tpu as pltpu

F32 = jnp.float32
BF16 = jnp.bfloat16
MESH = pl.DeviceIdType.MESH
HIGHEST = lax.Precision.HIGHEST

D_MODEL = 1024
N_HEADS = 4
D_NOPE = 128
D_ROPE = 64
D_HEAD = D_NOPE + D_ROPE
D_HEAD_PAD = 256
D_V = 128
R_Q = 256
R_KV = 128
D_ATTN = 512
D_POOL = 512
POOL_WINDOWS = (2, 4, 8, 16)
GROUP_DIM = 128
GRID_W = 64
ROPE_BASE = 10000.0
NORM_EPS = 1e-6
ATTN_SCALE = D_HEAD ** -0.5
LOG2_E = 1.4426950408889634
LN_2 = 0.6931471805599453
ATTN_ROWS = 256
D_IN_PROJ = 1984
U_PAD = 2048
O_GA, O_PIN, O_GP, O_CQ, O_CKV, O_KR = 0, 512, 1024, 1536, 1792, 1920
TILE = 256
MIX_TILE = 512
Q_BLOCK = 128
HALO = 16
N_GATES = 1536
N_MLA = D_IN_PROJ - N_GATES
N_DEV = 8
VMEM_LIMIT = 56 * 1024 * 1024

ADAM_LR = 0.001
ADAM_B1 = 0.9
ADAM_B2 = 0.999
ADAM_EPS = 1e-08
ADAM_WD = 0.01
ADAM_STEP = 10

SMALL_ROWS = 16


def _dot(a, b):
    return lax.dot_general(a, b, (((1,), (0,)), ((), ())), preferred_element_type=F32)


def _dot_nt(a, b):
    return lax.dot_general(a, b, (((1,), (1,)), ((), ())), preferred_element_type=F32)


def _dot_tn(a, b):
    return lax.dot_general(a, b, (((0,), (0,)), ((), ())), preferred_element_type=F32)


def _sigmoid(x):
    return 1.0 / (1.0 + jnp.exp(-x))


def _rope(p, cos, slo, shi):
    return p * cos + pltpu.roll(p, 112, 1) * slo + pltpu.roll(p, 16, 1) * shi


def _rope_t(d, cos, slo, shi):
    return d * cos + pltpu.roll(d * slo, 16, 1) + pltpu.roll(d * shi, 112, 1)


def _shift_rows(a, k):
    n = a.shape[0]
    return pltpu.roll(a, (-k) % n, 0)


def _window_sum(x, w, transposed):
    s = (x + _shift_rows(x, 1)) if transposed else (_shift_rows(x, -1) + x)
    step = 1
    while 2 * step < w:
        s = _shift_rows(s, -step) + _shift_rows(s, step)
        step *= 2
    return s


def _inv_count(tpos, w, seq):
    lo = jnp.maximum(tpos - w // 2, 0)
    hi = jnp.minimum(tpos - w // 2 + w, seq)
    return 1.0 / jnp.maximum(hi - lo, 1).astype(F32)


def _coords():
    return lax.axis_index("x"), lax.axis_index("y"), lax.axis_index("c")


def _flip(v, bit):
    return (1 - v) if bit else v


def _peer(k):
    x, y, c = _coords()
    return (_flip(x, (k >> 2) & 1), _flip(y, (k >> 1) & 1), _flip(c, k & 1))


def _lin(p):
    return 4 * p[0] + 2 * p[1] + p[2]


def _gather_to_all(src_ref, slots_ref, send_sems, recv_sems):
    me = _coords()
    copies = []
    for k in range(1, N_DEV):
        cp = pltpu.make_async_remote_copy(
            src_ref=src_ref, dst_ref=slots_ref.at[_lin(me)], send_sem=send_sems.at[k - 1], recv_sem=recv_sems.at[k - 1],
            device_id=_peer(k), device_id_type=MESH)
        cp.start()
        copies.append(cp)

    def wait_recv():
        for k in range(1, N_DEV):
            pltpu.make_async_remote_copy(
                src_ref=src_ref, dst_ref=slots_ref.at[_lin(_peer(k))], send_sem=send_sems.at[k - 1],
                recv_sem=recv_sems.at[k - 1], device_id=_peer(k), device_id_type=MESH).wait_recv()

    def wait_send():
        for cp in copies:
            cp.wait_send()

    return wait_recv, wait_send


def _prologue(c8, cctx8, w_mod_l, shards):
    n_mod = w_mod_l.shape[1]
    n_w = len(shards)

    def body(c_ref, cctx_ref, wmod_ref, *refs):
        w_refs = refs[0:n_w]
        cg_ref, modg_ref = refs[n_w], refs[n_w + 1]
        wg_refs = refs[n_w + 2:2 * n_w + 2]
        modblk_ref = refs[2 * n_w + 2]
        wb_refs = refs[2 * n_w + 3:3 * n_w + 3]
        c8_ref = refs[3 * n_w + 3]
        ssem_w, rsem_w, ssem_c, rsem_c, ssem_m, rsem_m = refs[3 * n_w + 4:]
        x, y, c = _coords()
        me = (x, y, c)
        sibling = (x, y, 1 - c)
        chips = [(1 - x, y), (x, 1 - y), (1 - x, 1 - y)]

        def wcopies(k, block, to, own=False):
            out = []
            for a in range(n_w):
                slot = wg_refs[a].at[_lin(block)]
                out.append(pltpu.make_async_remote_copy(
                    src_ref=wb_refs[a] if own else slot, dst_ref=slot, send_sem=ssem_w.at[a, k], recv_sem=rsem_w.at[a, k],
                    device_id=to, device_id_type=MESH))
            return out

        c8_ref[...] = jnp.broadcast_to(c_ref[...], (8, D_MODEL))
        cg_ref[_lin(me)] = c8_ref[...]
        c_recv, c_send = _gather_to_all(c8_ref, cg_ref, ssem_c, rsem_c)

        for a in range(n_w):
            wb_refs[a][...] = w_refs[a][...].astype(BF16)
        first = wcopies(0, me, sibling, own=True)
        for j, chip in enumerate(chips):
            first += wcopies(1 + j, me, (*chip, c), own=True)
        late = [idx for idx in range(len(first)) if idx % n_w == 0 and idx >= n_w]
        for idx, cp in enumerate(first):
            if idx not in late:
                cp.start()
        for a in range(n_w):
            wg_refs[a][_lin(me)] = wb_refs[a][...]

        c_recv()
        row = lax.broadcasted_iota(jnp.int32, (8, D_MODEL), 0)
        c_all = jnp.zeros((8, D_MODEL), F32)
        for d in range(N_DEV):
            c_all = c_all + jnp.where(row == d, cg_ref[d], 0.0)
        cc = jnp.broadcast_to(cctx_ref[...], (8, D_MODEL))
        a = jnp.concatenate([c_all * _sigmoid(c_all), cc * _sigmoid(cc)], axis=0)
        modblk_ref[...] = lax.dot_general(a, wmod_ref[...], (((1,), (0,)), ((), ())), precision=HIGHEST,
                                          preferred_element_type=F32)
        modg_ref[_lin(me)] = modblk_ref[...]
        m_recv, m_send = _gather_to_all(modblk_ref, modg_ref, ssem_m, rsem_m)
        for idx in late:
            first[idx].start()
        m_recv()

        passed = []
        for j, chip in enumerate(chips):
            for cp in wcopies(1 + j, (*chip, c), me):
                cp.wait_recv()
            fwd = wcopies(4 + j, (*chip, c), sibling)
            for cp in fwd:
                cp.start()
            passed += fwd
        for cp in wcopies(0, sibling, me):
            cp.wait_recv()
        for j, chip in enumerate(chips):
            for cp in wcopies(4 + j, (*chip, 1 - c), me):
                cp.wait_recv()
        for cp in first + passed:
            cp.wait_send()
        c_send()
        m_send()

    vm = pl.BlockSpec(memory_space=pltpu.VMEM)
    return pl.pallas_call(
        body, name="prologue_gather",
        out_shape=(jax.ShapeDtypeStruct((N_DEV, 8, D_MODEL), F32), jax.ShapeDtypeStruct((N_DEV, 16, n_mod), F32),
                   *[jax.ShapeDtypeStruct((N_DEV,) + s.shape, BF16) for s in shards]),
        in_specs=[vm] * (3 + n_w), out_specs=tuple([vm] * (2 + n_w)),
        scratch_shapes=[pltpu.VMEM((16, n_mod), F32), *[pltpu.VMEM(s.shape, BF16) for s in shards],
                        pltpu.VMEM((8, D_MODEL), F32),
                        pltpu.SemaphoreType.DMA((n_w, 7)), pltpu.SemaphoreType.DMA((n_w, 7)),
                        pltpu.SemaphoreType.DMA((7,)), pltpu.SemaphoreType.DMA((7,)),
                        pltpu.SemaphoreType.DMA((7,)), pltpu.SemaphoreType.DMA((7,))],
        compiler_params=pltpu.CompilerParams(vmem_limit_bytes=VMEM_LIMIT),
    )(c8, cctx8, w_mod_l, *shards)


def _modulated_input(is_ctx, x_ref, ctx_ref, mod_ref, bmod_ref, ng_ref):
    xt = jnp.where(is_ctx, ctx_ref[...], x_ref[...])
    shift = jnp.where(is_ctx, mod_ref[3:4, :] + bmod_ref[3:4, :], mod_ref[0:1, :] + bmod_ref[0:1, :])
    scale = jnp.where(is_ctx, mod_ref[4:5, :] + bmod_ref[4:5, :], mod_ref[1:2, :] + bmod_ref[1:2, :])
    r = lax.rsqrt(jnp.mean(xt * xt, axis=-1, keepdims=True) + NORM_EPS)
    xg = (xt * r) * ng_ref[...]
    h = xg * (1.0 + scale) + shift
    return xt, r, xg, h, scale


def _inproj_fwd(x, ctx, modrows, bmodrows, norm_g, w_in_p, q_lora_g, w_uq_p, kv_lora_g, w_ukv, qng_p, kng_p, cos, slo, shi):
    seq = x.shape[0]
    n_tiles = seq // TILE + 1
    tot = seq + TILE

    n_mla = R_Q + R_KV + 128

    def body(x_ref, ctx_ref, mod_ref, bmod_ref, ng_ref, win_ref, qlg_ref, wuq_ref, kvlg_ref, wukv_ref, qng_ref, kng_ref,
             cos_ref, slo_ref, shi_ref, ug_ref, um_ref, q_ref, k_ref, v_ref, stash_even, stash_odd):
        s = pl.program_id(0)

        @pl.when(s == 0)
        def _():
            stash_odd[...] = jnp.zeros_like(stash_odd)

        refs = (x_ref, ctx_ref, mod_ref, bmod_ref, ng_ref, win_ref, qlg_ref, wuq_ref, kvlg_ref, wukv_ref, qng_ref, kng_ref,
                cos_ref, slo_ref, shi_ref, ug_ref, um_ref, q_ref, k_ref, v_ref)

        @pl.when(lax.rem(s, 2) == 0)
        def _():
            stages(s, refs, stash_even, stash_odd)

        @pl.when(lax.rem(s, 2) == 1)
        def _():
            stages(s, refs, stash_odd, stash_even)

    def stages(s, refs, fill, prev_ref):
        (x_ref, ctx_ref, mod_ref, bmod_ref, ng_ref, win_ref, qlg_ref, wuq_ref, kvlg_ref, wukv_ref, qng_ref, kng_ref,
         cos_ref, slo_ref, shi_ref, ug_ref, um_ref, q_ref, k_ref, v_ref) = refs
        cos_t, slo_t, shi_t = cos_ref[...], slo_ref[...], shi_ref[...]
        prev = prev_ref[...]
        cq = prev[:, 0:R_Q]
        qn = cq * lax.rsqrt(jnp.mean(cq * cq, axis=-1, keepdims=True) + NORM_EPS) * qlg_ref[...]
        q = _dot_nt(qn.astype(BF16), wuq_ref[...])
        qg = qng_ref[...] * (ATTN_SCALE * LOG2_E)
        for hd in range(N_HEADS):
            qh = q[:, hd * D_HEAD_PAD:(hd + 1) * D_HEAD_PAD]
            rr = lax.rsqrt(jnp.sum(qh * qh, axis=-1, keepdims=True) * (1.0 / D_HEAD) + NORM_EPS)
            qy = qh * rr * qg
            q_ref[hd, :, 0:D_NOPE] = qy[:, 0:D_NOPE].astype(BF16)
            q_ref[hd, :, D_NOPE:D_HEAD_PAD] = _rope(qy[:, D_NOPE:D_HEAD_PAD], cos_t, slo_t, shi_t).astype(BF16)

        ckv = prev[:, R_Q:R_Q + R_KV]
        kvn = ckv * lax.rsqrt(jnp.mean(ckv * ckv, axis=-1, keepdims=True) + NORM_EPS) * kvlg_ref[...]
        kv = _dot(kvn.astype(BF16), wukv_ref[...])
        krz = prev[:, R_Q + R_KV:n_mla]
        kr_ss = jnp.sum(krz * krz, axis=-1, keepdims=True)
        kg = kng_ref[...]
        for hd in range(N_HEADS):
            kn = kv[:, hd * 256:hd * 256 + D_NOPE]
            rr = lax.rsqrt((jnp.sum(kn * kn, axis=-1, keepdims=True) + kr_ss) * (1.0 / D_HEAD) + NORM_EPS)
            k_ref[hd, :, 0:D_NOPE] = (kn * rr * kg[:, 0:D_NOPE]).astype(BF16)
            k_ref[hd, :, D_NOPE:D_HEAD_PAD] = _rope(krz * rr * kg[:, D_NOPE:D_HEAD_PAD], cos_t, slo_t, shi_t).astype(BF16)
            v_ref[hd] = kv[:, hd * 256 + D_NOPE:(hd + 1) * 256].astype(BF16)

        _, _, _, h, _ = _modulated_input(s == 0, x_ref, ctx_ref, mod_ref, bmod_ref, ng_ref)
        hb = h.astype(BF16)
        ug_ref[...] = _dot_nt(hb, win_ref[N_MLA:D_IN_PROJ, :]).astype(BF16)
        um = _dot_nt(hb, win_ref[0:n_mla, :])
        lane = lax.broadcasted_iota(jnp.int32, (TILE, 128), 1)
        um = jnp.concatenate([um[:, 0:R_Q + R_KV], jnp.where(lane < D_ROPE, um[:, R_Q + R_KV:n_mla], 0.0)], axis=1)
        um_ref[...] = um
        fill[...] = um

    first = lambda s: jnp.minimum(s, n_tiles - 1)
    second = lambda s: jnp.maximum(s - 1, 0)
    latent = lambda t: jnp.maximum(t - 1, 0)
    full = lambda shape: pl.BlockSpec(shape, lambda s: (0,) * len(shape))
    rope_spec = pl.BlockSpec((TILE, 128), lambda s: (second(s), 0))
    return pl.pallas_call(
        body, name="inproj_fwd", grid=(n_tiles + 1,),
        out_shape=(jax.ShapeDtypeStruct((seq, N_GATES), BF16), jax.ShapeDtypeStruct((tot, n_mla), F32),
                   jax.ShapeDtypeStruct((N_HEADS, seq, D_HEAD_PAD), BF16),
                   jax.ShapeDtypeStruct((N_HEADS, tot, D_HEAD_PAD), BF16),
                   jax.ShapeDtypeStruct((N_HEADS, tot, D_V), BF16)),
        in_specs=[pl.BlockSpec((TILE, D_MODEL), lambda s: (latent(first(s)), 0)), full((TILE, D_MODEL)), full((8, D_MODEL)),
                  full((8, D_MODEL)), full((1, D_MODEL)), full((D_IN_PROJ, D_MODEL)), full((1, R_Q)),
                  full((N_HEADS * D_HEAD_PAD, R_Q)), full((1, R_KV)), full((R_KV, N_HEADS * 256)), full((1, D_HEAD_PAD)),
                  full((1, D_HEAD_PAD)), rope_spec, rope_spec, rope_spec],
        out_specs=(pl.BlockSpec((TILE, N_GATES), lambda s: (latent(first(s)), 0)),
                   pl.BlockSpec((TILE, n_mla), lambda s: (first(s), 0)),
                   pl.BlockSpec((N_HEADS, TILE, D_HEAD_PAD), lambda s: (0, latent(second(s)), 0)),
                   pl.BlockSpec((N_HEADS, TILE, D_HEAD_PAD), lambda s: (0, second(s), 0)),
                   pl.BlockSpec((N_HEADS, TILE, D_V), lambda s: (0, second(s), 0))),
        scratch_shapes=[pltpu.VMEM((TILE, n_mla), F32), pltpu.VMEM((TILE, n_mla), F32)],
        compiler_params=pltpu.CompilerParams(dimension_semantics=("arbitrary",), vmem_limit_bytes=VMEM_LIMIT),
    )(x, ctx, modrows, bmodrows, norm_g, w_in_p, q_lora_g, w_uq_p, kv_lora_g, w_ukv, qng_p, kng_p, cos, slo, shi)


def _hosted_exchange(first, last, items, send_sems, recv_sems, local_sems):
    me = _lin(_coords())

    def remote(n, k, src, land, scatter):
        peer = _peer(k)
        return pltpu.make_async_remote_copy(
            src_ref=src.at[_lin(peer)] if scatter else src, dst_ref=land.at[me], send_sem=send_sems.at[n, k - 1],
            recv_sem=recv_sems.at[n, k - 1], device_id=peer, device_id_type=MESH)

    def local(n, src, land, scatter):
        return pltpu.make_async_copy(src.at[me] if scatter else src, land.at[me], local_sems.at[n])

    @pl.when(first)
    def _():
        for n, (src, land, scatter) in enumerate(items):
            for k in range(1, N_DEV):
                remote(n, k, src, land, scatter).start()
            local(n, src, land, scatter).start()

    @pl.when(last)
    def _():
        for n, (src, land, scatter) in enumerate(items):
            for k in range(1, N_DEV):
                peer = _peer(k)
                pltpu.make_async_remote_copy(
                    src_ref=src.at[0] if scatter else src, dst_ref=land.at[_lin(peer)], send_sem=send_sems.at[n, k - 1],
                    recv_sem=recv_sems.at[n, k - 1], device_id=peer, device_id_type=MESH).wait_recv()
            for k in range(1, N_DEV):
                remote(n, k, src, land, scatter).wait_send()
            local(n, src, land, scatter).wait()


def _attn_fwd(q, k, v, block_q, w_out_b):
    _, seq, _ = q.shape
    n_keys = k.shape[1]
    n_q = seq // block_q

    def body(q_ref, k_ref, v_ref, wo_ref, o_ref, lse_ref, wog_ref, ssem, rsem, lsem):
        h, i = pl.program_id(0), pl.program_id(1)
        _hosted_exchange((h == 0) & (i == 0), (h == N_HEADS - 1) & (i == n_q - 1), [(wo_ref, wog_ref, False)],
                         ssem, rsem, lsem)
        kb, vb = k_ref[0], v_ref[0]
        for r0 in range(0, block_q, ATTN_ROWS):
            rows = slice(r0, r0 + ATTN_ROWS)
            s = _dot_nt(q_ref[0, rows, :], kb)
            m = jnp.max(s, axis=-1, keepdims=True)
            p = jnp.exp2(s - m)
            l = jnp.sum(p, axis=-1, keepdims=True)
            o_ref[rows, :] = _dot(p.astype(BF16), vb) * (1.0 / l)
            lse_ref[0, rows, :] = m + jnp.log2(l)

    hbm = pl.BlockSpec(memory_space=pl.ANY)
    return pl.pallas_call(
        body, name="attn_fwd", grid=(N_HEADS, n_q),
        out_shape=(jax.ShapeDtypeStruct((seq, D_ATTN), F32), jax.ShapeDtypeStruct((N_HEADS, seq, 1), F32),
                   jax.ShapeDtypeStruct((N_DEV,) + w_out_b.shape, w_out_b.dtype)),
        in_specs=[pl.BlockSpec((1, block_q, D_HEAD_PAD), lambda h, i: (h, i, 0)),
                  pl.BlockSpec((1, n_keys, D_HEAD_PAD), lambda h, i: (h, 0, 0)),
                  pl.BlockSpec((1, n_keys, D_V), lambda h, i: (h, 0, 0)), hbm],
        out_specs=(pl.BlockSpec((block_q, D_V), lambda h, i: (i, h)),
                   pl.BlockSpec((1, block_q, 1), lambda h, i: (h, i, 0)), hbm),
        scratch_shapes=[pltpu.SemaphoreType.DMA((1, 7)), pltpu.SemaphoreType.DMA((1, 7)), pltpu.SemaphoreType.DMA((1,))],
        compiler_params=pltpu.CompilerParams(dimension_semantics=("arbitrary", "arbitrary"), vmem_limit_bytes=VMEM_LIMIT),
    )(q, k, v, w_out_b)


def _attn_bwd(q, k, v, o, lse, d_o, block_q, gwo_blocks, gwp):
    _, seq, _ = q.shape
    n_keys = k.shape[1]
    n_q = seq // block_q

    def body(q_ref, k_ref, v_ref, o_ref, lse_ref, do_ref, gwo_ref, gwp_ref, dq_ref, dkt_ref, dvt_ref, lwo_ref, lwp_ref,
             ssem, rsem, lsem):
        h, i = pl.program_id(0), pl.program_id(1)
        _hosted_exchange((h == 0) & (i == 0), (h == N_HEADS - 1) & (i == n_q - 1),
                         [(gwo_ref, lwo_ref, True), (gwp_ref, lwp_ref, False)], ssem, rsem, lsem)

        @pl.when(i == 0)
        def _():
            dkt_ref[...] = jnp.zeros_like(dkt_ref)
            dvt_ref[...] = jnp.zeros_like(dvt_ref)

        kb, vb = k_ref[0], v_ref[0]
        raws, ps = [], []
        for r0 in range(0, block_q, ATTN_ROWS):
            rows = slice(r0, r0 + ATTN_ROWS)
            d_out = do_ref[rows, :]
            p = jnp.exp2(_dot_nt(q_ref[0, rows, :], kb) - lse_ref[0, rows, :])
            dp = _dot_nt(d_out.astype(BF16), vb)
            delta = jnp.sum(d_out * o_ref[rows, :], axis=-1, keepdims=True)
            raw = (p * (dp - delta)).astype(BF16)
            dq_ref[0, rows, :] = _dot(raw, kb)
            raws.append(raw)
            ps.append(p.astype(BF16))
        dkt_ref[0] += _dot_tn(q_ref[0], jnp.concatenate(raws, axis=0))
        dvt_ref[0] += _dot_tn(do_ref[...].astype(BF16), jnp.concatenate(ps, axis=0))

    hbm = pl.BlockSpec(memory_space=pl.ANY)
    return pl.pallas_call(
        body, name="attn_bwd", grid=(N_HEADS, n_q),
        out_shape=(jax.ShapeDtypeStruct((N_HEADS, seq, D_HEAD_PAD), F32),
                   jax.ShapeDtypeStruct((N_HEADS, D_HEAD_PAD, n_keys), F32),
                   jax.ShapeDtypeStruct((N_HEADS, D_V, n_keys), F32),
                   jax.ShapeDtypeStruct(gwo_blocks.shape, gwo_blocks.dtype),
                   jax.ShapeDtypeStruct((N_DEV,) + gwp.shape, gwp.dtype)),
        in_specs=[pl.BlockSpec((1, block_q, D_HEAD_PAD), lambda h, i: (h, i, 0)),
                  pl.BlockSpec((1, n_keys, D_HEAD_PAD), lambda h, i: (h, 0, 0)),
                  pl.BlockSpec((1, n_keys, D_V), lambda h, i: (h, 0, 0)),
                  pl.BlockSpec((block_q, D_V), lambda h, i: (i, h)),
                  pl.BlockSpec((1, block_q, 1), lambda h, i: (h, i, 0)),
                  pl.BlockSpec((block_q, D_V), lambda h, i: (i, h)), hbm, hbm],
        out_specs=(pl.BlockSpec((1, block_q, D_HEAD_PAD), lambda h, i: (h, i, 0)),
                   pl.BlockSpec((1, D_HEAD_PAD, n_keys), lambda h, i: (h, 0, 0)),
                   pl.BlockSpec((1, D_V, n_keys), lambda h, i: (h, 0, 0)), hbm, hbm),
        scratch_shapes=[pltpu.SemaphoreType.DMA((2, 7)), pltpu.SemaphoreType.DMA((2, 7)), pltpu.SemaphoreType.DMA((2,))],
        compiler_params=pltpu.CompilerParams(dimension_semantics=("arbitrary", "arbitrary"), vmem_limit_bytes=VMEM_LIMIT),
    )(q, k, v, o, lse, d_o, gwo_blocks, gwp)


def _mix(x, target, attn, u, modrows, bmodrows, w_pool, pool_scale, w_out):
    seq = x.shape[0]
    rows = min(MIX_TILE, seq // 2)
    n_tiles = seq // rows
    rows8 = rows // HALO
    last8 = seq // HALO - 1

    n_blk = seq // Q_BLOCK
    per = rows // n_blk
    assert rows % n_blk == 0 and per % 8 == 0 and n_tiles >= 2
    n_stash = 8

    def body(x_ref, t_ref, o_hbm, ga_ref, pin_ref, hb_ref, ha_ref, gp_ref, mod_ref, bmod_ref, wp_ref, ps_ref, wo_ref,
             loss_ref, g1_ref, dgate_ref, do_hbm, dga_ref, dgp_ref, dpl_ref, gwob_ref, gwp_ref, dps_ref,
             abuf, dbuf, gwo_ref, *rest):
        stash_even, stash_odd = rest[0:n_stash], rest[n_stash:2 * n_stash]
        rsem, wsem = rest[2 * n_stash:]
        s = pl.program_id(0)

        def slab_copies(tile, slot, fetch):
            window = pl.ds(tile * per, per)
            if fetch:
                return [pltpu.make_async_copy(o_hbm.at[:, window, :], abuf.at[slot], rsem.at[slot])]
            return [pltpu.make_async_copy(dbuf.at[slot], do_hbm.at[:, window, :], wsem.at[slot])]

        def wait_all(slot, fetch):
            slab_copies(0, slot, fetch)[0].wait()

        a_slot = lax.rem(s, 2)
        b_slot = 1 - a_slot

        @pl.when(s == 0)
        def _():
            loss_ref[...] = jnp.zeros_like(loss_ref)
            dgate_ref[...] = jnp.zeros_like(dgate_ref)
            gwo_ref[...] = jnp.zeros_like(gwo_ref)
            gwp_ref[...] = jnp.zeros_like(gwp_ref)
            dps_ref[...] = jnp.zeros_like(dps_ref)
            for cp in slab_copies(0, 0, True):
                cp.start()

        @pl.when(s + 1 < n_tiles)
        def _():
            for cp in slab_copies(s + 1, b_slot, True):
                cp.start()

        @pl.when(s < n_tiles)
        def _():
            wait_all(a_slot, True)

        @pl.when(s >= 3)
        def _():
            wait_all(b_slot, False)

        gate = mod_ref[2:3, :] + bmod_ref[2:3, :]
        ps = ps_ref[...]

        def a_vector(slot):
            attn_t = jnp.swapaxes(abuf[slot], 0, 1).reshape(rows, D_ATTN)
            ga = ga_ref[...].astype(F32)
            sga = _sigmoid(ga)
            silu_ga = ga * sga
            pin = pin_ref[...].astype(F32)
            xs = jnp.concatenate([jnp.where(s > 0, hb_ref[...].astype(F32), 0.0), pin,
                                  jnp.where(s < n_tiles - 1, ha_ref[...].astype(F32), 0.0)], axis=0)
            tpos = s * rows + lax.broadcasted_iota(jnp.int32, (rows, 1), 0)
            pooled = []
            for g, w in enumerate(POOL_WINDOWS):
                sl = slice(g * GROUP_DIM, (g + 1) * GROUP_DIM)
                ws = _window_sum(xs[:, sl], w, False)[HALO:HALO + rows]
                pooled.append((ws * _inv_count(tpos, w, seq) - pin[:, sl]).astype(BF16))
            return attn_t, ga, sga, silu_ga, pooled

        def a_group_maps(pooled):
            return jnp.concatenate([_dot(pooled[g], wp_ref[g].astype(BF16)) for g in range(len(POOL_WINDOWS))], axis=1)

        def a_project(stash, yp, attn_t, ga, sga, silu_ga, pooled):
            zb_ref, _, fa_ref, sa_ref, fp_ref, sp_ref, yp_ref, pooled_ref = stash
            ypool = yp * ps
            gp = gp_ref[...].astype(F32)
            sgp = _sigmoid(gp)
            silu_gp = gp * sgp
            z = jnp.concatenate([silu_ga * attn_t, silu_gp * ypool], axis=1).astype(BF16)
            y = _dot(z, wo_ref[...])
            zb_ref[...] = z
            fa_ref[...] = attn_t * (sga * (1.0 + ga * (1.0 - sga)))
            sa_ref[...] = silu_ga
            fp_ref[...] = ypool * (sgp * (1.0 + gp * (1.0 - sgp)))
            sp_ref[...] = silu_gp
            yp_ref[...] = yp
            pooled_ref[...] = jnp.concatenate(pooled, axis=1)
            return y

        def a_loss(stash, y):
            res = x_ref[...] + gate * y - t_ref[...]
            loss_ref[...] += jnp.sum(res * res) * (0.5 / D_MODEL)
            dxn = res * (1.0 / D_MODEL)
            g1_ref[...] = dxn
            dgate_ref[...] += jnp.sum(dxn * y, axis=0, keepdims=True)
            stash[1][...] = (gate * dxn).astype(BF16)

        def b_dz(stash):
            return _dot_nt(stash[1][...], wo_ref[...])

        def b_gwo(stash):
            gwo_ref[...] += _dot_tn(stash[0][...], stash[1][...])

        def b_vector(stash, slot, dz):
            _, _, fa_ref, sa_ref, fp_ref, sp_ref, yp_ref, _ = stash
            dbra = dz[:, 0:D_ATTN]
            dbrp = dz[:, D_ATTN:]
            dga_ref[...] = (dbra * fa_ref[...]).astype(BF16)
            dbuf[slot] = jnp.swapaxes((dbra * sa_ref[...]).reshape(per, n_blk, D_ATTN), 0, 1)
            dgp_ref[...] = (dbrp * fp_ref[...]).astype(BF16)
            dyp_s = dbrp * sp_ref[...]
            dps_ref[...] += jnp.sum(dyp_s * yp_ref[...], axis=0, keepdims=True)
            return (dyp_s * ps).astype(BF16)

        def b_small(stash, dyp):
            pooled_ref = stash[7]
            for g in range(len(POOL_WINDOWS)):
                sl = slice(g * GROUP_DIM, (g + 1) * GROUP_DIM)
                gwp_ref[g] += _dot_tn(pooled_ref[:, sl], dyp[:, sl])
                dpl_ref[:, sl] = _dot_nt(dyp[:, sl], wp_ref[g].astype(BF16)).astype(BF16)

        def both(a_stash, b_stash, slot_a):
            dz = b_dz(b_stash)
            front = a_vector(slot_a)
            yp = a_group_maps(front[-1])
            b_gwo(b_stash)
            y = a_project(a_stash, yp, *front)
            dyp = b_vector(b_stash, 1 - slot_a, dz)
            a_loss(a_stash, y)
            b_small(b_stash, dyp)

        @pl.when(s == 0)
        def _():
            front = a_vector(0)
            a_loss(stash_even, a_project(stash_even, a_group_maps(front[-1]), *front))

        @pl.when((s > 0) & (s < n_tiles) & (a_slot == 0))
        def _():
            both(stash_even, stash_odd, 0)

        @pl.when((s > 0) & (s < n_tiles) & (a_slot == 1))
        def _():
            both(stash_odd, stash_even, 1)

        @pl.when(s == n_tiles)
        def _():
            last = (n_tiles - 1) % 2
            stash = stash_odd if last else stash_even
            dz = b_dz(stash)
            b_gwo(stash)
            b_small(stash, b_vector(stash, last, dz))
            gwob_ref[...] = gwo_ref[...].astype(BF16)

        @pl.when(s >= 1)
        def _():
            for cp in slab_copies(s - 1, b_slot, False):
                cp.start()

        @pl.when(s == n_tiles)
        def _():
            wait_all(b_slot, False)
            wait_all(a_slot, False)

    ta = lambda s: jnp.minimum(s, n_tiles - 1)
    tb = lambda s: jnp.maximum(s - 1, 0)
    tile = lambda w: pl.BlockSpec((rows, w), lambda s: (ta(s), 0))
    tile_b = lambda w: pl.BlockSpec((rows, w), lambda s: (tb(s), 0))
    ucol = lambda col: pl.BlockSpec((rows, 512), lambda s: (ta(s), col))
    full = lambda shape: pl.BlockSpec(shape, lambda s: (0,) * len(shape))
    stash_shapes = [pltpu.VMEM((rows, D_MODEL), BF16), pltpu.VMEM((rows, D_MODEL), BF16)] + \
        [pltpu.VMEM((rows, 512), F32)] * 5 + [pltpu.VMEM((rows, D_POOL), BF16)]
    return pl.pallas_call(
        body, name="mix_fwd_bwd", grid=(n_tiles + 1,),
        out_shape=(jax.ShapeDtypeStruct((8, 128), F32), jax.ShapeDtypeStruct((seq, D_MODEL), F32),
                   jax.ShapeDtypeStruct((1, D_MODEL), F32), jax.ShapeDtypeStruct((n_blk, Q_BLOCK, D_ATTN), F32),
                   jax.ShapeDtypeStruct((seq, D_ATTN), BF16), jax.ShapeDtypeStruct((seq, D_POOL), BF16),
                   jax.ShapeDtypeStruct((seq, D_POOL), BF16), jax.ShapeDtypeStruct((D_MODEL, D_MODEL), BF16),
                   jax.ShapeDtypeStruct((4, GROUP_DIM, GROUP_DIM), F32), jax.ShapeDtypeStruct((1, D_POOL), F32)),
        in_specs=[tile(D_MODEL), tile(D_MODEL), pl.BlockSpec(memory_space=pl.ANY), ucol(0), ucol(1),
                  pl.BlockSpec((HALO, 512), lambda s: (jnp.maximum(ta(s) * rows8 - 1, 0), 1)),
                  pl.BlockSpec((HALO, 512), lambda s: (jnp.minimum((ta(s) + 1) * rows8, last8), 1)),
                  ucol(2), full((8, D_MODEL)), full((8, D_MODEL)), full((4, GROUP_DIM, GROUP_DIM)), full((1, D_POOL)),
                  full((D_MODEL, D_MODEL))],
        out_specs=(full((8, 128)), tile(D_MODEL), full((1, D_MODEL)), pl.BlockSpec(memory_space=pl.ANY), tile_b(D_ATTN),
                   tile_b(D_POOL), tile_b(D_POOL), full((D_MODEL, D_MODEL)), full((4, GROUP_DIM, GROUP_DIM)),
                   full((1, D_POOL))),
        scratch_shapes=[pltpu.VMEM((2, n_blk, per, D_ATTN), F32), pltpu.VMEM((2, n_blk, per, D_ATTN), F32),
                        pltpu.VMEM((D_MODEL, D_MODEL), F32), *stash_shapes, *stash_shapes,
                        pltpu.SemaphoreType.DMA((2,)), pltpu.SemaphoreType.DMA((2,))],
        compiler_params=pltpu.CompilerParams(dimension_semantics=("arbitrary",), vmem_limit_bytes=VMEM_LIMIT),
    )(x, target, attn.reshape(n_blk, Q_BLOCK, D_ATTN), u, u, u, u, u, modrows, bmodrows, w_pool, pool_scale, w_out)


def _inproj_bwd(x, ctx, modrows, bmodrows, norm_g, w_in_p, q_lora_g, w_uq_p, kv_lora_g, w_ukv, qng_p, kng_p, cos, slo, shi,
                u, dq, dk, dv, dga, dgp, dpl, g1):
    seq = x.shape[0]
    n_tiles = seq // TILE + 1
    rows8 = TILE // HALO
    last8 = seq // HALO - 1

    def body(*refs):
        du_even, du_odd, dh_even, dh_odd = refs[-4:]
        gwin_ref, gwuq_ref, gwukv_ref, dqlg_ref, dkvlg_ref, dqng_ref, dkng_ref, dng_ref, dmod_ref = refs[-13:-4]
        s = pl.program_id(0)

        @pl.when(s == 0)
        def _():
            for ref in (gwin_ref, gwuq_ref, gwukv_ref, dqlg_ref, dkvlg_ref, dqng_ref, dkng_ref, dng_ref, dmod_ref,
                        du_odd, dh_even):
                ref[...] = jnp.zeros_like(ref)

        @pl.when(lax.rem(s, 2) == 0)
        def _():
            stages(s, refs[:-4], du_even, du_odd, dh_odd, dh_even)

        @pl.when(lax.rem(s, 2) == 1)
        def _():
            stages(s, refs[:-4], du_odd, du_even, dh_even, dh_odd)

    def stages(s, refs, du_ref, du_prev, dh_fill, dh_prev):
        (xb_ref, xc_ref, ctx_ref, mod_ref, bmod_ref, ng_ref, win_ref, qlg_ref, wuq_ref, kvlg_ref, wukv_ref, qng_ref, kng_ref,
         cos_ref, slo_ref, shi_ref, cq_ref, ckv_ref, kr_ref, dq_ref, dk_ref, dv_ref, dga_ref, dgp_ref, dpl_ref,
         hb_ref, ha_ref, g1_ref,
         gx_ref, gwin_ref, gwuq_ref, gwukv_ref, dqlg_ref, dkvlg_ref, dqng_ref, dkng_ref, dng_ref, dmod_ref) = refs


        i = jnp.minimum(s, n_tiles - 1)
        valid = s < n_tiles
        is_lat = (s > 0) & valid
        cq = cq_ref[...]
        rq = lax.rsqrt(jnp.mean(cq * cq, axis=-1, keepdims=True) + NORM_EPS)
        qhat = cq * rq
        qnb = (qhat * qlg_ref[...]).astype(BF16)
        q = _dot_nt(qnb, wuq_ref[...])
        ckv = ckv_ref[...]
        rkv = lax.rsqrt(jnp.mean(ckv * ckv, axis=-1, keepdims=True) + NORM_EPS)
        kvhat = ckv * rkv
        kvnb = (kvhat * kvlg_ref[...]).astype(BF16)
        kv = _dot(kvnb, wukv_ref[...])

        _, _, _, h2, _ = _modulated_input(s <= 1, xb_ref, ctx_ref, mod_ref, bmod_ref, ng_ref)
        dub = du_prev[...]
        du_g, du_m = dub[:, 0:N_GATES], dub[:, N_GATES:U_PAD]
        h2b = h2.astype(BF16)
        dh_fill[...] = _dot(du_g, win_ref[N_MLA:D_IN_PROJ, :]) + _dot(du_m, win_ref[0:U_PAD - N_GATES, :])
        gwin_ref[N_MLA:D_IN_PROJ, :] += _dot_tn(du_g, h2b)
        gwin_ref[0:N_MLA, :] += _dot_tn(du_m, h2b)[0:N_MLA]

        ctx3 = s <= 2
        xt, r, xg, _, scale = _modulated_input(ctx3, xc_ref, ctx_ref, mod_ref, bmod_ref, ng_ref)
        dh = dh_prev[...]
        dshift = jnp.sum(dh, axis=0, keepdims=True)
        dscale = jnp.sum(dh * xg, axis=0, keepdims=True)
        zero = jnp.zeros_like(dshift)
        dmod_ref[0:1, :] += jnp.where(ctx3, zero, dshift)
        dmod_ref[1:2, :] += jnp.where(ctx3, zero, dscale)
        dmod_ref[2:3, :] += jnp.where(ctx3, dshift, zero)
        dmod_ref[3:4, :] += jnp.where(ctx3, dscale, zero)
        dxg = dh * (1.0 + scale)
        xr = xt * r
        dng_ref[...] += jnp.sum(dxg * xr, axis=0, keepdims=True)
        dxn = dxg * ng_ref[...]
        gx_ref[...] = g1_ref[...] + r * (dxn - xr * jnp.mean(dxn * xr, axis=-1, keepdims=True))

        cos_t, slo_t, shi_t = cos_ref[...], slo_ref[...], shi_ref[...]
        qg = qng_ref[...]
        dq_heads = []
        dqng = jnp.zeros((1, D_HEAD_PAD), F32)
        for hd in range(N_HEADS):
            qh = q[:, hd * D_HEAD_PAD:(hd + 1) * D_HEAD_PAD]
            rr = lax.rsqrt(jnp.sum(qh * qh, axis=-1, keepdims=True) * (1.0 / D_HEAD) + NORM_EPS)
            yq = qh * rr
            dpost = jnp.where(is_lat, dq_ref[hd] * ATTN_SCALE, 0.0)
            dyn = jnp.concatenate([dpost[:, 0:D_NOPE], _rope_t(dpost[:, D_NOPE:], cos_t, slo_t, shi_t)], axis=1)
            dqng = dqng + jnp.sum(dyn * yq, axis=0, keepdims=True)
            dyg = dyn * qg
            dq_heads.append(rr * (dyg - yq * (jnp.sum(dyg * yq, axis=-1, keepdims=True) * (1.0 / D_HEAD))))
        dqng_ref[...] += dqng
        dqf = jnp.concatenate(dq_heads, axis=1).astype(BF16)

        krz = kr_ref[...]
        kr_ss = jnp.sum(krz * krz, axis=-1, keepdims=True)
        kg = kng_ref[...]
        kg_n, kg_r = kg[:, 0:D_NOPE], kg[:, D_NOPE:]
        dkv_parts = []
        dkr = jnp.zeros((TILE, 128), F32)
        dkng_n = jnp.zeros((1, D_NOPE), F32)
        dkng_r = jnp.zeros((1, 128), F32)
        for hd in range(N_HEADS):
            kn = kv[:, hd * 256:hd * 256 + D_NOPE]
            rr = lax.rsqrt((jnp.sum(kn * kn, axis=-1, keepdims=True) + kr_ss) * (1.0 / D_HEAD) + NORM_EPS)
            yn, yr = kn * rr, krz * rr
            dk_h = jnp.where(valid, jnp.transpose(dk_ref[hd]) * LN_2, 0.0)
            dpn = dk_h[:, 0:D_NOPE]
            dpr = _rope_t(dk_h[:, D_NOPE:], cos_t, slo_t, shi_t)
            dkng_n = dkng_n + jnp.sum(dpn * yn, axis=0, keepdims=True)
            dkng_r = dkng_r + jnp.sum(dpr * yr, axis=0, keepdims=True)
            dyn, dyr = dpn * kg_n, dpr * kg_r
            proj = (jnp.sum(dyn * yn, axis=-1, keepdims=True) + jnp.sum(dyr * yr, axis=-1, keepdims=True)) * (1.0 / D_HEAD)
            dkv_parts.append(rr * (dyn - yn * proj))
            dkv_parts.append(jnp.where(valid, jnp.transpose(dv_ref[hd]), 0.0))
            dkr = dkr + rr * (dyr - yr * proj)
        dkng_ref[:, 0:D_NOPE] += dkng_n
        dkng_ref[:, D_NOPE:] += dkng_r
        dkvf = jnp.concatenate(dkv_parts, axis=1).astype(BF16)

        lat_t = jnp.maximum(i - 1, 0)
        dpl_t = dpl_ref[...].astype(F32)
        ds = jnp.concatenate([jnp.where(i > 1, hb_ref[...].astype(F32), 0.0), dpl_t,
                              jnp.where(i < n_tiles - 1, ha_ref[...].astype(F32), 0.0)], axis=0)
        tpos = lat_t * TILE - HALO + lax.broadcasted_iota(jnp.int32, (TILE + 2 * HALO, 1), 0)
        for g, w in enumerate(POOL_WINDOWS):
            sl = slice(g * GROUP_DIM, (g + 1) * GROUP_DIM)
            wsum = _window_sum(ds[:, sl] * _inv_count(tpos, w, seq), w, True)[HALO:HALO + TILE]
            du_ref[:, O_PIN + g * GROUP_DIM:O_PIN + (g + 1) * GROUP_DIM] = jnp.where(
                is_lat, wsum - dpl_t[:, sl], 0.0).astype(BF16)

        du_ref[:, O_GA:O_GA + D_ATTN] = jnp.where(is_lat, dga_ref[...], jnp.zeros((), BF16))
        du_ref[:, O_GP:O_GP + D_POOL] = jnp.where(is_lat, dgp_ref[...], jnp.zeros((), BF16))
        du_ref[:, O_KR:U_PAD] = dkr.astype(BF16)

        gwuq_ref[...] += _dot_tn(dqf, qnb)
        dqn = _dot(dqf, wuq_ref[...])
        gwukv_ref[...] += _dot_tn(dkvf, kvnb)
        dkvn = _dot_nt(dkvf, wukv_ref[...])
        dqlg_ref[...] += jnp.sum(dqn * qhat, axis=0, keepdims=True)
        dqhat = dqn * qlg_ref[...]
        dcq = rq * (dqhat - qhat * jnp.mean(dqhat * qhat, axis=-1, keepdims=True))
        dkvlg_ref[...] += jnp.sum(dkvn * kvhat, axis=0, keepdims=True)
        dkvhat = dkvn * kvlg_ref[...]
        dckv = rkv * (dkvhat - kvhat * jnp.mean(dkvhat * kvhat, axis=-1, keepdims=True))
        du_ref[:, O_CQ:O_CQ + R_Q] = dcq.astype(BF16)
        du_ref[:, O_CKV:O_CKV + R_KV] = dckv.astype(BF16)

    t1 = lambda s: jnp.minimum(s, n_tiles - 1)
    t2 = lambda s: jnp.clip(s - 1, 0, n_tiles - 1)
    t3 = lambda s: jnp.clip(s - 2, 0, n_tiles - 1)
    latent = lambda t: jnp.maximum(t - 1, 0)
    lat = lambda s: (latent(t1(s)), 0)
    lat3 = lambda s: (latent(t3(s)), 0)
    full = lambda shape: pl.BlockSpec(shape, lambda s: (0,) * len(shape))
    rope_spec = pl.BlockSpec((TILE, 128), lambda s: (t1(s), 0))
    return pl.pallas_call(
        body, name="inproj_bwd", grid=(n_tiles + 2,),
        out_shape=(jax.ShapeDtypeStruct((seq, D_MODEL), F32), jax.ShapeDtypeStruct((D_IN_PROJ, D_MODEL), F32),
                   jax.ShapeDtypeStruct((N_HEADS * D_HEAD_PAD, R_Q), F32), jax.ShapeDtypeStruct((N_HEADS * 256, R_KV), F32),
                   jax.ShapeDtypeStruct((1, R_Q), F32), jax.ShapeDtypeStruct((1, R_KV), F32),
                   jax.ShapeDtypeStruct((1, D_HEAD_PAD), F32), jax.ShapeDtypeStruct((1, D_HEAD_PAD), F32),
                   jax.ShapeDtypeStruct((1, D_MODEL), F32), jax.ShapeDtypeStruct((8, D_MODEL), F32)),
        in_specs=[pl.BlockSpec((TILE, D_MODEL), lambda s: (latent(t2(s)), 0)), pl.BlockSpec((TILE, D_MODEL), lat3),
                  full((TILE, D_MODEL)), full((8, D_MODEL)), full((8, D_MODEL)),
                  full((1, D_MODEL)), full((D_IN_PROJ, D_MODEL)), full((1, R_Q)), full((N_HEADS * D_HEAD_PAD, R_Q)),
                  full((1, R_KV)), full((R_KV, N_HEADS * 256)), full((1, D_HEAD_PAD)), full((1, D_HEAD_PAD)),
                  rope_spec, rope_spec, rope_spec,
                  pl.BlockSpec((TILE, R_Q), lambda s: (t1(s), (O_CQ - N_GATES) // R_Q)),
                  pl.BlockSpec((TILE, R_KV), lambda s: (t1(s), (O_CKV - N_GATES) // R_KV)),
                  pl.BlockSpec((TILE, 128), lambda s: (t1(s), (O_KR - N_GATES) // 128)),
                  pl.BlockSpec((N_HEADS, TILE, D_HEAD_PAD), lambda s: (0, latent(t1(s)), 0)),
                  pl.BlockSpec((N_HEADS, D_HEAD_PAD, TILE), lambda s: (0, 0, t1(s))),
                  pl.BlockSpec((N_HEADS, D_V, TILE), lambda s: (0, 0, t1(s))),
                  pl.BlockSpec((TILE, D_ATTN), lat), pl.BlockSpec((TILE, D_POOL), lat), pl.BlockSpec((TILE, D_POOL), lat),
                  pl.BlockSpec((HALO, D_POOL), lambda s: (jnp.maximum((t1(s) - 1) * rows8 - 1, 0), 0)),
                  pl.BlockSpec((HALO, D_POOL), lambda s: (jnp.minimum(jnp.maximum(t1(s), 1) * rows8, last8), 0)),
                  pl.BlockSpec((TILE, D_MODEL), lat3)],
        out_specs=(pl.BlockSpec((TILE, D_MODEL), lat3), full((D_IN_PROJ, D_MODEL)), full((N_HEADS * D_HEAD_PAD, R_Q)),
                   full((N_HEADS * 256, R_KV)), full((1, R_Q)), full((1, R_KV)), full((1, D_HEAD_PAD)),
                   full((1, D_HEAD_PAD)), full((1, D_MODEL)), full((8, D_MODEL))),
        scratch_shapes=[pltpu.VMEM((TILE, U_PAD), BF16), pltpu.VMEM((TILE, U_PAD), BF16),
                        pltpu.VMEM((TILE, D_MODEL), F32), pltpu.VMEM((TILE, D_MODEL), F32)],
        compiler_params=pltpu.CompilerParams(dimension_semantics=("arbitrary",), vmem_limit_bytes=VMEM_LIMIT),
    )(x, x, ctx, modrows, bmodrows, norm_g, w_in_p, q_lora_g, w_uq_p, kv_lora_g, w_ukv, qng_p, kng_p, cos, slo, shi,
      u, u, u, dq, dk, dv, dga, dgp, dpl, dpl, dpl, g1)


SMALL_LAYOUT = ((0, D_MODEL), (1, R_Q), (2, R_KV), (3, D_HEAD_PAD), (4, D_HEAD_PAD), (5, D_POOL))
ROW_DMOD_B, ROW_DMOD_C, ROW_LOSS = 6, 9, 12


def _epilogue(parts, landed, smalls, dmod4, dgate, loss_acc, w_mod_l):
    n_a, n_l, n_s = len(parts), len(landed), len(smalls)
    n_mod = w_mod_l.shape[1]
    blk = [p.shape[1:] for p in parts]
    small_out = [D_MODEL, R_Q, R_KV, D_HEAD, D_HEAD, D_POOL]

    def body(*refs):
        part_refs = refs[0:n_a]
        land_refs = refs[n_a:n_a + n_l]
        small_in = refs[n_a + n_l:n_a + n_l + n_s]
        dmod4_ref, dgate_ref, loss_ref, wmod_ref = refs[n_a + n_l + n_s:n_a + n_l + n_s + 4]
        outs = refs[n_a + n_l + n_s + 4:]
        red_refs = outs[0:n_a]
        landsum_refs = outs[n_a:n_a + n_l]
        ccg_ref = outs[n_a + n_l]
        sum_refs = outs[n_a + n_l + 1:n_a + n_l + 1 + n_s]
        gbmod_ref, dmy_ref, lossout_ref = outs[n_a + n_l + 1 + n_s:n_a + n_l + 4 + n_s]
        scr = outs[n_a + n_l + 4 + n_s:]
        recv1, own1, sum1b, recv2 = scr[0:n_a], scr[n_a:2 * n_a], scr[2 * n_a:3 * n_a], scr[3 * n_a:4 * n_a]
        small_ref, smallg_ref, tot_ref, cc_ref = scr[4 * n_a:4 * n_a + 4]
        ssem1, rsem1, lsem, ssem2, rsem2, ssem_s, rsem_s, ssem_cc, rsem_cc = scr[4 * n_a + 4:]
        x, y, c = _coords()
        me = (x, y, c)
        sibling = (x, y, 1 - c)

        small_ref[...] = jnp.zeros_like(small_ref)
        for ref, (row, width) in zip(small_in, SMALL_LAYOUT):
            small_ref[row:row + 1, 0:width] = ref[...]
        small_ref[ROW_DMOD_B:ROW_DMOD_B + 2, :] = dmod4_ref[0:2, :]
        small_ref[ROW_DMOD_B + 2:ROW_DMOD_B + 3, :] = dgate_ref[...]
        small_ref[ROW_DMOD_C:ROW_DMOD_C + 2, :] = dmod4_ref[2:4, :]
        small_ref[ROW_LOSS:ROW_LOSS + 1, 0:128] = loss_ref[0:1, :]
        smallg_ref[_lin(me)] = small_ref[...]
        s_recv, s_send = _gather_to_all(small_ref, smallg_ref, ssem_s, rsem_s)

        stage1, own_copies = [], []
        for a in range(n_a):
            for b in range(4):
                dev = 4 * (b >> 1) + 2 * (b & 1)
                stage1.append(pltpu.make_async_remote_copy(
                    src_ref=part_refs[a].at[dev + (1 - c)], dst_ref=recv1[a].at[b],
                    send_sem=ssem1.at[a, b], recv_sem=rsem1.at[a, b], device_id=sibling, device_id_type=MESH))
                own_copies.append(pltpu.make_async_copy(part_refs[a].at[dev + c], own1[a].at[b], lsem.at[a, b]))
                stage1[-1].start()
                own_copies[-1].start()

        s_recv()
        tot = smallg_ref[0]
        for d in range(1, N_DEV):
            tot = tot + smallg_ref[d]
        tot_ref[...] = tot
        for ref, (row, _), width in zip(sum_refs, SMALL_LAYOUT, small_out):
            ref[...] = tot_ref[row:row + 1, 0:width]
        lossout_ref[...] = tot_ref[8:16, 0:128]
        lin = _lin(me)

        def my_columns(three_rows):
            v = jnp.concatenate(three_rows, axis=1)
            out = jnp.zeros((1, n_mod), F32)
            for d in range(N_DEV):
                out = out + jnp.where(lin == d, v[:, d * n_mod:(d + 1) * n_mod], 0.0)
            return out

        rows3 = lambda ref, r0: [ref[r0 + t:r0 + t + 1, :] for t in range(3)]
        dmodc = rows3(tot_ref, ROW_DMOD_C)
        gbmod_ref[...] = jnp.concatenate(rows3(tot_ref, ROW_DMOD_B), axis=1) + jnp.concatenate(dmodc, axis=1)
        dmy_ref[...] = jnp.zeros_like(dmy_ref)
        for b in range(N_DEV):
            dmy_ref[b:b + 1, :] = my_columns(rows3(smallg_ref.at[b], ROW_DMOD_B))
        dmy = my_columns(dmodc)
        dmy_ref[N_DEV:N_DEV + 1, :] = dmy
        part = lax.dot_general(jnp.broadcast_to(dmy, (8, n_mod)), wmod_ref[...], (((1,), (1,)), ((), ())),
                               precision=HIGHEST, preferred_element_type=F32)

        cc_ref[...] = part
        ccg_ref[_lin(me)] = part
        cc_recv, cc_send = _gather_to_all(cc_ref, ccg_ref, ssem_cc, rsem_cc)

        stage2 = []
        for a in range(n_a):
            for b in range(4):
                stage1[4 * a + b].wait_recv()
                own_copies[4 * a + b].wait()
                sum1b[a][b] = (own1[a][b] + recv1[a][b]).astype(BF16)
            for k in (1, 2, 3):
                tx, ty = _flip(x, (k >> 1) & 1), _flip(y, k & 1)
                stage2.append(pltpu.make_async_remote_copy(
                    src_ref=sum1b[a].at[2 * tx + ty], dst_ref=recv2[a].at[k - 1], send_sem=ssem2.at[a, k - 1],
                    recv_sem=rsem2.at[a, k - 1], device_id=(tx, ty, c), device_id_type=MESH))
                stage2[-1].start()
        for a in range(n_a):
            own = own1[a][2 * x + y] + recv1[a][2 * x + y]
            for k in (1, 2, 3):
                stage2[3 * a + k - 1].wait_recv()
                own = own + recv2[a][k - 1].astype(F32)
            red_refs[a][...] = own

        for ref, out in zip(land_refs, landsum_refs):
            acc = ref[0].astype(F32)
            for d in range(1, N_DEV):
                acc = acc + ref[d].astype(F32)
            out[...] = acc
        cc_recv()
        for cp in stage1 + stage2:
            cp.wait_send()
        s_send()
        cc_send()

    vm = pl.BlockSpec(memory_space=pltpu.VMEM)
    sds = jax.ShapeDtypeStruct
    return pl.pallas_call(
        body, name="epilogue_reduce",
        out_shape=(*[sds(s, F32) for s in blk], *[sds(a.shape[1:], F32) for a in landed], sds((N_DEV, 8, D_MODEL), F32),
                   *[sds((1, w), F32) for w in small_out], sds((1, 3 * D_MODEL), F32), sds((16, n_mod), F32),
                   sds((8, 128), F32)),
        in_specs=[pl.BlockSpec(memory_space=pl.ANY)] * n_a + [vm] * (n_l + n_s + 4),
        out_specs=tuple([vm] * (n_a + n_l + n_s + 4)),
        scratch_shapes=[*[pltpu.VMEM((4,) + s, F32) for s in blk], *[pltpu.VMEM((4,) + s, F32) for s in blk],
                        *[pltpu.VMEM((4,) + s, BF16) for s in blk], *[pltpu.VMEM((3,) + s, BF16) for s in blk],
                        pltpu.VMEM((SMALL_ROWS, D_MODEL), F32), pltpu.VMEM((N_DEV, SMALL_ROWS, D_MODEL), F32),
                        pltpu.VMEM((SMALL_ROWS, D_MODEL), F32), pltpu.VMEM((8, D_MODEL), F32),
                        pltpu.SemaphoreType.DMA((n_a, 4)), pltpu.SemaphoreType.DMA((n_a, 4)), pltpu.SemaphoreType.DMA((n_a, 4)),
                        pltpu.SemaphoreType.DMA((n_a, 3)), pltpu.SemaphoreType.DMA((n_a, 3)),
                        pltpu.SemaphoreType.DMA((7,)), pltpu.SemaphoreType.DMA((7,)),
                        pltpu.SemaphoreType.DMA((7,)), pltpu.SemaphoreType.DMA((7,))],
        compiler_params=pltpu.CompilerParams(vmem_limit_bytes=VMEM_LIMIT),
    )(*parts, *landed, *smalls, dmod4, dgate, loss_acc, w_mod_l)


def _adamw(weights, grads, ms, vs, c_all_t, dmod_my, p2g):
    n = len(weights)

    def body(*refs):
        w_refs = refs[0:n]
        g_in = refs[n:2 * n - 2]
        m_refs = refs[2 * n - 2:3 * n - 2]
        v_refs = refs[3 * n - 2:4 * n - 2]
        cat_ref, dmod_ref, p2g_ref = refs[4 * n - 2:4 * n + 1]
        outs = refs[4 * n + 1:]
        g_refs, d_refs, nm_refs, nv_refs = outs[0:n], outs[n:2 * n], outs[2 * n:3 * n], outs[3 * n:4 * n]
        gcc_ref, gwm_ref = g_refs[0], g_refs[1]

        part = p2g_ref[0, 0:1, :]
        for d in range(1, N_DEV):
            part = part + p2g_ref[d, 0:1, :]
        cc = w_refs[0][...]
        sg = _sigmoid(cc)
        gcc_ref[...] = part * (sg * (1.0 + cc * (1.0 - sg)))
        cat = cat_ref[...]
        gwm_ref[...] = lax.dot_general(cat * _sigmoid(cat), dmod_ref[...], (((1,), (0,)), ((), ())), precision=HIGHEST,
                                       preferred_element_type=F32)
        for idx in range(n):
            if idx >= 2:
                g_refs[idx][...] = g_in[idx - 2][...]
            g = g_refs[idx][...]
            m = ADAM_B1 * m_refs[idx][...] + (1.0 - ADAM_B1) * g
            v = ADAM_B2 * v_refs[idx][...] + (1.0 - ADAM_B2) * (g * g)
            m_hat = m / (1.0 - ADAM_B1 ** ADAM_STEP)
            v_hat = v / (1.0 - ADAM_B2 ** ADAM_STEP)
            d_refs[idx][...] = -ADAM_LR * (m_hat / (jnp.sqrt(v_hat) + ADAM_EPS) + ADAM_WD * w_refs[idx][...])
            nm_refs[idx][...] = m
            nv_refs[idx][...] = v

    vm = pl.BlockSpec(memory_space=pltpu.VMEM)
    shapes = [jax.ShapeDtypeStruct(w.shape, F32) for w in weights]
    args = list(weights) + list(grads[2:]) + list(ms) + list(vs) + [c_all_t, dmod_my, p2g]
    out_shape = tuple(shapes * 4)
    return pl.pallas_call(
        body, name="adamw_update", out_shape=out_shape, in_specs=[vm] * len(args), out_specs=tuple([vm] * len(out_shape)),
        compiler_params=pltpu.CompilerParams(vmem_limit_bytes=VMEM_LIMIT),
    )(*args)


def _rope_tables(seq):
    rows = seq // GRID_W
    row = np.repeat(np.arange(rows, dtype=np.float32), GRID_W)
    col = np.tile(np.arange(GRID_W, dtype=np.float32), rows)
    n_freq = D_ROPE // 4
    inv = (np.float32(ROPE_BASE) ** (-np.arange(n_freq, dtype=np.float32) / np.float32(n_freq))).astype(np.float32)
    ang_r = (row[:, None] * inv).astype(np.float32)
    ang_c = (col[:, None] * inv).astype(np.float32)
    ang = np.concatenate([ang_r, ang_r, ang_c, ang_c], axis=-1)
    cos, sin = np.cos(ang).astype(np.float32), np.sin(ang).astype(np.float32)
    low = (np.arange(D_ROPE) % 32) < 16

    def table(t, fill):
        out = np.full((TILE + seq, 128), fill, np.float32)
        out[TILE:, 0:D_ROPE] = t
        return jnp.asarray(out)

    return table(cos, 1.0), table(np.where(low, -sin, 0.0), 0.0), table(np.where(low, 0.0, sin), 0.0)


def kernel(x, c, ctx, c_ctx, w_mod, b_mod, norm_g, w_in, q_lora_g, w_uq, kv_lora_g, w_ukv, q_norm_g, k_norm_g, w_pool, pool_scale, w_out, loss_target, m_c_ctx, m_w_mod, m_b_mod, m_norm_g, m_w_in, m_q_lora_g, m_w_uq, m_kv_lora_g, m_w_ukv, m_q_norm_g, m_k_norm_g, m_w_pool, m_pool_scale, m_w_out, v_c_ctx, v_w_mod, v_b_mod, v_norm_g, v_w_in, v_q_lora_g, v_w_uq, v_kv_lora_g, v_w_ukv, v_q_norm_g, v_k_norm_g, v_w_pool, v_pool_scale, v_w_out):
    seq = x.shape[1]
    assert ctx.shape[1] == TILE and seq % TILE == 0 and seq % GRID_W == 0
    ix, iy, ic = lax.axis_index("x"), lax.axis_index("y"), lax.axis_index("c")
    me = 4 * ix + 2 * iy + ic
    x2, ctx2, tgt2 = x[0], ctx[0], loss_target[0]
    w_mod_l, w_ukv_l, w_out_l = w_mod[0], w_ukv[0], w_out[0]
    c_ctx_row = c_ctx.reshape(1, D_MODEL)

    tr = lambda a: jnp.transpose(a[0])
    w_in_tl, w_uq_tl = tr(w_in), tr(w_uq)
    cg, modg, wg_in, wg_uq, wg_ukv = _prologue(
        c, c_ctx_row, w_mod_l,
        [w_in_tl, w_uq_tl, w_ukv_l])
    mod_all = jnp.transpose(modg, (1, 0, 2)).reshape(16, 3 * D_MODEL)
    mod_b = lax.dynamic_slice_in_dim(mod_all, me, 1, axis=0).reshape(3, D_MODEL)
    mod_c = mod_all[8].reshape(3, D_MODEL)
    modrows = jnp.concatenate([mod_b, mod_c[0:2], jnp.zeros((3, D_MODEL), F32)], axis=0)
    b3 = b_mod.reshape(3, D_MODEL)
    bmodrows = jnp.concatenate([b3, b3[0:2], jnp.zeros((3, D_MODEL), F32)], axis=0)

    w_in_p = wg_in.reshape(D_IN_PROJ, D_MODEL)
    w_uq_p = jnp.concatenate([wg_uq.reshape(N_HEADS, D_HEAD, R_Q), jnp.zeros((N_HEADS, D_HEAD_PAD - D_HEAD, R_Q), BF16)],
                             axis=1).reshape(N_HEADS * D_HEAD_PAD, R_Q)
    w_ukv_f = jnp.transpose(wg_ukv, (1, 0, 2)).reshape(R_KV, N_HEADS * 256)
    qng_p = jnp.concatenate([q_norm_g, jnp.zeros((1, D_HEAD_PAD - D_HEAD), F32)], axis=1)
    kng_p = jnp.concatenate([k_norm_g, jnp.zeros((1, D_HEAD_PAD - D_HEAD), F32)], axis=1)
    cos, slo, shi = _rope_tables(seq)

    u_gates, u_mla, q, k, v = _inproj_fwd(x2, ctx2, modrows, bmodrows, norm_g, w_in_p, q_lora_g, w_uq_p, kv_lora_g, w_ukv_f,
                             qng_p, kng_p, cos, slo, shi)
    attn, lse, wg_out = _attn_fwd(q, k, v, min(2048, seq), w_out_l.astype(BF16))
    (loss_acc, g1, dgate, d_attn, dga, dgp, dpl, gwo_b, gwp, dps) = _mix(
        x2, tgt2, attn, u_gates, modrows, bmodrows, w_pool[0], pool_scale, wg_out.reshape(D_MODEL, D_MODEL))

    blocks = lambda a: a.reshape((N_DEV, a.shape[0] // N_DEV) + a.shape[1:])
    dq, dk, dv, land_wo, land_wp = _attn_bwd(q, k, v, attn, lse, d_attn.reshape(seq, D_ATTN), min(1024, seq), blocks(gwo_b),
                                             gwp.reshape(4 * GROUP_DIM, GROUP_DIM))
    (grad_x, gwin_t, gwuq_p, gwukv_t, dqlg, dkvlg, dqng, dkng, dng, dmod4) = _inproj_bwd(
        x2, ctx2, modrows, bmodrows, norm_g, w_in_p, q_lora_g, w_uq_p, kv_lora_g, w_ukv_f, qng_p, kng_p, cos, slo, shi,
        u_mla, dq, dk, dv, dga, dgp, dpl, g1)

    gwuq_t = gwuq_p.reshape(N_HEADS, D_HEAD_PAD, R_Q)[:, 0:D_HEAD].reshape(N_HEADS * D_HEAD, R_Q)
    parts = [blocks(gwin_t), blocks(gwuq_t), blocks(gwukv_t)]
    (r_win, r_wuq, r_wukv, r_wout, g_w_pool, ccg, g_ng, g_qlg, g_kvlg, g_qng, g_kng, g_ps, g_bmod, dmod_my,
     loss_rows) = _epilogue(parts, [land_wo, land_wp], [dng, dqlg, dkvlg, dqng, dkng, dps], dmod4, dgate, loss_acc, w_mod_l)

    g_w_ukv = jnp.transpose(r_wukv)
    c_rows = cg[:, 0, :]
    c_all_t = jnp.transpose(jnp.concatenate([c_rows, c_ctx_row, jnp.zeros((16 - N_DEV - 1, D_MODEL), F32)], axis=0))

    weights = [c_ctx_row, w_mod_l, b_mod, norm_g, w_in_tl, q_lora_g, w_uq_tl, kv_lora_g, w_ukv_l, q_norm_g, k_norm_g,
               w_pool.reshape(4 * GROUP_DIM, GROUP_DIM), pool_scale, w_out_l]
    grads = [None, None, g_bmod, g_ng, r_win, g_qlg, r_wuq, g_kvlg, g_w_ukv, g_qng, g_kng, g_w_pool, g_ps, r_wout]
    ms = [m_c_ctx.reshape(1, D_MODEL), m_w_mod[0], m_b_mod, m_norm_g, tr(m_w_in), m_q_lora_g, tr(m_w_uq), m_kv_lora_g,
          m_w_ukv[0], m_q_norm_g, m_k_norm_g, m_w_pool.reshape(4 * GROUP_DIM, GROUP_DIM), m_pool_scale, m_w_out[0]]
    vs = [v_c_ctx.reshape(1, D_MODEL), v_w_mod[0], v_b_mod, v_norm_g, tr(v_w_in), v_q_lora_g, tr(v_w_uq), v_kv_lora_g,
          v_w_ukv[0], v_q_norm_g, v_k_norm_g, v_w_pool.reshape(4 * GROUP_DIM, GROUP_DIM), v_pool_scale, v_w_out[0]]
    outs = _adamw(weights, grads, ms, vs, c_all_t, dmod_my, ccg)
    n = len(weights)
    grads, deltas, new_m, new_v = outs[0:n], outs[n:2 * n], outs[2 * n:3 * n], outs[3 * n:4 * n]

    loss = loss_rows[ROW_LOSS - 8, 0]
    final = [c_ctx.shape, w_mod.shape, b_mod.shape, norm_g.shape, w_in.shape, q_lora_g.shape, w_uq.shape, kv_lora_g.shape,
             w_ukv.shape, q_norm_g.shape, k_norm_g.shape, w_pool.shape, pool_scale.shape, w_out.shape]
    transposed = (4, 6)

    def shaped(arrs):
        return [(jnp.transpose(a) if i in transposed else a).reshape(s) for i, (a, s) in enumerate(zip(arrs, final))]

    return (loss, grad_x[None], *shaped(grads), *shaped(deltas), *shaped(new_m), *shaped(new_v))
```

```python
import numpy as np

import jax
import jax.numpy as jnp
from jax import lax
from jax.experimental import pallas as pl
from jax.experimental.pallas import tpu as pltpu

F32 = jnp.float32
BF16 = jnp.bfloat16
MESH = pl.DeviceIdType.MESH
HIGHEST = lax.Precision.HIGHEST

D_MODEL = 1024
N_HEADS = 4
D_NOPE = 128
D_ROPE = 64
D_HEAD = D_NOPE + D_ROPE
D_HEAD_PAD = 256
D_V = 128
R_Q = 256
R_KV = 128
D_ATTN = 512
D_POOL = 512
POOL_WINDOWS = (2, 4, 8, 16)
GROUP_DIM = 128
GRID_W = 64
ROPE_BASE = 10000.0
NORM_EPS = 1e-6
ATTN_SCALE = D_HEAD ** -0.5
LOG2_E = 1.4426950408889634
LN_2 = 0.6931471805599453
ATTN_ROWS = 256
D_IN_PROJ = 1984
U_PAD = 2048
O_GA, O_PIN, O_GP, O_CQ, O_CKV, O_KR = 0, 512, 1024, 1536, 1792, 1920
TILE = 256
MIX_TILE = 512
Q_BLOCK = 128
HALO = 16
N_GATES = 1536
N_MLA = D_IN_PROJ - N_GATES
N_DEV = 8
VMEM_LIMIT = 56 * 1024 * 1024

ADAM_LR = 0.001
ADAM_B1 = 0.9
ADAM_B2 = 0.999
ADAM_EPS = 1e-08
ADAM_WD = 0.01
ADAM_STEP = 10

SMALL_ROWS = 16


def _dot(a, b):
    return lax.dot_general(a, b, (((1,), (0,)), ((), ())), preferred_element_type=F32)


def _dot_nt(a, b):
    return lax.dot_general(a, b, (((1,), (1,)), ((), ())), preferred_element_type=F32)


def _dot_tn(a, b):
    return lax.dot_general(a, b, (((0,), (0,)), ((), ())), preferred_element_type=F32)


def _sigmoid(x):
    return 1.0 / (1.0 + jnp.exp(-x))


def _rope(p, cos, slo, shi):
    return p * cos + pltpu.roll(p, 112, 1) * slo + pltpu.roll(p, 16, 1) * shi


def _rope_t(d, cos, slo, shi):
    return d * cos + pltpu.roll(d * slo, 16, 1) + pltpu.roll(d * shi, 112, 1)


def _shift_rows(a, k):
    n = a.shape[0]
    return pltpu.roll(a, (-k) % n, 0)


def _window_sum(x, w, transposed):
    s = (x + _shift_rows(x, 1)) if transposed else (_shift_rows(x, -1) + x)
    step = 1
    while 2 * step < w:
        s = _shift_rows(s, -step) + _shift_rows(s, step)
        step *= 2
    return s


def _inv_count(tpos, w, seq):
    lo = jnp.maximum(tpos - w // 2, 0)
    hi = jnp.minimum(tpos - w // 2 + w, seq)
    return 1.0 / jnp.maximum(hi - lo, 1).astype(F32)


def _coords():
    return lax.axis_index("x"), lax.axis_index("y"), lax.axis_index("c")


def _flip(v, bit):
    return (1 - v) if bit else v


def _peer(k):
    x, y, c = _coords()
    return (_flip(x, (k >> 2) & 1), _flip(y, (k >> 1) & 1), _flip(c, k & 1))


def _lin(p):
    return 4 * p[0] + 2 * p[1] + p[2]


def _gather_to_all(src_ref, slots_ref, send_sems, recv_sems):
    me = _coords()
    copies = []
    for k in range(1, N_DEV):
        cp = pltpu.make_async_remote_copy(
            src_ref=src_ref, dst_ref=slots_ref.at[_lin(me)], send_sem=send_sems.at[k - 1], recv_sem=recv_sems.at[k - 1],
            device_id=_peer(k), device_id_type=MESH)
        cp.start()
        copies.append(cp)

    def wait_recv():
        for k in range(1, N_DEV):
            pltpu.make_async_remote_copy(
                src_ref=src_ref, dst_ref=slots_ref.at[_lin(_peer(k))], send_sem=send_sems.at[k - 1],
                recv_sem=recv_sems.at[k - 1], device_id=_peer(k), device_id_type=MESH).wait_recv()

    def wait_send():
        for cp in copies:
            cp.wait_send()

    return wait_recv, wait_send


def _prologue(c8, cctx8, w_mod_l, shards):
    n_mod = w_mod_l.shape[1]
    n_w = len(shards)

    def body(c_ref, cctx_ref, wmod_ref, *refs):
        w_refs = refs[0:n_w]
        cg_ref, modg_ref = refs[n_w], refs[n_w + 1]
        wg_refs = refs[n_w + 2:2 * n_w + 2]
        modblk_ref = refs[2 * n_w + 2]
        wb_refs = refs[2 * n_w + 3:3 * n_w + 3]
        c8_ref = refs[3 * n_w + 3]
        ssem_w, rsem_w, ssem_c, rsem_c, ssem_m, rsem_m = refs[3 * n_w + 4:]
        x, y, c = _coords()
        me = (x, y, c)
        sibling = (x, y, 1 - c)
        chips = [(1 - x, y), (x, 1 - y), (1 - x, 1 - y)]

        half = shards[0].shape[1] // 2
        pieces = [(0, pl.ds(0, half)), (0, pl.ds(half, half))] + [(a, None) for a in range(1, n_w)]
        n_p = len(pieces)

        def wcopies(k, block, to, own=False):
            out = []
            for p, (a, cols) in enumerate(pieces):
                slot = wg_refs[a].at[_lin(block)]
                src = wb_refs[a] if own else slot
                if cols is not None:
                    slot, src = slot.at[:, cols], src.at[:, cols]
                out.append(pltpu.make_async_remote_copy(
                    src_ref=src, dst_ref=slot, send_sem=ssem_w.at[p, k], recv_sem=rsem_w.at[p, k],
                    device_id=to, device_id_type=MESH))
            return out

        c8_ref[...] = jnp.broadcast_to(c_ref[...], (8, D_MODEL))
        cg_ref[_lin(me)] = c8_ref[...]
        c_recv, c_send = _gather_to_all(c8_ref, cg_ref, ssem_c, rsem_c)

        for a in range(n_w):
            wb_refs[a][...] = w_refs[a][...].astype(BF16)
        first = wcopies(0, me, sibling, own=True)
        for j, chip in enumerate(chips):
            first += wcopies(1 + j, me, (*chip, c), own=True)
        late = [idx for idx in range(len(first)) if idx % n_p == 1 and idx >= n_p]
        for idx, cp in enumerate(first):
            if idx not in late:
                cp.start()
        for a in range(n_w):
            wg_refs[a][_lin(me)] = wb_refs[a][...]

        c_recv()
        row = lax.broadcasted_iota(jnp.int32, (8, D_MODEL), 0)
        c_all = jnp.zeros((8, D_MODEL), F32)
        for d in range(N_DEV):
            c_all = c_all + jnp.where(row == d, cg_ref[d], 0.0)
        cc = jnp.broadcast_to(cctx_ref[...], (8, D_MODEL))
        a = jnp.concatenate([c_all * _sigmoid(c_all), cc * _sigmoid(cc)], axis=0)
        modblk_ref[...] = lax.dot_general(a, wmod_ref[...], (((1,), (0,)), ((), ())), precision=HIGHEST,
                                          preferred_element_type=F32)
        modg_ref[_lin(me)] = modblk_ref[...]
        m_recv, m_send = _gather_to_all(modblk_ref, modg_ref, ssem_m, rsem_m)
        for idx in late:
            first[idx].start()
        m_recv()

        passed = []
        for j, chip in enumerate(chips):
            for cp in wcopies(1 + j, (*chip, c), me):
                cp.wait_recv()
            fwd = wcopies(4 + j, (*chip, c), sibling)
            for cp in fwd:
                cp.start()
            passed += fwd
        for cp in wcopies(0, sibling, me):
            cp.wait_recv()
        for j, chip in enumerate(chips):
            for cp in wcopies(4 + j, (*chip, 1 - c), me):
                cp.wait_recv()
        for cp in first + passed:
            cp.wait_send()
        c_send()
        m_send()

    vm = pl.BlockSpec(memory_space=pltpu.VMEM)
    return pl.pallas_call(
        body, name="prologue_gather",
        out_shape=(jax.ShapeDtypeStruct((N_DEV, 8, D_MODEL), F32), jax.ShapeDtypeStruct((N_DEV, 16, n_mod), F32),
                   *[jax.ShapeDtypeStruct((N_DEV,) + s.shape, BF16) for s in shards]),
        in_specs=[vm] * (3 + n_w), out_specs=tuple([vm] * (2 + n_w)),
        scratch_shapes=[pltpu.VMEM((16, n_mod), F32), *[pltpu.VMEM(s.shape, BF16) for s in shards],
                        pltpu.VMEM((8, D_MODEL), F32),
                        pltpu.SemaphoreType.DMA((n_w + 1, 7)), pltpu.SemaphoreType.DMA((n_w + 1, 7)),
                        pltpu.SemaphoreType.DMA((7,)), pltpu.SemaphoreType.DMA((7,)),
                        pltpu.SemaphoreType.DMA((7,)), pltpu.SemaphoreType.DMA((7,))],
        compiler_params=pltpu.CompilerParams(vmem_limit_bytes=VMEM_LIMIT),
    )(c8, cctx8, w_mod_l, *shards)


def _modulated_input(is_ctx, x_ref, ctx_ref, mod_ref, bmod_ref, ng_ref):
    xt = jnp.where(is_ctx, ctx_ref[...], x_ref[...])
    shift = jnp.where(is_ctx, mod_ref[3:4, :] + bmod_ref[3:4, :], mod_ref[0:1, :] + bmod_ref[0:1, :])
    scale = jnp.where(is_ctx, mod_ref[4:5, :] + bmod_ref[4:5, :], mod_ref[1:2, :] + bmod_ref[1:2, :])
    r = lax.rsqrt(jnp.mean(xt * xt, axis=-1, keepdims=True) + NORM_EPS)
    xg = (xt * r) * ng_ref[...]
    h = xg * (1.0 + scale) + shift
    return xt, r, xg, h, scale


def _inproj_fwd(x, ctx, modrows, bmodrows, norm_g, w_in_p, q_lora_g, w_uq_p, kv_lora_g, w_ukv, qng_p, kng_p, cos, slo, shi):
    seq = x.shape[0]
    n_tiles = seq // TILE + 1
    tot = seq + TILE

    n_mla = R_Q + R_KV + 128

    def body(x_ref, ctx_ref, mod_ref, bmod_ref, ng_ref, win_ref, qlg_ref, wuq_ref, kvlg_ref, wukv_ref, qng_ref, kng_ref,
             cos_ref, slo_ref, shi_ref, ug_ref, um_ref, q_ref, k_ref, v_ref, stash_even, stash_odd):
        s = pl.program_id(0)

        @pl.when(s == 0)
        def _():
            stash_odd[...] = jnp.zeros_like(stash_odd)

        refs = (x_ref, ctx_ref, mod_ref, bmod_ref, ng_ref, win_ref, qlg_ref, wuq_ref, kvlg_ref, wukv_ref, qng_ref, kng_ref,
                cos_ref, slo_ref, shi_ref, ug_ref, um_ref, q_ref, k_ref, v_ref)

        @pl.when(lax.rem(s, 2) == 0)
        def _():
            stages(s, refs, stash_even, stash_odd)

        @pl.when(lax.rem(s, 2) == 1)
        def _():
            stages(s, refs, stash_odd, stash_even)

    def stages(s, refs, fill, prev_ref):
        (x_ref, ctx_ref, mod_ref, bmod_ref, ng_ref, win_ref, qlg_ref, wuq_ref, kvlg_ref, wukv_ref, qng_ref, kng_ref,
         cos_ref, slo_ref, shi_ref, ug_ref, um_ref, q_ref, k_ref, v_ref) = refs
        cos_t, slo_t, shi_t = cos_ref[...], slo_ref[...], shi_ref[...]
        prev = prev_ref[...]
        cq = prev[:, 0:R_Q]
        qn = cq * lax.rsqrt(jnp.mean(cq * cq, axis=-1, keepdims=True) + NORM_EPS) * qlg_ref[...]
        q = _dot_nt(qn.astype(BF16), wuq_ref[...])
        qg = qng_ref[...] * (ATTN_SCALE * LOG2_E)
        for hd in range(N_HEADS):
            qh = q[:, hd * D_HEAD_PAD:(hd + 1) * D_HEAD_PAD]
            rr = lax.rsqrt(jnp.sum(qh * qh, axis=-1, keepdims=True) * (1.0 / D_HEAD) + NORM_EPS)
            qy = qh * rr * qg
            q_ref[hd, :, 0:D_NOPE] = qy[:, 0:D_NOPE].astype(BF16)
            q_ref[hd, :, D_NOPE:D_HEAD_PAD] = _rope(qy[:, D_NOPE:D_HEAD_PAD], cos_t, slo_t, shi_t).astype(BF16)

        ckv = prev[:, R_Q:R_Q + R_KV]
        kvn = ckv * lax.rsqrt(jnp.mean(ckv * ckv, axis=-1, keepdims=True) + NORM_EPS) * kvlg_ref[...]
        kv = _dot(kvn.astype(BF16), wukv_ref[...])
        krz = prev[:, R_Q + R_KV:n_mla]
        kr_ss = jnp.sum(krz * krz, axis=-1, keepdims=True)
        kg = kng_ref[...]
        for hd in range(N_HEADS):
            kn = kv[:, hd * 256:hd * 256 + D_NOPE]
            rr = lax.rsqrt((jnp.sum(kn * kn, axis=-1, keepdims=True) + kr_ss) * (1.0 / D_HEAD) + NORM_EPS)
            k_ref[hd, :, 0:D_NOPE] = (kn * rr * kg[:, 0:D_NOPE]).astype(BF16)
            k_ref[hd, :, D_NOPE:D_HEAD_PAD] = _rope(krz * rr * kg[:, D_NOPE:D_HEAD_PAD], cos_t, slo_t, shi_t).astype(BF16)
            v_ref[hd] = kv[:, hd * 256 + D_NOPE:(hd + 1) * 256].astype(BF16)

        _, _, _, h, _ = _modulated_input(s == 0, x_ref, ctx_ref, mod_ref, bmod_ref, ng_ref)
        hb = h.astype(BF16)
        ug_ref[...] = _dot_nt(hb, win_ref[N_MLA:D_IN_PROJ, :]).astype(BF16)
        um = _dot_nt(hb, win_ref[0:n_mla, :])
        lane = lax.broadcasted_iota(jnp.int32, (TILE, 128), 1)
        um = jnp.concatenate([um[:, 0:R_Q + R_KV], jnp.where(lane < D_ROPE, um[:, R_Q + R_KV:n_mla], 0.0)], axis=1)
        um_ref[...] = um
        fill[...] = um

    first = lambda s: jnp.minimum(s, n_tiles - 1)
    second = lambda s: jnp.maximum(s - 1, 0)
    latent = lambda t: jnp.maximum(t - 1, 0)
    full = lambda shape: pl.BlockSpec(shape, lambda s: (0,) * len(shape))
    rope_spec = pl.BlockSpec((TILE, 128), lambda s: (second(s), 0))
    return pl.pallas_call(
        body, name="inproj_fwd", grid=(n_tiles + 1,),
        out_shape=(jax.ShapeDtypeStruct((seq, N_GATES), BF16), jax.ShapeDtypeStruct((tot, n_mla), F32),
                   jax.ShapeDtypeStruct((N_HEADS, seq, D_HEAD_PAD), BF16),
                   jax.ShapeDtypeStruct((N_HEADS, tot, D_HEAD_PAD), BF16),
                   jax.ShapeDtypeStruct((N_HEADS, tot, D_V), BF16)),
        in_specs=[pl.BlockSpec((TILE, D_MODEL), lambda s: (latent(first(s)), 0)), full((TILE, D_MODEL)), full((8, D_MODEL)),
                  full((8, D_MODEL)), full((1, D_MODEL)), full((D_IN_PROJ, D_MODEL)), full((1, R_Q)),
                  full((N_HEADS * D_HEAD_PAD, R_Q)), full((1, R_KV)), full((R_KV, N_HEADS * 256)), full((1, D_HEAD_PAD)),
                  full((1, D_HEAD_PAD)), rope_spec, rope_spec, rope_spec],
        out_specs=(pl.BlockSpec((TILE, N_GATES), lambda s: (latent(first(s)), 0)),
                   pl.BlockSpec((TILE, n_mla), lambda s: (first(s), 0)),
                   pl.BlockSpec((N_HEADS, TILE, D_HEAD_PAD), lambda s: (0, latent(second(s)), 0)),
                   pl.BlockSpec((N_HEADS, TILE, D_HEAD_PAD), lambda s: (0, second(s), 0)),
                   pl.BlockSpec((N_HEADS, TILE, D_V), lambda s: (0, second(s), 0))),
        scratch_shapes=[pltpu.VMEM((TILE, n_mla), F32), pltpu.VMEM((TILE, n_mla), F32)],
        compiler_params=pltpu.CompilerParams(dimension_semantics=("arbitrary",), vmem_limit_bytes=VMEM_LIMIT),
    )(x, ctx, modrows, bmodrows, norm_g, w_in_p, q_lora_g, w_uq_p, kv_lora_g, w_ukv, qng_p, kng_p, cos, slo, shi)


def _hosted_exchange(first, last, items, send_sems, recv_sems, local_sems):
    me = _lin(_coords())

    def remote(n, k, src, land, scatter):
        peer = _peer(k)
        return pltpu.make_async_remote_copy(
            src_ref=src.at[_lin(peer)] if scatter else src, dst_ref=land.at[me], send_sem=send_sems.at[n, k - 1],
            recv_sem=recv_sems.at[n, k - 1], device_id=peer, device_id_type=MESH)

    def local(n, src, land, scatter):
        return pltpu.make_async_copy(src.at[me] if scatter else src, land.at[me], local_sems.at[n])

    @pl.when(first)
    def _():
        for n, (src, land, scatter) in enumerate(items):
            for k in range(1, N_DEV):
                remote(n, k, src, land, scatter).start()
            local(n, src, land, scatter).start()

    @pl.when(last)
    def _():
        for n, (src, land, scatter) in enumerate(items):
            for k in range(1, N_DEV):
                peer = _peer(k)
                pltpu.make_async_remote_copy(
                    src_ref=src.at[0] if scatter else src, dst_ref=land.at[_lin(peer)], send_sem=send_sems.at[n, k - 1],
                    recv_sem=recv_sems.at[n, k - 1], device_id=peer, device_id_type=MESH).wait_recv()
            for k in range(1, N_DEV):
                remote(n, k, src, land, scatter).wait_send()
            local(n, src, land, scatter).wait()


def _attn_fwd(q, k, v, block_q, w_out_b):
    _, seq, _ = q.shape
    n_keys = k.shape[1]
    n_q = seq // block_q

    def body(q_ref, k_ref, v_ref, wo_ref, o_ref, lse_ref, wog_ref, ssem, rsem, lsem):
        h, i = pl.program_id(0), pl.program_id(1)
        _hosted_exchange((h == 0) & (i == 0), (h == N_HEADS - 1) & (i == n_q - 1), [(wo_ref, wog_ref, False)],
                         ssem, rsem, lsem)
        kb, vb = k_ref[0], v_ref[0]
        for r0 in range(0, block_q, ATTN_ROWS):
            rows = slice(r0, r0 + ATTN_ROWS)
            s = _dot_nt(q_ref[0, rows, :], kb)
            m = jnp.max(s, axis=-1, keepdims=True)
            p = jnp.exp2(s - m)
            l = jnp.sum(p, axis=-1, keepdims=True)
            o_ref[rows, :] = _dot(p.astype(BF16), vb) * (1.0 / l)
            lse_ref[0, rows, :] = m + jnp.log2(l)

    hbm = pl.BlockSpec(memory_space=pl.ANY)
    return pl.pallas_call(
        body, name="attn_fwd", grid=(N_HEADS, n_q),
        out_shape=(jax.ShapeDtypeStruct((seq, D_ATTN), F32), jax.ShapeDtypeStruct((N_HEADS, seq, 1), F32),
                   jax.ShapeDtypeStruct((N_DEV,) + w_out_b.shape, w_out_b.dtype)),
        in_specs=[pl.BlockSpec((1, block_q, D_HEAD_PAD), lambda h, i: (h, i, 0)),
                  pl.BlockSpec((1, n_keys, D_HEAD_PAD), lambda h, i: (h, 0, 0)),
                  pl.BlockSpec((1, n_keys, D_V), lambda h, i: (h, 0, 0)), hbm],
        out_specs=(pl.BlockSpec((block_q, D_V), lambda h, i: (i, h)),
                   pl.BlockSpec((1, block_q, 1), lambda h, i: (h, i, 0)), hbm),
        scratch_shapes=[pltpu.SemaphoreType.DMA((1, 7)), pltpu.SemaphoreType.DMA((1, 7)), pltpu.SemaphoreType.DMA((1,))],
        compiler_params=pltpu.CompilerParams(dimension_semantics=("arbitrary", "arbitrary"), vmem_limit_bytes=VMEM_LIMIT),
    )(q, k, v, w_out_b)


def _attn_bwd(q, k, v, o, lse, d_o, block_q, gwo_blocks, gwp):
    _, seq, _ = q.shape
    n_keys = k.shape[1]
    n_q = seq // block_q

    def body(q_ref, k_ref, v_ref, o_ref, lse_ref, do_ref, gwo_ref, gwp_ref, dq_ref, dkt_ref, dvt_ref, lwo_ref, lwp_ref,
             ssem, rsem, lsem):
        h, i = pl.program_id(0), pl.program_id(1)
        _hosted_exchange((h == 0) & (i == 0), (h == N_HEADS - 1) & (i == n_q - 1),
                         [(gwo_ref, lwo_ref, True), (gwp_ref, lwp_ref, False)], ssem, rsem, lsem)

        @pl.when(i == 0)
        def _():
            dkt_ref[...] = jnp.zeros_like(dkt_ref)
            dvt_ref[...] = jnp.zeros_like(dvt_ref)

        kb, vb = k_ref[0], v_ref[0]
        raws, ps = [], []
        for r0 in range(0, block_q, ATTN_ROWS):
            rows = slice(r0, r0 + ATTN_ROWS)
            d_out = do_ref[rows, :]
            p = jnp.exp2(_dot_nt(q_ref[0, rows, :], kb) - lse_ref[0, rows, :])
            dp = _dot_nt(d_out.astype(BF16), vb)
            delta = jnp.sum(d_out * o_ref[rows, :], axis=-1, keepdims=True)
            raw = (p * (dp - delta)).astype(BF16)
            dq_ref[0, rows, :] = _dot(raw, kb)
            raws.append(raw)
            ps.append(p.astype(BF16))
        dkt_ref[0] += _dot_tn(q_ref[0], jnp.concatenate(raws, axis=0))
        dvt_ref[0] += _dot_tn(do_ref[...].astype(BF16), jnp.concatenate(ps, axis=0))

    hbm = pl.BlockSpec(memory_space=pl.ANY)
    return pl.pallas_call(
        body, name="attn_bwd", grid=(N_HEADS, n_q),
        out_shape=(jax.ShapeDtypeStruct((N_HEADS, seq, D_HEAD_PAD), F32),
                   jax.ShapeDtypeStruct((N_HEADS, D_HEAD_PAD, n_keys), F32),
                   jax.ShapeDtypeStruct((N_HEADS, D_V, n_keys), F32),
                   jax.ShapeDtypeStruct(gwo_blocks.shape, gwo_blocks.dtype),
                   jax.ShapeDtypeStruct((N_DEV,) + gwp.shape, gwp.dtype)),
        in_specs=[pl.BlockSpec((1, block_q, D_HEAD_PAD), lambda h, i: (h, i, 0)),
                  pl.BlockSpec((1, n_keys, D_HEAD_PAD), lambda h, i: (h, 0, 0)),
                  pl.BlockSpec((1, n_keys, D_V), lambda h, i: (h, 0, 0)),
                  pl.BlockSpec((block_q, D_V), lambda h, i: (i, h)),
                  pl.BlockSpec((1, block_q, 1), lambda h, i: (h, i, 0)),
                  pl.BlockSpec((block_q, D_V), lambda h, i: (i, h)), hbm, hbm],
        out_specs=(pl.BlockSpec((1, block_q, D_HEAD_PAD), lambda h, i: (h, i, 0)),
                   pl.BlockSpec((1, D_HEAD_PAD, n_keys), lambda h, i: (h, 0, 0)),
                   pl.BlockSpec((1, D_V, n_keys), lambda h, i: (h, 0, 0)), hbm, hbm),
        scratch_shapes=[pltpu.SemaphoreType.DMA((2, 7)), pltpu.SemaphoreType.DMA((2, 7)), pltpu.SemaphoreType.DMA((2,))],
        compiler_params=pltpu.CompilerParams(dimension_semantics=("arbitrary", "arbitrary"), vmem_limit_bytes=VMEM_LIMIT),
    )(q, k, v, o, lse, d_o, gwo_blocks, gwp)


def _mix(x, target, attn, u, modrows, bmodrows, w_pool, pool_scale, w_out):
    seq = x.shape[0]
    rows = min(MIX_TILE, seq // 2)
    n_tiles = seq // rows
    rows8 = rows // HALO
    last8 = seq // HALO - 1

    n_blk = seq // Q_BLOCK
    per = rows // n_blk
    assert rows % n_blk == 0 and per % 8 == 0 and n_tiles >= 2
    n_stash = 8

    def body(x_ref, t_ref, o_hbm, ga_ref, pin_ref, hb_ref, ha_ref, gp_ref, mod_ref, bmod_ref, wp_ref, ps_ref, wo_ref,
             loss_ref, g1_ref, dgate_ref, do_hbm, dga_ref, dgp_ref, dpl_ref, gwob_ref, gwp_ref, dps_ref,
             abuf, dbuf, gwo_ref, *rest):
        stash_even, stash_odd = rest[0:n_stash], rest[n_stash:2 * n_stash]
        rsem, wsem = rest[2 * n_stash:]
        s = pl.program_id(0)

        def slab_copies(tile, slot, fetch):
            window = pl.ds(tile * per, per)
            if fetch:
                return [pltpu.make_async_copy(o_hbm.at[:, window, :], abuf.at[slot], rsem.at[slot])]
            return [pltpu.make_async_copy(dbuf.at[slot], do_hbm.at[:, window, :], wsem.at[slot])]

        def wait_all(slot, fetch):
            slab_copies(0, slot, fetch)[0].wait()

        a_slot = lax.rem(s, 2)
        b_slot = 1 - a_slot

        @pl.when(s == 0)
        def _():
            loss_ref[...] = jnp.zeros_like(loss_ref)
            dgate_ref[...] = jnp.zeros_like(dgate_ref)
            gwo_ref[...] = jnp.zeros_like(gwo_ref)
            gwp_ref[...] = jnp.zeros_like(gwp_ref)
            dps_ref[...] = jnp.zeros_like(dps_ref)
            for cp in slab_copies(0, 0, True):
                cp.start()

        @pl.when(s + 1 < n_tiles)
        def _():
            for cp in slab_copies(s + 1, b_slot, True):
                cp.start()

        @pl.when(s < n_tiles)
        def _():
            wait_all(a_slot, True)

        @pl.when(s >= 3)
        def _():
            wait_all(b_slot, False)

        gate = mod_ref[2:3, :] + bmod_ref[2:3, :]
        ps = ps_ref[...]

        def a_vector(slot):
            attn_t = jnp.swapaxes(abuf[slot], 0, 1).reshape(rows, D_ATTN)
            ga = ga_ref[...].astype(F32)
            sga = _sigmoid(ga)
            silu_ga = ga * sga
            pin = pin_ref[...].astype(F32)
            xs = jnp.concatenate([jnp.where(s > 0, hb_ref[...].astype(F32), 0.0), pin,
                                  jnp.where(s < n_tiles - 1, ha_ref[...].astype(F32), 0.0)], axis=0)
            tpos = s * rows + lax.broadcasted_iota(jnp.int32, (rows, 1), 0)
            pooled = []
            for g, w in enumerate(POOL_WINDOWS):
                sl = slice(g * GROUP_DIM, (g + 1) * GROUP_DIM)
                ws = _window_sum(xs[:, sl], w, False)[HALO:HALO + rows]
                pooled.append((ws * _inv_count(tpos, w, seq) - pin[:, sl]).astype(BF16))
            return attn_t, ga, sga, silu_ga, pooled

        def a_group_maps(pooled):
            return jnp.concatenate([_dot(pooled[g], wp_ref[g].astype(BF16)) for g in range(len(POOL_WINDOWS))], axis=1)

        def a_project(stash, yp, attn_t, ga, sga, silu_ga, pooled):
            zb_ref, _, fa_ref, sa_ref, fp_ref, sp_ref, yp_ref, pooled_ref = stash
            ypool = yp * ps
            gp = gp_ref[...].astype(F32)
            sgp = _sigmoid(gp)
            silu_gp = gp * sgp
            z = jnp.concatenate([silu_ga * attn_t, silu_gp * ypool], axis=1).astype(BF16)
            y = _dot(z, wo_ref[...])
            zb_ref[...] = z
            fa_ref[...] = attn_t * (sga * (1.0 + ga * (1.0 - sga)))
            sa_ref[...] = silu_ga
            fp_ref[...] = ypool * (sgp * (1.0 + gp * (1.0 - sgp)))
            sp_ref[...] = silu_gp
            yp_ref[...] = yp
            pooled_ref[...] = jnp.concatenate(pooled, axis=1)
            return y

        def a_loss(stash, y):
            res = x_ref[...] + gate * y - t_ref[...]
            loss_ref[...] += jnp.sum(res * res) * (0.5 / D_MODEL)
            dxn = res * (1.0 / D_MODEL)
            g1_ref[...] = dxn
            dgate_ref[...] += jnp.sum(dxn * y, axis=0, keepdims=True)
            stash[1][...] = (gate * dxn).astype(BF16)

        def b_dz(stash):
            return _dot_nt(stash[1][...], wo_ref[...])

        def b_gwo(stash):
            gwo_ref[...] += _dot_tn(stash[0][...], stash[1][...])

        def b_vector(stash, slot, dz):
            _, _, fa_ref, sa_ref, fp_ref, sp_ref, yp_ref, _ = stash
            dbra = dz[:, 0:D_ATTN]
            dbrp = dz[:, D_ATTN:]
            dga_ref[...] = (dbra * fa_ref[...]).astype(BF16)
            dbuf[slot] = jnp.swapaxes((dbra * sa_ref[...]).reshape(per, n_blk, D_ATTN), 0, 1)
            dgp_ref[...] = (dbrp * fp_ref[...]).astype(BF16)
            dyp_s = dbrp * sp_ref[...]
            dps_ref[...] += jnp.sum(dyp_s * yp_ref[...], axis=0, keepdims=True)
            return (dyp_s * ps).astype(BF16)

        def b_small(stash, dyp):
            pooled_ref = stash[7]
            for g in range(len(POOL_WINDOWS)):
                sl = slice(g * GROUP_DIM, (g + 1) * GROUP_DIM)
                gwp_ref[g] += _dot_tn(pooled_ref[:, sl], dyp[:, sl])
                dpl_ref[:, sl] = _dot_nt(dyp[:, sl], wp_ref[g].astype(BF16)).astype(BF16)

        def both(a_stash, b_stash, slot_a):
            dz = b_dz(b_stash)
            front = a_vector(slot_a)
            yp = a_group_maps(front[-1])
            b_gwo(b_stash)
            y = a_project(a_stash, yp, *front)
            dyp = b_vector(b_stash, 1 - slot_a, dz)
            a_loss(a_stash, y)
            b_small(b_stash, dyp)

        @pl.when(s == 0)
        def _():
            front = a_vector(0)
            a_loss(stash_even, a_project(stash_even, a_group_maps(front[-1]), *front))

        @pl.when((s > 0) & (s < n_tiles) & (a_slot == 0))
        def _():
            both(stash_even, stash_odd, 0)

        @pl.when((s > 0) & (s < n_tiles) & (a_slot == 1))
        def _():
            both(stash_odd, stash_even, 1)

        @pl.when(s == n_tiles)
        def _():
            last = (n_tiles - 1) % 2
            stash = stash_odd if last else stash_even
            dz = b_dz(stash)
            b_gwo(stash)
            b_small(stash, b_vector(stash, last, dz))
            gwob_ref[...] = gwo_ref[...].astype(BF16)

        @pl.when(s >= 1)
        def _():
            for cp in slab_copies(s - 1, b_slot, False):
                cp.start()

        @pl.when(s == n_tiles)
        def _():
            wait_all(b_slot, False)
            wait_all(a_slot, False)

    ta = lambda s: jnp.minimum(s, n_tiles - 1)
    tb = lambda s: jnp.maximum(s - 1, 0)
    tile = lambda w: pl.BlockSpec((rows, w), lambda s: (ta(s), 0))
    tile_b = lambda w: pl.BlockSpec((rows, w), lambda s: (tb(s), 0))
    ucol = lambda col: pl.BlockSpec((rows, 512), lambda s: (ta(s), col))
    full = lambda shape: pl.BlockSpec(shape, lambda s: (0,) * len(shape))
    stash_shapes = [pltpu.VMEM((rows, D_MODEL), BF16), pltpu.VMEM((rows, D_MODEL), BF16)] + \
        [pltpu.VMEM((rows, 512), F32)] * 5 + [pltpu.VMEM((rows, D_POOL), BF16)]
    return pl.pallas_call(
        body, name="mix_fwd_bwd", grid=(n_tiles + 1,),
        out_shape=(jax.ShapeDtypeStruct((8, 128), F32), jax.ShapeDtypeStruct((seq, D_MODEL), F32),
                   jax.ShapeDtypeStruct((1, D_MODEL), F32), jax.ShapeDtypeStruct((n_blk, Q_BLOCK, D_ATTN), F32),
                   jax.ShapeDtypeStruct((seq, D_ATTN), BF16), jax.ShapeDtypeStruct((seq, D_POOL), BF16),
                   jax.ShapeDtypeStruct((seq, D_POOL), BF16), jax.ShapeDtypeStruct((D_MODEL, D_MODEL), BF16),
                   jax.ShapeDtypeStruct((4, GROUP_DIM, GROUP_DIM), F32), jax.ShapeDtypeStruct((1, D_POOL), F32)),
        in_specs=[tile(D_MODEL), tile(D_MODEL), pl.BlockSpec(memory_space=pl.ANY), ucol(0), ucol(1),
                  pl.BlockSpec((HALO, 512), lambda s: (jnp.maximum(ta(s) * rows8 - 1, 0), 1)),
                  pl.BlockSpec((HALO, 512), lambda s: (jnp.minimum((ta(s) + 1) * rows8, last8), 1)),
                  ucol(2), full((8, D_MODEL)), full((8, D_MODEL)), full((4, GROUP_DIM, GROUP_DIM)), full((1, D_POOL)),
                  full((D_MODEL, D_MODEL))],
        out_specs=(full((8, 128)), tile(D_MODEL), full((1, D_MODEL)), pl.BlockSpec(memory_space=pl.ANY), tile_b(D_ATTN),
                   tile_b(D_POOL), tile_b(D_POOL), full((D_MODEL, D_MODEL)), full((4, GROUP_DIM, GROUP_DIM)),
                   full((1, D_POOL))),
        scratch_shapes=[pltpu.VMEM((2, n_blk, per, D_ATTN), F32), pltpu.VMEM((2, n_blk, per, D_ATTN), F32),
                        pltpu.VMEM((D_MODEL, D_MODEL), F32), *stash_shapes, *stash_shapes,
                        pltpu.SemaphoreType.DMA((2,)), pltpu.SemaphoreType.DMA((2,))],
        compiler_params=pltpu.CompilerParams(dimension_semantics=("arbitrary",), vmem_limit_bytes=VMEM_LIMIT),
    )(x, target, attn.reshape(n_blk, Q_BLOCK, D_ATTN), u, u, u, u, u, modrows, bmodrows, w_pool, pool_scale, w_out)


def _inproj_bwd(x, ctx, modrows, bmodrows, norm_g, w_in_p, q_lora_g, w_uq_p, kv_lora_g, w_ukv, qng_p, kng_p, cos, slo, shi,
                u, dq, dk, dv, dga, dgp, dpl, g1):
    seq = x.shape[0]
    n_tiles = seq // TILE + 1
    rows8 = TILE // HALO
    last8 = seq // HALO - 1

    def body(*refs):
        du_even, du_odd, dh_even, dh_odd = refs[-4:]
        gwin_ref, gwuq_ref, gwukv_ref, dqlg_ref, dkvlg_ref, dqng_ref, dkng_ref, dng_ref, dmod_ref = refs[-13:-4]
        s = pl.program_id(0)

        @pl.when(s == 0)
        def _():
            for ref in (gwin_ref, gwuq_ref, gwukv_ref, dqlg_ref, dkvlg_ref, dqng_ref, dkng_ref, dng_ref, dmod_ref,
                        du_odd, dh_even):
                ref[...] = jnp.zeros_like(ref)

        @pl.when(lax.rem(s, 2) == 0)
        def _():
            stages(s, refs[:-4], du_even, du_odd, dh_odd, dh_even)

        @pl.when(lax.rem(s, 2) == 1)
        def _():
            stages(s, refs[:-4], du_odd, du_even, dh_even, dh_odd)

    def stages(s, refs, du_ref, du_prev, dh_fill, dh_prev):
        (xb_ref, xc_ref, ctx_ref, mod_ref, bmod_ref, ng_ref, win_ref, qlg_ref, wuq_ref, kvlg_ref, wukv_ref, qng_ref, kng_ref,
         cos_ref, slo_ref, shi_ref, cq_ref, ckv_ref, kr_ref, dq_ref, dk_ref, dv_ref, dga_ref, dgp_ref, dpl_ref,
         hb_ref, ha_ref, g1_ref,
         gx_ref, gwin_ref, gwuq_ref, gwukv_ref, dqlg_ref, dkvlg_ref, dqng_ref, dkng_ref, dng_ref, dmod_ref) = refs


        i = jnp.minimum(s, n_tiles - 1)
        valid = s < n_tiles
        is_lat = (s > 0) & valid
        cq = cq_ref[...]
        rq = lax.rsqrt(jnp.mean(cq * cq, axis=-1, keepdims=True) + NORM_EPS)
        qhat = cq * rq
        qnb = (qhat * qlg_ref[...]).astype(BF16)
        q = _dot_nt(qnb, wuq_ref[...])
        ckv = ckv_ref[...]
        rkv = lax.rsqrt(jnp.mean(ckv * ckv, axis=-1, keepdims=True) + NORM_EPS)
        kvhat = ckv * rkv
        kvnb = (kvhat * kvlg_ref[...]).astype(BF16)
        kv = _dot(kvnb, wukv_ref[...])

        _, _, _, h2, _ = _modulated_input(s <= 1, xb_ref, ctx_ref, mod_ref, bmod_ref, ng_ref)
        dub = du_prev[...]
        du_g, du_m = dub[:, 0:N_GATES], dub[:, N_GATES:U_PAD]
        h2b = h2.astype(BF16)
        dh_fill[...] = _dot(du_g, win_ref[N_MLA:D_IN_PROJ, :]) + _dot(du_m, win_ref[0:U_PAD - N_GATES, :])
        gwin_ref[N_MLA:D_IN_PROJ, :] += _dot_tn(du_g, h2b)
        gwin_ref[0:N_MLA, :] += _dot_tn(du_m, h2b)[0:N_MLA]

        ctx3 = s <= 2
        xt, r, xg, _, scale = _modulated_input(ctx3, xc_ref, ctx_ref, mod_ref, bmod_ref, ng_ref)
        dh = dh_prev[...]
        dshift = jnp.sum(dh, axis=0, keepdims=True)
        dscale = jnp.sum(dh * xg, axis=0, keepdims=True)
        zero = jnp.zeros_like(dshift)
        dmod_ref[0:1, :] += jnp.where(ctx3, zero, dshift)
        dmod_ref[1:2, :] += jnp.where(ctx3, zero, dscale)
        dmod_ref[2:3, :] += jnp.where(ctx3, dshift, zero)
        dmod_ref[3:4, :] += jnp.where(ctx3, dscale, zero)
        dxg = dh * (1.0 + scale)
        xr = xt * r
        dng_ref[...] += jnp.sum(dxg * xr, axis=0, keepdims=True)
        dxn = dxg * ng_ref[...]
        gx_ref[...] = g1_ref[...] + r * (dxn - xr * jnp.mean(dxn * xr, axis=-1, keepdims=True))

        cos_t, slo_t, shi_t = cos_ref[...], slo_ref[...], shi_ref[...]
        qg = qng_ref[...]
        dq_heads = []
        dqng = jnp.zeros((1, D_HEAD_PAD), F32)
        for hd in range(N_HEADS):
            qh = q[:, hd * D_HEAD_PAD:(hd + 1) * D_HEAD_PAD]
            rr = lax.rsqrt(jnp.sum(qh * qh, axis=-1, keepdims=True) * (1.0 / D_HEAD) + NORM_EPS)
            yq = qh * rr
            dpost = jnp.where(is_lat, dq_ref[hd] * ATTN_SCALE, 0.0)
            dyn = jnp.concatenate([dpost[:, 0:D_NOPE], _rope_t(dpost[:, D_NOPE:], cos_t, slo_t, shi_t)], axis=1)
            dqng = dqng + jnp.sum(dyn * yq, axis=0, keepdims=True)
            dyg = dyn * qg
            dq_heads.append(rr * (dyg - yq * (jnp.sum(dyg * yq, axis=-1, keepdims=True) * (1.0 / D_HEAD))))
        dqng_ref[...] += dqng
        dqf = jnp.concatenate(dq_heads, axis=1).astype(BF16)

        krz = kr_ref[...]
        kr_ss = jnp.sum(krz * krz, axis=-1, keepdims=True)
        kg = kng_ref[...]
        kg_n, kg_r = kg[:, 0:D_NOPE], kg[:, D_NOPE:]
        dkv_parts = []
        dkr = jnp.zeros((TILE, 128), F32)
        dkng_n = jnp.zeros((1, D_NOPE), F32)
        dkng_r = jnp.zeros((1, 128), F32)
        for hd in range(N_HEADS):
            kn = kv[:, hd * 256:hd * 256 + D_NOPE]
            rr = lax.rsqrt((jnp.sum(kn * kn, axis=-1, keepdims=True) + kr_ss) * (1.0 / D_HEAD) + NORM_EPS)
            yn, yr = kn * rr, krz * rr
            dk_h = jnp.where(valid, jnp.transpose(dk_ref[hd]) * LN_2, 0.0)
            dpn = dk_h[:, 0:D_NOPE]
            dpr = _rope_t(dk_h[:, D_NOPE:], cos_t, slo_t, shi_t)
            dkng_n = dkng_n + jnp.sum(dpn * yn, axis=0, keepdims=True)
            dkng_r = dkng_r + jnp.sum(dpr * yr, axis=0, keepdims=True)
            dyn, dyr = dpn * kg_n, dpr * kg_r
            proj = (jnp.sum(dyn * yn, axis=-1, keepdims=True) + jnp.sum(dyr * yr, axis=-1, keepdims=True)) * (1.0 / D_HEAD)
            dkv_parts.append(rr * (dyn - yn * proj))
            dkv_parts.append(jnp.where(valid, jnp.transpose(dv_ref[hd]), 0.0))
            dkr = dkr + rr * (dyr - yr * proj)
        dkng_ref[:, 0:D_NOPE] += dkng_n
        dkng_ref[:, D_NOPE:] += dkng_r
        dkvf = jnp.concatenate(dkv_parts, axis=1).astype(BF16)

        lat_t = jnp.maximum(i - 1, 0)
        dpl_t = dpl_ref[...].astype(F32)
        ds = jnp.concatenate([jnp.where(i > 1, hb_ref[...].astype(F32), 0.0), dpl_t,
                              jnp.where(i < n_tiles - 1, ha_ref[...].astype(F32), 0.0)], axis=0)
        tpos = lat_t * TILE - HALO + lax.broadcasted_iota(jnp.int32, (TILE + 2 * HALO, 1), 0)
        for g, w in enumerate(POOL_WINDOWS):
            sl = slice(g * GROUP_DIM, (g + 1) * GROUP_DIM)
            wsum = _window_sum(ds[:, sl] * _inv_count(tpos, w, seq), w, True)[HALO:HALO + TILE]
            du_ref[:, O_PIN + g * GROUP_DIM:O_PIN + (g + 1) * GROUP_DIM] = jnp.where(
                is_lat, wsum - dpl_t[:, sl], 0.0).astype(BF16)

        du_ref[:, O_GA:O_GA + D_ATTN] = jnp.where(is_lat, dga_ref[...], jnp.zeros((), BF16))
        du_ref[:, O_GP:O_GP + D_POOL] = jnp.where(is_lat, dgp_ref[...], jnp.zeros((), BF16))
        du_ref[:, O_KR:U_PAD] = dkr.astype(BF16)

        gwuq_ref[...] += _dot_tn(dqf, qnb)
        dqn = _dot(dqf, wuq_ref[...])
        gwukv_ref[...] += _dot_tn(dkvf, kvnb)
        dkvn = _dot_nt(dkvf, wukv_ref[...])
        dqlg_ref[...] += jnp.sum(dqn * qhat, axis=0, keepdims=True)
        dqhat = dqn * qlg_ref[...]
        dcq = rq * (dqhat - qhat * jnp.mean(dqhat * qhat, axis=-1, keepdims=True))
        dkvlg_ref[...] += jnp.sum(dkvn * kvhat, axis=0, keepdims=True)
        dkvhat = dkvn * kvlg_ref[...]
        dckv = rkv * (dkvhat - kvhat * jnp.mean(dkvhat * kvhat, axis=-1, keepdims=True))
        du_ref[:, O_CQ:O_CQ + R_Q] = dcq.astype(BF16)
        du_ref[:, O_CKV:O_CKV + R_KV] = dckv.astype(BF16)

    t1 = lambda s: jnp.minimum(s, n_tiles - 1)
    t2 = lambda s: jnp.clip(s - 1, 0, n_tiles - 1)
    t3 = lambda s: jnp.clip(s - 2, 0, n_tiles - 1)
    latent = lambda t: jnp.maximum(t - 1, 0)
    lat = lambda s: (latent(t1(s)), 0)
    lat3 = lambda s: (latent(t3(s)), 0)
    full = lambda shape: pl.BlockSpec(shape, lambda s: (0,) * len(shape))
    rope_spec = pl.BlockSpec((TILE, 128), lambda s: (t1(s), 0))
    return pl.pallas_call(
        body, name="inproj_bwd", grid=(n_tiles + 2,),
        out_shape=(jax.ShapeDtypeStruct((seq, D_MODEL), F32), jax.ShapeDtypeStruct((D_IN_PROJ, D_MODEL), F32),
                   jax.ShapeDtypeStruct((N_HEADS * D_HEAD_PAD, R_Q), F32), jax.ShapeDtypeStruct((N_HEADS * 256, R_KV), F32),
                   jax.ShapeDtypeStruct((1, R_Q), F32), jax.ShapeDtypeStruct((1, R_KV), F32),
                   jax.ShapeDtypeStruct((1, D_HEAD_PAD), F32), jax.ShapeDtypeStruct((1, D_HEAD_PAD), F32),
                   jax.ShapeDtypeStruct((1, D_MODEL), F32), jax.ShapeDtypeStruct((8, D_MODEL), F32)),
        in_specs=[pl.BlockSpec((TILE, D_MODEL), lambda s: (latent(t2(s)), 0)), pl.BlockSpec((TILE, D_MODEL), lat3),
                  full((TILE, D_MODEL)), full((8, D_MODEL)), full((8, D_MODEL)),
                  full((1, D_MODEL)), full((D_IN_PROJ, D_MODEL)), full((1, R_Q)), full((N_HEADS * D_HEAD_PAD, R_Q)),
                  full((1, R_KV)), full((R_KV, N_HEADS * 256)), full((1, D_HEAD_PAD)), full((1, D_HEAD_PAD)),
                  rope_spec, rope_spec, rope_spec,
                  pl.BlockSpec((TILE, R_Q), lambda s: (t1(s), (O_CQ - N_GATES) // R_Q)),
                  pl.BlockSpec((TILE, R_KV), lambda s: (t1(s), (O_CKV - N_GATES) // R_KV)),
                  pl.BlockSpec((TILE, 128), lambda s: (t1(s), (O_KR - N_GATES) // 128)),
                  pl.BlockSpec((N_HEADS, TILE, D_HEAD_PAD), lambda s: (0, latent(t1(s)), 0)),
                  pl.BlockSpec((N_HEADS, D_HEAD_PAD, TILE), lambda s: (0, 0, t1(s))),
                  pl.BlockSpec((N_HEADS, D_V, TILE), lambda s: (0, 0, t1(s))),
                  pl.BlockSpec((TILE, D_ATTN), lat), pl.BlockSpec((TILE, D_POOL), lat), pl.BlockSpec((TILE, D_POOL), lat),
                  pl.BlockSpec((HALO, D_POOL), lambda s: (jnp.maximum((t1(s) - 1) * rows8 - 1, 0), 0)),
                  pl.BlockSpec((HALO, D_POOL), lambda s: (jnp.minimum(jnp.maximum(t1(s), 1) * rows8, last8), 0)),
                  pl.BlockSpec((TILE, D_MODEL), lat3)],
        out_specs=(pl.BlockSpec((TILE, D_MODEL), lat3), full((D_IN_PROJ, D_MODEL)), full((N_HEADS * D_HEAD_PAD, R_Q)),
                   full((N_HEADS * 256, R_KV)), full((1, R_Q)), full((1, R_KV)), full((1, D_HEAD_PAD)),
                   full((1, D_HEAD_PAD)), full((1, D_MODEL)), full((8, D_MODEL))),
        scratch_shapes=[pltpu.VMEM((TILE, U_PAD), BF16), pltpu.VMEM((TILE, U_PAD), BF16),
                        pltpu.VMEM((TILE, D_MODEL), F32), pltpu.VMEM((TILE, D_MODEL), F32)],
        compiler_params=pltpu.CompilerParams(dimension_semantics=("arbitrary",), vmem_limit_bytes=VMEM_LIMIT),
    )(x, x, ctx, modrows, bmodrows, norm_g, w_in_p, q_lora_g, w_uq_p, kv_lora_g, w_ukv, qng_p, kng_p, cos, slo, shi,
      u, u, u, dq, dk, dv, dga, dgp, dpl, dpl, dpl, g1)


SMALL_LAYOUT = ((0, D_MODEL), (1, R_Q), (2, R_KV), (3, D_HEAD_PAD), (4, D_HEAD_PAD), (5, D_POOL))
ROW_DMOD_B, ROW_DMOD_C, ROW_LOSS = 6, 9, 12


def _epilogue(parts, landed, smalls, dmod4, dgate, loss_acc, w_mod_l):
    n_a, n_l, n_s = len(parts), len(landed), len(smalls)
    n_mod = w_mod_l.shape[1]
    blk = [p.shape[1:] for p in parts]
    small_out = [D_MODEL, R_Q, R_KV, D_HEAD, D_HEAD, D_POOL]

    def body(*refs):
        part_refs = refs[0:n_a]
        land_refs = refs[n_a:n_a + n_l]
        small_in = refs[n_a + n_l:n_a + n_l + n_s]
        dmod4_ref, dgate_ref, loss_ref, wmod_ref = refs[n_a + n_l + n_s:n_a + n_l + n_s + 4]
        outs = refs[n_a + n_l + n_s + 4:]
        red_refs = outs[0:n_a]
        landsum_refs = outs[n_a:n_a + n_l]
        ccg_ref = outs[n_a + n_l]
        sum_refs = outs[n_a + n_l + 1:n_a + n_l + 1 + n_s]
        gbmod_ref, dmy_ref, lossout_ref = outs[n_a + n_l + 1 + n_s:n_a + n_l + 4 + n_s]
        scr = outs[n_a + n_l + 4 + n_s:]
        recv1, own1, sum1b, recv2 = scr[0:n_a], scr[n_a:2 * n_a], scr[2 * n_a:3 * n_a], scr[3 * n_a:4 * n_a]
        small_ref, smallg_ref, tot_ref, cc_ref = scr[4 * n_a:4 * n_a + 4]
        ssem1, rsem1, lsem, ssem2, rsem2, ssem_s, rsem_s, ssem_cc, rsem_cc = scr[4 * n_a + 4:]
        x, y, c = _coords()
        me = (x, y, c)
        sibling = (x, y, 1 - c)

        small_ref[...] = jnp.zeros_like(small_ref)
        for ref, (row, width) in zip(small_in, SMALL_LAYOUT):
            small_ref[row:row + 1, 0:width] = ref[...]
        small_ref[ROW_DMOD_B:ROW_DMOD_B + 2, :] = dmod4_ref[0:2, :]
        small_ref[ROW_DMOD_B + 2:ROW_DMOD_B + 3, :] = dgate_ref[...]
        small_ref[ROW_DMOD_C:ROW_DMOD_C + 2, :] = dmod4_ref[2:4, :]
        small_ref[ROW_LOSS:ROW_LOSS + 1, 0:128] = loss_ref[0:1, :]
        smallg_ref[_lin(me)] = small_ref[...]
        s_recv, s_send = _gather_to_all(small_ref, smallg_ref, ssem_s, rsem_s)

        stage1, own_copies = [], []
        for a in range(n_a):
            for b in range(4):
                dev = 4 * (b >> 1) + 2 * (b & 1)
                stage1.append(pltpu.make_async_remote_copy(
                    src_ref=part_refs[a].at[dev + (1 - c)], dst_ref=recv1[a].at[b],
                    send_sem=ssem1.at[a, b], recv_sem=rsem1.at[a, b], device_id=sibling, device_id_type=MESH))
                own_copies.append(pltpu.make_async_copy(part_refs[a].at[dev + c], own1[a].at[b], lsem.at[a, b]))
                stage1[-1].start()
                own_copies[-1].start()

        s_recv()
        tot = smallg_ref[0]
        for d in range(1, N_DEV):
            tot = tot + smallg_ref[d]
        tot_ref[...] = tot
        for ref, (row, _), width in zip(sum_refs, SMALL_LAYOUT, small_out):
            ref[...] = tot_ref[row:row + 1, 0:width]
        lossout_ref[...] = tot_ref[8:16, 0:128]
        lin = _lin(me)

        def my_columns(three_rows):
            v = jnp.concatenate(three_rows, axis=1)
            out = jnp.zeros((1, n_mod), F32)
            for d in range(N_DEV):
                out = out + jnp.where(lin == d, v[:, d * n_mod:(d + 1) * n_mod], 0.0)
            return out

        rows3 = lambda ref, r0: [ref[r0 + t:r0 + t + 1, :] for t in range(3)]
        dmodc = rows3(tot_ref, ROW_DMOD_C)
        gbmod_ref[...] = jnp.concatenate(rows3(tot_ref, ROW_DMOD_B), axis=1) + jnp.concatenate(dmodc, axis=1)
        dmy_ref[...] = jnp.zeros_like(dmy_ref)
        for b in range(N_DEV):
            dmy_ref[b:b + 1, :] = my_columns(rows3(smallg_ref.at[b], ROW_DMOD_B))
        dmy = my_columns(dmodc)
        dmy_ref[N_DEV:N_DEV + 1, :] = dmy
        part = lax.dot_general(jnp.broadcast_to(dmy, (8, n_mod)), wmod_ref[...], (((1,), (1,)), ((), ())),
                               precision=HIGHEST, preferred_element_type=F32)

        cc_ref[...] = part
        ccg_ref[_lin(me)] = part
        cc_recv, cc_send = _gather_to_all(cc_ref, ccg_ref, ssem_cc, rsem_cc)

        stage2 = []
        for a in range(n_a):
            for b in range(4):
                stage1[4 * a + b].wait_recv()
                own_copies[4 * a + b].wait()
                sum1b[a][b] = (own1[a][b] + recv1[a][b]).astype(BF16)
            for k in (1, 2, 3):
                tx, ty = _flip(x, (k >> 1) & 1), _flip(y, k & 1)
                stage2.append(pltpu.make_async_remote_copy(
                    src_ref=sum1b[a].at[2 * tx + ty], dst_ref=recv2[a].at[k - 1], send_sem=ssem2.at[a, k - 1],
                    recv_sem=rsem2.at[a, k - 1], device_id=(tx, ty, c), device_id_type=MESH))
                stage2[-1].start()
        for a in range(n_a):
            own = own1[a][2 * x + y] + recv1[a][2 * x + y]
            for k in (1, 2, 3):
                stage2[3 * a + k - 1].wait_recv()
                own = own + recv2[a][k - 1].astype(F32)
            red_refs[a][...] = own

        for ref, out in zip(land_refs, landsum_refs):
            acc = ref[0].astype(F32)
            for d in range(1, N_DEV):
                acc = acc + ref[d].astype(F32)
            out[...] = acc
        cc_recv()
        for cp in stage1 + stage2:
            cp.wait_send()
        s_send()
        cc_send()

    vm = pl.BlockSpec(memory_space=pltpu.VMEM)
    sds = jax.ShapeDtypeStruct
    return pl.pallas_call(
        body, name="epilogue_reduce",
        out_shape=(*[sds(s, F32) for s in blk], *[sds(a.shape[1:], F32) for a in landed], sds((N_DEV, 8, D_MODEL), F32),
                   *[sds((1, w), F32) for w in small_out], sds((1, 3 * D_MODEL), F32), sds((16, n_mod), F32),
                   sds((8, 128), F32)),
        in_specs=[pl.BlockSpec(memory_space=pl.ANY)] * n_a + [vm] * (n_l + n_s + 4),
        out_specs=tuple([vm] * (n_a + n_l + n_s + 4)),
        scratch_shapes=[*[pltpu.VMEM((4,) + s, F32) for s in blk], *[pltpu.VMEM((4,) + s, F32) for s in blk],
                        *[pltpu.VMEM((4,) + s, BF16) for s in blk], *[pltpu.VMEM((3,) + s, BF16) for s in blk],
                        pltpu.VMEM((SMALL_ROWS, D_MODEL), F32), pltpu.VMEM((N_DEV, SMALL_ROWS, D_MODEL), F32),
                        pltpu.VMEM((SMALL_ROWS, D_MODEL), F32), pltpu.VMEM((8, D_MODEL), F32),
                        pltpu.SemaphoreType.DMA((n_a, 4)), pltpu.SemaphoreType.DMA((n_a, 4)), pltpu.SemaphoreType.DMA((n_a, 4)),
                        pltpu.SemaphoreType.DMA((n_a, 3)), pltpu.SemaphoreType.DMA((n_a, 3)),
                        pltpu.SemaphoreType.DMA((7,)), pltpu.SemaphoreType.DMA((7,)),
                        pltpu.SemaphoreType.DMA((7,)), pltpu.SemaphoreType.DMA((7,))],
        compiler_params=pltpu.CompilerParams(vmem_limit_bytes=VMEM_LIMIT),
    )(*parts, *landed, *smalls, dmod4, dgate, loss_acc, w_mod_l)


def _adamw(weights, grads, ms, vs, c_all_t, dmod_my, p2g):
    n = len(weights)

    def body(*refs):
        w_refs = refs[0:n]
        g_in = refs[n:2 * n - 2]
        m_refs = refs[2 * n - 2:3 * n - 2]
        v_refs = refs[3 * n - 2:4 * n - 2]
        cat_ref, dmod_ref, p2g_ref = refs[4 * n - 2:4 * n + 1]
        outs = refs[4 * n + 1:]
        g_refs, d_refs, nm_refs, nv_refs = outs[0:n], outs[n:2 * n], outs[2 * n:3 * n], outs[3 * n:4 * n]
        gcc_ref, gwm_ref = g_refs[0], g_refs[1]

        part = p2g_ref[0, 0:1, :]
        for d in range(1, N_DEV):
            part = part + p2g_ref[d, 0:1, :]
        cc = w_refs[0][...]
        sg = _sigmoid(cc)
        gcc_ref[...] = part * (sg * (1.0 + cc * (1.0 - sg)))
        cat = cat_ref[...]
        gwm_ref[...] = lax.dot_general(cat * _sigmoid(cat), dmod_ref[...], (((1,), (0,)), ((), ())), precision=HIGHEST,
                                       preferred_element_type=F32)
        for idx in range(n):
            if idx >= 2:
                g_refs[idx][...] = g_in[idx - 2][...]
            g = g_refs[idx][...]
            m = ADAM_B1 * m_refs[idx][...] + (1.0 - ADAM_B1) * g
            v = ADAM_B2 * v_refs[idx][...] + (1.0 - ADAM_B2) * (g * g)
            m_hat = m / (1.0 - ADAM_B1 ** ADAM_STEP)
            v_hat = v / (1.0 - ADAM_B2 ** ADAM_STEP)
            d_refs[idx][...] = -ADAM_LR * (m_hat / (jnp.sqrt(v_hat) + ADAM_EPS) + ADAM_WD * w_refs[idx][...])
            nm_refs[idx][...] = m
            nv_refs[idx][...] = v

    vm = pl.BlockSpec(memory_space=pltpu.VMEM)
    shapes = [jax.ShapeDtypeStruct(w.shape, F32) for w in weights]
    args = list(weights) + list(grads[2:]) + list(ms) + list(vs) + [c_all_t, dmod_my, p2g]
    out_shape = tuple(shapes * 4)
    return pl.pallas_call(
        body, name="adamw_update", out_shape=out_shape, in_specs=[vm] * len(args), out_specs=tuple([vm] * len(out_shape)),
        compiler_params=pltpu.CompilerParams(vmem_limit_bytes=VMEM_LIMIT),
    )(*args)


def _rope_tables(seq):
    rows = seq // GRID_W
    row = np.repeat(np.arange(rows, dtype=np.float32), GRID_W)
    col = np.tile(np.arange(GRID_W, dtype=np.float32), rows)
    n_freq = D_ROPE // 4
    inv = (np.float32(ROPE_BASE) ** (-np.arange(n_freq, dtype=np.float32) / np.float32(n_freq))).astype(np.float32)
    ang_r = (row[:, None] * inv).astype(np.float32)
    ang_c = (col[:, None] * inv).astype(np.float32)
    ang = np.concatenate([ang_r, ang_r, ang_c, ang_c], axis=-1)
    cos, sin = np.cos(ang).astype(np.float32), np.sin(ang).astype(np.float32)
    low = (np.arange(D_ROPE) % 32) < 16

    def table(t, fill):
        out = np.full((TILE + seq, 128), fill, np.float32)
        out[TILE:, 0:D_ROPE] = t
        return jnp.asarray(out)

    return table(cos, 1.0), table(np.where(low, -sin, 0.0), 0.0), table(np.where(low, 0.0, sin), 0.0)


def kernel(x, c, ctx, c_ctx, w_mod, b_mod, norm_g, w_in, q_lora_g, w_uq, kv_lora_g, w_ukv, q_norm_g, k_norm_g, w_pool, pool_scale, w_out, loss_target, m_c_ctx, m_w_mod, m_b_mod, m_norm_g, m_w_in, m_q_lora_g, m_w_uq, m_kv_lora_g, m_w_ukv, m_q_norm_g, m_k_norm_g, m_w_pool, m_pool_scale, m_w_out, v_c_ctx, v_w_mod, v_b_mod, v_norm_g, v_w_in, v_q_lora_g, v_w_uq, v_kv_lora_g, v_w_ukv, v_q_norm_g, v_k_norm_g, v_w_pool, v_pool_scale, v_w_out):
    seq = x.shape[1]
    assert ctx.shape[1] == TILE and seq % TILE == 0 and seq % GRID_W == 0
    ix, iy, ic = lax.axis_index("x"), lax.axis_index("y"), lax.axis_index("c")
    me = 4 * ix + 2 * iy + ic
    x2, ctx2, tgt2 = x[0], ctx[0], loss_target[0]
    w_mod_l, w_ukv_l, w_out_l = w_mod[0], w_ukv[0], w_out[0]
    c_ctx_row = c_ctx.reshape(1, D_MODEL)

    tr = lambda a: jnp.transpose(a[0])
    w_in_tl, w_uq_tl = tr(w_in), tr(w_uq)
    cg, modg, wg_in, wg_uq, wg_ukv = _prologue(
        c, c_ctx_row, w_mod_l,
        [w_in_tl, w_uq_tl, w_ukv_l])
    mod_all = jnp.transpose(modg, (1, 0, 2)).reshape(16, 3 * D_MODEL)
    mod_b = lax.dynamic_slice_in_dim(mod_all, me, 1, axis=0).reshape(3, D_MODEL)
    mod_c = mod_all[8].reshape(3, D_MODEL)
    modrows = jnp.concatenate([mod_b, mod_c[0:2], jnp.zeros((3, D_MODEL), F32)], axis=0)
    b3 = b_mod.reshape(3, D_MODEL)
    bmodrows = jnp.concatenate([b3, b3[0:2], jnp.zeros((3, D_MODEL), F32)], axis=0)

    w_in_p = wg_in.reshape(D_IN_PROJ, D_MODEL)
    w_uq_p = jnp.concatenate([wg_uq.reshape(N_HEADS, D_HEAD, R_Q), jnp.zeros((N_HEADS, D_HEAD_PAD - D_HEAD, R_Q), BF16)],
                             axis=1).reshape(N_HEADS * D_HEAD_PAD, R_Q)
    w_ukv_f = jnp.transpose(wg_ukv, (1, 0, 2)).reshape(R_KV, N_HEADS * 256)
    qng_p = jnp.concatenate([q_norm_g, jnp.zeros((1, D_HEAD_PAD - D_HEAD), F32)], axis=1)
    kng_p = jnp.concatenate([k_norm_g, jnp.zeros((1, D_HEAD_PAD - D_HEAD), F32)], axis=1)
    cos, slo, shi = _rope_tables(seq)

    u_gates, u_mla, q, k, v = _inproj_fwd(x2, ctx2, modrows, bmodrows, norm_g, w_in_p, q_lora_g, w_uq_p, kv_lora_g, w_ukv_f,
                             qng_p, kng_p, cos, slo, shi)
    attn, lse, wg_out = _attn_fwd(q, k, v, min(2048, seq), w_out_l.astype(BF16))
    (loss_acc, g1, dgate, d_attn, dga, dgp, dpl, gwo_b, gwp, dps) = _mix(
        x2, tgt2, attn, u_gates, modrows, bmodrows, w_pool[0], pool_scale, wg_out.reshape(D_MODEL, D_MODEL))

    blocks = lambda a: a.reshape((N_DEV, a.shape[0] // N_DEV) + a.shape[1:])
    dq, dk, dv, land_wo, land_wp = _attn_bwd(q, k, v, attn, lse, d_attn.reshape(seq, D_ATTN), min(1024, seq), blocks(gwo_b),
                                             gwp.reshape(4 * GROUP_DIM, GROUP_DIM))
    (grad_x, gwin_t, gwuq_p, gwukv_t, dqlg, dkvlg, dqng, dkng, dng, dmod4) = _inproj_bwd(
        x2, ctx2, modrows, bmodrows, norm_g, w_in_p, q_lora_g, w_uq_p, kv_lora_g, w_ukv_f, qng_p, kng_p, cos, slo, shi,
        u_mla, dq, dk, dv, dga, dgp, dpl, g1)

    gwuq_t = gwuq_p.reshape(N_HEADS, D_HEAD_PAD, R_Q)[:, 0:D_HEAD].reshape(N_HEADS * D_HEAD, R_Q)
    parts = [blocks(gwin_t), blocks(gwuq_t), blocks(gwukv_t)]
    (r_win, r_wuq, r_wukv, r_wout, g_w_pool, ccg, g_ng, g_qlg, g_kvlg, g_qng, g_kng, g_ps, g_bmod, dmod_my,
     loss_rows) = _epilogue(parts, [land_wo, land_wp], [dng, dqlg, dkvlg, dqng, dkng, dps], dmod4, dgate, loss_acc, w_mod_l)

    g_w_ukv = jnp.transpose(r_wukv)
    c_rows = cg[:, 0, :]
    c_all_t = jnp.transpose(jnp.concatenate([c_rows, c_ctx_row, jnp.zeros((16 - N_DEV - 1, D_MODEL), F32)], axis=0))

    weights = [c_ctx_row, w_mod_l, b_mod, norm_g, w_in_tl, q_lora_g, w_uq_tl, kv_lora_g, w_ukv_l, q_norm_g, k_norm_g,
               w_pool.reshape(4 * GROUP_DIM, GROUP_DIM), pool_scale, w_out_l]
    grads = [None, None, g_bmod, g_ng, r_win, g_qlg, r_wuq, g_kvlg, g_w_ukv, g_qng, g_kng, g_w_pool, g_ps, r_wout]
    ms = [m_c_ctx.reshape(1, D_MODEL), m_w_mod[0], m_b_mod, m_norm_g, tr(m_w_in), m_q_lora_g, tr(m_w_uq), m_kv_lora_g,
          m_w_ukv[0], m_q_norm_g, m_k_norm_g, m_w_pool.reshape(4 * GROUP_DIM, GROUP_DIM), m_pool_scale, m_w_out[0]]
    vs = [v_c_ctx.reshape(1, D_MODEL), v_w_mod[0], v_b_mod, v_norm_g, tr(v_w_in), v_q_lora_g, tr(v_w_uq), v_kv_lora_g,
          v_w_ukv[0], v_q_norm_g, v_k_norm_g, v_w_pool.reshape(4 * GROUP_DIM, GROUP_DIM), v_pool_scale, v_w_out[0]]
    outs = _adamw(weights, grads, ms, vs, c_all_t, dmod_my, ccg)
    n = len(weights)
    grads, deltas, new_m, new_v = outs[0:n], outs[n:2 * n], outs[2 * n:3 * n], outs[3 * n:4 * n]

    loss = loss_rows[ROW_LOSS - 8, 0]
    final = [c_ctx.shape, w_mod.shape, b_mod.shape, norm_g.shape, w_in.shape, q_lora_g.shape, w_uq.shape, kv_lora_g.shape,
             w_ukv.shape, q_norm_g.shape, k_norm_g.shape, w_pool.shape, pool_scale.shape, w_out.shape]
    transposed = (4, 6)

    def shaped(arrs):
        return [(jnp.transpose(a) if i in transposed else a).reshape(s) for i, (a, s) in enumerate(zip(arrs, final))]

    return (loss, grad_x[None], *shaped(grads), *shaped(deltas), *shaped(new_m), *shaped(new_v))
```

```python
import numpy as np

import jax
import jax.numpy as jnp
from jax import lax
from jax.experimental import pallas as pl
from jax.experimental.pallas import tpu as pltpu

F32 = jnp.float32
BF16 = jnp.bfloat16
MESH = pl.DeviceIdType.MESH
HIGHEST = lax.Precision.HIGHEST

D_MODEL = 1024
N_HEADS = 4
D_NOPE = 128
D_ROPE = 64
D_HEAD = D_NOPE + D_ROPE
D_HEAD_PAD = 256
D_V = 128
R_Q = 256
R_KV = 128
D_ATTN = 512
D_POOL = 512
POOL_WINDOWS = (2, 4, 8, 16)
GROUP_DIM = 128
GRID_W = 64
ROPE_BASE = 10000.0
NORM_EPS = 1e-6
ATTN_SCALE = D_HEAD ** -0.5
LOG2_E = 1.4426950408889634
LN_2 = 0.6931471805599453
ATTN_ROWS = 256
D_IN_PROJ = 1984
U_PAD = 2048
O_GA, O_PIN, O_GP, O_CQ, O_CKV, O_KR = 0, 512, 1024, 1536, 1792, 1920
TILE = 256
MIX_TILE = 512
Q_BLOCK = 128
HALO = 16
N_GATES = 1536
N_MLA = D_IN_PROJ - N_GATES
N_DEV = 8
VMEM_LIMIT = 56 * 1024 * 1024

ADAM_LR = 0.001
ADAM_B1 = 0.9
ADAM_B2 = 0.999
ADAM_EPS = 1e-08
ADAM_WD = 0.01
ADAM_STEP = 10

SMALL_ROWS = 16


def _dot(a, b):
    return lax.dot_general(a, b, (((1,), (0,)), ((), ())), preferred_element_type=F32)


def _dot_nt(a, b):
    return lax.dot_general(a, b, (((1,), (1,)), ((), ())), preferred_element_type=F32)


def _dot_tn(a, b):
    return lax.dot_general(a, b, (((0,), (0,)), ((), ())), preferred_element_type=F32)


def _sigmoid(x):
    return 1.0 / (1.0 + jnp.exp(-x))


def _rope(p, cos, slo, shi):
    return p * cos + pltpu.roll(p, 112, 1) * slo + pltpu.roll(p, 16, 1) * shi


def _rope_t(d, cos, slo, shi):
    return d * cos + pltpu.roll(d * slo, 16, 1) + pltpu.roll(d * shi, 112, 1)


def _shift_rows(a, k):
    n = a.shape[0]
    return pltpu.roll(a, (-k) % n, 0)


def _window_sum(x, w, transposed):
    s = (x + _shift_rows(x, 1)) if transposed else (_shift_rows(x, -1) + x)
    step = 1
    while 2 * step < w:
        s = _shift_rows(s, -step) + _shift_rows(s, step)
        step *= 2
    return s


def _inv_count(tpos, w, seq):
    lo = jnp.maximum(tpos - w // 2, 0)
    hi = jnp.minimum(tpos - w // 2 + w, seq)
    return 1.0 / jnp.maximum(hi - lo, 1).astype(F32)


def _coords():
    return lax.axis_index("x"), lax.axis_index("y"), lax.axis_index("c")


def _flip(v, bit):
    return (1 - v) if bit else v


def _peer(k):
    x, y, c = _coords()
    return (_flip(x, (k >> 2) & 1), _flip(y, (k >> 1) & 1), _flip(c, k & 1))


def _lin(p):
    return 4 * p[0] + 2 * p[1] + p[2]


def _gather_to_all(src_ref, slots_ref, send_sems, recv_sems):
    me = _coords()
    copies = []
    for k in range(1, N_DEV):
        cp = pltpu.make_async_remote_copy(
            src_ref=src_ref, dst_ref=slots_ref.at[_lin(me)], send_sem=send_sems.at[k - 1], recv_sem=recv_sems.at[k - 1],
            device_id=_peer(k), device_id_type=MESH)
        cp.start()
        copies.append(cp)

    def wait_recv():
        for k in range(1, N_DEV):
            pltpu.make_async_remote_copy(
                src_ref=src_ref, dst_ref=slots_ref.at[_lin(_peer(k))], send_sem=send_sems.at[k - 1],
                recv_sem=recv_sems.at[k - 1], device_id=_peer(k), device_id_type=MESH).wait_recv()

    def wait_send():
        for cp in copies:
            cp.wait_send()

    return wait_recv, wait_send


def _prologue(c8, cctx8, w_mod_l, shards):
    n_mod = w_mod_l.shape[1]
    n_w = len(shards)

    def body(c_ref, cctx_ref, wmod_ref, *refs):
        w_refs = refs[0:n_w]
        cg_ref, modg_ref = refs[n_w], refs[n_w + 1]
        wg_refs = refs[n_w + 2:2 * n_w + 2]
        modblk_ref = refs[2 * n_w + 2]
        wb_refs = refs[2 * n_w + 3:3 * n_w + 3]
        c8_ref = refs[3 * n_w + 3]
        ssem_w, rsem_w, ssem_c, rsem_c, ssem_m, rsem_m = refs[3 * n_w + 4:]
        x, y, c = _coords()
        me = (x, y, c)
        sibling = (x, y, 1 - c)
        chips = [(1 - x, y), (x, 1 - y), (1 - x, 1 - y)]

        def wcopies(k, block, to, own=False):
            out = []
            for a in range(n_w):
                slot = wg_refs[a].at[_lin(block)]
                out.append(pltpu.make_async_remote_copy(
                    src_ref=wb_refs[a] if own else slot, dst_ref=slot, send_sem=ssem_w.at[a, k], recv_sem=rsem_w.at[a, k],
                    device_id=to, device_id_type=MESH))
            return out

        c8_ref[...] = jnp.broadcast_to(c_ref[...], (8, D_MODEL))
        cg_ref[_lin(me)] = c8_ref[...]
        c_recv, c_send = _gather_to_all(c8_ref, cg_ref, ssem_c, rsem_c)

        for a in range(n_w):
            wb_refs[a][...] = w_refs[a][...].astype(BF16)
        first = wcopies(0, me, sibling, own=True)
        for j, chip in enumerate(chips):
            first += wcopies(1 + j, me, (*chip, c), own=True)
        late = [idx for idx in range(len(first)) if idx % n_w == 0 and idx >= n_w]
        for idx, cp in enumerate(first):
            if idx not in late:
                cp.start()
        for a in range(n_w):
            wg_refs[a][_lin(me)] = wb_refs[a][...]

        c_recv()
        row = lax.broadcasted_iota(jnp.int32, (8, D_MODEL), 0)
        c_all = jnp.zeros((8, D_MODEL), F32)
        for d in range(N_DEV):
            c_all = c_all + jnp.where(row == d, cg_ref[d], 0.0)
        cc = jnp.broadcast_to(cctx_ref[...], (8, D_MODEL))
        a = jnp.concatenate([c_all * _sigmoid(c_all), cc * _sigmoid(cc)], axis=0)
        modblk_ref[...] = lax.dot_general(a, wmod_ref[...], (((1,), (0,)), ((), ())), precision=HIGHEST,
                                          preferred_element_type=F32)
        modg_ref[_lin(me)] = modblk_ref[...]
        m_recv, m_send = _gather_to_all(modblk_ref, modg_ref, ssem_m, rsem_m)
        for idx in late:
            first[idx].start()
        m_recv()

        passed = []
        for j, chip in enumerate(chips):
            for cp in wcopies(1 + j, (*chip, c), me):
                cp.wait_recv()
            fwd = wcopies(4 + j, (*chip, c), sibling)
            for cp in fwd:
                cp.start()
            passed += fwd
        for cp in wcopies(0, sibling, me):
            cp.wait_recv()
        for j, chip in enumerate(chips):
            for cp in wcopies(4 + j, (*chip, 1 - c), me):
                cp.wait_recv()
        for cp in first + passed:
            cp.wait_send()
        c_send()
        m_send()

    vm = pl.BlockSpec(memory_space=pltpu.VMEM)
    return pl.pallas_call(
        body, name="prologue_gather",
        out_shape=(jax.ShapeDtypeStruct((N_DEV, 8, D_MODEL), F32), jax.ShapeDtypeStruct((N_DEV, 16, n_mod), F32),
                   *[jax.ShapeDtypeStruct((N_DEV,) + s.shape, BF16) for s in shards]),
        in_specs=[vm] * (3 + n_w), out_specs=tuple([vm] * (2 + n_w)),
        scratch_shapes=[pltpu.VMEM((16, n_mod), F32), *[pltpu.VMEM(s.shape, BF16) for s in shards],
                        pltpu.VMEM((8, D_MODEL), F32),
                        pltpu.SemaphoreType.DMA((n_w, 7)), pltpu.SemaphoreType.DMA((n_w, 7)),
                        pltpu.SemaphoreType.DMA((7,)), pltpu.SemaphoreType.DMA((7,)),
                        pltpu.SemaphoreType.DMA((7,)), pltpu.SemaphoreType.DMA((7,))],
        compiler_params=pltpu.CompilerParams(vmem_limit_bytes=VMEM_LIMIT),
    )(c8, cctx8, w_mod_l, *shards)


def _modulated_input(is_ctx, x_ref, ctx_ref, mod_ref, bmod_ref, ng_ref):
    xt = jnp.where(is_ctx, ctx_ref[...], x_ref[...])
    shift = jnp.where(is_ctx, mod_ref[3:4, :] + bmod_ref[3:4, :], mod_ref[0:1, :] + bmod_ref[0:1, :])
    scale = jnp.where(is_ctx, mod_ref[4:5, :] + bmod_ref[4:5, :], mod_ref[1:2, :] + bmod_ref[1:2, :])
    r = lax.rsqrt(jnp.mean(xt * xt, axis=-1, keepdims=True) + NORM_EPS)
    xg = (xt * r) * ng_ref[...]
    h = xg * (1.0 + scale) + shift
    return xt, r, xg, h, scale


def _inproj_fwd(x, ctx, modrows, bmodrows, norm_g, w_in_p, q_lora_g, w_uq_p, kv_lora_g, w_ukv, qng_p, kng_p, cos, slo, shi):
    seq = x.shape[0]
    n_tiles = seq // TILE + 1
    tot = seq + TILE

    n_mla = R_Q + R_KV + 128

    def body(x_ref, ctx_ref, mod_ref, bmod_ref, ng_ref, win_ref, qlg_ref, wuq_ref, kvlg_ref, wukv_ref, qng_ref, kng_ref,
             cos_ref, slo_ref, shi_ref, ug_ref, um_ref, q_ref, k_ref, v_ref, stash_even, stash_odd):
        s = pl.program_id(0)

        @pl.when(s == 0)
        def _():
            stash_odd[...] = jnp.zeros_like(stash_odd)

        refs = (x_ref, ctx_ref, mod_ref, bmod_ref, ng_ref, win_ref, qlg_ref, wuq_ref, kvlg_ref, wukv_ref, qng_ref, kng_ref,
                cos_ref, slo_ref, shi_ref, ug_ref, um_ref, q_ref, k_ref, v_ref)

        @pl.when(lax.rem(s, 2) == 0)
        def _():
            stages(s, refs, stash_even, stash_odd)

        @pl.when(lax.rem(s, 2) == 1)
        def _():
            stages(s, refs, stash_odd, stash_even)

    def stages(s, refs, fill, prev_ref):
        (x_ref, ctx_ref, mod_ref, bmod_ref, ng_ref, win_ref, qlg_ref, wuq_ref, kvlg_ref, wukv_ref, qng_ref, kng_ref,
         cos_ref, slo_ref, shi_ref, ug_ref, um_ref, q_ref, k_ref, v_ref) = refs
        cos_t, slo_t, shi_t = cos_ref[...], slo_ref[...], shi_ref[...]
        prev = prev_ref[...]
        cq = prev[:, 0:R_Q]
        qn = cq * lax.rsqrt(jnp.mean(cq * cq, axis=-1, keepdims=True) + NORM_EPS) * qlg_ref[...]
        q = _dot_nt(qn.astype(BF16), wuq_ref[...])
        qg = qng_ref[...] * (ATTN_SCALE * LOG2_E)
        for hd in range(N_HEADS):
            qh = q[:, hd * D_HEAD_PAD:(hd + 1) * D_HEAD_PAD]
            rr = lax.rsqrt(jnp.sum(qh * qh, axis=-1, keepdims=True) * (1.0 / D_HEAD) + NORM_EPS)
            qy = qh * rr * qg
            q_ref[hd, :, 0:D_NOPE] = qy[:, 0:D_NOPE].astype(BF16)
            q_ref[hd, :, D_NOPE:D_HEAD_PAD] = _rope(qy[:, D_NOPE:D_HEAD_PAD], cos_t, slo_t, shi_t).astype(BF16)

        ckv = prev[:, R_Q:R_Q + R_KV]
        kvn = ckv * lax.rsqrt(jnp.mean(ckv * ckv, axis=-1, keepdims=True) + NORM_EPS) * kvlg_ref[...]
        kv = _dot(kvn.astype(BF16), wukv_ref[...])
        krz = prev[:, R_Q + R_KV:n_mla]
        kr_ss = jnp.sum(krz * krz, axis=-1, keepdims=True)
        kg = kng_ref[...]
        for hd in range(N_HEADS):
            kn = kv[:, hd * 256:hd * 256 + D_NOPE]
            rr = lax.rsqrt((jnp.sum(kn * kn, axis=-1, keepdims=True) + kr_ss) * (1.0 / D_HEAD) + NORM_EPS)
            k_ref[hd, :, 0:D_NOPE] = (kn * rr * kg[:, 0:D_NOPE]).astype(BF16)
            k_ref[hd, :, D_NOPE:D_HEAD_PAD] = _rope(krz * rr * kg[:, D_NOPE:D_HEAD_PAD], cos_t, slo_t, shi_t).astype(BF16)
            v_ref[hd] = kv[:, hd * 256 + D_NOPE:(hd + 1) * 256].astype(BF16)

        _, _, _, h, _ = _modulated_input(s == 0, x_ref, ctx_ref, mod_ref, bmod_ref, ng_ref)
        hb = h.astype(BF16)
        ug_ref[...] = _dot_nt(hb, win_ref[N_MLA:D_IN_PROJ, :]).astype(BF16)
        um = _dot_nt(hb, win_ref[0:n_mla, :])
        lane = lax.broadcasted_iota(jnp.int32, (TILE, 128), 1)
        um = jnp.concatenate([um[:, 0:R_Q + R_KV], jnp.where(lane < D_ROPE, um[:, R_Q + R_KV:n_mla], 0.0)], axis=1)
        um_ref[...] = um
        fill[...] = um

    first = lambda s: jnp.minimum(s, n_tiles - 1)
    second = lambda s: jnp.maximum(s - 1, 0)
    latent = lambda t: jnp.maximum(t - 1, 0)
    full = lambda shape: pl.BlockSpec(shape, lambda s: (0,) * len(shape))
    rope_spec = pl.BlockSpec((TILE, 128), lambda s: (second(s), 0))
    return pl.pallas_call(
        body, name="inproj_fwd", grid=(n_tiles + 1,),
        out_shape=(jax.ShapeDtypeStruct((seq, N_GATES), BF16), jax.ShapeDtypeStruct((tot, n_mla), F32),
                   jax.ShapeDtypeStruct((N_HEADS, seq, D_HEAD_PAD), BF16),
                   jax.ShapeDtypeStruct((N_HEADS, tot, D_HEAD_PAD), BF16),
                   jax.ShapeDtypeStruct((N_HEADS, tot, D_V), BF16)),
        in_specs=[pl.BlockSpec((TILE, D_MODEL), lambda s: (latent(first(s)), 0)), full((TILE, D_MODEL)), full((8, D_MODEL)),
                  full((8, D_MODEL)), full((1, D_MODEL)), full((D_IN_PROJ, D_MODEL)), full((1, R_Q)),
                  full((N_HEADS * D_HEAD_PAD, R_Q)), full((1, R_KV)), full((R_KV, N_HEADS * 256)), full((1, D_HEAD_PAD)),
                  full((1, D_HEAD_PAD)), rope_spec, rope_spec, rope_spec],
        out_specs=(pl.BlockSpec((TILE, N_GATES), lambda s: (latent(first(s)), 0)),
                   pl.BlockSpec((TILE, n_mla), lambda s: (first(s), 0)),
                   pl.BlockSpec((N_HEADS, TILE, D_HEAD_PAD), lambda s: (0, latent(second(s)), 0)),
                   pl.BlockSpec((N_HEADS, TILE, D_HEAD_PAD), lambda s: (0, second(s), 0)),
                   pl.BlockSpec((N_HEADS, TILE, D_V), lambda s: (0, second(s), 0))),
        scratch_shapes=[pltpu.VMEM((TILE, n_mla), F32), pltpu.VMEM((TILE, n_mla), F32)],
        compiler_params=pltpu.CompilerParams(dimension_semantics=("arbitrary",), vmem_limit_bytes=VMEM_LIMIT),
    )(x, ctx, modrows, bmodrows, norm_g, w_in_p, q_lora_g, w_uq_p, kv_lora_g, w_ukv, qng_p, kng_p, cos, slo, shi)


def _hosted_exchange(first, last, items, send_sems, recv_sems, local_sems):
    me = _lin(_coords())

    def remote(n, k, src, land, scatter):
        peer = _peer(k)
        return pltpu.make_async_remote_copy(
            src_ref=src.at[_lin(peer)] if scatter else src, dst_ref=land.at[me], send_sem=send_sems.at[n, k - 1],
            recv_sem=recv_sems.at[n, k - 1], device_id=peer, device_id_type=MESH)

    def local(n, src, land, scatter):
        return pltpu.make_async_copy(src.at[me] if scatter else src, land.at[me], local_sems.at[n])

    @pl.when(first)
    def _():
        for n, (src, land, scatter) in enumerate(items):
            for k in range(1, N_DEV):
                remote(n, k, src, land, scatter).start()
            local(n, src, land, scatter).start()

    @pl.when(last)
    def _():
        for n, (src, land, scatter) in enumerate(items):
            for k in range(1, N_DEV):
                peer = _peer(k)
                pltpu.make_async_remote_copy(
                    src_ref=src.at[0] if scatter else src, dst_ref=land.at[_lin(peer)], send_sem=send_sems.at[n, k - 1],
                    recv_sem=recv_sems.at[n, k - 1], device_id=peer, device_id_type=MESH).wait_recv()
            for k in range(1, N_DEV):
                remote(n, k, src, land, scatter).wait_send()
            local(n, src, land, scatter).wait()


def _attn_fwd(q, k, v, block_q, w_out_b):
    _, seq, _ = q.shape
    n_keys = k.shape[1]
    n_q = seq // block_q

    def body(q_ref, k_ref, v_ref, wo_ref, o_ref, lse_ref, wog_ref, ssem, rsem, lsem):
        h, i = pl.program_id(0), pl.program_id(1)
        _hosted_exchange((h == 0) & (i == 0), (h == N_HEADS - 1) & (i == n_q - 1), [(wo_ref, wog_ref, False)],
                         ssem, rsem, lsem)
        kb, vb = k_ref[0], v_ref[0]
        for r0 in range(0, block_q, ATTN_ROWS):
            rows = slice(r0, r0 + ATTN_ROWS)
            s = _dot_nt(q_ref[0, rows, :], kb)
            m = jnp.max(s, axis=-1, keepdims=True)
            p = jnp.exp2(s - m)
            l = jnp.sum(p, axis=-1, keepdims=True)
            o_ref[rows, :] = _dot(p.astype(BF16), vb) * (1.0 / l)
            lse_ref[0, rows, :] = m + jnp.log2(l)

    hbm = pl.BlockSpec(memory_space=pl.ANY)
    return pl.pallas_call(
        body, name="attn_fwd", grid=(N_HEADS, n_q),
        out_shape=(jax.ShapeDtypeStruct((seq, D_ATTN), F32), jax.ShapeDtypeStruct((N_HEADS, seq, 1), F32),
                   jax.ShapeDtypeStruct((N_DEV,) + w_out_b.shape, w_out_b.dtype)),
        in_specs=[pl.BlockSpec((1, block_q, D_HEAD_PAD), lambda h, i: (h, i, 0)),
                  pl.BlockSpec((1, n_keys, D_HEAD_PAD), lambda h, i: (h, 0, 0)),
                  pl.BlockSpec((1, n_keys, D_V), lambda h, i: (h, 0, 0)), hbm],
        out_specs=(pl.BlockSpec((block_q, D_V), lambda h, i: (i, h)),
                   pl.BlockSpec((1, block_q, 1), lambda h, i: (h, i, 0)), hbm),
        scratch_shapes=[pltpu.SemaphoreType.DMA((1, 7)), pltpu.SemaphoreType.DMA((1, 7)), pltpu.SemaphoreType.DMA((1,))],
        compiler_params=pltpu.CompilerParams(dimension_semantics=("arbitrary", "arbitrary"), vmem_limit_bytes=VMEM_LIMIT),
    )(q, k, v, w_out_b)


def _attn_bwd(q, k, v, o, lse, d_o, block_q, gwo_blocks, gwp):
    _, seq, _ = q.shape
    n_keys = k.shape[1]
    n_q = seq // block_q

    def body(q_ref, k_ref, v_ref, o_ref, lse_ref, do_ref, gwo_ref, gwp_ref, dq_ref, dk_ref, dv_ref, lwo_ref, lwp_ref,
             dkt_ref, dvt_ref, ssem, rsem, lsem):
        h, i = pl.program_id(0), pl.program_id(1)
        _hosted_exchange((h == 0) & (i == 0), (h == N_HEADS - 1) & (i == n_q - 1),
                         [(gwo_ref, lwo_ref, True), (gwp_ref, lwp_ref, False)], ssem, rsem, lsem)

        @pl.when(i == 0)
        def _():
            dkt_ref[...] = jnp.zeros_like(dkt_ref)
            dvt_ref[...] = jnp.zeros_like(dvt_ref)

        kb, vb = k_ref[0], v_ref[0]
        raws, ps = [], []
        for r0 in range(0, block_q, ATTN_ROWS):
            rows = slice(r0, r0 + ATTN_ROWS)
            d_out = do_ref[rows, :]
            p = jnp.exp2(_dot_nt(q_ref[0, rows, :], kb) - lse_ref[0, rows, :])
            dp = _dot_nt(d_out.astype(BF16), vb)
            delta = jnp.sum(d_out * o_ref[rows, :], axis=-1, keepdims=True)
            raw = (p * (dp - delta)).astype(BF16)
            dq_ref[0, rows, :] = _dot(raw, kb).astype(BF16)
            raws.append(raw)
            ps.append(p.astype(BF16))
        dkt_ref[...] += _dot_tn(q_ref[0], jnp.concatenate(raws, axis=0))
        dvt_ref[...] += _dot_tn(do_ref[...].astype(BF16), jnp.concatenate(ps, axis=0))

        @pl.when(i == n_q - 1)
        def _():
            dk_ref[0] = jnp.transpose(dkt_ref[...]).astype(BF16)
            dv_ref[0] = jnp.transpose(dvt_ref[...]).astype(BF16)

    hbm = pl.BlockSpec(memory_space=pl.ANY)
    return pl.pallas_call(
        body, name="attn_bwd", grid=(N_HEADS, n_q),
        out_shape=(jax.ShapeDtypeStruct((N_HEADS, seq, D_HEAD_PAD), BF16),
                   jax.ShapeDtypeStruct((N_HEADS, n_keys, D_HEAD_PAD), BF16),
                   jax.ShapeDtypeStruct((N_HEADS, n_keys, D_V), BF16),
                   jax.ShapeDtypeStruct(gwo_blocks.shape, gwo_blocks.dtype),
                   jax.ShapeDtypeStruct((N_DEV,) + gwp.shape, gwp.dtype)),
        in_specs=[pl.BlockSpec((1, block_q, D_HEAD_PAD), lambda h, i: (h, i, 0)),
                  pl.BlockSpec((1, n_keys, D_HEAD_PAD), lambda h, i: (h, 0, 0)),
                  pl.BlockSpec((1, n_keys, D_V), lambda h, i: (h, 0, 0)),
                  pl.BlockSpec((block_q, D_V), lambda h, i: (i, h)),
                  pl.BlockSpec((1, block_q, 1), lambda h, i: (h, i, 0)),
                  pl.BlockSpec((block_q, D_V), lambda h, i: (i, h)), hbm, hbm],
        out_specs=(pl.BlockSpec((1, block_q, D_HEAD_PAD), lambda h, i: (h, i, 0)),
                   pl.BlockSpec((1, n_keys, D_HEAD_PAD), lambda h, i: (h, 0, 0)),
                   pl.BlockSpec((1, n_keys, D_V), lambda h, i: (h, 0, 0)), hbm, hbm),
        scratch_shapes=[pltpu.VMEM((D_HEAD_PAD, n_keys), F32), pltpu.VMEM((D_V, n_keys), F32),
                        pltpu.SemaphoreType.DMA((2, 7)), pltpu.SemaphoreType.DMA((2, 7)), pltpu.SemaphoreType.DMA((2,))],
        compiler_params=pltpu.CompilerParams(dimension_semantics=("arbitrary", "arbitrary"), vmem_limit_bytes=VMEM_LIMIT),
    )(q, k, v, o, lse, d_o, gwo_blocks, gwp)


def _mix(x, target, attn, u, modrows, bmodrows, w_pool, pool_scale, w_out):
    seq = x.shape[0]
    rows = min(MIX_TILE, seq // 2)
    n_tiles = seq // rows
    rows8 = rows // HALO
    last8 = seq // HALO - 1

    n_blk = seq // Q_BLOCK
    per = rows // n_blk
    assert rows % n_blk == 0 and per % 8 == 0 and n_tiles >= 2
    n_stash = 8

    def body(x_ref, t_ref, o_hbm, ga_ref, pin_ref, hb_ref, ha_ref, gp_ref, mod_ref, bmod_ref, wp_ref, ps_ref, wo_ref,
             loss_ref, g1_ref, dgate_ref, do_hbm, dga_ref, dgp_ref, dpl_ref, gwob_ref, gwp_ref, dps_ref,
             abuf, dbuf, gwo_ref, *rest):
        stash_even, stash_odd = rest[0:n_stash], rest[n_stash:2 * n_stash]
        rsem, wsem = rest[2 * n_stash:]
        s = pl.program_id(0)

        def slab_copies(tile, slot, fetch):
            window = pl.ds(tile * per, per)
            if fetch:
                return [pltpu.make_async_copy(o_hbm.at[:, window, :], abuf.at[slot], rsem.at[slot])]
            return [pltpu.make_async_copy(dbuf.at[slot], do_hbm.at[:, window, :], wsem.at[slot])]

        def wait_all(slot, fetch):
            slab_copies(0, slot, fetch)[0].wait()

        a_slot = lax.rem(s, 2)
        b_slot = 1 - a_slot

        @pl.when(s == 0)
        def _():
            loss_ref[...] = jnp.zeros_like(loss_ref)
            dgate_ref[...] = jnp.zeros_like(dgate_ref)
            gwo_ref[...] = jnp.zeros_like(gwo_ref)
            gwp_ref[...] = jnp.zeros_like(gwp_ref)
            dps_ref[...] = jnp.zeros_like(dps_ref)
            for cp in slab_copies(0, 0, True):
                cp.start()

        @pl.when(s + 1 < n_tiles)
        def _():
            for cp in slab_copies(s + 1, b_slot, True):
                cp.start()

        @pl.when(s < n_tiles)
        def _():
            wait_all(a_slot, True)

        @pl.when(s >= 3)
        def _():
            wait_all(b_slot, False)

        gate = mod_ref[2:3, :] + bmod_ref[2:3, :]
        ps = ps_ref[...]

        def a_vector(slot):
            attn_t = jnp.swapaxes(abuf[slot], 0, 1).reshape(rows, D_ATTN)
            ga = ga_ref[...].astype(F32)
            sga = _sigmoid(ga)
            silu_ga = ga * sga
            pin = pin_ref[...].astype(F32)
            xs = jnp.concatenate([jnp.where(s > 0, hb_ref[...].astype(F32), 0.0), pin,
                                  jnp.where(s < n_tiles - 1, ha_ref[...].astype(F32), 0.0)], axis=0)
            tpos = s * rows + lax.broadcasted_iota(jnp.int32, (rows, 1), 0)
            pooled = []
            for g, w in enumerate(POOL_WINDOWS):
                sl = slice(g * GROUP_DIM, (g + 1) * GROUP_DIM)
                ws = _window_sum(xs[:, sl], w, False)[HALO:HALO + rows]
                pooled.append((ws * _inv_count(tpos, w, seq) - pin[:, sl]).astype(BF16))
            return attn_t, ga, sga, silu_ga, pooled

        def a_group_maps(pooled):
            return jnp.concatenate([_dot(pooled[g], wp_ref[g].astype(BF16)) for g in range(len(POOL_WINDOWS))], axis=1)

        def a_project(stash, yp, attn_t, ga, sga, silu_ga, pooled):
            zb_ref, _, fa_ref, sa_ref, fp_ref, sp_ref, yp_ref, pooled_ref = stash
            ypool = yp * ps
            gp = gp_ref[...].astype(F32)
            sgp = _sigmoid(gp)
            silu_gp = gp * sgp
            z = jnp.concatenate([silu_ga * attn_t, silu_gp * ypool], axis=1).astype(BF16)
            y = _dot(z, wo_ref[...])
            zb_ref[...] = z
            fa_ref[...] = attn_t * (sga * (1.0 + ga * (1.0 - sga)))
            sa_ref[...] = silu_ga
            fp_ref[...] = ypool * (sgp * (1.0 + gp * (1.0 - sgp)))
            sp_ref[...] = silu_gp
            yp_ref[...] = yp
            pooled_ref[...] = jnp.concatenate(pooled, axis=1)
            return y

        def a_loss(stash, y):
            res = x_ref[...] + gate * y - t_ref[...]
            loss_ref[...] += jnp.sum(res * res) * (0.5 / D_MODEL)
            dxn = res * (1.0 / D_MODEL)
            g1_ref[...] = dxn
            dgate_ref[...] += jnp.sum(dxn * y, axis=0, keepdims=True)
            stash[1][...] = (gate * dxn).astype(BF16)

        def b_dz(stash):
            return _dot_nt(stash[1][...], wo_ref[...])

        def b_gwo(stash):
            gwo_ref[...] += _dot_tn(stash[0][...], stash[1][...])

        def b_vector(stash, slot, dz):
            _, _, fa_ref, sa_ref, fp_ref, sp_ref, yp_ref, _ = stash
            dbra = dz[:, 0:D_ATTN]
            dbrp = dz[:, D_ATTN:]
            dga_ref[...] = (dbra * fa_ref[...]).astype(BF16)
            dbuf[slot] = jnp.swapaxes((dbra * sa_ref[...]).reshape(per, n_blk, D_ATTN), 0, 1)
            dgp_ref[...] = (dbrp * fp_ref[...]).astype(BF16)
            dyp_s = dbrp * sp_ref[...]
            dps_ref[...] += jnp.sum(dyp_s * yp_ref[...], axis=0, keepdims=True)
            return (dyp_s * ps).astype(BF16)

        def b_small(stash, dyp):
            pooled_ref = stash[7]
            for g in range(len(POOL_WINDOWS)):
                sl = slice(g * GROUP_DIM, (g + 1) * GROUP_DIM)
                gwp_ref[g] += _dot_tn(pooled_ref[:, sl], dyp[:, sl])
                dpl_ref[:, sl] = _dot_nt(dyp[:, sl], wp_ref[g].astype(BF16)).astype(BF16)

        def both(a_stash, b_stash, slot_a):
            dz = b_dz(b_stash)
            front = a_vector(slot_a)
            yp = a_group_maps(front[-1])
            b_gwo(b_stash)
            y = a_project(a_stash, yp, *front)
            dyp = b_vector(b_stash, 1 - slot_a, dz)
            a_loss(a_stash, y)
            b_small(b_stash, dyp)

        @pl.when(s == 0)
        def _():
            front = a_vector(0)
            a_loss(stash_even, a_project(stash_even, a_group_maps(front[-1]), *front))

        @pl.when((s > 0) & (s < n_tiles) & (a_slot == 0))
        def _():
            both(stash_even, stash_odd, 0)

        @pl.when((s > 0) & (s < n_tiles) & (a_slot == 1))
        def _():
            both(stash_odd, stash_even, 1)

        @pl.when(s == n_tiles)
        def _():
            last = (n_tiles - 1) % 2
            stash = stash_odd if last else stash_even
            dz = b_dz(stash)
            b_gwo(stash)
            b_small(stash, b_vector(stash, last, dz))
            gwob_ref[...] = gwo_ref[...].astype(BF16)

        @pl.when(s >= 1)
        def _():
            for cp in slab_copies(s - 1, b_slot, False):
                cp.start()

        @pl.when(s == n_tiles)
        def _():
            wait_all(b_slot, False)
            wait_all(a_slot, False)

    ta = lambda s: jnp.minimum(s, n_tiles - 1)
    tb = lambda s: jnp.maximum(s - 1, 0)
    tile = lambda w: pl.BlockSpec((rows, w), lambda s: (ta(s), 0))
    tile_b = lambda w: pl.BlockSpec((rows, w), lambda s: (tb(s), 0))
    ucol = lambda col: pl.BlockSpec((rows, 512), lambda s: (ta(s), col))
    full = lambda shape: pl.BlockSpec(shape, lambda s: (0,) * len(shape))
    stash_shapes = [pltpu.VMEM((rows, D_MODEL), BF16), pltpu.VMEM((rows, D_MODEL), BF16)] + \
        [pltpu.VMEM((rows, 512), F32)] * 5 + [pltpu.VMEM((rows, D_POOL), BF16)]
    return pl.pallas_call(
        body, name="mix_fwd_bwd", grid=(n_tiles + 1,),
        out_shape=(jax.ShapeDtypeStruct((8, 128), F32), jax.ShapeDtypeStruct((seq, D_MODEL), F32),
                   jax.ShapeDtypeStruct((1, D_MODEL), F32), jax.ShapeDtypeStruct((n_blk, Q_BLOCK, D_ATTN), F32),
                   jax.ShapeDtypeStruct((seq, D_ATTN), BF16), jax.ShapeDtypeStruct((seq, D_POOL), BF16),
                   jax.ShapeDtypeStruct((seq, D_POOL), BF16), jax.ShapeDtypeStruct((D_MODEL, D_MODEL), BF16),
                   jax.ShapeDtypeStruct((4, GROUP_DIM, GROUP_DIM), F32), jax.ShapeDtypeStruct((1, D_POOL), F32)),
        in_specs=[tile(D_MODEL), tile(D_MODEL), pl.BlockSpec(memory_space=pl.ANY), ucol(0), ucol(1),
                  pl.BlockSpec((HALO, 512), lambda s: (jnp.maximum(ta(s) * rows8 - 1, 0), 1)),
                  pl.BlockSpec((HALO, 512), lambda s: (jnp.minimum((ta(s) + 1) * rows8, last8), 1)),
                  ucol(2), full((8, D_MODEL)), full((8, D_MODEL)), full((4, GROUP_DIM, GROUP_DIM)), full((1, D_POOL)),
                  full((D_MODEL, D_MODEL))],
        out_specs=(full((8, 128)), tile(D_MODEL), full((1, D_MODEL)), pl.BlockSpec(memory_space=pl.ANY), tile_b(D_ATTN),
                   tile_b(D_POOL), tile_b(D_POOL), full((D_MODEL, D_MODEL)), full((4, GROUP_DIM, GROUP_DIM)),
                   full((1, D_POOL))),
        scratch_shapes=[pltpu.VMEM((2, n_blk, per, D_ATTN), F32), pltpu.VMEM((2, n_blk, per, D_ATTN), F32),
                        pltpu.VMEM((D_MODEL, D_MODEL), F32), *stash_shapes, *stash_shapes,
                        pltpu.SemaphoreType.DMA((2,)), pltpu.SemaphoreType.DMA((2,))],
        compiler_params=pltpu.CompilerParams(dimension_semantics=("arbitrary",), vmem_limit_bytes=VMEM_LIMIT),
    )(x, target, attn.reshape(n_blk, Q_BLOCK, D_ATTN), u, u, u, u, u, modrows, bmodrows, w_pool, pool_scale, w_out)


def _inproj_bwd(x, ctx, modrows, bmodrows, norm_g, w_in_p, q_lora_g, w_uq_p, kv_lora_g, w_ukv, qng_p, kng_p, cos, slo, shi,
                u, dq, dk, dv, dga, dgp, dpl, g1):
    seq = x.shape[0]
    n_tiles = seq // TILE + 1
    rows8 = TILE // HALO
    last8 = seq // HALO - 1

    def body(*refs):
        du_even, du_odd, dh_even, dh_odd = refs[-4:]
        gwin_ref, gwuq_ref, gwukv_ref, dqlg_ref, dkvlg_ref, dqng_ref, dkng_ref, dng_ref, dmod_ref = refs[-13:-4]
        s = pl.program_id(0)

        @pl.when(s == 0)
        def _():
            for ref in (gwin_ref, gwuq_ref, gwukv_ref, dqlg_ref, dkvlg_ref, dqng_ref, dkng_ref, dng_ref, dmod_ref,
                        du_odd, dh_even):
                ref[...] = jnp.zeros_like(ref)

        @pl.when(lax.rem(s, 2) == 0)
        def _():
            stages(s, refs[:-4], du_even, du_odd, dh_odd, dh_even)

        @pl.when(lax.rem(s, 2) == 1)
        def _():
            stages(s, refs[:-4], du_odd, du_even, dh_even, dh_odd)

    def stages(s, refs, du_ref, du_prev, dh_fill, dh_prev):
        (xb_ref, xc_ref, ctx_ref, mod_ref, bmod_ref, ng_ref, win_ref, qlg_ref, wuq_ref, kvlg_ref, wukv_ref, qng_ref, kng_ref,
         cos_ref, slo_ref, shi_ref, cq_ref, ckv_ref, kr_ref, dq_ref, dk_ref, dv_ref, dga_ref, dgp_ref, dpl_ref,
         hb_ref, ha_ref, g1_ref,
         gx_ref, gwin_ref, gwuq_ref, gwukv_ref, dqlg_ref, dkvlg_ref, dqng_ref, dkng_ref, dng_ref, dmod_ref) = refs


        i = jnp.minimum(s, n_tiles - 1)
        valid = s < n_tiles
        is_lat = (s > 0) & valid
        cq = cq_ref[...]
        rq = lax.rsqrt(jnp.mean(cq * cq, axis=-1, keepdims=True) + NORM_EPS)
        qhat = cq * rq
        qnb = (qhat * qlg_ref[...]).astype(BF16)
        q = _dot_nt(qnb, wuq_ref[...])
        ckv = ckv_ref[...]
        rkv = lax.rsqrt(jnp.mean(ckv * ckv, axis=-1, keepdims=True) + NORM_EPS)
        kvhat = ckv * rkv
        kvnb = (kvhat * kvlg_ref[...]).astype(BF16)
        kv = _dot(kvnb, wukv_ref[...])

        _, _, _, h2, _ = _modulated_input(s <= 1, xb_ref, ctx_ref, mod_ref, bmod_ref, ng_ref)
        dub = du_prev[...]
        du_g, du_m = dub[:, 0:N_GATES], dub[:, N_GATES:U_PAD]
        h2b = h2.astype(BF16)
        dh_fill[...] = _dot(du_g, win_ref[N_MLA:D_IN_PROJ, :]) + _dot(du_m, win_ref[0:U_PAD - N_GATES, :])
        gwin_ref[N_MLA:D_IN_PROJ, :] += _dot_tn(du_g, h2b)
        gwin_ref[0:N_MLA, :] += _dot_tn(du_m, h2b)[0:N_MLA]

        ctx3 = s <= 2
        xt, r, xg, _, scale = _modulated_input(ctx3, xc_ref, ctx_ref, mod_ref, bmod_ref, ng_ref)
        dh = dh_prev[...]
        dshift = jnp.sum(dh, axis=0, keepdims=True)
        dscale = jnp.sum(dh * xg, axis=0, keepdims=True)
        zero = jnp.zeros_like(dshift)
        dmod_ref[0:1, :] += jnp.where(ctx3, zero, dshift)
        dmod_ref[1:2, :] += jnp.where(ctx3, zero, dscale)
        dmod_ref[2:3, :] += jnp.where(ctx3, dshift, zero)
        dmod_ref[3:4, :] += jnp.where(ctx3, dscale, zero)
        dxg = dh * (1.0 + scale)
        xr = xt * r
        dng_ref[...] += jnp.sum(dxg * xr, axis=0, keepdims=True)
        dxn = dxg * ng_ref[...]
        gx_ref[...] = g1_ref[...] + r * (dxn - xr * jnp.mean(dxn * xr, axis=-1, keepdims=True))

        cos_t, slo_t, shi_t = cos_ref[...], slo_ref[...], shi_ref[...]
        qg = qng_ref[...]
        dq_heads = []
        dqng = jnp.zeros((1, D_HEAD_PAD), F32)
        for hd in range(N_HEADS):
            qh = q[:, hd * D_HEAD_PAD:(hd + 1) * D_HEAD_PAD]
            rr = lax.rsqrt(jnp.sum(qh * qh, axis=-1, keepdims=True) * (1.0 / D_HEAD) + NORM_EPS)
            yq = qh * rr
            dpost = jnp.where(is_lat, dq_ref[hd].astype(F32) * ATTN_SCALE, 0.0)
            dyn = jnp.concatenate([dpost[:, 0:D_NOPE], _rope_t(dpost[:, D_NOPE:], cos_t, slo_t, shi_t)], axis=1)
            dqng = dqng + jnp.sum(dyn * yq, axis=0, keepdims=True)
            dyg = dyn * qg
            dq_heads.append(rr * (dyg - yq * (jnp.sum(dyg * yq, axis=-1, keepdims=True) * (1.0 / D_HEAD))))
        dqng_ref[...] += dqng
        dqf = jnp.concatenate(dq_heads, axis=1).astype(BF16)

        krz = kr_ref[...]
        kr_ss = jnp.sum(krz * krz, axis=-1, keepdims=True)
        kg = kng_ref[...]
        kg_n, kg_r = kg[:, 0:D_NOPE], kg[:, D_NOPE:]
        dkv_parts = []
        dkr = jnp.zeros((TILE, 128), F32)
        dkng_n = jnp.zeros((1, D_NOPE), F32)
        dkng_r = jnp.zeros((1, 128), F32)
        for hd in range(N_HEADS):
            kn = kv[:, hd * 256:hd * 256 + D_NOPE]
            rr = lax.rsqrt((jnp.sum(kn * kn, axis=-1, keepdims=True) + kr_ss) * (1.0 / D_HEAD) + NORM_EPS)
            yn, yr = kn * rr, krz * rr
            dk_h = jnp.where(valid, dk_ref[hd].astype(F32) * LN_2, 0.0)
            dpn = dk_h[:, 0:D_NOPE]
            dpr = _rope_t(dk_h[:, D_NOPE:], cos_t, slo_t, shi_t)
            dkng_n = dkng_n + jnp.sum(dpn * yn, axis=0, keepdims=True)
            dkng_r = dkng_r + jnp.sum(dpr * yr, axis=0, keepdims=True)
            dyn, dyr = dpn * kg_n, dpr * kg_r
            proj = (jnp.sum(dyn * yn, axis=-1, keepdims=True) + jnp.sum(dyr * yr, axis=-1, keepdims=True)) * (1.0 / D_HEAD)
            dkv_parts.append(rr * (dyn - yn * proj))
            dkv_parts.append(jnp.where(valid, dv_ref[hd].astype(F32), 0.0))
            dkr = dkr + rr * (dyr - yr * proj)
        dkng_ref[:, 0:D_NOPE] += dkng_n
        dkng_ref[:, D_NOPE:] += dkng_r
        dkvf = jnp.concatenate(dkv_parts, axis=1).astype(BF16)

        lat_t = jnp.maximum(i - 1, 0)
        dpl_t = dpl_ref[...].astype(F32)
        ds = jnp.concatenate([jnp.where(i > 1, hb_ref[...].astype(F32), 0.0), dpl_t,
                              jnp.where(i < n_tiles - 1, ha_ref[...].astype(F32), 0.0)], axis=0)
        tpos = lat_t * TILE - HALO + lax.broadcasted_iota(jnp.int32, (TILE + 2 * HALO, 1), 0)
        for g, w in enumerate(POOL_WINDOWS):
            sl = slice(g * GROUP_DIM, (g + 1) * GROUP_DIM)
            wsum = _window_sum(ds[:, sl] * _inv_count(tpos, w, seq), w, True)[HALO:HALO + TILE]
            du_ref[:, O_PIN + g * GROUP_DIM:O_PIN + (g + 1) * GROUP_DIM] = jnp.where(
                is_lat, wsum - dpl_t[:, sl], 0.0).astype(BF16)

        du_ref[:, O_GA:O_GA + D_ATTN] = jnp.where(is_lat, dga_ref[...], jnp.zeros((), BF16))
        du_ref[:, O_GP:O_GP + D_POOL] = jnp.where(is_lat, dgp_ref[...], jnp.zeros((), BF16))
        du_ref[:, O_KR:U_PAD] = dkr.astype(BF16)

        gwuq_ref[...] += _dot_tn(dqf, qnb)
        dqn = _dot(dqf, wuq_ref[...])
        gwukv_ref[...] += _dot_tn(dkvf, kvnb)
        dkvn = _dot_nt(dkvf, wukv_ref[...])
        dqlg_ref[...] += jnp.sum(dqn * qhat, axis=0, keepdims=True)
        dqhat = dqn * qlg_ref[...]
        dcq = rq * (dqhat - qhat * jnp.mean(dqhat * qhat, axis=-1, keepdims=True))
        dkvlg_ref[...] += jnp.sum(dkvn * kvhat, axis=0, keepdims=True)
        dkvhat = dkvn * kvlg_ref[...]
        dckv = rkv * (dkvhat - kvhat * jnp.mean(dkvhat * kvhat, axis=-1, keepdims=True))
        du_ref[:, O_CQ:O_CQ + R_Q] = dcq.astype(BF16)
        du_ref[:, O_CKV:O_CKV + R_KV] = dckv.astype(BF16)

    t1 = lambda s: jnp.minimum(s, n_tiles - 1)
    t2 = lambda s: jnp.clip(s - 1, 0, n_tiles - 1)
    t3 = lambda s: jnp.clip(s - 2, 0, n_tiles - 1)
    latent = lambda t: jnp.maximum(t - 1, 0)
    lat = lambda s: (latent(t1(s)), 0)
    lat3 = lambda s: (latent(t3(s)), 0)
    full = lambda shape: pl.BlockSpec(shape, lambda s: (0,) * len(shape))
    rope_spec = pl.BlockSpec((TILE, 128), lambda s: (t1(s), 0))
    return pl.pallas_call(
        body, name="inproj_bwd", grid=(n_tiles + 2,),
        out_shape=(jax.ShapeDtypeStruct((seq, D_MODEL), F32), jax.ShapeDtypeStruct((D_IN_PROJ, D_MODEL), F32),
                   jax.ShapeDtypeStruct((N_HEADS * D_HEAD_PAD, R_Q), F32), jax.ShapeDtypeStruct((N_HEADS * 256, R_KV), F32),
                   jax.ShapeDtypeStruct((1, R_Q), F32), jax.ShapeDtypeStruct((1, R_KV), F32),
                   jax.ShapeDtypeStruct((1, D_HEAD_PAD), F32), jax.ShapeDtypeStruct((1, D_HEAD_PAD), F32),
                   jax.ShapeDtypeStruct((1, D_MODEL), F32), jax.ShapeDtypeStruct((8, D_MODEL), F32)),
        in_specs=[pl.BlockSpec((TILE, D_MODEL), lambda s: (latent(t2(s)), 0)), pl.BlockSpec((TILE, D_MODEL), lat3),
                  full((TILE, D_MODEL)), full((8, D_MODEL)), full((8, D_MODEL)),
                  full((1, D_MODEL)), full((D_IN_PROJ, D_MODEL)), full((1, R_Q)), full((N_HEADS * D_HEAD_PAD, R_Q)),
                  full((1, R_KV)), full((R_KV, N_HEADS * 256)), full((1, D_HEAD_PAD)), full((1, D_HEAD_PAD)),
                  rope_spec, rope_spec, rope_spec,
                  pl.BlockSpec((TILE, R_Q), lambda s: (t1(s), (O_CQ - N_GATES) // R_Q)),
                  pl.BlockSpec((TILE, R_KV), lambda s: (t1(s), (O_CKV - N_GATES) // R_KV)),
                  pl.BlockSpec((TILE, 128), lambda s: (t1(s), (O_KR - N_GATES) // 128)),
                  pl.BlockSpec((N_HEADS, TILE, D_HEAD_PAD), lambda s: (0, latent(t1(s)), 0)),
                  pl.BlockSpec((N_HEADS, TILE, D_HEAD_PAD), lambda s: (0, t1(s), 0)),
                  pl.BlockSpec((N_HEADS, TILE, D_V), lambda s: (0, t1(s), 0)),
                  pl.BlockSpec((TILE, D_ATTN), lat), pl.BlockSpec((TILE, D_POOL), lat), pl.BlockSpec((TILE, D_POOL), lat),
                  pl.BlockSpec((HALO, D_POOL), lambda s: (jnp.maximum((t1(s) - 1) * rows8 - 1, 0), 0)),
                  pl.BlockSpec((HALO, D_POOL), lambda s: (jnp.minimum(jnp.maximum(t1(s), 1) * rows8, last8), 0)),
                  pl.BlockSpec((TILE, D_MODEL), lat3)],
        out_specs=(pl.BlockSpec((TILE, D_MODEL), lat3), full((D_IN_PROJ, D_MODEL)), full((N_HEADS * D_HEAD_PAD, R_Q)),
                   full((N_HEADS * 256, R_KV)), full((1, R_Q)), full((1, R_KV)), full((1, D_HEAD_PAD)),
                   full((1, D_HEAD_PAD)), full((1, D_MODEL)), full((8, D_MODEL))),
        scratch_shapes=[pltpu.VMEM((TILE, U_PAD), BF16), pltpu.VMEM((TILE, U_PAD), BF16),
                        pltpu.VMEM((TILE, D_MODEL), F32), pltpu.VMEM((TILE, D_MODEL), F32)],
        compiler_params=pltpu.CompilerParams(dimension_semantics=("arbitrary",), vmem_limit_bytes=VMEM_LIMIT),
    )(x, x, ctx, modrows, bmodrows, norm_g, w_in_p, q_lora_g, w_uq_p, kv_lora_g, w_ukv, qng_p, kng_p, cos, slo, shi,
      u, u, u, dq, dk, dv, dga, dgp, dpl, dpl, dpl, g1)


SMALL_LAYOUT = ((0, D_MODEL), (1, R_Q), (2, R_KV), (3, D_HEAD_PAD), (4, D_HEAD_PAD), (5, D_POOL))
ROW_DMOD_B, ROW_DMOD_C, ROW_LOSS = 6, 9, 12


def _epilogue(parts, landed, smalls, dmod4, dgate, loss_acc, w_mod_l):
    n_a, n_l, n_s = len(parts), len(landed), len(smalls)
    n_mod = w_mod_l.shape[1]
    blk = [p.shape[1:] for p in parts]
    small_out = [D_MODEL, R_Q, R_KV, D_HEAD, D_HEAD, D_POOL]

    def body(*refs):
        part_refs = refs[0:n_a]
        land_refs = refs[n_a:n_a + n_l]
        small_in = refs[n_a + n_l:n_a + n_l + n_s]
        dmod4_ref, dgate_ref, loss_ref, wmod_ref = refs[n_a + n_l + n_s:n_a + n_l + n_s + 4]
        outs = refs[n_a + n_l + n_s + 4:]
        red_refs = outs[0:n_a]
        landsum_refs = outs[n_a:n_a + n_l]
        ccg_ref = outs[n_a + n_l]
        sum_refs = outs[n_a + n_l + 1:n_a + n_l + 1 + n_s]
        gbmod_ref, dmy_ref, lossout_ref = outs[n_a + n_l + 1 + n_s:n_a + n_l + 4 + n_s]
        scr = outs[n_a + n_l + 4 + n_s:]
        recv1, own1, sum1b, recv2 = scr[0:n_a], scr[n_a:2 * n_a], scr[2 * n_a:3 * n_a], scr[3 * n_a:4 * n_a]
        small_ref, smallg_ref, tot_ref, cc_ref = scr[4 * n_a:4 * n_a + 4]
        ssem1, rsem1, lsem, ssem2, rsem2, ssem_s, rsem_s, ssem_cc, rsem_cc = scr[4 * n_a + 4:]
        x, y, c = _coords()
        me = (x, y, c)
        sibling = (x, y, 1 - c)

        small_ref[...] = jnp.zeros_like(small_ref)
        for ref, (row, width) in zip(small_in, SMALL_LAYOUT):
            small_ref[row:row + 1, 0:width] = ref[...]
        small_ref[ROW_DMOD_B:ROW_DMOD_B + 2, :] = dmod4_ref[0:2, :]
        small_ref[ROW_DMOD_B + 2:ROW_DMOD_B + 3, :] = dgate_ref[...]
        small_ref[ROW_DMOD_C:ROW_DMOD_C + 2, :] = dmod4_ref[2:4, :]
        small_ref[ROW_LOSS:ROW_LOSS + 1, 0:128] = loss_ref[0:1, :]
        smallg_ref[_lin(me)] = small_ref[...]
        s_recv, s_send = _gather_to_all(small_ref, smallg_ref, ssem_s, rsem_s)

        stage1, own_copies = [], []
        for a in range(n_a):
            for b in range(4):
                dev = 4 * (b >> 1) + 2 * (b & 1)
                stage1.append(pltpu.make_async_remote_copy(
                    src_ref=part_refs[a].at[dev + (1 - c)], dst_ref=recv1[a].at[b],
                    send_sem=ssem1.at[a, b], recv_sem=rsem1.at[a, b], device_id=sibling, device_id_type=MESH))
                own_copies.append(pltpu.make_async_copy(part_refs[a].at[dev + c], own1[a].at[b], lsem.at[a, b]))
                stage1[-1].start()
                own_copies[-1].start()

        s_recv()
        tot = smallg_ref[0]
        for d in range(1, N_DEV):
            tot = tot + smallg_ref[d]
        tot_ref[...] = tot
        for ref, (row, _), width in zip(sum_refs, SMALL_LAYOUT, small_out):
            ref[...] = tot_ref[row:row + 1, 0:width]
        lossout_ref[...] = tot_ref[8:16, 0:128]
        lin = _lin(me)

        def my_columns(three_rows):
            v = jnp.concatenate(three_rows, axis=1)
            out = jnp.zeros((1, n_mod), F32)
            for d in range(N_DEV):
                out = out + jnp.where(lin == d, v[:, d * n_mod:(d + 1) * n_mod], 0.0)
            return out

        rows3 = lambda ref, r0: [ref[r0 + t:r0 + t + 1, :] for t in range(3)]
        dmodc = rows3(tot_ref, ROW_DMOD_C)
        gbmod_ref[...] = jnp.concatenate(rows3(tot_ref, ROW_DMOD_B), axis=1) + jnp.concatenate(dmodc, axis=1)
        dmy_ref[...] = jnp.zeros_like(dmy_ref)
        for b in range(N_DEV):
            dmy_ref[b:b + 1, :] = my_columns(rows3(smallg_ref.at[b], ROW_DMOD_B))
        dmy = my_columns(dmodc)
        dmy_ref[N_DEV:N_DEV + 1, :] = dmy
        part = lax.dot_general(jnp.broadcast_to(dmy, (8, n_mod)), wmod_ref[...], (((1,), (1,)), ((), ())),
                               precision=HIGHEST, preferred_element_type=F32)

        cc_ref[...] = part
        ccg_ref[_lin(me)] = part
        cc_recv, cc_send = _gather_to_all(cc_ref, ccg_ref, ssem_cc, rsem_cc)

        stage2 = []
        for a in range(n_a):
            for b in range(4):
                stage1[4 * a + b].wait_recv()
                own_copies[4 * a + b].wait()
                sum1b[a][b] = (own1[a][b] + recv1[a][b]).astype(BF16)
            for k in (1, 2, 3):
                tx, ty = _flip(x, (k >> 1) & 1), _flip(y, k & 1)
                stage2.append(pltpu.make_async_remote_copy(
                    src_ref=sum1b[a].at[2 * tx + ty], dst_ref=recv2[a].at[k - 1], send_sem=ssem2.at[a, k - 1],
                    recv_sem=rsem2.at[a, k - 1], device_id=(tx, ty, c), device_id_type=MESH))
                stage2[-1].start()
        for a in range(n_a):
            own = own1[a][2 * x + y] + recv1[a][2 * x + y]
            for k in (1, 2, 3):
                stage2[3 * a + k - 1].wait_recv()
                own = own + recv2[a][k - 1].astype(F32)
            red_refs[a][...] = own

        for ref, out in zip(land_refs, landsum_refs):
            acc = ref[0].astype(F32)
            for d in range(1, N_DEV):
                acc = acc + ref[d].astype(F32)
            out[...] = acc
        cc_recv()
        for cp in stage1 + stage2:
            cp.wait_send()
        s_send()
        cc_send()

    vm = pl.BlockSpec(memory_space=pltpu.VMEM)
    sds = jax.ShapeDtypeStruct
    return pl.pallas_call(
        body, name="epilogue_reduce",
        out_shape=(*[sds(s, F32) for s in blk], *[sds(a.shape[1:], F32) for a in landed], sds((N_DEV, 8, D_MODEL), F32),
                   *[sds((1, w), F32) for w in small_out], sds((1, 3 * D_MODEL), F32), sds((16, n_mod), F32),
                   sds((8, 128), F32)),
        in_specs=[pl.BlockSpec(memory_space=pl.ANY)] * n_a + [vm] * (n_l + n_s + 4),
        out_specs=tuple([vm] * (n_a + n_l + n_s + 4)),
        scratch_shapes=[*[pltpu.VMEM((4,) + s, F32) for s in blk], *[pltpu.VMEM((4,) + s, F32) for s in blk],
                        *[pltpu.VMEM((4,) + s, BF16) for s in blk], *[pltpu.VMEM((3,) + s, BF16) for s in blk],
                        pltpu.VMEM((SMALL_ROWS, D_MODEL), F32), pltpu.VMEM((N_DEV, SMALL_ROWS, D_MODEL), F32),
                        pltpu.VMEM((SMALL_ROWS, D_MODEL), F32), pltpu.VMEM((8, D_MODEL), F32),
                        pltpu.SemaphoreType.DMA((n_a, 4)), pltpu.SemaphoreType.DMA((n_a, 4)), pltpu.SemaphoreType.DMA((n_a, 4)),
                        pltpu.SemaphoreType.DMA((n_a, 3)), pltpu.SemaphoreType.DMA((n_a, 3)),
                        pltpu.SemaphoreType.DMA((7,)), pltpu.SemaphoreType.DMA((7,)),
                        pltpu.SemaphoreType.DMA((7,)), pltpu.SemaphoreType.DMA((7,))],
        compiler_params=pltpu.CompilerParams(vmem_limit_bytes=VMEM_LIMIT),
    )(*parts, *landed, *smalls, dmod4, dgate, loss_acc, w_mod_l)


def _adamw(weights, grads, ms, vs, c_all_t, dmod_my, p2g):
    n = len(weights)

    def body(*refs):
        w_refs = refs[0:n]
        g_in = refs[n:2 * n - 2]
        m_refs = refs[2 * n - 2:3 * n - 2]
        v_refs = refs[3 * n - 2:4 * n - 2]
        cat_ref, dmod_ref, p2g_ref = refs[4 * n - 2:4 * n + 1]
        outs = refs[4 * n + 1:]
        g_refs, d_refs, nm_refs, nv_refs = outs[0:n], outs[n:2 * n], outs[2 * n:3 * n], outs[3 * n:4 * n]
        gcc_ref, gwm_ref = g_refs[0], g_refs[1]

        part = p2g_ref[0, 0:1, :]
        for d in range(1, N_DEV):
            part = part + p2g_ref[d, 0:1, :]
        cc = w_refs[0][...]
        sg = _sigmoid(cc)
        gcc_ref[...] = part * (sg * (1.0 + cc * (1.0 - sg)))
        cat = cat_ref[...]
        gwm_ref[...] = lax.dot_general(cat * _sigmoid(cat), dmod_ref[...], (((1,), (0,)), ((), ())), precision=HIGHEST,
                                       preferred_element_type=F32)
        for idx in range(n):
            if idx >= 2:
                g_refs[idx][...] = g_in[idx - 2][...]
            g = g_refs[idx][...]
            m = ADAM_B1 * m_refs[idx][...] + (1.0 - ADAM_B1) * g
            v = ADAM_B2 * v_refs[idx][...] + (1.0 - ADAM_B2) * (g * g)
            m_hat = m / (1.0 - ADAM_B1 ** ADAM_STEP)
            v_hat = v / (1.0 - ADAM_B2 ** ADAM_STEP)
            d_refs[idx][...] = -ADAM_LR * (m_hat / (jnp.sqrt(v_hat) + ADAM_EPS) + ADAM_WD * w_refs[idx][...])
            nm_refs[idx][...] = m
            nv_refs[idx][...] = v

    vm = pl.BlockSpec(memory_space=pltpu.VMEM)
    shapes = [jax.ShapeDtypeStruct(w.shape, F32) for w in weights]
    args = list(weights) + list(grads[2:]) + list(ms) + list(vs) + [c_all_t, dmod_my, p2g]
    out_shape = tuple(shapes * 4)
    return pl.pallas_call(
        body, name="adamw_update", out_shape=out_shape, in_specs=[vm] * len(args), out_specs=tuple([vm] * len(out_shape)),
        compiler_params=pltpu.CompilerParams(vmem_limit_bytes=VMEM_LIMIT),
    )(*args)


def _rope_tables(seq):
    rows = seq // GRID_W
    row = np.repeat(np.arange(rows, dtype=np.float32), GRID_W)
    col = np.tile(np.arange(GRID_W, dtype=np.float32), rows)
    n_freq = D_ROPE // 4
    inv = (np.float32(ROPE_BASE) ** (-np.arange(n_freq, dtype=np.float32) / np.float32(n_freq))).astype(np.float32)
    ang_r = (row[:, None] * inv).astype(np.float32)
    ang_c = (col[:, None] * inv).astype(np.float32)
    ang = np.concatenate([ang_r, ang_r, ang_c, ang_c], axis=-1)
    cos, sin = np.cos(ang).astype(np.float32), np.sin(ang).astype(np.float32)
    low = (np.arange(D_ROPE) % 32) < 16

    def table(t, fill):
        out = np.full((TILE + seq, 128), fill, np.float32)
        out[TILE:, 0:D_ROPE] = t
        return jnp.asarray(out)

    return table(cos, 1.0), table(np.where(low, -sin, 0.0), 0.0), table(np.where(low, 0.0, sin), 0.0)


def kernel(x, c, ctx, c_ctx, w_mod, b_mod, norm_g, w_in, q_lora_g, w_uq, kv_lora_g, w_ukv, q_norm_g, k_norm_g, w_pool, pool_scale, w_out, loss_target, m_c_ctx, m_w_mod, m_b_mod, m_norm_g, m_w_in, m_q_lora_g, m_w_uq, m_kv_lora_g, m_w_ukv, m_q_norm_g, m_k_norm_g, m_w_pool, m_pool_scale, m_w_out, v_c_ctx, v_w_mod, v_b_mod, v_norm_g, v_w_in, v_q_lora_g, v_w_uq, v_kv_lora_g, v_w_ukv, v_q_norm_g, v_k_norm_g, v_w_pool, v_pool_scale, v_w_out):
    seq = x.shape[1]
    assert ctx.shape[1] == TILE and seq % TILE == 0 and seq % GRID_W == 0
    ix, iy, ic = lax.axis_index("x"), lax.axis_index("y"), lax.axis_index("c")
    me = 4 * ix + 2 * iy + ic
    x2, ctx2, tgt2 = x[0], ctx[0], loss_target[0]
    w_mod_l, w_ukv_l, w_out_l = w_mod[0], w_ukv[0], w_out[0]
    c_ctx_row = c_ctx.reshape(1, D_MODEL)

    tr = lambda a: jnp.transpose(a[0])
    w_in_tl, w_uq_tl = tr(w_in), tr(w_uq)
    cg, modg, wg_in, wg_uq, wg_ukv = _prologue(
        c, c_ctx_row, w_mod_l,
        [w_in_tl, w_uq_tl, w_ukv_l])
    mod_all = jnp.transpose(modg, (1, 0, 2)).reshape(16, 3 * D_MODEL)
    mod_b = lax.dynamic_slice_in_dim(mod_all, me, 1, axis=0).reshape(3, D_MODEL)
    mod_c = mod_all[8].reshape(3, D_MODEL)
    modrows = jnp.concatenate([mod_b, mod_c[0:2], jnp.zeros((3, D_MODEL), F32)], axis=0)
    b3 = b_mod.reshape(3, D_MODEL)
    bmodrows = jnp.concatenate([b3, b3[0:2], jnp.zeros((3, D_MODEL), F32)], axis=0)

    w_in_p = wg_in.reshape(D_IN_PROJ, D_MODEL)
    w_uq_p = jnp.concatenate([wg_uq.reshape(N_HEADS, D_HEAD, R_Q), jnp.zeros((N_HEADS, D_HEAD_PAD - D_HEAD, R_Q), BF16)],
                             axis=1).reshape(N_HEADS * D_HEAD_PAD, R_Q)
    w_ukv_f = jnp.transpose(wg_ukv, (1, 0, 2)).reshape(R_KV, N_HEADS * 256)
    qng_p = jnp.concatenate([q_norm_g, jnp.zeros((1, D_HEAD_PAD - D_HEAD), F32)], axis=1)
    kng_p = jnp.concatenate([k_norm_g, jnp.zeros((1, D_HEAD_PAD - D_HEAD), F32)], axis=1)
    cos, slo, shi = _rope_tables(seq)

    u_gates, u_mla, q, k, v = _inproj_fwd(x2, ctx2, modrows, bmodrows, norm_g, w_in_p, q_lora_g, w_uq_p, kv_lora_g, w_ukv_f,
                             qng_p, kng_p, cos, slo, shi)
    attn, lse, wg_out = _attn_fwd(q, k, v, min(2048, seq), w_out_l.astype(BF16))
    (loss_acc, g1, dgate, d_attn, dga, dgp, dpl, gwo_b, gwp, dps) = _mix(
        x2, tgt2, attn, u_gates, modrows, bmodrows, w_pool[0], pool_scale, wg_out.reshape(D_MODEL, D_MODEL))

    blocks = lambda a: a.reshape((N_DEV, a.shape[0] // N_DEV) + a.shape[1:])
    dq, dk, dv, land_wo, land_wp = _attn_bwd(q, k, v, attn, lse, d_attn.reshape(seq, D_ATTN), min(1024, seq), blocks(gwo_b),
                                             gwp.reshape(4 * GROUP_DIM, GROUP_DIM))
    (grad_x, gwin_t, gwuq_p, gwukv_t, dqlg, dkvlg, dqng, dkng, dng, dmod4) = _inproj_bwd(
        x2, ctx2, modrows, bmodrows, norm_g, w_in_p, q_lora_g, w_uq_p, kv_lora_g, w_ukv_f, qng_p, kng_p, cos, slo, shi,
        u_mla, dq, dk, dv, dga, dgp, dpl, g1)

    gwuq_t = gwuq_p.reshape(N_HEADS, D_HEAD_PAD, R_Q)[:, 0:D_HEAD].reshape(N_HEADS * D_HEAD, R_Q)
    parts = [blocks(gwin_t), blocks(gwuq_t), blocks(gwukv_t)]
    (r_win, r_wuq, r_wukv, r_wout, g_w_pool, ccg, g_ng, g_qlg, g_kvlg, g_qng, g_kng, g_ps, g_bmod, dmod_my,
     loss_rows) = _epilogue(parts, [land_wo, land_wp], [dng, dqlg, dkvlg, dqng, dkng, dps], dmod4, dgate, loss_acc, w_mod_l)

    g_w_ukv = jnp.transpose(r_wukv)
    c_rows = cg[:, 0, :]
    c_all_t = jnp.transpose(jnp.concatenate([c_rows, c_ctx_row, jnp.zeros((16 - N_DEV - 1, D_MODEL), F32)], axis=0))

    weights = [c_ctx_row, w_mod_l, b_mod, norm_g, w_in_tl, q_lora_g, w_uq_tl, kv_lora_g, w_ukv_l, q_norm_g, k_norm_g,
               w_pool.reshape(4 * GROUP_DIM, GROUP_DIM), pool_scale, w_out_l]
    grads = [None, None, g_bmod, g_ng, r_win, g_qlg, r_wuq, g_kvlg, g_w_ukv, g_qng, g_kng, g_w_pool, g_ps, r_wout]
    ms = [m_c_ctx.reshape(1, D_MODEL), m_w_mod[0], m_b_mod, m_norm_g, tr(m_w_in), m_q_lora_g, tr(m_w_uq), m_kv_lora_g,
          m_w_ukv[0], m_q_norm_g, m_k_norm_g, m_w_pool.reshape(4 * GROUP_DIM, GROUP_DIM), m_pool_scale, m_w_out[0]]
    vs = [v_c_ctx.reshape(1, D_MODEL), v_w_mod[0], v_b_mod, v_norm_g, tr(v_w_in), v_q_lora_g, tr(v_w_uq), v_kv_lora_g,
          v_w_ukv[0], v_q_norm_g, v_k_norm_g, v_w_pool.reshape(4 * GROUP_DIM, GROUP_DIM), v_pool_scale, v_w_out[0]]
    outs = _adamw(weights, grads, ms, vs, c_all_t, dmod_my, ccg)
    n = len(weights)
    grads, deltas, new_m, new_v = outs[0:n], outs[n:2 * n], outs[2 * n:3 * n], outs[3 * n:4 * n]

    loss = loss_rows[ROW_LOSS - 8, 0]
    final = [c_ctx.shape, w_mod.shape, b_mod.shape, norm_g.shape, w_in.shape, q_lora_g.shape, w_uq.shape, kv_lora_g.shape,
             w_ukv.shape, q_norm_g.shape, k_norm_g.shape, w_pool.shape, pool_scale.shape, w_out.shape]
    transposed = (4, 6)

    def shaped(arrs):
        return [(jnp.transpose(a) if i in transposed else a).reshape(s) for i, (a, s) in enumerate(zip(arrs, final))]

    return (loss, grad_x[None], *shaped(grads), *shaped(deltas), *shaped(new_m), *shaped(new_v))
```

```python
import numpy as np

import jax
import jax.numpy as jnp
from jax import lax
from jax.experimental import pallas as pl
from jax.experimental.pallas import tpu as pltpu

F32 = jnp.float32
BF16 = jnp.bfloat16
MESH = pl.DeviceIdType.MESH
HIGHEST = lax.Precision.HIGHEST

D_MODEL = 1024
N_HEADS = 4
D_NOPE = 128
D_ROPE = 64
D_HEAD = D_NOPE + D_ROPE
D_HEAD_PAD = 256
D_V = 128
R_Q = 256
R_KV = 128
D_ATTN = 512
D_POOL = 512
POOL_WINDOWS = (2, 4, 8, 16)
GROUP_DIM = 128
GRID_W = 64
ROPE_BASE = 10000.0
NORM_EPS = 1e-6
ATTN_SCALE = D_HEAD ** -0.5
LOG2_E = 1.4426950408889634
LN_2 = 0.6931471805599453
ATTN_ROWS = 256
D_IN_PROJ = 1984
U_PAD = 2048
O_GA, O_PIN, O_GP, O_CQ, O_CKV, O_KR = 0, 512, 1024, 1536, 1792, 1920
TILE = 256
MIX_TILE = 512
Q_BLOCK = 128
HALO = 16
N_GATES = 1536
N_MLA = D_IN_PROJ - N_GATES
N_DEV = 8
VMEM_LIMIT = 56 * 1024 * 1024

ADAM_LR = 0.001
ADAM_B1 = 0.9
ADAM_B2 = 0.999
ADAM_EPS = 1e-08
ADAM_WD = 0.01
ADAM_STEP = 10

SMALL_ROWS = 16


def _dot(a, b):
    return lax.dot_general(a, b, (((1,), (0,)), ((), ())), preferred_element_type=F32)


def _dot_nt(a, b):
    return lax.dot_general(a, b, (((1,), (1,)), ((), ())), preferred_element_type=F32)


def _dot_tn(a, b):
    return lax.dot_general(a, b, (((0,), (0,)), ((), ())), preferred_element_type=F32)


def _sigmoid(x):
    return 1.0 / (1.0 + jnp.exp(-x))


def _rope(p, cos, slo, shi):
    return p * cos + pltpu.roll(p, 112, 1) * slo + pltpu.roll(p, 16, 1) * shi


def _rope_t(d, cos, slo, shi):
    return d * cos + pltpu.roll(d * slo, 16, 1) + pltpu.roll(d * shi, 112, 1)


def _shift_rows(a, k):
    n = a.shape[0]
    return pltpu.roll(a, (-k) % n, 0)


def _window_sum(x, w, transposed):
    s = (x + _shift_rows(x, 1)) if transposed else (_shift_rows(x, -1) + x)
    step = 1
    while 2 * step < w:
        s = _shift_rows(s, -step) + _shift_rows(s, step)
        step *= 2
    return s


def _inv_count(tpos, w, seq):
    lo = jnp.maximum(tpos - w // 2, 0)
    hi = jnp.minimum(tpos - w // 2 + w, seq)
    return 1.0 / jnp.maximum(hi - lo, 1).astype(F32)


def _coords():
    return lax.axis_index("x"), lax.axis_index("y"), lax.axis_index("c")


def _flip(v, bit):
    return (1 - v) if bit else v


def _peer(k):
    x, y, c = _coords()
    return (_flip(x, (k >> 2) & 1), _flip(y, (k >> 1) & 1), _flip(c, k & 1))


def _lin(p):
    return 4 * p[0] + 2 * p[1] + p[2]


def _gather_to_all(src_ref, slots_ref, send_sems, recv_sems):
    me = _coords()
    copies = []
    for k in range(1, N_DEV):
        cp = pltpu.make_async_remote_copy(
            src_ref=src_ref, dst_ref=slots_ref.at[_lin(me)], send_sem=send_sems.at[k - 1], recv_sem=recv_sems.at[k - 1],
            device_id=_peer(k), device_id_type=MESH)
        cp.start()
        copies.append(cp)

    def wait_recv():
        for k in range(1, N_DEV):
            pltpu.make_async_remote_copy(
                src_ref=src_ref, dst_ref=slots_ref.at[_lin(_peer(k))], send_sem=send_sems.at[k - 1],
                recv_sem=recv_sems.at[k - 1], device_id=_peer(k), device_id_type=MESH).wait_recv()

    def wait_send():
        for cp in copies:
            cp.wait_send()

    return wait_recv, wait_send


def _prologue(c8, cctx8, w_mod_l, shards):
    n_mod = w_mod_l.shape[1]
    n_w = len(shards)

    def body(c_ref, cctx_ref, wmod_ref, *refs):
        w_refs = refs[0:n_w]
        cg_ref, modg_ref = refs[n_w], refs[n_w + 1]
        wg_refs = refs[n_w + 2:2 * n_w + 2]
        modblk_ref = refs[2 * n_w + 2]
        wb_refs = refs[2 * n_w + 3:3 * n_w + 3]
        c8_ref = refs[3 * n_w + 3]
        ssem_w, rsem_w, ssem_c, rsem_c, ssem_m, rsem_m = refs[3 * n_w + 4:]
        x, y, c = _coords()
        me = (x, y, c)
        sibling = (x, y, 1 - c)
        chips = [(1 - x, y), (x, 1 - y), (1 - x, 1 - y)]

        def wcopies(k, block, to, own=False):
            out = []
            for a in range(n_w):
                slot = wg_refs[a].at[_lin(block)]
                out.append(pltpu.make_async_remote_copy(
                    src_ref=wb_refs[a] if own else slot, dst_ref=slot, send_sem=ssem_w.at[a, k], recv_sem=rsem_w.at[a, k],
                    device_id=to, device_id_type=MESH))
            return out

        c8_ref[...] = jnp.broadcast_to(c_ref[...], (8, D_MODEL))
        cg_ref[_lin(me)] = c8_ref[...]
        c_recv, c_send = _gather_to_all(c8_ref, cg_ref, ssem_c, rsem_c)

        for a in range(n_w):
            wb_refs[a][...] = w_refs[a][...].astype(BF16)
        first = wcopies(0, me, sibling, own=True)
        for j, chip in enumerate(chips):
            first += wcopies(1 + j, me, (*chip, c), own=True)
        late = [idx for idx in range(len(first)) if idx % n_w == 0 and idx >= n_w]
        for idx, cp in enumerate(first):
            if idx not in late:
                cp.start()
        for a in range(n_w):
            wg_refs[a][_lin(me)] = wb_refs[a][...]

        c_recv()
        row = lax.broadcasted_iota(jnp.int32, (8, D_MODEL), 0)
        c_all = jnp.zeros((8, D_MODEL), F32)
        for d in range(N_DEV):
            c_all = c_all + jnp.where(row == d, cg_ref[d], 0.0)
        cc = jnp.broadcast_to(cctx_ref[...], (8, D_MODEL))
        a = jnp.concatenate([c_all * _sigmoid(c_all), cc * _sigmoid(cc)], axis=0)
        modblk_ref[...] = lax.dot_general(a, wmod_ref[...], (((1,), (0,)), ((), ())), precision=HIGHEST,
                                          preferred_element_type=F32)
        modg_ref[_lin(me)] = modblk_ref[...]
        m_recv, m_send = _gather_to_all(modblk_ref, modg_ref, ssem_m, rsem_m)
        for idx in late:
            first[idx].start()
        m_recv()

        passed = []
        for j, chip in enumerate(chips):
            for cp in wcopies(1 + j, (*chip, c), me):
                cp.wait_recv()
            fwd = wcopies(4 + j, (*chip, c), sibling)
            for cp in fwd:
                cp.start()
            passed += fwd
        for cp in wcopies(0, sibling, me):
            cp.wait_recv()
        for j, chip in enumerate(chips):
            for cp in wcopies(4 + j, (*chip, 1 - c), me):
                cp.wait_recv()
        for cp in first + passed:
            cp.wait_send()
        c_send()
        m_send()

    vm = pl.BlockSpec(memory_space=pltpu.VMEM)
    return pl.pallas_call(
        body, name="prologue_gather",
        out_shape=(jax.ShapeDtypeStruct((N_DEV, 8, D_MODEL), F32), jax.ShapeDtypeStruct((N_DEV, 16, n_mod), F32),
                   *[jax.ShapeDtypeStruct((N_DEV,) + s.shape, BF16) for s in shards]),
        in_specs=[vm] * (3 + n_w), out_specs=tuple([vm] * (2 + n_w)),
        scratch_shapes=[pltpu.VMEM((16, n_mod), F32), *[pltpu.VMEM(s.shape, BF16) for s in shards],
                        pltpu.VMEM((8, D_MODEL), F32),
                        pltpu.SemaphoreType.DMA((n_w, 7)), pltpu.SemaphoreType.DMA((n_w, 7)),
                        pltpu.SemaphoreType.DMA((7,)), pltpu.SemaphoreType.DMA((7,)),
                        pltpu.SemaphoreType.DMA((7,)), pltpu.SemaphoreType.DMA((7,))],
        compiler_params=pltpu.CompilerParams(vmem_limit_bytes=VMEM_LIMIT),
    )(c8, cctx8, w_mod_l, *shards)


def _modulated_input(is_ctx, x_ref, ctx_ref, mod_ref, bmod_ref, ng_ref):
    xt = jnp.where(is_ctx, ctx_ref[...], x_ref[...])
    shift = jnp.where(is_ctx, mod_ref[3:4, :] + bmod_ref[3:4, :], mod_ref[0:1, :] + bmod_ref[0:1, :])
    scale = jnp.where(is_ctx, mod_ref[4:5, :] + bmod_ref[4:5, :], mod_ref[1:2, :] + bmod_ref[1:2, :])
    r = lax.rsqrt(jnp.mean(xt * xt, axis=-1, keepdims=True) + NORM_EPS)
    xg = (xt * r) * ng_ref[...]
    h = xg * (1.0 + scale) + shift
    return xt, r, xg, h, scale


def _inproj_fwd(x, ctx, modrows, bmodrows, norm_g, w_in_p, q_lora_g, w_uq_p, kv_lora_g, w_ukv, qng_p, kng_p, cos, slo, shi):
    seq = x.shape[0]
    n_tiles = seq // TILE + 1
    tot = seq + TILE

    n_mla = R_Q + R_KV + 128

    def body(x_ref, ctx_ref, mod_ref, bmod_ref, ng_ref, win_ref, qlg_ref, wuq_ref, kvlg_ref, wukv_ref, qng_ref, kng_ref,
             cos_ref, slo_ref, shi_ref, ug_ref, um_ref, q_ref, k_ref, v_ref, stash):
        s = pl.program_id(0)

        @pl.when(s == 0)
        def _():
            stash[...] = jnp.zeros_like(stash)

        refs = (x_ref, ctx_ref, mod_ref, bmod_ref, ng_ref, win_ref, qlg_ref, wuq_ref, kvlg_ref, wukv_ref, qng_ref, kng_ref,
                cos_ref, slo_ref, shi_ref, ug_ref, um_ref, q_ref, k_ref, v_ref)
        stages(s, refs, stash, stash)

    def stages(s, refs, fill, prev_ref):
        (x_ref, ctx_ref, mod_ref, bmod_ref, ng_ref, win_ref, qlg_ref, wuq_ref, kvlg_ref, wukv_ref, qng_ref, kng_ref,
         cos_ref, slo_ref, shi_ref, ug_ref, um_ref, q_ref, k_ref, v_ref) = refs
        cos_t, slo_t, shi_t = cos_ref[...], slo_ref[...], shi_ref[...]
        prev = prev_ref[...]
        cq = prev[:, 0:R_Q]
        qn = cq * lax.rsqrt(jnp.mean(cq * cq, axis=-1, keepdims=True) + NORM_EPS) * qlg_ref[...]
        q = _dot_nt(qn.astype(BF16), wuq_ref[...])
        qg = qng_ref[...] * (ATTN_SCALE * LOG2_E)
        for hd in range(N_HEADS):
            qh = q[:, hd * D_HEAD_PAD:(hd + 1) * D_HEAD_PAD]
            rr = lax.rsqrt(jnp.sum(qh * qh, axis=-1, keepdims=True) * (1.0 / D_HEAD) + NORM_EPS)
            qy = qh * rr * qg
            q_ref[hd, :, 0:D_NOPE] = qy[:, 0:D_NOPE].astype(BF16)
            q_ref[hd, :, D_NOPE:D_HEAD_PAD] = _rope(qy[:, D_NOPE:D_HEAD_PAD], cos_t, slo_t, shi_t).astype(BF16)

        ckv = prev[:, R_Q:R_Q + R_KV]
        kvn = ckv * lax.rsqrt(jnp.mean(ckv * ckv, axis=-1, keepdims=True) + NORM_EPS) * kvlg_ref[...]
        kv = _dot(kvn.astype(BF16), wukv_ref[...])
        krz = prev[:, R_Q + R_KV:n_mla]
        kr_ss = jnp.sum(krz * krz, axis=-1, keepdims=True)
        kg = kng_ref[...]
        for hd in range(N_HEADS):
            kn = kv[:, hd * 256:hd * 256 + D_NOPE]
            rr = lax.rsqrt((jnp.sum(kn * kn, axis=-1, keepdims=True) + kr_ss) * (1.0 / D_HEAD) + NORM_EPS)
            k_ref[hd, :, 0:D_NOPE] = (kn * rr * kg[:, 0:D_NOPE]).astype(BF16)
            k_ref[hd, :, D_NOPE:D_HEAD_PAD] = _rope(krz * rr * kg[:, D_NOPE:D_HEAD_PAD], cos_t, slo_t, shi_t).astype(BF16)
            v_ref[hd] = kv[:, hd * 256 + D_NOPE:(hd + 1) * 256].astype(BF16)

        _, _, _, h, _ = _modulated_input(s == 0, x_ref, ctx_ref, mod_ref, bmod_ref, ng_ref)
        hb = h.astype(BF16)
        ug_ref[...] = _dot_nt(hb, win_ref[N_MLA:D_IN_PROJ, :]).astype(BF16)
        um = _dot_nt(hb, win_ref[0:n_mla, :])
        lane = lax.broadcasted_iota(jnp.int32, (TILE, 128), 1)
        um = jnp.concatenate([um[:, 0:R_Q + R_KV], jnp.where(lane < D_ROPE, um[:, R_Q + R_KV:n_mla], 0.0)], axis=1)
        um_ref[...] = um
        fill[...] = um

    first = lambda s: jnp.minimum(s, n_tiles - 1)
    second = lambda s: jnp.maximum(s - 1, 0)
    latent = lambda t: jnp.maximum(t - 1, 0)
    full = lambda shape: pl.BlockSpec(shape, lambda s: (0,) * len(shape))
    rope_spec = pl.BlockSpec((TILE, 128), lambda s: (second(s), 0))
    return pl.pallas_call(
        body, name="inproj_fwd", grid=(n_tiles + 1,),
        out_shape=(jax.ShapeDtypeStruct((seq, N_GATES), BF16), jax.ShapeDtypeStruct((tot, n_mla), F32),
                   jax.ShapeDtypeStruct((N_HEADS, seq, D_HEAD_PAD), BF16),
                   jax.ShapeDtypeStruct((N_HEADS, tot, D_HEAD_PAD), BF16),
                   jax.ShapeDtypeStruct((N_HEADS, tot, D_V), BF16)),
        in_specs=[pl.BlockSpec((TILE, D_MODEL), lambda s: (latent(first(s)), 0)), full((TILE, D_MODEL)), full((8, D_MODEL)),
                  full((8, D_MODEL)), full((1, D_MODEL)), full((D_IN_PROJ, D_MODEL)), full((1, R_Q)),
                  full((N_HEADS * D_HEAD_PAD, R_Q)), full((1, R_KV)), full((R_KV, N_HEADS * 256)), full((1, D_HEAD_PAD)),
                  full((1, D_HEAD_PAD)), rope_spec, rope_spec, rope_spec],
        out_specs=(pl.BlockSpec((TILE, N_GATES), lambda s: (latent(first(s)), 0)),
                   pl.BlockSpec((TILE, n_mla), lambda s: (first(s), 0)),
                   pl.BlockSpec((N_HEADS, TILE, D_HEAD_PAD), lambda s: (0, latent(second(s)), 0)),
                   pl.BlockSpec((N_HEADS, TILE, D_HEAD_PAD), lambda s: (0, second(s), 0)),
                   pl.BlockSpec((N_HEADS, TILE, D_V), lambda s: (0, second(s), 0))),
        scratch_shapes=[pltpu.VMEM((TILE, n_mla), F32)],
        compiler_params=pltpu.CompilerParams(dimension_semantics=("arbitrary",), vmem_limit_bytes=VMEM_LIMIT),
    )(x, ctx, modrows, bmodrows, norm_g, w_in_p, q_lora_g, w_uq_p, kv_lora_g, w_ukv, qng_p, kng_p, cos, slo, shi)


def _hosted_exchange(first, last, items, send_sems, recv_sems, local_sems):
    me = _lin(_coords())

    def remote(n, k, src, land, scatter):
        peer = _peer(k)
        return pltpu.make_async_remote_copy(
            src_ref=src.at[_lin(peer)] if scatter else src, dst_ref=land.at[me], send_sem=send_sems.at[n, k - 1],
            recv_sem=recv_sems.at[n, k - 1], device_id=peer, device_id_type=MESH)

    def local(n, src, land, scatter):
        return pltpu.make_async_copy(src.at[me] if scatter else src, land.at[me], local_sems.at[n])

    @pl.when(first)
    def _():
        for n, (src, land, scatter) in enumerate(items):
            for k in range(1, N_DEV):
                remote(n, k, src, land, scatter).start()
            local(n, src, land, scatter).start()

    @pl.when(last)
    def _():
        for n, (src, land, scatter) in enumerate(items):
            for k in range(1, N_DEV):
                peer = _peer(k)
                pltpu.make_async_remote_copy(
                    src_ref=src.at[0] if scatter else src, dst_ref=land.at[_lin(peer)], send_sem=send_sems.at[n, k - 1],
                    recv_sem=recv_sems.at[n, k - 1], device_id=peer, device_id_type=MESH).wait_recv()
            for k in range(1, N_DEV):
                remote(n, k, src, land, scatter).wait_send()
            local(n, src, land, scatter).wait()


def _attn_fwd(q, k, v, block_q, w_out_b):
    _, seq, _ = q.shape
    n_keys = k.shape[1]
    n_q = seq // block_q

    def body(q_ref, k_ref, v_ref, wo_ref, o_ref, lse_ref, wog_ref, ssem, rsem, lsem):
        h, i = pl.program_id(0), pl.program_id(1)
        _hosted_exchange((h == 0) & (i == 0), (h == N_HEADS - 1) & (i == n_q - 1), [(wo_ref, wog_ref, False)],
                         ssem, rsem, lsem)
        kb, vb = k_ref[0], v_ref[0]
        for r0 in range(0, block_q, ATTN_ROWS):
            rows = slice(r0, r0 + ATTN_ROWS)
            s = _dot_nt(q_ref[0, rows, :], kb)
            m = jnp.max(s, axis=-1, keepdims=True)
            p = jnp.exp2(s - m)
            l = jnp.sum(p, axis=-1, keepdims=True)
            o_ref[rows, :] = _dot(p.astype(BF16), vb) * (1.0 / l)
            lse_ref[0, rows, :] = m + jnp.log2(l)

    hbm = pl.BlockSpec(memory_space=pl.ANY)
    return pl.pallas_call(
        body, name="attn_fwd", grid=(N_HEADS, n_q),
        out_shape=(jax.ShapeDtypeStruct((seq, D_ATTN), F32), jax.ShapeDtypeStruct((N_HEADS, seq, 1), F32),
                   jax.ShapeDtypeStruct((N_DEV,) + w_out_b.shape, w_out_b.dtype)),
        in_specs=[pl.BlockSpec((1, block_q, D_HEAD_PAD), lambda h, i: (h, i, 0)),
                  pl.BlockSpec((1, n_keys, D_HEAD_PAD), lambda h, i: (h, 0, 0)),
                  pl.BlockSpec((1, n_keys, D_V), lambda h, i: (h, 0, 0)), hbm],
        out_specs=(pl.BlockSpec((block_q, D_V), lambda h, i: (i, h)),
                   pl.BlockSpec((1, block_q, 1), lambda h, i: (h, i, 0)), hbm),
        scratch_shapes=[pltpu.SemaphoreType.DMA((1, 7)), pltpu.SemaphoreType.DMA((1, 7)), pltpu.SemaphoreType.DMA((1,))],
        compiler_params=pltpu.CompilerParams(dimension_semantics=("arbitrary", "arbitrary"), vmem_limit_bytes=VMEM_LIMIT),
    )(q, k, v, w_out_b)


def _attn_bwd(q, k, v, o, lse, d_o, block_q, gwo_blocks, gwp):
    _, seq, _ = q.shape
    n_keys = k.shape[1]
    n_q = seq // block_q

    def body(q_ref, k_ref, v_ref, o_ref, lse_ref, do_ref, gwo_ref, gwp_ref, dq_ref, dkt_ref, dvt_ref, lwo_ref, lwp_ref,
             ssem, rsem, lsem):
        h, i = pl.program_id(0), pl.program_id(1)
        _hosted_exchange((h == 0) & (i == 0), (h == N_HEADS - 1) & (i == n_q - 1),
                         [(gwo_ref, lwo_ref, True), (gwp_ref, lwp_ref, False)], ssem, rsem, lsem)

        @pl.when(i == 0)
        def _():
            dkt_ref[...] = jnp.zeros_like(dkt_ref)
            dvt_ref[...] = jnp.zeros_like(dvt_ref)

        kb, vb = k_ref[0], v_ref[0]
        raws, ps = [], []
        for r0 in range(0, block_q, ATTN_ROWS):
            rows = slice(r0, r0 + ATTN_ROWS)
            d_out = do_ref[rows, :]
            p = jnp.exp2(_dot_nt(q_ref[0, rows, :], kb) - lse_ref[0, rows, :])
            dp = _dot_nt(d_out.astype(BF16), vb)
            delta = jnp.sum(d_out * o_ref[rows, :], axis=-1, keepdims=True)
            raw = (p * (dp - delta)).astype(BF16)
            dq_ref[0, rows, :] = _dot(raw, kb)
            raws.append(raw)
            ps.append(p.astype(BF16))
        dkt_ref[0] += _dot_tn(q_ref[0], jnp.concatenate(raws, axis=0))
        dvt_ref[0] += _dot_tn(do_ref[...].astype(BF16), jnp.concatenate(ps, axis=0))

    hbm = pl.BlockSpec(memory_space=pl.ANY)
    return pl.pallas_call(
        body, name="attn_bwd", grid=(N_HEADS, n_q),
        out_shape=(jax.ShapeDtypeStruct((N_HEADS, seq, D_HEAD_PAD), F32),
                   jax.ShapeDtypeStruct((N_HEADS, D_HEAD_PAD, n_keys), F32),
                   jax.ShapeDtypeStruct((N_HEADS, D_V, n_keys), F32),
                   jax.ShapeDtypeStruct(gwo_blocks.shape, gwo_blocks.dtype),
                   jax.ShapeDtypeStruct((N_DEV,) + gwp.shape, gwp.dtype)),
        in_specs=[pl.BlockSpec((1, block_q, D_HEAD_PAD), lambda h, i: (h, i, 0)),
                  pl.BlockSpec((1, n_keys, D_HEAD_PAD), lambda h, i: (h, 0, 0)),
                  pl.BlockSpec((1, n_keys, D_V), lambda h, i: (h, 0, 0)),
                  pl.BlockSpec((block_q, D_V), lambda h, i: (i, h)),
                  pl.BlockSpec((1, block_q, 1), lambda h, i: (h, i, 0)),
                  pl.BlockSpec((block_q, D_V), lambda h, i: (i, h)), hbm, hbm],
        out_specs=(pl.BlockSpec((1, block_q, D_HEAD_PAD), lambda h, i: (h, i, 0)),
                   pl.BlockSpec((1, D_HEAD_PAD, n_keys), lambda h, i: (h, 0, 0)),
                   pl.BlockSpec((1, D_V, n_keys), lambda h, i: (h, 0, 0)), hbm, hbm),
        scratch_shapes=[pltpu.SemaphoreType.DMA((2, 7)), pltpu.SemaphoreType.DMA((2, 7)), pltpu.SemaphoreType.DMA((2,))],
        compiler_params=pltpu.CompilerParams(dimension_semantics=("arbitrary", "arbitrary"), vmem_limit_bytes=VMEM_LIMIT),
    )(q, k, v, o, lse, d_o, gwo_blocks, gwp)


def _mix(x, target, attn, u, modrows, bmodrows, w_pool, pool_scale, w_out):
    seq = x.shape[0]
    rows = min(MIX_TILE, seq // 2)
    n_tiles = seq // rows
    rows8 = rows // HALO
    last8 = seq // HALO - 1

    n_blk = seq // Q_BLOCK
    per = rows // n_blk
    assert rows % n_blk == 0 and per % 8 == 0 and n_tiles >= 2
    n_stash = 8

    def body(x_ref, t_ref, o_hbm, ga_ref, pin_ref, hb_ref, ha_ref, gp_ref, mod_ref, bmod_ref, wp_ref, ps_ref, wo_ref,
             loss_ref, g1_ref, dgate_ref, do_hbm, dga_ref, dgp_ref, dpl_ref, gwob_ref, gwp_ref, dps_ref,
             abuf, dbuf, gwo_ref, *rest):
        stash_even, stash_odd = rest[0:n_stash], rest[n_stash:2 * n_stash]
        rsem, wsem = rest[2 * n_stash:]
        s = pl.program_id(0)

        def slab_copies(tile, slot, fetch):
            window = pl.ds(tile * per, per)
            if fetch:
                return [pltpu.make_async_copy(o_hbm.at[:, window, :], abuf.at[slot], rsem.at[slot])]
            return [pltpu.make_async_copy(dbuf.at[slot], do_hbm.at[:, window, :], wsem.at[slot])]

        def wait_all(slot, fetch):
            slab_copies(0, slot, fetch)[0].wait()

        a_slot = lax.rem(s, 2)
        b_slot = 1 - a_slot

        @pl.when(s == 0)
        def _():
            loss_ref[...] = jnp.zeros_like(loss_ref)
            dgate_ref[...] = jnp.zeros_like(dgate_ref)
            gwo_ref[...] = jnp.zeros_like(gwo_ref)
            gwp_ref[...] = jnp.zeros_like(gwp_ref)
            dps_ref[...] = jnp.zeros_like(dps_ref)
            for cp in slab_copies(0, 0, True):
                cp.start()

        @pl.when(s + 1 < n_tiles)
        def _():
            for cp in slab_copies(s + 1, b_slot, True):
                cp.start()

        @pl.when(s < n_tiles)
        def _():
            wait_all(a_slot, True)

        @pl.when(s >= 3)
        def _():
            wait_all(b_slot, False)

        gate = mod_ref[2:3, :] + bmod_ref[2:3, :]
        ps = ps_ref[...]

        def a_vector(slot):
            attn_t = jnp.swapaxes(abuf[slot], 0, 1).reshape(rows, D_ATTN)
            ga = ga_ref[...].astype(F32)
            sga = _sigmoid(ga)
            silu_ga = ga * sga
            pin = pin_ref[...].astype(F32)
            xs = jnp.concatenate([jnp.where(s > 0, hb_ref[...].astype(F32), 0.0), pin,
                                  jnp.where(s < n_tiles - 1, ha_ref[...].astype(F32), 0.0)], axis=0)
            tpos = s * rows + lax.broadcasted_iota(jnp.int32, (rows, 1), 0)
            pooled = []
            for g, w in enumerate(POOL_WINDOWS):
                sl = slice(g * GROUP_DIM, (g + 1) * GROUP_DIM)
                ws = _window_sum(xs[:, sl], w, False)[HALO:HALO + rows]
                pooled.append((ws * _inv_count(tpos, w, seq) - pin[:, sl]).astype(BF16))
            return attn_t, ga, sga, silu_ga, pooled

        def a_group_maps(pooled):
            return jnp.concatenate([_dot(pooled[g], wp_ref[g].astype(BF16)) for g in range(len(POOL_WINDOWS))], axis=1)

        def a_project(stash, yp, attn_t, ga, sga, silu_ga, pooled):
            zb_ref, _, fa_ref, sa_ref, fp_ref, sp_ref, yp_ref, pooled_ref = stash
            ypool = yp * ps
            gp = gp_ref[...].astype(F32)
            sgp = _sigmoid(gp)
            silu_gp = gp * sgp
            z = jnp.concatenate([silu_ga * attn_t, silu_gp * ypool], axis=1).astype(BF16)
            y = _dot(z, wo_ref[...])
            zb_ref[...] = z
            fa_ref[...] = attn_t * (sga * (1.0 + ga * (1.0 - sga)))
            sa_ref[...] = silu_ga
            fp_ref[...] = ypool * (sgp * (1.0 + gp * (1.0 - sgp)))
            sp_ref[...] = silu_gp
            yp_ref[...] = yp
            pooled_ref[...] = jnp.concatenate(pooled, axis=1)
            return y

        def a_loss(stash, y):
            res = x_ref[...] + gate * y - t_ref[...]
            loss_ref[...] += jnp.sum(res * res) * (0.5 / D_MODEL)
            dxn = res * (1.0 / D_MODEL)
            g1_ref[...] = dxn
            dgate_ref[...] += jnp.sum(dxn * y, axis=0, keepdims=True)
            stash[1][...] = (gate * dxn).astype(BF16)

        def b_dz(stash):
            return _dot_nt(stash[1][...], wo_ref[...])

        def b_gwo(stash):
            gwo_ref[...] += _dot_tn(stash[0][...], stash[1][...])

        def b_vector(stash, slot, dz):
            _, _, fa_ref, sa_ref, fp_ref, sp_ref, yp_ref, _ = stash
            dbra = dz[:, 0:D_ATTN]
            dbrp = dz[:, D_ATTN:]
            dga_ref[...] = (dbra * fa_ref[...]).astype(BF16)
            dbuf[slot] = jnp.swapaxes((dbra * sa_ref[...]).reshape(per, n_blk, D_ATTN), 0, 1)
            dgp_ref[...] = (dbrp * fp_ref[...]).astype(BF16)
            dyp_s = dbrp * sp_ref[...]
            dps_ref[...] += jnp.sum(dyp_s * yp_ref[...], axis=0, keepdims=True)
            return (dyp_s * ps).astype(BF16)

        def b_small(stash, dyp):
            pooled_ref = stash[7]
            for g in range(len(POOL_WINDOWS)):
                sl = slice(g * GROUP_DIM, (g + 1) * GROUP_DIM)
                gwp_ref[g] += _dot_tn(pooled_ref[:, sl], dyp[:, sl])
                dpl_ref[:, sl] = _dot_nt(dyp[:, sl], wp_ref[g].astype(BF16)).astype(BF16)

        def both(a_stash, b_stash, slot_a):
            dz = b_dz(b_stash)
            front = a_vector(slot_a)
            yp = a_group_maps(front[-1])
            b_gwo(b_stash)
            y = a_project(a_stash, yp, *front)
            dyp = b_vector(b_stash, 1 - slot_a, dz)
            a_loss(a_stash, y)
            b_small(b_stash, dyp)

        @pl.when(s == 0)
        def _():
            front = a_vector(0)
            a_loss(stash_even, a_project(stash_even, a_group_maps(front[-1]), *front))

        @pl.when((s > 0) & (s < n_tiles) & (a_slot == 0))
        def _():
            both(stash_even, stash_odd, 0)

        @pl.when((s > 0) & (s < n_tiles) & (a_slot == 1))
        def _():
            both(stash_odd, stash_even, 1)

        @pl.when(s == n_tiles)
        def _():
            last = (n_tiles - 1) % 2
            stash = stash_odd if last else stash_even
            dz = b_dz(stash)
            b_gwo(stash)
            b_small(stash, b_vector(stash, last, dz))
            gwob_ref[...] = gwo_ref[...].astype(BF16)

        @pl.when(s >= 1)
        def _():
            for cp in slab_copies(s - 1, b_slot, False):
                cp.start()

        @pl.when(s == n_tiles)
        def _():
            wait_all(b_slot, False)
            wait_all(a_slot, False)

    ta = lambda s: jnp.minimum(s, n_tiles - 1)
    tb = lambda s: jnp.maximum(s - 1, 0)
    tile = lambda w: pl.BlockSpec((rows, w), lambda s: (ta(s), 0))
    tile_b = lambda w: pl.BlockSpec((rows, w), lambda s: (tb(s), 0))
    ucol = lambda col: pl.BlockSpec((rows, 512), lambda s: (ta(s), col))
    full = lambda shape: pl.BlockSpec(shape, lambda s: (0,) * len(shape))
    stash_shapes = [pltpu.VMEM((rows, D_MODEL), BF16), pltpu.VMEM((rows, D_MODEL), BF16)] + \
        [pltpu.VMEM((rows, 512), F32)] * 5 + [pltpu.VMEM((rows, D_POOL), BF16)]
    return pl.pallas_call(
        body, name="mix_fwd_bwd", grid=(n_tiles + 1,),
        out_shape=(jax.ShapeDtypeStruct((8, 128), F32), jax.ShapeDtypeStruct((seq, D_MODEL), F32),
                   jax.ShapeDtypeStruct((1, D_MODEL), F32), jax.ShapeDtypeStruct((n_blk, Q_BLOCK, D_ATTN), F32),
                   jax.ShapeDtypeStruct((seq, D_ATTN), BF16), jax.ShapeDtypeStruct((seq, D_POOL), BF16),
                   jax.ShapeDtypeStruct((seq, D_POOL), BF16), jax.ShapeDtypeStruct((D_MODEL, D_MODEL), BF16),
                   jax.ShapeDtypeStruct((4, GROUP_DIM, GROUP_DIM), F32), jax.ShapeDtypeStruct((1, D_POOL), F32)),
        in_specs=[tile(D_MODEL), tile(D_MODEL), pl.BlockSpec(memory_space=pl.ANY), ucol(0), ucol(1),
                  pl.BlockSpec((HALO, 512), lambda s: (jnp.maximum(ta(s) * rows8 - 1, 0), 1)),
                  pl.BlockSpec((HALO, 512), lambda s: (jnp.minimum((ta(s) + 1) * rows8, last8), 1)),
                  ucol(2), full((8, D_MODEL)), full((8, D_MODEL)), full((4, GROUP_DIM, GROUP_DIM)), full((1, D_POOL)),
                  full((D_MODEL, D_MODEL))],
        out_specs=(full((8, 128)), tile(D_MODEL), full((1, D_MODEL)), pl.BlockSpec(memory_space=pl.ANY), tile_b(D_ATTN),
                   tile_b(D_POOL), tile_b(D_POOL), full((D_MODEL, D_MODEL)), full((4, GROUP_DIM, GROUP_DIM)),
                   full((1, D_POOL))),
        scratch_shapes=[pltpu.VMEM((2, n_blk, per, D_ATTN), F32), pltpu.VMEM((2, n_blk, per, D_ATTN), F32),
                        pltpu.VMEM((D_MODEL, D_MODEL), F32), *stash_shapes, *stash_shapes,
                        pltpu.SemaphoreType.DMA((2,)), pltpu.SemaphoreType.DMA((2,))],
        compiler_params=pltpu.CompilerParams(dimension_semantics=("arbitrary",), vmem_limit_bytes=VMEM_LIMIT),
    )(x, target, attn.reshape(n_blk, Q_BLOCK, D_ATTN), u, u, u, u, u, modrows, bmodrows, w_pool, pool_scale, w_out)


def _inproj_bwd(x, ctx, modrows, bmodrows, norm_g, w_in_p, q_lora_g, w_uq_p, kv_lora_g, w_ukv, qng_p, kng_p, cos, slo, shi,
                u, dq, dk, dv, dga, dgp, dpl, g1):
    seq = x.shape[0]
    n_tiles = seq // TILE + 1
    rows8 = TILE // HALO
    last8 = seq // HALO - 1

    def body(*refs):
        du_even, du_odd, dh_even, dh_odd = refs[-4:]
        gwin_ref, gwuq_ref, gwukv_ref, dqlg_ref, dkvlg_ref, dqng_ref, dkng_ref, dng_ref, dmod_ref = refs[-13:-4]
        s = pl.program_id(0)

        @pl.when(s == 0)
        def _():
            for ref in (gwin_ref, gwuq_ref, gwukv_ref, dqlg_ref, dkvlg_ref, dqng_ref, dkng_ref, dng_ref, dmod_ref,
                        du_odd, dh_even):
                ref[...] = jnp.zeros_like(ref)

        @pl.when(lax.rem(s, 2) == 0)
        def _():
            stages(s, refs[:-4], du_even, du_odd, dh_odd, dh_even)

        @pl.when(lax.rem(s, 2) == 1)
        def _():
            stages(s, refs[:-4], du_odd, du_even, dh_even, dh_odd)

    def stages(s, refs, du_ref, du_prev, dh_fill, dh_prev):
        (xb_ref, xc_ref, ctx_ref, mod_ref, bmod_ref, ng_ref, win_ref, qlg_ref, wuq_ref, kvlg_ref, wukv_ref, qng_ref, kng_ref,
         cos_ref, slo_ref, shi_ref, cq_ref, ckv_ref, kr_ref, dq_ref, dk_ref, dv_ref, dga_ref, dgp_ref, dpl_ref,
         hb_ref, ha_ref, g1_ref,
         gx_ref, gwin_ref, gwuq_ref, gwukv_ref, dqlg_ref, dkvlg_ref, dqng_ref, dkng_ref, dng_ref, dmod_ref) = refs


        i = jnp.minimum(s, n_tiles - 1)
        valid = s < n_tiles
        is_lat = (s > 0) & valid
        cq = cq_ref[...]
        rq = lax.rsqrt(jnp.mean(cq * cq, axis=-1, keepdims=True) + NORM_EPS)
        qhat = cq * rq
        qnb = (qhat * qlg_ref[...]).astype(BF16)
        q = _dot_nt(qnb, wuq_ref[...])
        ckv = ckv_ref[...]
        rkv = lax.rsqrt(jnp.mean(ckv * ckv, axis=-1, keepdims=True) + NORM_EPS)
        kvhat = ckv * rkv
        kvnb = (kvhat * kvlg_ref[...]).astype(BF16)
        kv = _dot(kvnb, wukv_ref[...])

        _, _, _, h2, _ = _modulated_input(s <= 1, xb_ref, ctx_ref, mod_ref, bmod_ref, ng_ref)
        dub = du_prev[...]
        du_g, du_m = dub[:, 0:N_GATES], dub[:, N_GATES:U_PAD]
        h2b = h2.astype(BF16)
        dh_fill[...] = _dot(du_g, win_ref[N_MLA:D_IN_PROJ, :]) + _dot(du_m, win_ref[0:U_PAD - N_GATES, :])
        gwin_ref[N_MLA:D_IN_PROJ, :] += _dot_tn(du_g, h2b)
        gwin_ref[0:N_MLA, :] += _dot_tn(du_m, h2b)[0:N_MLA]

        ctx3 = s <= 2
        xt, r, xg, _, scale = _modulated_input(ctx3, xc_ref, ctx_ref, mod_ref, bmod_ref, ng_ref)
        dh = dh_prev[...]
        dshift = jnp.sum(dh, axis=0, keepdims=True)
        dscale = jnp.sum(dh * xg, axis=0, keepdims=True)
        zero = jnp.zeros_like(dshift)
        dmod_ref[0:1, :] += jnp.where(ctx3, zero, dshift)
        dmod_ref[1:2, :] += jnp.where(ctx3, zero, dscale)
        dmod_ref[2:3, :] += jnp.where(ctx3, dshift, zero)
        dmod_ref[3:4, :] += jnp.where(ctx3, dscale, zero)
        dxg = dh * (1.0 + scale)
        xr = xt * r
        dng_ref[...] += jnp.sum(dxg * xr, axis=0, keepdims=True)
        dxn = dxg * ng_ref[...]
        gx_ref[...] = g1_ref[...] + r * (dxn - xr * jnp.mean(dxn * xr, axis=-1, keepdims=True))

        cos_t, slo_t, shi_t = cos_ref[...], slo_ref[...], shi_ref[...]
        qg = qng_ref[...]
        dq_heads = []
        dqng = jnp.zeros((1, D_HEAD_PAD), F32)
        for hd in range(N_HEADS):
            qh = q[:, hd * D_HEAD_PAD:(hd + 1) * D_HEAD_PAD]
            rr = lax.rsqrt(jnp.sum(qh * qh, axis=-1, keepdims=True) * (1.0 / D_HEAD) + NORM_EPS)
            yq = qh * rr
            dpost = jnp.where(is_lat, dq_ref[hd] * ATTN_SCALE, 0.0)
            dyn = jnp.concatenate([dpost[:, 0:D_NOPE], _rope_t(dpost[:, D_NOPE:], cos_t, slo_t, shi_t)], axis=1)
            dqng = dqng + jnp.sum(dyn * yq, axis=0, keepdims=True)
            dyg = dyn * qg
            dq_heads.append(rr * (dyg - yq * (jnp.sum(dyg * yq, axis=-1, keepdims=True) * (1.0 / D_HEAD))))
        dqng_ref[...] += dqng
        dqf = jnp.concatenate(dq_heads, axis=1).astype(BF16)

        krz = kr_ref[...]
        kr_ss = jnp.sum(krz * krz, axis=-1, keepdims=True)
        kg = kng_ref[...]
        kg_n, kg_r = kg[:, 0:D_NOPE], kg[:, D_NOPE:]
        dkv_parts = []
        dkr = jnp.zeros((TILE, 128), F32)
        dkng_n = jnp.zeros((1, D_NOPE), F32)
        dkng_r = jnp.zeros((1, 128), F32)
        for hd in range(N_HEADS):
            kn = kv[:, hd * 256:hd * 256 + D_NOPE]
            rr = lax.rsqrt((jnp.sum(kn * kn, axis=-1, keepdims=True) + kr_ss) * (1.0 / D_HEAD) + NORM_EPS)
            yn, yr = kn * rr, krz * rr
            dk_h = jnp.where(valid, jnp.transpose(dk_ref[hd]) * LN_2, 0.0)
            dpn = dk_h[:, 0:D_NOPE]
            dpr = _rope_t(dk_h[:, D_NOPE:], cos_t, slo_t, shi_t)
            dkng_n = dkng_n + jnp.sum(dpn * yn, axis=0, keepdims=True)
            dkng_r = dkng_r + jnp.sum(dpr * yr, axis=0, keepdims=True)
            dyn, dyr = dpn * kg_n, dpr * kg_r
            proj = (jnp.sum(dyn * yn, axis=-1, keepdims=True) + jnp.sum(dyr * yr, axis=-1, keepdims=True)) * (1.0 / D_HEAD)
            dkv_parts.append(rr * (dyn - yn * proj))
            dkv_parts.append(jnp.where(valid, jnp.transpose(dv_ref[hd]), 0.0))
            dkr = dkr + rr * (dyr - yr * proj)
        dkng_ref[:, 0:D_NOPE] += dkng_n
        dkng_ref[:, D_NOPE:] += dkng_r
        dkvf = jnp.concatenate(dkv_parts, axis=1).astype(BF16)

        lat_t = jnp.maximum(i - 1, 0)
        dpl_t = dpl_ref[...].astype(F32)
        ds = jnp.concatenate([jnp.where(i > 1, hb_ref[...].astype(F32), 0.0), dpl_t,
                              jnp.where(i < n_tiles - 1, ha_ref[...].astype(F32), 0.0)], axis=0)
        tpos = lat_t * TILE - HALO + lax.broadcasted_iota(jnp.int32, (TILE + 2 * HALO, 1), 0)
        for g, w in enumerate(POOL_WINDOWS):
            sl = slice(g * GROUP_DIM, (g + 1) * GROUP_DIM)
            wsum = _window_sum(ds[:, sl] * _inv_count(tpos, w, seq), w, True)[HALO:HALO + TILE]
            du_ref[:, O_PIN + g * GROUP_DIM:O_PIN + (g + 1) * GROUP_DIM] = jnp.where(
                is_lat, wsum - dpl_t[:, sl], 0.0).astype(BF16)

        du_ref[:, O_GA:O_GA + D_ATTN] = jnp.where(is_lat, dga_ref[...], jnp.zeros((), BF16))
        du_ref[:, O_GP:O_GP + D_POOL] = jnp.where(is_lat, dgp_ref[...], jnp.zeros((), BF16))
        du_ref[:, O_KR:U_PAD] = dkr.astype(BF16)

        gwuq_ref[...] += _dot_tn(dqf, qnb)
        dqn = _dot(dqf, wuq_ref[...])
        gwukv_ref[...] += _dot_tn(dkvf, kvnb)
        dkvn = _dot_nt(dkvf, wukv_ref[...])
        dqlg_ref[...] += jnp.sum(dqn * qhat, axis=0, keepdims=True)
        dqhat = dqn * qlg_ref[...]
        dcq = rq * (dqhat - qhat * jnp.mean(dqhat * qhat, axis=-1, keepdims=True))
        dkvlg_ref[...] += jnp.sum(dkvn * kvhat, axis=0, keepdims=True)
        dkvhat = dkvn * kvlg_ref[...]
        dckv = rkv * (dkvhat - kvhat * jnp.mean(dkvhat * kvhat, axis=-1, keepdims=True))
        du_ref[:, O_CQ:O_CQ + R_Q] = dcq.astype(BF16)
        du_ref[:, O_CKV:O_CKV + R_KV] = dckv.astype(BF16)

    t1 = lambda s: jnp.minimum(s, n_tiles - 1)
    t2 = lambda s: jnp.clip(s - 1, 0, n_tiles - 1)
    t3 = lambda s: jnp.clip(s - 2, 0, n_tiles - 1)
    latent = lambda t: jnp.maximum(t - 1, 0)
    lat = lambda s: (latent(t1(s)), 0)
    lat3 = lambda s: (latent(t3(s)), 0)
    full = lambda shape: pl.BlockSpec(shape, lambda s: (0,) * len(shape))
    rope_spec = pl.BlockSpec((TILE, 128), lambda s: (t1(s), 0))
    return pl.pallas_call(
        body, name="inproj_bwd", grid=(n_tiles + 2,),
        out_shape=(jax.ShapeDtypeStruct((seq, D_MODEL), F32), jax.ShapeDtypeStruct((D_IN_PROJ, D_MODEL), F32),
                   jax.ShapeDtypeStruct((N_HEADS * D_HEAD_PAD, R_Q), F32), jax.ShapeDtypeStruct((N_HEADS * 256, R_KV), F32),
                   jax.ShapeDtypeStruct((1, R_Q), F32), jax.ShapeDtypeStruct((1, R_KV), F32),
                   jax.ShapeDtypeStruct((1, D_HEAD_PAD), F32), jax.ShapeDtypeStruct((1, D_HEAD_PAD), F32),
                   jax.ShapeDtypeStruct((1, D_MODEL), F32), jax.ShapeDtypeStruct((8, D_MODEL), F32)),
        in_specs=[pl.BlockSpec((TILE, D_MODEL), lambda s: (latent(t2(s)), 0)), pl.BlockSpec((TILE, D_MODEL), lat3),
                  full((TILE, D_MODEL)), full((8, D_MODEL)), full((8, D_MODEL)),
                  full((1, D_MODEL)), full((D_IN_PROJ, D_MODEL)), full((1, R_Q)), full((N_HEADS * D_HEAD_PAD, R_Q)),
                  full((1, R_KV)), full((R_KV, N_HEADS * 256)), full((1, D_HEAD_PAD)), full((1, D_HEAD_PAD)),
                  rope_spec, rope_spec, rope_spec,
                  pl.BlockSpec((TILE, R_Q), lambda s: (t1(s), (O_CQ - N_GATES) // R_Q)),
                  pl.BlockSpec((TILE, R_KV), lambda s: (t1(s), (O_CKV - N_GATES) // R_KV)),
                  pl.BlockSpec((TILE, 128), lambda s: (t1(s), (O_KR - N_GATES) // 128)),
                  pl.BlockSpec((N_HEADS, TILE, D_HEAD_PAD), lambda s: (0, latent(t1(s)), 0)),
                  pl.BlockSpec((N_HEADS, D_HEAD_PAD, TILE), lambda s: (0, 0, t1(s))),
                  pl.BlockSpec((N_HEADS, D_V, TILE), lambda s: (0, 0, t1(s))),
                  pl.BlockSpec((TILE, D_ATTN), lat), pl.BlockSpec((TILE, D_POOL), lat), pl.BlockSpec((TILE, D_POOL), lat),
                  pl.BlockSpec((HALO, D_POOL), lambda s: (jnp.maximum((t1(s) - 1) * rows8 - 1, 0), 0)),
                  pl.BlockSpec((HALO, D_POOL), lambda s: (jnp.minimum(jnp.maximum(t1(s), 1) * rows8, last8), 0)),
                  pl.BlockSpec((TILE, D_MODEL), lat3)],
        out_specs=(pl.BlockSpec((TILE, D_MODEL), lat3), full((D_IN_PROJ, D_MODEL)), full((N_HEADS * D_HEAD_PAD, R_Q)),
                   full((N_HEADS * 256, R_KV)), full((1, R_Q)), full((1, R_KV)), full((1, D_HEAD_PAD)),
                   full((1, D_HEAD_PAD)), full((1, D_MODEL)), full((8, D_MODEL))),
        scratch_shapes=[pltpu.VMEM((TILE, U_PAD), BF16), pltpu.VMEM((TILE, U_PAD), BF16),
                        pltpu.VMEM((TILE, D_MODEL), F32), pltpu.VMEM((TILE, D_MODEL), F32)],
        compiler_params=pltpu.CompilerParams(dimension_semantics=("arbitrary",), vmem_limit_bytes=VMEM_LIMIT),
    )(x, x, ctx, modrows, bmodrows, norm_g, w_in_p, q_lora_g, w_uq_p, kv_lora_g, w_ukv, qng_p, kng_p, cos, slo, shi,
      u, u, u, dq, dk, dv, dga, dgp, dpl, dpl, dpl, g1)


SMALL_LAYOUT = ((0, D_MODEL), (1, R_Q), (2, R_KV), (3, D_HEAD_PAD), (4, D_HEAD_PAD), (5, D_POOL))
ROW_DMOD_B, ROW_DMOD_C, ROW_LOSS = 6, 9, 12


def _epilogue(parts, landed, smalls, dmod4, dgate, loss_acc, w_mod_l):
    n_a, n_l, n_s = len(parts), len(landed), len(smalls)
    n_mod = w_mod_l.shape[1]
    blk = [p.shape[1:] for p in parts]
    small_out = [D_MODEL, R_Q, R_KV, D_HEAD, D_HEAD, D_POOL]

    def body(*refs):
        part_refs = refs[0:n_a]
        land_refs = refs[n_a:n_a + n_l]
        small_in = refs[n_a + n_l:n_a + n_l + n_s]
        dmod4_ref, dgate_ref, loss_ref, wmod_ref = refs[n_a + n_l + n_s:n_a + n_l + n_s + 4]
        outs = refs[n_a + n_l + n_s + 4:]
        red_refs = outs[0:n_a]
        landsum_refs = outs[n_a:n_a + n_l]
        ccg_ref = outs[n_a + n_l]
        sum_refs = outs[n_a + n_l + 1:n_a + n_l + 1 + n_s]
        gbmod_ref, dmy_ref, lossout_ref = outs[n_a + n_l + 1 + n_s:n_a + n_l + 4 + n_s]
        scr = outs[n_a + n_l + 4 + n_s:]
        recv1, own1, sum1b, recv2 = scr[0:n_a], scr[n_a:2 * n_a], scr[2 * n_a:3 * n_a], scr[3 * n_a:4 * n_a]
        small_ref, smallg_ref, tot_ref, cc_ref = scr[4 * n_a:4 * n_a + 4]
        ssem1, rsem1, lsem, ssem2, rsem2, ssem_s, rsem_s, ssem_cc, rsem_cc = scr[4 * n_a + 4:]
        x, y, c = _coords()
        me = (x, y, c)
        sibling = (x, y, 1 - c)

        small_ref[...] = jnp.zeros_like(small_ref)
        for ref, (row, width) in zip(small_in, SMALL_LAYOUT):
            small_ref[row:row + 1, 0:width] = ref[...]
        small_ref[ROW_DMOD_B:ROW_DMOD_B + 2, :] = dmod4_ref[0:2, :]
        small_ref[ROW_DMOD_B + 2:ROW_DMOD_B + 3, :] = dgate_ref[...]
        small_ref[ROW_DMOD_C:ROW_DMOD_C + 2, :] = dmod4_ref[2:4, :]
        small_ref[ROW_LOSS:ROW_LOSS + 1, 0:128] = loss_ref[0:1, :]
        smallg_ref[_lin(me)] = small_ref[...]
        s_recv, s_send = _gather_to_all(small_ref, smallg_ref, ssem_s, rsem_s)

        stage1, own_copies = [], []
        for a in range(n_a):
            for b in range(4):
                dev = 4 * (b >> 1) + 2 * (b & 1)
                stage1.append(pltpu.make_async_remote_copy(
                    src_ref=part_refs[a].at[dev + (1 - c)], dst_ref=recv1[a].at[b],
                    send_sem=ssem1.at[a, b], recv_sem=rsem1.at[a, b], device_id=sibling, device_id_type=MESH))
                own_copies.append(pltpu.make_async_copy(part_refs[a].at[dev + c], own1[a].at[b], lsem.at[a, b]))
                stage1[-1].start()
                own_copies[-1].start()

        s_recv()
        tot = smallg_ref[0]
        for d in range(1, N_DEV):
            tot = tot + smallg_ref[d]
        tot_ref[...] = tot
        for ref, (row, _), width in zip(sum_refs, SMALL_LAYOUT, small_out):
            ref[...] = tot_ref[row:row + 1, 0:width]
        lossout_ref[...] = tot_ref[8:16, 0:128]
        lin = _lin(me)

        def my_columns(three_rows):
            v = jnp.concatenate(three_rows, axis=1)
            out = jnp.zeros((1, n_mod), F32)
            for d in range(N_DEV):
                out = out + jnp.where(lin == d, v[:, d * n_mod:(d + 1) * n_mod], 0.0)
            return out

        rows3 = lambda ref, r0: [ref[r0 + t:r0 + t + 1, :] for t in range(3)]
        dmodc = rows3(tot_ref, ROW_DMOD_C)
        gbmod_ref[...] = jnp.concatenate(rows3(tot_ref, ROW_DMOD_B), axis=1) + jnp.concatenate(dmodc, axis=1)
        dmy_ref[...] = jnp.zeros_like(dmy_ref)
        for b in range(N_DEV):
            dmy_ref[b:b + 1, :] = my_columns(rows3(smallg_ref.at[b], ROW_DMOD_B))
        dmy = my_columns(dmodc)
        dmy_ref[N_DEV:N_DEV + 1, :] = dmy
        part = lax.dot_general(jnp.broadcast_to(dmy, (8, n_mod)), wmod_ref[...], (((1,), (1,)), ((), ())),
                               precision=HIGHEST, preferred_element_type=F32)

        cc_ref[...] = part
        ccg_ref[_lin(me)] = part
        cc_recv, cc_send = _gather_to_all(cc_ref, ccg_ref, ssem_cc, rsem_cc)

        stage2 = []
        for a in range(n_a):
            for b in range(4):
                stage1[4 * a + b].wait_recv()
                own_copies[4 * a + b].wait()
                sum1b[a][b] = (own1[a][b] + recv1[a][b]).astype(BF16)
            for k in (1, 2, 3):
                tx, ty = _flip(x, (k >> 1) & 1), _flip(y, k & 1)
                stage2.append(pltpu.make_async_remote_copy(
                    src_ref=sum1b[a].at[2 * tx + ty], dst_ref=recv2[a].at[k - 1], send_sem=ssem2.at[a, k - 1],
                    recv_sem=rsem2.at[a, k - 1], device_id=(tx, ty, c), device_id_type=MESH))
                stage2[-1].start()
        for a in range(n_a):
            own = own1[a][2 * x + y] + recv1[a][2 * x + y]
            for k in (1, 2, 3):
                stage2[3 * a + k - 1].wait_recv()
                own = own + recv2[a][k - 1].astype(F32)
            red_refs[a][...] = own

        for ref, out in zip(land_refs, landsum_refs):
            acc = ref[0].astype(F32)
            for d in range(1, N_DEV):
                acc = acc + ref[d].astype(F32)
            out[...] = acc
        cc_recv()
        for cp in stage1 + stage2:
            cp.wait_send()
        s_send()
        cc_send()

    vm = pl.BlockSpec(memory_space=pltpu.VMEM)
    sds = jax.ShapeDtypeStruct
    return pl.pallas_call(
        body, name="epilogue_reduce",
        out_shape=(*[sds(s, F32) for s in blk], *[sds(a.shape[1:], F32) for a in landed], sds((N_DEV, 8, D_MODEL), F32),
                   *[sds((1, w), F32) for w in small_out], sds((1, 3 * D_MODEL), F32), sds((16, n_mod), F32),
                   sds((8, 128), F32)),
        in_specs=[pl.BlockSpec(memory_space=pl.ANY)] * n_a + [vm] * (n_l + n_s + 4),
        out_specs=tuple([vm] * (n_a + n_l + n_s + 4)),
        scratch_shapes=[*[pltpu.VMEM((4,) + s, F32) for s in blk], *[pltpu.VMEM((4,) + s, F32) for s in blk],
                        *[pltpu.VMEM((4,) + s, BF16) for s in blk], *[pltpu.VMEM((3,) + s, BF16) for s in blk],
                        pltpu.VMEM((SMALL_ROWS, D_MODEL), F32), pltpu.VMEM((N_DEV, SMALL_ROWS, D_MODEL), F32),
                        pltpu.VMEM((SMALL_ROWS, D_MODEL), F32), pltpu.VMEM((8, D_MODEL), F32),
                        pltpu.SemaphoreType.DMA((n_a, 4)), pltpu.SemaphoreType.DMA((n_a, 4)), pltpu.SemaphoreType.DMA((n_a, 4)),
                        pltpu.SemaphoreType.DMA((n_a, 3)), pltpu.SemaphoreType.DMA((n_a, 3)),
                        pltpu.SemaphoreType.DMA((7,)), pltpu.SemaphoreType.DMA((7,)),
                        pltpu.SemaphoreType.DMA((7,)), pltpu.SemaphoreType.DMA((7,))],
        compiler_params=pltpu.CompilerParams(vmem_limit_bytes=VMEM_LIMIT),
    )(*parts, *landed, *smalls, dmod4, dgate, loss_acc, w_mod_l)


def _adamw(weights, grads, ms, vs, c_all_t, dmod_my, p2g):
    n = len(weights)

    def body(*refs):
        w_refs = refs[0:n]
        g_in = refs[n:2 * n - 2]
        m_refs = refs[2 * n - 2:3 * n - 2]
        v_refs = refs[3 * n - 2:4 * n - 2]
        cat_ref, dmod_ref, p2g_ref = refs[4 * n - 2:4 * n + 1]
        outs = refs[4 * n + 1:]
        g_refs, d_refs, nm_refs, nv_refs = outs[0:n], outs[n:2 * n], outs[2 * n:3 * n], outs[3 * n:4 * n]
        gcc_ref, gwm_ref = g_refs[0], g_refs[1]

        part = p2g_ref[0, 0:1, :]
        for d in range(1, N_DEV):
            part = part + p2g_ref[d, 0:1, :]
        cc = w_refs[0][...]
        sg = _sigmoid(cc)
        gcc_ref[...] = part * (sg * (1.0 + cc * (1.0 - sg)))
        cat = cat_ref[...]
        gwm_ref[...] = lax.dot_general(cat * _sigmoid(cat), dmod_ref[...], (((1,), (0,)), ((), ())), precision=HIGHEST,
                                       preferred_element_type=F32)
        for idx in range(n):
            if idx >= 2:
                g_refs[idx][...] = g_in[idx - 2][...]
            g = g_refs[idx][...]
            m = ADAM_B1 * m_refs[idx][...] + (1.0 - ADAM_B1) * g
            v = ADAM_B2 * v_refs[idx][...] + (1.0 - ADAM_B2) * (g * g)
            m_hat = m / (1.0 - ADAM_B1 ** ADAM_STEP)
            v_hat = v / (1.0 - ADAM_B2 ** ADAM_STEP)
            d_refs[idx][...] = -ADAM_LR * (m_hat / (jnp.sqrt(v_hat) + ADAM_EPS) + ADAM_WD * w_refs[idx][...])
            nm_refs[idx][...] = m
            nv_refs[idx][...] = v

    vm = pl.BlockSpec(memory_space=pltpu.VMEM)
    shapes = [jax.ShapeDtypeStruct(w.shape, F32) for w in weights]
    args = list(weights) + list(grads[2:]) + list(ms) + list(vs) + [c_all_t, dmod_my, p2g]
    out_shape = tuple(shapes * 4)
    return pl.pallas_call(
        body, name="adamw_update", out_shape=out_shape, in_specs=[vm] * len(args), out_specs=tuple([vm] * len(out_shape)),
        compiler_params=pltpu.CompilerParams(vmem_limit_bytes=VMEM_LIMIT),
    )(*args)


def _rope_tables(seq):
    rows = seq // GRID_W
    row = np.repeat(np.arange(rows, dtype=np.float32), GRID_W)
    col = np.tile(np.arange(GRID_W, dtype=np.float32), rows)
    n_freq = D_ROPE // 4
    inv = (np.float32(ROPE_BASE) ** (-np.arange(n_freq, dtype=np.float32) / np.float32(n_freq))).astype(np.float32)
    ang_r = (row[:, None] * inv).astype(np.float32)
    ang_c = (col[:, None] * inv).astype(np.float32)
    ang = np.concatenate([ang_r, ang_r, ang_c, ang_c], axis=-1)
    cos, sin = np.cos(ang).astype(np.float32), np.sin(ang).astype(np.float32)
    low = (np.arange(D_ROPE) % 32) < 16

    def table(t, fill):
        out = np.full((TILE + seq, 128), fill, np.float32)
        out[TILE:, 0:D_ROPE] = t
        return jnp.asarray(out)

    return table(cos, 1.0), table(np.where(low, -sin, 0.0), 0.0), table(np.where(low, 0.0, sin), 0.0)


def kernel(x, c, ctx, c_ctx, w_mod, b_mod, norm_g, w_in, q_lora_g, w_uq, kv_lora_g, w_ukv, q_norm_g, k_norm_g, w_pool, pool_scale, w_out, loss_target, m_c_ctx, m_w_mod, m_b_mod, m_norm_g, m_w_in, m_q_lora_g, m_w_uq, m_kv_lora_g, m_w_ukv, m_q_norm_g, m_k_norm_g, m_w_pool, m_pool_scale, m_w_out, v_c_ctx, v_w_mod, v_b_mod, v_norm_g, v_w_in, v_q_lora_g, v_w_uq, v_kv_lora_g, v_w_ukv, v_q_norm_g, v_k_norm_g, v_w_pool, v_pool_scale, v_w_out):
    seq = x.shape[1]
    assert ctx.shape[1] == TILE and seq % TILE == 0 and seq % GRID_W == 0
    ix, iy, ic = lax.axis_index("x"), lax.axis_index("y"), lax.axis_index("c")
    me = 4 * ix + 2 * iy + ic
    x2, ctx2, tgt2 = x[0], ctx[0], loss_target[0]
    w_mod_l, w_ukv_l, w_out_l = w_mod[0], w_ukv[0], w_out[0]
    c_ctx_row = c_ctx.reshape(1, D_MODEL)

    tr = lambda a: jnp.transpose(a[0])
    w_in_tl, w_uq_tl = tr(w_in), tr(w_uq)
    cg, modg, wg_in, wg_uq, wg_ukv = _prologue(
        c, c_ctx_row, w_mod_l,
        [w_in_tl, w_uq_tl, w_ukv_l])
    mod_all = jnp.transpose(modg, (1, 0, 2)).reshape(16, 3 * D_MODEL)
    mod_b = lax.dynamic_slice_in_dim(mod_all, me, 1, axis=0).reshape(3, D_MODEL)
    mod_c = mod_all[8].reshape(3, D_MODEL)
    modrows = jnp.concatenate([mod_b, mod_c[0:2], jnp.zeros((3, D_MODEL), F32)], axis=0)
    b3 = b_mod.reshape(3, D_MODEL)
    bmodrows = jnp.concatenate([b3, b3[0:2], jnp.zeros((3, D_MODEL), F32)], axis=0)

    w_in_p = wg_in.reshape(D_IN_PROJ, D_MODEL)
    w_uq_p = jnp.concatenate([wg_uq.reshape(N_HEADS, D_HEAD, R_Q), jnp.zeros((N_HEADS, D_HEAD_PAD - D_HEAD, R_Q), BF16)],
                             axis=1).reshape(N_HEADS * D_HEAD_PAD, R_Q)
    w_ukv_f = jnp.transpose(wg_ukv, (1, 0, 2)).reshape(R_KV, N_HEADS * 256)
    qng_p = jnp.concatenate([q_norm_g, jnp.zeros((1, D_HEAD_PAD - D_HEAD), F32)], axis=1)
    kng_p = jnp.concatenate([k_norm_g, jnp.zeros((1, D_HEAD_PAD - D_HEAD), F32)], axis=1)
    cos, slo, shi = _rope_tables(seq)

    u_gates, u_mla, q, k, v = _inproj_fwd(x2, ctx2, modrows, bmodrows, norm_g, w_in_p, q_lora_g, w_uq_p, kv_lora_g, w_ukv_f,
                             qng_p, kng_p, cos, slo, shi)
    attn, lse, wg_out = _attn_fwd(q, k, v, min(2048, seq), w_out_l.astype(BF16))
    (loss_acc, g1, dgate, d_attn, dga, dgp, dpl, gwo_b, gwp, dps) = _mix(
        x2, tgt2, attn, u_gates, modrows, bmodrows, w_pool[0], pool_scale, wg_out.reshape(D_MODEL, D_MODEL))

    blocks = lambda a: a.reshape((N_DEV, a.shape[0] // N_DEV) + a.shape[1:])
    dq, dk, dv, land_wo, land_wp = _attn_bwd(q, k, v, attn, lse, d_attn.reshape(seq, D_ATTN), min(1024, seq), blocks(gwo_b),
                                             gwp.reshape(4 * GROUP_DIM, GROUP_DIM))
    (grad_x, gwin_t, gwuq_p, gwukv_t, dqlg, dkvlg, dqng, dkng, dng, dmod4) = _inproj_bwd(
        x2, ctx2, modrows, bmodrows, norm_g, w_in_p, q_lora_g, w_uq_p, kv_lora_g, w_ukv_f, qng_p, kng_p, cos, slo, shi,
        u_mla, dq, dk, dv, dga, dgp, dpl, g1)

    gwuq_t = gwuq_p.reshape(N_HEADS, D_HEAD_PAD, R_Q)[:, 0:D_HEAD].reshape(N_HEADS * D_HEAD, R_Q)
    parts = [blocks(gwin_t), blocks(gwuq_t), blocks(gwukv_t)]
    (r_win, r_wuq, r_wukv, r_wout, g_w_pool, ccg, g_ng, g_qlg, g_kvlg, g_qng, g_kng, g_ps, g_bmod, dmod_my,
     loss_rows) = _epilogue(parts, [land_wo, land_wp], [dng, dqlg, dkvlg, dqng, dkng, dps], dmod4, dgate, loss_acc, w_mod_l)

    g_w_ukv = jnp.transpose(r_wukv)
    c_rows = cg[:, 0, :]
    c_all_t = jnp.transpose(jnp.concatenate([c_rows, c_ctx_row, jnp.zeros((16 - N_DEV - 1, D_MODEL), F32)], axis=0))

    weights = [c_ctx_row, w_mod_l, b_mod, norm_g, w_in_tl, q_lora_g, w_uq_tl, kv_lora_g, w_ukv_l, q_norm_g, k_norm_g,
               w_pool.reshape(4 * GROUP_DIM, GROUP_DIM), pool_scale, w_out_l]
    grads = [None, None, g_bmod, g_ng, r_win, g_qlg, r_wuq, g_kvlg, g_w_ukv, g_qng, g_kng, g_w_pool, g_ps, r_wout]
    ms = [m_c_ctx.reshape(1, D_MODEL), m_w_mod[0], m_b_mod, m_norm_g, tr(m_w_in), m_q_lora_g, tr(m_w_uq), m_kv_lora_g,
          m_w_ukv[0], m_q_norm_g, m_k_norm_g, m_w_pool.reshape(4 * GROUP_DIM, GROUP_DIM), m_pool_scale, m_w_out[0]]
    vs = [v_c_ctx.reshape(1, D_MODEL), v_w_mod[0], v_b_mod, v_norm_g, tr(v_w_in), v_q_lora_g, tr(v_w_uq), v_kv_lora_g,
          v_w_ukv[0], v_q_norm_g, v_k_norm_g, v_w_pool.reshape(4 * GROUP_DIM, GROUP_DIM), v_pool_scale, v_w_out[0]]
    outs = _adamw(weights, grads, ms, vs, c_all_t, dmod_my, ccg)
    n = len(weights)
    grads, deltas, new_m, new_v = outs[0:n], outs[n:2 * n], outs[2 * n:3 * n], outs[3 * n:4 * n]

    loss = loss_rows[ROW_LOSS - 8, 0]
    final = [c_ctx.shape, w_mod.shape, b_mod.shape, norm_g.shape, w_in.shape, q_lora_g.shape, w_uq.shape, kv_lora_g.shape,
             w_ukv.shape, q_norm_g.shape, k_norm_g.shape, w_pool.shape, pool_scale.shape, w_out.shape]
    transposed = (4, 6)

    def shaped(arrs):
        return [(jnp.transpose(a) if i in transposed else a).reshape(s) for i, (a, s) in enumerate(zip(arrs, final))]

    return (loss, grad_x[None], *shaped(grads), *shaped(deltas), *shaped(new_m), *shaped(new_v))
```

```python
import numpy as np

import jax
import jax.numpy as jnp
from jax import lax
from jax.experimental import pallas as pl
from jax.experimental.pallas import tpu as pltpu

F32 = jnp.float32
BF16 = jnp.bfloat16
MESH = pl.DeviceIdType.MESH
HIGHEST = lax.Precision.HIGHEST

D_MODEL = 1024
N_HEADS = 4
D_NOPE = 128
D_ROPE = 64
D_HEAD = D_NOPE + D_ROPE
D_HEAD_PAD = 256
D_V = 128
R_Q = 256
R_KV = 128
D_ATTN = 512
D_POOL = 512
POOL_WINDOWS = (2, 4, 8, 16)
GROUP_DIM = 128
GRID_W = 64
ROPE_BASE = 10000.0
NORM_EPS = 1e-6
ATTN_SCALE = D_HEAD ** -0.5
LOG2_E = 1.4426950408889634
LN_2 = 0.6931471805599453
ATTN_ROWS = 256
D_IN_PROJ = 1984
U_PAD = 2048
O_GA, O_PIN, O_GP, O_CQ, O_CKV, O_KR = 0, 512, 1024, 1536, 1792, 1920
TILE = 256
MIX_TILE = 512
Q_BLOCK = 128
HALO = 16
N_GATES = 1536
N_MLA = D_IN_PROJ - N_GATES
N_DEV = 8
VMEM_LIMIT = 56 * 1024 * 1024

ADAM_LR = 0.001
ADAM_B1 = 0.9
ADAM_B2 = 0.999
ADAM_EPS = 1e-08
ADAM_WD = 0.01
ADAM_STEP = 10

SMALL_ROWS = 16


def _dot(a, b):
    return lax.dot_general(a, b, (((1,), (0,)), ((), ())), preferred_element_type=F32)


def _dot_nt(a, b):
    return lax.dot_general(a, b, (((1,), (1,)), ((), ())), preferred_element_type=F32)


def _dot_tn(a, b):
    return lax.dot_general(a, b, (((0,), (0,)), ((), ())), preferred_element_type=F32)


def _sigmoid(x):
    return 1.0 / (1.0 + jnp.exp(-x))


def _rope(p, cos, slo, shi):
    return p * cos + pltpu.roll(p, 112, 1) * slo + pltpu.roll(p, 16, 1) * shi


def _rope_t(d, cos, slo, shi):
    return d * cos + pltpu.roll(d * slo, 16, 1) + pltpu.roll(d * shi, 112, 1)


def _shift_rows(a, k):
    n = a.shape[0]
    return pltpu.roll(a, (-k) % n, 0)


def _window_sum(x, w, transposed):
    s = (x + _shift_rows(x, 1)) if transposed else (_shift_rows(x, -1) + x)
    step = 1
    while 2 * step < w:
        s = _shift_rows(s, -step) + _shift_rows(s, step)
        step *= 2
    return s


def _inv_count(tpos, w, seq):
    lo = jnp.maximum(tpos - w // 2, 0)
    hi = jnp.minimum(tpos - w // 2 + w, seq)
    return 1.0 / jnp.maximum(hi - lo, 1).astype(F32)


def _coords():
    return lax.axis_index("x"), lax.axis_index("y"), lax.axis_index("c")


def _flip(v, bit):
    return (1 - v) if bit else v


def _peer(k):
    x, y, c = _coords()
    return (_flip(x, (k >> 2) & 1), _flip(y, (k >> 1) & 1), _flip(c, k & 1))


def _lin(p):
    return 4 * p[0] + 2 * p[1] + p[2]


def _gather_to_all(src_ref, slots_ref, send_sems, recv_sems):
    me = _coords()
    copies = []
    for k in range(1, N_DEV):
        cp = pltpu.make_async_remote_copy(
            src_ref=src_ref, dst_ref=slots_ref.at[_lin(me)], send_sem=send_sems.at[k - 1], recv_sem=recv_sems.at[k - 1],
            device_id=_peer(k), device_id_type=MESH)
        cp.start()
        copies.append(cp)

    def wait_recv():
        for k in range(1, N_DEV):
            pltpu.make_async_remote_copy(
                src_ref=src_ref, dst_ref=slots_ref.at[_lin(_peer(k))], send_sem=send_sems.at[k - 1],
                recv_sem=recv_sems.at[k - 1], device_id=_peer(k), device_id_type=MESH).wait_recv()

    def wait_send():
        for cp in copies:
            cp.wait_send()

    return wait_recv, wait_send


def _prologue(c8, cctx8, w_mod_l, shards):
    n_mod = w_mod_l.shape[1]
    n_w = len(shards)

    def body(c_ref, cctx_ref, wmod_ref, *refs):
        w_refs = refs[0:n_w]
        cg_ref, modg_ref = refs[n_w], refs[n_w + 1]
        wg_refs = refs[n_w + 2:2 * n_w + 2]
        modblk_ref = refs[2 * n_w + 2]
        wb_refs = refs[2 * n_w + 3:3 * n_w + 3]
        c8_ref = refs[3 * n_w + 3]
        ssem_w, rsem_w, ssem_c, rsem_c, ssem_m, rsem_m = refs[3 * n_w + 4:]
        x, y, c = _coords()
        me = (x, y, c)
        sibling = (x, y, 1 - c)
        chips = [(1 - x, y), (x, 1 - y), (1 - x, 1 - y)]

        def wcopies(k, block, to, own=False):
            out = []
            for a in range(n_w):
                slot = wg_refs[a].at[_lin(block)]
                out.append(pltpu.make_async_remote_copy(
                    src_ref=wb_refs[a] if own else slot, dst_ref=slot, send_sem=ssem_w.at[a, k], recv_sem=rsem_w.at[a, k],
                    device_id=to, device_id_type=MESH))
            return out

        c8_ref[...] = jnp.broadcast_to(c_ref[...], (8, D_MODEL))
        cg_ref[_lin(me)] = c8_ref[...]
        c_recv, c_send = _gather_to_all(c8_ref, cg_ref, ssem_c, rsem_c)

        for a in range(n_w):
            wb_refs[a][...] = w_refs[a][...].astype(BF16)
        first = wcopies(0, me, sibling, own=True)
        for j, chip in enumerate(chips):
            first += wcopies(1 + j, me, (*chip, c), own=True)
        late = [idx for idx in range(len(first)) if idx % n_w == 0 and idx >= n_w]
        for idx, cp in enumerate(first):
            if idx not in late:
                cp.start()
        for a in range(n_w):
            wg_refs[a][_lin(me)] = wb_refs[a][...]

        c_recv()
        row = lax.broadcasted_iota(jnp.int32, (8, D_MODEL), 0)
        c_all = jnp.zeros((8, D_MODEL), F32)
        for d in range(N_DEV):
            c_all = c_all + jnp.where(row == d, cg_ref[d], 0.0)
        cc = jnp.broadcast_to(cctx_ref[...], (8, D_MODEL))
        a = jnp.concatenate([c_all * _sigmoid(c_all), cc * _sigmoid(cc)], axis=0)
        modblk_ref[...] = lax.dot_general(a, wmod_ref[...], (((1,), (0,)), ((), ())), precision=HIGHEST,
                                          preferred_element_type=F32)
        modg_ref[_lin(me)] = modblk_ref[...]
        m_recv, m_send = _gather_to_all(modblk_ref, modg_ref, ssem_m, rsem_m)
        for idx in late:
            first[idx].start()
        m_recv()

        passed = []
        for j, chip in enumerate(chips):
            for cp in wcopies(1 + j, (*chip, c), me):
                cp.wait_recv()
            fwd = wcopies(4 + j, (*chip, c), sibling)
            for cp in fwd:
                cp.start()
            passed += fwd
        for cp in wcopies(0, sibling, me):
            cp.wait_recv()
        for j, chip in enumerate(chips):
            for cp in wcopies(4 + j, (*chip, 1 - c), me):
                cp.wait_recv()
        for cp in first + passed:
            cp.wait_send()
        c_send()
        m_send()

    vm = pl.BlockSpec(memory_space=pltpu.VMEM)
    return pl.pallas_call(
        body, name="prologue_gather",
        out_shape=(jax.ShapeDtypeStruct((N_DEV, 8, D_MODEL), F32), jax.ShapeDtypeStruct((N_DEV, 16, n_mod), F32),
                   *[jax.ShapeDtypeStruct((N_DEV,) + s.shape, BF16) for s in shards]),
        in_specs=[vm] * (3 + n_w), out_specs=tuple([vm] * (2 + n_w)),
        scratch_shapes=[pltpu.VMEM((16, n_mod), F32), *[pltpu.VMEM(s.shape, BF16) for s in shards],
                        pltpu.VMEM((8, D_MODEL), F32),
                        pltpu.SemaphoreType.DMA((n_w, 7)), pltpu.SemaphoreType.DMA((n_w, 7)),
                        pltpu.SemaphoreType.DMA((7,)), pltpu.SemaphoreType.DMA((7,)),
                        pltpu.SemaphoreType.DMA((7,)), pltpu.SemaphoreType.DMA((7,))],
        compiler_params=pltpu.CompilerParams(vmem_limit_bytes=VMEM_LIMIT),
    )(c8, cctx8, w_mod_l, *shards)


def _modulated_input(is_ctx, x_ref, ctx_ref, mod_ref, bmod_ref, ng_ref):
    xt = jnp.where(is_ctx, ctx_ref[...], x_ref[...])
    shift = jnp.where(is_ctx, mod_ref[3:4, :] + bmod_ref[3:4, :], mod_ref[0:1, :] + bmod_ref[0:1, :])
    scale = jnp.where(is_ctx, mod_ref[4:5, :] + bmod_ref[4:5, :], mod_ref[1:2, :] + bmod_ref[1:2, :])
    r = lax.rsqrt(jnp.mean(xt * xt, axis=-1, keepdims=True) + NORM_EPS)
    xg = (xt * r) * ng_ref[...]
    h = xg * (1.0 + scale) + shift
    return xt, r, xg, h, scale


def _inproj_fwd(x, ctx, modrows, bmodrows, norm_g, w_in_p, q_lora_g, w_uq_p, kv_lora_g, w_ukv, qng_p, kng_p, cos, slo, shi):
    seq = x.shape[0]
    n_tiles = seq // TILE + 1
    tot = seq + TILE

    n_mla = R_Q + R_KV + 128

    def body(x_ref, ctx_ref, mod_ref, bmod_ref, ng_ref, win_ref, qlg_ref, wuq_ref, kvlg_ref, wukv_ref, qng_ref, kng_ref,
             cos_ref, slo_ref, shi_ref, ug_ref, um_ref, q_ref, k_ref, v_ref, stash):
        s = pl.program_id(0)

        @pl.when(s == 0)
        def _():
            stash[...] = jnp.zeros_like(stash)

        refs = (x_ref, ctx_ref, mod_ref, bmod_ref, ng_ref, win_ref, qlg_ref, wuq_ref, kvlg_ref, wukv_ref, qng_ref, kng_ref,
                cos_ref, slo_ref, shi_ref, ug_ref, um_ref, q_ref, k_ref, v_ref)
        stages(s, refs, stash, stash)

    def stages(s, refs, fill, prev_ref):
        (x_ref, ctx_ref, mod_ref, bmod_ref, ng_ref, win_ref, qlg_ref, wuq_ref, kvlg_ref, wukv_ref, qng_ref, kng_ref,
         cos_ref, slo_ref, shi_ref, ug_ref, um_ref, q_ref, k_ref, v_ref) = refs
        cos_t, slo_t, shi_t = cos_ref[...], slo_ref[...], shi_ref[...]
        prev = prev_ref[...]
        cq = prev[:, 0:R_Q]
        qn = cq * lax.rsqrt(jnp.mean(cq * cq, axis=-1, keepdims=True) + NORM_EPS) * qlg_ref[...]
        q = _dot_nt(qn.astype(BF16), wuq_ref[...])
        qg = qng_ref[...] * (ATTN_SCALE * LOG2_E)
        for hd in range(N_HEADS):
            qh = q[:, hd * D_HEAD_PAD:(hd + 1) * D_HEAD_PAD]
            rr = lax.rsqrt(jnp.sum(qh * qh, axis=-1, keepdims=True) * (1.0 / D_HEAD) + NORM_EPS)
            qy = qh * rr * qg
            q_ref[hd, :, 0:D_NOPE] = qy[:, 0:D_NOPE].astype(BF16)
            q_ref[hd, :, D_NOPE:D_HEAD_PAD] = _rope(qy[:, D_NOPE:D_HEAD_PAD], cos_t, slo_t, shi_t).astype(BF16)

        ckv = prev[:, R_Q:R_Q + R_KV]
        kvn = ckv * lax.rsqrt(jnp.mean(ckv * ckv, axis=-1, keepdims=True) + NORM_EPS) * kvlg_ref[...]
        kv = _dot(kvn.astype(BF16), wukv_ref[...])
        krz = prev[:, R_Q + R_KV:n_mla]
        kr_ss = jnp.sum(krz * krz, axis=-1, keepdims=True)
        kg = kng_ref[...]
        for hd in range(N_HEADS):
            kn = kv[:, hd * 256:hd * 256 + D_NOPE]
            rr = lax.rsqrt((jnp.sum(kn * kn, axis=-1, keepdims=True) + kr_ss) * (1.0 / D_HEAD) + NORM_EPS)
            k_ref[hd, :, 0:D_NOPE] = (kn * rr * kg[:, 0:D_NOPE]).astype(BF16)
            k_ref[hd, :, D_NOPE:D_HEAD_PAD] = _rope(krz * rr * kg[:, D_NOPE:D_HEAD_PAD], cos_t, slo_t, shi_t).astype(BF16)
            v_ref[hd] = kv[:, hd * 256 + D_NOPE:(hd + 1) * 256].astype(BF16)

        _, _, _, h, _ = _modulated_input(s == 0, x_ref, ctx_ref, mod_ref, bmod_ref, ng_ref)
        hb = h.astype(BF16)
        ug_ref[...] = _dot_nt(hb, win_ref[N_MLA:D_IN_PROJ, :]).astype(BF16)
        um = _dot_nt(hb, win_ref[0:n_mla, :])
        lane = lax.broadcasted_iota(jnp.int32, (TILE, 128), 1)
        um = jnp.concatenate([um[:, 0:R_Q + R_KV], jnp.where(lane < D_ROPE, um[:, R_Q + R_KV:n_mla], 0.0)], axis=1)
        um_ref[...] = um
        fill[...] = um

    first = lambda s: jnp.minimum(s, n_tiles - 1)
    second = lambda s: jnp.maximum(s - 1, 0)
    latent = lambda t: jnp.maximum(t - 1, 0)
    full = lambda shape: pl.BlockSpec(shape, lambda s: (0,) * len(shape))
    rope_spec = pl.BlockSpec((TILE, 128), lambda s: (second(s), 0))
    return pl.pallas_call(
        body, name="inproj_fwd", grid=(n_tiles + 1,),
        out_shape=(jax.ShapeDtypeStruct((seq, N_GATES), BF16), jax.ShapeDtypeStruct((tot, n_mla), F32),
                   jax.ShapeDtypeStruct((N_HEADS, seq, D_HEAD_PAD), BF16),
                   jax.ShapeDtypeStruct((N_HEADS, tot, D_HEAD_PAD), BF16),
                   jax.ShapeDtypeStruct((N_HEADS, tot, D_V), BF16)),
        in_specs=[pl.BlockSpec((TILE, D_MODEL), lambda s: (latent(first(s)), 0)), full((TILE, D_MODEL)), full((8, D_MODEL)),
                  full((8, D_MODEL)), full((1, D_MODEL)), full((D_IN_PROJ, D_MODEL)), full((1, R_Q)),
                  full((N_HEADS * D_HEAD_PAD, R_Q)), full((1, R_KV)), full((R_KV, N_HEADS * 256)), full((1, D_HEAD_PAD)),
                  full((1, D_HEAD_PAD)), rope_spec, rope_spec, rope_spec],
        out_specs=(pl.BlockSpec((TILE, N_GATES), lambda s: (latent(first(s)), 0)),
                   pl.BlockSpec((TILE, n_mla), lambda s: (first(s), 0)),
                   pl.BlockSpec((N_HEADS, TILE, D_HEAD_PAD), lambda s: (0, latent(second(s)), 0)),
                   pl.BlockSpec((N_HEADS, TILE, D_HEAD_PAD), lambda s: (0, second(s), 0)),
                   pl.BlockSpec((N_HEADS, TILE, D_V), lambda s: (0, second(s), 0))),
        scratch_shapes=[pltpu.VMEM((TILE, n_mla), F32)],
        compiler_params=pltpu.CompilerParams(dimension_semantics=("arbitrary",), vmem_limit_bytes=VMEM_LIMIT),
    )(x, ctx, modrows, bmodrows, norm_g, w_in_p, q_lora_g, w_uq_p, kv_lora_g, w_ukv, qng_p, kng_p, cos, slo, shi)


def _hosted_exchange(first, last, items, send_sems, recv_sems, local_sems):
    me = _lin(_coords())

    def remote(n, k, src, land, scatter):
        peer = _peer(k)
        return pltpu.make_async_remote_copy(
            src_ref=src.at[_lin(peer)] if scatter else src, dst_ref=land.at[me], send_sem=send_sems.at[n, k - 1],
            recv_sem=recv_sems.at[n, k - 1], device_id=peer, device_id_type=MESH)

    def local(n, src, land, scatter):
        return pltpu.make_async_copy(src.at[me] if scatter else src, land.at[me], local_sems.at[n])

    @pl.when(first)
    def _():
        for n, (src, land, scatter) in enumerate(items):
            for k in range(1, N_DEV):
                remote(n, k, src, land, scatter).start()
            local(n, src, land, scatter).start()

    @pl.when(last)
    def _():
        for n, (src, land, scatter) in enumerate(items):
            for k in range(1, N_DEV):
                peer = _peer(k)
                pltpu.make_async_remote_copy(
                    src_ref=src.at[0] if scatter else src, dst_ref=land.at[_lin(peer)], send_sem=send_sems.at[n, k - 1],
                    recv_sem=recv_sems.at[n, k - 1], device_id=peer, device_id_type=MESH).wait_recv()
            for k in range(1, N_DEV):
                remote(n, k, src, land, scatter).wait_send()
            local(n, src, land, scatter).wait()


def _attn_fwd(q, k, v, block_q, w_out_b):
    _, seq, _ = q.shape
    n_keys = k.shape[1]
    n_q = seq // block_q

    def body(q_ref, k_ref, v_ref, wo_ref, o_ref, lse_ref, wog_ref, ssem, rsem, lsem):
        h, i = pl.program_id(0), pl.program_id(1)
        _hosted_exchange((h == 0) & (i == 0), (h == N_HEADS - 1) & (i == n_q - 1), [(wo_ref, wog_ref, False)],
                         ssem, rsem, lsem)
        kb, vb = k_ref[0], v_ref[0]
        for r0 in range(0, block_q, ATTN_ROWS):
            rows = slice(r0, r0 + ATTN_ROWS)
            s = _dot_nt(q_ref[0, rows, :], kb)
            m = jnp.max(s, axis=-1, keepdims=True)
            p = jnp.exp2(s - m)
            l = jnp.sum(p, axis=-1, keepdims=True)
            o_ref[rows, :] = _dot(p.astype(BF16), vb) * (1.0 / l)
            lse_ref[0, rows, :] = m + jnp.log2(l)

    hbm = pl.BlockSpec(memory_space=pl.ANY)
    return pl.pallas_call(
        body, name="attn_fwd", grid=(N_HEADS, n_q),
        out_shape=(jax.ShapeDtypeStruct((seq, D_ATTN), F32), jax.ShapeDtypeStruct((N_HEADS, seq, 1), F32),
                   jax.ShapeDtypeStruct((N_DEV,) + w_out_b.shape, w_out_b.dtype)),
        in_specs=[pl.BlockSpec((1, block_q, D_HEAD_PAD), lambda h, i: (h, i, 0)),
                  pl.BlockSpec((1, n_keys, D_HEAD_PAD), lambda h, i: (h, 0, 0)),
                  pl.BlockSpec((1, n_keys, D_V), lambda h, i: (h, 0, 0)), hbm],
        out_specs=(pl.BlockSpec((block_q, D_V), lambda h, i: (i, h)),
                   pl.BlockSpec((1, block_q, 1), lambda h, i: (h, i, 0)), hbm),
        scratch_shapes=[pltpu.SemaphoreType.DMA((1, 7)), pltpu.SemaphoreType.DMA((1, 7)), pltpu.SemaphoreType.DMA((1,))],
        compiler_params=pltpu.CompilerParams(dimension_semantics=("arbitrary", "arbitrary"), vmem_limit_bytes=VMEM_LIMIT),
    )(q, k, v, w_out_b)


def _attn_bwd(q, k, v, o, lse, d_o, block_q, gwo_blocks, gwp):
    _, seq, _ = q.shape
    n_keys = k.shape[1]
    n_q = seq // block_q

    def body(q_ref, k_ref, v_ref, o_ref, lse_ref, do_ref, gwo_ref, gwp_ref, dq_ref, dkt_ref, dvt_ref, lwo_ref, lwp_ref,
             ssem, rsem, lsem):
        h, i = pl.program_id(0), pl.program_id(1)
        _hosted_exchange((h == 0) & (i == 0), (h == N_HEADS - 1) & (i == n_q - 1),
                         [(gwo_ref, lwo_ref, True), (gwp_ref, lwp_ref, False)], ssem, rsem, lsem)

        @pl.when(i == 0)
        def _():
            dkt_ref[...] = jnp.zeros_like(dkt_ref)
            dvt_ref[...] = jnp.zeros_like(dvt_ref)

        kb, vb = k_ref[0], v_ref[0]
        raws, ps = [], []
        for r0 in range(0, block_q, ATTN_ROWS):
            rows = slice(r0, r0 + ATTN_ROWS)
            d_out = do_ref[rows, :]
            p = jnp.exp2(_dot_nt(q_ref[0, rows, :], kb) - lse_ref[0, rows, :])
            dp = _dot_nt(d_out.astype(BF16), vb)
            delta = jnp.sum(d_out * o_ref[rows, :], axis=-1, keepdims=True)
            raw = (p * (dp - delta)).astype(BF16)
            dq_ref[0, rows, :] = _dot(raw, kb)
            raws.append(raw)
            ps.append(p.astype(BF16))
        dkt_ref[0] += _dot_tn(q_ref[0], jnp.concatenate(raws, axis=0))
        dvt_ref[0] += _dot_tn(do_ref[...].astype(BF16), jnp.concatenate(ps, axis=0))

    hbm = pl.BlockSpec(memory_space=pl.ANY)
    return pl.pallas_call(
        body, name="attn_bwd", grid=(N_HEADS, n_q),
        out_shape=(jax.ShapeDtypeStruct((N_HEADS, seq, D_HEAD_PAD), F32),
                   jax.ShapeDtypeStruct((N_HEADS, D_HEAD_PAD, n_keys), F32),
                   jax.ShapeDtypeStruct((N_HEADS, D_V, n_keys), F32),
                   jax.ShapeDtypeStruct(gwo_blocks.shape, gwo_blocks.dtype),
                   jax.ShapeDtypeStruct((N_DEV,) + gwp.shape, gwp.dtype)),
        in_specs=[pl.BlockSpec((1, block_q, D_HEAD_PAD), lambda h, i: (h, i, 0)),
                  pl.BlockSpec((1, n_keys, D_HEAD_PAD), lambda h, i: (h, 0, 0)),
                  pl.BlockSpec((1, n_keys, D_V), lambda h, i: (h, 0, 0)),
                  pl.BlockSpec((block_q, D_V), lambda h, i: (i, h)),
                  pl.BlockSpec((1, block_q, 1), lambda h, i: (h, i, 0)),
                  pl.BlockSpec((block_q, D_V), lambda h, i: (i, h)), hbm, hbm],
        out_specs=(pl.BlockSpec((1, block_q, D_HEAD_PAD), lambda h, i: (h, i, 0)),
                   pl.BlockSpec((1, D_HEAD_PAD, n_keys), lambda h, i: (h, 0, 0)),
                   pl.BlockSpec((1, D_V, n_keys), lambda h, i: (h, 0, 0)), hbm, hbm),
        scratch_shapes=[pltpu.SemaphoreType.DMA((2, 7)), pltpu.SemaphoreType.DMA((2, 7)), pltpu.SemaphoreType.DMA((2,))],
        compiler_params=pltpu.CompilerParams(dimension_semantics=("arbitrary", "arbitrary"), vmem_limit_bytes=VMEM_LIMIT),
    )(q, k, v, o, lse, d_o, gwo_blocks, gwp)


def _mix(x, target, attn, u, modrows, bmodrows, w_pool, pool_scale, w_out):
    seq = x.shape[0]
    rows = min(MIX_TILE, seq // 2)
    n_tiles = seq // rows
    rows8 = rows // HALO
    last8 = seq // HALO - 1

    n_blk = seq // Q_BLOCK
    per = rows // n_blk
    assert rows % n_blk == 0 and per % 8 == 0 and n_tiles >= 2
    n_stash = 8

    def body(x_ref, t_ref, o_hbm, ga_ref, pin_ref, hb_ref, ha_ref, gp_ref, mod_ref, bmod_ref, wp_ref, ps_ref, wo_ref,
             loss_ref, g1_ref, dgate_ref, do_hbm, dga_ref, dgp_ref, dpl_ref, gwob_ref, gwp_ref, dps_ref,
             abuf, dbuf, gwo_ref, *rest):
        stash_even, stash_odd = rest[0:n_stash], rest[n_stash:2 * n_stash]
        rsem, wsem = rest[2 * n_stash:]
        s = pl.program_id(0)

        def slab_copies(tile, slot, fetch):
            window = pl.ds(tile * per, per)
            if fetch:
                return [pltpu.make_async_copy(o_hbm.at[:, window, :], abuf.at[slot], rsem.at[slot])]
            return [pltpu.make_async_copy(dbuf.at[slot], do_hbm.at[:, window, :], wsem.at[slot])]

        def wait_all(slot, fetch):
            slab_copies(0, slot, fetch)[0].wait()

        a_slot = lax.rem(s, 2)
        b_slot = 1 - a_slot

        @pl.when(s == 0)
        def _():
            loss_ref[...] = jnp.zeros_like(loss_ref)
            dgate_ref[...] = jnp.zeros_like(dgate_ref)
            gwo_ref[...] = jnp.zeros_like(gwo_ref)
            gwp_ref[...] = jnp.zeros_like(gwp_ref)
            dps_ref[...] = jnp.zeros_like(dps_ref)
            for cp in slab_copies(0, 0, True):
                cp.start()

        @pl.when(s + 1 < n_tiles)
        def _():
            for cp in slab_copies(s + 1, b_slot, True):
                cp.start()

        @pl.when(s < n_tiles)
        def _():
            wait_all(a_slot, True)

        @pl.when(s >= 3)
        def _():
            wait_all(b_slot, False)

        gate = mod_ref[2:3, :] + bmod_ref[2:3, :]
        ps = ps_ref[...]

        def a_vector(slot):
            attn_t = jnp.swapaxes(abuf[slot], 0, 1).reshape(rows, D_ATTN)
            ga = ga_ref[...].astype(F32)
            sga = _sigmoid(ga)
            silu_ga = ga * sga
            pin = pin_ref[...].astype(F32)
            xs = jnp.concatenate([jnp.where(s > 0, hb_ref[...].astype(F32), 0.0), pin,
                                  jnp.where(s < n_tiles - 1, ha_ref[...].astype(F32), 0.0)], axis=0)
            tpos = s * rows + lax.broadcasted_iota(jnp.int32, (rows, 1), 0)
            pooled = []
            for g, w in enumerate(POOL_WINDOWS):
                sl = slice(g * GROUP_DIM, (g + 1) * GROUP_DIM)
                ws = _window_sum(xs[:, sl], w, False)[HALO:HALO + rows]
                pooled.append((ws * _inv_count(tpos, w, seq) - pin[:, sl]).astype(BF16))
            return attn_t, ga, sga, silu_ga, pooled

        def a_group_maps(pooled):
            return jnp.concatenate([_dot(pooled[g], wp_ref[g].astype(BF16)) for g in range(len(POOL_WINDOWS))], axis=1)

        def a_project(stash, yp, attn_t, ga, sga, silu_ga, pooled):
            zb_ref, _, fa_ref, sa_ref, fp_ref, sp_ref, yp_ref, pooled_ref = stash
            ypool = yp * ps
            gp = gp_ref[...].astype(F32)
            sgp = _sigmoid(gp)
            silu_gp = gp * sgp
            z = jnp.concatenate([silu_ga * attn_t, silu_gp * ypool], axis=1).astype(BF16)
            y = _dot(z, wo_ref[...])
            zb_ref[...] = z
            fa_ref[...] = attn_t * (sga * (1.0 + ga * (1.0 - sga)))
            sa_ref[...] = silu_ga
            fp_ref[...] = ypool * (sgp * (1.0 + gp * (1.0 - sgp)))
            sp_ref[...] = silu_gp
            yp_ref[...] = yp
            pooled_ref[...] = jnp.concatenate(pooled, axis=1)
            return y

        def a_loss(stash, y):
            res = x_ref[...] + gate * y - t_ref[...]
            loss_ref[...] += jnp.sum(res * res) * (0.5 / D_MODEL)
            dxn = res * (1.0 / D_MODEL)
            g1_ref[...] = dxn
            dgate_ref[...] += jnp.sum(dxn * y, axis=0, keepdims=True)
            stash[1][...] = (gate * dxn).astype(BF16)

        def b_dz(stash):
            return _dot_nt(stash[1][...], wo_ref[...])

        def b_gwo(stash):
            gwo_ref[...] += _dot_tn(stash[0][...], stash[1][...])

        def b_vector(stash, slot, dz):
            _, _, fa_ref, sa_ref, fp_ref, sp_ref, yp_ref, _ = stash
            dbra = dz[:, 0:D_ATTN]
            dbrp = dz[:, D_ATTN:]
            dga_ref[...] = (dbra * fa_ref[...]).astype(BF16)
            dbuf[slot] = jnp.swapaxes((dbra * sa_ref[...]).reshape(per, n_blk, D_ATTN), 0, 1)
            dgp_ref[...] = (dbrp * fp_ref[...]).astype(BF16)
            dyp_s = dbrp * sp_ref[...]
            dps_ref[...] += jnp.sum(dyp_s * yp_ref[...], axis=0, keepdims=True)
            return (dyp_s * ps).astype(BF16)

        def b_small(stash, dyp):
            pooled_ref = stash[7]
            for g in range(len(POOL_WINDOWS)):
                sl = slice(g * GROUP_DIM, (g + 1) * GROUP_DIM)
                gwp_ref[g] += _dot_tn(pooled_ref[:, sl], dyp[:, sl])
                dpl_ref[:, sl] = _dot_nt(dyp[:, sl], wp_ref[g].astype(BF16)).astype(BF16)

        def both(a_stash, b_stash, slot_a):
            dz = b_dz(b_stash)
            front = a_vector(slot_a)
            yp = a_group_maps(front[-1])
            b_gwo(b_stash)
            y = a_project(a_stash, yp, *front)
            dyp = b_vector(b_stash, 1 - slot_a, dz)
            a_loss(a_stash, y)
            b_small(b_stash, dyp)

        @pl.when(s == 0)
        def _():
            front = a_vector(0)
            a_loss(stash_even, a_project(stash_even, a_group_maps(front[-1]), *front))

        @pl.when((s > 0) & (s < n_tiles) & (a_slot == 0))
        def _():
            both(stash_even, stash_odd, 0)

        @pl.when((s > 0) & (s < n_tiles) & (a_slot == 1))
        def _():
            both(stash_odd, stash_even, 1)

        @pl.when(s == n_tiles)
        def _():
            last = (n_tiles - 1) % 2
            stash = stash_odd if last else stash_even
            dz = b_dz(stash)
            b_gwo(stash)
            b_small(stash, b_vector(stash, last, dz))
            gwob_ref[...] = gwo_ref[...].astype(BF16)

        @pl.when(s >= 1)
        def _():
            for cp in slab_copies(s - 1, b_slot, False):
                cp.start()

        @pl.when(s == n_tiles)
        def _():
            wait_all(b_slot, False)
            wait_all(a_slot, False)

    ta = lambda s: jnp.minimum(s, n_tiles - 1)
    tb = lambda s: jnp.maximum(s - 1, 0)
    tile = lambda w: pl.BlockSpec((rows, w), lambda s: (ta(s), 0))
    tile_b = lambda w: pl.BlockSpec((rows, w), lambda s: (tb(s), 0))
    ucol = lambda col: pl.BlockSpec((rows, 512), lambda s: (ta(s), col))
    full = lambda shape: pl.BlockSpec(shape, lambda s: (0,) * len(shape))
    stash_shapes = [pltpu.VMEM((rows, D_MODEL), BF16), pltpu.VMEM((rows, D_MODEL), BF16)] + \
        [pltpu.VMEM((rows, 512), F32)] * 5 + [pltpu.VMEM((rows, D_POOL), BF16)]
    return pl.pallas_call(
        body, name="mix_fwd_bwd", grid=(n_tiles + 1,),
        out_shape=(jax.ShapeDtypeStruct((8, 128), F32), jax.ShapeDtypeStruct((seq, D_MODEL), F32),
                   jax.ShapeDtypeStruct((1, D_MODEL), F32), jax.ShapeDtypeStruct((n_blk, Q_BLOCK, D_ATTN), F32),
                   jax.ShapeDtypeStruct((seq, D_ATTN), BF16), jax.ShapeDtypeStruct((seq, D_POOL), BF16),
                   jax.ShapeDtypeStruct((seq, D_POOL), BF16), jax.ShapeDtypeStruct((D_MODEL, D_MODEL), BF16),
                   jax.ShapeDtypeStruct((4, GROUP_DIM, GROUP_DIM), F32), jax.ShapeDtypeStruct((1, D_POOL), F32)),
        in_specs=[tile(D_MODEL), tile(D_MODEL), pl.BlockSpec(memory_space=pl.ANY), ucol(0), ucol(1),
                  pl.BlockSpec((HALO, 512), lambda s: (jnp.maximum(ta(s) * rows8 - 1, 0), 1)),
                  pl.BlockSpec((HALO, 512), lambda s: (jnp.minimum((ta(s) + 1) * rows8, last8), 1)),
                  ucol(2), full((8, D_MODEL)), full((8, D_MODEL)), full((4, GROUP_DIM, GROUP_DIM)), full((1, D_POOL)),
                  full((D_MODEL, D_MODEL))],
        out_specs=(full((8, 128)), tile(D_MODEL), full((1, D_MODEL)), pl.BlockSpec(memory_space=pl.ANY), tile_b(D_ATTN),
                   tile_b(D_POOL), tile_b(D_POOL), full((D_MODEL, D_MODEL)), full((4, GROUP_DIM, GROUP_DIM)),
                   full((1, D_POOL))),
        scratch_shapes=[pltpu.VMEM((2, n_blk, per, D_ATTN), F32), pltpu.VMEM((2, n_blk, per, D_ATTN), F32),
                        pltpu.VMEM((D_MODEL, D_MODEL), F32), *stash_shapes, *stash_shapes,
                        pltpu.SemaphoreType.DMA((2,)), pltpu.SemaphoreType.DMA((2,))],
        compiler_params=pltpu.CompilerParams(dimension_semantics=("arbitrary",), vmem_limit_bytes=VMEM_LIMIT),
    )(x, target, attn.reshape(n_blk, Q_BLOCK, D_ATTN), u, u, u, u, u, modrows, bmodrows, w_pool, pool_scale, w_out)


def _inproj_bwd(x, ctx, modrows, bmodrows, norm_g, w_in_p, q_lora_g, w_uq_p, kv_lora_g, w_ukv, qng_p, kng_p, cos, slo, shi,
                u, dq, dk, dv, dga, dgp, dpl, g1):
    seq = x.shape[0]
    n_tiles = seq // TILE + 1
    rows8 = TILE // HALO
    last8 = seq // HALO - 1

    def body(*refs):
        du_even, du_odd, dh_even, dh_odd = refs[-4:]
        gwin_ref, gwuq_ref, gwukv_ref, dqlg_ref, dkvlg_ref, dqng_ref, dkng_ref, dng_ref, dmod_ref = refs[-13:-4]
        s = pl.program_id(0)

        @pl.when(s == 0)
        def _():
            for ref in (gwin_ref, gwuq_ref, gwukv_ref, dqlg_ref, dkvlg_ref, dqng_ref, dkng_ref, dng_ref, dmod_ref,
                        du_odd, dh_even):
                ref[...] = jnp.zeros_like(ref)

        @pl.when(lax.rem(s, 2) == 0)
        def _():
            stages(s, refs[:-4], du_even, du_odd, dh_odd, dh_even)

        @pl.when(lax.rem(s, 2) == 1)
        def _():
            stages(s, refs[:-4], du_odd, du_even, dh_even, dh_odd)

    def stages(s, refs, du_ref, du_prev, dh_fill, dh_prev):
        (xb_ref, xc_ref, ctx_ref, mod_ref, bmod_ref, ng_ref, win_ref, qlg_ref, wuq_ref, kvlg_ref, wukv_ref, qng_ref, kng_ref,
         cos_ref, slo_ref, shi_ref, cq_ref, ckv_ref, kr_ref, dq_ref, dk_ref, dv_ref, dga_ref, dgp_ref, dpl_ref,
         hb_ref, ha_ref, g1_ref,
         gx_ref, gwin_ref, gwuq_ref, gwukv_ref, dqlg_ref, dkvlg_ref, dqng_ref, dkng_ref, dng_ref, dmod_ref) = refs


        i = jnp.minimum(s, n_tiles - 1)
        valid = s < n_tiles
        is_lat = (s > 0) & valid
        cq = cq_ref[...]
        rq = lax.rsqrt(jnp.mean(cq * cq, axis=-1, keepdims=True) + NORM_EPS)
        qhat = cq * rq
        qnb = (qhat * qlg_ref[...]).astype(BF16)
        q = _dot_nt(qnb, wuq_ref[...])
        ckv = ckv_ref[...]
        rkv = lax.rsqrt(jnp.mean(ckv * ckv, axis=-1, keepdims=True) + NORM_EPS)
        kvhat = ckv * rkv
        kvnb = (kvhat * kvlg_ref[...]).astype(BF16)
        kv = _dot(kvnb, wukv_ref[...])

        _, _, _, h2, _ = _modulated_input(s <= 1, xb_ref, ctx_ref, mod_ref, bmod_ref, ng_ref)
        dub = du_prev[...]
        du_g, du_m = dub[:, 0:N_GATES], dub[:, N_GATES:U_PAD]
        h2b = h2.astype(BF16)
        dh_fill[...] = _dot(du_g, win_ref[N_MLA:D_IN_PROJ, :]) + _dot(du_m, win_ref[0:U_PAD - N_GATES, :])
        gwin_ref[N_MLA:D_IN_PROJ, :] += _dot_tn(du_g, h2b)
        gwin_ref[0:N_MLA, :] += _dot_tn(du_m, h2b)[0:N_MLA]

        ctx3 = s <= 2
        xt, r, xg, _, scale = _modulated_input(ctx3, xc_ref, ctx_ref, mod_ref, bmod_ref, ng_ref)
        dh = dh_prev[...]
        dshift = jnp.sum(dh, axis=0, keepdims=True)
        dscale = jnp.sum(dh * xg, axis=0, keepdims=True)
        zero = jnp.zeros_like(dshift)
        dmod_ref[0:1, :] += jnp.where(ctx3, zero, dshift)
        dmod_ref[1:2, :] += jnp.where(ctx3, zero, dscale)
        dmod_ref[2:3, :] += jnp.where(ctx3, dshift, zero)
        dmod_ref[3:4, :] += jnp.where(ctx3, dscale, zero)
        dxg = dh * (1.0 + scale)
        xr = xt * r
        dng_ref[...] += jnp.sum(dxg * xr, axis=0, keepdims=True)
        dxn = dxg * ng_ref[...]
        gx_ref[...] = g1_ref[...] + r * (dxn - xr * jnp.mean(dxn * xr, axis=-1, keepdims=True))

        cos_t, slo_t, shi_t = cos_ref[...], slo_ref[...], shi_ref[...]
        qg = qng_ref[...]
        dq_heads = []
        dqng = jnp.zeros((1, D_HEAD_PAD), F32)
        for hd in range(N_HEADS):
            qh = q[:, hd * D_HEAD_PAD:(hd + 1) * D_HEAD_PAD]
            rr = lax.rsqrt(jnp.sum(qh * qh, axis=-1, keepdims=True) * (1.0 / D_HEAD) + NORM_EPS)
            yq = qh * rr
            dpost = jnp.where(is_lat, dq_ref[hd] * ATTN_SCALE, 0.0)
            dyn = jnp.concatenate([dpost[:, 0:D_NOPE], _rope_t(dpost[:, D_NOPE:], cos_t, slo_t, shi_t)], axis=1)
            dqng = dqng + jnp.sum(dyn * yq, axis=0, keepdims=True)
            dyg = dyn * qg
            dq_heads.append(rr * (dyg - yq * (jnp.sum(dyg * yq, axis=-1, keepdims=True) * (1.0 / D_HEAD))))
        dqng_ref[...] += dqng
        dqf = jnp.concatenate(dq_heads, axis=1).astype(BF16)

        krz = kr_ref[...]
        kr_ss = jnp.sum(krz * krz, axis=-1, keepdims=True)
        kg = kng_ref[...]
        kg_n, kg_r = kg[:, 0:D_NOPE], kg[:, D_NOPE:]
        dkv_parts = []
        dkr = jnp.zeros((TILE, 128), F32)
        dkng_n = jnp.zeros((1, D_NOPE), F32)
        dkng_r = jnp.zeros((1, 128), F32)
        for hd in range(N_HEADS):
            kn = kv[:, hd * 256:hd * 256 + D_NOPE]
            rr = lax.rsqrt((jnp.sum(kn * kn, axis=-1, keepdims=True) + kr_ss) * (1.0 / D_HEAD) + NORM_EPS)
            yn, yr = kn * rr, krz * rr
            dk_h = jnp.where(valid, jnp.transpose(dk_ref[hd]) * LN_2, 0.0)
            dpn = dk_h[:, 0:D_NOPE]
            dpr = _rope_t(dk_h[:, D_NOPE:], cos_t, slo_t, shi_t)
            dkng_n = dkng_n + jnp.sum(dpn * yn, axis=0, keepdims=True)
            dkng_r = dkng_r + jnp.sum(dpr * yr, axis=0, keepdims=True)
            dyn, dyr = dpn * kg_n, dpr * kg_r
            proj = (jnp.sum(dyn * yn, axis=-1, keepdims=True) + jnp.sum(dyr * yr, axis=-1, keepdims=True)) * (1.0 / D_HEAD)
            dkv_parts.append(rr * (dyn - yn * proj))
            dkv_parts.append(jnp.where(valid, jnp.transpose(dv_ref[hd]), 0.0))
            dkr = dkr + rr * (dyr - yr * proj)
        dkng_ref[:, 0:D_NOPE] += dkng_n
        dkng_ref[:, D_NOPE:] += dkng_r
        dkvf = jnp.concatenate(dkv_parts, axis=1).astype(BF16)

        lat_t = jnp.maximum(i - 1, 0)
        dpl_t = dpl_ref[...].astype(F32)
        ds = jnp.concatenate([jnp.where(i > 1, hb_ref[...].astype(F32), 0.0), dpl_t,
                              jnp.where(i < n_tiles - 1, ha_ref[...].astype(F32), 0.0)], axis=0)
        tpos = lat_t * TILE - HALO + lax.broadcasted_iota(jnp.int32, (TILE + 2 * HALO, 1), 0)
        for g, w in enumerate(POOL_WINDOWS):
            sl = slice(g * GROUP_DIM, (g + 1) * GROUP_DIM)
            wsum = _window_sum(ds[:, sl] * _inv_count(tpos, w, seq), w, True)[HALO:HALO + TILE]
            du_ref[:, O_PIN + g * GROUP_DIM:O_PIN + (g + 1) * GROUP_DIM] = jnp.where(
                is_lat, wsum - dpl_t[:, sl], 0.0).astype(BF16)

        du_ref[:, O_GA:O_GA + D_ATTN] = jnp.where(is_lat, dga_ref[...], jnp.zeros((), BF16))
        du_ref[:, O_GP:O_GP + D_POOL] = jnp.where(is_lat, dgp_ref[...], jnp.zeros((), BF16))
        du_ref[:, O_KR:U_PAD] = dkr.astype(BF16)

        gwuq_ref[...] += _dot_tn(dqf, qnb)
        dqn = _dot(dqf, wuq_ref[...])
        gwukv_ref[...] += _dot_tn(dkvf, kvnb)
        dkvn = _dot_nt(dkvf, wukv_ref[...])
        dqlg_ref[...] += jnp.sum(dqn * qhat, axis=0, keepdims=True)
        dqhat = dqn * qlg_ref[...]
        dcq = rq * (dqhat - qhat * jnp.mean(dqhat * qhat, axis=-1, keepdims=True))
        dkvlg_ref[...] += jnp.sum(dkvn * kvhat, axis=0, keepdims=True)
        dkvhat = dkvn * kvlg_ref[...]
        dckv = rkv * (dkvhat - kvhat * jnp.mean(dkvhat * kvhat, axis=-1, keepdims=True))
        du_ref[:, O_CQ:O_CQ + R_Q] = dcq.astype(BF16)
        du_ref[:, O_CKV:O_CKV + R_KV] = dckv.astype(BF16)

    t1 = lambda s: jnp.minimum(s, n_tiles - 1)
    t2 = lambda s: jnp.clip(s - 1, 0, n_tiles - 1)
    t3 = lambda s: jnp.clip(s - 2, 0, n_tiles - 1)
    latent = lambda t: jnp.maximum(t - 1, 0)
    lat = lambda s: (latent(t1(s)), 0)
    lat3 = lambda s: (latent(t3(s)), 0)
    full = lambda shape: pl.BlockSpec(shape, lambda s: (0,) * len(shape))
    rope_spec = pl.BlockSpec((TILE, 128), lambda s: (t1(s), 0))
    return pl.pallas_call(
        body, name="inproj_bwd", grid=(n_tiles + 2,),
        out_shape=(jax.ShapeDtypeStruct((seq, D_MODEL), F32), jax.ShapeDtypeStruct((D_IN_PROJ, D_MODEL), F32),
                   jax.ShapeDtypeStruct((N_HEADS * D_HEAD_PAD, R_Q), F32), jax.ShapeDtypeStruct((N_HEADS * 256, R_KV), F32),
                   jax.ShapeDtypeStruct((1, R_Q), F32), jax.ShapeDtypeStruct((1, R_KV), F32),
                   jax.ShapeDtypeStruct((1, D_HEAD_PAD), F32), jax.ShapeDtypeStruct((1, D_HEAD_PAD), F32),
                   jax.ShapeDtypeStruct((1, D_MODEL), F32), jax.ShapeDtypeStruct((8, D_MODEL), F32)),
        in_specs=[pl.BlockSpec((TILE, D_MODEL), lambda s: (latent(t2(s)), 0)), pl.BlockSpec((TILE, D_MODEL), lat3),
                  full((TILE, D_MODEL)), full((8, D_MODEL)), full((8, D_MODEL)),
                  full((1, D_MODEL)), full((D_IN_PROJ, D_MODEL)), full((1, R_Q)), full((N_HEADS * D_HEAD_PAD, R_Q)),
                  full((1, R_KV)), full((R_KV, N_HEADS * 256)), full((1, D_HEAD_PAD)), full((1, D_HEAD_PAD)),
                  rope_spec, rope_spec, rope_spec,
                  pl.BlockSpec((TILE, R_Q), lambda s: (t1(s), (O_CQ - N_GATES) // R_Q)),
                  pl.BlockSpec((TILE, R_KV), lambda s: (t1(s), (O_CKV - N_GATES) // R_KV)),
                  pl.BlockSpec((TILE, 128), lambda s: (t1(s), (O_KR - N_GATES) // 128)),
                  pl.BlockSpec((N_HEADS, TILE, D_HEAD_PAD), lambda s: (0, latent(t1(s)), 0)),
                  pl.BlockSpec((N_HEADS, D_HEAD_PAD, TILE), lambda s: (0, 0, t1(s))),
                  pl.BlockSpec((N_HEADS, D_V, TILE), lambda s: (0, 0, t1(s))),
                  pl.BlockSpec((TILE, D_ATTN), lat), pl.BlockSpec((TILE, D_POOL), lat), pl.BlockSpec((TILE, D_POOL), lat),
                  pl.BlockSpec((HALO, D_POOL), lambda s: (jnp.maximum((t1(s) - 1) * rows8 - 1, 0), 0)),
                  pl.BlockSpec((HALO, D_POOL), lambda s: (jnp.minimum(jnp.maximum(t1(s), 1) * rows8, last8), 0)),
                  pl.BlockSpec((TILE, D_MODEL), lat3)],
        out_specs=(pl.BlockSpec((TILE, D_MODEL), lat3), full((D_IN_PROJ, D_MODEL)), full((N_HEADS * D_HEAD_PAD, R_Q)),
                   full((N_HEADS * 256, R_KV)), full((1, R_Q)), full((1, R_KV)), full((1, D_HEAD_PAD)),
                   full((1, D_HEAD_PAD)), full((1, D_MODEL)), full((8, D_MODEL))),
        scratch_shapes=[pltpu.VMEM((TILE, U_PAD), BF16), pltpu.VMEM((TILE, U_PAD), BF16),
                        pltpu.VMEM((TILE, D_MODEL), F32), pltpu.VMEM((TILE, D_MODEL), F32)],
        compiler_params=pltpu.CompilerParams(dimension_semantics=("arbitrary",), vmem_limit_bytes=VMEM_LIMIT),
    )(x, x, ctx, modrows, bmodrows, norm_g, w_in_p, q_lora_g, w_uq_p, kv_lora_g, w_ukv, qng_p, kng_p, cos, slo, shi,
      u, u, u, dq, dk, dv, dga, dgp, dpl, dpl, dpl, g1)


SMALL_LAYOUT = ((0, D_MODEL), (1, R_Q), (2, R_KV), (3, D_HEAD_PAD), (4, D_HEAD_PAD), (5, D_POOL))
ROW_DMOD_B, ROW_DMOD_C, ROW_LOSS = 6, 9, 12


def _epilogue(parts, landed, smalls, dmod4, dgate, loss_acc, w_mod_l):
    n_a, n_l, n_s = len(parts), len(landed), len(smalls)
    n_mod = w_mod_l.shape[1]
    blk = [p.shape[1:] for p in parts]
    small_out = [D_MODEL, R_Q, R_KV, D_HEAD, D_HEAD, D_POOL]

    def body(*refs):
        part_refs = refs[0:n_a]
        land_refs = refs[n_a:n_a + n_l]
        small_in = refs[n_a + n_l:n_a + n_l + n_s]
        dmod4_ref, dgate_ref, loss_ref, wmod_ref = refs[n_a + n_l + n_s:n_a + n_l + n_s + 4]
        outs = refs[n_a + n_l + n_s + 4:]
        red_refs = outs[0:n_a]
        landsum_refs = outs[n_a:n_a + n_l]
        ccg_ref = outs[n_a + n_l]
        sum_refs = outs[n_a + n_l + 1:n_a + n_l + 1 + n_s]
        gbmod_ref, dmy_ref, lossout_ref = outs[n_a + n_l + 1 + n_s:n_a + n_l + 4 + n_s]
        scr = outs[n_a + n_l + 4 + n_s:]
        recv1, own1, sum1b, recv2 = scr[0:n_a], scr[n_a:2 * n_a], scr[2 * n_a:3 * n_a], scr[3 * n_a:4 * n_a]
        small_ref, smallg_ref, tot_ref, cc_ref = scr[4 * n_a:4 * n_a + 4]
        ssem1, rsem1, lsem, ssem2, rsem2, ssem_s, rsem_s, ssem_cc, rsem_cc = scr[4 * n_a + 4:]
        x, y, c = _coords()
        me = (x, y, c)
        sibling = (x, y, 1 - c)

        small_ref[...] = jnp.zeros_like(small_ref)
        for ref, (row, width) in zip(small_in, SMALL_LAYOUT):
            small_ref[row:row + 1, 0:width] = ref[...]
        small_ref[ROW_DMOD_B:ROW_DMOD_B + 2, :] = dmod4_ref[0:2, :]
        small_ref[ROW_DMOD_B + 2:ROW_DMOD_B + 3, :] = dgate_ref[...]
        small_ref[ROW_DMOD_C:ROW_DMOD_C + 2, :] = dmod4_ref[2:4, :]
        small_ref[ROW_LOSS:ROW_LOSS + 1, 0:128] = loss_ref[0:1, :]
        smallg_ref[_lin(me)] = small_ref[...]
        s_recv, s_send = _gather_to_all(small_ref, smallg_ref, ssem_s, rsem_s)

        stage1, own_copies = [], []
        for a in range(n_a):
            for b in range(4):
                dev = 4 * (b >> 1) + 2 * (b & 1)
                stage1.append(pltpu.make_async_remote_copy(
                    src_ref=part_refs[a].at[dev + (1 - c)], dst_ref=recv1[a].at[b],
                    send_sem=ssem1.at[a, b], recv_sem=rsem1.at[a, b], device_id=sibling, device_id_type=MESH))
                own_copies.append(pltpu.make_async_copy(part_refs[a].at[dev + c], own1[a].at[b], lsem.at[a, b]))
                stage1[-1].start()
                own_copies[-1].start()

        s_recv()
        tot = smallg_ref[0]
        for d in range(1, N_DEV):
            tot = tot + smallg_ref[d]
        tot_ref[...] = tot
        for ref, (row, _), width in zip(sum_refs, SMALL_LAYOUT, small_out):
            ref[...] = tot_ref[row:row + 1, 0:width]
        lossout_ref[...] = tot_ref[8:16, 0:128]
        lin = _lin(me)

        def my_columns(three_rows):
            v = jnp.concatenate(three_rows, axis=1)
            out = jnp.zeros((1, n_mod), F32)
            for d in range(N_DEV):
                out = out + jnp.where(lin == d, v[:, d * n_mod:(d + 1) * n_mod], 0.0)
            return out

        rows3 = lambda ref, r0: [ref[r0 + t:r0 + t + 1, :] for t in range(3)]
        dmodc = rows3(tot_ref, ROW_DMOD_C)
        gbmod_ref[...] = jnp.concatenate(rows3(tot_ref, ROW_DMOD_B), axis=1) + jnp.concatenate(dmodc, axis=1)
        dmy_ref[...] = jnp.zeros_like(dmy_ref)
        for b in range(N_DEV):
            dmy_ref[b:b + 1, :] = my_columns(rows3(smallg_ref.at[b], ROW_DMOD_B))
        dmy = my_columns(dmodc)
        dmy_ref[N_DEV:N_DEV + 1, :] = dmy
        part = lax.dot_general(jnp.broadcast_to(dmy, (8, n_mod)), wmod_ref[...], (((1,), (1,)), ((), ())),
                               precision=HIGHEST, preferred_element_type=F32)

        cc_ref[...] = part
        ccg_ref[_lin(me)] = part
        cc_recv, cc_send = _gather_to_all(cc_ref, ccg_ref, ssem_cc, rsem_cc)

        stage2 = []
        for a in range(n_a):
            for b in range(4):
                stage1[4 * a + b].wait_recv()
                own_copies[4 * a + b].wait()
                sum1b[a][b] = (own1[a][b] + recv1[a][b]).astype(BF16)
            for k in (1, 2, 3):
                tx, ty = _flip(x, (k >> 1) & 1), _flip(y, k & 1)
                stage2.append(pltpu.make_async_remote_copy(
                    src_ref=sum1b[a].at[2 * tx + ty], dst_ref=recv2[a].at[k - 1], send_sem=ssem2.at[a, k - 1],
                    recv_sem=rsem2.at[a, k - 1], device_id=(tx, ty, c), device_id_type=MESH))
                stage2[-1].start()
        for a in range(n_a):
            own = own1[a][2 * x + y] + recv1[a][2 * x + y]
            for k in (1, 2, 3):
                stage2[3 * a + k - 1].wait_recv()
                own = own + recv2[a][k - 1].astype(F32)
            red_refs[a][...] = own

        for ref, out in zip(land_refs, landsum_refs):
            acc = ref[0].astype(F32)
            for d in range(1, N_DEV):
                acc = acc + ref[d].astype(F32)
            out[...] = acc
        cc_recv()
        for cp in stage1 + stage2:
            cp.wait_send()
        s_send()
        cc_send()

    vm = pl.BlockSpec(memory_space=pltpu.VMEM)
    sds = jax.ShapeDtypeStruct
    return pl.pallas_call(
        body, name="epilogue_reduce",
        out_shape=(*[sds(s, F32) for s in blk], *[sds(a.shape[1:], F32) for a in landed], sds((N_DEV, 8, D_MODEL), F32),
                   *[sds((1, w), F32) for w in small_out], sds((1, 3 * D_MODEL), F32), sds((16, n_mod), F32),
                   sds((8, 128), F32)),
        in_specs=[pl.BlockSpec(memory_space=pl.ANY)] * n_a + [vm] * (n_l + n_s + 4),
        out_specs=tuple([vm] * (n_a + n_l + n_s + 4)),
        scratch_shapes=[*[pltpu.VMEM((4,) + s, F32) for s in blk], *[pltpu.VMEM((4,) + s, F32) for s in blk],
                        *[pltpu.VMEM((4,) + s, BF16) for s in blk], *[pltpu.VMEM((3,) + s, BF16) for s in blk],
                        pltpu.VMEM((SMALL_ROWS, D_MODEL), F32), pltpu.VMEM((N_DEV, SMALL_ROWS, D_MODEL), F32),
                        pltpu.VMEM((SMALL_ROWS, D_MODEL), F32), pltpu.VMEM((8, D_MODEL), F32),
                        pltpu.SemaphoreType.DMA((n_a, 4)), pltpu.SemaphoreType.DMA((n_a, 4)), pltpu.SemaphoreType.DMA((n_a, 4)),
                        pltpu.SemaphoreType.DMA((n_a, 3)), pltpu.SemaphoreType.DMA((n_a, 3)),
                        pltpu.SemaphoreType.DMA((7,)), pltpu.SemaphoreType.DMA((7,)),
                        pltpu.SemaphoreType.DMA((7,)), pltpu.SemaphoreType.DMA((7,))],
        compiler_params=pltpu.CompilerParams(vmem_limit_bytes=VMEM_LIMIT),
    )(*parts, *landed, *smalls, dmod4, dgate, loss_acc, w_mod_l)


def _adamw(weights, grads, ms, vs, c_all_t, dmod_my, p2g):
    n = len(weights)

    def body(*refs):
        w_refs = refs[0:n]
        g_in = refs[n:2 * n - 2]
        m_refs = refs[2 * n - 2:3 * n - 2]
        v_refs = refs[3 * n - 2:4 * n - 2]
        cat_ref, dmod_ref, p2g_ref = refs[4 * n - 2:4 * n + 1]
        outs = refs[4 * n + 1:]
        g_refs, d_refs, nm_refs, nv_refs = outs[0:n], outs[n:2 * n], outs[2 * n:3 * n], outs[3 * n:4 * n]
        gcc_ref, gwm_ref = g_refs[0], g_refs[1]

        part = p2g_ref[0, 0:1, :]
        for d in range(1, N_DEV):
            part = part + p2g_ref[d, 0:1, :]
        cc = w_refs[0][...]
        sg = _sigmoid(cc)
        gcc_ref[...] = part * (sg * (1.0 + cc * (1.0 - sg)))
        cat = cat_ref[...]
        gwm_ref[...] = lax.dot_general(cat * _sigmoid(cat), dmod_ref[...], (((1,), (0,)), ((), ())), precision=HIGHEST,
                                       preferred_element_type=F32)
        for idx in range(n):
            if idx >= 2:
                g_refs[idx][...] = g_in[idx - 2][...]
            g = g_refs[idx][...]
            m = ADAM_B1 * m_refs[idx][...] + (1.0 - ADAM_B1) * g
            v = ADAM_B2 * v_refs[idx][...] + (1.0 - ADAM_B2) * (g * g)
            m_hat = m / (1.0 - ADAM_B1 ** ADAM_STEP)
            v_hat = v / (1.0 - ADAM_B2 ** ADAM_STEP)
            d_refs[idx][...] = -ADAM_LR * (m_hat / (jnp.sqrt(v_hat) + ADAM_EPS) + ADAM_WD * w_refs[idx][...])
            nm_refs[idx][...] = m
            nv_refs[idx][...] = v

    vm = pl.BlockSpec(memory_space=pltpu.VMEM)
    shapes = [jax.ShapeDtypeStruct(w.shape, F32) for w in weights]
    args = list(weights) + list(grads[2:]) + list(ms) + list(vs) + [c_all_t, dmod_my, p2g]
    out_shape = tuple(shapes * 4)
    return pl.pallas_call(
        body, name="adamw_update", out_shape=out_shape, in_specs=[vm] * len(args), out_specs=tuple([vm] * len(out_shape)),
        compiler_params=pltpu.CompilerParams(vmem_limit_bytes=VMEM_LIMIT),
    )(*args)


def _rope_tables(seq):
    rows = seq // GRID_W
    row = np.repeat(np.arange(rows, dtype=np.float32), GRID_W)
    col = np.tile(np.arange(GRID_W, dtype=np.float32), rows)
    n_freq = D_ROPE // 4
    inv = (np.float32(ROPE_BASE) ** (-np.arange(n_freq, dtype=np.float32) / np.float32(n_freq))).astype(np.float32)
    ang_r = (row[:, None] * inv).astype(np.float32)
    ang_c = (col[:, None] * inv).astype(np.float32)
    ang = np.concatenate([ang_r, ang_r, ang_c, ang_c], axis=-1)
    cos, sin = np.cos(ang).astype(np.float32), np.sin(ang).astype(np.float32)
    low = (np.arange(D_ROPE) % 32) < 16

    def table(t, fill):
        out = np.full((TILE + seq, 128), fill, np.float32)
        out[TILE:, 0:D_ROPE] = t
        return jnp.asarray(out)

    return table(cos, 1.0), table(np.where(low, -sin, 0.0), 0.0), table(np.where(low, 0.0, sin), 0.0)


def kernel(x, c, ctx, c_ctx, w_mod, b_mod, norm_g, w_in, q_lora_g, w_uq, kv_lora_g, w_ukv, q_norm_g, k_norm_g, w_pool, pool_scale, w_out, loss_target, m_c_ctx, m_w_mod, m_b_mod, m_norm_g, m_w_in, m_q_lora_g, m_w_uq, m_kv_lora_g, m_w_ukv, m_q_norm_g, m_k_norm_g, m_w_pool, m_pool_scale, m_w_out, v_c_ctx, v_w_mod, v_b_mod, v_norm_g, v_w_in, v_q_lora_g, v_w_uq, v_kv_lora_g, v_w_ukv, v_q_norm_g, v_k_norm_g, v_w_pool, v_pool_scale, v_w_out):
    seq = x.shape[1]
    assert ctx.shape[1] == TILE and seq % TILE == 0 and seq % GRID_W == 0
    ix, iy, ic = lax.axis_index("x"), lax.axis_index("y"), lax.axis_index("c")
    me = 4 * ix + 2 * iy + ic
    x2, ctx2, tgt2 = x[0], ctx[0], loss_target[0]
    w_mod_l, w_ukv_l, w_out_l = w_mod[0], w_ukv[0], w_out[0]
    c_ctx_row = c_ctx.reshape(1, D_MODEL)

    tr = lambda a: jnp.transpose(a[0])
    w_in_tl, w_uq_tl = tr(w_in), tr(w_uq)
    cg, modg, wg_in, wg_uq, wg_ukv = _prologue(
        c, c_ctx_row, w_mod_l,
        [w_in_tl, w_uq_tl, w_ukv_l])
    mod_all = jnp.transpose(modg, (1, 0, 2)).reshape(16, 3 * D_MODEL)
    mod_b = lax.dynamic_slice_in_dim(mod_all, me, 1, axis=0).reshape(3, D_MODEL)
    mod_c = mod_all[8].reshape(3, D_MODEL)
    modrows = jnp.concatenate([mod_b, mod_c[0:2], jnp.zeros((3, D_MODEL), F32)], axis=0)
    b3 = b_mod.reshape(3, D_MODEL)
    bmodrows = jnp.concatenate([b3, b3[0:2], jnp.zeros((3, D_MODEL), F32)], axis=0)

    w_in_p = wg_in.reshape(D_IN_PROJ, D_MODEL)
    w_uq_p = jnp.concatenate([wg_uq.reshape(N_HEADS, D_HEAD, R_Q), jnp.zeros((N_HEADS, D_HEAD_PAD - D_HEAD, R_Q), BF16)],
                             axis=1).reshape(N_HEADS * D_HEAD_PAD, R_Q)
    w_ukv_f = jnp.transpose(wg_ukv, (1, 0, 2)).reshape(R_KV, N_HEADS * 256)
    qng_p = jnp.concatenate([q_norm_g, jnp.zeros((1, D_HEAD_PAD - D_HEAD), F32)], axis=1)
    kng_p = jnp.concatenate([k_norm_g, jnp.zeros((1, D_HEAD_PAD - D_HEAD), F32)], axis=1)
    cos, slo, shi = _rope_tables(seq)

    u_gates, u_mla, q, k, v = _inproj_fwd(x2, ctx2, modrows, bmodrows, norm_g, w_in_p, q_lora_g, w_uq_p, kv_lora_g, w_ukv_f,
                             qng_p, kng_p, cos, slo, shi)
    attn, lse, wg_out = _attn_fwd(q, k, v, min(4096, seq), w_out_l.astype(BF16))
    (loss_acc, g1, dgate, d_attn, dga, dgp, dpl, gwo_b, gwp, dps) = _mix(
        x2, tgt2, attn, u_gates, modrows, bmodrows, w_pool[0], pool_scale, wg_out.reshape(D_MODEL, D_MODEL))

    blocks = lambda a: a.reshape((N_DEV, a.shape[0] // N_DEV) + a.shape[1:])
    dq, dk, dv, land_wo, land_wp = _attn_bwd(q, k, v, attn, lse, d_attn.reshape(seq, D_ATTN), min(1024, seq), blocks(gwo_b),
                                             gwp.reshape(4 * GROUP_DIM, GROUP_DIM))
    (grad_x, gwin_t, gwuq_p, gwukv_t, dqlg, dkvlg, dqng, dkng, dng, dmod4) = _inproj_bwd(
        x2, ctx2, modrows, bmodrows, norm_g, w_in_p, q_lora_g, w_uq_p, kv_lora_g, w_ukv_f, qng_p, kng_p, cos, slo, shi,
        u_mla, dq, dk, dv, dga, dgp, dpl, g1)

    gwuq_t = gwuq_p.reshape(N_HEADS, D_HEAD_PAD, R_Q)[:, 0:D_HEAD].reshape(N_HEADS * D_HEAD, R_Q)
    parts = [blocks(gwin_t), blocks(gwuq_t), blocks(gwukv_t)]
    (r_win, r_wuq, r_wukv, r_wout, g_w_pool, ccg, g_ng, g_qlg, g_kvlg, g_qng, g_kng, g_ps, g_bmod, dmod_my,
     loss_rows) = _epilogue(parts, [land_wo, land_wp], [dng, dqlg, dkvlg, dqng, dkng, dps], dmod4, dgate, loss_acc, w_mod_l)

    g_w_ukv = jnp.transpose(r_wukv)
    c_rows = cg[:, 0, :]
    c_all_t = jnp.transpose(jnp.concatenate([c_rows, c_ctx_row, jnp.zeros((16 - N_DEV - 1, D_MODEL), F32)], axis=0))

    weights = [c_ctx_row, w_mod_l, b_mod, norm_g, w_in_tl, q_lora_g, w_uq_tl, kv_lora_g, w_ukv_l, q_norm_g, k_norm_g,
               w_pool.reshape(4 * GROUP_DIM, GROUP_DIM), pool_scale, w_out_l]
    grads = [None, None, g_bmod, g_ng, r_win, g_qlg, r_wuq, g_kvlg, g_w_ukv, g_qng, g_kng, g_w_pool, g_ps, r_wout]
    ms = [m_c_ctx.reshape(1, D_MODEL), m_w_mod[0], m_b_mod, m_norm_g, tr(m_w_in), m_q_lora_g, tr(m_w_uq), m_kv_lora_g,
          m_w_ukv[0], m_q_norm_g, m_k_norm_g, m_w_pool.reshape(4 * GROUP_DIM, GROUP_DIM), m_pool_scale, m_w_out[0]]
    vs = [v_c_ctx.reshape(1, D_MODEL), v_w_mod[0], v_b_mod, v_norm_g, tr(v_w_in), v_q_lora_g, tr(v_w_uq), v_kv_lora_g,
          v_w_ukv[0], v_q_norm_g, v_k_norm_g, v_w_pool.reshape(4 * GROUP_DIM, GROUP_DIM), v_pool_scale, v_w_out[0]]
    outs = _adamw(weights, grads, ms, vs, c_all_t, dmod_my, ccg)
    n = len(weights)
    grads, deltas, new_m, new_v = outs[0:n], outs[n:2 * n], outs[2 * n:3 * n], outs[3 * n:4 * n]

    loss = loss_rows[ROW_LOSS - 8, 0]
    final = [c_ctx.shape, w_mod.shape, b_mod.shape, norm_g.shape, w_in.shape, q_lora_g.shape, w_uq.shape, kv_lora_g.shape,
             w_ukv.shape, q_norm_g.shape, k_norm_g.shape, w_pool.shape, pool_scale.shape, w_out.shape]
    transposed = (4, 6)

    def shaped(arrs):
        return [(jnp.transpose(a) if i in transposed else a).reshape(s) for i, (a, s) in enumerate(zip(arrs, final))]

    return (loss, grad_x[None], *shaped(grads), *shaped(deltas), *shaped(new_m), *shaped(new_v))
```

```python
import numpy as np

import jax
import jax.numpy as jnp
from jax import lax
from jax.experimental import pallas as pl
from jax.experimental.pallas import tpu as pltpu

F32 = jnp.float32
BF16 = jnp.bfloat16
MESH = pl.DeviceIdType.MESH
HIGHEST = lax.Precision.HIGHEST

D_MODEL = 1024
N_HEADS = 4
D_NOPE = 128
D_ROPE = 64
D_HEAD = D_NOPE + D_ROPE
D_HEAD_PAD = 256
D_V = 128
R_Q = 256
R_KV = 128
D_ATTN = 512
D_POOL = 512
POOL_WINDOWS = (2, 4, 8, 16)
GROUP_DIM = 128
GRID_W = 64
ROPE_BASE = 10000.0
NORM_EPS = 1e-6
ATTN_SCALE = D_HEAD ** -0.5
LOG2_E = 1.4426950408889634
LN_2 = 0.6931471805599453
ATTN_ROWS = 256
D_IN_PROJ = 1984
U_PAD = 2048
O_GA, O_PIN, O_GP, O_CQ, O_CKV, O_KR = 0, 512, 1024, 1536, 1792, 1920
TILE = 256
MIX_TILE = 512
Q_BLOCK = 128
HALO = 16
N_GATES = 1536
N_MLA = D_IN_PROJ - N_GATES
N_DEV = 8
VMEM_LIMIT = 56 * 1024 * 1024

ADAM_LR = 0.001
ADAM_B1 = 0.9
ADAM_B2 = 0.999
ADAM_EPS = 1e-08
ADAM_WD = 0.01
ADAM_STEP = 10

SMALL_ROWS = 16


def _dot(a, b):
    return lax.dot_general(a, b, (((1,), (0,)), ((), ())), preferred_element_type=F32)


def _dot_nt(a, b):
    return lax.dot_general(a, b, (((1,), (1,)), ((), ())), preferred_element_type=F32)


def _dot_tn(a, b):
    return lax.dot_general(a, b, (((0,), (0,)), ((), ())), preferred_element_type=F32)


def _sigmoid(x):
    return 1.0 / (1.0 + jnp.exp(-x))


def _rope(p, cos, slo, shi):
    return p * cos + pltpu.roll(p, 112, 1) * slo + pltpu.roll(p, 16, 1) * shi


def _rope_t(d, cos, slo, shi):
    return d * cos + pltpu.roll(d * slo, 16, 1) + pltpu.roll(d * shi, 112, 1)


def _shift_rows(a, k):
    n = a.shape[0]
    return pltpu.roll(a, (-k) % n, 0)


def _window_sum(x, w, transposed):
    s = (x + _shift_rows(x, 1)) if transposed else (_shift_rows(x, -1) + x)
    step = 1
    while 2 * step < w:
        s = _shift_rows(s, -step) + _shift_rows(s, step)
        step *= 2
    return s


def _inv_count(tpos, w, seq):
    lo = jnp.maximum(tpos - w // 2, 0)
    hi = jnp.minimum(tpos - w // 2 + w, seq)
    return 1.0 / jnp.maximum(hi - lo, 1).astype(F32)


def _coords():
    return lax.axis_index("x"), lax.axis_index("y"), lax.axis_index("c")


def _flip(v, bit):
    return (1 - v) if bit else v


def _peer(k):
    x, y, c = _coords()
    return (_flip(x, (k >> 2) & 1), _flip(y, (k >> 1) & 1), _flip(c, k & 1))


def _lin(p):
    return 4 * p[0] + 2 * p[1] + p[2]


def _gather_to_all(src_ref, slots_ref, send_sems, recv_sems):
    me = _coords()
    copies = []
    for k in range(1, N_DEV):
        cp = pltpu.make_async_remote_copy(
            src_ref=src_ref, dst_ref=slots_ref.at[_lin(me)], send_sem=send_sems.at[k - 1], recv_sem=recv_sems.at[k - 1],
            device_id=_peer(k), device_id_type=MESH)
        cp.start()
        copies.append(cp)

    def wait_recv():
        for k in range(1, N_DEV):
            pltpu.make_async_remote_copy(
                src_ref=src_ref, dst_ref=slots_ref.at[_lin(_peer(k))], send_sem=send_sems.at[k - 1],
                recv_sem=recv_sems.at[k - 1], device_id=_peer(k), device_id_type=MESH).wait_recv()

    def wait_send():
        for cp in copies:
            cp.wait_send()

    return wait_recv, wait_send


def _prologue(c8, cctx8, w_mod_l, shards):
    n_mod = w_mod_l.shape[1]
    n_w = len(shards)

    def body(c_ref, cctx_ref, wmod_ref, *refs):
        w_refs = refs[0:n_w]
        cg_ref, modg_ref = refs[n_w], refs[n_w + 1]
        wg_refs = refs[n_w + 2:2 * n_w + 2]
        modblk_ref = refs[2 * n_w + 2]
        wb_refs = refs[2 * n_w + 3:3 * n_w + 3]
        c8_ref = refs[3 * n_w + 3]
        ssem_w, rsem_w, lsem_w, ssem_c, rsem_c, ssem_m, rsem_m = refs[3 * n_w + 4:]
        x, y, c = _coords()
        me = (x, y, c)
        sibling = (x, y, 1 - c)
        chips = [(1 - x, y), (x, 1 - y), (1 - x, 1 - y)]

        def wcopies(k, block, to, own=False):
            out = []
            for a in range(n_w):
                slot = wg_refs[a].at[_lin(block)]
                out.append(pltpu.make_async_remote_copy(
                    src_ref=wb_refs[a] if own else slot, dst_ref=slot, send_sem=ssem_w.at[a, k], recv_sem=rsem_w.at[a, k],
                    device_id=to, device_id_type=MESH))
            return out

        c8_ref[...] = jnp.broadcast_to(c_ref[...], (8, D_MODEL))
        cg_ref[_lin(me)] = c8_ref[...]
        c_recv, c_send = _gather_to_all(c8_ref, cg_ref, ssem_c, rsem_c)

        for a in range(n_w):
            wb_refs[a][...] = w_refs[a][...].astype(BF16)
        first = wcopies(0, me, sibling, own=True)
        for j, chip in enumerate(chips):
            first += wcopies(1 + j, me, (*chip, c), own=True)
        late = [idx for idx in range(len(first)) if idx % n_w == 0 and idx >= n_w]
        for idx, cp in enumerate(first):
            if idx not in late:
                cp.start()
        mine = [pltpu.make_async_copy(wb_refs[a], wg_refs[a].at[_lin(me)], lsem_w.at[a]) for a in range(n_w)]
        for cp in mine:
            cp.start()

        c_recv()
        row = lax.broadcasted_iota(jnp.int32, (8, D_MODEL), 0)
        c_all = jnp.zeros((8, D_MODEL), F32)
        for d in range(N_DEV):
            c_all = c_all + jnp.where(row == d, cg_ref[d], 0.0)
        cc = jnp.broadcast_to(cctx_ref[...], (8, D_MODEL))
        a = jnp.concatenate([c_all * _sigmoid(c_all), cc * _sigmoid(cc)], axis=0)
        modblk_ref[...] = _dot(a.astype(BF16), wmod_ref[...].astype(BF16))
        modg_ref[_lin(me)] = modblk_ref[...]
        m_recv, m_send = _gather_to_all(modblk_ref, modg_ref, ssem_m, rsem_m)
        for idx in late:
            first[idx].start()
        m_recv()

        passed = []
        for j, chip in enumerate(chips):
            for cp in wcopies(1 + j, (*chip, c), me):
                cp.wait_recv()
            fwd = wcopies(4 + j, (*chip, c), sibling)
            for cp in fwd:
                cp.start()
            passed += fwd
        for cp in wcopies(0, sibling, me):
            cp.wait_recv()
        for j, chip in enumerate(chips):
            for cp in wcopies(4 + j, (*chip, 1 - c), me):
                cp.wait_recv()
        for cp in first + passed:
            cp.wait_send()
        for cp in mine:
            cp.wait()
        c_send()
        m_send()

    vm = pl.BlockSpec(memory_space=pltpu.VMEM)
    hbm = pl.BlockSpec(memory_space=pl.ANY)
    return pl.pallas_call(
        body, name="prologue_gather",
        out_shape=(jax.ShapeDtypeStruct((N_DEV, 8, D_MODEL), F32), jax.ShapeDtypeStruct((N_DEV, 16, n_mod), F32),
                   *[jax.ShapeDtypeStruct((N_DEV,) + s.shape, BF16) for s in shards]),
        in_specs=[vm] * (3 + n_w), out_specs=(vm, vm, *[hbm] * n_w),
        scratch_shapes=[pltpu.VMEM((16, n_mod), F32), *[pltpu.VMEM(s.shape, BF16) for s in shards],
                        pltpu.VMEM((8, D_MODEL), F32),
                        pltpu.SemaphoreType.DMA((n_w, 7)), pltpu.SemaphoreType.DMA((n_w, 7)), pltpu.SemaphoreType.DMA((n_w,)),
                        pltpu.SemaphoreType.DMA((7,)), pltpu.SemaphoreType.DMA((7,)),
                        pltpu.SemaphoreType.DMA((7,)), pltpu.SemaphoreType.DMA((7,))],
        compiler_params=pltpu.CompilerParams(vmem_limit_bytes=VMEM_LIMIT),
    )(c8, cctx8, w_mod_l, *shards)


def _modulated_input(is_ctx, x_ref, ctx_ref, mod_ref, bmod_ref, ng_ref):
    xt = jnp.where(is_ctx, ctx_ref[...], x_ref[...])
    shift = jnp.where(is_ctx, mod_ref[3:4, :] + bmod_ref[3:4, :], mod_ref[0:1, :] + bmod_ref[0:1, :])
    scale = jnp.where(is_ctx, mod_ref[4:5, :] + bmod_ref[4:5, :], mod_ref[1:2, :] + bmod_ref[1:2, :])
    r = lax.rsqrt(jnp.mean(xt * xt, axis=-1, keepdims=True) + NORM_EPS)
    xg = (xt * r) * ng_ref[...]
    h = xg * (1.0 + scale) + shift
    return xt, r, xg, h, scale


def _inproj_fwd(x, ctx, modrows, bmodrows, norm_g, w_in_p, q_lora_g, w_uq_p, kv_lora_g, w_ukv, qng_p, kng_p, cos, slo, shi):
    seq = x.shape[0]
    n_tiles = seq // TILE + 1
    tot = seq + TILE

    n_mla = R_Q + R_KV + 128

    def body(x_ref, ctx_ref, mod_ref, bmod_ref, ng_ref, win_ref, qlg_ref, wuq_ref, kvlg_ref, wukv_ref, qng_ref, kng_ref,
             cos_ref, slo_ref, shi_ref, ug_ref, um_ref, q_ref, k_ref, v_ref, stash):
        s = pl.program_id(0)

        @pl.when(s == 0)
        def _():
            stash[...] = jnp.zeros_like(stash)

        refs = (x_ref, ctx_ref, mod_ref, bmod_ref, ng_ref, win_ref, qlg_ref, wuq_ref, kvlg_ref, wukv_ref, qng_ref, kng_ref,
                cos_ref, slo_ref, shi_ref, ug_ref, um_ref, q_ref, k_ref, v_ref)
        stages(s, refs, stash, stash)

    def stages(s, refs, fill, prev_ref):
        (x_ref, ctx_ref, mod_ref, bmod_ref, ng_ref, win_ref, qlg_ref, wuq_ref, kvlg_ref, wukv_ref, qng_ref, kng_ref,
         cos_ref, slo_ref, shi_ref, ug_ref, um_ref, q_ref, k_ref, v_ref) = refs
        cos_t, slo_t, shi_t = cos_ref[...], slo_ref[...], shi_ref[...]
        prev = prev_ref[...]
        cq = prev[:, 0:R_Q]
        qn = cq * lax.rsqrt(jnp.mean(cq * cq, axis=-1, keepdims=True) + NORM_EPS) * qlg_ref[...]
        q = _dot_nt(qn.astype(BF16), wuq_ref[...])
        qg = qng_ref[...] * (ATTN_SCALE * LOG2_E)
        for hd in range(N_HEADS):
            qh = q[:, hd * D_HEAD_PAD:(hd + 1) * D_HEAD_PAD]
            rr = lax.rsqrt(jnp.sum(qh * qh, axis=-1, keepdims=True) * (1.0 / D_HEAD) + NORM_EPS)
            qy = qh * rr * qg
            q_ref[hd, :, 0:D_NOPE] = qy[:, 0:D_NOPE].astype(BF16)
            q_ref[hd, :, D_NOPE:D_HEAD_PAD] = _rope(qy[:, D_NOPE:D_HEAD_PAD], cos_t, slo_t, shi_t).astype(BF16)

        ckv = prev[:, R_Q:R_Q + R_KV]
        kvn = ckv * lax.rsqrt(jnp.mean(ckv * ckv, axis=-1, keepdims=True) + NORM_EPS) * kvlg_ref[...]
        kv = _dot(kvn.astype(BF16), wukv_ref[...])
        krz = prev[:, R_Q + R_KV:n_mla]
        kr_ss = jnp.sum(krz * krz, axis=-1, keepdims=True)
        kg = kng_ref[...]
        for hd in range(N_HEADS):
            kn = kv[:, hd * 256:hd * 256 + D_NOPE]
            rr = lax.rsqrt((jnp.sum(kn * kn, axis=-1, keepdims=True) + kr_ss) * (1.0 / D_HEAD) + NORM_EPS)
            k_ref[hd, :, 0:D_NOPE] = (kn * rr * kg[:, 0:D_NOPE]).astype(BF16)
            k_ref[hd, :, D_NOPE:D_HEAD_PAD] = _rope(krz * rr * kg[:, D_NOPE:D_HEAD_PAD], cos_t, slo_t, shi_t).astype(BF16)
            v_ref[hd] = kv[:, hd * 256 + D_NOPE:(hd + 1) * 256].astype(BF16)

        _, _, _, h, _ = _modulated_input(s == 0, x_ref, ctx_ref, mod_ref, bmod_ref, ng_ref)
        hb = h.astype(BF16)
        ug_ref[...] = _dot_nt(hb, win_ref[N_MLA:D_IN_PROJ, :]).astype(BF16)
        um = _dot_nt(hb, win_ref[0:n_mla, :])
        lane = lax.broadcasted_iota(jnp.int32, (TILE, 128), 1)
        um = jnp.concatenate([um[:, 0:R_Q + R_KV], jnp.where(lane < D_ROPE, um[:, R_Q + R_KV:n_mla], 0.0)], axis=1)
        um_ref[...] = um
        fill[...] = um

    first = lambda s: jnp.minimum(s, n_tiles - 1)
    second = lambda s: jnp.maximum(s - 1, 0)
    latent = lambda t: jnp.maximum(t - 1, 0)
    full = lambda shape: pl.BlockSpec(shape, lambda s: (0,) * len(shape))
    rope_spec = pl.BlockSpec((TILE, 128), lambda s: (second(s), 0))
    return pl.pallas_call(
        body, name="inproj_fwd", grid=(n_tiles + 1,),
        out_shape=(jax.ShapeDtypeStruct((seq, N_GATES), BF16), jax.ShapeDtypeStruct((tot, n_mla), F32),
                   jax.ShapeDtypeStruct((N_HEADS, seq, D_HEAD_PAD), BF16),
                   jax.ShapeDtypeStruct((N_HEADS, tot, D_HEAD_PAD), BF16),
                   jax.ShapeDtypeStruct((N_HEADS, tot, D_V), BF16)),
        in_specs=[pl.BlockSpec((TILE, D_MODEL), lambda s: (latent(first(s)), 0)), full((TILE, D_MODEL)), full((8, D_MODEL)),
                  full((8, D_MODEL)), full((1, D_MODEL)), full((D_IN_PROJ, D_MODEL)), full((1, R_Q)),
                  full((N_HEADS * D_HEAD_PAD, R_Q)), full((1, R_KV)), full((R_KV, N_HEADS * 256)), full((1, D_HEAD_PAD)),
                  full((1, D_HEAD_PAD)), rope_spec, rope_spec, rope_spec],
        out_specs=(pl.BlockSpec((TILE, N_GATES), lambda s: (latent(first(s)), 0)),
                   pl.BlockSpec((TILE, n_mla), lambda s: (first(s), 0)),
                   pl.BlockSpec((N_HEADS, TILE, D_HEAD_PAD), lambda s: (0, latent(second(s)), 0)),
                   pl.BlockSpec((N_HEADS, TILE, D_HEAD_PAD), lambda s: (0, second(s), 0)),
                   pl.BlockSpec((N_HEADS, TILE, D_V), lambda s: (0, second(s), 0))),
        scratch_shapes=[pltpu.VMEM((TILE, n_mla), F32)],
        compiler_params=pltpu.CompilerParams(dimension_semantics=("arbitrary",), vmem_limit_bytes=VMEM_LIMIT),
    )(x, ctx, modrows, bmodrows, norm_g, w_in_p, q_lora_g, w_uq_p, kv_lora_g, w_ukv, qng_p, kng_p, cos, slo, shi)


def _hosted_exchange(first, last, items, send_sems, recv_sems, local_sems):
    me = _lin(_coords())

    def remote(n, k, src, land, scatter):
        peer = _peer(k)
        return pltpu.make_async_remote_copy(
            src_ref=src.at[_lin(peer)] if scatter else src, dst_ref=land.at[me], send_sem=send_sems.at[n, k - 1],
            recv_sem=recv_sems.at[n, k - 1], device_id=peer, device_id_type=MESH)

    def local(n, src, land, scatter):
        return pltpu.make_async_copy(src.at[me] if scatter else src, land.at[me], local_sems.at[n])

    @pl.when(first)
    def _():
        for n, (src, land, scatter) in enumerate(items):
            for k in range(1, N_DEV):
                remote(n, k, src, land, scatter).start()
            local(n, src, land, scatter).start()

    @pl.when(last)
    def _():
        for n, (src, land, scatter) in enumerate(items):
            for k in range(1, N_DEV):
                peer = _peer(k)
                pltpu.make_async_remote_copy(
                    src_ref=src.at[0] if scatter else src, dst_ref=land.at[_lin(peer)], send_sem=send_sems.at[n, k - 1],
                    recv_sem=recv_sems.at[n, k - 1], device_id=peer, device_id_type=MESH).wait_recv()
            for k in range(1, N_DEV):
                remote(n, k, src, land, scatter).wait_send()
            local(n, src, land, scatter).wait()


def _attn_fwd(q, k, v, block_q, w_out_b):
    _, seq, _ = q.shape
    n_keys = k.shape[1]
    n_q = seq // block_q

    def body(q_ref, k_ref, v_ref, wo_ref, o_ref, lse_ref, wog_ref, ssem, rsem, lsem):
        h, i = pl.program_id(0), pl.program_id(1)
        _hosted_exchange((h == 0) & (i == 0), (h == N_HEADS - 1) & (i == n_q - 1), [(wo_ref, wog_ref, False)],
                         ssem, rsem, lsem)
        kb, vb = k_ref[0], v_ref[0]
        for r0 in range(0, block_q, ATTN_ROWS):
            rows = slice(r0, r0 + ATTN_ROWS)
            s = _dot_nt(q_ref[0, rows, :], kb)
            m = jnp.max(s, axis=-1, keepdims=True)
            p = jnp.exp2(s - m)
            l = jnp.sum(p, axis=-1, keepdims=True)
            o_ref[rows, :] = _dot(p.astype(BF16), vb) * (1.0 / l)
            lse_ref[0, rows, :] = m + jnp.log2(l)

    hbm = pl.BlockSpec(memory_space=pl.ANY)
    return pl.pallas_call(
        body, name="attn_fwd", grid=(N_HEADS, n_q),
        out_shape=(jax.ShapeDtypeStruct((seq, D_ATTN), F32), jax.ShapeDtypeStruct((N_HEADS, seq, 1), F32),
                   jax.ShapeDtypeStruct((N_DEV,) + w_out_b.shape, w_out_b.dtype)),
        in_specs=[pl.BlockSpec((1, block_q, D_HEAD_PAD), lambda h, i: (h, i, 0)),
                  pl.BlockSpec((1, n_keys, D_HEAD_PAD), lambda h, i: (h, 0, 0)),
                  pl.BlockSpec((1, n_keys, D_V), lambda h, i: (h, 0, 0)), hbm],
        out_specs=(pl.BlockSpec((block_q, D_V), lambda h, i: (i, h)),
                   pl.BlockSpec((1, block_q, 1), lambda h, i: (h, i, 0)), hbm),
        scratch_shapes=[pltpu.SemaphoreType.DMA((1, 7)), pltpu.SemaphoreType.DMA((1, 7)), pltpu.SemaphoreType.DMA((1,))],
        compiler_params=pltpu.CompilerParams(dimension_semantics=("arbitrary", "arbitrary"), vmem_limit_bytes=VMEM_LIMIT),
    )(q, k, v, w_out_b)


def _attn_bwd(q, k, v, o, lse, d_o, block_q, gwo_blocks, gwp):
    _, seq, _ = q.shape
    n_keys = k.shape[1]
    n_q = seq // block_q

    def body(q_ref, k_ref, v_ref, o_ref, lse_ref, do_ref, gwo_ref, gwp_ref, dq_ref, dkt_ref, dvt_ref, lwo_ref, lwp_ref,
             ssem, rsem, lsem):
        h, i = pl.program_id(0), pl.program_id(1)
        _hosted_exchange((h == 0) & (i == 0), (h == N_HEADS - 1) & (i == n_q - 1),
                         [(gwo_ref, lwo_ref, True), (gwp_ref, lwp_ref, False)], ssem, rsem, lsem)

        @pl.when(i == 0)
        def _():
            dkt_ref[...] = jnp.zeros_like(dkt_ref)
            dvt_ref[...] = jnp.zeros_like(dvt_ref)

        kb, vb = k_ref[0], v_ref[0]
        raws, ps = [], []
        for r0 in range(0, block_q, ATTN_ROWS):
            rows = slice(r0, r0 + ATTN_ROWS)
            d_out = do_ref[rows, :]
            p = jnp.exp2(_dot_nt(q_ref[0, rows, :], kb) - lse_ref[0, rows, :])
            dp = _dot_nt(d_out.astype(BF16), vb)
            delta = jnp.sum(d_out * o_ref[rows, :], axis=-1, keepdims=True)
            raw = (p * (dp - delta)).astype(BF16)
            dq_ref[0, rows, :] = _dot(raw, kb)
            raws.append(raw)
            ps.append(p.astype(BF16))
        dkt_ref[0] += _dot_tn(q_ref[0], jnp.concatenate(raws, axis=0))
        dvt_ref[0] += _dot_tn(do_ref[...].astype(BF16), jnp.concatenate(ps, axis=0))

    hbm = pl.BlockSpec(memory_space=pl.ANY)
    return pl.pallas_call(
        body, name="attn_bwd", grid=(N_HEADS, n_q),
        out_shape=(jax.ShapeDtypeStruct((N_HEADS, seq, D_HEAD_PAD), F32),
                   jax.ShapeDtypeStruct((N_HEADS, D_HEAD_PAD, n_keys), F32),
                   jax.ShapeDtypeStruct((N_HEADS, D_V, n_keys), F32),
                   jax.ShapeDtypeStruct(gwo_blocks.shape, gwo_blocks.dtype),
                   jax.ShapeDtypeStruct((N_DEV,) + gwp.shape, gwp.dtype)),
        in_specs=[pl.BlockSpec((1, block_q, D_HEAD_PAD), lambda h, i: (h, i, 0)),
                  pl.BlockSpec((1, n_keys, D_HEAD_PAD), lambda h, i: (h, 0, 0)),
                  pl.BlockSpec((1, n_keys, D_V), lambda h, i: (h, 0, 0)),
                  pl.BlockSpec((block_q, D_V), lambda h, i: (i, h)),
                  pl.BlockSpec((1, block_q, 1), lambda h, i: (h, i, 0)),
                  pl.BlockSpec((block_q, D_V), lambda h, i: (i, h)), hbm, hbm],
        out_specs=(pl.BlockSpec((1, block_q, D_HEAD_PAD), lambda h, i: (h, i, 0)),
                   pl.BlockSpec((1, D_HEAD_PAD, n_keys), lambda h, i: (h, 0, 0)),
                   pl.BlockSpec((1, D_V, n_keys), lambda h, i: (h, 0, 0)), hbm, hbm),
        scratch_shapes=[pltpu.SemaphoreType.DMA((2, 7)), pltpu.SemaphoreType.DMA((2, 7)), pltpu.SemaphoreType.DMA((2,))],
        compiler_params=pltpu.CompilerParams(dimension_semantics=("arbitrary", "arbitrary"), vmem_limit_bytes=VMEM_LIMIT),
    )(q, k, v, o, lse, d_o, gwo_blocks, gwp)


def _mix(x, target, attn, u, modrows, bmodrows, w_pool, pool_scale, w_out):
    seq = x.shape[0]
    rows = min(MIX_TILE, seq // 2)
    n_tiles = seq // rows
    rows8 = rows // HALO
    last8 = seq // HALO - 1

    n_blk = seq // Q_BLOCK
    per = rows // n_blk
    assert rows % n_blk == 0 and per % 8 == 0 and n_tiles >= 2
    n_stash = 8

    def body(x_ref, t_ref, o_hbm, ga_ref, pin_ref, hb_ref, ha_ref, gp_ref, mod_ref, bmod_ref, wp_ref, ps_ref, wo_ref,
             loss_ref, g1_ref, dgate_ref, do_hbm, dga_ref, dgp_ref, dpl_ref, gwob_ref, gwp_ref, dps_ref,
             abuf, dbuf, gwo_ref, *rest):
        stash_even, stash_odd = rest[0:n_stash], rest[n_stash:2 * n_stash]
        rsem, wsem = rest[2 * n_stash:]
        s = pl.program_id(0)

        def slab_copies(tile, slot, fetch):
            window = pl.ds(tile * per, per)
            if fetch:
                return [pltpu.make_async_copy(o_hbm.at[:, window, :], abuf.at[slot], rsem.at[slot])]
            return [pltpu.make_async_copy(dbuf.at[slot], do_hbm.at[:, window, :], wsem.at[slot])]

        def wait_all(slot, fetch):
            slab_copies(0, slot, fetch)[0].wait()

        a_slot = lax.rem(s, 2)
        b_slot = 1 - a_slot

        @pl.when(s == 0)
        def _():
            loss_ref[...] = jnp.zeros_like(loss_ref)
            dgate_ref[...] = jnp.zeros_like(dgate_ref)
            gwo_ref[...] = jnp.zeros_like(gwo_ref)
            gwp_ref[...] = jnp.zeros_like(gwp_ref)
            dps_ref[...] = jnp.zeros_like(dps_ref)
            for cp in slab_copies(0, 0, True):
                cp.start()

        @pl.when(s + 1 < n_tiles)
        def _():
            for cp in slab_copies(s + 1, b_slot, True):
                cp.start()

        @pl.when(s < n_tiles)
        def _():
            wait_all(a_slot, True)

        @pl.when(s >= 3)
        def _():
            wait_all(b_slot, False)

        gate = mod_ref[2:3, :] + bmod_ref[2:3, :]
        ps = ps_ref[...]

        def a_vector(slot):
            attn_t = jnp.swapaxes(abuf[slot], 0, 1).reshape(rows, D_ATTN)
            ga = ga_ref[...].astype(F32)
            sga = _sigmoid(ga)
            silu_ga = ga * sga
            pin = pin_ref[...].astype(F32)
            xs = jnp.concatenate([jnp.where(s > 0, hb_ref[...].astype(F32), 0.0), pin,
                                  jnp.where(s < n_tiles - 1, ha_ref[...].astype(F32), 0.0)], axis=0)
            tpos = s * rows + lax.broadcasted_iota(jnp.int32, (rows, 1), 0)
            pooled = []
            for g, w in enumerate(POOL_WINDOWS):
                sl = slice(g * GROUP_DIM, (g + 1) * GROUP_DIM)
                ws = _window_sum(xs[:, sl], w, False)[HALO:HALO + rows]
                pooled.append((ws * _inv_count(tpos, w, seq) - pin[:, sl]).astype(BF16))
            return attn_t, ga, sga, silu_ga, pooled

        def a_group_maps(pooled):
            return jnp.concatenate([_dot(pooled[g], wp_ref[g].astype(BF16)) for g in range(len(POOL_WINDOWS))], axis=1)

        def a_project(stash, yp, attn_t, ga, sga, silu_ga, pooled):
            zb_ref, _, fa_ref, sa_ref, fp_ref, sp_ref, yp_ref, pooled_ref = stash
            ypool = yp * ps
            gp = gp_ref[...].astype(F32)
            sgp = _sigmoid(gp)
            silu_gp = gp * sgp
            z = jnp.concatenate([silu_ga * attn_t, silu_gp * ypool], axis=1).astype(BF16)
            y = _dot(z, wo_ref[...])
            zb_ref[...] = z
            fa_ref[...] = attn_t * (sga * (1.0 + ga * (1.0 - sga)))
            sa_ref[...] = silu_ga
            fp_ref[...] = ypool * (sgp * (1.0 + gp * (1.0 - sgp)))
            sp_ref[...] = silu_gp
            yp_ref[...] = yp
            pooled_ref[...] = jnp.concatenate(pooled, axis=1)
            return y

        def a_loss(stash, y):
            res = x_ref[...] + gate * y - t_ref[...]
            loss_ref[...] += jnp.sum(res * res) * (0.5 / D_MODEL)
            dxn = res * (1.0 / D_MODEL)
            g1_ref[...] = dxn
            dgate_ref[...] += jnp.sum(dxn * y, axis=0, keepdims=True)
            stash[1][...] = (gate * dxn).astype(BF16)

        def b_dz(stash):
            return _dot_nt(stash[1][...], wo_ref[...])

        def b_gwo(stash):
            gwo_ref[...] += _dot_tn(stash[0][...], stash[1][...])

        def b_vector(stash, slot, dz):
            _, _, fa_ref, sa_ref, fp_ref, sp_ref, yp_ref, _ = stash
            dbra = dz[:, 0:D_ATTN]
            dbrp = dz[:, D_ATTN:]
            dga_ref[...] = (dbra * fa_ref[...]).astype(BF16)
            dbuf[slot] = jnp.swapaxes((dbra * sa_ref[...]).reshape(per, n_blk, D_ATTN), 0, 1)
            dgp_ref[...] = (dbrp * fp_ref[...]).astype(BF16)
            dyp_s = dbrp * sp_ref[...]
            dps_ref[...] += jnp.sum(dyp_s * yp_ref[...], axis=0, keepdims=True)
            return (dyp_s * ps).astype(BF16)

        def b_small(stash, dyp):
            pooled_ref = stash[7]
            for g in range(len(POOL_WINDOWS)):
                sl = slice(g * GROUP_DIM, (g + 1) * GROUP_DIM)
                gwp_ref[g] += _dot_tn(pooled_ref[:, sl], dyp[:, sl])
                dpl_ref[:, sl] = _dot_nt(dyp[:, sl], wp_ref[g].astype(BF16)).astype(BF16)

        def both(a_stash, b_stash, slot_a):
            dz = b_dz(b_stash)
            front = a_vector(slot_a)
            yp = a_group_maps(front[-1])
            b_gwo(b_stash)
            y = a_project(a_stash, yp, *front)
            dyp = b_vector(b_stash, 1 - slot_a, dz)
            a_loss(a_stash, y)
            b_small(b_stash, dyp)

        @pl.when(s == 0)
        def _():
            front = a_vector(0)
            a_loss(stash_even, a_project(stash_even, a_group_maps(front[-1]), *front))

        @pl.when((s > 0) & (s < n_tiles) & (a_slot == 0))
        def _():
            both(stash_even, stash_odd, 0)

        @pl.when((s > 0) & (s < n_tiles) & (a_slot == 1))
        def _():
            both(stash_odd, stash_even, 1)

        @pl.when(s == n_tiles)
        def _():
            last = (n_tiles - 1) % 2
            stash = stash_odd if last else stash_even
            dz = b_dz(stash)
            b_gwo(stash)
            b_small(stash, b_vector(stash, last, dz))
            gwob_ref[...] = gwo_ref[...].astype(BF16)

        @pl.when(s >= 1)
        def _():
            for cp in slab_copies(s - 1, b_slot, False):
                cp.start()

        @pl.when(s == n_tiles)
        def _():
            wait_all(b_slot, False)
            wait_all(a_slot, False)

    ta = lambda s: jnp.minimum(s, n_tiles - 1)
    tb = lambda s: jnp.maximum(s - 1, 0)
    tile = lambda w: pl.BlockSpec((rows, w), lambda s: (ta(s), 0))
    tile_b = lambda w: pl.BlockSpec((rows, w), lambda s: (tb(s), 0))
    ucol = lambda col: pl.BlockSpec((rows, 512), lambda s: (ta(s), col))
    full = lambda shape: pl.BlockSpec(shape, lambda s: (0,) * len(shape))
    stash_shapes = [pltpu.VMEM((rows, D_MODEL), BF16), pltpu.VMEM((rows, D_MODEL), BF16)] + \
        [pltpu.VMEM((rows, 512), F32)] * 5 + [pltpu.VMEM((rows, D_POOL), BF16)]
    return pl.pallas_call(
        body, name="mix_fwd_bwd", grid=(n_tiles + 1,),
        out_shape=(jax.ShapeDtypeStruct((8, 128), F32), jax.ShapeDtypeStruct((seq, D_MODEL), F32),
                   jax.ShapeDtypeStruct((1, D_MODEL), F32), jax.ShapeDtypeStruct((n_blk, Q_BLOCK, D_ATTN), F32),
                   jax.ShapeDtypeStruct((seq, D_ATTN), BF16), jax.ShapeDtypeStruct((seq, D_POOL), BF16),
                   jax.ShapeDtypeStruct((seq, D_POOL), BF16), jax.ShapeDtypeStruct((D_MODEL, D_MODEL), BF16),
                   jax.ShapeDtypeStruct((4, GROUP_DIM, GROUP_DIM), F32), jax.ShapeDtypeStruct((1, D_POOL), F32)),
        in_specs=[tile(D_MODEL), tile(D_MODEL), pl.BlockSpec(memory_space=pl.ANY), ucol(0), ucol(1),
                  pl.BlockSpec((HALO, 512), lambda s: (jnp.maximum(ta(s) * rows8 - 1, 0), 1)),
                  pl.BlockSpec((HALO, 512), lambda s: (jnp.minimum((ta(s) + 1) * rows8, last8), 1)),
                  ucol(2), full((8, D_MODEL)), full((8, D_MODEL)), full((4, GROUP_DIM, GROUP_DIM)), full((1, D_POOL)),
                  full((D_MODEL, D_MODEL))],
        out_specs=(full((8, 128)), tile(D_MODEL), full((1, D_MODEL)), pl.BlockSpec(memory_space=pl.ANY), tile_b(D_ATTN),
                   tile_b(D_POOL), tile_b(D_POOL), full((D_MODEL, D_MODEL)), full((4, GROUP_DIM, GROUP_DIM)),
                   full((1, D_POOL))),
        scratch_shapes=[pltpu.VMEM((2, n_blk, per, D_ATTN), F32), pltpu.VMEM((2, n_blk, per, D_ATTN), F32),
                        pltpu.VMEM((D_MODEL, D_MODEL), F32), *stash_shapes, *stash_shapes,
                        pltpu.SemaphoreType.DMA((2,)), pltpu.SemaphoreType.DMA((2,))],
        compiler_params=pltpu.CompilerParams(dimension_semantics=("arbitrary",), vmem_limit_bytes=VMEM_LIMIT),
    )(x, target, attn.reshape(n_blk, Q_BLOCK, D_ATTN), u, u, u, u, u, modrows, bmodrows, w_pool, pool_scale, w_out)


def _inproj_bwd(x, ctx, modrows, bmodrows, norm_g, w_in_p, q_lora_g, w_uq_p, kv_lora_g, w_ukv, qng_p, kng_p, cos, slo, shi,
                u, dq, dk, dv, dga, dgp, dpl, g1):
    seq = x.shape[0]
    n_tiles = seq // TILE + 1
    rows8 = TILE // HALO
    last8 = seq // HALO - 1

    def body(*refs):
        du_even, du_odd, dh_even, dh_odd = refs[-4:]
        gwin_ref, gwuq_ref, gwukv_ref, dqlg_ref, dkvlg_ref, dqng_ref, dkng_ref, dng_ref, dmod_ref = refs[-13:-4]
        s = pl.program_id(0)

        @pl.when(s == 0)
        def _():
            for ref in (gwin_ref, gwuq_ref, gwukv_ref, dqlg_ref, dkvlg_ref, dqng_ref, dkng_ref, dng_ref, dmod_ref,
                        du_odd, dh_even):
                ref[...] = jnp.zeros_like(ref)

        @pl.when(lax.rem(s, 2) == 0)
        def _():
            stages(s, refs[:-4], du_even, du_odd, dh_odd, dh_even)

        @pl.when(lax.rem(s, 2) == 1)
        def _():
            stages(s, refs[:-4], du_odd, du_even, dh_even, dh_odd)

    def stages(s, refs, du_ref, du_prev, dh_fill, dh_prev):
        (xb_ref, xc_ref, ctx_ref, mod_ref, bmod_ref, ng_ref, win_ref, qlg_ref, wuq_ref, kvlg_ref, wukv_ref, qng_ref, kng_ref,
         cos_ref, slo_ref, shi_ref, cq_ref, ckv_ref, kr_ref, dq_ref, dk_ref, dv_ref, dga_ref, dgp_ref, dpl_ref,
         hb_ref, ha_ref, g1_ref,
         gx_ref, gwin_ref, gwuq_ref, gwukv_ref, dqlg_ref, dkvlg_ref, dqng_ref, dkng_ref, dng_ref, dmod_ref) = refs


        i = jnp.minimum(s, n_tiles - 1)
        valid = s < n_tiles
        is_lat = (s > 0) & valid
        cq = cq_ref[...]
        rq = lax.rsqrt(jnp.mean(cq * cq, axis=-1, keepdims=True) + NORM_EPS)
        qhat = cq * rq
        qnb = (qhat * qlg_ref[...]).astype(BF16)
        q = _dot_nt(qnb, wuq_ref[...])
        ckv = ckv_ref[...]
        rkv = lax.rsqrt(jnp.mean(ckv * ckv, axis=-1, keepdims=True) + NORM_EPS)
        kvhat = ckv * rkv
        kvnb = (kvhat * kvlg_ref[...]).astype(BF16)
        kv = _dot(kvnb, wukv_ref[...])

        _, _, _, h2, _ = _modulated_input(s <= 1, xb_ref, ctx_ref, mod_ref, bmod_ref, ng_ref)
        dub = du_prev[...]
        du_g, du_m = dub[:, 0:N_GATES], dub[:, N_GATES:U_PAD]
        h2b = h2.astype(BF16)
        dh_fill[...] = _dot(du_g, win_ref[N_MLA:D_IN_PROJ, :]) + _dot(du_m, win_ref[0:U_PAD - N_GATES, :])
        gwin_ref[N_MLA:D_IN_PROJ, :] += _dot_tn(du_g, h2b)
        gwin_ref[0:N_MLA, :] += _dot_tn(du_m, h2b)[0:N_MLA]

        ctx3 = s <= 2
        xt, r, xg, _, scale = _modulated_input(ctx3, xc_ref, ctx_ref, mod_ref, bmod_ref, ng_ref)
        dh = dh_prev[...]
        dshift = jnp.sum(dh, axis=0, keepdims=True)
        dscale = jnp.sum(dh * xg, axis=0, keepdims=True)
        zero = jnp.zeros_like(dshift)
        dmod_ref[0:1, :] += jnp.where(ctx3, zero, dshift)
        dmod_ref[1:2, :] += jnp.where(ctx3, zero, dscale)
        dmod_ref[2:3, :] += jnp.where(ctx3, dshift, zero)
        dmod_ref[3:4, :] += jnp.where(ctx3, dscale, zero)
        dxg = dh * (1.0 + scale)
        xr = xt * r
        dng_ref[...] += jnp.sum(dxg * xr, axis=0, keepdims=True)
        dxn = dxg * ng_ref[...]
        gx_ref[...] = g1_ref[...] + r * (dxn - xr * jnp.mean(dxn * xr, axis=-1, keepdims=True))

        cos_t, slo_t, shi_t = cos_ref[...], slo_ref[...], shi_ref[...]
        qg = qng_ref[...]
        dq_heads = []
        dqng = jnp.zeros((1, D_HEAD_PAD), F32)
        for hd in range(N_HEADS):
            qh = q[:, hd * D_HEAD_PAD:(hd + 1) * D_HEAD_PAD]
            rr = lax.rsqrt(jnp.sum(qh * qh, axis=-1, keepdims=True) * (1.0 / D_HEAD) + NORM_EPS)
            yq = qh * rr
            dpost = jnp.where(is_lat, dq_ref[hd] * ATTN_SCALE, 0.0)
            dyn = jnp.concatenate([dpost[:, 0:D_NOPE], _rope_t(dpost[:, D_NOPE:], cos_t, slo_t, shi_t)], axis=1)
            dqng = dqng + jnp.sum(dyn * yq, axis=0, keepdims=True)
            dyg = dyn * qg
            dq_heads.append(rr * (dyg - yq * (jnp.sum(dyg * yq, axis=-1, keepdims=True) * (1.0 / D_HEAD))))
        dqng_ref[...] += dqng
        dqf = jnp.concatenate(dq_heads, axis=1).astype(BF16)

        krz = kr_ref[...]
        kr_ss = jnp.sum(krz * krz, axis=-1, keepdims=True)
        kg = kng_ref[...]
        kg_n, kg_r = kg[:, 0:D_NOPE], kg[:, D_NOPE:]
        dkv_parts = []
        dkr = jnp.zeros((TILE, 128), F32)
        dkng_n = jnp.zeros((1, D_NOPE), F32)
        dkng_r = jnp.zeros((1, 128), F32)
        for hd in range(N_HEADS):
            kn = kv[:, hd * 256:hd * 256 + D_NOPE]
            rr = lax.rsqrt((jnp.sum(kn * kn, axis=-1, keepdims=True) + kr_ss) * (1.0 / D_HEAD) + NORM_EPS)
            yn, yr = kn * rr, krz * rr
            dk_h = jnp.where(valid, jnp.transpose(dk_ref[hd]) * LN_2, 0.0)
            dpn = dk_h[:, 0:D_NOPE]
            dpr = _rope_t(dk_h[:, D_NOPE:], cos_t, slo_t, shi_t)
            dkng_n = dkng_n + jnp.sum(dpn * yn, axis=0, keepdims=True)
            dkng_r = dkng_r + jnp.sum(dpr * yr, axis=0, keepdims=True)
            dyn, dyr = dpn * kg_n, dpr * kg_r
            proj = (jnp.sum(dyn * yn, axis=-1, keepdims=True) + jnp.sum(dyr * yr, axis=-1, keepdims=True)) * (1.0 / D_HEAD)
            dkv_parts.append(rr * (dyn - yn * proj))
            dkv_parts.append(jnp.where(valid, jnp.transpose(dv_ref[hd]), 0.0))
            dkr = dkr + rr * (dyr - yr * proj)
        dkng_ref[:, 0:D_NOPE] += dkng_n
        dkng_ref[:, D_NOPE:] += dkng_r
        dkvf = jnp.concatenate(dkv_parts, axis=1).astype(BF16)

        lat_t = jnp.maximum(i - 1, 0)
        dpl_t = dpl_ref[...].astype(F32)
        ds = jnp.concatenate([jnp.where(i > 1, hb_ref[...].astype(F32), 0.0), dpl_t,
                              jnp.where(i < n_tiles - 1, ha_ref[...].astype(F32), 0.0)], axis=0)
        tpos = lat_t * TILE - HALO + lax.broadcasted_iota(jnp.int32, (TILE + 2 * HALO, 1), 0)
        for g, w in enumerate(POOL_WINDOWS):
            sl = slice(g * GROUP_DIM, (g + 1) * GROUP_DIM)
            wsum = _window_sum(ds[:, sl] * _inv_count(tpos, w, seq), w, True)[HALO:HALO + TILE]
            du_ref[:, O_PIN + g * GROUP_DIM:O_PIN + (g + 1) * GROUP_DIM] = jnp.where(
                is_lat, wsum - dpl_t[:, sl], 0.0).astype(BF16)

        du_ref[:, O_GA:O_GA + D_ATTN] = jnp.where(is_lat, dga_ref[...], jnp.zeros((), BF16))
        du_ref[:, O_GP:O_GP + D_POOL] = jnp.where(is_lat, dgp_ref[...], jnp.zeros((), BF16))
        du_ref[:, O_KR:U_PAD] = dkr.astype(BF16)

        gwuq_ref[...] += _dot_tn(dqf, qnb)
        dqn = _dot(dqf, wuq_ref[...])
        gwukv_ref[...] += _dot_tn(dkvf, kvnb)
        dkvn = _dot_nt(dkvf, wukv_ref[...])
        dqlg_ref[...] += jnp.sum(dqn * qhat, axis=0, keepdims=True)
        dqhat = dqn * qlg_ref[...]
        dcq = rq * (dqhat - qhat * jnp.mean(dqhat * qhat, axis=-1, keepdims=True))
        dkvlg_ref[...] += jnp.sum(dkvn * kvhat, axis=0, keepdims=True)
        dkvhat = dkvn * kvlg_ref[...]
        dckv = rkv * (dkvhat - kvhat * jnp.mean(dkvhat * kvhat, axis=-1, keepdims=True))
        du_ref[:, O_CQ:O_CQ + R_Q] = dcq.astype(BF16)
        du_ref[:, O_CKV:O_CKV + R_KV] = dckv.astype(BF16)

    t1 = lambda s: jnp.minimum(s, n_tiles - 1)
    t2 = lambda s: jnp.clip(s - 1, 0, n_tiles - 1)
    t3 = lambda s: jnp.clip(s - 2, 0, n_tiles - 1)
    latent = lambda t: jnp.maximum(t - 1, 0)
    lat = lambda s: (latent(t1(s)), 0)
    lat3 = lambda s: (latent(t3(s)), 0)
    full = lambda shape: pl.BlockSpec(shape, lambda s: (0,) * len(shape))
    rope_spec = pl.BlockSpec((TILE, 128), lambda s: (t1(s), 0))
    return pl.pallas_call(
        body, name="inproj_bwd", grid=(n_tiles + 2,),
        out_shape=(jax.ShapeDtypeStruct((seq, D_MODEL), F32), jax.ShapeDtypeStruct((D_IN_PROJ, D_MODEL), F32),
                   jax.ShapeDtypeStruct((N_HEADS * D_HEAD_PAD, R_Q), F32), jax.ShapeDtypeStruct((N_HEADS * 256, R_KV), F32),
                   jax.ShapeDtypeStruct((1, R_Q), F32), jax.ShapeDtypeStruct((1, R_KV), F32),
                   jax.ShapeDtypeStruct((1, D_HEAD_PAD), F32), jax.ShapeDtypeStruct((1, D_HEAD_PAD), F32),
                   jax.ShapeDtypeStruct((1, D_MODEL), F32), jax.ShapeDtypeStruct((8, D_MODEL), F32)),
        in_specs=[pl.BlockSpec((TILE, D_MODEL), lambda s: (latent(t2(s)), 0)), pl.BlockSpec((TILE, D_MODEL), lat3),
                  full((TILE, D_MODEL)), full((8, D_MODEL)), full((8, D_MODEL)),
                  full((1, D_MODEL)), full((D_IN_PROJ, D_MODEL)), full((1, R_Q)), full((N_HEADS * D_HEAD_PAD, R_Q)),
                  full((1, R_KV)), full((R_KV, N_HEADS * 256)), full((1, D_HEAD_PAD)), full((1, D_HEAD_PAD)),
                  rope_spec, rope_spec, rope_spec,
                  pl.BlockSpec((TILE, R_Q), lambda s: (t1(s), (O_CQ - N_GATES) // R_Q)),
                  pl.BlockSpec((TILE, R_KV), lambda s: (t1(s), (O_CKV - N_GATES) // R_KV)),
                  pl.BlockSpec((TILE, 128), lambda s: (t1(s), (O_KR - N_GATES) // 128)),
                  pl.BlockSpec((N_HEADS, TILE, D_HEAD_PAD), lambda s: (0, latent(t1(s)), 0)),
                  pl.BlockSpec((N_HEADS, D_HEAD_PAD, TILE), lambda s: (0, 0, t1(s))),
                  pl.BlockSpec((N_HEADS, D_V, TILE), lambda s: (0, 0, t1(s))),
                  pl.BlockSpec((TILE, D_ATTN), lat), pl.BlockSpec((TILE, D_POOL), lat), pl.BlockSpec((TILE, D_POOL), lat),
                  pl.BlockSpec((HALO, D_POOL), lambda s: (jnp.maximum((t1(s) - 1) * rows8 - 1, 0), 0)),
                  pl.BlockSpec((HALO, D_POOL), lambda s: (jnp.minimum(jnp.maximum(t1(s), 1) * rows8, last8), 0)),
                  pl.BlockSpec((TILE, D_MODEL), lat3)],
        out_specs=(pl.BlockSpec((TILE, D_MODEL), lat3), full((D_IN_PROJ, D_MODEL)), full((N_HEADS * D_HEAD_PAD, R_Q)),
                   full((N_HEADS * 256, R_KV)), full((1, R_Q)), full((1, R_KV)), full((1, D_HEAD_PAD)),
                   full((1, D_HEAD_PAD)), full((1, D_MODEL)), full((8, D_MODEL))),
        scratch_shapes=[pltpu.VMEM((TILE, U_PAD), BF16), pltpu.VMEM((TILE, U_PAD), BF16),
                        pltpu.VMEM((TILE, D_MODEL), F32), pltpu.VMEM((TILE, D_MODEL), F32)],
        compiler_params=pltpu.CompilerParams(dimension_semantics=("arbitrary",), vmem_limit_bytes=VMEM_LIMIT),
    )(x, x, ctx, modrows, bmodrows, norm_g, w_in_p, q_lora_g, w_uq_p, kv_lora_g, w_ukv, qng_p, kng_p, cos, slo, shi,
      u, u, u, dq, dk, dv, dga, dgp, dpl, dpl, dpl, g1)


SMALL_LAYOUT = ((0, D_MODEL), (1, R_Q), (2, R_KV), (3, D_HEAD_PAD), (4, D_HEAD_PAD), (5, D_POOL))
ROW_DMOD_B, ROW_DMOD_C, ROW_LOSS = 6, 9, 12


def _epilogue(parts, landed, smalls, dmod4, dgate, loss_acc, w_mod_l):
    n_a, n_l, n_s = len(parts), len(landed), len(smalls)
    n_mod = w_mod_l.shape[1]
    blk = [p.shape[1:] for p in parts]
    small_out = [D_MODEL, R_Q, R_KV, D_HEAD, D_HEAD, D_POOL]

    def body(*refs):
        part_refs = refs[0:n_a]
        land_refs = refs[n_a:n_a + n_l]
        small_in = refs[n_a + n_l:n_a + n_l + n_s]
        dmod4_ref, dgate_ref, loss_ref, wmod_ref = refs[n_a + n_l + n_s:n_a + n_l + n_s + 4]
        outs = refs[n_a + n_l + n_s + 4:]
        red_refs = outs[0:n_a]
        landsum_refs = outs[n_a:n_a + n_l]
        ccg_ref = outs[n_a + n_l]
        sum_refs = outs[n_a + n_l + 1:n_a + n_l + 1 + n_s]
        gbmod_ref, dmy_ref, lossout_ref = outs[n_a + n_l + 1 + n_s:n_a + n_l + 4 + n_s]
        scr = outs[n_a + n_l + 4 + n_s:]
        recv1, own1, sum1b, recv2 = scr[0:n_a], scr[n_a:2 * n_a], scr[2 * n_a:3 * n_a], scr[3 * n_a:4 * n_a]
        small_ref, smallg_ref, tot_ref, cc_ref = scr[4 * n_a:4 * n_a + 4]
        ssem1, rsem1, lsem, ssem2, rsem2, ssem_s, rsem_s, ssem_cc, rsem_cc = scr[4 * n_a + 4:]
        x, y, c = _coords()
        me = (x, y, c)
        sibling = (x, y, 1 - c)

        small_ref[...] = jnp.zeros_like(small_ref)
        for ref, (row, width) in zip(small_in, SMALL_LAYOUT):
            small_ref[row:row + 1, 0:width] = ref[...]
        small_ref[ROW_DMOD_B:ROW_DMOD_B + 2, :] = dmod4_ref[0:2, :]
        small_ref[ROW_DMOD_B + 2:ROW_DMOD_B + 3, :] = dgate_ref[...]
        small_ref[ROW_DMOD_C:ROW_DMOD_C + 2, :] = dmod4_ref[2:4, :]
        small_ref[ROW_LOSS:ROW_LOSS + 1, 0:128] = loss_ref[0:1, :]
        smallg_ref[_lin(me)] = small_ref[...]
        s_recv, s_send = _gather_to_all(small_ref, smallg_ref, ssem_s, rsem_s)

        stage1, own_copies = [], []
        for a in range(n_a):
            for b in range(4):
                dev = 4 * (b >> 1) + 2 * (b & 1)
                stage1.append(pltpu.make_async_remote_copy(
                    src_ref=part_refs[a].at[dev + (1 - c)], dst_ref=recv1[a].at[b],
                    send_sem=ssem1.at[a, b], recv_sem=rsem1.at[a, b], device_id=sibling, device_id_type=MESH))
                own_copies.append(pltpu.make_async_copy(part_refs[a].at[dev + c], own1[a].at[b], lsem.at[a, b]))
                stage1[-1].start()
                own_copies[-1].start()

        s_recv()
        tot = smallg_ref[0]
        for d in range(1, N_DEV):
            tot = tot + smallg_ref[d]
        tot_ref[...] = tot
        for ref, (row, _), width in zip(sum_refs, SMALL_LAYOUT, small_out):
            ref[...] = tot_ref[row:row + 1, 0:width]
        lossout_ref[...] = tot_ref[8:16, 0:128]
        lin = _lin(me)

        def my_columns(three_rows):
            v = jnp.concatenate(three_rows, axis=1)
            out = jnp.zeros((1, n_mod), F32)
            for d in range(N_DEV):
                out = out + jnp.where(lin == d, v[:, d * n_mod:(d + 1) * n_mod], 0.0)
            return out

        rows3 = lambda ref, r0: [ref[r0 + t:r0 + t + 1, :] for t in range(3)]
        dmodc = rows3(tot_ref, ROW_DMOD_C)
        gbmod_ref[...] = jnp.concatenate(rows3(tot_ref, ROW_DMOD_B), axis=1) + jnp.concatenate(dmodc, axis=1)
        dmy_ref[...] = jnp.zeros_like(dmy_ref)
        for b in range(N_DEV):
            dmy_ref[b:b + 1, :] = my_columns(rows3(smallg_ref.at[b], ROW_DMOD_B))
        dmy = my_columns(dmodc)
        dmy_ref[N_DEV:N_DEV + 1, :] = dmy
        part = lax.dot_general(jnp.broadcast_to(dmy, (8, n_mod)), wmod_ref[...], (((1,), (1,)), ((), ())),
                               precision=HIGHEST, preferred_element_type=F32)

        cc_ref[...] = part
        ccg_ref[_lin(me)] = part
        cc_recv, cc_send = _gather_to_all(cc_ref, ccg_ref, ssem_cc, rsem_cc)

        stage2 = []
        for a in range(n_a):
            for b in range(4):
                stage1[4 * a + b].wait_recv()
                own_copies[4 * a + b].wait()
                sum1b[a][b] = (own1[a][b] + recv1[a][b]).astype(BF16)
            for k in (1, 2, 3):
                tx, ty = _flip(x, (k >> 1) & 1), _flip(y, k & 1)
                stage2.append(pltpu.make_async_remote_copy(
                    src_ref=sum1b[a].at[2 * tx + ty], dst_ref=recv2[a].at[k - 1], send_sem=ssem2.at[a, k - 1],
                    recv_sem=rsem2.at[a, k - 1], device_id=(tx, ty, c), device_id_type=MESH))
                stage2[-1].start()
        for a in range(n_a):
            own = own1[a][2 * x + y] + recv1[a][2 * x + y]
            for k in (1, 2, 3):
                stage2[3 * a + k - 1].wait_recv()
                own = own + recv2[a][k - 1].astype(F32)
            red_refs[a][...] = own

        for ref, out in zip(land_refs, landsum_refs):
            acc = ref[0].astype(F32)
            for d in range(1, N_DEV):
                acc = acc + ref[d].astype(F32)
            out[...] = acc
        cc_recv()
        for cp in stage1 + stage2:
            cp.wait_send()
        s_send()
        cc_send()

    vm = pl.BlockSpec(memory_space=pltpu.VMEM)
    sds = jax.ShapeDtypeStruct
    return pl.pallas_call(
        body, name="epilogue_reduce",
        out_shape=(*[sds(s, F32) for s in blk], *[sds(a.shape[1:], F32) for a in landed], sds((N_DEV, 8, D_MODEL), F32),
                   *[sds((1, w), F32) for w in small_out], sds((1, 3 * D_MODEL), F32), sds((16, n_mod), F32),
                   sds((8, 128), F32)),
        in_specs=[pl.BlockSpec(memory_space=pl.ANY)] * n_a + [vm] * (n_l + n_s + 4),
        out_specs=tuple([vm] * (n_a + n_l + n_s + 4)),
        scratch_shapes=[*[pltpu.VMEM((4,) + s, F32) for s in blk], *[pltpu.VMEM((4,) + s, F32) for s in blk],
                        *[pltpu.VMEM((4,) + s, BF16) for s in blk], *[pltpu.VMEM((3,) + s, BF16) for s in blk],
                        pltpu.VMEM((SMALL_ROWS, D_MODEL), F32), pltpu.VMEM((N_DEV, SMALL_ROWS, D_MODEL), F32),
                        pltpu.VMEM((SMALL_ROWS, D_MODEL), F32), pltpu.VMEM((8, D_MODEL), F32),
                        pltpu.SemaphoreType.DMA((n_a, 4)), pltpu.SemaphoreType.DMA((n_a, 4)), pltpu.SemaphoreType.DMA((n_a, 4)),
                        pltpu.SemaphoreType.DMA((n_a, 3)), pltpu.SemaphoreType.DMA((n_a, 3)),
                        pltpu.SemaphoreType.DMA((7,)), pltpu.SemaphoreType.DMA((7,)),
                        pltpu.SemaphoreType.DMA((7,)), pltpu.SemaphoreType.DMA((7,))],
        compiler_params=pltpu.CompilerParams(vmem_limit_bytes=VMEM_LIMIT),
    )(*parts, *landed, *smalls, dmod4, dgate, loss_acc, w_mod_l)


def _adamw(weights, grads, ms, vs, c_all_t, dmod_my, p2g):
    n = len(weights)

    def body(*refs):
        w_refs = refs[0:n]
        g_in = refs[n:2 * n - 2]
        m_refs = refs[2 * n - 2:3 * n - 2]
        v_refs = refs[3 * n - 2:4 * n - 2]
        cat_ref, dmod_ref, p2g_ref = refs[4 * n - 2:4 * n + 1]
        outs = refs[4 * n + 1:]
        g_refs, d_refs, nm_refs, nv_refs = outs[0:n], outs[n:2 * n], outs[2 * n:3 * n], outs[3 * n:4 * n]
        gcc_ref, gwm_ref = g_refs[0], g_refs[1]

        part = p2g_ref[0, 0:1, :]
        for d in range(1, N_DEV):
            part = part + p2g_ref[d, 0:1, :]
        cc = w_refs[0][...]
        sg = _sigmoid(cc)
        gcc_ref[...] = part * (sg * (1.0 + cc * (1.0 - sg)))
        cat = cat_ref[...]
        gwm_ref[...] = lax.dot_general(cat * _sigmoid(cat), dmod_ref[...], (((1,), (0,)), ((), ())), precision=HIGHEST,
                                       preferred_element_type=F32)
        for idx in range(n):
            if idx >= 2:
                g_refs[idx][...] = g_in[idx - 2][...]
            g = g_refs[idx][...]
            m = ADAM_B1 * m_refs[idx][...] + (1.0 - ADAM_B1) * g
            v = ADAM_B2 * v_refs[idx][...] + (1.0 - ADAM_B2) * (g * g)
            m_hat = m / (1.0 - ADAM_B1 ** ADAM_STEP)
            v_hat = v / (1.0 - ADAM_B2 ** ADAM_STEP)
            d_refs[idx][...] = -ADAM_LR * (m_hat / (jnp.sqrt(v_hat) + ADAM_EPS) + ADAM_WD * w_refs[idx][...])
            nm_refs[idx][...] = m
            nv_refs[idx][...] = v

    vm = pl.BlockSpec(memory_space=pltpu.VMEM)
    shapes = [jax.ShapeDtypeStruct(w.shape, F32) for w in weights]
    args = list(weights) + list(grads[2:]) + list(ms) + list(vs) + [c_all_t, dmod_my, p2g]
    out_shape = tuple(shapes * 4)
    return pl.pallas_call(
        body, name="adamw_update", out_shape=out_shape, in_specs=[vm] * len(args), out_specs=tuple([vm] * len(out_shape)),
        compiler_params=pltpu.CompilerParams(vmem_limit_bytes=VMEM_LIMIT),
    )(*args)


def _rope_tables(seq):
    rows = seq // GRID_W
    row = np.repeat(np.arange(rows, dtype=np.float32), GRID_W)
    col = np.tile(np.arange(GRID_W, dtype=np.float32), rows)
    n_freq = D_ROPE // 4
    inv = (np.float32(ROPE_BASE) ** (-np.arange(n_freq, dtype=np.float32) / np.float32(n_freq))).astype(np.float32)
    ang_r = (row[:, None] * inv).astype(np.float32)
    ang_c = (col[:, None] * inv).astype(np.float32)
    ang = np.concatenate([ang_r, ang_r, ang_c, ang_c], axis=-1)
    cos, sin = np.cos(ang).astype(np.float32), np.sin(ang).astype(np.float32)
    low = (np.arange(D_ROPE) % 32) < 16

    def table(t, fill):
        out = np.full((TILE + seq, 128), fill, np.float32)
        out[TILE:, 0:D_ROPE] = t
        return jnp.asarray(out)

    return table(cos, 1.0), table(np.where(low, -sin, 0.0), 0.0), table(np.where(low, 0.0, sin), 0.0)


def kernel(x, c, ctx, c_ctx, w_mod, b_mod, norm_g, w_in, q_lora_g, w_uq, kv_lora_g, w_ukv, q_norm_g, k_norm_g, w_pool, pool_scale, w_out, loss_target, m_c_ctx, m_w_mod, m_b_mod, m_norm_g, m_w_in, m_q_lora_g, m_w_uq, m_kv_lora_g, m_w_ukv, m_q_norm_g, m_k_norm_g, m_w_pool, m_pool_scale, m_w_out, v_c_ctx, v_w_mod, v_b_mod, v_norm_g, v_w_in, v_q_lora_g, v_w_uq, v_kv_lora_g, v_w_ukv, v_q_norm_g, v_k_norm_g, v_w_pool, v_pool_scale, v_w_out):
    seq = x.shape[1]
    assert ctx.shape[1] == TILE and seq % TILE == 0 and seq % GRID_W == 0
    ix, iy, ic = lax.axis_index("x"), lax.axis_index("y"), lax.axis_index("c")
    me = 4 * ix + 2 * iy + ic
    x2, ctx2, tgt2 = x[0], ctx[0], loss_target[0]
    w_mod_l, w_ukv_l, w_out_l = w_mod[0], w_ukv[0], w_out[0]
    c_ctx_row = c_ctx.reshape(1, D_MODEL)

    tr = lambda a: jnp.transpose(a[0])
    w_in_tl, w_uq_tl = tr(w_in), tr(w_uq)
    cg, modg, wg_in, wg_uq, wg_ukv = _prologue(
        c, c_ctx_row, w_mod_l,
        [w_in_tl, w_uq_tl, w_ukv_l])
    mod_all = jnp.transpose(modg, (1, 0, 2)).reshape(16, 3 * D_MODEL)
    mod_b = lax.dynamic_slice_in_dim(mod_all, me, 1, axis=0).reshape(3, D_MODEL)
    mod_c = mod_all[8].reshape(3, D_MODEL)
    modrows = jnp.concatenate([mod_b, mod_c[0:2], jnp.zeros((3, D_MODEL), F32)], axis=0)
    b3 = b_mod.reshape(3, D_MODEL)
    bmodrows = jnp.concatenate([b3, b3[0:2], jnp.zeros((3, D_MODEL), F32)], axis=0)

    w_in_p = wg_in.reshape(D_IN_PROJ, D_MODEL)
    w_uq_p = jnp.concatenate([wg_uq.reshape(N_HEADS, D_HEAD, R_Q), jnp.zeros((N_HEADS, D_HEAD_PAD - D_HEAD, R_Q), BF16)],
                             axis=1).reshape(N_HEADS * D_HEAD_PAD, R_Q)
    w_ukv_f = jnp.transpose(wg_ukv, (1, 0, 2)).reshape(R_KV, N_HEADS * 256)
    qng_p = jnp.concatenate([q_norm_g, jnp.zeros((1, D_HEAD_PAD - D_HEAD), F32)], axis=1)
    kng_p = jnp.concatenate([k_norm_g, jnp.zeros((1, D_HEAD_PAD - D_HEAD), F32)], axis=1)
    cos, slo, shi = _rope_tables(seq)

    u_gates, u_mla, q, k, v = _inproj_fwd(x2, ctx2, modrows, bmodrows, norm_g, w_in_p, q_lora_g, w_uq_p, kv_lora_g, w_ukv_f,
                             qng_p, kng_p, cos, slo, shi)
    attn, lse, wg_out = _attn_fwd(q, k, v, min(2048, seq), w_out_l.astype(BF16))
    (loss_acc, g1, dgate, d_attn, dga, dgp, dpl, gwo_b, gwp, dps) = _mix(
        x2, tgt2, attn, u_gates, modrows, bmodrows, w_pool[0], pool_scale, wg_out.reshape(D_MODEL, D_MODEL))

    blocks = lambda a: a.reshape((N_DEV, a.shape[0] // N_DEV) + a.shape[1:])
    dq, dk, dv, land_wo, land_wp = _attn_bwd(q, k, v, attn, lse, d_attn.reshape(seq, D_ATTN), min(1024, seq), blocks(gwo_b),
                                             gwp.reshape(4 * GROUP_DIM, GROUP_DIM))
    (grad_x, gwin_t, gwuq_p, gwukv_t, dqlg, dkvlg, dqng, dkng, dng, dmod4) = _inproj_bwd(
        x2, ctx2, modrows, bmodrows, norm_g, w_in_p, q_lora_g, w_uq_p, kv_lora_g, w_ukv_f, qng_p, kng_p, cos, slo, shi,
        u_mla, dq, dk, dv, dga, dgp, dpl, g1)

    gwuq_t = gwuq_p.reshape(N_HEADS, D_HEAD_PAD, R_Q)[:, 0:D_HEAD].reshape(N_HEADS * D_HEAD, R_Q)
    parts = [blocks(gwin_t), blocks(gwuq_t), blocks(gwukv_t)]
    (r_win, r_wuq, r_wukv, r_wout, g_w_pool, ccg, g_ng, g_qlg, g_kvlg, g_qng, g_kng, g_ps, g_bmod, dmod_my,
     loss_rows) = _epilogue(parts, [land_wo, land_wp], [dng, dqlg, dkvlg, dqng, dkng, dps], dmod4, dgate, loss_acc, w_mod_l)

    g_w_ukv = jnp.transpose(r_wukv)
    c_rows = cg[:, 0, :]
    c_all_t = jnp.transpose(jnp.concatenate([c_rows, c_ctx_row, jnp.zeros((16 - N_DEV - 1, D_MODEL), F32)], axis=0))

    weights = [c_ctx_row, w_mod_l, b_mod, norm_g, w_in_tl, q_lora_g, w_uq_tl, kv_lora_g, w_ukv_l, q_norm_g, k_norm_g,
               w_pool.reshape(4 * GROUP_DIM, GROUP_DIM), pool_scale, w_out_l]
    grads = [None, None, g_bmod, g_ng, r_win, g_qlg, r_wuq, g_kvlg, g_w_ukv, g_qng, g_kng, g_w_pool, g_ps, r_wout]
    ms = [m_c_ctx.reshape(1, D_MODEL), m_w_mod[0], m_b_mod, m_norm_g, tr(m_w_in), m_q_lora_g, tr(m_w_uq), m_kv_lora_g,
          m_w_ukv[0], m_q_norm_g, m_k_norm_g, m_w_pool.reshape(4 * GROUP_DIM, GROUP_DIM), m_pool_scale, m_w_out[0]]
    vs = [v_c_ctx.reshape(1, D_MODEL), v_w_mod[0], v_b_mod, v_norm_g, tr(v_w_in), v_q_lora_g, tr(v_w_uq), v_kv_lora_g,
          v_w_ukv[0], v_q_norm_g, v_k_norm_g, v_w_pool.reshape(4 * GROUP_DIM, GROUP_DIM), v_pool_scale, v_w_out[0]]
    outs = _adamw(weights, grads, ms, vs, c_all_t, dmod_my, ccg)
    n = len(weights)
    grads, deltas, new_m, new_v = outs[0:n], outs[n:2 * n], outs[2 * n:3 * n], outs[3 * n:4 * n]

    loss = loss_rows[ROW_LOSS - 8, 0]
    final = [c_ctx.shape, w_mod.shape, b_mod.shape, norm_g.shape, w_in.shape, q_lora_g.shape, w_uq.shape, kv_lora_g.shape,
             w_ukv.shape, q_norm_g.shape, k_norm_g.shape, w_pool.shape, pool_scale.shape, w_out.shape]
    transposed = (4, 6)

    def shaped(arrs):
        return [(jnp.transpose(a) if i in transposed else a).reshape(s) for i, (a, s) in enumerate(zip(arrs, final))]

    return (loss, grad_x[None], *shaped(grads), *shaped(deltas), *shaped(new_m), *shaped(new_v))
```

```python
import numpy as np

import jax
import jax.numpy as jnp
from jax import lax
from jax.experimental import pallas as pl
from jax.experimental.pallas import tpu as pltpu

F32 = jnp.float32
BF16 = jnp.bfloat16
MESH = pl.DeviceIdType.MESH
HIGHEST = lax.Precision.HIGHEST

D_MODEL = 1024
N_HEADS = 4
D_NOPE = 128
D_ROPE = 64
D_HEAD = D_NOPE + D_ROPE
D_HEAD_PAD = 256
D_V = 128
R_Q = 256
R_KV = 128
D_ATTN = 512
D_POOL = 512
POOL_WINDOWS = (2, 4, 8, 16)
GROUP_DIM = 128
GRID_W = 64
ROPE_BASE = 10000.0
NORM_EPS = 1e-6
ATTN_SCALE = D_HEAD ** -0.5
LOG2_E = 1.4426950408889634
LN_2 = 0.6931471805599453
ATTN_ROWS = 256
D_IN_PROJ = 1984
U_PAD = 2048
O_GA, O_PIN, O_GP, O_CQ, O_CKV, O_KR = 0, 512, 1024, 1536, 1792, 1920
TILE = 256
MIX_TILE = 512
Q_BLOCK = 128
HALO = 16
N_GATES = 1536
N_MLA = D_IN_PROJ - N_GATES
N_DEV = 8
VMEM_LIMIT = 56 * 1024 * 1024

ADAM_LR = 0.001
ADAM_B1 = 0.9
ADAM_B2 = 0.999
ADAM_EPS = 1e-08
ADAM_WD = 0.01
ADAM_STEP = 10

SMALL_ROWS = 16


def _dot(a, b):
    return lax.dot_general(a, b, (((1,), (0,)), ((), ())), preferred_element_type=F32)


def _dot_nt(a, b):
    return lax.dot_general(a, b, (((1,), (1,)), ((), ())), preferred_element_type=F32)


def _dot_tn(a, b):
    return lax.dot_general(a, b, (((0,), (0,)), ((), ())), preferred_element_type=F32)


def _sigmoid(x):
    return 1.0 / (1.0 + jnp.exp(-x))


def _rope(p, cos, slo, shi):
    return p * cos + pltpu.roll(p, 112, 1) * slo + pltpu.roll(p, 16, 1) * shi


def _rope_t(d, cos, slo, shi):
    return d * cos + pltpu.roll(d * slo, 16, 1) + pltpu.roll(d * shi, 112, 1)


def _shift_rows(a, k):
    n = a.shape[0]
    return pltpu.roll(a, (-k) % n, 0)


def _window_sum(x, w, transposed):
    s = (x + _shift_rows(x, 1)) if transposed else (_shift_rows(x, -1) + x)
    step = 1
    while 2 * step < w:
        s = _shift_rows(s, -step) + _shift_rows(s, step)
        step *= 2
    return s


def _inv_count(tpos, w, seq):
    lo = jnp.maximum(tpos - w // 2, 0)
    hi = jnp.minimum(tpos - w // 2 + w, seq)
    return 1.0 / jnp.maximum(hi - lo, 1).astype(F32)


def _coords():
    return lax.axis_index("x"), lax.axis_index("y"), lax.axis_index("c")


def _flip(v, bit):
    return (1 - v) if bit else v


def _peer(k):
    x, y, c = _coords()
    return (_flip(x, (k >> 2) & 1), _flip(y, (k >> 1) & 1), _flip(c, k & 1))


def _lin(p):
    return 4 * p[0] + 2 * p[1] + p[2]


def _gather_to_all(src_ref, slots_ref, send_sems, recv_sems):
    me = _coords()
    copies = []
    for k in range(1, N_DEV):
        cp = pltpu.make_async_remote_copy(
            src_ref=src_ref, dst_ref=slots_ref.at[_lin(me)], send_sem=send_sems.at[k - 1], recv_sem=recv_sems.at[k - 1],
            device_id=_peer(k), device_id_type=MESH)
        cp.start()
        copies.append(cp)

    def wait_recv():
        for k in range(1, N_DEV):
            pltpu.make_async_remote_copy(
                src_ref=src_ref, dst_ref=slots_ref.at[_lin(_peer(k))], send_sem=send_sems.at[k - 1],
                recv_sem=recv_sems.at[k - 1], device_id=_peer(k), device_id_type=MESH).wait_recv()

    def wait_send():
        for cp in copies:
            cp.wait_send()

    return wait_recv, wait_send


def _prologue(c8, cctx8, w_mod_l, shards):
    n_mod = w_mod_l.shape[1]
    n_w = len(shards)

    def body(c_ref, cctx_ref, wmod_ref, *refs):
        w_refs = refs[0:n_w]
        cg_ref, modg_ref = refs[n_w], refs[n_w + 1]
        wg_refs = refs[n_w + 2:2 * n_w + 2]
        modblk_ref = refs[2 * n_w + 2]
        wb_refs = refs[2 * n_w + 3:3 * n_w + 3]
        c8_ref = refs[3 * n_w + 3]
        ssem_w, rsem_w, lsem_w, ssem_c, rsem_c, ssem_m, rsem_m = refs[3 * n_w + 4:]
        x, y, c = _coords()
        me = (x, y, c)
        sibling = (x, y, 1 - c)
        chips = [(1 - x, y), (x, 1 - y), (1 - x, 1 - y)]

        def wcopies(k, block, to, own=False):
            out = []
            for a in range(n_w):
                slot = wg_refs[a].at[_lin(block)]
                out.append(pltpu.make_async_remote_copy(
                    src_ref=wb_refs[a] if own else slot, dst_ref=slot, send_sem=ssem_w.at[a, k], recv_sem=rsem_w.at[a, k],
                    device_id=to, device_id_type=MESH))
            return out

        c8_ref[...] = jnp.broadcast_to(c_ref[...], (8, D_MODEL))
        cg_ref[_lin(me)] = c8_ref[...]
        c_recv, c_send = _gather_to_all(c8_ref, cg_ref, ssem_c, rsem_c)

        for a in range(n_w):
            wb_refs[a][...] = w_refs[a][...].astype(BF16)
        first = wcopies(0, me, sibling, own=True)
        for j, chip in enumerate(chips):
            first += wcopies(1 + j, me, (*chip, c), own=True)
        late = [idx for idx in range(len(first)) if idx % n_w == 0 and idx >= n_w]
        for idx, cp in enumerate(first):
            if idx not in late:
                cp.start()
        mine = [pltpu.make_async_copy(wb_refs[a], wg_refs[a].at[_lin(me)], lsem_w.at[a]) for a in range(n_w)]
        for cp in mine:
            cp.start()

        c_recv()
        row = lax.broadcasted_iota(jnp.int32, (8, D_MODEL), 0)
        c_all = jnp.zeros((8, D_MODEL), F32)
        for d in range(N_DEV):
            c_all = c_all + jnp.where(row == d, cg_ref[d], 0.0)
        cc = jnp.broadcast_to(cctx_ref[...], (8, D_MODEL))
        a = jnp.concatenate([c_all * _sigmoid(c_all), cc * _sigmoid(cc)], axis=0)
        modblk_ref[...] = _dot(a.astype(BF16), wmod_ref[...].astype(BF16))
        modg_ref[_lin(me)] = modblk_ref[...]
        m_recv, m_send = _gather_to_all(modblk_ref, modg_ref, ssem_m, rsem_m)
        for idx in late:
            first[idx].start()
        m_recv()

        passed = []
        for j, chip in enumerate(chips):
            for cp in wcopies(1 + j, (*chip, c), me):
                cp.wait_recv()
            fwd = wcopies(4 + j, (*chip, c), sibling)
            for cp in fwd:
                cp.start()
            passed += fwd
        for cp in wcopies(0, sibling, me):
            cp.wait_recv()
        for j, chip in enumerate(chips):
            for cp in wcopies(4 + j, (*chip, 1 - c), me):
                cp.wait_recv()
        for cp in first + passed:
            cp.wait_send()
        for cp in mine:
            cp.wait()
        c_send()
        m_send()

    vm = pl.BlockSpec(memory_space=pltpu.VMEM)
    hbm = pl.BlockSpec(memory_space=pl.ANY)
    return pl.pallas_call(
        body, name="prologue_gather",
        out_shape=(jax.ShapeDtypeStruct((N_DEV, 8, D_MODEL), F32), jax.ShapeDtypeStruct((N_DEV, 16, n_mod), F32),
                   *[jax.ShapeDtypeStruct((N_DEV,) + s.shape, BF16) for s in shards]),
        in_specs=[vm] * (3 + n_w), out_specs=(vm, vm, *[hbm] * n_w),
        scratch_shapes=[pltpu.VMEM((16, n_mod), F32), *[pltpu.VMEM(s.shape, BF16) for s in shards],
                        pltpu.VMEM((8, D_MODEL), F32),
                        pltpu.SemaphoreType.DMA((n_w, 7)), pltpu.SemaphoreType.DMA((n_w, 7)), pltpu.SemaphoreType.DMA((n_w,)),
                        pltpu.SemaphoreType.DMA((7,)), pltpu.SemaphoreType.DMA((7,)),
                        pltpu.SemaphoreType.DMA((7,)), pltpu.SemaphoreType.DMA((7,))],
        compiler_params=pltpu.CompilerParams(vmem_limit_bytes=VMEM_LIMIT),
    )(c8, cctx8, w_mod_l, *shards)


def _modulated_input(is_ctx, x_ref, ctx_ref, mod_ref, bmod_ref, ng_ref):
    xt = jnp.where(is_ctx, ctx_ref[...], x_ref[...])
    shift = jnp.where(is_ctx, mod_ref[3:4, :] + bmod_ref[3:4, :], mod_ref[0:1, :] + bmod_ref[0:1, :])
    scale = jnp.where(is_ctx, mod_ref[4:5, :] + bmod_ref[4:5, :], mod_ref[1:2, :] + bmod_ref[1:2, :])
    r = lax.rsqrt(jnp.mean(xt * xt, axis=-1, keepdims=True) + NORM_EPS)
    xg = (xt * r) * ng_ref[...]
    h = xg * (1.0 + scale) + shift
    return xt, r, xg, h, scale


def _inproj_fwd(x, ctx, modrows, bmodrows, norm_g, w_in_p, q_lora_g, w_uq_p, kv_lora_g, w_ukv, qng_p, kng_p, cos, slo, shi):
    seq = x.shape[0]
    n_tiles = seq // TILE + 1
    tot = seq + TILE

    n_mla = R_Q + R_KV + 128

    def body(x_ref, ctx_ref, mod_ref, bmod_ref, ng_ref, win_ref, qlg_ref, wuq_ref, kvlg_ref, wukv_ref, qng_ref, kng_ref,
             cos_ref, slo_ref, shi_ref, ug_ref, um_ref, q_ref, k_ref, v_ref, stash):
        s = pl.program_id(0)

        @pl.when(s == 0)
        def _():
            stash[...] = jnp.zeros_like(stash)

        refs = (x_ref, ctx_ref, mod_ref, bmod_ref, ng_ref, win_ref, qlg_ref, wuq_ref, kvlg_ref, wukv_ref, qng_ref, kng_ref,
                cos_ref, slo_ref, shi_ref, ug_ref, um_ref, q_ref, k_ref, v_ref)
        stages(s, refs, stash, stash)

    def stages(s, refs, fill, prev_ref):
        (x_ref, ctx_ref, mod_ref, bmod_ref, ng_ref, win_ref, qlg_ref, wuq_ref, kvlg_ref, wukv_ref, qng_ref, kng_ref,
         cos_ref, slo_ref, shi_ref, ug_ref, um_ref, q_ref, k_ref, v_ref) = refs
        cos_t, slo_t, shi_t = cos_ref[...], slo_ref[...], shi_ref[...]
        prev = prev_ref[...]
        cq = prev[:, 0:R_Q]
        qn = cq * lax.rsqrt(jnp.mean(cq * cq, axis=-1, keepdims=True) + NORM_EPS) * qlg_ref[...]
        q = _dot_nt(qn.astype(BF16), wuq_ref[...])
        qg = qng_ref[...] * (ATTN_SCALE * LOG2_E)
        for hd in range(N_HEADS):
            qh = q[:, hd * D_HEAD_PAD:(hd + 1) * D_HEAD_PAD]
            rr = lax.rsqrt(jnp.sum(qh * qh, axis=-1, keepdims=True) * (1.0 / D_HEAD) + NORM_EPS)
            qy = qh * rr * qg
            q_ref[hd, :, 0:D_NOPE] = qy[:, 0:D_NOPE].astype(BF16)
            q_ref[hd, :, D_NOPE:D_HEAD_PAD] = _rope(qy[:, D_NOPE:D_HEAD_PAD], cos_t, slo_t, shi_t).astype(BF16)

        ckv = prev[:, R_Q:R_Q + R_KV]
        kvn = ckv * lax.rsqrt(jnp.mean(ckv * ckv, axis=-1, keepdims=True) + NORM_EPS) * kvlg_ref[...]
        kv = _dot(kvn.astype(BF16), wukv_ref[...])
        krz = prev[:, R_Q + R_KV:n_mla]
        kr_ss = jnp.sum(krz * krz, axis=-1, keepdims=True)
        kg = kng_ref[...]
        for hd in range(N_HEADS):
            kn = kv[:, hd * 256:hd * 256 + D_NOPE]
            rr = lax.rsqrt((jnp.sum(kn * kn, axis=-1, keepdims=True) + kr_ss) * (1.0 / D_HEAD) + NORM_EPS)
            k_ref[hd, :, 0:D_NOPE] = (kn * rr * kg[:, 0:D_NOPE]).astype(BF16)
            k_ref[hd, :, D_NOPE:D_HEAD_PAD] = _rope(krz * rr * kg[:, D_NOPE:D_HEAD_PAD], cos_t, slo_t, shi_t).astype(BF16)
            v_ref[hd] = kv[:, hd * 256 + D_NOPE:(hd + 1) * 256].astype(BF16)

        _, _, _, h, _ = _modulated_input(s == 0, x_ref, ctx_ref, mod_ref, bmod_ref, ng_ref)
        hb = h.astype(BF16)
        ug_ref[...] = _dot_nt(hb, win_ref[N_MLA:D_IN_PROJ, :]).astype(BF16)
        um = _dot_nt(hb, win_ref[0:n_mla, :])
        lane = lax.broadcasted_iota(jnp.int32, (TILE, 128), 1)
        um = jnp.concatenate([um[:, 0:R_Q + R_KV], jnp.where(lane < D_ROPE, um[:, R_Q + R_KV:n_mla], 0.0)], axis=1)
        um_ref[...] = um
        fill[...] = um

    first = lambda s: jnp.minimum(s, n_tiles - 1)
    second = lambda s: jnp.maximum(s - 1, 0)
    latent = lambda t: jnp.maximum(t - 1, 0)
    full = lambda shape: pl.BlockSpec(shape, lambda s: (0,) * len(shape))
    rope_spec = pl.BlockSpec((TILE, 128), lambda s: (second(s), 0))
    return pl.pallas_call(
        body, name="inproj_fwd", grid=(n_tiles + 1,),
        out_shape=(jax.ShapeDtypeStruct((seq, N_GATES), BF16), jax.ShapeDtypeStruct((tot, n_mla), F32),
                   jax.ShapeDtypeStruct((N_HEADS, seq, D_HEAD_PAD), BF16),
                   jax.ShapeDtypeStruct((N_HEADS, tot, D_HEAD_PAD), BF16),
                   jax.ShapeDtypeStruct((N_HEADS, tot, D_V), BF16)),
        in_specs=[pl.BlockSpec((TILE, D_MODEL), lambda s: (latent(first(s)), 0)), full((TILE, D_MODEL)), full((8, D_MODEL)),
                  full((8, D_MODEL)), full((1, D_MODEL)), full((D_IN_PROJ, D_MODEL)), full((1, R_Q)),
                  full((N_HEADS * D_HEAD_PAD, R_Q)), full((1, R_KV)), full((R_KV, N_HEADS * 256)), full((1, D_HEAD_PAD)),
                  full((1, D_HEAD_PAD)), rope_spec, rope_spec, rope_spec],
        out_specs=(pl.BlockSpec((TILE, N_GATES), lambda s: (latent(first(s)), 0)),
                   pl.BlockSpec((TILE, n_mla), lambda s: (first(s), 0)),
                   pl.BlockSpec((N_HEADS, TILE, D_HEAD_PAD), lambda s: (0, latent(second(s)), 0)),
                   pl.BlockSpec((N_HEADS, TILE, D_HEAD_PAD), lambda s: (0, second(s), 0)),
                   pl.BlockSpec((N_HEADS, TILE, D_V), lambda s: (0, second(s), 0))),
        scratch_shapes=[pltpu.VMEM((TILE, n_mla), F32)],
        compiler_params=pltpu.CompilerParams(dimension_semantics=("arbitrary",), vmem_limit_bytes=VMEM_LIMIT),
    )(x, ctx, modrows, bmodrows, norm_g, w_in_p, q_lora_g, w_uq_p, kv_lora_g, w_ukv, qng_p, kng_p, cos, slo, shi)


def _hosted_exchange(first, last, items, send_sems, recv_sems, local_sems):
    me = _lin(_coords())

    def remote(n, k, src, land, scatter):
        peer = _peer(k)
        return pltpu.make_async_remote_copy(
            src_ref=src.at[_lin(peer)] if scatter else src, dst_ref=land.at[me], send_sem=send_sems.at[n, k - 1],
            recv_sem=recv_sems.at[n, k - 1], device_id=peer, device_id_type=MESH)

    def local(n, src, land, scatter):
        return pltpu.make_async_copy(src.at[me] if scatter else src, land.at[me], local_sems.at[n])

    @pl.when(first)
    def _():
        for n, (src, land, scatter) in enumerate(items):
            for k in range(1, N_DEV):
                remote(n, k, src, land, scatter).start()
            local(n, src, land, scatter).start()

    @pl.when(last)
    def _():
        for n, (src, land, scatter) in enumerate(items):
            for k in range(1, N_DEV):
                peer = _peer(k)
                pltpu.make_async_remote_copy(
                    src_ref=src.at[0] if scatter else src, dst_ref=land.at[_lin(peer)], send_sem=send_sems.at[n, k - 1],
                    recv_sem=recv_sems.at[n, k - 1], device_id=peer, device_id_type=MESH).wait_recv()
            for k in range(1, N_DEV):
                remote(n, k, src, land, scatter).wait_send()
            local(n, src, land, scatter).wait()


def _attn_fwd(q, k, v, block_q, w_out_b):
    _, seq, _ = q.shape
    n_keys = k.shape[1]
    n_q = seq // block_q

    def body(q_ref, k_ref, v_ref, wo_ref, o_ref, lse_ref, wog_ref, ssem, rsem, lsem):
        h, i = pl.program_id(0), pl.program_id(1)
        _hosted_exchange((h == 0) & (i == 0), (h == N_HEADS - 1) & (i == n_q - 1), [(wo_ref, wog_ref, False)],
                         ssem, rsem, lsem)
        kb, vb = k_ref[0], v_ref[0]
        for r0 in range(0, block_q, ATTN_ROWS):
            rows = slice(r0, r0 + ATTN_ROWS)
            s = _dot_nt(q_ref[0, rows, :], kb)
            m = jnp.max(s, axis=-1, keepdims=True)
            p = jnp.exp2(s - m)
            l = jnp.sum(p, axis=-1, keepdims=True)
            o_ref[rows, :] = _dot(p.astype(BF16), vb) * (1.0 / l)
            lse_ref[0, rows, :] = m + jnp.log2(l)

    hbm = pl.BlockSpec(memory_space=pl.ANY)
    return pl.pallas_call(
        body, name="attn_fwd", grid=(N_HEADS, n_q),
        out_shape=(jax.ShapeDtypeStruct((seq, D_ATTN), F32), jax.ShapeDtypeStruct((N_HEADS, seq, 1), F32),
                   jax.ShapeDtypeStruct((N_DEV,) + w_out_b.shape, w_out_b.dtype)),
        in_specs=[pl.BlockSpec((1, block_q, D_HEAD_PAD), lambda h, i: (h, i, 0)),
                  pl.BlockSpec((1, n_keys, D_HEAD_PAD), lambda h, i: (h, 0, 0)),
                  pl.BlockSpec((1, n_keys, D_V), lambda h, i: (h, 0, 0)), hbm],
        out_specs=(pl.BlockSpec((block_q, D_V), lambda h, i: (i, h)),
                   pl.BlockSpec((1, block_q, 1), lambda h, i: (h, i, 0)), hbm),
        scratch_shapes=[pltpu.SemaphoreType.DMA((1, 7)), pltpu.SemaphoreType.DMA((1, 7)), pltpu.SemaphoreType.DMA((1,))],
        compiler_params=pltpu.CompilerParams(dimension_semantics=("arbitrary", "arbitrary"), vmem_limit_bytes=VMEM_LIMIT),
    )(q, k, v, w_out_b)


def _attn_bwd(q, k, v, o, lse, d_o, block_q, gwo_blocks, gwp):
    _, seq, _ = q.shape
    n_keys = k.shape[1]
    n_q = seq // block_q

    def body(q_ref, k_ref, v_ref, o_ref, lse_ref, do_ref, gwo_ref, gwp_ref, dq_ref, dkt_ref, dvt_ref, lwo_ref, lwp_ref,
             ssem, rsem, lsem):
        h, i = pl.program_id(0), pl.program_id(1)
        _hosted_exchange((h == 0) & (i == 0), (h == N_HEADS - 1) & (i == n_q - 1),
                         [(gwo_ref, lwo_ref, True), (gwp_ref, lwp_ref, False)], ssem, rsem, lsem)

        @pl.when(i == 0)
        def _():
            dkt_ref[...] = jnp.zeros_like(dkt_ref)
            dvt_ref[...] = jnp.zeros_like(dvt_ref)

        kb, vb = k_ref[0], v_ref[0]
        raws, ps = [], []
        for r0 in range(0, block_q, ATTN_ROWS):
            rows = slice(r0, r0 + ATTN_ROWS)
            d_out = do_ref[rows, :]
            p = jnp.exp2(_dot_nt(q_ref[0, rows, :], kb) - lse_ref[0, rows, :])
            dp = _dot_nt(d_out.astype(BF16), vb)
            delta = jnp.sum(d_out * o_ref[rows, :], axis=-1, keepdims=True)
            raw = (p * (dp - delta)).astype(BF16)
            dq_ref[0, rows, :] = _dot(raw, kb)
            raws.append(raw)
            ps.append(p.astype(BF16))
        dkt_ref[0] += _dot_tn(q_ref[0], jnp.concatenate(raws, axis=0))
        dvt_ref[0] += _dot_tn(do_ref[...].astype(BF16), jnp.concatenate(ps, axis=0))

    hbm = pl.BlockSpec(memory_space=pl.ANY)
    return pl.pallas_call(
        body, name="attn_bwd", grid=(N_HEADS, n_q),
        out_shape=(jax.ShapeDtypeStruct((N_HEADS, seq, D_HEAD_PAD), F32),
                   jax.ShapeDtypeStruct((N_HEADS, D_HEAD_PAD, n_keys), F32),
                   jax.ShapeDtypeStruct((N_HEADS, D_V, n_keys), F32),
                   jax.ShapeDtypeStruct(gwo_blocks.shape, gwo_blocks.dtype),
                   jax.ShapeDtypeStruct((N_DEV,) + gwp.shape, gwp.dtype)),
        in_specs=[pl.BlockSpec((1, block_q, D_HEAD_PAD), lambda h, i: (h, i, 0)),
                  pl.BlockSpec((1, n_keys, D_HEAD_PAD), lambda h, i: (h, 0, 0)),
                  pl.BlockSpec((1, n_keys, D_V), lambda h, i: (h, 0, 0)),
                  pl.BlockSpec((block_q, D_V), lambda h, i: (i, h)),
                  pl.BlockSpec((1, block_q, 1), lambda h, i: (h, i, 0)),
                  pl.BlockSpec((block_q, D_V), lambda h, i: (i, h)), hbm, hbm],
        out_specs=(pl.BlockSpec((1, block_q, D_HEAD_PAD), lambda h, i: (h, i, 0)),
                   pl.BlockSpec((1, D_HEAD_PAD, n_keys), lambda h, i: (h, 0, 0)),
                   pl.BlockSpec((1, D_V, n_keys), lambda h, i: (h, 0, 0)), hbm, hbm),
        scratch_shapes=[pltpu.SemaphoreType.DMA((2, 7)), pltpu.SemaphoreType.DMA((2, 7)), pltpu.SemaphoreType.DMA((2,))],
        compiler_params=pltpu.CompilerParams(dimension_semantics=("arbitrary", "arbitrary"), vmem_limit_bytes=VMEM_LIMIT),
    )(q, k, v, o, lse, d_o, gwo_blocks, gwp)


def _mix(x, target, attn, u, modrows, bmodrows, w_pool, pool_scale, w_out):
    seq = x.shape[0]
    rows = min(MIX_TILE, seq // 2)
    n_tiles = seq // rows
    rows8 = rows // HALO
    last8 = seq // HALO - 1

    n_blk = seq // Q_BLOCK
    per = rows // n_blk
    assert rows % n_blk == 0 and per % 8 == 0 and n_tiles >= 2
    n_stash = 8

    def body(x_ref, t_ref, o_hbm, ga_ref, pin_ref, hb_ref, ha_ref, gp_ref, mod_ref, bmod_ref, wp_ref, ps_ref, wo_ref,
             loss_ref, g1_ref, dgate_ref, do_hbm, dga_ref, dgp_ref, dpl_ref, gwob_ref, gwp_ref, dps_ref,
             abuf, dbuf, gwo_ref, *rest):
        stash_even, stash_odd = rest[0:n_stash], rest[n_stash:2 * n_stash]
        rsem, wsem = rest[2 * n_stash:]
        s = pl.program_id(0)

        def slab_copies(tile, slot, fetch):
            window = pl.ds(tile * per, per)
            if fetch:
                return [pltpu.make_async_copy(o_hbm.at[:, window, :], abuf.at[slot], rsem.at[slot])]
            return [pltpu.make_async_copy(dbuf.at[slot], do_hbm.at[:, window, :], wsem.at[slot])]

        def wait_all(slot, fetch):
            slab_copies(0, slot, fetch)[0].wait()

        a_slot = lax.rem(s, 2)
        b_slot = 1 - a_slot

        @pl.when(s == 0)
        def _():
            loss_ref[...] = jnp.zeros_like(loss_ref)
            dgate_ref[...] = jnp.zeros_like(dgate_ref)
            gwo_ref[...] = jnp.zeros_like(gwo_ref)
            gwp_ref[...] = jnp.zeros_like(gwp_ref)
            dps_ref[...] = jnp.zeros_like(dps_ref)
            for cp in slab_copies(0, 0, True):
                cp.start()

        @pl.when(s + 1 < n_tiles)
        def _():
            for cp in slab_copies(s + 1, b_slot, True):
                cp.start()

        @pl.when(s < n_tiles)
        def _():
            wait_all(a_slot, True)

        @pl.when(s >= 3)
        def _():
            wait_all(b_slot, False)

        gate = mod_ref[2:3, :] + bmod_ref[2:3, :]
        ps = ps_ref[...]

        def a_vector(slot):
            attn_t = jnp.swapaxes(abuf[slot], 0, 1).reshape(rows, D_ATTN)
            ga = ga_ref[...].astype(F32)
            sga = _sigmoid(ga)
            silu_ga = ga * sga
            pin = pin_ref[...].astype(F32)
            xs = jnp.concatenate([jnp.where(s > 0, hb_ref[...].astype(F32), 0.0), pin,
                                  jnp.where(s < n_tiles - 1, ha_ref[...].astype(F32), 0.0)], axis=0)
            tpos = s * rows + lax.broadcasted_iota(jnp.int32, (rows, 1), 0)
            pooled = []
            for g, w in enumerate(POOL_WINDOWS):
                sl = slice(g * GROUP_DIM, (g + 1) * GROUP_DIM)
                ws = _window_sum(xs[:, sl], w, False)[HALO:HALO + rows]
                pooled.append((ws * _inv_count(tpos, w, seq) - pin[:, sl]).astype(BF16))
            return attn_t, ga, sga, silu_ga, pooled

        def a_group_maps(pooled):
            return jnp.concatenate([_dot(pooled[g], wp_ref[g].astype(BF16)) for g in range(len(POOL_WINDOWS))], axis=1)

        def a_project(stash, yp, attn_t, ga, sga, silu_ga, pooled):
            zb_ref, _, fa_ref, sa_ref, fp_ref, sp_ref, yp_ref, pooled_ref = stash
            ypool = yp * ps
            gp = gp_ref[...].astype(F32)
            sgp = _sigmoid(gp)
            silu_gp = gp * sgp
            z = jnp.concatenate([silu_ga * attn_t, silu_gp * ypool], axis=1).astype(BF16)
            y = _dot(z, wo_ref[...])
            zb_ref[...] = z
            fa_ref[...] = attn_t * (sga * (1.0 + ga * (1.0 - sga)))
            sa_ref[...] = silu_ga
            fp_ref[...] = ypool * (sgp * (1.0 + gp * (1.0 - sgp)))
            sp_ref[...] = silu_gp
            yp_ref[...] = yp
            pooled_ref[...] = jnp.concatenate(pooled, axis=1)
            return y

        def a_loss(stash, y):
            res = x_ref[...] + gate * y - t_ref[...]
            loss_ref[...] += jnp.sum(res * res) * (0.5 / D_MODEL)
            dxn = res * (1.0 / D_MODEL)
            g1_ref[...] = dxn
            dgate_ref[...] += jnp.sum(dxn * y, axis=0, keepdims=True)
            stash[1][...] = (gate * dxn).astype(BF16)

        def b_dz(stash):
            return _dot_nt(stash[1][...], wo_ref[...])

        def b_gwo(stash):
            gwo_ref[...] += _dot_tn(stash[0][...], stash[1][...])

        def b_vector(stash, slot, dz):
            _, _, fa_ref, sa_ref, fp_ref, sp_ref, yp_ref, _ = stash
            dbra = dz[:, 0:D_ATTN]
            dbrp = dz[:, D_ATTN:]
            dga_ref[...] = (dbra * fa_ref[...]).astype(BF16)
            dbuf[slot] = jnp.swapaxes((dbra * sa_ref[...]).reshape(per, n_blk, D_ATTN), 0, 1)
            dgp_ref[...] = (dbrp * fp_ref[...]).astype(BF16)
            dyp_s = dbrp * sp_ref[...]
            dps_ref[...] += jnp.sum(dyp_s * yp_ref[...], axis=0, keepdims=True)
            return (dyp_s * ps).astype(BF16)

        def b_small(stash, dyp):
            pooled_ref = stash[7]
            for g in range(len(POOL_WINDOWS)):
                sl = slice(g * GROUP_DIM, (g + 1) * GROUP_DIM)
                gwp_ref[g] += _dot_tn(pooled_ref[:, sl], dyp[:, sl])
                dpl_ref[:, sl] = _dot_nt(dyp[:, sl], wp_ref[g].astype(BF16)).astype(BF16)

        def both(a_stash, b_stash, slot_a):
            dz = b_dz(b_stash)
            front = a_vector(slot_a)
            yp = a_group_maps(front[-1])
            b_gwo(b_stash)
            y = a_project(a_stash, yp, *front)
            dyp = b_vector(b_stash, 1 - slot_a, dz)
            a_loss(a_stash, y)
            b_small(b_stash, dyp)

        @pl.when(s == 0)
        def _():
            front = a_vector(0)
            a_loss(stash_even, a_project(stash_even, a_group_maps(front[-1]), *front))

        @pl.when((s > 0) & (s < n_tiles) & (a_slot == 0))
        def _():
            both(stash_even, stash_odd, 0)

        @pl.when((s > 0) & (s < n_tiles) & (a_slot == 1))
        def _():
            both(stash_odd, stash_even, 1)

        @pl.when(s == n_tiles)
        def _():
            last = (n_tiles - 1) % 2
            stash = stash_odd if last else stash_even
            dz = b_dz(stash)
            b_gwo(stash)
            b_small(stash, b_vector(stash, last, dz))
            gwob_ref[...] = gwo_ref[...].astype(BF16)

        @pl.when(s >= 1)
        def _():
            for cp in slab_copies(s - 1, b_slot, False):
                cp.start()

        @pl.when(s == n_tiles)
        def _():
            wait_all(b_slot, False)
            wait_all(a_slot, False)

    ta = lambda s: jnp.minimum(s, n_tiles - 1)
    tb = lambda s: jnp.maximum(s - 1, 0)
    tile = lambda w: pl.BlockSpec((rows, w), lambda s: (ta(s), 0))
    tile_b = lambda w: pl.BlockSpec((rows, w), lambda s: (tb(s), 0))
    ucol = lambda col: pl.BlockSpec((rows, 512), lambda s: (ta(s), col))
    full = lambda shape: pl.BlockSpec(shape, lambda s: (0,) * len(shape))
    stash_shapes = [pltpu.VMEM((rows, D_MODEL), BF16), pltpu.VMEM((rows, D_MODEL), BF16)] + \
        [pltpu.VMEM((rows, 512), F32)] * 5 + [pltpu.VMEM((rows, D_POOL), BF16)]
    return pl.pallas_call(
        body, name="mix_fwd_bwd", grid=(n_tiles + 1,),
        out_shape=(jax.ShapeDtypeStruct((8, 128), F32), jax.ShapeDtypeStruct((seq, D_MODEL), F32),
                   jax.ShapeDtypeStruct((1, D_MODEL), F32), jax.ShapeDtypeStruct((n_blk, Q_BLOCK, D_ATTN), F32),
                   jax.ShapeDtypeStruct((seq, D_ATTN), BF16), jax.ShapeDtypeStruct((seq, D_POOL), BF16),
                   jax.ShapeDtypeStruct((seq, D_POOL), BF16), jax.ShapeDtypeStruct((D_MODEL, D_MODEL), BF16),
                   jax.ShapeDtypeStruct((4, GROUP_DIM, GROUP_DIM), F32), jax.ShapeDtypeStruct((1, D_POOL), F32)),
        in_specs=[tile(D_MODEL), tile(D_MODEL), pl.BlockSpec(memory_space=pl.ANY), ucol(0), ucol(1),
                  pl.BlockSpec((HALO, 512), lambda s: (jnp.maximum(ta(s) * rows8 - 1, 0), 1)),
                  pl.BlockSpec((HALO, 512), lambda s: (jnp.minimum((ta(s) + 1) * rows8, last8), 1)),
                  ucol(2), full((8, D_MODEL)), full((8, D_MODEL)), full((4, GROUP_DIM, GROUP_DIM)), full((1, D_POOL)),
                  full((D_MODEL, D_MODEL))],
        out_specs=(full((8, 128)), tile(D_MODEL), full((1, D_MODEL)), pl.BlockSpec(memory_space=pl.ANY), tile_b(D_ATTN),
                   tile_b(D_POOL), tile_b(D_POOL), full((D_MODEL, D_MODEL)), full((4, GROUP_DIM, GROUP_DIM)),
                   full((1, D_POOL))),
        scratch_shapes=[pltpu.VMEM((2, n_blk, per, D_ATTN), F32), pltpu.VMEM((2, n_blk, per, D_ATTN), F32),
                        pltpu.VMEM((D_MODEL, D_MODEL), F32), *stash_shapes, *stash_shapes,
                        pltpu.SemaphoreType.DMA((2,)), pltpu.SemaphoreType.DMA((2,))],
        compiler_params=pltpu.CompilerParams(dimension_semantics=("arbitrary",), vmem_limit_bytes=VMEM_LIMIT),
    )(x, target, attn.reshape(n_blk, Q_BLOCK, D_ATTN), u, u, u, u, u, modrows, bmodrows, w_pool, pool_scale, w_out)


def _inproj_bwd(x, ctx, modrows, bmodrows, norm_g, w_in_p, q_lora_g, w_uq_p, kv_lora_g, w_ukv, qng_p, kng_p, cos, slo, shi,
                u, dq, dk, dv, dga, dgp, dpl, g1):
    seq = x.shape[0]
    n_tiles = seq // TILE + 1
    rows8 = TILE // HALO
    last8 = seq // HALO - 1

    def body(*refs):
        du_even, du_odd, dh_even, dh_odd = refs[-4:]
        gwin_ref, gwuq_ref, gwukv_ref, dqlg_ref, dkvlg_ref, dqng_ref, dkng_ref, dng_ref, dmod_ref = refs[-13:-4]
        s = pl.program_id(0)

        @pl.when(s == 0)
        def _():
            for ref in (gwin_ref, gwuq_ref, gwukv_ref, dqlg_ref, dkvlg_ref, dqng_ref, dkng_ref, dng_ref, dmod_ref,
                        du_odd, dh_even):
                ref[...] = jnp.zeros_like(ref)

        @pl.when(lax.rem(s, 2) == 0)
        def _():
            stages(s, refs[:-4], du_even, du_odd, dh_odd, dh_even)

        @pl.when(lax.rem(s, 2) == 1)
        def _():
            stages(s, refs[:-4], du_odd, du_even, dh_even, dh_odd)

    def stages(s, refs, du_ref, du_prev, dh_fill, dh_prev):
        (xb_ref, xc_ref, ctx_ref, mod_ref, bmod_ref, ng_ref, win_ref, qlg_ref, wuq_ref, kvlg_ref, wukv_ref, qng_ref, kng_ref,
         cos_ref, slo_ref, shi_ref, cq_ref, ckv_ref, kr_ref, dq_ref, dk_ref, dv_ref, dga_ref, dgp_ref, dpl_ref,
         hb_ref, ha_ref, g1_ref,
         gx_ref, gwin_ref, gwuq_ref, gwukv_ref, dqlg_ref, dkvlg_ref, dqng_ref, dkng_ref, dng_ref, dmod_ref) = refs


        i = jnp.minimum(s, n_tiles - 1)
        valid = s < n_tiles
        is_lat = (s > 0) & valid
        cq = cq_ref[...]
        rq = lax.rsqrt(jnp.mean(cq * cq, axis=-1, keepdims=True) + NORM_EPS)
        qhat = cq * rq
        qnb = (qhat * qlg_ref[...]).astype(BF16)
        q = _dot_nt(qnb, wuq_ref[...])
        ckv = ckv_ref[...]
        rkv = lax.rsqrt(jnp.mean(ckv * ckv, axis=-1, keepdims=True) + NORM_EPS)
        kvhat = ckv * rkv
        kvnb = (kvhat * kvlg_ref[...]).astype(BF16)
        kv = _dot(kvnb, wukv_ref[...])

        _, _, _, h2, _ = _modulated_input(s <= 1, xb_ref, ctx_ref, mod_ref, bmod_ref, ng_ref)
        dub = du_prev[...]
        du_g, du_m = dub[:, 0:N_GATES], dub[:, N_GATES:U_PAD]
        h2b = h2.astype(BF16)
        dh_fill[...] = _dot(du_g, win_ref[N_MLA:D_IN_PROJ, :]) + _dot(du_m, win_ref[0:U_PAD - N_GATES, :])
        gwin_ref[N_MLA:D_IN_PROJ, :] += _dot_tn(du_g, h2b)
        gwin_ref[0:N_MLA, :] += _dot_tn(du_m, h2b)[0:N_MLA]

        ctx3 = s <= 2
        xt, r, xg, _, scale = _modulated_input(ctx3, xc_ref, ctx_ref, mod_ref, bmod_ref, ng_ref)
        dh = dh_prev[...]
        dshift = jnp.sum(dh, axis=0, keepdims=True)
        dscale = jnp.sum(dh * xg, axis=0, keepdims=True)
        zero = jnp.zeros_like(dshift)
        dmod_ref[0:1, :] += jnp.where(ctx3, zero, dshift)
        dmod_ref[1:2, :] += jnp.where(ctx3, zero, dscale)
        dmod_ref[2:3, :] += jnp.where(ctx3, dshift, zero)
        dmod_ref[3:4, :] += jnp.where(ctx3, dscale, zero)
        dxg = dh * (1.0 + scale)
        xr = xt * r
        dng_ref[...] += jnp.sum(dxg * xr, axis=0, keepdims=True)
        dxn = dxg * ng_ref[...]
        gx_ref[...] = g1_ref[...] + r * (dxn - xr * jnp.mean(dxn * xr, axis=-1, keepdims=True))

        cos_t, slo_t, shi_t = cos_ref[...], slo_ref[...], shi_ref[...]
        qg = qng_ref[...]
        dq_heads = []
        dqng = jnp.zeros((1, D_HEAD_PAD), F32)
        for hd in range(N_HEADS):
            qh = q[:, hd * D_HEAD_PAD:(hd + 1) * D_HEAD_PAD]
            rr = lax.rsqrt(jnp.sum(qh * qh, axis=-1, keepdims=True) * (1.0 / D_HEAD) + NORM_EPS)
            yq = qh * rr
            dpost = jnp.where(is_lat, dq_ref[hd] * ATTN_SCALE, 0.0)
            dyn = jnp.concatenate([dpost[:, 0:D_NOPE], _rope_t(dpost[:, D_NOPE:], cos_t, slo_t, shi_t)], axis=1)
            dqng = dqng + jnp.sum(dyn * yq, axis=0, keepdims=True)
            dyg = dyn * qg
            dq_heads.append(rr * (dyg - yq * (jnp.sum(dyg * yq, axis=-1, keepdims=True) * (1.0 / D_HEAD))))
        dqng_ref[...] += dqng
        dqf = jnp.concatenate(dq_heads, axis=1).astype(BF16)

        krz = kr_ref[...]
        kr_ss = jnp.sum(krz * krz, axis=-1, keepdims=True)
        kg = kng_ref[...]
        kg_n, kg_r = kg[:, 0:D_NOPE], kg[:, D_NOPE:]
        dkv_parts = []
        dkr = jnp.zeros((TILE, 128), F32)
        dkng_n = jnp.zeros((1, D_NOPE), F32)
        dkng_r = jnp.zeros((1, 128), F32)
        for hd in range(N_HEADS):
            kn = kv[:, hd * 256:hd * 256 + D_NOPE]
            rr = lax.rsqrt((jnp.sum(kn * kn, axis=-1, keepdims=True) + kr_ss) * (1.0 / D_HEAD) + NORM_EPS)
            yn, yr = kn * rr, krz * rr
            dk_h = jnp.where(valid, jnp.transpose(dk_ref[hd]) * LN_2, 0.0)
            dpn = dk_h[:, 0:D_NOPE]
            dpr = _rope_t(dk_h[:, D_NOPE:], cos_t, slo_t, shi_t)
            dkng_n = dkng_n + jnp.sum(dpn * yn, axis=0, keepdims=True)
            dkng_r = dkng_r + jnp.sum(dpr * yr, axis=0, keepdims=True)
            dyn, dyr = dpn * kg_n, dpr * kg_r
            proj = (jnp.sum(dyn * yn, axis=-1, keepdims=True) + jnp.sum(dyr * yr, axis=-1, keepdims=True)) * (1.0 / D_HEAD)
            dkv_parts.append(rr * (dyn - yn * proj))
            dkv_parts.append(jnp.where(valid, jnp.transpose(dv_ref[hd]), 0.0))
            dkr = dkr + rr * (dyr - yr * proj)
        dkng_ref[:, 0:D_NOPE] += dkng_n
        dkng_ref[:, D_NOPE:] += dkng_r
        dkvf = jnp.concatenate(dkv_parts, axis=1).astype(BF16)

        lat_t = jnp.maximum(i - 1, 0)
        dpl_t = dpl_ref[...].astype(F32)
        ds = jnp.concatenate([jnp.where(i > 1, hb_ref[...].astype(F32), 0.0), dpl_t,
                              jnp.where(i < n_tiles - 1, ha_ref[...].astype(F32), 0.0)], axis=0)
        tpos = lat_t * TILE - HALO + lax.broadcasted_iota(jnp.int32, (TILE + 2 * HALO, 1), 0)
        for g, w in enumerate(POOL_WINDOWS):
            sl = slice(g * GROUP_DIM, (g + 1) * GROUP_DIM)
            wsum = _window_sum(ds[:, sl] * _inv_count(tpos, w, seq), w, True)[HALO:HALO + TILE]
            du_ref[:, O_PIN + g * GROUP_DIM:O_PIN + (g + 1) * GROUP_DIM] = jnp.where(
                is_lat, wsum - dpl_t[:, sl], 0.0).astype(BF16)

        du_ref[:, O_GA:O_GA + D_ATTN] = jnp.where(is_lat, dga_ref[...], jnp.zeros((), BF16))
        du_ref[:, O_GP:O_GP + D_POOL] = jnp.where(is_lat, dgp_ref[...], jnp.zeros((), BF16))
        du_ref[:, O_KR:U_PAD] = dkr.astype(BF16)

        gwuq_ref[...] += _dot_tn(dqf, qnb)
        dqn = _dot(dqf, wuq_ref[...])
        gwukv_ref[...] += _dot_tn(dkvf, kvnb)
        dkvn = _dot_nt(dkvf, wukv_ref[...])
        dqlg_ref[...] += jnp.sum(dqn * qhat, axis=0, keepdims=True)
        dqhat = dqn * qlg_ref[...]
        dcq = rq * (dqhat - qhat * jnp.mean(dqhat * qhat, axis=-1, keepdims=True))
        dkvlg_ref[...] += jnp.sum(dkvn * kvhat, axis=0, keepdims=True)
        dkvhat = dkvn * kvlg_ref[...]
        dckv = rkv * (dkvhat - kvhat * jnp.mean(dkvhat * kvhat, axis=-1, keepdims=True))
        du_ref[:, O_CQ:O_CQ + R_Q] = dcq.astype(BF16)
        du_ref[:, O_CKV:O_CKV + R_KV] = dckv.astype(BF16)

    t1 = lambda s: jnp.minimum(s, n_tiles - 1)
    t2 = lambda s: jnp.clip(s - 1, 0, n_tiles - 1)
    t3 = lambda s: jnp.clip(s - 2, 0, n_tiles - 1)
    latent = lambda t: jnp.maximum(t - 1, 0)
    lat = lambda s: (latent(t1(s)), 0)
    lat3 = lambda s: (latent(t3(s)), 0)
    full = lambda shape: pl.BlockSpec(shape, lambda s: (0,) * len(shape))
    rope_spec = pl.BlockSpec((TILE, 128), lambda s: (t1(s), 0))
    return pl.pallas_call(
        body, name="inproj_bwd", grid=(n_tiles + 2,),
        out_shape=(jax.ShapeDtypeStruct((seq, D_MODEL), F32), jax.ShapeDtypeStruct((D_IN_PROJ, D_MODEL), F32),
                   jax.ShapeDtypeStruct((N_HEADS * D_HEAD_PAD, R_Q), F32), jax.ShapeDtypeStruct((N_HEADS * 256, R_KV), F32),
                   jax.ShapeDtypeStruct((1, R_Q), F32), jax.ShapeDtypeStruct((1, R_KV), F32),
                   jax.ShapeDtypeStruct((1, D_HEAD_PAD), F32), jax.ShapeDtypeStruct((1, D_HEAD_PAD), F32),
                   jax.ShapeDtypeStruct((1, D_MODEL), F32), jax.ShapeDtypeStruct((8, D_MODEL), F32)),
        in_specs=[pl.BlockSpec((TILE, D_MODEL), lambda s: (latent(t2(s)), 0)), pl.BlockSpec((TILE, D_MODEL), lat3),
                  full((TILE, D_MODEL)), full((8, D_MODEL)), full((8, D_MODEL)),
                  full((1, D_MODEL)), full((D_IN_PROJ, D_MODEL)), full((1, R_Q)), full((N_HEADS * D_HEAD_PAD, R_Q)),
                  full((1, R_KV)), full((R_KV, N_HEADS * 256)), full((1, D_HEAD_PAD)), full((1, D_HEAD_PAD)),
                  rope_spec, rope_spec, rope_spec,
                  pl.BlockSpec((TILE, R_Q), lambda s: (t1(s), (O_CQ - N_GATES) // R_Q)),
                  pl.BlockSpec((TILE, R_KV), lambda s: (t1(s), (O_CKV - N_GATES) // R_KV)),
                  pl.BlockSpec((TILE, 128), lambda s: (t1(s), (O_KR - N_GATES) // 128)),
                  pl.BlockSpec((N_HEADS, TILE, D_HEAD_PAD), lambda s: (0, latent(t1(s)), 0)),
                  pl.BlockSpec((N_HEADS, D_HEAD_PAD, TILE), lambda s: (0, 0, t1(s))),
                  pl.BlockSpec((N_HEADS, D_V, TILE), lambda s: (0, 0, t1(s))),
                  pl.BlockSpec((TILE, D_ATTN), lat), pl.BlockSpec((TILE, D_POOL), lat), pl.BlockSpec((TILE, D_POOL), lat),
                  pl.BlockSpec((HALO, D_POOL), lambda s: (jnp.maximum((t1(s) - 1) * rows8 - 1, 0), 0)),
                  pl.BlockSpec((HALO, D_POOL), lambda s: (jnp.minimum(jnp.maximum(t1(s), 1) * rows8, last8), 0)),
                  pl.BlockSpec((TILE, D_MODEL), lat3)],
        out_specs=(pl.BlockSpec((TILE, D_MODEL), lat3), full((D_IN_PROJ, D_MODEL)), full((N_HEADS * D_HEAD_PAD, R_Q)),
                   full((N_HEADS * 256, R_KV)), full((1, R_Q)), full((1, R_KV)), full((1, D_HEAD_PAD)),
                   full((1, D_HEAD_PAD)), full((1, D_MODEL)), full((8, D_MODEL))),
        scratch_shapes=[pltpu.VMEM((TILE, U_PAD), BF16), pltpu.VMEM((TILE, U_PAD), BF16),
                        pltpu.VMEM((TILE, D_MODEL), F32), pltpu.VMEM((TILE, D_MODEL), F32)],
        compiler_params=pltpu.CompilerParams(dimension_semantics=("arbitrary",), vmem_limit_bytes=VMEM_LIMIT),
    )(x, x, ctx, modrows, bmodrows, norm_g, w_in_p, q_lora_g, w_uq_p, kv_lora_g, w_ukv, qng_p, kng_p, cos, slo, shi,
      u, u, u, dq, dk, dv, dga, dgp, dpl, dpl, dpl, g1)


SMALL_LAYOUT = ((0, D_MODEL), (1, R_Q), (2, R_KV), (3, D_HEAD_PAD), (4, D_HEAD_PAD), (5, D_POOL))
ROW_DMOD_B, ROW_DMOD_C, ROW_LOSS = 6, 9, 12


def _epilogue(parts, landed, smalls, dmod4, dgate, loss_acc, w_mod_l):
    n_a, n_l, n_s = len(parts), len(landed), len(smalls)
    n_mod = w_mod_l.shape[1]
    blk = [p.shape[1:] for p in parts]
    small_out = [D_MODEL, R_Q, R_KV, D_HEAD, D_HEAD, D_POOL]

    def body(*refs):
        part_refs = refs[0:n_a]
        land_refs = refs[n_a:n_a + n_l]
        small_in = refs[n_a + n_l:n_a + n_l + n_s]
        dmod4_ref, dgate_ref, loss_ref, wmod_ref = refs[n_a + n_l + n_s:n_a + n_l + n_s + 4]
        outs = refs[n_a + n_l + n_s + 4:]
        red_refs = outs[0:n_a]
        landsum_refs = outs[n_a:n_a + n_l]
        ccg_ref = outs[n_a + n_l]
        sum_refs = outs[n_a + n_l + 1:n_a + n_l + 1 + n_s]
        gbmod_ref, dmy_ref, lossout_ref = outs[n_a + n_l + 1 + n_s:n_a + n_l + 4 + n_s]
        scr = outs[n_a + n_l + 4 + n_s:]
        recv1, own1, sum1b, recv2 = scr[0:n_a], scr[n_a:2 * n_a], scr[2 * n_a:3 * n_a], scr[3 * n_a:4 * n_a]
        small_ref, smallg_ref, tot_ref, cc_ref = scr[4 * n_a:4 * n_a + 4]
        land_vmem = scr[4 * n_a + 4:4 * n_a + 4 + n_l]
        ssem1, rsem1, lsem, ssem2, rsem2, ssem_s, rsem_s, ssem_cc, rsem_cc, land_sem = scr[4 * n_a + 4 + n_l:]
        x, y, c = _coords()
        me = (x, y, c)
        sibling = (x, y, 1 - c)

        small_ref[...] = jnp.zeros_like(small_ref)
        for ref, (row, width) in zip(small_in, SMALL_LAYOUT):
            small_ref[row:row + 1, 0:width] = ref[...]
        small_ref[ROW_DMOD_B:ROW_DMOD_B + 2, :] = dmod4_ref[0:2, :]
        small_ref[ROW_DMOD_B + 2:ROW_DMOD_B + 3, :] = dgate_ref[...]
        small_ref[ROW_DMOD_C:ROW_DMOD_C + 2, :] = dmod4_ref[2:4, :]
        small_ref[ROW_LOSS:ROW_LOSS + 1, 0:128] = loss_ref[0:1, :]
        smallg_ref[_lin(me)] = small_ref[...]
        s_recv, s_send = _gather_to_all(small_ref, smallg_ref, ssem_s, rsem_s)

        stage1, own_copies = [], []
        for a in range(n_a):
            for b in range(4):
                dev = 4 * (b >> 1) + 2 * (b & 1)
                stage1.append(pltpu.make_async_remote_copy(
                    src_ref=part_refs[a].at[dev + (1 - c)], dst_ref=recv1[a].at[b],
                    send_sem=ssem1.at[a, b], recv_sem=rsem1.at[a, b], device_id=sibling, device_id_type=MESH))
                own_copies.append(pltpu.make_async_copy(part_refs[a].at[dev + c], own1[a].at[b], lsem.at[a, b]))
                stage1[-1].start()
                own_copies[-1].start()
        land_copies = [pltpu.make_async_copy(land_refs[n], land_vmem[n], land_sem.at[n]) for n in range(n_l)]
        for cp in land_copies:
            cp.start()

        s_recv()
        tot = smallg_ref[0]
        for d in range(1, N_DEV):
            tot = tot + smallg_ref[d]
        tot_ref[...] = tot
        for ref, (row, _), width in zip(sum_refs, SMALL_LAYOUT, small_out):
            ref[...] = tot_ref[row:row + 1, 0:width]
        lossout_ref[...] = tot_ref[8:16, 0:128]
        lin = _lin(me)

        def my_columns(three_rows):
            v = jnp.concatenate(three_rows, axis=1)
            out = jnp.zeros((1, n_mod), F32)
            for d in range(N_DEV):
                out = out + jnp.where(lin == d, v[:, d * n_mod:(d + 1) * n_mod], 0.0)
            return out

        rows3 = lambda ref, r0: [ref[r0 + t:r0 + t + 1, :] for t in range(3)]
        dmodc = rows3(tot_ref, ROW_DMOD_C)
        gbmod_ref[...] = jnp.concatenate(rows3(tot_ref, ROW_DMOD_B), axis=1) + jnp.concatenate(dmodc, axis=1)
        dmy_ref[...] = jnp.zeros_like(dmy_ref)
        for b in range(N_DEV):
            dmy_ref[b:b + 1, :] = my_columns(rows3(smallg_ref.at[b], ROW_DMOD_B))
        dmy = my_columns(dmodc)
        dmy_ref[N_DEV:N_DEV + 1, :] = dmy
        part = lax.dot_general(jnp.broadcast_to(dmy, (8, n_mod)), wmod_ref[...], (((1,), (1,)), ((), ())),
                               precision=HIGHEST, preferred_element_type=F32)

        cc_ref[...] = part
        ccg_ref[_lin(me)] = part
        cc_recv, cc_send = _gather_to_all(cc_ref, ccg_ref, ssem_cc, rsem_cc)

        stage2 = []
        for a in range(n_a):
            for b in range(4):
                stage1[4 * a + b].wait_recv()
                own_copies[4 * a + b].wait()
                sum1b[a][b] = (own1[a][b] + recv1[a][b]).astype(BF16)
            for k in (1, 2, 3):
                tx, ty = _flip(x, (k >> 1) & 1), _flip(y, k & 1)
                stage2.append(pltpu.make_async_remote_copy(
                    src_ref=sum1b[a].at[2 * tx + ty], dst_ref=recv2[a].at[k - 1], send_sem=ssem2.at[a, k - 1],
                    recv_sem=rsem2.at[a, k - 1], device_id=(tx, ty, c), device_id_type=MESH))
                stage2[-1].start()
        for a in range(n_a):
            own = own1[a][2 * x + y] + recv1[a][2 * x + y]
            for k in (1, 2, 3):
                stage2[3 * a + k - 1].wait_recv()
                own = own + recv2[a][k - 1].astype(F32)
            red_refs[a][...] = own

        for cp in land_copies:
            cp.wait()
        for ref, out in zip(land_vmem, landsum_refs):
            acc = ref[0].astype(F32)
            for d in range(1, N_DEV):
                acc = acc + ref[d].astype(F32)
            out[...] = acc
        cc_recv()
        for cp in stage1 + stage2:
            cp.wait_send()
        s_send()
        cc_send()

    vm = pl.BlockSpec(memory_space=pltpu.VMEM)
    sds = jax.ShapeDtypeStruct
    return pl.pallas_call(
        body, name="epilogue_reduce",
        out_shape=(*[sds(s, F32) for s in blk], *[sds(a.shape[1:], F32) for a in landed], sds((N_DEV, 8, D_MODEL), F32),
                   *[sds((1, w), F32) for w in small_out], sds((1, 3 * D_MODEL), F32), sds((16, n_mod), F32),
                   sds((8, 128), F32)),
        in_specs=[pl.BlockSpec(memory_space=pl.ANY)] * (n_a + n_l) + [vm] * (n_s + 4),
        out_specs=tuple([vm] * (n_a + n_l + n_s + 4)),
        scratch_shapes=[*[pltpu.VMEM((4,) + s, F32) for s in blk], *[pltpu.VMEM((4,) + s, F32) for s in blk],
                        *[pltpu.VMEM((4,) + s, BF16) for s in blk], *[pltpu.VMEM((3,) + s, BF16) for s in blk],
                        pltpu.VMEM((SMALL_ROWS, D_MODEL), F32), pltpu.VMEM((N_DEV, SMALL_ROWS, D_MODEL), F32),
                        pltpu.VMEM((SMALL_ROWS, D_MODEL), F32), pltpu.VMEM((8, D_MODEL), F32),
                        *[pltpu.VMEM(a.shape, a.dtype) for a in landed],
                        pltpu.SemaphoreType.DMA((n_a, 4)), pltpu.SemaphoreType.DMA((n_a, 4)), pltpu.SemaphoreType.DMA((n_a, 4)),
                        pltpu.SemaphoreType.DMA((n_a, 3)), pltpu.SemaphoreType.DMA((n_a, 3)),
                        pltpu.SemaphoreType.DMA((7,)), pltpu.SemaphoreType.DMA((7,)),
                        pltpu.SemaphoreType.DMA((7,)), pltpu.SemaphoreType.DMA((7,)), pltpu.SemaphoreType.DMA((n_l,))],
        compiler_params=pltpu.CompilerParams(vmem_limit_bytes=VMEM_LIMIT),
    )(*parts, *landed, *smalls, dmod4, dgate, loss_acc, w_mod_l)


def _adamw(weights, grads, ms, vs, c_all_t, dmod_my, p2g):
    n = len(weights)

    def body(*refs):
        w_refs = refs[0:n]
        g_in = refs[n:2 * n - 2]
        m_refs = refs[2 * n - 2:3 * n - 2]
        v_refs = refs[3 * n - 2:4 * n - 2]
        cat_ref, dmod_ref, p2g_ref = refs[4 * n - 2:4 * n + 1]
        outs = refs[4 * n + 1:]
        g_refs, d_refs, nm_refs, nv_refs = outs[0:n], outs[n:2 * n], outs[2 * n:3 * n], outs[3 * n:4 * n]
        gcc_ref, gwm_ref = g_refs[0], g_refs[1]

        part = p2g_ref[0, 0:1, :]
        for d in range(1, N_DEV):
            part = part + p2g_ref[d, 0:1, :]
        cc = w_refs[0][...]
        sg = _sigmoid(cc)
        gcc_ref[...] = part * (sg * (1.0 + cc * (1.0 - sg)))
        cat = cat_ref[...]
        gwm_ref[...] = lax.dot_general(cat * _sigmoid(cat), dmod_ref[...], (((1,), (0,)), ((), ())), precision=HIGHEST,
                                       preferred_element_type=F32)
        for idx in range(n):
            if idx >= 2:
                g_refs[idx][...] = g_in[idx - 2][...]
            g = g_refs[idx][...]
            m = ADAM_B1 * m_refs[idx][...] + (1.0 - ADAM_B1) * g
            v = ADAM_B2 * v_refs[idx][...] + (1.0 - ADAM_B2) * (g * g)
            m_hat = m / (1.0 - ADAM_B1 ** ADAM_STEP)
            v_hat = v / (1.0 - ADAM_B2 ** ADAM_STEP)
            d_refs[idx][...] = -ADAM_LR * (m_hat / (jnp.sqrt(v_hat) + ADAM_EPS) + ADAM_WD * w_refs[idx][...])
            nm_refs[idx][...] = m
            nv_refs[idx][...] = v

    vm = pl.BlockSpec(memory_space=pltpu.VMEM)
    shapes = [jax.ShapeDtypeStruct(w.shape, F32) for w in weights]
    args = list(weights) + list(grads[2:]) + list(ms) + list(vs) + [c_all_t, dmod_my, p2g]
    out_shape = tuple(shapes * 4)
    return pl.pallas_call(
        body, name="adamw_update", out_shape=out_shape, in_specs=[vm] * len(args), out_specs=tuple([vm] * len(out_shape)),
        compiler_params=pltpu.CompilerParams(vmem_limit_bytes=VMEM_LIMIT),
    )(*args)


def _rope_tables(seq):
    rows = seq // GRID_W
    row = np.repeat(np.arange(rows, dtype=np.float32), GRID_W)
    col = np.tile(np.arange(GRID_W, dtype=np.float32), rows)
    n_freq = D_ROPE // 4
    inv = (np.float32(ROPE_BASE) ** (-np.arange(n_freq, dtype=np.float32) / np.float32(n_freq))).astype(np.float32)
    ang_r = (row[:, None] * inv).astype(np.float32)
    ang_c = (col[:, None] * inv).astype(np.float32)
    ang = np.concatenate([ang_r, ang_r, ang_c, ang_c], axis=-1)
    cos, sin = np.cos(ang).astype(np.float32), np.sin(ang).astype(np.float32)
    low = (np.arange(D_ROPE) % 32) < 16

    def table(t, fill):
        out = np.full((TILE + seq, 128), fill, np.float32)
        out[TILE:, 0:D_ROPE] = t
        return jnp.asarray(out)

    return table(cos, 1.0), table(np.where(low, -sin, 0.0), 0.0), table(np.where(low, 0.0, sin), 0.0)


def kernel(x, c, ctx, c_ctx, w_mod, b_mod, norm_g, w_in, q_lora_g, w_uq, kv_lora_g, w_ukv, q_norm_g, k_norm_g, w_pool, pool_scale, w_out, loss_target, m_c_ctx, m_w_mod, m_b_mod, m_norm_g, m_w_in, m_q_lora_g, m_w_uq, m_kv_lora_g, m_w_ukv, m_q_norm_g, m_k_norm_g, m_w_pool, m_pool_scale, m_w_out, v_c_ctx, v_w_mod, v_b_mod, v_norm_g, v_w_in, v_q_lora_g, v_w_uq, v_kv_lora_g, v_w_ukv, v_q_norm_g, v_k_norm_g, v_w_pool, v_pool_scale, v_w_out):
    seq = x.shape[1]
    assert ctx.shape[1] == TILE and seq % TILE == 0 and seq % GRID_W == 0
    ix, iy, ic = lax.axis_index("x"), lax.axis_index("y"), lax.axis_index("c")
    me = 4 * ix + 2 * iy + ic
    x2, ctx2, tgt2 = x[0], ctx[0], loss_target[0]
    w_mod_l, w_ukv_l, w_out_l = w_mod[0], w_ukv[0], w_out[0]
    c_ctx_row = c_ctx.reshape(1, D_MODEL)

    tr = lambda a: jnp.transpose(a[0])
    w_in_tl, w_uq_tl = tr(w_in), tr(w_uq)
    cg, modg, wg_in, wg_uq, wg_ukv = _prologue(
        c, c_ctx_row, w_mod_l,
        [w_in_tl, w_uq_tl, w_ukv_l])
    mod_all = jnp.transpose(modg, (1, 0, 2)).reshape(16, 3 * D_MODEL)
    mod_b = lax.dynamic_slice_in_dim(mod_all, me, 1, axis=0).reshape(3, D_MODEL)
    mod_c = mod_all[8].reshape(3, D_MODEL)
    modrows = jnp.concatenate([mod_b, mod_c[0:2], jnp.zeros((3, D_MODEL), F32)], axis=0)
    b3 = b_mod.reshape(3, D_MODEL)
    bmodrows = jnp.concatenate([b3, b3[0:2], jnp.zeros((3, D_MODEL), F32)], axis=0)

    w_in_p = wg_in.reshape(D_IN_PROJ, D_MODEL)
    w_uq_p = jnp.concatenate([wg_uq.reshape(N_HEADS, D_HEAD, R_Q), jnp.zeros((N_HEADS, D_HEAD_PAD - D_HEAD, R_Q), BF16)],
                             axis=1).reshape(N_HEADS * D_HEAD_PAD, R_Q)
    w_ukv_f = jnp.transpose(wg_ukv, (1, 0, 2)).reshape(R_KV, N_HEADS * 256)
    qng_p = jnp.concatenate([q_norm_g, jnp.zeros((1, D_HEAD_PAD - D_HEAD), F32)], axis=1)
    kng_p = jnp.concatenate([k_norm_g, jnp.zeros((1, D_HEAD_PAD - D_HEAD), F32)], axis=1)
    cos, slo, shi = _rope_tables(seq)

    u_gates, u_mla, q, k, v = _inproj_fwd(x2, ctx2, modrows, bmodrows, norm_g, w_in_p, q_lora_g, w_uq_p, kv_lora_g, w_ukv_f,
                             qng_p, kng_p, cos, slo, shi)
    attn, lse, wg_out = _attn_fwd(q, k, v, min(2048, seq), w_out_l.astype(BF16))
    (loss_acc, g1, dgate, d_attn, dga, dgp, dpl, gwo_b, gwp, dps) = _mix(
        x2, tgt2, attn, u_gates, modrows, bmodrows, w_pool[0], pool_scale, wg_out.reshape(D_MODEL, D_MODEL))

    blocks = lambda a: a.reshape((N_DEV, a.shape[0] // N_DEV) + a.shape[1:])
    dq, dk, dv, land_wo, land_wp = _attn_bwd(q, k, v, attn, lse, d_attn.reshape(seq, D_ATTN), min(1024, seq), blocks(gwo_b),
                                             gwp.reshape(4 * GROUP_DIM, GROUP_DIM))
    (grad_x, gwin_t, gwuq_p, gwukv_t, dqlg, dkvlg, dqng, dkng, dng, dmod4) = _inproj_bwd(
        x2, ctx2, modrows, bmodrows, norm_g, w_in_p, q_lora_g, w_uq_p, kv_lora_g, w_ukv_f, qng_p, kng_p, cos, slo, shi,
        u_mla, dq, dk, dv, dga, dgp, dpl, g1)

    gwuq_t = gwuq_p.reshape(N_HEADS, D_HEAD_PAD, R_Q)[:, 0:D_HEAD].reshape(N_HEADS * D_HEAD, R_Q)
    parts = [blocks(gwin_t), blocks(gwuq_t), blocks(gwukv_t)]
    (r_win, r_wuq, r_wukv, r_wout, g_w_pool, ccg, g_ng, g_qlg, g_kvlg, g_qng, g_kng, g_ps, g_bmod, dmod_my,
     loss_rows) = _epilogue(parts, [land_wo, land_wp], [dng, dqlg, dkvlg, dqng, dkng, dps], dmod4, dgate, loss_acc, w_mod_l)

    g_w_ukv = jnp.transpose(r_wukv)
    c_rows = cg[:, 0, :]
    c_all_t = jnp.transpose(jnp.concatenate([c_rows, c_ctx_row, jnp.zeros((16 - N_DEV - 1, D_MODEL), F32)], axis=0))

    weights = [c_ctx_row, w_mod_l, b_mod, norm_g, w_in_tl, q_lora_g, w_uq_tl, kv_lora_g, w_ukv_l, q_norm_g, k_norm_g,
               w_pool.reshape(4 * GROUP_DIM, GROUP_DIM), pool_scale, w_out_l]
    grads = [None, None, g_bmod, g_ng, r_win, g_qlg, r_wuq, g_kvlg, g_w_ukv, g_qng, g_kng, g_w_pool, g_ps, r_wout]
    ms = [m_c_ctx.reshape(1, D_MODEL), m_w_mod[0], m_b_mod, m_norm_g, tr(m_w_in), m_q_lora_g, tr(m_w_uq), m_kv_lora_g,
          m_w_ukv[0], m_q_norm_g, m_k_norm_g, m_w_pool.reshape(4 * GROUP_DIM, GROUP_DIM), m_pool_scale, m_w_out[0]]
    vs = [v_c_ctx.reshape(1, D_MODEL), v_w_mod[0], v_b_mod, v_norm_g, tr(v_w_in), v_q_lora_g, tr(v_w_uq), v_kv_lora_g,
          v_w_ukv[0], v_q_norm_g, v_k_norm_g, v_w_pool.reshape(4 * GROUP_DIM, GROUP_DIM), v_pool_scale, v_w_out[0]]
    outs = _adamw(weights, grads, ms, vs, c_all_t, dmod_my, ccg)
    n = len(weights)
    grads, deltas, new_m, new_v = outs[0:n], outs[n:2 * n], outs[2 * n:3 * n], outs[3 * n:4 * n]

    loss = loss_rows[ROW_LOSS - 8, 0]
    final = [c_ctx.shape, w_mod.shape, b_mod.shape, norm_g.shape, w_in.shape, q_lora_g.shape, w_uq.shape, kv_lora_g.shape,
             w_ukv.shape, q_norm_g.shape, k_norm_g.shape, w_pool.shape, pool_scale.shape, w_out.shape]
    transposed = (4, 6)

    def shaped(arrs):
        return [(jnp.transpose(a) if i in transposed else a).reshape(s) for i, (a, s) in enumerate(zip(arrs, final))]

    return (loss, grad_x[None], *shaped(grads), *shaped(deltas), *shaped(new_m), *shaped(new_v))
```

```python
import numpy as np

import jax
import jax.numpy as jnp
from jax import lax
from jax.experimental import pallas as pl
from jax.experimental.pallas import tpu as pltpu

F32 = jnp.float32
BF16 = jnp.bfloat16
MESH = pl.DeviceIdType.MESH
HIGHEST = lax.Precision.HIGHEST

D_MODEL = 1024
N_HEADS = 4
D_NOPE = 128
D_ROPE = 64
D_HEAD = D_NOPE + D_ROPE
D_HEAD_PAD = 256
D_V = 128
R_Q = 256
R_KV = 128
D_ATTN = 512
D_POOL = 512
POOL_WINDOWS = (2, 4, 8, 16)
GROUP_DIM = 128
GRID_W = 64
ROPE_BASE = 10000.0
NORM_EPS = 1e-6
ATTN_SCALE = D_HEAD ** -0.5
LOG2_E = 1.4426950408889634
LN_2 = 0.6931471805599453
ATTN_ROWS = 256
D_IN_PROJ = 1984
U_PAD = 2048
O_GA, O_PIN, O_GP, O_CQ, O_CKV, O_KR = 0, 512, 1024, 1536, 1792, 1920
TILE = 256
MIX_TILE = 512
Q_BLOCK = 128
HALO = 16
N_GATES = 1536
N_MLA = D_IN_PROJ - N_GATES
N_DEV = 8
VMEM_LIMIT = 56 * 1024 * 1024

ADAM_LR = 0.001
ADAM_B1 = 0.9
ADAM_B2 = 0.999
ADAM_EPS = 1e-08
ADAM_WD = 0.01
ADAM_STEP = 10

SMALL_ROWS = 16


def _dot(a, b):
    return lax.dot_general(a, b, (((1,), (0,)), ((), ())), preferred_element_type=F32)


def _dot_nt(a, b):
    return lax.dot_general(a, b, (((1,), (1,)), ((), ())), preferred_element_type=F32)


def _dot_tn(a, b):
    return lax.dot_general(a, b, (((0,), (0,)), ((), ())), preferred_element_type=F32)


def _sigmoid(x):
    return 1.0 / (1.0 + jnp.exp(-x))


def _rope(p, cos, slo, shi):
    return p * cos + pltpu.roll(p, 112, 1) * slo + pltpu.roll(p, 16, 1) * shi


def _rope_t(d, cos, slo, shi):
    return d * cos + pltpu.roll(d * slo, 16, 1) + pltpu.roll(d * shi, 112, 1)


def _shift_rows(a, k):
    n = a.shape[0]
    return pltpu.roll(a, (-k) % n, 0)


def _window_sum(x, w, transposed):
    s = (x + _shift_rows(x, 1)) if transposed else (_shift_rows(x, -1) + x)
    step = 1
    while 2 * step < w:
        s = _shift_rows(s, -step) + _shift_rows(s, step)
        step *= 2
    return s


def _inv_count(tpos, w, seq):
    lo = jnp.maximum(tpos - w // 2, 0)
    hi = jnp.minimum(tpos - w // 2 + w, seq)
    return 1.0 / jnp.maximum(hi - lo, 1).astype(F32)


def _coords():
    return lax.axis_index("x"), lax.axis_index("y"), lax.axis_index("c")


def _flip(v, bit):
    return (1 - v) if bit else v


def _peer(k):
    x, y, c = _coords()
    return (_flip(x, (k >> 2) & 1), _flip(y, (k >> 1) & 1), _flip(c, k & 1))


def _lin(p):
    return 4 * p[0] + 2 * p[1] + p[2]


def _gather_to_all(src_ref, slots_ref, send_sems, recv_sems):
    me = _coords()
    copies = []
    for k in range(1, N_DEV):
        cp = pltpu.make_async_remote_copy(
            src_ref=src_ref, dst_ref=slots_ref.at[_lin(me)], send_sem=send_sems.at[k - 1], recv_sem=recv_sems.at[k - 1],
            device_id=_peer(k), device_id_type=MESH)
        cp.start()
        copies.append(cp)

    def wait_recv():
        for k in range(1, N_DEV):
            pltpu.make_async_remote_copy(
                src_ref=src_ref, dst_ref=slots_ref.at[_lin(_peer(k))], send_sem=send_sems.at[k - 1],
                recv_sem=recv_sems.at[k - 1], device_id=_peer(k), device_id_type=MESH).wait_recv()

    def wait_send():
        for cp in copies:
            cp.wait_send()

    return wait_recv, wait_send


def _prologue(c8, cctx8, w_mod_l, shards):
    n_mod = w_mod_l.shape[1]
    n_w = len(shards)

    def body(c_ref, cctx_ref, wmod_ref, *refs):
        w_refs = refs[0:n_w]
        cg_ref, modg_ref = refs[n_w], refs[n_w + 1]
        wg_refs = refs[n_w + 2:2 * n_w + 2]
        modblk_ref = refs[2 * n_w + 2]
        wb_refs = refs[2 * n_w + 3:3 * n_w + 3]
        c8_ref = refs[3 * n_w + 3]
        ssem_w, rsem_w, lsem_w, ssem_c, rsem_c, ssem_m, rsem_m = refs[3 * n_w + 4:]
        x, y, c = _coords()
        me = (x, y, c)
        sibling = (x, y, 1 - c)
        chips = [(1 - x, y), (x, 1 - y), (1 - x, 1 - y)]

        def wcopies(k, block, to, own=False):
            out = []
            for a in range(n_w):
                slot = wg_refs[a].at[_lin(block)]
                out.append(pltpu.make_async_remote_copy(
                    src_ref=wb_refs[a] if own else slot, dst_ref=slot, send_sem=ssem_w.at[a, k], recv_sem=rsem_w.at[a, k],
                    device_id=to, device_id_type=MESH))
            return out

        c8_ref[...] = jnp.broadcast_to(c_ref[...], (8, D_MODEL))
        cg_ref[_lin(me)] = c8_ref[...]
        c_recv, c_send = _gather_to_all(c8_ref, cg_ref, ssem_c, rsem_c)

        for a in range(n_w):
            wb_refs[a][...] = w_refs[a][...].astype(BF16)
        first = wcopies(0, me, sibling, own=True)
        for j, chip in enumerate(chips):
            first += wcopies(1 + j, me, (*chip, c), own=True)
        late = [idx for idx in range(len(first)) if idx % n_w == 0 and idx >= n_w]
        for idx, cp in enumerate(first):
            if idx not in late:
                cp.start()
        mine = [pltpu.make_async_copy(wb_refs[a], wg_refs[a].at[_lin(me)], lsem_w.at[a]) for a in range(n_w)]
        for cp in mine:
            cp.start()

        c_recv()
        row = lax.broadcasted_iota(jnp.int32, (8, D_MODEL), 0)
        c_all = jnp.zeros((8, D_MODEL), F32)
        for d in range(N_DEV):
            c_all = c_all + jnp.where(row == d, cg_ref[d], 0.0)
        cc = jnp.broadcast_to(cctx_ref[...], (8, D_MODEL))
        a = jnp.concatenate([c_all * _sigmoid(c_all), cc * _sigmoid(cc)], axis=0)
        modblk_ref[...] = _dot(a.astype(BF16), wmod_ref[...].astype(BF16))
        modg_ref[_lin(me)] = modblk_ref[...]
        m_recv, m_send = _gather_to_all(modblk_ref, modg_ref, ssem_m, rsem_m)
        for idx in late:
            first[idx].start()
        m_recv()

        passed = []
        for j, chip in enumerate(chips):
            for cp in wcopies(1 + j, (*chip, c), me):
                cp.wait_recv()
            fwd = wcopies(4 + j, (*chip, c), sibling)
            for cp in fwd:
                cp.start()
            passed += fwd
        for cp in wcopies(0, sibling, me):
            cp.wait_recv()
        for j, chip in enumerate(chips):
            for cp in wcopies(4 + j, (*chip, 1 - c), me):
                cp.wait_recv()
        for cp in first + passed:
            cp.wait_send()
        for cp in mine:
            cp.wait()
        c_send()
        m_send()

    vm = pl.BlockSpec(memory_space=pltpu.VMEM)
    hbm = pl.BlockSpec(memory_space=pl.ANY)
    return pl.pallas_call(
        body, name="prologue_gather",
        out_shape=(jax.ShapeDtypeStruct((N_DEV, 8, D_MODEL), F32), jax.ShapeDtypeStruct((N_DEV, 16, n_mod), F32),
                   *[jax.ShapeDtypeStruct((N_DEV,) + s.shape, BF16) for s in shards]),
        in_specs=[vm] * (3 + n_w), out_specs=(vm, vm, *[hbm] * n_w),
        scratch_shapes=[pltpu.VMEM((16, n_mod), F32), *[pltpu.VMEM(s.shape, BF16) for s in shards],
                        pltpu.VMEM((8, D_MODEL), F32),
                        pltpu.SemaphoreType.DMA((n_w, 7)), pltpu.SemaphoreType.DMA((n_w, 7)), pltpu.SemaphoreType.DMA((n_w,)),
                        pltpu.SemaphoreType.DMA((7,)), pltpu.SemaphoreType.DMA((7,)),
                        pltpu.SemaphoreType.DMA((7,)), pltpu.SemaphoreType.DMA((7,))],
        compiler_params=pltpu.CompilerParams(vmem_limit_bytes=VMEM_LIMIT),
    )(c8, cctx8, w_mod_l, *shards)


def _modulated_input(is_ctx, x_ref, ctx_ref, mod_ref, bmod_ref, ng_ref):
    xt = jnp.where(is_ctx, ctx_ref[...], x_ref[...])
    shift = jnp.where(is_ctx, mod_ref[3:4, :] + bmod_ref[3:4, :], mod_ref[0:1, :] + bmod_ref[0:1, :])
    scale = jnp.where(is_ctx, mod_ref[4:5, :] + bmod_ref[4:5, :], mod_ref[1:2, :] + bmod_ref[1:2, :])
    r = lax.rsqrt(jnp.mean(xt * xt, axis=-1, keepdims=True) + NORM_EPS)
    xg = (xt * r) * ng_ref[...]
    h = xg * (1.0 + scale) + shift
    return xt, r, xg, h, scale


def _inproj_fwd(x, ctx, modrows, bmodrows, norm_g, w_in_p, q_lora_g, w_uq_p, kv_lora_g, w_ukv, qng_p, kng_p, cos, slo, shi):
    seq = x.shape[0]
    n_tiles = seq // TILE + 1
    tot = seq + TILE

    n_mla = R_Q + R_KV + 128

    def body(x_ref, ctx_ref, mod_ref, bmod_ref, ng_ref, win_ref, qlg_ref, wuq_ref, kvlg_ref, wukv_ref, qng_ref, kng_ref,
             cos_ref, slo_ref, shi_ref, ug_ref, um_ref, q_ref, k_ref, v_ref, stash):
        s = pl.program_id(0)

        @pl.when(s == 0)
        def _():
            stash[...] = jnp.zeros_like(stash)

        refs = (x_ref, ctx_ref, mod_ref, bmod_ref, ng_ref, win_ref, qlg_ref, wuq_ref, kvlg_ref, wukv_ref, qng_ref, kng_ref,
                cos_ref, slo_ref, shi_ref, ug_ref, um_ref, q_ref, k_ref, v_ref)
        stages(s, refs, stash, stash)

    def stages(s, refs, fill, prev_ref):
        (x_ref, ctx_ref, mod_ref, bmod_ref, ng_ref, win_ref, qlg_ref, wuq_ref, kvlg_ref, wukv_ref, qng_ref, kng_ref,
         cos_ref, slo_ref, shi_ref, ug_ref, um_ref, q_ref, k_ref, v_ref) = refs
        cos_t, slo_t, shi_t = cos_ref[...], slo_ref[...], shi_ref[...]
        prev = prev_ref[...]
        cq = prev[:, 0:R_Q]
        qn = cq * lax.rsqrt(jnp.mean(cq * cq, axis=-1, keepdims=True) + NORM_EPS) * qlg_ref[...]
        q = _dot_nt(qn.astype(BF16), wuq_ref[...])
        qg = qng_ref[...] * (ATTN_SCALE * LOG2_E)
        for hd in range(N_HEADS):
            qh = q[:, hd * D_HEAD_PAD:(hd + 1) * D_HEAD_PAD]
            rr = lax.rsqrt(jnp.sum(qh * qh, axis=-1, keepdims=True) * (1.0 / D_HEAD) + NORM_EPS)
            qy = qh * rr * qg
            q_ref[hd, :, 0:D_NOPE] = qy[:, 0:D_NOPE].astype(BF16)
            q_ref[hd, :, D_NOPE:D_HEAD_PAD] = _rope(qy[:, D_NOPE:D_HEAD_PAD], cos_t, slo_t, shi_t).astype(BF16)

        ckv = prev[:, R_Q:R_Q + R_KV]
        kvn = ckv * lax.rsqrt(jnp.mean(ckv * ckv, axis=-1, keepdims=True) + NORM_EPS) * kvlg_ref[...]
        kv = _dot(kvn.astype(BF16), wukv_ref[...])
        krz = prev[:, R_Q + R_KV:n_mla]
        kr_ss = jnp.sum(krz * krz, axis=-1, keepdims=True)
        kg = kng_ref[...]
        for hd in range(N_HEADS):
            kn = kv[:, hd * 256:hd * 256 + D_NOPE]
            rr = lax.rsqrt((jnp.sum(kn * kn, axis=-1, keepdims=True) + kr_ss) * (1.0 / D_HEAD) + NORM_EPS)
            k_ref[hd, :, 0:D_NOPE] = (kn * rr * kg[:, 0:D_NOPE]).astype(BF16)
            k_ref[hd, :, D_NOPE:D_HEAD_PAD] = _rope(krz * rr * kg[:, D_NOPE:D_HEAD_PAD], cos_t, slo_t, shi_t).astype(BF16)
            v_ref[hd] = kv[:, hd * 256 + D_NOPE:(hd + 1) * 256].astype(BF16)

        _, _, _, h, _ = _modulated_input(s == 0, x_ref, ctx_ref, mod_ref, bmod_ref, ng_ref)
        hb = h.astype(BF16)
        ug_ref[...] = _dot_nt(hb, win_ref[N_MLA:D_IN_PROJ, :]).astype(BF16)
        um = _dot_nt(hb, win_ref[0:n_mla, :])
        lane = lax.broadcasted_iota(jnp.int32, (TILE, 128), 1)
        um = jnp.concatenate([um[:, 0:R_Q + R_KV], jnp.where(lane < D_ROPE, um[:, R_Q + R_KV:n_mla], 0.0)], axis=1)
        um_ref[...] = um
        fill[...] = um

    first = lambda s: jnp.minimum(s, n_tiles - 1)
    second = lambda s: jnp.maximum(s - 1, 0)
    latent = lambda t: jnp.maximum(t - 1, 0)
    full = lambda shape: pl.BlockSpec(shape, lambda s: (0,) * len(shape))
    rope_spec = pl.BlockSpec((TILE, 128), lambda s: (second(s), 0))
    return pl.pallas_call(
        body, name="inproj_fwd", grid=(n_tiles + 1,),
        out_shape=(jax.ShapeDtypeStruct((seq, N_GATES), BF16), jax.ShapeDtypeStruct((tot, n_mla), F32),
                   jax.ShapeDtypeStruct((N_HEADS, seq, D_HEAD_PAD), BF16),
                   jax.ShapeDtypeStruct((N_HEADS, tot, D_HEAD_PAD), BF16),
                   jax.ShapeDtypeStruct((N_HEADS, tot, D_V), BF16)),
        in_specs=[pl.BlockSpec((TILE, D_MODEL), lambda s: (latent(first(s)), 0)), full((TILE, D_MODEL)), full((8, D_MODEL)),
                  full((8, D_MODEL)), full((1, D_MODEL)), full((D_IN_PROJ, D_MODEL)), full((1, R_Q)),
                  full((N_HEADS * D_HEAD_PAD, R_Q)), full((1, R_KV)), full((R_KV, N_HEADS * 256)), full((1, D_HEAD_PAD)),
                  full((1, D_HEAD_PAD)), rope_spec, rope_spec, rope_spec],
        out_specs=(pl.BlockSpec((TILE, N_GATES), lambda s: (latent(first(s)), 0)),
                   pl.BlockSpec((TILE, n_mla), lambda s: (first(s), 0)),
                   pl.BlockSpec((N_HEADS, TILE, D_HEAD_PAD), lambda s: (0, latent(second(s)), 0)),
                   pl.BlockSpec((N_HEADS, TILE, D_HEAD_PAD), lambda s: (0, second(s), 0)),
                   pl.BlockSpec((N_HEADS, TILE, D_V), lambda s: (0, second(s), 0))),
        scratch_shapes=[pltpu.VMEM((TILE, n_mla), F32)],
        compiler_params=pltpu.CompilerParams(dimension_semantics=("arbitrary",), vmem_limit_bytes=VMEM_LIMIT),
    )(x, ctx, modrows, bmodrows, norm_g, w_in_p, q_lora_g, w_uq_p, kv_lora_g, w_ukv, qng_p, kng_p, cos, slo, shi)


def _hosted_exchange(first, last, items, send_sems, recv_sems, local_sems):
    me = _lin(_coords())

    def remote(n, k, src, land, scatter):
        peer = _peer(k)
        return pltpu.make_async_remote_copy(
            src_ref=src.at[_lin(peer)] if scatter else src, dst_ref=land.at[me], send_sem=send_sems.at[n, k - 1],
            recv_sem=recv_sems.at[n, k - 1], device_id=peer, device_id_type=MESH)

    def local(n, src, land, scatter):
        return pltpu.make_async_copy(src.at[me] if scatter else src, land.at[me], local_sems.at[n])

    @pl.when(first)
    def _():
        for n, (src, land, scatter) in enumerate(items):
            for k in range(1, N_DEV):
                remote(n, k, src, land, scatter).start()
            local(n, src, land, scatter).start()

    @pl.when(last)
    def _():
        for n, (src, land, scatter) in enumerate(items):
            for k in range(1, N_DEV):
                peer = _peer(k)
                pltpu.make_async_remote_copy(
                    src_ref=src.at[0] if scatter else src, dst_ref=land.at[_lin(peer)], send_sem=send_sems.at[n, k - 1],
                    recv_sem=recv_sems.at[n, k - 1], device_id=peer, device_id_type=MESH).wait_recv()
            for k in range(1, N_DEV):
                remote(n, k, src, land, scatter).wait_send()
            local(n, src, land, scatter).wait()


def _attn_fwd(q, k, v, block_q, w_out_b):
    _, seq, _ = q.shape
    n_keys = k.shape[1]
    n_q = seq // block_q

    def body(q_ref, k_ref, v_ref, wo_ref, o_ref, lse_ref, wog_ref, ssem, rsem, lsem):
        h, i = pl.program_id(0), pl.program_id(1)
        _hosted_exchange((h == 0) & (i == 0), (h == N_HEADS - 1) & (i == n_q - 1), [(wo_ref, wog_ref, False)],
                         ssem, rsem, lsem)
        kb, vb = k_ref[0], v_ref[0]
        for r0 in range(0, block_q, ATTN_ROWS):
            rows = slice(r0, r0 + ATTN_ROWS)
            s = _dot_nt(q_ref[0, rows, :], kb)
            m = jnp.max(s, axis=-1, keepdims=True)
            p = jnp.exp2(s - m)
            l = jnp.sum(p, axis=-1, keepdims=True)
            o_ref[rows, :] = _dot(p.astype(BF16), vb) * (1.0 / l)
            lse_ref[0, rows, :] = m + jnp.log2(l)

    hbm = pl.BlockSpec(memory_space=pl.ANY)
    return pl.pallas_call(
        body, name="attn_fwd", grid=(N_HEADS, n_q),
        out_shape=(jax.ShapeDtypeStruct((seq, D_ATTN), F32), jax.ShapeDtypeStruct((N_HEADS, seq, 1), F32),
                   jax.ShapeDtypeStruct((N_DEV,) + w_out_b.shape, w_out_b.dtype)),
        in_specs=[pl.BlockSpec((1, block_q, D_HEAD_PAD), lambda h, i: (h, i, 0)),
                  pl.BlockSpec((1, n_keys, D_HEAD_PAD), lambda h, i: (h, 0, 0)),
                  pl.BlockSpec((1, n_keys, D_V), lambda h, i: (h, 0, 0)), hbm],
        out_specs=(pl.BlockSpec((block_q, D_V), lambda h, i: (i, h)),
                   pl.BlockSpec((1, block_q, 1), lambda h, i: (h, i, 0)), hbm),
        scratch_shapes=[pltpu.SemaphoreType.DMA((1, 7)), pltpu.SemaphoreType.DMA((1, 7)), pltpu.SemaphoreType.DMA((1,))],
        compiler_params=pltpu.CompilerParams(dimension_semantics=("arbitrary", "arbitrary"), vmem_limit_bytes=VMEM_LIMIT),
    )(q, k, v, w_out_b)


def _attn_bwd(q, k, v, o, lse, d_o, block_q, gwo_blocks, gwp):
    _, seq, _ = q.shape
    n_keys = k.shape[1]
    n_q = seq // block_q

    def body(q_ref, k_ref, v_ref, o_ref, lse_ref, do_ref, gwo_ref, gwp_ref, dq_ref, dkt_ref, dvt_ref, lwo_ref, lwp_ref,
             ssem, rsem, lsem):
        h, i = pl.program_id(0), pl.program_id(1)
        _hosted_exchange((h == 0) & (i == 0), (h == N_HEADS - 1) & (i == n_q - 1),
                         [(gwo_ref, lwo_ref, True), (gwp_ref, lwp_ref, False)], ssem, rsem, lsem)

        @pl.when(i == 0)
        def _():
            dkt_ref[...] = jnp.zeros_like(dkt_ref)
            dvt_ref[...] = jnp.zeros_like(dvt_ref)

        kb, vb = k_ref[0], v_ref[0]
        raws, ps = [], []
        for r0 in range(0, block_q, ATTN_ROWS):
            rows = slice(r0, r0 + ATTN_ROWS)
            d_out = do_ref[rows, :]
            p = jnp.exp2(_dot_nt(q_ref[0, rows, :], kb) - lse_ref[0, rows, :])
            dp = _dot_nt(d_out.astype(BF16), vb)
            delta = jnp.sum(d_out * o_ref[rows, :], axis=-1, keepdims=True)
            raw = (p * (dp - delta)).astype(BF16)
            dq_ref[0, rows, :] = _dot(raw, kb)
            raws.append(raw)
            ps.append(p.astype(BF16))
        dkt_ref[0] += _dot_tn(q_ref[0], jnp.concatenate(raws, axis=0))
        dvt_ref[0] += _dot_tn(do_ref[...].astype(BF16), jnp.concatenate(ps, axis=0))

    hbm = pl.BlockSpec(memory_space=pl.ANY)
    return pl.pallas_call(
        body, name="attn_bwd", grid=(N_HEADS, n_q),
        out_shape=(jax.ShapeDtypeStruct((N_HEADS, seq, D_HEAD_PAD), F32),
                   jax.ShapeDtypeStruct((N_HEADS, D_HEAD_PAD, n_keys), F32),
                   jax.ShapeDtypeStruct((N_HEADS, D_V, n_keys), F32),
                   jax.ShapeDtypeStruct(gwo_blocks.shape, gwo_blocks.dtype),
                   jax.ShapeDtypeStruct((N_DEV,) + gwp.shape, gwp.dtype)),
        in_specs=[pl.BlockSpec((1, block_q, D_HEAD_PAD), lambda h, i: (h, i, 0)),
                  pl.BlockSpec((1, n_keys, D_HEAD_PAD), lambda h, i: (h, 0, 0)),
                  pl.BlockSpec((1, n_keys, D_V), lambda h, i: (h, 0, 0)),
                  pl.BlockSpec((block_q, D_V), lambda h, i: (i, h)),
                  pl.BlockSpec((1, block_q, 1), lambda h, i: (h, i, 0)),
                  pl.BlockSpec((block_q, D_V), lambda h, i: (i, h)), hbm, hbm],
        out_specs=(pl.BlockSpec((1, block_q, D_HEAD_PAD), lambda h, i: (h, i, 0)),
                   pl.BlockSpec((1, D_HEAD_PAD, n_keys), lambda h, i: (h, 0, 0)),
                   pl.BlockSpec((1, D_V, n_keys), lambda h, i: (h, 0, 0)), hbm, hbm),
        scratch_shapes=[pltpu.SemaphoreType.DMA((2, 7)), pltpu.SemaphoreType.DMA((2, 7)), pltpu.SemaphoreType.DMA((2,))],
        compiler_params=pltpu.CompilerParams(dimension_semantics=("arbitrary", "arbitrary"), vmem_limit_bytes=VMEM_LIMIT),
    )(q, k, v, o, lse, d_o, gwo_blocks, gwp)


def _mix(x, target, attn, u, modrows, bmodrows, w_pool, pool_scale, w_out):
    seq = x.shape[0]
    rows = min(MIX_TILE, seq // 2)
    n_tiles = seq // rows
    rows8 = rows // HALO
    last8 = seq // HALO - 1

    n_blk = seq // Q_BLOCK
    per = rows // n_blk
    assert rows % n_blk == 0 and per % 8 == 0 and n_tiles >= 2
    n_stash = 8

    def body(x_ref, t_ref, o_hbm, ga_ref, pin_ref, hb_ref, ha_ref, gp_ref, mod_ref, bmod_ref, wp_ref, ps_ref, wo_ref,
             loss_ref, g1_ref, dgate_ref, do_hbm, dga_ref, dgp_ref, dpl_ref, gwob_ref, gwp_ref, dps_ref,
             abuf, dbuf, gwo_ref, *rest):
        stash_even, stash_odd = rest[0:n_stash], rest[n_stash:2 * n_stash]
        rsem, wsem = rest[2 * n_stash:]
        s = pl.program_id(0)

        def slab_copies(tile, slot, fetch):
            window = pl.ds(tile * per, per)
            if fetch:
                return [pltpu.make_async_copy(o_hbm.at[:, window, :], abuf.at[slot], rsem.at[slot])]
            return [pltpu.make_async_copy(dbuf.at[slot], do_hbm.at[:, window, :], wsem.at[slot])]

        def wait_all(slot, fetch):
            slab_copies(0, slot, fetch)[0].wait()

        a_slot = lax.rem(s, 2)
        b_slot = 1 - a_slot

        @pl.when(s == 0)
        def _():
            loss_ref[...] = jnp.zeros_like(loss_ref)
            dgate_ref[...] = jnp.zeros_like(dgate_ref)
            gwo_ref[...] = jnp.zeros_like(gwo_ref)
            gwp_ref[...] = jnp.zeros_like(gwp_ref)
            dps_ref[...] = jnp.zeros_like(dps_ref)
            for cp in slab_copies(0, 0, True):
                cp.start()

        @pl.when(s + 1 < n_tiles)
        def _():
            for cp in slab_copies(s + 1, b_slot, True):
                cp.start()

        @pl.when(s < n_tiles)
        def _():
            wait_all(a_slot, True)

        @pl.when(s >= 3)
        def _():
            wait_all(b_slot, False)

        gate = mod_ref[2:3, :] + bmod_ref[2:3, :]
        ps = ps_ref[...]

        def a_vector(slot):
            attn_t = jnp.swapaxes(abuf[slot], 0, 1).reshape(rows, D_ATTN)
            ga = ga_ref[...].astype(F32)
            sga = _sigmoid(ga)
            silu_ga = ga * sga
            pin = pin_ref[...].astype(F32)
            xs = jnp.concatenate([jnp.where(s > 0, hb_ref[...].astype(F32), 0.0), pin,
                                  jnp.where(s < n_tiles - 1, ha_ref[...].astype(F32), 0.0)], axis=0)
            tpos = s * rows + lax.broadcasted_iota(jnp.int32, (rows, 1), 0)
            pooled = []
            for g, w in enumerate(POOL_WINDOWS):
                sl = slice(g * GROUP_DIM, (g + 1) * GROUP_DIM)
                ws = _window_sum(xs[:, sl], w, False)[HALO:HALO + rows]
                pooled.append((ws * _inv_count(tpos, w, seq) - pin[:, sl]).astype(BF16))
            return attn_t, ga, sga, silu_ga, pooled

        def a_group_maps(pooled):
            return jnp.concatenate([_dot(pooled[g], wp_ref[g].astype(BF16)) for g in range(len(POOL_WINDOWS))], axis=1)

        def a_project(stash, yp, attn_t, ga, sga, silu_ga, pooled):
            zb_ref, _, fa_ref, sa_ref, fp_ref, sp_ref, yp_ref, pooled_ref = stash
            ypool = yp * ps
            gp = gp_ref[...].astype(F32)
            sgp = _sigmoid(gp)
            silu_gp = gp * sgp
            z = jnp.concatenate([silu_ga * attn_t, silu_gp * ypool], axis=1).astype(BF16)
            y = _dot(z, wo_ref[...])
            zb_ref[...] = z
            fa_ref[...] = attn_t * (sga * (1.0 + ga * (1.0 - sga)))
            sa_ref[...] = silu_ga
            fp_ref[...] = ypool * (sgp * (1.0 + gp * (1.0 - sgp)))
            sp_ref[...] = silu_gp
            yp_ref[...] = yp
            pooled_ref[...] = jnp.concatenate(pooled, axis=1)
            return y

        def a_loss(stash, y):
            res = x_ref[...] + gate * y - t_ref[...]
            loss_ref[...] += jnp.sum(res * res) * (0.5 / D_MODEL)
            dxn = res * (1.0 / D_MODEL)
            g1_ref[...] = dxn
            dgate_ref[...] += jnp.sum(dxn * y, axis=0, keepdims=True)
            stash[1][...] = (gate * dxn).astype(BF16)

        def b_dz(stash):
            return _dot_nt(stash[1][...], wo_ref[...])

        def b_gwo(stash):
            gwo_ref[...] += _dot_tn(stash[0][...], stash[1][...])

        def b_vector(stash, slot, dz):
            _, _, fa_ref, sa_ref, fp_ref, sp_ref, yp_ref, _ = stash
            dbra = dz[:, 0:D_ATTN]
            dbrp = dz[:, D_ATTN:]
            dga_ref[...] = (dbra * fa_ref[...]).astype(BF16)
            dbuf[slot] = jnp.swapaxes((dbra * sa_ref[...]).reshape(per, n_blk, D_ATTN), 0, 1)
            dgp_ref[...] = (dbrp * fp_ref[...]).astype(BF16)
            dyp_s = dbrp * sp_ref[...]
            dps_ref[...] += jnp.sum(dyp_s * yp_ref[...], axis=0, keepdims=True)
            return (dyp_s * ps).astype(BF16)

        def b_small(stash, dyp):
            pooled_ref = stash[7]
            for g in range(len(POOL_WINDOWS)):
                sl = slice(g * GROUP_DIM, (g + 1) * GROUP_DIM)
                gwp_ref[g] += _dot_tn(pooled_ref[:, sl], dyp[:, sl])
                dpl_ref[:, sl] = _dot_nt(dyp[:, sl], wp_ref[g].astype(BF16)).astype(BF16)

        def both(a_stash, b_stash, slot_a):
            dz = b_dz(b_stash)
            front = a_vector(slot_a)
            yp = a_group_maps(front[-1])
            b_gwo(b_stash)
            y = a_project(a_stash, yp, *front)
            dyp = b_vector(b_stash, 1 - slot_a, dz)
            a_loss(a_stash, y)
            b_small(b_stash, dyp)

        @pl.when(s == 0)
        def _():
            front = a_vector(0)
            a_loss(stash_even, a_project(stash_even, a_group_maps(front[-1]), *front))

        @pl.when((s > 0) & (s < n_tiles) & (a_slot == 0))
        def _():
            both(stash_even, stash_odd, 0)

        @pl.when((s > 0) & (s < n_tiles) & (a_slot == 1))
        def _():
            both(stash_odd, stash_even, 1)

        @pl.when(s == n_tiles)
        def _():
            last = (n_tiles - 1) % 2
            stash = stash_odd if last else stash_even
            dz = b_dz(stash)
            b_gwo(stash)
            b_small(stash, b_vector(stash, last, dz))
            gwob_ref[...] = gwo_ref[...].astype(BF16)

        @pl.when(s >= 1)
        def _():
            for cp in slab_copies(s - 1, b_slot, False):
                cp.start()

        @pl.when(s == n_tiles)
        def _():
            wait_all(b_slot, False)
            wait_all(a_slot, False)

    ta = lambda s: jnp.minimum(s, n_tiles - 1)
    tb = lambda s: jnp.maximum(s - 1, 0)
    tile = lambda w: pl.BlockSpec((rows, w), lambda s: (ta(s), 0))
    tile_b = lambda w: pl.BlockSpec((rows, w), lambda s: (tb(s), 0))
    ucol = lambda col: pl.BlockSpec((rows, 512), lambda s: (ta(s), col))
    full = lambda shape: pl.BlockSpec(shape, lambda s: (0,) * len(shape))
    stash_shapes = [pltpu.VMEM((rows, D_MODEL), BF16), pltpu.VMEM((rows, D_MODEL), BF16)] + \
        [pltpu.VMEM((rows, 512), F32)] * 5 + [pltpu.VMEM((rows, D_POOL), BF16)]
    return pl.pallas_call(
        body, name="mix_fwd_bwd", grid=(n_tiles + 1,),
        out_shape=(jax.ShapeDtypeStruct((8, 128), F32), jax.ShapeDtypeStruct((seq, D_MODEL), F32),
                   jax.ShapeDtypeStruct((1, D_MODEL), F32), jax.ShapeDtypeStruct((n_blk, Q_BLOCK, D_ATTN), F32),
                   jax.ShapeDtypeStruct((seq, D_ATTN), BF16), jax.ShapeDtypeStruct((seq, D_POOL), BF16),
                   jax.ShapeDtypeStruct((seq, D_POOL), BF16), jax.ShapeDtypeStruct((D_MODEL, D_MODEL), BF16),
                   jax.ShapeDtypeStruct((4, GROUP_DIM, GROUP_DIM), F32), jax.ShapeDtypeStruct((1, D_POOL), F32)),
        in_specs=[tile(D_MODEL), tile(D_MODEL), pl.BlockSpec(memory_space=pl.ANY), ucol(0), ucol(1),
                  pl.BlockSpec((HALO, 512), lambda s: (jnp.maximum(ta(s) * rows8 - 1, 0), 1)),
                  pl.BlockSpec((HALO, 512), lambda s: (jnp.minimum((ta(s) + 1) * rows8, last8), 1)),
                  ucol(2), full((8, D_MODEL)), full((8, D_MODEL)), full((4, GROUP_DIM, GROUP_DIM)), full((1, D_POOL)),
                  full((D_MODEL, D_MODEL))],
        out_specs=(full((8, 128)), tile(D_MODEL), full((1, D_MODEL)), pl.BlockSpec(memory_space=pl.ANY), tile_b(D_ATTN),
                   tile_b(D_POOL), tile_b(D_POOL), full((D_MODEL, D_MODEL)), full((4, GROUP_DIM, GROUP_DIM)),
                   full((1, D_POOL))),
        scratch_shapes=[pltpu.VMEM((2, n_blk, per, D_ATTN), F32), pltpu.VMEM((2, n_blk, per, D_ATTN), F32),
                        pltpu.VMEM((D_MODEL, D_MODEL), F32), *stash_shapes, *stash_shapes,
                        pltpu.SemaphoreType.DMA((2,)), pltpu.SemaphoreType.DMA((2,))],
        compiler_params=pltpu.CompilerParams(dimension_semantics=("arbitrary",), vmem_limit_bytes=VMEM_LIMIT),
    )(x, target, attn.reshape(n_blk, Q_BLOCK, D_ATTN), u, u, u, u, u, modrows, bmodrows, w_pool, pool_scale, w_out)


def _inproj_bwd(x, ctx, modrows, bmodrows, norm_g, w_in_p, q_lora_g, w_uq_p, kv_lora_g, w_ukv, qng_p, kng_p, cos, slo, shi,
                u, dq, dk, dv, dga, dgp, dpl, g1):
    seq = x.shape[0]
    n_tiles = seq // TILE + 1
    rows8 = TILE // HALO
    last8 = seq // HALO - 1

    def body(*refs):
        du_even, du_odd, dh_even, dh_odd = refs[-4:]
        gwin_ref, gwuq_ref, gwukv_ref, dqlg_ref, dkvlg_ref, dqng_ref, dkng_ref, dng_ref, dmod_ref = refs[-13:-4]
        s = pl.program_id(0)

        @pl.when(s == 0)
        def _():
            for ref in (gwin_ref, gwuq_ref, gwukv_ref, dqlg_ref, dkvlg_ref, dqng_ref, dkng_ref, dng_ref, dmod_ref,
                        du_odd, dh_even):
                ref[...] = jnp.zeros_like(ref)

        @pl.when(lax.rem(s, 2) == 0)
        def _():
            stages(s, refs[:-4], du_even, du_odd, dh_odd, dh_even)

        @pl.when(lax.rem(s, 2) == 1)
        def _():
            stages(s, refs[:-4], du_odd, du_even, dh_even, dh_odd)

    def stages(s, refs, du_ref, du_prev, dh_fill, dh_prev):
        (xb_ref, xc_ref, ctx_ref, mod_ref, bmod_ref, ng_ref, win_ref, qlg_ref, wuq_ref, kvlg_ref, wukv_ref, qng_ref, kng_ref,
         cos_ref, slo_ref, shi_ref, cq_ref, ckv_ref, kr_ref, dq_ref, dk_ref, dv_ref, dga_ref, dgp_ref, dpl_ref,
         hb_ref, ha_ref, g1_ref,
         gx_ref, gwin_ref, gwuq_ref, gwukv_ref, dqlg_ref, dkvlg_ref, dqng_ref, dkng_ref, dng_ref, dmod_ref) = refs


        i = jnp.minimum(s, n_tiles - 1)
        valid = s < n_tiles
        is_lat = (s > 0) & valid
        cq = cq_ref[...]
        rq = lax.rsqrt(jnp.mean(cq * cq, axis=-1, keepdims=True) + NORM_EPS)
        qhat = cq * rq
        qnb = (qhat * qlg_ref[...]).astype(BF16)
        q = _dot_nt(qnb, wuq_ref[...])
        ckv = ckv_ref[...]
        rkv = lax.rsqrt(jnp.mean(ckv * ckv, axis=-1, keepdims=True) + NORM_EPS)
        kvhat = ckv * rkv
        kvnb = (kvhat * kvlg_ref[...]).astype(BF16)
        kv = _dot(kvnb, wukv_ref[...])

        _, _, _, h2, _ = _modulated_input(s <= 1, xb_ref, ctx_ref, mod_ref, bmod_ref, ng_ref)
        dub = du_prev[...]
        du_g, du_m = dub[:, 0:N_GATES], dub[:, N_GATES:U_PAD]
        h2b = h2.astype(BF16)
        dh_fill[...] = _dot(du_g, win_ref[N_MLA:D_IN_PROJ, :]) + _dot(du_m, win_ref[0:U_PAD - N_GATES, :])
        gwin_ref[N_MLA:D_IN_PROJ, :] += _dot_tn(du_g, h2b)
        gwin_ref[0:N_MLA, :] += _dot_tn(du_m, h2b)[0:N_MLA]

        ctx3 = s <= 2
        xt, r, xg, _, scale = _modulated_input(ctx3, xc_ref, ctx_ref, mod_ref, bmod_ref, ng_ref)
        dh = dh_prev[...]
        dshift = jnp.sum(dh, axis=0, keepdims=True)
        dscale = jnp.sum(dh * xg, axis=0, keepdims=True)
        zero = jnp.zeros_like(dshift)
        dmod_ref[0:1, :] += jnp.where(ctx3, zero, dshift)
        dmod_ref[1:2, :] += jnp.where(ctx3, zero, dscale)
        dmod_ref[2:3, :] += jnp.where(ctx3, dshift, zero)
        dmod_ref[3:4, :] += jnp.where(ctx3, dscale, zero)
        dxg = dh * (1.0 + scale)
        xr = xt * r
        dng_ref[...] += jnp.sum(dxg * xr, axis=0, keepdims=True)
        dxn = dxg * ng_ref[...]
        gx_ref[...] = g1_ref[...] + r * (dxn - xr * jnp.mean(dxn * xr, axis=-1, keepdims=True))

        cos_t, slo_t, shi_t = cos_ref[...], slo_ref[...], shi_ref[...]
        qg = qng_ref[...]
        dq_heads = []
        dqng = jnp.zeros((1, D_HEAD_PAD), F32)
        for hd in range(N_HEADS):
            qh = q[:, hd * D_HEAD_PAD:(hd + 1) * D_HEAD_PAD]
            rr = lax.rsqrt(jnp.sum(qh * qh, axis=-1, keepdims=True) * (1.0 / D_HEAD) + NORM_EPS)
            yq = qh * rr
            dpost = jnp.where(is_lat, dq_ref[hd] * ATTN_SCALE, 0.0)
            dyn = jnp.concatenate([dpost[:, 0:D_NOPE], _rope_t(dpost[:, D_NOPE:], cos_t, slo_t, shi_t)], axis=1)
            dqng = dqng + jnp.sum(dyn * yq, axis=0, keepdims=True)
            dyg = dyn * qg
            dq_heads.append(rr * (dyg - yq * (jnp.sum(dyg * yq, axis=-1, keepdims=True) * (1.0 / D_HEAD))))
        dqng_ref[...] += dqng
        dqf = jnp.concatenate(dq_heads, axis=1).astype(BF16)

        krz = kr_ref[...]
        kr_ss = jnp.sum(krz * krz, axis=-1, keepdims=True)
        kg = kng_ref[...]
        kg_n, kg_r = kg[:, 0:D_NOPE], kg[:, D_NOPE:]
        dkv_parts = []
        dkr = jnp.zeros((TILE, 128), F32)
        dkng_n = jnp.zeros((1, D_NOPE), F32)
        dkng_r = jnp.zeros((1, 128), F32)
        for hd in range(N_HEADS):
            kn = kv[:, hd * 256:hd * 256 + D_NOPE]
            rr = lax.rsqrt((jnp.sum(kn * kn, axis=-1, keepdims=True) + kr_ss) * (1.0 / D_HEAD) + NORM_EPS)
            yn, yr = kn * rr, krz * rr
            dk_h = jnp.where(valid, jnp.transpose(dk_ref[hd]) * LN_2, 0.0)
            dpn = dk_h[:, 0:D_NOPE]
            dpr = _rope_t(dk_h[:, D_NOPE:], cos_t, slo_t, shi_t)
            dkng_n = dkng_n + jnp.sum(dpn * yn, axis=0, keepdims=True)
            dkng_r = dkng_r + jnp.sum(dpr * yr, axis=0, keepdims=True)
            dyn, dyr = dpn * kg_n, dpr * kg_r
            proj = (jnp.sum(dyn * yn, axis=-1, keepdims=True) + jnp.sum(dyr * yr, axis=-1, keepdims=True)) * (1.0 / D_HEAD)
            dkv_parts.append(rr * (dyn - yn * proj))
            dkv_parts.append(jnp.where(valid, jnp.transpose(dv_ref[hd]), 0.0))
            dkr = dkr + rr * (dyr - yr * proj)
        dkng_ref[:, 0:D_NOPE] += dkng_n
        dkng_ref[:, D_NOPE:] += dkng_r
        dkvf = jnp.concatenate(dkv_parts, axis=1).astype(BF16)

        lat_t = jnp.maximum(i - 1, 0)
        dpl_t = dpl_ref[...].astype(F32)
        ds = jnp.concatenate([jnp.where(i > 1, hb_ref[...].astype(F32), 0.0), dpl_t,
                              jnp.where(i < n_tiles - 1, ha_ref[...].astype(F32), 0.0)], axis=0)
        tpos = lat_t * TILE - HALO + lax.broadcasted_iota(jnp.int32, (TILE + 2 * HALO, 1), 0)
        for g, w in enumerate(POOL_WINDOWS):
            sl = slice(g * GROUP_DIM, (g + 1) * GROUP_DIM)
            wsum = _window_sum(ds[:, sl] * _inv_count(tpos, w, seq), w, True)[HALO:HALO + TILE]
            du_ref[:, O_PIN + g * GROUP_DIM:O_PIN + (g + 1) * GROUP_DIM] = jnp.where(
                is_lat, wsum - dpl_t[:, sl], 0.0).astype(BF16)

        du_ref[:, O_GA:O_GA + D_ATTN] = jnp.where(is_lat, dga_ref[...], jnp.zeros((), BF16))
        du_ref[:, O_GP:O_GP + D_POOL] = jnp.where(is_lat, dgp_ref[...], jnp.zeros((), BF16))
        du_ref[:, O_KR:U_PAD] = dkr.astype(BF16)

        gwuq_ref[...] += _dot_tn(dqf, qnb)
        dqn = _dot(dqf, wuq_ref[...])
        gwukv_ref[...] += _dot_tn(dkvf, kvnb)
        dkvn = _dot_nt(dkvf, wukv_ref[...])
        dqlg_ref[...] += jnp.sum(dqn * qhat, axis=0, keepdims=True)
        dqhat = dqn * qlg_ref[...]
        dcq = rq * (dqhat - qhat * jnp.mean(dqhat * qhat, axis=-1, keepdims=True))
        dkvlg_ref[...] += jnp.sum(dkvn * kvhat, axis=0, keepdims=True)
        dkvhat = dkvn * kvlg_ref[...]
        dckv = rkv * (dkvhat - kvhat * jnp.mean(dkvhat * kvhat, axis=-1, keepdims=True))
        du_ref[:, O_CQ:O_CQ + R_Q] = dcq.astype(BF16)
        du_ref[:, O_CKV:O_CKV + R_KV] = dckv.astype(BF16)

    t1 = lambda s: jnp.minimum(s, n_tiles - 1)
    t2 = lambda s: jnp.clip(s - 1, 0, n_tiles - 1)
    t3 = lambda s: jnp.clip(s - 2, 0, n_tiles - 1)
    latent = lambda t: jnp.maximum(t - 1, 0)
    lat = lambda s: (latent(t1(s)), 0)
    lat3 = lambda s: (latent(t3(s)), 0)
    full = lambda shape: pl.BlockSpec(shape, lambda s: (0,) * len(shape))
    rope_spec = pl.BlockSpec((TILE, 128), lambda s: (t1(s), 0))
    return pl.pallas_call(
        body, name="inproj_bwd", grid=(n_tiles + 2,),
        out_shape=(jax.ShapeDtypeStruct((seq, D_MODEL), F32), jax.ShapeDtypeStruct((D_IN_PROJ, D_MODEL), F32),
                   jax.ShapeDtypeStruct((N_HEADS * D_HEAD_PAD, R_Q), F32), jax.ShapeDtypeStruct((N_HEADS * 256, R_KV), F32),
                   jax.ShapeDtypeStruct((1, R_Q), F32), jax.ShapeDtypeStruct((1, R_KV), F32),
                   jax.ShapeDtypeStruct((1, D_HEAD_PAD), F32), jax.ShapeDtypeStruct((1, D_HEAD_PAD), F32),
                   jax.ShapeDtypeStruct((1, D_MODEL), F32), jax.ShapeDtypeStruct((8, D_MODEL), F32)),
        in_specs=[pl.BlockSpec((TILE, D_MODEL), lambda s: (latent(t2(s)), 0)), pl.BlockSpec((TILE, D_MODEL), lat3),
                  full((TILE, D_MODEL)), full((8, D_MODEL)), full((8, D_MODEL)),
                  full((1, D_MODEL)), full((D_IN_PROJ, D_MODEL)), full((1, R_Q)), full((N_HEADS * D_HEAD_PAD, R_Q)),
                  full((1, R_KV)), full((R_KV, N_HEADS * 256)), full((1, D_HEAD_PAD)), full((1, D_HEAD_PAD)),
                  rope_spec, rope_spec, rope_spec,
                  pl.BlockSpec((TILE, R_Q), lambda s: (t1(s), (O_CQ - N_GATES) // R_Q)),
                  pl.BlockSpec((TILE, R_KV), lambda s: (t1(s), (O_CKV - N_GATES) // R_KV)),
                  pl.BlockSpec((TILE, 128), lambda s: (t1(s), (O_KR - N_GATES) // 128)),
                  pl.BlockSpec((N_HEADS, TILE, D_HEAD_PAD), lambda s: (0, latent(t1(s)), 0)),
                  pl.BlockSpec((N_HEADS, D_HEAD_PAD, TILE), lambda s: (0, 0, t1(s))),
                  pl.BlockSpec((N_HEADS, D_V, TILE), lambda s: (0, 0, t1(s))),
                  pl.BlockSpec((TILE, D_ATTN), lat), pl.BlockSpec((TILE, D_POOL), lat), pl.BlockSpec((TILE, D_POOL), lat),
                  pl.BlockSpec((HALO, D_POOL), lambda s: (jnp.maximum((t1(s) - 1) * rows8 - 1, 0), 0)),
                  pl.BlockSpec((HALO, D_POOL), lambda s: (jnp.minimum(jnp.maximum(t1(s), 1) * rows8, last8), 0)),
                  pl.BlockSpec((TILE, D_MODEL), lat3)],
        out_specs=(pl.BlockSpec((TILE, D_MODEL), lat3), full((D_IN_PROJ, D_MODEL)), full((N_HEADS * D_HEAD_PAD, R_Q)),
                   full((N_HEADS * 256, R_KV)), full((1, R_Q)), full((1, R_KV)), full((1, D_HEAD_PAD)),
                   full((1, D_HEAD_PAD)), full((1, D_MODEL)), full((8, D_MODEL))),
        scratch_shapes=[pltpu.VMEM((TILE, U_PAD), BF16), pltpu.VMEM((TILE, U_PAD), BF16),
                        pltpu.VMEM((TILE, D_MODEL), F32), pltpu.VMEM((TILE, D_MODEL), F32)],
        compiler_params=pltpu.CompilerParams(dimension_semantics=("arbitrary",), vmem_limit_bytes=VMEM_LIMIT),
    )(x, x, ctx, modrows, bmodrows, norm_g, w_in_p, q_lora_g, w_uq_p, kv_lora_g, w_ukv, qng_p, kng_p, cos, slo, shi,
      u, u, u, dq, dk, dv, dga, dgp, dpl, dpl, dpl, g1)


SMALL_LAYOUT = ((0, D_MODEL), (1, R_Q), (2, R_KV), (3, D_HEAD_PAD), (4, D_HEAD_PAD), (5, D_POOL))
ROW_DMOD_B, ROW_DMOD_C, ROW_LOSS = 6, 9, 12


def _epilogue(parts, landed, smalls, dmod4, dgate, loss_acc, w_mod_l):
    n_a, n_l, n_s = len(parts), len(landed), len(smalls)
    n_mod = w_mod_l.shape[1]
    blk = [p.shape[1:] for p in parts]
    small_out = [D_MODEL, R_Q, R_KV, D_HEAD, D_HEAD, D_POOL]

    def body(*refs):
        part_refs = refs[0:n_a]
        land_refs = refs[n_a:n_a + n_l]
        small_in = refs[n_a + n_l:n_a + n_l + n_s]
        dmod4_ref, dgate_ref, loss_ref, wmod_ref = refs[n_a + n_l + n_s:n_a + n_l + n_s + 4]
        outs = refs[n_a + n_l + n_s + 4:]
        red_refs = outs[0:n_a]
        landsum_refs = outs[n_a:n_a + n_l]
        ccg_ref = outs[n_a + n_l]
        sum_refs = outs[n_a + n_l + 1:n_a + n_l + 1 + n_s]
        gbmod_ref, dmy_ref, lossout_ref = outs[n_a + n_l + 1 + n_s:n_a + n_l + 4 + n_s]
        scr = outs[n_a + n_l + 4 + n_s:]
        recv1, own1, sum1b, recv2 = scr[0:n_a], scr[n_a:2 * n_a], scr[2 * n_a:3 * n_a], scr[3 * n_a:4 * n_a]
        small_ref, smallg_ref, tot_ref, cc_ref = scr[4 * n_a:4 * n_a + 4]
        land_vmem = scr[4 * n_a + 4:4 * n_a + 4 + n_l]
        ssem1, rsem1, lsem, ssem2, rsem2, ssem_s, rsem_s, ssem_cc, rsem_cc, land_sem = scr[4 * n_a + 4 + n_l:]
        x, y, c = _coords()
        me = (x, y, c)
        sibling = (x, y, 1 - c)

        small_ref[...] = jnp.zeros_like(small_ref)
        for ref, (row, width) in zip(small_in, SMALL_LAYOUT):
            small_ref[row:row + 1, 0:width] = ref[...]
        small_ref[ROW_DMOD_B:ROW_DMOD_B + 2, :] = dmod4_ref[0:2, :]
        small_ref[ROW_DMOD_B + 2:ROW_DMOD_B + 3, :] = dgate_ref[...]
        small_ref[ROW_DMOD_C:ROW_DMOD_C + 2, :] = dmod4_ref[2:4, :]
        small_ref[ROW_LOSS:ROW_LOSS + 1, 0:128] = loss_ref[0:1, :]
        smallg_ref[_lin(me)] = small_ref[...]
        s_recv, s_send = _gather_to_all(small_ref, smallg_ref, ssem_s, rsem_s)

        stage1, own_copies = [], []
        for a in range(n_a):
            for b in range(4):
                dev = 4 * (b >> 1) + 2 * (b & 1)
                stage1.append(pltpu.make_async_remote_copy(
                    src_ref=part_refs[a].at[dev + (1 - c)], dst_ref=recv1[a].at[b],
                    send_sem=ssem1.at[a, b], recv_sem=rsem1.at[a, b], device_id=sibling, device_id_type=MESH))
                own_copies.append(pltpu.make_async_copy(part_refs[a].at[dev + c], own1[a].at[b], lsem.at[a, b]))
                stage1[-1].start()
                own_copies[-1].start()
        land_copies = [pltpu.make_async_copy(land_refs[n], land_vmem[n], land_sem.at[n]) for n in range(n_l)]
        for cp in land_copies:
            cp.start()

        s_recv()
        tot = smallg_ref[0]
        for d in range(1, N_DEV):
            tot = tot + smallg_ref[d]
        tot_ref[...] = tot
        for ref, (row, _), width in zip(sum_refs, SMALL_LAYOUT, small_out):
            ref[...] = tot_ref[row:row + 1, 0:width]
        lossout_ref[...] = tot_ref[8:16, 0:128]
        lin = _lin(me)

        def my_columns(three_rows):
            v = jnp.concatenate(three_rows, axis=1)
            out = jnp.zeros((1, n_mod), F32)
            for d in range(N_DEV):
                out = out + jnp.where(lin == d, v[:, d * n_mod:(d + 1) * n_mod], 0.0)
            return out

        rows3 = lambda ref, r0: [ref[r0 + t:r0 + t + 1, :] for t in range(3)]
        dmodc = rows3(tot_ref, ROW_DMOD_C)
        gbmod_ref[...] = jnp.concatenate(rows3(tot_ref, ROW_DMOD_B), axis=1) + jnp.concatenate(dmodc, axis=1)
        dmy_ref[...] = jnp.zeros_like(dmy_ref)
        for b in range(N_DEV):
            dmy_ref[b:b + 1, :] = my_columns(rows3(smallg_ref.at[b], ROW_DMOD_B))
        dmy = my_columns(dmodc)
        dmy_ref[N_DEV:N_DEV + 1, :] = dmy
        part = _dot_nt(jnp.broadcast_to(dmy, (8, n_mod)).astype(BF16), wmod_ref[...].astype(BF16))

        cc_ref[...] = part
        ccg_ref[_lin(me)] = part
        cc_recv, cc_send = _gather_to_all(cc_ref, ccg_ref, ssem_cc, rsem_cc)

        stage2 = []
        for a in range(n_a):
            for b in range(4):
                stage1[4 * a + b].wait_recv()
                own_copies[4 * a + b].wait()
                sum1b[a][b] = (own1[a][b] + recv1[a][b]).astype(BF16)
            for k in (1, 2, 3):
                tx, ty = _flip(x, (k >> 1) & 1), _flip(y, k & 1)
                stage2.append(pltpu.make_async_remote_copy(
                    src_ref=sum1b[a].at[2 * tx + ty], dst_ref=recv2[a].at[k - 1], send_sem=ssem2.at[a, k - 1],
                    recv_sem=rsem2.at[a, k - 1], device_id=(tx, ty, c), device_id_type=MESH))
                stage2[-1].start()
        for a in range(n_a):
            own = own1[a][2 * x + y] + recv1[a][2 * x + y]
            for k in (1, 2, 3):
                stage2[3 * a + k - 1].wait_recv()
                own = own + recv2[a][k - 1].astype(F32)
            red_refs[a][...] = own

        for cp in land_copies:
            cp.wait()
        for ref, out in zip(land_vmem, landsum_refs):
            acc = ref[0].astype(F32)
            for d in range(1, N_DEV):
                acc = acc + ref[d].astype(F32)
            out[...] = acc
        cc_recv()
        for cp in stage1 + stage2:
            cp.wait_send()
        s_send()
        cc_send()

    vm = pl.BlockSpec(memory_space=pltpu.VMEM)
    sds = jax.ShapeDtypeStruct
    return pl.pallas_call(
        body, name="epilogue_reduce",
        out_shape=(*[sds(s, F32) for s in blk], *[sds(a.shape[1:], F32) for a in landed], sds((N_DEV, 8, D_MODEL), F32),
                   *[sds((1, w), F32) for w in small_out], sds((1, 3 * D_MODEL), F32), sds((16, n_mod), F32),
                   sds((8, 128), F32)),
        in_specs=[pl.BlockSpec(memory_space=pl.ANY)] * (n_a + n_l) + [vm] * (n_s + 4),
        out_specs=tuple([vm] * (n_a + n_l + n_s + 4)),
        scratch_shapes=[*[pltpu.VMEM((4,) + s, F32) for s in blk], *[pltpu.VMEM((4,) + s, F32) for s in blk],
                        *[pltpu.VMEM((4,) + s, BF16) for s in blk], *[pltpu.VMEM((3,) + s, BF16) for s in blk],
                        pltpu.VMEM((SMALL_ROWS, D_MODEL), F32), pltpu.VMEM((N_DEV, SMALL_ROWS, D_MODEL), F32),
                        pltpu.VMEM((SMALL_ROWS, D_MODEL), F32), pltpu.VMEM((8, D_MODEL), F32),
                        *[pltpu.VMEM(a.shape, a.dtype) for a in landed],
                        pltpu.SemaphoreType.DMA((n_a, 4)), pltpu.SemaphoreType.DMA((n_a, 4)), pltpu.SemaphoreType.DMA((n_a, 4)),
                        pltpu.SemaphoreType.DMA((n_a, 3)), pltpu.SemaphoreType.DMA((n_a, 3)),
                        pltpu.SemaphoreType.DMA((7,)), pltpu.SemaphoreType.DMA((7,)),
                        pltpu.SemaphoreType.DMA((7,)), pltpu.SemaphoreType.DMA((7,)), pltpu.SemaphoreType.DMA((n_l,))],
        compiler_params=pltpu.CompilerParams(vmem_limit_bytes=VMEM_LIMIT),
    )(*parts, *landed, *smalls, dmod4, dgate, loss_acc, w_mod_l)


def _adamw(weights, grads, ms, vs, c_all_t, dmod_my, p2g):
    n = len(weights)

    def body(*refs):
        w_refs = refs[0:n]
        g_in = refs[n:2 * n - 2]
        m_refs = refs[2 * n - 2:3 * n - 2]
        v_refs = refs[3 * n - 2:4 * n - 2]
        cat_ref, dmod_ref, p2g_ref = refs[4 * n - 2:4 * n + 1]
        outs = refs[4 * n + 1:]
        g_refs, d_refs, nm_refs, nv_refs = outs[0:n], outs[n:2 * n], outs[2 * n:3 * n], outs[3 * n:4 * n]
        gcc_ref, gwm_ref = g_refs[0], g_refs[1]

        part = p2g_ref[0, 0:1, :]
        for d in range(1, N_DEV):
            part = part + p2g_ref[d, 0:1, :]
        cc = w_refs[0][...]
        sg = _sigmoid(cc)
        gcc_ref[...] = part * (sg * (1.0 + cc * (1.0 - sg)))
        cat = cat_ref[...]
        gwm_ref[...] = lax.dot_general(cat * _sigmoid(cat), dmod_ref[...], (((1,), (0,)), ((), ())), precision=HIGHEST,
                                       preferred_element_type=F32)
        for idx in range(n):
            if idx >= 2:
                g_refs[idx][...] = g_in[idx - 2][...]
            g = g_refs[idx][...]
            m = ADAM_B1 * m_refs[idx][...] + (1.0 - ADAM_B1) * g
            v = ADAM_B2 * v_refs[idx][...] + (1.0 - ADAM_B2) * (g * g)
            m_hat = m / (1.0 - ADAM_B1 ** ADAM_STEP)
            v_hat = v / (1.0 - ADAM_B2 ** ADAM_STEP)
            d_refs[idx][...] = -ADAM_LR * (m_hat / (jnp.sqrt(v_hat) + ADAM_EPS) + ADAM_WD * w_refs[idx][...])
            nm_refs[idx][...] = m
            nv_refs[idx][...] = v

    vm = pl.BlockSpec(memory_space=pltpu.VMEM)
    shapes = [jax.ShapeDtypeStruct(w.shape, F32) for w in weights]
    args = list(weights) + list(grads[2:]) + list(ms) + list(vs) + [c_all_t, dmod_my, p2g]
    out_shape = tuple(shapes * 4)
    return pl.pallas_call(
        body, name="adamw_update", out_shape=out_shape, in_specs=[vm] * len(args), out_specs=tuple([vm] * len(out_shape)),
        compiler_params=pltpu.CompilerParams(vmem_limit_bytes=VMEM_LIMIT),
    )(*args)


def _rope_tables(seq):
    rows = seq // GRID_W
    row = np.repeat(np.arange(rows, dtype=np.float32), GRID_W)
    col = np.tile(np.arange(GRID_W, dtype=np.float32), rows)
    n_freq = D_ROPE // 4
    inv = (np.float32(ROPE_BASE) ** (-np.arange(n_freq, dtype=np.float32) / np.float32(n_freq))).astype(np.float32)
    ang_r = (row[:, None] * inv).astype(np.float32)
    ang_c = (col[:, None] * inv).astype(np.float32)
    ang = np.concatenate([ang_r, ang_r, ang_c, ang_c], axis=-1)
    cos, sin = np.cos(ang).astype(np.float32), np.sin(ang).astype(np.float32)
    low = (np.arange(D_ROPE) % 32) < 16

    def table(t, fill):
        out = np.full((TILE + seq, 128), fill, np.float32)
        out[TILE:, 0:D_ROPE] = t
        return jnp.asarray(out)

    return table(cos, 1.0), table(np.where(low, -sin, 0.0), 0.0), table(np.where(low, 0.0, sin), 0.0)


def kernel(x, c, ctx, c_ctx, w_mod, b_mod, norm_g, w_in, q_lora_g, w_uq, kv_lora_g, w_ukv, q_norm_g, k_norm_g, w_pool, pool_scale, w_out, loss_target, m_c_ctx, m_w_mod, m_b_mod, m_norm_g, m_w_in, m_q_lora_g, m_w_uq, m_kv_lora_g, m_w_ukv, m_q_norm_g, m_k_norm_g, m_w_pool, m_pool_scale, m_w_out, v_c_ctx, v_w_mod, v_b_mod, v_norm_g, v_w_in, v_q_lora_g, v_w_uq, v_kv_lora_g, v_w_ukv, v_q_norm_g, v_k_norm_g, v_w_pool, v_pool_scale, v_w_out):
    seq = x.shape[1]
    assert ctx.shape[1] == TILE and seq % TILE == 0 and seq % GRID_W == 0
    ix, iy, ic = lax.axis_index("x"), lax.axis_index("y"), lax.axis_index("c")
    me = 4 * ix + 2 * iy + ic
    x2, ctx2, tgt2 = x[0], ctx[0], loss_target[0]
    w_mod_l, w_ukv_l, w_out_l = w_mod[0], w_ukv[0], w_out[0]
    c_ctx_row = c_ctx.reshape(1, D_MODEL)

    tr = lambda a: jnp.transpose(a[0])
    w_in_tl, w_uq_tl = tr(w_in), tr(w_uq)
    cg, modg, wg_in, wg_uq, wg_ukv = _prologue(
        c, c_ctx_row, w_mod_l,
        [w_in_tl, w_uq_tl, w_ukv_l])
    mod_all = jnp.transpose(modg, (1, 0, 2)).reshape(16, 3 * D_MODEL)
    mod_b = lax.dynamic_slice_in_dim(mod_all, me, 1, axis=0).reshape(3, D_MODEL)
    mod_c = mod_all[8].reshape(3, D_MODEL)
    modrows = jnp.concatenate([mod_b, mod_c[0:2], jnp.zeros((3, D_MODEL), F32)], axis=0)
    b3 = b_mod.reshape(3, D_MODEL)
    bmodrows = jnp.concatenate([b3, b3[0:2], jnp.zeros((3, D_MODEL), F32)], axis=0)

    w_in_p = wg_in.reshape(D_IN_PROJ, D_MODEL)
    w_uq_p = jnp.concatenate([wg_uq.reshape(N_HEADS, D_HEAD, R_Q), jnp.zeros((N_HEADS, D_HEAD_PAD - D_HEAD, R_Q), BF16)],
                             axis=1).reshape(N_HEADS * D_HEAD_PAD, R_Q)
    w_ukv_f = jnp.transpose(wg_ukv, (1, 0, 2)).reshape(R_KV, N_HEADS * 256)
    qng_p = jnp.concatenate([q_norm_g, jnp.zeros((1, D_HEAD_PAD - D_HEAD), F32)], axis=1)
    kng_p = jnp.concatenate([k_norm_g, jnp.zeros((1, D_HEAD_PAD - D_HEAD), F32)], axis=1)
    cos, slo, shi = _rope_tables(seq)

    u_gates, u_mla, q, k, v = _inproj_fwd(x2, ctx2, modrows, bmodrows, norm_g, w_in_p, q_lora_g, w_uq_p, kv_lora_g, w_ukv_f,
                             qng_p, kng_p, cos, slo, shi)
    attn, lse, wg_out = _attn_fwd(q, k, v, min(2048, seq), w_out_l.astype(BF16))
    (loss_acc, g1, dgate, d_attn, dga, dgp, dpl, gwo_b, gwp, dps) = _mix(
        x2, tgt2, attn, u_gates, modrows, bmodrows, w_pool[0], pool_scale, wg_out.reshape(D_MODEL, D_MODEL))

    blocks = lambda a: a.reshape((N_DEV, a.shape[0] // N_DEV) + a.shape[1:])
    dq, dk, dv, land_wo, land_wp = _attn_bwd(q, k, v, attn, lse, d_attn.reshape(seq, D_ATTN), min(1024, seq), blocks(gwo_b),
                                             gwp.reshape(4 * GROUP_DIM, GROUP_DIM))
    (grad_x, gwin_t, gwuq_p, gwukv_t, dqlg, dkvlg, dqng, dkng, dng, dmod4) = _inproj_bwd(
        x2, ctx2, modrows, bmodrows, norm_g, w_in_p, q_lora_g, w_uq_p, kv_lora_g, w_ukv_f, qng_p, kng_p, cos, slo, shi,
        u_mla, dq, dk, dv, dga, dgp, dpl, g1)

    gwuq_t = gwuq_p.reshape(N_HEADS, D_HEAD_PAD, R_Q)[:, 0:D_HEAD].reshape(N_HEADS * D_HEAD, R_Q)
    parts = [blocks(gwin_t), blocks(gwuq_t), blocks(gwukv_t)]
    (r_win, r_wuq, r_wukv, r_wout, g_w_pool, ccg, g_ng, g_qlg, g_kvlg, g_qng, g_kng, g_ps, g_bmod, dmod_my,
     loss_rows) = _epilogue(parts, [land_wo, land_wp], [dng, dqlg, dkvlg, dqng, dkng, dps], dmod4, dgate, loss_acc, w_mod_l)

    g_w_ukv = jnp.transpose(r_wukv)
    c_rows = cg[:, 0, :]
    c_all_t = jnp.transpose(jnp.concatenate([c_rows, c_ctx_row, jnp.zeros((16 - N_DEV - 1, D_MODEL), F32)], axis=0))

    weights = [c_ctx_row, w_mod_l, b_mod, norm_g, w_in_tl, q_lora_g, w_uq_tl, kv_lora_g, w_ukv_l, q_norm_g, k_norm_g,
               w_pool.reshape(4 * GROUP_DIM, GROUP_DIM), pool_scale, w_out_l]
    grads = [None, None, g_bmod, g_ng, r_win, g_qlg, r_wuq, g_kvlg, g_w_ukv, g_qng, g_kng, g_w_pool, g_ps, r_wout]
    ms = [m_c_ctx.reshape(1, D_MODEL), m_w_mod[0], m_b_mod, m_norm_g, tr(m_w_in), m_q_lora_g, tr(m_w_uq), m_kv_lora_g,
          m_w_ukv[0], m_q_norm_g, m_k_norm_g, m_w_pool.reshape(4 * GROUP_DIM, GROUP_DIM), m_pool_scale, m_w_out[0]]
    vs = [v_c_ctx.reshape(1, D_MODEL), v_w_mod[0], v_b_mod, v_norm_g, tr(v_w_in), v_q_lora_g, tr(v_w_uq), v_kv_lora_g,
          v_w_ukv[0], v_q_norm_g, v_k_norm_g, v_w_pool.reshape(4 * GROUP_DIM, GROUP_DIM), v_pool_scale, v_w_out[0]]
    outs = _adamw(weights, grads, ms, vs, c_all_t, dmod_my, ccg)
    n = len(weights)
    grads, deltas, new_m, new_v = outs[0:n], outs[n:2 * n], outs[2 * n:3 * n], outs[3 * n:4 * n]

    loss = loss_rows[ROW_LOSS - 8, 0]
    final = [c_ctx.shape, w_mod.shape, b_mod.shape, norm_g.shape, w_in.shape, q_lora_g.shape, w_uq.shape, kv_lora_g.shape,
             w_ukv.shape, q_norm_g.shape, k_norm_g.shape, w_pool.shape, pool_scale.shape, w_out.shape]
    transposed = (4, 6)

    def shaped(arrs):
        return [(jnp.transpose(a) if i in transposed else a).reshape(s) for i, (a, s) in enumerate(zip(arrs, final))]

    return (loss, grad_x[None], *shaped(grads), *shaped(deltas), *shaped(new_m), *shaped(new_v))
```

```python
import numpy as np

import jax
import jax.numpy as jnp
from jax import lax
from jax.experimental import pallas as pl
from jax.experimental.pallas import tpu as pltpu

F32 = jnp.float32
BF16 = jnp.bfloat16
MESH = pl.DeviceIdType.MESH
HIGHEST = lax.Precision.HIGHEST

D_MODEL = 1024
N_HEADS = 4
D_NOPE = 128
D_ROPE = 64
D_HEAD = D_NOPE + D_ROPE
D_HEAD_PAD = 256
D_V = 128
R_Q = 256
R_KV = 128
D_ATTN = 512
D_POOL = 512
POOL_WINDOWS = (2, 4, 8, 16)
GROUP_DIM = 128
GRID_W = 64
ROPE_BASE = 10000.0
NORM_EPS = 1e-6
ATTN_SCALE = D_HEAD ** -0.5
LOG2_E = 1.4426950408889634
LN_2 = 0.6931471805599453
ATTN_ROWS = 256
D_IN_PROJ = 1984
U_PAD = 2048
O_GA, O_PIN, O_GP, O_CQ, O_CKV, O_KR = 0, 512, 1024, 1536, 1792, 1920
TILE = 256
MIX_TILE = 512
Q_BLOCK = 128
HALO = 16
N_GATES = 1536
N_MLA = D_IN_PROJ - N_GATES
N_DEV = 8
VMEM_LIMIT = 56 * 1024 * 1024

ADAM_LR = 0.001
ADAM_B1 = 0.9
ADAM_B2 = 0.999
ADAM_EPS = 1e-08
ADAM_WD = 0.01
ADAM_STEP = 10

SMALL_ROWS = 16


def _dot(a, b):
    return lax.dot_general(a, b, (((1,), (0,)), ((), ())), preferred_element_type=F32)


def _dot_nt(a, b):
    return lax.dot_general(a, b, (((1,), (1,)), ((), ())), preferred_element_type=F32)


def _dot_tn(a, b):
    return lax.dot_general(a, b, (((0,), (0,)), ((), ())), preferred_element_type=F32)


def _sigmoid(x):
    return 1.0 / (1.0 + jnp.exp(-x))


def _rope(p, cos, slo, shi):
    return p * cos + pltpu.roll(p, 112, 1) * slo + pltpu.roll(p, 16, 1) * shi


def _rope_t(d, cos, slo, shi):
    return d * cos + pltpu.roll(d * slo, 16, 1) + pltpu.roll(d * shi, 112, 1)


def _shift_rows(a, k):
    n = a.shape[0]
    return pltpu.roll(a, (-k) % n, 0)


def _window_sum(x, w, transposed):
    s = (x + _shift_rows(x, 1)) if transposed else (_shift_rows(x, -1) + x)
    step = 1
    while 2 * step < w:
        s = _shift_rows(s, -step) + _shift_rows(s, step)
        step *= 2
    return s


def _inv_count(tpos, w, seq):
    lo = jnp.maximum(tpos - w // 2, 0)
    hi = jnp.minimum(tpos - w // 2 + w, seq)
    return 1.0 / jnp.maximum(hi - lo, 1).astype(F32)


def _coords():
    return lax.axis_index("x"), lax.axis_index("y"), lax.axis_index("c")


def _flip(v, bit):
    return (1 - v) if bit else v


def _peer(k):
    x, y, c = _coords()
    return (_flip(x, (k >> 2) & 1), _flip(y, (k >> 1) & 1), _flip(c, k & 1))


def _lin(p):
    return 4 * p[0] + 2 * p[1] + p[2]


def _gather_to_all(src_ref, slots_ref, send_sems, recv_sems):
    me = _coords()
    copies = []
    for k in range(1, N_DEV):
        cp = pltpu.make_async_remote_copy(
            src_ref=src_ref, dst_ref=slots_ref.at[_lin(me)], send_sem=send_sems.at[k - 1], recv_sem=recv_sems.at[k - 1],
            device_id=_peer(k), device_id_type=MESH)
        cp.start()
        copies.append(cp)

    def wait_recv():
        for k in range(1, N_DEV):
            pltpu.make_async_remote_copy(
                src_ref=src_ref, dst_ref=slots_ref.at[_lin(_peer(k))], send_sem=send_sems.at[k - 1],
                recv_sem=recv_sems.at[k - 1], device_id=_peer(k), device_id_type=MESH).wait_recv()

    def wait_send():
        for cp in copies:
            cp.wait_send()

    return wait_recv, wait_send


def _prologue(c8, cctx8, w_mod_l, shards):
    n_mod = w_mod_l.shape[1]
    n_w = len(shards)

    def body(c_ref, cctx_ref, wmod_ref, *refs):
        w_refs = refs[0:n_w]
        cg_ref, modg_ref = refs[n_w], refs[n_w + 1]
        wg_refs = refs[n_w + 2:2 * n_w + 2]
        modblk_ref = refs[2 * n_w + 2]
        wb_refs = refs[2 * n_w + 3:3 * n_w + 3]
        c8_ref = refs[3 * n_w + 3]
        ssem_w, rsem_w, lsem_w, ssem_c, rsem_c, ssem_m, rsem_m = refs[3 * n_w + 4:]
        x, y, c = _coords()
        me = (x, y, c)
        sibling = (x, y, 1 - c)
        chips = [(1 - x, y), (x, 1 - y), (1 - x, 1 - y)]

        def wcopies(k, block, to, own=False):
            out = []
            for a in range(n_w):
                slot = wg_refs[a].at[_lin(block)]
                out.append(pltpu.make_async_remote_copy(
                    src_ref=wb_refs[a] if own else slot, dst_ref=slot, send_sem=ssem_w.at[a, k], recv_sem=rsem_w.at[a, k],
                    device_id=to, device_id_type=MESH))
            return out

        c8_ref[...] = jnp.broadcast_to(c_ref[...], (8, D_MODEL))
        cg_ref[_lin(me)] = c8_ref[...]
        c_recv, c_send = _gather_to_all(c8_ref, cg_ref, ssem_c, rsem_c)

        for a in range(n_w):
            wb_refs[a][...] = w_refs[a][...].astype(BF16)
        first = wcopies(0, me, sibling, own=True)
        for j, chip in enumerate(chips):
            first += wcopies(1 + j, me, (*chip, c), own=True)
        late = [idx for idx in range(len(first)) if idx % n_w == 0 and idx >= n_w]
        for idx, cp in enumerate(first):
            if idx not in late:
                cp.start()
        mine = [pltpu.make_async_copy(wb_refs[a], wg_refs[a].at[_lin(me)], lsem_w.at[a]) for a in range(n_w)]
        for cp in mine:
            cp.start()

        c_recv()
        row = lax.broadcasted_iota(jnp.int32, (8, D_MODEL), 0)
        c_all = jnp.zeros((8, D_MODEL), F32)
        for d in range(N_DEV):
            c_all = c_all + jnp.where(row == d, cg_ref[d], 0.0)
        cc = jnp.broadcast_to(cctx_ref[...], (8, D_MODEL))
        a = jnp.concatenate([c_all * _sigmoid(c_all), cc * _sigmoid(cc)], axis=0)
        modblk_ref[...] = _dot(a.astype(BF16), wmod_ref[...].astype(BF16))
        modg_ref[_lin(me)] = modblk_ref[...]
        m_recv, m_send = _gather_to_all(modblk_ref, modg_ref, ssem_m, rsem_m)
        for idx in late:
            first[idx].start()
        m_recv()

        passed = []
        for j, chip in enumerate(chips):
            for cp in wcopies(1 + j, (*chip, c), me):
                cp.wait_recv()
            fwd = wcopies(4 + j, (*chip, c), sibling)
            for cp in fwd:
                cp.start()
            passed += fwd
        for cp in wcopies(0, sibling, me):
            cp.wait_recv()
        for j, chip in enumerate(chips):
            for cp in wcopies(4 + j, (*chip, 1 - c), me):
                cp.wait_recv()
        for cp in first + passed:
            cp.wait_send()
        for cp in mine:
            cp.wait()
        c_send()
        m_send()

    vm = pl.BlockSpec(memory_space=pltpu.VMEM)
    hbm = pl.BlockSpec(memory_space=pl.ANY)
    return pl.pallas_call(
        body, name="prologue_gather",
        out_shape=(jax.ShapeDtypeStruct((N_DEV, 8, D_MODEL), F32), jax.ShapeDtypeStruct((N_DEV, 16, n_mod), F32),
                   *[jax.ShapeDtypeStruct((N_DEV,) + s.shape, BF16) for s in shards]),
        in_specs=[vm] * (3 + n_w), out_specs=(vm, vm, *[hbm] * n_w),
        scratch_shapes=[pltpu.VMEM((16, n_mod), F32), *[pltpu.VMEM(s.shape, BF16) for s in shards],
                        pltpu.VMEM((8, D_MODEL), F32),
                        pltpu.SemaphoreType.DMA((n_w, 7)), pltpu.SemaphoreType.DMA((n_w, 7)), pltpu.SemaphoreType.DMA((n_w,)),
                        pltpu.SemaphoreType.DMA((7,)), pltpu.SemaphoreType.DMA((7,)),
                        pltpu.SemaphoreType.DMA((7,)), pltpu.SemaphoreType.DMA((7,))],
        compiler_params=pltpu.CompilerParams(vmem_limit_bytes=VMEM_LIMIT),
    )(c8, cctx8, w_mod_l, *shards)


def _modulated_input(is_ctx, x_ref, ctx_ref, mod_ref, bmod_ref, ng_ref):
    xt = jnp.where(is_ctx, ctx_ref[...], x_ref[...])
    shift = jnp.where(is_ctx, mod_ref[3:4, :] + bmod_ref[3:4, :], mod_ref[0:1, :] + bmod_ref[0:1, :])
    scale = jnp.where(is_ctx, mod_ref[4:5, :] + bmod_ref[4:5, :], mod_ref[1:2, :] + bmod_ref[1:2, :])
    r = lax.rsqrt(jnp.mean(xt * xt, axis=-1, keepdims=True) + NORM_EPS)
    xg = (xt * r) * ng_ref[...]
    h = xg * (1.0 + scale) + shift
    return xt, r, xg, h, scale


def _inproj_fwd(x, ctx, modrows, bmodrows, norm_g, w_in_p, q_lora_g, w_uq_p, kv_lora_g, w_ukv, qng_p, kng_p, cos, slo, shi):
    seq = x.shape[0]
    n_tiles = seq // TILE + 1
    tot = seq + TILE

    n_mla = R_Q + R_KV + 128

    def body(x_ref, ctx_ref, mod_ref, bmod_ref, ng_ref, win_ref, qlg_ref, wuq_ref, kvlg_ref, wukv_ref, qng_ref, kng_ref,
             cos_ref, slo_ref, shi_ref, ug_ref, um_ref, q_ref, k_ref, v_ref, stash):
        s = pl.program_id(0)

        @pl.when(s == 0)
        def _():
            stash[...] = jnp.zeros_like(stash)

        refs = (x_ref, ctx_ref, mod_ref, bmod_ref, ng_ref, win_ref, qlg_ref, wuq_ref, kvlg_ref, wukv_ref, qng_ref, kng_ref,
                cos_ref, slo_ref, shi_ref, ug_ref, um_ref, q_ref, k_ref, v_ref)
        stages(s, refs, stash, stash)

    def stages(s, refs, fill, prev_ref):
        (x_ref, ctx_ref, mod_ref, bmod_ref, ng_ref, win_ref, qlg_ref, wuq_ref, kvlg_ref, wukv_ref, qng_ref, kng_ref,
         cos_ref, slo_ref, shi_ref, ug_ref, um_ref, q_ref, k_ref, v_ref) = refs
        cos_t, slo_t, shi_t = cos_ref[...], slo_ref[...], shi_ref[...]
        prev = prev_ref[...]
        cq = prev[:, 0:R_Q]
        qn = cq * lax.rsqrt(jnp.mean(cq * cq, axis=-1, keepdims=True) + NORM_EPS) * qlg_ref[...]
        q = _dot_nt(qn.astype(BF16), wuq_ref[...])
        qg = qng_ref[...] * (ATTN_SCALE * LOG2_E)
        for hd in range(N_HEADS):
            qh = q[:, hd * D_HEAD_PAD:(hd + 1) * D_HEAD_PAD]
            rr = lax.rsqrt(jnp.sum(qh * qh, axis=-1, keepdims=True) * (1.0 / D_HEAD) + NORM_EPS)
            qy = qh * rr * qg
            q_ref[hd, :, 0:D_NOPE] = qy[:, 0:D_NOPE].astype(BF16)
            q_ref[hd, :, D_NOPE:D_HEAD_PAD] = _rope(qy[:, D_NOPE:D_HEAD_PAD], cos_t, slo_t, shi_t).astype(BF16)

        ckv = prev[:, R_Q:R_Q + R_KV]
        kvn = ckv * lax.rsqrt(jnp.mean(ckv * ckv, axis=-1, keepdims=True) + NORM_EPS) * kvlg_ref[...]
        kv = _dot(kvn.astype(BF16), wukv_ref[...])
        krz = prev[:, R_Q + R_KV:n_mla]
        kr_ss = jnp.sum(krz * krz, axis=-1, keepdims=True)
        kg = kng_ref[...]
        for hd in range(N_HEADS):
            kn = kv[:, hd * 256:hd * 256 + D_NOPE]
            rr = lax.rsqrt((jnp.sum(kn * kn, axis=-1, keepdims=True) + kr_ss) * (1.0 / D_HEAD) + NORM_EPS)
            k_ref[hd, :, 0:D_NOPE] = (kn * rr * kg[:, 0:D_NOPE]).astype(BF16)
            k_ref[hd, :, D_NOPE:D_HEAD_PAD] = _rope(krz * rr * kg[:, D_NOPE:D_HEAD_PAD], cos_t, slo_t, shi_t).astype(BF16)
            v_ref[hd] = kv[:, hd * 256 + D_NOPE:(hd + 1) * 256].astype(BF16)

        _, _, _, h, _ = _modulated_input(s == 0, x_ref, ctx_ref, mod_ref, bmod_ref, ng_ref)
        hb = h.astype(BF16)
        ug_ref[...] = _dot_nt(hb, win_ref[N_MLA:D_IN_PROJ, :]).astype(BF16)
        um = _dot_nt(hb, win_ref[0:n_mla, :])
        lane = lax.broadcasted_iota(jnp.int32, (TILE, 128), 1)
        um = jnp.concatenate([um[:, 0:R_Q + R_KV], jnp.where(lane < D_ROPE, um[:, R_Q + R_KV:n_mla], 0.0)], axis=1)
        um_ref[...] = um
        fill[...] = um

    first = lambda s: jnp.minimum(s, n_tiles - 1)
    second = lambda s: jnp.maximum(s - 1, 0)
    latent = lambda t: jnp.maximum(t - 1, 0)
    full = lambda shape: pl.BlockSpec(shape, lambda s: (0,) * len(shape))
    rope_spec = pl.BlockSpec((TILE, 128), lambda s: (second(s), 0))
    return pl.pallas_call(
        body, name="inproj_fwd", grid=(n_tiles + 1,),
        out_shape=(jax.ShapeDtypeStruct((seq, N_GATES), BF16), jax.ShapeDtypeStruct((tot, n_mla), F32),
                   jax.ShapeDtypeStruct((N_HEADS, seq, D_HEAD_PAD), BF16),
                   jax.ShapeDtypeStruct((N_HEADS, tot, D_HEAD_PAD), BF16),
                   jax.ShapeDtypeStruct((N_HEADS, tot, D_V), BF16)),
        in_specs=[pl.BlockSpec((TILE, D_MODEL), lambda s: (latent(first(s)), 0)), full((TILE, D_MODEL)), full((8, D_MODEL)),
                  full((8, D_MODEL)), full((1, D_MODEL)), full((D_IN_PROJ, D_MODEL)), full((1, R_Q)),
                  full((N_HEADS * D_HEAD_PAD, R_Q)), full((1, R_KV)), full((R_KV, N_HEADS * 256)), full((1, D_HEAD_PAD)),
                  full((1, D_HEAD_PAD)), rope_spec, rope_spec, rope_spec],
        out_specs=(pl.BlockSpec((TILE, N_GATES), lambda s: (latent(first(s)), 0)),
                   pl.BlockSpec((TILE, n_mla), lambda s: (first(s), 0)),
                   pl.BlockSpec((N_HEADS, TILE, D_HEAD_PAD), lambda s: (0, latent(second(s)), 0)),
                   pl.BlockSpec((N_HEADS, TILE, D_HEAD_PAD), lambda s: (0, second(s), 0)),
                   pl.BlockSpec((N_HEADS, TILE, D_V), lambda s: (0, second(s), 0))),
        scratch_shapes=[pltpu.VMEM((TILE, n_mla), F32)],
        compiler_params=pltpu.CompilerParams(dimension_semantics=("arbitrary",), vmem_limit_bytes=VMEM_LIMIT),
    )(x, ctx, modrows, bmodrows, norm_g, w_in_p, q_lora_g, w_uq_p, kv_lora_g, w_ukv, qng_p, kng_p, cos, slo, shi)


def _hosted_exchange(first, last, items, send_sems, recv_sems, local_sems):
    me = _lin(_coords())

    def remote(n, k, src, land, scatter):
        peer = _peer(k)
        return pltpu.make_async_remote_copy(
            src_ref=src.at[_lin(peer)] if scatter else src, dst_ref=land.at[me], send_sem=send_sems.at[n, k - 1],
            recv_sem=recv_sems.at[n, k - 1], device_id=peer, device_id_type=MESH)

    def local(n, src, land, scatter):
        return pltpu.make_async_copy(src.at[me] if scatter else src, land.at[me], local_sems.at[n])

    @pl.when(first)
    def _():
        for n, (src, land, scatter) in enumerate(items):
            for k in range(1, N_DEV):
                remote(n, k, src, land, scatter).start()
            local(n, src, land, scatter).start()

    @pl.when(last)
    def _():
        for n, (src, land, scatter) in enumerate(items):
            for k in range(1, N_DEV):
                peer = _peer(k)
                pltpu.make_async_remote_copy(
                    src_ref=src.at[0] if scatter else src, dst_ref=land.at[_lin(peer)], send_sem=send_sems.at[n, k - 1],
                    recv_sem=recv_sems.at[n, k - 1], device_id=peer, device_id_type=MESH).wait_recv()
            for k in range(1, N_DEV):
                remote(n, k, src, land, scatter).wait_send()
            local(n, src, land, scatter).wait()


def _attn_fwd(q, k, v, block_q, w_out_b):
    _, seq, _ = q.shape
    n_keys = k.shape[1]
    n_q = seq // block_q

    def body(q_ref, k_ref, v_ref, wo_ref, o_ref, lse_ref, wog_ref, ssem, rsem, lsem):
        h, i = pl.program_id(0), pl.program_id(1)
        _hosted_exchange((h == 0) & (i == 0), (h == N_HEADS - 1) & (i == n_q - 1), [(wo_ref, wog_ref, False)],
                         ssem, rsem, lsem)
        kb, vb = k_ref[0], v_ref[0]
        for r0 in range(0, block_q, ATTN_ROWS):
            rows = slice(r0, r0 + ATTN_ROWS)
            s = _dot_nt(q_ref[0, rows, :], kb)
            m = jnp.max(s, axis=-1, keepdims=True)
            p = jnp.exp2(s - m)
            l = jnp.sum(p, axis=-1, keepdims=True)
            o_ref[rows, :] = _dot(p.astype(BF16), vb) * (1.0 / l)
            lse_ref[0, rows, :] = m + jnp.log2(l)

    hbm = pl.BlockSpec(memory_space=pl.ANY)
    return pl.pallas_call(
        body, name="attn_fwd", grid=(N_HEADS, n_q),
        out_shape=(jax.ShapeDtypeStruct((seq, D_ATTN), F32), jax.ShapeDtypeStruct((N_HEADS, seq, 1), F32),
                   jax.ShapeDtypeStruct((N_DEV,) + w_out_b.shape, w_out_b.dtype)),
        in_specs=[pl.BlockSpec((1, block_q, D_HEAD_PAD), lambda h, i: (h, i, 0)),
                  pl.BlockSpec((1, n_keys, D_HEAD_PAD), lambda h, i: (h, 0, 0)),
                  pl.BlockSpec((1, n_keys, D_V), lambda h, i: (h, 0, 0)), hbm],
        out_specs=(pl.BlockSpec((block_q, D_V), lambda h, i: (i, h)),
                   pl.BlockSpec((1, block_q, 1), lambda h, i: (h, i, 0)), hbm),
        scratch_shapes=[pltpu.SemaphoreType.DMA((1, 7)), pltpu.SemaphoreType.DMA((1, 7)), pltpu.SemaphoreType.DMA((1,))],
        compiler_params=pltpu.CompilerParams(dimension_semantics=("arbitrary", "arbitrary"), vmem_limit_bytes=VMEM_LIMIT),
    )(q, k, v, w_out_b)


def _attn_bwd(q, k, v, o, lse, d_o, block_q, gwo_blocks, gwp):
    _, seq, _ = q.shape
    n_keys = k.shape[1]
    n_q = seq // block_q

    def body(q_ref, k_ref, v_ref, o_ref, lse_ref, do_ref, gwo_ref, gwp_ref, dq_ref, dkt_ref, dvt_ref, lwo_ref, lwp_ref,
             ssem, rsem, lsem):
        h, i = pl.program_id(0), pl.program_id(1)
        _hosted_exchange((h == 0) & (i == 0), (h == N_HEADS - 1) & (i == n_q - 1),
                         [(gwo_ref, lwo_ref, True), (gwp_ref, lwp_ref, False)], ssem, rsem, lsem)

        @pl.when(i == 0)
        def _():
            dkt_ref[...] = jnp.zeros_like(dkt_ref)
            dvt_ref[...] = jnp.zeros_like(dvt_ref)

        kb, vb = k_ref[0], v_ref[0]
        raws, ps = [], []
        for r0 in range(0, block_q, ATTN_ROWS):
            rows = slice(r0, r0 + ATTN_ROWS)
            d_out = do_ref[rows, :]
            p = jnp.exp2(_dot_nt(q_ref[0, rows, :], kb) - lse_ref[0, rows, :])
            dp = _dot_nt(d_out.astype(BF16), vb)
            delta = jnp.sum(d_out * o_ref[rows, :], axis=-1, keepdims=True)
            raw = (p * (dp - delta)).astype(BF16)
            dq_ref[0, rows, :] = _dot(raw, kb)
            raws.append(raw)
            ps.append(p.astype(BF16))
        dkt_ref[0] += _dot_tn(q_ref[0], jnp.concatenate(raws, axis=0))
        dvt_ref[0] += _dot_tn(do_ref[...].astype(BF16), jnp.concatenate(ps, axis=0))

    hbm = pl.BlockSpec(memory_space=pl.ANY)
    return pl.pallas_call(
        body, name="attn_bwd", grid=(N_HEADS, n_q),
        out_shape=(jax.ShapeDtypeStruct((N_HEADS, seq, D_HEAD_PAD), F32),
                   jax.ShapeDtypeStruct((N_HEADS, D_HEAD_PAD, n_keys), F32),
                   jax.ShapeDtypeStruct((N_HEADS, D_V, n_keys), F32),
                   jax.ShapeDtypeStruct(gwo_blocks.shape, gwo_blocks.dtype),
                   jax.ShapeDtypeStruct((N_DEV,) + gwp.shape, gwp.dtype)),
        in_specs=[pl.BlockSpec((1, block_q, D_HEAD_PAD), lambda h, i: (h, i, 0)),
                  pl.BlockSpec((1, n_keys, D_HEAD_PAD), lambda h, i: (h, 0, 0)),
                  pl.BlockSpec((1, n_keys, D_V), lambda h, i: (h, 0, 0)),
                  pl.BlockSpec((block_q, D_V), lambda h, i: (i, h)),
                  pl.BlockSpec((1, block_q, 1), lambda h, i: (h, i, 0)),
                  pl.BlockSpec((block_q, D_V), lambda h, i: (i, h)), hbm, hbm],
        out_specs=(pl.BlockSpec((1, block_q, D_HEAD_PAD), lambda h, i: (h, i, 0)),
                   pl.BlockSpec((1, D_HEAD_PAD, n_keys), lambda h, i: (h, 0, 0)),
                   pl.BlockSpec((1, D_V, n_keys), lambda h, i: (h, 0, 0)), hbm, hbm),
        scratch_shapes=[pltpu.SemaphoreType.DMA((2, 7)), pltpu.SemaphoreType.DMA((2, 7)), pltpu.SemaphoreType.DMA((2,))],
        compiler_params=pltpu.CompilerParams(dimension_semantics=("arbitrary", "arbitrary"), vmem_limit_bytes=VMEM_LIMIT),
    )(q, k, v, o, lse, d_o, gwo_blocks, gwp)


def _mix(x, target, attn, u, modrows, bmodrows, w_pool, pool_scale, w_out):
    seq = x.shape[0]
    rows = min(MIX_TILE, seq // 2)
    n_tiles = seq // rows
    rows8 = rows // HALO
    last8 = seq // HALO - 1

    n_blk = seq // Q_BLOCK
    per = rows // n_blk
    assert rows % n_blk == 0 and per % 8 == 0 and n_tiles >= 2
    n_stash = 8

    def body(x_ref, t_ref, o_hbm, ga_ref, pin_ref, hb_ref, ha_ref, gp_ref, mod_ref, bmod_ref, wp_ref, ps_ref, wo_ref,
             loss_ref, g1_ref, dgate_ref, do_hbm, dga_ref, dgp_ref, dpl_ref, gwob_ref, gwp_ref, dps_ref,
             abuf, dbuf, gwo_ref, *rest):
        stash_even, stash_odd = rest[0:n_stash], rest[n_stash:2 * n_stash]
        rsem, wsem = rest[2 * n_stash:]
        s = pl.program_id(0)

        def slab_copies(tile, slot, fetch):
            window = pl.ds(tile * per, per)
            if fetch:
                return [pltpu.make_async_copy(o_hbm.at[:, window, :], abuf.at[slot], rsem.at[slot])]
            return [pltpu.make_async_copy(dbuf.at[slot], do_hbm.at[:, window, :], wsem.at[slot])]

        def wait_all(slot, fetch):
            slab_copies(0, slot, fetch)[0].wait()

        a_slot = lax.rem(s, 2)
        b_slot = 1 - a_slot

        @pl.when(s == 0)
        def _():
            loss_ref[...] = jnp.zeros_like(loss_ref)
            dgate_ref[...] = jnp.zeros_like(dgate_ref)
            gwo_ref[...] = jnp.zeros_like(gwo_ref)
            gwp_ref[...] = jnp.zeros_like(gwp_ref)
            dps_ref[...] = jnp.zeros_like(dps_ref)
            for cp in slab_copies(0, 0, True):
                cp.start()

        @pl.when(s + 1 < n_tiles)
        def _():
            for cp in slab_copies(s + 1, b_slot, True):
                cp.start()

        @pl.when(s < n_tiles)
        def _():
            wait_all(a_slot, True)

        @pl.when(s >= 3)
        def _():
            wait_all(b_slot, False)

        gate = mod_ref[2:3, :] + bmod_ref[2:3, :]
        ps = ps_ref[...]

        def a_vector(slot):
            attn_t = jnp.swapaxes(abuf[slot], 0, 1).reshape(rows, D_ATTN)
            ga = ga_ref[...].astype(F32)
            sga = _sigmoid(ga)
            silu_ga = ga * sga
            pin = pin_ref[...].astype(F32)
            xs = jnp.concatenate([jnp.where(s > 0, hb_ref[...].astype(F32), 0.0), pin,
                                  jnp.where(s < n_tiles - 1, ha_ref[...].astype(F32), 0.0)], axis=0)
            tpos = s * rows + lax.broadcasted_iota(jnp.int32, (rows, 1), 0)
            pooled = []
            for g, w in enumerate(POOL_WINDOWS):
                sl = slice(g * GROUP_DIM, (g + 1) * GROUP_DIM)
                ws = _window_sum(xs[:, sl], w, False)[HALO:HALO + rows]
                pooled.append((ws * _inv_count(tpos, w, seq) - pin[:, sl]).astype(BF16))
            return attn_t, ga, sga, silu_ga, pooled

        def a_group_maps(pooled):
            return jnp.concatenate([_dot(pooled[g], wp_ref[g].astype(BF16)) for g in range(len(POOL_WINDOWS))], axis=1)

        def a_project(stash, yp, attn_t, ga, sga, silu_ga, pooled):
            zb_ref, _, fa_ref, sa_ref, fp_ref, sp_ref, yp_ref, pooled_ref = stash
            ypool = yp * ps
            gp = gp_ref[...].astype(F32)
            sgp = _sigmoid(gp)
            silu_gp = gp * sgp
            z = jnp.concatenate([silu_ga * attn_t, silu_gp * ypool], axis=1).astype(BF16)
            y = _dot(z, wo_ref[...])
            zb_ref[...] = z
            fa_ref[...] = attn_t * (sga * (1.0 + ga * (1.0 - sga)))
            sa_ref[...] = silu_ga
            fp_ref[...] = ypool * (sgp * (1.0 + gp * (1.0 - sgp)))
            sp_ref[...] = silu_gp
            yp_ref[...] = yp
            pooled_ref[...] = jnp.concatenate(pooled, axis=1)
            return y

        def a_loss(stash, y):
            res = x_ref[...] + gate * y - t_ref[...]
            loss_ref[...] += jnp.sum(res * res) * (0.5 / D_MODEL)
            dxn = res * (1.0 / D_MODEL)
            g1_ref[...] = dxn
            dgate_ref[...] += jnp.sum(dxn * y, axis=0, keepdims=True)
            stash[1][...] = (gate * dxn).astype(BF16)

        def b_dz(stash):
            return _dot_nt(stash[1][...], wo_ref[...])

        def b_gwo(stash):
            gwo_ref[...] += _dot_tn(stash[0][...], stash[1][...])

        def b_vector(stash, slot, dz):
            _, _, fa_ref, sa_ref, fp_ref, sp_ref, yp_ref, _ = stash
            dbra = dz[:, 0:D_ATTN]
            dbrp = dz[:, D_ATTN:]
            dga_ref[...] = (dbra * fa_ref[...]).astype(BF16)
            dbuf[slot] = jnp.swapaxes((dbra * sa_ref[...]).reshape(per, n_blk, D_ATTN), 0, 1)
            dgp_ref[...] = (dbrp * fp_ref[...]).astype(BF16)
            dyp_s = dbrp * sp_ref[...]
            dps_ref[...] += jnp.sum(dyp_s * yp_ref[...], axis=0, keepdims=True)
            return (dyp_s * ps).astype(BF16)

        def b_small(stash, dyp):
            pooled_ref = stash[7]
            for g in range(len(POOL_WINDOWS)):
                sl = slice(g * GROUP_DIM, (g + 1) * GROUP_DIM)
                gwp_ref[g] += _dot_tn(pooled_ref[:, sl], dyp[:, sl])
                dpl_ref[:, sl] = _dot_nt(dyp[:, sl], wp_ref[g].astype(BF16)).astype(BF16)

        def both(a_stash, b_stash, slot_a):
            dz = b_dz(b_stash)
            front = a_vector(slot_a)
            yp = a_group_maps(front[-1])
            b_gwo(b_stash)
            y = a_project(a_stash, yp, *front)
            dyp = b_vector(b_stash, 1 - slot_a, dz)
            a_loss(a_stash, y)
            b_small(b_stash, dyp)

        @pl.when(s == 0)
        def _():
            front = a_vector(0)
            a_loss(stash_even, a_project(stash_even, a_group_maps(front[-1]), *front))

        @pl.when((s > 0) & (s < n_tiles) & (a_slot == 0))
        def _():
            both(stash_even, stash_odd, 0)

        @pl.when((s > 0) & (s < n_tiles) & (a_slot == 1))
        def _():
            both(stash_odd, stash_even, 1)

        @pl.when(s == n_tiles)
        def _():
            last = (n_tiles - 1) % 2
            stash = stash_odd if last else stash_even
            dz = b_dz(stash)
            b_gwo(stash)
            b_small(stash, b_vector(stash, last, dz))
            gwob_ref[...] = gwo_ref[...].astype(BF16)

        @pl.when(s >= 1)
        def _():
            for cp in slab_copies(s - 1, b_slot, False):
                cp.start()

        @pl.when(s == n_tiles)
        def _():
            wait_all(b_slot, False)
            wait_all(a_slot, False)

    ta = lambda s: jnp.minimum(s, n_tiles - 1)
    tb = lambda s: jnp.maximum(s - 1, 0)
    tile = lambda w: pl.BlockSpec((rows, w), lambda s: (ta(s), 0))
    tile_b = lambda w: pl.BlockSpec((rows, w), lambda s: (tb(s), 0))
    ucol = lambda col: pl.BlockSpec((rows, 512), lambda s: (ta(s), col))
    full = lambda shape: pl.BlockSpec(shape, lambda s: (0,) * len(shape))
    stash_shapes = [pltpu.VMEM((rows, D_MODEL), BF16), pltpu.VMEM((rows, D_MODEL), BF16)] + \
        [pltpu.VMEM((rows, 512), F32)] * 5 + [pltpu.VMEM((rows, D_POOL), BF16)]
    return pl.pallas_call(
        body, name="mix_fwd_bwd", grid=(n_tiles + 1,),
        out_shape=(jax.ShapeDtypeStruct((8, 128), F32), jax.ShapeDtypeStruct((seq, D_MODEL), F32),
                   jax.ShapeDtypeStruct((1, D_MODEL), F32), jax.ShapeDtypeStruct((n_blk, Q_BLOCK, D_ATTN), F32),
                   jax.ShapeDtypeStruct((seq, D_ATTN), BF16), jax.ShapeDtypeStruct((seq, D_POOL), BF16),
                   jax.ShapeDtypeStruct((seq, D_POOL), BF16), jax.ShapeDtypeStruct((D_MODEL, D_MODEL), BF16),
                   jax.ShapeDtypeStruct((4, GROUP_DIM, GROUP_DIM), F32), jax.ShapeDtypeStruct((1, D_POOL), F32)),
        in_specs=[tile(D_MODEL), tile(D_MODEL), pl.BlockSpec(memory_space=pl.ANY), ucol(0), ucol(1),
                  pl.BlockSpec((HALO, 512), lambda s: (jnp.maximum(ta(s) * rows8 - 1, 0), 1)),
                  pl.BlockSpec((HALO, 512), lambda s: (jnp.minimum((ta(s) + 1) * rows8, last8), 1)),
                  ucol(2), full((8, D_MODEL)), full((8, D_MODEL)), full((4, GROUP_DIM, GROUP_DIM)), full((1, D_POOL)),
                  full((D_MODEL, D_MODEL))],
        out_specs=(full((8, 128)), tile(D_MODEL), full((1, D_MODEL)), pl.BlockSpec(memory_space=pl.ANY), tile_b(D_ATTN),
                   tile_b(D_POOL), tile_b(D_POOL), full((D_MODEL, D_MODEL)), full((4, GROUP_DIM, GROUP_DIM)),
                   full((1, D_POOL))),
        scratch_shapes=[pltpu.VMEM((2, n_blk, per, D_ATTN), F32), pltpu.VMEM((2, n_blk, per, D_ATTN), F32),
                        pltpu.VMEM((D_MODEL, D_MODEL), F32), *stash_shapes, *stash_shapes,
                        pltpu.SemaphoreType.DMA((2,)), pltpu.SemaphoreType.DMA((2,))],
        compiler_params=pltpu.CompilerParams(dimension_semantics=("arbitrary",), vmem_limit_bytes=VMEM_LIMIT),
    )(x, target, attn.reshape(n_blk, Q_BLOCK, D_ATTN), u, u, u, u, u, modrows, bmodrows, w_pool, pool_scale, w_out)


def _inproj_bwd(x, ctx, modrows, bmodrows, norm_g, w_in_p, q_lora_g, w_uq_p, kv_lora_g, w_ukv, qng_p, kng_p, cos, slo, shi,
                u, dq, dk, dv, dga, dgp, dpl, g1):
    seq = x.shape[0]
    n_tiles = seq // TILE + 1
    rows8 = TILE // HALO
    last8 = seq // HALO - 1

    def body(*refs):
        du_even, du_odd, dh_even, dh_odd = refs[-4:]
        gwin_ref, gwuq_ref, gwukv_ref, dqlg_ref, dkvlg_ref, dqng_ref, dkng_ref, dng_ref, dmod_ref = refs[-13:-4]
        s = pl.program_id(0)
        even = lax.rem(s, 2) == 0

        @pl.when(s == 0)
        def _():
            for ref in (gwin_ref, gwuq_ref, gwukv_ref, dqlg_ref, dkvlg_ref, dqng_ref, dkng_ref, dng_ref, dmod_ref,
                        du_odd, dh_even):
                ref[...] = jnp.zeros_like(ref)

        @pl.when((s < n_tiles) & even)
        def _():
            stages(s, refs[:-4], du_even, du_odd, dh_odd, dh_even, (True, True, True))

        @pl.when((s < n_tiles) & jnp.logical_not(even))
        def _():
            stages(s, refs[:-4], du_odd, du_even, dh_even, dh_odd, (True, True, True))

        for tail, run in ((n_tiles, (False, True, True)), (n_tiles + 1, (False, False, True))):
            @pl.when(s == tail)
            def _():
                if tail % 2 == 0:
                    stages(s, refs[:-4], du_even, du_odd, dh_odd, dh_even, run)
                else:
                    stages(s, refs[:-4], du_odd, du_even, dh_even, dh_odd, run)

    def stages(s, refs, du_ref, du_prev, dh_fill, dh_prev, run):
        (xb_ref, xc_ref, ctx_ref, mod_ref, bmod_ref, ng_ref, win_ref, qlg_ref, wuq_ref, kvlg_ref, wukv_ref, qng_ref, kng_ref,
         cos_ref, slo_ref, shi_ref, cq_ref, ckv_ref, kr_ref, dq_ref, dk_ref, dv_ref, dga_ref, dgp_ref, dpl_ref,
         hb_ref, ha_ref, g1_ref,
         gx_ref, gwin_ref, gwuq_ref, gwukv_ref, dqlg_ref, dkvlg_ref, dqng_ref, dkng_ref, dng_ref, dmod_ref) = refs


        i = s
        is_lat = s > 0
        if run[0]:
            cq = cq_ref[...]
            rq = lax.rsqrt(jnp.mean(cq * cq, axis=-1, keepdims=True) + NORM_EPS)
            qhat = cq * rq
            qnb = (qhat * qlg_ref[...]).astype(BF16)
            q = _dot_nt(qnb, wuq_ref[...])
            ckv = ckv_ref[...]
            rkv = lax.rsqrt(jnp.mean(ckv * ckv, axis=-1, keepdims=True) + NORM_EPS)
            kvhat = ckv * rkv
            kvnb = (kvhat * kvlg_ref[...]).astype(BF16)
            kv = _dot(kvnb, wukv_ref[...])

        if run[1]:
            _, _, _, h2, _ = _modulated_input(s <= 1, xb_ref, ctx_ref, mod_ref, bmod_ref, ng_ref)
            dub = du_prev[...]
            du_g, du_m = dub[:, 0:N_GATES], dub[:, N_GATES:U_PAD]
            h2b = h2.astype(BF16)
            dh_fill[...] = _dot(du_g, win_ref[N_MLA:D_IN_PROJ, :]) + _dot(du_m, win_ref[0:U_PAD - N_GATES, :])
            gwin_ref[N_MLA:D_IN_PROJ, :] += _dot_tn(du_g, h2b)
            gwin_ref[0:N_MLA, :] += _dot_tn(du_m, h2b)[0:N_MLA]

        if run[2]:
            ctx3 = s <= 2
            xt, r, xg, _, scale = _modulated_input(ctx3, xc_ref, ctx_ref, mod_ref, bmod_ref, ng_ref)
            dh = dh_prev[...]
            dshift = jnp.sum(dh, axis=0, keepdims=True)
            dscale = jnp.sum(dh * xg, axis=0, keepdims=True)
            zero = jnp.zeros_like(dshift)
            dmod_ref[0:1, :] += jnp.where(ctx3, zero, dshift)
            dmod_ref[1:2, :] += jnp.where(ctx3, zero, dscale)
            dmod_ref[2:3, :] += jnp.where(ctx3, dshift, zero)
            dmod_ref[3:4, :] += jnp.where(ctx3, dscale, zero)
            dxg = dh * (1.0 + scale)
            xr = xt * r
            dng_ref[...] += jnp.sum(dxg * xr, axis=0, keepdims=True)
            dxn = dxg * ng_ref[...]
            gx_ref[...] = g1_ref[...] + r * (dxn - xr * jnp.mean(dxn * xr, axis=-1, keepdims=True))

        if not run[0]:
            return

        cos_t, slo_t, shi_t = cos_ref[...], slo_ref[...], shi_ref[...]
        qg = qng_ref[...]
        dq_heads = []
        dqng = jnp.zeros((1, D_HEAD_PAD), F32)
        for hd in range(N_HEADS):
            qh = q[:, hd * D_HEAD_PAD:(hd + 1) * D_HEAD_PAD]
            rr = lax.rsqrt(jnp.sum(qh * qh, axis=-1, keepdims=True) * (1.0 / D_HEAD) + NORM_EPS)
            yq = qh * rr
            dpost = jnp.where(is_lat, dq_ref[hd] * ATTN_SCALE, 0.0)
            dyn = jnp.concatenate([dpost[:, 0:D_NOPE], _rope_t(dpost[:, D_NOPE:], cos_t, slo_t, shi_t)], axis=1)
            dqng = dqng + jnp.sum(dyn * yq, axis=0, keepdims=True)
            dyg = dyn * qg
            dq_heads.append(rr * (dyg - yq * (jnp.sum(dyg * yq, axis=-1, keepdims=True) * (1.0 / D_HEAD))))
        dqng_ref[...] += dqng
        dqf = jnp.concatenate(dq_heads, axis=1).astype(BF16)

        krz = kr_ref[...]
        kr_ss = jnp.sum(krz * krz, axis=-1, keepdims=True)
        kg = kng_ref[...]
        kg_n, kg_r = kg[:, 0:D_NOPE], kg[:, D_NOPE:]
        dkv_parts = []
        dkr = jnp.zeros((TILE, 128), F32)
        dkng_n = jnp.zeros((1, D_NOPE), F32)
        dkng_r = jnp.zeros((1, 128), F32)
        for hd in range(N_HEADS):
            kn = kv[:, hd * 256:hd * 256 + D_NOPE]
            rr = lax.rsqrt((jnp.sum(kn * kn, axis=-1, keepdims=True) + kr_ss) * (1.0 / D_HEAD) + NORM_EPS)
            yn, yr = kn * rr, krz * rr
            dk_h = jnp.transpose(dk_ref[hd]) * LN_2
            dpn = dk_h[:, 0:D_NOPE]
            dpr = _rope_t(dk_h[:, D_NOPE:], cos_t, slo_t, shi_t)
            dkng_n = dkng_n + jnp.sum(dpn * yn, axis=0, keepdims=True)
            dkng_r = dkng_r + jnp.sum(dpr * yr, axis=0, keepdims=True)
            dyn, dyr = dpn * kg_n, dpr * kg_r
            proj = (jnp.sum(dyn * yn, axis=-1, keepdims=True) + jnp.sum(dyr * yr, axis=-1, keepdims=True)) * (1.0 / D_HEAD)
            dkv_parts.append(rr * (dyn - yn * proj))
            dkv_parts.append(jnp.transpose(dv_ref[hd]))
            dkr = dkr + rr * (dyr - yr * proj)
        dkng_ref[:, 0:D_NOPE] += dkng_n
        dkng_ref[:, D_NOPE:] += dkng_r
        dkvf = jnp.concatenate(dkv_parts, axis=1).astype(BF16)

        lat_t = jnp.maximum(i - 1, 0)
        dpl_t = dpl_ref[...].astype(F32)
        ds = jnp.concatenate([jnp.where(i > 1, hb_ref[...].astype(F32), 0.0), dpl_t,
                              jnp.where(i < n_tiles - 1, ha_ref[...].astype(F32), 0.0)], axis=0)
        tpos = lat_t * TILE - HALO + lax.broadcasted_iota(jnp.int32, (TILE + 2 * HALO, 1), 0)
        for g, w in enumerate(POOL_WINDOWS):
            sl = slice(g * GROUP_DIM, (g + 1) * GROUP_DIM)
            wsum = _window_sum(ds[:, sl] * _inv_count(tpos, w, seq), w, True)[HALO:HALO + TILE]
            du_ref[:, O_PIN + g * GROUP_DIM:O_PIN + (g + 1) * GROUP_DIM] = jnp.where(
                is_lat, wsum - dpl_t[:, sl], 0.0).astype(BF16)

        du_ref[:, O_GA:O_GA + D_ATTN] = jnp.where(is_lat, dga_ref[...], jnp.zeros((), BF16))
        du_ref[:, O_GP:O_GP + D_POOL] = jnp.where(is_lat, dgp_ref[...], jnp.zeros((), BF16))
        du_ref[:, O_KR:U_PAD] = dkr.astype(BF16)

        gwuq_ref[...] += _dot_tn(dqf, qnb)
        dqn = _dot(dqf, wuq_ref[...])
        gwukv_ref[...] += _dot_tn(dkvf, kvnb)
        dkvn = _dot_nt(dkvf, wukv_ref[...])
        dqlg_ref[...] += jnp.sum(dqn * qhat, axis=0, keepdims=True)
        dqhat = dqn * qlg_ref[...]
        dcq = rq * (dqhat - qhat * jnp.mean(dqhat * qhat, axis=-1, keepdims=True))
        dkvlg_ref[...] += jnp.sum(dkvn * kvhat, axis=0, keepdims=True)
        dkvhat = dkvn * kvlg_ref[...]
        dckv = rkv * (dkvhat - kvhat * jnp.mean(dkvhat * kvhat, axis=-1, keepdims=True))
        du_ref[:, O_CQ:O_CQ + R_Q] = dcq.astype(BF16)
        du_ref[:, O_CKV:O_CKV + R_KV] = dckv.astype(BF16)

    t1 = lambda s: jnp.minimum(s, n_tiles - 1)
    t2 = lambda s: jnp.clip(s - 1, 0, n_tiles - 1)
    t3 = lambda s: jnp.clip(s - 2, 0, n_tiles - 1)
    latent = lambda t: jnp.maximum(t - 1, 0)
    lat = lambda s: (latent(t1(s)), 0)
    lat3 = lambda s: (latent(t3(s)), 0)
    full = lambda shape: pl.BlockSpec(shape, lambda s: (0,) * len(shape))
    rope_spec = pl.BlockSpec((TILE, 128), lambda s: (t1(s), 0))
    return pl.pallas_call(
        body, name="inproj_bwd", grid=(n_tiles + 2,),
        out_shape=(jax.ShapeDtypeStruct((seq, D_MODEL), F32), jax.ShapeDtypeStruct((D_IN_PROJ, D_MODEL), F32),
                   jax.ShapeDtypeStruct((N_HEADS * D_HEAD_PAD, R_Q), F32), jax.ShapeDtypeStruct((N_HEADS * 256, R_KV), F32),
                   jax.ShapeDtypeStruct((1, R_Q), F32), jax.ShapeDtypeStruct((1, R_KV), F32),
                   jax.ShapeDtypeStruct((1, D_HEAD_PAD), F32), jax.ShapeDtypeStruct((1, D_HEAD_PAD), F32),
                   jax.ShapeDtypeStruct((1, D_MODEL), F32), jax.ShapeDtypeStruct((8, D_MODEL), F32)),
        in_specs=[pl.BlockSpec((TILE, D_MODEL), lambda s: (latent(t2(s)), 0)), pl.BlockSpec((TILE, D_MODEL), lat3),
                  full((TILE, D_MODEL)), full((8, D_MODEL)), full((8, D_MODEL)),
                  full((1, D_MODEL)), full((D_IN_PROJ, D_MODEL)), full((1, R_Q)), full((N_HEADS * D_HEAD_PAD, R_Q)),
                  full((1, R_KV)), full((R_KV, N_HEADS * 256)), full((1, D_HEAD_PAD)), full((1, D_HEAD_PAD)),
                  rope_spec, rope_spec, rope_spec,
                  pl.BlockSpec((TILE, R_Q), lambda s: (t1(s), (O_CQ - N_GATES) // R_Q)),
                  pl.BlockSpec((TILE, R_KV), lambda s: (t1(s), (O_CKV - N_GATES) // R_KV)),
                  pl.BlockSpec((TILE, 128), lambda s: (t1(s), (O_KR - N_GATES) // 128)),
                  pl.BlockSpec((N_HEADS, TILE, D_HEAD_PAD), lambda s: (0, latent(t1(s)), 0)),
                  pl.BlockSpec((N_HEADS, D_HEAD_PAD, TILE), lambda s: (0, 0, t1(s))),
                  pl.BlockSpec((N_HEADS, D_V, TILE), lambda s: (0, 0, t1(s))),
                  pl.BlockSpec((TILE, D_ATTN), lat), pl.BlockSpec((TILE, D_POOL), lat), pl.BlockSpec((TILE, D_POOL), lat),
                  pl.BlockSpec((HALO, D_POOL), lambda s: (jnp.maximum((t1(s) - 1) * rows8 - 1, 0), 0)),
                  pl.BlockSpec((HALO, D_POOL), lambda s: (jnp.minimum(jnp.maximum(t1(s), 1) * rows8, last8), 0)),
                  pl.BlockSpec((TILE, D_MODEL), lat3)],
        out_specs=(pl.BlockSpec((TILE, D_MODEL), lat3), full((D_IN_PROJ, D_MODEL)), full((N_HEADS * D_HEAD_PAD, R_Q)),
                   full((N_HEADS * 256, R_KV)), full((1, R_Q)), full((1, R_KV)), full((1, D_HEAD_PAD)),
                   full((1, D_HEAD_PAD)), full((1, D_MODEL)), full((8, D_MODEL))),
        scratch_shapes=[pltpu.VMEM((TILE, U_PAD), BF16), pltpu.VMEM((TILE, U_PAD), BF16),
                        pltpu.VMEM((TILE, D_MODEL), F32), pltpu.VMEM((TILE, D_MODEL), F32)],
        compiler_params=pltpu.CompilerParams(dimension_semantics=("arbitrary",), vmem_limit_bytes=VMEM_LIMIT),
    )(x, x, ctx, modrows, bmodrows, norm_g, w_in_p, q_lora_g, w_uq_p, kv_lora_g, w_ukv, qng_p, kng_p, cos, slo, shi,
      u, u, u, dq, dk, dv, dga, dgp, dpl, dpl, dpl, g1)


SMALL_LAYOUT = ((0, D_MODEL), (1, R_Q), (2, R_KV), (3, D_HEAD_PAD), (4, D_HEAD_PAD), (5, D_POOL))
ROW_DMOD_B, ROW_DMOD_C, ROW_LOSS = 6, 9, 12


def _epilogue(parts, landed, smalls, dmod4, dgate, loss_acc, w_mod_l):
    n_a, n_l, n_s = len(parts), len(landed), len(smalls)
    n_mod = w_mod_l.shape[1]
    blk = [p.shape[1:] for p in parts]
    small_out = [D_MODEL, R_Q, R_KV, D_HEAD, D_HEAD, D_POOL]

    def body(*refs):
        part_refs = refs[0:n_a]
        land_refs = refs[n_a:n_a + n_l]
        small_in = refs[n_a + n_l:n_a + n_l + n_s]
        dmod4_ref, dgate_ref, loss_ref, wmod_ref = refs[n_a + n_l + n_s:n_a + n_l + n_s + 4]
        outs = refs[n_a + n_l + n_s + 4:]
        red_refs = outs[0:n_a]
        landsum_refs = outs[n_a:n_a + n_l]
        ccg_ref = outs[n_a + n_l]
        sum_refs = outs[n_a + n_l + 1:n_a + n_l + 1 + n_s]
        gbmod_ref, dmy_ref, lossout_ref = outs[n_a + n_l + 1 + n_s:n_a + n_l + 4 + n_s]
        scr = outs[n_a + n_l + 4 + n_s:]
        recv1, own1, sum1b, recv2 = scr[0:n_a], scr[n_a:2 * n_a], scr[2 * n_a:3 * n_a], scr[3 * n_a:4 * n_a]
        small_ref, smallg_ref, tot_ref, cc_ref = scr[4 * n_a:4 * n_a + 4]
        land_vmem = scr[4 * n_a + 4:4 * n_a + 4 + n_l]
        ssem1, rsem1, lsem, ssem2, rsem2, ssem_s, rsem_s, ssem_cc, rsem_cc, land_sem = scr[4 * n_a + 4 + n_l:]
        x, y, c = _coords()
        me = (x, y, c)
        sibling = (x, y, 1 - c)

        small_ref[...] = jnp.zeros_like(small_ref)
        for ref, (row, width) in zip(small_in, SMALL_LAYOUT):
            small_ref[row:row + 1, 0:width] = ref[...]
        small_ref[ROW_DMOD_B:ROW_DMOD_B + 2, :] = dmod4_ref[0:2, :]
        small_ref[ROW_DMOD_B + 2:ROW_DMOD_B + 3, :] = dgate_ref[...]
        small_ref[ROW_DMOD_C:ROW_DMOD_C + 2, :] = dmod4_ref[2:4, :]
        small_ref[ROW_LOSS:ROW_LOSS + 1, 0:128] = loss_ref[0:1, :]
        smallg_ref[_lin(me)] = small_ref[...]
        s_recv, s_send = _gather_to_all(small_ref, smallg_ref, ssem_s, rsem_s)

        stage1, own_copies = [], []
        for a in range(n_a):
            for b in range(4):
                dev = 4 * (b >> 1) + 2 * (b & 1)
                stage1.append(pltpu.make_async_remote_copy(
                    src_ref=part_refs[a].at[dev + (1 - c)], dst_ref=recv1[a].at[b],
                    send_sem=ssem1.at[a, b], recv_sem=rsem1.at[a, b], device_id=sibling, device_id_type=MESH))
                own_copies.append(pltpu.make_async_copy(part_refs[a].at[dev + c], own1[a].at[b], lsem.at[a, b]))
                stage1[-1].start()
                own_copies[-1].start()
        land_copies = [pltpu.make_async_copy(land_refs[n], land_vmem[n], land_sem.at[n]) for n in range(n_l)]
        for cp in land_copies:
            cp.start()

        s_recv()
        tot = smallg_ref[0]
        for d in range(1, N_DEV):
            tot = tot + smallg_ref[d]
        tot_ref[...] = tot
        for ref, (row, _), width in zip(sum_refs, SMALL_LAYOUT, small_out):
            ref[...] = tot_ref[row:row + 1, 0:width]
        lossout_ref[...] = tot_ref[8:16, 0:128]
        lin = _lin(me)

        def my_columns(three_rows):
            v = jnp.concatenate(three_rows, axis=1)
            out = jnp.zeros((1, n_mod), F32)
            for d in range(N_DEV):
                out = out + jnp.where(lin == d, v[:, d * n_mod:(d + 1) * n_mod], 0.0)
            return out

        rows3 = lambda ref, r0: [ref[r0 + t:r0 + t + 1, :] for t in range(3)]
        dmodc = rows3(tot_ref, ROW_DMOD_C)
        gbmod_ref[...] = jnp.concatenate(rows3(tot_ref, ROW_DMOD_B), axis=1) + jnp.concatenate(dmodc, axis=1)
        dmy_ref[...] = jnp.zeros_like(dmy_ref)
        for b in range(N_DEV):
            dmy_ref[b:b + 1, :] = my_columns(rows3(smallg_ref.at[b], ROW_DMOD_B))
        dmy = my_columns(dmodc)
        dmy_ref[N_DEV:N_DEV + 1, :] = dmy
        part = _dot_nt(jnp.broadcast_to(dmy, (8, n_mod)).astype(BF16), wmod_ref[...].astype(BF16))

        cc_ref[...] = part
        ccg_ref[_lin(me)] = part
        cc_recv, cc_send = _gather_to_all(cc_ref, ccg_ref, ssem_cc, rsem_cc)

        stage2 = []
        for a in range(n_a):
            for b in range(4):
                stage1[4 * a + b].wait_recv()
                own_copies[4 * a + b].wait()
                sum1b[a][b] = (own1[a][b] + recv1[a][b]).astype(BF16)
            for k in (1, 2, 3):
                tx, ty = _flip(x, (k >> 1) & 1), _flip(y, k & 1)
                stage2.append(pltpu.make_async_remote_copy(
                    src_ref=sum1b[a].at[2 * tx + ty], dst_ref=recv2[a].at[k - 1], send_sem=ssem2.at[a, k - 1],
                    recv_sem=rsem2.at[a, k - 1], device_id=(tx, ty, c), device_id_type=MESH))
                stage2[-1].start()
        for a in range(n_a):
            own = own1[a][2 * x + y] + recv1[a][2 * x + y]
            for k in (1, 2, 3):
                stage2[3 * a + k - 1].wait_recv()
                own = own + recv2[a][k - 1].astype(F32)
            red_refs[a][...] = own

        for cp in land_copies:
            cp.wait()
        for ref, out in zip(land_vmem, landsum_refs):
            acc = ref[0].astype(F32)
            for d in range(1, N_DEV):
                acc = acc + ref[d].astype(F32)
            out[...] = acc
        cc_recv()
        for cp in stage1 + stage2:
            cp.wait_send()
        s_send()
        cc_send()

    vm = pl.BlockSpec(memory_space=pltpu.VMEM)
    sds = jax.ShapeDtypeStruct
    return pl.pallas_call(
        body, name="epilogue_reduce",
        out_shape=(*[sds(s, F32) for s in blk], *[sds(a.shape[1:], F32) for a in landed], sds((N_DEV, 8, D_MODEL), F32),
                   *[sds((1, w), F32) for w in small_out], sds((1, 3 * D_MODEL), F32), sds((16, n_mod), F32),
                   sds((8, 128), F32)),
        in_specs=[pl.BlockSpec(memory_space=pl.ANY)] * (n_a + n_l) + [vm] * (n_s + 4),
        out_specs=tuple([vm] * (n_a + n_l + n_s + 4)),
        scratch_shapes=[*[pltpu.VMEM((4,) + s, F32) for s in blk], *[pltpu.VMEM((4,) + s, F32) for s in blk],
                        *[pltpu.VMEM((4,) + s, BF16) for s in blk], *[pltpu.VMEM((3,) + s, BF16) for s in blk],
                        pltpu.VMEM((SMALL_ROWS, D_MODEL), F32), pltpu.VMEM((N_DEV, SMALL_ROWS, D_MODEL), F32),
                        pltpu.VMEM((SMALL_ROWS, D_MODEL), F32), pltpu.VMEM((8, D_MODEL), F32),
                        *[pltpu.VMEM(a.shape, a.dtype) for a in landed],
                        pltpu.SemaphoreType.DMA((n_a, 4)), pltpu.SemaphoreType.DMA((n_a, 4)), pltpu.SemaphoreType.DMA((n_a, 4)),
                        pltpu.SemaphoreType.DMA((n_a, 3)), pltpu.SemaphoreType.DMA((n_a, 3)),
                        pltpu.SemaphoreType.DMA((7,)), pltpu.SemaphoreType.DMA((7,)),
                        pltpu.SemaphoreType.DMA((7,)), pltpu.SemaphoreType.DMA((7,)), pltpu.SemaphoreType.DMA((n_l,))],
        compiler_params=pltpu.CompilerParams(vmem_limit_bytes=VMEM_LIMIT),
    )(*parts, *landed, *smalls, dmod4, dgate, loss_acc, w_mod_l)


def _adamw(weights, grads, ms, vs, c_all_t, dmod_my, p2g):
    n = len(weights)

    def body(*refs):
        w_refs = refs[0:n]
        g_in = refs[n:2 * n - 2]
        m_refs = refs[2 * n - 2:3 * n - 2]
        v_refs = refs[3 * n - 2:4 * n - 2]
        cat_ref, dmod_ref, p2g_ref = refs[4 * n - 2:4 * n + 1]
        outs = refs[4 * n + 1:]
        g_refs, d_refs, nm_refs, nv_refs = outs[0:n], outs[n:2 * n], outs[2 * n:3 * n], outs[3 * n:4 * n]
        gcc_ref, gwm_ref = g_refs[0], g_refs[1]

        part = p2g_ref[0, 0:1, :]
        for d in range(1, N_DEV):
            part = part + p2g_ref[d, 0:1, :]
        cc = w_refs[0][...]
        sg = _sigmoid(cc)
        gcc_ref[...] = part * (sg * (1.0 + cc * (1.0 - sg)))
        cat = cat_ref[...]
        gwm_ref[...] = lax.dot_general(cat * _sigmoid(cat), dmod_ref[...], (((1,), (0,)), ((), ())), precision=HIGHEST,
                                       preferred_element_type=F32)
        for idx in range(n):
            if idx >= 2:
                g_refs[idx][...] = g_in[idx - 2][...]
            g = g_refs[idx][...]
            m = ADAM_B1 * m_refs[idx][...] + (1.0 - ADAM_B1) * g
            v = ADAM_B2 * v_refs[idx][...] + (1.0 - ADAM_B2) * (g * g)
            m_hat = m / (1.0 - ADAM_B1 ** ADAM_STEP)
            v_hat = v / (1.0 - ADAM_B2 ** ADAM_STEP)
            d_refs[idx][...] = -ADAM_LR * (m_hat / (jnp.sqrt(v_hat) + ADAM_EPS) + ADAM_WD * w_refs[idx][...])
            nm_refs[idx][...] = m
            nv_refs[idx][...] = v

    vm = pl.BlockSpec(memory_space=pltpu.VMEM)
    shapes = [jax.ShapeDtypeStruct(w.shape, F32) for w in weights]
    args = list(weights) + list(grads[2:]) + list(ms) + list(vs) + [c_all_t, dmod_my, p2g]
    out_shape = tuple(shapes * 4)
    return pl.pallas_call(
        body, name="adamw_update", out_shape=out_shape, in_specs=[vm] * len(args), out_specs=tuple([vm] * len(out_shape)),
        compiler_params=pltpu.CompilerParams(vmem_limit_bytes=VMEM_LIMIT),
    )(*args)


def _rope_tables(seq):
    rows = seq // GRID_W
    row = np.repeat(np.arange(rows, dtype=np.float32), GRID_W)
    col = np.tile(np.arange(GRID_W, dtype=np.float32), rows)
    n_freq = D_ROPE // 4
    inv = (np.float32(ROPE_BASE) ** (-np.arange(n_freq, dtype=np.float32) / np.float32(n_freq))).astype(np.float32)
    ang_r = (row[:, None] * inv).astype(np.float32)
    ang_c = (col[:, None] * inv).astype(np.float32)
    ang = np.concatenate([ang_r, ang_r, ang_c, ang_c], axis=-1)
    cos, sin = np.cos(ang).astype(np.float32), np.sin(ang).astype(np.float32)
    low = (np.arange(D_ROPE) % 32) < 16

    def table(t, fill):
        out = np.full((TILE + seq, 128), fill, np.float32)
        out[TILE:, 0:D_ROPE] = t
        return jnp.asarray(out)

    return table(cos, 1.0), table(np.where(low, -sin, 0.0), 0.0), table(np.where(low, 0.0, sin), 0.0)


def kernel(x, c, ctx, c_ctx, w_mod, b_mod, norm_g, w_in, q_lora_g, w_uq, kv_lora_g, w_ukv, q_norm_g, k_norm_g, w_pool, pool_scale, w_out, loss_target, m_c_ctx, m_w_mod, m_b_mod, m_norm_g, m_w_in, m_q_lora_g, m_w_uq, m_kv_lora_g, m_w_ukv, m_q_norm_g, m_k_norm_g, m_w_pool, m_pool_scale, m_w_out, v_c_ctx, v_w_mod, v_b_mod, v_norm_g, v_w_in, v_q_lora_g, v_w_uq, v_kv_lora_g, v_w_ukv, v_q_norm_g, v_k_norm_g, v_w_pool, v_pool_scale, v_w_out):
    seq = x.shape[1]
    assert ctx.shape[1] == TILE and seq % TILE == 0 and seq % GRID_W == 0
    ix, iy, ic = lax.axis_index("x"), lax.axis_index("y"), lax.axis_index("c")
    me = 4 * ix + 2 * iy + ic
    x2, ctx2, tgt2 = x[0], ctx[0], loss_target[0]
    w_mod_l, w_ukv_l, w_out_l = w_mod[0], w_ukv[0], w_out[0]
    c_ctx_row = c_ctx.reshape(1, D_MODEL)

    tr = lambda a: jnp.transpose(a[0])
    w_in_tl, w_uq_tl = tr(w_in), tr(w_uq)
    cg, modg, wg_in, wg_uq, wg_ukv = _prologue(
        c, c_ctx_row, w_mod_l,
        [w_in_tl, w_uq_tl, w_ukv_l])
    mod_all = jnp.transpose(modg, (1, 0, 2)).reshape(16, 3 * D_MODEL)
    mod_b = lax.dynamic_slice_in_dim(mod_all, me, 1, axis=0).reshape(3, D_MODEL)
    mod_c = mod_all[8].reshape(3, D_MODEL)
    modrows = jnp.concatenate([mod_b, mod_c[0:2], jnp.zeros((3, D_MODEL), F32)], axis=0)
    b3 = b_mod.reshape(3, D_MODEL)
    bmodrows = jnp.concatenate([b3, b3[0:2], jnp.zeros((3, D_MODEL), F32)], axis=0)

    w_in_p = wg_in.reshape(D_IN_PROJ, D_MODEL)
    w_uq_p = jnp.concatenate([wg_uq.reshape(N_HEADS, D_HEAD, R_Q), jnp.zeros((N_HEADS, D_HEAD_PAD - D_HEAD, R_Q), BF16)],
                             axis=1).reshape(N_HEADS * D_HEAD_PAD, R_Q)
    w_ukv_f = jnp.transpose(wg_ukv, (1, 0, 2)).reshape(R_KV, N_HEADS * 256)
    qng_p = jnp.concatenate([q_norm_g, jnp.zeros((1, D_HEAD_PAD - D_HEAD), F32)], axis=1)
    kng_p = jnp.concatenate([k_norm_g, jnp.zeros((1, D_HEAD_PAD - D_HEAD), F32)], axis=1)
    cos, slo, shi = _rope_tables(seq)

    u_gates, u_mla, q, k, v = _inproj_fwd(x2, ctx2, modrows, bmodrows, norm_g, w_in_p, q_lora_g, w_uq_p, kv_lora_g, w_ukv_f,
                             qng_p, kng_p, cos, slo, shi)
    attn, lse, wg_out = _attn_fwd(q, k, v, min(2048, seq), w_out_l.astype(BF16))
    (loss_acc, g1, dgate, d_attn, dga, dgp, dpl, gwo_b, gwp, dps) = _mix(
        x2, tgt2, attn, u_gates, modrows, bmodrows, w_pool[0], pool_scale, wg_out.reshape(D_MODEL, D_MODEL))

    blocks = lambda a: a.reshape((N_DEV, a.shape[0] // N_DEV) + a.shape[1:])
    dq, dk, dv, land_wo, land_wp = _attn_bwd(q, k, v, attn, lse, d_attn.reshape(seq, D_ATTN), min(1024, seq), blocks(gwo_b),
                                             gwp.reshape(4 * GROUP_DIM, GROUP_DIM))
    (grad_x, gwin_t, gwuq_p, gwukv_t, dqlg, dkvlg, dqng, dkng, dng, dmod4) = _inproj_bwd(
        x2, ctx2, modrows, bmodrows, norm_g, w_in_p, q_lora_g, w_uq_p, kv_lora_g, w_ukv_f, qng_p, kng_p, cos, slo, shi,
        u_mla, dq, dk, dv, dga, dgp, dpl, g1)

    gwuq_t = gwuq_p.reshape(N_HEADS, D_HEAD_PAD, R_Q)[:, 0:D_HEAD].reshape(N_HEADS * D_HEAD, R_Q)
    parts = [blocks(gwin_t), blocks(gwuq_t), blocks(gwukv_t)]
    (r_win, r_wuq, r_wukv, r_wout, g_w_pool, ccg, g_ng, g_qlg, g_kvlg, g_qng, g_kng, g_ps, g_bmod, dmod_my,
     loss_rows) = _epilogue(parts, [land_wo, land_wp], [dng, dqlg, dkvlg, dqng, dkng, dps], dmod4, dgate, loss_acc, w_mod_l)

    g_w_ukv = jnp.transpose(r_wukv)
    c_rows = cg[:, 0, :]
    c_all_t = jnp.transpose(jnp.concatenate([c_rows, c_ctx_row, jnp.zeros((16 - N_DEV - 1, D_MODEL), F32)], axis=0))

    weights = [c_ctx_row, w_mod_l, b_mod, norm_g, w_in_tl, q_lora_g, w_uq_tl, kv_lora_g, w_ukv_l, q_norm_g, k_norm_g,
               w_pool.reshape(4 * GROUP_DIM, GROUP_DIM), pool_scale, w_out_l]
    grads = [None, None, g_bmod, g_ng, r_win, g_qlg, r_wuq, g_kvlg, g_w_ukv, g_qng, g_kng, g_w_pool, g_ps, r_wout]
    ms = [m_c_ctx.reshape(1, D_MODEL), m_w_mod[0], m_b_mod, m_norm_g, tr(m_w_in), m_q_lora_g, tr(m_w_uq), m_kv_lora_g,
          m_w_ukv[0], m_q_norm_g, m_k_norm_g, m_w_pool.reshape(4 * GROUP_DIM, GROUP_DIM), m_pool_scale, m_w_out[0]]
    vs = [v_c_ctx.reshape(1, D_MODEL), v_w_mod[0], v_b_mod, v_norm_g, tr(v_w_in), v_q_lora_g, tr(v_w_uq), v_kv_lora_g,
          v_w_ukv[0], v_q_norm_g, v_k_norm_g, v_w_pool.reshape(4 * GROUP_DIM, GROUP_DIM), v_pool_scale, v_w_out[0]]
    outs = _adamw(weights, grads, ms, vs, c_all_t, dmod_my, ccg)
    n = len(weights)
    grads, deltas, new_m, new_v = outs[0:n], outs[n:2 * n], outs[2 * n:3 * n], outs[3 * n:4 * n]

    loss = loss_rows[ROW_LOSS - 8, 0]
    final = [c_ctx.shape, w_mod.shape, b_mod.shape, norm_g.shape, w_in.shape, q_lora_g.shape, w_uq.shape, kv_lora_g.shape,
             w_ukv.shape, q_norm_g.shape, k_norm_g.shape, w_pool.shape, pool_scale.shape, w_out.shape]
    transposed = (4, 6)

    def shaped(arrs):
        return [(jnp.transpose(a) if i in transposed else a).reshape(s) for i, (a, s) in enumerate(zip(arrs, final))]

    return (loss, grad_x[None], *shaped(grads), *shaped(deltas), *shaped(new_m), *shaped(new_v))
```

```python
import numpy as np

import jax
import jax.numpy as jnp
from jax import lax
from jax.experimental import pallas as pl
from jax.experimental.pallas import tpu as pltpu

F32 = jnp.float32
BF16 = jnp.bfloat16
MESH = pl.DeviceIdType.MESH
HIGHEST = lax.Precision.HIGHEST

D_MODEL = 1024
N_HEADS = 4
D_NOPE = 128
D_ROPE = 64
D_HEAD = D_NOPE + D_ROPE
D_HEAD_PAD = 256
D_V = 128
R_Q = 256
R_KV = 128
D_ATTN = 512
D_POOL = 512
POOL_WINDOWS = (2, 4, 8, 16)
GROUP_DIM = 128
GRID_W = 64
ROPE_BASE = 10000.0
NORM_EPS = 1e-6
ATTN_SCALE = D_HEAD ** -0.5
LOG2_E = 1.4426950408889634
LN_2 = 0.6931471805599453
ATTN_ROWS = 256
D_IN_PROJ = 1984
U_PAD = 2048
O_GA, O_PIN, O_GP, O_CQ, O_CKV, O_KR = 0, 512, 1024, 1536, 1792, 1920
TILE = 256
MIX_TILE = 512
Q_BLOCK = 128
HALO = 16
N_GATES = 1536
N_MLA = D_IN_PROJ - N_GATES
N_DEV = 8
VMEM_LIMIT = 56 * 1024 * 1024

ADAM_LR = 0.001
ADAM_B1 = 0.9
ADAM_B2 = 0.999
ADAM_EPS = 1e-08
ADAM_WD = 0.01
ADAM_STEP = 10

SMALL_ROWS = 16


def _dot(a, b):
    return lax.dot_general(a, b, (((1,), (0,)), ((), ())), preferred_element_type=F32)


def _dot_nt(a, b):
    return lax.dot_general(a, b, (((1,), (1,)), ((), ())), preferred_element_type=F32)


def _dot_tn(a, b):
    return lax.dot_general(a, b, (((0,), (0,)), ((), ())), preferred_element_type=F32)


def _sigmoid(x):
    return 1.0 / (1.0 + jnp.exp(-x))


def _rope(p, cos, slo, shi):
    return p * cos + pltpu.roll(p, 112, 1) * slo + pltpu.roll(p, 16, 1) * shi


def _rope_t(d, cos, slo, shi):
    return d * cos + pltpu.roll(d * slo, 16, 1) + pltpu.roll(d * shi, 112, 1)


def _shift_rows(a, k):
    n = a.shape[0]
    return pltpu.roll(a, (-k) % n, 0)


def _window_sum(x, w, transposed):
    s = (x + _shift_rows(x, 1)) if transposed else (_shift_rows(x, -1) + x)
    step = 1
    while 2 * step < w:
        s = _shift_rows(s, -step) + _shift_rows(s, step)
        step *= 2
    return s


def _inv_count(tpos, w, seq):
    lo = jnp.maximum(tpos - w // 2, 0)
    hi = jnp.minimum(tpos - w // 2 + w, seq)
    return 1.0 / jnp.maximum(hi - lo, 1).astype(F32)


def _coords():
    return lax.axis_index("x"), lax.axis_index("y"), lax.axis_index("c")


def _flip(v, bit):
    return (1 - v) if bit else v


def _peer(k):
    x, y, c = _coords()
    return (_flip(x, (k >> 2) & 1), _flip(y, (k >> 1) & 1), _flip(c, k & 1))


def _lin(p):
    return 4 * p[0] + 2 * p[1] + p[2]


def _gather_to_all(src_ref, slots_ref, send_sems, recv_sems):
    me = _coords()
    copies = []
    for k in range(1, N_DEV):
        cp = pltpu.make_async_remote_copy(
            src_ref=src_ref, dst_ref=slots_ref.at[_lin(me)], send_sem=send_sems.at[k - 1], recv_sem=recv_sems.at[k - 1],
            device_id=_peer(k), device_id_type=MESH)
        cp.start()
        copies.append(cp)

    def wait_recv():
        for k in range(1, N_DEV):
            pltpu.make_async_remote_copy(
                src_ref=src_ref, dst_ref=slots_ref.at[_lin(_peer(k))], send_sem=send_sems.at[k - 1],
                recv_sem=recv_sems.at[k - 1], device_id=_peer(k), device_id_type=MESH).wait_recv()

    def wait_send():
        for cp in copies:
            cp.wait_send()

    return wait_recv, wait_send


def _prologue(c8, cctx8, w_mod_l, shards):
    n_mod = w_mod_l.shape[1]
    n_w = len(shards)

    def body(c_ref, cctx_ref, wmod_ref, *refs):
        w_refs = refs[0:n_w]
        cg_ref, modg_ref = refs[n_w], refs[n_w + 1]
        wg_refs = refs[n_w + 2:2 * n_w + 2]
        modblk_ref = refs[2 * n_w + 2]
        wb_refs = refs[2 * n_w + 3:3 * n_w + 3]
        c8_ref = refs[3 * n_w + 3]
        ssem_w, rsem_w, lsem_w, ssem_c, rsem_c, ssem_m, rsem_m = refs[3 * n_w + 4:]
        x, y, c = _coords()
        me = (x, y, c)
        sibling = (x, y, 1 - c)
        chips = [(1 - x, y), (x, 1 - y), (1 - x, 1 - y)]

        def wcopies(k, block, to, own=False):
            out = []
            for a in range(n_w):
                slot = wg_refs[a].at[_lin(block)]
                out.append(pltpu.make_async_remote_copy(
                    src_ref=wb_refs[a] if own else slot, dst_ref=slot, send_sem=ssem_w.at[a, k], recv_sem=rsem_w.at[a, k],
                    device_id=to, device_id_type=MESH))
            return out

        c8_ref[...] = jnp.broadcast_to(c_ref[...], (8, D_MODEL))
        cg_ref[_lin(me)] = c8_ref[...]
        c_recv, c_send = _gather_to_all(c8_ref, cg_ref, ssem_c, rsem_c)

        for a in range(n_w):
            wb_refs[a][...] = w_refs[a][...].astype(BF16)
        first = wcopies(0, me, sibling, own=True)
        for j, chip in enumerate(chips):
            first += wcopies(1 + j, me, (*chip, c), own=True)
        late = [idx for idx in range(len(first)) if idx % n_w == 0 and idx >= n_w]
        for idx, cp in enumerate(first):
            if idx not in late:
                cp.start()
        mine = [pltpu.make_async_copy(wb_refs[a], wg_refs[a].at[_lin(me)], lsem_w.at[a]) for a in range(n_w)]
        for cp in mine:
            cp.start()

        c_recv()
        row = lax.broadcasted_iota(jnp.int32, (8, D_MODEL), 0)
        c_all = jnp.zeros((8, D_MODEL), F32)
        for d in range(N_DEV):
            c_all = c_all + jnp.where(row == d, cg_ref[d], 0.0)
        cc = jnp.broadcast_to(cctx_ref[...], (8, D_MODEL))
        a = jnp.concatenate([c_all * _sigmoid(c_all), cc * _sigmoid(cc)], axis=0)
        modblk_ref[...] = _dot(a.astype(BF16), wmod_ref[...].astype(BF16))
        modg_ref[_lin(me)] = modblk_ref[...]
        m_recv, m_send = _gather_to_all(modblk_ref, modg_ref, ssem_m, rsem_m)
        for idx in late:
            first[idx].start()
        m_recv()

        passed = []
        for j, chip in enumerate(chips):
            for cp in wcopies(1 + j, (*chip, c), me):
                cp.wait_recv()
            fwd = wcopies(4 + j, (*chip, c), sibling)
            for cp in fwd:
                cp.start()
            passed += fwd
        for cp in wcopies(0, sibling, me):
            cp.wait_recv()
        for j, chip in enumerate(chips):
            for cp in wcopies(4 + j, (*chip, 1 - c), me):
                cp.wait_recv()
        for cp in first + passed:
            cp.wait_send()
        for cp in mine:
            cp.wait()
        c_send()
        m_send()

    vm = pl.BlockSpec(memory_space=pltpu.VMEM)
    hbm = pl.BlockSpec(memory_space=pl.ANY)
    return pl.pallas_call(
        body, name="prologue_gather",
        out_shape=(jax.ShapeDtypeStruct((N_DEV, 8, D_MODEL), F32), jax.ShapeDtypeStruct((N_DEV, 16, n_mod), F32),
                   *[jax.ShapeDtypeStruct((N_DEV,) + s.shape, BF16) for s in shards]),
        in_specs=[vm] * (3 + n_w), out_specs=(vm, vm, *[hbm] * n_w),
        scratch_shapes=[pltpu.VMEM((16, n_mod), F32), *[pltpu.VMEM(s.shape, BF16) for s in shards],
                        pltpu.VMEM((8, D_MODEL), F32),
                        pltpu.SemaphoreType.DMA((n_w, 7)), pltpu.SemaphoreType.DMA((n_w, 7)), pltpu.SemaphoreType.DMA((n_w,)),
                        pltpu.SemaphoreType.DMA((7,)), pltpu.SemaphoreType.DMA((7,)),
                        pltpu.SemaphoreType.DMA((7,)), pltpu.SemaphoreType.DMA((7,))],
        compiler_params=pltpu.CompilerParams(vmem_limit_bytes=VMEM_LIMIT),
    )(c8, cctx8, w_mod_l, *shards)


def _modulated_input(is_ctx, x_ref, ctx_ref, mod_ref, bmod_ref, ng_ref):
    xt = jnp.where(is_ctx, ctx_ref[...], x_ref[...])
    shift = jnp.where(is_ctx, mod_ref[3:4, :] + bmod_ref[3:4, :], mod_ref[0:1, :] + bmod_ref[0:1, :])
    scale = jnp.where(is_ctx, mod_ref[4:5, :] + bmod_ref[4:5, :], mod_ref[1:2, :] + bmod_ref[1:2, :])
    r = lax.rsqrt(jnp.mean(xt * xt, axis=-1, keepdims=True) + NORM_EPS)
    xg = (xt * r) * ng_ref[...]
    h = xg * (1.0 + scale) + shift
    return xt, r, xg, h, scale


def _inproj_fwd(x, ctx, modrows, bmodrows, norm_g, w_in_p, q_lora_g, w_uq_p, kv_lora_g, w_ukv, qng_p, kng_p, cos, slo, shi):
    seq = x.shape[0]
    n_tiles = seq // TILE + 1
    tot = seq + TILE

    n_mla = R_Q + R_KV + 128

    def body(x_ref, ctx_ref, mod_ref, bmod_ref, ng_ref, win_ref, qlg_ref, wuq_ref, kvlg_ref, wukv_ref, qng_ref, kng_ref,
             cos_ref, slo_ref, shi_ref, ug_ref, um_ref, q_ref, k_ref, v_ref, stash):
        s = pl.program_id(0)

        refs = (x_ref, ctx_ref, mod_ref, bmod_ref, ng_ref, win_ref, qlg_ref, wuq_ref, kvlg_ref, wukv_ref, qng_ref, kng_ref,
                cos_ref, slo_ref, shi_ref, ug_ref, um_ref, q_ref, k_ref, v_ref)

        @pl.when(s == 0)
        def _():
            stages(s, refs, stash, stash, True, False)

        @pl.when((s > 0) & (s < n_tiles))
        def _():
            stages(s, refs, stash, stash, True, True)

        @pl.when(s == n_tiles)
        def _():
            stages(s, refs, stash, stash, False, True)

    def stages(s, refs, fill, prev_ref, run_first, run_second):
        (x_ref, ctx_ref, mod_ref, bmod_ref, ng_ref, win_ref, qlg_ref, wuq_ref, kvlg_ref, wukv_ref, qng_ref, kng_ref,
         cos_ref, slo_ref, shi_ref, ug_ref, um_ref, q_ref, k_ref, v_ref) = refs
        if run_second:
            cos_t, slo_t, shi_t = cos_ref[...], slo_ref[...], shi_ref[...]
            prev = prev_ref[...]
            cq = prev[:, 0:R_Q]
            qn = cq * lax.rsqrt(jnp.mean(cq * cq, axis=-1, keepdims=True) + NORM_EPS) * qlg_ref[...]
            q = _dot_nt(qn.astype(BF16), wuq_ref[...])
            qg = qng_ref[...] * (ATTN_SCALE * LOG2_E)
            for hd in range(N_HEADS):
                qh = q[:, hd * D_HEAD_PAD:(hd + 1) * D_HEAD_PAD]
                rr = lax.rsqrt(jnp.sum(qh * qh, axis=-1, keepdims=True) * (1.0 / D_HEAD) + NORM_EPS)
                qy = qh * rr * qg
                q_ref[hd, :, 0:D_NOPE] = qy[:, 0:D_NOPE].astype(BF16)
                q_ref[hd, :, D_NOPE:D_HEAD_PAD] = _rope(qy[:, D_NOPE:D_HEAD_PAD], cos_t, slo_t, shi_t).astype(BF16)

            ckv = prev[:, R_Q:R_Q + R_KV]
            kvn = ckv * lax.rsqrt(jnp.mean(ckv * ckv, axis=-1, keepdims=True) + NORM_EPS) * kvlg_ref[...]
            kv = _dot(kvn.astype(BF16), wukv_ref[...])
            krz = prev[:, R_Q + R_KV:n_mla]
            kr_ss = jnp.sum(krz * krz, axis=-1, keepdims=True)
            kg = kng_ref[...]
            for hd in range(N_HEADS):
                kn = kv[:, hd * 256:hd * 256 + D_NOPE]
                rr = lax.rsqrt((jnp.sum(kn * kn, axis=-1, keepdims=True) + kr_ss) * (1.0 / D_HEAD) + NORM_EPS)
                k_ref[hd, :, 0:D_NOPE] = (kn * rr * kg[:, 0:D_NOPE]).astype(BF16)
                k_ref[hd, :, D_NOPE:D_HEAD_PAD] = _rope(krz * rr * kg[:, D_NOPE:D_HEAD_PAD], cos_t, slo_t, shi_t).astype(BF16)
                v_ref[hd] = kv[:, hd * 256 + D_NOPE:(hd + 1) * 256].astype(BF16)

        if not run_first:
            return
        _, _, _, h, _ = _modulated_input(s == 0, x_ref, ctx_ref, mod_ref, bmod_ref, ng_ref)
        hb = h.astype(BF16)
        ug_ref[...] = _dot_nt(hb, win_ref[N_MLA:D_IN_PROJ, :]).astype(BF16)
        um = _dot_nt(hb, win_ref[0:n_mla, :])
        lane = lax.broadcasted_iota(jnp.int32, (TILE, 128), 1)
        um = jnp.concatenate([um[:, 0:R_Q + R_KV], jnp.where(lane < D_ROPE, um[:, R_Q + R_KV:n_mla], 0.0)], axis=1)
        um_ref[...] = um
        fill[...] = um

    first = lambda s: jnp.minimum(s, n_tiles - 1)
    second = lambda s: jnp.maximum(s - 1, 0)
    latent = lambda t: jnp.maximum(t - 1, 0)
    full = lambda shape: pl.BlockSpec(shape, lambda s: (0,) * len(shape))
    rope_spec = pl.BlockSpec((TILE, 128), lambda s: (second(s), 0))
    return pl.pallas_call(
        body, name="inproj_fwd", grid=(n_tiles + 1,),
        out_shape=(jax.ShapeDtypeStruct((seq, N_GATES), BF16), jax.ShapeDtypeStruct((tot, n_mla), F32),
                   jax.ShapeDtypeStruct((N_HEADS, seq, D_HEAD_PAD), BF16),
                   jax.ShapeDtypeStruct((N_HEADS, tot, D_HEAD_PAD), BF16),
                   jax.ShapeDtypeStruct((N_HEADS, tot, D_V), BF16)),
        in_specs=[pl.BlockSpec((TILE, D_MODEL), lambda s: (latent(first(s)), 0)), full((TILE, D_MODEL)), full((8, D_MODEL)),
                  full((8, D_MODEL)), full((1, D_MODEL)), full((D_IN_PROJ, D_MODEL)), full((1, R_Q)),
                  full((N_HEADS * D_HEAD_PAD, R_Q)), full((1, R_KV)), full((R_KV, N_HEADS * 256)), full((1, D_HEAD_PAD)),
                  full((1, D_HEAD_PAD)), rope_spec, rope_spec, rope_spec],
        out_specs=(pl.BlockSpec((TILE, N_GATES), lambda s: (latent(first(s)), 0)),
                   pl.BlockSpec((TILE, n_mla), lambda s: (first(s), 0)),
                   pl.BlockSpec((N_HEADS, TILE, D_HEAD_PAD), lambda s: (0, latent(second(s)), 0)),
                   pl.BlockSpec((N_HEADS, TILE, D_HEAD_PAD), lambda s: (0, second(s), 0)),
                   pl.BlockSpec((N_HEADS, TILE, D_V), lambda s: (0, second(s), 0))),
        scratch_shapes=[pltpu.VMEM((TILE, n_mla), F32)],
        compiler_params=pltpu.CompilerParams(dimension_semantics=("arbitrary",), vmem_limit_bytes=VMEM_LIMIT),
    )(x, ctx, modrows, bmodrows, norm_g, w_in_p, q_lora_g, w_uq_p, kv_lora_g, w_ukv, qng_p, kng_p, cos, slo, shi)


def _hosted_exchange(first, last, items, send_sems, recv_sems, local_sems):
    me = _lin(_coords())

    def remote(n, k, src, land, scatter):
        peer = _peer(k)
        return pltpu.make_async_remote_copy(
            src_ref=src.at[_lin(peer)] if scatter else src, dst_ref=land.at[me], send_sem=send_sems.at[n, k - 1],
            recv_sem=recv_sems.at[n, k - 1], device_id=peer, device_id_type=MESH)

    def local(n, src, land, scatter):
        return pltpu.make_async_copy(src.at[me] if scatter else src, land.at[me], local_sems.at[n])

    @pl.when(first)
    def _():
        for n, (src, land, scatter) in enumerate(items):
            for k in range(1, N_DEV):
                remote(n, k, src, land, scatter).start()
            local(n, src, land, scatter).start()

    @pl.when(last)
    def _():
        for n, (src, land, scatter) in enumerate(items):
            for k in range(1, N_DEV):
                peer = _peer(k)
                pltpu.make_async_remote_copy(
                    src_ref=src.at[0] if scatter else src, dst_ref=land.at[_lin(peer)], send_sem=send_sems.at[n, k - 1],
                    recv_sem=recv_sems.at[n, k - 1], device_id=peer, device_id_type=MESH).wait_recv()
            for k in range(1, N_DEV):
                remote(n, k, src, land, scatter).wait_send()
            local(n, src, land, scatter).wait()


def _attn_fwd(q, k, v, block_q, w_out_b):
    _, seq, _ = q.shape
    n_keys = k.shape[1]
    n_q = seq // block_q

    def body(q_ref, k_ref, v_ref, wo_ref, o_ref, lse_ref, wog_ref, ssem, rsem, lsem):
        h, i = pl.program_id(0), pl.program_id(1)
        _hosted_exchange((h == 0) & (i == 0), (h == N_HEADS - 1) & (i == n_q - 1), [(wo_ref, wog_ref, False)],
                         ssem, rsem, lsem)
        kb, vb = k_ref[0], v_ref[0]
        for r0 in range(0, block_q, ATTN_ROWS):
            rows = slice(r0, r0 + ATTN_ROWS)
            s = _dot_nt(q_ref[0, rows, :], kb)
            m = jnp.max(s, axis=-1, keepdims=True)
            p = jnp.exp2(s - m)
            l = jnp.sum(p, axis=-1, keepdims=True)
            o_ref[rows, :] = _dot(p.astype(BF16), vb) * (1.0 / l)
            lse_ref[0, rows, :] = m + jnp.log2(l)

    hbm = pl.BlockSpec(memory_space=pl.ANY)
    return pl.pallas_call(
        body, name="attn_fwd", grid=(N_HEADS, n_q),
        out_shape=(jax.ShapeDtypeStruct((seq, D_ATTN), F32), jax.ShapeDtypeStruct((N_HEADS, seq, 1), F32),
                   jax.ShapeDtypeStruct((N_DEV,) + w_out_b.shape, w_out_b.dtype)),
        in_specs=[pl.BlockSpec((1, block_q, D_HEAD_PAD), lambda h, i: (h, i, 0)),
                  pl.BlockSpec((1, n_keys, D_HEAD_PAD), lambda h, i: (h, 0, 0)),
                  pl.BlockSpec((1, n_keys, D_V), lambda h, i: (h, 0, 0)), hbm],
        out_specs=(pl.BlockSpec((block_q, D_V), lambda h, i: (i, h)),
                   pl.BlockSpec((1, block_q, 1), lambda h, i: (h, i, 0)), hbm),
        scratch_shapes=[pltpu.SemaphoreType.DMA((1, 7)), pltpu.SemaphoreType.DMA((1, 7)), pltpu.SemaphoreType.DMA((1,))],
        compiler_params=pltpu.CompilerParams(dimension_semantics=("arbitrary", "arbitrary"), vmem_limit_bytes=VMEM_LIMIT),
    )(q, k, v, w_out_b)


def _attn_bwd(q, k, v, o, lse, d_o, block_q, gwo_blocks, gwp):
    _, seq, _ = q.shape
    n_keys = k.shape[1]
    n_q = seq // block_q

    def body(q_ref, k_ref, v_ref, o_ref, lse_ref, do_ref, gwo_ref, gwp_ref, dq_ref, dkt_ref, dvt_ref, lwo_ref, lwp_ref,
             ssem, rsem, lsem):
        h, i = pl.program_id(0), pl.program_id(1)
        _hosted_exchange((h == 0) & (i == 0), (h == N_HEADS - 1) & (i == n_q - 1),
                         [(gwo_ref, lwo_ref, True), (gwp_ref, lwp_ref, False)], ssem, rsem, lsem)

        @pl.when(i == 0)
        def _():
            dkt_ref[...] = jnp.zeros_like(dkt_ref)
            dvt_ref[...] = jnp.zeros_like(dvt_ref)

        kb, vb = k_ref[0], v_ref[0]
        raws, ps = [], []
        for r0 in range(0, block_q, ATTN_ROWS):
            rows = slice(r0, r0 + ATTN_ROWS)
            d_out = do_ref[rows, :]
            p = jnp.exp2(_dot_nt(q_ref[0, rows, :], kb) - lse_ref[0, rows, :])
            dp = _dot_nt(d_out.astype(BF16), vb)
            delta = jnp.sum(d_out * o_ref[rows, :], axis=-1, keepdims=True)
            raw = (p * (dp - delta)).astype(BF16)
            dq_ref[0, rows, :] = _dot(raw, kb)
            raws.append(raw)
            ps.append(p.astype(BF16))
        dkt_ref[0] += _dot_tn(q_ref[0], jnp.concatenate(raws, axis=0))
        dvt_ref[0] += _dot_tn(do_ref[...].astype(BF16), jnp.concatenate(ps, axis=0))

    hbm = pl.BlockSpec(memory_space=pl.ANY)
    return pl.pallas_call(
        body, name="attn_bwd", grid=(N_HEADS, n_q),
        out_shape=(jax.ShapeDtypeStruct((N_HEADS, seq, D_HEAD_PAD), F32),
                   jax.ShapeDtypeStruct((N_HEADS, D_HEAD_PAD, n_keys), F32),
                   jax.ShapeDtypeStruct((N_HEADS, D_V, n_keys), F32),
                   jax.ShapeDtypeStruct(gwo_blocks.shape, gwo_blocks.dtype),
                   jax.ShapeDtypeStruct((N_DEV,) + gwp.shape, gwp.dtype)),
        in_specs=[pl.BlockSpec((1, block_q, D_HEAD_PAD), lambda h, i: (h, i, 0)),
                  pl.BlockSpec((1, n_keys, D_HEAD_PAD), lambda h, i: (h, 0, 0)),
                  pl.BlockSpec((1, n_keys, D_V), lambda h, i: (h, 0, 0)),
                  pl.BlockSpec((block_q, D_V), lambda h, i: (i, h)),
                  pl.BlockSpec((1, block_q, 1), lambda h, i: (h, i, 0)),
                  pl.BlockSpec((block_q, D_V), lambda h, i: (i, h)), hbm, hbm],
        out_specs=(pl.BlockSpec((1, block_q, D_HEAD_PAD), lambda h, i: (h, i, 0)),
                   pl.BlockSpec((1, D_HEAD_PAD, n_keys), lambda h, i: (h, 0, 0)),
                   pl.BlockSpec((1, D_V, n_keys), lambda h, i: (h, 0, 0)), hbm, hbm),
        scratch_shapes=[pltpu.SemaphoreType.DMA((2, 7)), pltpu.SemaphoreType.DMA((2, 7)), pltpu.SemaphoreType.DMA((2,))],
        compiler_params=pltpu.CompilerParams(dimension_semantics=("arbitrary", "arbitrary"), vmem_limit_bytes=VMEM_LIMIT),
    )(q, k, v, o, lse, d_o, gwo_blocks, gwp)


def _mix(x, target, attn, u, modrows, bmodrows, w_pool, pool_scale, w_out):
    seq = x.shape[0]
    rows = min(MIX_TILE, seq // 2)
    n_tiles = seq // rows
    rows8 = rows // HALO
    last8 = seq // HALO - 1

    n_blk = seq // Q_BLOCK
    per = rows // n_blk
    assert rows % n_blk == 0 and per % 8 == 0 and n_tiles >= 2
    n_stash = 8

    def body(x_ref, t_ref, o_hbm, ga_ref, pin_ref, hb_ref, ha_ref, gp_ref, mod_ref, bmod_ref, wp_ref, ps_ref, wo_ref,
             loss_ref, g1_ref, dgate_ref, do_hbm, dga_ref, dgp_ref, dpl_ref, gwob_ref, gwp_ref, dps_ref,
             abuf, dbuf, gwo_ref, *rest):
        stash_even, stash_odd = rest[0:n_stash], rest[n_stash:2 * n_stash]
        rsem, wsem = rest[2 * n_stash:]
        s = pl.program_id(0)

        def slab_copies(tile, slot, fetch):
            window = pl.ds(tile * per, per)
            if fetch:
                return [pltpu.make_async_copy(o_hbm.at[:, window, :], abuf.at[slot], rsem.at[slot])]
            return [pltpu.make_async_copy(dbuf.at[slot], do_hbm.at[:, window, :], wsem.at[slot])]

        def wait_all(slot, fetch):
            slab_copies(0, slot, fetch)[0].wait()

        a_slot = lax.rem(s, 2)
        b_slot = 1 - a_slot

        @pl.when(s == 0)
        def _():
            loss_ref[...] = jnp.zeros_like(loss_ref)
            dgate_ref[...] = jnp.zeros_like(dgate_ref)
            gwo_ref[...] = jnp.zeros_like(gwo_ref)
            gwp_ref[...] = jnp.zeros_like(gwp_ref)
            dps_ref[...] = jnp.zeros_like(dps_ref)
            for cp in slab_copies(0, 0, True):
                cp.start()

        @pl.when(s + 1 < n_tiles)
        def _():
            for cp in slab_copies(s + 1, b_slot, True):
                cp.start()

        @pl.when(s < n_tiles)
        def _():
            wait_all(a_slot, True)

        @pl.when(s >= 3)
        def _():
            wait_all(b_slot, False)

        gate = mod_ref[2:3, :] + bmod_ref[2:3, :]
        ps = ps_ref[...]

        def a_vector(slot):
            attn_t = jnp.swapaxes(abuf[slot], 0, 1).reshape(rows, D_ATTN)
            ga = ga_ref[...].astype(F32)
            sga = _sigmoid(ga)
            silu_ga = ga * sga
            pin = pin_ref[...].astype(F32)
            xs = jnp.concatenate([jnp.where(s > 0, hb_ref[...].astype(F32), 0.0), pin,
                                  jnp.where(s < n_tiles - 1, ha_ref[...].astype(F32), 0.0)], axis=0)
            tpos = s * rows + lax.broadcasted_iota(jnp.int32, (rows, 1), 0)
            pooled = []
            for g, w in enumerate(POOL_WINDOWS):
                sl = slice(g * GROUP_DIM, (g + 1) * GROUP_DIM)
                ws = _window_sum(xs[:, sl], w, False)[HALO:HALO + rows]
                pooled.append((ws * _inv_count(tpos, w, seq) - pin[:, sl]).astype(BF16))
            return attn_t, ga, sga, silu_ga, pooled

        def a_group_maps(pooled):
            return jnp.concatenate([_dot(pooled[g], wp_ref[g].astype(BF16)) for g in range(len(POOL_WINDOWS))], axis=1)

        def a_project(stash, yp, attn_t, ga, sga, silu_ga, pooled):
            zb_ref, _, fa_ref, sa_ref, fp_ref, sp_ref, yp_ref, pooled_ref = stash
            ypool = yp * ps
            gp = gp_ref[...].astype(F32)
            sgp = _sigmoid(gp)
            silu_gp = gp * sgp
            z = jnp.concatenate([silu_ga * attn_t, silu_gp * ypool], axis=1).astype(BF16)
            y = _dot(z, wo_ref[...])
            zb_ref[...] = z
            fa_ref[...] = attn_t * (sga * (1.0 + ga * (1.0 - sga)))
            sa_ref[...] = silu_ga
            fp_ref[...] = ypool * (sgp * (1.0 + gp * (1.0 - sgp)))
            sp_ref[...] = silu_gp
            yp_ref[...] = yp
            pooled_ref[...] = jnp.concatenate(pooled, axis=1)
            return y

        def a_loss(stash, y):
            res = x_ref[...] + gate * y - t_ref[...]
            loss_ref[...] += jnp.sum(res * res) * (0.5 / D_MODEL)
            dxn = res * (1.0 / D_MODEL)
            g1_ref[...] = dxn
            dgate_ref[...] += jnp.sum(dxn * y, axis=0, keepdims=True)
            stash[1][...] = (gate * dxn).astype(BF16)

        def b_dz(stash):
            return _dot_nt(stash[1][...], wo_ref[...])

        def b_gwo(stash):
            gwo_ref[...] += _dot_tn(stash[0][...], stash[1][...])

        def b_vector(stash, slot, dz):
            _, _, fa_ref, sa_ref, fp_ref, sp_ref, yp_ref, _ = stash
            dbra = dz[:, 0:D_ATTN]
            dbrp = dz[:, D_ATTN:]
            dga_ref[...] = (dbra * fa_ref[...]).astype(BF16)
            dbuf[slot] = jnp.swapaxes((dbra * sa_ref[...]).reshape(per, n_blk, D_ATTN), 0, 1)
            dgp_ref[...] = (dbrp * fp_ref[...]).astype(BF16)
            dyp_s = dbrp * sp_ref[...]
            dps_ref[...] += jnp.sum(dyp_s * yp_ref[...], axis=0, keepdims=True)
            return (dyp_s * ps).astype(BF16)

        def b_small(stash, dyp):
            pooled_ref = stash[7]
            for g in range(len(POOL_WINDOWS)):
                sl = slice(g * GROUP_DIM, (g + 1) * GROUP_DIM)
                gwp_ref[g] += _dot_tn(pooled_ref[:, sl], dyp[:, sl])
                dpl_ref[:, sl] = _dot_nt(dyp[:, sl], wp_ref[g].astype(BF16)).astype(BF16)

        def both(a_stash, b_stash, slot_a):
            dz = b_dz(b_stash)
            front = a_vector(slot_a)
            yp = a_group_maps(front[-1])
            b_gwo(b_stash)
            y = a_project(a_stash, yp, *front)
            dyp = b_vector(b_stash, 1 - slot_a, dz)
            a_loss(a_stash, y)
            b_small(b_stash, dyp)

        @pl.when(s == 0)
        def _():
            front = a_vector(0)
            a_loss(stash_even, a_project(stash_even, a_group_maps(front[-1]), *front))

        @pl.when((s > 0) & (s < n_tiles) & (a_slot == 0))
        def _():
            both(stash_even, stash_odd, 0)

        @pl.when((s > 0) & (s < n_tiles) & (a_slot == 1))
        def _():
            both(stash_odd, stash_even, 1)

        @pl.when(s == n_tiles)
        def _():
            last = (n_tiles - 1) % 2
            stash = stash_odd if last else stash_even
            dz = b_dz(stash)
            b_gwo(stash)
            b_small(stash, b_vector(stash, last, dz))
            gwob_ref[...] = gwo_ref[...].astype(BF16)

        @pl.when(s >= 1)
        def _():
            for cp in slab_copies(s - 1, b_slot, False):
                cp.start()

        @pl.when(s == n_tiles)
        def _():
            wait_all(b_slot, False)
            wait_all(a_slot, False)

    ta = lambda s: jnp.minimum(s, n_tiles - 1)
    tb = lambda s: jnp.maximum(s - 1, 0)
    tile = lambda w: pl.BlockSpec((rows, w), lambda s: (ta(s), 0))
    tile_b = lambda w: pl.BlockSpec((rows, w), lambda s: (tb(s), 0))
    ucol = lambda col: pl.BlockSpec((rows, 512), lambda s: (ta(s), col))
    full = lambda shape: pl.BlockSpec(shape, lambda s: (0,) * len(shape))
    stash_shapes = [pltpu.VMEM((rows, D_MODEL), BF16), pltpu.VMEM((rows, D_MODEL), BF16)] + \
        [pltpu.VMEM((rows, 512), F32)] * 5 + [pltpu.VMEM((rows, D_POOL), BF16)]
    return pl.pallas_call(
        body, name="mix_fwd_bwd", grid=(n_tiles + 1,),
        out_shape=(jax.ShapeDtypeStruct((8, 128), F32), jax.ShapeDtypeStruct((seq, D_MODEL), F32),
                   jax.ShapeDtypeStruct((1, D_MODEL), F32), jax.ShapeDtypeStruct((n_blk, Q_BLOCK, D_ATTN), F32),
                   jax.ShapeDtypeStruct((seq, D_ATTN), BF16), jax.ShapeDtypeStruct((seq, D_POOL), BF16),
                   jax.ShapeDtypeStruct((seq, D_POOL), BF16), jax.ShapeDtypeStruct((D_MODEL, D_MODEL), BF16),
                   jax.ShapeDtypeStruct((4, GROUP_DIM, GROUP_DIM), F32), jax.ShapeDtypeStruct((1, D_POOL), F32)),
        in_specs=[tile(D_MODEL), tile(D_MODEL), pl.BlockSpec(memory_space=pl.ANY), ucol(0), ucol(1),
                  pl.BlockSpec((HALO, 512), lambda s: (jnp.maximum(ta(s) * rows8 - 1, 0), 1)),
                  pl.BlockSpec((HALO, 512), lambda s: (jnp.minimum((ta(s) + 1) * rows8, last8), 1)),
                  ucol(2), full((8, D_MODEL)), full((8, D_MODEL)), full((4, GROUP_DIM, GROUP_DIM)), full((1, D_POOL)),
                  full((D_MODEL, D_MODEL))],
        out_specs=(full((8, 128)), tile(D_MODEL), full((1, D_MODEL)), pl.BlockSpec(memory_space=pl.ANY), tile_b(D_ATTN),
                   tile_b(D_POOL), tile_b(D_POOL), full((D_MODEL, D_MODEL)), full((4, GROUP_DIM, GROUP_DIM)),
                   full((1, D_POOL))),
        scratch_shapes=[pltpu.VMEM((2, n_blk, per, D_ATTN), F32), pltpu.VMEM((2, n_blk, per, D_ATTN), F32),
                        pltpu.VMEM((D_MODEL, D_MODEL), F32), *stash_shapes, *stash_shapes,
                        pltpu.SemaphoreType.DMA((2,)), pltpu.SemaphoreType.DMA((2,))],
        compiler_params=pltpu.CompilerParams(dimension_semantics=("arbitrary",), vmem_limit_bytes=VMEM_LIMIT),
    )(x, target, attn.reshape(n_blk, Q_BLOCK, D_ATTN), u, u, u, u, u, modrows, bmodrows, w_pool, pool_scale, w_out)


def _inproj_bwd(x, ctx, modrows, bmodrows, norm_g, w_in_p, q_lora_g, w_uq_p, kv_lora_g, w_ukv, qng_p, kng_p, cos, slo, shi,
                u, dq, dk, dv, dga, dgp, dpl, g1):
    seq = x.shape[0]
    n_tiles = seq // TILE + 1
    rows8 = TILE // HALO
    last8 = seq // HALO - 1

    def body(*refs):
        du_even, du_odd, dh_even, dh_odd = refs[-4:]
        gwin_ref, gwuq_ref, gwukv_ref, dqlg_ref, dkvlg_ref, dqng_ref, dkng_ref, dng_ref, dmod_ref = refs[-13:-4]
        s = pl.program_id(0)
        even = lax.rem(s, 2) == 0

        @pl.when(s == 0)
        def _():
            for ref in (gwin_ref, gwuq_ref, gwukv_ref, dqlg_ref, dkvlg_ref, dqng_ref, dkng_ref, dng_ref, dmod_ref,
                        du_odd, dh_even):
                ref[...] = jnp.zeros_like(ref)

        @pl.when((s < n_tiles) & even)
        def _():
            stages(s, refs[:-4], du_even, du_odd, dh_odd, dh_even, (True, True, True))

        @pl.when((s < n_tiles) & jnp.logical_not(even))
        def _():
            stages(s, refs[:-4], du_odd, du_even, dh_even, dh_odd, (True, True, True))

        for tail, run in ((n_tiles, (False, True, True)), (n_tiles + 1, (False, False, True))):
            @pl.when(s == tail)
            def _():
                if tail % 2 == 0:
                    stages(s, refs[:-4], du_even, du_odd, dh_odd, dh_even, run)
                else:
                    stages(s, refs[:-4], du_odd, du_even, dh_even, dh_odd, run)

    def stages(s, refs, du_ref, du_prev, dh_fill, dh_prev, run):
        (xb_ref, xc_ref, ctx_ref, mod_ref, bmod_ref, ng_ref, win_ref, qlg_ref, wuq_ref, kvlg_ref, wukv_ref, qng_ref, kng_ref,
         cos_ref, slo_ref, shi_ref, cq_ref, ckv_ref, kr_ref, dq_ref, dk_ref, dv_ref, dga_ref, dgp_ref, dpl_ref,
         hb_ref, ha_ref, g1_ref,
         gx_ref, gwin_ref, gwuq_ref, gwukv_ref, dqlg_ref, dkvlg_ref, dqng_ref, dkng_ref, dng_ref, dmod_ref) = refs


        i = s
        is_lat = s > 0
        if run[0]:
            cq = cq_ref[...]
            rq = lax.rsqrt(jnp.mean(cq * cq, axis=-1, keepdims=True) + NORM_EPS)
            qhat = cq * rq
            qnb = (qhat * qlg_ref[...]).astype(BF16)
            q = _dot_nt(qnb, wuq_ref[...])
            ckv = ckv_ref[...]
            rkv = lax.rsqrt(jnp.mean(ckv * ckv, axis=-1, keepdims=True) + NORM_EPS)
            kvhat = ckv * rkv
            kvnb = (kvhat * kvlg_ref[...]).astype(BF16)
            kv = _dot(kvnb, wukv_ref[...])

        if run[1]:
            _, _, _, h2, _ = _modulated_input(s <= 1, xb_ref, ctx_ref, mod_ref, bmod_ref, ng_ref)
            dub = du_prev[...]
            du_g, du_m = dub[:, 0:N_GATES], dub[:, N_GATES:U_PAD]
            h2b = h2.astype(BF16)
            dh_fill[...] = _dot(du_g, win_ref[N_MLA:D_IN_PROJ, :]) + _dot(du_m, win_ref[0:U_PAD - N_GATES, :])
            gwin_ref[N_MLA:D_IN_PROJ, :] += _dot_tn(du_g, h2b)
            gwin_ref[0:N_MLA, :] += _dot_tn(du_m, h2b)[0:N_MLA]

        if run[2]:
            ctx3 = s <= 2
            xt, r, xg, _, scale = _modulated_input(ctx3, xc_ref, ctx_ref, mod_ref, bmod_ref, ng_ref)
            dh = dh_prev[...]
            dshift = jnp.sum(dh, axis=0, keepdims=True)
            dscale = jnp.sum(dh * xg, axis=0, keepdims=True)
            zero = jnp.zeros_like(dshift)
            dmod_ref[0:1, :] += jnp.where(ctx3, zero, dshift)
            dmod_ref[1:2, :] += jnp.where(ctx3, zero, dscale)
            dmod_ref[2:3, :] += jnp.where(ctx3, dshift, zero)
            dmod_ref[3:4, :] += jnp.where(ctx3, dscale, zero)
            dxg = dh * (1.0 + scale)
            xr = xt * r
            dng_ref[...] += jnp.sum(dxg * xr, axis=0, keepdims=True)
            dxn = dxg * ng_ref[...]
            gx_ref[...] = g1_ref[...] + r * (dxn - xr * jnp.mean(dxn * xr, axis=-1, keepdims=True))

        if not run[0]:
            return

        cos_t, slo_t, shi_t = cos_ref[...], slo_ref[...], shi_ref[...]
        qg = qng_ref[...]
        dq_heads = []
        dqng = jnp.zeros((1, D_HEAD_PAD), F32)
        for hd in range(N_HEADS):
            qh = q[:, hd * D_HEAD_PAD:(hd + 1) * D_HEAD_PAD]
            rr = lax.rsqrt(jnp.sum(qh * qh, axis=-1, keepdims=True) * (1.0 / D_HEAD) + NORM_EPS)
            yq = qh * rr
            dpost = jnp.where(is_lat, dq_ref[hd] * ATTN_SCALE, 0.0)
            dyn = jnp.concatenate([dpost[:, 0:D_NOPE], _rope_t(dpost[:, D_NOPE:], cos_t, slo_t, shi_t)], axis=1)
            dqng = dqng + jnp.sum(dyn * yq, axis=0, keepdims=True)
            dyg = dyn * qg
            dq_heads.append(rr * (dyg - yq * (jnp.sum(dyg * yq, axis=-1, keepdims=True) * (1.0 / D_HEAD))))
        dqng_ref[...] += dqng
        dqf = jnp.concatenate(dq_heads, axis=1).astype(BF16)

        krz = kr_ref[...]
        kr_ss = jnp.sum(krz * krz, axis=-1, keepdims=True)
        kg = kng_ref[...]
        kg_n, kg_r = kg[:, 0:D_NOPE], kg[:, D_NOPE:]
        dkv_parts = []
        dkr = jnp.zeros((TILE, 128), F32)
        dkng_n = jnp.zeros((1, D_NOPE), F32)
        dkng_r = jnp.zeros((1, 128), F32)
        for hd in range(N_HEADS):
            kn = kv[:, hd * 256:hd * 256 + D_NOPE]
            rr = lax.rsqrt((jnp.sum(kn * kn, axis=-1, keepdims=True) + kr_ss) * (1.0 / D_HEAD) + NORM_EPS)
            yn, yr = kn * rr, krz * rr
            dk_h = jnp.transpose(dk_ref[hd]) * LN_2
            dpn = dk_h[:, 0:D_NOPE]
            dpr = _rope_t(dk_h[:, D_NOPE:], cos_t, slo_t, shi_t)
            dkng_n = dkng_n + jnp.sum(dpn * yn, axis=0, keepdims=True)
            dkng_r = dkng_r + jnp.sum(dpr * yr, axis=0, keepdims=True)
            dyn, dyr = dpn * kg_n, dpr * kg_r
            proj = (jnp.sum(dyn * yn, axis=-1, keepdims=True) + jnp.sum(dyr * yr, axis=-1, keepdims=True)) * (1.0 / D_HEAD)
            dkv_parts.append(rr * (dyn - yn * proj))
            dkv_parts.append(jnp.transpose(dv_ref[hd]))
            dkr = dkr + rr * (dyr - yr * proj)
        dkng_ref[:, 0:D_NOPE] += dkng_n
        dkng_ref[:, D_NOPE:] += dkng_r
        dkvf = jnp.concatenate(dkv_parts, axis=1).astype(BF16)

        lat_t = jnp.maximum(i - 1, 0)
        dpl_t = dpl_ref[...].astype(F32)
        ds = jnp.concatenate([jnp.where(i > 1, hb_ref[...].astype(F32), 0.0), dpl_t,
                              jnp.where(i < n_tiles - 1, ha_ref[...].astype(F32), 0.0)], axis=0)
        tpos = lat_t * TILE - HALO + lax.broadcasted_iota(jnp.int32, (TILE + 2 * HALO, 1), 0)
        for g, w in enumerate(POOL_WINDOWS):
            sl = slice(g * GROUP_DIM, (g + 1) * GROUP_DIM)
            wsum = _window_sum(ds[:, sl] * _inv_count(tpos, w, seq), w, True)[HALO:HALO + TILE]
            du_ref[:, O_PIN + g * GROUP_DIM:O_PIN + (g + 1) * GROUP_DIM] = jnp.where(
                is_lat, wsum - dpl_t[:, sl], 0.0).astype(BF16)

        du_ref[:, O_GA:O_GA + D_ATTN] = jnp.where(is_lat, dga_ref[...], jnp.zeros((), BF16))
        du_ref[:, O_GP:O_GP + D_POOL] = jnp.where(is_lat, dgp_ref[...], jnp.zeros((), BF16))
        du_ref[:, O_KR:U_PAD] = dkr.astype(BF16)

        gwuq_ref[...] += _dot_tn(dqf, qnb)
        dqn = _dot(dqf, wuq_ref[...])
        gwukv_ref[...] += _dot_tn(dkvf, kvnb)
        dkvn = _dot_nt(dkvf, wukv_ref[...])
        dqlg_ref[...] += jnp.sum(dqn * qhat, axis=0, keepdims=True)
        dqhat = dqn * qlg_ref[...]
        dcq = rq * (dqhat - qhat * jnp.mean(dqhat * qhat, axis=-1, keepdims=True))
        dkvlg_ref[...] += jnp.sum(dkvn * kvhat, axis=0, keepdims=True)
        dkvhat = dkvn * kvlg_ref[...]
        dckv = rkv * (dkvhat - kvhat * jnp.mean(dkvhat * kvhat, axis=-1, keepdims=True))
        du_ref[:, O_CQ:O_CQ + R_Q] = dcq.astype(BF16)
        du_ref[:, O_CKV:O_CKV + R_KV] = dckv.astype(BF16)

    t1 = lambda s: jnp.minimum(s, n_tiles - 1)
    t2 = lambda s: jnp.clip(s - 1, 0, n_tiles - 1)
    t3 = lambda s: jnp.clip(s - 2, 0, n_tiles - 1)
    latent = lambda t: jnp.maximum(t - 1, 0)
    lat = lambda s: (latent(t1(s)), 0)
    lat3 = lambda s: (latent(t3(s)), 0)
    full = lambda shape: pl.BlockSpec(shape, lambda s: (0,) * len(shape))
    rope_spec = pl.BlockSpec((TILE, 128), lambda s: (t1(s), 0))
    return pl.pallas_call(
        body, name="inproj_bwd", grid=(n_tiles + 2,),
        out_shape=(jax.ShapeDtypeStruct((seq, D_MODEL), F32), jax.ShapeDtypeStruct((D_IN_PROJ, D_MODEL), F32),
                   jax.ShapeDtypeStruct((N_HEADS * D_HEAD_PAD, R_Q), F32), jax.ShapeDtypeStruct((N_HEADS * 256, R_KV), F32),
                   jax.ShapeDtypeStruct((1, R_Q), F32), jax.ShapeDtypeStruct((1, R_KV), F32),
                   jax.ShapeDtypeStruct((1, D_HEAD_PAD), F32), jax.ShapeDtypeStruct((1, D_HEAD_PAD), F32),
                   jax.ShapeDtypeStruct((1, D_MODEL), F32), jax.ShapeDtypeStruct((8, D_MODEL), F32)),
        in_specs=[pl.BlockSpec((TILE, D_MODEL), lambda s: (latent(t2(s)), 0)), pl.BlockSpec((TILE, D_MODEL), lat3),
                  full((TILE, D_MODEL)), full((8, D_MODEL)), full((8, D_MODEL)),
                  full((1, D_MODEL)), full((D_IN_PROJ, D_MODEL)), full((1, R_Q)), full((N_HEADS * D_HEAD_PAD, R_Q)),
                  full((1, R_KV)), full((R_KV, N_HEADS * 256)), full((1, D_HEAD_PAD)), full((1, D_HEAD_PAD)),
                  rope_spec, rope_spec, rope_spec,
                  pl.BlockSpec((TILE, R_Q), lambda s: (t1(s), (O_CQ - N_GATES) // R_Q)),
                  pl.BlockSpec((TILE, R_KV), lambda s: (t1(s), (O_CKV - N_GATES) // R_KV)),
                  pl.BlockSpec((TILE, 128), lambda s: (t1(s), (O_KR - N_GATES) // 128)),
                  pl.BlockSpec((N_HEADS, TILE, D_HEAD_PAD), lambda s: (0, latent(t1(s)), 0)),
                  pl.BlockSpec((N_HEADS, D_HEAD_PAD, TILE), lambda s: (0, 0, t1(s))),
                  pl.BlockSpec((N_HEADS, D_V, TILE), lambda s: (0, 0, t1(s))),
                  pl.BlockSpec((TILE, D_ATTN), lat), pl.BlockSpec((TILE, D_POOL), lat), pl.BlockSpec((TILE, D_POOL), lat),
                  pl.BlockSpec((HALO, D_POOL), lambda s: (jnp.maximum((t1(s) - 1) * rows8 - 1, 0), 0)),
                  pl.BlockSpec((HALO, D_POOL), lambda s: (jnp.minimum(jnp.maximum(t1(s), 1) * rows8, last8), 0)),
                  pl.BlockSpec((TILE, D_MODEL), lat3)],
        out_specs=(pl.BlockSpec((TILE, D_MODEL), lat3), full((D_IN_PROJ, D_MODEL)), full((N_HEADS * D_HEAD_PAD, R_Q)),
                   full((N_HEADS * 256, R_KV)), full((1, R_Q)), full((1, R_KV)), full((1, D_HEAD_PAD)),
                   full((1, D_HEAD_PAD)), full((1, D_MODEL)), full((8, D_MODEL))),
        scratch_shapes=[pltpu.VMEM((TILE, U_PAD), BF16), pltpu.VMEM((TILE, U_PAD), BF16),
                        pltpu.VMEM((TILE, D_MODEL), F32), pltpu.VMEM((TILE, D_MODEL), F32)],
        compiler_params=pltpu.CompilerParams(dimension_semantics=("arbitrary",), vmem_limit_bytes=VMEM_LIMIT),
    )(x, x, ctx, modrows, bmodrows, norm_g, w_in_p, q_lora_g, w_uq_p, kv_lora_g, w_ukv, qng_p, kng_p, cos, slo, shi,
      u, u, u, dq, dk, dv, dga, dgp, dpl, dpl, dpl, g1)


SMALL_LAYOUT = ((0, D_MODEL), (1, R_Q), (2, R_KV), (3, D_HEAD_PAD), (4, D_HEAD_PAD), (5, D_POOL))
ROW_DMOD_B, ROW_DMOD_C, ROW_LOSS = 6, 9, 12


def _epilogue(parts, landed, smalls, dmod4, dgate, loss_acc, w_mod_l):
    n_a, n_l, n_s = len(parts), len(landed), len(smalls)
    n_mod = w_mod_l.shape[1]
    blk = [p.shape[1:] for p in parts]
    small_out = [D_MODEL, R_Q, R_KV, D_HEAD, D_HEAD, D_POOL]

    def body(*refs):
        part_refs = refs[0:n_a]
        land_refs = refs[n_a:n_a + n_l]
        small_in = refs[n_a + n_l:n_a + n_l + n_s]
        dmod4_ref, dgate_ref, loss_ref, wmod_ref = refs[n_a + n_l + n_s:n_a + n_l + n_s + 4]
        outs = refs[n_a + n_l + n_s + 4:]
        red_refs = outs[0:n_a]
        landsum_refs = outs[n_a:n_a + n_l]
        ccg_ref = outs[n_a + n_l]
        sum_refs = outs[n_a + n_l + 1:n_a + n_l + 1 + n_s]
        gbmod_ref, dmy_ref, lossout_ref = outs[n_a + n_l + 1 + n_s:n_a + n_l + 4 + n_s]
        scr = outs[n_a + n_l + 4 + n_s:]
        recv1, own1, sum1b, recv2 = scr[0:n_a], scr[n_a:2 * n_a], scr[2 * n_a:3 * n_a], scr[3 * n_a:4 * n_a]
        small_ref, smallg_ref, tot_ref, cc_ref = scr[4 * n_a:4 * n_a + 4]
        land_vmem = scr[4 * n_a + 4:4 * n_a + 4 + n_l]
        ssem1, rsem1, lsem, ssem2, rsem2, ssem_s, rsem_s, ssem_cc, rsem_cc, land_sem = scr[4 * n_a + 4 + n_l:]
        x, y, c = _coords()
        me = (x, y, c)
        sibling = (x, y, 1 - c)

        small_ref[...] = jnp.zeros_like(small_ref)
        for ref, (row, width) in zip(small_in, SMALL_LAYOUT):
            small_ref[row:row + 1, 0:width] = ref[...]
        small_ref[ROW_DMOD_B:ROW_DMOD_B + 2, :] = dmod4_ref[0:2, :]
        small_ref[ROW_DMOD_B + 2:ROW_DMOD_B + 3, :] = dgate_ref[...]
        small_ref[ROW_DMOD_C:ROW_DMOD_C + 2, :] = dmod4_ref[2:4, :]
        small_ref[ROW_LOSS:ROW_LOSS + 1, 0:128] = loss_ref[0:1, :]
        smallg_ref[_lin(me)] = small_ref[...]
        s_recv, s_send = _gather_to_all(small_ref, smallg_ref, ssem_s, rsem_s)

        stage1, own_copies = [], []
        for a in range(n_a):
            for b in range(4):
                dev = 4 * (b >> 1) + 2 * (b & 1)
                stage1.append(pltpu.make_async_remote_copy(
                    src_ref=part_refs[a].at[dev + (1 - c)], dst_ref=recv1[a].at[b],
                    send_sem=ssem1.at[a, b], recv_sem=rsem1.at[a, b], device_id=sibling, device_id_type=MESH))
                own_copies.append(pltpu.make_async_copy(part_refs[a].at[dev + c], own1[a].at[b], lsem.at[a, b]))
                stage1[-1].start()
                own_copies[-1].start()
        land_copies = [pltpu.make_async_copy(land_refs[n], land_vmem[n], land_sem.at[n]) for n in range(n_l)]
        for cp in land_copies:
            cp.start()

        s_recv()
        tot = smallg_ref[0]
        for d in range(1, N_DEV):
            tot = tot + smallg_ref[d]
        tot_ref[...] = tot
        for ref, (row, _), width in zip(sum_refs, SMALL_LAYOUT, small_out):
            ref[...] = tot_ref[row:row + 1, 0:width]
        lossout_ref[...] = tot_ref[8:16, 0:128]
        lin = _lin(me)

        def my_columns(three_rows):
            v = jnp.concatenate(three_rows, axis=1)
            out = jnp.zeros((1, n_mod), F32)
            for d in range(N_DEV):
                out = out + jnp.where(lin == d, v[:, d * n_mod:(d + 1) * n_mod], 0.0)
            return out

        rows3 = lambda ref, r0: [ref[r0 + t:r0 + t + 1, :] for t in range(3)]
        dmodc = rows3(tot_ref, ROW_DMOD_C)
        gbmod_ref[...] = jnp.concatenate(rows3(tot_ref, ROW_DMOD_B), axis=1) + jnp.concatenate(dmodc, axis=1)
        dmy_ref[...] = jnp.zeros_like(dmy_ref)
        for b in range(N_DEV):
            dmy_ref[b:b + 1, :] = my_columns(rows3(smallg_ref.at[b], ROW_DMOD_B))
        dmy = my_columns(dmodc)
        dmy_ref[N_DEV:N_DEV + 1, :] = dmy
        part = _dot_nt(jnp.broadcast_to(dmy, (8, n_mod)).astype(BF16), wmod_ref[...].astype(BF16))

        cc_ref[...] = part
        ccg_ref[_lin(me)] = part
        cc_recv, cc_send = _gather_to_all(cc_ref, ccg_ref, ssem_cc, rsem_cc)

        stage2 = []
        for a in range(n_a):
            for b in range(4):
                stage1[4 * a + b].wait_recv()
                own_copies[4 * a + b].wait()
                sum1b[a][b] = (own1[a][b] + recv1[a][b]).astype(BF16)
            for k in (1, 2, 3):
                tx, ty = _flip(x, (k >> 1) & 1), _flip(y, k & 1)
                stage2.append(pltpu.make_async_remote_copy(
                    src_ref=sum1b[a].at[2 * tx + ty], dst_ref=recv2[a].at[k - 1], send_sem=ssem2.at[a, k - 1],
                    recv_sem=rsem2.at[a, k - 1], device_id=(tx, ty, c), device_id_type=MESH))
                stage2[-1].start()
        for a in range(n_a):
            own = own1[a][2 * x + y] + recv1[a][2 * x + y]
            for k in (1, 2, 3):
                stage2[3 * a + k - 1].wait_recv()
                own = own + recv2[a][k - 1].astype(F32)
            red_refs[a][...] = own

        for cp in land_copies:
            cp.wait()
        for ref, out in zip(land_vmem, landsum_refs):
            acc = ref[0].astype(F32)
            for d in range(1, N_DEV):
                acc = acc + ref[d].astype(F32)
            out[...] = acc
        cc_recv()
        for cp in stage1 + stage2:
            cp.wait_send()
        s_send()
        cc_send()

    vm = pl.BlockSpec(memory_space=pltpu.VMEM)
    sds = jax.ShapeDtypeStruct
    return pl.pallas_call(
        body, name="epilogue_reduce",
        out_shape=(*[sds(s, F32) for s in blk], *[sds(a.shape[1:], F32) for a in landed], sds((N_DEV, 8, D_MODEL), F32),
                   *[sds((1, w), F32) for w in small_out], sds((1, 3 * D_MODEL), F32), sds((16, n_mod), F32),
                   sds((8, 128), F32)),
        in_specs=[pl.BlockSpec(memory_space=pl.ANY)] * (n_a + n_l) + [vm] * (n_s + 4),
        out_specs=tuple([vm] * (n_a + n_l + n_s + 4)),
        scratch_shapes=[*[pltpu.VMEM((4,) + s, F32) for s in blk], *[pltpu.VMEM((4,) + s, F32) for s in blk],
                        *[pltpu.VMEM((4,) + s, BF16) for s in blk], *[pltpu.VMEM((3,) + s, BF16) for s in blk],
                        pltpu.VMEM((SMALL_ROWS, D_MODEL), F32), pltpu.VMEM((N_DEV, SMALL_ROWS, D_MODEL), F32),
                        pltpu.VMEM((SMALL_ROWS, D_MODEL), F32), pltpu.VMEM((8, D_MODEL), F32),
                        *[pltpu.VMEM(a.shape, a.dtype) for a in landed],
                        pltpu.SemaphoreType.DMA((n_a, 4)), pltpu.SemaphoreType.DMA((n_a, 4)), pltpu.SemaphoreType.DMA((n_a, 4)),
                        pltpu.SemaphoreType.DMA((n_a, 3)), pltpu.SemaphoreType.DMA((n_a, 3)),
                        pltpu.SemaphoreType.DMA((7,)), pltpu.SemaphoreType.DMA((7,)),
                        pltpu.SemaphoreType.DMA((7,)), pltpu.SemaphoreType.DMA((7,)), pltpu.SemaphoreType.DMA((n_l,))],
        compiler_params=pltpu.CompilerParams(vmem_limit_bytes=VMEM_LIMIT),
    )(*parts, *landed, *smalls, dmod4, dgate, loss_acc, w_mod_l)


def _adamw(weights, grads, ms, vs, c_all_t, dmod_my, p2g):
    n = len(weights)

    def body(*refs):
        w_refs = refs[0:n]
        g_in = refs[n:2 * n - 2]
        m_refs = refs[2 * n - 2:3 * n - 2]
        v_refs = refs[3 * n - 2:4 * n - 2]
        cat_ref, dmod_ref, p2g_ref = refs[4 * n - 2:4 * n + 1]
        outs = refs[4 * n + 1:]
        g_refs, d_refs, nm_refs, nv_refs = outs[0:n], outs[n:2 * n], outs[2 * n:3 * n], outs[3 * n:4 * n]
        gcc_ref, gwm_ref = g_refs[0], g_refs[1]

        part = p2g_ref[0, 0:1, :]
        for d in range(1, N_DEV):
            part = part + p2g_ref[d, 0:1, :]
        cc = w_refs[0][...]
        sg = _sigmoid(cc)
        gcc_ref[...] = part * (sg * (1.0 + cc * (1.0 - sg)))
        cat = cat_ref[...]
        gwm_ref[...] = lax.dot_general(cat * _sigmoid(cat), dmod_ref[...], (((1,), (0,)), ((), ())), precision=HIGHEST,
                                       preferred_element_type=F32)
        for idx in range(n):
            if idx >= 2:
                g_refs[idx][...] = g_in[idx - 2][...]
            g = g_refs[idx][...]
            m = ADAM_B1 * m_refs[idx][...] + (1.0 - ADAM_B1) * g
            v = ADAM_B2 * v_refs[idx][...] + (1.0 - ADAM_B2) * (g * g)
            m_hat = m / (1.0 - ADAM_B1 ** ADAM_STEP)
            v_hat = v / (1.0 - ADAM_B2 ** ADAM_STEP)
            d_refs[idx][...] = -ADAM_LR * (m_hat / (jnp.sqrt(v_hat) + ADAM_EPS) + ADAM_WD * w_refs[idx][...])
            nm_refs[idx][...] = m
            nv_refs[idx][...] = v

    vm = pl.BlockSpec(memory_space=pltpu.VMEM)
    shapes = [jax.ShapeDtypeStruct(w.shape, F32) for w in weights]
    args = list(weights) + list(grads[2:]) + list(ms) + list(vs) + [c_all_t, dmod_my, p2g]
    out_shape = tuple(shapes * 4)
    return pl.pallas_call(
        body, name="adamw_update", out_shape=out_shape, in_specs=[vm] * len(args), out_specs=tuple([vm] * len(out_shape)),
        compiler_params=pltpu.CompilerParams(vmem_limit_bytes=VMEM_LIMIT),
    )(*args)


def _rope_tables(seq):
    rows = seq // GRID_W
    row = np.repeat(np.arange(rows, dtype=np.float32), GRID_W)
    col = np.tile(np.arange(GRID_W, dtype=np.float32), rows)
    n_freq = D_ROPE // 4
    inv = (np.float32(ROPE_BASE) ** (-np.arange(n_freq, dtype=np.float32) / np.float32(n_freq))).astype(np.float32)
    ang_r = (row[:, None] * inv).astype(np.float32)
    ang_c = (col[:, None] * inv).astype(np.float32)
    ang = np.concatenate([ang_r, ang_r, ang_c, ang_c], axis=-1)
    cos, sin = np.cos(ang).astype(np.float32), np.sin(ang).astype(np.float32)
    low = (np.arange(D_ROPE) % 32) < 16

    def table(t, fill):
        out = np.full((TILE + seq, 128), fill, np.float32)
        out[TILE:, 0:D_ROPE] = t
        return jnp.asarray(out)

    return table(cos, 1.0), table(np.where(low, -sin, 0.0), 0.0), table(np.where(low, 0.0, sin), 0.0)


def kernel(x, c, ctx, c_ctx, w_mod, b_mod, norm_g, w_in, q_lora_g, w_uq, kv_lora_g, w_ukv, q_norm_g, k_norm_g, w_pool, pool_scale, w_out, loss_target, m_c_ctx, m_w_mod, m_b_mod, m_norm_g, m_w_in, m_q_lora_g, m_w_uq, m_kv_lora_g, m_w_ukv, m_q_norm_g, m_k_norm_g, m_w_pool, m_pool_scale, m_w_out, v_c_ctx, v_w_mod, v_b_mod, v_norm_g, v_w_in, v_q_lora_g, v_w_uq, v_kv_lora_g, v_w_ukv, v_q_norm_g, v_k_norm_g, v_w_pool, v_pool_scale, v_w_out):
    seq = x.shape[1]
    assert ctx.shape[1] == TILE and seq % TILE == 0 and seq % GRID_W == 0
    ix, iy, ic = lax.axis_index("x"), lax.axis_index("y"), lax.axis_index("c")
    me = 4 * ix + 2 * iy + ic
    x2, ctx2, tgt2 = x[0], ctx[0], loss_target[0]
    w_mod_l, w_ukv_l, w_out_l = w_mod[0], w_ukv[0], w_out[0]
    c_ctx_row = c_ctx.reshape(1, D_MODEL)

    tr = lambda a: jnp.transpose(a[0])
    w_in_tl, w_uq_tl = tr(w_in), tr(w_uq)
    cg, modg, wg_in, wg_uq, wg_ukv = _prologue(
        c, c_ctx_row, w_mod_l,
        [w_in_tl, w_uq_tl, w_ukv_l])
    mod_all = jnp.transpose(modg, (1, 0, 2)).reshape(16, 3 * D_MODEL)
    mod_b = lax.dynamic_slice_in_dim(mod_all, me, 1, axis=0).reshape(3, D_MODEL)
    mod_c = mod_all[8].reshape(3, D_MODEL)
    modrows = jnp.concatenate([mod_b, mod_c[0:2], jnp.zeros((3, D_MODEL), F32)], axis=0)
    b3 = b_mod.reshape(3, D_MODEL)
    bmodrows = jnp.concatenate([b3, b3[0:2], jnp.zeros((3, D_MODEL), F32)], axis=0)

    w_in_p = wg_in.reshape(D_IN_PROJ, D_MODEL)
    w_uq_p = jnp.concatenate([wg_uq.reshape(N_HEADS, D_HEAD, R_Q), jnp.zeros((N_HEADS, D_HEAD_PAD - D_HEAD, R_Q), BF16)],
                             axis=1).reshape(N_HEADS * D_HEAD_PAD, R_Q)
    w_ukv_f = jnp.transpose(wg_ukv, (1, 0, 2)).reshape(R_KV, N_HEADS * 256)
    qng_p = jnp.concatenate([q_norm_g, jnp.zeros((1, D_HEAD_PAD - D_HEAD), F32)], axis=1)
    kng_p = jnp.concatenate([k_norm_g, jnp.zeros((1, D_HEAD_PAD - D_HEAD), F32)], axis=1)
    cos, slo, shi = _rope_tables(seq)

    u_gates, u_mla, q, k, v = _inproj_fwd(x2, ctx2, modrows, bmodrows, norm_g, w_in_p, q_lora_g, w_uq_p, kv_lora_g, w_ukv_f,
                             qng_p, kng_p, cos, slo, shi)
    attn, lse, wg_out = _attn_fwd(q, k, v, min(2048, seq), w_out_l.astype(BF16))
    (loss_acc, g1, dgate, d_attn, dga, dgp, dpl, gwo_b, gwp, dps) = _mix(
        x2, tgt2, attn, u_gates, modrows, bmodrows, w_pool[0], pool_scale, wg_out.reshape(D_MODEL, D_MODEL))

    blocks = lambda a: a.reshape((N_DEV, a.shape[0] // N_DEV) + a.shape[1:])
    dq, dk, dv, land_wo, land_wp = _attn_bwd(q, k, v, attn, lse, d_attn.reshape(seq, D_ATTN), min(1024, seq), blocks(gwo_b),
                                             gwp.reshape(4 * GROUP_DIM, GROUP_DIM))
    (grad_x, gwin_t, gwuq_p, gwukv_t, dqlg, dkvlg, dqng, dkng, dng, dmod4) = _inproj_bwd(
        x2, ctx2, modrows, bmodrows, norm_g, w_in_p, q_lora_g, w_uq_p, kv_lora_g, w_ukv_f, qng_p, kng_p, cos, slo, shi,
        u_mla, dq, dk, dv, dga, dgp, dpl, g1)

    gwuq_t = gwuq_p.reshape(N_HEADS, D_HEAD_PAD, R_Q)[:, 0:D_HEAD].reshape(N_HEADS * D_HEAD, R_Q)
    parts = [blocks(gwin_t), blocks(gwuq_t), blocks(gwukv_t)]
    (r_win, r_wuq, r_wukv, r_wout, g_w_pool, ccg, g_ng, g_qlg, g_kvlg, g_qng, g_kng, g_ps, g_bmod, dmod_my,
     loss_rows) = _epilogue(parts, [land_wo, land_wp], [dng, dqlg, dkvlg, dqng, dkng, dps], dmod4, dgate, loss_acc, w_mod_l)

    g_w_ukv = jnp.transpose(r_wukv)
    c_rows = cg[:, 0, :]
    c_all_t = jnp.transpose(jnp.concatenate([c_rows, c_ctx_row, jnp.zeros((16 - N_DEV - 1, D_MODEL), F32)], axis=0))

    weights = [c_ctx_row, w_mod_l, b_mod, norm_g, w_in_tl, q_lora_g, w_uq_tl, kv_lora_g, w_ukv_l, q_norm_g, k_norm_g,
               w_pool.reshape(4 * GROUP_DIM, GROUP_DIM), pool_scale, w_out_l]
    grads = [None, None, g_bmod, g_ng, r_win, g_qlg, r_wuq, g_kvlg, g_w_ukv, g_qng, g_kng, g_w_pool, g_ps, r_wout]
    ms = [m_c_ctx.reshape(1, D_MODEL), m_w_mod[0], m_b_mod, m_norm_g, tr(m_w_in), m_q_lora_g, tr(m_w_uq), m_kv_lora_g,
          m_w_ukv[0], m_q_norm_g, m_k_norm_g, m_w_pool.reshape(4 * GROUP_DIM, GROUP_DIM), m_pool_scale, m_w_out[0]]
    vs = [v_c_ctx.reshape(1, D_MODEL), v_w_mod[0], v_b_mod, v_norm_g, tr(v_w_in), v_q_lora_g, tr(v_w_uq), v_kv_lora_g,
          v_w_ukv[0], v_q_norm_g, v_k_norm_g, v_w_pool.reshape(4 * GROUP_DIM, GROUP_DIM), v_pool_scale, v_w_out[0]]
    outs = _adamw(weights, grads, ms, vs, c_all_t, dmod_my, ccg)
    n = len(weights)
    grads, deltas, new_m, new_v = outs[0:n], outs[n:2 * n], outs[2 * n:3 * n], outs[3 * n:4 * n]

    loss = loss_rows[ROW_LOSS - 8, 0]
    final = [c_ctx.shape, w_mod.shape, b_mod.shape, norm_g.shape, w_in.shape, q_lora_g.shape, w_uq.shape, kv_lora_g.shape,
             w_ukv.shape, q_norm_g.shape, k_norm_g.shape, w_pool.shape, pool_scale.shape, w_out.shape]
    transposed = (4, 6)

    def shaped(arrs):
        return [(jnp.transpose(a) if i in transposed else a).reshape(s) for i, (a, s) in enumerate(zip(arrs, final))]

    return (loss, grad_x[None], *shaped(grads), *shaped(deltas), *shaped(new_m), *shaped(new_v))
```

```python
import numpy as np

import jax
import jax.numpy as jnp
from jax import lax
from jax.experimental import pallas as pl
from jax.experimental.pallas import tpu as pltpu

F32 = jnp.float32
BF16 = jnp.bfloat16
MESH = pl.DeviceIdType.MESH
HIGHEST = lax.Precision.HIGHEST

D_MODEL = 1024
N_HEADS = 4
D_NOPE = 128
D_ROPE = 64
D_HEAD = D_NOPE + D_ROPE
D_HEAD_PAD = 256
D_V = 128
R_Q = 256
R_KV = 128
D_ATTN = 512
D_POOL = 512
POOL_WINDOWS = (2, 4, 8, 16)
GROUP_DIM = 128
GRID_W = 64
ROPE_BASE = 10000.0
NORM_EPS = 1e-6
ATTN_SCALE = D_HEAD ** -0.5
LOG2_E = 1.4426950408889634
LN_2 = 0.6931471805599453
ATTN_ROWS = 256
D_IN_PROJ = 1984
U_PAD = 2048
O_GA, O_PIN, O_GP, O_CQ, O_CKV, O_KR = 0, 512, 1024, 1536, 1792, 1920
TILE = 256
MIX_TILE = 512
Q_BLOCK = 128
HALO = 16
N_GATES = 1536
N_MLA = D_IN_PROJ - N_GATES
N_DEV = 8
VMEM_LIMIT = 56 * 1024 * 1024

ADAM_LR = 0.001
ADAM_B1 = 0.9
ADAM_B2 = 0.999
ADAM_EPS = 1e-08
ADAM_WD = 0.01
ADAM_STEP = 10

SMALL_ROWS = 16


def _dot(a, b):
    return lax.dot_general(a, b, (((1,), (0,)), ((), ())), preferred_element_type=F32)


def _dot_nt(a, b):
    return lax.dot_general(a, b, (((1,), (1,)), ((), ())), preferred_element_type=F32)


def _dot_tn(a, b):
    return lax.dot_general(a, b, (((0,), (0,)), ((), ())), preferred_element_type=F32)


def _sigmoid(x):
    return 1.0 / (1.0 + jnp.exp(-x))


def _rope(p, cos, slo, shi):
    return p * cos + pltpu.roll(p, 112, 1) * slo + pltpu.roll(p, 16, 1) * shi


def _rope_t(d, cos, slo, shi):
    return d * cos + pltpu.roll(d * slo, 16, 1) + pltpu.roll(d * shi, 112, 1)


def _shift_rows(a, k):
    n = a.shape[0]
    return pltpu.roll(a, (-k) % n, 0)


def _window_sum(x, w, transposed):
    s = (x + _shift_rows(x, 1)) if transposed else (_shift_rows(x, -1) + x)
    step = 1
    while 2 * step < w:
        s = _shift_rows(s, -step) + _shift_rows(s, step)
        step *= 2
    return s


def _inv_count(tpos, w, seq):
    lo = jnp.maximum(tpos - w // 2, 0)
    hi = jnp.minimum(tpos - w // 2 + w, seq)
    return 1.0 / jnp.maximum(hi - lo, 1).astype(F32)


def _coords():
    return lax.axis_index("x"), lax.axis_index("y"), lax.axis_index("c")


def _flip(v, bit):
    return (1 - v) if bit else v


def _peer(k):
    x, y, c = _coords()
    return (_flip(x, (k >> 2) & 1), _flip(y, (k >> 1) & 1), _flip(c, k & 1))


def _lin(p):
    return 4 * p[0] + 2 * p[1] + p[2]


def _gather_to_all(src_ref, slots_ref, send_sems, recv_sems):
    me = _coords()
    copies = []
    for k in range(1, N_DEV):
        cp = pltpu.make_async_remote_copy(
            src_ref=src_ref, dst_ref=slots_ref.at[_lin(me)], send_sem=send_sems.at[k - 1], recv_sem=recv_sems.at[k - 1],
            device_id=_peer(k), device_id_type=MESH)
        cp.start()
        copies.append(cp)

    def wait_recv():
        for k in range(1, N_DEV):
            pltpu.make_async_remote_copy(
                src_ref=src_ref, dst_ref=slots_ref.at[_lin(_peer(k))], send_sem=send_sems.at[k - 1],
                recv_sem=recv_sems.at[k - 1], device_id=_peer(k), device_id_type=MESH).wait_recv()

    def wait_send():
        for cp in copies:
            cp.wait_send()

    return wait_recv, wait_send


def _prologue(c8, cctx8, w_mod_l, shards):
    n_mod = w_mod_l.shape[1]
    n_w = len(shards)

    def body(c_ref, cctx_ref, wmod_ref, *refs):
        w_refs = refs[0:n_w]
        cg_ref, modg_ref = refs[n_w], refs[n_w + 1]
        wg_refs = refs[n_w + 2:2 * n_w + 2]
        modblk_ref = refs[2 * n_w + 2]
        wb_refs = refs[2 * n_w + 3:3 * n_w + 3]
        c8_ref = refs[3 * n_w + 3]
        ssem_w, rsem_w, lsem_w, ssem_c, rsem_c, ssem_m, rsem_m = refs[3 * n_w + 4:]
        x, y, c = _coords()
        me = (x, y, c)
        sibling = (x, y, 1 - c)
        chips = [(1 - x, y), (x, 1 - y), (1 - x, 1 - y)]

        def wcopies(k, block, to, own=False):
            out = []
            for a in range(n_w):
                slot = wg_refs[a].at[_lin(block)]
                out.append(pltpu.make_async_remote_copy(
                    src_ref=wb_refs[a] if own else slot, dst_ref=slot, send_sem=ssem_w.at[a, k], recv_sem=rsem_w.at[a, k],
                    device_id=to, device_id_type=MESH))
            return out

        c8_ref[...] = jnp.broadcast_to(c_ref[...], (8, D_MODEL))
        cg_ref[_lin(me)] = c8_ref[...]
        c_recv, c_send = _gather_to_all(c8_ref, cg_ref, ssem_c, rsem_c)

        for a in range(n_w):
            wb_refs[a][...] = w_refs[a][...].astype(BF16)
        first = wcopies(0, me, sibling, own=True)
        for j, chip in enumerate(chips):
            first += wcopies(1 + j, me, (*chip, c), own=True)
        late = [idx for idx in range(len(first)) if idx % n_w == 0 and idx >= n_w]
        for idx, cp in enumerate(first):
            if idx not in late:
                cp.start()
        mine = [pltpu.make_async_copy(wb_refs[a], wg_refs[a].at[_lin(me)], lsem_w.at[a]) for a in range(n_w)]
        for cp in mine:
            cp.start()

        c_recv()
        row = lax.broadcasted_iota(jnp.int32, (8, D_MODEL), 0)
        c_all = jnp.zeros((8, D_MODEL), F32)
        for d in range(N_DEV):
            c_all = c_all + jnp.where(row == d, cg_ref[d], 0.0)
        cc = jnp.broadcast_to(cctx_ref[...], (8, D_MODEL))
        a = jnp.concatenate([c_all * _sigmoid(c_all), cc * _sigmoid(cc)], axis=0)
        modblk_ref[...] = _dot(a.astype(BF16), wmod_ref[...].astype(BF16))
        modg_ref[_lin(me)] = modblk_ref[...]
        m_recv, m_send = _gather_to_all(modblk_ref, modg_ref, ssem_m, rsem_m)
        for idx in late:
            first[idx].start()
        m_recv()

        passed = []
        for j, chip in enumerate(chips):
            for cp in wcopies(1 + j, (*chip, c), me):
                cp.wait_recv()
            fwd = wcopies(4 + j, (*chip, c), sibling)
            for cp in fwd:
                cp.start()
            passed += fwd
        for cp in wcopies(0, sibling, me):
            cp.wait_recv()
        for j, chip in enumerate(chips):
            for cp in wcopies(4 + j, (*chip, 1 - c), me):
                cp.wait_recv()
        for cp in first + passed:
            cp.wait_send()
        for cp in mine:
            cp.wait()
        c_send()
        m_send()

    vm = pl.BlockSpec(memory_space=pltpu.VMEM)
    hbm = pl.BlockSpec(memory_space=pl.ANY)
    return pl.pallas_call(
        body, name="prologue_gather",
        out_shape=(jax.ShapeDtypeStruct((N_DEV, 8, D_MODEL), F32), jax.ShapeDtypeStruct((N_DEV, 16, n_mod), F32),
                   *[jax.ShapeDtypeStruct((N_DEV,) + s.shape, BF16) for s in shards]),
        in_specs=[vm] * (3 + n_w), out_specs=(vm, vm, *[hbm] * n_w),
        scratch_shapes=[pltpu.VMEM((16, n_mod), F32), *[pltpu.VMEM(s.shape, BF16) for s in shards],
                        pltpu.VMEM((8, D_MODEL), F32),
                        pltpu.SemaphoreType.DMA((n_w, 7)), pltpu.SemaphoreType.DMA((n_w, 7)), pltpu.SemaphoreType.DMA((n_w,)),
                        pltpu.SemaphoreType.DMA((7,)), pltpu.SemaphoreType.DMA((7,)),
                        pltpu.SemaphoreType.DMA((7,)), pltpu.SemaphoreType.DMA((7,))],
        compiler_params=pltpu.CompilerParams(vmem_limit_bytes=VMEM_LIMIT),
    )(c8, cctx8, w_mod_l, *shards)


def _modulated_input(is_ctx, x_ref, ctx_ref, mod_ref, bmod_ref, ng_ref):
    xt = jnp.where(is_ctx, ctx_ref[...], x_ref[...])
    shift = jnp.where(is_ctx, mod_ref[3:4, :] + bmod_ref[3:4, :], mod_ref[0:1, :] + bmod_ref[0:1, :])
    scale = jnp.where(is_ctx, mod_ref[4:5, :] + bmod_ref[4:5, :], mod_ref[1:2, :] + bmod_ref[1:2, :])
    r = lax.rsqrt(jnp.mean(xt * xt, axis=-1, keepdims=True) + NORM_EPS)
    xg = (xt * r) * ng_ref[...]
    h = xg * (1.0 + scale) + shift
    return xt, r, xg, h, scale


def _inproj_fwd(x, ctx, modrows, bmodrows, norm_g, w_in_p, q_lora_g, w_uq_p, kv_lora_g, w_ukv, qng_p, kng_p, cos, slo, shi):
    seq = x.shape[0]
    n_tiles = seq // TILE + 1
    tot = seq + TILE

    n_mla = R_Q + R_KV + 128

    def body(x_ref, ctx_ref, mod_ref, bmod_ref, ng_ref, win_ref, qlg_ref, wuq_ref, kvlg_ref, wukv_ref, qng_ref, kng_ref,
             cos_ref, slo_ref, shi_ref, ug_ref, um_ref, q_ref, k_ref, v_ref, stash):
        s = pl.program_id(0)

        @pl.when(s == 0)
        def _():
            stash[...] = jnp.zeros_like(stash)

        refs = (x_ref, ctx_ref, mod_ref, bmod_ref, ng_ref, win_ref, qlg_ref, wuq_ref, kvlg_ref, wukv_ref, qng_ref, kng_ref,
                cos_ref, slo_ref, shi_ref, ug_ref, um_ref, q_ref, k_ref, v_ref)
        stages(s, refs, stash, stash)

    def stages(s, refs, fill, prev_ref):
        (x_ref, ctx_ref, mod_ref, bmod_ref, ng_ref, win_ref, qlg_ref, wuq_ref, kvlg_ref, wukv_ref, qng_ref, kng_ref,
         cos_ref, slo_ref, shi_ref, ug_ref, um_ref, q_ref, k_ref, v_ref) = refs
        cos_t, slo_t, shi_t = cos_ref[...], slo_ref[...], shi_ref[...]
        prev = prev_ref[...]
        cq = prev[:, 0:R_Q]
        qn = cq * lax.rsqrt(jnp.mean(cq * cq, axis=-1, keepdims=True) + NORM_EPS) * qlg_ref[...]
        q = _dot_nt(qn.astype(BF16), wuq_ref[...])
        qg = qng_ref[...] * (ATTN_SCALE * LOG2_E)
        for hd in range(N_HEADS):
            qh = q[:, hd * D_HEAD_PAD:(hd + 1) * D_HEAD_PAD]
            rr = lax.rsqrt(jnp.sum(qh * qh, axis=-1, keepdims=True) * (1.0 / D_HEAD) + NORM_EPS)
            qy = qh * rr * qg
            q_ref[hd, :, 0:D_NOPE] = qy[:, 0:D_NOPE].astype(BF16)
            q_ref[hd, :, D_NOPE:D_HEAD_PAD] = _rope(qy[:, D_NOPE:D_HEAD_PAD], cos_t, slo_t, shi_t).astype(BF16)

        ckv = prev[:, R_Q:R_Q + R_KV]
        kvn = ckv * lax.rsqrt(jnp.mean(ckv * ckv, axis=-1, keepdims=True) + NORM_EPS) * kvlg_ref[...]
        kv = _dot(kvn.astype(BF16), wukv_ref[...])
        krz = prev[:, R_Q + R_KV:n_mla]
        kr_ss = jnp.sum(krz * krz, axis=-1, keepdims=True)
        kg = kng_ref[...]
        for hd in range(N_HEADS):
            kn = kv[:, hd * 256:hd * 256 + D_NOPE]
            rr = lax.rsqrt((jnp.sum(kn * kn, axis=-1, keepdims=True) + kr_ss) * (1.0 / D_HEAD) + NORM_EPS)
            k_ref[hd, :, 0:D_NOPE] = (kn * rr * kg[:, 0:D_NOPE]).astype(BF16)
            k_ref[hd, :, D_NOPE:D_HEAD_PAD] = _rope(krz * rr * kg[:, D_NOPE:D_HEAD_PAD], cos_t, slo_t, shi_t).astype(BF16)
            v_ref[hd] = kv[:, hd * 256 + D_NOPE:(hd + 1) * 256].astype(BF16)

        _, _, _, h, _ = _modulated_input(s == 0, x_ref, ctx_ref, mod_ref, bmod_ref, ng_ref)
        hb = h.astype(BF16)
        ug_ref[...] = _dot_nt(hb, win_ref[N_MLA:D_IN_PROJ, :]).astype(BF16)
        um = _dot_nt(hb, win_ref[0:n_mla, :])
        lane = lax.broadcasted_iota(jnp.int32, (TILE, 128), 1)
        um = jnp.concatenate([um[:, 0:R_Q + R_KV], jnp.where(lane < D_ROPE, um[:, R_Q + R_KV:n_mla], 0.0)], axis=1)
        um_ref[...] = um
        fill[...] = um

    first = lambda s: jnp.minimum(s, n_tiles - 1)
    second = lambda s: jnp.maximum(s - 1, 0)
    latent = lambda t: jnp.maximum(t - 1, 0)
    full = lambda shape: pl.BlockSpec(shape, lambda s: (0,) * len(shape))
    rope_spec = pl.BlockSpec((TILE, 128), lambda s: (second(s), 0))
    return pl.pallas_call(
        body, name="inproj_fwd", grid=(n_tiles + 1,),
        out_shape=(jax.ShapeDtypeStruct((seq, N_GATES), BF16), jax.ShapeDtypeStruct((tot, n_mla), F32),
                   jax.ShapeDtypeStruct((N_HEADS, seq, D_HEAD_PAD), BF16),
                   jax.ShapeDtypeStruct((N_HEADS, tot, D_HEAD_PAD), BF16),
                   jax.ShapeDtypeStruct((N_HEADS, tot, D_V), BF16)),
        in_specs=[pl.BlockSpec((TILE, D_MODEL), lambda s: (latent(first(s)), 0)), full((TILE, D_MODEL)), full((8, D_MODEL)),
                  full((8, D_MODEL)), full((1, D_MODEL)), full((D_IN_PROJ, D_MODEL)), full((1, R_Q)),
                  full((N_HEADS * D_HEAD_PAD, R_Q)), full((1, R_KV)), full((R_KV, N_HEADS * 256)), full((1, D_HEAD_PAD)),
                  full((1, D_HEAD_PAD)), rope_spec, rope_spec, rope_spec],
        out_specs=(pl.BlockSpec((TILE, N_GATES), lambda s: (latent(first(s)), 0)),
                   pl.BlockSpec((TILE, n_mla), lambda s: (first(s), 0)),
                   pl.BlockSpec((N_HEADS, TILE, D_HEAD_PAD), lambda s: (0, latent(second(s)), 0)),
                   pl.BlockSpec((N_HEADS, TILE, D_HEAD_PAD), lambda s: (0, second(s), 0)),
                   pl.BlockSpec((N_HEADS, TILE, D_V), lambda s: (0, second(s), 0))),
        scratch_shapes=[pltpu.VMEM((TILE, n_mla), F32)],
        compiler_params=pltpu.CompilerParams(dimension_semantics=("arbitrary",), vmem_limit_bytes=VMEM_LIMIT),
    )(x, ctx, modrows, bmodrows, norm_g, w_in_p, q_lora_g, w_uq_p, kv_lora_g, w_ukv, qng_p, kng_p, cos, slo, shi)


def _hosted_exchange(first, last, items, send_sems, recv_sems, local_sems):
    me = _lin(_coords())

    def remote(n, k, src, land, scatter):
        peer = _peer(k)
        return pltpu.make_async_remote_copy(
            src_ref=src.at[_lin(peer)] if scatter else src, dst_ref=land.at[me], send_sem=send_sems.at[n, k - 1],
            recv_sem=recv_sems.at[n, k - 1], device_id=peer, device_id_type=MESH)

    def local(n, src, land, scatter):
        return pltpu.make_async_copy(src.at[me] if scatter else src, land.at[me], local_sems.at[n])

    @pl.when(first)
    def _():
        for n, (src, land, scatter) in enumerate(items):
            for k in range(1, N_DEV):
                remote(n, k, src, land, scatter).start()
            local(n, src, land, scatter).start()

    @pl.when(last)
    def _():
        for n, (src, land, scatter) in enumerate(items):
            for k in range(1, N_DEV):
                peer = _peer(k)
                pltpu.make_async_remote_copy(
                    src_ref=src.at[0] if scatter else src, dst_ref=land.at[_lin(peer)], send_sem=send_sems.at[n, k - 1],
                    recv_sem=recv_sems.at[n, k - 1], device_id=peer, device_id_type=MESH).wait_recv()
            for k in range(1, N_DEV):
                remote(n, k, src, land, scatter).wait_send()
            local(n, src, land, scatter).wait()


def _attn_fwd(q, k, v, block_q, w_out_b):
    _, seq, _ = q.shape
    n_keys = k.shape[1]
    n_q = seq // block_q

    def body(q_ref, k_ref, v_ref, wo_ref, o_ref, lse_ref, wog_ref, ssem, rsem, lsem):
        h, i = pl.program_id(0), pl.program_id(1)
        _hosted_exchange((h == 0) & (i == 0), (h == N_HEADS - 1) & (i == n_q - 1), [(wo_ref, wog_ref, False)],
                         ssem, rsem, lsem)
        kb, vb = k_ref[0], v_ref[0]
        for r0 in range(0, block_q, ATTN_ROWS):
            rows = slice(r0, r0 + ATTN_ROWS)
            s = _dot_nt(q_ref[0, rows, :], kb)
            m = jnp.max(s, axis=-1, keepdims=True)
            p = jnp.exp2(s - m)
            l = jnp.sum(p, axis=-1, keepdims=True)
            o_ref[rows, :] = _dot(p.astype(BF16), vb) * (1.0 / l)
            lse_ref[0, rows, :] = m + jnp.log2(l)

    hbm = pl.BlockSpec(memory_space=pl.ANY)
    return pl.pallas_call(
        body, name="attn_fwd", grid=(N_HEADS, n_q),
        out_shape=(jax.ShapeDtypeStruct((seq, D_ATTN), F32), jax.ShapeDtypeStruct((N_HEADS, seq, 1), F32),
                   jax.ShapeDtypeStruct((N_DEV,) + w_out_b.shape, w_out_b.dtype)),
        in_specs=[pl.BlockSpec((1, block_q, D_HEAD_PAD), lambda h, i: (h, i, 0)),
                  pl.BlockSpec((1, n_keys, D_HEAD_PAD), lambda h, i: (h, 0, 0)),
                  pl.BlockSpec((1, n_keys, D_V), lambda h, i: (h, 0, 0)), hbm],
        out_specs=(pl.BlockSpec((block_q, D_V), lambda h, i: (i, h)),
                   pl.BlockSpec((1, block_q, 1), lambda h, i: (h, i, 0)), hbm),
        scratch_shapes=[pltpu.SemaphoreType.DMA((1, 7)), pltpu.SemaphoreType.DMA((1, 7)), pltpu.SemaphoreType.DMA((1,))],
        compiler_params=pltpu.CompilerParams(dimension_semantics=("arbitrary", "arbitrary"), vmem_limit_bytes=VMEM_LIMIT),
    )(q, k, v, w_out_b)


def _attn_bwd(q, k, v, o, lse, d_o, block_q, gwo_blocks, gwp):
    _, seq, _ = q.shape
    n_keys = k.shape[1]
    n_q = seq // block_q

    def body(q_ref, k_ref, v_ref, o_ref, lse_ref, do_ref, gwo_ref, gwp_ref, dq_ref, dkt_ref, dvt_ref, lwo_ref, lwp_ref,
             ssem, rsem, lsem):
        h, i = pl.program_id(0), pl.program_id(1)
        _hosted_exchange((h == 0) & (i == 0), (h == N_HEADS - 1) & (i == n_q - 1),
                         [(gwo_ref, lwo_ref, True), (gwp_ref, lwp_ref, False)], ssem, rsem, lsem)

        @pl.when(i == 0)
        def _():
            dkt_ref[...] = jnp.zeros_like(dkt_ref)
            dvt_ref[...] = jnp.zeros_like(dvt_ref)

        kb, vb = k_ref[0], v_ref[0]
        raws, ps = [], []
        for r0 in range(0, block_q, ATTN_ROWS):
            rows = slice(r0, r0 + ATTN_ROWS)
            d_out = do_ref[rows, :]
            p = jnp.exp2(_dot_nt(q_ref[0, rows, :], kb) - lse_ref[0, rows, :])
            dp = _dot_nt(d_out.astype(BF16), vb)
            delta = jnp.sum(d_out * o_ref[rows, :], axis=-1, keepdims=True)
            raw = (p * (dp - delta)).astype(BF16)
            dq_ref[0, rows, :] = _dot(raw, kb)
            raws.append(raw)
            ps.append(p.astype(BF16))
        dkt_ref[0] += _dot_tn(q_ref[0], jnp.concatenate(raws, axis=0))
        dvt_ref[0] += _dot_tn(do_ref[...].astype(BF16), jnp.concatenate(ps, axis=0))

    hbm = pl.BlockSpec(memory_space=pl.ANY)
    return pl.pallas_call(
        body, name="attn_bwd", grid=(N_HEADS, n_q),
        out_shape=(jax.ShapeDtypeStruct((N_HEADS, seq, D_HEAD_PAD), F32),
                   jax.ShapeDtypeStruct((N_HEADS, D_HEAD_PAD, n_keys), F32),
                   jax.ShapeDtypeStruct((N_HEADS, D_V, n_keys), F32),
                   jax.ShapeDtypeStruct(gwo_blocks.shape, gwo_blocks.dtype),
                   jax.ShapeDtypeStruct((N_DEV,) + gwp.shape, gwp.dtype)),
        in_specs=[pl.BlockSpec((1, block_q, D_HEAD_PAD), lambda h, i: (h, i, 0)),
                  pl.BlockSpec((1, n_keys, D_HEAD_PAD), lambda h, i: (h, 0, 0)),
                  pl.BlockSpec((1, n_keys, D_V), lambda h, i: (h, 0, 0)),
                  pl.BlockSpec((block_q, D_V), lambda h, i: (i, h)),
                  pl.BlockSpec((1, block_q, 1), lambda h, i: (h, i, 0)),
                  pl.BlockSpec((block_q, D_V), lambda h, i: (i, h)), hbm, hbm],
        out_specs=(pl.BlockSpec((1, block_q, D_HEAD_PAD), lambda h, i: (h, i, 0)),
                   pl.BlockSpec((1, D_HEAD_PAD, n_keys), lambda h, i: (h, 0, 0)),
                   pl.BlockSpec((1, D_V, n_keys), lambda h, i: (h, 0, 0)), hbm, hbm),
        scratch_shapes=[pltpu.SemaphoreType.DMA((2, 7)), pltpu.SemaphoreType.DMA((2, 7)), pltpu.SemaphoreType.DMA((2,))],
        compiler_params=pltpu.CompilerParams(dimension_semantics=("arbitrary", "arbitrary"), vmem_limit_bytes=VMEM_LIMIT),
    )(q, k, v, o, lse, d_o, gwo_blocks, gwp)


def _mix(x, target, attn, u, modrows, bmodrows, w_pool, pool_scale, w_out):
    seq = x.shape[0]
    rows = min(MIX_TILE, seq // 2)
    n_tiles = seq // rows
    rows8 = rows // HALO
    last8 = seq // HALO - 1

    n_blk = seq // Q_BLOCK
    per = rows // n_blk
    assert rows % n_blk == 0 and per % 8 == 0 and n_tiles >= 2
    n_stash = 8

    def body(x_ref, t_ref, o_hbm, ga_ref, pin_ref, hb_ref, ha_ref, gp_ref, mod_ref, bmod_ref, wp_ref, ps_ref, wo_ref,
             loss_ref, g1_ref, dgate_ref, do_hbm, dga_ref, dgp_ref, dpl_ref, gwob_ref, gwp_ref, dps_ref,
             abuf, dbuf, gwo_ref, *rest):
        stash_even, stash_odd = rest[0:n_stash], rest[n_stash:2 * n_stash]
        rsem, wsem = rest[2 * n_stash:]
        s = pl.program_id(0)

        def slab_copies(tile, slot, fetch):
            window = pl.ds(tile * per, per)
            if fetch:
                return [pltpu.make_async_copy(o_hbm.at[:, window, :], abuf.at[slot], rsem.at[slot])]
            return [pltpu.make_async_copy(dbuf.at[slot], do_hbm.at[:, window, :], wsem.at[slot])]

        def wait_all(slot, fetch):
            slab_copies(0, slot, fetch)[0].wait()

        a_slot = lax.rem(s, 2)
        b_slot = 1 - a_slot

        @pl.when(s == 0)
        def _():
            loss_ref[...] = jnp.zeros_like(loss_ref)
            dgate_ref[...] = jnp.zeros_like(dgate_ref)
            gwo_ref[...] = jnp.zeros_like(gwo_ref)
            gwp_ref[...] = jnp.zeros_like(gwp_ref)
            dps_ref[...] = jnp.zeros_like(dps_ref)
            for cp in slab_copies(0, 0, True):
                cp.start()

        @pl.when(s + 1 < n_tiles)
        def _():
            for cp in slab_copies(s + 1, b_slot, True):
                cp.start()

        @pl.when(s < n_tiles)
        def _():
            wait_all(a_slot, True)

        @pl.when(s >= 3)
        def _():
            wait_all(b_slot, False)

        gate = mod_ref[2:3, :] + bmod_ref[2:3, :]
        ps = ps_ref[...]

        def a_vector(slot):
            attn_t = jnp.swapaxes(abuf[slot], 0, 1).reshape(rows, D_ATTN)
            ga = ga_ref[...].astype(F32)
            sga = _sigmoid(ga)
            silu_ga = ga * sga
            pin = pin_ref[...].astype(F32)
            xs = jnp.concatenate([jnp.where(s > 0, hb_ref[...].astype(F32), 0.0), pin,
                                  jnp.where(s < n_tiles - 1, ha_ref[...].astype(F32), 0.0)], axis=0)
            tpos = s * rows + lax.broadcasted_iota(jnp.int32, (rows, 1), 0)
            pooled = []
            for g, w in enumerate(POOL_WINDOWS):
                sl = slice(g * GROUP_DIM, (g + 1) * GROUP_DIM)
                ws = _window_sum(xs[:, sl], w, False)[HALO:HALO + rows]
                pooled.append((ws * _inv_count(tpos, w, seq) - pin[:, sl]).astype(BF16))
            return attn_t, ga, sga, silu_ga, pooled

        def a_group_maps(pooled):
            return jnp.concatenate([_dot(pooled[g], wp_ref[g].astype(BF16)) for g in range(len(POOL_WINDOWS))], axis=1)

        def a_project(stash, yp, attn_t, ga, sga, silu_ga, pooled):
            zb_ref, _, fa_ref, sa_ref, fp_ref, sp_ref, yp_ref, pooled_ref = stash
            ypool = yp * ps
            gp = gp_ref[...].astype(F32)
            sgp = _sigmoid(gp)
            silu_gp = gp * sgp
            z = jnp.concatenate([silu_ga * attn_t, silu_gp * ypool], axis=1).astype(BF16)
            y = _dot(z, wo_ref[...])
            zb_ref[...] = z
            fa_ref[...] = attn_t * (sga * (1.0 + ga * (1.0 - sga)))
            sa_ref[...] = silu_ga
            fp_ref[...] = ypool * (sgp * (1.0 + gp * (1.0 - sgp)))
            sp_ref[...] = silu_gp
            yp_ref[...] = yp
            pooled_ref[...] = jnp.concatenate(pooled, axis=1)
            return y

        def a_loss(stash, y):
            res = x_ref[...] + gate * y - t_ref[...]
            loss_ref[...] += jnp.sum(res * res) * (0.5 / D_MODEL)
            dxn = res * (1.0 / D_MODEL)
            g1_ref[...] = dxn.astype(BF16)
            dgate_ref[...] += jnp.sum(dxn * y, axis=0, keepdims=True)
            stash[1][...] = (gate * dxn).astype(BF16)

        def b_dz(stash):
            return _dot_nt(stash[1][...], wo_ref[...])

        def b_gwo(stash):
            gwo_ref[...] += _dot_tn(stash[0][...], stash[1][...])

        def b_vector(stash, slot, dz):
            _, _, fa_ref, sa_ref, fp_ref, sp_ref, yp_ref, _ = stash
            dbra = dz[:, 0:D_ATTN]
            dbrp = dz[:, D_ATTN:]
            dga_ref[...] = (dbra * fa_ref[...]).astype(BF16)
            dbuf[slot] = jnp.swapaxes((dbra * sa_ref[...]).reshape(per, n_blk, D_ATTN), 0, 1)
            dgp_ref[...] = (dbrp * fp_ref[...]).astype(BF16)
            dyp_s = dbrp * sp_ref[...]
            dps_ref[...] += jnp.sum(dyp_s * yp_ref[...], axis=0, keepdims=True)
            return (dyp_s * ps).astype(BF16)

        def b_small(stash, dyp):
            pooled_ref = stash[7]
            for g in range(len(POOL_WINDOWS)):
                sl = slice(g * GROUP_DIM, (g + 1) * GROUP_DIM)
                gwp_ref[g] += _dot_tn(pooled_ref[:, sl], dyp[:, sl])
                dpl_ref[:, sl] = _dot_nt(dyp[:, sl], wp_ref[g].astype(BF16)).astype(BF16)

        def both(a_stash, b_stash, slot_a):
            dz = b_dz(b_stash)
            front = a_vector(slot_a)
            yp = a_group_maps(front[-1])
            b_gwo(b_stash)
            y = a_project(a_stash, yp, *front)
            dyp = b_vector(b_stash, 1 - slot_a, dz)
            a_loss(a_stash, y)
            b_small(b_stash, dyp)

        @pl.when(s == 0)
        def _():
            front = a_vector(0)
            a_loss(stash_even, a_project(stash_even, a_group_maps(front[-1]), *front))

        @pl.when((s > 0) & (s < n_tiles) & (a_slot == 0))
        def _():
            both(stash_even, stash_odd, 0)

        @pl.when((s > 0) & (s < n_tiles) & (a_slot == 1))
        def _():
            both(stash_odd, stash_even, 1)

        @pl.when(s == n_tiles)
        def _():
            last = (n_tiles - 1) % 2
            stash = stash_odd if last else stash_even
            dz = b_dz(stash)
            b_gwo(stash)
            b_small(stash, b_vector(stash, last, dz))
            gwob_ref[...] = gwo_ref[...].astype(BF16)

        @pl.when(s >= 1)
        def _():
            for cp in slab_copies(s - 1, b_slot, False):
                cp.start()

        @pl.when(s == n_tiles)
        def _():
            wait_all(b_slot, False)
            wait_all(a_slot, False)

    ta = lambda s: jnp.minimum(s, n_tiles - 1)
    tb = lambda s: jnp.maximum(s - 1, 0)
    tile = lambda w: pl.BlockSpec((rows, w), lambda s: (ta(s), 0))
    tile_b = lambda w: pl.BlockSpec((rows, w), lambda s: (tb(s), 0))
    ucol = lambda col: pl.BlockSpec((rows, 512), lambda s: (ta(s), col))
    full = lambda shape: pl.BlockSpec(shape, lambda s: (0,) * len(shape))
    stash_shapes = [pltpu.VMEM((rows, D_MODEL), BF16), pltpu.VMEM((rows, D_MODEL), BF16)] + \
        [pltpu.VMEM((rows, 512), F32)] * 5 + [pltpu.VMEM((rows, D_POOL), BF16)]
    return pl.pallas_call(
        body, name="mix_fwd_bwd", grid=(n_tiles + 1,),
        out_shape=(jax.ShapeDtypeStruct((8, 128), F32), jax.ShapeDtypeStruct((seq, D_MODEL), BF16),
                   jax.ShapeDtypeStruct((1, D_MODEL), F32), jax.ShapeDtypeStruct((n_blk, Q_BLOCK, D_ATTN), F32),
                   jax.ShapeDtypeStruct((seq, D_ATTN), BF16), jax.ShapeDtypeStruct((seq, D_POOL), BF16),
                   jax.ShapeDtypeStruct((seq, D_POOL), BF16), jax.ShapeDtypeStruct((D_MODEL, D_MODEL), BF16),
                   jax.ShapeDtypeStruct((4, GROUP_DIM, GROUP_DIM), F32), jax.ShapeDtypeStruct((1, D_POOL), F32)),
        in_specs=[tile(D_MODEL), tile(D_MODEL), pl.BlockSpec(memory_space=pl.ANY), ucol(0), ucol(1),
                  pl.BlockSpec((HALO, 512), lambda s: (jnp.maximum(ta(s) * rows8 - 1, 0), 1)),
                  pl.BlockSpec((HALO, 512), lambda s: (jnp.minimum((ta(s) + 1) * rows8, last8), 1)),
                  ucol(2), full((8, D_MODEL)), full((8, D_MODEL)), full((4, GROUP_DIM, GROUP_DIM)), full((1, D_POOL)),
                  full((D_MODEL, D_MODEL))],
        out_specs=(full((8, 128)), tile(D_MODEL), full((1, D_MODEL)), pl.BlockSpec(memory_space=pl.ANY), tile_b(D_ATTN),
                   tile_b(D_POOL), tile_b(D_POOL), full((D_MODEL, D_MODEL)), full((4, GROUP_DIM, GROUP_DIM)),
                   full((1, D_POOL))),
        scratch_shapes=[pltpu.VMEM((2, n_blk, per, D_ATTN), F32), pltpu.VMEM((2, n_blk, per, D_ATTN), F32),
                        pltpu.VMEM((D_MODEL, D_MODEL), F32), *stash_shapes, *stash_shapes,
                        pltpu.SemaphoreType.DMA((2,)), pltpu.SemaphoreType.DMA((2,))],
        compiler_params=pltpu.CompilerParams(dimension_semantics=("arbitrary",), vmem_limit_bytes=VMEM_LIMIT),
    )(x, target, attn.reshape(n_blk, Q_BLOCK, D_ATTN), u, u, u, u, u, modrows, bmodrows, w_pool, pool_scale, w_out)


def _inproj_bwd(x, ctx, modrows, bmodrows, norm_g, w_in_p, q_lora_g, w_uq_p, kv_lora_g, w_ukv, qng_p, kng_p, cos, slo, shi,
                u, dq, dk, dv, dga, dgp, dpl, g1):
    seq = x.shape[0]
    n_tiles = seq // TILE + 1
    rows8 = TILE // HALO
    last8 = seq // HALO - 1

    def body(*refs):
        du_even, du_odd, dh_even, dh_odd = refs[-4:]
        gwin_ref, gwuq_ref, gwukv_ref, dqlg_ref, dkvlg_ref, dqng_ref, dkng_ref, dng_ref, dmod_ref = refs[-13:-4]
        s = pl.program_id(0)
        even = lax.rem(s, 2) == 0

        @pl.when(s == 0)
        def _():
            for ref in (gwin_ref, gwuq_ref, gwukv_ref, dqlg_ref, dkvlg_ref, dqng_ref, dkng_ref, dng_ref, dmod_ref,
                        du_odd, dh_even):
                ref[...] = jnp.zeros_like(ref)

        @pl.when((s < n_tiles) & even)
        def _():
            stages(s, refs[:-4], du_even, du_odd, dh_odd, dh_even, (True, True, True))

        @pl.when((s < n_tiles) & jnp.logical_not(even))
        def _():
            stages(s, refs[:-4], du_odd, du_even, dh_even, dh_odd, (True, True, True))

        for tail, run in ((n_tiles, (False, True, True)), (n_tiles + 1, (False, False, True))):
            @pl.when(s == tail)
            def _():
                if tail % 2 == 0:
                    stages(s, refs[:-4], du_even, du_odd, dh_odd, dh_even, run)
                else:
                    stages(s, refs[:-4], du_odd, du_even, dh_even, dh_odd, run)

    def stages(s, refs, du_ref, du_prev, dh_fill, dh_prev, run):
        (xb_ref, xc_ref, ctx_ref, mod_ref, bmod_ref, ng_ref, win_ref, qlg_ref, wuq_ref, kvlg_ref, wukv_ref, qng_ref, kng_ref,
         cos_ref, slo_ref, shi_ref, cq_ref, ckv_ref, kr_ref, dq_ref, dk_ref, dv_ref, dga_ref, dgp_ref, dpl_ref,
         hb_ref, ha_ref, g1_ref,
         gx_ref, gwin_ref, gwuq_ref, gwukv_ref, dqlg_ref, dkvlg_ref, dqng_ref, dkng_ref, dng_ref, dmod_ref) = refs


        i = s
        is_lat = s > 0
        if run[0]:
            cq = cq_ref[...]
            rq = lax.rsqrt(jnp.mean(cq * cq, axis=-1, keepdims=True) + NORM_EPS)
            qhat = cq * rq
            qnb = (qhat * qlg_ref[...]).astype(BF16)
            q = _dot_nt(qnb, wuq_ref[...])
            ckv = ckv_ref[...]
            rkv = lax.rsqrt(jnp.mean(ckv * ckv, axis=-1, keepdims=True) + NORM_EPS)
            kvhat = ckv * rkv
            kvnb = (kvhat * kvlg_ref[...]).astype(BF16)
            kv = _dot(kvnb, wukv_ref[...])

        if run[1]:
            _, _, _, h2, _ = _modulated_input(s <= 1, xb_ref, ctx_ref, mod_ref, bmod_ref, ng_ref)
            dub = du_prev[...]
            du_g, du_m = dub[:, 0:N_GATES], dub[:, N_GATES:U_PAD]
            h2b = h2.astype(BF16)
            dh_fill[...] = _dot(du_g, win_ref[N_MLA:D_IN_PROJ, :]) + _dot(du_m, win_ref[0:U_PAD - N_GATES, :])
            gwin_ref[N_MLA:D_IN_PROJ, :] += _dot_tn(du_g, h2b)
            gwin_ref[0:N_MLA, :] += _dot_tn(du_m, h2b)[0:N_MLA]

        if run[2]:
            ctx3 = s <= 2
            xt, r, xg, _, scale = _modulated_input(ctx3, xc_ref, ctx_ref, mod_ref, bmod_ref, ng_ref)
            dh = dh_prev[...]
            dshift = jnp.sum(dh, axis=0, keepdims=True)
            dscale = jnp.sum(dh * xg, axis=0, keepdims=True)
            zero = jnp.zeros_like(dshift)
            dmod_ref[0:1, :] += jnp.where(ctx3, zero, dshift)
            dmod_ref[1:2, :] += jnp.where(ctx3, zero, dscale)
            dmod_ref[2:3, :] += jnp.where(ctx3, dshift, zero)
            dmod_ref[3:4, :] += jnp.where(ctx3, dscale, zero)
            dxg = dh * (1.0 + scale)
            xr = xt * r
            dng_ref[...] += jnp.sum(dxg * xr, axis=0, keepdims=True)
            dxn = dxg * ng_ref[...]
            gx_ref[...] = g1_ref[...].astype(F32) + r * (dxn - xr * jnp.mean(dxn * xr, axis=-1, keepdims=True))

        if not run[0]:
            return

        cos_t, slo_t, shi_t = cos_ref[...], slo_ref[...], shi_ref[...]
        qg = qng_ref[...]
        dq_heads = []
        dqng = jnp.zeros((1, D_HEAD_PAD), F32)
        for hd in range(N_HEADS):
            qh = q[:, hd * D_HEAD_PAD:(hd + 1) * D_HEAD_PAD]
            rr = lax.rsqrt(jnp.sum(qh * qh, axis=-1, keepdims=True) * (1.0 / D_HEAD) + NORM_EPS)
            yq = qh * rr
            dpost = jnp.where(is_lat, dq_ref[hd] * ATTN_SCALE, 0.0)
            dyn = jnp.concatenate([dpost[:, 0:D_NOPE], _rope_t(dpost[:, D_NOPE:], cos_t, slo_t, shi_t)], axis=1)
            dqng = dqng + jnp.sum(dyn * yq, axis=0, keepdims=True)
            dyg = dyn * qg
            dq_heads.append(rr * (dyg - yq * (jnp.sum(dyg * yq, axis=-1, keepdims=True) * (1.0 / D_HEAD))))
        dqng_ref[...] += dqng
        dqf = jnp.concatenate(dq_heads, axis=1).astype(BF16)

        krz = kr_ref[...]
        kr_ss = jnp.sum(krz * krz, axis=-1, keepdims=True)
        kg = kng_ref[...]
        kg_n, kg_r = kg[:, 0:D_NOPE], kg[:, D_NOPE:]
        dkv_parts = []
        dkr = jnp.zeros((TILE, 128), F32)
        dkng_n = jnp.zeros((1, D_NOPE), F32)
        dkng_r = jnp.zeros((1, 128), F32)
        for hd in range(N_HEADS):
            kn = kv[:, hd * 256:hd * 256 + D_NOPE]
            rr = lax.rsqrt((jnp.sum(kn * kn, axis=-1, keepdims=True) + kr_ss) * (1.0 / D_HEAD) + NORM_EPS)
            yn, yr = kn * rr, krz * rr
            dk_h = jnp.transpose(dk_ref[hd]) * LN_2
            dpn = dk_h[:, 0:D_NOPE]
            dpr = _rope_t(dk_h[:, D_NOPE:], cos_t, slo_t, shi_t)
            dkng_n = dkng_n + jnp.sum(dpn * yn, axis=0, keepdims=True)
            dkng_r = dkng_r + jnp.sum(dpr * yr, axis=0, keepdims=True)
            dyn, dyr = dpn * kg_n, dpr * kg_r
            proj = (jnp.sum(dyn * yn, axis=-1, keepdims=True) + jnp.sum(dyr * yr, axis=-1, keepdims=True)) * (1.0 / D_HEAD)
            dkv_parts.append(rr * (dyn - yn * proj))
            dkv_parts.append(jnp.transpose(dv_ref[hd]))
            dkr = dkr + rr * (dyr - yr * proj)
        dkng_ref[:, 0:D_NOPE] += dkng_n
        dkng_ref[:, D_NOPE:] += dkng_r
        dkvf = jnp.concatenate(dkv_parts, axis=1).astype(BF16)

        lat_t = jnp.maximum(i - 1, 0)
        dpl_t = dpl_ref[...].astype(F32)
        ds = jnp.concatenate([jnp.where(i > 1, hb_ref[...].astype(F32), 0.0), dpl_t,
                              jnp.where(i < n_tiles - 1, ha_ref[...].astype(F32), 0.0)], axis=0)
        tpos = lat_t * TILE - HALO + lax.broadcasted_iota(jnp.int32, (TILE + 2 * HALO, 1), 0)
        for g, w in enumerate(POOL_WINDOWS):
            sl = slice(g * GROUP_DIM, (g + 1) * GROUP_DIM)
            wsum = _window_sum(ds[:, sl] * _inv_count(tpos, w, seq), w, True)[HALO:HALO + TILE]
            du_ref[:, O_PIN + g * GROUP_DIM:O_PIN + (g + 1) * GROUP_DIM] = jnp.where(
                is_lat, wsum - dpl_t[:, sl], 0.0).astype(BF16)

        du_ref[:, O_GA:O_GA + D_ATTN] = jnp.where(is_lat, dga_ref[...], jnp.zeros((), BF16))
        du_ref[:, O_GP:O_GP + D_POOL] = jnp.where(is_lat, dgp_ref[...], jnp.zeros((), BF16))
        du_ref[:, O_KR:U_PAD] = dkr.astype(BF16)

        gwuq_ref[...] += _dot_tn(dqf, qnb)
        dqn = _dot(dqf, wuq_ref[...])
        gwukv_ref[...] += _dot_tn(dkvf, kvnb)
        dkvn = _dot_nt(dkvf, wukv_ref[...])
        dqlg_ref[...] += jnp.sum(dqn * qhat, axis=0, keepdims=True)
        dqhat = dqn * qlg_ref[...]
        dcq = rq * (dqhat - qhat * jnp.mean(dqhat * qhat, axis=-1, keepdims=True))
        dkvlg_ref[...] += jnp.sum(dkvn * kvhat, axis=0, keepdims=True)
        dkvhat = dkvn * kvlg_ref[...]
        dckv = rkv * (dkvhat - kvhat * jnp.mean(dkvhat * kvhat, axis=-1, keepdims=True))
        du_ref[:, O_CQ:O_CQ + R_Q] = dcq.astype(BF16)
        du_ref[:, O_CKV:O_CKV + R_KV] = dckv.astype(BF16)

    t1 = lambda s: jnp.minimum(s, n_tiles - 1)
    t2 = lambda s: jnp.clip(s - 1, 0, n_tiles - 1)
    t3 = lambda s: jnp.clip(s - 2, 0, n_tiles - 1)
    latent = lambda t: jnp.maximum(t - 1, 0)
    lat = lambda s: (latent(t1(s)), 0)
    lat3 = lambda s: (latent(t3(s)), 0)
    full = lambda shape: pl.BlockSpec(shape, lambda s: (0,) * len(shape))
    rope_spec = pl.BlockSpec((TILE, 128), lambda s: (t1(s), 0))
    return pl.pallas_call(
        body, name="inproj_bwd", grid=(n_tiles + 2,),
        out_shape=(jax.ShapeDtypeStruct((seq, D_MODEL), F32), jax.ShapeDtypeStruct((D_IN_PROJ, D_MODEL), F32),
                   jax.ShapeDtypeStruct((N_HEADS * D_HEAD_PAD, R_Q), F32), jax.ShapeDtypeStruct((N_HEADS * 256, R_KV), F32),
                   jax.ShapeDtypeStruct((1, R_Q), F32), jax.ShapeDtypeStruct((1, R_KV), F32),
                   jax.ShapeDtypeStruct((1, D_HEAD_PAD), F32), jax.ShapeDtypeStruct((1, D_HEAD_PAD), F32),
                   jax.ShapeDtypeStruct((1, D_MODEL), F32), jax.ShapeDtypeStruct((8, D_MODEL), F32)),
        in_specs=[pl.BlockSpec((TILE, D_MODEL), lambda s: (latent(t2(s)), 0)), pl.BlockSpec((TILE, D_MODEL), lat3),
                  full((TILE, D_MODEL)), full((8, D_MODEL)), full((8, D_MODEL)),
                  full((1, D_MODEL)), full((D_IN_PROJ, D_MODEL)), full((1, R_Q)), full((N_HEADS * D_HEAD_PAD, R_Q)),
                  full((1, R_KV)), full((R_KV, N_HEADS * 256)), full((1, D_HEAD_PAD)), full((1, D_HEAD_PAD)),
                  rope_spec, rope_spec, rope_spec,
                  pl.BlockSpec((TILE, R_Q), lambda s: (t1(s), (O_CQ - N_GATES) // R_Q)),
                  pl.BlockSpec((TILE, R_KV), lambda s: (t1(s), (O_CKV - N_GATES) // R_KV)),
                  pl.BlockSpec((TILE, 128), lambda s: (t1(s), (O_KR - N_GATES) // 128)),
                  pl.BlockSpec((N_HEADS, TILE, D_HEAD_PAD), lambda s: (0, latent(t1(s)), 0)),
                  pl.BlockSpec((N_HEADS, D_HEAD_PAD, TILE), lambda s: (0, 0, t1(s))),
                  pl.BlockSpec((N_HEADS, D_V, TILE), lambda s: (0, 0, t1(s))),
                  pl.BlockSpec((TILE, D_ATTN), lat), pl.BlockSpec((TILE, D_POOL), lat), pl.BlockSpec((TILE, D_POOL), lat),
                  pl.BlockSpec((HALO, D_POOL), lambda s: (jnp.maximum((t1(s) - 1) * rows8 - 1, 0), 0)),
                  pl.BlockSpec((HALO, D_POOL), lambda s: (jnp.minimum(jnp.maximum(t1(s), 1) * rows8, last8), 0)),
                  pl.BlockSpec((TILE, D_MODEL), lat3)],
        out_specs=(pl.BlockSpec((TILE, D_MODEL), lat3), full((D_IN_PROJ, D_MODEL)), full((N_HEADS * D_HEAD_PAD, R_Q)),
                   full((N_HEADS * 256, R_KV)), full((1, R_Q)), full((1, R_KV)), full((1, D_HEAD_PAD)),
                   full((1, D_HEAD_PAD)), full((1, D_MODEL)), full((8, D_MODEL))),
        scratch_shapes=[pltpu.VMEM((TILE, U_PAD), BF16), pltpu.VMEM((TILE, U_PAD), BF16),
                        pltpu.VMEM((TILE, D_MODEL), F32), pltpu.VMEM((TILE, D_MODEL), F32)],
        compiler_params=pltpu.CompilerParams(dimension_semantics=("arbitrary",), vmem_limit_bytes=VMEM_LIMIT),
    )(x, x, ctx, modrows, bmodrows, norm_g, w_in_p, q_lora_g, w_uq_p, kv_lora_g, w_ukv, qng_p, kng_p, cos, slo, shi,
      u, u, u, dq, dk, dv, dga, dgp, dpl, dpl, dpl, g1)


SMALL_LAYOUT = ((0, D_MODEL), (1, R_Q), (2, R_KV), (3, D_HEAD_PAD), (4, D_HEAD_PAD), (5, D_POOL))
ROW_DMOD_B, ROW_DMOD_C, ROW_LOSS = 6, 9, 12


def _epilogue(parts, landed, smalls, dmod4, dgate, loss_acc, w_mod_l):
    n_a, n_l, n_s = len(parts), len(landed), len(smalls)
    n_mod = w_mod_l.shape[1]
    blk = [p.shape[1:] for p in parts]
    small_out = [D_MODEL, R_Q, R_KV, D_HEAD, D_HEAD, D_POOL]

    def body(*refs):
        part_refs = refs[0:n_a]
        land_refs = refs[n_a:n_a + n_l]
        small_in = refs[n_a + n_l:n_a + n_l + n_s]
        dmod4_ref, dgate_ref, loss_ref, wmod_ref = refs[n_a + n_l + n_s:n_a + n_l + n_s + 4]
        outs = refs[n_a + n_l + n_s + 4:]
        red_refs = outs[0:n_a]
        landsum_refs = outs[n_a:n_a + n_l]
        ccg_ref = outs[n_a + n_l]
        sum_refs = outs[n_a + n_l + 1:n_a + n_l + 1 + n_s]
        gbmod_ref, dmy_ref, lossout_ref = outs[n_a + n_l + 1 + n_s:n_a + n_l + 4 + n_s]
        scr = outs[n_a + n_l + 4 + n_s:]
        recv1, own1, sum1b, recv2 = scr[0:n_a], scr[n_a:2 * n_a], scr[2 * n_a:3 * n_a], scr[3 * n_a:4 * n_a]
        small_ref, smallg_ref, tot_ref, cc_ref = scr[4 * n_a:4 * n_a + 4]
        land_vmem = scr[4 * n_a + 4:4 * n_a + 4 + n_l]
        ssem1, rsem1, lsem, ssem2, rsem2, ssem_s, rsem_s, ssem_cc, rsem_cc, land_sem = scr[4 * n_a + 4 + n_l:]
        x, y, c = _coords()
        me = (x, y, c)
        sibling = (x, y, 1 - c)

        small_ref[...] = jnp.zeros_like(small_ref)
        for ref, (row, width) in zip(small_in, SMALL_LAYOUT):
            small_ref[row:row + 1, 0:width] = ref[...]
        small_ref[ROW_DMOD_B:ROW_DMOD_B + 2, :] = dmod4_ref[0:2, :]
        small_ref[ROW_DMOD_B + 2:ROW_DMOD_B + 3, :] = dgate_ref[...]
        small_ref[ROW_DMOD_C:ROW_DMOD_C + 2, :] = dmod4_ref[2:4, :]
        small_ref[ROW_LOSS:ROW_LOSS + 1, 0:128] = loss_ref[0:1, :]
        smallg_ref[_lin(me)] = small_ref[...]
        s_recv, s_send = _gather_to_all(small_ref, smallg_ref, ssem_s, rsem_s)

        stage1, own_copies = [], []
        for a in range(n_a):
            for b in range(4):
                dev = 4 * (b >> 1) + 2 * (b & 1)
                stage1.append(pltpu.make_async_remote_copy(
                    src_ref=part_refs[a].at[dev + (1 - c)], dst_ref=recv1[a].at[b],
                    send_sem=ssem1.at[a, b], recv_sem=rsem1.at[a, b], device_id=sibling, device_id_type=MESH))
                own_copies.append(pltpu.make_async_copy(part_refs[a].at[dev + c], own1[a].at[b], lsem.at[a, b]))
                stage1[-1].start()
                own_copies[-1].start()
        land_copies = [pltpu.make_async_copy(land_refs[n], land_vmem[n], land_sem.at[n]) for n in range(n_l)]
        for cp in land_copies:
            cp.start()

        s_recv()
        tot = smallg_ref[0]
        for d in range(1, N_DEV):
            tot = tot + smallg_ref[d]
        tot_ref[...] = tot
        for ref, (row, _), width in zip(sum_refs, SMALL_LAYOUT, small_out):
            ref[...] = tot_ref[row:row + 1, 0:width]
        lossout_ref[...] = tot_ref[8:16, 0:128]
        lin = _lin(me)

        def my_columns(three_rows):
            v = jnp.concatenate(three_rows, axis=1)
            out = jnp.zeros((1, n_mod), F32)
            for d in range(N_DEV):
                out = out + jnp.where(lin == d, v[:, d * n_mod:(d + 1) * n_mod], 0.0)
            return out

        rows3 = lambda ref, r0: [ref[r0 + t:r0 + t + 1, :] for t in range(3)]
        dmodc = rows3(tot_ref, ROW_DMOD_C)
        gbmod_ref[...] = jnp.concatenate(rows3(tot_ref, ROW_DMOD_B), axis=1) + jnp.concatenate(dmodc, axis=1)
        dmy_ref[...] = jnp.zeros_like(dmy_ref)
        for b in range(N_DEV):
            dmy_ref[b:b + 1, :] = my_columns(rows3(smallg_ref.at[b], ROW_DMOD_B))
        dmy = my_columns(dmodc)
        dmy_ref[N_DEV:N_DEV + 1, :] = dmy
        part = _dot_nt(jnp.broadcast_to(dmy, (8, n_mod)).astype(BF16), wmod_ref[...].astype(BF16))

        cc_ref[...] = part
        ccg_ref[_lin(me)] = part
        cc_recv, cc_send = _gather_to_all(cc_ref, ccg_ref, ssem_cc, rsem_cc)

        stage2 = []
        for a in range(n_a):
            for b in range(4):
                stage1[4 * a + b].wait_recv()
                own_copies[4 * a + b].wait()
                sum1b[a][b] = (own1[a][b] + recv1[a][b]).astype(BF16)
            for k in (1, 2, 3):
                tx, ty = _flip(x, (k >> 1) & 1), _flip(y, k & 1)
                stage2.append(pltpu.make_async_remote_copy(
                    src_ref=sum1b[a].at[2 * tx + ty], dst_ref=recv2[a].at[k - 1], send_sem=ssem2.at[a, k - 1],
                    recv_sem=rsem2.at[a, k - 1], device_id=(tx, ty, c), device_id_type=MESH))
                stage2[-1].start()
        for a in range(n_a):
            own = own1[a][2 * x + y] + recv1[a][2 * x + y]
            for k in (1, 2, 3):
                stage2[3 * a + k - 1].wait_recv()
                own = own + recv2[a][k - 1].astype(F32)
            red_refs[a][...] = own

        for cp in land_copies:
            cp.wait()
        for ref, out in zip(land_vmem, landsum_refs):
            acc = ref[0].astype(F32)
            for d in range(1, N_DEV):
                acc = acc + ref[d].astype(F32)
            out[...] = acc
        cc_recv()
        for cp in stage1 + stage2:
            cp.wait_send()
        s_send()
        cc_send()

    vm = pl.BlockSpec(memory_space=pltpu.VMEM)
    sds = jax.ShapeDtypeStruct
    return pl.pallas_call(
        body, name="epilogue_reduce",
        out_shape=(*[sds(s, F32) for s in blk], *[sds(a.shape[1:], F32) for a in landed], sds((N_DEV, 8, D_MODEL), F32),
                   *[sds((1, w), F32) for w in small_out], sds((1, 3 * D_MODEL), F32), sds((16, n_mod), F32),
                   sds((8, 128), F32)),
        in_specs=[pl.BlockSpec(memory_space=pl.ANY)] * (n_a + n_l) + [vm] * (n_s + 4),
        out_specs=tuple([vm] * (n_a + n_l + n_s + 4)),
        scratch_shapes=[*[pltpu.VMEM((4,) + s, F32) for s in blk], *[pltpu.VMEM((4,) + s, F32) for s in blk],
                        *[pltpu.VMEM((4,) + s, BF16) for s in blk], *[pltpu.VMEM((3,) + s, BF16) for s in blk],
                        pltpu.VMEM((SMALL_ROWS, D_MODEL), F32), pltpu.VMEM((N_DEV, SMALL_ROWS, D_MODEL), F32),
                        pltpu.VMEM((SMALL_ROWS, D_MODEL), F32), pltpu.VMEM((8, D_MODEL), F32),
                        *[pltpu.VMEM(a.shape, a.dtype) for a in landed],
                        pltpu.SemaphoreType.DMA((n_a, 4)), pltpu.SemaphoreType.DMA((n_a, 4)), pltpu.SemaphoreType.DMA((n_a, 4)),
                        pltpu.SemaphoreType.DMA((n_a, 3)), pltpu.SemaphoreType.DMA((n_a, 3)),
                        pltpu.SemaphoreType.DMA((7,)), pltpu.SemaphoreType.DMA((7,)),
                        pltpu.SemaphoreType.DMA((7,)), pltpu.SemaphoreType.DMA((7,)), pltpu.SemaphoreType.DMA((n_l,))],
        compiler_params=pltpu.CompilerParams(vmem_limit_bytes=VMEM_LIMIT),
    )(*parts, *landed, *smalls, dmod4, dgate, loss_acc, w_mod_l)


def _adamw(weights, grads, ms, vs, c_all_t, dmod_my, p2g):
    n = len(weights)

    def body(*refs):
        w_refs = refs[0:n]
        g_in = refs[n:2 * n - 2]
        m_refs = refs[2 * n - 2:3 * n - 2]
        v_refs = refs[3 * n - 2:4 * n - 2]
        cat_ref, dmod_ref, p2g_ref = refs[4 * n - 2:4 * n + 1]
        outs = refs[4 * n + 1:]
        g_refs, d_refs, nm_refs, nv_refs = outs[0:n], outs[n:2 * n], outs[2 * n:3 * n], outs[3 * n:4 * n]
        gcc_ref, gwm_ref = g_refs[0], g_refs[1]

        part = p2g_ref[0, 0:1, :]
        for d in range(1, N_DEV):
            part = part + p2g_ref[d, 0:1, :]
        cc = w_refs[0][...]
        sg = _sigmoid(cc)
        gcc_ref[...] = part * (sg * (1.0 + cc * (1.0 - sg)))
        cat = cat_ref[...]
        gwm_ref[...] = lax.dot_general(cat * _sigmoid(cat), dmod_ref[...], (((1,), (0,)), ((), ())), precision=HIGHEST,
                                       preferred_element_type=F32)
        for idx in range(n):
            if idx >= 2:
                g_refs[idx][...] = g_in[idx - 2][...]
            g = g_refs[idx][...]
            m = ADAM_B1 * m_refs[idx][...] + (1.0 - ADAM_B1) * g
            v = ADAM_B2 * v_refs[idx][...] + (1.0 - ADAM_B2) * (g * g)
            m_hat = m / (1.0 - ADAM_B1 ** ADAM_STEP)
            v_hat = v / (1.0 - ADAM_B2 ** ADAM_STEP)
            d_refs[idx][...] = -ADAM_LR * (m_hat / (jnp.sqrt(v_hat) + ADAM_EPS) + ADAM_WD * w_refs[idx][...])
            nm_refs[idx][...] = m
            nv_refs[idx][...] = v

    vm = pl.BlockSpec(memory_space=pltpu.VMEM)
    shapes = [jax.ShapeDtypeStruct(w.shape, F32) for w in weights]
    args = list(weights) + list(grads[2:]) + list(ms) + list(vs) + [c_all_t, dmod_my, p2g]
    out_shape = tuple(shapes * 4)
    return pl.pallas_call(
        body, name="adamw_update", out_shape=out_shape, in_specs=[vm] * len(args), out_specs=tuple([vm] * len(out_shape)),
        compiler_params=pltpu.CompilerParams(vmem_limit_bytes=VMEM_LIMIT),
    )(*args)


def _rope_tables(seq):
    rows = seq // GRID_W
    row = np.repeat(np.arange(rows, dtype=np.float32), GRID_W)
    col = np.tile(np.arange(GRID_W, dtype=np.float32), rows)
    n_freq = D_ROPE // 4
    inv = (np.float32(ROPE_BASE) ** (-np.arange(n_freq, dtype=np.float32) / np.float32(n_freq))).astype(np.float32)
    ang_r = (row[:, None] * inv).astype(np.float32)
    ang_c = (col[:, None] * inv).astype(np.float32)
    ang = np.concatenate([ang_r, ang_r, ang_c, ang_c], axis=-1)
    cos, sin = np.cos(ang).astype(np.float32), np.sin(ang).astype(np.float32)
    low = (np.arange(D_ROPE) % 32) < 16

    def table(t, fill):
        out = np.full((TILE + seq, 128), fill, np.float32)
        out[TILE:, 0:D_ROPE] = t
        return jnp.asarray(out)

    return table(cos, 1.0), table(np.where(low, -sin, 0.0), 0.0), table(np.where(low, 0.0, sin), 0.0)


def kernel(x, c, ctx, c_ctx, w_mod, b_mod, norm_g, w_in, q_lora_g, w_uq, kv_lora_g, w_ukv, q_norm_g, k_norm_g, w_pool, pool_scale, w_out, loss_target, m_c_ctx, m_w_mod, m_b_mod, m_norm_g, m_w_in, m_q_lora_g, m_w_uq, m_kv_lora_g, m_w_ukv, m_q_norm_g, m_k_norm_g, m_w_pool, m_pool_scale, m_w_out, v_c_ctx, v_w_mod, v_b_mod, v_norm_g, v_w_in, v_q_lora_g, v_w_uq, v_kv_lora_g, v_w_ukv, v_q_norm_g, v_k_norm_g, v_w_pool, v_pool_scale, v_w_out):
    seq = x.shape[1]
    assert ctx.shape[1] == TILE and seq % TILE == 0 and seq % GRID_W == 0
    ix, iy, ic = lax.axis_index("x"), lax.axis_index("y"), lax.axis_index("c")
    me = 4 * ix + 2 * iy + ic
    x2, ctx2, tgt2 = x[0], ctx[0], loss_target[0]
    w_mod_l, w_ukv_l, w_out_l = w_mod[0], w_ukv[0], w_out[0]
    c_ctx_row = c_ctx.reshape(1, D_MODEL)

    tr = lambda a: jnp.transpose(a[0])
    w_in_tl, w_uq_tl = tr(w_in), tr(w_uq)
    cg, modg, wg_in, wg_uq, wg_ukv = _prologue(
        c, c_ctx_row, w_mod_l,
        [w_in_tl, w_uq_tl, w_ukv_l])
    mod_all = jnp.transpose(modg, (1, 0, 2)).reshape(16, 3 * D_MODEL)
    mod_b = lax.dynamic_slice_in_dim(mod_all, me, 1, axis=0).reshape(3, D_MODEL)
    mod_c = mod_all[8].reshape(3, D_MODEL)
    modrows = jnp.concatenate([mod_b, mod_c[0:2], jnp.zeros((3, D_MODEL), F32)], axis=0)
    b3 = b_mod.reshape(3, D_MODEL)
    bmodrows = jnp.concatenate([b3, b3[0:2], jnp.zeros((3, D_MODEL), F32)], axis=0)

    w_in_p = wg_in.reshape(D_IN_PROJ, D_MODEL)
    w_uq_p = jnp.concatenate([wg_uq.reshape(N_HEADS, D_HEAD, R_Q), jnp.zeros((N_HEADS, D_HEAD_PAD - D_HEAD, R_Q), BF16)],
                             axis=1).reshape(N_HEADS * D_HEAD_PAD, R_Q)
    w_ukv_f = jnp.transpose(wg_ukv, (1, 0, 2)).reshape(R_KV, N_HEADS * 256)
    qng_p = jnp.concatenate([q_norm_g, jnp.zeros((1, D_HEAD_PAD - D_HEAD), F32)], axis=1)
    kng_p = jnp.concatenate([k_norm_g, jnp.zeros((1, D_HEAD_PAD - D_HEAD), F32)], axis=1)
    cos, slo, shi = _rope_tables(seq)

    u_gates, u_mla, q, k, v = _inproj_fwd(x2, ctx2, modrows, bmodrows, norm_g, w_in_p, q_lora_g, w_uq_p, kv_lora_g, w_ukv_f,
                             qng_p, kng_p, cos, slo, shi)
    attn, lse, wg_out = _attn_fwd(q, k, v, min(2048, seq), w_out_l.astype(BF16))
    (loss_acc, g1, dgate, d_attn, dga, dgp, dpl, gwo_b, gwp, dps) = _mix(
        x2, tgt2, attn, u_gates, modrows, bmodrows, w_pool[0], pool_scale, wg_out.reshape(D_MODEL, D_MODEL))

    blocks = lambda a: a.reshape((N_DEV, a.shape[0] // N_DEV) + a.shape[1:])
    dq, dk, dv, land_wo, land_wp = _attn_bwd(q, k, v, attn, lse, d_attn.reshape(seq, D_ATTN), min(1024, seq), blocks(gwo_b),
                                             gwp.reshape(4 * GROUP_DIM, GROUP_DIM))
    (grad_x, gwin_t, gwuq_p, gwukv_t, dqlg, dkvlg, dqng, dkng, dng, dmod4) = _inproj_bwd(
        x2, ctx2, modrows, bmodrows, norm_g, w_in_p, q_lora_g, w_uq_p, kv_lora_g, w_ukv_f, qng_p, kng_p, cos, slo, shi,
        u_mla, dq, dk, dv, dga, dgp, dpl, g1)

    gwuq_t = gwuq_p.reshape(N_HEADS, D_HEAD_PAD, R_Q)[:, 0:D_HEAD].reshape(N_HEADS * D_HEAD, R_Q)
    parts = [blocks(gwin_t), blocks(gwuq_t), blocks(gwukv_t)]
    (r_win, r_wuq, r_wukv, r_wout, g_w_pool, ccg, g_ng, g_qlg, g_kvlg, g_qng, g_kng, g_ps, g_bmod, dmod_my,
     loss_rows) = _epilogue(parts, [land_wo, land_wp], [dng, dqlg, dkvlg, dqng, dkng, dps], dmod4, dgate, loss_acc, w_mod_l)

    g_w_ukv = jnp.transpose(r_wukv)
    c_rows = cg[:, 0, :]
    c_all_t = jnp.transpose(jnp.concatenate([c_rows, c_ctx_row, jnp.zeros((16 - N_DEV - 1, D_MODEL), F32)], axis=0))

    weights = [c_ctx_row, w_mod_l, b_mod, norm_g, w_in_tl, q_lora_g, w_uq_tl, kv_lora_g, w_ukv_l, q_norm_g, k_norm_g,
               w_pool.reshape(4 * GROUP_DIM, GROUP_DIM), pool_scale, w_out_l]
    grads = [None, None, g_bmod, g_ng, r_win, g_qlg, r_wuq, g_kvlg, g_w_ukv, g_qng, g_kng, g_w_pool, g_ps, r_wout]
    ms = [m_c_ctx.reshape(1, D_MODEL), m_w_mod[0], m_b_mod, m_norm_g, tr(m_w_in), m_q_lora_g, tr(m_w_uq), m_kv_lora_g,
          m_w_ukv[0], m_q_norm_g, m_k_norm_g, m_w_pool.reshape(4 * GROUP_DIM, GROUP_DIM), m_pool_scale, m_w_out[0]]
    vs = [v_c_ctx.reshape(1, D_MODEL), v_w_mod[0], v_b_mod, v_norm_g, tr(v_w_in), v_q_lora_g, tr(v_w_uq), v_kv_lora_g,
          v_w_ukv[0], v_q_norm_g, v_k_norm_g, v_w_pool.reshape(4 * GROUP_DIM, GROUP_DIM), v_pool_scale, v_w_out[0]]
    outs = _adamw(weights, grads, ms, vs, c_all_t, dmod_my, ccg)
    n = len(weights)
    grads, deltas, new_m, new_v = outs[0:n], outs[n:2 * n], outs[2 * n:3 * n], outs[3 * n:4 * n]

    loss = loss_rows[ROW_LOSS - 8, 0]
    final = [c_ctx.shape, w_mod.shape, b_mod.shape, norm_g.shape, w_in.shape, q_lora_g.shape, w_uq.shape, kv_lora_g.shape,
             w_ukv.shape, q_norm_g.shape, k_norm_g.shape, w_pool.shape, pool_scale.shape, w_out.shape]
    transposed = (4, 6)

    def shaped(arrs):
        return [(jnp.transpose(a) if i in transposed else a).reshape(s) for i, (a, s) in enumerate(zip(arrs, final))]

    return (loss, grad_x[None], *shaped(grads), *shaped(deltas), *shaped(new_m), *shaped(new_v))
```

```python
import numpy as np

import jax
import jax.numpy as jnp
from jax import lax
from jax.experimental import pallas as pl
from jax.experimental.pallas import tpu as pltpu

F32 = jnp.float32
BF16 = jnp.bfloat16
MESH = pl.DeviceIdType.MESH
HIGHEST = lax.Precision.HIGHEST

D_MODEL = 1024
N_HEADS = 4
D_NOPE = 128
D_ROPE = 64
D_HEAD = D_NOPE + D_ROPE
D_HEAD_PAD = 256
D_V = 128
R_Q = 256
R_KV = 128
D_ATTN = 512
D_POOL = 512
POOL_WINDOWS = (2, 4, 8, 16)
GROUP_DIM = 128
GRID_W = 64
ROPE_BASE = 10000.0
NORM_EPS = 1e-6
ATTN_SCALE = D_HEAD ** -0.5
LOG2_E = 1.4426950408889634
LN_2 = 0.6931471805599453
ATTN_ROWS = 256
D_IN_PROJ = 1984
U_PAD = 2048
O_GA, O_PIN, O_GP, O_CQ, O_CKV, O_KR = 0, 512, 1024, 1536, 1792, 1920
TILE = 256
MIX_TILE = 512
Q_BLOCK = 128
HALO = 16
N_GATES = 1536
N_MLA = D_IN_PROJ - N_GATES
N_DEV = 8
VMEM_LIMIT = 56 * 1024 * 1024

ADAM_LR = 0.001
ADAM_B1 = 0.9
ADAM_B2 = 0.999
ADAM_EPS = 1e-08
ADAM_WD = 0.01
ADAM_STEP = 10

SMALL_ROWS = 16


def _dot(a, b):
    return lax.dot_general(a, b, (((1,), (0,)), ((), ())), preferred_element_type=F32)


def _dot_nt(a, b):
    return lax.dot_general(a, b, (((1,), (1,)), ((), ())), preferred_element_type=F32)


def _dot_tn(a, b):
    return lax.dot_general(a, b, (((0,), (0,)), ((), ())), preferred_element_type=F32)


def _sigmoid(x):
    return 1.0 / (1.0 + jnp.exp(-x))


def _rope(p, cos, slo, shi):
    return p * cos + pltpu.roll(p, 112, 1) * slo + pltpu.roll(p, 16, 1) * shi


def _rope_t(d, cos, slo, shi):
    return d * cos + pltpu.roll(d * slo, 16, 1) + pltpu.roll(d * shi, 112, 1)


def _shift_rows(a, k):
    n = a.shape[0]
    return pltpu.roll(a, (-k) % n, 0)


def _window_sum(x, w, transposed):
    s = (x + _shift_rows(x, 1)) if transposed else (_shift_rows(x, -1) + x)
    step = 1
    while 2 * step < w:
        s = _shift_rows(s, -step) + _shift_rows(s, step)
        step *= 2
    return s


def _inv_count(tpos, w, seq):
    lo = jnp.maximum(tpos - w // 2, 0)
    hi = jnp.minimum(tpos - w // 2 + w, seq)
    return 1.0 / jnp.maximum(hi - lo, 1).astype(F32)


def _coords():
    return lax.axis_index("x"), lax.axis_index("y"), lax.axis_index("c")


def _flip(v, bit):
    return (1 - v) if bit else v


def _peer(k):
    x, y, c = _coords()
    return (_flip(x, (k >> 2) & 1), _flip(y, (k >> 1) & 1), _flip(c, k & 1))


def _lin(p):
    return 4 * p[0] + 2 * p[1] + p[2]


def _gather_to_all(src_ref, slots_ref, send_sems, recv_sems):
    me = _coords()
    copies = []
    for k in range(1, N_DEV):
        cp = pltpu.make_async_remote_copy(
            src_ref=src_ref, dst_ref=slots_ref.at[_lin(me)], send_sem=send_sems.at[k - 1], recv_sem=recv_sems.at[k - 1],
            device_id=_peer(k), device_id_type=MESH)
        cp.start()
        copies.append(cp)

    def wait_recv():
        for k in range(1, N_DEV):
            pltpu.make_async_remote_copy(
                src_ref=src_ref, dst_ref=slots_ref.at[_lin(_peer(k))], send_sem=send_sems.at[k - 1],
                recv_sem=recv_sems.at[k - 1], device_id=_peer(k), device_id_type=MESH).wait_recv()

    def wait_send():
        for cp in copies:
            cp.wait_send()

    return wait_recv, wait_send


def _prologue(c8, cctx8, w_mod_l, shards):
    n_mod = w_mod_l.shape[1]
    n_w = len(shards)

    def body(c_ref, cctx_ref, wmod_ref, *refs):
        w_refs = refs[0:n_w]
        cg_ref, modg_ref = refs[n_w], refs[n_w + 1]
        wg_refs = refs[n_w + 2:2 * n_w + 2]
        modblk_ref = refs[2 * n_w + 2]
        wb_refs = refs[2 * n_w + 3:3 * n_w + 3]
        c8_ref = refs[3 * n_w + 3]
        ssem_w, rsem_w, lsem_w, ssem_c, rsem_c, ssem_m, rsem_m = refs[3 * n_w + 4:]
        x, y, c = _coords()
        me = (x, y, c)
        sibling = (x, y, 1 - c)
        chips = [(1 - x, y), (x, 1 - y), (1 - x, 1 - y)]

        def wcopies(k, block, to, own=False):
            out = []
            for a in range(n_w):
                slot = wg_refs[a].at[_lin(block)]
                out.append(pltpu.make_async_remote_copy(
                    src_ref=wb_refs[a] if own else slot, dst_ref=slot, send_sem=ssem_w.at[a, k], recv_sem=rsem_w.at[a, k],
                    device_id=to, device_id_type=MESH))
            return out

        c8_ref[...] = jnp.broadcast_to(c_ref[...], (8, D_MODEL))
        cg_ref[_lin(me)] = c8_ref[...]
        c_recv, c_send = _gather_to_all(c8_ref, cg_ref, ssem_c, rsem_c)

        for a in range(n_w):
            wb_refs[a][...] = w_refs[a][...].astype(BF16)
        first = wcopies(0, me, sibling, own=True)
        for j, chip in enumerate(chips):
            first += wcopies(1 + j, me, (*chip, c), own=True)
        late = [idx for idx in range(len(first)) if idx % n_w == 0 and idx >= n_w]
        for idx, cp in enumerate(first):
            if idx not in late:
                cp.start()
        mine = [pltpu.make_async_copy(wb_refs[a], wg_refs[a].at[_lin(me)], lsem_w.at[a]) for a in range(n_w)]
        for cp in mine:
            cp.start()

        c_recv()
        row = lax.broadcasted_iota(jnp.int32, (8, D_MODEL), 0)
        c_all = jnp.zeros((8, D_MODEL), F32)
        for d in range(N_DEV):
            c_all = c_all + jnp.where(row == d, cg_ref[d], 0.0)
        cc = jnp.broadcast_to(cctx_ref[...], (8, D_MODEL))
        a = jnp.concatenate([c_all * _sigmoid(c_all), cc * _sigmoid(cc)], axis=0)
        modblk_ref[...] = _dot(a.astype(BF16), wmod_ref[...].astype(BF16))
        modg_ref[_lin(me)] = modblk_ref[...]
        m_recv, m_send = _gather_to_all(modblk_ref, modg_ref, ssem_m, rsem_m)
        for idx in late:
            first[idx].start()
        m_recv()

        passed = []
        for j, chip in enumerate(chips):
            for cp in wcopies(1 + j, (*chip, c), me):
                cp.wait_recv()
            fwd = wcopies(4 + j, (*chip, c), sibling)
            for cp in fwd:
                cp.start()
            passed += fwd
        for cp in wcopies(0, sibling, me):
            cp.wait_recv()
        for j, chip in enumerate(chips):
            for cp in wcopies(4 + j, (*chip, 1 - c), me):
                cp.wait_recv()
        for cp in first + passed:
            cp.wait_send()
        for cp in mine:
            cp.wait()
        c_send()
        m_send()

    vm = pl.BlockSpec(memory_space=pltpu.VMEM)
    hbm = pl.BlockSpec(memory_space=pl.ANY)
    return pl.pallas_call(
        body, name="prologue_gather",
        out_shape=(jax.ShapeDtypeStruct((N_DEV, 8, D_MODEL), F32), jax.ShapeDtypeStruct((N_DEV, 16, n_mod), F32),
                   *[jax.ShapeDtypeStruct((N_DEV,) + s.shape, BF16) for s in shards]),
        in_specs=[vm] * (3 + n_w), out_specs=(vm, vm, *[hbm] * n_w),
        scratch_shapes=[pltpu.VMEM((16, n_mod), F32), *[pltpu.VMEM(s.shape, BF16) for s in shards],
                        pltpu.VMEM((8, D_MODEL), F32),
                        pltpu.SemaphoreType.DMA((n_w, 7)), pltpu.SemaphoreType.DMA((n_w, 7)), pltpu.SemaphoreType.DMA((n_w,)),
                        pltpu.SemaphoreType.DMA((7,)), pltpu.SemaphoreType.DMA((7,)),
                        pltpu.SemaphoreType.DMA((7,)), pltpu.SemaphoreType.DMA((7,))],
        compiler_params=pltpu.CompilerParams(vmem_limit_bytes=VMEM_LIMIT),
    )(c8, cctx8, w_mod_l, *shards)


def _modulated_input(is_ctx, x_ref, ctx_ref, mod_ref, bmod_ref, ng_ref):
    xt = jnp.where(is_ctx, ctx_ref[...], x_ref[...])
    shift = jnp.where(is_ctx, mod_ref[3:4, :] + bmod_ref[3:4, :], mod_ref[0:1, :] + bmod_ref[0:1, :])
    scale = jnp.where(is_ctx, mod_ref[4:5, :] + bmod_ref[4:5, :], mod_ref[1:2, :] + bmod_ref[1:2, :])
    r = lax.rsqrt(jnp.mean(xt * xt, axis=-1, keepdims=True) + NORM_EPS)
    xg = (xt * r) * ng_ref[...]
    h = xg * (1.0 + scale) + shift
    return xt, r, xg, h, scale


def _inproj_fwd(x, ctx, modrows, bmodrows, norm_g, w_in_p, q_lora_g, w_uq_p, kv_lora_g, w_ukv, qng_p, kng_p, cos, slo, shi):
    seq = x.shape[0]
    n_tiles = seq // TILE + 1
    tot = seq + TILE

    n_mla = R_Q + R_KV + 128

    n_x = n_tiles - 1

    def body(x_hbm, ctx_ref, mod_ref, bmod_ref, ng_ref, win_ref, qlg_ref, wuq_ref, kvlg_ref, wukv_ref, qng_ref, kng_ref,
             cos_ref, slo_ref, shi_ref, ug_ref, um_ref, q_ref, k_ref, v_ref, stash, xring, xsem):
        s = pl.program_id(0)

        def fetch(t, slot):
            return pltpu.make_async_copy(x_hbm.at[pl.ds(t * TILE, TILE), :], xring.at[slot], xsem.at[slot])

        rest = (mod_ref, bmod_ref, ng_ref, win_ref, qlg_ref, wuq_ref, kvlg_ref, wukv_ref, qng_ref, kng_ref,
                cos_ref, slo_ref, shi_ref, ug_ref, um_ref, q_ref, k_ref, v_ref)

        @pl.when(s == 0)
        def _():
            stash[...] = jnp.zeros_like(stash)
            fetch(0, 0).start()
            if n_x > 1:
                fetch(1, 1).start()
            stages(True, (ctx_ref, ctx_ref) + rest, stash, stash)

        t = jnp.minimum(s, n_tiles - 1) - 1
        for j in range(min(3, n_x)):
            @pl.when((s > 0) & (t % 3 == j))
            def _(j=j):
                @pl.when(s < n_tiles)
                def _():
                    fetch(t, j).wait()

                @pl.when(t + 2 < n_x)
                def _():
                    fetch(t + 2, (j + 2) % 3).start()

                stages(False, (xring.at[j], ctx_ref) + rest, stash, stash)

    def stages(is_ctx, refs, fill, prev_ref):
        (x_ref, ctx_ref, mod_ref, bmod_ref, ng_ref, win_ref, qlg_ref, wuq_ref, kvlg_ref, wukv_ref, qng_ref, kng_ref,
         cos_ref, slo_ref, shi_ref, ug_ref, um_ref, q_ref, k_ref, v_ref) = refs
        cos_t, slo_t, shi_t = cos_ref[...], slo_ref[...], shi_ref[...]
        prev = prev_ref[...]
        cq = prev[:, 0:R_Q]
        qn = cq * lax.rsqrt(jnp.mean(cq * cq, axis=-1, keepdims=True) + NORM_EPS) * qlg_ref[...]
        q = _dot_nt(qn.astype(BF16), wuq_ref[...])
        qg = qng_ref[...] * (ATTN_SCALE * LOG2_E)
        for hd in range(N_HEADS):
            qh = q[:, hd * D_HEAD_PAD:(hd + 1) * D_HEAD_PAD]
            rr = lax.rsqrt(jnp.sum(qh * qh, axis=-1, keepdims=True) * (1.0 / D_HEAD) + NORM_EPS)
            qy = qh * rr * qg
            q_ref[hd, :, 0:D_NOPE] = qy[:, 0:D_NOPE].astype(BF16)
            q_ref[hd, :, D_NOPE:D_HEAD_PAD] = _rope(qy[:, D_NOPE:D_HEAD_PAD], cos_t, slo_t, shi_t).astype(BF16)

        ckv = prev[:, R_Q:R_Q + R_KV]
        kvn = ckv * lax.rsqrt(jnp.mean(ckv * ckv, axis=-1, keepdims=True) + NORM_EPS) * kvlg_ref[...]
        kv = _dot(kvn.astype(BF16), wukv_ref[...])
        krz = prev[:, R_Q + R_KV:n_mla]
        kr_ss = jnp.sum(krz * krz, axis=-1, keepdims=True)
        kg = kng_ref[...]
        for hd in range(N_HEADS):
            kn = kv[:, hd * 256:hd * 256 + D_NOPE]
            rr = lax.rsqrt((jnp.sum(kn * kn, axis=-1, keepdims=True) + kr_ss) * (1.0 / D_HEAD) + NORM_EPS)
            k_ref[hd, :, 0:D_NOPE] = (kn * rr * kg[:, 0:D_NOPE]).astype(BF16)
            k_ref[hd, :, D_NOPE:D_HEAD_PAD] = _rope(krz * rr * kg[:, D_NOPE:D_HEAD_PAD], cos_t, slo_t, shi_t).astype(BF16)
            v_ref[hd] = kv[:, hd * 256 + D_NOPE:(hd + 1) * 256].astype(BF16)

        _, _, _, h, _ = _modulated_input(is_ctx, x_ref, ctx_ref, mod_ref, bmod_ref, ng_ref)
        hb = h.astype(BF16)
        ug_ref[...] = _dot_nt(hb, win_ref[N_MLA:D_IN_PROJ, :]).astype(BF16)
        um = _dot_nt(hb, win_ref[0:n_mla, :])
        lane = lax.broadcasted_iota(jnp.int32, (TILE, 128), 1)
        um = jnp.concatenate([um[:, 0:R_Q + R_KV], jnp.where(lane < D_ROPE, um[:, R_Q + R_KV:n_mla], 0.0)], axis=1)
        um_ref[...] = um
        fill[...] = um

    first = lambda s: jnp.minimum(s, n_tiles - 1)
    second = lambda s: jnp.maximum(s - 1, 0)
    latent = lambda t: jnp.maximum(t - 1, 0)
    full = lambda shape: pl.BlockSpec(shape, lambda s: (0,) * len(shape))
    rope_spec = pl.BlockSpec((TILE, 128), lambda s: (second(s), 0))
    return pl.pallas_call(
        body, name="inproj_fwd", grid=(n_tiles + 1,),
        out_shape=(jax.ShapeDtypeStruct((seq, N_GATES), BF16), jax.ShapeDtypeStruct((tot, n_mla), F32),
                   jax.ShapeDtypeStruct((N_HEADS, seq, D_HEAD_PAD), BF16),
                   jax.ShapeDtypeStruct((N_HEADS, tot, D_HEAD_PAD), BF16),
                   jax.ShapeDtypeStruct((N_HEADS, tot, D_V), BF16)),
        in_specs=[pl.BlockSpec(memory_space=pl.ANY), full((TILE, D_MODEL)), full((8, D_MODEL)),
                  full((8, D_MODEL)), full((1, D_MODEL)), full((D_IN_PROJ, D_MODEL)), full((1, R_Q)),
                  full((N_HEADS * D_HEAD_PAD, R_Q)), full((1, R_KV)), full((R_KV, N_HEADS * 256)), full((1, D_HEAD_PAD)),
                  full((1, D_HEAD_PAD)), rope_spec, rope_spec, rope_spec],
        out_specs=(pl.BlockSpec((TILE, N_GATES), lambda s: (latent(first(s)), 0)),
                   pl.BlockSpec((TILE, n_mla), lambda s: (first(s), 0)),
                   pl.BlockSpec((N_HEADS, TILE, D_HEAD_PAD), lambda s: (0, latent(second(s)), 0)),
                   pl.BlockSpec((N_HEADS, TILE, D_HEAD_PAD), lambda s: (0, second(s), 0)),
                   pl.BlockSpec((N_HEADS, TILE, D_V), lambda s: (0, second(s), 0))),
        scratch_shapes=[pltpu.VMEM((TILE, n_mla), F32), pltpu.VMEM((3, TILE, D_MODEL), F32), pltpu.SemaphoreType.DMA((3,))],
        compiler_params=pltpu.CompilerParams(dimension_semantics=("arbitrary",), vmem_limit_bytes=VMEM_LIMIT),
    )(x, ctx, modrows, bmodrows, norm_g, w_in_p, q_lora_g, w_uq_p, kv_lora_g, w_ukv, qng_p, kng_p, cos, slo, shi)


def _hosted_exchange(first, last, items, send_sems, recv_sems, local_sems):
    me = _lin(_coords())

    def remote(n, k, src, land, scatter):
        peer = _peer(k)
        return pltpu.make_async_remote_copy(
            src_ref=src.at[_lin(peer)] if scatter else src, dst_ref=land.at[me], send_sem=send_sems.at[n, k - 1],
            recv_sem=recv_sems.at[n, k - 1], device_id=peer, device_id_type=MESH)

    def local(n, src, land, scatter):
        return pltpu.make_async_copy(src.at[me] if scatter else src, land.at[me], local_sems.at[n])

    @pl.when(first)
    def _():
        for n, (src, land, scatter) in enumerate(items):
            for k in range(1, N_DEV):
                remote(n, k, src, land, scatter).start()
            local(n, src, land, scatter).start()

    @pl.when(last)
    def _():
        for n, (src, land, scatter) in enumerate(items):
            for k in range(1, N_DEV):
                peer = _peer(k)
                pltpu.make_async_remote_copy(
                    src_ref=src.at[0] if scatter else src, dst_ref=land.at[_lin(peer)], send_sem=send_sems.at[n, k - 1],
                    recv_sem=recv_sems.at[n, k - 1], device_id=peer, device_id_type=MESH).wait_recv()
            for k in range(1, N_DEV):
                remote(n, k, src, land, scatter).wait_send()
            local(n, src, land, scatter).wait()


def _attn_fwd(q, k, v, block_q, w_out_b):
    _, seq, _ = q.shape
    n_keys = k.shape[1]
    n_q = seq // block_q

    def body(q_ref, k_ref, v_ref, wo_ref, o_ref, lse_ref, wog_ref, ssem, rsem, lsem):
        h, i = pl.program_id(0), pl.program_id(1)
        _hosted_exchange((h == 0) & (i == 0), (h == N_HEADS - 1) & (i == n_q - 1), [(wo_ref, wog_ref, False)],
                         ssem, rsem, lsem)
        kb, vb = k_ref[0], v_ref[0]
        for r0 in range(0, block_q, ATTN_ROWS):
            rows = slice(r0, r0 + ATTN_ROWS)
            s = _dot_nt(q_ref[0, rows, :], kb)
            m = jnp.max(s, axis=-1, keepdims=True)
            p = jnp.exp2(s - m)
            l = jnp.sum(p, axis=-1, keepdims=True)
            o_ref[rows, :] = _dot(p.astype(BF16), vb) * (1.0 / l)
            lse_ref[0, rows, :] = m + jnp.log2(l)

    hbm = pl.BlockSpec(memory_space=pl.ANY)
    return pl.pallas_call(
        body, name="attn_fwd", grid=(N_HEADS, n_q),
        out_shape=(jax.ShapeDtypeStruct((seq, D_ATTN), F32), jax.ShapeDtypeStruct((N_HEADS, seq, 1), F32),
                   jax.ShapeDtypeStruct((N_DEV,) + w_out_b.shape, w_out_b.dtype)),
        in_specs=[pl.BlockSpec((1, block_q, D_HEAD_PAD), lambda h, i: (h, i, 0)),
                  pl.BlockSpec((1, n_keys, D_HEAD_PAD), lambda h, i: (h, 0, 0)),
                  pl.BlockSpec((1, n_keys, D_V), lambda h, i: (h, 0, 0)), hbm],
        out_specs=(pl.BlockSpec((block_q, D_V), lambda h, i: (i, h)),
                   pl.BlockSpec((1, block_q, 1), lambda h, i: (h, i, 0)), hbm),
        scratch_shapes=[pltpu.SemaphoreType.DMA((1, 7)), pltpu.SemaphoreType.DMA((1, 7)), pltpu.SemaphoreType.DMA((1,))],
        compiler_params=pltpu.CompilerParams(dimension_semantics=("arbitrary", "arbitrary"), vmem_limit_bytes=VMEM_LIMIT),
    )(q, k, v, w_out_b)


def _attn_bwd(q, k, v, o, lse, d_o, block_q, gwo_blocks, gwp):
    _, seq, _ = q.shape
    n_keys = k.shape[1]
    n_q = seq // block_q

    def body(q_ref, k_ref, v_ref, o_ref, lse_ref, do_ref, gwo_ref, gwp_ref, dq_ref, dkt_ref, dvt_ref, lwo_ref, lwp_ref,
             ssem, rsem, lsem):
        h, i = pl.program_id(0), pl.program_id(1)
        _hosted_exchange((h == 0) & (i == 0), (h == N_HEADS - 1) & (i == n_q - 1),
                         [(gwo_ref, lwo_ref, True), (gwp_ref, lwp_ref, False)], ssem, rsem, lsem)

        @pl.when(i == 0)
        def _():
            dkt_ref[...] = jnp.zeros_like(dkt_ref)
            dvt_ref[...] = jnp.zeros_like(dvt_ref)

        kb, vb = k_ref[0], v_ref[0]
        raws, ps = [], []
        for r0 in range(0, block_q, ATTN_ROWS):
            rows = slice(r0, r0 + ATTN_ROWS)
            d_out = do_ref[rows, :]
            p = jnp.exp2(_dot_nt(q_ref[0, rows, :], kb) - lse_ref[0, rows, :])
            dp = _dot_nt(d_out.astype(BF16), vb)
            delta = jnp.sum(d_out * o_ref[rows, :], axis=-1, keepdims=True)
            raw = (p * (dp - delta)).astype(BF16)
            dq_ref[0, rows, :] = _dot(raw, kb)
            raws.append(raw)
            ps.append(p.astype(BF16))
        dkt_ref[0] += _dot_tn(q_ref[0], jnp.concatenate(raws, axis=0))
        dvt_ref[0] += _dot_tn(do_ref[...].astype(BF16), jnp.concatenate(ps, axis=0))

    hbm = pl.BlockSpec(memory_space=pl.ANY)
    return pl.pallas_call(
        body, name="attn_bwd", grid=(N_HEADS, n_q),
        out_shape=(jax.ShapeDtypeStruct((N_HEADS, seq, D_HEAD_PAD), F32),
                   jax.ShapeDtypeStruct((N_HEADS, D_HEAD_PAD, n_keys), F32),
                   jax.ShapeDtypeStruct((N_HEADS, D_V, n_keys), F32),
                   jax.ShapeDtypeStruct(gwo_blocks.shape, gwo_blocks.dtype),
                   jax.ShapeDtypeStruct((N_DEV,) + gwp.shape, gwp.dtype)),
        in_specs=[pl.BlockSpec((1, block_q, D_HEAD_PAD), lambda h, i: (h, i, 0)),
                  pl.BlockSpec((1, n_keys, D_HEAD_PAD), lambda h, i: (h, 0, 0)),
                  pl.BlockSpec((1, n_keys, D_V), lambda h, i: (h, 0, 0)),
                  pl.BlockSpec((block_q, D_V), lambda h, i: (i, h)),
                  pl.BlockSpec((1, block_q, 1), lambda h, i: (h, i, 0)),
                  pl.BlockSpec((block_q, D_V), lambda h, i: (i, h)), hbm, hbm],
        out_specs=(pl.BlockSpec((1, block_q, D_HEAD_PAD), lambda h, i: (h, i, 0)),
                   pl.BlockSpec((1, D_HEAD_PAD, n_keys), lambda h, i: (h, 0, 0)),
                   pl.BlockSpec((1, D_V, n_keys), lambda h, i: (h, 0, 0)), hbm, hbm),
        scratch_shapes=[pltpu.SemaphoreType.DMA((2, 7)), pltpu.SemaphoreType.DMA((2, 7)), pltpu.SemaphoreType.DMA((2,))],
        compiler_params=pltpu.CompilerParams(dimension_semantics=("arbitrary", "arbitrary"), vmem_limit_bytes=VMEM_LIMIT),
    )(q, k, v, o, lse, d_o, gwo_blocks, gwp)


def _mix(x, target, attn, u, modrows, bmodrows, w_pool, pool_scale, w_out):
    seq = x.shape[0]
    rows = min(MIX_TILE, seq // 2)
    n_tiles = seq // rows
    rows8 = rows // HALO
    last8 = seq // HALO - 1

    n_blk = seq // Q_BLOCK
    per = rows // n_blk
    assert rows % n_blk == 0 and per % 8 == 0 and n_tiles >= 2
    n_stash = 8

    def body(x_ref, t_ref, o_hbm, ga_ref, pin_ref, hb_ref, ha_ref, gp_ref, mod_ref, bmod_ref, wp_ref, ps_ref, wo_ref,
             loss_ref, g1_ref, dgate_ref, do_hbm, dga_ref, dgp_ref, dpl_ref, gwob_ref, gwp_ref, dps_ref,
             abuf, dbuf, gwo_ref, *rest):
        stash_even, stash_odd = rest[0:n_stash], rest[n_stash:2 * n_stash]
        rsem, wsem = rest[2 * n_stash:]
        s = pl.program_id(0)

        def slab_copies(tile, slot, fetch):
            window = pl.ds(tile * per, per)
            if fetch:
                return [pltpu.make_async_copy(o_hbm.at[:, window, :], abuf.at[slot], rsem.at[slot])]
            return [pltpu.make_async_copy(dbuf.at[slot], do_hbm.at[:, window, :], wsem.at[slot])]

        def wait_all(slot, fetch):
            slab_copies(0, slot, fetch)[0].wait()

        a_slot = lax.rem(s, 2)
        b_slot = 1 - a_slot

        @pl.when(s == 0)
        def _():
            loss_ref[...] = jnp.zeros_like(loss_ref)
            dgate_ref[...] = jnp.zeros_like(dgate_ref)
            gwo_ref[...] = jnp.zeros_like(gwo_ref)
            gwp_ref[...] = jnp.zeros_like(gwp_ref)
            dps_ref[...] = jnp.zeros_like(dps_ref)
            for cp in slab_copies(0, 0, True):
                cp.start()

        @pl.when(s + 1 < n_tiles)
        def _():
            for cp in slab_copies(s + 1, b_slot, True):
                cp.start()

        @pl.when(s < n_tiles)
        def _():
            wait_all(a_slot, True)

        @pl.when(s >= 3)
        def _():
            wait_all(b_slot, False)

        gate = mod_ref[2:3, :] + bmod_ref[2:3, :]
        ps = ps_ref[...]

        def a_vector(slot):
            attn_t = jnp.swapaxes(abuf[slot], 0, 1).reshape(rows, D_ATTN)
            ga = ga_ref[...].astype(F32)
            sga = _sigmoid(ga)
            silu_ga = ga * sga
            pin = pin_ref[...].astype(F32)
            xs = jnp.concatenate([jnp.where(s > 0, hb_ref[...].astype(F32), 0.0), pin,
                                  jnp.where(s < n_tiles - 1, ha_ref[...].astype(F32), 0.0)], axis=0)
            tpos = s * rows + lax.broadcasted_iota(jnp.int32, (rows, 1), 0)
            pooled = []
            for g, w in enumerate(POOL_WINDOWS):
                sl = slice(g * GROUP_DIM, (g + 1) * GROUP_DIM)
                ws = _window_sum(xs[:, sl], w, False)[HALO:HALO + rows]
                pooled.append((ws * _inv_count(tpos, w, seq) - pin[:, sl]).astype(BF16))
            return attn_t, ga, sga, silu_ga, pooled

        def a_group_maps(pooled):
            return jnp.concatenate([_dot(pooled[g], wp_ref[g].astype(BF16)) for g in range(len(POOL_WINDOWS))], axis=1)

        def a_project(stash, yp, attn_t, ga, sga, silu_ga, pooled):
            zb_ref, _, fa_ref, sa_ref, fp_ref, sp_ref, yp_ref, pooled_ref = stash
            ypool = yp * ps
            gp = gp_ref[...].astype(F32)
            sgp = _sigmoid(gp)
            silu_gp = gp * sgp
            z = jnp.concatenate([silu_ga * attn_t, silu_gp * ypool], axis=1).astype(BF16)
            y = _dot(z, wo_ref[...])
            zb_ref[...] = z
            fa_ref[...] = attn_t * (sga * (1.0 + ga * (1.0 - sga)))
            sa_ref[...] = silu_ga
            fp_ref[...] = ypool * (sgp * (1.0 + gp * (1.0 - sgp)))
            sp_ref[...] = silu_gp
            yp_ref[...] = yp
            pooled_ref[...] = jnp.concatenate(pooled, axis=1)
            return y

        def a_loss(stash, y):
            res = x_ref[...] + gate * y - t_ref[...]
            loss_ref[...] += jnp.sum(res * res) * (0.5 / D_MODEL)
            dxn = res * (1.0 / D_MODEL)
            g1_ref[...] = dxn
            dgate_ref[...] += jnp.sum(dxn * y, axis=0, keepdims=True)
            stash[1][...] = (gate * dxn).astype(BF16)

        def b_dz(stash):
            return _dot_nt(stash[1][...], wo_ref[...])

        def b_gwo(stash):
            gwo_ref[...] += _dot_tn(stash[0][...], stash[1][...])

        def b_vector(stash, slot, dz):
            _, _, fa_ref, sa_ref, fp_ref, sp_ref, yp_ref, _ = stash
            dbra = dz[:, 0:D_ATTN]
            dbrp = dz[:, D_ATTN:]
            dga_ref[...] = (dbra * fa_ref[...]).astype(BF16)
            dbuf[slot] = jnp.swapaxes((dbra * sa_ref[...]).reshape(per, n_blk, D_ATTN), 0, 1)
            dgp_ref[...] = (dbrp * fp_ref[...]).astype(BF16)
            dyp_s = dbrp * sp_ref[...]
            dps_ref[...] += jnp.sum(dyp_s * yp_ref[...], axis=0, keepdims=True)
            return (dyp_s * ps).astype(BF16)

        def b_small(stash, dyp):
            pooled_ref = stash[7]
            for g in range(len(POOL_WINDOWS)):
                sl = slice(g * GROUP_DIM, (g + 1) * GROUP_DIM)
                gwp_ref[g] += _dot_tn(pooled_ref[:, sl], dyp[:, sl])
                dpl_ref[:, sl] = _dot_nt(dyp[:, sl], wp_ref[g].astype(BF16)).astype(BF16)

        def both(a_stash, b_stash, slot_a):
            dz = b_dz(b_stash)
            front = a_vector(slot_a)
            yp = a_group_maps(front[-1])
            b_gwo(b_stash)
            y = a_project(a_stash, yp, *front)
            dyp = b_vector(b_stash, 1 - slot_a, dz)
            a_loss(a_stash, y)
            b_small(b_stash, dyp)

        @pl.when(s == 0)
        def _():
            front = a_vector(0)
            a_loss(stash_even, a_project(stash_even, a_group_maps(front[-1]), *front))

        @pl.when((s > 0) & (s < n_tiles) & (a_slot == 0))
        def _():
            both(stash_even, stash_odd, 0)

        @pl.when((s > 0) & (s < n_tiles) & (a_slot == 1))
        def _():
            both(stash_odd, stash_even, 1)

        @pl.when(s == n_tiles)
        def _():
            last = (n_tiles - 1) % 2
            stash = stash_odd if last else stash_even
            dz = b_dz(stash)
            b_gwo(stash)
            b_small(stash, b_vector(stash, last, dz))
            gwob_ref[...] = gwo_ref[...].astype(BF16)

        @pl.when(s >= 1)
        def _():
            for cp in slab_copies(s - 1, b_slot, False):
                cp.start()

        @pl.when(s == n_tiles)
        def _():
            wait_all(b_slot, False)
            wait_all(a_slot, False)

    ta = lambda s: jnp.minimum(s, n_tiles - 1)
    tb = lambda s: jnp.maximum(s - 1, 0)
    tile = lambda w: pl.BlockSpec((rows, w), lambda s: (ta(s), 0))
    tile_b = lambda w: pl.BlockSpec((rows, w), lambda s: (tb(s), 0))
    ucol = lambda col: pl.BlockSpec((rows, 512), lambda s: (ta(s), col))
    full = lambda shape: pl.BlockSpec(shape, lambda s: (0,) * len(shape))
    stash_shapes = [pltpu.VMEM((rows, D_MODEL), BF16), pltpu.VMEM((rows, D_MODEL), BF16)] + \
        [pltpu.VMEM((rows, 512), F32)] * 5 + [pltpu.VMEM((rows, D_POOL), BF16)]
    return pl.pallas_call(
        body, name="mix_fwd_bwd", grid=(n_tiles + 1,),
        out_shape=(jax.ShapeDtypeStruct((8, 128), F32), jax.ShapeDtypeStruct((seq, D_MODEL), F32),
                   jax.ShapeDtypeStruct((1, D_MODEL), F32), jax.ShapeDtypeStruct((n_blk, Q_BLOCK, D_ATTN), F32),
                   jax.ShapeDtypeStruct((seq, D_ATTN), BF16), jax.ShapeDtypeStruct((seq, D_POOL), BF16),
                   jax.ShapeDtypeStruct((seq, D_POOL), BF16), jax.ShapeDtypeStruct((D_MODEL, D_MODEL), BF16),
                   jax.ShapeDtypeStruct((4, GROUP_DIM, GROUP_DIM), F32), jax.ShapeDtypeStruct((1, D_POOL), F32)),
        in_specs=[tile(D_MODEL), tile(D_MODEL), pl.BlockSpec(memory_space=pl.ANY), ucol(0), ucol(1),
                  pl.BlockSpec((HALO, 512), lambda s: (jnp.maximum(ta(s) * rows8 - 1, 0), 1)),
                  pl.BlockSpec((HALO, 512), lambda s: (jnp.minimum((ta(s) + 1) * rows8, last8), 1)),
                  ucol(2), full((8, D_MODEL)), full((8, D_MODEL)), full((4, GROUP_DIM, GROUP_DIM)), full((1, D_POOL)),
                  full((D_MODEL, D_MODEL))],
        out_specs=(full((8, 128)), tile(D_MODEL), full((1, D_MODEL)), pl.BlockSpec(memory_space=pl.ANY), tile_b(D_ATTN),
                   tile_b(D_POOL), tile_b(D_POOL), full((D_MODEL, D_MODEL)), full((4, GROUP_DIM, GROUP_DIM)),
                   full((1, D_POOL))),
        scratch_shapes=[pltpu.VMEM((2, n_blk, per, D_ATTN), F32), pltpu.VMEM((2, n_blk, per, D_ATTN), F32),
                        pltpu.VMEM((D_MODEL, D_MODEL), F32), *stash_shapes, *stash_shapes,
                        pltpu.SemaphoreType.DMA((2,)), pltpu.SemaphoreType.DMA((2,))],
        compiler_params=pltpu.CompilerParams(dimension_semantics=("arbitrary",), vmem_limit_bytes=VMEM_LIMIT),
    )(x, target, attn.reshape(n_blk, Q_BLOCK, D_ATTN), u, u, u, u, u, modrows, bmodrows, w_pool, pool_scale, w_out)


def _inproj_bwd(x, ctx, modrows, bmodrows, norm_g, w_in_p, q_lora_g, w_uq_p, kv_lora_g, w_ukv, qng_p, kng_p, cos, slo, shi,
                u, dq, dk, dv, dga, dgp, dpl, g1):
    seq = x.shape[0]
    n_tiles = seq // TILE + 1
    rows8 = TILE // HALO
    last8 = seq // HALO - 1

    def body(*refs):
        du_even, du_odd, dh_even, dh_odd = refs[-4:]
        gwin_ref, gwuq_ref, gwukv_ref, dqlg_ref, dkvlg_ref, dqng_ref, dkng_ref, dng_ref, dmod_ref = refs[-13:-4]
        s = pl.program_id(0)
        even = lax.rem(s, 2) == 0

        @pl.when(s == 0)
        def _():
            for ref in (gwin_ref, gwuq_ref, gwukv_ref, dqlg_ref, dkvlg_ref, dqng_ref, dkng_ref, dng_ref, dmod_ref,
                        du_odd, dh_even):
                ref[...] = jnp.zeros_like(ref)

        @pl.when((s < n_tiles) & even)
        def _():
            stages(s, refs[:-4], du_even, du_odd, dh_odd, dh_even, (True, True, True))

        @pl.when((s < n_tiles) & jnp.logical_not(even))
        def _():
            stages(s, refs[:-4], du_odd, du_even, dh_even, dh_odd, (True, True, True))

        for tail, run in ((n_tiles, (False, True, True)), (n_tiles + 1, (False, False, True))):
            @pl.when(s == tail)
            def _():
                if tail % 2 == 0:
                    stages(s, refs[:-4], du_even, du_odd, dh_odd, dh_even, run)
                else:
                    stages(s, refs[:-4], du_odd, du_even, dh_even, dh_odd, run)

    def stages(s, refs, du_ref, du_prev, dh_fill, dh_prev, run):
        (xb_ref, xc_ref, ctx_ref, mod_ref, bmod_ref, ng_ref, win_ref, qlg_ref, wuq_ref, kvlg_ref, wukv_ref, qng_ref, kng_ref,
         cos_ref, slo_ref, shi_ref, cq_ref, ckv_ref, kr_ref, dq_ref, dk_ref, dv_ref, dga_ref, dgp_ref, dpl_ref,
         hb_ref, ha_ref, g1_ref,
         gx_ref, gwin_ref, gwuq_ref, gwukv_ref, dqlg_ref, dkvlg_ref, dqng_ref, dkng_ref, dng_ref, dmod_ref) = refs


        i = s
        is_lat = s > 0
        if run[0]:
            cq = cq_ref[...]
            rq = lax.rsqrt(jnp.mean(cq * cq, axis=-1, keepdims=True) + NORM_EPS)
            qhat = cq * rq
            qnb = (qhat * qlg_ref[...]).astype(BF16)
            q = _dot_nt(qnb, wuq_ref[...])
            ckv = ckv_ref[...]
            rkv = lax.rsqrt(jnp.mean(ckv * ckv, axis=-1, keepdims=True) + NORM_EPS)
            kvhat = ckv * rkv
            kvnb = (kvhat * kvlg_ref[...]).astype(BF16)
            kv = _dot(kvnb, wukv_ref[...])

        if run[1]:
            _, _, _, h2, _ = _modulated_input(s <= 1, xb_ref, ctx_ref, mod_ref, bmod_ref, ng_ref)
            dub = du_prev[...]
            du_g, du_m = dub[:, 0:N_GATES], dub[:, N_GATES:U_PAD]
            h2b = h2.astype(BF16)
            dh_fill[...] = _dot(du_g, win_ref[N_MLA:D_IN_PROJ, :]) + _dot(du_m, win_ref[0:U_PAD - N_GATES, :])
            gwin_ref[N_MLA:D_IN_PROJ, :] += _dot_tn(du_g, h2b)
            gwin_ref[0:N_MLA, :] += _dot_tn(du_m, h2b)[0:N_MLA]

        if run[2]:
            ctx3 = s <= 2
            xt, r, xg, _, scale = _modulated_input(ctx3, xc_ref, ctx_ref, mod_ref, bmod_ref, ng_ref)
            dh = dh_prev[...]
            dshift = jnp.sum(dh, axis=0, keepdims=True)
            dscale = jnp.sum(dh * xg, axis=0, keepdims=True)
            zero = jnp.zeros_like(dshift)
            dmod_ref[0:1, :] += jnp.where(ctx3, zero, dshift)
            dmod_ref[1:2, :] += jnp.where(ctx3, zero, dscale)
            dmod_ref[2:3, :] += jnp.where(ctx3, dshift, zero)
            dmod_ref[3:4, :] += jnp.where(ctx3, dscale, zero)
            dxg = dh * (1.0 + scale)
            xr = xt * r
            dng_ref[...] += jnp.sum(dxg * xr, axis=0, keepdims=True)
            dxn = dxg * ng_ref[...]
            gx_ref[...] = g1_ref[...] + r * (dxn - xr * jnp.mean(dxn * xr, axis=-1, keepdims=True))

        if not run[0]:
            return

        cos_t, slo_t, shi_t = cos_ref[...], slo_ref[...], shi_ref[...]
        qg = qng_ref[...]
        dq_heads = []
        dqng = jnp.zeros((1, D_HEAD_PAD), F32)
        for hd in range(N_HEADS):
            qh = q[:, hd * D_HEAD_PAD:(hd + 1) * D_HEAD_PAD]
            rr = lax.rsqrt(jnp.sum(qh * qh, axis=-1, keepdims=True) * (1.0 / D_HEAD) + NORM_EPS)
            yq = qh * rr
            dpost = jnp.where(is_lat, dq_ref[hd] * ATTN_SCALE, 0.0)
            dyn = jnp.concatenate([dpost[:, 0:D_NOPE], _rope_t(dpost[:, D_NOPE:], cos_t, slo_t, shi_t)], axis=1)
            dqng = dqng + jnp.sum(dyn * yq, axis=0, keepdims=True)
            dyg = dyn * qg
            dq_heads.append(rr * (dyg - yq * (jnp.sum(dyg * yq, axis=-1, keepdims=True) * (1.0 / D_HEAD))))
        dqng_ref[...] += dqng
        dqf = jnp.concatenate(dq_heads, axis=1).astype(BF16)

        krz = kr_ref[...]
        kr_ss = jnp.sum(krz * krz, axis=-1, keepdims=True)
        kg = kng_ref[...]
        kg_n, kg_r = kg[:, 0:D_NOPE], kg[:, D_NOPE:]
        dkv_parts = []
        dkr = jnp.zeros((TILE, 128), F32)
        dkng_n = jnp.zeros((1, D_NOPE), F32)
        dkng_r = jnp.zeros((1, 128), F32)
        for hd in range(N_HEADS):
            kn = kv[:, hd * 256:hd * 256 + D_NOPE]
            rr = lax.rsqrt((jnp.sum(kn * kn, axis=-1, keepdims=True) + kr_ss) * (1.0 / D_HEAD) + NORM_EPS)
            yn, yr = kn * rr, krz * rr
            dk_h = jnp.transpose(dk_ref[hd]) * LN_2
            dpn = dk_h[:, 0:D_NOPE]
            dpr = _rope_t(dk_h[:, D_NOPE:], cos_t, slo_t, shi_t)
            dkng_n = dkng_n + jnp.sum(dpn * yn, axis=0, keepdims=True)
            dkng_r = dkng_r + jnp.sum(dpr * yr, axis=0, keepdims=True)
            dyn, dyr = dpn * kg_n, dpr * kg_r
            proj = (jnp.sum(dyn * yn, axis=-1, keepdims=True) + jnp.sum(dyr * yr, axis=-1, keepdims=True)) * (1.0 / D_HEAD)
            dkv_parts.append(rr * (dyn - yn * proj))
            dkv_parts.append(jnp.transpose(dv_ref[hd]))
            dkr = dkr + rr * (dyr - yr * proj)
        dkng_ref[:, 0:D_NOPE] += dkng_n
        dkng_ref[:, D_NOPE:] += dkng_r
        dkvf = jnp.concatenate(dkv_parts, axis=1).astype(BF16)

        lat_t = jnp.maximum(i - 1, 0)
        dpl_t = dpl_ref[...].astype(F32)
        ds = jnp.concatenate([jnp.where(i > 1, hb_ref[...].astype(F32), 0.0), dpl_t,
                              jnp.where(i < n_tiles - 1, ha_ref[...].astype(F32), 0.0)], axis=0)
        tpos = lat_t * TILE - HALO + lax.broadcasted_iota(jnp.int32, (TILE + 2 * HALO, 1), 0)
        for g, w in enumerate(POOL_WINDOWS):
            sl = slice(g * GROUP_DIM, (g + 1) * GROUP_DIM)
            wsum = _window_sum(ds[:, sl] * _inv_count(tpos, w, seq), w, True)[HALO:HALO + TILE]
            du_ref[:, O_PIN + g * GROUP_DIM:O_PIN + (g + 1) * GROUP_DIM] = jnp.where(
                is_lat, wsum - dpl_t[:, sl], 0.0).astype(BF16)

        du_ref[:, O_GA:O_GA + D_ATTN] = jnp.where(is_lat, dga_ref[...], jnp.zeros((), BF16))
        du_ref[:, O_GP:O_GP + D_POOL] = jnp.where(is_lat, dgp_ref[...], jnp.zeros((), BF16))
        du_ref[:, O_KR:U_PAD] = dkr.astype(BF16)

        gwuq_ref[...] += _dot_tn(dqf, qnb)
        dqn = _dot(dqf, wuq_ref[...])
        gwukv_ref[...] += _dot_tn(dkvf, kvnb)
        dkvn = _dot_nt(dkvf, wukv_ref[...])
        dqlg_ref[...] += jnp.sum(dqn * qhat, axis=0, keepdims=True)
        dqhat = dqn * qlg_ref[...]
        dcq = rq * (dqhat - qhat * jnp.mean(dqhat * qhat, axis=-1, keepdims=True))
        dkvlg_ref[...] += jnp.sum(dkvn * kvhat, axis=0, keepdims=True)
        dkvhat = dkvn * kvlg_ref[...]
        dckv = rkv * (dkvhat - kvhat * jnp.mean(dkvhat * kvhat, axis=-1, keepdims=True))
        du_ref[:, O_CQ:O_CQ + R_Q] = dcq.astype(BF16)
        du_ref[:, O_CKV:O_CKV + R_KV] = dckv.astype(BF16)

    t1 = lambda s: jnp.minimum(s, n_tiles - 1)
    t2 = lambda s: jnp.clip(s - 1, 0, n_tiles - 1)
    t3 = lambda s: jnp.clip(s - 2, 0, n_tiles - 1)
    latent = lambda t: jnp.maximum(t - 1, 0)
    lat = lambda s: (latent(t1(s)), 0)
    lat3 = lambda s: (latent(t3(s)), 0)
    full = lambda shape: pl.BlockSpec(shape, lambda s: (0,) * len(shape))
    rope_spec = pl.BlockSpec((TILE, 128), lambda s: (t1(s), 0))
    return pl.pallas_call(
        body, name="inproj_bwd", grid=(n_tiles + 2,),
        out_shape=(jax.ShapeDtypeStruct((seq, D_MODEL), F32), jax.ShapeDtypeStruct((D_IN_PROJ, D_MODEL), F32),
                   jax.ShapeDtypeStruct((N_HEADS * D_HEAD_PAD, R_Q), F32), jax.ShapeDtypeStruct((N_HEADS * 256, R_KV), F32),
                   jax.ShapeDtypeStruct((1, R_Q), F32), jax.ShapeDtypeStruct((1, R_KV), F32),
                   jax.ShapeDtypeStruct((1, D_HEAD_PAD), F32), jax.ShapeDtypeStruct((1, D_HEAD_PAD), F32),
                   jax.ShapeDtypeStruct((1, D_MODEL), F32), jax.ShapeDtypeStruct((8, D_MODEL), F32)),
        in_specs=[pl.BlockSpec((TILE, D_MODEL), lambda s: (latent(t2(s)), 0)), pl.BlockSpec((TILE, D_MODEL), lat3),
                  full((TILE, D_MODEL)), full((8, D_MODEL)), full((8, D_MODEL)),
                  full((1, D_MODEL)), full((D_IN_PROJ, D_MODEL)), full((1, R_Q)), full((N_HEADS * D_HEAD_PAD, R_Q)),
                  full((1, R_KV)), full((R_KV, N_HEADS * 256)), full((1, D_HEAD_PAD)), full((1, D_HEAD_PAD)),
                  rope_spec, rope_spec, rope_spec,
                  pl.BlockSpec((TILE, R_Q), lambda s: (t1(s), (O_CQ - N_GATES) // R_Q)),
                  pl.BlockSpec((TILE, R_KV), lambda s: (t1(s), (O_CKV - N_GATES) // R_KV)),
                  pl.BlockSpec((TILE, 128), lambda s: (t1(s), (O_KR - N_GATES) // 128)),
                  pl.BlockSpec((N_HEADS, TILE, D_HEAD_PAD), lambda s: (0, latent(t1(s)), 0)),
                  pl.BlockSpec((N_HEADS, D_HEAD_PAD, TILE), lambda s: (0, 0, t1(s))),
                  pl.BlockSpec((N_HEADS, D_V, TILE), lambda s: (0, 0, t1(s))),
                  pl.BlockSpec((TILE, D_ATTN), lat), pl.BlockSpec((TILE, D_POOL), lat), pl.BlockSpec((TILE, D_POOL), lat),
                  pl.BlockSpec((HALO, D_POOL), lambda s: (jnp.maximum((t1(s) - 1) * rows8 - 1, 0), 0)),
                  pl.BlockSpec((HALO, D_POOL), lambda s: (jnp.minimum(jnp.maximum(t1(s), 1) * rows8, last8), 0)),
                  pl.BlockSpec((TILE, D_MODEL), lat3)],
        out_specs=(pl.BlockSpec((TILE, D_MODEL), lat3), full((D_IN_PROJ, D_MODEL)), full((N_HEADS * D_HEAD_PAD, R_Q)),
                   full((N_HEADS * 256, R_KV)), full((1, R_Q)), full((1, R_KV)), full((1, D_HEAD_PAD)),
                   full((1, D_HEAD_PAD)), full((1, D_MODEL)), full((8, D_MODEL))),
        scratch_shapes=[pltpu.VMEM((TILE, U_PAD), BF16), pltpu.VMEM((TILE, U_PAD), BF16),
                        pltpu.VMEM((TILE, D_MODEL), F32), pltpu.VMEM((TILE, D_MODEL), F32)],
        compiler_params=pltpu.CompilerParams(dimension_semantics=("arbitrary",), vmem_limit_bytes=VMEM_LIMIT),
    )(x, x, ctx, modrows, bmodrows, norm_g, w_in_p, q_lora_g, w_uq_p, kv_lora_g, w_ukv, qng_p, kng_p, cos, slo, shi,
      u, u, u, dq, dk, dv, dga, dgp, dpl, dpl, dpl, g1)


SMALL_LAYOUT = ((0, D_MODEL), (1, R_Q), (2, R_KV), (3, D_HEAD_PAD), (4, D_HEAD_PAD), (5, D_POOL))
ROW_DMOD_B, ROW_DMOD_C, ROW_LOSS = 6, 9, 12


def _epilogue(parts, landed, smalls, dmod4, dgate, loss_acc, w_mod_l):
    n_a, n_l, n_s = len(parts), len(landed), len(smalls)
    n_mod = w_mod_l.shape[1]
    blk = [p.shape[1:] for p in parts]
    small_out = [D_MODEL, R_Q, R_KV, D_HEAD, D_HEAD, D_POOL]

    def body(*refs):
        part_refs = refs[0:n_a]
        land_refs = refs[n_a:n_a + n_l]
        small_in = refs[n_a + n_l:n_a + n_l + n_s]
        dmod4_ref, dgate_ref, loss_ref, wmod_ref = refs[n_a + n_l + n_s:n_a + n_l + n_s + 4]
        outs = refs[n_a + n_l + n_s + 4:]
        red_refs = outs[0:n_a]
        landsum_refs = outs[n_a:n_a + n_l]
        ccg_ref = outs[n_a + n_l]
        sum_refs = outs[n_a + n_l + 1:n_a + n_l + 1 + n_s]
        gbmod_ref, dmy_ref, lossout_ref = outs[n_a + n_l + 1 + n_s:n_a + n_l + 4 + n_s]
        scr = outs[n_a + n_l + 4 + n_s:]
        recv1, own1, sum1b, recv2 = scr[0:n_a], scr[n_a:2 * n_a], scr[2 * n_a:3 * n_a], scr[3 * n_a:4 * n_a]
        small_ref, smallg_ref, tot_ref, cc_ref = scr[4 * n_a:4 * n_a + 4]
        land_vmem = scr[4 * n_a + 4:4 * n_a + 4 + n_l]
        ssem1, rsem1, lsem, ssem2, rsem2, ssem_s, rsem_s, ssem_cc, rsem_cc, land_sem = scr[4 * n_a + 4 + n_l:]
        x, y, c = _coords()
        me = (x, y, c)
        sibling = (x, y, 1 - c)

        small_ref[...] = jnp.zeros_like(small_ref)
        for ref, (row, width) in zip(small_in, SMALL_LAYOUT):
            small_ref[row:row + 1, 0:width] = ref[...]
        small_ref[ROW_DMOD_B:ROW_DMOD_B + 2, :] = dmod4_ref[0:2, :]
        small_ref[ROW_DMOD_B + 2:ROW_DMOD_B + 3, :] = dgate_ref[...]
        small_ref[ROW_DMOD_C:ROW_DMOD_C + 2, :] = dmod4_ref[2:4, :]
        small_ref[ROW_LOSS:ROW_LOSS + 1, 0:128] = loss_ref[0:1, :]
        smallg_ref[_lin(me)] = small_ref[...]
        s_recv, s_send = _gather_to_all(small_ref, smallg_ref, ssem_s, rsem_s)

        stage1, own_copies = [], []
        for a in range(n_a):
            for b in range(4):
                dev = 4 * (b >> 1) + 2 * (b & 1)
                stage1.append(pltpu.make_async_remote_copy(
                    src_ref=part_refs[a].at[dev + (1 - c)], dst_ref=recv1[a].at[b],
                    send_sem=ssem1.at[a, b], recv_sem=rsem1.at[a, b], device_id=sibling, device_id_type=MESH))
                own_copies.append(pltpu.make_async_copy(part_refs[a].at[dev + c], own1[a].at[b], lsem.at[a, b]))
                stage1[-1].start()
                own_copies[-1].start()
        land_copies = [pltpu.make_async_copy(land_refs[n], land_vmem[n], land_sem.at[n]) for n in range(n_l)]
        for cp in land_copies:
            cp.start()

        s_recv()
        tot = smallg_ref[0]
        for d in range(1, N_DEV):
            tot = tot + smallg_ref[d]
        tot_ref[...] = tot
        for ref, (row, _), width in zip(sum_refs, SMALL_LAYOUT, small_out):
            ref[...] = tot_ref[row:row + 1, 0:width]
        lossout_ref[...] = tot_ref[8:16, 0:128]
        lin = _lin(me)

        def my_columns(three_rows):
            v = jnp.concatenate(three_rows, axis=1)
            out = jnp.zeros((1, n_mod), F32)
            for d in range(N_DEV):
                out = out + jnp.where(lin == d, v[:, d * n_mod:(d + 1) * n_mod], 0.0)
            return out

        rows3 = lambda ref, r0: [ref[r0 + t:r0 + t + 1, :] for t in range(3)]
        dmodc = rows3(tot_ref, ROW_DMOD_C)
        gbmod_ref[...] = jnp.concatenate(rows3(tot_ref, ROW_DMOD_B), axis=1) + jnp.concatenate(dmodc, axis=1)
        dmy_ref[...] = jnp.zeros_like(dmy_ref)
        for b in range(N_DEV):
            dmy_ref[b:b + 1, :] = my_columns(rows3(smallg_ref.at[b], ROW_DMOD_B))
        dmy = my_columns(dmodc)
        dmy_ref[N_DEV:N_DEV + 1, :] = dmy
        part = _dot_nt(jnp.broadcast_to(dmy, (8, n_mod)).astype(BF16), wmod_ref[...].astype(BF16))

        cc_ref[...] = part
        ccg_ref[_lin(me)] = part
        cc_recv, cc_send = _gather_to_all(cc_ref, ccg_ref, ssem_cc, rsem_cc)

        stage2 = []
        for a in range(n_a):
            for b in range(4):
                stage1[4 * a + b].wait_recv()
                own_copies[4 * a + b].wait()
                sum1b[a][b] = (own1[a][b] + recv1[a][b]).astype(BF16)
            for k in (1, 2, 3):
                tx, ty = _flip(x, (k >> 1) & 1), _flip(y, k & 1)
                stage2.append(pltpu.make_async_remote_copy(
                    src_ref=sum1b[a].at[2 * tx + ty], dst_ref=recv2[a].at[k - 1], send_sem=ssem2.at[a, k - 1],
                    recv_sem=rsem2.at[a, k - 1], device_id=(tx, ty, c), device_id_type=MESH))
                stage2[-1].start()
        for a in range(n_a):
            own = own1[a][2 * x + y] + recv1[a][2 * x + y]
            for k in (1, 2, 3):
                stage2[3 * a + k - 1].wait_recv()
                own = own + recv2[a][k - 1].astype(F32)
            red_refs[a][...] = own

        for cp in land_copies:
            cp.wait()
        for ref, out in zip(land_vmem, landsum_refs):
            acc = ref[0].astype(F32)
            for d in range(1, N_DEV):
                acc = acc + ref[d].astype(F32)
            out[...] = acc
        cc_recv()
        for cp in stage1 + stage2:
            cp.wait_send()
        s_send()
        cc_send()

    vm = pl.BlockSpec(memory_space=pltpu.VMEM)
    sds = jax.ShapeDtypeStruct
    return pl.pallas_call(
        body, name="epilogue_reduce",
        out_shape=(*[sds(s, F32) for s in blk], *[sds(a.shape[1:], F32) for a in landed], sds((N_DEV, 8, D_MODEL), F32),
                   *[sds((1, w), F32) for w in small_out], sds((1, 3 * D_MODEL), F32), sds((16, n_mod), F32),
                   sds((8, 128), F32)),
        in_specs=[pl.BlockSpec(memory_space=pl.ANY)] * (n_a + n_l) + [vm] * (n_s + 4),
        out_specs=tuple([vm] * (n_a + n_l + n_s + 4)),
        scratch_shapes=[*[pltpu.VMEM((4,) + s, F32) for s in blk], *[pltpu.VMEM((4,) + s, F32) for s in blk],
                        *[pltpu.VMEM((4,) + s, BF16) for s in blk], *[pltpu.VMEM((3,) + s, BF16) for s in blk],
                        pltpu.VMEM((SMALL_ROWS, D_MODEL), F32), pltpu.VMEM((N_DEV, SMALL_ROWS, D_MODEL), F32),
                        pltpu.VMEM((SMALL_ROWS, D_MODEL), F32), pltpu.VMEM((8, D_MODEL), F32),
                        *[pltpu.VMEM(a.shape, a.dtype) for a in landed],
                        pltpu.SemaphoreType.DMA((n_a, 4)), pltpu.SemaphoreType.DMA((n_a, 4)), pltpu.SemaphoreType.DMA((n_a, 4)),
                        pltpu.SemaphoreType.DMA((n_a, 3)), pltpu.SemaphoreType.DMA((n_a, 3)),
                        pltpu.SemaphoreType.DMA((7,)), pltpu.SemaphoreType.DMA((7,)),
                        pltpu.SemaphoreType.DMA((7,)), pltpu.SemaphoreType.DMA((7,)), pltpu.SemaphoreType.DMA((n_l,))],
        compiler_params=pltpu.CompilerParams(vmem_limit_bytes=VMEM_LIMIT),
    )(*parts, *landed, *smalls, dmod4, dgate, loss_acc, w_mod_l)


def _adamw(weights, grads, ms, vs, c_all_t, dmod_my, p2g):
    n = len(weights)

    def body(*refs):
        w_refs = refs[0:n]
        g_in = refs[n:2 * n - 2]
        m_refs = refs[2 * n - 2:3 * n - 2]
        v_refs = refs[3 * n - 2:4 * n - 2]
        cat_ref, dmod_ref, p2g_ref = refs[4 * n - 2:4 * n + 1]
        outs = refs[4 * n + 1:]
        g_refs, d_refs, nm_refs, nv_refs = outs[0:n], outs[n:2 * n], outs[2 * n:3 * n], outs[3 * n:4 * n]
        gcc_ref, gwm_ref = g_refs[0], g_refs[1]

        part = p2g_ref[0, 0:1, :]
        for d in range(1, N_DEV):
            part = part + p2g_ref[d, 0:1, :]
        cc = w_refs[0][...]
        sg = _sigmoid(cc)
        gcc_ref[...] = part * (sg * (1.0 + cc * (1.0 - sg)))
        cat = cat_ref[...]
        gwm_ref[...] = lax.dot_general(cat * _sigmoid(cat), dmod_ref[...], (((1,), (0,)), ((), ())), precision=HIGHEST,
                                       preferred_element_type=F32)
        for idx in range(n):
            if idx >= 2:
                g_refs[idx][...] = g_in[idx - 2][...]
            g = g_refs[idx][...]
            m = ADAM_B1 * m_refs[idx][...] + (1.0 - ADAM_B1) * g
            v = ADAM_B2 * v_refs[idx][...] + (1.0 - ADAM_B2) * (g * g)
            m_hat = m / (1.0 - ADAM_B1 ** ADAM_STEP)
            v_hat = v / (1.0 - ADAM_B2 ** ADAM_STEP)
            d_refs[idx][...] = -ADAM_LR * (m_hat / (jnp.sqrt(v_hat) + ADAM_EPS) + ADAM_WD * w_refs[idx][...])
            nm_refs[idx][...] = m
            nv_refs[idx][...] = v

    vm = pl.BlockSpec(memory_space=pltpu.VMEM)
    shapes = [jax.ShapeDtypeStruct(w.shape, F32) for w in weights]
    args = list(weights) + list(grads[2:]) + list(ms) + list(vs) + [c_all_t, dmod_my, p2g]
    out_shape = tuple(shapes * 4)
    return pl.pallas_call(
        body, name="adamw_update", out_shape=out_shape, in_specs=[vm] * len(args), out_specs=tuple([vm] * len(out_shape)),
        compiler_params=pltpu.CompilerParams(vmem_limit_bytes=VMEM_LIMIT),
    )(*args)


def _rope_tables(seq):
    rows = seq // GRID_W
    row = np.repeat(np.arange(rows, dtype=np.float32), GRID_W)
    col = np.tile(np.arange(GRID_W, dtype=np.float32), rows)
    n_freq = D_ROPE // 4
    inv = (np.float32(ROPE_BASE) ** (-np.arange(n_freq, dtype=np.float32) / np.float32(n_freq))).astype(np.float32)
    ang_r = (row[:, None] * inv).astype(np.float32)
    ang_c = (col[:, None] * inv).astype(np.float32)
    ang = np.concatenate([ang_r, ang_r, ang_c, ang_c], axis=-1)
    cos, sin = np.cos(ang).astype(np.float32), np.sin(ang).astype(np.float32)
    low = (np.arange(D_ROPE) % 32) < 16

    def table(t, fill):
        out = np.full((TILE + seq, 128), fill, np.float32)
        out[TILE:, 0:D_ROPE] = t
        return jnp.asarray(out)

    return table(cos, 1.0), table(np.where(low, -sin, 0.0), 0.0), table(np.where(low, 0.0, sin), 0.0)


def kernel(x, c, ctx, c_ctx, w_mod, b_mod, norm_g, w_in, q_lora_g, w_uq, kv_lora_g, w_ukv, q_norm_g, k_norm_g, w_pool, pool_scale, w_out, loss_target, m_c_ctx, m_w_mod, m_b_mod, m_norm_g, m_w_in, m_q_lora_g, m_w_uq, m_kv_lora_g, m_w_ukv, m_q_norm_g, m_k_norm_g, m_w_pool, m_pool_scale, m_w_out, v_c_ctx, v_w_mod, v_b_mod, v_norm_g, v_w_in, v_q_lora_g, v_w_uq, v_kv_lora_g, v_w_ukv, v_q_norm_g, v_k_norm_g, v_w_pool, v_pool_scale, v_w_out):
    seq = x.shape[1]
    assert ctx.shape[1] == TILE and seq % TILE == 0 and seq % GRID_W == 0
    ix, iy, ic = lax.axis_index("x"), lax.axis_index("y"), lax.axis_index("c")
    me = 4 * ix + 2 * iy + ic
    x2, ctx2, tgt2 = x[0], ctx[0], loss_target[0]
    w_mod_l, w_ukv_l, w_out_l = w_mod[0], w_ukv[0], w_out[0]
    c_ctx_row = c_ctx.reshape(1, D_MODEL)

    tr = lambda a: jnp.transpose(a[0])
    w_in_tl, w_uq_tl = tr(w_in), tr(w_uq)
    cg, modg, wg_in, wg_uq, wg_ukv = _prologue(
        c, c_ctx_row, w_mod_l,
        [w_in_tl, w_uq_tl, w_ukv_l])
    mod_all = jnp.transpose(modg, (1, 0, 2)).reshape(16, 3 * D_MODEL)
    mod_b = lax.dynamic_slice_in_dim(mod_all, me, 1, axis=0).reshape(3, D_MODEL)
    mod_c = mod_all[8].reshape(3, D_MODEL)
    modrows = jnp.concatenate([mod_b, mod_c[0:2], jnp.zeros((3, D_MODEL), F32)], axis=0)
    b3 = b_mod.reshape(3, D_MODEL)
    bmodrows = jnp.concatenate([b3, b3[0:2], jnp.zeros((3, D_MODEL), F32)], axis=0)

    w_in_p = wg_in.reshape(D_IN_PROJ, D_MODEL)
    w_uq_p = jnp.concatenate([wg_uq.reshape(N_HEADS, D_HEAD, R_Q), jnp.zeros((N_HEADS, D_HEAD_PAD - D_HEAD, R_Q), BF16)],
                             axis=1).reshape(N_HEADS * D_HEAD_PAD, R_Q)
    w_ukv_f = jnp.transpose(wg_ukv, (1, 0, 2)).reshape(R_KV, N_HEADS * 256)
    qng_p = jnp.concatenate([q_norm_g, jnp.zeros((1, D_HEAD_PAD - D_HEAD), F32)], axis=1)
    kng_p = jnp.concatenate([k_norm_g, jnp.zeros((1, D_HEAD_PAD - D_HEAD), F32)], axis=1)
    cos, slo, shi = _rope_tables(seq)

    u_gates, u_mla, q, k, v = _inproj_fwd(x2, ctx2, modrows, bmodrows, norm_g, w_in_p, q_lora_g, w_uq_p, kv_lora_g, w_ukv_f,
                             qng_p, kng_p, cos, slo, shi)
    attn, lse, wg_out = _attn_fwd(q, k, v, min(2048, seq), w_out_l.astype(BF16))
    (loss_acc, g1, dgate, d_attn, dga, dgp, dpl, gwo_b, gwp, dps) = _mix(
        x2, tgt2, attn, u_gates, modrows, bmodrows, w_pool[0], pool_scale, wg_out.reshape(D_MODEL, D_MODEL))

    blocks = lambda a: a.reshape((N_DEV, a.shape[0] // N_DEV) + a.shape[1:])
    dq, dk, dv, land_wo, land_wp = _attn_bwd(q, k, v, attn, lse, d_attn.reshape(seq, D_ATTN), min(1024, seq), blocks(gwo_b),
                                             gwp.reshape(4 * GROUP_DIM, GROUP_DIM))
    (grad_x, gwin_t, gwuq_p, gwukv_t, dqlg, dkvlg, dqng, dkng, dng, dmod4) = _inproj_bwd(
        x2, ctx2, modrows, bmodrows, norm_g, w_in_p, q_lora_g, w_uq_p, kv_lora_g, w_ukv_f, qng_p, kng_p, cos, slo, shi,
        u_mla, dq, dk, dv, dga, dgp, dpl, g1)

    gwuq_t = gwuq_p.reshape(N_HEADS, D_HEAD_PAD, R_Q)[:, 0:D_HEAD].reshape(N_HEADS * D_HEAD, R_Q)
    parts = [blocks(gwin_t), blocks(gwuq_t), blocks(gwukv_t)]
    (r_win, r_wuq, r_wukv, r_wout, g_w_pool, ccg, g_ng, g_qlg, g_kvlg, g_qng, g_kng, g_ps, g_bmod, dmod_my,
     loss_rows) = _epilogue(parts, [land_wo, land_wp], [dng, dqlg, dkvlg, dqng, dkng, dps], dmod4, dgate, loss_acc, w_mod_l)

    g_w_ukv = jnp.transpose(r_wukv)
    c_rows = cg[:, 0, :]
    c_all_t = jnp.transpose(jnp.concatenate([c_rows, c_ctx_row, jnp.zeros((16 - N_DEV - 1, D_MODEL), F32)], axis=0))

    weights = [c_ctx_row, w_mod_l, b_mod, norm_g, w_in_tl, q_lora_g, w_uq_tl, kv_lora_g, w_ukv_l, q_norm_g, k_norm_g,
               w_pool.reshape(4 * GROUP_DIM, GROUP_DIM), pool_scale, w_out_l]
    grads = [None, None, g_bmod, g_ng, r_win, g_qlg, r_wuq, g_kvlg, g_w_ukv, g_qng, g_kng, g_w_pool, g_ps, r_wout]
    ms = [m_c_ctx.reshape(1, D_MODEL), m_w_mod[0], m_b_mod, m_norm_g, tr(m_w_in), m_q_lora_g, tr(m_w_uq), m_kv_lora_g,
          m_w_ukv[0], m_q_norm_g, m_k_norm_g, m_w_pool.reshape(4 * GROUP_DIM, GROUP_DIM), m_pool_scale, m_w_out[0]]
    vs = [v_c_ctx.reshape(1, D_MODEL), v_w_mod[0], v_b_mod, v_norm_g, tr(v_w_in), v_q_lora_g, tr(v_w_uq), v_kv_lora_g,
          v_w_ukv[0], v_q_norm_g, v_k_norm_g, v_w_pool.reshape(4 * GROUP_DIM, GROUP_DIM), v_pool_scale, v_w_out[0]]
    outs = _adamw(weights, grads, ms, vs, c_all_t, dmod_my, ccg)
    n = len(weights)
    grads, deltas, new_m, new_v = outs[0:n], outs[n:2 * n], outs[2 * n:3 * n], outs[3 * n:4 * n]

    loss = loss_rows[ROW_LOSS - 8, 0]
    final = [c_ctx.shape, w_mod.shape, b_mod.shape, norm_g.shape, w_in.shape, q_lora_g.shape, w_uq.shape, kv_lora_g.shape,
             w_ukv.shape, q_norm_g.shape, k_norm_g.shape, w_pool.shape, pool_scale.shape, w_out.shape]
    transposed = (4, 6)

    def shaped(arrs):
        return [(jnp.transpose(a) if i in transposed else a).reshape(s) for i, (a, s) in enumerate(zip(arrs, final))]

    return (loss, grad_x[None], *shaped(grads), *shaped(deltas), *shaped(new_m), *shaped(new_v))
```

```python
import numpy as np

import jax
import jax.numpy as jnp
from jax import lax
from jax.experimental import pallas as pl
from jax.experimental.pallas import tpu as pltpu

F32 = jnp.float32
BF16 = jnp.bfloat16
MESH = pl.DeviceIdType.MESH
HIGHEST = lax.Precision.HIGHEST

D_MODEL = 1024
N_HEADS = 4
D_NOPE = 128
D_ROPE = 64
D_HEAD = D_NOPE + D_ROPE
D_HEAD_PAD = 256
D_V = 128
R_Q = 256
R_KV = 128
D_ATTN = 512
D_POOL = 512
POOL_WINDOWS = (2, 4, 8, 16)
GROUP_DIM = 128
GRID_W = 64
ROPE_BASE = 10000.0
NORM_EPS = 1e-6
ATTN_SCALE = D_HEAD ** -0.5
LOG2_E = 1.4426950408889634
LN_2 = 0.6931471805599453
ATTN_ROWS = 256
D_IN_PROJ = 1984
U_PAD = 2048
O_GA, O_PIN, O_GP, O_CQ, O_CKV, O_KR = 0, 512, 1024, 1536, 1792, 1920
TILE = 256
MIX_TILE = 512
Q_BLOCK = 128
HALO = 16
N_GATES = 1536
N_MLA = D_IN_PROJ - N_GATES
N_DEV = 8
VMEM_LIMIT = 56 * 1024 * 1024

ADAM_LR = 0.001
ADAM_B1 = 0.9
ADAM_B2 = 0.999
ADAM_EPS = 1e-08
ADAM_WD = 0.01
ADAM_STEP = 10

SMALL_ROWS = 16


def _dot(a, b):
    return lax.dot_general(a, b, (((1,), (0,)), ((), ())), preferred_element_type=F32)


def _dot_nt(a, b):
    return lax.dot_general(a, b, (((1,), (1,)), ((), ())), preferred_element_type=F32)


def _dot_tn(a, b):
    return lax.dot_general(a, b, (((0,), (0,)), ((), ())), preferred_element_type=F32)


def _sigmoid(x):
    return 1.0 / (1.0 + jnp.exp(-x))


def _rope(p, cos, slo, shi):
    return p * cos + pltpu.roll(p, 112, 1) * slo + pltpu.roll(p, 16, 1) * shi


def _rope_t(d, cos, slo, shi):
    return d * cos + pltpu.roll(d * slo, 16, 1) + pltpu.roll(d * shi, 112, 1)


def _shift_rows(a, k):
    n = a.shape[0]
    return pltpu.roll(a, (-k) % n, 0)


def _window_sum(x, w, transposed):
    s = (x + _shift_rows(x, 1)) if transposed else (_shift_rows(x, -1) + x)
    step = 1
    while 2 * step < w:
        s = _shift_rows(s, -step) + _shift_rows(s, step)
        step *= 2
    return s


def _inv_count(tpos, w, seq):
    lo = jnp.maximum(tpos - w // 2, 0)
    hi = jnp.minimum(tpos - w // 2 + w, seq)
    return 1.0 / jnp.maximum(hi - lo, 1).astype(F32)


def _coords():
    return lax.axis_index("x"), lax.axis_index("y"), lax.axis_index("c")


def _flip(v, bit):
    return (1 - v) if bit else v


def _peer(k):
    x, y, c = _coords()
    return (_flip(x, (k >> 2) & 1), _flip(y, (k >> 1) & 1), _flip(c, k & 1))


def _lin(p):
    return 4 * p[0] + 2 * p[1] + p[2]


def _gather_to_all(src_ref, slots_ref, send_sems, recv_sems):
    me = _coords()
    copies = []
    for k in range(1, N_DEV):
        cp = pltpu.make_async_remote_copy(
            src_ref=src_ref, dst_ref=slots_ref.at[_lin(me)], send_sem=send_sems.at[k - 1], recv_sem=recv_sems.at[k - 1],
            device_id=_peer(k), device_id_type=MESH)
        cp.start()
        copies.append(cp)

    def wait_recv():
        for k in range(1, N_DEV):
            pltpu.make_async_remote_copy(
                src_ref=src_ref, dst_ref=slots_ref.at[_lin(_peer(k))], send_sem=send_sems.at[k - 1],
                recv_sem=recv_sems.at[k - 1], device_id=_peer(k), device_id_type=MESH).wait_recv()

    def wait_send():
        for cp in copies:
            cp.wait_send()

    return wait_recv, wait_send


def _prologue(c8, cctx8, w_mod_l, shards):
    n_mod = w_mod_l.shape[1]
    n_w = len(shards)

    def body(c_ref, cctx_ref, wmod_ref, *refs):
        w_refs = refs[0:n_w]
        cg_ref, modg_ref = refs[n_w], refs[n_w + 1]
        wg_refs = refs[n_w + 2:2 * n_w + 2]
        modblk_ref = refs[2 * n_w + 2]
        wb_refs = refs[2 * n_w + 3:3 * n_w + 3]
        c8_ref = refs[3 * n_w + 3]
        ssem_w, rsem_w, lsem_w, ssem_c, rsem_c, ssem_m, rsem_m = refs[3 * n_w + 4:]
        x, y, c = _coords()
        me = (x, y, c)
        sibling = (x, y, 1 - c)
        chips = [(1 - x, y), (x, 1 - y), (1 - x, 1 - y)]

        def wcopies(k, block, to, own=False):
            out = []
            for a in range(n_w):
                slot = wg_refs[a].at[_lin(block)]
                out.append(pltpu.make_async_remote_copy(
                    src_ref=wb_refs[a] if own else slot, dst_ref=slot, send_sem=ssem_w.at[a, k], recv_sem=rsem_w.at[a, k],
                    device_id=to, device_id_type=MESH))
            return out

        c8_ref[...] = jnp.broadcast_to(c_ref[...], (8, D_MODEL))
        cg_ref[_lin(me)] = c8_ref[...]
        c_recv, c_send = _gather_to_all(c8_ref, cg_ref, ssem_c, rsem_c)

        for a in range(n_w):
            wb_refs[a][...] = w_refs[a][...].astype(BF16)
        first = wcopies(0, me, sibling, own=True)
        for j, chip in enumerate(chips):
            first += wcopies(1 + j, me, (*chip, c), own=True)
        late = [idx for idx in range(len(first)) if idx % n_w == 0 and idx >= n_w]
        for idx, cp in enumerate(first):
            if idx not in late:
                cp.start()
        mine = [pltpu.make_async_copy(wb_refs[a], wg_refs[a].at[_lin(me)], lsem_w.at[a]) for a in range(n_w)]
        for cp in mine:
            cp.start()

        c_recv()
        row = lax.broadcasted_iota(jnp.int32, (8, D_MODEL), 0)
        c_all = jnp.zeros((8, D_MODEL), F32)
        for d in range(N_DEV):
            c_all = c_all + jnp.where(row == d, cg_ref[d], 0.0)
        cc = jnp.broadcast_to(cctx_ref[...], (8, D_MODEL))
        a = jnp.concatenate([c_all * _sigmoid(c_all), cc * _sigmoid(cc)], axis=0)
        modblk_ref[...] = _dot(a.astype(BF16), wmod_ref[...].astype(BF16))
        modg_ref[_lin(me)] = modblk_ref[...]
        m_recv, m_send = _gather_to_all(modblk_ref, modg_ref, ssem_m, rsem_m)
        for idx in late:
            first[idx].start()
        m_recv()

        passed = []
        for j, chip in enumerate(chips):
            for cp in wcopies(1 + j, (*chip, c), me):
                cp.wait_recv()
            fwd = wcopies(4 + j, (*chip, c), sibling)
            for cp in fwd:
                cp.start()
            passed += fwd
        for cp in wcopies(0, sibling, me):
            cp.wait_recv()
        for j, chip in enumerate(chips):
            for cp in wcopies(4 + j, (*chip, 1 - c), me):
                cp.wait_recv()
        for cp in first + passed:
            cp.wait_send()
        for cp in mine:
            cp.wait()
        c_send()
        m_send()

    vm = pl.BlockSpec(memory_space=pltpu.VMEM)
    hbm = pl.BlockSpec(memory_space=pl.ANY)
    return pl.pallas_call(
        body, name="prologue_gather",
        out_shape=(jax.ShapeDtypeStruct((N_DEV, 8, D_MODEL), F32), jax.ShapeDtypeStruct((N_DEV, 16, n_mod), F32),
                   *[jax.ShapeDtypeStruct((N_DEV,) + s.shape, BF16) for s in shards]),
        in_specs=[vm] * (3 + n_w), out_specs=(vm, vm, *[hbm] * n_w),
        scratch_shapes=[pltpu.VMEM((16, n_mod), F32), *[pltpu.VMEM(s.shape, BF16) for s in shards],
                        pltpu.VMEM((8, D_MODEL), F32),
                        pltpu.SemaphoreType.DMA((n_w, 7)), pltpu.SemaphoreType.DMA((n_w, 7)), pltpu.SemaphoreType.DMA((n_w,)),
                        pltpu.SemaphoreType.DMA((7,)), pltpu.SemaphoreType.DMA((7,)),
                        pltpu.SemaphoreType.DMA((7,)), pltpu.SemaphoreType.DMA((7,))],
        compiler_params=pltpu.CompilerParams(vmem_limit_bytes=VMEM_LIMIT),
    )(c8, cctx8, w_mod_l, *shards)


def _modulated_input(is_ctx, x_ref, ctx_ref, mod_ref, bmod_ref, ng_ref):
    xt = jnp.where(is_ctx, ctx_ref[...], x_ref[...])
    shift = jnp.where(is_ctx, mod_ref[3:4, :] + bmod_ref[3:4, :], mod_ref[0:1, :] + bmod_ref[0:1, :])
    scale = jnp.where(is_ctx, mod_ref[4:5, :] + bmod_ref[4:5, :], mod_ref[1:2, :] + bmod_ref[1:2, :])
    r = lax.rsqrt(jnp.mean(xt * xt, axis=-1, keepdims=True) + NORM_EPS)
    xg = (xt * r) * ng_ref[...]
    h = xg * (1.0 + scale) + shift
    return xt, r, xg, h, scale


def _inproj_fwd(x, ctx, modrows, bmodrows, norm_g, w_in_p, q_lora_g, w_uq_p, kv_lora_g, w_ukv, qng_p, kng_p, cos, slo, shi):
    seq = x.shape[0]
    n_tiles = seq // TILE + 1
    tot = seq + TILE

    n_mla = R_Q + R_KV + 128

    n_x = n_tiles - 1

    def body(x_hbm, ctx_ref, mod_ref, bmod_ref, ng_ref, win_ref, qlg_ref, wuq_ref, kvlg_ref, wukv_ref, qng_ref, kng_ref,
             cos_ref, slo_ref, shi_ref, ug_ref, um_ref, q_ref, k_ref, v_ref, stash, xring, xsem):
        s = pl.program_id(0)

        def fetch(t, slot):
            return pltpu.make_async_copy(x_hbm.at[pl.ds(t * TILE, TILE), :], xring.at[slot], xsem.at[slot])

        rest = (mod_ref, bmod_ref, ng_ref, win_ref, qlg_ref, wuq_ref, kvlg_ref, wukv_ref, qng_ref, kng_ref,
                cos_ref, slo_ref, shi_ref, ug_ref, um_ref, q_ref, k_ref, v_ref)

        @pl.when(s == 0)
        def _():
            stash[...] = jnp.zeros_like(stash)
            xring[2] = jnp.zeros((TILE, D_MODEL), F32)
            fetch(0, 0).start()
            if n_x > 1:
                fetch(1, 1).start()

        t = jnp.maximum(jnp.minimum(s, n_tiles - 1) - 1, 0)
        slot = lax.rem(t, 3)

        @pl.when((s > 0) & (s < n_tiles))
        def _():
            fetch(t, slot).wait()

        @pl.when((s > 0) & (t + 2 < n_x))
        def _():
            fetch(t + 2, lax.rem(t + 2, 3)).start()

        stages(s == 0, (xring.at[jnp.where(s == 0, 2, slot)], ctx_ref) + rest, stash, stash)

    def stages(is_ctx, refs, fill, prev_ref):
        (x_ref, ctx_ref, mod_ref, bmod_ref, ng_ref, win_ref, qlg_ref, wuq_ref, kvlg_ref, wukv_ref, qng_ref, kng_ref,
         cos_ref, slo_ref, shi_ref, ug_ref, um_ref, q_ref, k_ref, v_ref) = refs
        cos_t, slo_t, shi_t = cos_ref[...], slo_ref[...], shi_ref[...]
        prev = prev_ref[...]
        cq = prev[:, 0:R_Q]
        qn = cq * lax.rsqrt(jnp.mean(cq * cq, axis=-1, keepdims=True) + NORM_EPS) * qlg_ref[...]
        q = _dot_nt(qn.astype(BF16), wuq_ref[...])
        qg = qng_ref[...] * (ATTN_SCALE * LOG2_E)
        for hd in range(N_HEADS):
            qh = q[:, hd * D_HEAD_PAD:(hd + 1) * D_HEAD_PAD]
            rr = lax.rsqrt(jnp.sum(qh * qh, axis=-1, keepdims=True) * (1.0 / D_HEAD) + NORM_EPS)
            qy = qh * rr * qg
            q_ref[hd, :, 0:D_NOPE] = qy[:, 0:D_NOPE].astype(BF16)
            q_ref[hd, :, D_NOPE:D_HEAD_PAD] = _rope(qy[:, D_NOPE:D_HEAD_PAD], cos_t, slo_t, shi_t).astype(BF16)

        ckv = prev[:, R_Q:R_Q + R_KV]
        kvn = ckv * lax.rsqrt(jnp.mean(ckv * ckv, axis=-1, keepdims=True) + NORM_EPS) * kvlg_ref[...]
        kv = _dot(kvn.astype(BF16), wukv_ref[...])
        krz = prev[:, R_Q + R_KV:n_mla]
        kr_ss = jnp.sum(krz * krz, axis=-1, keepdims=True)
        kg = kng_ref[...]
        for hd in range(N_HEADS):
            kn = kv[:, hd * 256:hd * 256 + D_NOPE]
            rr = lax.rsqrt((jnp.sum(kn * kn, axis=-1, keepdims=True) + kr_ss) * (1.0 / D_HEAD) + NORM_EPS)
            k_ref[hd, :, 0:D_NOPE] = (kn * rr * kg[:, 0:D_NOPE]).astype(BF16)
            k_ref[hd, :, D_NOPE:D_HEAD_PAD] = _rope(krz * rr * kg[:, D_NOPE:D_HEAD_PAD], cos_t, slo_t, shi_t).astype(BF16)
            v_ref[hd] = kv[:, hd * 256 + D_NOPE:(hd + 1) * 256].astype(BF16)

        _, _, _, h, _ = _modulated_input(is_ctx, x_ref, ctx_ref, mod_ref, bmod_ref, ng_ref)
        hb = h.astype(BF16)
        ug_ref[...] = _dot_nt(hb, win_ref[N_MLA:D_IN_PROJ, :]).astype(BF16)
        um = _dot_nt(hb, win_ref[0:n_mla, :])
        lane = lax.broadcasted_iota(jnp.int32, (TILE, 128), 1)
        um = jnp.concatenate([um[:, 0:R_Q + R_KV], jnp.where(lane < D_ROPE, um[:, R_Q + R_KV:n_mla], 0.0)], axis=1)
        um_ref[...] = um
        fill[...] = um

    first = lambda s: jnp.minimum(s, n_tiles - 1)
    second = lambda s: jnp.maximum(s - 1, 0)
    latent = lambda t: jnp.maximum(t - 1, 0)
    full = lambda shape: pl.BlockSpec(shape, lambda s: (0,) * len(shape))
    rope_spec = pl.BlockSpec((TILE, 128), lambda s: (second(s), 0))
    return pl.pallas_call(
        body, name="inproj_fwd", grid=(n_tiles + 1,),
        out_shape=(jax.ShapeDtypeStruct((seq, N_GATES), BF16), jax.ShapeDtypeStruct((tot, n_mla), F32),
                   jax.ShapeDtypeStruct((N_HEADS, seq, D_HEAD_PAD), BF16),
                   jax.ShapeDtypeStruct((N_HEADS, tot, D_HEAD_PAD), BF16),
                   jax.ShapeDtypeStruct((N_HEADS, tot, D_V), BF16)),
        in_specs=[pl.BlockSpec(memory_space=pl.ANY), full((TILE, D_MODEL)), full((8, D_MODEL)),
                  full((8, D_MODEL)), full((1, D_MODEL)), full((D_IN_PROJ, D_MODEL)), full((1, R_Q)),
                  full((N_HEADS * D_HEAD_PAD, R_Q)), full((1, R_KV)), full((R_KV, N_HEADS * 256)), full((1, D_HEAD_PAD)),
                  full((1, D_HEAD_PAD)), rope_spec, rope_spec, rope_spec],
        out_specs=(pl.BlockSpec((TILE, N_GATES), lambda s: (latent(first(s)), 0)),
                   pl.BlockSpec((TILE, n_mla), lambda s: (first(s), 0)),
                   pl.BlockSpec((N_HEADS, TILE, D_HEAD_PAD), lambda s: (0, latent(second(s)), 0)),
                   pl.BlockSpec((N_HEADS, TILE, D_HEAD_PAD), lambda s: (0, second(s), 0)),
                   pl.BlockSpec((N_HEADS, TILE, D_V), lambda s: (0, second(s), 0))),
        scratch_shapes=[pltpu.VMEM((TILE, n_mla), F32), pltpu.VMEM((3, TILE, D_MODEL), F32), pltpu.SemaphoreType.DMA((3,))],
        compiler_params=pltpu.CompilerParams(dimension_semantics=("arbitrary",), vmem_limit_bytes=VMEM_LIMIT),
    )(x, ctx, modrows, bmodrows, norm_g, w_in_p, q_lora_g, w_uq_p, kv_lora_g, w_ukv, qng_p, kng_p, cos, slo, shi)


def _hosted_exchange(first, last, items, send_sems, recv_sems, local_sems):
    me = _lin(_coords())

    def remote(n, k, src, land, scatter):
        peer = _peer(k)
        return pltpu.make_async_remote_copy(
            src_ref=src.at[_lin(peer)] if scatter else src, dst_ref=land.at[me], send_sem=send_sems.at[n, k - 1],
            recv_sem=recv_sems.at[n, k - 1], device_id=peer, device_id_type=MESH)

    def local(n, src, land, scatter):
        return pltpu.make_async_copy(src.at[me] if scatter else src, land.at[me], local_sems.at[n])

    @pl.when(first)
    def _():
        for n, (src, land, scatter) in enumerate(items):
            for k in range(1, N_DEV):
                remote(n, k, src, land, scatter).start()
            local(n, src, land, scatter).start()

    @pl.when(last)
    def _():
        for n, (src, land, scatter) in enumerate(items):
            for k in range(1, N_DEV):
                peer = _peer(k)
                pltpu.make_async_remote_copy(
                    src_ref=src.at[0] if scatter else src, dst_ref=land.at[_lin(peer)], send_sem=send_sems.at[n, k - 1],
                    recv_sem=recv_sems.at[n, k - 1], device_id=peer, device_id_type=MESH).wait_recv()
            for k in range(1, N_DEV):
                remote(n, k, src, land, scatter).wait_send()
            local(n, src, land, scatter).wait()


def _attn_fwd(q, k, v, block_q, w_out_b):
    _, seq, _ = q.shape
    n_keys = k.shape[1]
    n_q = seq // block_q

    def body(q_ref, k_ref, v_ref, wo_ref, o_ref, lse_ref, wog_ref, ssem, rsem, lsem):
        h, i = pl.program_id(0), pl.program_id(1)
        _hosted_exchange((h == 0) & (i == 0), (h == N_HEADS - 1) & (i == n_q - 1), [(wo_ref, wog_ref, False)],
                         ssem, rsem, lsem)
        kb, vb = k_ref[0], v_ref[0]
        for r0 in range(0, block_q, ATTN_ROWS):
            rows = slice(r0, r0 + ATTN_ROWS)
            s = _dot_nt(q_ref[0, rows, :], kb)
            m = jnp.max(s, axis=-1, keepdims=True)
            p = jnp.exp2(s - m)
            l = jnp.sum(p, axis=-1, keepdims=True)
            o_ref[rows, :] = _dot(p.astype(BF16), vb) * (1.0 / l)
            lse_ref[0, rows, :] = m + jnp.log2(l)

    hbm = pl.BlockSpec(memory_space=pl.ANY)
    return pl.pallas_call(
        body, name="attn_fwd", grid=(N_HEADS, n_q),
        out_shape=(jax.ShapeDtypeStruct((seq, D_ATTN), F32), jax.ShapeDtypeStruct((N_HEADS, seq, 1), F32),
                   jax.ShapeDtypeStruct((N_DEV,) + w_out_b.shape, w_out_b.dtype)),
        in_specs=[pl.BlockSpec((1, block_q, D_HEAD_PAD), lambda h, i: (h, i, 0)),
                  pl.BlockSpec((1, n_keys, D_HEAD_PAD), lambda h, i: (h, 0, 0)),
                  pl.BlockSpec((1, n_keys, D_V), lambda h, i: (h, 0, 0)), hbm],
        out_specs=(pl.BlockSpec((block_q, D_V), lambda h, i: (i, h)),
                   pl.BlockSpec((1, block_q, 1), lambda h, i: (h, i, 0)), hbm),
        scratch_shapes=[pltpu.SemaphoreType.DMA((1, 7)), pltpu.SemaphoreType.DMA((1, 7)), pltpu.SemaphoreType.DMA((1,))],
        compiler_params=pltpu.CompilerParams(dimension_semantics=("arbitrary", "arbitrary"), vmem_limit_bytes=VMEM_LIMIT),
    )(q, k, v, w_out_b)


def _attn_bwd(q, k, v, o, lse, d_o, block_q, gwo_blocks, gwp):
    _, seq, _ = q.shape
    n_keys = k.shape[1]
    n_q = seq // block_q

    def body(q_ref, k_ref, v_ref, o_ref, lse_ref, do_ref, gwo_ref, gwp_ref, dq_ref, dkt_ref, dvt_ref, lwo_ref, lwp_ref,
             ssem, rsem, lsem):
        h, i = pl.program_id(0), pl.program_id(1)
        _hosted_exchange((h == 0) & (i == 0), (h == N_HEADS - 1) & (i == n_q - 1),
                         [(gwo_ref, lwo_ref, True), (gwp_ref, lwp_ref, False)], ssem, rsem, lsem)

        @pl.when(i == 0)
        def _():
            dkt_ref[...] = jnp.zeros_like(dkt_ref)
            dvt_ref[...] = jnp.zeros_like(dvt_ref)

        kb, vb = k_ref[0], v_ref[0]
        raws, ps = [], []
        for r0 in range(0, block_q, ATTN_ROWS):
            rows = slice(r0, r0 + ATTN_ROWS)
            d_out = do_ref[rows, :]
            p = jnp.exp2(_dot_nt(q_ref[0, rows, :], kb) - lse_ref[0, rows, :])
            dp = _dot_nt(d_out.astype(BF16), vb)
            delta = jnp.sum(d_out * o_ref[rows, :], axis=-1, keepdims=True)
            raw = (p * (dp - delta)).astype(BF16)
            dq_ref[0, rows, :] = _dot(raw, kb)
            raws.append(raw)
            ps.append(p.astype(BF16))
        dkt_ref[0] += _dot_tn(q_ref[0], jnp.concatenate(raws, axis=0))
        dvt_ref[0] += _dot_tn(do_ref[...].astype(BF16), jnp.concatenate(ps, axis=0))

    hbm = pl.BlockSpec(memory_space=pl.ANY)
    return pl.pallas_call(
        body, name="attn_bwd", grid=(N_HEADS, n_q),
        out_shape=(jax.ShapeDtypeStruct((N_HEADS, seq, D_HEAD_PAD), F32),
                   jax.ShapeDtypeStruct((N_HEADS, D_HEAD_PAD, n_keys), F32),
                   jax.ShapeDtypeStruct((N_HEADS, D_V, n_keys), F32),
                   jax.ShapeDtypeStruct(gwo_blocks.shape, gwo_blocks.dtype),
                   jax.ShapeDtypeStruct((N_DEV,) + gwp.shape, gwp.dtype)),
        in_specs=[pl.BlockSpec((1, block_q, D_HEAD_PAD), lambda h, i: (h, i, 0)),
                  pl.BlockSpec((1, n_keys, D_HEAD_PAD), lambda h, i: (h, 0, 0)),
                  pl.BlockSpec((1, n_keys, D_V), lambda h, i: (h, 0, 0)),
                  pl.BlockSpec((block_q, D_V), lambda h, i: (i, h)),
                  pl.BlockSpec((1, block_q, 1), lambda h, i: (h, i, 0)),
                  pl.BlockSpec((block_q, D_V), lambda h, i: (i, h)), hbm, hbm],
        out_specs=(pl.BlockSpec((1, block_q, D_HEAD_PAD), lambda h, i: (h, i, 0)),
                   pl.BlockSpec((1, D_HEAD_PAD, n_keys), lambda h, i: (h, 0, 0)),
                   pl.BlockSpec((1, D_V, n_keys), lambda h, i: (h, 0, 0)), hbm, hbm),
        scratch_shapes=[pltpu.SemaphoreType.DMA((2, 7)), pltpu.SemaphoreType.DMA((2, 7)), pltpu.SemaphoreType.DMA((2,))],
        compiler_params=pltpu.CompilerParams(dimension_semantics=("arbitrary", "arbitrary"), vmem_limit_bytes=VMEM_LIMIT),
    )(q, k, v, o, lse, d_o, gwo_blocks, gwp)


def _mix(x, target, attn, u, modrows, bmodrows, w_pool, pool_scale, w_out):
    seq = x.shape[0]
    rows = min(MIX_TILE, seq // 2)
    n_tiles = seq // rows
    rows8 = rows // HALO
    last8 = seq // HALO - 1

    n_blk = seq // Q_BLOCK
    per = rows // n_blk
    assert rows % n_blk == 0 and per % 8 == 0 and n_tiles >= 2
    n_stash = 8

    def body(x_ref, t_ref, o_hbm, ga_ref, pin_ref, hb_ref, ha_ref, gp_ref, mod_ref, bmod_ref, wp_ref, ps_ref, wo_ref,
             loss_ref, g1_ref, dgate_ref, do_hbm, dga_ref, dgp_ref, dpl_ref, gwob_ref, gwp_ref, dps_ref,
             abuf, dbuf, gwo_ref, *rest):
        stash_even, stash_odd = rest[0:n_stash], rest[n_stash:2 * n_stash]
        rsem, wsem = rest[2 * n_stash:]
        s = pl.program_id(0)

        def slab_copies(tile, slot, fetch):
            window = pl.ds(tile * per, per)
            if fetch:
                return [pltpu.make_async_copy(o_hbm.at[:, window, :], abuf.at[slot], rsem.at[slot])]
            return [pltpu.make_async_copy(dbuf.at[slot], do_hbm.at[:, window, :], wsem.at[slot])]

        def wait_all(slot, fetch):
            slab_copies(0, slot, fetch)[0].wait()

        a_slot = lax.rem(s, 2)
        b_slot = 1 - a_slot

        @pl.when(s == 0)
        def _():
            loss_ref[...] = jnp.zeros_like(loss_ref)
            dgate_ref[...] = jnp.zeros_like(dgate_ref)
            gwo_ref[...] = jnp.zeros_like(gwo_ref)
            gwp_ref[...] = jnp.zeros_like(gwp_ref)
            dps_ref[...] = jnp.zeros_like(dps_ref)
            for cp in slab_copies(0, 0, True):
                cp.start()

        @pl.when(s + 1 < n_tiles)
        def _():
            for cp in slab_copies(s + 1, b_slot, True):
                cp.start()

        @pl.when(s < n_tiles)
        def _():
            wait_all(a_slot, True)

        @pl.when(s >= 3)
        def _():
            wait_all(b_slot, False)

        gate = mod_ref[2:3, :] + bmod_ref[2:3, :]
        ps = ps_ref[...]

        def a_vector(slot):
            attn_t = jnp.swapaxes(abuf[slot], 0, 1).reshape(rows, D_ATTN)
            ga = ga_ref[...].astype(F32)
            sga = _sigmoid(ga)
            silu_ga = ga * sga
            pin = pin_ref[...].astype(F32)
            xs = jnp.concatenate([jnp.where(s > 0, hb_ref[...].astype(F32), 0.0), pin,
                                  jnp.where(s < n_tiles - 1, ha_ref[...].astype(F32), 0.0)], axis=0)
            tpos = s * rows + lax.broadcasted_iota(jnp.int32, (rows, 1), 0)
            pooled = []
            for g, w in enumerate(POOL_WINDOWS):
                sl = slice(g * GROUP_DIM, (g + 1) * GROUP_DIM)
                ws = _window_sum(xs[:, sl], w, False)[HALO:HALO + rows]
                pooled.append((ws * _inv_count(tpos, w, seq) - pin[:, sl]).astype(BF16))
            return attn_t, ga, sga, silu_ga, pooled

        def a_group_maps(pooled):
            return jnp.concatenate([_dot(pooled[g], wp_ref[g].astype(BF16)) for g in range(len(POOL_WINDOWS))], axis=1)

        def a_project(stash, yp, attn_t, ga, sga, silu_ga, pooled):
            zb_ref, _, fa_ref, sa_ref, fp_ref, sp_ref, yp_ref, pooled_ref = stash
            ypool = yp * ps
            gp = gp_ref[...].astype(F32)
            sgp = _sigmoid(gp)
            silu_gp = gp * sgp
            z = jnp.concatenate([silu_ga * attn_t, silu_gp * ypool], axis=1).astype(BF16)
            y = _dot(z, wo_ref[...])
            zb_ref[...] = z
            fa_ref[...] = attn_t * (sga * (1.0 + ga * (1.0 - sga)))
            sa_ref[...] = silu_ga
            fp_ref[...] = ypool * (sgp * (1.0 + gp * (1.0 - sgp)))
            sp_ref[...] = silu_gp
            yp_ref[...] = yp
            pooled_ref[...] = jnp.concatenate(pooled, axis=1)
            return y

        def a_loss(stash, y):
            res = x_ref[...] + gate * y - t_ref[...]
            loss_ref[...] += jnp.sum(res * res) * (0.5 / D_MODEL)
            dxn = res * (1.0 / D_MODEL)
            g1_ref[...] = dxn
            dgate_ref[...] += jnp.sum(dxn * y, axis=0, keepdims=True)
            stash[1][...] = (gate * dxn).astype(BF16)

        def b_dz(stash):
            return _dot_nt(stash[1][...], wo_ref[...])

        def b_gwo(stash):
            gwo_ref[...] += _dot_tn(stash[0][...], stash[1][...])

        def b_vector(stash, slot, dz):
            _, _, fa_ref, sa_ref, fp_ref, sp_ref, yp_ref, _ = stash
            dbra = dz[:, 0:D_ATTN]
            dbrp = dz[:, D_ATTN:]
            dga_ref[...] = (dbra * fa_ref[...]).astype(BF16)
            dbuf[slot] = jnp.swapaxes((dbra * sa_ref[...]).reshape(per, n_blk, D_ATTN), 0, 1)
            dgp_ref[...] = (dbrp * fp_ref[...]).astype(BF16)
            dyp_s = dbrp * sp_ref[...]
            dps_ref[...] += jnp.sum(dyp_s * yp_ref[...], axis=0, keepdims=True)
            return (dyp_s * ps).astype(BF16)

        def b_small(stash, dyp):
            pooled_ref = stash[7]
            for g in range(len(POOL_WINDOWS)):
                sl = slice(g * GROUP_DIM, (g + 1) * GROUP_DIM)
                gwp_ref[g] += _dot_tn(pooled_ref[:, sl], dyp[:, sl])
                dpl_ref[:, sl] = _dot_nt(dyp[:, sl], wp_ref[g].astype(BF16)).astype(BF16)

        def both(a_stash, b_stash, slot_a):
            dz = b_dz(b_stash)
            front = a_vector(slot_a)
            yp = a_group_maps(front[-1])
            b_gwo(b_stash)
            y = a_project(a_stash, yp, *front)
            dyp = b_vector(b_stash, 1 - slot_a, dz)
            a_loss(a_stash, y)
            b_small(b_stash, dyp)

        @pl.when(s == 0)
        def _():
            front = a_vector(0)
            a_loss(stash_even, a_project(stash_even, a_group_maps(front[-1]), *front))

        @pl.when((s > 0) & (s < n_tiles) & (a_slot == 0))
        def _():
            both(stash_even, stash_odd, 0)

        @pl.when((s > 0) & (s < n_tiles) & (a_slot == 1))
        def _():
            both(stash_odd, stash_even, 1)

        @pl.when(s == n_tiles)
        def _():
            last = (n_tiles - 1) % 2
            stash = stash_odd if last else stash_even
            dz = b_dz(stash)
            b_gwo(stash)
            b_small(stash, b_vector(stash, last, dz))
            gwob_ref[...] = gwo_ref[...].astype(BF16)

        @pl.when(s >= 1)
        def _():
            for cp in slab_copies(s - 1, b_slot, False):
                cp.start()

        @pl.when(s == n_tiles)
        def _():
            wait_all(b_slot, False)
            wait_all(a_slot, False)

    ta = lambda s: jnp.minimum(s, n_tiles - 1)
    tb = lambda s: jnp.maximum(s - 1, 0)
    tile = lambda w: pl.BlockSpec((rows, w), lambda s: (ta(s), 0))
    tile_b = lambda w: pl.BlockSpec((rows, w), lambda s: (tb(s), 0))
    ucol = lambda col: pl.BlockSpec((rows, 512), lambda s: (ta(s), col))
    full = lambda shape: pl.BlockSpec(shape, lambda s: (0,) * len(shape))
    stash_shapes = [pltpu.VMEM((rows, D_MODEL), BF16), pltpu.VMEM((rows, D_MODEL), BF16)] + \
        [pltpu.VMEM((rows, 512), F32)] * 5 + [pltpu.VMEM((rows, D_POOL), BF16)]
    return pl.pallas_call(
        body, name="mix_fwd_bwd", grid=(n_tiles + 1,),
        out_shape=(jax.ShapeDtypeStruct((8, 128), F32), jax.ShapeDtypeStruct((seq, D_MODEL), F32),
                   jax.ShapeDtypeStruct((1, D_MODEL), F32), jax.ShapeDtypeStruct((n_blk, Q_BLOCK, D_ATTN), F32),
                   jax.ShapeDtypeStruct((seq, D_ATTN), BF16), jax.ShapeDtypeStruct((seq, D_POOL), BF16),
                   jax.ShapeDtypeStruct((seq, D_POOL), BF16), jax.ShapeDtypeStruct((D_MODEL, D_MODEL), BF16),
                   jax.ShapeDtypeStruct((4, GROUP_DIM, GROUP_DIM), F32), jax.ShapeDtypeStruct((1, D_POOL), F32)),
        in_specs=[tile(D_MODEL), tile(D_MODEL), pl.BlockSpec(memory_space=pl.ANY), ucol(0), ucol(1),
                  pl.BlockSpec((HALO, 512), lambda s: (jnp.maximum(ta(s) * rows8 - 1, 0), 1)),
                  pl.BlockSpec((HALO, 512), lambda s: (jnp.minimum((ta(s) + 1) * rows8, last8), 1)),
                  ucol(2), full((8, D_MODEL)), full((8, D_MODEL)), full((4, GROUP_DIM, GROUP_DIM)), full((1, D_POOL)),
                  full((D_MODEL, D_MODEL))],
        out_specs=(full((8, 128)), tile(D_MODEL), full((1, D_MODEL)), pl.BlockSpec(memory_space=pl.ANY), tile_b(D_ATTN),
                   tile_b(D_POOL), tile_b(D_POOL), full((D_MODEL, D_MODEL)), full((4, GROUP_DIM, GROUP_DIM)),
                   full((1, D_POOL))),
        scratch_shapes=[pltpu.VMEM((2, n_blk, per, D_ATTN), F32), pltpu.VMEM((2, n_blk, per, D_ATTN), F32),
                        pltpu.VMEM((D_MODEL, D_MODEL), F32), *stash_shapes, *stash_shapes,
                        pltpu.SemaphoreType.DMA((2,)), pltpu.SemaphoreType.DMA((2,))],
        compiler_params=pltpu.CompilerParams(dimension_semantics=("arbitrary",), vmem_limit_bytes=VMEM_LIMIT),
    )(x, target, attn.reshape(n_blk, Q_BLOCK, D_ATTN), u, u, u, u, u, modrows, bmodrows, w_pool, pool_scale, w_out)


def _inproj_bwd(x, ctx, modrows, bmodrows, norm_g, w_in_p, q_lora_g, w_uq_p, kv_lora_g, w_ukv, qng_p, kng_p, cos, slo, shi,
                u, dq, dk, dv, dga, dgp, dpl, g1):
    seq = x.shape[0]
    n_tiles = seq // TILE + 1
    rows8 = TILE // HALO
    last8 = seq // HALO - 1

    def body(*refs):
        du_even, du_odd, dh_even, dh_odd = refs[-4:]
        gwin_ref, gwuq_ref, gwukv_ref, dqlg_ref, dkvlg_ref, dqng_ref, dkng_ref, dng_ref, dmod_ref = refs[-13:-4]
        s = pl.program_id(0)
        even = lax.rem(s, 2) == 0

        @pl.when(s == 0)
        def _():
            for ref in (gwin_ref, gwuq_ref, gwukv_ref, dqlg_ref, dkvlg_ref, dqng_ref, dkng_ref, dng_ref, dmod_ref,
                        du_odd, dh_even):
                ref[...] = jnp.zeros_like(ref)

        @pl.when((s < n_tiles) & even)
        def _():
            stages(s, refs[:-4], du_even, du_odd, dh_odd, dh_even, (True, True, True))

        @pl.when((s < n_tiles) & jnp.logical_not(even))
        def _():
            stages(s, refs[:-4], du_odd, du_even, dh_even, dh_odd, (True, True, True))

        for tail, run in ((n_tiles, (False, True, True)), (n_tiles + 1, (False, False, True))):
            @pl.when(s == tail)
            def _():
                if tail % 2 == 0:
                    stages(s, refs[:-4], du_even, du_odd, dh_odd, dh_even, run)
                else:
                    stages(s, refs[:-4], du_odd, du_even, dh_even, dh_odd, run)

    def stages(s, refs, du_ref, du_prev, dh_fill, dh_prev, run):
        (xb_ref, xc_ref, ctx_ref, mod_ref, bmod_ref, ng_ref, win_ref, qlg_ref, wuq_ref, kvlg_ref, wukv_ref, qng_ref, kng_ref,
         cos_ref, slo_ref, shi_ref, cq_ref, ckv_ref, kr_ref, dq_ref, dk_ref, dv_ref, dga_ref, dgp_ref, dpl_ref,
         hb_ref, ha_ref, g1_ref,
         gx_ref, gwin_ref, gwuq_ref, gwukv_ref, dqlg_ref, dkvlg_ref, dqng_ref, dkng_ref, dng_ref, dmod_ref) = refs


        i = s
        is_lat = s > 0
        if run[0]:
            cq = cq_ref[...]
            rq = lax.rsqrt(jnp.mean(cq * cq, axis=-1, keepdims=True) + NORM_EPS)
            qhat = cq * rq
            qnb = (qhat * qlg_ref[...]).astype(BF16)
            q = _dot_nt(qnb, wuq_ref[...])
            ckv = ckv_ref[...]
            rkv = lax.rsqrt(jnp.mean(ckv * ckv, axis=-1, keepdims=True) + NORM_EPS)
            kvhat = ckv * rkv
            kvnb = (kvhat * kvlg_ref[...]).astype(BF16)
            kv = _dot(kvnb, wukv_ref[...])

        if run[1]:
            _, _, _, h2, _ = _modulated_input(s <= 1, xb_ref, ctx_ref, mod_ref, bmod_ref, ng_ref)
            dub = du_prev[...]
            du_g, du_m = dub[:, 0:N_GATES], dub[:, N_GATES:U_PAD]
            h2b = h2.astype(BF16)
            dh_fill[...] = _dot(du_g, win_ref[N_MLA:D_IN_PROJ, :]) + _dot(du_m, win_ref[0:U_PAD - N_GATES, :])
            gwin_ref[N_MLA:D_IN_PROJ, :] += _dot_tn(du_g, h2b)
            gwin_ref[0:N_MLA, :] += _dot_tn(du_m, h2b)[0:N_MLA]

        if run[2]:
            ctx3 = s <= 2
            xt, r, xg, _, scale = _modulated_input(ctx3, xc_ref, ctx_ref, mod_ref, bmod_ref, ng_ref)
            dh = dh_prev[...]
            dshift = jnp.sum(dh, axis=0, keepdims=True)
            dscale = jnp.sum(dh * xg, axis=0, keepdims=True)
            zero = jnp.zeros_like(dshift)
            dmod_ref[0:1, :] += jnp.where(ctx3, zero, dshift)
            dmod_ref[1:2, :] += jnp.where(ctx3, zero, dscale)
            dmod_ref[2:3, :] += jnp.where(ctx3, dshift, zero)
            dmod_ref[3:4, :] += jnp.where(ctx3, dscale, zero)
            dxg = dh * (1.0 + scale)
            xr = xt * r
            dng_ref[...] += jnp.sum(dxg * xr, axis=0, keepdims=True)
            dxn = dxg * ng_ref[...]
            gx_ref[...] = g1_ref[...] + r * (dxn - xr * jnp.mean(dxn * xr, axis=-1, keepdims=True))

        if not run[0]:
            return

        cos_t, slo_t, shi_t = cos_ref[...], slo_ref[...], shi_ref[...]
        qg = qng_ref[...]
        dq_heads = []
        dqng = jnp.zeros((1, D_HEAD_PAD), F32)
        for hd in range(N_HEADS):
            qh = q[:, hd * D_HEAD_PAD:(hd + 1) * D_HEAD_PAD]
            rr = lax.rsqrt(jnp.sum(qh * qh, axis=-1, keepdims=True) * (1.0 / D_HEAD) + NORM_EPS)
            yq = qh * rr
            dpost = jnp.where(is_lat, dq_ref[hd] * ATTN_SCALE, 0.0)
            dyn = jnp.concatenate([dpost[:, 0:D_NOPE], _rope_t(dpost[:, D_NOPE:], cos_t, slo_t, shi_t)], axis=1)
            dqng = dqng + jnp.sum(dyn * yq, axis=0, keepdims=True)
            dyg = dyn * qg
            dq_heads.append(rr * (dyg - yq * (jnp.sum(dyg * yq, axis=-1, keepdims=True) * (1.0 / D_HEAD))))
        dqng_ref[...] += dqng
        dqf = jnp.concatenate(dq_heads, axis=1).astype(BF16)

        krz = kr_ref[...]
        kr_ss = jnp.sum(krz * krz, axis=-1, keepdims=True)
        kg = kng_ref[...]
        kg_n, kg_r = kg[:, 0:D_NOPE], kg[:, D_NOPE:]
        dkv_parts = []
        dkr = jnp.zeros((TILE, 128), F32)
        dkng_n = jnp.zeros((1, D_NOPE), F32)
        dkng_r = jnp.zeros((1, 128), F32)
        for hd in range(N_HEADS):
            kn = kv[:, hd * 256:hd * 256 + D_NOPE]
            rr = lax.rsqrt((jnp.sum(kn * kn, axis=-1, keepdims=True) + kr_ss) * (1.0 / D_HEAD) + NORM_EPS)
            yn, yr = kn * rr, krz * rr
            dk_h = jnp.transpose(dk_ref[hd]) * LN_2
            dpn = dk_h[:, 0:D_NOPE]
            dpr = _rope_t(dk_h[:, D_NOPE:], cos_t, slo_t, shi_t)
            dkng_n = dkng_n + jnp.sum(dpn * yn, axis=0, keepdims=True)
            dkng_r = dkng_r + jnp.sum(dpr * yr, axis=0, keepdims=True)
            dyn, dyr = dpn * kg_n, dpr * kg_r
            proj = (jnp.sum(dyn * yn, axis=-1, keepdims=True) + jnp.sum(dyr * yr, axis=-1, keepdims=True)) * (1.0 / D_HEAD)
            dkv_parts.append(rr * (dyn - yn * proj))
            dkv_parts.append(jnp.transpose(dv_ref[hd]))
            dkr = dkr + rr * (dyr - yr * proj)
        dkng_ref[:, 0:D_NOPE] += dkng_n
        dkng_ref[:, D_NOPE:] += dkng_r
        dkvf = jnp.concatenate(dkv_parts, axis=1).astype(BF16)

        lat_t = jnp.maximum(i - 1, 0)
        dpl_t = dpl_ref[...].astype(F32)
        ds = jnp.concatenate([jnp.where(i > 1, hb_ref[...].astype(F32), 0.0), dpl_t,
                              jnp.where(i < n_tiles - 1, ha_ref[...].astype(F32), 0.0)], axis=0)
        tpos = lat_t * TILE - HALO + lax.broadcasted_iota(jnp.int32, (TILE + 2 * HALO, 1), 0)
        for g, w in enumerate(POOL_WINDOWS):
            sl = slice(g * GROUP_DIM, (g + 1) * GROUP_DIM)
            wsum = _window_sum(ds[:, sl] * _inv_count(tpos, w, seq), w, True)[HALO:HALO + TILE]
            du_ref[:, O_PIN + g * GROUP_DIM:O_PIN + (g + 1) * GROUP_DIM] = jnp.where(
                is_lat, wsum - dpl_t[:, sl], 0.0).astype(BF16)

        du_ref[:, O_GA:O_GA + D_ATTN] = jnp.where(is_lat, dga_ref[...], jnp.zeros((), BF16))
        du_ref[:, O_GP:O_GP + D_POOL] = jnp.where(is_lat, dgp_ref[...], jnp.zeros((), BF16))
        du_ref[:, O_KR:U_PAD] = dkr.astype(BF16)

        gwuq_ref[...] += _dot_tn(dqf, qnb)
        dqn = _dot(dqf, wuq_ref[...])
        gwukv_ref[...] += _dot_tn(dkvf, kvnb)
        dkvn = _dot_nt(dkvf, wukv_ref[...])
        dqlg_ref[...] += jnp.sum(dqn * qhat, axis=0, keepdims=True)
        dqhat = dqn * qlg_ref[...]
        dcq = rq * (dqhat - qhat * jnp.mean(dqhat * qhat, axis=-1, keepdims=True))
        dkvlg_ref[...] += jnp.sum(dkvn * kvhat, axis=0, keepdims=True)
        dkvhat = dkvn * kvlg_ref[...]
        dckv = rkv * (dkvhat - kvhat * jnp.mean(dkvhat * kvhat, axis=-1, keepdims=True))
        du_ref[:, O_CQ:O_CQ + R_Q] = dcq.astype(BF16)
        du_ref[:, O_CKV:O_CKV + R_KV] = dckv.astype(BF16)

    t1 = lambda s: jnp.minimum(s, n_tiles - 1)
    t2 = lambda s: jnp.clip(s - 1, 0, n_tiles - 1)
    t3 = lambda s: jnp.clip(s - 2, 0, n_tiles - 1)
    latent = lambda t: jnp.maximum(t - 1, 0)
    lat = lambda s: (latent(t1(s)), 0)
    lat3 = lambda s: (latent(t3(s)), 0)
    full = lambda shape: pl.BlockSpec(shape, lambda s: (0,) * len(shape))
    rope_spec = pl.BlockSpec((TILE, 128), lambda s: (t1(s), 0))
    return pl.pallas_call(
        body, name="inproj_bwd", grid=(n_tiles + 2,),
        out_shape=(jax.ShapeDtypeStruct((seq, D_MODEL), F32), jax.ShapeDtypeStruct((D_IN_PROJ, D_MODEL), F32),
                   jax.ShapeDtypeStruct((N_HEADS * D_HEAD_PAD, R_Q), F32), jax.ShapeDtypeStruct((N_HEADS * 256, R_KV), F32),
                   jax.ShapeDtypeStruct((1, R_Q), F32), jax.ShapeDtypeStruct((1, R_KV), F32),
                   jax.ShapeDtypeStruct((1, D_HEAD_PAD), F32), jax.ShapeDtypeStruct((1, D_HEAD_PAD), F32),
                   jax.ShapeDtypeStruct((1, D_MODEL), F32), jax.ShapeDtypeStruct((8, D_MODEL), F32)),
        in_specs=[pl.BlockSpec((TILE, D_MODEL), lambda s: (latent(t2(s)), 0)), pl.BlockSpec((TILE, D_MODEL), lat3),
                  full((TILE, D_MODEL)), full((8, D_MODEL)), full((8, D_MODEL)),
                  full((1, D_MODEL)), full((D_IN_PROJ, D_MODEL)), full((1, R_Q)), full((N_HEADS * D_HEAD_PAD, R_Q)),
                  full((1, R_KV)), full((R_KV, N_HEADS * 256)), full((1, D_HEAD_PAD)), full((1, D_HEAD_PAD)),
                  rope_spec, rope_spec, rope_spec,
                  pl.BlockSpec((TILE, R_Q), lambda s: (t1(s), (O_CQ - N_GATES) // R_Q)),
                  pl.BlockSpec((TILE, R_KV), lambda s: (t1(s), (O_CKV - N_GATES) // R_KV)),
                  pl.BlockSpec((TILE, 128), lambda s: (t1(s), (O_KR - N_GATES) // 128)),
                  pl.BlockSpec((N_HEADS, TILE, D_HEAD_PAD), lambda s: (0, latent(t1(s)), 0)),
                  pl.BlockSpec((N_HEADS, D_HEAD_PAD, TILE), lambda s: (0, 0, t1(s))),
                  pl.BlockSpec((N_HEADS, D_V, TILE), lambda s: (0, 0, t1(s))),
                  pl.BlockSpec((TILE, D_ATTN), lat), pl.BlockSpec((TILE, D_POOL), lat), pl.BlockSpec((TILE, D_POOL), lat),
                  pl.BlockSpec((HALO, D_POOL), lambda s: (jnp.maximum((t1(s) - 1) * rows8 - 1, 0), 0)),
                  pl.BlockSpec((HALO, D_POOL), lambda s: (jnp.minimum(jnp.maximum(t1(s), 1) * rows8, last8), 0)),
                  pl.BlockSpec((TILE, D_MODEL), lat3)],
        out_specs=(pl.BlockSpec((TILE, D_MODEL), lat3), full((D_IN_PROJ, D_MODEL)), full((N_HEADS * D_HEAD_PAD, R_Q)),
                   full((N_HEADS * 256, R_KV)), full((1, R_Q)), full((1, R_KV)), full((1, D_HEAD_PAD)),
                   full((1, D_HEAD_PAD)), full((1, D_MODEL)), full((8, D_MODEL))),
        scratch_shapes=[pltpu.VMEM((TILE, U_PAD), BF16), pltpu.VMEM((TILE, U_PAD), BF16),
                        pltpu.VMEM((TILE, D_MODEL), F32), pltpu.VMEM((TILE, D_MODEL), F32)],
        compiler_params=pltpu.CompilerParams(dimension_semantics=("arbitrary",), vmem_limit_bytes=VMEM_LIMIT),
    )(x, x, ctx, modrows, bmodrows, norm_g, w_in_p, q_lora_g, w_uq_p, kv_lora_g, w_ukv, qng_p, kng_p, cos, slo, shi,
      u, u, u, dq, dk, dv, dga, dgp, dpl, dpl, dpl, g1)


SMALL_LAYOUT = ((0, D_MODEL), (1, R_Q), (2, R_KV), (3, D_HEAD_PAD), (4, D_HEAD_PAD), (5, D_POOL))
ROW_DMOD_B, ROW_DMOD_C, ROW_LOSS = 6, 9, 12


def _epilogue(parts, landed, smalls, dmod4, dgate, loss_acc, w_mod_l):
    n_a, n_l, n_s = len(parts), len(landed), len(smalls)
    n_mod = w_mod_l.shape[1]
    blk = [p.shape[1:] for p in parts]
    small_out = [D_MODEL, R_Q, R_KV, D_HEAD, D_HEAD, D_POOL]

    def body(*refs):
        part_refs = refs[0:n_a]
        land_refs = refs[n_a:n_a + n_l]
        small_in = refs[n_a + n_l:n_a + n_l + n_s]
        dmod4_ref, dgate_ref, loss_ref, wmod_ref = refs[n_a + n_l + n_s:n_a + n_l + n_s + 4]
        outs = refs[n_a + n_l + n_s + 4:]
        red_refs = outs[0:n_a]
        landsum_refs = outs[n_a:n_a + n_l]
        ccg_ref = outs[n_a + n_l]
        sum_refs = outs[n_a + n_l + 1:n_a + n_l + 1 + n_s]
        gbmod_ref, dmy_ref, lossout_ref = outs[n_a + n_l + 1 + n_s:n_a + n_l + 4 + n_s]
        scr = outs[n_a + n_l + 4 + n_s:]
        recv1, own1, sum1b, recv2 = scr[0:n_a], scr[n_a:2 * n_a], scr[2 * n_a:3 * n_a], scr[3 * n_a:4 * n_a]
        small_ref, smallg_ref, tot_ref, cc_ref = scr[4 * n_a:4 * n_a + 4]
        land_vmem = scr[4 * n_a + 4:4 * n_a + 4 + n_l]
        ssem1, rsem1, lsem, ssem2, rsem2, ssem_s, rsem_s, ssem_cc, rsem_cc, land_sem = scr[4 * n_a + 4 + n_l:]
        x, y, c = _coords()
        me = (x, y, c)
        sibling = (x, y, 1 - c)

        small_ref[...] = jnp.zeros_like(small_ref)
        for ref, (row, width) in zip(small_in, SMALL_LAYOUT):
            small_ref[row:row + 1, 0:width] = ref[...]
        small_ref[ROW_DMOD_B:ROW_DMOD_B + 2, :] = dmod4_ref[0:2, :]
        small_ref[ROW_DMOD_B + 2:ROW_DMOD_B + 3, :] = dgate_ref[...]
        small_ref[ROW_DMOD_C:ROW_DMOD_C + 2, :] = dmod4_ref[2:4, :]
        small_ref[ROW_LOSS:ROW_LOSS + 1, 0:128] = loss_ref[0:1, :]
        smallg_ref[_lin(me)] = small_ref[...]
        s_recv, s_send = _gather_to_all(small_ref, smallg_ref, ssem_s, rsem_s)

        stage1, own_copies = [], []
        for a in range(n_a):
            for b in range(4):
                dev = 4 * (b >> 1) + 2 * (b & 1)
                stage1.append(pltpu.make_async_remote_copy(
                    src_ref=part_refs[a].at[dev + (1 - c)], dst_ref=recv1[a].at[b],
                    send_sem=ssem1.at[a, b], recv_sem=rsem1.at[a, b], device_id=sibling, device_id_type=MESH))
                own_copies.append(pltpu.make_async_copy(part_refs[a].at[dev + c], own1[a].at[b], lsem.at[a, b]))
                stage1[-1].start()
                own_copies[-1].start()
        land_copies = [pltpu.make_async_copy(land_refs[n], land_vmem[n], land_sem.at[n]) for n in range(n_l)]
        for cp in land_copies:
            cp.start()

        s_recv()
        tot = smallg_ref[0]
        for d in range(1, N_DEV):
            tot = tot + smallg_ref[d]
        tot_ref[...] = tot
        for ref, (row, _), width in zip(sum_refs, SMALL_LAYOUT, small_out):
            ref[...] = tot_ref[row:row + 1, 0:width]
        lossout_ref[...] = tot_ref[8:16, 0:128]
        lin = _lin(me)

        def my_columns(three_rows):
            v = jnp.concatenate(three_rows, axis=1)
            out = jnp.zeros((1, n_mod), F32)
            for d in range(N_DEV):
                out = out + jnp.where(lin == d, v[:, d * n_mod:(d + 1) * n_mod], 0.0)
            return out

        rows3 = lambda ref, r0: [ref[r0 + t:r0 + t + 1, :] for t in range(3)]
        dmodc = rows3(tot_ref, ROW_DMOD_C)
        gbmod_ref[...] = jnp.concatenate(rows3(tot_ref, ROW_DMOD_B), axis=1) + jnp.concatenate(dmodc, axis=1)
        dmy_ref[...] = jnp.zeros_like(dmy_ref)
        for b in range(N_DEV):
            dmy_ref[b:b + 1, :] = my_columns(rows3(smallg_ref.at[b], ROW_DMOD_B))
        dmy = my_columns(dmodc)
        dmy_ref[N_DEV:N_DEV + 1, :] = dmy
        part = _dot_nt(jnp.broadcast_to(dmy, (8, n_mod)).astype(BF16), wmod_ref[...].astype(BF16))

        cc_ref[...] = part
        ccg_ref[_lin(me)] = part
        cc_recv, cc_send = _gather_to_all(cc_ref, ccg_ref, ssem_cc, rsem_cc)

        stage2 = []
        for a in range(n_a):
            for b in range(4):
                stage1[4 * a + b].wait_recv()
                own_copies[4 * a + b].wait()
                sum1b[a][b] = (own1[a][b] + recv1[a][b]).astype(BF16)
            for k in (1, 2, 3):
                tx, ty = _flip(x, (k >> 1) & 1), _flip(y, k & 1)
                stage2.append(pltpu.make_async_remote_copy(
                    src_ref=sum1b[a].at[2 * tx + ty], dst_ref=recv2[a].at[k - 1], send_sem=ssem2.at[a, k - 1],
                    recv_sem=rsem2.at[a, k - 1], device_id=(tx, ty, c), device_id_type=MESH))
                stage2[-1].start()
        for a in range(n_a):
            own = own1[a][2 * x + y] + recv1[a][2 * x + y]
            for k in (1, 2, 3):
                stage2[3 * a + k - 1].wait_recv()
                own = own + recv2[a][k - 1].astype(F32)
            red_refs[a][...] = own

        for cp in land_copies:
            cp.wait()
        for ref, out in zip(land_vmem, landsum_refs):
            acc = ref[0].astype(F32)
            for d in range(1, N_DEV):
                acc = acc + ref[d].astype(F32)
            out[...] = acc
        cc_recv()
        for cp in stage1 + stage2:
            cp.wait_send()
        s_send()
        cc_send()

    vm = pl.BlockSpec(memory_space=pltpu.VMEM)
    sds = jax.ShapeDtypeStruct
    return pl.pallas_call(
        body, name="epilogue_reduce",
        out_shape=(*[sds(s, F32) for s in blk], *[sds(a.shape[1:], F32) for a in landed], sds((N_DEV, 8, D_MODEL), F32),
                   *[sds((1, w), F32) for w in small_out], sds((1, 3 * D_MODEL), F32), sds((16, n_mod), F32),
                   sds((8, 128), F32)),
        in_specs=[pl.BlockSpec(memory_space=pl.ANY)] * (n_a + n_l) + [vm] * (n_s + 4),
        out_specs=tuple([vm] * (n_a + n_l + n_s + 4)),
        scratch_shapes=[*[pltpu.VMEM((4,) + s, F32) for s in blk], *[pltpu.VMEM((4,) + s, F32) for s in blk],
                        *[pltpu.VMEM((4,) + s, BF16) for s in blk], *[pltpu.VMEM((3,) + s, BF16) for s in blk],
                        pltpu.VMEM((SMALL_ROWS, D_MODEL), F32), pltpu.VMEM((N_DEV, SMALL_ROWS, D_MODEL), F32),
                        pltpu.VMEM((SMALL_ROWS, D_MODEL), F32), pltpu.VMEM((8, D_MODEL), F32),
                        *[pltpu.VMEM(a.shape, a.dtype) for a in landed],
                        pltpu.SemaphoreType.DMA((n_a, 4)), pltpu.SemaphoreType.DMA((n_a, 4)), pltpu.SemaphoreType.DMA((n_a, 4)),
                        pltpu.SemaphoreType.DMA((n_a, 3)), pltpu.SemaphoreType.DMA((n_a, 3)),
                        pltpu.SemaphoreType.DMA((7,)), pltpu.SemaphoreType.DMA((7,)),
                        pltpu.SemaphoreType.DMA((7,)), pltpu.SemaphoreType.DMA((7,)), pltpu.SemaphoreType.DMA((n_l,))],
        compiler_params=pltpu.CompilerParams(vmem_limit_bytes=VMEM_LIMIT),
    )(*parts, *landed, *smalls, dmod4, dgate, loss_acc, w_mod_l)


def _adamw(weights, grads, ms, vs, c_all_t, dmod_my, p2g):
    n = len(weights)

    def body(*refs):
        w_refs = refs[0:n]
        g_in = refs[n:2 * n - 2]
        m_refs = refs[2 * n - 2:3 * n - 2]
        v_refs = refs[3 * n - 2:4 * n - 2]
        cat_ref, dmod_ref, p2g_ref = refs[4 * n - 2:4 * n + 1]
        outs = refs[4 * n + 1:]
        g_refs, d_refs, nm_refs, nv_refs = outs[0:n], outs[n:2 * n], outs[2 * n:3 * n], outs[3 * n:4 * n]
        gcc_ref, gwm_ref = g_refs[0], g_refs[1]

        part = p2g_ref[0, 0:1, :]
        for d in range(1, N_DEV):
            part = part + p2g_ref[d, 0:1, :]
        cc = w_refs[0][...]
        sg = _sigmoid(cc)
        gcc_ref[...] = part * (sg * (1.0 + cc * (1.0 - sg)))
        cat = cat_ref[...]
        gwm_ref[...] = lax.dot_general(cat * _sigmoid(cat), dmod_ref[...], (((1,), (0,)), ((), ())), precision=HIGHEST,
                                       preferred_element_type=F32)
        for idx in range(n):
            if idx >= 2:
                g_refs[idx][...] = g_in[idx - 2][...]
            g = g_refs[idx][...]
            m = ADAM_B1 * m_refs[idx][...] + (1.0 - ADAM_B1) * g
            v = ADAM_B2 * v_refs[idx][...] + (1.0 - ADAM_B2) * (g * g)
            m_hat = m / (1.0 - ADAM_B1 ** ADAM_STEP)
            v_hat = v / (1.0 - ADAM_B2 ** ADAM_STEP)
            d_refs[idx][...] = -ADAM_LR * (m_hat / (jnp.sqrt(v_hat) + ADAM_EPS) + ADAM_WD * w_refs[idx][...])
            nm_refs[idx][...] = m
            nv_refs[idx][...] = v

    vm = pl.BlockSpec(memory_space=pltpu.VMEM)
    shapes = [jax.ShapeDtypeStruct(w.shape, F32) for w in weights]
    args = list(weights) + list(grads[2:]) + list(ms) + list(vs) + [c_all_t, dmod_my, p2g]
    out_shape = tuple(shapes * 4)
    return pl.pallas_call(
        body, name="adamw_update", out_shape=out_shape, in_specs=[vm] * len(args), out_specs=tuple([vm] * len(out_shape)),
        compiler_params=pltpu.CompilerParams(vmem_limit_bytes=VMEM_LIMIT),
    )(*args)


def _rope_tables(seq):
    rows = seq // GRID_W
    row = np.repeat(np.arange(rows, dtype=np.float32), GRID_W)
    col = np.tile(np.arange(GRID_W, dtype=np.float32), rows)
    n_freq = D_ROPE // 4
    inv = (np.float32(ROPE_BASE) ** (-np.arange(n_freq, dtype=np.float32) / np.float32(n_freq))).astype(np.float32)
    ang_r = (row[:, None] * inv).astype(np.float32)
    ang_c = (col[:, None] * inv).astype(np.float32)
    ang = np.concatenate([ang_r, ang_r, ang_c, ang_c], axis=-1)
    cos, sin = np.cos(ang).astype(np.float32), np.sin(ang).astype(np.float32)
    low = (np.arange(D_ROPE) % 32) < 16

    def table(t, fill):
        out = np.full((TILE + seq, 128), fill, np.float32)
        out[TILE:, 0:D_ROPE] = t
        return jnp.asarray(out)

    return table(cos, 1.0), table(np.where(low, -sin, 0.0), 0.0), table(np.where(low, 0.0, sin), 0.0)


def kernel(x, c, ctx, c_ctx, w_mod, b_mod, norm_g, w_in, q_lora_g, w_uq, kv_lora_g, w_ukv, q_norm_g, k_norm_g, w_pool, pool_scale, w_out, loss_target, m_c_ctx, m_w_mod, m_b_mod, m_norm_g, m_w_in, m_q_lora_g, m_w_uq, m_kv_lora_g, m_w_ukv, m_q_norm_g, m_k_norm_g, m_w_pool, m_pool_scale, m_w_out, v_c_ctx, v_w_mod, v_b_mod, v_norm_g, v_w_in, v_q_lora_g, v_w_uq, v_kv_lora_g, v_w_ukv, v_q_norm_g, v_k_norm_g, v_w_pool, v_pool_scale, v_w_out):
    seq = x.shape[1]
    assert ctx.shape[1] == TILE and seq % TILE == 0 and seq % GRID_W == 0
    ix, iy, ic = lax.axis_index("x"), lax.axis_index("y"), lax.axis_index("c")
    me = 4 * ix + 2 * iy + ic
    x2, ctx2, tgt2 = x[0], ctx[0], loss_target[0]
    w_mod_l, w_ukv_l, w_out_l = w_mod[0], w_ukv[0], w_out[0]
    c_ctx_row = c_ctx.reshape(1, D_MODEL)

    tr = lambda a: jnp.transpose(a[0])
    w_in_tl, w_uq_tl = tr(w_in), tr(w_uq)
    cg, modg, wg_in, wg_uq, wg_ukv = _prologue(
        c, c_ctx_row, w_mod_l,
        [w_in_tl, w_uq_tl, w_ukv_l])
    mod_all = jnp.transpose(modg, (1, 0, 2)).reshape(16, 3 * D_MODEL)
    mod_b = lax.dynamic_slice_in_dim(mod_all, me, 1, axis=0).reshape(3, D_MODEL)
    mod_c = mod_all[8].reshape(3, D_MODEL)
    modrows = jnp.concatenate([mod_b, mod_c[0:2], jnp.zeros((3, D_MODEL), F32)], axis=0)
    b3 = b_mod.reshape(3, D_MODEL)
    bmodrows = jnp.concatenate([b3, b3[0:2], jnp.zeros((3, D_MODEL), F32)], axis=0)

    w_in_p = wg_in.reshape(D_IN_PROJ, D_MODEL)
    w_uq_p = jnp.concatenate([wg_uq.reshape(N_HEADS, D_HEAD, R_Q), jnp.zeros((N_HEADS, D_HEAD_PAD - D_HEAD, R_Q), BF16)],
                             axis=1).reshape(N_HEADS * D_HEAD_PAD, R_Q)
    w_ukv_f = jnp.transpose(wg_ukv, (1, 0, 2)).reshape(R_KV, N_HEADS * 256)
    qng_p = jnp.concatenate([q_norm_g, jnp.zeros((1, D_HEAD_PAD - D_HEAD), F32)], axis=1)
    kng_p = jnp.concatenate([k_norm_g, jnp.zeros((1, D_HEAD_PAD - D_HEAD), F32)], axis=1)
    cos, slo, shi = _rope_tables(seq)

    u_gates, u_mla, q, k, v = _inproj_fwd(x2, ctx2, modrows, bmodrows, norm_g, w_in_p, q_lora_g, w_uq_p, kv_lora_g, w_ukv_f,
                             qng_p, kng_p, cos, slo, shi)
    attn, lse, wg_out = _attn_fwd(q, k, v, min(2048, seq), w_out_l.astype(BF16))
    (loss_acc, g1, dgate, d_attn, dga, dgp, dpl, gwo_b, gwp, dps) = _mix(
        x2, tgt2, attn, u_gates, modrows, bmodrows, w_pool[0], pool_scale, wg_out.reshape(D_MODEL, D_MODEL))

    blocks = lambda a: a.reshape((N_DEV, a.shape[0] // N_DEV) + a.shape[1:])
    dq, dk, dv, land_wo, land_wp = _attn_bwd(q, k, v, attn, lse, d_attn.reshape(seq, D_ATTN), min(1024, seq), blocks(gwo_b),
                                             gwp.reshape(4 * GROUP_DIM, GROUP_DIM))
    (grad_x, gwin_t, gwuq_p, gwukv_t, dqlg, dkvlg, dqng, dkng, dng, dmod4) = _inproj_bwd(
        x2, ctx2, modrows, bmodrows, norm_g, w_in_p, q_lora_g, w_uq_p, kv_lora_g, w_ukv_f, qng_p, kng_p, cos, slo, shi,
        u_mla, dq, dk, dv, dga, dgp, dpl, g1)

    gwuq_t = gwuq_p.reshape(N_HEADS, D_HEAD_PAD, R_Q)[:, 0:D_HEAD].reshape(N_HEADS * D_HEAD, R_Q)
    parts = [blocks(gwin_t), blocks(gwuq_t), blocks(gwukv_t)]
    (r_win, r_wuq, r_wukv, r_wout, g_w_pool, ccg, g_ng, g_qlg, g_kvlg, g_qng, g_kng, g_ps, g_bmod, dmod_my,
     loss_rows) = _epilogue(parts, [land_wo, land_wp], [dng, dqlg, dkvlg, dqng, dkng, dps], dmod4, dgate, loss_acc, w_mod_l)

    g_w_ukv = jnp.transpose(r_wukv)
    c_rows = cg[:, 0, :]
    c_all_t = jnp.transpose(jnp.concatenate([c_rows, c_ctx_row, jnp.zeros((16 - N_DEV - 1, D_MODEL), F32)], axis=0))

    weights = [c_ctx_row, w_mod_l, b_mod, norm_g, w_in_tl, q_lora_g, w_uq_tl, kv_lora_g, w_ukv_l, q_norm_g, k_norm_g,
               w_pool.reshape(4 * GROUP_DIM, GROUP_DIM), pool_scale, w_out_l]
    grads = [None, None, g_bmod, g_ng, r_win, g_qlg, r_wuq, g_kvlg, g_w_ukv, g_qng, g_kng, g_w_pool, g_ps, r_wout]
    ms = [m_c_ctx.reshape(1, D_MODEL), m_w_mod[0], m_b_mod, m_norm_g, tr(m_w_in), m_q_lora_g, tr(m_w_uq), m_kv_lora_g,
          m_w_ukv[0], m_q_norm_g, m_k_norm_g, m_w_pool.reshape(4 * GROUP_DIM, GROUP_DIM), m_pool_scale, m_w_out[0]]
    vs = [v_c_ctx.reshape(1, D_MODEL), v_w_mod[0], v_b_mod, v_norm_g, tr(v_w_in), v_q_lora_g, tr(v_w_uq), v_kv_lora_g,
          v_w_ukv[0], v_q_norm_g, v_k_norm_g, v_w_pool.reshape(4 * GROUP_DIM, GROUP_DIM), v_pool_scale, v_w_out[0]]
    outs = _adamw(weights, grads, ms, vs, c_all_t, dmod_my, ccg)
    n = len(weights)
    grads, deltas, new_m, new_v = outs[0:n], outs[n:2 * n], outs[2 * n:3 * n], outs[3 * n:4 * n]

    loss = loss_rows[ROW_LOSS - 8, 0]
    final = [c_ctx.shape, w_mod.shape, b_mod.shape, norm_g.shape, w_in.shape, q_lora_g.shape, w_uq.shape, kv_lora_g.shape,
             w_ukv.shape, q_norm_g.shape, k_norm_g.shape, w_pool.shape, pool_scale.shape, w_out.shape]
    transposed = (4, 6)

    def shaped(arrs):
        return [(jnp.transpose(a) if i in transposed else a).reshape(s) for i, (a, s) in enumerate(zip(arrs, final))]

    return (loss, grad_x[None], *shaped(grads), *shaped(deltas), *shaped(new_m), *shaped(new_v))
```

```python
import numpy as np

import jax
import jax.numpy as jnp
from jax import lax
from jax.experimental import pallas as pl
from jax.experimental.pallas import tpu as pltpu

F32 = jnp.float32
BF16 = jnp.bfloat16
MESH = pl.DeviceIdType.MESH
HIGHEST = lax.Precision.HIGHEST

D_MODEL = 1024
N_HEADS = 4
D_NOPE = 128
D_ROPE = 64
D_HEAD = D_NOPE + D_ROPE
D_HEAD_PAD = 256
D_V = 128
R_Q = 256
R_KV = 128
D_ATTN = 512
D_POOL = 512
POOL_WINDOWS = (2, 4, 8, 16)
GROUP_DIM = 128
GRID_W = 64
ROPE_BASE = 10000.0
NORM_EPS = 1e-6
ATTN_SCALE = D_HEAD ** -0.5
LOG2_E = 1.4426950408889634
LN_2 = 0.6931471805599453
ATTN_ROWS = 256
D_IN_PROJ = 1984
U_PAD = 2048
O_GA, O_PIN, O_GP, O_CQ, O_CKV, O_KR = 0, 512, 1024, 1536, 1792, 1920
TILE = 256
MIX_TILE = 512
Q_BLOCK = 128
HALO = 16
N_GATES = 1536
N_MLA = D_IN_PROJ - N_GATES
N_DEV = 8
VMEM_LIMIT = 56 * 1024 * 1024

ADAM_LR = 0.001
ADAM_B1 = 0.9
ADAM_B2 = 0.999
ADAM_EPS = 1e-08
ADAM_WD = 0.01
ADAM_STEP = 10

SMALL_ROWS = 16


def _dot(a, b):
    return lax.dot_general(a, b, (((1,), (0,)), ((), ())), preferred_element_type=F32)


def _dot_nt(a, b):
    return lax.dot_general(a, b, (((1,), (1,)), ((), ())), preferred_element_type=F32)


def _dot_tn(a, b):
    return lax.dot_general(a, b, (((0,), (0,)), ((), ())), preferred_element_type=F32)


def _sigmoid(x):
    return 1.0 / (1.0 + jnp.exp(-x))


def _rope(p, cos, slo, shi):
    return p * cos + pltpu.roll(p, 112, 1) * slo + pltpu.roll(p, 16, 1) * shi


def _rope_t(d, cos, slo, shi):
    return d * cos + pltpu.roll(d * slo, 16, 1) + pltpu.roll(d * shi, 112, 1)


def _shift_rows(a, k):
    n = a.shape[0]
    return pltpu.roll(a, (-k) % n, 0)


def _window_sum(x, w, transposed):
    s = (x + _shift_rows(x, 1)) if transposed else (_shift_rows(x, -1) + x)
    step = 1
    while 2 * step < w:
        s = _shift_rows(s, -step) + _shift_rows(s, step)
        step *= 2
    return s


def _inv_count(tpos, w, seq):
    lo = jnp.maximum(tpos - w // 2, 0)
    hi = jnp.minimum(tpos - w // 2 + w, seq)
    return 1.0 / jnp.maximum(hi - lo, 1).astype(F32)


def _coords():
    return lax.axis_index("x"), lax.axis_index("y"), lax.axis_index("c")


def _flip(v, bit):
    return (1 - v) if bit else v


def _peer(k):
    x, y, c = _coords()
    return (_flip(x, (k >> 2) & 1), _flip(y, (k >> 1) & 1), _flip(c, k & 1))


def _lin(p):
    return 4 * p[0] + 2 * p[1] + p[2]


def _gather_to_all(src_ref, slots_ref, send_sems, recv_sems):
    me = _coords()
    copies = []
    for k in range(1, N_DEV):
        cp = pltpu.make_async_remote_copy(
            src_ref=src_ref, dst_ref=slots_ref.at[_lin(me)], send_sem=send_sems.at[k - 1], recv_sem=recv_sems.at[k - 1],
            device_id=_peer(k), device_id_type=MESH)
        cp.start()
        copies.append(cp)

    def wait_recv():
        for k in range(1, N_DEV):
            pltpu.make_async_remote_copy(
                src_ref=src_ref, dst_ref=slots_ref.at[_lin(_peer(k))], send_sem=send_sems.at[k - 1],
                recv_sem=recv_sems.at[k - 1], device_id=_peer(k), device_id_type=MESH).wait_recv()

    def wait_send():
        for cp in copies:
            cp.wait_send()

    return wait_recv, wait_send


def _prologue(c8, cctx8, w_mod_l, shards):
    n_mod = w_mod_l.shape[1]
    n_w = len(shards)

    def body(c_ref, cctx_ref, wmod_ref, *refs):
        w_refs = refs[0:n_w]
        cg_ref, modg_ref = refs[n_w], refs[n_w + 1]
        wg_refs = refs[n_w + 2:2 * n_w + 2]
        modblk_ref = refs[2 * n_w + 2]
        wb_refs = refs[2 * n_w + 3:3 * n_w + 3]
        c8_ref = refs[3 * n_w + 3]
        ssem_w, rsem_w, lsem_w, ssem_c, rsem_c, ssem_m, rsem_m = refs[3 * n_w + 4:]
        x, y, c = _coords()
        me = (x, y, c)
        sibling = (x, y, 1 - c)
        chips = [(1 - x, y), (x, 1 - y), (1 - x, 1 - y)]

        def wcopies(k, block, to, own=False):
            out = []
            for a in range(n_w):
                slot = wg_refs[a].at[_lin(block)]
                out.append(pltpu.make_async_remote_copy(
                    src_ref=wb_refs[a] if own else slot, dst_ref=slot, send_sem=ssem_w.at[a, k], recv_sem=rsem_w.at[a, k],
                    device_id=to, device_id_type=MESH))
            return out

        c8_ref[...] = jnp.broadcast_to(c_ref[...], (8, D_MODEL))
        cg_ref[_lin(me)] = c8_ref[...]
        c_recv, c_send = _gather_to_all(c8_ref, cg_ref, ssem_c, rsem_c)

        for a in range(n_w):
            wb_refs[a][...] = w_refs[a][...].astype(BF16)
        first = wcopies(0, me, sibling, own=True)
        for j, chip in enumerate(chips):
            first += wcopies(1 + j, me, (*chip, c), own=True)
        late = [idx for idx in range(len(first)) if idx % n_w == 0 and idx >= n_w]
        for idx, cp in enumerate(first):
            if idx not in late:
                cp.start()
        mine = [pltpu.make_async_copy(wb_refs[a], wg_refs[a].at[_lin(me)], lsem_w.at[a]) for a in range(n_w)]
        for cp in mine:
            cp.start()

        c_recv()
        row = lax.broadcasted_iota(jnp.int32, (8, D_MODEL), 0)
        c_all = jnp.zeros((8, D_MODEL), F32)
        for d in range(N_DEV):
            c_all = c_all + jnp.where(row == d, cg_ref[d], 0.0)
        cc = jnp.broadcast_to(cctx_ref[...], (8, D_MODEL))
        a = jnp.concatenate([c_all * _sigmoid(c_all), cc * _sigmoid(cc)], axis=0)
        modblk_ref[...] = _dot(a.astype(BF16), wmod_ref[...].astype(BF16))
        modg_ref[_lin(me)] = modblk_ref[...]
        m_recv, m_send = _gather_to_all(modblk_ref, modg_ref, ssem_m, rsem_m)
        for idx in late:
            first[idx].start()
        m_recv()

        passed = []
        for j, chip in enumerate(chips):
            for cp in wcopies(1 + j, (*chip, c), me):
                cp.wait_recv()
            fwd = wcopies(4 + j, (*chip, c), sibling)
            for cp in fwd:
                cp.start()
            passed += fwd
        for cp in wcopies(0, sibling, me):
            cp.wait_recv()
        for j, chip in enumerate(chips):
            for cp in wcopies(4 + j, (*chip, 1 - c), me):
                cp.wait_recv()
        for cp in first + passed:
            cp.wait_send()
        for cp in mine:
            cp.wait()
        c_send()
        m_send()

    vm = pl.BlockSpec(memory_space=pltpu.VMEM)
    hbm = pl.BlockSpec(memory_space=pl.ANY)
    return pl.pallas_call(
        body, name="prologue_gather",
        out_shape=(jax.ShapeDtypeStruct((N_DEV, 8, D_MODEL), F32), jax.ShapeDtypeStruct((N_DEV, 16, n_mod), F32),
                   *[jax.ShapeDtypeStruct((N_DEV,) + s.shape, BF16) for s in shards]),
        in_specs=[vm] * (3 + n_w), out_specs=(vm, vm, *[hbm] * n_w),
        scratch_shapes=[pltpu.VMEM((16, n_mod), F32), *[pltpu.VMEM(s.shape, BF16) for s in shards],
                        pltpu.VMEM((8, D_MODEL), F32),
                        pltpu.SemaphoreType.DMA((n_w, 7)), pltpu.SemaphoreType.DMA((n_w, 7)), pltpu.SemaphoreType.DMA((n_w,)),
                        pltpu.SemaphoreType.DMA((7,)), pltpu.SemaphoreType.DMA((7,)),
                        pltpu.SemaphoreType.DMA((7,)), pltpu.SemaphoreType.DMA((7,))],
        compiler_params=pltpu.CompilerParams(vmem_limit_bytes=VMEM_LIMIT),
    )(c8, cctx8, w_mod_l, *shards)


def _modulated_input(is_ctx, x_ref, ctx_ref, mod_ref, bmod_ref, ng_ref):
    xt = jnp.where(is_ctx, ctx_ref[...], x_ref[...])
    shift = jnp.where(is_ctx, mod_ref[3:4, :] + bmod_ref[3:4, :], mod_ref[0:1, :] + bmod_ref[0:1, :])
    scale = jnp.where(is_ctx, mod_ref[4:5, :] + bmod_ref[4:5, :], mod_ref[1:2, :] + bmod_ref[1:2, :])
    r = lax.rsqrt(jnp.mean(xt * xt, axis=-1, keepdims=True) + NORM_EPS)
    xg = (xt * r) * ng_ref[...]
    h = xg * (1.0 + scale) + shift
    return xt, r, xg, h, scale


def _inproj_fwd(x, ctx, modrows, bmodrows, norm_g, w_in_p, q_lora_g, w_uq_p, kv_lora_g, w_ukv, qng_p, kng_p, cos, slo, shi):
    seq = x.shape[0]
    n_tiles = seq // TILE + 1
    tot = seq + TILE

    n_mla = R_Q + R_KV + 128

    n_x = n_tiles - 1

    def body(x_hbm, ctx_ref, mod_ref, bmod_ref, ng_ref, win_ref, qlg_ref, wuq_ref, kvlg_ref, wukv_ref, qng_ref, kng_ref,
             cos_ref, slo_ref, shi_ref, ug_ref, um_ref, q_ref, k_ref, v_ref, stash, xring, xsem):
        s = pl.program_id(0)

        def fetch(t, slot):
            return pltpu.make_async_copy(x_hbm.at[pl.ds(t * TILE, TILE), :], xring.at[slot], xsem.at[slot])

        rest = (mod_ref, bmod_ref, ng_ref, win_ref, qlg_ref, wuq_ref, kvlg_ref, wukv_ref, qng_ref, kng_ref,
                cos_ref, slo_ref, shi_ref, ug_ref, um_ref, q_ref, k_ref, v_ref)

        @pl.when(s == 0)
        def _():
            stash[...] = jnp.zeros_like(stash)
            xring[2] = jnp.zeros((TILE, D_MODEL), F32)
            fetch(0, 0).start()
            if n_x > 1:
                fetch(1, 1).start()

        t = jnp.maximum(jnp.minimum(s, n_tiles - 1) - 1, 0)
        slot = jnp.where(s == 0, 2, lax.rem(t, 3))

        for j in range(3):
            @pl.when(slot == j)
            def _(j=j):
                @pl.when((s > 0) & (s < n_tiles))
                def _():
                    fetch(t, j).wait()

                @pl.when((s > 0) & (t + 2 < n_x))
                def _():
                    fetch(t + 2, (j + 2) % 3).start()

                stages(s == 0, (xring.at[j], ctx_ref) + rest, stash, stash)

    def stages(is_ctx, refs, fill, prev_ref):
        (x_ref, ctx_ref, mod_ref, bmod_ref, ng_ref, win_ref, qlg_ref, wuq_ref, kvlg_ref, wukv_ref, qng_ref, kng_ref,
         cos_ref, slo_ref, shi_ref, ug_ref, um_ref, q_ref, k_ref, v_ref) = refs
        cos_t, slo_t, shi_t = cos_ref[...], slo_ref[...], shi_ref[...]
        prev = prev_ref[...]
        cq = prev[:, 0:R_Q]
        qn = cq * lax.rsqrt(jnp.mean(cq * cq, axis=-1, keepdims=True) + NORM_EPS) * qlg_ref[...]
        q = _dot_nt(qn.astype(BF16), wuq_ref[...])
        qg = qng_ref[...] * (ATTN_SCALE * LOG2_E)
        for hd in range(N_HEADS):
            qh = q[:, hd * D_HEAD_PAD:(hd + 1) * D_HEAD_PAD]
            rr = lax.rsqrt(jnp.sum(qh * qh, axis=-1, keepdims=True) * (1.0 / D_HEAD) + NORM_EPS)
            qy = qh * rr * qg
            q_ref[hd, :, 0:D_NOPE] = qy[:, 0:D_NOPE].astype(BF16)
            q_ref[hd, :, D_NOPE:D_HEAD_PAD] = _rope(qy[:, D_NOPE:D_HEAD_PAD], cos_t, slo_t, shi_t).astype(BF16)

        ckv = prev[:, R_Q:R_Q + R_KV]
        kvn = ckv * lax.rsqrt(jnp.mean(ckv * ckv, axis=-1, keepdims=True) + NORM_EPS) * kvlg_ref[...]
        kv = _dot(kvn.astype(BF16), wukv_ref[...])
        krz = prev[:, R_Q + R_KV:n_mla]
        kr_ss = jnp.sum(krz * krz, axis=-1, keepdims=True)
        kg = kng_ref[...]
        for hd in range(N_HEADS):
            kn = kv[:, hd * 256:hd * 256 + D_NOPE]
            rr = lax.rsqrt((jnp.sum(kn * kn, axis=-1, keepdims=True) + kr_ss) * (1.0 / D_HEAD) + NORM_EPS)
            k_ref[hd, :, 0:D_NOPE] = (kn * rr * kg[:, 0:D_NOPE]).astype(BF16)
            k_ref[hd, :, D_NOPE:D_HEAD_PAD] = _rope(krz * rr * kg[:, D_NOPE:D_HEAD_PAD], cos_t, slo_t, shi_t).astype(BF16)
            v_ref[hd] = kv[:, hd * 256 + D_NOPE:(hd + 1) * 256].astype(BF16)

        _, _, _, h, _ = _modulated_input(is_ctx, x_ref, ctx_ref, mod_ref, bmod_ref, ng_ref)
        hb = h.astype(BF16)
        ug_ref[...] = _dot_nt(hb, win_ref[N_MLA:D_IN_PROJ, :]).astype(BF16)
        um = _dot_nt(hb, win_ref[0:n_mla, :])
        lane = lax.broadcasted_iota(jnp.int32, (TILE, 128), 1)
        um = jnp.concatenate([um[:, 0:R_Q + R_KV], jnp.where(lane < D_ROPE, um[:, R_Q + R_KV:n_mla], 0.0)], axis=1)
        um_ref[...] = um
        fill[...] = um

    first = lambda s: jnp.minimum(s, n_tiles - 1)
    second = lambda s: jnp.maximum(s - 1, 0)
    latent = lambda t: jnp.maximum(t - 1, 0)
    full = lambda shape: pl.BlockSpec(shape, lambda s: (0,) * len(shape))
    rope_spec = pl.BlockSpec((TILE, 128), lambda s: (second(s), 0))
    return pl.pallas_call(
        body, name="inproj_fwd", grid=(n_tiles + 1,),
        out_shape=(jax.ShapeDtypeStruct((seq, N_GATES), BF16), jax.ShapeDtypeStruct((tot, n_mla), F32),
                   jax.ShapeDtypeStruct((N_HEADS, seq, D_HEAD_PAD), BF16),
                   jax.ShapeDtypeStruct((N_HEADS, tot, D_HEAD_PAD), BF16),
                   jax.ShapeDtypeStruct((N_HEADS, tot, D_V), BF16)),
        in_specs=[pl.BlockSpec(memory_space=pl.ANY), full((TILE, D_MODEL)), full((8, D_MODEL)),
                  full((8, D_MODEL)), full((1, D_MODEL)), full((D_IN_PROJ, D_MODEL)), full((1, R_Q)),
                  full((N_HEADS * D_HEAD_PAD, R_Q)), full((1, R_KV)), full((R_KV, N_HEADS * 256)), full((1, D_HEAD_PAD)),
                  full((1, D_HEAD_PAD)), rope_spec, rope_spec, rope_spec],
        out_specs=(pl.BlockSpec((TILE, N_GATES), lambda s: (latent(first(s)), 0)),
                   pl.BlockSpec((TILE, n_mla), lambda s: (first(s), 0)),
                   pl.BlockSpec((N_HEADS, TILE, D_HEAD_PAD), lambda s: (0, latent(second(s)), 0)),
                   pl.BlockSpec((N_HEADS, TILE, D_HEAD_PAD), lambda s: (0, second(s), 0)),
                   pl.BlockSpec((N_HEADS, TILE, D_V), lambda s: (0, second(s), 0))),
        scratch_shapes=[pltpu.VMEM((TILE, n_mla), F32), pltpu.VMEM((3, TILE, D_MODEL), F32), pltpu.SemaphoreType.DMA((3,))],
        compiler_params=pltpu.CompilerParams(dimension_semantics=("arbitrary",), vmem_limit_bytes=VMEM_LIMIT),
    )(x, ctx, modrows, bmodrows, norm_g, w_in_p, q_lora_g, w_uq_p, kv_lora_g, w_ukv, qng_p, kng_p, cos, slo, shi)


def _hosted_exchange(first, last, items, send_sems, recv_sems, local_sems):
    me = _lin(_coords())

    def remote(n, k, src, land, scatter):
        peer = _peer(k)
        return pltpu.make_async_remote_copy(
            src_ref=src.at[_lin(peer)] if scatter else src, dst_ref=land.at[me], send_sem=send_sems.at[n, k - 1],
            recv_sem=recv_sems.at[n, k - 1], device_id=peer, device_id_type=MESH)

    def local(n, src, land, scatter):
        return pltpu.make_async_copy(src.at[me] if scatter else src, land.at[me], local_sems.at[n])

    @pl.when(first)
    def _():
        for n, (src, land, scatter) in enumerate(items):
            for k in range(1, N_DEV):
                remote(n, k, src, land, scatter).start()
            local(n, src, land, scatter).start()

    @pl.when(last)
    def _():
        for n, (src, land, scatter) in enumerate(items):
            for k in range(1, N_DEV):
                peer = _peer(k)
                pltpu.make_async_remote_copy(
                    src_ref=src.at[0] if scatter else src, dst_ref=land.at[_lin(peer)], send_sem=send_sems.at[n, k - 1],
                    recv_sem=recv_sems.at[n, k - 1], device_id=peer, device_id_type=MESH).wait_recv()
            for k in range(1, N_DEV):
                remote(n, k, src, land, scatter).wait_send()
            local(n, src, land, scatter).wait()


def _attn_fwd(q, k, v, block_q, w_out_b):
    _, seq, _ = q.shape
    n_keys = k.shape[1]
    n_q = seq // block_q

    def body(q_ref, k_ref, v_ref, wo_ref, o_ref, lse_ref, wog_ref, ssem, rsem, lsem):
        h, i = pl.program_id(0), pl.program_id(1)
        _hosted_exchange((h == 0) & (i == 0), (h == N_HEADS - 1) & (i == n_q - 1), [(wo_ref, wog_ref, False)],
                         ssem, rsem, lsem)
        kb, vb = k_ref[0], v_ref[0]
        for r0 in range(0, block_q, ATTN_ROWS):
            rows = slice(r0, r0 + ATTN_ROWS)
            s = _dot_nt(q_ref[0, rows, :], kb)
            m = jnp.max(s, axis=-1, keepdims=True)
            p = jnp.exp2(s - m)
            l = jnp.sum(p, axis=-1, keepdims=True)
            o_ref[rows, :] = _dot(p.astype(BF16), vb) * (1.0 / l)
            lse_ref[0, rows, :] = m + jnp.log2(l)

    hbm = pl.BlockSpec(memory_space=pl.ANY)
    return pl.pallas_call(
        body, name="attn_fwd", grid=(N_HEADS, n_q),
        out_shape=(jax.ShapeDtypeStruct((seq, D_ATTN), F32), jax.ShapeDtypeStruct((N_HEADS, seq, 1), F32),
                   jax.ShapeDtypeStruct((N_DEV,) + w_out_b.shape, w_out_b.dtype)),
        in_specs=[pl.BlockSpec((1, block_q, D_HEAD_PAD), lambda h, i: (h, i, 0)),
                  pl.BlockSpec((1, n_keys, D_HEAD_PAD), lambda h, i: (h, 0, 0)),
                  pl.BlockSpec((1, n_keys, D_V), lambda h, i: (h, 0, 0)), hbm],
        out_specs=(pl.BlockSpec((block_q, D_V), lambda h, i: (i, h)),
                   pl.BlockSpec((1, block_q, 1), lambda h, i: (h, i, 0)), hbm),
        scratch_shapes=[pltpu.SemaphoreType.DMA((1, 7)), pltpu.SemaphoreType.DMA((1, 7)), pltpu.SemaphoreType.DMA((1,))],
        compiler_params=pltpu.CompilerParams(dimension_semantics=("arbitrary", "arbitrary"), vmem_limit_bytes=VMEM_LIMIT),
    )(q, k, v, w_out_b)


def _attn_bwd(q, k, v, o, lse, d_o, block_q, gwo_blocks, gwp):
    _, seq, _ = q.shape
    n_keys = k.shape[1]
    n_q = seq // block_q

    def body(q_ref, k_ref, v_ref, o_ref, lse_ref, do_ref, gwo_ref, gwp_ref, dq_ref, dkt_ref, dvt_ref, lwo_ref, lwp_ref,
             ssem, rsem, lsem):
        h, i = pl.program_id(0), pl.program_id(1)
        _hosted_exchange((h == 0) & (i == 0), (h == N_HEADS - 1) & (i == n_q - 1),
                         [(gwo_ref, lwo_ref, True), (gwp_ref, lwp_ref, False)], ssem, rsem, lsem)

        @pl.when(i == 0)
        def _():
            dkt_ref[...] = jnp.zeros_like(dkt_ref)
            dvt_ref[...] = jnp.zeros_like(dvt_ref)

        kb, vb = k_ref[0], v_ref[0]
        raws, ps = [], []
        for r0 in range(0, block_q, ATTN_ROWS):
            rows = slice(r0, r0 + ATTN_ROWS)
            d_out = do_ref[rows, :]
            p = jnp.exp2(_dot_nt(q_ref[0, rows, :], kb) - lse_ref[0, rows, :])
            dp = _dot_nt(d_out.astype(BF16), vb)
            delta = jnp.sum(d_out * o_ref[rows, :], axis=-1, keepdims=True)
            raw = (p * (dp - delta)).astype(BF16)
            dq_ref[0, rows, :] = _dot(raw, kb)
            raws.append(raw)
            ps.append(p.astype(BF16))
        dkt_ref[0] += _dot_tn(q_ref[0], jnp.concatenate(raws, axis=0))
        dvt_ref[0] += _dot_tn(do_ref[...].astype(BF16), jnp.concatenate(ps, axis=0))

    hbm = pl.BlockSpec(memory_space=pl.ANY)
    return pl.pallas_call(
        body, name="attn_bwd", grid=(N_HEADS, n_q),
        out_shape=(jax.ShapeDtypeStruct((N_HEADS, seq, D_HEAD_PAD), F32),
                   jax.ShapeDtypeStruct((N_HEADS, D_HEAD_PAD, n_keys), F32),
                   jax.ShapeDtypeStruct((N_HEADS, D_V, n_keys), F32),
                   jax.ShapeDtypeStruct(gwo_blocks.shape, gwo_blocks.dtype),
                   jax.ShapeDtypeStruct((N_DEV,) + gwp.shape, gwp.dtype)),
        in_specs=[pl.BlockSpec((1, block_q, D_HEAD_PAD), lambda h, i: (h, i, 0)),
                  pl.BlockSpec((1, n_keys, D_HEAD_PAD), lambda h, i: (h, 0, 0)),
                  pl.BlockSpec((1, n_keys, D_V), lambda h, i: (h, 0, 0)),
                  pl.BlockSpec((block_q, D_V), lambda h, i: (i, h)),
                  pl.BlockSpec((1, block_q, 1), lambda h, i: (h, i, 0)),
                  pl.BlockSpec((block_q, D_V), lambda h, i: (i, h)), hbm, hbm],
        out_specs=(pl.BlockSpec((1, block_q, D_HEAD_PAD), lambda h, i: (h, i, 0)),
                   pl.BlockSpec((1, D_HEAD_PAD, n_keys), lambda h, i: (h, 0, 0)),
                   pl.BlockSpec((1, D_V, n_keys), lambda h, i: (h, 0, 0)), hbm, hbm),
        scratch_shapes=[pltpu.SemaphoreType.DMA((2, 7)), pltpu.SemaphoreType.DMA((2, 7)), pltpu.SemaphoreType.DMA((2,))],
        compiler_params=pltpu.CompilerParams(dimension_semantics=("arbitrary", "arbitrary"), vmem_limit_bytes=VMEM_LIMIT),
    )(q, k, v, o, lse, d_o, gwo_blocks, gwp)


def _mix(x, target, attn, u, modrows, bmodrows, w_pool, pool_scale, w_out):
    seq = x.shape[0]
    rows = min(MIX_TILE, seq // 2)
    n_tiles = seq // rows
    rows8 = rows // HALO
    last8 = seq // HALO - 1

    n_blk = seq // Q_BLOCK
    per = rows // n_blk
    assert rows % n_blk == 0 and per % 8 == 0 and n_tiles >= 2
    n_stash = 8

    def body(x_ref, t_ref, o_hbm, ga_ref, pin_ref, hb_ref, ha_ref, gp_ref, mod_ref, bmod_ref, wp_ref, ps_ref, wo_ref,
             loss_ref, g1_ref, dgate_ref, do_hbm, dga_ref, dgp_ref, dpl_ref, gwob_ref, gwp_ref, dps_ref,
             abuf, dbuf, gwo_ref, *rest):
        stash_even, stash_odd = rest[0:n_stash], rest[n_stash:2 * n_stash]
        rsem, wsem = rest[2 * n_stash:]
        s = pl.program_id(0)

        def slab_copies(tile, slot, fetch):
            window = pl.ds(tile * per, per)
            if fetch:
                return [pltpu.make_async_copy(o_hbm.at[:, window, :], abuf.at[slot], rsem.at[slot])]
            return [pltpu.make_async_copy(dbuf.at[slot], do_hbm.at[:, window, :], wsem.at[slot])]

        def wait_all(slot, fetch):
            slab_copies(0, slot, fetch)[0].wait()

        a_slot = lax.rem(s, 2)
        b_slot = 1 - a_slot

        @pl.when(s == 0)
        def _():
            loss_ref[...] = jnp.zeros_like(loss_ref)
            dgate_ref[...] = jnp.zeros_like(dgate_ref)
            gwo_ref[...] = jnp.zeros_like(gwo_ref)
            gwp_ref[...] = jnp.zeros_like(gwp_ref)
            dps_ref[...] = jnp.zeros_like(dps_ref)
            for cp in slab_copies(0, 0, True):
                cp.start()

        @pl.when(s + 1 < n_tiles)
        def _():
            for cp in slab_copies(s + 1, b_slot, True):
                cp.start()

        @pl.when(s < n_tiles)
        def _():
            wait_all(a_slot, True)

        @pl.when(s >= 3)
        def _():
            wait_all(b_slot, False)

        gate = mod_ref[2:3, :] + bmod_ref[2:3, :]
        ps = ps_ref[...]

        def a_vector(slot):
            attn_t = jnp.swapaxes(abuf[slot], 0, 1).reshape(rows, D_ATTN)
            ga = ga_ref[...].astype(F32)
            sga = _sigmoid(ga)
            silu_ga = ga * sga
            pin = pin_ref[...].astype(F32)
            xs = jnp.concatenate([jnp.where(s > 0, hb_ref[...].astype(F32), 0.0), pin,
                                  jnp.where(s < n_tiles - 1, ha_ref[...].astype(F32), 0.0)], axis=0)
            tpos = s * rows + lax.broadcasted_iota(jnp.int32, (rows, 1), 0)
            pooled = []
            for g, w in enumerate(POOL_WINDOWS):
                sl = slice(g * GROUP_DIM, (g + 1) * GROUP_DIM)
                ws = _window_sum(xs[:, sl], w, False)[HALO:HALO + rows]
                pooled.append((ws * _inv_count(tpos, w, seq) - pin[:, sl]).astype(BF16))
            return attn_t, ga, sga, silu_ga, pooled

        def a_group_maps(pooled):
            return jnp.concatenate([_dot(pooled[g], wp_ref[g].astype(BF16)) for g in range(len(POOL_WINDOWS))], axis=1)

        def a_project(stash, yp, attn_t, ga, sga, silu_ga, pooled):
            zb_ref, _, fa_ref, sa_ref, fp_ref, sp_ref, yp_ref, pooled_ref = stash
            ypool = yp * ps
            gp = gp_ref[...].astype(F32)
            sgp = _sigmoid(gp)
            silu_gp = gp * sgp
            z = jnp.concatenate([silu_ga * attn_t, silu_gp * ypool], axis=1).astype(BF16)
            y = _dot(z, wo_ref[...])
            zb_ref[...] = z
            fa_ref[...] = attn_t * (sga * (1.0 + ga * (1.0 - sga)))
            sa_ref[...] = silu_ga
            fp_ref[...] = ypool * (sgp * (1.0 + gp * (1.0 - sgp)))
            sp_ref[...] = silu_gp
            yp_ref[...] = yp
            pooled_ref[...] = jnp.concatenate(pooled, axis=1)
            return y

        def a_loss(stash, y):
            res = x_ref[...] + gate * y - t_ref[...]
            loss_ref[...] += jnp.sum(res * res) * (0.5 / D_MODEL)
            dxn = res * (1.0 / D_MODEL)
            g1_ref[...] = dxn
            dgate_ref[...] += jnp.sum(dxn * y, axis=0, keepdims=True)
            stash[1][...] = (gate * dxn).astype(BF16)

        def b_dz(stash):
            return _dot_nt(stash[1][...], wo_ref[...])

        def b_gwo(stash):
            gwo_ref[...] += _dot_tn(stash[0][...], stash[1][...])

        def b_vector(stash, slot, dz):
            _, _, fa_ref, sa_ref, fp_ref, sp_ref, yp_ref, _ = stash
            dbra = dz[:, 0:D_ATTN]
            dbrp = dz[:, D_ATTN:]
            dga_ref[...] = (dbra * fa_ref[...]).astype(BF16)
            dbuf[slot] = jnp.swapaxes((dbra * sa_ref[...]).reshape(per, n_blk, D_ATTN), 0, 1)
            dgp_ref[...] = (dbrp * fp_ref[...]).astype(BF16)
            dyp_s = dbrp * sp_ref[...]
            dps_ref[...] += jnp.sum(dyp_s * yp_ref[...], axis=0, keepdims=True)
            return (dyp_s * ps).astype(BF16)

        def b_small(stash, dyp):
            pooled_ref = stash[7]
            for g in range(len(POOL_WINDOWS)):
                sl = slice(g * GROUP_DIM, (g + 1) * GROUP_DIM)
                gwp_ref[g] += _dot_tn(pooled_ref[:, sl], dyp[:, sl])
                dpl_ref[:, sl] = _dot_nt(dyp[:, sl], wp_ref[g].astype(BF16)).astype(BF16)

        def both(a_stash, b_stash, slot_a):
            dz = b_dz(b_stash)
            front = a_vector(slot_a)
            yp = a_group_maps(front[-1])
            b_gwo(b_stash)
            y = a_project(a_stash, yp, *front)
            dyp = b_vector(b_stash, 1 - slot_a, dz)
            a_loss(a_stash, y)
            b_small(b_stash, dyp)

        @pl.when(s == 0)
        def _():
            front = a_vector(0)
            a_loss(stash_even, a_project(stash_even, a_group_maps(front[-1]), *front))

        @pl.when((s > 0) & (s < n_tiles) & (a_slot == 0))
        def _():
            both(stash_even, stash_odd, 0)

        @pl.when((s > 0) & (s < n_tiles) & (a_slot == 1))
        def _():
            both(stash_odd, stash_even, 1)

        @pl.when(s == n_tiles)
        def _():
            last = (n_tiles - 1) % 2
            stash = stash_odd if last else stash_even
            dz = b_dz(stash)
            b_gwo(stash)
            b_small(stash, b_vector(stash, last, dz))
            gwob_ref[...] = gwo_ref[...].astype(BF16)

        @pl.when(s >= 1)
        def _():
            for cp in slab_copies(s - 1, b_slot, False):
                cp.start()

        @pl.when(s == n_tiles)
        def _():
            wait_all(b_slot, False)
            wait_all(a_slot, False)

    ta = lambda s: jnp.minimum(s, n_tiles - 1)
    tb = lambda s: jnp.maximum(s - 1, 0)
    tile = lambda w: pl.BlockSpec((rows, w), lambda s: (ta(s), 0))
    tile_b = lambda w: pl.BlockSpec((rows, w), lambda s: (tb(s), 0))
    ucol = lambda col: pl.BlockSpec((rows, 512), lambda s: (ta(s), col))
    full = lambda shape: pl.BlockSpec(shape, lambda s: (0,) * len(shape))
    stash_shapes = [pltpu.VMEM((rows, D_MODEL), BF16), pltpu.VMEM((rows, D_MODEL), BF16)] + \
        [pltpu.VMEM((rows, 512), F32)] * 5 + [pltpu.VMEM((rows, D_POOL), BF16)]
    return pl.pallas_call(
        body, name="mix_fwd_bwd", grid=(n_tiles + 1,),
        out_shape=(jax.ShapeDtypeStruct((8, 128), F32), jax.ShapeDtypeStruct((seq, D_MODEL), F32),
                   jax.ShapeDtypeStruct((1, D_MODEL), F32), jax.ShapeDtypeStruct((n_blk, Q_BLOCK, D_ATTN), F32),
                   jax.ShapeDtypeStruct((seq, D_ATTN), BF16), jax.ShapeDtypeStruct((seq, D_POOL), BF16),
                   jax.ShapeDtypeStruct((seq, D_POOL), BF16), jax.ShapeDtypeStruct((D_MODEL, D_MODEL), BF16),
                   jax.ShapeDtypeStruct((4, GROUP_DIM, GROUP_DIM), F32), jax.ShapeDtypeStruct((1, D_POOL), F32)),
        in_specs=[tile(D_MODEL), tile(D_MODEL), pl.BlockSpec(memory_space=pl.ANY), ucol(0), ucol(1),
                  pl.BlockSpec((HALO, 512), lambda s: (jnp.maximum(ta(s) * rows8 - 1, 0), 1)),
                  pl.BlockSpec((HALO, 512), lambda s: (jnp.minimum((ta(s) + 1) * rows8, last8), 1)),
                  ucol(2), full((8, D_MODEL)), full((8, D_MODEL)), full((4, GROUP_DIM, GROUP_DIM)), full((1, D_POOL)),
                  full((D_MODEL, D_MODEL))],
        out_specs=(full((8, 128)), tile(D_MODEL), full((1, D_MODEL)), pl.BlockSpec(memory_space=pl.ANY), tile_b(D_ATTN),
                   tile_b(D_POOL), tile_b(D_POOL), full((D_MODEL, D_MODEL)), full((4, GROUP_DIM, GROUP_DIM)),
                   full((1, D_POOL))),
        scratch_shapes=[pltpu.VMEM((2, n_blk, per, D_ATTN), F32), pltpu.VMEM((2, n_blk, per, D_ATTN), F32),
                        pltpu.VMEM((D_MODEL, D_MODEL), F32), *stash_shapes, *stash_shapes,
                        pltpu.SemaphoreType.DMA((2,)), pltpu.SemaphoreType.DMA((2,))],
        compiler_params=pltpu.CompilerParams(dimension_semantics=("arbitrary",), vmem_limit_bytes=VMEM_LIMIT),
    )(x, target, attn.reshape(n_blk, Q_BLOCK, D_ATTN), u, u, u, u, u, modrows, bmodrows, w_pool, pool_scale, w_out)


def _inproj_bwd(x, ctx, modrows, bmodrows, norm_g, w_in_p, q_lora_g, w_uq_p, kv_lora_g, w_ukv, qng_p, kng_p, cos, slo, shi,
                u, dq, dk, dv, dga, dgp, dpl, g1):
    seq = x.shape[0]
    n_tiles = seq // TILE + 1
    rows8 = TILE // HALO
    last8 = seq // HALO - 1

    def body(*refs):
        du_even, du_odd, dh_even, dh_odd = refs[-4:]
        gwin_ref, gwuq_ref, gwukv_ref, dqlg_ref, dkvlg_ref, dqng_ref, dkng_ref, dng_ref, dmod_ref = refs[-13:-4]
        s = pl.program_id(0)
        even = lax.rem(s, 2) == 0

        @pl.when(s == 0)
        def _():
            for ref in (gwin_ref, gwuq_ref, gwukv_ref, dqlg_ref, dkvlg_ref, dqng_ref, dkng_ref, dng_ref, dmod_ref,
                        du_odd, dh_even):
                ref[...] = jnp.zeros_like(ref)

        @pl.when((s < n_tiles) & even)
        def _():
            stages(s, refs[:-4], du_even, du_odd, dh_odd, dh_even, (True, True, True))

        @pl.when((s < n_tiles) & jnp.logical_not(even))
        def _():
            stages(s, refs[:-4], du_odd, du_even, dh_even, dh_odd, (True, True, True))

        for tail, run in ((n_tiles, (False, True, True)), (n_tiles + 1, (False, False, True))):
            @pl.when(s == tail)
            def _():
                if tail % 2 == 0:
                    stages(s, refs[:-4], du_even, du_odd, dh_odd, dh_even, run)
                else:
                    stages(s, refs[:-4], du_odd, du_even, dh_even, dh_odd, run)

    def stages(s, refs, du_ref, du_prev, dh_fill, dh_prev, run):
        (xb_ref, xc_ref, ctx_ref, mod_ref, bmod_ref, ng_ref, win_ref, qlg_ref, wuq_ref, kvlg_ref, wukv_ref, qng_ref, kng_ref,
         cos_ref, slo_ref, shi_ref, cq_ref, ckv_ref, kr_ref, dq_ref, dk_ref, dv_ref, dga_ref, dgp_ref, dpl_ref,
         hb_ref, ha_ref, g1_ref,
         gx_ref, gwin_ref, gwuq_ref, gwukv_ref, dqlg_ref, dkvlg_ref, dqng_ref, dkng_ref, dng_ref, dmod_ref) = refs


        i = s
        is_lat = s > 0
        if run[0]:
            cq = cq_ref[...]
            rq = lax.rsqrt(jnp.mean(cq * cq, axis=-1, keepdims=True) + NORM_EPS)
            qhat = cq * rq
            qnb = (qhat * qlg_ref[...]).astype(BF16)
            q = _dot_nt(qnb, wuq_ref[...])
            ckv = ckv_ref[...]
            rkv = lax.rsqrt(jnp.mean(ckv * ckv, axis=-1, keepdims=True) + NORM_EPS)
            kvhat = ckv * rkv
            kvnb = (kvhat * kvlg_ref[...]).astype(BF16)
            kv = _dot(kvnb, wukv_ref[...])

        if run[1]:
            _, _, _, h2, _ = _modulated_input(s <= 1, xb_ref, ctx_ref, mod_ref, bmod_ref, ng_ref)
            dub = du_prev[...]
            du_g, du_m = dub[:, 0:N_GATES], dub[:, N_GATES:U_PAD]
            h2b = h2.astype(BF16)
            dh_fill[...] = _dot(du_g, win_ref[N_MLA:D_IN_PROJ, :]) + _dot(du_m, win_ref[0:U_PAD - N_GATES, :])
            gwin_ref[N_MLA:D_IN_PROJ, :] += _dot_tn(du_g, h2b)
            gwin_ref[0:N_MLA, :] += _dot_tn(du_m, h2b)[0:N_MLA]

        if run[2]:
            ctx3 = s <= 2
            xt, r, xg, _, scale = _modulated_input(ctx3, xc_ref, ctx_ref, mod_ref, bmod_ref, ng_ref)
            dh = dh_prev[...]
            dshift = jnp.sum(dh, axis=0, keepdims=True)
            dscale = jnp.sum(dh * xg, axis=0, keepdims=True)
            zero = jnp.zeros_like(dshift)
            dmod_ref[0:1, :] += jnp.where(ctx3, zero, dshift)
            dmod_ref[1:2, :] += jnp.where(ctx3, zero, dscale)
            dmod_ref[2:3, :] += jnp.where(ctx3, dshift, zero)
            dmod_ref[3:4, :] += jnp.where(ctx3, dscale, zero)
            dxg = dh * (1.0 + scale)
            xr = xt * r
            dng_ref[...] += jnp.sum(dxg * xr, axis=0, keepdims=True)
            dxn = dxg * ng_ref[...]
            gx_ref[...] = g1_ref[...] + r * (dxn - xr * jnp.mean(dxn * xr, axis=-1, keepdims=True))

        if not run[0]:
            return

        cos_t, slo_t, shi_t = cos_ref[...], slo_ref[...], shi_ref[...]
        qg = qng_ref[...]
        dq_heads = []
        dqng = jnp.zeros((1, D_HEAD_PAD), F32)
        for hd in range(N_HEADS):
            qh = q[:, hd * D_HEAD_PAD:(hd + 1) * D_HEAD_PAD]
            rr = lax.rsqrt(jnp.sum(qh * qh, axis=-1, keepdims=True) * (1.0 / D_HEAD) + NORM_EPS)
            yq = qh * rr
            dpost = jnp.where(is_lat, dq_ref[hd] * ATTN_SCALE, 0.0)
            dyn = jnp.concatenate([dpost[:, 0:D_NOPE], _rope_t(dpost[:, D_NOPE:], cos_t, slo_t, shi_t)], axis=1)
            dqng = dqng + jnp.sum(dyn * yq, axis=0, keepdims=True)
            dyg = dyn * qg
            dq_heads.append(rr * (dyg - yq * (jnp.sum(dyg * yq, axis=-1, keepdims=True) * (1.0 / D_HEAD))))
        dqng_ref[...] += dqng
        dqf = jnp.concatenate(dq_heads, axis=1).astype(BF16)

        krz = kr_ref[...]
        kr_ss = jnp.sum(krz * krz, axis=-1, keepdims=True)
        kg = kng_ref[...]
        kg_n, kg_r = kg[:, 0:D_NOPE], kg[:, D_NOPE:]
        dkv_parts = []
        dkr = jnp.zeros((TILE, 128), F32)
        dkng_n = jnp.zeros((1, D_NOPE), F32)
        dkng_r = jnp.zeros((1, 128), F32)
        for hd in range(N_HEADS):
            kn = kv[:, hd * 256:hd * 256 + D_NOPE]
            rr = lax.rsqrt((jnp.sum(kn * kn, axis=-1, keepdims=True) + kr_ss) * (1.0 / D_HEAD) + NORM_EPS)
            yn, yr = kn * rr, krz * rr
            dk_h = jnp.transpose(dk_ref[hd]) * LN_2
            dpn = dk_h[:, 0:D_NOPE]
            dpr = _rope_t(dk_h[:, D_NOPE:], cos_t, slo_t, shi_t)
            dkng_n = dkng_n + jnp.sum(dpn * yn, axis=0, keepdims=True)
            dkng_r = dkng_r + jnp.sum(dpr * yr, axis=0, keepdims=True)
            dyn, dyr = dpn * kg_n, dpr * kg_r
            proj = (jnp.sum(dyn * yn, axis=-1, keepdims=True) + jnp.sum(dyr * yr, axis=-1, keepdims=True)) * (1.0 / D_HEAD)
            dkv_parts.append(rr * (dyn - yn * proj))
            dkv_parts.append(jnp.transpose(dv_ref[hd]))
            dkr = dkr + rr * (dyr - yr * proj)
        dkng_ref[:, 0:D_NOPE] += dkng_n
        dkng_ref[:, D_NOPE:] += dkng_r
        dkvf = jnp.concatenate(dkv_parts, axis=1).astype(BF16)

        lat_t = jnp.maximum(i - 1, 0)
        dpl_t = dpl_ref[...].astype(F32)
        ds = jnp.concatenate([jnp.where(i > 1, hb_ref[...].astype(F32), 0.0), dpl_t,
                              jnp.where(i < n_tiles - 1, ha_ref[...].astype(F32), 0.0)], axis=0)
        tpos = lat_t * TILE - HALO + lax.broadcasted_iota(jnp.int32, (TILE + 2 * HALO, 1), 0)
        for g, w in enumerate(POOL_WINDOWS):
            sl = slice(g * GROUP_DIM, (g + 1) * GROUP_DIM)
            wsum = _window_sum(ds[:, sl] * _inv_count(tpos, w, seq), w, True)[HALO:HALO + TILE]
            du_ref[:, O_PIN + g * GROUP_DIM:O_PIN + (g + 1) * GROUP_DIM] = jnp.where(
                is_lat, wsum - dpl_t[:, sl], 0.0).astype(BF16)

        du_ref[:, O_GA:O_GA + D_ATTN] = jnp.where(is_lat, dga_ref[...], jnp.zeros((), BF16))
        du_ref[:, O_GP:O_GP + D_POOL] = jnp.where(is_lat, dgp_ref[...], jnp.zeros((), BF16))
        du_ref[:, O_KR:U_PAD] = dkr.astype(BF16)

        gwuq_ref[...] += _dot_tn(dqf, qnb)
        dqn = _dot(dqf, wuq_ref[...])
        gwukv_ref[...] += _dot_tn(dkvf, kvnb)
        dkvn = _dot_nt(dkvf, wukv_ref[...])
        dqlg_ref[...] += jnp.sum(dqn * qhat, axis=0, keepdims=True)
        dqhat = dqn * qlg_ref[...]
        dcq = rq * (dqhat - qhat * jnp.mean(dqhat * qhat, axis=-1, keepdims=True))
        dkvlg_ref[...] += jnp.sum(dkvn * kvhat, axis=0, keepdims=True)
        dkvhat = dkvn * kvlg_ref[...]
        dckv = rkv * (dkvhat - kvhat * jnp.mean(dkvhat * kvhat, axis=-1, keepdims=True))
        du_ref[:, O_CQ:O_CQ + R_Q] = dcq.astype(BF16)
        du_ref[:, O_CKV:O_CKV + R_KV] = dckv.astype(BF16)

    t1 = lambda s: jnp.minimum(s, n_tiles - 1)
    t2 = lambda s: jnp.clip(s - 1, 0, n_tiles - 1)
    t3 = lambda s: jnp.clip(s - 2, 0, n_tiles - 1)
    latent = lambda t: jnp.maximum(t - 1, 0)
    lat = lambda s: (latent(t1(s)), 0)
    lat3 = lambda s: (latent(t3(s)), 0)
    full = lambda shape: pl.BlockSpec(shape, lambda s: (0,) * len(shape))
    rope_spec = pl.BlockSpec((TILE, 128), lambda s: (t1(s), 0))
    return pl.pallas_call(
        body, name="inproj_bwd", grid=(n_tiles + 2,),
        out_shape=(jax.ShapeDtypeStruct((seq, D_MODEL), F32), jax.ShapeDtypeStruct((D_IN_PROJ, D_MODEL), F32),
                   jax.ShapeDtypeStruct((N_HEADS * D_HEAD_PAD, R_Q), F32), jax.ShapeDtypeStruct((N_HEADS * 256, R_KV), F32),
                   jax.ShapeDtypeStruct((1, R_Q), F32), jax.ShapeDtypeStruct((1, R_KV), F32),
                   jax.ShapeDtypeStruct((1, D_HEAD_PAD), F32), jax.ShapeDtypeStruct((1, D_HEAD_PAD), F32),
                   jax.ShapeDtypeStruct((1, D_MODEL), F32), jax.ShapeDtypeStruct((8, D_MODEL), F32)),
        in_specs=[pl.BlockSpec((TILE, D_MODEL), lambda s: (latent(t2(s)), 0)), pl.BlockSpec((TILE, D_MODEL), lat3),
                  full((TILE, D_MODEL)), full((8, D_MODEL)), full((8, D_MODEL)),
                  full((1, D_MODEL)), full((D_IN_PROJ, D_MODEL)), full((1, R_Q)), full((N_HEADS * D_HEAD_PAD, R_Q)),
                  full((1, R_KV)), full((R_KV, N_HEADS * 256)), full((1, D_HEAD_PAD)), full((1, D_HEAD_PAD)),
                  rope_spec, rope_spec, rope_spec,
                  pl.BlockSpec((TILE, R_Q), lambda s: (t1(s), (O_CQ - N_GATES) // R_Q)),
                  pl.BlockSpec((TILE, R_KV), lambda s: (t1(s), (O_CKV - N_GATES) // R_KV)),
                  pl.BlockSpec((TILE, 128), lambda s: (t1(s), (O_KR - N_GATES) // 128)),
                  pl.BlockSpec((N_HEADS, TILE, D_HEAD_PAD), lambda s: (0, latent(t1(s)), 0)),
                  pl.BlockSpec((N_HEADS, D_HEAD_PAD, TILE), lambda s: (0, 0, t1(s))),
                  pl.BlockSpec((N_HEADS, D_V, TILE), lambda s: (0, 0, t1(s))),
                  pl.BlockSpec((TILE, D_ATTN), lat), pl.BlockSpec((TILE, D_POOL), lat), pl.BlockSpec((TILE, D_POOL), lat),
                  pl.BlockSpec((HALO, D_POOL), lambda s: (jnp.maximum((t1(s) - 1) * rows8 - 1, 0), 0)),
                  pl.BlockSpec((HALO, D_POOL), lambda s: (jnp.minimum(jnp.maximum(t1(s), 1) * rows8, last8), 0)),
                  pl.BlockSpec((TILE, D_MODEL), lat3)],
        out_specs=(pl.BlockSpec((TILE, D_MODEL), lat3), full((D_IN_PROJ, D_MODEL)), full((N_HEADS * D_HEAD_PAD, R_Q)),
                   full((N_HEADS * 256, R_KV)), full((1, R_Q)), full((1, R_KV)), full((1, D_HEAD_PAD)),
                   full((1, D_HEAD_PAD)), full((1, D_MODEL)), full((8, D_MODEL))),
        scratch_shapes=[pltpu.VMEM((TILE, U_PAD), BF16), pltpu.VMEM((TILE, U_PAD), BF16),
                        pltpu.VMEM((TILE, D_MODEL), F32), pltpu.VMEM((TILE, D_MODEL), F32)],
        compiler_params=pltpu.CompilerParams(dimension_semantics=("arbitrary",), vmem_limit_bytes=VMEM_LIMIT),
    )(x, x, ctx, modrows, bmodrows, norm_g, w_in_p, q_lora_g, w_uq_p, kv_lora_g, w_ukv, qng_p, kng_p, cos, slo, shi,
      u, u, u, dq, dk, dv, dga, dgp, dpl, dpl, dpl, g1)


SMALL_LAYOUT = ((0, D_MODEL), (1, R_Q), (2, R_KV), (3, D_HEAD_PAD), (4, D_HEAD_PAD), (5, D_POOL))
ROW_DMOD_B, ROW_DMOD_C, ROW_LOSS = 6, 9, 12


def _epilogue(parts, landed, smalls, dmod4, dgate, loss_acc, w_mod_l):
    n_a, n_l, n_s = len(parts), len(landed), len(smalls)
    n_mod = w_mod_l.shape[1]
    blk = [p.shape[1:] for p in parts]
    small_out = [D_MODEL, R_Q, R_KV, D_HEAD, D_HEAD, D_POOL]

    def body(*refs):
        part_refs = refs[0:n_a]
        land_refs = refs[n_a:n_a + n_l]
        small_in = refs[n_a + n_l:n_a + n_l + n_s]
        dmod4_ref, dgate_ref, loss_ref, wmod_ref = refs[n_a + n_l + n_s:n_a + n_l + n_s + 4]
        outs = refs[n_a + n_l + n_s + 4:]
        red_refs = outs[0:n_a]
        landsum_refs = outs[n_a:n_a + n_l]
        ccg_ref = outs[n_a + n_l]
        sum_refs = outs[n_a + n_l + 1:n_a + n_l + 1 + n_s]
        gbmod_ref, dmy_ref, lossout_ref = outs[n_a + n_l + 1 + n_s:n_a + n_l + 4 + n_s]
        scr = outs[n_a + n_l + 4 + n_s:]
        recv1, own1, sum1b, recv2 = scr[0:n_a], scr[n_a:2 * n_a], scr[2 * n_a:3 * n_a], scr[3 * n_a:4 * n_a]
        small_ref, smallg_ref, tot_ref, cc_ref = scr[4 * n_a:4 * n_a + 4]
        land_vmem = scr[4 * n_a + 4:4 * n_a + 4 + n_l]
        ssem1, rsem1, lsem, ssem2, rsem2, ssem_s, rsem_s, ssem_cc, rsem_cc, land_sem = scr[4 * n_a + 4 + n_l:]
        x, y, c = _coords()
        me = (x, y, c)
        sibling = (x, y, 1 - c)

        small_ref[...] = jnp.zeros_like(small_ref)
        for ref, (row, width) in zip(small_in, SMALL_LAYOUT):
            small_ref[row:row + 1, 0:width] = ref[...]
        small_ref[ROW_DMOD_B:ROW_DMOD_B + 2, :] = dmod4_ref[0:2, :]
        small_ref[ROW_DMOD_B + 2:ROW_DMOD_B + 3, :] = dgate_ref[...]
        small_ref[ROW_DMOD_C:ROW_DMOD_C + 2, :] = dmod4_ref[2:4, :]
        small_ref[ROW_LOSS:ROW_LOSS + 1, 0:128] = loss_ref[0:1, :]
        smallg_ref[_lin(me)] = small_ref[...]
        s_recv, s_send = _gather_to_all(small_ref, smallg_ref, ssem_s, rsem_s)

        stage1, own_copies = [], []
        for a in range(n_a):
            for b in range(4):
                dev = 4 * (b >> 1) + 2 * (b & 1)
                stage1.append(pltpu.make_async_remote_copy(
                    src_ref=part_refs[a].at[dev + (1 - c)], dst_ref=recv1[a].at[b],
                    send_sem=ssem1.at[a, b], recv_sem=rsem1.at[a, b], device_id=sibling, device_id_type=MESH))
                own_copies.append(pltpu.make_async_copy(part_refs[a].at[dev + c], own1[a].at[b], lsem.at[a, b]))
                stage1[-1].start()
                own_copies[-1].start()
        land_copies = [pltpu.make_async_copy(land_refs[n], land_vmem[n], land_sem.at[n]) for n in range(n_l)]
        for cp in land_copies:
            cp.start()

        s_recv()
        tot = smallg_ref[0]
        for d in range(1, N_DEV):
            tot = tot + smallg_ref[d]
        tot_ref[...] = tot
        for ref, (row, _), width in zip(sum_refs, SMALL_LAYOUT, small_out):
            ref[...] = tot_ref[row:row + 1, 0:width]
        lossout_ref[...] = tot_ref[8:16, 0:128]
        lin = _lin(me)

        def my_columns(three_rows):
            v = jnp.concatenate(three_rows, axis=1)
            out = jnp.zeros((1, n_mod), F32)
            for d in range(N_DEV):
                out = out + jnp.where(lin == d, v[:, d * n_mod:(d + 1) * n_mod], 0.0)
            return out

        rows3 = lambda ref, r0: [ref[r0 + t:r0 + t + 1, :] for t in range(3)]
        dmodc = rows3(tot_ref, ROW_DMOD_C)
        gbmod_ref[...] = jnp.concatenate(rows3(tot_ref, ROW_DMOD_B), axis=1) + jnp.concatenate(dmodc, axis=1)
        dmy_ref[...] = jnp.zeros_like(dmy_ref)
        for b in range(N_DEV):
            dmy_ref[b:b + 1, :] = my_columns(rows3(smallg_ref.at[b], ROW_DMOD_B))
        dmy = my_columns(dmodc)
        dmy_ref[N_DEV:N_DEV + 1, :] = dmy
        part = _dot_nt(jnp.broadcast_to(dmy, (8, n_mod)).astype(BF16), wmod_ref[...].astype(BF16))

        cc_ref[...] = part
        ccg_ref[_lin(me)] = part
        cc_recv, cc_send = _gather_to_all(cc_ref, ccg_ref, ssem_cc, rsem_cc)

        stage2 = []
        for a in range(n_a):
            for b in range(4):
                stage1[4 * a + b].wait_recv()
                own_copies[4 * a + b].wait()
                sum1b[a][b] = (own1[a][b] + recv1[a][b]).astype(BF16)
            for k in (1, 2, 3):
                tx, ty = _flip(x, (k >> 1) & 1), _flip(y, k & 1)
                stage2.append(pltpu.make_async_remote_copy(
                    src_ref=sum1b[a].at[2 * tx + ty], dst_ref=recv2[a].at[k - 1], send_sem=ssem2.at[a, k - 1],
                    recv_sem=rsem2.at[a, k - 1], device_id=(tx, ty, c), device_id_type=MESH))
                stage2[-1].start()
        for a in range(n_a):
            own = own1[a][2 * x + y] + recv1[a][2 * x + y]
            for k in (1, 2, 3):
                stage2[3 * a + k - 1].wait_recv()
                own = own + recv2[a][k - 1].astype(F32)
            red_refs[a][...] = own

        for cp in land_copies:
            cp.wait()
        for ref, out in zip(land_vmem, landsum_refs):
            acc = ref[0].astype(F32)
            for d in range(1, N_DEV):
                acc = acc + ref[d].astype(F32)
            out[...] = acc
        cc_recv()
        for cp in stage1 + stage2:
            cp.wait_send()
        s_send()
        cc_send()

    vm = pl.BlockSpec(memory_space=pltpu.VMEM)
    sds = jax.ShapeDtypeStruct
    return pl.pallas_call(
        body, name="epilogue_reduce",
        out_shape=(*[sds(s, F32) for s in blk], *[sds(a.shape[1:], F32) for a in landed], sds((N_DEV, 8, D_MODEL), F32),
                   *[sds((1, w), F32) for w in small_out], sds((1, 3 * D_MODEL), F32), sds((16, n_mod), F32),
                   sds((8, 128), F32)),
        in_specs=[pl.BlockSpec(memory_space=pl.ANY)] * (n_a + n_l) + [vm] * (n_s + 4),
        out_specs=tuple([vm] * (n_a + n_l + n_s + 4)),
        scratch_shapes=[*[pltpu.VMEM((4,) + s, F32) for s in blk], *[pltpu.VMEM((4,) + s, F32) for s in blk],
                        *[pltpu.VMEM((4,) + s, BF16) for s in blk], *[pltpu.VMEM((3,) + s, BF16) for s in blk],
                        pltpu.VMEM((SMALL_ROWS, D_MODEL), F32), pltpu.VMEM((N_DEV, SMALL_ROWS, D_MODEL), F32),
                        pltpu.VMEM((SMALL_ROWS, D_MODEL), F32), pltpu.VMEM((8, D_MODEL), F32),
                        *[pltpu.VMEM(a.shape, a.dtype) for a in landed],
                        pltpu.SemaphoreType.DMA((n_a, 4)), pltpu.SemaphoreType.DMA((n_a, 4)), pltpu.SemaphoreType.DMA((n_a, 4)),
                        pltpu.SemaphoreType.DMA((n_a, 3)), pltpu.SemaphoreType.DMA((n_a, 3)),
                        pltpu.SemaphoreType.DMA((7,)), pltpu.SemaphoreType.DMA((7,)),
                        pltpu.SemaphoreType.DMA((7,)), pltpu.SemaphoreType.DMA((7,)), pltpu.SemaphoreType.DMA((n_l,))],
        compiler_params=pltpu.CompilerParams(vmem_limit_bytes=VMEM_LIMIT),
    )(*parts, *landed, *smalls, dmod4, dgate, loss_acc, w_mod_l)


def _adamw(weights, grads, ms, vs, c_all_t, dmod_my, p2g):
    n = len(weights)

    def body(*refs):
        w_refs = refs[0:n]
        g_in = refs[n:2 * n - 2]
        m_refs = refs[2 * n - 2:3 * n - 2]
        v_refs = refs[3 * n - 2:4 * n - 2]
        cat_ref, dmod_ref, p2g_ref = refs[4 * n - 2:4 * n + 1]
        outs = refs[4 * n + 1:]
        g_refs, d_refs, nm_refs, nv_refs = outs[0:n], outs[n:2 * n], outs[2 * n:3 * n], outs[3 * n:4 * n]
        gcc_ref, gwm_ref = g_refs[0], g_refs[1]

        part = p2g_ref[0, 0:1, :]
        for d in range(1, N_DEV):
            part = part + p2g_ref[d, 0:1, :]
        cc = w_refs[0][...]
        sg = _sigmoid(cc)
        gcc_ref[...] = part * (sg * (1.0 + cc * (1.0 - sg)))
        cat = cat_ref[...]
        gwm_ref[...] = lax.dot_general(cat * _sigmoid(cat), dmod_ref[...], (((1,), (0,)), ((), ())), precision=HIGHEST,
                                       preferred_element_type=F32)
        for idx in range(n):
            if idx >= 2:
                g_refs[idx][...] = g_in[idx - 2][...]
            g = g_refs[idx][...]
            m = ADAM_B1 * m_refs[idx][...] + (1.0 - ADAM_B1) * g
            v = ADAM_B2 * v_refs[idx][...] + (1.0 - ADAM_B2) * (g * g)
            m_hat = m / (1.0 - ADAM_B1 ** ADAM_STEP)
            v_hat = v / (1.0 - ADAM_B2 ** ADAM_STEP)
            d_refs[idx][...] = -ADAM_LR * (m_hat / (jnp.sqrt(v_hat) + ADAM_EPS) + ADAM_WD * w_refs[idx][...])
            nm_refs[idx][...] = m
            nv_refs[idx][...] = v

    vm = pl.BlockSpec(memory_space=pltpu.VMEM)
    shapes = [jax.ShapeDtypeStruct(w.shape, F32) for w in weights]
    args = list(weights) + list(grads[2:]) + list(ms) + list(vs) + [c_all_t, dmod_my, p2g]
    out_shape = tuple(shapes * 4)
    return pl.pallas_call(
        body, name="adamw_update", out_shape=out_shape, in_specs=[vm] * len(args), out_specs=tuple([vm] * len(out_shape)),
        compiler_params=pltpu.CompilerParams(vmem_limit_bytes=VMEM_LIMIT),
    )(*args)


def _rope_tables(seq):
    rows = seq // GRID_W
    row = np.repeat(np.arange(rows, dtype=np.float32), GRID_W)
    col = np.tile(np.arange(GRID_W, dtype=np.float32), rows)
    n_freq = D_ROPE // 4
    inv = (np.float32(ROPE_BASE) ** (-np.arange(n_freq, dtype=np.float32) / np.float32(n_freq))).astype(np.float32)
    ang_r = (row[:, None] * inv).astype(np.float32)
    ang_c = (col[:, None] * inv).astype(np.float32)
    ang = np.concatenate([ang_r, ang_r, ang_c, ang_c], axis=-1)
    cos, sin = np.cos(ang).astype(np.float32), np.sin(ang).astype(np.float32)
    low = (np.arange(D_ROPE) % 32) < 16

    def table(t, fill):
        out = np.full((TILE + seq, 128), fill, np.float32)
        out[TILE:, 0:D_ROPE] = t
        return jnp.asarray(out)

    return table(cos, 1.0), table(np.where(low, -sin, 0.0), 0.0), table(np.where(low, 0.0, sin), 0.0)


def kernel(x, c, ctx, c_ctx, w_mod, b_mod, norm_g, w_in, q_lora_g, w_uq, kv_lora_g, w_ukv, q_norm_g, k_norm_g, w_pool, pool_scale, w_out, loss_target, m_c_ctx, m_w_mod, m_b_mod, m_norm_g, m_w_in, m_q_lora_g, m_w_uq, m_kv_lora_g, m_w_ukv, m_q_norm_g, m_k_norm_g, m_w_pool, m_pool_scale, m_w_out, v_c_ctx, v_w_mod, v_b_mod, v_norm_g, v_w_in, v_q_lora_g, v_w_uq, v_kv_lora_g, v_w_ukv, v_q_norm_g, v_k_norm_g, v_w_pool, v_pool_scale, v_w_out):
    seq = x.shape[1]
    assert ctx.shape[1] == TILE and seq % TILE == 0 and seq % GRID_W == 0
    ix, iy, ic = lax.axis_index("x"), lax.axis_index("y"), lax.axis_index("c")
    me = 4 * ix + 2 * iy + ic
    x2, ctx2, tgt2 = x[0], ctx[0], loss_target[0]
    w_mod_l, w_ukv_l, w_out_l = w_mod[0], w_ukv[0], w_out[0]
    c_ctx_row = c_ctx.reshape(1, D_MODEL)

    tr = lambda a: jnp.transpose(a[0])
    w_in_tl, w_uq_tl = tr(w_in), tr(w_uq)
    cg, modg, wg_in, wg_uq, wg_ukv = _prologue(
        c, c_ctx_row, w_mod_l,
        [w_in_tl, w_uq_tl, w_ukv_l])
    mod_all = jnp.transpose(modg, (1, 0, 2)).reshape(16, 3 * D_MODEL)
    mod_b = lax.dynamic_slice_in_dim(mod_all, me, 1, axis=0).reshape(3, D_MODEL)
    mod_c = mod_all[8].reshape(3, D_MODEL)
    modrows = jnp.concatenate([mod_b, mod_c[0:2], jnp.zeros((3, D_MODEL), F32)], axis=0)
    b3 = b_mod.reshape(3, D_MODEL)
    bmodrows = jnp.concatenate([b3, b3[0:2], jnp.zeros((3, D_MODEL), F32)], axis=0)

    w_in_p = wg_in.reshape(D_IN_PROJ, D_MODEL)
    w_uq_p = jnp.concatenate([wg_uq.reshape(N_HEADS, D_HEAD, R_Q), jnp.zeros((N_HEADS, D_HEAD_PAD - D_HEAD, R_Q), BF16)],
                             axis=1).reshape(N_HEADS * D_HEAD_PAD, R_Q)
    w_ukv_f = jnp.transpose(wg_ukv, (1, 0, 2)).reshape(R_KV, N_HEADS * 256)
    qng_p = jnp.concatenate([q_norm_g, jnp.zeros((1, D_HEAD_PAD - D_HEAD), F32)], axis=1)
    kng_p = jnp.concatenate([k_norm_g, jnp.zeros((1, D_HEAD_PAD - D_HEAD), F32)], axis=1)
    cos, slo, shi = _rope_tables(seq)

    u_gates, u_mla, q, k, v = _inproj_fwd(x2, ctx2, modrows, bmodrows, norm_g, w_in_p, q_lora_g, w_uq_p, kv_lora_g, w_ukv_f,
                             qng_p, kng_p, cos, slo, shi)
    attn, lse, wg_out = _attn_fwd(q, k, v, min(2048, seq), w_out_l.astype(BF16))
    (loss_acc, g1, dgate, d_attn, dga, dgp, dpl, gwo_b, gwp, dps) = _mix(
        x2, tgt2, attn, u_gates, modrows, bmodrows, w_pool[0], pool_scale, wg_out.reshape(D_MODEL, D_MODEL))

    blocks = lambda a: a.reshape((N_DEV, a.shape[0] // N_DEV) + a.shape[1:])
    dq, dk, dv, land_wo, land_wp = _attn_bwd(q, k, v, attn, lse, d_attn.reshape(seq, D_ATTN), min(1024, seq), blocks(gwo_b),
                                             gwp.reshape(4 * GROUP_DIM, GROUP_DIM))
    (grad_x, gwin_t, gwuq_p, gwukv_t, dqlg, dkvlg, dqng, dkng, dng, dmod4) = _inproj_bwd(
        x2, ctx2, modrows, bmodrows, norm_g, w_in_p, q_lora_g, w_uq_p, kv_lora_g, w_ukv_f, qng_p, kng_p, cos, slo, shi,
        u_mla, dq, dk, dv, dga, dgp, dpl, g1)

    gwuq_t = gwuq_p.reshape(N_HEADS, D_HEAD_PAD, R_Q)[:, 0:D_HEAD].reshape(N_HEADS * D_HEAD, R_Q)
    parts = [blocks(gwin_t), blocks(gwuq_t), blocks(gwukv_t)]
    (r_win, r_wuq, r_wukv, r_wout, g_w_pool, ccg, g_ng, g_qlg, g_kvlg, g_qng, g_kng, g_ps, g_bmod, dmod_my,
     loss_rows) = _epilogue(parts, [land_wo, land_wp], [dng, dqlg, dkvlg, dqng, dkng, dps], dmod4, dgate, loss_acc, w_mod_l)

    g_w_ukv = jnp.transpose(r_wukv)
    c_rows = cg[:, 0, :]
    c_all_t = jnp.transpose(jnp.concatenate([c_rows, c_ctx_row, jnp.zeros((16 - N_DEV - 1, D_MODEL), F32)], axis=0))

    weights = [c_ctx_row, w_mod_l, b_mod, norm_g, w_in_tl, q_lora_g, w_uq_tl, kv_lora_g, w_ukv_l, q_norm_g, k_norm_g,
               w_pool.reshape(4 * GROUP_DIM, GROUP_DIM), pool_scale, w_out_l]
    grads = [None, None, g_bmod, g_ng, r_win, g_qlg, r_wuq, g_kvlg, g_w_ukv, g_qng, g_kng, g_w_pool, g_ps, r_wout]
    ms = [m_c_ctx.reshape(1, D_MODEL), m_w_mod[0], m_b_mod, m_norm_g, tr(m_w_in), m_q_lora_g, tr(m_w_uq), m_kv_lora_g,
          m_w_ukv[0], m_q_norm_g, m_k_norm_g, m_w_pool.reshape(4 * GROUP_DIM, GROUP_DIM), m_pool_scale, m_w_out[0]]
    vs = [v_c_ctx.reshape(1, D_MODEL), v_w_mod[0], v_b_mod, v_norm_g, tr(v_w_in), v_q_lora_g, tr(v_w_uq), v_kv_lora_g,
          v_w_ukv[0], v_q_norm_g, v_k_norm_g, v_w_pool.reshape(4 * GROUP_DIM, GROUP_DIM), v_pool_scale, v_w_out[0]]
    outs = _adamw(weights, grads, ms, vs, c_all_t, dmod_my, ccg)
    n = len(weights)
    grads, deltas, new_m, new_v = outs[0:n], outs[n:2 * n], outs[2 * n:3 * n], outs[3 * n:4 * n]

    loss = loss_rows[ROW_LOSS - 8, 0]
    final = [c_ctx.shape, w_mod.shape, b_mod.shape, norm_g.shape, w_in.shape, q_lora_g.shape, w_uq.shape, kv_lora_g.shape,
             w_ukv.shape, q_norm_g.shape, k_norm_g.shape, w_pool.shape, pool_scale.shape, w_out.shape]
    transposed = (4, 6)

    def shaped(arrs):
        return [(jnp.transpose(a) if i in transposed else a).reshape(s) for i, (a, s) in enumerate(zip(arrs, final))]

    return (loss, grad_x[None], *shaped(grads), *shaped(deltas), *shaped(new_m), *shaped(new_v))
```

```python
import numpy as np

import jax
import jax.numpy as jnp
from jax import lax
from jax.experimental import pallas as pl
from jax.experimental.pallas import tpu as pltpu

F32 = jnp.float32
BF16 = jnp.bfloat16
MESH = pl.DeviceIdType.MESH
HIGHEST = lax.Precision.HIGHEST

D_MODEL = 1024
N_HEADS = 4
D_NOPE = 128
D_ROPE = 64
D_HEAD = D_NOPE + D_ROPE
D_HEAD_PAD = 256
D_V = 128
R_Q = 256
R_KV = 128
D_ATTN = 512
D_POOL = 512
POOL_WINDOWS = (2, 4, 8, 16)
GROUP_DIM = 128
GRID_W = 64
ROPE_BASE = 10000.0
NORM_EPS = 1e-6
ATTN_SCALE = D_HEAD ** -0.5
LOG2_E = 1.4426950408889634
LN_2 = 0.6931471805599453
ATTN_ROWS = 256
D_IN_PROJ = 1984
U_PAD = 2048
O_GA, O_PIN, O_GP, O_CQ, O_CKV, O_KR = 0, 512, 1024, 1536, 1792, 1920
TILE = 256
MIX_TILE = 512
Q_BLOCK = 128
HALO = 16
N_GATES = 1536
N_MLA = D_IN_PROJ - N_GATES
N_DEV = 8
VMEM_LIMIT = 56 * 1024 * 1024

ADAM_LR = 0.001
ADAM_B1 = 0.9
ADAM_B2 = 0.999
ADAM_EPS = 1e-08
ADAM_WD = 0.01
ADAM_STEP = 10

SMALL_ROWS = 16


def _dot(a, b):
    return lax.dot_general(a, b, (((1,), (0,)), ((), ())), preferred_element_type=F32)


def _dot_nt(a, b):
    return lax.dot_general(a, b, (((1,), (1,)), ((), ())), preferred_element_type=F32)


def _dot_tn(a, b):
    return lax.dot_general(a, b, (((0,), (0,)), ((), ())), preferred_element_type=F32)


def _sigmoid(x):
    return 1.0 / (1.0 + jnp.exp(-x))


def _rope(p, cos, slo, shi):
    return p * cos + pltpu.roll(p, 112, 1) * slo + pltpu.roll(p, 16, 1) * shi


def _rope_t(d, cos, slo, shi):
    return d * cos + pltpu.roll(d * slo, 16, 1) + pltpu.roll(d * shi, 112, 1)


def _shift_rows(a, k):
    n = a.shape[0]
    return pltpu.roll(a, (-k) % n, 0)


def _window_sum(x, w, transposed):
    s = (x + _shift_rows(x, 1)) if transposed else (_shift_rows(x, -1) + x)
    step = 1
    while 2 * step < w:
        s = _shift_rows(s, -step) + _shift_rows(s, step)
        step *= 2
    return s


def _inv_count(tpos, w, seq):
    lo = jnp.maximum(tpos - w // 2, 0)
    hi = jnp.minimum(tpos - w // 2 + w, seq)
    return 1.0 / jnp.maximum(hi - lo, 1).astype(F32)


def _coords():
    return lax.axis_index("x"), lax.axis_index("y"), lax.axis_index("c")


def _flip(v, bit):
    return (1 - v) if bit else v


def _peer(k):
    x, y, c = _coords()
    return (_flip(x, (k >> 2) & 1), _flip(y, (k >> 1) & 1), _flip(c, k & 1))


def _lin(p):
    return 4 * p[0] + 2 * p[1] + p[2]


def _gather_to_all(src_ref, slots_ref, send_sems, recv_sems):
    me = _coords()
    copies = []
    for k in range(1, N_DEV):
        cp = pltpu.make_async_remote_copy(
            src_ref=src_ref, dst_ref=slots_ref.at[_lin(me)], send_sem=send_sems.at[k - 1], recv_sem=recv_sems.at[k - 1],
            device_id=_peer(k), device_id_type=MESH)
        cp.start()
        copies.append(cp)

    def wait_recv():
        for k in range(1, N_DEV):
            pltpu.make_async_remote_copy(
                src_ref=src_ref, dst_ref=slots_ref.at[_lin(_peer(k))], send_sem=send_sems.at[k - 1],
                recv_sem=recv_sems.at[k - 1], device_id=_peer(k), device_id_type=MESH).wait_recv()

    def wait_send():
        for cp in copies:
            cp.wait_send()

    return wait_recv, wait_send


def _prologue(c8, cctx8, w_mod_l, shards):
    n_mod = w_mod_l.shape[1]
    n_w = len(shards)

    def body(c_ref, cctx_ref, wmod_ref, *refs):
        w_refs = refs[0:n_w]
        cg_ref, modg_ref = refs[n_w], refs[n_w + 1]
        wg_refs = refs[n_w + 2:2 * n_w + 2]
        modblk_ref = refs[2 * n_w + 2]
        wb_refs = refs[2 * n_w + 3:3 * n_w + 3]
        c8_ref = refs[3 * n_w + 3]
        ssem_w, rsem_w, lsem_w, ssem_c, rsem_c, ssem_m, rsem_m = refs[3 * n_w + 4:]
        x, y, c = _coords()
        me = (x, y, c)
        sibling = (x, y, 1 - c)
        chips = [(1 - x, y), (x, 1 - y), (1 - x, 1 - y)]

        def wcopies(k, block, to, own=False):
            out = []
            for a in range(n_w):
                slot = wg_refs[a].at[_lin(block)]
                out.append(pltpu.make_async_remote_copy(
                    src_ref=wb_refs[a] if own else slot, dst_ref=slot, send_sem=ssem_w.at[a, k], recv_sem=rsem_w.at[a, k],
                    device_id=to, device_id_type=MESH))
            return out

        c8_ref[...] = jnp.broadcast_to(c_ref[...], (8, D_MODEL))
        cg_ref[_lin(me)] = c8_ref[...]
        c_recv, c_send = _gather_to_all(c8_ref, cg_ref, ssem_c, rsem_c)

        for a in range(n_w):
            wb_refs[a][...] = w_refs[a][...].astype(BF16)
        first = wcopies(0, me, sibling, own=True)
        for j, chip in enumerate(chips):
            first += wcopies(1 + j, me, (*chip, c), own=True)
        late = [idx for idx in range(len(first)) if idx % n_w == 0 and idx >= n_w]
        for idx, cp in enumerate(first):
            if idx not in late:
                cp.start()
        mine = [pltpu.make_async_copy(wb_refs[a], wg_refs[a].at[_lin(me)], lsem_w.at[a]) for a in range(n_w)]
        for cp in mine:
            cp.start()

        c_recv()
        row = lax.broadcasted_iota(jnp.int32, (8, D_MODEL), 0)
        c_all = jnp.zeros((8, D_MODEL), F32)
        for d in range(N_DEV):
            c_all = c_all + jnp.where(row == d, cg_ref[d], 0.0)
        cc = jnp.broadcast_to(cctx_ref[...], (8, D_MODEL))
        a = jnp.concatenate([c_all * _sigmoid(c_all), cc * _sigmoid(cc)], axis=0)
        modblk_ref[...] = _dot(a.astype(BF16), wmod_ref[...].astype(BF16))
        modg_ref[_lin(me)] = modblk_ref[...]
        m_recv, m_send = _gather_to_all(modblk_ref, modg_ref, ssem_m, rsem_m)
        for idx in late:
            first[idx].start()
        m_recv()

        passed = []
        for j, chip in enumerate(chips):
            for cp in wcopies(1 + j, (*chip, c), me):
                cp.wait_recv()
            fwd = wcopies(4 + j, (*chip, c), sibling)
            for cp in fwd:
                cp.start()
            passed += fwd
        for cp in wcopies(0, sibling, me):
            cp.wait_recv()
        for j, chip in enumerate(chips):
            for cp in wcopies(4 + j, (*chip, 1 - c), me):
                cp.wait_recv()
        for cp in first + passed:
            cp.wait_send()
        for cp in mine:
            cp.wait()
        c_send()
        m_send()

    vm = pl.BlockSpec(memory_space=pltpu.VMEM)
    hbm = pl.BlockSpec(memory_space=pl.ANY)
    return pl.pallas_call(
        body, name="prologue_gather",
        out_shape=(jax.ShapeDtypeStruct((N_DEV, 8, D_MODEL), F32), jax.ShapeDtypeStruct((N_DEV, 16, n_mod), F32),
                   *[jax.ShapeDtypeStruct((N_DEV,) + s.shape, BF16) for s in shards]),
        in_specs=[vm] * (3 + n_w), out_specs=(vm, vm, *[hbm] * n_w),
        scratch_shapes=[pltpu.VMEM((16, n_mod), F32), *[pltpu.VMEM(s.shape, BF16) for s in shards],
                        pltpu.VMEM((8, D_MODEL), F32),
                        pltpu.SemaphoreType.DMA((n_w, 7)), pltpu.SemaphoreType.DMA((n_w, 7)), pltpu.SemaphoreType.DMA((n_w,)),
                        pltpu.SemaphoreType.DMA((7,)), pltpu.SemaphoreType.DMA((7,)),
                        pltpu.SemaphoreType.DMA((7,)), pltpu.SemaphoreType.DMA((7,))],
        compiler_params=pltpu.CompilerParams(vmem_limit_bytes=VMEM_LIMIT),
    )(c8, cctx8, w_mod_l, *shards)


def _modulated_input(is_ctx, x_ref, ctx_ref, mod_ref, bmod_ref, ng_ref):
    xt = jnp.where(is_ctx, ctx_ref[...], x_ref[...])
    shift = jnp.where(is_ctx, mod_ref[3:4, :] + bmod_ref[3:4, :], mod_ref[0:1, :] + bmod_ref[0:1, :])
    scale = jnp.where(is_ctx, mod_ref[4:5, :] + bmod_ref[4:5, :], mod_ref[1:2, :] + bmod_ref[1:2, :])
    r = lax.rsqrt(jnp.mean(xt * xt, axis=-1, keepdims=True) + NORM_EPS)
    xg = (xt * r) * ng_ref[...]
    h = xg * (1.0 + scale) + shift
    return xt, r, xg, h, scale


def _inproj_fwd(x, ctx, modrows, bmodrows, norm_g, w_in_p, q_lora_g, w_uq_p, kv_lora_g, w_ukv, qng_p, kng_p, cos, slo, shi):
    seq = x.shape[0]
    n_tiles = seq // TILE + 1
    tot = seq + TILE

    n_mla = R_Q + R_KV + 128

    def body(x_ref, ctx_ref, mod_ref, bmod_ref, ng_ref, win_ref, qlg_ref, wuq_ref, kvlg_ref, wukv_ref, qng_ref, kng_ref,
             cos_ref, slo_ref, shi_ref, ug_ref, um_ref, q_ref, k_ref, v_ref, stash):
        s = pl.program_id(0)

        @pl.when(s == 0)
        def _():
            stash[...] = jnp.zeros_like(stash)

        refs = (x_ref, ctx_ref, mod_ref, bmod_ref, ng_ref, win_ref, qlg_ref, wuq_ref, kvlg_ref, wukv_ref, qng_ref, kng_ref,
                cos_ref, slo_ref, shi_ref, ug_ref, um_ref, q_ref, k_ref, v_ref)
        stages(s, refs, stash, stash)

    def stages(s, refs, fill, prev_ref):
        (x_ref, ctx_ref, mod_ref, bmod_ref, ng_ref, win_ref, qlg_ref, wuq_ref, kvlg_ref, wukv_ref, qng_ref, kng_ref,
         cos_ref, slo_ref, shi_ref, ug_ref, um_ref, q_ref, k_ref, v_ref) = refs
        cos_t, slo_t, shi_t = cos_ref[...], slo_ref[...], shi_ref[...]
        prev = prev_ref[...]
        cq = prev[:, 0:R_Q]
        qn = cq * lax.rsqrt(jnp.mean(cq * cq, axis=-1, keepdims=True) + NORM_EPS) * qlg_ref[...]
        q = _dot_nt(qn.astype(BF16), wuq_ref[...])
        qg = qng_ref[...] * (ATTN_SCALE * LOG2_E)
        for hd in range(N_HEADS):
            qh = q[:, hd * D_HEAD_PAD:(hd + 1) * D_HEAD_PAD]
            rr = lax.rsqrt(jnp.sum(qh * qh, axis=-1, keepdims=True) * (1.0 / D_HEAD) + NORM_EPS)
            qy = qh * rr * qg
            q_ref[hd, :, 0:D_NOPE] = qy[:, 0:D_NOPE].astype(BF16)
            q_ref[hd, :, D_NOPE:D_HEAD_PAD] = _rope(qy[:, D_NOPE:D_HEAD_PAD], cos_t, slo_t, shi_t).astype(BF16)

        ckv = prev[:, R_Q:R_Q + R_KV]
        kvn = ckv * lax.rsqrt(jnp.mean(ckv * ckv, axis=-1, keepdims=True) + NORM_EPS) * kvlg_ref[...]
        kv = _dot(kvn.astype(BF16), wukv_ref[...])
        krz = prev[:, R_Q + R_KV:n_mla]
        kr_ss = jnp.sum(krz * krz, axis=-1, keepdims=True)
        kg = kng_ref[...]
        for hd in range(N_HEADS):
            kn = kv[:, hd * 256:hd * 256 + D_NOPE]
            rr = lax.rsqrt((jnp.sum(kn * kn, axis=-1, keepdims=True) + kr_ss) * (1.0 / D_HEAD) + NORM_EPS)
            k_ref[hd, :, 0:D_NOPE] = (kn * rr * kg[:, 0:D_NOPE]).astype(BF16)
            k_ref[hd, :, D_NOPE:D_HEAD_PAD] = _rope(krz * rr * kg[:, D_NOPE:D_HEAD_PAD], cos_t, slo_t, shi_t).astype(BF16)
            v_ref[hd] = kv[:, hd * 256 + D_NOPE:(hd + 1) * 256].astype(BF16)

        _, _, _, h, _ = _modulated_input(s == 0, x_ref, ctx_ref, mod_ref, bmod_ref, ng_ref)
        hb = h.astype(BF16)
        ug_ref[...] = _dot_nt(hb, win_ref[N_MLA:D_IN_PROJ, :]).astype(BF16)
        um = _dot_nt(hb, win_ref[0:n_mla, :])
        lane = lax.broadcasted_iota(jnp.int32, (TILE, 128), 1)
        um = jnp.concatenate([um[:, 0:R_Q + R_KV], jnp.where(lane < D_ROPE, um[:, R_Q + R_KV:n_mla], 0.0)], axis=1)
        um_ref[...] = um
        fill[...] = um

    first = lambda s: jnp.minimum(s, n_tiles - 1)
    second = lambda s: jnp.maximum(s - 1, 0)
    latent = lambda t: jnp.maximum(t - 1, 0)
    full = lambda shape: pl.BlockSpec(shape, lambda s: (0,) * len(shape))
    rope_spec = pl.BlockSpec((TILE, 128), lambda s: (second(s), 0))
    return pl.pallas_call(
        body, name="inproj_fwd", grid=(n_tiles + 1,),
        out_shape=(jax.ShapeDtypeStruct((seq, N_GATES), BF16), jax.ShapeDtypeStruct((tot, n_mla), F32),
                   jax.ShapeDtypeStruct((N_HEADS, seq, D_HEAD_PAD), BF16),
                   jax.ShapeDtypeStruct((N_HEADS, tot, D_HEAD_PAD), BF16),
                   jax.ShapeDtypeStruct((N_HEADS, tot, D_V), BF16)),
        in_specs=[pl.BlockSpec((TILE, D_MODEL), lambda s: (latent(first(s)), 0)), full((TILE, D_MODEL)), full((8, D_MODEL)),
                  full((8, D_MODEL)), full((1, D_MODEL)), full((D_IN_PROJ, D_MODEL)), full((1, R_Q)),
                  full((N_HEADS * D_HEAD_PAD, R_Q)), full((1, R_KV)), full((R_KV, N_HEADS * 256)), full((1, D_HEAD_PAD)),
                  full((1, D_HEAD_PAD)), rope_spec, rope_spec, rope_spec],
        out_specs=(pl.BlockSpec((TILE, N_GATES), lambda s: (latent(first(s)), 0)),
                   pl.BlockSpec((TILE, n_mla), lambda s: (first(s), 0)),
                   pl.BlockSpec((N_HEADS, TILE, D_HEAD_PAD), lambda s: (0, latent(second(s)), 0)),
                   pl.BlockSpec((N_HEADS, TILE, D_HEAD_PAD), lambda s: (0, second(s), 0)),
                   pl.BlockSpec((N_HEADS, TILE, D_V), lambda s: (0, second(s), 0))),
        scratch_shapes=[pltpu.VMEM((TILE, n_mla), F32)],
        compiler_params=pltpu.CompilerParams(dimension_semantics=("arbitrary",), vmem_limit_bytes=VMEM_LIMIT),
    )(x, ctx, modrows, bmodrows, norm_g, w_in_p, q_lora_g, w_uq_p, kv_lora_g, w_ukv, qng_p, kng_p, cos, slo, shi)


def _hosted_exchange(first, last, items, send_sems, recv_sems, local_sems):
    me = _lin(_coords())

    def remote(n, k, src, land, scatter):
        peer = _peer(k)
        return pltpu.make_async_remote_copy(
            src_ref=src.at[_lin(peer)] if scatter else src, dst_ref=land.at[me], send_sem=send_sems.at[n, k - 1],
            recv_sem=recv_sems.at[n, k - 1], device_id=peer, device_id_type=MESH)

    def local(n, src, land, scatter):
        return pltpu.make_async_copy(src.at[me] if scatter else src, land.at[me], local_sems.at[n])

    @pl.when(first)
    def _():
        for n, (src, land, scatter) in enumerate(items):
            for k in range(1, N_DEV):
                remote(n, k, src, land, scatter).start()
            local(n, src, land, scatter).start()

    @pl.when(last)
    def _():
        for n, (src, land, scatter) in enumerate(items):
            for k in range(1, N_DEV):
                peer = _peer(k)
                pltpu.make_async_remote_copy(
                    src_ref=src.at[0] if scatter else src, dst_ref=land.at[_lin(peer)], send_sem=send_sems.at[n, k - 1],
                    recv_sem=recv_sems.at[n, k - 1], device_id=peer, device_id_type=MESH).wait_recv()
            for k in range(1, N_DEV):
                remote(n, k, src, land, scatter).wait_send()
            local(n, src, land, scatter).wait()


def _attn_fwd(q, k, v, block_q, w_out_b):
    _, seq, _ = q.shape
    n_keys = k.shape[1]
    n_q = seq // block_q

    def body(q_ref, k_ref, v_ref, wo_ref, o_ref, lse_ref, wog_ref, ssem, rsem, lsem):
        h, i = pl.program_id(0), pl.program_id(1)
        _hosted_exchange((h == 0) & (i == 0), (h == N_HEADS - 1) & (i == n_q - 1), [(wo_ref, wog_ref, False)],
                         ssem, rsem, lsem)
        kb, vb = k_ref[0], v_ref[0]
        for r0 in range(0, block_q, ATTN_ROWS):
            rows = slice(r0, r0 + ATTN_ROWS)
            s = _dot_nt(q_ref[0, rows, :], kb)
            m = jnp.max(s, axis=-1, keepdims=True)
            p = jnp.exp2(s - m)
            l = jnp.sum(p, axis=-1, keepdims=True)
            o_ref[rows, :] = _dot(p.astype(BF16), vb) * (1.0 / l)
            lse_ref[0, rows, :] = m + jnp.log2(l)

    hbm = pl.BlockSpec(memory_space=pl.ANY)
    return pl.pallas_call(
        body, name="attn_fwd", grid=(N_HEADS, n_q),
        out_shape=(jax.ShapeDtypeStruct((seq, D_ATTN), F32), jax.ShapeDtypeStruct((N_HEADS, seq, 1), F32),
                   jax.ShapeDtypeStruct((N_DEV,) + w_out_b.shape, w_out_b.dtype)),
        in_specs=[pl.BlockSpec((1, block_q, D_HEAD_PAD), lambda h, i: (h, i, 0)),
                  pl.BlockSpec((1, n_keys, D_HEAD_PAD), lambda h, i: (h, 0, 0)),
                  pl.BlockSpec((1, n_keys, D_V), lambda h, i: (h, 0, 0)), hbm],
        out_specs=(pl.BlockSpec((block_q, D_V), lambda h, i: (i, h)),
                   pl.BlockSpec((1, block_q, 1), lambda h, i: (h, i, 0)), hbm),
        scratch_shapes=[pltpu.SemaphoreType.DMA((1, 7)), pltpu.SemaphoreType.DMA((1, 7)), pltpu.SemaphoreType.DMA((1,))],
        compiler_params=pltpu.CompilerParams(dimension_semantics=("arbitrary", "arbitrary"), vmem_limit_bytes=VMEM_LIMIT),
    )(q, k, v, w_out_b)


def _attn_bwd(q, k, v, o, lse, d_o, block_q, gwo_blocks, gwp):
    _, seq, _ = q.shape
    n_keys = k.shape[1]
    n_q = seq // block_q

    def body(q_ref, k_ref, v_ref, o_ref, lse_ref, do_ref, gwo_ref, gwp_ref, dq_ref, dkt_ref, dvt_ref, lwo_ref, lwp_ref,
             ssem, rsem, lsem):
        h, i = pl.program_id(0), pl.program_id(1)
        _hosted_exchange((h == 0) & (i == 0), (h == N_HEADS - 1) & (i == n_q - 1),
                         [(gwo_ref, lwo_ref, True), (gwp_ref, lwp_ref, False)], ssem, rsem, lsem)

        @pl.when(i == 0)
        def _():
            dkt_ref[...] = jnp.zeros_like(dkt_ref)
            dvt_ref[...] = jnp.zeros_like(dvt_ref)

        kb, vb = k_ref[0], v_ref[0]
        raws, ps = [], []
        for r0 in range(0, block_q, ATTN_ROWS):
            rows = slice(r0, r0 + ATTN_ROWS)
            d_out = do_ref[rows, :]
            p = jnp.exp2(_dot_nt(q_ref[0, rows, :], kb) - lse_ref[0, rows, :])
            dp = _dot_nt(d_out.astype(BF16), vb)
            delta = jnp.sum(d_out * o_ref[rows, :], axis=-1, keepdims=True)
            raw = (p * (dp - delta)).astype(BF16)
            dq_ref[0, rows, :] = _dot(raw, kb)
            raws.append(raw)
            ps.append(p.astype(BF16))
        dkt_ref[0] += _dot_tn(q_ref[0], jnp.concatenate(raws, axis=0))
        dvt_ref[0] += _dot_tn(do_ref[...].astype(BF16), jnp.concatenate(ps, axis=0))

    hbm = pl.BlockSpec(memory_space=pl.ANY)
    return pl.pallas_call(
        body, name="attn_bwd", grid=(N_HEADS, n_q),
        out_shape=(jax.ShapeDtypeStruct((N_HEADS, seq, D_HEAD_PAD), F32),
                   jax.ShapeDtypeStruct((N_HEADS, D_HEAD_PAD, n_keys), F32),
                   jax.ShapeDtypeStruct((N_HEADS, D_V, n_keys), F32),
                   jax.ShapeDtypeStruct(gwo_blocks.shape, gwo_blocks.dtype),
                   jax.ShapeDtypeStruct((N_DEV,) + gwp.shape, gwp.dtype)),
        in_specs=[pl.BlockSpec((1, block_q, D_HEAD_PAD), lambda h, i: (h, i, 0)),
                  pl.BlockSpec((1, n_keys, D_HEAD_PAD), lambda h, i: (h, 0, 0)),
                  pl.BlockSpec((1, n_keys, D_V), lambda h, i: (h, 0, 0)),
                  pl.BlockSpec((block_q, D_V), lambda h, i: (i, h)),
                  pl.BlockSpec((1, block_q, 1), lambda h, i: (h, i, 0)),
                  pl.BlockSpec((block_q, D_V), lambda h, i: (i, h)), hbm, hbm],
        out_specs=(pl.BlockSpec((1, block_q, D_HEAD_PAD), lambda h, i: (h, i, 0)),
                   pl.BlockSpec((1, D_HEAD_PAD, n_keys), lambda h, i: (h, 0, 0)),
                   pl.BlockSpec((1, D_V, n_keys), lambda h, i: (h, 0, 0)), hbm, hbm),
        scratch_shapes=[pltpu.SemaphoreType.DMA((2, 7)), pltpu.SemaphoreType.DMA((2, 7)), pltpu.SemaphoreType.DMA((2,))],
        compiler_params=pltpu.CompilerParams(dimension_semantics=("arbitrary", "arbitrary"), vmem_limit_bytes=VMEM_LIMIT),
    )(q, k, v, o, lse, d_o, gwo_blocks, gwp)


def _mix(x, target, attn, u, modrows, bmodrows, w_pool, pool_scale, w_out):
    seq = x.shape[0]
    rows = min(MIX_TILE, seq // 2)
    n_tiles = seq // rows
    rows8 = rows // HALO
    last8 = seq // HALO - 1

    n_blk = seq // Q_BLOCK
    per = rows // n_blk
    assert rows % n_blk == 0 and per % 8 == 0 and n_tiles >= 2
    n_stash = 8

    def body(x_ref, t_ref, o_hbm, ga_ref, pin_ref, hb_ref, ha_ref, gp_ref, mod_ref, bmod_ref, wp_ref, ps_ref, wo_ref,
             loss_ref, g1_ref, dgate_ref, do_hbm, dga_ref, dgp_ref, dpl_ref, gwob_ref, gwp_ref, dps_ref,
             abuf, dbuf, gwo_ref, *rest):
        stash_even, stash_odd = rest[0:n_stash], rest[n_stash:2 * n_stash]
        rsem, wsem = rest[2 * n_stash:]
        s = pl.program_id(0)

        def slab_copies(tile, slot, fetch):
            window = pl.ds(tile * per, per)
            if fetch:
                return [pltpu.make_async_copy(o_hbm.at[:, window, :], abuf.at[slot], rsem.at[slot])]
            return [pltpu.make_async_copy(dbuf.at[slot], do_hbm.at[:, window, :], wsem.at[slot])]

        def wait_all(slot, fetch):
            slab_copies(0, slot, fetch)[0].wait()

        a_slot = lax.rem(s, 2)
        b_slot = 1 - a_slot

        @pl.when(s == 0)
        def _():
            loss_ref[...] = jnp.zeros_like(loss_ref)
            dgate_ref[...] = jnp.zeros_like(dgate_ref)
            gwo_ref[...] = jnp.zeros_like(gwo_ref)
            gwp_ref[...] = jnp.zeros_like(gwp_ref)
            dps_ref[...] = jnp.zeros_like(dps_ref)
            for cp in slab_copies(0, 0, True):
                cp.start()

        @pl.when(s + 1 < n_tiles)
        def _():
            for cp in slab_copies(s + 1, b_slot, True):
                cp.start()

        @pl.when(s < n_tiles)
        def _():
            wait_all(a_slot, True)

        @pl.when(s >= 3)
        def _():
            wait_all(b_slot, False)

        gate = mod_ref[2:3, :] + bmod_ref[2:3, :]
        ps = ps_ref[...]

        def a_vector(slot):
            attn_t = jnp.swapaxes(abuf[slot], 0, 1).reshape(rows, D_ATTN)
            ga = ga_ref[...].astype(F32)
            sga = _sigmoid(ga)
            silu_ga = ga * sga
            pin = pin_ref[...].astype(F32)
            xs = jnp.concatenate([jnp.where(s > 0, hb_ref[...].astype(F32), 0.0), pin,
                                  jnp.where(s < n_tiles - 1, ha_ref[...].astype(F32), 0.0)], axis=0)
            tpos = s * rows + lax.broadcasted_iota(jnp.int32, (rows, 1), 0)
            pooled = []
            for g, w in enumerate(POOL_WINDOWS):
                sl = slice(g * GROUP_DIM, (g + 1) * GROUP_DIM)
                ws = _window_sum(xs[:, sl], w, False)[HALO:HALO + rows]
                pooled.append((ws * _inv_count(tpos, w, seq) - pin[:, sl]).astype(BF16))
            return attn_t, ga, sga, silu_ga, pooled

        def a_group_maps(pooled):
            return jnp.concatenate([_dot(pooled[g], wp_ref[g].astype(BF16)) for g in range(len(POOL_WINDOWS))], axis=1)

        def a_project(stash, yp, attn_t, ga, sga, silu_ga, pooled):
            zb_ref, _, fa_ref, sa_ref, fp_ref, sp_ref, yp_ref, pooled_ref = stash
            ypool = yp * ps
            gp = gp_ref[...].astype(F32)
            sgp = _sigmoid(gp)
            silu_gp = gp * sgp
            z = jnp.concatenate([silu_ga * attn_t, silu_gp * ypool], axis=1).astype(BF16)
            y = _dot(z, wo_ref[...])
            zb_ref[...] = z
            fa_ref[...] = attn_t * (sga * (1.0 + ga * (1.0 - sga)))
            sa_ref[...] = silu_ga
            fp_ref[...] = ypool * (sgp * (1.0 + gp * (1.0 - sgp)))
            sp_ref[...] = silu_gp
            yp_ref[...] = yp
            pooled_ref[...] = jnp.concatenate(pooled, axis=1)
            return y

        def a_loss(stash, y):
            res = x_ref[...] + gate * y - t_ref[...]
            loss_ref[...] += jnp.sum(res * res) * (0.5 / D_MODEL)
            dxn = res * (1.0 / D_MODEL)
            g1_ref[...] = dxn
            dgate_ref[...] += jnp.sum(dxn * y, axis=0, keepdims=True)
            stash[1][...] = (gate * dxn).astype(BF16)

        def b_dz(stash):
            return _dot_nt(stash[1][...], wo_ref[...])

        def b_gwo(stash):
            gwo_ref[...] += _dot_tn(stash[0][...], stash[1][...])

        def b_vector(stash, slot, dz):
            _, _, fa_ref, sa_ref, fp_ref, sp_ref, yp_ref, _ = stash
            dbra = dz[:, 0:D_ATTN]
            dbrp = dz[:, D_ATTN:]
            dga_ref[...] = (dbra * fa_ref[...]).astype(BF16)
            dbuf[slot] = jnp.swapaxes((dbra * sa_ref[...]).reshape(per, n_blk, D_ATTN), 0, 1)
            dgp_ref[...] = (dbrp * fp_ref[...]).astype(BF16)
            dyp_s = dbrp * sp_ref[...]
            dps_ref[...] += jnp.sum(dyp_s * yp_ref[...], axis=0, keepdims=True)
            return (dyp_s * ps).astype(BF16)

        def b_small(stash, dyp):
            pooled_ref = stash[7]
            for g in range(len(POOL_WINDOWS)):
                sl = slice(g * GROUP_DIM, (g + 1) * GROUP_DIM)
                gwp_ref[g] += _dot_tn(pooled_ref[:, sl], dyp[:, sl])
                dpl_ref[:, sl] = _dot_nt(dyp[:, sl], wp_ref[g].astype(BF16)).astype(BF16)

        def both(a_stash, b_stash, slot_a):
            dz = b_dz(b_stash)
            front = a_vector(slot_a)
            yp = a_group_maps(front[-1])
            b_gwo(b_stash)
            y = a_project(a_stash, yp, *front)
            dyp = b_vector(b_stash, 1 - slot_a, dz)
            a_loss(a_stash, y)
            b_small(b_stash, dyp)

        @pl.when(s == 0)
        def _():
            front = a_vector(0)
            a_loss(stash_even, a_project(stash_even, a_group_maps(front[-1]), *front))

        @pl.when((s > 0) & (s < n_tiles) & (a_slot == 0))
        def _():
            both(stash_even, stash_odd, 0)

        @pl.when((s > 0) & (s < n_tiles) & (a_slot == 1))
        def _():
            both(stash_odd, stash_even, 1)

        @pl.when(s == n_tiles)
        def _():
            last = (n_tiles - 1) % 2
            stash = stash_odd if last else stash_even
            dz = b_dz(stash)
            b_gwo(stash)
            b_small(stash, b_vector(stash, last, dz))
            gwob_ref[...] = gwo_ref[...].astype(BF16)

        @pl.when(s >= 1)
        def _():
            for cp in slab_copies(s - 1, b_slot, False):
                cp.start()

        @pl.when(s == n_tiles)
        def _():
            wait_all(b_slot, False)
            wait_all(a_slot, False)

    ta = lambda s: jnp.minimum(s, n_tiles - 1)
    tb = lambda s: jnp.maximum(s - 1, 0)
    tile = lambda w: pl.BlockSpec((rows, w), lambda s: (ta(s), 0))
    tile_b = lambda w: pl.BlockSpec((rows, w), lambda s: (tb(s), 0))
    ucol = lambda col: pl.BlockSpec((rows, 512), lambda s: (ta(s), col))
    full = lambda shape: pl.BlockSpec(shape, lambda s: (0,) * len(shape))
    stash_shapes = [pltpu.VMEM((rows, D_MODEL), BF16), pltpu.VMEM((rows, D_MODEL), BF16)] + \
        [pltpu.VMEM((rows, 512), F32)] * 5 + [pltpu.VMEM((rows, D_POOL), BF16)]
    return pl.pallas_call(
        body, name="mix_fwd_bwd", grid=(n_tiles + 1,),
        out_shape=(jax.ShapeDtypeStruct((8, 128), F32), jax.ShapeDtypeStruct((seq, D_MODEL), F32),
                   jax.ShapeDtypeStruct((1, D_MODEL), F32), jax.ShapeDtypeStruct((n_blk, Q_BLOCK, D_ATTN), F32),
                   jax.ShapeDtypeStruct((seq, D_ATTN), BF16), jax.ShapeDtypeStruct((seq, D_POOL), BF16),
                   jax.ShapeDtypeStruct((seq, D_POOL), BF16), jax.ShapeDtypeStruct((D_MODEL, D_MODEL), BF16),
                   jax.ShapeDtypeStruct((4, GROUP_DIM, GROUP_DIM), F32), jax.ShapeDtypeStruct((1, D_POOL), F32)),
        in_specs=[tile(D_MODEL), tile(D_MODEL), pl.BlockSpec(memory_space=pl.ANY), ucol(0), ucol(1),
                  pl.BlockSpec((HALO, 512), lambda s: (jnp.maximum(ta(s) * rows8 - 1, 0), 1)),
                  pl.BlockSpec((HALO, 512), lambda s: (jnp.minimum((ta(s) + 1) * rows8, last8), 1)),
                  ucol(2), full((8, D_MODEL)), full((8, D_MODEL)), full((4, GROUP_DIM, GROUP_DIM)), full((1, D_POOL)),
                  full((D_MODEL, D_MODEL))],
        out_specs=(full((8, 128)), tile(D_MODEL), full((1, D_MODEL)), pl.BlockSpec(memory_space=pl.ANY), tile_b(D_ATTN),
                   tile_b(D_POOL), tile_b(D_POOL), full((D_MODEL, D_MODEL)), full((4, GROUP_DIM, GROUP_DIM)),
                   full((1, D_POOL))),
        scratch_shapes=[pltpu.VMEM((2, n_blk, per, D_ATTN), F32), pltpu.VMEM((2, n_blk, per, D_ATTN), F32),
                        pltpu.VMEM((D_MODEL, D_MODEL), F32), *stash_shapes, *stash_shapes,
                        pltpu.SemaphoreType.DMA((2,)), pltpu.SemaphoreType.DMA((2,))],
        compiler_params=pltpu.CompilerParams(dimension_semantics=("arbitrary",), vmem_limit_bytes=VMEM_LIMIT),
    )(x, target, attn.reshape(n_blk, Q_BLOCK, D_ATTN), u, u, u, u, u, modrows, bmodrows, w_pool, pool_scale, w_out)


def _inproj_bwd(x, ctx, modrows, bmodrows, norm_g, w_in_p, q_lora_g, w_uq_p, kv_lora_g, w_ukv, qng_p, kng_p, cos, slo, shi,
                u, dq, dk, dv, dga, dgp, dpl, g1):
    seq = x.shape[0]
    n_tiles = seq // TILE + 1
    rows8 = TILE // HALO
    last8 = seq // HALO - 1

    def body(*refs):
        du_even, du_odd, dh_even, dh_odd = refs[-4:]
        gwin_ref, gwuq_ref, gwukv_ref, dqlg_ref, dkvlg_ref, dqng_ref, dkng_ref, dng_ref, dmod_ref = refs[-13:-4]
        s = pl.program_id(0)
        even = lax.rem(s, 2) == 0

        @pl.when(s == 0)
        def _():
            for ref in (gwin_ref, gwuq_ref, gwukv_ref, dqlg_ref, dkvlg_ref, dqng_ref, dkng_ref, dng_ref, dmod_ref,
                        du_odd, dh_even):
                ref[...] = jnp.zeros_like(ref)

        @pl.when((s < n_tiles) & even)
        def _():
            stages(s, refs[:-4], du_even, du_odd, dh_odd, dh_even, (True, True, True))

        @pl.when((s < n_tiles) & jnp.logical_not(even))
        def _():
            stages(s, refs[:-4], du_odd, du_even, dh_even, dh_odd, (True, True, True))

        for tail, run in ((n_tiles, (False, True, True)), (n_tiles + 1, (False, False, True))):
            @pl.when(s == tail)
            def _():
                if tail % 2 == 0:
                    stages(s, refs[:-4], du_even, du_odd, dh_odd, dh_even, run)
                else:
                    stages(s, refs[:-4], du_odd, du_even, dh_even, dh_odd, run)

    def stages(s, refs, du_ref, du_prev, dh_fill, dh_prev, run):
        (xb_ref, xc_ref, ctx_ref, mod_ref, bmod_ref, ng_ref, win_ref, qlg_ref, wuq_ref, kvlg_ref, wukv_ref, qng_ref, kng_ref,
         cos_ref, slo_ref, shi_ref, cq_ref, ckv_ref, kr_ref, dq_ref, dk_ref, dv_ref, dga_ref, dgp_ref, dpl_ref,
         hb_ref, ha_ref, g1_ref,
         gx_ref, gwin_ref, gwuq_ref, gwukv_ref, dqlg_ref, dkvlg_ref, dqng_ref, dkng_ref, dng_ref, dmod_ref) = refs


        i = s
        is_lat = s > 0
        if run[0]:
            cq = cq_ref[...]
            rq = lax.rsqrt(jnp.mean(cq * cq, axis=-1, keepdims=True) + NORM_EPS)
            qhat = cq * rq
            qnb = (qhat * qlg_ref[...]).astype(BF16)
            q = _dot_nt(qnb, wuq_ref[...])
            ckv = ckv_ref[...]
            rkv = lax.rsqrt(jnp.mean(ckv * ckv, axis=-1, keepdims=True) + NORM_EPS)
            kvhat = ckv * rkv
            kvnb = (kvhat * kvlg_ref[...]).astype(BF16)
            kv = _dot(kvnb, wukv_ref[...])

        if run[1]:
            _, _, _, h2, _ = _modulated_input(s <= 1, xb_ref, ctx_ref, mod_ref, bmod_ref, ng_ref)
            dub = du_prev[...]
            du_g, du_m = dub[:, 0:N_GATES], dub[:, N_GATES:U_PAD]
            h2b = h2.astype(BF16)
            dh_fill[...] = _dot(du_g, win_ref[N_MLA:D_IN_PROJ, :]) + _dot(du_m, win_ref[0:U_PAD - N_GATES, :])
            gwin_ref[N_MLA:D_IN_PROJ, :] += _dot_tn(du_g, h2b)
            gwin_ref[0:N_MLA, :] += _dot_tn(du_m, h2b)[0:N_MLA]

        if run[2]:
            ctx3 = s <= 2
            xt, r, xg, _, scale = _modulated_input(ctx3, xc_ref, ctx_ref, mod_ref, bmod_ref, ng_ref)
            dh = dh_prev[...]
            dshift = jnp.sum(dh, axis=0, keepdims=True)
            dscale = jnp.sum(dh * xg, axis=0, keepdims=True)
            zero = jnp.zeros_like(dshift)
            dmod_ref[0:1, :] += jnp.where(ctx3, zero, dshift)
            dmod_ref[1:2, :] += jnp.where(ctx3, zero, dscale)
            dmod_ref[2:3, :] += jnp.where(ctx3, dshift, zero)
            dmod_ref[3:4, :] += jnp.where(ctx3, dscale, zero)
            dxg = dh * (1.0 + scale)
            xr = xt * r
            dng_ref[...] += jnp.sum(dxg * xr, axis=0, keepdims=True)
            dxn = dxg * ng_ref[...]
            gx_ref[...] = g1_ref[...] + r * (dxn - xr * jnp.mean(dxn * xr, axis=-1, keepdims=True))

        if not run[0]:
            return

        cos_t, slo_t, shi_t = cos_ref[...], slo_ref[...], shi_ref[...]
        qg = qng_ref[...]
        dq_heads = []
        dqng = jnp.zeros((1, D_HEAD_PAD), F32)
        for hd in range(N_HEADS):
            qh = q[:, hd * D_HEAD_PAD:(hd + 1) * D_HEAD_PAD]
            rr = lax.rsqrt(jnp.sum(qh * qh, axis=-1, keepdims=True) * (1.0 / D_HEAD) + NORM_EPS)
            yq = qh * rr
            dpost = jnp.where(is_lat, dq_ref[hd] * ATTN_SCALE, 0.0)
            dyn = jnp.concatenate([dpost[:, 0:D_NOPE], _rope_t(dpost[:, D_NOPE:], cos_t, slo_t, shi_t)], axis=1)
            dqng = dqng + jnp.sum(dyn * yq, axis=0, keepdims=True)
            dyg = dyn * qg
            dq_heads.append(rr * (dyg - yq * (jnp.sum(dyg * yq, axis=-1, keepdims=True) * (1.0 / D_HEAD))))
        dqng_ref[...] += dqng
        dqf = jnp.concatenate(dq_heads, axis=1).astype(BF16)

        krz = kr_ref[...]
        kr_ss = jnp.sum(krz * krz, axis=-1, keepdims=True)
        kg = kng_ref[...]
        kg_n, kg_r = kg[:, 0:D_NOPE], kg[:, D_NOPE:]
        dkv_parts = []
        dkr = jnp.zeros((TILE, 128), F32)
        dkng_n = jnp.zeros((1, D_NOPE), F32)
        dkng_r = jnp.zeros((1, 128), F32)
        for hd in range(N_HEADS):
            kn = kv[:, hd * 256:hd * 256 + D_NOPE]
            rr = lax.rsqrt((jnp.sum(kn * kn, axis=-1, keepdims=True) + kr_ss) * (1.0 / D_HEAD) + NORM_EPS)
            yn, yr = kn * rr, krz * rr
            dk_h = jnp.transpose(dk_ref[hd]) * LN_2
            dpn = dk_h[:, 0:D_NOPE]
            dpr = _rope_t(dk_h[:, D_NOPE:], cos_t, slo_t, shi_t)
            dkng_n = dkng_n + jnp.sum(dpn * yn, axis=0, keepdims=True)
            dkng_r = dkng_r + jnp.sum(dpr * yr, axis=0, keepdims=True)
            dyn, dyr = dpn * kg_n, dpr * kg_r
            proj = (jnp.sum(dyn * yn, axis=-1, keepdims=True) + jnp.sum(dyr * yr, axis=-1, keepdims=True)) * (1.0 / D_HEAD)
            dkv_parts.append(rr * (dyn - yn * proj))
            dkv_parts.append(jnp.transpose(dv_ref[hd]))
            dkr = dkr + rr * (dyr - yr * proj)
        dkng_ref[:, 0:D_NOPE] += dkng_n
        dkng_ref[:, D_NOPE:] += dkng_r
        dkvf = jnp.concatenate(dkv_parts, axis=1).astype(BF16)

        lat_t = jnp.maximum(i - 1, 0)
        dpl_t = dpl_ref[...].astype(F32)
        ds = jnp.concatenate([jnp.where(i > 1, hb_ref[...].astype(F32), 0.0), dpl_t,
                              jnp.where(i < n_tiles - 1, ha_ref[...].astype(F32), 0.0)], axis=0)
        tpos = lat_t * TILE - HALO + lax.broadcasted_iota(jnp.int32, (TILE + 2 * HALO, 1), 0)
        for g, w in enumerate(POOL_WINDOWS):
            sl = slice(g * GROUP_DIM, (g + 1) * GROUP_DIM)
            wsum = _window_sum(ds[:, sl] * _inv_count(tpos, w, seq), w, True)[HALO:HALO + TILE]
            du_ref[:, O_PIN + g * GROUP_DIM:O_PIN + (g + 1) * GROUP_DIM] = jnp.where(
                is_lat, wsum - dpl_t[:, sl], 0.0).astype(BF16)

        du_ref[:, O_GA:O_GA + D_ATTN] = jnp.where(is_lat, dga_ref[...], jnp.zeros((), BF16))
        du_ref[:, O_GP:O_GP + D_POOL] = jnp.where(is_lat, dgp_ref[...], jnp.zeros((), BF16))
        du_ref[:, O_KR:U_PAD] = dkr.astype(BF16)

        gwuq_ref[...] += _dot_tn(dqf, qnb)
        dqn = _dot(dqf, wuq_ref[...])
        gwukv_ref[...] += _dot_tn(dkvf, kvnb)
        dkvn = _dot_nt(dkvf, wukv_ref[...])
        dqlg_ref[...] += jnp.sum(dqn * qhat, axis=0, keepdims=True)
        dqhat = dqn * qlg_ref[...]
        dcq = rq * (dqhat - qhat * jnp.mean(dqhat * qhat, axis=-1, keepdims=True))
        dkvlg_ref[...] += jnp.sum(dkvn * kvhat, axis=0, keepdims=True)
        dkvhat = dkvn * kvlg_ref[...]
        dckv = rkv * (dkvhat - kvhat * jnp.mean(dkvhat * kvhat, axis=-1, keepdims=True))
        du_ref[:, O_CQ:O_CQ + R_Q] = dcq.astype(BF16)
        du_ref[:, O_CKV:O_CKV + R_KV] = dckv.astype(BF16)

    t1 = lambda s: jnp.minimum(s, n_tiles - 1)
    t2 = lambda s: jnp.clip(s - 1, 0, n_tiles - 1)
    t3 = lambda s: jnp.clip(s - 2, 0, n_tiles - 1)
    latent = lambda t: jnp.maximum(t - 1, 0)
    lat = lambda s: (latent(t1(s)), 0)
    lat3 = lambda s: (latent(t3(s)), 0)
    full = lambda shape: pl.BlockSpec(shape, lambda s: (0,) * len(shape))
    rope_spec = pl.BlockSpec((TILE, 128), lambda s: (t1(s), 0))
    return pl.pallas_call(
        body, name="inproj_bwd", grid=(n_tiles + 2,),
        out_shape=(jax.ShapeDtypeStruct((seq, D_MODEL), F32), jax.ShapeDtypeStruct((D_IN_PROJ, D_MODEL), F32),
                   jax.ShapeDtypeStruct((N_HEADS * D_HEAD_PAD, R_Q), F32), jax.ShapeDtypeStruct((N_HEADS * 256, R_KV), F32),
                   jax.ShapeDtypeStruct((1, R_Q), F32), jax.ShapeDtypeStruct((1, R_KV), F32),
                   jax.ShapeDtypeStruct((1, D_HEAD_PAD), F32), jax.ShapeDtypeStruct((1, D_HEAD_PAD), F32),
                   jax.ShapeDtypeStruct((1, D_MODEL), F32), jax.ShapeDtypeStruct((8, D_MODEL), F32)),
        in_specs=[pl.BlockSpec((TILE, D_MODEL), lambda s: (latent(t2(s)), 0)), pl.BlockSpec((TILE, D_MODEL), lat3),
                  full((TILE, D_MODEL)), full((8, D_MODEL)), full((8, D_MODEL)),
                  full((1, D_MODEL)), full((D_IN_PROJ, D_MODEL)), full((1, R_Q)), full((N_HEADS * D_HEAD_PAD, R_Q)),
                  full((1, R_KV)), full((R_KV, N_HEADS * 256)), full((1, D_HEAD_PAD)), full((1, D_HEAD_PAD)),
                  rope_spec, rope_spec, rope_spec,
                  pl.BlockSpec((TILE, R_Q), lambda s: (t1(s), (O_CQ - N_GATES) // R_Q)),
                  pl.BlockSpec((TILE, R_KV), lambda s: (t1(s), (O_CKV - N_GATES) // R_KV)),
                  pl.BlockSpec((TILE, 128), lambda s: (t1(s), (O_KR - N_GATES) // 128)),
                  pl.BlockSpec((N_HEADS, TILE, D_HEAD_PAD), lambda s: (0, latent(t1(s)), 0)),
                  pl.BlockSpec((N_HEADS, D_HEAD_PAD, TILE), lambda s: (0, 0, t1(s))),
                  pl.BlockSpec((N_HEADS, D_V, TILE), lambda s: (0, 0, t1(s))),
                  pl.BlockSpec((TILE, D_ATTN), lat), pl.BlockSpec((TILE, D_POOL), lat), pl.BlockSpec((TILE, D_POOL), lat),
                  pl.BlockSpec((HALO, D_POOL), lambda s: (jnp.maximum((t1(s) - 1) * rows8 - 1, 0), 0)),
                  pl.BlockSpec((HALO, D_POOL), lambda s: (jnp.minimum(jnp.maximum(t1(s), 1) * rows8, last8), 0)),
                  pl.BlockSpec((TILE, D_MODEL), lat3)],
        out_specs=(pl.BlockSpec((TILE, D_MODEL), lat3), full((D_IN_PROJ, D_MODEL)), full((N_HEADS * D_HEAD_PAD, R_Q)),
                   full((N_HEADS * 256, R_KV)), full((1, R_Q)), full((1, R_KV)), full((1, D_HEAD_PAD)),
                   full((1, D_HEAD_PAD)), full((1, D_MODEL)), full((8, D_MODEL))),
        scratch_shapes=[pltpu.VMEM((TILE, U_PAD), BF16), pltpu.VMEM((TILE, U_PAD), BF16),
                        pltpu.VMEM((TILE, D_MODEL), F32), pltpu.VMEM((TILE, D_MODEL), F32)],
        compiler_params=pltpu.CompilerParams(dimension_semantics=("arbitrary",), vmem_limit_bytes=VMEM_LIMIT),
    )(x, x, ctx, modrows, bmodrows, norm_g, w_in_p, q_lora_g, w_uq_p, kv_lora_g, w_ukv, qng_p, kng_p, cos, slo, shi,
      u, u, u, dq, dk, dv, dga, dgp, dpl, dpl, dpl, g1)


SMALL_LAYOUT = ((0, D_MODEL), (1, R_Q), (2, R_KV), (3, D_HEAD_PAD), (4, D_HEAD_PAD), (5, D_POOL))
ROW_DMOD_B, ROW_DMOD_C, ROW_LOSS = 6, 9, 12


def _epilogue(parts, landed, smalls, dmod4, dgate, loss_acc, w_mod_l):
    n_a, n_l, n_s = len(parts), len(landed), len(smalls)
    n_mod = w_mod_l.shape[1]
    blk = [p.shape[1:] for p in parts]
    small_out = [D_MODEL, R_Q, R_KV, D_HEAD, D_HEAD, D_POOL]

    def body(*refs):
        part_refs = refs[0:n_a]
        land_refs = refs[n_a:n_a + n_l]
        small_in = refs[n_a + n_l:n_a + n_l + n_s]
        dmod4_ref, dgate_ref, loss_ref, wmod_ref = refs[n_a + n_l + n_s:n_a + n_l + n_s + 4]
        outs = refs[n_a + n_l + n_s + 4:]
        red_refs = outs[0:n_a]
        landsum_refs = outs[n_a:n_a + n_l]
        ccg_ref = outs[n_a + n_l]
        sum_refs = outs[n_a + n_l + 1:n_a + n_l + 1 + n_s]
        gbmod_ref, dmy_ref, lossout_ref = outs[n_a + n_l + 1 + n_s:n_a + n_l + 4 + n_s]
        scr = outs[n_a + n_l + 4 + n_s:]
        recv1, own1, sum1b, recv2 = scr[0:n_a], scr[n_a:2 * n_a], scr[2 * n_a:3 * n_a], scr[3 * n_a:4 * n_a]
        small_ref, smallg_ref, tot_ref, cc_ref = scr[4 * n_a:4 * n_a + 4]
        land_vmem = scr[4 * n_a + 4:4 * n_a + 4 + n_l]
        ssem1, rsem1, lsem, ssem2, rsem2, ssem_s, rsem_s, ssem_cc, rsem_cc, land_sem = scr[4 * n_a + 4 + n_l:]
        x, y, c = _coords()
        me = (x, y, c)
        sibling = (x, y, 1 - c)

        small_ref[...] = jnp.zeros_like(small_ref)
        for ref, (row, width) in zip(small_in, SMALL_LAYOUT):
            small_ref[row:row + 1, 0:width] = ref[...]
        small_ref[ROW_DMOD_B:ROW_DMOD_B + 2, :] = dmod4_ref[0:2, :]
        small_ref[ROW_DMOD_B + 2:ROW_DMOD_B + 3, :] = dgate_ref[...]
        small_ref[ROW_DMOD_C:ROW_DMOD_C + 2, :] = dmod4_ref[2:4, :]
        small_ref[ROW_LOSS:ROW_LOSS + 1, 0:128] = loss_ref[0:1, :]
        smallg_ref[_lin(me)] = small_ref[...]
        s_recv, s_send = _gather_to_all(small_ref, smallg_ref, ssem_s, rsem_s)

        stage1, own_copies = [], []
        for a in range(n_a):
            for b in range(4):
                dev = 4 * (b >> 1) + 2 * (b & 1)
                stage1.append(pltpu.make_async_remote_copy(
                    src_ref=part_refs[a].at[dev + (1 - c)], dst_ref=recv1[a].at[b],
                    send_sem=ssem1.at[a, b], recv_sem=rsem1.at[a, b], device_id=sibling, device_id_type=MESH))
                own_copies.append(pltpu.make_async_copy(part_refs[a].at[dev + c], own1[a].at[b], lsem.at[a, b]))
                stage1[-1].start()
                own_copies[-1].start()
        land_copies = [pltpu.make_async_copy(land_refs[n], land_vmem[n], land_sem.at[n]) for n in range(n_l)]
        for cp in land_copies:
            cp.start()

        s_recv()
        tot = smallg_ref[0]
        for d in range(1, N_DEV):
            tot = tot + smallg_ref[d]
        tot_ref[...] = tot
        for ref, (row, _), width in zip(sum_refs, SMALL_LAYOUT, small_out):
            ref[...] = tot_ref[row:row + 1, 0:width]
        lossout_ref[...] = tot_ref[8:16, 0:128]
        lin = _lin(me)

        def my_columns(three_rows):
            v = jnp.concatenate(three_rows, axis=1)
            out = jnp.zeros((1, n_mod), F32)
            for d in range(N_DEV):
                out = out + jnp.where(lin == d, v[:, d * n_mod:(d + 1) * n_mod], 0.0)
            return out

        rows3 = lambda ref, r0: [ref[r0 + t:r0 + t + 1, :] for t in range(3)]
        dmodc = rows3(tot_ref, ROW_DMOD_C)
        gbmod_ref[...] = jnp.concatenate(rows3(tot_ref, ROW_DMOD_B), axis=1) + jnp.concatenate(dmodc, axis=1)
        dmy_ref[...] = jnp.zeros_like(dmy_ref)
        for b in range(N_DEV):
            dmy_ref[b:b + 1, :] = my_columns(rows3(smallg_ref.at[b], ROW_DMOD_B))
        dmy = my_columns(dmodc)
        dmy_ref[N_DEV:N_DEV + 1, :] = dmy
        part = _dot_nt(jnp.broadcast_to(dmy, (8, n_mod)).astype(BF16), wmod_ref[...].astype(BF16))

        cc_ref[...] = part
        ccg_ref[_lin(me)] = part
        cc_recv, cc_send = _gather_to_all(cc_ref, ccg_ref, ssem_cc, rsem_cc)

        stage2 = []
        for a in range(n_a):
            for b in range(4):
                stage1[4 * a + b].wait_recv()
                own_copies[4 * a + b].wait()
                sum1b[a][b] = (own1[a][b] + recv1[a][b]).astype(BF16)
            for k in (1, 2, 3):
                tx, ty = _flip(x, (k >> 1) & 1), _flip(y, k & 1)
                stage2.append(pltpu.make_async_remote_copy(
                    src_ref=sum1b[a].at[2 * tx + ty], dst_ref=recv2[a].at[k - 1], send_sem=ssem2.at[a, k - 1],
                    recv_sem=rsem2.at[a, k - 1], device_id=(tx, ty, c), device_id_type=MESH))
                stage2[-1].start()
        for a in range(n_a):
            own = own1[a][2 * x + y] + recv1[a][2 * x + y]
            for k in (1, 2, 3):
                stage2[3 * a + k - 1].wait_recv()
                own = own + recv2[a][k - 1].astype(F32)
            red_refs[a][...] = own

        for cp in land_copies:
            cp.wait()
        for ref, out in zip(land_vmem, landsum_refs):
            acc = ref[0].astype(F32)
            for d in range(1, N_DEV):
                acc = acc + ref[d].astype(F32)
            out[...] = acc
        cc_recv()
        for cp in stage1 + stage2:
            cp.wait_send()
        s_send()
        cc_send()

    vm = pl.BlockSpec(memory_space=pltpu.VMEM)
    sds = jax.ShapeDtypeStruct
    return pl.pallas_call(
        body, name="epilogue_reduce",
        out_shape=(*[sds(s, F32) for s in blk], *[sds(a.shape[1:], F32) for a in landed], sds((N_DEV, 8, D_MODEL), F32),
                   *[sds((1, w), F32) for w in small_out], sds((1, 3 * D_MODEL), F32), sds((16, n_mod), F32),
                   sds((8, 128), F32)),
        in_specs=[pl.BlockSpec(memory_space=pl.ANY)] * (n_a + n_l) + [vm] * (n_s + 4),
        out_specs=tuple([vm] * (n_a + n_l + n_s + 4)),
        scratch_shapes=[*[pltpu.VMEM((4,) + s, F32) for s in blk], *[pltpu.VMEM((4,) + s, F32) for s in blk],
                        *[pltpu.VMEM((4,) + s, BF16) for s in blk], *[pltpu.VMEM((3,) + s, BF16) for s in blk],
                        pltpu.VMEM((SMALL_ROWS, D_MODEL), F32), pltpu.VMEM((N_DEV, SMALL_ROWS, D_MODEL), F32),
                        pltpu.VMEM((SMALL_ROWS, D_MODEL), F32), pltpu.VMEM((8, D_MODEL), F32),
                        *[pltpu.VMEM(a.shape, a.dtype) for a in landed],
                        pltpu.SemaphoreType.DMA((n_a, 4)), pltpu.SemaphoreType.DMA((n_a, 4)), pltpu.SemaphoreType.DMA((n_a, 4)),
                        pltpu.SemaphoreType.DMA((n_a, 3)), pltpu.SemaphoreType.DMA((n_a, 3)),
                        pltpu.SemaphoreType.DMA((7,)), pltpu.SemaphoreType.DMA((7,)),
                        pltpu.SemaphoreType.DMA((7,)), pltpu.SemaphoreType.DMA((7,)), pltpu.SemaphoreType.DMA((n_l,))],
        compiler_params=pltpu.CompilerParams(vmem_limit_bytes=VMEM_LIMIT),
    )(*parts, *landed, *smalls, dmod4, dgate, loss_acc, w_mod_l)


def _adamw(weights, grads, ms, vs, c_all_t, dmod_my, p2g):
    n = len(weights)

    def body(*refs):
        w_refs = refs[0:n]
        g_in = refs[n:2 * n - 2]
        m_refs = refs[2 * n - 2:3 * n - 2]
        v_refs = refs[3 * n - 2:4 * n - 2]
        cat_ref, dmod_ref, p2g_ref = refs[4 * n - 2:4 * n + 1]
        outs = refs[4 * n + 1:]
        g_refs, d_refs, nm_refs, nv_refs = outs[0:n], outs[n:2 * n], outs[2 * n:3 * n], outs[3 * n:4 * n]
        gcc_ref, gwm_ref = g_refs[0], g_refs[1]

        part = p2g_ref[0, 0:1, :]
        for d in range(1, N_DEV):
            part = part + p2g_ref[d, 0:1, :]
        cc = w_refs[0][...]
        sg = _sigmoid(cc)
        gcc_ref[...] = part * (sg * (1.0 + cc * (1.0 - sg)))
        cat = cat_ref[...]
        gwm_ref[...] = lax.dot_general(cat * _sigmoid(cat), dmod_ref[...], (((1,), (0,)), ((), ())), precision=HIGHEST,
                                       preferred_element_type=F32)
        for idx in range(n):
            if idx >= 2:
                g_refs[idx][...] = g_in[idx - 2][...]
            g = g_refs[idx][...]
            m = ADAM_B1 * m_refs[idx][...] + (1.0 - ADAM_B1) * g
            v = ADAM_B2 * v_refs[idx][...] + (1.0 - ADAM_B2) * (g * g)
            m_hat = m / (1.0 - ADAM_B1 ** ADAM_STEP)
            v_hat = v / (1.0 - ADAM_B2 ** ADAM_STEP)
            d_refs[idx][...] = -ADAM_LR * (m_hat / (jnp.sqrt(v_hat) + ADAM_EPS) + ADAM_WD * w_refs[idx][...])
            nm_refs[idx][...] = m
            nv_refs[idx][...] = v

    vm = pl.BlockSpec(memory_space=pltpu.VMEM)
    shapes = [jax.ShapeDtypeStruct(w.shape, F32) for w in weights]
    args = list(weights) + list(grads[2:]) + list(ms) + list(vs) + [c_all_t, dmod_my, p2g]
    out_shape = tuple(shapes * 4)
    return pl.pallas_call(
        body, name="adamw_update", out_shape=out_shape, in_specs=[vm] * len(args), out_specs=tuple([vm] * len(out_shape)),
        compiler_params=pltpu.CompilerParams(vmem_limit_bytes=VMEM_LIMIT),
    )(*args)


def _rope_tables(seq):
    rows = seq // GRID_W
    row = np.repeat(np.arange(rows, dtype=np.float32), GRID_W)
    col = np.tile(np.arange(GRID_W, dtype=np.float32), rows)
    n_freq = D_ROPE // 4
    inv = (np.float32(ROPE_BASE) ** (-np.arange(n_freq, dtype=np.float32) / np.float32(n_freq))).astype(np.float32)
    ang_r = (row[:, None] * inv).astype(np.float32)
    ang_c = (col[:, None] * inv).astype(np.float32)
    ang = np.concatenate([ang_r, ang_r, ang_c, ang_c], axis=-1)
    cos, sin = np.cos(ang).astype(np.float32), np.sin(ang).astype(np.float32)
    low = (np.arange(D_ROPE) % 32) < 16

    def table(t, fill):
        out = np.full((TILE + seq, 128), fill, np.float32)
        out[TILE:, 0:D_ROPE] = t
        return jnp.asarray(out)

    return table(cos, 1.0), table(np.where(low, -sin, 0.0), 0.0), table(np.where(low, 0.0, sin), 0.0)


def kernel(x, c, ctx, c_ctx, w_mod, b_mod, norm_g, w_in, q_lora_g, w_uq, kv_lora_g, w_ukv, q_norm_g, k_norm_g, w_pool, pool_scale, w_out, loss_target, m_c_ctx, m_w_mod, m_b_mod, m_norm_g, m_w_in, m_q_lora_g, m_w_uq, m_kv_lora_g, m_w_ukv, m_q_norm_g, m_k_norm_g, m_w_pool, m_pool_scale, m_w_out, v_c_ctx, v_w_mod, v_b_mod, v_norm_g, v_w_in, v_q_lora_g, v_w_uq, v_kv_lora_g, v_w_ukv, v_q_norm_g, v_k_norm_g, v_w_pool, v_pool_scale, v_w_out):
    seq = x.shape[1]
    assert ctx.shape[1] == TILE and seq % TILE == 0 and seq % GRID_W == 0
    ix, iy, ic = lax.axis_index("x"), lax.axis_index("y"), lax.axis_index("c")
    me = 4 * ix + 2 * iy + ic
    x2, ctx2, tgt2 = x[0], ctx[0], loss_target[0]
    w_mod_l, w_ukv_l, w_out_l = w_mod[0], w_ukv[0], w_out[0]
    c_ctx_row = c_ctx.reshape(1, D_MODEL)

    tr = lambda a: jnp.transpose(a[0])
    w_in_tl, w_uq_tl = tr(w_in), tr(w_uq)
    cg, modg, wg_in, wg_uq, wg_ukv = _prologue(
        c, c_ctx_row, w_mod_l,
        [w_in_tl, w_uq_tl, w_ukv_l])
    mod_all = jnp.transpose(modg, (1, 0, 2)).reshape(16, 3 * D_MODEL)
    mod_b = lax.dynamic_slice_in_dim(mod_all, me, 1, axis=0).reshape(3, D_MODEL)
    mod_c = mod_all[8].reshape(3, D_MODEL)
    modrows = jnp.concatenate([mod_b, mod_c[0:2], jnp.zeros((3, D_MODEL), F32)], axis=0)
    b3 = b_mod.reshape(3, D_MODEL)
    bmodrows = jnp.concatenate([b3, b3[0:2], jnp.zeros((3, D_MODEL), F32)], axis=0)

    w_in_p = wg_in.reshape(D_IN_PROJ, D_MODEL)
    w_uq_p = jnp.concatenate([wg_uq.reshape(N_HEADS, D_HEAD, R_Q), jnp.zeros((N_HEADS, D_HEAD_PAD - D_HEAD, R_Q), BF16)],
                             axis=1).reshape(N_HEADS * D_HEAD_PAD, R_Q)
    w_ukv_f = jnp.transpose(wg_ukv, (1, 0, 2)).reshape(R_KV, N_HEADS * 256)
    qng_p = jnp.concatenate([q_norm_g, jnp.zeros((1, D_HEAD_PAD - D_HEAD), F32)], axis=1)
    kng_p = jnp.concatenate([k_norm_g, jnp.zeros((1, D_HEAD_PAD - D_HEAD), F32)], axis=1)
    cos, slo, shi = _rope_tables(seq)

    u_gates, u_mla, q, k, v = _inproj_fwd(x2, ctx2, modrows, bmodrows, norm_g, w_in_p, q_lora_g, w_uq_p, kv_lora_g, w_ukv_f,
                             qng_p, kng_p, cos, slo, shi)
    attn, lse, wg_out = _attn_fwd(q, k, v, min(1024, seq), w_out_l.astype(BF16))
    (loss_acc, g1, dgate, d_attn, dga, dgp, dpl, gwo_b, gwp, dps) = _mix(
        x2, tgt2, attn, u_gates, modrows, bmodrows, w_pool[0], pool_scale, wg_out.reshape(D_MODEL, D_MODEL))

    blocks = lambda a: a.reshape((N_DEV, a.shape[0] // N_DEV) + a.shape[1:])
    dq, dk, dv, land_wo, land_wp = _attn_bwd(q, k, v, attn, lse, d_attn.reshape(seq, D_ATTN), min(1024, seq), blocks(gwo_b),
                                             gwp.reshape(4 * GROUP_DIM, GROUP_DIM))
    (grad_x, gwin_t, gwuq_p, gwukv_t, dqlg, dkvlg, dqng, dkng, dng, dmod4) = _inproj_bwd(
        x2, ctx2, modrows, bmodrows, norm_g, w_in_p, q_lora_g, w_uq_p, kv_lora_g, w_ukv_f, qng_p, kng_p, cos, slo, shi,
        u_mla, dq, dk, dv, dga, dgp, dpl, g1)

    gwuq_t = gwuq_p.reshape(N_HEADS, D_HEAD_PAD, R_Q)[:, 0:D_HEAD].reshape(N_HEADS * D_HEAD, R_Q)
    parts = [blocks(gwin_t), blocks(gwuq_t), blocks(gwukv_t)]
    (r_win, r_wuq, r_wukv, r_wout, g_w_pool, ccg, g_ng, g_qlg, g_kvlg, g_qng, g_kng, g_ps, g_bmod, dmod_my,
     loss_rows) = _epilogue(parts, [land_wo, land_wp], [dng, dqlg, dkvlg, dqng, dkng, dps], dmod4, dgate, loss_acc, w_mod_l)

    g_w_ukv = jnp.transpose(r_wukv)
    c_rows = cg[:, 0, :]
    c_all_t = jnp.transpose(jnp.concatenate([c_rows, c_ctx_row, jnp.zeros((16 - N_DEV - 1, D_MODEL), F32)], axis=0))

    weights = [c_ctx_row, w_mod_l, b_mod, norm_g, w_in_tl, q_lora_g, w_uq_tl, kv_lora_g, w_ukv_l, q_norm_g, k_norm_g,
               w_pool.reshape(4 * GROUP_DIM, GROUP_DIM), pool_scale, w_out_l]
    grads = [None, None, g_bmod, g_ng, r_win, g_qlg, r_wuq, g_kvlg, g_w_ukv, g_qng, g_kng, g_w_pool, g_ps, r_wout]
    ms = [m_c_ctx.reshape(1, D_MODEL), m_w_mod[0], m_b_mod, m_norm_g, tr(m_w_in), m_q_lora_g, tr(m_w_uq), m_kv_lora_g,
          m_w_ukv[0], m_q_norm_g, m_k_norm_g, m_w_pool.reshape(4 * GROUP_DIM, GROUP_DIM), m_pool_scale, m_w_out[0]]
    vs = [v_c_ctx.reshape(1, D_MODEL), v_w_mod[0], v_b_mod, v_norm_g, tr(v_w_in), v_q_lora_g, tr(v_w_uq), v_kv_lora_g,
          v_w_ukv[0], v_q_norm_g, v_k_norm_g, v_w_pool.reshape(4 * GROUP_DIM, GROUP_DIM), v_pool_scale, v_w_out[0]]
    outs = _adamw(weights, grads, ms, vs, c_all_t, dmod_my, ccg)
    n = len(weights)
    grads, deltas, new_m, new_v = outs[0:n], outs[n:2 * n], outs[2 * n:3 * n], outs[3 * n:4 * n]

    loss = loss_rows[ROW_LOSS - 8, 0]
    final = [c_ctx.shape, w_mod.shape, b_mod.shape, norm_g.shape, w_in.shape, q_lora_g.shape, w_uq.shape, kv_lora_g.shape,
             w_ukv.shape, q_norm_g.shape, k_norm_g.shape, w_pool.shape, pool_scale.shape, w_out.shape]
    transposed = (4, 6)

    def shaped(arrs):
        return [(jnp.transpose(a) if i in transposed else a).reshape(s) for i, (a, s) in enumerate(zip(arrs, final))]

    return (loss, grad_x[None], *shaped(grads), *shaped(deltas), *shaped(new_m), *shaped(new_v))
```

```python
import numpy as np

import jax
import jax.numpy as jnp
from jax import lax
from jax.experimental import pallas as pl
from jax.experimental.pallas import tpu as pltpu

F32 = jnp.float32
BF16 = jnp.bfloat16
MESH = pl.DeviceIdType.MESH
HIGHEST = lax.Precision.HIGHEST

D_MODEL = 1024
N_HEADS = 4
D_NOPE = 128
D_ROPE = 64
D_HEAD = D_NOPE + D_ROPE
D_HEAD_PAD = 256
D_V = 128
R_Q = 256
R_KV = 128
D_ATTN = 512
D_POOL = 512
POOL_WINDOWS = (2, 4, 8, 16)
GROUP_DIM = 128
GRID_W = 64
ROPE_BASE = 10000.0
NORM_EPS = 1e-6
ATTN_SCALE = D_HEAD ** -0.5
LOG2_E = 1.4426950408889634
LN_2 = 0.6931471805599453
ATTN_ROWS = 256
D_IN_PROJ = 1984
U_PAD = 2048
O_GA, O_PIN, O_GP, O_CQ, O_CKV, O_KR = 0, 512, 1024, 1536, 1792, 1920
TILE = 256
MIX_TILE = 512
Q_BLOCK = 128
HALO = 16
N_GATES = 1536
N_MLA = D_IN_PROJ - N_GATES
N_DEV = 8
VMEM_LIMIT = 56 * 1024 * 1024

ADAM_LR = 0.001
ADAM_B1 = 0.9
ADAM_B2 = 0.999
ADAM_EPS = 1e-08
ADAM_WD = 0.01
ADAM_STEP = 10

SMALL_ROWS = 16


def _dot(a, b):
    return lax.dot_general(a, b, (((1,), (0,)), ((), ())), preferred_element_type=F32)


def _dot_nt(a, b):
    return lax.dot_general(a, b, (((1,), (1,)), ((), ())), preferred_element_type=F32)


def _dot_tn(a, b):
    return lax.dot_general(a, b, (((0,), (0,)), ((), ())), preferred_element_type=F32)


def _sigmoid(x):
    return 1.0 / (1.0 + jnp.exp(-x))


def _rope(p, cos, slo, shi):
    return p * cos + pltpu.roll(p, 112, 1) * slo + pltpu.roll(p, 16, 1) * shi


def _rope_t(d, cos, slo, shi):
    return d * cos + pltpu.roll(d * slo, 16, 1) + pltpu.roll(d * shi, 112, 1)


def _shift_rows(a, k):
    n = a.shape[0]
    return pltpu.roll(a, (-k) % n, 0)


def _window_sum(x, w, transposed):
    s = (x + _shift_rows(x, 1)) if transposed else (_shift_rows(x, -1) + x)
    step = 1
    while 2 * step < w:
        s = _shift_rows(s, -step) + _shift_rows(s, step)
        step *= 2
    return s


def _inv_count(tpos, w, seq):
    lo = jnp.maximum(tpos - w // 2, 0)
    hi = jnp.minimum(tpos - w // 2 + w, seq)
    return 1.0 / jnp.maximum(hi - lo, 1).astype(F32)


def _coords():
    return lax.axis_index("x"), lax.axis_index("y"), lax.axis_index("c")


def _flip(v, bit):
    return (1 - v) if bit else v


def _peer(k):
    x, y, c = _coords()
    return (_flip(x, (k >> 2) & 1), _flip(y, (k >> 1) & 1), _flip(c, k & 1))


def _lin(p):
    return 4 * p[0] + 2 * p[1] + p[2]


def _gather_to_all(src_ref, slots_ref, send_sems, recv_sems):
    me = _coords()
    copies = []
    for k in range(1, N_DEV):
        cp = pltpu.make_async_remote_copy(
            src_ref=src_ref, dst_ref=slots_ref.at[_lin(me)], send_sem=send_sems.at[k - 1], recv_sem=recv_sems.at[k - 1],
            device_id=_peer(k), device_id_type=MESH)
        cp.start()
        copies.append(cp)

    def wait_recv():
        for k in range(1, N_DEV):
            pltpu.make_async_remote_copy(
                src_ref=src_ref, dst_ref=slots_ref.at[_lin(_peer(k))], send_sem=send_sems.at[k - 1],
                recv_sem=recv_sems.at[k - 1], device_id=_peer(k), device_id_type=MESH).wait_recv()

    def wait_send():
        for cp in copies:
            cp.wait_send()

    return wait_recv, wait_send


def _prologue(c8, cctx8, w_mod_l, shards):
    n_mod = w_mod_l.shape[1]
    n_w = len(shards)

    def body(c_ref, cctx_ref, wmod_ref, *refs):
        w_refs = refs[0:n_w]
        cg_ref, modg_ref = refs[n_w], refs[n_w + 1]
        wg_refs = refs[n_w + 2:2 * n_w + 2]
        modblk_ref = refs[2 * n_w + 2]
        wb_refs = refs[2 * n_w + 3:3 * n_w + 3]
        c8_ref = refs[3 * n_w + 3]
        ssem_w, rsem_w, lsem_w, ssem_c, rsem_c, ssem_m, rsem_m = refs[3 * n_w + 4:]
        x, y, c = _coords()
        me = (x, y, c)
        sibling = (x, y, 1 - c)
        chips = [(1 - x, y), (x, 1 - y), (1 - x, 1 - y)]

        def wcopies(k, block, to, own=False):
            out = []
            for a in range(n_w):
                slot = wg_refs[a].at[_lin(block)]
                out.append(pltpu.make_async_remote_copy(
                    src_ref=wb_refs[a] if own else slot, dst_ref=slot, send_sem=ssem_w.at[a, k], recv_sem=rsem_w.at[a, k],
                    device_id=to, device_id_type=MESH))
            return out

        c8_ref[...] = jnp.broadcast_to(c_ref[...], (8, D_MODEL))
        cg_ref[_lin(me)] = c8_ref[...]
        c_recv, c_send = _gather_to_all(c8_ref, cg_ref, ssem_c, rsem_c)

        for a in range(n_w):
            wb_refs[a][...] = w_refs[a][...].astype(BF16)
        first = wcopies(0, me, sibling, own=True)
        for j, chip in enumerate(chips):
            first += wcopies(1 + j, me, (*chip, c), own=True)
        late = [idx for idx in range(len(first)) if idx % n_w == 0 and idx >= n_w]
        for idx, cp in enumerate(first):
            if idx not in late:
                cp.start()
        mine = [pltpu.make_async_copy(wb_refs[a], wg_refs[a].at[_lin(me)], lsem_w.at[a]) for a in range(n_w)]
        for cp in mine:
            cp.start()

        c_recv()
        row = lax.broadcasted_iota(jnp.int32, (8, D_MODEL), 0)
        c_all = jnp.zeros((8, D_MODEL), F32)
        for d in range(N_DEV):
            c_all = c_all + jnp.where(row == d, cg_ref[d], 0.0)
        cc = jnp.broadcast_to(cctx_ref[...], (8, D_MODEL))
        a = jnp.concatenate([c_all * _sigmoid(c_all), cc * _sigmoid(cc)], axis=0)
        modblk_ref[...] = _dot(a.astype(BF16), wmod_ref[...].astype(BF16))
        modg_ref[_lin(me)] = modblk_ref[...]
        m_recv, m_send = _gather_to_all(modblk_ref, modg_ref, ssem_m, rsem_m)
        for idx in late:
            first[idx].start()
        m_recv()

        passed = []
        for j, chip in enumerate(chips):
            for cp in wcopies(1 + j, (*chip, c), me):
                cp.wait_recv()
            fwd = wcopies(4 + j, (*chip, c), sibling)
            for cp in fwd:
                cp.start()
            passed += fwd
        for cp in wcopies(0, sibling, me):
            cp.wait_recv()
        for j, chip in enumerate(chips):
            for cp in wcopies(4 + j, (*chip, 1 - c), me):
                cp.wait_recv()
        for cp in first + passed:
            cp.wait_send()
        for cp in mine:
            cp.wait()
        c_send()
        m_send()

    vm = pl.BlockSpec(memory_space=pltpu.VMEM)
    hbm = pl.BlockSpec(memory_space=pl.ANY)
    return pl.pallas_call(
        body, name="prologue_gather",
        out_shape=(jax.ShapeDtypeStruct((N_DEV, 8, D_MODEL), F32), jax.ShapeDtypeStruct((N_DEV, 16, n_mod), F32),
                   *[jax.ShapeDtypeStruct((N_DEV,) + s.shape, BF16) for s in shards]),
        in_specs=[vm] * (3 + n_w), out_specs=(vm, vm, *[hbm] * n_w),
        scratch_shapes=[pltpu.VMEM((16, n_mod), F32), *[pltpu.VMEM(s.shape, BF16) for s in shards],
                        pltpu.VMEM((8, D_MODEL), F32),
                        pltpu.SemaphoreType.DMA((n_w, 7)), pltpu.SemaphoreType.DMA((n_w, 7)), pltpu.SemaphoreType.DMA((n_w,)),
                        pltpu.SemaphoreType.DMA((7,)), pltpu.SemaphoreType.DMA((7,)),
                        pltpu.SemaphoreType.DMA((7,)), pltpu.SemaphoreType.DMA((7,))],
        compiler_params=pltpu.CompilerParams(vmem_limit_bytes=VMEM_LIMIT),
    )(c8, cctx8, w_mod_l, *shards)


def _modulated_input(is_ctx, x_ref, ctx_ref, mod_ref, bmod_ref, ng_ref):
    xt = jnp.where(is_ctx, ctx_ref[...], x_ref[...])
    shift = jnp.where(is_ctx, mod_ref[3:4, :] + bmod_ref[3:4, :], mod_ref[0:1, :] + bmod_ref[0:1, :])
    scale = jnp.where(is_ctx, mod_ref[4:5, :] + bmod_ref[4:5, :], mod_ref[1:2, :] + bmod_ref[1:2, :])
    r = lax.rsqrt(jnp.mean(xt * xt, axis=-1, keepdims=True) + NORM_EPS)
    xg = (xt * r) * ng_ref[...]
    h = xg * (1.0 + scale) + shift
    return xt, r, xg, h, scale


def _inproj_fwd(x, ctx, modrows, bmodrows, norm_g, w_in_p, q_lora_g, w_uq_p, kv_lora_g, w_ukv, qng_p, kng_p, cos, slo, shi):
    seq = x.shape[0]
    n_tiles = seq // TILE + 1
    tot = seq + TILE

    n_mla = R_Q + R_KV + 128

    def body(x_ref, ctx_ref, mod_ref, bmod_ref, ng_ref, win_ref, qlg_ref, wuq_ref, kvlg_ref, wukv_ref, qng_ref, kng_ref,
             cos_ref, slo_ref, shi_ref, ug_ref, um_ref, q_ref, k_ref, v_ref, stash):
        s = pl.program_id(0)

        @pl.when(s == 0)
        def _():
            stash[...] = jnp.zeros_like(stash)

        refs = (x_ref, ctx_ref, mod_ref, bmod_ref, ng_ref, win_ref, qlg_ref, wuq_ref, kvlg_ref, wukv_ref, qng_ref, kng_ref,
                cos_ref, slo_ref, shi_ref, ug_ref, um_ref, q_ref, k_ref, v_ref)
        stages(s, refs, stash, stash)

    def stages(s, refs, fill, prev_ref):
        (x_ref, ctx_ref, mod_ref, bmod_ref, ng_ref, win_ref, qlg_ref, wuq_ref, kvlg_ref, wukv_ref, qng_ref, kng_ref,
         cos_ref, slo_ref, shi_ref, ug_ref, um_ref, q_ref, k_ref, v_ref) = refs
        cos_t, slo_t, shi_t = cos_ref[...], slo_ref[...], shi_ref[...]
        prev = prev_ref[...]
        cq = prev[:, 0:R_Q]
        qn = cq * lax.rsqrt(jnp.mean(cq * cq, axis=-1, keepdims=True) + NORM_EPS) * qlg_ref[...]
        q = _dot_nt(qn.astype(BF16), wuq_ref[...])
        qg = qng_ref[...] * (ATTN_SCALE * LOG2_E)
        for hd in range(N_HEADS):
            qh = q[:, hd * D_HEAD_PAD:(hd + 1) * D_HEAD_PAD]
            rr = lax.rsqrt(jnp.sum(qh * qh, axis=-1, keepdims=True) * (1.0 / D_HEAD) + NORM_EPS)
            qy = qh * rr * qg
            q_ref[hd, :, 0:D_NOPE] = qy[:, 0:D_NOPE].astype(BF16)
            q_ref[hd, :, D_NOPE:D_HEAD_PAD] = _rope(qy[:, D_NOPE:D_HEAD_PAD], cos_t, slo_t, shi_t).astype(BF16)

        ckv = prev[:, R_Q:R_Q + R_KV]
        kvn = ckv * lax.rsqrt(jnp.mean(ckv * ckv, axis=-1, keepdims=True) + NORM_EPS) * kvlg_ref[...]
        kv = _dot(kvn.astype(BF16), wukv_ref[...])
        krz = prev[:, R_Q + R_KV:n_mla]
        kr_ss = jnp.sum(krz * krz, axis=-1, keepdims=True)
        kg = kng_ref[...]
        for hd in range(N_HEADS):
            kn = kv[:, hd * 256:hd * 256 + D_NOPE]
            rr = lax.rsqrt((jnp.sum(kn * kn, axis=-1, keepdims=True) + kr_ss) * (1.0 / D_HEAD) + NORM_EPS)
            k_ref[hd, :, 0:D_NOPE] = (kn * rr * kg[:, 0:D_NOPE]).astype(BF16)
            k_ref[hd, :, D_NOPE:D_HEAD_PAD] = _rope(krz * rr * kg[:, D_NOPE:D_HEAD_PAD], cos_t, slo_t, shi_t).astype(BF16)
            v_ref[hd] = kv[:, hd * 256 + D_NOPE:(hd + 1) * 256].astype(BF16)

        _, _, _, h, _ = _modulated_input(s == 0, x_ref, ctx_ref, mod_ref, bmod_ref, ng_ref)
        hb = h.astype(BF16)
        ug_ref[...] = _dot_nt(hb, win_ref[N_MLA:D_IN_PROJ, :]).astype(BF16)
        um = _dot_nt(hb, win_ref[0:n_mla, :])
        lane = lax.broadcasted_iota(jnp.int32, (TILE, 128), 1)
        um = jnp.concatenate([um[:, 0:R_Q + R_KV], jnp.where(lane < D_ROPE, um[:, R_Q + R_KV:n_mla], 0.0)], axis=1)
        um_ref[...] = um
        fill[...] = um

    first = lambda s: jnp.minimum(s, n_tiles - 1)
    second = lambda s: jnp.maximum(s - 1, 0)
    latent = lambda t: jnp.maximum(t - 1, 0)
    full = lambda shape: pl.BlockSpec(shape, lambda s: (0,) * len(shape))
    rope_spec = pl.BlockSpec((TILE, 128), lambda s: (second(s), 0))
    return pl.pallas_call(
        body, name="inproj_fwd", grid=(n_tiles + 1,),
        out_shape=(jax.ShapeDtypeStruct((seq, N_GATES), BF16), jax.ShapeDtypeStruct((tot, n_mla), F32),
                   jax.ShapeDtypeStruct((N_HEADS, seq, D_HEAD_PAD), BF16),
                   jax.ShapeDtypeStruct((N_HEADS, tot, D_HEAD_PAD), BF16),
                   jax.ShapeDtypeStruct((N_HEADS, tot, D_V), BF16)),
        in_specs=[pl.BlockSpec((TILE, D_MODEL), lambda s: (latent(first(s)), 0)), full((TILE, D_MODEL)), full((8, D_MODEL)),
                  full((8, D_MODEL)), full((1, D_MODEL)), full((D_IN_PROJ, D_MODEL)), full((1, R_Q)),
                  full((N_HEADS * D_HEAD_PAD, R_Q)), full((1, R_KV)), full((R_KV, N_HEADS * 256)), full((1, D_HEAD_PAD)),
                  full((1, D_HEAD_PAD)), rope_spec, rope_spec, rope_spec],
        out_specs=(pl.BlockSpec((TILE, N_GATES), lambda s: (latent(first(s)), 0)),
                   pl.BlockSpec((TILE, n_mla), lambda s: (first(s), 0)),
                   pl.BlockSpec((N_HEADS, TILE, D_HEAD_PAD), lambda s: (0, latent(second(s)), 0)),
                   pl.BlockSpec((N_HEADS, TILE, D_HEAD_PAD), lambda s: (0, second(s), 0)),
                   pl.BlockSpec((N_HEADS, TILE, D_V), lambda s: (0, second(s), 0))),
        scratch_shapes=[pltpu.VMEM((TILE, n_mla), F32)],
        compiler_params=pltpu.CompilerParams(dimension_semantics=("arbitrary",), vmem_limit_bytes=VMEM_LIMIT),
    )(x, ctx, modrows, bmodrows, norm_g, w_in_p, q_lora_g, w_uq_p, kv_lora_g, w_ukv, qng_p, kng_p, cos, slo, shi)


def _hosted_exchange(first, last, items, send_sems, recv_sems, local_sems):
    me = _lin(_coords())

    def remote(n, k, src, land, scatter):
        peer = _peer(k)
        return pltpu.make_async_remote_copy(
            src_ref=src.at[_lin(peer)] if scatter else src, dst_ref=land.at[me], send_sem=send_sems.at[n, k - 1],
            recv_sem=recv_sems.at[n, k - 1], device_id=peer, device_id_type=MESH)

    def local(n, src, land, scatter):
        return pltpu.make_async_copy(src.at[me] if scatter else src, land.at[me], local_sems.at[n])

    @pl.when(first)
    def _():
        for n, (src, land, scatter) in enumerate(items):
            for k in range(1, N_DEV):
                remote(n, k, src, land, scatter).start()
            local(n, src, land, scatter).start()

    @pl.when(last)
    def _():
        for n, (src, land, scatter) in enumerate(items):
            for k in range(1, N_DEV):
                peer = _peer(k)
                pltpu.make_async_remote_copy(
                    src_ref=src.at[0] if scatter else src, dst_ref=land.at[_lin(peer)], send_sem=send_sems.at[n, k - 1],
                    recv_sem=recv_sems.at[n, k - 1], device_id=peer, device_id_type=MESH).wait_recv()
            for k in range(1, N_DEV):
                remote(n, k, src, land, scatter).wait_send()
            local(n, src, land, scatter).wait()


def _attn_fwd(q, k, v, block_q, w_out_b):
    _, seq, _ = q.shape
    n_keys = k.shape[1]
    n_q = seq // block_q

    def body(q_ref, k_ref, v_ref, wo_ref, o_ref, lse_ref, wog_ref, ssem, rsem, lsem):
        h, i = pl.program_id(0), pl.program_id(1)
        _hosted_exchange((h == 0) & (i == 0), (h == N_HEADS - 1) & (i == n_q - 1), [(wo_ref, wog_ref, False)],
                         ssem, rsem, lsem)
        kb, vb = k_ref[0], v_ref[0]
        for r0 in range(0, block_q, ATTN_ROWS):
            rows = slice(r0, r0 + ATTN_ROWS)
            s = _dot_nt(q_ref[0, rows, :], kb)
            m = jnp.max(s, axis=-1, keepdims=True)
            p = jnp.exp2(s - m)
            l = jnp.sum(p, axis=-1, keepdims=True)
            o_ref[rows, :] = _dot(p.astype(BF16), vb) * (1.0 / l)
            lse_ref[0, rows, :] = m + jnp.log2(l)

    hbm = pl.BlockSpec(memory_space=pl.ANY)
    return pl.pallas_call(
        body, name="attn_fwd", grid=(N_HEADS, n_q),
        out_shape=(jax.ShapeDtypeStruct((seq, D_ATTN), F32), jax.ShapeDtypeStruct((N_HEADS, seq, 1), F32),
                   jax.ShapeDtypeStruct((N_DEV,) + w_out_b.shape, w_out_b.dtype)),
        in_specs=[pl.BlockSpec((1, block_q, D_HEAD_PAD), lambda h, i: (h, i, 0)),
                  pl.BlockSpec((1, n_keys, D_HEAD_PAD), lambda h, i: (h, 0, 0)),
                  pl.BlockSpec((1, n_keys, D_V), lambda h, i: (h, 0, 0)), hbm],
        out_specs=(pl.BlockSpec((block_q, D_V), lambda h, i: (i, h)),
                   pl.BlockSpec((1, block_q, 1), lambda h, i: (h, i, 0)), hbm),
        scratch_shapes=[pltpu.SemaphoreType.DMA((1, 7)), pltpu.SemaphoreType.DMA((1, 7)), pltpu.SemaphoreType.DMA((1,))],
        compiler_params=pltpu.CompilerParams(dimension_semantics=("arbitrary", "arbitrary"), vmem_limit_bytes=VMEM_LIMIT),
    )(q, k, v, w_out_b)


def _attn_bwd(q, k, v, o, lse, d_o, block_q, gwo_blocks, gwp):
    _, seq, _ = q.shape
    n_keys = k.shape[1]
    n_q = seq // block_q

    def body(q_ref, k_ref, v_ref, o_ref, lse_ref, do_ref, gwo_ref, gwp_ref, dq_ref, dkt_ref, dvt_ref, lwo_ref, lwp_ref,
             ssem, rsem, lsem):
        h, i = pl.program_id(0), pl.program_id(1)
        _hosted_exchange((h == 0) & (i == 0), (h == N_HEADS - 1) & (i == n_q - 1),
                         [(gwo_ref, lwo_ref, True), (gwp_ref, lwp_ref, False)], ssem, rsem, lsem)

        @pl.when(i == 0)
        def _():
            dkt_ref[...] = jnp.zeros_like(dkt_ref)
            dvt_ref[...] = jnp.zeros_like(dvt_ref)

        kb, vb = k_ref[0], v_ref[0]
        raws, ps = [], []
        for r0 in range(0, block_q, ATTN_ROWS):
            rows = slice(r0, r0 + ATTN_ROWS)
            d_out = do_ref[rows, :]
            p = jnp.exp2(_dot_nt(q_ref[0, rows, :], kb) - lse_ref[0, rows, :])
            dp = _dot_nt(d_out.astype(BF16), vb)
            delta = jnp.sum(d_out * o_ref[rows, :], axis=-1, keepdims=True)
            raw = (p * (dp - delta)).astype(BF16)
            dq_ref[0, rows, :] = _dot(raw, kb)
            raws.append(raw)
            ps.append(p.astype(BF16))
        dkt_ref[0] += _dot_tn(q_ref[0], jnp.concatenate(raws, axis=0))
        dvt_ref[0] += _dot_tn(do_ref[...].astype(BF16), jnp.concatenate(ps, axis=0))

    hbm = pl.BlockSpec(memory_space=pl.ANY)
    return pl.pallas_call(
        body, name="attn_bwd", grid=(N_HEADS, n_q),
        out_shape=(jax.ShapeDtypeStruct((N_HEADS, seq, D_HEAD_PAD), F32),
                   jax.ShapeDtypeStruct((N_HEADS, D_HEAD_PAD, n_keys), F32),
                   jax.ShapeDtypeStruct((N_HEADS, D_V, n_keys), F32),
                   jax.ShapeDtypeStruct(gwo_blocks.shape, gwo_blocks.dtype),
                   jax.ShapeDtypeStruct((N_DEV,) + gwp.shape, gwp.dtype)),
        in_specs=[pl.BlockSpec((1, block_q, D_HEAD_PAD), lambda h, i: (h, i, 0)),
                  pl.BlockSpec((1, n_keys, D_HEAD_PAD), lambda h, i: (h, 0, 0)),
                  pl.BlockSpec((1, n_keys, D_V), lambda h, i: (h, 0, 0)),
                  pl.BlockSpec((block_q, D_V), lambda h, i: (i, h)),
                  pl.BlockSpec((1, block_q, 1), lambda h, i: (h, i, 0)),
                  pl.BlockSpec((block_q, D_V), lambda h, i: (i, h)), hbm, hbm],
        out_specs=(pl.BlockSpec((1, block_q, D_HEAD_PAD), lambda h, i: (h, i, 0)),
                   pl.BlockSpec((1, D_HEAD_PAD, n_keys), lambda h, i: (h, 0, 0)),
                   pl.BlockSpec((1, D_V, n_keys), lambda h, i: (h, 0, 0)), hbm, hbm),
        scratch_shapes=[pltpu.SemaphoreType.DMA((2, 7)), pltpu.SemaphoreType.DMA((2, 7)), pltpu.SemaphoreType.DMA((2,))],
        compiler_params=pltpu.CompilerParams(dimension_semantics=("arbitrary", "arbitrary"), vmem_limit_bytes=VMEM_LIMIT),
    )(q, k, v, o, lse, d_o, gwo_blocks, gwp)


def _mix(x, target, attn, u, modrows, bmodrows, w_pool, pool_scale, w_out):
    seq = x.shape[0]
    rows = min(MIX_TILE, seq // 2)
    n_tiles = seq // rows
    rows8 = rows // HALO
    last8 = seq // HALO - 1

    n_blk = seq // Q_BLOCK
    per = rows // n_blk
    assert rows % n_blk == 0 and per % 8 == 0 and n_tiles >= 2
    n_stash = 8

    def body(x_ref, t_ref, o_hbm, ga_ref, pin_ref, hb_ref, ha_ref, gp_ref, mod_ref, bmod_ref, wp_ref, ps_ref, wo_ref,
             loss_ref, g1_ref, dgate_ref, do_hbm, dga_ref, dgp_ref, dpl_ref, gwob_ref, gwp_ref, dps_ref,
             abuf, dbuf, gwo_ref, *rest):
        stash_even, stash_odd = rest[0:n_stash], rest[n_stash:2 * n_stash]
        rsem, wsem = rest[2 * n_stash:]
        s = pl.program_id(0)

        def slab_copies(tile, slot, fetch):
            window = pl.ds(tile * per, per)
            if fetch:
                return [pltpu.make_async_copy(o_hbm.at[:, window, :], abuf.at[slot], rsem.at[slot])]
            return [pltpu.make_async_copy(dbuf.at[slot], do_hbm.at[:, window, :], wsem.at[slot])]

        def wait_all(slot, fetch):
            slab_copies(0, slot, fetch)[0].wait()

        a_slot = lax.rem(s, 2)
        b_slot = 1 - a_slot

        @pl.when(s == 0)
        def _():
            loss_ref[...] = jnp.zeros_like(loss_ref)
            dgate_ref[...] = jnp.zeros_like(dgate_ref)
            gwo_ref[...] = jnp.zeros_like(gwo_ref)
            gwp_ref[...] = jnp.zeros_like(gwp_ref)
            dps_ref[...] = jnp.zeros_like(dps_ref)
            for cp in slab_copies(0, 0, True):
                cp.start()

        @pl.when(s + 1 < n_tiles)
        def _():
            for cp in slab_copies(s + 1, b_slot, True):
                cp.start(priority=1)

        @pl.when(s < n_tiles)
        def _():
            wait_all(a_slot, True)

        @pl.when(s >= 3)
        def _():
            wait_all(b_slot, False)

        gate = mod_ref[2:3, :] + bmod_ref[2:3, :]
        ps = ps_ref[...]

        def a_vector(slot):
            attn_t = jnp.swapaxes(abuf[slot], 0, 1).reshape(rows, D_ATTN)
            ga = ga_ref[...].astype(F32)
            sga = _sigmoid(ga)
            silu_ga = ga * sga
            pin = pin_ref[...].astype(F32)
            xs = jnp.concatenate([jnp.where(s > 0, hb_ref[...].astype(F32), 0.0), pin,
                                  jnp.where(s < n_tiles - 1, ha_ref[...].astype(F32), 0.0)], axis=0)
            tpos = s * rows + lax.broadcasted_iota(jnp.int32, (rows, 1), 0)
            pooled = []
            for g, w in enumerate(POOL_WINDOWS):
                sl = slice(g * GROUP_DIM, (g + 1) * GROUP_DIM)
                ws = _window_sum(xs[:, sl], w, False)[HALO:HALO + rows]
                pooled.append((ws * _inv_count(tpos, w, seq) - pin[:, sl]).astype(BF16))
            return attn_t, ga, sga, silu_ga, pooled

        def a_group_maps(pooled):
            return jnp.concatenate([_dot(pooled[g], wp_ref[g].astype(BF16)) for g in range(len(POOL_WINDOWS))], axis=1)

        def a_project(stash, yp, attn_t, ga, sga, silu_ga, pooled):
            zb_ref, _, fa_ref, sa_ref, fp_ref, sp_ref, yp_ref, pooled_ref = stash
            ypool = yp * ps
            gp = gp_ref[...].astype(F32)
            sgp = _sigmoid(gp)
            silu_gp = gp * sgp
            z = jnp.concatenate([silu_ga * attn_t, silu_gp * ypool], axis=1).astype(BF16)
            y = _dot(z, wo_ref[...])
            zb_ref[...] = z
            fa_ref[...] = attn_t * (sga * (1.0 + ga * (1.0 - sga)))
            sa_ref[...] = silu_ga
            fp_ref[...] = ypool * (sgp * (1.0 + gp * (1.0 - sgp)))
            sp_ref[...] = silu_gp
            yp_ref[...] = yp
            pooled_ref[...] = jnp.concatenate(pooled, axis=1)
            return y

        def a_loss(stash, y):
            res = x_ref[...] + gate * y - t_ref[...]
            loss_ref[...] += jnp.sum(res * res) * (0.5 / D_MODEL)
            dxn = res * (1.0 / D_MODEL)
            g1_ref[...] = dxn
            dgate_ref[...] += jnp.sum(dxn * y, axis=0, keepdims=True)
            stash[1][...] = (gate * dxn).astype(BF16)

        def b_dz(stash):
            return _dot_nt(stash[1][...], wo_ref[...])

        def b_gwo(stash):
            gwo_ref[...] += _dot_tn(stash[0][...], stash[1][...])

        def b_vector(stash, slot, dz):
            _, _, fa_ref, sa_ref, fp_ref, sp_ref, yp_ref, _ = stash
            dbra = dz[:, 0:D_ATTN]
            dbrp = dz[:, D_ATTN:]
            dga_ref[...] = (dbra * fa_ref[...]).astype(BF16)
            dbuf[slot] = jnp.swapaxes((dbra * sa_ref[...]).reshape(per, n_blk, D_ATTN), 0, 1)
            dgp_ref[...] = (dbrp * fp_ref[...]).astype(BF16)
            dyp_s = dbrp * sp_ref[...]
            dps_ref[...] += jnp.sum(dyp_s * yp_ref[...], axis=0, keepdims=True)
            return (dyp_s * ps).astype(BF16)

        def b_small(stash, dyp):
            pooled_ref = stash[7]
            for g in range(len(POOL_WINDOWS)):
                sl = slice(g * GROUP_DIM, (g + 1) * GROUP_DIM)
                gwp_ref[g] += _dot_tn(pooled_ref[:, sl], dyp[:, sl])
                dpl_ref[:, sl] = _dot_nt(dyp[:, sl], wp_ref[g].astype(BF16)).astype(BF16)

        def both(a_stash, b_stash, slot_a):
            dz = b_dz(b_stash)
            front = a_vector(slot_a)
            yp = a_group_maps(front[-1])
            b_gwo(b_stash)
            y = a_project(a_stash, yp, *front)
            dyp = b_vector(b_stash, 1 - slot_a, dz)
            a_loss(a_stash, y)
            b_small(b_stash, dyp)

        @pl.when(s == 0)
        def _():
            front = a_vector(0)
            a_loss(stash_even, a_project(stash_even, a_group_maps(front[-1]), *front))

        @pl.when((s > 0) & (s < n_tiles) & (a_slot == 0))
        def _():
            both(stash_even, stash_odd, 0)

        @pl.when((s > 0) & (s < n_tiles) & (a_slot == 1))
        def _():
            both(stash_odd, stash_even, 1)

        @pl.when(s == n_tiles)
        def _():
            last = (n_tiles - 1) % 2
            stash = stash_odd if last else stash_even
            dz = b_dz(stash)
            b_gwo(stash)
            b_small(stash, b_vector(stash, last, dz))
            gwob_ref[...] = gwo_ref[...].astype(BF16)

        @pl.when(s >= 1)
        def _():
            for cp in slab_copies(s - 1, b_slot, False):
                cp.start()

        @pl.when(s == n_tiles)
        def _():
            wait_all(b_slot, False)
            wait_all(a_slot, False)

    ta = lambda s: jnp.minimum(s, n_tiles - 1)
    tb = lambda s: jnp.maximum(s - 1, 0)
    tile = lambda w: pl.BlockSpec((rows, w), lambda s: (ta(s), 0))
    tile_b = lambda w: pl.BlockSpec((rows, w), lambda s: (tb(s), 0))
    ucol = lambda col: pl.BlockSpec((rows, 512), lambda s: (ta(s), col))
    full = lambda shape: pl.BlockSpec(shape, lambda s: (0,) * len(shape))
    stash_shapes = [pltpu.VMEM((rows, D_MODEL), BF16), pltpu.VMEM((rows, D_MODEL), BF16)] + \
        [pltpu.VMEM((rows, 512), F32)] * 5 + [pltpu.VMEM((rows, D_POOL), BF16)]
    return pl.pallas_call(
        body, name="mix_fwd_bwd", grid=(n_tiles + 1,),
        out_shape=(jax.ShapeDtypeStruct((8, 128), F32), jax.ShapeDtypeStruct((seq, D_MODEL), F32),
                   jax.ShapeDtypeStruct((1, D_MODEL), F32), jax.ShapeDtypeStruct((n_blk, Q_BLOCK, D_ATTN), F32),
                   jax.ShapeDtypeStruct((seq, D_ATTN), BF16), jax.ShapeDtypeStruct((seq, D_POOL), BF16),
                   jax.ShapeDtypeStruct((seq, D_POOL), BF16), jax.ShapeDtypeStruct((D_MODEL, D_MODEL), BF16),
                   jax.ShapeDtypeStruct((4, GROUP_DIM, GROUP_DIM), F32), jax.ShapeDtypeStruct((1, D_POOL), F32)),
        in_specs=[tile(D_MODEL), tile(D_MODEL), pl.BlockSpec(memory_space=pl.ANY), ucol(0), ucol(1),
                  pl.BlockSpec((HALO, 512), lambda s: (jnp.maximum(ta(s) * rows8 - 1, 0), 1)),
                  pl.BlockSpec((HALO, 512), lambda s: (jnp.minimum((ta(s) + 1) * rows8, last8), 1)),
                  ucol(2), full((8, D_MODEL)), full((8, D_MODEL)), full((4, GROUP_DIM, GROUP_DIM)), full((1, D_POOL)),
                  full((D_MODEL, D_MODEL))],
        out_specs=(full((8, 128)), tile(D_MODEL), full((1, D_MODEL)), pl.BlockSpec(memory_space=pl.ANY), tile_b(D_ATTN),
                   tile_b(D_POOL), tile_b(D_POOL), full((D_MODEL, D_MODEL)), full((4, GROUP_DIM, GROUP_DIM)),
                   full((1, D_POOL))),
        scratch_shapes=[pltpu.VMEM((2, n_blk, per, D_ATTN), F32), pltpu.VMEM((2, n_blk, per, D_ATTN), F32),
                        pltpu.VMEM((D_MODEL, D_MODEL), F32), *stash_shapes, *stash_shapes,
                        pltpu.SemaphoreType.DMA((2,)), pltpu.SemaphoreType.DMA((2,))],
        compiler_params=pltpu.CompilerParams(dimension_semantics=("arbitrary",), vmem_limit_bytes=VMEM_LIMIT),
    )(x, target, attn.reshape(n_blk, Q_BLOCK, D_ATTN), u, u, u, u, u, modrows, bmodrows, w_pool, pool_scale, w_out)


def _inproj_bwd(x, ctx, modrows, bmodrows, norm_g, w_in_p, q_lora_g, w_uq_p, kv_lora_g, w_ukv, qng_p, kng_p, cos, slo, shi,
                u, dq, dk, dv, dga, dgp, dpl, g1):
    seq = x.shape[0]
    n_tiles = seq // TILE + 1
    rows8 = TILE // HALO
    last8 = seq // HALO - 1

    def body(*refs):
        du_even, du_odd, dh_even, dh_odd = refs[-4:]
        gwin_ref, gwuq_ref, gwukv_ref, dqlg_ref, dkvlg_ref, dqng_ref, dkng_ref, dng_ref, dmod_ref = refs[-13:-4]
        s = pl.program_id(0)
        even = lax.rem(s, 2) == 0

        @pl.when(s == 0)
        def _():
            for ref in (gwin_ref, gwuq_ref, gwukv_ref, dqlg_ref, dkvlg_ref, dqng_ref, dkng_ref, dng_ref, dmod_ref,
                        du_odd, dh_even):
                ref[...] = jnp.zeros_like(ref)

        @pl.when((s < n_tiles) & even)
        def _():
            stages(s, refs[:-4], du_even, du_odd, dh_odd, dh_even, (True, True, True))

        @pl.when((s < n_tiles) & jnp.logical_not(even))
        def _():
            stages(s, refs[:-4], du_odd, du_even, dh_even, dh_odd, (True, True, True))

        for tail, run in ((n_tiles, (False, True, True)), (n_tiles + 1, (False, False, True))):
            @pl.when(s == tail)
            def _():
                if tail % 2 == 0:
                    stages(s, refs[:-4], du_even, du_odd, dh_odd, dh_even, run)
                else:
                    stages(s, refs[:-4], du_odd, du_even, dh_even, dh_odd, run)

    def stages(s, refs, du_ref, du_prev, dh_fill, dh_prev, run):
        (xb_ref, xc_ref, ctx_ref, mod_ref, bmod_ref, ng_ref, win_ref, qlg_ref, wuq_ref, kvlg_ref, wukv_ref, qng_ref, kng_ref,
         cos_ref, slo_ref, shi_ref, cq_ref, ckv_ref, kr_ref, dq_ref, dk_ref, dv_ref, dga_ref, dgp_ref, dpl_ref,
         hb_ref, ha_ref, g1_ref,
         gx_ref, gwin_ref, gwuq_ref, gwukv_ref, dqlg_ref, dkvlg_ref, dqng_ref, dkng_ref, dng_ref, dmod_ref) = refs


        i = s
        is_lat = s > 0
        if run[0]:
            cq = cq_ref[...]
            rq = lax.rsqrt(jnp.mean(cq * cq, axis=-1, keepdims=True) + NORM_EPS)
            qhat = cq * rq
            qnb = (qhat * qlg_ref[...]).astype(BF16)
            q = _dot_nt(qnb, wuq_ref[...])
            ckv = ckv_ref[...]
            rkv = lax.rsqrt(jnp.mean(ckv * ckv, axis=-1, keepdims=True) + NORM_EPS)
            kvhat = ckv * rkv
            kvnb = (kvhat * kvlg_ref[...]).astype(BF16)
            kv = _dot(kvnb, wukv_ref[...])

        if run[1]:
            _, _, _, h2, _ = _modulated_input(s <= 1, xb_ref, ctx_ref, mod_ref, bmod_ref, ng_ref)
            dub = du_prev[...]
            du_g, du_m = dub[:, 0:N_GATES], dub[:, N_GATES:U_PAD]
            h2b = h2.astype(BF16)
            dh_fill[...] = _dot(du_g, win_ref[N_MLA:D_IN_PROJ, :]) + _dot(du_m, win_ref[0:U_PAD - N_GATES, :])
            gwin_ref[N_MLA:D_IN_PROJ, :] += _dot_tn(du_g, h2b)
            gwin_ref[0:N_MLA, :] += _dot_tn(du_m, h2b)[0:N_MLA]

        if run[2]:
            ctx3 = s <= 2
            xt, r, xg, _, scale = _modulated_input(ctx3, xc_ref, ctx_ref, mod_ref, bmod_ref, ng_ref)
            dh = dh_prev[...]
            dshift = jnp.sum(dh, axis=0, keepdims=True)
            dscale = jnp.sum(dh * xg, axis=0, keepdims=True)
            zero = jnp.zeros_like(dshift)
            dmod_ref[0:1, :] += jnp.where(ctx3, zero, dshift)
            dmod_ref[1:2, :] += jnp.where(ctx3, zero, dscale)
            dmod_ref[2:3, :] += jnp.where(ctx3, dshift, zero)
            dmod_ref[3:4, :] += jnp.where(ctx3, dscale, zero)
            dxg = dh * (1.0 + scale)
            xr = xt * r
            dng_ref[...] += jnp.sum(dxg * xr, axis=0, keepdims=True)
            dxn = dxg * ng_ref[...]
            gx_ref[...] = g1_ref[...] + r * (dxn - xr * jnp.mean(dxn * xr, axis=-1, keepdims=True))

        if not run[0]:
            return

        cos_t, slo_t, shi_t = cos_ref[...], slo_ref[...], shi_ref[...]
        qg = qng_ref[...]
        dq_heads = []
        dqng = jnp.zeros((1, D_HEAD_PAD), F32)
        for hd in range(N_HEADS):
            qh = q[:, hd * D_HEAD_PAD:(hd + 1) * D_HEAD_PAD]
            rr = lax.rsqrt(jnp.sum(qh * qh, axis=-1, keepdims=True) * (1.0 / D_HEAD) + NORM_EPS)
            yq = qh * rr
            dpost = jnp.where(is_lat, dq_ref[hd] * ATTN_SCALE, 0.0)
            dyn = jnp.concatenate([dpost[:, 0:D_NOPE], _rope_t(dpost[:, D_NOPE:], cos_t, slo_t, shi_t)], axis=1)
            dqng = dqng + jnp.sum(dyn * yq, axis=0, keepdims=True)
            dyg = dyn * qg
            dq_heads.append(rr * (dyg - yq * (jnp.sum(dyg * yq, axis=-1, keepdims=True) * (1.0 / D_HEAD))))
        dqng_ref[...] += dqng
        dqf = jnp.concatenate(dq_heads, axis=1).astype(BF16)

        krz = kr_ref[...]
        kr_ss = jnp.sum(krz * krz, axis=-1, keepdims=True)
        kg = kng_ref[...]
        kg_n, kg_r = kg[:, 0:D_NOPE], kg[:, D_NOPE:]
        dkv_parts = []
        dkr = jnp.zeros((TILE, 128), F32)
        dkng_n = jnp.zeros((1, D_NOPE), F32)
        dkng_r = jnp.zeros((1, 128), F32)
        for hd in range(N_HEADS):
            kn = kv[:, hd * 256:hd * 256 + D_NOPE]
            rr = lax.rsqrt((jnp.sum(kn * kn, axis=-1, keepdims=True) + kr_ss) * (1.0 / D_HEAD) + NORM_EPS)
            yn, yr = kn * rr, krz * rr
            dk_h = jnp.transpose(dk_ref[hd]) * LN_2
            dpn = dk_h[:, 0:D_NOPE]
            dpr = _rope_t(dk_h[:, D_NOPE:], cos_t, slo_t, shi_t)
            dkng_n = dkng_n + jnp.sum(dpn * yn, axis=0, keepdims=True)
            dkng_r = dkng_r + jnp.sum(dpr * yr, axis=0, keepdims=True)
            dyn, dyr = dpn * kg_n, dpr * kg_r
            proj = (jnp.sum(dyn * yn, axis=-1, keepdims=True) + jnp.sum(dyr * yr, axis=-1, keepdims=True)) * (1.0 / D_HEAD)
            dkv_parts.append(rr * (dyn - yn * proj))
            dkv_parts.append(jnp.transpose(dv_ref[hd]))
            dkr = dkr + rr * (dyr - yr * proj)
        dkng_ref[:, 0:D_NOPE] += dkng_n
        dkng_ref[:, D_NOPE:] += dkng_r
        dkvf = jnp.concatenate(dkv_parts, axis=1).astype(BF16)

        lat_t = jnp.maximum(i - 1, 0)
        dpl_t = dpl_ref[...].astype(F32)
        ds = jnp.concatenate([jnp.where(i > 1, hb_ref[...].astype(F32), 0.0), dpl_t,
                              jnp.where(i < n_tiles - 1, ha_ref[...].astype(F32), 0.0)], axis=0)
        tpos = lat_t * TILE - HALO + lax.broadcasted_iota(jnp.int32, (TILE + 2 * HALO, 1), 0)
        for g, w in enumerate(POOL_WINDOWS):
            sl = slice(g * GROUP_DIM, (g + 1) * GROUP_DIM)
            wsum = _window_sum(ds[:, sl] * _inv_count(tpos, w, seq), w, True)[HALO:HALO + TILE]
            du_ref[:, O_PIN + g * GROUP_DIM:O_PIN + (g + 1) * GROUP_DIM] = jnp.where(
                is_lat, wsum - dpl_t[:, sl], 0.0).astype(BF16)

        du_ref[:, O_GA:O_GA + D_ATTN] = jnp.where(is_lat, dga_ref[...], jnp.zeros((), BF16))
        du_ref[:, O_GP:O_GP + D_POOL] = jnp.where(is_lat, dgp_ref[...], jnp.zeros((), BF16))
        du_ref[:, O_KR:U_PAD] = dkr.astype(BF16)

        gwuq_ref[...] += _dot_tn(dqf, qnb)
        dqn = _dot(dqf, wuq_ref[...])
        gwukv_ref[...] += _dot_tn(dkvf, kvnb)
        dkvn = _dot_nt(dkvf, wukv_ref[...])
        dqlg_ref[...] += jnp.sum(dqn * qhat, axis=0, keepdims=True)
        dqhat = dqn * qlg_ref[...]
        dcq = rq * (dqhat - qhat * jnp.mean(dqhat * qhat, axis=-1, keepdims=True))
        dkvlg_ref[...] += jnp.sum(dkvn * kvhat, axis=0, keepdims=True)
        dkvhat = dkvn * kvlg_ref[...]
        dckv = rkv * (dkvhat - kvhat * jnp.mean(dkvhat * kvhat, axis=-1, keepdims=True))
        du_ref[:, O_CQ:O_CQ + R_Q] = dcq.astype(BF16)
        du_ref[:, O_CKV:O_CKV + R_KV] = dckv.astype(BF16)

    t1 = lambda s: jnp.minimum(s, n_tiles - 1)
    t2 = lambda s: jnp.clip(s - 1, 0, n_tiles - 1)
    t3 = lambda s: jnp.clip(s - 2, 0, n_tiles - 1)
    latent = lambda t: jnp.maximum(t - 1, 0)
    lat = lambda s: (latent(t1(s)), 0)
    lat3 = lambda s: (latent(t3(s)), 0)
    full = lambda shape: pl.BlockSpec(shape, lambda s: (0,) * len(shape))
    rope_spec = pl.BlockSpec((TILE, 128), lambda s: (t1(s), 0))
    return pl.pallas_call(
        body, name="inproj_bwd", grid=(n_tiles + 2,),
        out_shape=(jax.ShapeDtypeStruct((seq, D_MODEL), F32), jax.ShapeDtypeStruct((D_IN_PROJ, D_MODEL), F32),
                   jax.ShapeDtypeStruct((N_HEADS * D_HEAD_PAD, R_Q), F32), jax.ShapeDtypeStruct((N_HEADS * 256, R_KV), F32),
                   jax.ShapeDtypeStruct((1, R_Q), F32), jax.ShapeDtypeStruct((1, R_KV), F32),
                   jax.ShapeDtypeStruct((1, D_HEAD_PAD), F32), jax.ShapeDtypeStruct((1, D_HEAD_PAD), F32),
                   jax.ShapeDtypeStruct((1, D_MODEL), F32), jax.ShapeDtypeStruct((8, D_MODEL), F32)),
        in_specs=[pl.BlockSpec((TILE, D_MODEL), lambda s: (latent(t2(s)), 0)), pl.BlockSpec((TILE, D_MODEL), lat3),
                  full((TILE, D_MODEL)), full((8, D_MODEL)), full((8, D_MODEL)),
                  full((1, D_MODEL)), full((D_IN_PROJ, D_MODEL)), full((1, R_Q)), full((N_HEADS * D_HEAD_PAD, R_Q)),
                  full((1, R_KV)), full((R_KV, N_HEADS * 256)), full((1, D_HEAD_PAD)), full((1, D_HEAD_PAD)),
                  rope_spec, rope_spec, rope_spec,
                  pl.BlockSpec((TILE, R_Q), lambda s: (t1(s), (O_CQ - N_GATES) // R_Q)),
                  pl.BlockSpec((TILE, R_KV), lambda s: (t1(s), (O_CKV - N_GATES) // R_KV)),
                  pl.BlockSpec((TILE, 128), lambda s: (t1(s), (O_KR - N_GATES) // 128)),
                  pl.BlockSpec((N_HEADS, TILE, D_HEAD_PAD), lambda s: (0, latent(t1(s)), 0)),
                  pl.BlockSpec((N_HEADS, D_HEAD_PAD, TILE), lambda s: (0, 0, t1(s))),
                  pl.BlockSpec((N_HEADS, D_V, TILE), lambda s: (0, 0, t1(s))),
                  pl.BlockSpec((TILE, D_ATTN), lat), pl.BlockSpec((TILE, D_POOL), lat), pl.BlockSpec((TILE, D_POOL), lat),
                  pl.BlockSpec((HALO, D_POOL), lambda s: (jnp.maximum((t1(s) - 1) * rows8 - 1, 0), 0)),
                  pl.BlockSpec((HALO, D_POOL), lambda s: (jnp.minimum(jnp.maximum(t1(s), 1) * rows8, last8), 0)),
                  pl.BlockSpec((TILE, D_MODEL), lat3)],
        out_specs=(pl.BlockSpec((TILE, D_MODEL), lat3), full((D_IN_PROJ, D_MODEL)), full((N_HEADS * D_HEAD_PAD, R_Q)),
                   full((N_HEADS * 256, R_KV)), full((1, R_Q)), full((1, R_KV)), full((1, D_HEAD_PAD)),
                   full((1, D_HEAD_PAD)), full((1, D_MODEL)), full((8, D_MODEL))),
        scratch_shapes=[pltpu.VMEM((TILE, U_PAD), BF16), pltpu.VMEM((TILE, U_PAD), BF16),
                        pltpu.VMEM((TILE, D_MODEL), F32), pltpu.VMEM((TILE, D_MODEL), F32)],
        compiler_params=pltpu.CompilerParams(dimension_semantics=("arbitrary",), vmem_limit_bytes=VMEM_LIMIT),
    )(x, x, ctx, modrows, bmodrows, norm_g, w_in_p, q_lora_g, w_uq_p, kv_lora_g, w_ukv, qng_p, kng_p, cos, slo, shi,
      u, u, u, dq, dk, dv, dga, dgp, dpl, dpl, dpl, g1)


SMALL_LAYOUT = ((0, D_MODEL), (1, R_Q), (2, R_KV), (3, D_HEAD_PAD), (4, D_HEAD_PAD), (5, D_POOL))
ROW_DMOD_B, ROW_DMOD_C, ROW_LOSS = 6, 9, 12


def _epilogue(parts, landed, smalls, dmod4, dgate, loss_acc, w_mod_l):
    n_a, n_l, n_s = len(parts), len(landed), len(smalls)
    n_mod = w_mod_l.shape[1]
    blk = [p.shape[1:] for p in parts]
    small_out = [D_MODEL, R_Q, R_KV, D_HEAD, D_HEAD, D_POOL]

    def body(*refs):
        part_refs = refs[0:n_a]
        land_refs = refs[n_a:n_a + n_l]
        small_in = refs[n_a + n_l:n_a + n_l + n_s]
        dmod4_ref, dgate_ref, loss_ref, wmod_ref = refs[n_a + n_l + n_s:n_a + n_l + n_s + 4]
        outs = refs[n_a + n_l + n_s + 4:]
        red_refs = outs[0:n_a]
        landsum_refs = outs[n_a:n_a + n_l]
        ccg_ref = outs[n_a + n_l]
        sum_refs = outs[n_a + n_l + 1:n_a + n_l + 1 + n_s]
        gbmod_ref, dmy_ref, lossout_ref = outs[n_a + n_l + 1 + n_s:n_a + n_l + 4 + n_s]
        scr = outs[n_a + n_l + 4 + n_s:]
        recv1, own1, sum1b, recv2 = scr[0:n_a], scr[n_a:2 * n_a], scr[2 * n_a:3 * n_a], scr[3 * n_a:4 * n_a]
        small_ref, smallg_ref, tot_ref, cc_ref = scr[4 * n_a:4 * n_a + 4]
        land_vmem = scr[4 * n_a + 4:4 * n_a + 4 + n_l]
        ssem1, rsem1, lsem, ssem2, rsem2, ssem_s, rsem_s, ssem_cc, rsem_cc, land_sem = scr[4 * n_a + 4 + n_l:]
        x, y, c = _coords()
        me = (x, y, c)
        sibling = (x, y, 1 - c)

        small_ref[...] = jnp.zeros_like(small_ref)
        for ref, (row, width) in zip(small_in, SMALL_LAYOUT):
            small_ref[row:row + 1, 0:width] = ref[...]
        small_ref[ROW_DMOD_B:ROW_DMOD_B + 2, :] = dmod4_ref[0:2, :]
        small_ref[ROW_DMOD_B + 2:ROW_DMOD_B + 3, :] = dgate_ref[...]
        small_ref[ROW_DMOD_C:ROW_DMOD_C + 2, :] = dmod4_ref[2:4, :]
        small_ref[ROW_LOSS:ROW_LOSS + 1, 0:128] = loss_ref[0:1, :]
        smallg_ref[_lin(me)] = small_ref[...]
        s_recv, s_send = _gather_to_all(small_ref, smallg_ref, ssem_s, rsem_s)

        stage1, own_copies = [], []
        for a in range(n_a):
            for b in range(4):
                dev = 4 * (b >> 1) + 2 * (b & 1)
                stage1.append(pltpu.make_async_remote_copy(
                    src_ref=part_refs[a].at[dev + (1 - c)], dst_ref=recv1[a].at[b],
                    send_sem=ssem1.at[a, b], recv_sem=rsem1.at[a, b], device_id=sibling, device_id_type=MESH))
                own_copies.append(pltpu.make_async_copy(part_refs[a].at[dev + c], own1[a].at[b], lsem.at[a, b]))
                stage1[-1].start()
                own_copies[-1].start()
        land_copies = [pltpu.make_async_copy(land_refs[n], land_vmem[n], land_sem.at[n]) for n in range(n_l)]
        for cp in land_copies:
            cp.start()

        s_recv()
        tot = smallg_ref[0]
        for d in range(1, N_DEV):
            tot = tot + smallg_ref[d]
        tot_ref[...] = tot
        for ref, (row, _), width in zip(sum_refs, SMALL_LAYOUT, small_out):
            ref[...] = tot_ref[row:row + 1, 0:width]
        lossout_ref[...] = tot_ref[8:16, 0:128]
        lin = _lin(me)

        def my_columns(three_rows):
            v = jnp.concatenate(three_rows, axis=1)
            out = jnp.zeros((1, n_mod), F32)
            for d in range(N_DEV):
                out = out + jnp.where(lin == d, v[:, d * n_mod:(d + 1) * n_mod], 0.0)
            return out

        rows3 = lambda ref, r0: [ref[r0 + t:r0 + t + 1, :] for t in range(3)]
        dmodc = rows3(tot_ref, ROW_DMOD_C)
        gbmod_ref[...] = jnp.concatenate(rows3(tot_ref, ROW_DMOD_B), axis=1) + jnp.concatenate(dmodc, axis=1)
        dmy_ref[...] = jnp.zeros_like(dmy_ref)
        for b in range(N_DEV):
            dmy_ref[b:b + 1, :] = my_columns(rows3(smallg_ref.at[b], ROW_DMOD_B))
        dmy = my_columns(dmodc)
        dmy_ref[N_DEV:N_DEV + 1, :] = dmy
        part = _dot_nt(jnp.broadcast_to(dmy, (8, n_mod)).astype(BF16), wmod_ref[...].astype(BF16))

        cc_ref[...] = part
        ccg_ref[_lin(me)] = part
        cc_recv, cc_send = _gather_to_all(cc_ref, ccg_ref, ssem_cc, rsem_cc)

        stage2 = []
        for a in range(n_a):
            for b in range(4):
                stage1[4 * a + b].wait_recv()
                own_copies[4 * a + b].wait()
                sum1b[a][b] = (own1[a][b] + recv1[a][b]).astype(BF16)
            for k in (1, 2, 3):
                tx, ty = _flip(x, (k >> 1) & 1), _flip(y, k & 1)
                stage2.append(pltpu.make_async_remote_copy(
                    src_ref=sum1b[a].at[2 * tx + ty], dst_ref=recv2[a].at[k - 1], send_sem=ssem2.at[a, k - 1],
                    recv_sem=rsem2.at[a, k - 1], device_id=(tx, ty, c), device_id_type=MESH))
                stage2[-1].start()
        for a in range(n_a):
            own = own1[a][2 * x + y] + recv1[a][2 * x + y]
            for k in (1, 2, 3):
                stage2[3 * a + k - 1].wait_recv()
                own = own + recv2[a][k - 1].astype(F32)
            red_refs[a][...] = own

        for cp in land_copies:
            cp.wait()
        for ref, out in zip(land_vmem, landsum_refs):
            acc = ref[0].astype(F32)
            for d in range(1, N_DEV):
                acc = acc + ref[d].astype(F32)
            out[...] = acc
        cc_recv()
        for cp in stage1 + stage2:
            cp.wait_send()
        s_send()
        cc_send()

    vm = pl.BlockSpec(memory_space=pltpu.VMEM)
    sds = jax.ShapeDtypeStruct
    return pl.pallas_call(
        body, name="epilogue_reduce",
        out_shape=(*[sds(s, F32) for s in blk], *[sds(a.shape[1:], F32) for a in landed], sds((N_DEV, 8, D_MODEL), F32),
                   *[sds((1, w), F32) for w in small_out], sds((1, 3 * D_MODEL), F32), sds((16, n_mod), F32),
                   sds((8, 128), F32)),
        in_specs=[pl.BlockSpec(memory_space=pl.ANY)] * (n_a + n_l) + [vm] * (n_s + 4),
        out_specs=tuple([vm] * (n_a + n_l + n_s + 4)),
        scratch_shapes=[*[pltpu.VMEM((4,) + s, F32) for s in blk], *[pltpu.VMEM((4,) + s, F32) for s in blk],
                        *[pltpu.VMEM((4,) + s, BF16) for s in blk], *[pltpu.VMEM((3,) + s, BF16) for s in blk],
                        pltpu.VMEM((SMALL_ROWS, D_MODEL), F32), pltpu.VMEM((N_DEV, SMALL_ROWS, D_MODEL), F32),
                        pltpu.VMEM((SMALL_ROWS, D_MODEL), F32), pltpu.VMEM((8, D_MODEL), F32),
                        *[pltpu.VMEM(a.shape, a.dtype) for a in landed],
                        pltpu.SemaphoreType.DMA((n_a, 4)), pltpu.SemaphoreType.DMA((n_a, 4)), pltpu.SemaphoreType.DMA((n_a, 4)),
                        pltpu.SemaphoreType.DMA((n_a, 3)), pltpu.SemaphoreType.DMA((n_a, 3)),
                        pltpu.SemaphoreType.DMA((7,)), pltpu.SemaphoreType.DMA((7,)),
                        pltpu.SemaphoreType.DMA((7,)), pltpu.SemaphoreType.DMA((7,)), pltpu.SemaphoreType.DMA((n_l,))],
        compiler_params=pltpu.CompilerParams(vmem_limit_bytes=VMEM_LIMIT),
    )(*parts, *landed, *smalls, dmod4, dgate, loss_acc, w_mod_l)


def _adamw(weights, grads, ms, vs, c_all_t, dmod_my, p2g):
    n = len(weights)

    def body(*refs):
        w_refs = refs[0:n]
        g_in = refs[n:2 * n - 2]
        m_refs = refs[2 * n - 2:3 * n - 2]
        v_refs = refs[3 * n - 2:4 * n - 2]
        cat_ref, dmod_ref, p2g_ref = refs[4 * n - 2:4 * n + 1]
        outs = refs[4 * n + 1:]
        g_refs, d_refs, nm_refs, nv_refs = outs[0:n], outs[n:2 * n], outs[2 * n:3 * n], outs[3 * n:4 * n]
        gcc_ref, gwm_ref = g_refs[0], g_refs[1]

        part = p2g_ref[0, 0:1, :]
        for d in range(1, N_DEV):
            part = part + p2g_ref[d, 0:1, :]
        cc = w_refs[0][...]
        sg = _sigmoid(cc)
        gcc_ref[...] = part * (sg * (1.0 + cc * (1.0 - sg)))
        cat = cat_ref[...]
        gwm_ref[...] = lax.dot_general(cat * _sigmoid(cat), dmod_ref[...], (((1,), (0,)), ((), ())), precision=HIGHEST,
                                       preferred_element_type=F32)
        for idx in range(n):
            if idx >= 2:
                g_refs[idx][...] = g_in[idx - 2][...]
            g = g_refs[idx][...]
            m = ADAM_B1 * m_refs[idx][...] + (1.0 - ADAM_B1) * g
            v = ADAM_B2 * v_refs[idx][...] + (1.0 - ADAM_B2) * (g * g)
            m_hat = m / (1.0 - ADAM_B1 ** ADAM_STEP)
            v_hat = v / (1.0 - ADAM_B2 ** ADAM_STEP)
            d_refs[idx][...] = -ADAM_LR * (m_hat / (jnp.sqrt(v_hat) + ADAM_EPS) + ADAM_WD * w_refs[idx][...])
            nm_refs[idx][...] = m
            nv_refs[idx][...] = v

    vm = pl.BlockSpec(memory_space=pltpu.VMEM)
    shapes = [jax.ShapeDtypeStruct(w.shape, F32) for w in weights]
    args = list(weights) + list(grads[2:]) + list(ms) + list(vs) + [c_all_t, dmod_my, p2g]
    out_shape = tuple(shapes * 4)
    return pl.pallas_call(
        body, name="adamw_update", out_shape=out_shape, in_specs=[vm] * len(args), out_specs=tuple([vm] * len(out_shape)),
        compiler_params=pltpu.CompilerParams(vmem_limit_bytes=VMEM_LIMIT),
    )(*args)


def _rope_tables(seq):
    rows = seq // GRID_W
    row = np.repeat(np.arange(rows, dtype=np.float32), GRID_W)
    col = np.tile(np.arange(GRID_W, dtype=np.float32), rows)
    n_freq = D_ROPE // 4
    inv = (np.float32(ROPE_BASE) ** (-np.arange(n_freq, dtype=np.float32) / np.float32(n_freq))).astype(np.float32)
    ang_r = (row[:, None] * inv).astype(np.float32)
    ang_c = (col[:, None] * inv).astype(np.float32)
    ang = np.concatenate([ang_r, ang_r, ang_c, ang_c], axis=-1)
    cos, sin = np.cos(ang).astype(np.float32), np.sin(ang).astype(np.float32)
    low = (np.arange(D_ROPE) % 32) < 16

    def table(t, fill):
        out = np.full((TILE + seq, 128), fill, np.float32)
        out[TILE:, 0:D_ROPE] = t
        return jnp.asarray(out)

    return table(cos, 1.0), table(np.where(low, -sin, 0.0), 0.0), table(np.where(low, 0.0, sin), 0.0)


def kernel(x, c, ctx, c_ctx, w_mod, b_mod, norm_g, w_in, q_lora_g, w_uq, kv_lora_g, w_ukv, q_norm_g, k_norm_g, w_pool, pool_scale, w_out, loss_target, m_c_ctx, m_w_mod, m_b_mod, m_norm_g, m_w_in, m_q_lora_g, m_w_uq, m_kv_lora_g, m_w_ukv, m_q_norm_g, m_k_norm_g, m_w_pool, m_pool_scale, m_w_out, v_c_ctx, v_w_mod, v_b_mod, v_norm_g, v_w_in, v_q_lora_g, v_w_uq, v_kv_lora_g, v_w_ukv, v_q_norm_g, v_k_norm_g, v_w_pool, v_pool_scale, v_w_out):
    seq = x.shape[1]
    assert ctx.shape[1] == TILE and seq % TILE == 0 and seq % GRID_W == 0
    ix, iy, ic = lax.axis_index("x"), lax.axis_index("y"), lax.axis_index("c")
    me = 4 * ix + 2 * iy + ic
    x2, ctx2, tgt2 = x[0], ctx[0], loss_target[0]
    w_mod_l, w_ukv_l, w_out_l = w_mod[0], w_ukv[0], w_out[0]
    c_ctx_row = c_ctx.reshape(1, D_MODEL)

    tr = lambda a: jnp.transpose(a[0])
    w_in_tl, w_uq_tl = tr(w_in), tr(w_uq)
    cg, modg, wg_in, wg_uq, wg_ukv = _prologue(
        c, c_ctx_row, w_mod_l,
        [w_in_tl, w_uq_tl, w_ukv_l])
    mod_all = jnp.transpose(modg, (1, 0, 2)).reshape(16, 3 * D_MODEL)
    mod_b = lax.dynamic_slice_in_dim(mod_all, me, 1, axis=0).reshape(3, D_MODEL)
    mod_c = mod_all[8].reshape(3, D_MODEL)
    modrows = jnp.concatenate([mod_b, mod_c[0:2], jnp.zeros((3, D_MODEL), F32)], axis=0)
    b3 = b_mod.reshape(3, D_MODEL)
    bmodrows = jnp.concatenate([b3, b3[0:2], jnp.zeros((3, D_MODEL), F32)], axis=0)

    w_in_p = wg_in.reshape(D_IN_PROJ, D_MODEL)
    w_uq_p = jnp.concatenate([wg_uq.reshape(N_HEADS, D_HEAD, R_Q), jnp.zeros((N_HEADS, D_HEAD_PAD - D_HEAD, R_Q), BF16)],
                             axis=1).reshape(N_HEADS * D_HEAD_PAD, R_Q)
    w_ukv_f = jnp.transpose(wg_ukv, (1, 0, 2)).reshape(R_KV, N_HEADS * 256)
    qng_p = jnp.concatenate([q_norm_g, jnp.zeros((1, D_HEAD_PAD - D_HEAD), F32)], axis=1)
    kng_p = jnp.concatenate([k_norm_g, jnp.zeros((1, D_HEAD_PAD - D_HEAD), F32)], axis=1)
    cos, slo, shi = _rope_tables(seq)

    u_gates, u_mla, q, k, v = _inproj_fwd(x2, ctx2, modrows, bmodrows, norm_g, w_in_p, q_lora_g, w_uq_p, kv_lora_g, w_ukv_f,
                             qng_p, kng_p, cos, slo, shi)
    attn, lse, wg_out = _attn_fwd(q, k, v, min(2048, seq), w_out_l.astype(BF16))
    (loss_acc, g1, dgate, d_attn, dga, dgp, dpl, gwo_b, gwp, dps) = _mix(
        x2, tgt2, attn, u_gates, modrows, bmodrows, w_pool[0], pool_scale, wg_out.reshape(D_MODEL, D_MODEL))

    blocks = lambda a: a.reshape((N_DEV, a.shape[0] // N_DEV) + a.shape[1:])
    dq, dk, dv, land_wo, land_wp = _attn_bwd(q, k, v, attn, lse, d_attn.reshape(seq, D_ATTN), min(1024, seq), blocks(gwo_b),
                                             gwp.reshape(4 * GROUP_DIM, GROUP_DIM))
    (grad_x, gwin_t, gwuq_p, gwukv_t, dqlg, dkvlg, dqng, dkng, dng, dmod4) = _inproj_bwd(
        x2, ctx2, modrows, bmodrows, norm_g, w_in_p, q_lora_g, w_uq_p, kv_lora_g, w_ukv_f, qng_p, kng_p, cos, slo, shi,
        u_mla, dq, dk, dv, dga, dgp, dpl, g1)

    gwuq_t = gwuq_p.reshape(N_HEADS, D_HEAD_PAD, R_Q)[:, 0:D_HEAD].reshape(N_HEADS * D_HEAD, R_Q)
    parts = [blocks(gwin_t), blocks(gwuq_t), blocks(gwukv_t)]
    (r_win, r_wuq, r_wukv, r_wout, g_w_pool, ccg, g_ng, g_qlg, g_kvlg, g_qng, g_kng, g_ps, g_bmod, dmod_my,
     loss_rows) = _epilogue(parts, [land_wo, land_wp], [dng, dqlg, dkvlg, dqng, dkng, dps], dmod4, dgate, loss_acc, w_mod_l)

    g_w_ukv = jnp.transpose(r_wukv)
    c_rows = cg[:, 0, :]
    c_all_t = jnp.transpose(jnp.concatenate([c_rows, c_ctx_row, jnp.zeros((16 - N_DEV - 1, D_MODEL), F32)], axis=0))

    weights = [c_ctx_row, w_mod_l, b_mod, norm_g, w_in_tl, q_lora_g, w_uq_tl, kv_lora_g, w_ukv_l, q_norm_g, k_norm_g,
               w_pool.reshape(4 * GROUP_DIM, GROUP_DIM), pool_scale, w_out_l]
    grads = [None, None, g_bmod, g_ng, r_win, g_qlg, r_wuq, g_kvlg, g_w_ukv, g_qng, g_kng, g_w_pool, g_ps, r_wout]
    ms = [m_c_ctx.reshape(1, D_MODEL), m_w_mod[0], m_b_mod, m_norm_g, tr(m_w_in), m_q_lora_g, tr(m_w_uq), m_kv_lora_g,
          m_w_ukv[0], m_q_norm_g, m_k_norm_g, m_w_pool.reshape(4 * GROUP_DIM, GROUP_DIM), m_pool_scale, m_w_out[0]]
    vs = [v_c_ctx.reshape(1, D_MODEL), v_w_mod[0], v_b_mod, v_norm_g, tr(v_w_in), v_q_lora_g, tr(v_w_uq), v_kv_lora_g,
          v_w_ukv[0], v_q_norm_g, v_k_norm_g, v_w_pool.reshape(4 * GROUP_DIM, GROUP_DIM), v_pool_scale, v_w_out[0]]
    outs = _adamw(weights, grads, ms, vs, c_all_t, dmod_my, ccg)
    n = len(weights)
    grads, deltas, new_m, new_v = outs[0:n], outs[n:2 * n], outs[2 * n:3 * n], outs[3 * n:4 * n]

    loss = loss_rows[ROW_LOSS - 8, 0]
    final = [c_ctx.shape, w_mod.shape, b_mod.shape, norm_g.shape, w_in.shape, q_lora_g.shape, w_uq.shape, kv_lora_g.shape,
             w_ukv.shape, q_norm_g.shape, k_norm_g.shape, w_pool.shape, pool_scale.shape, w_out.shape]
    transposed = (4, 6)

    def shaped(arrs):
        return [(jnp.transpose(a) if i in transposed else a).reshape(s) for i, (a, s) in enumerate(zip(arrs, final))]

    return (loss, grad_x[None], *shaped(grads), *shaped(deltas), *shaped(new_m), *shaped(new_v))
```
